```python
import jax, jax.numpy as jnp
from jax import lax
import numpy as np

D_MODEL = 1024
BATCH = 8
SEQ = 8192
DEPTH = 2

HEAD_DIM = 64
ATTN_WIDTH = D_MODEL // 2
N_HEADS_A = ATTN_WIDTH // HEAD_DIM
Q_BLOCK = 128
CONV_CH = D_MODEL // 2
CONV_K = 3
POOL_WINDOWS = (2, 4, 8, 16)
POOL_GROUPS = len(POOL_WINDOWS)
POOL_CG = D_MODEL // POOL_GROUPS
D_FF = 4 * D_MODEL
RMS_EPS = 1e-6
MIX_IN = 3 * ATTN_WIDTH + N_HEADS_A + 3 * CONV_CH
MIX_OUT = ATTN_WIDTH + CONV_CH

kernel_name = "fox_shortconv_pool_hybrid"


def rms_norm(x, g):
    xf = x.astype(jnp.float32)
    y = xf * lax.rsqrt(jnp.mean(xf * xf, axis=-1, keepdims=True) + RMS_EPS)
    return (y * g.astype(jnp.float32)).astype(x.dtype)


def forgetting_attention(q, k, v, f_logit):
    b, s, h, dh = q.shape
    nblk = s // Q_BLOCK
    log_f = jax.nn.log_sigmoid(f_logit.astype(jnp.float32))
    cum = jnp.cumsum(log_f, axis=1)
    cum_k = cum.transpose(0, 2, 1)
    q_blocks = q.reshape(b, nblk, Q_BLOCK, h, dh).transpose(1, 0, 2, 3, 4)
    cum_q_blocks = cum.reshape(b, nblk, Q_BLOCK, h).transpose(1, 0, 3, 2)
    q_pos = jnp.arange(s).reshape(nblk, Q_BLOCK)
    k_pos = jnp.arange(s)
    scale = dh ** -0.5
    neg = jnp.finfo(jnp.float32).min

    def one_block(args):
        qi, cqi, pi = args
        logits = jnp.einsum('bqhd,bkhd->bhqk', qi, k).astype(jnp.float32) * scale
        logits = logits + cqi[..., None] - cum_k[:, :, None, :]
        mask = k_pos[None, :] <= pi[:, None]
        logits = jnp.where(mask, logits, neg)
        p = jax.nn.softmax(logits, axis=-1)
        return jnp.einsum('bhqk,bkhd->bqhd', p.astype(v.dtype), v)

    out = lax.map(one_block, (q_blocks, cum_q_blocks, q_pos))
    return out.transpose(1, 0, 2, 3, 4).reshape(b, s, h * dh)


def causal_dwconv3(u, w):
    s = u.shape[1]
    up = jnp.pad(u, ((0, 0), (CONV_K - 1, 0), (0, 0)))
    return w[0] * up[:, 0:s] + w[1] * up[:, 1:s + 1] + w[2] * up[:, 2:s + 2]


def attn_conv_mixer(h, w_in, b_f, conv_w, w_out):
    b, s, _ = h.shape
    proj = h @ w_in
    a = ATTN_WIDTH
    splits = [a, 2 * a, 3 * a, 3 * a + N_HEADS_A,
              3 * a + N_HEADS_A + CONV_CH, 3 * a + N_HEADS_A + 2 * CONV_CH]
    q, k, v, f_logit, b_gate, c_gate, x_in = jnp.split(proj, splits, axis=-1)
    shp = (b, s, N_HEADS_A, HEAD_DIM)
    att = forgetting_attention(q.reshape(shp), k.reshape(shp), v.reshape(shp), f_logit + b_f)
    conv = b_gate * causal_dwconv3(c_gate * x_in, conv_w)
    return jnp.concatenate([att, conv], axis=-1) @ w_out


def causal_mean_pool_minus_self(u, window):
    s = u.shape[1]
    uf = u.astype(jnp.float32)
    cs = jnp.pad(jnp.cumsum(uf, axis=1), ((0, 0), (1, 0), (0, 0)))
    lagged = jnp.pad(cs, ((0, 0), (window - 1, 0), (0, 0)))[:, :s]
    count = jnp.minimum(jnp.arange(1, s + 1), window).astype(jnp.float32)[None, :, None]
    return ((cs[:, 1:] - lagged) / count - uf).astype(u.dtype)


def pool_mixer(h, pool_w, pool_scale):
    b, s, d = h.shape
    groups = jnp.split(h, POOL_GROUPS, axis=-1)
    pooled = jnp.stack([causal_mean_pool_minus_self(g, w) for g, w in zip(groups, POOL_WINDOWS)],
                       axis=2)
    y = jnp.einsum('bsgc,gcd->bsgd', pooled, pool_w).reshape(b, s, d)
    return y * pool_scale


def sq_relu_mlp(h, w_up, w_down):
    return jnp.square(jax.nn.relu(h @ w_up)) @ w_down


def _fwd_setup_inputs(seed: int = 0) -> dict:
    key = jax.random.key(seed)
    ks = jax.random.split(key, 20)
    f32 = jnp.float32

    def nrm(k, shape, scale):
        return jax.random.normal(k, shape, f32) * scale

    def gain(k):
        return 1.0 + 0.05 * jax.random.normal(k, (D_MODEL,), f32)

    return {
        "x": jax.random.normal(ks[0], (BATCH, SEQ, D_MODEL), f32),
        "norm_mix_0": gain(ks[1]),
        "w_in_0": nrm(ks[2], (D_MODEL, MIX_IN), D_MODEL ** -0.5),
        "b_f_0": 2.0 + 0.5 * jax.random.normal(ks[3], (N_HEADS_A,), f32),
        "conv_w_0": nrm(ks[4], (CONV_K, CONV_CH), CONV_K ** -0.5),
        "w_out_0": nrm(ks[5], (MIX_OUT, D_MODEL), MIX_OUT ** -0.5),
        "norm_ffn_0": gain(ks[6]),
        "w_up_0": nrm(ks[7], (D_MODEL, D_FF), D_MODEL ** -0.5),
        "w_down_0": nrm(ks[8], (D_FF, D_MODEL), D_FF ** -0.5),
        "norm_mix_1": gain(ks[9]),
        "pool_w_1": nrm(ks[10], (POOL_GROUPS, POOL_CG, POOL_CG), POOL_CG ** -0.5),
        "pool_scale_1": 1.0 + 0.05 * jax.random.normal(ks[11], (D_MODEL,), f32),
        "norm_ffn_1": gain(ks[12]),
        "w_up_1": nrm(ks[13], (D_MODEL, D_FF), D_MODEL ** -0.5),
        "w_down_1": nrm(ks[14], (D_FF, D_MODEL), D_FF ** -0.5),
        "final_norm": gain(ks[15]),
    }


def _fwd_reference(x, norm_mix_0, w_in_0, b_f_0, conv_w_0, w_out_0, norm_ffn_0, w_up_0, w_down_0,
              norm_mix_1, pool_w_1, pool_scale_1, norm_ffn_1, w_up_1, w_down_1, final_norm):
    mix_params = [(norm_mix_0, w_in_0, b_f_0, conv_w_0, w_out_0),
                  (norm_mix_1, pool_w_1, pool_scale_1)]
    ffn_params = [(norm_ffn_0, w_up_0, w_down_0), (norm_ffn_1, w_up_1, w_down_1)]
    h = x
    for i in range(DEPTH):
        mp = mix_params[i]
        if i % 2 == 0:
            h = h + attn_conv_mixer(rms_norm(h, mp[0]), *mp[1:])
        else:
            h = h + pool_mixer(rms_norm(h, mp[0]), *mp[1:])
        g, w_up, w_down = ffn_params[i]
        h = h + sq_relu_mlp(rms_norm(h, g), w_up, w_down)
    return rms_norm(h, final_norm)


import jax as _jax
import jax.numpy as _jnp

TWIN_FORMAT = 'train_step'
FWD_PARAMS = ['x', 'norm_mix_0', 'w_in_0', 'b_f_0', 'conv_w_0', 'w_out_0', 'norm_ffn_0', 'w_up_0', 'w_down_0', 'norm_mix_1', 'pool_w_1', 'pool_scale_1', 'norm_ffn_1', 'w_up_1', 'w_down_1', 'final_norm']
TWIN_WEIGHTS = ['norm_mix_0', 'w_in_0', 'b_f_0', 'conv_w_0', 'w_out_0', 'norm_ffn_0', 'w_up_0', 'w_down_0', 'norm_mix_1', 'pool_w_1', 'pool_scale_1', 'norm_ffn_1', 'w_up_1', 'w_down_1', 'final_norm']
TWIN_DIFF_INPUT = 'x'
TWIN_INPUTS = ['x', 'norm_mix_0', 'w_in_0', 'b_f_0', 'conv_w_0', 'w_out_0', 'norm_ffn_0', 'w_up_0', 'w_down_0', 'norm_mix_1', 'pool_w_1', 'pool_scale_1', 'norm_ffn_1', 'w_up_1', 'w_down_1', 'final_norm', 'loss_target', 'm_norm_mix_0', 'm_w_in_0', 'm_b_f_0', 'm_conv_w_0', 'm_w_out_0', 'm_norm_ffn_0', 'm_w_up_0', 'm_w_down_0', 'm_norm_mix_1', 'm_pool_w_1', 'm_pool_scale_1', 'm_norm_ffn_1', 'm_w_up_1', 'm_w_down_1', 'm_final_norm', 'v_norm_mix_0', 'v_w_in_0', 'v_b_f_0', 'v_conv_w_0', 'v_w_out_0', 'v_norm_ffn_0', 'v_w_up_0', 'v_w_down_0', 'v_norm_mix_1', 'v_pool_w_1', 'v_pool_scale_1', 'v_norm_ffn_1', 'v_w_up_1', 'v_w_down_1', 'v_final_norm']
TWIN_OUTPUTS = ['loss', 'grad_x', 'grad_norm_mix_0', 'grad_w_in_0', 'grad_b_f_0', 'grad_conv_w_0', 'grad_w_out_0', 'grad_norm_ffn_0', 'grad_w_up_0', 'grad_w_down_0', 'grad_norm_mix_1', 'grad_pool_w_1', 'grad_pool_scale_1', 'grad_norm_ffn_1', 'grad_w_up_1', 'grad_w_down_1', 'grad_final_norm', 'delta_norm_mix_0', 'delta_w_in_0', 'delta_b_f_0', 'delta_conv_w_0', 'delta_w_out_0', 'delta_norm_ffn_0', 'delta_w_up_0', 'delta_w_down_0', 'delta_norm_mix_1', 'delta_pool_w_1', 'delta_pool_scale_1', 'delta_norm_ffn_1', 'delta_w_up_1', 'delta_w_down_1', 'delta_final_norm', 'new_m_norm_mix_0', 'new_m_w_in_0', 'new_m_b_f_0', 'new_m_conv_w_0', 'new_m_w_out_0', 'new_m_norm_ffn_0', 'new_m_w_up_0', 'new_m_w_down_0', 'new_m_norm_mix_1', 'new_m_pool_w_1', 'new_m_pool_scale_1', 'new_m_norm_ffn_1', 'new_m_w_up_1', 'new_m_w_down_1', 'new_m_final_norm', 'new_v_norm_mix_0', 'new_v_w_in_0', 'new_v_b_f_0', 'new_v_conv_w_0', 'new_v_w_out_0', 'new_v_norm_ffn_0', 'new_v_w_up_0', 'new_v_w_down_0', 'new_v_norm_mix_1', 'new_v_pool_w_1', 'new_v_pool_scale_1', 'new_v_norm_ffn_1', 'new_v_w_up_1', 'new_v_w_down_1', 'new_v_final_norm']
TWIN_LEAF_KINDS = {'loss': 'loss', 'grad_x': 'grad_x', 'grad_norm_mix_0': 'grad_w', 'grad_w_in_0': 'grad_w', 'grad_b_f_0': 'grad_w', 'grad_conv_w_0': 'grad_w', 'grad_w_out_0': 'grad_w', 'grad_norm_ffn_0': 'grad_w', 'grad_w_up_0': 'grad_w', 'grad_w_down_0': 'grad_w', 'grad_norm_mix_1': 'grad_w', 'grad_pool_w_1': 'grad_w', 'grad_pool_scale_1': 'grad_w', 'grad_norm_ffn_1': 'grad_w', 'grad_w_up_1': 'grad_w', 'grad_w_down_1': 'grad_w', 'grad_final_norm': 'grad_w', 'delta_norm_mix_0': 'delta_w', 'delta_w_in_0': 'delta_w', 'delta_b_f_0': 'delta_w', 'delta_conv_w_0': 'delta_w', 'delta_w_out_0': 'delta_w', 'delta_norm_ffn_0': 'delta_w', 'delta_w_up_0': 'delta_w', 'delta_w_down_0': 'delta_w', 'delta_norm_mix_1': 'delta_w', 'delta_pool_w_1': 'delta_w', 'delta_pool_scale_1': 'delta_w', 'delta_norm_ffn_1': 'delta_w', 'delta_w_up_1': 'delta_w', 'delta_w_down_1': 'delta_w', 'delta_final_norm': 'delta_w', 'new_m_norm_mix_0': 'new_m', 'new_m_w_in_0': 'new_m', 'new_m_b_f_0': 'new_m', 'new_m_conv_w_0': 'new_m', 'new_m_w_out_0': 'new_m', 'new_m_norm_ffn_0': 'new_m', 'new_m_w_up_0': 'new_m', 'new_m_w_down_0': 'new_m', 'new_m_norm_mix_1': 'new_m', 'new_m_pool_w_1': 'new_m', 'new_m_pool_scale_1': 'new_m', 'new_m_norm_ffn_1': 'new_m', 'new_m_w_up_1': 'new_m', 'new_m_w_down_1': 'new_m', 'new_m_final_norm': 'new_m', 'new_v_norm_mix_0': 'new_v', 'new_v_w_in_0': 'new_v', 'new_v_b_f_0': 'new_v', 'new_v_conv_w_0': 'new_v', 'new_v_w_out_0': 'new_v', 'new_v_norm_ffn_0': 'new_v', 'new_v_w_up_0': 'new_v', 'new_v_w_down_0': 'new_v', 'new_v_norm_mix_1': 'new_v', 'new_v_pool_w_1': 'new_v', 'new_v_pool_scale_1': 'new_v', 'new_v_norm_ffn_1': 'new_v', 'new_v_w_up_1': 'new_v', 'new_v_w_down_1': 'new_v', 'new_v_final_norm': 'new_v'}


def _forward(args):
    return _fwd_reference(*[args[k] for k in FWD_PARAMS])


def _output_shape():
    def fwd():
        inp = _fwd_setup_inputs(0)
        return _fwd_reference(*[inp[k] for k in FWD_PARAMS])
    out = _jax.eval_shape(fwd)
    return out.shape, out.dtype

N_MICROBATCH = 1
ADAM_LR = 0.001
ADAM_B1 = 0.9
ADAM_B2 = 0.999
ADAM_EPS = 1e-08
ADAM_WD = 0.01
ADAM_STEP = 10
PER_EXAMPLE_BATCH_AXIS = {'x': 0, 'loss_target': 0}
SHARED_INPUTS = []
_WEIGHT_DTYPES = {'norm_mix_0': _jnp.float32, 'w_in_0': _jnp.float32, 'b_f_0': _jnp.float32, 'conv_w_0': _jnp.float32, 'w_out_0': _jnp.float32, 'norm_ffn_0': _jnp.float32, 'w_up_0': _jnp.float32, 'w_down_0': _jnp.float32, 'norm_mix_1': _jnp.float32, 'pool_w_1': _jnp.float32, 'pool_scale_1': _jnp.float32, 'norm_ffn_1': _jnp.float32, 'w_up_1': _jnp.float32, 'w_down_1': _jnp.float32, 'final_norm': _jnp.float32}
MOMENT_SCALE = {'norm_mix_0': 2.969702e-01, 'w_in_0': 1.699166e-01, 'b_f_0': 5.544945e-01, 'conv_w_0': 2.350884e-01, 'w_out_0': 1.751358e-01, 'norm_ffn_0': 2.165828e-01, 'w_up_0': 1.099368e-01, 'w_down_0': 3.609842e-01, 'norm_mix_1': 1.306589e-01, 'pool_w_1': 1.333604e-01, 'pool_scale_1': 7.709518e-01, 'norm_ffn_1': 1.877965e-01, 'w_up_1': 9.773152e-02, 'w_down_1': 3.725451e-01, 'final_norm': 6.480412e+01}


def _to_microbatches(a, axis):
    t = _jnp.moveaxis(a, axis, 0)
    t = t.reshape((N_MICROBATCH, t.shape[0] // N_MICROBATCH) + t.shape[1:])
    return _jnp.moveaxis(t, 1, axis + 1)


def setup_inputs(seed: int = 0) -> dict:
    inp = _fwd_setup_inputs(seed)
    key = _jax.random.fold_in(_jax.random.key(seed), 7919)
    shape, _ = _output_shape()
    out = dict(inp)
    out["loss_target"] = _jax.random.normal(_jax.random.fold_in(key, 0), shape, _jnp.float32)
    for i, name in enumerate(TWIN_WEIGHTS):
        w = inp[name].astype(_jnp.float32)
        if MOMENT_SCALE is None:
            s = _jnp.sqrt(_jnp.mean(_jnp.square(w)) + 1e-30)
        else:
            s = MOMENT_SCALE[name]
        km, kv = _jax.random.split(_jax.random.fold_in(key, i + 1))
        out[name] = w
        out["m_" + name] = s * _jax.random.normal(km, w.shape, _jnp.float32)
        out["v_" + name] = (s * s) * _jax.random.uniform(kv, w.shape, _jnp.float32, 0.5, 1.5)
    if N_MICROBATCH > 1:
        for name, axis in PER_EXAMPLE_BATCH_AXIS.items():
            out[name] = _to_microbatches(out[name], axis)
    return {'x': out['x'], 'norm_mix_0': out['norm_mix_0'], 'w_in_0': out['w_in_0'], 'b_f_0': out['b_f_0'], 'conv_w_0': out['conv_w_0'], 'w_out_0': out['w_out_0'], 'norm_ffn_0': out['norm_ffn_0'], 'w_up_0': out['w_up_0'], 'w_down_0': out['w_down_0'], 'norm_mix_1': out['norm_mix_1'], 'pool_w_1': out['pool_w_1'], 'pool_scale_1': out['pool_scale_1'], 'norm_ffn_1': out['norm_ffn_1'], 'w_up_1': out['w_up_1'], 'w_down_1': out['w_down_1'], 'final_norm': out['final_norm'], 'loss_target': out['loss_target'], 'm_norm_mix_0': out['m_norm_mix_0'], 'm_w_in_0': out['m_w_in_0'], 'm_b_f_0': out['m_b_f_0'], 'm_conv_w_0': out['m_conv_w_0'], 'm_w_out_0': out['m_w_out_0'], 'm_norm_ffn_0': out['m_norm_ffn_0'], 'm_w_up_0': out['m_w_up_0'], 'm_w_down_0': out['m_w_down_0'], 'm_norm_mix_1': out['m_norm_mix_1'], 'm_pool_w_1': out['m_pool_w_1'], 'm_pool_scale_1': out['m_pool_scale_1'], 'm_norm_ffn_1': out['m_norm_ffn_1'], 'm_w_up_1': out['m_w_up_1'], 'm_w_down_1': out['m_w_down_1'], 'm_final_norm': out['m_final_norm'], 'v_norm_mix_0': out['v_norm_mix_0'], 'v_w_in_0': out['v_w_in_0'], 'v_b_f_0': out['v_b_f_0'], 'v_conv_w_0': out['v_conv_w_0'], 'v_w_out_0': out['v_w_out_0'], 'v_norm_ffn_0': out['v_norm_ffn_0'], 'v_w_up_0': out['v_w_up_0'], 'v_w_down_0': out['v_w_down_0'], 'v_norm_mix_1': out['v_norm_mix_1'], 'v_pool_w_1': out['v_pool_w_1'], 'v_pool_scale_1': out['v_pool_scale_1'], 'v_norm_ffn_1': out['v_norm_ffn_1'], 'v_w_up_1': out['v_w_up_1'], 'v_w_down_1': out['v_w_down_1'], 'v_final_norm': out['v_final_norm']}


def _loss(weights, diff, rest, loss_target):
    with _jax.named_scope("forward"):
        args = {**rest, TWIN_DIFF_INPUT: diff, **{k: w.astype(_WEIGHT_DTYPES[k]) for k, w in weights.items()}}
        y = _forward(args)
    with _jax.named_scope("loss_head"):
        err = _jnp.square(y.astype(_jnp.float32) - loss_target)
        return 0.5 * _jnp.sum(_jnp.mean(err, axis=-1)) if err.ndim else 0.5 * err


def _adamw(w, g, m, v):
    m = ADAM_B1 * m + (1.0 - ADAM_B1) * g
    v = ADAM_B2 * v + (1.0 - ADAM_B2) * _jnp.square(g)
    m_hat = m / (1.0 - ADAM_B1 ** ADAM_STEP)
    v_hat = v / (1.0 - ADAM_B2 ** ADAM_STEP)
    delta = -ADAM_LR * (m_hat / (_jnp.sqrt(v_hat) + ADAM_EPS) + ADAM_WD * w)
    return delta, m, v


def reference(x, norm_mix_0, w_in_0, b_f_0, conv_w_0, w_out_0, norm_ffn_0, w_up_0, w_down_0, norm_mix_1, pool_w_1, pool_scale_1, norm_ffn_1, w_up_1, w_down_1, final_norm, loss_target, m_norm_mix_0, m_w_in_0, m_b_f_0, m_conv_w_0, m_w_out_0, m_norm_ffn_0, m_w_up_0, m_w_down_0, m_norm_mix_1, m_pool_w_1, m_pool_scale_1, m_norm_ffn_1, m_w_up_1, m_w_down_1, m_final_norm, v_norm_mix_0, v_w_in_0, v_b_f_0, v_conv_w_0, v_w_out_0, v_norm_ffn_0, v_w_up_0, v_w_down_0, v_norm_mix_1, v_pool_w_1, v_pool_scale_1, v_norm_ffn_1, v_w_up_1, v_w_down_1, v_final_norm):
    given = dict(x=x, norm_mix_0=norm_mix_0, w_in_0=w_in_0, b_f_0=b_f_0, conv_w_0=conv_w_0, w_out_0=w_out_0, norm_ffn_0=norm_ffn_0, w_up_0=w_up_0, w_down_0=w_down_0, norm_mix_1=norm_mix_1, pool_w_1=pool_w_1, pool_scale_1=pool_scale_1, norm_ffn_1=norm_ffn_1, w_up_1=w_up_1, w_down_1=w_down_1, final_norm=final_norm, loss_target=loss_target, m_norm_mix_0=m_norm_mix_0, m_w_in_0=m_w_in_0, m_b_f_0=m_b_f_0, m_conv_w_0=m_conv_w_0, m_w_out_0=m_w_out_0, m_norm_ffn_0=m_norm_ffn_0, m_w_up_0=m_w_up_0, m_w_down_0=m_w_down_0, m_norm_mix_1=m_norm_mix_1, m_pool_w_1=m_pool_w_1, m_pool_scale_1=m_pool_scale_1, m_norm_ffn_1=m_norm_ffn_1, m_w_up_1=m_w_up_1, m_w_down_1=m_w_down_1, m_final_norm=m_final_norm, v_norm_mix_0=v_norm_mix_0, v_w_in_0=v_w_in_0, v_b_f_0=v_b_f_0, v_conv_w_0=v_conv_w_0, v_w_out_0=v_w_out_0, v_norm_ffn_0=v_norm_ffn_0, v_w_up_0=v_w_up_0, v_w_down_0=v_w_down_0, v_norm_mix_1=v_norm_mix_1, v_pool_w_1=v_pool_w_1, v_pool_scale_1=v_pool_scale_1, v_norm_ffn_1=v_norm_ffn_1, v_w_up_1=v_w_up_1, v_w_down_1=v_w_down_1, v_final_norm=v_final_norm)
    weights = {n: given[n] for n in TWIN_WEIGHTS}
    shared = {n: given[n] for n in SHARED_INPUTS}
    per_example = {n: given[n] for n in ['x']}
    grad_fn = _jax.value_and_grad(_loss, argnums=(0, 1))

    def one_microbatch(ex, loss_target):
        ex = dict(ex)
        diff = ex.pop(TWIN_DIFF_INPUT)
        return grad_fn(weights, diff, {**shared, **ex}, loss_target)

    if N_MICROBATCH == 1:
        loss, (grad_w, grad_x) = one_microbatch(per_example, given["loss_target"])
    else:
        def body(carry, xs):
            loss_sum, grad_sum = carry
            l_k, (gw_k, gx_k) = one_microbatch(xs[0], xs[1])
            with _jax.named_scope("update"):
                return (loss_sum + l_k, _jax.tree.map(_jnp.add, grad_sum, gw_k)), gx_k

        init = (_jnp.zeros((), _jnp.float32), _jax.tree.map(_jnp.zeros_like, weights))
        (loss, grad_w), grad_x = _jax.lax.scan(body, init, (per_example, given["loss_target"]))
    with _jax.named_scope("update"):
        delta_w, new_m, new_v = {}, {}, {}
        for n in TWIN_WEIGHTS:
            delta_w[n], new_m[n], new_v[n] = _adamw(weights[n], grad_w[n], given["m_" + n], given["v_" + n])
    return (loss, grad_x, *[grad_w[n] for n in TWIN_WEIGHTS], *[delta_w[n] for n in TWIN_WEIGHTS],
            *[new_m[n] for n in TWIN_WEIGHTS], *[new_v[n] for n in TWIN_WEIGHTS])
```

```python
import functools

import jax
import jax.numpy as jnp
from jax import lax
from jax.experimental import pallas as pl
from jax.experimental.pallas import tpu as pltpu

F32 = jnp.float32
BF16 = jnp.bfloat16

N_DEV = 8
N_HEADS = 8
HEAD_DIM = 64
PAIR = 2 * HEAD_DIM
ATTN_W = N_HEADS * HEAD_DIM
CONV_CH = 512
F_PAD = 128
POOL_WINDOWS = (2, 4, 8, 16)
POOL_HALO = 16
CONV_HALO = 8
RMS_EPS = 1e-6
Q_SCALE = HEAD_DIM ** -0.5
NEG = -1e30
ADAM_LR, ADAM_B1, ADAM_B2, ADAM_EPS, ADAM_WD, ADAM_STEP = 0.001, 0.9, 0.999, 1e-08, 0.01, 10
MESH = pl.DeviceIdType.MESH
VMEM_LIMIT = 56 * 2**20


def _cp(sem=None, vmem=VMEM_LIMIT, **kw):
    return pltpu.CompilerParams(dimension_semantics=sem, vmem_limit_bytes=vmem, **kw)


def _dot(a, b):
    return jnp.dot(a, b, preferred_element_type=F32)


def _dot_nt(a, b):
    return lax.dot_general(a, b, (((1,), (1,)), ((), ())), preferred_element_type=F32)


def _dot_tn(a, b):
    return lax.dot_general(a, b, (((0,), (0,)), ((), ())), preferred_element_type=F32)


def _rstd(h):
    return lax.rsqrt(jnp.mean(h * h, axis=-1, keepdims=True) + RMS_EPS)


def _rows8(x):
    r, n = x.shape
    return jnp.sum(x.reshape(r // 8, 8, n), axis=0)


def _norm_bwd(dn, h, g):
    r = _rstd(h)
    xhat = h * r
    dy = dn * g
    dh = r * (dy - xhat * jnp.mean(dy * xhat, axis=-1, keepdims=True))
    return dh, _rows8(dn * xhat)


def _const_spec(shape):
    nd = len(shape)
    return pl.BlockSpec(shape, lambda *_: (0,) * nd, pipeline_mode=pl.Buffered(1))


def _norm_inproj(x, g, win_p, *, tm=512):
    t, d = x.shape
    n_all = win_p.shape[1]
    n_qkv = 3 * ATTN_W
    n_bcx = 3 * CONV_CH
    assert n_all == n_qkv + F_PAD + n_bcx
    tm = min(tm, t)

    def body(x_ref, g_ref, w_ref, n_ref, qkv_ref, f_ref, bcx_ref):
        h = x_ref[...]
        n = (h * _rstd(h) * g_ref[...]).astype(BF16)
        n_ref[...] = n
        for c0 in range(0, n_qkv, 512):
            acc = _dot(n, w_ref[:, c0:c0 + 512])
            if c0 < ATTN_W:
                acc = acc * Q_SCALE
            qkv_ref[:, c0:c0 + 512] = acc.astype(BF16)
        f_ref[...] = _dot(n, w_ref[:, n_qkv:n_qkv + F_PAD])
        for c0 in range(0, n_bcx, 512):
            bcx_ref[:, c0:c0 + 512] = _dot(n, w_ref[:, n_qkv + F_PAD + c0:n_qkv + F_PAD + c0 + 512])

    return pl.pallas_call(
        body, name="norm_inproj", grid=(t // tm,),
        in_specs=[pl.BlockSpec((tm, d), lambda i: (i, 0)), _const_spec((1, d)), _const_spec((d, n_all))],
        out_specs=[pl.BlockSpec((tm, d), lambda i: (i, 0)), pl.BlockSpec((tm, n_qkv), lambda i: (i, 0)),
                   pl.BlockSpec((tm, F_PAD), lambda i: (i, 0)), pl.BlockSpec((tm, n_bcx), lambda i: (i, 0))],
        out_shape=[jax.ShapeDtypeStruct((t, d), BF16), jax.ShapeDtypeStruct((t, n_qkv), BF16),
                   jax.ShapeDtypeStruct((t, F_PAD), F32), jax.ShapeDtypeStruct((t, n_bcx), F32)],
        compiler_params=_cp(("parallel",)),
    )(x, g, win_p)


def _fgate_fwd(flog, b_f, *, tm=256):
    t = flog.shape[0]
    tm = min(tm, t)

    def body(f_ref, b_ref, cum_ref, sg_ref, carry):
        @pl.when(pl.program_id(0) == 0)
        def _():
            carry[...] = jnp.zeros_like(carry)
        z = f_ref[...] + b_ref[...]
        e = jnp.exp(-jnp.abs(z))
        logf = jnp.minimum(z, 0.0) - jnp.log(1.0 + e)
        sg_ref[...] = jnp.where(z >= 0, e, 1.0) / (1.0 + e)
        r = lax.broadcasted_iota(jnp.int32, (tm, tm), 0)
        c = lax.broadcasted_iota(jnp.int32, (tm, tm), 1)
        tri = (c <= r).astype(F32)
        cs = jnp.dot(tri, logf, preferred_element_type=F32, precision=lax.Precision.HIGHEST) + carry[...]
        carry[...] = cs[tm - 1:tm, :]
        cst = cs.T
        for h in range(N_HEADS):
            cum_ref[h] = cst[h:h + 1, :]

    return pl.pallas_call(
        body, name="fgate_fwd", grid=(t // tm,),
        in_specs=[pl.BlockSpec((tm, F_PAD), lambda i: (i, 0)), _const_spec((1, F_PAD))],
        out_specs=[pl.BlockSpec((N_HEADS, 1, tm), lambda i: (0, 0, i)), pl.BlockSpec((tm, F_PAD), lambda i: (i, 0))],
        out_shape=[jax.ShapeDtypeStruct((N_HEADS, 1, t), F32), jax.ShapeDtypeStruct((t, F_PAD), F32)],
        scratch_shapes=[pltpu.VMEM((1, F_PAD), F32)],
        compiler_params=_cp(("arbitrary",)),
    )(flog, b_f)


def _attn_fwd(qkv, cum, *, tq=512):
    t = qkv.shape[0]
    tq = min(tq, t)
    tk = tq
    n_pairs = ATTN_W // PAIR

    def body(q_ref, k_ref, v_ref, ca_ref, cb_ref, o_ref, lse_ref):
        i = pl.program_id(1)
        lane = lax.broadcasted_iota(jnp.int32, (1, PAIR), 1)
        row = lax.broadcasted_iota(jnp.int32, (tq, tk), 0)
        col = lax.broadcasted_iota(jnp.int32, (tq, tk), 1)
        q = q_ref[...]
        res = []
        for hh, c_ref in ((0, ca_ref), (1, cb_ref)):
            qm = jnp.where(lane // HEAD_DIM == hh, q, jnp.zeros_like(q))

            def step(j, carry, diag, qm=qm, c_ref=c_ref):
                m, l, acc = carry
                ks = pl.multiple_of(j * tk, tk)
                k = k_ref[pl.ds(ks, tk), :]
                v = v_ref[pl.ds(ks, tk), :]
                s = _dot_nt(qm, k) - c_ref[:, pl.ds(ks, tk)]
                if diag:
                    s = jnp.where(col <= row, s, NEG)
                mn = jnp.maximum(m, jnp.max(s, axis=1, keepdims=True))
                a = jnp.exp(m - mn)
                p = jnp.exp(s - mn)
                l = a * l + jnp.sum(p, axis=1, keepdims=True)
                acc = a * acc + _dot(p.astype(BF16), v)
                return mn, l, acc

            init = (jnp.full((tq, 1), NEG, F32), jnp.zeros((tq, 1), F32), jnp.zeros((tq, PAIR), F32))
            carry = lax.fori_loop(0, i, functools.partial(step, diag=False), init)
            m, l, acc = step(i, carry, True)
            res.append((acc / l, m + jnp.log(l)))
        sel = lane < HEAD_DIM
        o_ref[...] = jnp.where(sel, res[0][0], res[1][0]).astype(BF16)
        lse_ref[...] = jnp.where(sel, res[0][1], res[1][1])

    kv_spec = lambda off: pl.BlockSpec((t, PAIR), lambda p, i: (0, off + p), pipeline_mode=pl.Buffered(1))
    cum_spec = lambda hh: pl.BlockSpec((None, 1, t), lambda p, i: (2 * p + hh, 0, 0), pipeline_mode=pl.Buffered(1))
    return pl.pallas_call(
        body, name="attn_fwd", grid=(n_pairs, t // tq),
        in_specs=[pl.BlockSpec((tq, PAIR), lambda p, i: (i, p)), kv_spec(n_pairs), kv_spec(2 * n_pairs),
                  cum_spec(0), cum_spec(1)],
        out_specs=[pl.BlockSpec((tq, PAIR), lambda p, i: (i, p)), pl.BlockSpec((tq, PAIR), lambda p, i: (i, p))],
        out_shape=[jax.ShapeDtypeStruct((t, ATTN_W), BF16), jax.ShapeDtypeStruct((t, ATTN_W), F32)],
        compiler_params=_cp(("parallel", "parallel")),
    )(qkv, qkv, qkv, cum, cum)


def _prev_halo(tm, halo):
    return lambda i: (jnp.maximum(i * (tm // halo) - 1, 0), 0)


def _next_halo(tm, halo, t):
    return lambda i: (jnp.minimum((i + 1) * (tm // halo), t // halo - 1), 0)


def _conv_fwd(bcx, conv_w, *, tm=512):
    t = bcx.shape[0]
    tm = min(tm, t)
    ch = CONV_CH

    def body(b_ref, c_ref, x_ref, hc_ref, hx_ref, w_ref, cv_ref, ext):
        first = pl.program_id(0) == 0
        ext[0:CONV_HALO, :] = jnp.where(first, 0.0, hc_ref[...] * hx_ref[...])
        ext[CONV_HALO:CONV_HALO + tm, :] = c_ref[...] * x_ref[...]
        conv = (w_ref[0:1, :] * ext[CONV_HALO - 2:CONV_HALO - 2 + tm, :]
                + w_ref[1:2, :] * ext[CONV_HALO - 1:CONV_HALO - 1 + tm, :]
                + w_ref[2:3, :] * ext[CONV_HALO:CONV_HALO + tm, :])
        cv_ref[...] = (b_ref[...] * conv).astype(BF16)

    col = lambda k: pl.BlockSpec((tm, ch), lambda i: (i, k))
    halo = lambda k: pl.BlockSpec((CONV_HALO, ch), lambda i: (_prev_halo(tm, CONV_HALO)(i)[0], k))
    return pl.pallas_call(
        body, name="conv_fwd", grid=(t // tm,),
        in_specs=[col(0), col(1), col(2), halo(1), halo(2), _const_spec((8, ch))],
        out_specs=pl.BlockSpec((tm, ch), lambda i: (i, 0)),
        out_shape=jax.ShapeDtypeStruct((t, ch), BF16),
        scratch_shapes=[pltpu.VMEM((CONV_HALO + tm, ch), F32)],
        compiler_params=_cp(("parallel",)),
    )(bcx, bcx, bcx, bcx, bcx, conv_w)


def _outproj(att, cv, x, wout, *, tm=512):
    t, d = x.shape
    tm = min(tm, t)

    def body(a_ref, c_ref, x_ref, w_ref, h_ref):
        h_ref[...] = x_ref[...] + _dot(a_ref[...], w_ref[0:ATTN_W, :]) + _dot(c_ref[...], w_ref[ATTN_W:, :])

    return pl.pallas_call(
        body, name="outproj", grid=(t // tm,),
        in_specs=[pl.BlockSpec((tm, ATTN_W), lambda i: (i, 0)), pl.BlockSpec((tm, CONV_CH), lambda i: (i, 0)),
                  pl.BlockSpec((tm, d), lambda i: (i, 0)), _const_spec(wout.shape)],
        out_specs=pl.BlockSpec((tm, d), lambda i: (i, 0)),
        out_shape=jax.ShapeDtypeStruct((t, d), F32),
        compiler_params=_cp(("parallel",)),
    )(att, cv, x, wout)


def _mlp_fwd(h, g, wup, wdown, *, name, tm=256):
    t, d = h.shape
    n_blk, _, fb = wup.shape
    f = n_blk * fb
    tm = min(tm, t)

    def body(h_ref, g_ref, wu_ref, wd_ref, ho_ref, n_ref, a_ref, z_ref):
        hh = h_ref[...]
        n = (hh * _rstd(hh) * g_ref[...]).astype(BF16)
        n_ref[...] = n
        acc = hh
        for k in range(n_blk):
            a = _dot(n, wu_ref[k])
            zz = jnp.square(jnp.maximum(a, 0.0)).astype(BF16)
            a_ref[:, k * fb:(k + 1) * fb] = a.astype(BF16)
            z_ref[:, k * fb:(k + 1) * fb] = zz
            acc = acc + _dot(zz, wd_ref[k * fb:(k + 1) * fb, :])
        ho_ref[...] = acc

    row = lambda n_: pl.BlockSpec((tm, n_), lambda i: (i, 0))
    return pl.pallas_call(
        body, name=name, grid=(t // tm,),
        in_specs=[row(d), _const_spec((1, d)), _const_spec(wup.shape), _const_spec(wdown.shape)],
        out_specs=[row(d), row(d), row(f), row(f)],
        out_shape=[jax.ShapeDtypeStruct((t, d), F32), jax.ShapeDtypeStruct((t, d), BF16),
                   jax.ShapeDtypeStruct((t, f), BF16), jax.ShapeDtypeStruct((t, f), BF16)],
        compiler_params=_cp(("parallel",)),
    )(h, g, wup, wdown)


def _pool_inv_count(i, tm):
    tglob = (i * tm + lax.broadcasted_iota(jnp.int32, (tm, 1), 0) + 1).astype(F32)
    return [1.0 / jnp.minimum(tglob, float(w)) for w in POOL_WINDOWS]


def _pool_fwd(h, g, poolw, scale, *, tm=256):
    t, d = h.shape
    tm = min(tm, t)
    cg = d // len(POOL_WINDOWS)

    def body(h_ref, hh_ref, g_ref, w_ref, s_ref, ho_ref, p_ref, ext):
        i = pl.program_id(0)
        hv = h_ref[...]
        halo = hh_ref[...]
        n = hv * _rstd(hv) * g_ref[...]
        ext[0:POOL_HALO, :] = jnp.where(i == 0, 0.0, halo * _rstd(halo) * g_ref[...])
        ext[POOL_HALO:POOL_HALO + tm, :] = n
        inv = _pool_inv_count(i, tm)
        for gi, w in enumerate(POOL_WINDOWS):
            cs = slice(gi * cg, (gi + 1) * cg)
            s = ext[POOL_HALO:POOL_HALO + tm, cs]
            for j in range(1, w):
                s = s + ext[POOL_HALO - j:POOL_HALO - j + tm, cs]
            pooled = (s * inv[gi] - n[:, cs]).astype(BF16)
            p_ref[:, cs] = pooled
            ho_ref[:, cs] = hv[:, cs] + _dot(pooled, w_ref[gi]) * s_ref[:, cs]

    row = lambda: pl.BlockSpec((tm, d), lambda i: (i, 0))
    return pl.pallas_call(
        body, name="pool_fwd", grid=(t // tm,),
        in_specs=[row(), pl.BlockSpec((POOL_HALO, d), _prev_halo(tm, POOL_HALO)), _const_spec((1, d)),
                  _const_spec(poolw.shape), _const_spec((1, d))],
        out_specs=[row(), row()],
        out_shape=[jax.ShapeDtypeStruct((t, d), F32), jax.ShapeDtypeStruct((t, d), BF16)],
        scratch_shapes=[pltpu.VMEM((POOL_HALO + tm, d), F32)],
        compiler_params=_cp(("parallel",)),
    )(h, h, g, poolw, scale)


def _loss_bwd(h, g, target, *, tm=512):
    t, d = h.shape
    tm = min(tm, t)
    nsteps = t // tm

    def body(h_ref, g_ref, y_ref, loss_ref, dh_ref, dg_ref, lacc):
        i = pl.program_id(0)

        @pl.when(i == 0)
        def _():
            lacc[...] = jnp.zeros_like(lacc)
            dg_ref[...] = jnp.zeros_like(dg_ref)
        hv = h_ref[...]
        gv = g_ref[...]
        r = _rstd(hv)
        xhat = hv * r
        err = xhat * gv - y_ref[...]
        lacc[...] += _rows8(err * err)
        dout = err * (1.0 / d)
        dy = dout * gv
        dg_ref[...] += _rows8(dout * xhat)
        dh_ref[...] = r * (dy - xhat * jnp.mean(dy * xhat, axis=-1, keepdims=True))

        @pl.when(i == nsteps - 1)
        def _():
            loss_ref[...] = jnp.full(loss_ref.shape, (0.5 / d) * jnp.sum(lacc[...]), F32)

    row = lambda: pl.BlockSpec((tm, d), lambda i: (i, 0))
    return pl.pallas_call(
        body, name="loss_bwd", grid=(nsteps,),
        in_specs=[row(), _const_spec((1, d)), row()],
        out_specs=[pl.BlockSpec((8, 128), lambda i: (0, 0)), row(), pl.BlockSpec((8, d), lambda i: (0, 0))],
        out_shape=[jax.ShapeDtypeStruct((8, 128), F32), jax.ShapeDtypeStruct((t, d), F32),
                   jax.ShapeDtypeStruct((8, d), F32)],
        scratch_shapes=[pltpu.VMEM((8, d), F32)],
        compiler_params=_cp(("arbitrary",)),
    )(h, g, target)


def _mm_tn(a, b, *, name, ta, tb, tt, blocked_out=False):
    t, ka = a.shape
    n = b.shape[1]
    ta, tb, tt = min(ta, ka), min(tb, n), min(tt, t)

    def body(a_ref, b_ref, o_ref):
        @pl.when(pl.program_id(2) == 0)
        def _():
            o_ref[...] = jnp.zeros_like(o_ref)
        o_ref[...] += _dot_tn(a_ref[...].astype(BF16), b_ref[...].astype(BF16))

    if blocked_out:
        assert ta == ka
        out_shape = jax.ShapeDtypeStruct((n // tb, ka, tb), F32)
        out_spec = pl.BlockSpec((None, ta, tb), lambda i, j, k: (j, i, 0))
    else:
        out_shape = jax.ShapeDtypeStruct((ka, n), F32)
        out_spec = pl.BlockSpec((ta, tb), lambda i, j, k: (i, j))
    return pl.pallas_call(
        body, name=name, grid=(ka // ta, n // tb, t // tt),
        in_specs=[pl.BlockSpec((tt, ta), lambda i, j, k: (k, i)), pl.BlockSpec((tt, tb), lambda i, j, k: (k, j))],
        out_specs=out_spec, out_shape=out_shape,
        compiler_params=_cp(("parallel", "parallel", "arbitrary")),
    )(a, b)


def _mlp_bwd(dho, h, a, g, wup, wdown, *, name, tm=256):
    t, d = h.shape
    n_blk, _, fb = wup.shape
    f = n_blk * fb
    tm = min(tm, t)

    def body(do_ref, h_ref, a_ref, g_ref, wu_ref, wd_ref, dh_ref, da_ref, dg_ref):
        @pl.when(pl.program_id(0) == 0)
        def _():
            dg_ref[...] = jnp.zeros_like(dg_ref)
        dho_v = do_ref[...]
        dob = dho_v.astype(BF16)
        dn = jnp.zeros((tm, d), F32)
        for k in range(n_blk):
            dz = _dot_nt(dob, wd_ref[k * fb:(k + 1) * fb, :])
            da = (dz * (2.0 * jnp.maximum(a_ref[:, k * fb:(k + 1) * fb].astype(F32), 0.0))).astype(BF16)
            da_ref[:, k * fb:(k + 1) * fb] = da
            dn = dn + _dot_nt(da, wu_ref[k])
        dh, dg = _norm_bwd(dn, h_ref[...], g_ref[...])
        dh_ref[...] = dho_v + dh
        dg_ref[...] += dg

    row = lambda n_: pl.BlockSpec((tm, n_), lambda i: (i, 0))
    return pl.pallas_call(
        body, name=name, grid=(t // tm,),
        in_specs=[row(d), row(d), row(f), _const_spec((1, d)), _const_spec(wup.shape), _const_spec(wdown.shape)],
        out_specs=[row(d), row(f), pl.BlockSpec((8, d), lambda i: (0, 0))],
        out_shape=[jax.ShapeDtypeStruct((t, d), F32), jax.ShapeDtypeStruct((t, f), BF16),
                   jax.ShapeDtypeStruct((8, d), F32)],
        compiler_params=_cp(("arbitrary",)),
    )(dho, h, a, g, wup, wdown)


def _pool_bwd(dho, h, pooled, g, poolw, scale, *, tm=256):
    t, d = h.shape
    tm = min(tm, t)
    ng = len(POOL_WINDOWS)
    cg = d // ng
    nsteps = t // tm

    def body(do_ref, dn_ref, h_ref, p_ref, g_ref, w_ref, s_ref, dh_ref, dw_ref, ds_ref, dg_ref, ext):
        i = pl.program_id(0)

        @pl.when(i == 0)
        def _():
            dw_ref[...] = jnp.zeros_like(dw_ref)
            ds_ref[...] = jnp.zeros_like(ds_ref)
            dg_ref[...] = jnp.zeros_like(dg_ref)
        dho_v = do_ref[...]
        sv = s_ref[...]
        dyp = (dho_v * sv).astype(BF16)
        dyp_halo = (dn_ref[...] * sv).astype(BF16)
        inv = _pool_inv_count(i, tm)
        tnext = ((i + 1) * tm + lax.broadcasted_iota(jnp.int32, (POOL_HALO, 1), 0) + 1).astype(F32)
        last = i == nsteps - 1
        ypre_parts, dpooled_parts = [], []
        for gi, w in enumerate(POOL_WINDOWS):
            cs = slice(gi * cg, (gi + 1) * cg)
            pg = p_ref[:, cs]
            ypre_parts.append(_dot(pg, w_ref[gi]))
            dw_ref[gi] += _dot_tn(pg, dyp[:, cs])
            dpool = _dot_nt(dyp[:, cs], w_ref[gi])
            dpooled_parts.append(dpool)
            ext[0:tm, cs] = dpool * inv[gi]
            dpool_halo = _dot_nt(dyp_halo[:, cs], w_ref[gi]) * (1.0 / jnp.minimum(tnext, float(w)))
            ext[tm:tm + POOL_HALO, cs] = jnp.where(last, 0.0, dpool_halo)
        ds_ref[...] += _rows8(dho_v * jnp.concatenate(ypre_parts, axis=1))
        dn_parts = []
        for gi, w in enumerate(POOL_WINDOWS):
            cs = slice(gi * cg, (gi + 1) * cg)
            s = ext[0:tm, cs]
            for j in range(1, w):
                s = s + ext[j:j + tm, cs]
            dn_parts.append(s - dpooled_parts[gi])
        dh, dg = _norm_bwd(jnp.concatenate(dn_parts, axis=1), h_ref[...], g_ref[...])
        dh_ref[...] = dho_v + dh
        dg_ref[...] += dg

    row = lambda: pl.BlockSpec((tm, d), lambda i: (i, 0))
    acc8 = lambda: pl.BlockSpec((8, d), lambda i: (0, 0))
    return pl.pallas_call(
        body, name="pool_bwd", grid=(nsteps,),
        in_specs=[row(), pl.BlockSpec((POOL_HALO, d), _next_halo(tm, POOL_HALO, t)), row(), row(),
                  _const_spec((1, d)), _const_spec(poolw.shape), _const_spec((1, d))],
        out_specs=[row(), pl.BlockSpec((ng, cg, cg), lambda i: (0, 0, 0)), acc8(), acc8()],
        out_shape=[jax.ShapeDtypeStruct((t, d), F32), jax.ShapeDtypeStruct((ng, cg, cg), F32),
                   jax.ShapeDtypeStruct((8, d), F32), jax.ShapeDtypeStruct((8, d), F32)],
        scratch_shapes=[pltpu.VMEM((tm + POOL_HALO, d), F32)],
        compiler_params=_cp(("arbitrary",)),
    )(dho, dho, h, pooled, g, poolw, scale)


def _outproj_bwd(dh, o, wout, *, tm=512):
    t, d = dh.shape
    tm = min(tm, t)

    def body(dh_ref, o_ref, w_ref, da_ref, dl_ref, dc_ref):
        dhb = dh_ref[...].astype(BF16)
        datt = _dot_nt(dhb, w_ref[0:ATTN_W, :])
        da_ref[...] = datt.astype(BF16)
        dc_ref[...] = _dot_nt(dhb, w_ref[ATTN_W:, :])
        r = lax.broadcasted_iota(jnp.int32, (ATTN_W, ATTN_W), 0) // HEAD_DIM
        c = lax.broadcasted_iota(jnp.int32, (ATTN_W, ATTN_W), 1) // HEAD_DIM
        seg = (r == c).astype(F32)
        dl_ref[...] = jnp.dot(datt * o_ref[...].astype(F32), seg, preferred_element_type=F32,
                              precision=lax.Precision.HIGHEST)

    row = lambda n_: pl.BlockSpec((tm, n_), lambda i: (i, 0))
    return pl.pallas_call(
        body, name="outproj_bwd", grid=(t // tm,),
        in_specs=[row(d), row(ATTN_W), _const_spec(wout.shape)],
        out_specs=[row(ATTN_W), row(ATTN_W), row(CONV_CH)],
        out_shape=[jax.ShapeDtypeStruct((t, ATTN_W), BF16), jax.ShapeDtypeStruct((t, ATTN_W), F32),
                   jax.ShapeDtypeStruct((t, CONV_CH), F32)],
        compiler_params=_cp(("parallel",)),
    )(dh, o, wout)


def _conv_bwd(bcx, dcv, conv_w, *, tm=512):
    t = bcx.shape[0]
    tm = min(tm, t)
    ch = CONV_CH
    nsteps = t // tm

    def body(b_ref, c_ref, x_ref, hc_ref, hx_ref, d_ref, nb_ref, nd_ref, w_ref, o_ref, dw_ref, ext_u, ext_d):
        i = pl.program_id(0)

        @pl.when(i == 0)
        def _():
            dw_ref[...] = jnp.zeros_like(dw_ref)
        b, c, x, dcv_v = b_ref[...], c_ref[...], x_ref[...], d_ref[...]
        ext_u[0:CONV_HALO, :] = jnp.where(i == 0, 0.0, hc_ref[...] * hx_ref[...])
        ext_u[CONV_HALO:CONV_HALO + tm, :] = c * x
        dconv = dcv_v * b
        ext_d[0:tm, :] = dconv
        ext_d[tm:tm + CONV_HALO, :] = jnp.where(i == nsteps - 1, 0.0, nd_ref[...] * nb_ref[...])
        u = [ext_u[CONV_HALO - 2 + k:CONV_HALO - 2 + k + tm, :] for k in range(3)]
        conv = w_ref[0:1, :] * u[0] + w_ref[1:2, :] * u[1] + w_ref[2:3, :] * u[2]
        du = (w_ref[2:3, :] * dconv + w_ref[1:2, :] * ext_d[1:1 + tm, :] + w_ref[0:1, :] * ext_d[2:2 + tm, :])
        o_ref[:, 0:ch] = (dcv_v * conv).astype(BF16)
        o_ref[:, ch:2 * ch] = (du * x).astype(BF16)
        o_ref[:, 2 * ch:3 * ch] = (du * c).astype(BF16)
        for k in range(3):
            dw_ref[k] += _rows8(dconv * u[k])

    col = lambda k: pl.BlockSpec((tm, ch), lambda i: (i, k))
    prev = lambda k: pl.BlockSpec((CONV_HALO, ch), lambda i: (_prev_halo(tm, CONV_HALO)(i)[0], k))
    nxt = lambda k: pl.BlockSpec((CONV_HALO, ch), lambda i: (_next_halo(tm, CONV_HALO, t)(i)[0], k))
    return pl.pallas_call(
        body, name="conv_bwd", grid=(nsteps,),
        in_specs=[col(0), col(1), col(2), prev(1), prev(2), col(0), nxt(0), nxt(0), _const_spec((8, ch))],
        out_specs=[pl.BlockSpec((tm, 3 * ch), lambda i: (i, 0)), pl.BlockSpec((3, 8, ch), lambda i: (0, 0, 0))],
        out_shape=[jax.ShapeDtypeStruct((t, 3 * ch), BF16), jax.ShapeDtypeStruct((3, 8, ch), F32)],
        scratch_shapes=[pltpu.VMEM((CONV_HALO + tm, ch), F32), pltpu.VMEM((tm + CONV_HALO, ch), F32)],
        compiler_params=_cp(("arbitrary",)),
    )(bcx, bcx, bcx, bcx, bcx, dcv, bcx, dcv, conv_w)


def _attn_bwd(qkv, do, lse, delta, cum, *, tq=512):
    t = qkv.shape[0]
    tq = min(tq, t)
    tk = tq
    nq = t // tq
    n_pairs = ATTN_W // PAIR

    def body(q_ref, do_ref, lse_ref, dl_ref, k_ref, v_ref, ca_ref, cb_ref,
             dq_ref, dk_ref, dv_ref, dca_ref, dcb_ref, dr_ref, dq_scr, dr_scr):
        j = pl.program_id(1)

        @pl.when(j == 0)
        def _():
            dq_scr[...] = jnp.zeros_like(dq_scr)
            dr_scr[...] = jnp.zeros_like(dr_scr)
        lane = lax.broadcasted_iota(jnp.int32, (1, PAIR), 1)
        row = lax.broadcasted_iota(jnp.int32, (tq, tk), 0)
        col = lax.broadcasted_iota(jnp.int32, (tq, tk), 1)
        k = k_ref[...]
        v = v_ref[...]
        dk_acc = jnp.zeros((tk, PAIR), F32)
        dv_acc = jnp.zeros((tk, PAIR), F32)
        for hh, c_ref, dc_ref in ((0, ca_ref, dca_ref), (1, cb_ref, dcb_ref)):
            hm = lane // HEAD_DIM == hh
            km = jnp.where(hm, k, jnp.zeros_like(k))
            vm = jnp.where(hm, v, jnp.zeros_like(v))
            ck = c_ref[...]

            def step(i, carry, diag, hm=hm, km=km, vm=vm, ck=ck, hh=hh):
                dk_a, dv_a, dck = carry
                qs = pl.multiple_of(i * tq, tq)
                q = q_ref[pl.ds(qs, tq), :]
                dov = do_ref[pl.ds(qs, tq), :]
                lcol = lse_ref[pl.ds(qs, tq), hh * HEAD_DIM:hh * HEAD_DIM + 1]
                dcol = dl_ref[pl.ds(qs, tq), hh * HEAD_DIM:hh * HEAD_DIM + 1]
                s = _dot_nt(q, km) - ck
                p = jnp.exp(s - lcol)
                if diag:
                    p = jnp.where(col <= row, p, 0.0)
                dp = _dot_nt(dov, vm)
                ds = p * (dp - dcol)
                pb = p.astype(BF16)
                dsb = ds.astype(BF16)
                dv_a = dv_a + _dot_tn(pb, jnp.where(hm, dov, jnp.zeros_like(dov)))
                dk_a = dk_a + _dot_tn(dsb, jnp.where(hm, q, jnp.zeros_like(q)))
                dq_scr[pl.ds(qs, tq), :] += _dot(dsb, km)
                dr_scr[pl.ds(qs, tq), :] += jnp.where(hm, jnp.sum(ds, axis=1, keepdims=True), 0.0)
                dck = dck + jnp.sum(ds, axis=0, keepdims=True)
                return dk_a, dv_a, dck

            carry = step(j, (dk_acc, dv_acc, jnp.zeros((1, tk), F32)), True)
            dk_acc, dv_acc, dck = lax.fori_loop(j + 1, nq, functools.partial(step, diag=False), carry)
            dc_ref[...] = dck
        dk_ref[...] = dk_acc.astype(BF16)
        dv_ref[...] = dv_acc.astype(BF16)

        @pl.when(j == nq - 1)
        def _():
            dq_ref[...] = (dq_scr[...] * Q_SCALE).astype(BF16)
            dr_ref[...] = dr_scr[...]

    res = lambda dt: pl.BlockSpec((t, PAIR), lambda p, j: (0, p), pipeline_mode=pl.Buffered(1))
    kv_in = lambda off: pl.BlockSpec((tk, PAIR), lambda p, j: (j, off + p))
    cum_in = lambda hh: pl.BlockSpec((None, 1, tk), lambda p, j: (2 * p + hh, 0, j))
    dqkv_shape = jax.ShapeDtypeStruct((t, ATTN_W), BF16)
    dcum_shape = jax.ShapeDtypeStruct((n_pairs, 1, t), F32)
    dq, dk, dv, dca, dcb, drow = pl.pallas_call(
        body, name="attn_bwd", grid=(n_pairs, nq),
        in_specs=[res(BF16), res(BF16), res(F32), res(F32), kv_in(n_pairs), kv_in(2 * n_pairs), cum_in(0), cum_in(1)],
        out_specs=[pl.BlockSpec((t, PAIR), lambda p, j: (0, p)), pl.BlockSpec((tk, PAIR), lambda p, j: (j, p)),
                   pl.BlockSpec((tk, PAIR), lambda p, j: (j, p)),
                   pl.BlockSpec((None, 1, tk), lambda p, j: (p, 0, j)),
                   pl.BlockSpec((None, 1, tk), lambda p, j: (p, 0, j)),
                   pl.BlockSpec((t, PAIR), lambda p, j: (0, p))],
        out_shape=[dqkv_shape, dqkv_shape, dqkv_shape, dcum_shape, dcum_shape,
                   jax.ShapeDtypeStruct((t, ATTN_W), F32)],
        scratch_shapes=[pltpu.VMEM((t, PAIR), F32), pltpu.VMEM((t, PAIR), F32)],
        compiler_params=_cp(("parallel", "arbitrary")),
    )(qkv, do, lse, delta, qkv, qkv, cum, cum)
    return dq, dk, dv, dca, dcb, drow


def _fgate_bwd(dca, dcb, drow, sgate, *, tm=256):
    t = sgate.shape[0]
    tm = min(tm, t)
    nsteps = t // tm

    def body(da_ref, db_ref, dr_ref, sg_ref, df_ref, dbf_ref, carry):
        @pl.when(pl.program_id(0) == 0)
        def _():
            carry[...] = jnp.zeros_like(carry)
            dbf_ref[...] = jnp.zeros_like(dbf_ref)
        rows = [(da_ref if h % 2 == 0 else db_ref)[h // 2] for h in range(N_HEADS)]
        x = jnp.concatenate(rows + [jnp.zeros((F_PAD - N_HEADS, tm), F32)], axis=0)
        hr = lax.broadcasted_iota(jnp.int32, (ATTN_W, F_PAD), 0)
        hc = lax.broadcasted_iota(jnp.int32, (ATTN_W, F_PAD), 1)
        pick = (hr == hc * HEAD_DIM).astype(F32)
        rsum = jnp.dot(dr_ref[...], pick, preferred_element_type=F32, precision=lax.Precision.HIGHEST)
        xt = rsum - x.T
        r = lax.broadcasted_iota(jnp.int32, (tm, tm), 0)
        c = lax.broadcasted_iota(jnp.int32, (tm, tm), 1)
        tri = (c >= r).astype(F32)
        rc = jnp.dot(tri, xt, preferred_element_type=F32, precision=lax.Precision.HIGHEST) + carry[...]
        carry[...] = rc[0:1, :]
        df = rc * sg_ref[...]
        df_ref[...] = df.astype(BF16)
        dbf_ref[...] += _rows8(df)

    rev = lambda i: nsteps - 1 - i
    cum_in = lambda: pl.BlockSpec((N_HEADS // 2, 1, tm), lambda i: (0, 0, rev(i)))
    return pl.pallas_call(
        body, name="fgate_bwd", grid=(nsteps,),
        in_specs=[cum_in(), cum_in(), pl.BlockSpec((tm, ATTN_W), lambda i: (rev(i), 0)),
                  pl.BlockSpec((tm, F_PAD), lambda i: (rev(i), 0))],
        out_specs=[pl.BlockSpec((tm, F_PAD), lambda i: (rev(i), 0)), pl.BlockSpec((8, F_PAD), lambda i: (0, 0))],
        out_shape=[jax.ShapeDtypeStruct((t, F_PAD), BF16), jax.ShapeDtypeStruct((8, F_PAD), F32)],
        scratch_shapes=[pltpu.VMEM((1, F_PAD), F32)],
        compiler_params=_cp(("arbitrary",)),
    )(dca, dcb, drow, sgate)


def _inproj_bwd(dq, dk, dv, df, dbcx, dh, x, g, win_p, *, tm=512):
    t, d = x.shape
    tm = min(tm, t)
    n_qkv = 3 * ATTN_W

    def body(dq_ref, dk_ref, dv_ref, df_ref, db_ref, dh_ref, x_ref, g_ref, w_ref, gx_ref, dg_ref):
        @pl.when(pl.program_id(0) == 0)
        def _():
            dg_ref[...] = jnp.zeros_like(dg_ref)
        dn = _dot_nt(df_ref[...], w_ref[:, n_qkv:n_qkv + F_PAD])
        for k, r in enumerate((dq_ref, dk_ref, dv_ref)):
            dn = dn + _dot_nt(r[...], w_ref[:, k * ATTN_W:(k + 1) * ATTN_W])
        for k in range(3):
            c0 = n_qkv + F_PAD + k * CONV_CH
            dn = dn + _dot_nt(db_ref[:, k * CONV_CH:(k + 1) * CONV_CH], w_ref[:, c0:c0 + CONV_CH])
        dx, dg = _norm_bwd(dn, x_ref[...], g_ref[...])
        gx_ref[...] = dh_ref[...] + dx
        dg_ref[...] += dg

    row = lambda n_: pl.BlockSpec((tm, n_), lambda i: (i, 0))
    return pl.pallas_call(
        body, name="inproj_bwd", grid=(t // tm,),
        in_specs=[row(ATTN_W), row(ATTN_W), row(ATTN_W), row(F_PAD), row(3 * CONV_CH), row(d), row(d),
                  _const_spec((1, d)), _const_spec(win_p.shape)],
        out_specs=[row(d), pl.BlockSpec((8, d), lambda i: (0, 0))],
        out_shape=[jax.ShapeDtypeStruct((t, d), F32), jax.ShapeDtypeStruct((8, d), F32)],
        compiler_params=_cp(("arbitrary",)),
    )(dq, dk, dv, df, dbcx, dh, x, g, win_p)


def _local_step(x, target, gains, b_f, conv_w, pool_scale, win_p, wout, wup0, wdown0, poolw, wup1, wdown1):
    d = x.shape[1]
    n0, qkv, flog, bcx = _norm_inproj(x, gains["mix0"], win_p)
    cum, sgate = _fgate_fwd(flog, b_f)
    att, lse = _attn_fwd(qkv, cum)
    cv = _conv_fwd(bcx, conv_w)
    h1 = _outproj(att, cv, x, wout)
    h2, n1, a0, z0 = _mlp_fwd(h1, gains["ffn0"], wup0, wdown0, name="mlp_fwd0")
    h3, pooled = _pool_fwd(h2, gains["mix1"], poolw, pool_scale)
    h4, n3, a1, z1 = _mlp_fwd(h3, gains["ffn1"], wup1, wdown1, name="mlp_fwd1")
    loss, dh4, dg_final = _loss_bwd(h4, gains["final"], target)
    f = a1.shape[1]
    fb = f // N_DEV
    dh3, da1, dg_ffn1 = _mlp_bwd(dh4, h3, a1, gains["ffn1"], wup1, wdown1, name="mlp_bwd1")
    dwdown1 = _mm_tn(z1, dh4, name="dwdown1", ta=1024, tb=1024, tt=1024)
    dwup1 = _mm_tn(n3, da1, name="dwup1", ta=d, tb=fb, tt=2048, blocked_out=True)
    dh2, dpoolw, dscale, dg_mix1 = _pool_bwd(dh3, h2, pooled, gains["mix1"], poolw, pool_scale)
    dh1, da0, dg_ffn0 = _mlp_bwd(dh2, h1, a0, gains["ffn0"], wup0, wdown0, name="mlp_bwd0")
    dwdown0 = _mm_tn(z0, dh2, name="dwdown0", ta=1024, tb=1024, tt=1024)
    dwup0 = _mm_tn(n1, da0, name="dwup0", ta=d, tb=fb, tt=2048, blocked_out=True)
    datt, delta, dcv = _outproj_bwd(dh1, att, wout)
    dwout = jnp.concatenate([_mm_tn(att, dh1, name="dwout_att", ta=512, tb=1024, tt=2048),
                             _mm_tn(cv, dh1, name="dwout_conv", ta=512, tb=1024, tt=2048)], axis=0)
    dbcx, dconvw = _conv_bwd(bcx, dcv, conv_w)
    dq, dk, dv, dca, dcb, drow = _attn_bwd(qkv, datt, lse, delta, cum)
    df, dbf = _fgate_bwd(dca, dcb, drow, sgate)
    grad_x, dg_mix0 = _inproj_bwd(dq, dk, dv, df, dbcx, dh1, x, gains["mix0"], win_p)
    dwin_p = jnp.concatenate(
        [_mm_tn(n0, dq, name="dwin_q", ta=d, tb=512, tt=2048), _mm_tn(n0, dk, name="dwin_k", ta=d, tb=512, tt=2048),
         _mm_tn(n0, dv, name="dwin_v", ta=d, tb=512, tt=2048), _mm_tn(n0, df, name="dwin_f", ta=d, tb=128, tt=2048),
         _mm_tn(n0, dbcx, name="dwin_bcx", ta=d, tb=512, tt=2048)], axis=1)
    big = dict(win_p=dwin_p, wout=dwout, wup0=dwup0, wdown0=dwdown0, poolw=dpoolw, wup1=dwup1, wdown1=dwdown1)
    small = dict(mix0=dg_mix0, ffn0=dg_ffn0, mix1=dg_mix1, pool_scale=dscale, ffn1=dg_ffn1, final=dg_final,
                 b_f=dbf, conv_w=dconvw)
    return loss, grad_x, big, small


HBM_SPEC = pl.BlockSpec(memory_space=pltpu.HBM)
VMEM_SPEC = pl.BlockSpec(memory_space=pltpu.VMEM)


def _mesh_places():
    x, y, c = lax.axis_index("x"), lax.axis_index("y"), lax.axis_index("c")
    chips = [(1 - x, y), (x, 1 - y), (1 - x, 1 - y)]
    return (x, y, c), (x, y, 1 - c), chips


def _slot(px, py, pc):
    return 4 * px + 2 * py + pc


def _all_gather(shards):
    n = len(shards)

    def body(*refs):
        ins, outs = refs[:n], refs[n:2 * n]
        send_sems, recv_sems, local_sems = refs[2 * n:]
        me, sib, chips = _mesh_places()
        c = me[2]

        def copy(ai, k, block, to, src=None):
            dst = outs[ai].at[_slot(*block)]
            return pltpu.make_async_remote_copy(
                src_ref=dst if src is None else src, dst_ref=dst, send_sem=send_sems.at[7 * ai + k],
                recv_sem=recv_sems.at[7 * ai + k], device_id=to, device_id_type=MESH)

        mine = [pltpu.make_async_copy(ins[ai], outs[ai].at[_slot(*me)], local_sems.at[ai]) for ai in range(n)]
        for cp in mine:
            cp.start()
        first = []
        for ai in range(n):
            first.append(copy(ai, 0, me, sib, src=ins[ai]))
            first += [copy(ai, 1 + j, me, (*chip, c), src=ins[ai]) for j, chip in enumerate(chips)]
        for cp in first:
            cp.start()
        passed = []
        for ai in range(n):
            for j, chip in enumerate(chips):
                copy(ai, 1 + j, (*chip, c), me).wait_recv()
                cp = copy(ai, 4 + j, (*chip, c), sib)
                cp.start()
                passed.append(cp)
        for ai in range(n):
            copy(ai, 0, sib, me).wait_recv()
            for j, chip in enumerate(chips):
                copy(ai, 4 + j, (*chip, 1 - c), me).wait_recv()
        for cp in first + passed:
            cp.wait_send()
        for cp in mine:
            cp.wait()

    return pl.pallas_call(
        body, name="all_gather",
        in_specs=[HBM_SPEC] * n, out_specs=[HBM_SPEC] * n,
        out_shape=[jax.ShapeDtypeStruct((N_DEV,) + s.shape, s.dtype) for s in shards],
        scratch_shapes=[pltpu.SemaphoreType.DMA((7 * n,)), pltpu.SemaphoreType.DMA((7 * n,)),
                        pltpu.SemaphoreType.DMA((n,))],
    )(*shards)


def _reduce_scatter(grads):
    n = len(grads)
    shapes = [g.shape[1:] for g in grads]

    def body(*refs):
        gs = refs[:n]
        outs, land_a, land_b = refs[n:2 * n], refs[2 * n:3 * n], refs[3 * n:4 * n]
        sendbufs = refs[4 * n:5 * n]
        sa_send, sa_recv, sb_send, sb_recv, lsem = refs[5 * n:]
        me, sib, chips = _mesh_places()
        x, y, c = me

        def copy_a(ai, q):
            qx, qy = q // 2, q % 2
            return pltpu.make_async_remote_copy(
                src_ref=gs[ai].at[_slot(qx, qy, 1 - c)], dst_ref=land_a[ai].at[q], send_sem=sa_send.at[4 * ai + q],
                recv_sem=sa_recv.at[4 * ai + q], device_id=sib, device_id_type=MESH)

        def copy_b(ai, j):
            return pltpu.make_async_remote_copy(
                src_ref=sendbufs[ai].at[j], dst_ref=land_b[ai].at[j], send_sem=sb_send.at[3 * ai + j],
                recv_sem=sb_recv.at[3 * ai + j], device_id=(*chips[j], c), device_id_type=MESH)

        def load(src, dst):
            cp = pltpu.make_async_copy(src, dst, lsem)
            cp.start()
            cp.wait()

        for ai in range(n):
            for q in range(4):
                copy_a(ai, q).start()
        for ai in range(n):
            def stage(buf_a, buf_b, ai=ai):
                for j, (qx, qy) in enumerate(chips):
                    q = 2 * qx + qy
                    pltpu.make_async_remote_copy(
                        src_ref=land_a[ai].at[0], dst_ref=land_a[ai].at[0], send_sem=sa_send.at[0],
                        recv_sem=sa_recv.at[4 * ai + q], device_id=sib, device_id_type=MESH).wait_recv()
                    load(gs[ai].at[_slot(qx, qy, c)], buf_a)
                    load(land_a[ai].at[q], buf_b)
                    sendbufs[ai][j] = (buf_a[...] + buf_b[...]).astype(BF16)
                    copy_b(ai, j).start()
            pl.run_scoped(stage, pltpu.VMEM(shapes[ai], F32), pltpu.VMEM(shapes[ai], F32))
        for ai in range(n):
            def final(buf_a, buf_b, buf_c, ai=ai):
                q = 2 * x + y
                pltpu.make_async_remote_copy(
                    src_ref=land_a[ai].at[0], dst_ref=land_a[ai].at[0], send_sem=sa_send.at[0],
                    recv_sem=sa_recv.at[4 * ai + q], device_id=sib, device_id_type=MESH).wait_recv()
                load(gs[ai].at[_slot(x, y, c)], buf_a)
                load(land_a[ai].at[q], buf_b)
                acc = buf_a[...] + buf_b[...]
                for j in range(3):
                    copy_b(ai, j).wait_recv()
                    load(land_b[ai].at[j], buf_c)
                    acc = acc + buf_c[...].astype(F32)
                buf_a[...] = acc
                load(buf_a, outs[ai])
            pl.run_scoped(final, pltpu.VMEM(shapes[ai], F32), pltpu.VMEM(shapes[ai], F32), pltpu.VMEM(shapes[ai], BF16))
        for ai in range(n):
            for q in range(4):
                copy_a(ai, q).wait_send()
            for j in range(3):
                copy_b(ai, j).wait_send()

    res = pl.pallas_call(
        body, name="reduce_scatter",
        in_specs=[HBM_SPEC] * n, out_specs=[HBM_SPEC] * (3 * n),
        out_shape=([jax.ShapeDtypeStruct(s, F32) for s in shapes]
                   + [jax.ShapeDtypeStruct((4,) + s, F32) for s in shapes]
                   + [jax.ShapeDtypeStruct((3,) + s, BF16) for s in shapes]),
        scratch_shapes=([pltpu.VMEM((3,) + s, BF16) for s in shapes]
                        + [pltpu.SemaphoreType.DMA((4 * n,)), pltpu.SemaphoreType.DMA((4 * n,)),
                           pltpu.SemaphoreType.DMA((3 * n,)), pltpu.SemaphoreType.DMA((3 * n,)),
                           pltpu.SemaphoreType.DMA(())]),
        compiler_params=pltpu.CompilerParams(vmem_limit_bytes=VMEM_LIMIT),
    )(*grads)
    return res[:n]


SMALL_ROWS = 16


def _small_allreduce(parts):
    n, _, w = parts.shape
    assert n <= SMALL_ROWS

    def body(p_ref, o_ref, gath, send_sems, recv_sems):
        x, y, c = lax.axis_index("x"), lax.axis_index("y"), lax.axis_index("c")
        my = _slot(x, y, c)
        rows = [jnp.sum(p_ref[i], axis=0, keepdims=True) for i in range(n)]
        rows.append(jnp.zeros((SMALL_ROWS - n, w), F32))
        gath[my] = jnp.concatenate(rows, axis=0)
        copies = []
        for k in range(1, N_DEV):
            px, py, pc = x ^ (k >> 2), y ^ ((k >> 1) & 1), c ^ (k & 1)
            cp = pltpu.make_async_remote_copy(
                src_ref=gath.at[my], dst_ref=gath.at[my], send_sem=send_sems.at[k - 1], recv_sem=recv_sems.at[k - 1],
                device_id=(px, py, pc), device_id_type=MESH)
            cp.start()
            copies.append(cp)
        for cp in copies:
            cp.wait()
        acc = gath[0]
        for d in range(1, N_DEV):
            acc = acc + gath[d]
        o_ref[...] = acc

    return pl.pallas_call(
        body, name="small_allreduce",
        in_specs=[VMEM_SPEC], out_specs=VMEM_SPEC,
        out_shape=jax.ShapeDtypeStruct((SMALL_ROWS, w), F32),
        scratch_shapes=[pltpu.VMEM((N_DEV, SMALL_ROWS, w), F32), pltpu.SemaphoreType.DMA((N_DEV - 1,)),
                        pltpu.SemaphoreType.DMA((N_DEV - 1,))],
    )(parts)


def _adamw(g, w, m, v, *, name, tm=256):
    r, c = g.shape
    tm = tm if r % tm == 0 else r
    bc1 = 1.0 - ADAM_B1 ** ADAM_STEP
    bc2 = 1.0 - ADAM_B2 ** ADAM_STEP

    def body(g_ref, w_ref, m_ref, v_ref, d_ref, nm_ref, nv_ref):
        gv = g_ref[...]
        nm = ADAM_B1 * m_ref[...] + (1.0 - ADAM_B1) * gv
        nv = ADAM_B2 * v_ref[...] + (1.0 - ADAM_B2) * jnp.square(gv)
        nm_ref[...] = nm
        nv_ref[...] = nv
        d_ref[...] = -ADAM_LR * ((nm / bc1) / (jnp.sqrt(nv / bc2) + ADAM_EPS) + ADAM_WD * w_ref[...])

    blk = pl.BlockSpec((tm, c), lambda i: (i, 0))
    shp = jax.ShapeDtypeStruct((r, c), F32)
    return pl.pallas_call(
        body, name=name, grid=(r // tm,), in_specs=[blk] * 4, out_specs=[blk] * 3, out_shape=[shp] * 3,
        compiler_params=_cp(("parallel",)),
    )(g, w, m, v)


BIG = ("w_in_0", "w_out_0", "w_up_0", "w_down_0", "pool_w_1", "w_up_1", "w_down_1")
SMALL = ("norm_mix_0", "norm_ffn_0", "norm_mix_1", "pool_scale_1", "norm_ffn_1", "final_norm", "b_f_0", "conv_w_0")
WEIGHTS = ("norm_mix_0", "w_in_0", "b_f_0", "conv_w_0", "w_out_0", "norm_ffn_0", "w_up_0", "w_down_0", "norm_mix_1",
           "pool_w_1", "pool_scale_1", "norm_ffn_1", "w_up_1", "w_down_1", "final_norm")


def _pad_to(a, rows, cols):
    return jnp.pad(a, ((0, rows - a.shape[0]), (0, cols - a.shape[1])))


def _pack_small(p, width):
    rows = [p[n].reshape(1, -1) for n in SMALL[:6]]
    rows.append(_pad_to(p["b_f_0"].reshape(1, -1), 1, width))
    rows.append(_pad_to(p["conv_w_0"], 3, width))
    return _pad_to(jnp.concatenate(rows, axis=0), SMALL_ROWS, width)


def _unpack_small(a, like):
    out = {n: a[i] for i, n in enumerate(SMALL[:6])}
    out["b_f_0"] = a[6, :like["b_f_0"].shape[0]]
    out["conv_w_0"] = a[7:10, :like["conv_w_0"].shape[1]]
    return out


def kernel(x, norm_mix_0, w_in_0, b_f_0, conv_w_0, w_out_0, norm_ffn_0, w_up_0, w_down_0, norm_mix_1, pool_w_1, pool_scale_1, norm_ffn_1, w_up_1, w_down_1, final_norm, loss_target, m_norm_mix_0, m_w_in_0, m_b_f_0, m_conv_w_0, m_w_out_0, m_norm_ffn_0, m_w_up_0, m_w_down_0, m_norm_mix_1, m_pool_w_1, m_pool_scale_1, m_norm_ffn_1, m_w_up_1, m_w_down_1, m_final_norm, v_norm_mix_0, v_w_in_0, v_b_f_0, v_conv_w_0, v_w_out_0, v_norm_ffn_0, v_w_up_0, v_w_down_0, v_norm_mix_1, v_pool_w_1, v_pool_scale_1, v_norm_ffn_1, v_w_up_1, v_w_down_1, v_final_norm):
    w = dict(norm_mix_0=norm_mix_0, w_in_0=w_in_0, b_f_0=b_f_0, conv_w_0=conv_w_0, w_out_0=w_out_0,
             norm_ffn_0=norm_ffn_0, w_up_0=w_up_0, w_down_0=w_down_0, norm_mix_1=norm_mix_1, pool_w_1=pool_w_1,
             pool_scale_1=pool_scale_1, norm_ffn_1=norm_ffn_1, w_up_1=w_up_1, w_down_1=w_down_1, final_norm=final_norm)
    m = dict(norm_mix_0=m_norm_mix_0, w_in_0=m_w_in_0, b_f_0=m_b_f_0, conv_w_0=m_conv_w_0, w_out_0=m_w_out_0,
             norm_ffn_0=m_norm_ffn_0, w_up_0=m_w_up_0, w_down_0=m_w_down_0, norm_mix_1=m_norm_mix_1,
             pool_w_1=m_pool_w_1, pool_scale_1=m_pool_scale_1, norm_ffn_1=m_norm_ffn_1, w_up_1=m_w_up_1,
             w_down_1=m_w_down_1, final_norm=m_final_norm)
    v = dict(norm_mix_0=v_norm_mix_0, w_in_0=v_w_in_0, b_f_0=v_b_f_0, conv_w_0=v_conv_w_0, w_out_0=v_w_out_0,
             norm_ffn_0=v_norm_ffn_0, w_up_0=v_w_up_0, w_down_0=v_w_down_0, norm_mix_1=v_norm_mix_1,
             pool_w_1=v_pool_w_1, pool_scale_1=v_pool_scale_1, norm_ffn_1=v_norm_ffn_1, w_up_1=v_w_up_1,
             w_down_1=v_w_down_1, final_norm=v_final_norm)
    d = x.shape[-1]
    n_in = w_in_0.shape[1] * N_DEV
    n_qkv = 3 * ATTN_W
    pool_g, pool_rows, pool_c = pool_w_1.shape

    def shard2d(p):
        return {n: (p[n].reshape(pool_g * pool_rows, pool_c) if n == "pool_w_1" else p[n]) for n in BIG}
    w2, m2, v2 = shard2d(w), shard2d(m), shard2d(v)

    conv_cols = conv_w_0.shape[1]
    *gathered, conv_g8 = _all_gather([w2[n].astype(BF16) for n in BIG] + [_pad_to(conv_w_0, 8, 128)])
    gath = dict(zip(BIG, gathered))
    conv_full = conv_g8[:, :, :conv_cols].transpose(1, 0, 2).reshape(8, N_DEV * conv_cols)
    win = gath["w_in_0"].transpose(1, 0, 2).reshape(d, n_in)
    win_p = jnp.concatenate([win[:, :n_qkv], _pad_to(win[:, n_qkv:n_qkv + N_HEADS], d, F_PAD),
                             win[:, n_qkv + N_HEADS:]], axis=1)
    wout = gath["w_out_0"].reshape(d, d)
    wdown0 = gath["w_down_0"].reshape(-1, d)
    wdown1 = gath["w_down_1"].reshape(-1, d)
    poolw = gath["pool_w_1"].reshape(N_DEV, pool_g, pool_rows, pool_c).transpose(1, 0, 2, 3).reshape(pool_g, pool_c, pool_c)

    gains = dict(mix0=norm_mix_0.reshape(1, d), ffn0=norm_ffn_0.reshape(1, d), mix1=norm_mix_1.reshape(1, d),
                 ffn1=norm_ffn_1.reshape(1, d), final=final_norm.reshape(1, d))
    dev = _slot(lax.axis_index("x"), lax.axis_index("y"), lax.axis_index("c"))
    loss8, grad_x, big, small = _local_step(
        x[0], loss_target[0], gains, _pad_to(b_f_0.reshape(1, -1), 1, F_PAD), conv_full, pool_scale_1.reshape(1, d),
        win_p, wout, gath["w_up_0"], wdown0, poolw, gath["w_up_1"], wdown1)
    loss = lax.psum(loss8[0, 0], ("x", "y", "c"))

    dwin = jnp.concatenate([big["win_p"][:, :n_qkv + N_HEADS], big["win_p"][:, n_qkv + F_PAD:]], axis=1)
    gfull = {
        "w_in_0": dwin.reshape(d, N_DEV, n_in // N_DEV).transpose(1, 0, 2),
        "w_out_0": big["wout"].reshape(N_DEV, d // N_DEV, d),
        "w_up_0": big["wup0"], "w_up_1": big["wup1"],
        "w_down_0": big["wdown0"].reshape(N_DEV, -1, d), "w_down_1": big["wdown1"].reshape(N_DEV, -1, d),
        "pool_w_1": big["poolw"].reshape(pool_g, N_DEV, pool_rows, pool_c).transpose(1, 0, 2, 3).reshape(
            N_DEV, pool_g * pool_rows, pool_c),
    }
    g2 = dict(zip(BIG, _reduce_scatter([gfull[n] for n in BIG])))

    parts = jnp.concatenate(
        [small[k][None] for k in ("mix0", "ffn0", "mix1", "pool_scale", "ffn1", "final")]
        + [_pad_to(small["b_f"], 8, d)[None], jnp.pad(small["conv_w"], ((0, 0), (0, 0), (0, d - CONV_CH)))], axis=0)
    tot = _small_allreduce(parts)
    conv_g = lax.dynamic_slice(tot, (7, dev * conv_cols), (3, conv_cols))
    gs = tot.at[7:10].set(_pad_to(conv_g, 3, d))

    grads, deltas, new_m, new_v = {}, {}, {}, {}
    for n in BIG:
        dl, nm, nv = _adamw(g2[n], w2[n], m2[n], v2[n], name="adamw_" + n)
        for dst, val in ((grads, g2[n]), (deltas, dl), (new_m, nm), (new_v, nv)):
            dst[n] = val.reshape(w[n].shape)
    dl, nm, nv = _adamw(gs, _pack_small(w, d), _pack_small(m, d), _pack_small(v, d), name="adamw_small")
    for dst, val in ((grads, gs), (deltas, dl), (new_m, nm), (new_v, nv)):
        dst.update(_unpack_small(val, w))
    return (loss, grad_x[None], *[grads[n] for n in WEIGHTS], *[deltas[n] for n in WEIGHTS],
            *[new_m[n] for n in WEIGHTS], *[new_v[n] for n in WEIGHTS])
```

```python
import functools

import jax
import jax.numpy as jnp
from jax import lax
from jax.experimental import pallas as pl
from jax.experimental.pallas import tpu as pltpu

F32 = jnp.float32
BF16 = jnp.bfloat16

N_DEV = 8
N_HEADS = 8
HEAD_DIM = 64
PAIR = 2 * HEAD_DIM
ATTN_W = N_HEADS * HEAD_DIM
CONV_CH = 512
F_PAD = 128
POOL_WINDOWS = (2, 4, 8, 16)
POOL_HALO = 16
CONV_HALO = 8
RMS_EPS = 1e-6
Q_SCALE = HEAD_DIM ** -0.5
LOG2E = 1.4426950408889634
NEG = -1e30
AUX_BIAS = 0
AUX_LSE = 3
AUX_ROWSUM = 6
ADAM_LR, ADAM_B1, ADAM_B2, ADAM_EPS, ADAM_WD, ADAM_STEP = 0.001, 0.9, 0.999, 1e-08, 0.01, 10
MESH = pl.DeviceIdType.MESH
VMEM_LIMIT = 56 * 2**20


def _cp(sem=None, vmem=VMEM_LIMIT, **kw):
    return pltpu.CompilerParams(dimension_semantics=sem, vmem_limit_bytes=vmem, **kw)


def _dot(a, b):
    return jnp.dot(a, b, preferred_element_type=F32)


def _dot_nt(a, b):
    return lax.dot_general(a, b, (((1,), (1,)), ((), ())), preferred_element_type=F32)


def _dot_tn(a, b):
    return lax.dot_general(a, b, (((0,), (0,)), ((), ())), preferred_element_type=F32)


def _rstd(h):
    return lax.rsqrt(jnp.mean(h * h, axis=-1, keepdims=True) + RMS_EPS)


def _rows8(x):
    r, n = x.shape
    return jnp.sum(x.reshape(r // 8, 8, n), axis=0)


def _norm_bwd(dn, h, g):
    r = _rstd(h)
    xhat = h * r
    dy = dn * g
    dh = r * (dy - xhat * jnp.mean(dy * xhat, axis=-1, keepdims=True))
    return dh, _rows8(dn * xhat)


def _const_spec(shape):
    nd = len(shape)
    return pl.BlockSpec(shape, lambda *_: (0,) * nd, pipeline_mode=pl.Buffered(1))


def _norm_inproj(x, g, win_p, *, tm=512):
    t, d = x.shape
    n_all = win_p.shape[1]
    n_qkv = 3 * ATTN_W
    n_bcx = 3 * CONV_CH
    assert n_all == n_qkv + F_PAD + n_bcx
    tm = min(tm, t)

    def body(x_ref, g_ref, w_ref, n_ref, qkv_ref, f_ref, bcx_ref):
        h = x_ref[...]
        n = (h * _rstd(h) * g_ref[...]).astype(BF16)
        n_ref[...] = n
        for c0 in range(0, n_qkv, 512):
            acc = _dot(n, w_ref[:, c0:c0 + 512])
            if c0 < ATTN_W:
                acc = acc * (Q_SCALE * LOG2E)
            qkv_ref[:, c0:c0 + 512] = acc.astype(BF16)
        f_ref[...] = _dot(n, w_ref[:, n_qkv:n_qkv + F_PAD])
        for c0 in range(0, n_bcx, 512):
            bcx_ref[:, c0:c0 + 512] = _dot(n, w_ref[:, n_qkv + F_PAD + c0:n_qkv + F_PAD + c0 + 512])

    return pl.pallas_call(
        body, name="norm_inproj", grid=(t // tm,),
        in_specs=[pl.BlockSpec((tm, d), lambda i: (i, 0)), _const_spec((1, d)), _const_spec((d, n_all))],
        out_specs=[pl.BlockSpec((tm, d), lambda i: (i, 0)), pl.BlockSpec((tm, n_qkv), lambda i: (i, 0)),
                   pl.BlockSpec((tm, F_PAD), lambda i: (i, 0)), pl.BlockSpec((tm, n_bcx), lambda i: (i, 0))],
        out_shape=[jax.ShapeDtypeStruct((t, d), BF16), jax.ShapeDtypeStruct((t, n_qkv), BF16),
                   jax.ShapeDtypeStruct((t, F_PAD), F32), jax.ShapeDtypeStruct((t, n_bcx), F32)],
        compiler_params=_cp(("parallel",)),
    )(x, g, win_p)


def _head_lanes(h):
    lane = lax.broadcasted_iota(jnp.int32, (1, PAIR), 1)
    hh = h % 2
    return lane, lane // HEAD_DIM == hh, HEAD_DIM * (1 - hh)


def _pieces(col):
    hi = col.astype(BF16).astype(F32)
    r1 = col - hi
    mid = r1.astype(BF16).astype(F32)
    lo = (r1 - mid).astype(BF16).astype(F32)
    return hi, mid, lo


def _put_pieces(lane, first, col, other):
    hi, mid, lo = _pieces(col)
    return jnp.where(lane == first, hi, jnp.where(lane == first + 1, mid, jnp.where(lane == first + 2, lo, other)))


def _fgate_prep(flog, b_f, qkv, *, tm=256):
    t = flog.shape[0]
    tm = min(tm, t)

    def body(f_ref, b_ref, qkv_ref, qa_ref, ka_ref, va_ref, sg_ref, carry):
        @pl.when(pl.program_id(0) == 0)
        def _():
            carry[...] = jnp.zeros_like(carry)
        z = f_ref[...] + b_ref[...]
        e = jnp.exp(-jnp.abs(z))
        logf = jnp.minimum(z, 0.0) - jnp.log(1.0 + e)
        sg_ref[...] = jnp.where(z >= 0, e, 1.0) / (1.0 + e)
        r = lax.broadcasted_iota(jnp.int32, (tm, tm), 0)
        c = lax.broadcasted_iota(jnp.int32, (tm, tm), 1)
        tri = (c <= r).astype(F32)
        cs = jnp.dot(tri, logf, preferred_element_type=F32, precision=lax.Precision.HIGHEST) + carry[...]
        carry[...] = cs[tm - 1:tm, :]
        cs2 = cs * LOG2E
        for h in range(N_HEADS):
            lane, head, aux = _head_lanes(h)
            p0 = (h // 2) * PAIR
            ones = ((lane >= aux + AUX_LSE) & (lane <= aux + AUX_ROWSUM)).astype(F32)
            bias = (lane >= aux + AUX_BIAS) & (lane < aux + AUX_BIAS + 3)
            k_aux = _put_pieces(lane, aux + AUX_BIAS, cs2[:, h:h + 1], ones)
            qa_ref[h] = jnp.where(head, qkv_ref[:, p0:p0 + PAIR], jnp.where(bias, -1.0, 0.0).astype(BF16))
            ka_ref[h] = jnp.where(head, qkv_ref[:, ATTN_W + p0:ATTN_W + p0 + PAIR], k_aux.astype(BF16))
            va_ref[h] = jnp.where(head, qkv_ref[:, 2 * ATTN_W + p0:2 * ATTN_W + p0 + PAIR],
                                  jnp.where(bias, 1.0, 0.0).astype(BF16))

    aug = lambda: pl.BlockSpec((N_HEADS, tm, PAIR), lambda i: (0, i, 0))
    aug_shape = jax.ShapeDtypeStruct((N_HEADS, t, PAIR), BF16)
    return pl.pallas_call(
        body, name="fgate_prep", grid=(t // tm,),
        in_specs=[pl.BlockSpec((tm, F_PAD), lambda i: (i, 0)), _const_spec((1, F_PAD)),
                  pl.BlockSpec((tm, 3 * ATTN_W), lambda i: (i, 0))],
        out_specs=[aug(), aug(), aug(), pl.BlockSpec((tm, F_PAD), lambda i: (i, 0))],
        out_shape=[aug_shape, aug_shape, aug_shape, jax.ShapeDtypeStruct((t, F_PAD), F32)],
        scratch_shapes=[pltpu.VMEM((1, F_PAD), F32)],
        compiler_params=_cp(("arbitrary",)),
    )(flog, b_f, qkv)


def _attn_fwd(q_aug, k_aug, v_aug, *, tq=512):
    t = q_aug.shape[1]
    tq = min(tq, t)
    tk = tq
    n_pairs = ATTN_W // PAIR

    def body(q_ref, k_ref, v_ref, o_ref, qb_ref, s_scr):
        i = pl.program_id(1)
        row = lax.broadcasted_iota(jnp.int32, (tq, tk), 0)
        col = lax.broadcasted_iota(jnp.int32, (tq, tk), 1)
        q = [q_ref[0], q_ref[1]]

        def logits(hh, tile, slot, diag):
            s = _dot_nt(q[hh], k_ref[hh, pl.ds(pl.multiple_of(tile * tk, tk), tk), :])
            if diag:
                s = jnp.where(col <= row, s, NEG)
            s_scr[hh, slot] = s
            return jnp.max(s, axis=1, keepdims=True)

        def probs(hh, tile, slot, m, acc, tmax):
            mn = jnp.maximum(m, tmax)
            p = jnp.exp2(s_scr[hh, slot] - mn).astype(BF16)
            acc = jnp.exp2(m - mn) * acc + _dot(p, v_ref[hh, pl.ds(pl.multiple_of(tile * tk, tk), tk), :])
            return mn, acc

        def advance(carry, prev, slot, nxt):
            out = []
            for hh in range(2):
                m, acc, tmax = carry[hh]
                m, acc = probs(hh, prev, slot, m, acc, tmax)
                out.append((m, acc, logits(hh, nxt, 1 - slot, False)))
            return tuple(out)

        def two_tiles(jj, carry):
            carry = advance(carry, jnp.where(jj == 0, i, 2 * jj - 1), 0, 2 * jj)
            return advance(carry, 2 * jj, 1, 2 * jj + 1)

        init = tuple((jnp.full((tq, 1), NEG, F32), jnp.zeros((tq, PAIR), F32), logits(hh, i, 0, True))
                     for hh in range(2))
        carry = lax.fori_loop(0, i // 2, two_tiles, init)
        odd = i % 2 == 1
        carry = lax.cond(odd, lambda c: advance(c, jnp.where(i == 1, i, i - 2), 0, i - 1), lambda c: c, carry)
        last = jnp.where(i == 0, i, i - 1)
        res = []
        for hh in range(2):
            lane, head, aux = _head_lanes(hh)
            m, acc, tmax = carry[hh]
            m, acc = lax.cond(odd, lambda a: probs(hh, last, 1, *a), lambda a: probs(hh, last, 0, *a), (m, acc, tmax))
            l = jnp.sum(jnp.where(lane == aux + AUX_BIAS, acc, 0.0), axis=1, keepdims=True)
            qb_ref[hh] = _put_pieces(lane, aux + AUX_LSE, -(m + jnp.log2(l)), q[hh].astype(F32)).astype(BF16)
            res.append(acc / l)
        lane = lax.broadcasted_iota(jnp.int32, (1, PAIR), 1)
        o_ref[...] = jnp.where(lane < HEAD_DIM, res[0], res[1]).astype(BF16)

    return pl.pallas_call(
        body, name="attn_fwd", grid=(n_pairs, t // tq),
        in_specs=[pl.BlockSpec((2, tq, PAIR), lambda p, i: (p, i, 0)),
                  pl.BlockSpec((2, t, PAIR), lambda p, i: (p, 0, 0), pipeline_mode=pl.Buffered(1)),
                  pl.BlockSpec((2, t, PAIR), lambda p, i: (p, 0, 0), pipeline_mode=pl.Buffered(1))],
        out_specs=[pl.BlockSpec((tq, PAIR), lambda p, i: (i, p)), pl.BlockSpec((2, tq, PAIR), lambda p, i: (p, i, 0))],
        out_shape=[jax.ShapeDtypeStruct((t, ATTN_W), BF16), jax.ShapeDtypeStruct((N_HEADS, t, PAIR), BF16)],
        scratch_shapes=[pltpu.VMEM((2, 2, tq, tk), F32)],
        compiler_params=_cp(("parallel", "parallel")),
    )(q_aug, k_aug, v_aug)


def _prev_halo(tm, halo):
    return lambda i: (jnp.maximum(i * (tm // halo) - 1, 0), 0)


def _next_halo(tm, halo, t):
    return lambda i: (jnp.minimum((i + 1) * (tm // halo), t // halo - 1), 0)


def _conv_fwd(bcx, conv_w, *, tm=512):
    t = bcx.shape[0]
    tm = min(tm, t)
    ch = CONV_CH

    def body(b_ref, c_ref, x_ref, hc_ref, hx_ref, w_ref, cv_ref, ext):
        first = pl.program_id(0) == 0
        ext[0:CONV_HALO, :] = jnp.where(first, 0.0, hc_ref[...] * hx_ref[...])
        ext[CONV_HALO:CONV_HALO + tm, :] = c_ref[...] * x_ref[...]
        conv = (w_ref[0:1, :] * ext[CONV_HALO - 2:CONV_HALO - 2 + tm, :]
                + w_ref[1:2, :] * ext[CONV_HALO - 1:CONV_HALO - 1 + tm, :]
                + w_ref[2:3, :] * ext[CONV_HALO:CONV_HALO + tm, :])
        cv_ref[...] = (b_ref[...] * conv).astype(BF16)

    col = lambda k: pl.BlockSpec((tm, ch), lambda i: (i, k))
    halo = lambda k: pl.BlockSpec((CONV_HALO, ch), lambda i: (_prev_halo(tm, CONV_HALO)(i)[0], k))
    return pl.pallas_call(
        body, name="conv_fwd", grid=(t // tm,),
        in_specs=[col(0), col(1), col(2), halo(1), halo(2), _const_spec((8, ch))],
        out_specs=pl.BlockSpec((tm, ch), lambda i: (i, 0)),
        out_shape=jax.ShapeDtypeStruct((t, ch), BF16),
        scratch_shapes=[pltpu.VMEM((CONV_HALO + tm, ch), F32)],
        compiler_params=_cp(("parallel",)),
    )(bcx, bcx, bcx, bcx, bcx, conv_w)


def _outproj(att, cv, x, wout, *, tm=512):
    t, d = x.shape
    tm = min(tm, t)

    def body(a_ref, c_ref, x_ref, w_ref, h_ref):
        h_ref[...] = x_ref[...] + _dot(a_ref[...], w_ref[0:ATTN_W, :]) + _dot(c_ref[...], w_ref[ATTN_W:, :])

    return pl.pallas_call(
        body, name="outproj", grid=(t // tm,),
        in_specs=[pl.BlockSpec((tm, ATTN_W), lambda i: (i, 0)), pl.BlockSpec((tm, CONV_CH), lambda i: (i, 0)),
                  pl.BlockSpec((tm, d), lambda i: (i, 0)), _const_spec(wout.shape)],
        out_specs=pl.BlockSpec((tm, d), lambda i: (i, 0)),
        out_shape=jax.ShapeDtypeStruct((t, d), F32),
        compiler_params=_cp(("parallel",)),
    )(att, cv, x, wout)


def _mlp_fwd(h, g, wup, wdown, *, name, tm=256):
    t, d = h.shape
    n_blk, _, fb = wup.shape
    f = n_blk * fb
    tm = min(tm, t)

    def body(h_ref, g_ref, wu_ref, wd_ref, ho_ref, n_ref, a_ref, z_ref):
        hh = h_ref[...]
        n = (hh * _rstd(hh) * g_ref[...]).astype(BF16)
        n_ref[...] = n
        acc = hh
        for k in range(n_blk):
            a = _dot(n, wu_ref[k])
            zz = jnp.square(jnp.maximum(a, 0.0)).astype(BF16)
            a_ref[:, k * fb:(k + 1) * fb] = a.astype(BF16)
            z_ref[:, k * fb:(k + 1) * fb] = zz
            acc = acc + _dot(zz, wd_ref[k * fb:(k + 1) * fb, :])
        ho_ref[...] = acc

    row = lambda n_: pl.BlockSpec((tm, n_), lambda i: (i, 0))
    return pl.pallas_call(
        body, name=name, grid=(t // tm,),
        in_specs=[row(d), _const_spec((1, d)), _const_spec(wup.shape), _const_spec(wdown.shape)],
        out_specs=[row(d), row(d), row(f), row(f)],
        out_shape=[jax.ShapeDtypeStruct((t, d), F32), jax.ShapeDtypeStruct((t, d), BF16),
                   jax.ShapeDtypeStruct((t, f), BF16), jax.ShapeDtypeStruct((t, f), BF16)],
        compiler_params=_cp(("parallel",)),
    )(h, g, wup, wdown)


def _pool_inv_count(i, tm):
    tglob = (i * tm + lax.broadcasted_iota(jnp.int32, (tm, 1), 0) + 1).astype(F32)
    return [1.0 / jnp.minimum(tglob, float(w)) for w in POOL_WINDOWS]


def _pool_fwd(h, g, poolw, scale, *, tm=256):
    t, d = h.shape
    tm = min(tm, t)
    cg = d // len(POOL_WINDOWS)

    def body(h_ref, hh_ref, g_ref, w_ref, s_ref, ho_ref, p_ref, ext):
        i = pl.program_id(0)
        hv = h_ref[...]
        halo = hh_ref[...]
        n = hv * _rstd(hv) * g_ref[...]
        ext[0:POOL_HALO, :] = jnp.where(i == 0, 0.0, halo * _rstd(halo) * g_ref[...])
        ext[POOL_HALO:POOL_HALO + tm, :] = n
        inv = _pool_inv_count(i, tm)
        for gi, w in enumerate(POOL_WINDOWS):
            cs = slice(gi * cg, (gi + 1) * cg)
            s = ext[POOL_HALO:POOL_HALO + tm, cs]
            for j in range(1, w):
                s = s + ext[POOL_HALO - j:POOL_HALO - j + tm, cs]
            pooled = (s * inv[gi] - n[:, cs]).astype(BF16)
            p_ref[:, cs] = pooled
            ho_ref[:, cs] = hv[:, cs] + _dot(pooled, w_ref[gi]) * s_ref[:, cs]

    row = lambda: pl.BlockSpec((tm, d), lambda i: (i, 0))
    return pl.pallas_call(
        body, name="pool_fwd", grid=(t // tm,),
        in_specs=[row(), pl.BlockSpec((POOL_HALO, d), _prev_halo(tm, POOL_HALO)), _const_spec((1, d)),
                  _const_spec(poolw.shape), _const_spec((1, d))],
        out_specs=[row(), row()],
        out_shape=[jax.ShapeDtypeStruct((t, d), F32), jax.ShapeDtypeStruct((t, d), BF16)],
        scratch_shapes=[pltpu.VMEM((POOL_HALO + tm, d), F32)],
        compiler_params=_cp(("parallel",)),
    )(h, h, g, poolw, scale)


def _loss_bwd(h, g, target, *, tm=512):
    t, d = h.shape
    tm = min(tm, t)
    nsteps = t // tm

    def body(h_ref, g_ref, y_ref, loss_ref, dh_ref, dg_ref, lacc):
        i = pl.program_id(0)

        @pl.when(i == 0)
        def _():
            lacc[...] = jnp.zeros_like(lacc)
            dg_ref[...] = jnp.zeros_like(dg_ref)
        hv = h_ref[...]
        gv = g_ref[...]
        r = _rstd(hv)
        xhat = hv * r
        err = xhat * gv - y_ref[...]
        lacc[...] += _rows8(err * err)
        dout = err * (1.0 / d)
        dy = dout * gv
        dg_ref[...] += _rows8(dout * xhat)
        dh_ref[...] = r * (dy - xhat * jnp.mean(dy * xhat, axis=-1, keepdims=True))

        @pl.when(i == nsteps - 1)
        def _():
            loss_ref[...] = jnp.full(loss_ref.shape, (0.5 / d) * jnp.sum(lacc[...]), F32)

    row = lambda: pl.BlockSpec((tm, d), lambda i: (i, 0))
    return pl.pallas_call(
        body, name="loss_bwd", grid=(nsteps,),
        in_specs=[row(), _const_spec((1, d)), row()],
        out_specs=[pl.BlockSpec((8, 128), lambda i: (0, 0)), row(), pl.BlockSpec((8, d), lambda i: (0, 0))],
        out_shape=[jax.ShapeDtypeStruct((8, 128), F32), jax.ShapeDtypeStruct((t, d), F32),
                   jax.ShapeDtypeStruct((8, d), F32)],
        scratch_shapes=[pltpu.VMEM((8, d), F32)],
        compiler_params=_cp(("arbitrary",)),
    )(h, g, target)


def _mm_tn(a, b, *, name, ta, tb, tt, blocked_out=False):
    t, ka = a.shape
    n = b.shape[1]
    ta, tb, tt = min(ta, ka), min(tb, n), min(tt, t)

    def body(a_ref, b_ref, o_ref):
        @pl.when(pl.program_id(2) == 0)
        def _():
            o_ref[...] = jnp.zeros_like(o_ref)
        o_ref[...] += _dot_tn(a_ref[...].astype(BF16), b_ref[...].astype(BF16))

    if blocked_out:
        assert ta == ka
        out_shape = jax.ShapeDtypeStruct((n // tb, ka, tb), F32)
        out_spec = pl.BlockSpec((None, ta, tb), lambda i, j, k: (j, i, 0))
    else:
        out_shape = jax.ShapeDtypeStruct((ka, n), F32)
        out_spec = pl.BlockSpec((ta, tb), lambda i, j, k: (i, j))
    return pl.pallas_call(
        body, name=name, grid=(ka // ta, n // tb, t // tt),
        in_specs=[pl.BlockSpec((tt, ta), lambda i, j, k: (k, i)), pl.BlockSpec((tt, tb), lambda i, j, k: (k, j))],
        out_specs=out_spec, out_shape=out_shape,
        compiler_params=_cp(("parallel", "parallel", "arbitrary")),
    )(a, b)


def _mlp_bwd(dho, h, a, g, wup, wdown, *, name, tm=256):
    t, d = h.shape
    n_blk, _, fb = wup.shape
    f = n_blk * fb
    tm = min(tm, t)

    def body(do_ref, h_ref, a_ref, g_ref, wu_ref, wd_ref, dh_ref, da_ref, dg_ref):
        @pl.when(pl.program_id(0) == 0)
        def _():
            dg_ref[...] = jnp.zeros_like(dg_ref)
        dho_v = do_ref[...]
        dob = dho_v.astype(BF16)
        dn = jnp.zeros((tm, d), F32)
        for k in range(n_blk):
            dz = _dot_nt(dob, wd_ref[k * fb:(k + 1) * fb, :])
            da = (dz * (2.0 * jnp.maximum(a_ref[:, k * fb:(k + 1) * fb].astype(F32), 0.0))).astype(BF16)
            da_ref[:, k * fb:(k + 1) * fb] = da
            dn = dn + _dot_nt(da, wu_ref[k])
        dh, dg = _norm_bwd(dn, h_ref[...], g_ref[...])
        dh_ref[...] = dho_v + dh
        dg_ref[...] += dg

    row = lambda n_: pl.BlockSpec((tm, n_), lambda i: (i, 0))
    return pl.pallas_call(
        body, name=name, grid=(t // tm,),
        in_specs=[row(d), row(d), row(f), _const_spec((1, d)), _const_spec(wup.shape), _const_spec(wdown.shape)],
        out_specs=[row(d), row(f), pl.BlockSpec((8, d), lambda i: (0, 0))],
        out_shape=[jax.ShapeDtypeStruct((t, d), F32), jax.ShapeDtypeStruct((t, f), BF16),
                   jax.ShapeDtypeStruct((8, d), F32)],
        compiler_params=_cp(("arbitrary",)),
    )(dho, h, a, g, wup, wdown)


def _pool_bwd(dho, h, pooled, g, poolw, scale, *, tm=256):
    t, d = h.shape
    tm = min(tm, t)
    ng = len(POOL_WINDOWS)
    cg = d // ng
    nsteps = t // tm

    def body(do_ref, dn_ref, h_ref, p_ref, g_ref, w_ref, s_ref, dh_ref, dw_ref, ds_ref, dg_ref, ext):
        i = pl.program_id(0)

        @pl.when(i == 0)
        def _():
            dw_ref[...] = jnp.zeros_like(dw_ref)
            ds_ref[...] = jnp.zeros_like(ds_ref)
            dg_ref[...] = jnp.zeros_like(dg_ref)
        dho_v = do_ref[...]
        sv = s_ref[...]
        dyp = (dho_v * sv).astype(BF16)
        dyp_halo = (dn_ref[...] * sv).astype(BF16)
        inv = _pool_inv_count(i, tm)
        tnext = ((i + 1) * tm + lax.broadcasted_iota(jnp.int32, (POOL_HALO, 1), 0) + 1).astype(F32)
        last = i == nsteps - 1
        ypre_parts, dpooled_parts = [], []
        for gi, w in enumerate(POOL_WINDOWS):
            cs = slice(gi * cg, (gi + 1) * cg)
            pg = p_ref[:, cs]
            ypre_parts.append(_dot(pg, w_ref[gi]))
            dw_ref[gi] += _dot_tn(pg, dyp[:, cs])
            dpool = _dot_nt(dyp[:, cs], w_ref[gi])
            dpooled_parts.append(dpool)
            ext[0:tm, cs] = dpool * inv[gi]
            dpool_halo = _dot_nt(dyp_halo[:, cs], w_ref[gi]) * (1.0 / jnp.minimum(tnext, float(w)))
            ext[tm:tm + POOL_HALO, cs] = jnp.where(last, 0.0, dpool_halo)
        ds_ref[...] += _rows8(dho_v * jnp.concatenate(ypre_parts, axis=1))
        dn_parts = []
        for gi, w in enumerate(POOL_WINDOWS):
            cs = slice(gi * cg, (gi + 1) * cg)
            s = ext[0:tm, cs]
            for j in range(1, w):
                s = s + ext[j:j + tm, cs]
            dn_parts.append(s - dpooled_parts[gi])
        dh, dg = _norm_bwd(jnp.concatenate(dn_parts, axis=1), h_ref[...], g_ref[...])
        dh_ref[...] = dho_v + dh
        dg_ref[...] += dg

    row = lambda: pl.BlockSpec((tm, d), lambda i: (i, 0))
    acc8 = lambda: pl.BlockSpec((8, d), lambda i: (0, 0))
    return pl.pallas_call(
        body, name="pool_bwd", grid=(nsteps,),
        in_specs=[row(), pl.BlockSpec((POOL_HALO, d), _next_halo(tm, POOL_HALO, t)), row(), row(),
                  _const_spec((1, d)), _const_spec(poolw.shape), _const_spec((1, d))],
        out_specs=[row(), pl.BlockSpec((ng, cg, cg), lambda i: (0, 0, 0)), acc8(), acc8()],
        out_shape=[jax.ShapeDtypeStruct((t, d), F32), jax.ShapeDtypeStruct((ng, cg, cg), F32),
                   jax.ShapeDtypeStruct((8, d), F32), jax.ShapeDtypeStruct((8, d), F32)],
        scratch_shapes=[pltpu.VMEM((tm + POOL_HALO, d), F32)],
        compiler_params=_cp(("arbitrary",)),
    )(dho, dho, h, pooled, g, poolw, scale)


def _outproj_bwd(dh, o, wout, *, tm=512):
    t, d = dh.shape
    tm = min(tm, t)

    def body(dh_ref, o_ref, w_ref, da_ref, dc_ref):
        dhb = dh_ref[...].astype(BF16)
        dc_ref[...] = _dot_nt(dhb, w_ref[ATTN_W:, :])
        for p in range(ATTN_W // PAIR):
            datt = _dot_nt(dhb, w_ref[p * PAIR:(p + 1) * PAIR, :])
            prod = datt * o_ref[:, p * PAIR:(p + 1) * PAIR].astype(F32)
            for hh in range(2):
                lane, head, aux = _head_lanes(hh)
                delta = jnp.sum(jnp.where(head, prod, 0.0), axis=1, keepdims=True)
                da_ref[2 * p + hh] = _put_pieces(lane, aux + AUX_BIAS, -delta, jnp.where(head, datt, 0.0)).astype(BF16)

    row = lambda n_: pl.BlockSpec((tm, n_), lambda i: (i, 0))
    return pl.pallas_call(
        body, name="outproj_bwd", grid=(t // tm,),
        in_specs=[row(d), row(ATTN_W), _const_spec(wout.shape)],
        out_specs=[pl.BlockSpec((N_HEADS, tm, PAIR), lambda i: (0, i, 0)), row(CONV_CH)],
        out_shape=[jax.ShapeDtypeStruct((N_HEADS, t, PAIR), BF16), jax.ShapeDtypeStruct((t, CONV_CH), F32)],
        compiler_params=_cp(("parallel",)),
    )(dh, o, wout)


def _conv_bwd(bcx, dcv, conv_w, *, tm=512):
    t = bcx.shape[0]
    tm = min(tm, t)
    ch = CONV_CH
    nsteps = t // tm

    def body(b_ref, c_ref, x_ref, hc_ref, hx_ref, d_ref, nb_ref, nd_ref, w_ref, o_ref, dw_ref, ext_u, ext_d):
        i = pl.program_id(0)

        @pl.when(i == 0)
        def _():
            dw_ref[...] = jnp.zeros_like(dw_ref)
        b, c, x, dcv_v = b_ref[...], c_ref[...], x_ref[...], d_ref[...]
        ext_u[0:CONV_HALO, :] = jnp.where(i == 0, 0.0, hc_ref[...] * hx_ref[...])
        ext_u[CONV_HALO:CONV_HALO + tm, :] = c * x
        dconv = dcv_v * b
        ext_d[0:tm, :] = dconv
        ext_d[tm:tm + CONV_HALO, :] = jnp.where(i == nsteps - 1, 0.0, nd_ref[...] * nb_ref[...])
        u = [ext_u[CONV_HALO - 2 + k:CONV_HALO - 2 + k + tm, :] for k in range(3)]
        conv = w_ref[0:1, :] * u[0] + w_ref[1:2, :] * u[1] + w_ref[2:3, :] * u[2]
        du = (w_ref[2:3, :] * dconv + w_ref[1:2, :] * ext_d[1:1 + tm, :] + w_ref[0:1, :] * ext_d[2:2 + tm, :])
        o_ref[:, 0:ch] = (dcv_v * conv).astype(BF16)
        o_ref[:, ch:2 * ch] = (du * x).astype(BF16)
        o_ref[:, 2 * ch:3 * ch] = (du * c).astype(BF16)
        for k in range(3):
            dw_ref[k] += _rows8(dconv * u[k])

    col = lambda k: pl.BlockSpec((tm, ch), lambda i: (i, k))
    prev = lambda k: pl.BlockSpec((CONV_HALO, ch), lambda i: (_prev_halo(tm, CONV_HALO)(i)[0], k))
    nxt = lambda k: pl.BlockSpec((CONV_HALO, ch), lambda i: (_next_halo(tm, CONV_HALO, t)(i)[0], k))
    return pl.pallas_call(
        body, name="conv_bwd", grid=(nsteps,),
        in_specs=[col(0), col(1), col(2), prev(1), prev(2), col(0), nxt(0), nxt(0), _const_spec((8, ch))],
        out_specs=[pl.BlockSpec((tm, 3 * ch), lambda i: (i, 0)), pl.BlockSpec((3, 8, ch), lambda i: (0, 0, 0))],
        out_shape=[jax.ShapeDtypeStruct((t, 3 * ch), BF16), jax.ShapeDtypeStruct((3, 8, ch), F32)],
        scratch_shapes=[pltpu.VMEM((CONV_HALO + tm, ch), F32), pltpu.VMEM((tm + CONV_HALO, ch), F32)],
        compiler_params=_cp(("arbitrary",)),
    )(bcx, bcx, bcx, bcx, bcx, dcv, bcx, dcv, conv_w)


def _attn_bwd(q_bwd, do_aug, k_aug, v_aug, *, tq=512):
    t = q_bwd.shape[1]
    tq = min(tq, t)
    tk = tq
    nq = t // tq
    n_pairs = ATTN_W // PAIR

    def body(q_ref, do_ref, k_ref, v_ref, dq_ref, dqx_ref, dk_ref, dkx_ref, dv_ref, dq_scr):
        j = pl.program_id(1)

        @pl.when(j == 0)
        def _():
            dq_scr[...] = jnp.zeros_like(dq_scr)
        row = lax.broadcasted_iota(jnp.int32, (tq, tk), 0)
        col = lax.broadcasted_iota(jnp.int32, (tq, tk), 1)
        k = [k_ref[0], k_ref[1]]
        v = [v_ref[0], v_ref[1]]

        def step(i, carry, diag):
            qs = pl.multiple_of(i * tq, tq)
            out = []
            for hh in range(2):
                dk_a, dv_a = carry[hh]
                q = q_ref[hh, pl.ds(qs, tq), :]
                dov = do_ref[hh, pl.ds(qs, tq), :]
                p = jnp.exp2(_dot_nt(q, k[hh]))
                if diag:
                    p = jnp.where(col <= row, p, 0.0)
                ds = (p * _dot_nt(dov, v[hh])).astype(BF16)
                dv_a = dv_a + _dot_tn(p.astype(BF16), dov)
                dk_a = dk_a + _dot_tn(ds, q)
                dq_scr[hh, pl.ds(qs, tq), :] += _dot(ds, k[hh])
                out.append((dk_a, dv_a))
            return tuple(out)

        zero = (jnp.zeros((tk, PAIR), F32), jnp.zeros((tk, PAIR), F32))
        carry = step(j, (zero, zero), True)
        (dk0, dv0), (dk1, dv1) = lax.fori_loop(j + 1, nq, functools.partial(step, diag=False), carry)
        first = lax.broadcasted_iota(jnp.int32, (1, PAIR), 1) < HEAD_DIM
        dk_ref[...] = (jnp.where(first, dk0, dk1) * (1.0 / LOG2E)).astype(BF16)
        dkx_ref[...] = jnp.where(first, dk1, dk0)
        dv_ref[...] = jnp.where(first, dv0, dv1).astype(BF16)

        @pl.when(j == nq - 1)
        def _():
            dq_ref[...] = (jnp.where(first, dq_scr[0], dq_scr[1]) * Q_SCALE).astype(BF16)
            dqx_ref[...] = jnp.where(first, dq_scr[1], dq_scr[0])

    res = lambda: pl.BlockSpec((2, t, PAIR), lambda p, j: (p, 0, 0), pipeline_mode=pl.Buffered(1))
    kv_in = lambda: pl.BlockSpec((2, tk, PAIR), lambda p, j: (p, j, 0))
    whole = lambda: pl.BlockSpec((t, PAIR), lambda p, j: (0, p))
    tile = lambda: pl.BlockSpec((tk, PAIR), lambda p, j: (j, p))
    b16 = jax.ShapeDtypeStruct((t, ATTN_W), BF16)
    f32 = jax.ShapeDtypeStruct((t, ATTN_W), F32)
    return pl.pallas_call(
        body, name="attn_bwd", grid=(n_pairs, nq),
        in_specs=[res(), res(), kv_in(), kv_in()],
        out_specs=[whole(), whole(), tile(), tile(), tile()],
        out_shape=[b16, f32, b16, f32, b16],
        scratch_shapes=[pltpu.VMEM((2, t, PAIR), F32)],
        compiler_params=_cp(("parallel", "arbitrary")),
    )(q_bwd, do_aug, k_aug, v_aug)


def _fgate_bwd(dqx, dkx, sgate, *, tm=256):
    t = sgate.shape[0]
    tm = min(tm, t)
    nsteps = t // tm

    def body(dq_ref, dk_ref, sg_ref, df_ref, dbf_ref, carry):
        @pl.when(pl.program_id(0) == 0)
        def _():
            carry[...] = jnp.zeros_like(carry)
            dbf_ref[...] = jnp.zeros_like(dbf_ref)
        lane = lax.broadcasted_iota(jnp.int32, (ATTN_W, F_PAD), 0)
        head = lax.broadcasted_iota(jnp.int32, (ATTN_W, F_PAD), 1)
        aux = (head // 2) * PAIR + HEAD_DIM * (1 - head % 2)
        valid = head < N_HEADS
        pick_r = (valid & (lane == aux + AUX_ROWSUM)).astype(F32)
        pick_c = (valid & (lane == aux + AUX_BIAS)).astype(F32)
        hp = lax.Precision.HIGHEST
        dcum = (jnp.dot(dq_ref[...], pick_r, preferred_element_type=F32, precision=hp)
                + jnp.dot(dk_ref[...], pick_c, preferred_element_type=F32, precision=hp))
        r = lax.broadcasted_iota(jnp.int32, (tm, tm), 0)
        c = lax.broadcasted_iota(jnp.int32, (tm, tm), 1)
        tri = (c >= r).astype(F32)
        rc = jnp.dot(tri, dcum, preferred_element_type=F32, precision=hp) + carry[...]
        carry[...] = rc[0:1, :]
        df = rc * sg_ref[...]
        df_ref[...] = df.astype(BF16)
        dbf_ref[...] += _rows8(df)

    rev = lambda i: nsteps - 1 - i
    return pl.pallas_call(
        body, name="fgate_bwd", grid=(nsteps,),
        in_specs=[pl.BlockSpec((tm, ATTN_W), lambda i: (rev(i), 0)), pl.BlockSpec((tm, ATTN_W), lambda i: (rev(i), 0)),
                  pl.BlockSpec((tm, F_PAD), lambda i: (rev(i), 0))],
        out_specs=[pl.BlockSpec((tm, F_PAD), lambda i: (rev(i), 0)), pl.BlockSpec((8, F_PAD), lambda i: (0, 0))],
        out_shape=[jax.ShapeDtypeStruct((t, F_PAD), BF16), jax.ShapeDtypeStruct((8, F_PAD), F32)],
        scratch_shapes=[pltpu.VMEM((1, F_PAD), F32)],
        compiler_params=_cp(("arbitrary",)),
    )(dqx, dkx, sgate)


def _inproj_bwd(dq, dk, dv, df, dbcx, dh, x, g, win_p, *, tm=512):
    t, d = x.shape
    tm = min(tm, t)
    n_qkv = 3 * ATTN_W

    def body(dq_ref, dk_ref, dv_ref, df_ref, db_ref, dh_ref, x_ref, g_ref, w_ref, gx_ref, dg_ref):
        @pl.when(pl.program_id(0) == 0)
        def _():
            dg_ref[...] = jnp.zeros_like(dg_ref)
        dn = _dot_nt(df_ref[...], w_ref[:, n_qkv:n_qkv + F_PAD])
        for k, r in enumerate((dq_ref, dk_ref, dv_ref)):
            dn = dn + _dot_nt(r[...], w_ref[:, k * ATTN_W:(k + 1) * ATTN_W])
        for k in range(3):
            c0 = n_qkv + F_PAD + k * CONV_CH
            dn = dn + _dot_nt(db_ref[:, k * CONV_CH:(k + 1) * CONV_CH], w_ref[:, c0:c0 + CONV_CH])
        dx, dg = _norm_bwd(dn, x_ref[...], g_ref[...])
        gx_ref[...] = dh_ref[...] + dx
        dg_ref[...] += dg

    row = lambda n_: pl.BlockSpec((tm, n_), lambda i: (i, 0))
    return pl.pallas_call(
        body, name="inproj_bwd", grid=(t // tm,),
        in_specs=[row(ATTN_W), row(ATTN_W), row(ATTN_W), row(F_PAD), row(3 * CONV_CH), row(d), row(d),
                  _const_spec((1, d)), _const_spec(win_p.shape)],
        out_specs=[row(d), pl.BlockSpec((8, d), lambda i: (0, 0))],
        out_shape=[jax.ShapeDtypeStruct((t, d), F32), jax.ShapeDtypeStruct((8, d), F32)],
        compiler_params=_cp(("arbitrary",)),
    )(dq, dk, dv, df, dbcx, dh, x, g, win_p)


def _local_step(x, target, gains, b_f, conv_w, pool_scale, win_p, wout, wup0, wdown0, poolw, wup1, wdown1):
    d = x.shape[1]
    n0, qkv, flog, bcx = _norm_inproj(x, gains["mix0"], win_p)
    q_aug, k_aug, v_aug, sgate = _fgate_prep(flog, b_f, qkv)
    att, q_bwd = _attn_fwd(q_aug, k_aug, v_aug)
    cv = _conv_fwd(bcx, conv_w)
    h1 = _outproj(att, cv, x, wout)
    h2, n1, a0, z0 = _mlp_fwd(h1, gains["ffn0"], wup0, wdown0, name="mlp_fwd0")
    h3, pooled = _pool_fwd(h2, gains["mix1"], poolw, pool_scale)
    h4, n3, a1, z1 = _mlp_fwd(h3, gains["ffn1"], wup1, wdown1, name="mlp_fwd1")
    loss, dh4, dg_final = _loss_bwd(h4, gains["final"], target)
    f = a1.shape[1]
    fb = f // N_DEV
    dh3, da1, dg_ffn1 = _mlp_bwd(dh4, h3, a1, gains["ffn1"], wup1, wdown1, name="mlp_bwd1")
    dwdown1 = _mm_tn(z1, dh4, name="dwdown1", ta=1024, tb=1024, tt=1024)
    dwup1 = _mm_tn(n3, da1, name="dwup1", ta=d, tb=fb, tt=2048, blocked_out=True)
    dh2, dpoolw, dscale, dg_mix1 = _pool_bwd(dh3, h2, pooled, gains["mix1"], poolw, pool_scale)
    dh1, da0, dg_ffn0 = _mlp_bwd(dh2, h1, a0, gains["ffn0"], wup0, wdown0, name="mlp_bwd0")
    dwdown0 = _mm_tn(z0, dh2, name="dwdown0", ta=1024, tb=1024, tt=1024)
    dwup0 = _mm_tn(n1, da0, name="dwup0", ta=d, tb=fb, tt=2048, blocked_out=True)
    do_aug, dcv = _outproj_bwd(dh1, att, wout)
    dwout = jnp.concatenate([_mm_tn(att, dh1, name="dwout_att", ta=512, tb=1024, tt=2048),
                             _mm_tn(cv, dh1, name="dwout_conv", ta=512, tb=1024, tt=2048)], axis=0)
    dbcx, dconvw = _conv_bwd(bcx, dcv, conv_w)
    dq, dqx, dk, dkx, dv = _attn_bwd(q_bwd, do_aug, k_aug, v_aug)
    df, dbf = _fgate_bwd(dqx, dkx, sgate)
    grad_x, dg_mix0 = _inproj_bwd(dq, dk, dv, df, dbcx, dh1, x, gains["mix0"], win_p)
    dwin_p = jnp.concatenate(
        [_mm_tn(n0, dq, name="dwin_q", ta=d, tb=512, tt=2048), _mm_tn(n0, dk, name="dwin_k", ta=d, tb=512, tt=2048),
         _mm_tn(n0, dv, name="dwin_v", ta=d, tb=512, tt=2048), _mm_tn(n0, df, name="dwin_f", ta=d, tb=128, tt=2048),
         _mm_tn(n0, dbcx, name="dwin_bcx", ta=d, tb=512, tt=2048)], axis=1)
    big = dict(win_p=dwin_p, wout=dwout, wup0=dwup0, wdown0=dwdown0, poolw=dpoolw, wup1=dwup1, wdown1=dwdown1)
    small = dict(mix0=dg_mix0, ffn0=dg_ffn0, mix1=dg_mix1, pool_scale=dscale, ffn1=dg_ffn1, final=dg_final,
                 b_f=dbf, conv_w=dconvw)
    return loss, grad_x, big, small


HBM_SPEC = pl.BlockSpec(memory_space=pltpu.HBM)
VMEM_SPEC = pl.BlockSpec(memory_space=pltpu.VMEM)


def _mesh_places():
    x, y, c = lax.axis_index("x"), lax.axis_index("y"), lax.axis_index("c")
    chips = [(1 - x, y), (x, 1 - y), (1 - x, 1 - y)]
    return (x, y, c), (x, y, 1 - c), chips


def _slot(px, py, pc):
    return 4 * px + 2 * py + pc


def _all_gather(shards):
    n = len(shards)

    def body(*refs):
        ins, outs = refs[:n], refs[n:2 * n]
        send_sems, recv_sems, local_sems = refs[2 * n:]
        me, sib, chips = _mesh_places()
        c = me[2]

        def copy(ai, k, block, to, src=None):
            dst = outs[ai].at[_slot(*block)]
            return pltpu.make_async_remote_copy(
                src_ref=dst if src is None else src, dst_ref=dst, send_sem=send_sems.at[7 * ai + k],
                recv_sem=recv_sems.at[7 * ai + k], device_id=to, device_id_type=MESH)

        mine = [pltpu.make_async_copy(ins[ai], outs[ai].at[_slot(*me)], local_sems.at[ai]) for ai in range(n)]
        for cp in mine:
            cp.start()
        first = []
        for ai in range(n):
            first.append(copy(ai, 0, me, sib, src=ins[ai]))
            first += [copy(ai, 1 + j, me, (*chip, c), src=ins[ai]) for j, chip in enumerate(chips)]
        for cp in first:
            cp.start()
        passed = []
        for ai in range(n):
            for j, chip in enumerate(chips):
                copy(ai, 1 + j, (*chip, c), me).wait_recv()
                cp = copy(ai, 4 + j, (*chip, c), sib)
                cp.start()
                passed.append(cp)
        for ai in range(n):
            copy(ai, 0, sib, me).wait_recv()
            for j, chip in enumerate(chips):
                copy(ai, 4 + j, (*chip, 1 - c), me).wait_recv()
        for cp in first + passed:
            cp.wait_send()
        for cp in mine:
            cp.wait()

    return pl.pallas_call(
        body, name="all_gather",
        in_specs=[HBM_SPEC] * n, out_specs=[HBM_SPEC] * n,
        out_shape=[jax.ShapeDtypeStruct((N_DEV,) + s.shape, s.dtype) for s in shards],
        scratch_shapes=[pltpu.SemaphoreType.DMA((7 * n,)), pltpu.SemaphoreType.DMA((7 * n,)),
                        pltpu.SemaphoreType.DMA((n,))],
    )(*shards)


def _reduce_scatter(grads):
    n = len(grads)
    shapes = [g.shape[1:] for g in grads]

    def body(*refs):
        gs = refs[:n]
        outs, land_a, land_b = refs[n:2 * n], refs[2 * n:3 * n], refs[3 * n:4 * n]
        sendbufs = refs[4 * n:5 * n]
        sa_send, sa_recv, sb_send, sb_recv, lsem = refs[5 * n:]
        me, sib, chips = _mesh_places()
        x, y, c = me

        def copy_a(ai, q):
            qx, qy = q // 2, q % 2
            return pltpu.make_async_remote_copy(
                src_ref=gs[ai].at[_slot(qx, qy, 1 - c)], dst_ref=land_a[ai].at[q], send_sem=sa_send.at[4 * ai + q],
                recv_sem=sa_recv.at[4 * ai + q], device_id=sib, device_id_type=MESH)

        def copy_b(ai, j):
            return pltpu.make_async_remote_copy(
                src_ref=sendbufs[ai].at[j], dst_ref=land_b[ai].at[j], send_sem=sb_send.at[3 * ai + j],
                recv_sem=sb_recv.at[3 * ai + j], device_id=(*chips[j], c), device_id_type=MESH)

        def load(src, dst):
            cp = pltpu.make_async_copy(src, dst, lsem)
            cp.start()
            cp.wait()

        for ai in range(n):
            for q in range(4):
                copy_a(ai, q).start()
        for ai in range(n):
            def stage(buf_a, buf_b, ai=ai):
                for j, (qx, qy) in enumerate(chips):
                    q = 2 * qx + qy
                    pltpu.make_async_remote_copy(
                        src_ref=land_a[ai].at[0], dst_ref=land_a[ai].at[0], send_sem=sa_send.at[0],
                        recv_sem=sa_recv.at[4 * ai + q], device_id=sib, device_id_type=MESH).wait_recv()
                    load(gs[ai].at[_slot(qx, qy, c)], buf_a)
                    load(land_a[ai].at[q], buf_b)
                    sendbufs[ai][j] = (buf_a[...] + buf_b[...]).astype(BF16)
                    copy_b(ai, j).start()
            pl.run_scoped(stage, pltpu.VMEM(shapes[ai], F32), pltpu.VMEM(shapes[ai], F32))
        for ai in range(n):
            def final(buf_a, buf_b, buf_c, ai=ai):
                q = 2 * x + y
                pltpu.make_async_remote_copy(
                    src_ref=land_a[ai].at[0], dst_ref=land_a[ai].at[0], send_sem=sa_send.at[0],
                    recv_sem=sa_recv.at[4 * ai + q], device_id=sib, device_id_type=MESH).wait_recv()
                load(gs[ai].at[_slot(x, y, c)], buf_a)
                load(land_a[ai].at[q], buf_b)
                acc = buf_a[...] + buf_b[...]
                for j in range(3):
                    copy_b(ai, j).wait_recv()
                    load(land_b[ai].at[j], buf_c)
                    acc = acc + buf_c[...].astype(F32)
                buf_a[...] = acc
                load(buf_a, outs[ai])
            pl.run_scoped(final, pltpu.VMEM(shapes[ai], F32), pltpu.VMEM(shapes[ai], F32), pltpu.VMEM(shapes[ai], BF16))
        for ai in range(n):
            for q in range(4):
                copy_a(ai, q).wait_send()
            for j in range(3):
                copy_b(ai, j).wait_send()

    res = pl.pallas_call(
        body, name="reduce_scatter",
        in_specs=[HBM_SPEC] * n, out_specs=[HBM_SPEC] * (3 * n),
        out_shape=([jax.ShapeDtypeStruct(s, F32) for s in shapes]
                   + [jax.ShapeDtypeStruct((4,) + s, F32) for s in shapes]
                   + [jax.ShapeDtypeStruct((3,) + s, BF16) for s in shapes]),
        scratch_shapes=([pltpu.VMEM((3,) + s, BF16) for s in shapes]
                        + [pltpu.SemaphoreType.DMA((4 * n,)), pltpu.SemaphoreType.DMA((4 * n,)),
                           pltpu.SemaphoreType.DMA((3 * n,)), pltpu.SemaphoreType.DMA((3 * n,)),
                           pltpu.SemaphoreType.DMA(())]),
        compiler_params=pltpu.CompilerParams(vmem_limit_bytes=VMEM_LIMIT),
    )(*grads)
    return res[:n]


SMALL_ROWS = 16


def _small_allreduce(parts):
    n, _, w = parts.shape
    assert n <= SMALL_ROWS

    def body(p_ref, o_ref, gath, send_sems, recv_sems):
        x, y, c = lax.axis_index("x"), lax.axis_index("y"), lax.axis_index("c")
        my = _slot(x, y, c)
        rows = [jnp.sum(p_ref[i], axis=0, keepdims=True) for i in range(n)]
        rows.append(jnp.zeros((SMALL_ROWS - n, w), F32))
        gath[my] = jnp.concatenate(rows, axis=0)
        copies = []
        for k in range(1, N_DEV):
            px, py, pc = x ^ (k >> 2), y ^ ((k >> 1) & 1), c ^ (k & 1)
            cp = pltpu.make_async_remote_copy(
                src_ref=gath.at[my], dst_ref=gath.at[my], send_sem=send_sems.at[k - 1], recv_sem=recv_sems.at[k - 1],
                device_id=(px, py, pc), device_id_type=MESH)
            cp.start()
            copies.append(cp)
        for cp in copies:
            cp.wait()
        acc = gath[0]
        for d in range(1, N_DEV):
            acc = acc + gath[d]
        o_ref[...] = acc

    return pl.pallas_call(
        body, name="small_allreduce",
        in_specs=[VMEM_SPEC], out_specs=VMEM_SPEC,
        out_shape=jax.ShapeDtypeStruct((SMALL_ROWS, w), F32),
        scratch_shapes=[pltpu.VMEM((N_DEV, SMALL_ROWS, w), F32), pltpu.SemaphoreType.DMA((N_DEV - 1,)),
                        pltpu.SemaphoreType.DMA((N_DEV - 1,))],
    )(parts)


def _adamw(g, w, m, v, *, name, tm=256):
    r, c = g.shape
    tm = tm if r % tm == 0 else r
    bc1 = 1.0 - ADAM_B1 ** ADAM_STEP
    bc2 = 1.0 - ADAM_B2 ** ADAM_STEP

    def body(g_ref, w_ref, m_ref, v_ref, d_ref, nm_ref, nv_ref):
        gv = g_ref[...]
        nm = ADAM_B1 * m_ref[...] + (1.0 - ADAM_B1) * gv
        nv = ADAM_B2 * v_ref[...] + (1.0 - ADAM_B2) * jnp.square(gv)
        nm_ref[...] = nm
        nv_ref[...] = nv
        d_ref[...] = -ADAM_LR * ((nm / bc1) / (jnp.sqrt(nv / bc2) + ADAM_EPS) + ADAM_WD * w_ref[...])

    blk = pl.BlockSpec((tm, c), lambda i: (i, 0))
    shp = jax.ShapeDtypeStruct((r, c), F32)
    return pl.pallas_call(
        body, name=name, grid=(r // tm,), in_specs=[blk] * 4, out_specs=[blk] * 3, out_shape=[shp] * 3,
        compiler_params=_cp(("parallel",)),
    )(g, w, m, v)


BIG = ("w_in_0", "w_out_0", "w_up_0", "w_down_0", "pool_w_1", "w_up_1", "w_down_1")
SMALL = ("norm_mix_0", "norm_ffn_0", "norm_mix_1", "pool_scale_1", "norm_ffn_1", "final_norm", "b_f_0", "conv_w_0")
WEIGHTS = ("norm_mix_0", "w_in_0", "b_f_0", "conv_w_0", "w_out_0", "norm_ffn_0", "w_up_0", "w_down_0", "norm_mix_1",
           "pool_w_1", "pool_scale_1", "norm_ffn_1", "w_up_1", "w_down_1", "final_norm")


def _pad_to(a, rows, cols):
    return jnp.pad(a, ((0, rows - a.shape[0]), (0, cols - a.shape[1])))


def _pack_small(p, width):
    rows = [p[n].reshape(1, -1) for n in SMALL[:6]]
    rows.append(_pad_to(p["b_f_0"].reshape(1, -1), 1, width))
    rows.append(_pad_to(p["conv_w_0"], 3, width))
    return _pad_to(jnp.concatenate(rows, axis=0), SMALL_ROWS, width)


def _unpack_small(a, like):
    out = {n: a[i] for i, n in enumerate(SMALL[:6])}
    out["b_f_0"] = a[6, :like["b_f_0"].shape[0]]
    out["conv_w_0"] = a[7:10, :like["conv_w_0"].shape[1]]
    return out


def kernel(x, norm_mix_0, w_in_0, b_f_0, conv_w_0, w_out_0, norm_ffn_0, w_up_0, w_down_0, norm_mix_1, pool_w_1, pool_scale_1, norm_ffn_1, w_up_1, w_down_1, final_norm, loss_target, m_norm_mix_0, m_w_in_0, m_b_f_0, m_conv_w_0, m_w_out_0, m_norm_ffn_0, m_w_up_0, m_w_down_0, m_norm_mix_1, m_pool_w_1, m_pool_scale_1, m_norm_ffn_1, m_w_up_1, m_w_down_1, m_final_norm, v_norm_mix_0, v_w_in_0, v_b_f_0, v_conv_w_0, v_w_out_0, v_norm_ffn_0, v_w_up_0, v_w_down_0, v_norm_mix_1, v_pool_w_1, v_pool_scale_1, v_norm_ffn_1, v_w_up_1, v_w_down_1, v_final_norm):
    w = dict(norm_mix_0=norm_mix_0, w_in_0=w_in_0, b_f_0=b_f_0, conv_w_0=conv_w_0, w_out_0=w_out_0,
             norm_ffn_0=norm_ffn_0, w_up_0=w_up_0, w_down_0=w_down_0, norm_mix_1=norm_mix_1, pool_w_1=pool_w_1,
             pool_scale_1=pool_scale_1, norm_ffn_1=norm_ffn_1, w_up_1=w_up_1, w_down_1=w_down_1, final_norm=final_norm)
    m = dict(norm_mix_0=m_norm_mix_0, w_in_0=m_w_in_0, b_f_0=m_b_f_0, conv_w_0=m_conv_w_0, w_out_0=m_w_out_0,
             norm_ffn_0=m_norm_ffn_0, w_up_0=m_w_up_0, w_down_0=m_w_down_0, norm_mix_1=m_norm_mix_1,
             pool_w_1=m_pool_w_1, pool_scale_1=m_pool_scale_1, norm_ffn_1=m_norm_ffn_1, w_up_1=m_w_up_1,
             w_down_1=m_w_down_1, final_norm=m_final_norm)
    v = dict(norm_mix_0=v_norm_mix_0, w_in_0=v_w_in_0, b_f_0=v_b_f_0, conv_w_0=v_conv_w_0, w_out_0=v_w_out_0,
             norm_ffn_0=v_norm_ffn_0, w_up_0=v_w_up_0, w_down_0=v_w_down_0, norm_mix_1=v_norm_mix_1,
             pool_w_1=v_pool_w_1, pool_scale_1=v_pool_scale_1, norm_ffn_1=v_norm_ffn_1, w_up_1=v_w_up_1,
             w_down_1=v_w_down_1, final_norm=v_final_norm)
    d = x.shape[-1]
    n_in = w_in_0.shape[1] * N_DEV
    n_qkv = 3 * ATTN_W
    pool_g, pool_rows, pool_c = pool_w_1.shape

    def shard2d(p):
        return {n: (p[n].reshape(pool_g * pool_rows, pool_c) if n == "pool_w_1" else p[n]) for n in BIG}
    w2, m2, v2 = shard2d(w), shard2d(m), shard2d(v)

    conv_cols = conv_w_0.shape[1]
    *gathered, conv_g8 = _all_gather([w2[n].astype(BF16) for n in BIG] + [_pad_to(conv_w_0, 8, 128)])
    gath = dict(zip(BIG, gathered))
    conv_full = conv_g8[:, :, :conv_cols].transpose(1, 0, 2).reshape(8, N_DEV * conv_cols)
    win = gath["w_in_0"].transpose(1, 0, 2).reshape(d, n_in)
    win_p = jnp.concatenate([win[:, :n_qkv], _pad_to(win[:, n_qkv:n_qkv + N_HEADS], d, F_PAD),
                             win[:, n_qkv + N_HEADS:]], axis=1)
    wout = gath["w_out_0"].reshape(d, d)
    wdown0 = gath["w_down_0"].reshape(-1, d)
    wdown1 = gath["w_down_1"].reshape(-1, d)
    poolw = gath["pool_w_1"].reshape(N_DEV, pool_g, pool_rows, pool_c).transpose(1, 0, 2, 3).reshape(pool_g, pool_c, pool_c)

    gains = dict(mix0=norm_mix_0.reshape(1, d), ffn0=norm_ffn_0.reshape(1, d), mix1=norm_mix_1.reshape(1, d),
                 ffn1=norm_ffn_1.reshape(1, d), final=final_norm.reshape(1, d))
    dev = _slot(lax.axis_index("x"), lax.axis_index("y"), lax.axis_index("c"))
    loss8, grad_x, big, small = _local_step(
        x[0], loss_target[0], gains, _pad_to(b_f_0.reshape(1, -1), 1, F_PAD), conv_full, pool_scale_1.reshape(1, d),
        win_p, wout, gath["w_up_0"], wdown0, poolw, gath["w_up_1"], wdown1)
    loss = lax.psum(loss8[0, 0], ("x", "y", "c"))

    dwin = jnp.concatenate([big["win_p"][:, :n_qkv + N_HEADS], big["win_p"][:, n_qkv + F_PAD:]], axis=1)
    gfull = {
        "w_in_0": dwin.reshape(d, N_DEV, n_in // N_DEV).transpose(1, 0, 2),
        "w_out_0": big["wout"].reshape(N_DEV, d // N_DEV, d),
        "w_up_0": big["wup0"], "w_up_1": big["wup1"],
        "w_down_0": big["wdown0"].reshape(N_DEV, -1, d), "w_down_1": big["wdown1"].reshape(N_DEV, -1, d),
        "pool_w_1": big["poolw"].reshape(pool_g, N_DEV, pool_rows, pool_c).transpose(1, 0, 2, 3).reshape(
            N_DEV, pool_g * pool_rows, pool_c),
    }
    g2 = dict(zip(BIG, _reduce_scatter([gfull[n] for n in BIG])))

    parts = jnp.concatenate(
        [small[k][None] for k in ("mix0", "ffn0", "mix1", "pool_scale", "ffn1", "final")]
        + [_pad_to(small["b_f"], 8, d)[None], jnp.pad(small["conv_w"], ((0, 0), (0, 0), (0, d - CONV_CH)))], axis=0)
    tot = _small_allreduce(parts)
    conv_g = lax.dynamic_slice(tot, (7, dev * conv_cols), (3, conv_cols))
    gs = tot.at[7:10].set(_pad_to(conv_g, 3, d))

    grads, deltas, new_m, new_v = {}, {}, {}, {}
    for n in BIG:
        dl, nm, nv = _adamw(g2[n], w2[n], m2[n], v2[n], name="adamw_" + n)
        for dst, val in ((grads, g2[n]), (deltas, dl), (new_m, nm), (new_v, nv)):
            dst[n] = val.reshape(w[n].shape)
    dl, nm, nv = _adamw(gs, _pack_small(w, d), _pack_small(m, d), _pack_small(v, d), name="adamw_small")
    for dst, val in ((grads, gs), (deltas, dl), (new_m, nm), (new_v, nv)):
        dst.update(_unpack_small(val, w))
    return (loss, grad_x[None], *[grads[n] for n in WEIGHTS], *[deltas[n] for n in WEIGHTS],
            *[new_m[n] for n in WEIGHTS], *[new_v[n] for n in WEIGHTS])
```

```python
import functools

import jax
import jax.numpy as jnp
from jax import lax
from jax.experimental import pallas as pl
from jax.experimental.pallas import tpu as pltpu

F32 = jnp.float32
BF16 = jnp.bfloat16

N_DEV = 8
N_HEADS = 8
HEAD_DIM = 64
PAIR = 2 * HEAD_DIM
ATTN_W = N_HEADS * HEAD_DIM
CONV_CH = 512
F_PAD = 128
POOL_WINDOWS = (2, 4, 8, 16)
POOL_HALO = 16
CONV_HALO = 8
RMS_EPS = 1e-6
Q_SCALE = HEAD_DIM ** -0.5
LOG2E = 1.4426950408889634
NEG = -1e30
AUX_BIAS = 0
AUX_LSE = 3
AUX_ROWSUM = 6
ADAM_LR, ADAM_B1, ADAM_B2, ADAM_EPS, ADAM_WD, ADAM_STEP = 0.001, 0.9, 0.999, 1e-08, 0.01, 10
MESH = pl.DeviceIdType.MESH
VMEM_LIMIT = 56 * 2**20


def _cp(sem=None, vmem=VMEM_LIMIT, **kw):
    return pltpu.CompilerParams(dimension_semantics=sem, vmem_limit_bytes=vmem, **kw)


def _dot(a, b):
    return jnp.dot(a, b, preferred_element_type=F32)


def _dot_nt(a, b):
    return lax.dot_general(a, b, (((1,), (1,)), ((), ())), preferred_element_type=F32)


def _dot_tn(a, b):
    return lax.dot_general(a, b, (((0,), (0,)), ((), ())), preferred_element_type=F32)


def _rstd(h):
    return lax.rsqrt(jnp.mean(h * h, axis=-1, keepdims=True) + RMS_EPS)


def _rows8(x):
    r, n = x.shape
    return jnp.sum(x.reshape(r // 8, 8, n), axis=0)


def _norm_bwd(dn, h, g):
    r = _rstd(h)
    xhat = h * r
    dy = dn * g
    dh = r * (dy - xhat * jnp.mean(dy * xhat, axis=-1, keepdims=True))
    return dh, _rows8(dn * xhat)


def _const_spec(shape):
    nd = len(shape)
    return pl.BlockSpec(shape, lambda *_: (0,) * nd, pipeline_mode=pl.Buffered(1))


HBM_SPEC = pl.BlockSpec(memory_space=pltpu.HBM)
VMEM_SPEC = pl.BlockSpec(memory_space=pltpu.VMEM)


def _slot(px, py, pc):
    return 4 * px + 2 * py + pc


class _Exchange:
    def __init__(self, srcs, dsts, send_sems, recv_sems, local_sems, gather):
        x, y, c = lax.axis_index("x"), lax.axis_index("y"), lax.axis_index("c")
        me = _slot(x, y, c)
        self.copies = []
        for a, (src, dst) in enumerate(zip(srcs, dsts)):
            self.copies.append(pltpu.make_async_copy(src if gather else src.at[me], dst.at[me], local_sems.at[a]))
            for k in range(1, N_DEV):
                px, py, pc = x ^ (k >> 2), y ^ ((k >> 1) & 1), c ^ (k & 1)
                self.copies.append(pltpu.make_async_remote_copy(
                    src_ref=src if gather else src.at[_slot(px, py, pc)], dst_ref=dst.at[me],
                    send_sem=send_sems.at[(N_DEV - 1) * a + k - 1], recv_sem=recv_sems.at[(N_DEV - 1) * a + k - 1],
                    device_id=(px, py, pc), device_id_type=MESH))

    def start(self):
        for cp in self.copies:
            cp.start()

    def wait(self):
        for cp in self.copies:
            cp.wait()

    @staticmethod
    def scratch(n):
        return [pltpu.SemaphoreType.DMA(((N_DEV - 1) * n,)), pltpu.SemaphoreType.DMA(((N_DEV - 1) * n,)),
                pltpu.SemaphoreType.DMA((n,))]


def _norm_inproj(x, g, win_p, *, tm=512):
    t, d = x.shape
    n_all = win_p.shape[1]
    n_qkv = 3 * ATTN_W
    n_bcx = 3 * CONV_CH
    assert n_all == n_qkv + F_PAD + n_bcx
    tm = min(tm, t)

    def body(x_ref, g_ref, w_ref, n_ref, qkv_ref, f_ref, bcx_ref):
        h = x_ref[...]
        n = (h * _rstd(h) * g_ref[...]).astype(BF16)
        n_ref[...] = n
        for c0 in range(0, n_qkv, 512):
            acc = _dot(n, w_ref[:, c0:c0 + 512])
            if c0 < ATTN_W:
                acc = acc * (Q_SCALE * LOG2E)
            qkv_ref[:, c0:c0 + 512] = acc.astype(BF16)
        f_ref[...] = _dot(n, w_ref[:, n_qkv:n_qkv + F_PAD])
        for c0 in range(0, n_bcx, 512):
            bcx_ref[:, c0:c0 + 512] = _dot(n, w_ref[:, n_qkv + F_PAD + c0:n_qkv + F_PAD + c0 + 512])

    return pl.pallas_call(
        body, name="norm_inproj", grid=(t // tm,),
        in_specs=[pl.BlockSpec((tm, d), lambda i: (i, 0)), _const_spec((1, d)), _const_spec((d, n_all))],
        out_specs=[pl.BlockSpec((tm, d), lambda i: (i, 0)), pl.BlockSpec((tm, n_qkv), lambda i: (i, 0)),
                   pl.BlockSpec((tm, F_PAD), lambda i: (i, 0)), pl.BlockSpec((tm, n_bcx), lambda i: (i, 0))],
        out_shape=[jax.ShapeDtypeStruct((t, d), BF16), jax.ShapeDtypeStruct((t, n_qkv), BF16),
                   jax.ShapeDtypeStruct((t, F_PAD), F32), jax.ShapeDtypeStruct((t, n_bcx), F32)],
        compiler_params=_cp(("parallel",)),
    )(x, g, win_p)


def _head_lanes(h):
    lane = lax.broadcasted_iota(jnp.int32, (1, PAIR), 1)
    hh = h % 2
    return lane, lane // HEAD_DIM == hh, HEAD_DIM * (1 - hh)


def _pieces(col):
    hi = col.astype(BF16).astype(F32)
    r1 = col - hi
    mid = r1.astype(BF16).astype(F32)
    lo = (r1 - mid).astype(BF16).astype(F32)
    return hi, mid, lo


def _put_pieces(lane, first, col, other):
    hi, mid, lo = _pieces(col)
    return jnp.where(lane == first, hi, jnp.where(lane == first + 1, mid, jnp.where(lane == first + 2, lo, other)))


def _fgate_prep(flog, b_f, qkv, *, tm=256):
    t = flog.shape[0]
    tm = min(tm, t)

    def body(f_ref, b_ref, qkv_ref, qa_ref, ka_ref, va_ref, sg_ref, carry):
        @pl.when(pl.program_id(0) == 0)
        def _():
            carry[...] = jnp.zeros_like(carry)
        z = f_ref[...] + b_ref[...]
        e = jnp.exp(-jnp.abs(z))
        logf = jnp.minimum(z, 0.0) - jnp.log(1.0 + e)
        sg_ref[...] = jnp.where(z >= 0, e, 1.0) / (1.0 + e)
        r = lax.broadcasted_iota(jnp.int32, (tm, tm), 0)
        c = lax.broadcasted_iota(jnp.int32, (tm, tm), 1)
        tri = (c <= r).astype(F32)
        cs = jnp.dot(tri, logf, preferred_element_type=F32, precision=lax.Precision.HIGHEST) + carry[...]
        carry[...] = cs[tm - 1:tm, :]
        cs2 = cs * LOG2E
        for h in range(N_HEADS):
            lane, head, aux = _head_lanes(h)
            p0 = (h // 2) * PAIR
            ones = ((lane >= aux + AUX_LSE) & (lane <= aux + AUX_ROWSUM)).astype(F32)
            bias = (lane >= aux + AUX_BIAS) & (lane < aux + AUX_BIAS + 3)
            k_aux = _put_pieces(lane, aux + AUX_BIAS, cs2[:, h:h + 1], ones)
            qa_ref[h] = jnp.where(head, qkv_ref[:, p0:p0 + PAIR], jnp.where(bias, -1.0, 0.0).astype(BF16))
            ka_ref[h] = jnp.where(head, qkv_ref[:, ATTN_W + p0:ATTN_W + p0 + PAIR], k_aux.astype(BF16))
            va_ref[h] = jnp.where(head, qkv_ref[:, 2 * ATTN_W + p0:2 * ATTN_W + p0 + PAIR],
                                  jnp.where(bias, 1.0, 0.0).astype(BF16))

    aug = lambda: pl.BlockSpec((N_HEADS, tm, PAIR), lambda i: (0, i, 0))
    aug_shape = jax.ShapeDtypeStruct((N_HEADS, t, PAIR), BF16)
    return pl.pallas_call(
        body, name="fgate_prep", grid=(t // tm,),
        in_specs=[pl.BlockSpec((tm, F_PAD), lambda i: (i, 0)), _const_spec((1, F_PAD)),
                  pl.BlockSpec((tm, 3 * ATTN_W), lambda i: (i, 0))],
        out_specs=[aug(), aug(), aug(), pl.BlockSpec((tm, F_PAD), lambda i: (i, 0))],
        out_shape=[aug_shape, aug_shape, aug_shape, jax.ShapeDtypeStruct((t, F_PAD), F32)],
        scratch_shapes=[pltpu.VMEM((1, F_PAD), F32)],
        compiler_params=_cp(("arbitrary",)),
    )(flog, b_f, qkv)


def _attn_fwd(q_aug, k_aug, v_aug, shards, *, tq=512):
    t = q_aug.shape[1]
    tq = min(tq, t)
    tk = tq
    nq = t // tq
    n_pairs = ATTN_W // PAIR
    n_sh = len(shards)

    def body(q_ref, k_ref, v_ref, *rest):
        o_ref, qb_ref = rest[n_sh:n_sh + 2]
        s_scr = rest[2 * n_sh + 2]
        gather = _Exchange(rest[:n_sh], rest[n_sh + 2:2 * n_sh + 2], *rest[2 * n_sh + 3:], gather=True)
        i = pl.program_id(1)

        @pl.when((pl.program_id(0) == 0) & (i == 0))
        def _():
            gather.start()
        row = lax.broadcasted_iota(jnp.int32, (tq, tk), 0)
        col = lax.broadcasted_iota(jnp.int32, (tq, tk), 1)
        q = [q_ref[0], q_ref[1]]

        def logits(hh, tile, slot, diag):
            s = _dot_nt(q[hh], k_ref[hh, pl.ds(pl.multiple_of(tile * tk, tk), tk), :])
            if diag:
                s = jnp.where(col <= row, s, NEG)
            s_scr[hh, slot] = s
            return jnp.max(s, axis=1, keepdims=True)

        def probs(hh, tile, slot, m, acc, tmax):
            mn = jnp.maximum(m, tmax)
            p = jnp.exp2(s_scr[hh, slot] - mn).astype(BF16)
            acc = jnp.exp2(m - mn) * acc + _dot(p, v_ref[hh, pl.ds(pl.multiple_of(tile * tk, tk), tk), :])
            return mn, acc

        def advance(carry, prev, slot, nxt):
            out = []
            for hh in range(2):
                m, acc, tmax = carry[hh]
                m, acc = probs(hh, prev, slot, m, acc, tmax)
                out.append((m, acc, logits(hh, nxt, 1 - slot, False)))
            return tuple(out)

        def two_tiles(jj, carry):
            carry = advance(carry, jnp.where(jj == 0, i, 2 * jj - 1), 0, 2 * jj)
            return advance(carry, 2 * jj, 1, 2 * jj + 1)

        init = tuple((jnp.full((tq, 1), NEG, F32), jnp.zeros((tq, PAIR), F32), logits(hh, i, 0, True))
                     for hh in range(2))
        carry = lax.fori_loop(0, i // 2, two_tiles, init)
        odd = i % 2 == 1
        carry = lax.cond(odd, lambda c: advance(c, jnp.where(i == 1, i, i - 2), 0, i - 1), lambda c: c, carry)
        last = jnp.where(i == 0, i, i - 1)
        res = []
        for hh in range(2):
            lane, head, aux = _head_lanes(hh)
            m, acc, tmax = carry[hh]
            m, acc = lax.cond(odd, lambda a: probs(hh, last, 1, *a), lambda a: probs(hh, last, 0, *a), (m, acc, tmax))
            l = jnp.sum(jnp.where(lane == aux + AUX_BIAS, acc, 0.0), axis=1, keepdims=True)
            qb_ref[hh] = _put_pieces(lane, aux + AUX_LSE, -(m + jnp.log2(l)), q[hh].astype(F32)).astype(BF16)
            res.append(acc / l)
        lane = lax.broadcasted_iota(jnp.int32, (1, PAIR), 1)
        o_ref[...] = jnp.where(lane < HEAD_DIM, res[0], res[1]).astype(BF16)

        @pl.when((pl.program_id(0) == n_pairs - 1) & (i == nq - 1))
        def _():
            gather.wait()

    res = pl.pallas_call(
        body, name="attn_fwd", grid=(n_pairs, nq),
        in_specs=[pl.BlockSpec((2, tq, PAIR), lambda p, i: (p, i, 0)),
                  pl.BlockSpec((2, t, PAIR), lambda p, i: (p, 0, 0), pipeline_mode=pl.Buffered(1)),
                  pl.BlockSpec((2, t, PAIR), lambda p, i: (p, 0, 0), pipeline_mode=pl.Buffered(1))] + [HBM_SPEC] * n_sh,
        out_specs=[pl.BlockSpec((tq, PAIR), lambda p, i: (i, p)),
                   pl.BlockSpec((2, tq, PAIR), lambda p, i: (p, i, 0))] + [HBM_SPEC] * n_sh,
        out_shape=[jax.ShapeDtypeStruct((t, ATTN_W), BF16), jax.ShapeDtypeStruct((N_HEADS, t, PAIR), BF16)]
        + [jax.ShapeDtypeStruct((N_DEV,) + s.shape, s.dtype) for s in shards],
        scratch_shapes=[pltpu.VMEM((2, 2, tq, tk), F32)] + _Exchange.scratch(n_sh),
        compiler_params=_cp(("arbitrary", "arbitrary")),
    )(q_aug, k_aug, v_aug, *shards)
    return res[0], res[1], res[2:]


def _prev_halo(tm, halo):
    return lambda i: (jnp.maximum(i * (tm // halo) - 1, 0), 0)


def _next_halo(tm, halo, t):
    return lambda i: (jnp.minimum((i + 1) * (tm // halo), t // halo - 1), 0)


def _conv_fwd(bcx, conv_w, *, tm=512):
    t = bcx.shape[0]
    tm = min(tm, t)
    ch = CONV_CH

    def body(b_ref, c_ref, x_ref, hc_ref, hx_ref, w_ref, cv_ref, ext):
        first = pl.program_id(0) == 0
        ext[0:CONV_HALO, :] = jnp.where(first, 0.0, hc_ref[...] * hx_ref[...])
        ext[CONV_HALO:CONV_HALO + tm, :] = c_ref[...] * x_ref[...]
        conv = (w_ref[0:1, :] * ext[CONV_HALO - 2:CONV_HALO - 2 + tm, :]
                + w_ref[1:2, :] * ext[CONV_HALO - 1:CONV_HALO - 1 + tm, :]
                + w_ref[2:3, :] * ext[CONV_HALO:CONV_HALO + tm, :])
        cv_ref[...] = (b_ref[...] * conv).astype(BF16)

    col = lambda k: pl.BlockSpec((tm, ch), lambda i: (i, k))
    halo = lambda k: pl.BlockSpec((CONV_HALO, ch), lambda i: (_prev_halo(tm, CONV_HALO)(i)[0], k))
    return pl.pallas_call(
        body, name="conv_fwd", grid=(t // tm,),
        in_specs=[col(0), col(1), col(2), halo(1), halo(2), _const_spec((8, ch))],
        out_specs=pl.BlockSpec((tm, ch), lambda i: (i, 0)),
        out_shape=jax.ShapeDtypeStruct((t, ch), BF16),
        scratch_shapes=[pltpu.VMEM((CONV_HALO + tm, ch), F32)],
        compiler_params=_cp(("parallel",)),
    )(bcx, bcx, bcx, bcx, bcx, conv_w)


def _outproj(att, cv, x, wout, *, tm=512):
    t, d = x.shape
    tm = min(tm, t)

    def body(a_ref, c_ref, x_ref, w_ref, h_ref):
        h_ref[...] = x_ref[...] + _dot(a_ref[...], w_ref[0:ATTN_W, :]) + _dot(c_ref[...], w_ref[ATTN_W:, :])

    return pl.pallas_call(
        body, name="outproj", grid=(t // tm,),
        in_specs=[pl.BlockSpec((tm, ATTN_W), lambda i: (i, 0)), pl.BlockSpec((tm, CONV_CH), lambda i: (i, 0)),
                  pl.BlockSpec((tm, d), lambda i: (i, 0)), _const_spec(wout.shape)],
        out_specs=pl.BlockSpec((tm, d), lambda i: (i, 0)),
        out_shape=jax.ShapeDtypeStruct((t, d), F32),
        compiler_params=_cp(("parallel",)),
    )(att, cv, x, wout)


def _mlp_fwd(h, g, wup, wdown, *, name, tm=256):
    t, d = h.shape
    n_blk, _, fb = wup.shape
    f = n_blk * fb
    tm = min(tm, t)

    def body(h_ref, g_ref, wu_ref, wd_ref, ho_ref, n_ref, a_ref, z_ref):
        hh = h_ref[...]
        n = (hh * _rstd(hh) * g_ref[...]).astype(BF16)
        n_ref[...] = n
        acc = hh
        for k in range(n_blk):
            a = _dot(n, wu_ref[k])
            zz = jnp.square(jnp.maximum(a, 0.0)).astype(BF16)
            a_ref[:, k * fb:(k + 1) * fb] = a.astype(BF16)
            z_ref[:, k * fb:(k + 1) * fb] = zz
            acc = acc + _dot(zz, wd_ref[k * fb:(k + 1) * fb, :])
        ho_ref[...] = acc

    row = lambda n_: pl.BlockSpec((tm, n_), lambda i: (i, 0))
    return pl.pallas_call(
        body, name=name, grid=(t // tm,),
        in_specs=[row(d), _const_spec((1, d)), _const_spec(wup.shape), _const_spec(wdown.shape)],
        out_specs=[row(d), row(d), row(f), row(f)],
        out_shape=[jax.ShapeDtypeStruct((t, d), F32), jax.ShapeDtypeStruct((t, d), BF16),
                   jax.ShapeDtypeStruct((t, f), BF16), jax.ShapeDtypeStruct((t, f), BF16)],
        compiler_params=_cp(("parallel",)),
    )(h, g, wup, wdown)


def _pool_inv_count(i, tm):
    tglob = (i * tm + lax.broadcasted_iota(jnp.int32, (tm, 1), 0) + 1).astype(F32)
    return [1.0 / jnp.minimum(tglob, float(w)) for w in POOL_WINDOWS]


def _pool_fwd(h, g, poolw, scale, *, tm=256):
    t, d = h.shape
    tm = min(tm, t)
    cg = d // len(POOL_WINDOWS)

    def body(h_ref, hh_ref, g_ref, w_ref, s_ref, ho_ref, p_ref, ext):
        i = pl.program_id(0)
        hv = h_ref[...]
        halo = hh_ref[...]
        n = hv * _rstd(hv) * g_ref[...]
        ext[0:POOL_HALO, :] = jnp.where(i == 0, 0.0, halo * _rstd(halo) * g_ref[...])
        ext[POOL_HALO:POOL_HALO + tm, :] = n
        inv = _pool_inv_count(i, tm)
        for gi, w in enumerate(POOL_WINDOWS):
            cs = slice(gi * cg, (gi + 1) * cg)
            s = ext[POOL_HALO:POOL_HALO + tm, cs]
            for j in range(1, w):
                s = s + ext[POOL_HALO - j:POOL_HALO - j + tm, cs]
            pooled = (s * inv[gi] - n[:, cs]).astype(BF16)
            p_ref[:, cs] = pooled
            ho_ref[:, cs] = hv[:, cs] + _dot(pooled, w_ref[gi]) * s_ref[:, cs]

    row = lambda: pl.BlockSpec((tm, d), lambda i: (i, 0))
    return pl.pallas_call(
        body, name="pool_fwd", grid=(t // tm,),
        in_specs=[row(), pl.BlockSpec((POOL_HALO, d), _prev_halo(tm, POOL_HALO)), _const_spec((1, d)),
                  _const_spec(poolw.shape), _const_spec((1, d))],
        out_specs=[row(), row()],
        out_shape=[jax.ShapeDtypeStruct((t, d), F32), jax.ShapeDtypeStruct((t, d), BF16)],
        scratch_shapes=[pltpu.VMEM((POOL_HALO + tm, d), F32)],
        compiler_params=_cp(("parallel",)),
    )(h, h, g, poolw, scale)


def _loss_bwd(h, g, target, *, tm=512):
    t, d = h.shape
    tm = min(tm, t)
    nsteps = t // tm

    def body(h_ref, g_ref, y_ref, loss_ref, dh_ref, dg_ref, lacc):
        i = pl.program_id(0)

        @pl.when(i == 0)
        def _():
            lacc[...] = jnp.zeros_like(lacc)
            dg_ref[...] = jnp.zeros_like(dg_ref)
        hv = h_ref[...]
        gv = g_ref[...]
        r = _rstd(hv)
        xhat = hv * r
        err = xhat * gv - y_ref[...]
        lacc[...] += _rows8(err * err)
        dout = err * (1.0 / d)
        dy = dout * gv
        dg_ref[...] += _rows8(dout * xhat)
        dh_ref[...] = r * (dy - xhat * jnp.mean(dy * xhat, axis=-1, keepdims=True))

        @pl.when(i == nsteps - 1)
        def _():
            loss_ref[...] = jnp.full(loss_ref.shape, (0.5 / d) * jnp.sum(lacc[...]), F32)

    row = lambda: pl.BlockSpec((tm, d), lambda i: (i, 0))
    return pl.pallas_call(
        body, name="loss_bwd", grid=(nsteps,),
        in_specs=[row(), _const_spec((1, d)), row()],
        out_specs=[pl.BlockSpec((8, 128), lambda i: (0, 0)), row(), pl.BlockSpec((8, d), lambda i: (0, 0))],
        out_shape=[jax.ShapeDtypeStruct((8, 128), F32), jax.ShapeDtypeStruct((t, d), F32),
                   jax.ShapeDtypeStruct((8, d), F32)],
        scratch_shapes=[pltpu.VMEM((8, d), F32)],
        compiler_params=_cp(("arbitrary",)),
    )(h, g, target)


def _mm_tn(a, b, *, name, ta, tb, tt, blocked_out=False, out_dtype=F32):
    t, ka = a.shape
    n = b.shape[1]
    ta, tb, tt = min(ta, ka), min(tb, n), min(tt, t)
    nt = t // tt

    def body(a_ref, b_ref, o_ref, acc):
        @pl.when(pl.program_id(2) == 0)
        def _():
            acc[...] = jnp.zeros_like(acc)
        acc[...] += _dot_tn(a_ref[...].astype(BF16), b_ref[...].astype(BF16))

        @pl.when(pl.program_id(2) == nt - 1)
        def _():
            o_ref[...] = acc[...].astype(out_dtype)

    if blocked_out:
        assert ta == ka
        out_shape = jax.ShapeDtypeStruct((n // tb, ka, tb), out_dtype)
        out_spec = pl.BlockSpec((None, ta, tb), lambda i, j, k: (j, i, 0))
    else:
        out_shape = jax.ShapeDtypeStruct((ka, n), out_dtype)
        out_spec = pl.BlockSpec((ta, tb), lambda i, j, k: (i, j))
    return pl.pallas_call(
        body, name=name, grid=(ka // ta, n // tb, nt),
        in_specs=[pl.BlockSpec((tt, ta), lambda i, j, k: (k, i)), pl.BlockSpec((tt, tb), lambda i, j, k: (k, j))],
        out_specs=out_spec, out_shape=out_shape, scratch_shapes=[pltpu.VMEM((ta, tb), F32)],
        compiler_params=_cp(("parallel", "parallel", "arbitrary")),
    )(a, b)


def _mlp_bwd(dho, h, a, g, wup, wdown, *, name, tm=256):
    t, d = h.shape
    n_blk, _, fb = wup.shape
    f = n_blk * fb
    tm = min(tm, t)

    def body(do_ref, h_ref, a_ref, g_ref, wu_ref, wd_ref, dh_ref, da_ref, dg_ref):
        @pl.when(pl.program_id(0) == 0)
        def _():
            dg_ref[...] = jnp.zeros_like(dg_ref)
        dho_v = do_ref[...]
        dob = dho_v.astype(BF16)
        dn = jnp.zeros((tm, d), F32)
        for k in range(n_blk):
            dz = _dot_nt(dob, wd_ref[k * fb:(k + 1) * fb, :])
            da = (dz * (2.0 * jnp.maximum(a_ref[:, k * fb:(k + 1) * fb].astype(F32), 0.0))).astype(BF16)
            da_ref[:, k * fb:(k + 1) * fb] = da
            dn = dn + _dot_nt(da, wu_ref[k])
        dh, dg = _norm_bwd(dn, h_ref[...], g_ref[...])
        dh_ref[...] = dho_v + dh
        dg_ref[...] += dg

    row = lambda n_: pl.BlockSpec((tm, n_), lambda i: (i, 0))
    return pl.pallas_call(
        body, name=name, grid=(t // tm,),
        in_specs=[row(d), row(d), row(f), _const_spec((1, d)), _const_spec(wup.shape), _const_spec(wdown.shape)],
        out_specs=[row(d), row(f), pl.BlockSpec((8, d), lambda i: (0, 0))],
        out_shape=[jax.ShapeDtypeStruct((t, d), F32), jax.ShapeDtypeStruct((t, f), BF16),
                   jax.ShapeDtypeStruct((8, d), F32)],
        compiler_params=_cp(("arbitrary",)),
    )(dho, h, a, g, wup, wdown)


def _pool_bwd(dho, h, pooled, g, poolw, scale, *, tm=256):
    t, d = h.shape
    tm = min(tm, t)
    ng = len(POOL_WINDOWS)
    cg = d // ng
    nsteps = t // tm

    def body(do_ref, dn_ref, h_ref, p_ref, g_ref, w_ref, s_ref, dh_ref, dw_ref, ds_ref, dg_ref, ext):
        i = pl.program_id(0)

        @pl.when(i == 0)
        def _():
            dw_ref[...] = jnp.zeros_like(dw_ref)
            ds_ref[...] = jnp.zeros_like(ds_ref)
            dg_ref[...] = jnp.zeros_like(dg_ref)
        dho_v = do_ref[...]
        sv = s_ref[...]
        dyp = (dho_v * sv).astype(BF16)
        dyp_halo = (dn_ref[...] * sv).astype(BF16)
        inv = _pool_inv_count(i, tm)
        tnext = ((i + 1) * tm + lax.broadcasted_iota(jnp.int32, (POOL_HALO, 1), 0) + 1).astype(F32)
        last = i == nsteps - 1
        ypre_parts, dpooled_parts = [], []
        for gi, w in enumerate(POOL_WINDOWS):
            cs = slice(gi * cg, (gi + 1) * cg)
            pg = p_ref[:, cs]
            ypre_parts.append(_dot(pg, w_ref[gi]))
            dw_ref[gi] += _dot_tn(pg, dyp[:, cs])
            dpool = _dot_nt(dyp[:, cs], w_ref[gi])
            dpooled_parts.append(dpool)
            ext[0:tm, cs] = dpool * inv[gi]
            dpool_halo = _dot_nt(dyp_halo[:, cs], w_ref[gi]) * (1.0 / jnp.minimum(tnext, float(w)))
            ext[tm:tm + POOL_HALO, cs] = jnp.where(last, 0.0, dpool_halo)
        ds_ref[...] += _rows8(dho_v * jnp.concatenate(ypre_parts, axis=1))
        dn_parts = []
        for gi, w in enumerate(POOL_WINDOWS):
            cs = slice(gi * cg, (gi + 1) * cg)
            s = ext[0:tm, cs]
            for j in range(1, w):
                s = s + ext[j:j + tm, cs]
            dn_parts.append(s - dpooled_parts[gi])
        dh, dg = _norm_bwd(jnp.concatenate(dn_parts, axis=1), h_ref[...], g_ref[...])
        dh_ref[...] = dho_v + dh
        dg_ref[...] += dg

    row = lambda: pl.BlockSpec((tm, d), lambda i: (i, 0))
    acc8 = lambda: pl.BlockSpec((8, d), lambda i: (0, 0))
    return pl.pallas_call(
        body, name="pool_bwd", grid=(nsteps,),
        in_specs=[row(), pl.BlockSpec((POOL_HALO, d), _next_halo(tm, POOL_HALO, t)), row(), row(),
                  _const_spec((1, d)), _const_spec(poolw.shape), _const_spec((1, d))],
        out_specs=[row(), pl.BlockSpec((ng, cg, cg), lambda i: (0, 0, 0)), acc8(), acc8()],
        out_shape=[jax.ShapeDtypeStruct((t, d), F32), jax.ShapeDtypeStruct((ng, cg, cg), F32),
                   jax.ShapeDtypeStruct((8, d), F32), jax.ShapeDtypeStruct((8, d), F32)],
        scratch_shapes=[pltpu.VMEM((tm + POOL_HALO, d), F32)],
        compiler_params=_cp(("arbitrary",)),
    )(dho, dho, h, pooled, g, poolw, scale)


def _outproj_bwd(dh, o, wout, *, tm=512):
    t, d = dh.shape
    tm = min(tm, t)

    def body(dh_ref, o_ref, w_ref, da_ref, dc_ref):
        dhb = dh_ref[...].astype(BF16)
        dc_ref[...] = _dot_nt(dhb, w_ref[ATTN_W:, :])
        for p in range(ATTN_W // PAIR):
            datt = _dot_nt(dhb, w_ref[p * PAIR:(p + 1) * PAIR, :])
            prod = datt * o_ref[:, p * PAIR:(p + 1) * PAIR].astype(F32)
            for hh in range(2):
                lane, head, aux = _head_lanes(hh)
                delta = jnp.sum(jnp.where(head, prod, 0.0), axis=1, keepdims=True)
                da_ref[2 * p + hh] = _put_pieces(lane, aux + AUX_BIAS, -delta, jnp.where(head, datt, 0.0)).astype(BF16)

    row = lambda n_: pl.BlockSpec((tm, n_), lambda i: (i, 0))
    return pl.pallas_call(
        body, name="outproj_bwd", grid=(t // tm,),
        in_specs=[row(d), row(ATTN_W), _const_spec(wout.shape)],
        out_specs=[pl.BlockSpec((N_HEADS, tm, PAIR), lambda i: (0, i, 0)), row(CONV_CH)],
        out_shape=[jax.ShapeDtypeStruct((N_HEADS, t, PAIR), BF16), jax.ShapeDtypeStruct((t, CONV_CH), F32)],
        compiler_params=_cp(("parallel",)),
    )(dh, o, wout)


def _conv_bwd(bcx, dcv, conv_w, *, tm=512):
    t = bcx.shape[0]
    tm = min(tm, t)
    ch = CONV_CH
    nsteps = t // tm

    def body(b_ref, c_ref, x_ref, hc_ref, hx_ref, d_ref, nb_ref, nd_ref, w_ref, o_ref, dw_ref, ext_u, ext_d):
        i = pl.program_id(0)

        @pl.when(i == 0)
        def _():
            dw_ref[...] = jnp.zeros_like(dw_ref)
        b, c, x, dcv_v = b_ref[...], c_ref[...], x_ref[...], d_ref[...]
        ext_u[0:CONV_HALO, :] = jnp.where(i == 0, 0.0, hc_ref[...] * hx_ref[...])
        ext_u[CONV_HALO:CONV_HALO + tm, :] = c * x
        dconv = dcv_v * b
        ext_d[0:tm, :] = dconv
        ext_d[tm:tm + CONV_HALO, :] = jnp.where(i == nsteps - 1, 0.0, nd_ref[...] * nb_ref[...])
        u = [ext_u[CONV_HALO - 2 + k:CONV_HALO - 2 + k + tm, :] for k in range(3)]
        conv = w_ref[0:1, :] * u[0] + w_ref[1:2, :] * u[1] + w_ref[2:3, :] * u[2]
        du = (w_ref[2:3, :] * dconv + w_ref[1:2, :] * ext_d[1:1 + tm, :] + w_ref[0:1, :] * ext_d[2:2 + tm, :])
        o_ref[:, 0:ch] = (dcv_v * conv).astype(BF16)
        o_ref[:, ch:2 * ch] = (du * x).astype(BF16)
        o_ref[:, 2 * ch:3 * ch] = (du * c).astype(BF16)
        for k in range(3):
            dw_ref[k] += _rows8(dconv * u[k])

    col = lambda k: pl.BlockSpec((tm, ch), lambda i: (i, k))
    prev = lambda k: pl.BlockSpec((CONV_HALO, ch), lambda i: (_prev_halo(tm, CONV_HALO)(i)[0], k))
    nxt = lambda k: pl.BlockSpec((CONV_HALO, ch), lambda i: (_next_halo(tm, CONV_HALO, t)(i)[0], k))
    return pl.pallas_call(
        body, name="conv_bwd", grid=(nsteps,),
        in_specs=[col(0), col(1), col(2), prev(1), prev(2), col(0), nxt(0), nxt(0), _const_spec((8, ch))],
        out_specs=[pl.BlockSpec((tm, 3 * ch), lambda i: (i, 0)), pl.BlockSpec((3, 8, ch), lambda i: (0, 0, 0))],
        out_shape=[jax.ShapeDtypeStruct((t, 3 * ch), BF16), jax.ShapeDtypeStruct((3, 8, ch), F32)],
        scratch_shapes=[pltpu.VMEM((CONV_HALO + tm, ch), F32), pltpu.VMEM((tm + CONV_HALO, ch), F32)],
        compiler_params=_cp(("arbitrary",)),
    )(bcx, bcx, bcx, bcx, bcx, dcv, bcx, dcv, conv_w)


def _attn_bwd(q_bwd, do_aug, k_aug, v_aug, gblocks, *, tq=512):
    t = q_bwd.shape[1]
    tq = min(tq, t)
    tk = tq
    nq = t // tq
    n_pairs = ATTN_W // PAIR
    n_g = len(gblocks)

    def body(q_ref, do_ref, k_ref, v_ref, *rest):
        dq_ref, dqx_ref, dk_ref, dkx_ref, dv_ref = rest[n_g:n_g + 5]
        dq_scr = rest[2 * n_g + 5]
        scatter = _Exchange(rest[:n_g], rest[n_g + 5:2 * n_g + 5], *rest[2 * n_g + 6:], gather=False)
        j = pl.program_id(1)

        @pl.when((pl.program_id(0) == 0) & (j == 0))
        def _():
            scatter.start()

        @pl.when(j == 0)
        def _():
            dq_scr[...] = jnp.zeros_like(dq_scr)
        row = lax.broadcasted_iota(jnp.int32, (tq, tk), 0)
        col = lax.broadcasted_iota(jnp.int32, (tq, tk), 1)
        k = [k_ref[0], k_ref[1]]
        v = [v_ref[0], v_ref[1]]

        def step(i, carry, diag):
            qs = pl.multiple_of(i * tq, tq)
            out = []
            for hh in range(2):
                dk_a, dv_a = carry[hh]
                q = q_ref[hh, pl.ds(qs, tq), :]
                dov = do_ref[hh, pl.ds(qs, tq), :]
                p = jnp.exp2(_dot_nt(q, k[hh]))
                if diag:
                    p = jnp.where(col <= row, p, 0.0)
                ds = (p * _dot_nt(dov, v[hh])).astype(BF16)
                dv_a = dv_a + _dot_tn(p.astype(BF16), dov)
                dk_a = dk_a + _dot_tn(ds, q)
                dq_scr[hh, pl.ds(qs, tq), :] += _dot(ds, k[hh])
                out.append((dk_a, dv_a))
            return tuple(out)

        zero = (jnp.zeros((tk, PAIR), F32), jnp.zeros((tk, PAIR), F32))
        carry = step(j, (zero, zero), True)
        (dk0, dv0), (dk1, dv1) = lax.fori_loop(j + 1, nq, functools.partial(step, diag=False), carry)
        first = lax.broadcasted_iota(jnp.int32, (1, PAIR), 1) < HEAD_DIM
        dk_ref[...] = (jnp.where(first, dk0, dk1) * (1.0 / LOG2E)).astype(BF16)
        dkx_ref[...] = jnp.where(first, dk1, dk0)
        dv_ref[...] = jnp.where(first, dv0, dv1).astype(BF16)

        @pl.when(j == nq - 1)
        def _():
            dq_ref[...] = (jnp.where(first, dq_scr[0], dq_scr[1]) * Q_SCALE).astype(BF16)
            dqx_ref[...] = jnp.where(first, dq_scr[1], dq_scr[0])

        @pl.when((pl.program_id(0) == n_pairs - 1) & (j == nq - 1))
        def _():
            scatter.wait()

    resident = lambda: pl.BlockSpec((2, t, PAIR), lambda p, j: (p, 0, 0), pipeline_mode=pl.Buffered(1))
    kv_in = lambda: pl.BlockSpec((2, tk, PAIR), lambda p, j: (p, j, 0))
    whole = lambda: pl.BlockSpec((t, PAIR), lambda p, j: (0, p))
    tile = lambda: pl.BlockSpec((tk, PAIR), lambda p, j: (j, p))
    b16 = jax.ShapeDtypeStruct((t, ATTN_W), BF16)
    f32 = jax.ShapeDtypeStruct((t, ATTN_W), F32)
    res = pl.pallas_call(
        body, name="attn_bwd", grid=(n_pairs, nq),
        in_specs=[resident(), resident(), kv_in(), kv_in()] + [HBM_SPEC] * n_g,
        out_specs=[whole(), whole(), tile(), tile(), tile()] + [HBM_SPEC] * n_g,
        out_shape=[b16, f32, b16, f32, b16] + [jax.ShapeDtypeStruct(g.shape, g.dtype) for g in gblocks],
        scratch_shapes=[pltpu.VMEM((2, t, PAIR), F32)] + _Exchange.scratch(n_g),
        compiler_params=_cp(("arbitrary", "arbitrary")),
    )(q_bwd, do_aug, k_aug, v_aug, *gblocks)
    return res[:5], res[5:]


def _fgate_bwd(dqx, dkx, sgate, *, tm=256):
    t = sgate.shape[0]
    tm = min(tm, t)
    nsteps = t // tm

    def body(dq_ref, dk_ref, sg_ref, df_ref, dbf_ref, carry):
        @pl.when(pl.program_id(0) == 0)
        def _():
            carry[...] = jnp.zeros_like(carry)
            dbf_ref[...] = jnp.zeros_like(dbf_ref)
        lane = lax.broadcasted_iota(jnp.int32, (ATTN_W, F_PAD), 0)
        head = lax.broadcasted_iota(jnp.int32, (ATTN_W, F_PAD), 1)
        aux = (head // 2) * PAIR + HEAD_DIM * (1 - head % 2)
        valid = head < N_HEADS
        pick_r = (valid & (lane == aux + AUX_ROWSUM)).astype(F32)
        pick_c = (valid & (lane == aux + AUX_BIAS)).astype(F32)
        hp = lax.Precision.HIGHEST
        dcum = (jnp.dot(dq_ref[...], pick_r, preferred_element_type=F32, precision=hp)
                + jnp.dot(dk_ref[...], pick_c, preferred_element_type=F32, precision=hp))
        r = lax.broadcasted_iota(jnp.int32, (tm, tm), 0)
        c = lax.broadcasted_iota(jnp.int32, (tm, tm), 1)
        tri = (c >= r).astype(F32)
        rc = jnp.dot(tri, dcum, preferred_element_type=F32, precision=hp) + carry[...]
        carry[...] = rc[0:1, :]
        df = rc * sg_ref[...]
        df_ref[...] = df.astype(BF16)
        dbf_ref[...] += _rows8(df)

    rev = lambda i: nsteps - 1 - i
    return pl.pallas_call(
        body, name="fgate_bwd", grid=(nsteps,),
        in_specs=[pl.BlockSpec((tm, ATTN_W), lambda i: (rev(i), 0)), pl.BlockSpec((tm, ATTN_W), lambda i: (rev(i), 0)),
                  pl.BlockSpec((tm, F_PAD), lambda i: (rev(i), 0))],
        out_specs=[pl.BlockSpec((tm, F_PAD), lambda i: (rev(i), 0)), pl.BlockSpec((8, F_PAD), lambda i: (0, 0))],
        out_shape=[jax.ShapeDtypeStruct((t, F_PAD), BF16), jax.ShapeDtypeStruct((8, F_PAD), F32)],
        scratch_shapes=[pltpu.VMEM((1, F_PAD), F32)],
        compiler_params=_cp(("arbitrary",)),
    )(dqx, dkx, sgate)


def _inproj_bwd(dq, dk, dv, df, dbcx, dh, x, g, win_p, *, tm=512):
    t, d = x.shape
    tm = min(tm, t)
    n_qkv = 3 * ATTN_W

    def body(dq_ref, dk_ref, dv_ref, df_ref, db_ref, dh_ref, x_ref, g_ref, w_ref, gx_ref, dg_ref):
        @pl.when(pl.program_id(0) == 0)
        def _():
            dg_ref[...] = jnp.zeros_like(dg_ref)
        dn = _dot_nt(df_ref[...], w_ref[:, n_qkv:n_qkv + F_PAD])
        for k, r in enumerate((dq_ref, dk_ref, dv_ref)):
            dn = dn + _dot_nt(r[...], w_ref[:, k * ATTN_W:(k + 1) * ATTN_W])
        for k in range(3):
            c0 = n_qkv + F_PAD + k * CONV_CH
            dn = dn + _dot_nt(db_ref[:, k * CONV_CH:(k + 1) * CONV_CH], w_ref[:, c0:c0 + CONV_CH])
        dx, dg = _norm_bwd(dn, x_ref[...], g_ref[...])
        gx_ref[...] = dh_ref[...] + dx
        dg_ref[...] += dg

    row = lambda n_: pl.BlockSpec((tm, n_), lambda i: (i, 0))
    return pl.pallas_call(
        body, name="inproj_bwd", grid=(t // tm,),
        in_specs=[row(ATTN_W), row(ATTN_W), row(ATTN_W), row(F_PAD), row(3 * CONV_CH), row(d), row(d),
                  _const_spec((1, d)), _const_spec(win_p.shape)],
        out_specs=[row(d), pl.BlockSpec((8, d), lambda i: (0, 0))],
        out_shape=[jax.ShapeDtypeStruct((t, d), F32), jax.ShapeDtypeStruct((8, d), F32)],
        compiler_params=_cp(("arbitrary",)),
    )(dq, dk, dv, df, dbcx, dh, x, g, win_p)


LATE = ("w_out_0", "w_up_0", "w_down_0", "pool_w_1", "w_up_1", "w_down_1")


def _local_step(x, target, gains, b_f, conv_w, pool_scale, win_p, shards):
    d = x.shape[1]
    n0, qkv, flog, bcx = _norm_inproj(x, gains["mix0"], win_p)
    q_aug, k_aug, v_aug, sgate = _fgate_prep(flog, b_f, qkv)
    att, q_bwd, gathered = _attn_fwd(q_aug, k_aug, v_aug, [shards[n] for n in LATE])
    g = dict(zip(LATE, gathered))
    wout = g["w_out_0"].reshape(d, d)
    wup0, wup1 = g["w_up_0"], g["w_up_1"]
    wdown0, wdown1 = g["w_down_0"].reshape(-1, d), g["w_down_1"].reshape(-1, d)
    n_grp = len(POOL_WINDOWS)
    cg = d // n_grp
    poolw = g["pool_w_1"].reshape(N_DEV, n_grp, cg // N_DEV, cg).transpose(1, 0, 2, 3).reshape(n_grp, cg, cg)
    cv = _conv_fwd(bcx, conv_w)
    h1 = _outproj(att, cv, x, wout)
    h2, n1, a0, z0 = _mlp_fwd(h1, gains["ffn0"], wup0, wdown0, name="mlp_fwd0")
    h3, pooled = _pool_fwd(h2, gains["mix1"], poolw, pool_scale)
    h4, n3, a1, z1 = _mlp_fwd(h3, gains["ffn1"], wup1, wdown1, name="mlp_fwd1")
    loss, dh4, dg_final = _loss_bwd(h4, gains["final"], target)
    f = a1.shape[1]
    fb = f // N_DEV
    dh3, da1, dg_ffn1 = _mlp_bwd(dh4, h3, a1, gains["ffn1"], wup1, wdown1, name="mlp_bwd1")
    dwdown1 = _mm_tn(z1, dh4, name="dwdown1", ta=1024, tb=1024, tt=1024, out_dtype=BF16)
    dwup1 = _mm_tn(n3, da1, name="dwup1", ta=d, tb=fb, tt=2048, blocked_out=True, out_dtype=BF16)
    dh2, dpoolw, dscale, dg_mix1 = _pool_bwd(dh3, h2, pooled, gains["mix1"], poolw, pool_scale)
    dh1, da0, dg_ffn0 = _mlp_bwd(dh2, h1, a0, gains["ffn0"], wup0, wdown0, name="mlp_bwd0")
    dwdown0 = _mm_tn(z0, dh2, name="dwdown0", ta=1024, tb=1024, tt=1024, out_dtype=BF16)
    dwup0 = _mm_tn(n1, da0, name="dwup0", ta=d, tb=fb, tt=2048, blocked_out=True, out_dtype=BF16)
    do_aug, dcv = _outproj_bwd(dh1, att, wout)
    dwout = jnp.concatenate([_mm_tn(att, dh1, name="dwout_att", ta=512, tb=1024, tt=2048, out_dtype=BF16),
                             _mm_tn(cv, dh1, name="dwout_conv", ta=512, tb=1024, tt=2048, out_dtype=BF16)], axis=0)
    dbcx, dconvw = _conv_bwd(bcx, dcv, conv_w)
    gblocks = {
        "w_out_0": dwout.reshape(N_DEV, d // N_DEV, d), "w_up_0": dwup0, "w_up_1": dwup1,
        "w_down_0": dwdown0.reshape(N_DEV, -1, d), "w_down_1": dwdown1.reshape(N_DEV, -1, d),
        "pool_w_1": dpoolw.astype(BF16).reshape(n_grp, N_DEV, cg // N_DEV, cg).transpose(1, 0, 2, 3).reshape(
            N_DEV, n_grp * (cg // N_DEV), cg),
    }
    (dq, dqx, dk, dkx, dv), landed = _attn_bwd(q_bwd, do_aug, k_aug, v_aug, [gblocks[n] for n in LATE])
    df, dbf = _fgate_bwd(dqx, dkx, sgate)
    grad_x, dg_mix0 = _inproj_bwd(dq, dk, dv, df, dbcx, dh1, x, gains["mix0"], win_p)
    dwin_p = jnp.concatenate(
        [_mm_tn(n0, dq, name="dwin_q", ta=d, tb=512, tt=2048), _mm_tn(n0, dk, name="dwin_k", ta=d, tb=512, tt=2048),
         _mm_tn(n0, dv, name="dwin_v", ta=d, tb=512, tt=2048), _mm_tn(n0, df, name="dwin_f", ta=d, tb=128, tt=2048),
         _mm_tn(n0, dbcx, name="dwin_bcx", ta=d, tb=512, tt=2048)], axis=1)
    small = dict(mix0=dg_mix0, ffn0=dg_ffn0, mix1=dg_mix1, pool_scale=dscale, ffn1=dg_ffn1, final=dg_final,
                 b_f=dbf, conv_w=dconvw)
    return loss, grad_x, dwin_p, dict(zip(LATE, landed)), small


def _mesh_places():
    x, y, c = lax.axis_index("x"), lax.axis_index("y"), lax.axis_index("c")
    chips = [(1 - x, y), (x, 1 - y), (1 - x, 1 - y)]
    return (x, y, c), (x, y, 1 - c), chips


def _all_gather(shards):
    n = len(shards)

    def body(*refs):
        ins, outs = refs[:n], refs[n:2 * n]
        send_sems, recv_sems, local_sems = refs[2 * n:]
        me, sib, chips = _mesh_places()
        c = me[2]

        def copy(ai, k, block, to, src=None):
            dst = outs[ai].at[_slot(*block)]
            return pltpu.make_async_remote_copy(
                src_ref=dst if src is None else src, dst_ref=dst, send_sem=send_sems.at[7 * ai + k],
                recv_sem=recv_sems.at[7 * ai + k], device_id=to, device_id_type=MESH)

        mine = [pltpu.make_async_copy(ins[ai], outs[ai].at[_slot(*me)], local_sems.at[ai]) for ai in range(n)]
        for cp in mine:
            cp.start()
        first = []
        for ai in range(n):
            first.append(copy(ai, 0, me, sib, src=ins[ai]))
            first += [copy(ai, 1 + j, me, (*chip, c), src=ins[ai]) for j, chip in enumerate(chips)]
        for cp in first:
            cp.start()
        passed = []
        for ai in range(n):
            for j, chip in enumerate(chips):
                copy(ai, 1 + j, (*chip, c), me).wait_recv()
                cp = copy(ai, 4 + j, (*chip, c), sib)
                cp.start()
                passed.append(cp)
        for ai in range(n):
            copy(ai, 0, sib, me).wait_recv()
            for j, chip in enumerate(chips):
                copy(ai, 4 + j, (*chip, 1 - c), me).wait_recv()
        for cp in first + passed:
            cp.wait_send()
        for cp in mine:
            cp.wait()

    return pl.pallas_call(
        body, name="all_gather",
        in_specs=[HBM_SPEC] * n, out_specs=[HBM_SPEC] * n,
        out_shape=[jax.ShapeDtypeStruct((N_DEV,) + s.shape, s.dtype) for s in shards],
        scratch_shapes=[pltpu.SemaphoreType.DMA((7 * n,)), pltpu.SemaphoreType.DMA((7 * n,)),
                        pltpu.SemaphoreType.DMA((n,))],
    )(*shards)


def _reduce_scatter(grads):
    n = len(grads)
    shapes = [g.shape[1:] for g in grads]

    def body(*refs):
        gs = refs[:n]
        outs, land_a, land_b = refs[n:2 * n], refs[2 * n:3 * n], refs[3 * n:4 * n]
        sendbufs = refs[4 * n:5 * n]
        sa_send, sa_recv, sb_send, sb_recv, lsem = refs[5 * n:]
        me, sib, chips = _mesh_places()
        x, y, c = me

        def copy_a(ai, q):
            qx, qy = q // 2, q % 2
            return pltpu.make_async_remote_copy(
                src_ref=gs[ai].at[_slot(qx, qy, 1 - c)], dst_ref=land_a[ai].at[q], send_sem=sa_send.at[4 * ai + q],
                recv_sem=sa_recv.at[4 * ai + q], device_id=sib, device_id_type=MESH)

        def copy_b(ai, j):
            return pltpu.make_async_remote_copy(
                src_ref=sendbufs[ai].at[j], dst_ref=land_b[ai].at[j], send_sem=sb_send.at[3 * ai + j],
                recv_sem=sb_recv.at[3 * ai + j], device_id=(*chips[j], c), device_id_type=MESH)

        def load(src, dst):
            cp = pltpu.make_async_copy(src, dst, lsem)
            cp.start()
            cp.wait()

        for ai in range(n):
            for q in range(4):
                copy_a(ai, q).start()
        for ai in range(n):
            def stage(buf_a, buf_b, ai=ai):
                for j, (qx, qy) in enumerate(chips):
                    q = 2 * qx + qy
                    pltpu.make_async_remote_copy(
                        src_ref=land_a[ai].at[0], dst_ref=land_a[ai].at[0], send_sem=sa_send.at[0],
                        recv_sem=sa_recv.at[4 * ai + q], device_id=sib, device_id_type=MESH).wait_recv()
                    load(gs[ai].at[_slot(qx, qy, c)], buf_a)
                    load(land_a[ai].at[q], buf_b)
                    sendbufs[ai][j] = (buf_a[...] + buf_b[...]).astype(BF16)
                    copy_b(ai, j).start()
            pl.run_scoped(stage, pltpu.VMEM(shapes[ai], F32), pltpu.VMEM(shapes[ai], F32))
        for ai in range(n):
            def final(buf_a, buf_b, buf_c, ai=ai):
                q = 2 * x + y
                pltpu.make_async_remote_copy(
                    src_ref=land_a[ai].at[0], dst_ref=land_a[ai].at[0], send_sem=sa_send.at[0],
                    recv_sem=sa_recv.at[4 * ai + q], device_id=sib, device_id_type=MESH).wait_recv()
                load(gs[ai].at[_slot(x, y, c)], buf_a)
                load(land_a[ai].at[q], buf_b)
                acc = buf_a[...] + buf_b[...]
                for j in range(3):
                    copy_b(ai, j).wait_recv()
                    load(land_b[ai].at[j], buf_c)
                    acc = acc + buf_c[...].astype(F32)
                buf_a[...] = acc
                load(buf_a, outs[ai])
            pl.run_scoped(final, pltpu.VMEM(shapes[ai], F32), pltpu.VMEM(shapes[ai], F32), pltpu.VMEM(shapes[ai], BF16))
        for ai in range(n):
            for q in range(4):
                copy_a(ai, q).wait_send()
            for j in range(3):
                copy_b(ai, j).wait_send()

    res = pl.pallas_call(
        body, name="reduce_scatter",
        in_specs=[HBM_SPEC] * n, out_specs=[HBM_SPEC] * (3 * n),
        out_shape=([jax.ShapeDtypeStruct(s, F32) for s in shapes]
                   + [jax.ShapeDtypeStruct((4,) + s, F32) for s in shapes]
                   + [jax.ShapeDtypeStruct((3,) + s, BF16) for s in shapes]),
        scratch_shapes=([pltpu.VMEM((3,) + s, BF16) for s in shapes]
                        + [pltpu.SemaphoreType.DMA((4 * n,)), pltpu.SemaphoreType.DMA((4 * n,)),
                           pltpu.SemaphoreType.DMA((3 * n,)), pltpu.SemaphoreType.DMA((3 * n,)),
                           pltpu.SemaphoreType.DMA(())]),
        compiler_params=pltpu.CompilerParams(vmem_limit_bytes=VMEM_LIMIT),
    )(*grads)
    return res[:n]


SMALL_ROWS = 16


def _small_allreduce(parts):
    n, _, w = parts.shape
    assert n <= SMALL_ROWS

    def body(p_ref, o_ref, gath, send_sems, recv_sems):
        x, y, c = lax.axis_index("x"), lax.axis_index("y"), lax.axis_index("c")
        my = _slot(x, y, c)
        rows = [jnp.sum(p_ref[i], axis=0, keepdims=True) for i in range(n)]
        rows.append(jnp.zeros((SMALL_ROWS - n, w), F32))
        gath[my] = jnp.concatenate(rows, axis=0)
        copies = []
        for k in range(1, N_DEV):
            px, py, pc = x ^ (k >> 2), y ^ ((k >> 1) & 1), c ^ (k & 1)
            cp = pltpu.make_async_remote_copy(
                src_ref=gath.at[my], dst_ref=gath.at[my], send_sem=send_sems.at[k - 1], recv_sem=recv_sems.at[k - 1],
                device_id=(px, py, pc), device_id_type=MESH)
            cp.start()
            copies.append(cp)
        for cp in copies:
            cp.wait()
        acc = gath[0]
        for d in range(1, N_DEV):
            acc = acc + gath[d]
        o_ref[...] = acc

    return pl.pallas_call(
        body, name="small_allreduce",
        in_specs=[VMEM_SPEC], out_specs=VMEM_SPEC,
        out_shape=jax.ShapeDtypeStruct((SMALL_ROWS, w), F32),
        scratch_shapes=[pltpu.VMEM((N_DEV, SMALL_ROWS, w), F32), pltpu.SemaphoreType.DMA((N_DEV - 1,)),
                        pltpu.SemaphoreType.DMA((N_DEV - 1,))],
    )(parts)


def _adamw(g, w, m, v, *, name, tm=256):
    r, c = g.shape
    tm = tm if r % tm == 0 else r
    bc1 = 1.0 - ADAM_B1 ** ADAM_STEP
    bc2 = 1.0 - ADAM_B2 ** ADAM_STEP

    def body(g_ref, w_ref, m_ref, v_ref, d_ref, nm_ref, nv_ref):
        gv = g_ref[...]
        nm = ADAM_B1 * m_ref[...] + (1.0 - ADAM_B1) * gv
        nv = ADAM_B2 * v_ref[...] + (1.0 - ADAM_B2) * jnp.square(gv)
        nm_ref[...] = nm
        nv_ref[...] = nv
        d_ref[...] = -ADAM_LR * ((nm / bc1) / (jnp.sqrt(nv / bc2) + ADAM_EPS) + ADAM_WD * w_ref[...])

    blk = pl.BlockSpec((tm, c), lambda i: (i, 0))
    shp = jax.ShapeDtypeStruct((r, c), F32)
    return pl.pallas_call(
        body, name=name, grid=(r // tm,), in_specs=[blk] * 4, out_specs=[blk] * 3, out_shape=[shp] * 3,
        compiler_params=_cp(("parallel",)),
    )(g, w, m, v)


def _adamw_sum(parts, w, m, v, *, name, tm=128):
    _, r, c = parts.shape
    tm = tm if r % tm == 0 else r
    bc1 = 1.0 - ADAM_B1 ** ADAM_STEP
    bc2 = 1.0 - ADAM_B2 ** ADAM_STEP

    def body(p_ref, w_ref, m_ref, v_ref, g_ref, d_ref, nm_ref, nv_ref):
        gv = p_ref[0].astype(F32)
        for k in range(1, N_DEV):
            gv = gv + p_ref[k].astype(F32)
        g_ref[...] = gv
        nm = ADAM_B1 * m_ref[...] + (1.0 - ADAM_B1) * gv
        nv = ADAM_B2 * v_ref[...] + (1.0 - ADAM_B2) * jnp.square(gv)
        nm_ref[...] = nm
        nv_ref[...] = nv
        d_ref[...] = -ADAM_LR * ((nm / bc1) / (jnp.sqrt(nv / bc2) + ADAM_EPS) + ADAM_WD * w_ref[...])

    blk = pl.BlockSpec((tm, c), lambda i: (i, 0))
    shp = jax.ShapeDtypeStruct((r, c), F32)
    return pl.pallas_call(
        body, name=name, grid=(r // tm,), in_specs=[pl.BlockSpec((N_DEV, tm, c), lambda i: (0, i, 0))] + [blk] * 3,
        out_specs=[blk] * 4, out_shape=[shp] * 4, compiler_params=_cp(("parallel",)),
    )(parts, w, m, v)


BIG = ("w_in_0", "w_out_0", "w_up_0", "w_down_0", "pool_w_1", "w_up_1", "w_down_1")
SMALL = ("norm_mix_0", "norm_ffn_0", "norm_mix_1", "pool_scale_1", "norm_ffn_1", "final_norm", "b_f_0", "conv_w_0")
WEIGHTS = ("norm_mix_0", "w_in_0", "b_f_0", "conv_w_0", "w_out_0", "norm_ffn_0", "w_up_0", "w_down_0", "norm_mix_1",
           "pool_w_1", "pool_scale_1", "norm_ffn_1", "w_up_1", "w_down_1", "final_norm")


def _pad_to(a, rows, cols):
    return jnp.pad(a, ((0, rows - a.shape[0]), (0, cols - a.shape[1])))


def _pack_small(p, width):
    rows = [p[n].reshape(1, -1) for n in SMALL[:6]]
    rows.append(_pad_to(p["b_f_0"].reshape(1, -1), 1, width))
    rows.append(_pad_to(p["conv_w_0"], 3, width))
    return _pad_to(jnp.concatenate(rows, axis=0), SMALL_ROWS, width)


def _unpack_small(a, like):
    out = {n: a[i] for i, n in enumerate(SMALL[:6])}
    out["b_f_0"] = a[6, :like["b_f_0"].shape[0]]
    out["conv_w_0"] = a[7:10, :like["conv_w_0"].shape[1]]
    return out


def kernel(x, norm_mix_0, w_in_0, b_f_0, conv_w_0, w_out_0, norm_ffn_0, w_up_0, w_down_0, norm_mix_1, pool_w_1, pool_scale_1, norm_ffn_1, w_up_1, w_down_1, final_norm, loss_target, m_norm_mix_0, m_w_in_0, m_b_f_0, m_conv_w_0, m_w_out_0, m_norm_ffn_0, m_w_up_0, m_w_down_0, m_norm_mix_1, m_pool_w_1, m_pool_scale_1, m_norm_ffn_1, m_w_up_1, m_w_down_1, m_final_norm, v_norm_mix_0, v_w_in_0, v_b_f_0, v_conv_w_0, v_w_out_0, v_norm_ffn_0, v_w_up_0, v_w_down_0, v_norm_mix_1, v_pool_w_1, v_pool_scale_1, v_norm_ffn_1, v_w_up_1, v_w_down_1, v_final_norm):
    w = dict(norm_mix_0=norm_mix_0, w_in_0=w_in_0, b_f_0=b_f_0, conv_w_0=conv_w_0, w_out_0=w_out_0,
             norm_ffn_0=norm_ffn_0, w_up_0=w_up_0, w_down_0=w_down_0, norm_mix_1=norm_mix_1, pool_w_1=pool_w_1,
             pool_scale_1=pool_scale_1, norm_ffn_1=norm_ffn_1, w_up_1=w_up_1, w_down_1=w_down_1, final_norm=final_norm)
    m = dict(norm_mix_0=m_norm_mix_0, w_in_0=m_w_in_0, b_f_0=m_b_f_0, conv_w_0=m_conv_w_0, w_out_0=m_w_out_0,
             norm_ffn_0=m_norm_ffn_0, w_up_0=m_w_up_0, w_down_0=m_w_down_0, norm_mix_1=m_norm_mix_1,
             pool_w_1=m_pool_w_1, pool_scale_1=m_pool_scale_1, norm_ffn_1=m_norm_ffn_1, w_up_1=m_w_up_1,
             w_down_1=m_w_down_1, final_norm=m_final_norm)
    v = dict(norm_mix_0=v_norm_mix_0, w_in_0=v_w_in_0, b_f_0=v_b_f_0, conv_w_0=v_conv_w_0, w_out_0=v_w_out_0,
             norm_ffn_0=v_norm_ffn_0, w_up_0=v_w_up_0, w_down_0=v_w_down_0, norm_mix_1=v_norm_mix_1,
             pool_w_1=v_pool_w_1, pool_scale_1=v_pool_scale_1, norm_ffn_1=v_norm_ffn_1, w_up_1=v_w_up_1,
             w_down_1=v_w_down_1, final_norm=v_final_norm)
    d = x.shape[-1]
    n_in = w_in_0.shape[1] * N_DEV
    n_qkv = 3 * ATTN_W
    pool_g, pool_rows, pool_c = pool_w_1.shape

    def shard2d(p):
        return {n: (p[n].reshape(pool_g * pool_rows, pool_c) if n == "pool_w_1" else p[n]) for n in BIG}
    w2, m2, v2 = shard2d(w), shard2d(m), shard2d(v)

    conv_cols = conv_w_0.shape[1]
    win_g8, conv_g8 = _all_gather([w_in_0.astype(BF16), _pad_to(conv_w_0, 8, 128)])
    conv_full = conv_g8[:, :, :conv_cols].transpose(1, 0, 2).reshape(8, N_DEV * conv_cols)
    win = win_g8.transpose(1, 0, 2).reshape(d, n_in)
    win_p = jnp.concatenate([win[:, :n_qkv], _pad_to(win[:, n_qkv:n_qkv + N_HEADS], d, F_PAD),
                             win[:, n_qkv + N_HEADS:]], axis=1)

    gains = dict(mix0=norm_mix_0.reshape(1, d), ffn0=norm_ffn_0.reshape(1, d), mix1=norm_mix_1.reshape(1, d),
                 ffn1=norm_ffn_1.reshape(1, d), final=final_norm.reshape(1, d))
    dev = _slot(lax.axis_index("x"), lax.axis_index("y"), lax.axis_index("c"))
    loss8, grad_x, dwin_p, landed, small = _local_step(
        x[0], loss_target[0], gains, _pad_to(b_f_0.reshape(1, -1), 1, F_PAD), conv_full, pool_scale_1.reshape(1, d),
        win_p, {n: w2[n].astype(BF16) for n in LATE})
    loss = lax.psum(loss8[0, 0], ("x", "y", "c"))

    dwin = jnp.concatenate([dwin_p[:, :n_qkv + N_HEADS], dwin_p[:, n_qkv + F_PAD:]], axis=1)
    g_win, = _reduce_scatter([dwin.reshape(d, N_DEV, n_in // N_DEV).transpose(1, 0, 2)])

    parts = jnp.concatenate(
        [small[k][None] for k in ("mix0", "ffn0", "mix1", "pool_scale", "ffn1", "final")]
        + [_pad_to(small["b_f"], 8, d)[None], jnp.pad(small["conv_w"], ((0, 0), (0, 0), (0, d - CONV_CH)))], axis=0)
    tot = _small_allreduce(parts)
    conv_g = lax.dynamic_slice(tot, (7, dev * conv_cols), (3, conv_cols))
    gs = tot.at[7:10].set(_pad_to(conv_g, 3, d))

    grads, deltas, new_m, new_v = {}, {}, {}, {}
    for n in BIG:
        if n in LATE:
            gr, dl, nm, nv = _adamw_sum(landed[n], w2[n], m2[n], v2[n], name="adamw_" + n)
        else:
            gr = g_win
            dl, nm, nv = _adamw(gr, w2[n], m2[n], v2[n], name="adamw_" + n)
        for dst, val in ((grads, gr), (deltas, dl), (new_m, nm), (new_v, nv)):
            dst[n] = val.reshape(w[n].shape)
    dl, nm, nv = _adamw(gs, _pack_small(w, d), _pack_small(m, d), _pack_small(v, d), name="adamw_small")
    for dst, val in ((grads, gs), (deltas, dl), (new_m, nm), (new_v, nv)):
        dst.update(_unpack_small(val, w))
    return (loss, grad_x[None], *[grads[n] for n in WEIGHTS], *[deltas[n] for n in WEIGHTS],
            *[new_m[n] for n in WEIGHTS], *[new_v[n] for n in WEIGHTS])
```

```python
import functools

import jax
import jax.numpy as jnp
from jax import lax
from jax.experimental import pallas as pl
from jax.experimental.pallas import tpu as pltpu

F32 = jnp.float32
BF16 = jnp.bfloat16

N_DEV = 8
N_HEADS = 8
HEAD_DIM = 64
PAIR = 2 * HEAD_DIM
ATTN_W = N_HEADS * HEAD_DIM
CONV_CH = 512
F_PAD = 128
POOL_WINDOWS = (2, 4, 8, 16)
POOL_HALO = 16
CONV_HALO = 8
RMS_EPS = 1e-6
Q_SCALE = HEAD_DIM ** -0.5
LOG2E = 1.4426950408889634
NEG = -1e30
AUX_BIAS = 0
AUX_LSE = 3
AUX_ROWSUM = 6
ADAM_LR, ADAM_B1, ADAM_B2, ADAM_EPS, ADAM_WD, ADAM_STEP = 0.001, 0.9, 0.999, 1e-08, 0.01, 10
MESH = pl.DeviceIdType.MESH
VMEM_LIMIT = 56 * 2**20


def _cp(sem=None, vmem=VMEM_LIMIT, **kw):
    return pltpu.CompilerParams(dimension_semantics=sem, vmem_limit_bytes=vmem, **kw)


def _dot(a, b):
    return jnp.dot(a, b, preferred_element_type=F32)


def _dot_nt(a, b):
    return lax.dot_general(a, b, (((1,), (1,)), ((), ())), preferred_element_type=F32)


def _dot_tn(a, b):
    return lax.dot_general(a, b, (((0,), (0,)), ((), ())), preferred_element_type=F32)


def _rstd(h):
    return lax.rsqrt(jnp.mean(h * h, axis=-1, keepdims=True) + RMS_EPS)


def _rows8(x):
    r, n = x.shape
    return jnp.sum(x.reshape(r // 8, 8, n), axis=0)


def _norm_bwd(dn, h, g):
    r = _rstd(h)
    xhat = h * r
    dy = dn * g
    dh = r * (dy - xhat * jnp.mean(dy * xhat, axis=-1, keepdims=True))
    return dh, _rows8(dn * xhat)


def _const_spec(shape):
    nd = len(shape)
    return pl.BlockSpec(shape, lambda *_: (0,) * nd, pipeline_mode=pl.Buffered(1))


HBM_SPEC = pl.BlockSpec(memory_space=pltpu.HBM)
VMEM_SPEC = pl.BlockSpec(memory_space=pltpu.VMEM)


def _slot(px, py, pc):
    return 4 * px + 2 * py + pc


class _Exchange:
    def __init__(self, srcs, dsts, send_sems, recv_sems, local_sems, gather):
        x, y, c = lax.axis_index("x"), lax.axis_index("y"), lax.axis_index("c")
        me = _slot(x, y, c)
        self.copies = []
        for a, (src, dst) in enumerate(zip(srcs, dsts)):
            self.copies.append(pltpu.make_async_copy(src if gather else src.at[me], dst.at[me], local_sems.at[a]))
            for k in range(1, N_DEV):
                px, py, pc = x ^ (k >> 2), y ^ ((k >> 1) & 1), c ^ (k & 1)
                self.copies.append(pltpu.make_async_remote_copy(
                    src_ref=src if gather else src.at[_slot(px, py, pc)], dst_ref=dst.at[me],
                    send_sem=send_sems.at[(N_DEV - 1) * a + k - 1], recv_sem=recv_sems.at[(N_DEV - 1) * a + k - 1],
                    device_id=(px, py, pc), device_id_type=MESH))

    def start(self):
        for cp in self.copies:
            cp.start()

    def wait(self):
        for cp in self.copies:
            cp.wait()

    @staticmethod
    def scratch(n):
        return [pltpu.SemaphoreType.DMA(((N_DEV - 1) * n,)), pltpu.SemaphoreType.DMA(((N_DEV - 1) * n,)),
                pltpu.SemaphoreType.DMA((n,))]


def _norm_inproj(x, g, win_p, *, tm=512):
    t, d = x.shape
    n_all = win_p.shape[1]
    n_qkv = 3 * ATTN_W
    n_bcx = 3 * CONV_CH
    assert n_all == n_qkv + F_PAD + n_bcx
    tm = min(tm, t)

    def body(x_ref, g_ref, w_ref, n_ref, qkv_ref, f_ref, bcx_ref):
        h = x_ref[...]
        n = (h * _rstd(h) * g_ref[...]).astype(BF16)
        n_ref[...] = n
        for c0 in range(0, n_qkv, 512):
            acc = _dot(n, w_ref[:, c0:c0 + 512])
            if c0 < ATTN_W:
                acc = acc * (Q_SCALE * LOG2E)
            qkv_ref[:, c0:c0 + 512] = acc.astype(BF16)
        f_ref[...] = _dot(n, w_ref[:, n_qkv:n_qkv + F_PAD])
        for c0 in range(0, n_bcx, 512):
            bcx_ref[:, c0:c0 + 512] = _dot(n, w_ref[:, n_qkv + F_PAD + c0:n_qkv + F_PAD + c0 + 512])

    return pl.pallas_call(
        body, name="norm_inproj", grid=(t // tm,),
        in_specs=[pl.BlockSpec((tm, d), lambda i: (i, 0)), _const_spec((1, d)), _const_spec((d, n_all))],
        out_specs=[pl.BlockSpec((tm, d), lambda i: (i, 0)), pl.BlockSpec((tm, n_qkv), lambda i: (i, 0)),
                   pl.BlockSpec((tm, F_PAD), lambda i: (i, 0)), pl.BlockSpec((tm, n_bcx), lambda i: (i, 0))],
        out_shape=[jax.ShapeDtypeStruct((t, d), BF16), jax.ShapeDtypeStruct((t, n_qkv), BF16),
                   jax.ShapeDtypeStruct((t, F_PAD), F32), jax.ShapeDtypeStruct((t, n_bcx), F32)],
        compiler_params=_cp(("parallel",)),
    )(x, g, win_p)


def _head_lanes(h):
    lane = lax.broadcasted_iota(jnp.int32, (1, PAIR), 1)
    hh = h % 2
    return lane, lane // HEAD_DIM == hh, HEAD_DIM * (1 - hh)


def _pieces(col):
    hi = col.astype(BF16).astype(F32)
    r1 = col - hi
    mid = r1.astype(BF16).astype(F32)
    lo = (r1 - mid).astype(BF16).astype(F32)
    return hi, mid, lo


def _put_pieces(lane, first, col, other):
    hi, mid, lo = _pieces(col)
    return jnp.where(lane == first, hi, jnp.where(lane == first + 1, mid, jnp.where(lane == first + 2, lo, other)))


def _fgate_prep(flog, b_f, qkv, *, tm=256):
    t = flog.shape[0]
    tm = min(tm, t)

    def body(f_ref, b_ref, qkv_ref, qa_ref, ka_ref, va_ref, sg_ref, carry):
        @pl.when(pl.program_id(0) == 0)
        def _():
            carry[...] = jnp.zeros_like(carry)
        z = f_ref[...] + b_ref[...]
        e = jnp.exp(-jnp.abs(z))
        logf = jnp.minimum(z, 0.0) - jnp.log(1.0 + e)
        sg_ref[...] = jnp.where(z >= 0, e, 1.0) / (1.0 + e)
        r = lax.broadcasted_iota(jnp.int32, (tm, tm), 0)
        c = lax.broadcasted_iota(jnp.int32, (tm, tm), 1)
        tri = (c <= r).astype(F32)
        cs = jnp.dot(tri, logf, preferred_element_type=F32, precision=lax.Precision.HIGHEST) + carry[...]
        carry[...] = cs[tm - 1:tm, :]
        cs2 = cs * LOG2E
        for h in range(N_HEADS):
            lane, head, aux = _head_lanes(h)
            p0 = (h // 2) * PAIR
            ones = ((lane >= aux + AUX_LSE) & (lane <= aux + AUX_ROWSUM)).astype(F32)
            bias = (lane >= aux + AUX_BIAS) & (lane < aux + AUX_BIAS + 3)
            k_aux = _put_pieces(lane, aux + AUX_BIAS, cs2[:, h:h + 1], ones)
            qa_ref[h] = jnp.where(head, qkv_ref[:, p0:p0 + PAIR], jnp.where(bias, -1.0, 0.0).astype(BF16))
            ka_ref[h] = jnp.where(head, qkv_ref[:, ATTN_W + p0:ATTN_W + p0 + PAIR], k_aux.astype(BF16))
            va_ref[h] = jnp.where(head, qkv_ref[:, 2 * ATTN_W + p0:2 * ATTN_W + p0 + PAIR],
                                  jnp.where(bias, 1.0, 0.0).astype(BF16))

    aug = lambda: pl.BlockSpec((N_HEADS, tm, PAIR), lambda i: (0, i, 0))
    aug_shape = jax.ShapeDtypeStruct((N_HEADS, t, PAIR), BF16)
    return pl.pallas_call(
        body, name="fgate_prep", grid=(t // tm,),
        in_specs=[pl.BlockSpec((tm, F_PAD), lambda i: (i, 0)), _const_spec((1, F_PAD)),
                  pl.BlockSpec((tm, 3 * ATTN_W), lambda i: (i, 0))],
        out_specs=[aug(), aug(), aug(), pl.BlockSpec((tm, F_PAD), lambda i: (i, 0))],
        out_shape=[aug_shape, aug_shape, aug_shape, jax.ShapeDtypeStruct((t, F_PAD), F32)],
        scratch_shapes=[pltpu.VMEM((1, F_PAD), F32)],
        compiler_params=_cp(("arbitrary",)),
    )(flog, b_f, qkv)


def _attn_fwd(q_aug, k_aug, v_aug, shards, *, tq=512):
    t = q_aug.shape[1]
    tq = min(tq, t)
    tk = tq
    nq = t // tq
    n_pairs = ATTN_W // PAIR
    n_sh = len(shards)

    def body(q_ref, k_ref, v_ref, *rest):
        o_ref, qb_ref, qbt_ref = rest[n_sh:n_sh + 3]
        s_scr = rest[2 * n_sh + 3]
        gather = _Exchange(rest[:n_sh], rest[n_sh + 3:2 * n_sh + 3], *rest[2 * n_sh + 4:], gather=True)
        i = pl.program_id(1)

        @pl.when((pl.program_id(0) == 0) & (i == 0))
        def _():
            gather.start()
        row = lax.broadcasted_iota(jnp.int32, (tq, tk), 0)
        col = lax.broadcasted_iota(jnp.int32, (tq, tk), 1)
        q = [q_ref[0], q_ref[1]]

        def logits(hh, tile, slot, diag):
            s = _dot_nt(q[hh], k_ref[hh, pl.ds(pl.multiple_of(tile * tk, tk), tk), :])
            if diag:
                s = jnp.where(col <= row, s, NEG)
            s_scr[hh, slot] = s
            return jnp.max(s, axis=1, keepdims=True)

        def probs(hh, tile, slot, m, acc, tmax):
            mn = jnp.maximum(m, tmax)
            p = jnp.exp2(s_scr[hh, slot] - mn).astype(BF16)
            acc = jnp.exp2(m - mn) * acc + _dot(p, v_ref[hh, pl.ds(pl.multiple_of(tile * tk, tk), tk), :])
            return mn, acc

        def advance(carry, prev, slot, nxt):
            out = []
            for hh in range(2):
                m, acc, tmax = carry[hh]
                m, acc = probs(hh, prev, slot, m, acc, tmax)
                out.append((m, acc, logits(hh, nxt, 1 - slot, False)))
            return tuple(out)

        def two_tiles(jj, carry):
            carry = advance(carry, jnp.where(jj == 0, i, 2 * jj - 1), 0, 2 * jj)
            return advance(carry, 2 * jj, 1, 2 * jj + 1)

        init = tuple((jnp.full((tq, 1), NEG, F32), jnp.zeros((tq, PAIR), F32), logits(hh, i, 0, True))
                     for hh in range(2))
        carry = lax.fori_loop(0, i // 2, two_tiles, init)
        odd = i % 2 == 1
        carry = lax.cond(odd, lambda c: advance(c, jnp.where(i == 1, i, i - 2), 0, i - 1), lambda c: c, carry)
        last = jnp.where(i == 0, i, i - 1)
        res = []
        for hh in range(2):
            lane, head, aux = _head_lanes(hh)
            m, acc, tmax = carry[hh]
            m, acc = lax.cond(odd, lambda a: probs(hh, last, 1, *a), lambda a: probs(hh, last, 0, *a), (m, acc, tmax))
            l = jnp.sum(jnp.where(lane == aux + AUX_BIAS, acc, 0.0), axis=1, keepdims=True)
            qb = _put_pieces(lane, aux + AUX_LSE, -(m + jnp.log2(l)), q[hh].astype(F32))
            qb_ref[hh] = qb.astype(BF16)
            qbt_ref[hh] = qb.T.astype(BF16)
            res.append(acc / l)
        lane = lax.broadcasted_iota(jnp.int32, (1, PAIR), 1)
        o_ref[...] = jnp.where(lane < HEAD_DIM, res[0], res[1]).astype(BF16)

        @pl.when((pl.program_id(0) == n_pairs - 1) & (i == nq - 1))
        def _():
            gather.wait()

    res = pl.pallas_call(
        body, name="attn_fwd", grid=(n_pairs, nq),
        in_specs=[pl.BlockSpec((2, tq, PAIR), lambda p, i: (p, i, 0)),
                  pl.BlockSpec((2, t, PAIR), lambda p, i: (p, 0, 0), pipeline_mode=pl.Buffered(1)),
                  pl.BlockSpec((2, t, PAIR), lambda p, i: (p, 0, 0), pipeline_mode=pl.Buffered(1))] + [HBM_SPEC] * n_sh,
        out_specs=[pl.BlockSpec((tq, PAIR), lambda p, i: (i, p)),
                   pl.BlockSpec((2, tq, PAIR), lambda p, i: (p, i, 0)),
                   pl.BlockSpec((2, PAIR, tq), lambda p, i: (p, 0, i))] + [HBM_SPEC] * n_sh,
        out_shape=[jax.ShapeDtypeStruct((t, ATTN_W), BF16), jax.ShapeDtypeStruct((N_HEADS, t, PAIR), BF16),
                   jax.ShapeDtypeStruct((N_HEADS, PAIR, t), BF16)]
        + [jax.ShapeDtypeStruct((N_DEV,) + s.shape, s.dtype) for s in shards],
        scratch_shapes=[pltpu.VMEM((2, 2, tq, tk), F32)] + _Exchange.scratch(n_sh),
        compiler_params=_cp(("arbitrary", "arbitrary")),
    )(q_aug, k_aug, v_aug, *shards)
    return res[0], res[1], res[2], res[3:]


def _prev_halo(tm, halo):
    return lambda i: (jnp.maximum(i * (tm // halo) - 1, 0), 0)


def _next_halo(tm, halo, t):
    return lambda i: (jnp.minimum((i + 1) * (tm // halo), t // halo - 1), 0)


def _conv_fwd(bcx, conv_w, *, tm=512):
    t = bcx.shape[0]
    tm = min(tm, t)
    ch = CONV_CH

    def body(b_ref, c_ref, x_ref, hc_ref, hx_ref, w_ref, cv_ref, ext):
        first = pl.program_id(0) == 0
        ext[0:CONV_HALO, :] = jnp.where(first, 0.0, hc_ref[...] * hx_ref[...])
        ext[CONV_HALO:CONV_HALO + tm, :] = c_ref[...] * x_ref[...]
        conv = (w_ref[0:1, :] * ext[CONV_HALO - 2:CONV_HALO - 2 + tm, :]
                + w_ref[1:2, :] * ext[CONV_HALO - 1:CONV_HALO - 1 + tm, :]
                + w_ref[2:3, :] * ext[CONV_HALO:CONV_HALO + tm, :])
        cv_ref[...] = (b_ref[...] * conv).astype(BF16)

    col = lambda k: pl.BlockSpec((tm, ch), lambda i: (i, k))
    halo = lambda k: pl.BlockSpec((CONV_HALO, ch), lambda i: (_prev_halo(tm, CONV_HALO)(i)[0], k))
    return pl.pallas_call(
        body, name="conv_fwd", grid=(t // tm,),
        in_specs=[col(0), col(1), col(2), halo(1), halo(2), _const_spec((8, ch))],
        out_specs=pl.BlockSpec((tm, ch), lambda i: (i, 0)),
        out_shape=jax.ShapeDtypeStruct((t, ch), BF16),
        scratch_shapes=[pltpu.VMEM((CONV_HALO + tm, ch), F32)],
        compiler_params=_cp(("parallel",)),
    )(bcx, bcx, bcx, bcx, bcx, conv_w)


def _outproj(att, cv, x, wout, *, tm=512):
    t, d = x.shape
    tm = min(tm, t)

    def body(a_ref, c_ref, x_ref, w_ref, h_ref):
        h_ref[...] = x_ref[...] + _dot(a_ref[...], w_ref[0:ATTN_W, :]) + _dot(c_ref[...], w_ref[ATTN_W:, :])

    return pl.pallas_call(
        body, name="outproj", grid=(t // tm,),
        in_specs=[pl.BlockSpec((tm, ATTN_W), lambda i: (i, 0)), pl.BlockSpec((tm, CONV_CH), lambda i: (i, 0)),
                  pl.BlockSpec((tm, d), lambda i: (i, 0)), _const_spec(wout.shape)],
        out_specs=pl.BlockSpec((tm, d), lambda i: (i, 0)),
        out_shape=jax.ShapeDtypeStruct((t, d), F32),
        compiler_params=_cp(("parallel",)),
    )(att, cv, x, wout)


def _mlp_fwd(h, g, wup, wdown, *, name, tm=256):
    t, d = h.shape
    n_blk, _, fb = wup.shape
    f = n_blk * fb
    tm = min(tm, t)

    def body(h_ref, g_ref, wu_ref, wd_ref, ho_ref, n_ref, a_ref, z_ref):
        hh = h_ref[...]
        n = (hh * _rstd(hh) * g_ref[...]).astype(BF16)
        n_ref[...] = n
        acc = hh
        for k in range(n_blk):
            a = _dot(n, wu_ref[k])
            zz = jnp.square(jnp.maximum(a, 0.0)).astype(BF16)
            a_ref[:, k * fb:(k + 1) * fb] = a.astype(BF16)
            z_ref[:, k * fb:(k + 1) * fb] = zz
            acc = acc + _dot(zz, wd_ref[k * fb:(k + 1) * fb, :])
        ho_ref[...] = acc

    row = lambda n_: pl.BlockSpec((tm, n_), lambda i: (i, 0))
    return pl.pallas_call(
        body, name=name, grid=(t // tm,),
        in_specs=[row(d), _const_spec((1, d)), _const_spec(wup.shape), _const_spec(wdown.shape)],
        out_specs=[row(d), row(d), row(f), row(f)],
        out_shape=[jax.ShapeDtypeStruct((t, d), F32), jax.ShapeDtypeStruct((t, d), BF16),
                   jax.ShapeDtypeStruct((t, f), BF16), jax.ShapeDtypeStruct((t, f), BF16)],
        compiler_params=_cp(("parallel",)),
    )(h, g, wup, wdown)


def _pool_inv_count(i, tm):
    tglob = (i * tm + lax.broadcasted_iota(jnp.int32, (tm, 1), 0) + 1).astype(F32)
    return [1.0 / jnp.minimum(tglob, float(w)) for w in POOL_WINDOWS]


def _pool_fwd(h, g, poolw, scale, *, tm=256):
    t, d = h.shape
    tm = min(tm, t)
    cg = d // len(POOL_WINDOWS)

    def body(h_ref, hh_ref, g_ref, w_ref, s_ref, ho_ref, p_ref, ext):
        i = pl.program_id(0)
        hv = h_ref[...]
        halo = hh_ref[...]
        n = hv * _rstd(hv) * g_ref[...]
        ext[0:POOL_HALO, :] = jnp.where(i == 0, 0.0, halo * _rstd(halo) * g_ref[...])
        ext[POOL_HALO:POOL_HALO + tm, :] = n
        inv = _pool_inv_count(i, tm)
        for gi, w in enumerate(POOL_WINDOWS):
            cs = slice(gi * cg, (gi + 1) * cg)
            s = ext[POOL_HALO:POOL_HALO + tm, cs]
            for j in range(1, w):
                s = s + ext[POOL_HALO - j:POOL_HALO - j + tm, cs]
            pooled = (s * inv[gi] - n[:, cs]).astype(BF16)
            p_ref[:, cs] = pooled
            ho_ref[:, cs] = hv[:, cs] + _dot(pooled, w_ref[gi]) * s_ref[:, cs]

    row = lambda: pl.BlockSpec((tm, d), lambda i: (i, 0))
    return pl.pallas_call(
        body, name="pool_fwd", grid=(t // tm,),
        in_specs=[row(), pl.BlockSpec((POOL_HALO, d), _prev_halo(tm, POOL_HALO)), _const_spec((1, d)),
                  _const_spec(poolw.shape), _const_spec((1, d))],
        out_specs=[row(), row()],
        out_shape=[jax.ShapeDtypeStruct((t, d), F32), jax.ShapeDtypeStruct((t, d), BF16)],
        scratch_shapes=[pltpu.VMEM((POOL_HALO + tm, d), F32)],
        compiler_params=_cp(("parallel",)),
    )(h, h, g, poolw, scale)


def _loss_bwd(h, g, target, *, tm=512):
    t, d = h.shape
    tm = min(tm, t)
    nsteps = t // tm

    def body(h_ref, g_ref, y_ref, loss_ref, dh_ref, dg_ref, lacc):
        i = pl.program_id(0)

        @pl.when(i == 0)
        def _():
            lacc[...] = jnp.zeros_like(lacc)
            dg_ref[...] = jnp.zeros_like(dg_ref)
        hv = h_ref[...]
        gv = g_ref[...]
        r = _rstd(hv)
        xhat = hv * r
        err = xhat * gv - y_ref[...]
        lacc[...] += _rows8(err * err)
        dout = err * (1.0 / d)
        dy = dout * gv
        dg_ref[...] += _rows8(dout * xhat)
        dh_ref[...] = r * (dy - xhat * jnp.mean(dy * xhat, axis=-1, keepdims=True))

        @pl.when(i == nsteps - 1)
        def _():
            loss_ref[...] = jnp.full(loss_ref.shape, (0.5 / d) * jnp.sum(lacc[...]), F32)

    row = lambda: pl.BlockSpec((tm, d), lambda i: (i, 0))
    return pl.pallas_call(
        body, name="loss_bwd", grid=(nsteps,),
        in_specs=[row(), _const_spec((1, d)), row()],
        out_specs=[pl.BlockSpec((8, 128), lambda i: (0, 0)), row(), pl.BlockSpec((8, d), lambda i: (0, 0))],
        out_shape=[jax.ShapeDtypeStruct((8, 128), F32), jax.ShapeDtypeStruct((t, d), F32),
                   jax.ShapeDtypeStruct((8, d), F32)],
        scratch_shapes=[pltpu.VMEM((8, d), F32)],
        compiler_params=_cp(("arbitrary",)),
    )(h, g, target)


def _mm_tn(a, b, *, name, ta, tb, tt, blocked_out=False, out_dtype=F32):
    t, ka = a.shape
    n = b.shape[1]
    ta, tb, tt = min(ta, ka), min(tb, n), min(tt, t)
    nt = t // tt

    def body(a_ref, b_ref, o_ref, acc):
        @pl.when(pl.program_id(2) == 0)
        def _():
            acc[...] = jnp.zeros_like(acc)
        acc[...] += _dot_tn(a_ref[...].astype(BF16), b_ref[...].astype(BF16))

        @pl.when(pl.program_id(2) == nt - 1)
        def _():
            o_ref[...] = acc[...].astype(out_dtype)

    if blocked_out:
        assert ta == ka
        out_shape = jax.ShapeDtypeStruct((n // tb, ka, tb), out_dtype)
        out_spec = pl.BlockSpec((None, ta, tb), lambda i, j, k: (j, i, 0))
    else:
        out_shape = jax.ShapeDtypeStruct((ka, n), out_dtype)
        out_spec = pl.BlockSpec((ta, tb), lambda i, j, k: (i, j))
    return pl.pallas_call(
        body, name=name, grid=(ka // ta, n // tb, nt),
        in_specs=[pl.BlockSpec((tt, ta), lambda i, j, k: (k, i)), pl.BlockSpec((tt, tb), lambda i, j, k: (k, j))],
        out_specs=out_spec, out_shape=out_shape, scratch_shapes=[pltpu.VMEM((ta, tb), F32)],
        compiler_params=_cp(("parallel", "parallel", "arbitrary")),
    )(a, b)


def _mlp_bwd(dho, h, a, g, wup, wdown, *, name, tm=256):
    t, d = h.shape
    n_blk, _, fb = wup.shape
    f = n_blk * fb
    tm = min(tm, t)

    def body(do_ref, h_ref, a_ref, g_ref, wu_ref, wd_ref, dh_ref, da_ref, dg_ref):
        @pl.when(pl.program_id(0) == 0)
        def _():
            dg_ref[...] = jnp.zeros_like(dg_ref)
        dho_v = do_ref[...]
        dob = dho_v.astype(BF16)
        dn = jnp.zeros((tm, d), F32)
        for k in range(n_blk):
            dz = _dot_nt(dob, wd_ref[k * fb:(k + 1) * fb, :])
            da = (dz * (2.0 * jnp.maximum(a_ref[:, k * fb:(k + 1) * fb].astype(F32), 0.0))).astype(BF16)
            da_ref[:, k * fb:(k + 1) * fb] = da
            dn = dn + _dot_nt(da, wu_ref[k])
        dh, dg = _norm_bwd(dn, h_ref[...], g_ref[...])
        dh_ref[...] = dho_v + dh
        dg_ref[...] += dg

    row = lambda n_: pl.BlockSpec((tm, n_), lambda i: (i, 0))
    return pl.pallas_call(
        body, name=name, grid=(t // tm,),
        in_specs=[row(d), row(d), row(f), _const_spec((1, d)), _const_spec(wup.shape), _const_spec(wdown.shape)],
        out_specs=[row(d), row(f), pl.BlockSpec((8, d), lambda i: (0, 0))],
        out_shape=[jax.ShapeDtypeStruct((t, d), F32), jax.ShapeDtypeStruct((t, f), BF16),
                   jax.ShapeDtypeStruct((8, d), F32)],
        compiler_params=_cp(("arbitrary",)),
    )(dho, h, a, g, wup, wdown)


def _pool_bwd(dho, h, pooled, g, poolw, scale, *, tm=256):
    t, d = h.shape
    tm = min(tm, t)
    ng = len(POOL_WINDOWS)
    cg = d // ng
    nsteps = t // tm

    def body(do_ref, dn_ref, h_ref, p_ref, g_ref, w_ref, s_ref, dh_ref, dw_ref, ds_ref, dg_ref, ext):
        i = pl.program_id(0)

        @pl.when(i == 0)
        def _():
            dw_ref[...] = jnp.zeros_like(dw_ref)
            ds_ref[...] = jnp.zeros_like(ds_ref)
            dg_ref[...] = jnp.zeros_like(dg_ref)
        dho_v = do_ref[...]
        sv = s_ref[...]
        dyp = (dho_v * sv).astype(BF16)
        dyp_halo = (dn_ref[...] * sv).astype(BF16)
        inv = _pool_inv_count(i, tm)
        tnext = ((i + 1) * tm + lax.broadcasted_iota(jnp.int32, (POOL_HALO, 1), 0) + 1).astype(F32)
        last = i == nsteps - 1
        ypre_parts, dpooled_parts = [], []
        for gi, w in enumerate(POOL_WINDOWS):
            cs = slice(gi * cg, (gi + 1) * cg)
            pg = p_ref[:, cs]
            ypre_parts.append(_dot(pg, w_ref[gi]))
            dw_ref[gi] += _dot_tn(pg, dyp[:, cs])
            dpool = _dot_nt(dyp[:, cs], w_ref[gi])
            dpooled_parts.append(dpool)
            ext[0:tm, cs] = dpool * inv[gi]
            dpool_halo = _dot_nt(dyp_halo[:, cs], w_ref[gi]) * (1.0 / jnp.minimum(tnext, float(w)))
            ext[tm:tm + POOL_HALO, cs] = jnp.where(last, 0.0, dpool_halo)
        ds_ref[...] += _rows8(dho_v * jnp.concatenate(ypre_parts, axis=1))
        dn_parts = []
        for gi, w in enumerate(POOL_WINDOWS):
            cs = slice(gi * cg, (gi + 1) * cg)
            s = ext[0:tm, cs]
            for j in range(1, w):
                s = s + ext[j:j + tm, cs]
            dn_parts.append(s - dpooled_parts[gi])
        dh, dg = _norm_bwd(jnp.concatenate(dn_parts, axis=1), h_ref[...], g_ref[...])
        dh_ref[...] = dho_v + dh
        dg_ref[...] += dg

    row = lambda: pl.BlockSpec((tm, d), lambda i: (i, 0))
    acc8 = lambda: pl.BlockSpec((8, d), lambda i: (0, 0))
    return pl.pallas_call(
        body, name="pool_bwd", grid=(nsteps,),
        in_specs=[row(), pl.BlockSpec((POOL_HALO, d), _next_halo(tm, POOL_HALO, t)), row(), row(),
                  _const_spec((1, d)), _const_spec(poolw.shape), _const_spec((1, d))],
        out_specs=[row(), pl.BlockSpec((ng, cg, cg), lambda i: (0, 0, 0)), acc8(), acc8()],
        out_shape=[jax.ShapeDtypeStruct((t, d), F32), jax.ShapeDtypeStruct((ng, cg, cg), F32),
                   jax.ShapeDtypeStruct((8, d), F32), jax.ShapeDtypeStruct((8, d), F32)],
        scratch_shapes=[pltpu.VMEM((tm + POOL_HALO, d), F32)],
        compiler_params=_cp(("arbitrary",)),
    )(dho, dho, h, pooled, g, poolw, scale)


def _outproj_bwd(dh, o, wout, *, tm=512):
    t, d = dh.shape
    tm = min(tm, t)

    def body(dh_ref, o_ref, w_ref, da_ref, dat_ref, dc_ref):
        dhb = dh_ref[...].astype(BF16)
        dc_ref[...] = _dot_nt(dhb, w_ref[ATTN_W:, :])
        for p in range(ATTN_W // PAIR):
            datt = _dot_nt(dhb, w_ref[p * PAIR:(p + 1) * PAIR, :])
            prod = datt * o_ref[:, p * PAIR:(p + 1) * PAIR].astype(F32)
            for hh in range(2):
                lane, head, aux = _head_lanes(hh)
                delta = jnp.sum(jnp.where(head, prod, 0.0), axis=1, keepdims=True)
                aug = _put_pieces(lane, aux + AUX_BIAS, -delta, jnp.where(head, datt, 0.0))
                da_ref[2 * p + hh] = aug.astype(BF16)
                dat_ref[2 * p + hh] = aug.T.astype(BF16)

    row = lambda n_: pl.BlockSpec((tm, n_), lambda i: (i, 0))
    return pl.pallas_call(
        body, name="outproj_bwd", grid=(t // tm,),
        in_specs=[row(d), row(ATTN_W), _const_spec(wout.shape)],
        out_specs=[pl.BlockSpec((N_HEADS, tm, PAIR), lambda i: (0, i, 0)),
                   pl.BlockSpec((N_HEADS, PAIR, tm), lambda i: (0, 0, i)), row(CONV_CH)],
        out_shape=[jax.ShapeDtypeStruct((N_HEADS, t, PAIR), BF16), jax.ShapeDtypeStruct((N_HEADS, PAIR, t), BF16),
                   jax.ShapeDtypeStruct((t, CONV_CH), F32)],
        compiler_params=_cp(("parallel",)),
    )(dh, o, wout)


def _conv_bwd(bcx, dcv, conv_w, *, tm=512):
    t = bcx.shape[0]
    tm = min(tm, t)
    ch = CONV_CH
    nsteps = t // tm

    def body(b_ref, c_ref, x_ref, hc_ref, hx_ref, d_ref, nb_ref, nd_ref, w_ref, o_ref, dw_ref, ext_u, ext_d):
        i = pl.program_id(0)

        @pl.when(i == 0)
        def _():
            dw_ref[...] = jnp.zeros_like(dw_ref)
        b, c, x, dcv_v = b_ref[...], c_ref[...], x_ref[...], d_ref[...]
        ext_u[0:CONV_HALO, :] = jnp.where(i == 0, 0.0, hc_ref[...] * hx_ref[...])
        ext_u[CONV_HALO:CONV_HALO + tm, :] = c * x
        dconv = dcv_v * b
        ext_d[0:tm, :] = dconv
        ext_d[tm:tm + CONV_HALO, :] = jnp.where(i == nsteps - 1, 0.0, nd_ref[...] * nb_ref[...])
        u = [ext_u[CONV_HALO - 2 + k:CONV_HALO - 2 + k + tm, :] for k in range(3)]
        conv = w_ref[0:1, :] * u[0] + w_ref[1:2, :] * u[1] + w_ref[2:3, :] * u[2]
        du = (w_ref[2:3, :] * dconv + w_ref[1:2, :] * ext_d[1:1 + tm, :] + w_ref[0:1, :] * ext_d[2:2 + tm, :])
        o_ref[:, 0:ch] = (dcv_v * conv).astype(BF16)
        o_ref[:, ch:2 * ch] = (du * x).astype(BF16)
        o_ref[:, 2 * ch:3 * ch] = (du * c).astype(BF16)
        for k in range(3):
            dw_ref[k] += _rows8(dconv * u[k])

    col = lambda k: pl.BlockSpec((tm, ch), lambda i: (i, k))
    prev = lambda k: pl.BlockSpec((CONV_HALO, ch), lambda i: (_prev_halo(tm, CONV_HALO)(i)[0], k))
    nxt = lambda k: pl.BlockSpec((CONV_HALO, ch), lambda i: (_next_halo(tm, CONV_HALO, t)(i)[0], k))
    return pl.pallas_call(
        body, name="conv_bwd", grid=(nsteps,),
        in_specs=[col(0), col(1), col(2), prev(1), prev(2), col(0), nxt(0), nxt(0), _const_spec((8, ch))],
        out_specs=[pl.BlockSpec((tm, 3 * ch), lambda i: (i, 0)), pl.BlockSpec((3, 8, ch), lambda i: (0, 0, 0))],
        out_shape=[jax.ShapeDtypeStruct((t, 3 * ch), BF16), jax.ShapeDtypeStruct((3, 8, ch), F32)],
        scratch_shapes=[pltpu.VMEM((CONV_HALO + tm, ch), F32), pltpu.VMEM((tm + CONV_HALO, ch), F32)],
        compiler_params=_cp(("arbitrary",)),
    )(bcx, bcx, bcx, bcx, bcx, dcv, bcx, dcv, conv_w)


def _attn_bwd(q_bwd, do_aug, q_bwd_t, do_aug_t, k_aug, v_aug, gblocks, *, tq=512):
    t = q_bwd.shape[1]
    tq = min(tq, t)
    tk = tq
    nq = t // tq
    n_pairs = ATTN_W // PAIR
    n_g = len(gblocks)

    def body(q_ref, do_ref, qt_ref, dot_ref, k_ref, v_ref, *rest):
        dq_ref, dqx_ref, dk_ref, dkx_ref, dv_ref = rest[n_g:n_g + 5]
        dq_scr = rest[2 * n_g + 5]
        scatter = _Exchange(rest[:n_g], rest[n_g + 5:2 * n_g + 5], *rest[2 * n_g + 6:], gather=False)
        j = pl.program_id(1)

        @pl.when((pl.program_id(0) == 0) & (j == 0))
        def _():
            scatter.start()

        @pl.when(j == 0)
        def _():
            dq_scr[...] = jnp.zeros_like(dq_scr)
        row = lax.broadcasted_iota(jnp.int32, (tq, tk), 0)
        col = lax.broadcasted_iota(jnp.int32, (tq, tk), 1)
        k = [k_ref[0], k_ref[1]]
        v = [v_ref[0], v_ref[1]]

        def step(i, carry, diag):
            qs = pl.multiple_of(i * tq, tq)
            out = []
            for hh in range(2):
                dk_a, dv_a = carry[hh]
                q = q_ref[hh, pl.ds(qs, tq), :]
                dov = do_ref[hh, pl.ds(qs, tq), :]
                p = jnp.exp2(_dot_nt(q, k[hh]))
                if diag:
                    p = jnp.where(col <= row, p, 0.0)
                ds = (p * _dot_nt(dov, v[hh])).astype(BF16)
                dv_a = dv_a + _dot(dot_ref[hh, :, pl.ds(qs, tq)], p.astype(BF16))
                dk_a = dk_a + _dot(qt_ref[hh, :, pl.ds(qs, tq)], ds)
                dq_scr[hh, pl.ds(qs, tq), :] += _dot(ds, k[hh])
                out.append((dk_a, dv_a))
            return tuple(out)

        zero = (jnp.zeros((PAIR, tk), F32), jnp.zeros((PAIR, tk), F32))
        carry = step(j, (zero, zero), True)
        (dk0, dv0), (dk1, dv1) = lax.fori_loop(j + 1, nq, functools.partial(step, diag=False), carry)
        first_t = lax.broadcasted_iota(jnp.int32, (PAIR, 1), 0) < HEAD_DIM
        first = lax.broadcasted_iota(jnp.int32, (1, PAIR), 1) < HEAD_DIM
        dk_ref[...] = (jnp.where(first_t, dk0, dk1).T * (1.0 / LOG2E)).astype(BF16)
        dkx_ref[...] = jnp.where(first_t, dk1, dk0).T
        dv_ref[...] = jnp.where(first_t, dv0, dv1).T.astype(BF16)

        @pl.when(j == nq - 1)
        def _():
            dq_ref[...] = (jnp.where(first, dq_scr[0], dq_scr[1]) * Q_SCALE).astype(BF16)
            dqx_ref[...] = jnp.where(first, dq_scr[1], dq_scr[0])

        @pl.when((pl.program_id(0) == n_pairs - 1) & (j == nq - 1))
        def _():
            scatter.wait()

    resident = lambda: pl.BlockSpec((2, t, PAIR), lambda p, j: (p, 0, 0), pipeline_mode=pl.Buffered(1))
    resident_t = lambda: pl.BlockSpec((2, PAIR, t), lambda p, j: (p, 0, 0), pipeline_mode=pl.Buffered(1))
    kv_in = lambda: pl.BlockSpec((2, tk, PAIR), lambda p, j: (p, j, 0))
    whole = lambda: pl.BlockSpec((t, PAIR), lambda p, j: (0, p))
    tile = lambda: pl.BlockSpec((tk, PAIR), lambda p, j: (j, p))
    b16 = jax.ShapeDtypeStruct((t, ATTN_W), BF16)
    f32 = jax.ShapeDtypeStruct((t, ATTN_W), F32)
    res = pl.pallas_call(
        body, name="attn_bwd", grid=(n_pairs, nq),
        in_specs=[resident(), resident(), resident_t(), resident_t(), kv_in(), kv_in()] + [HBM_SPEC] * n_g,
        out_specs=[whole(), whole(), tile(), tile(), tile()] + [HBM_SPEC] * n_g,
        out_shape=[b16, f32, b16, f32, b16] + [jax.ShapeDtypeStruct(g.shape, g.dtype) for g in gblocks],
        scratch_shapes=[pltpu.VMEM((2, t, PAIR), F32)] + _Exchange.scratch(n_g),
        compiler_params=_cp(("arbitrary", "arbitrary")),
    )(q_bwd, do_aug, q_bwd_t, do_aug_t, k_aug, v_aug, *gblocks)
    return res[:5], res[5:]


def _fgate_bwd(dqx, dkx, sgate, *, tm=256):
    t = sgate.shape[0]
    tm = min(tm, t)
    nsteps = t // tm

    def body(dq_ref, dk_ref, sg_ref, df_ref, dbf_ref, carry):
        @pl.when(pl.program_id(0) == 0)
        def _():
            carry[...] = jnp.zeros_like(carry)
            dbf_ref[...] = jnp.zeros_like(dbf_ref)
        lane = lax.broadcasted_iota(jnp.int32, (ATTN_W, F_PAD), 0)
        head = lax.broadcasted_iota(jnp.int32, (ATTN_W, F_PAD), 1)
        aux = (head // 2) * PAIR + HEAD_DIM * (1 - head % 2)
        valid = head < N_HEADS
        pick_r = (valid & (lane == aux + AUX_ROWSUM)).astype(F32)
        pick_c = (valid & (lane == aux + AUX_BIAS)).astype(F32)
        hp = lax.Precision.HIGHEST
        dcum = (jnp.dot(dq_ref[...], pick_r, preferred_element_type=F32, precision=hp)
                + jnp.dot(dk_ref[...], pick_c, preferred_element_type=F32, precision=hp))
        r = lax.broadcasted_iota(jnp.int32, (tm, tm), 0)
        c = lax.broadcasted_iota(jnp.int32, (tm, tm), 1)
        tri = (c >= r).astype(F32)
        rc = jnp.dot(tri, dcum, preferred_element_type=F32, precision=hp) + carry[...]
        carry[...] = rc[0:1, :]
        df = rc * sg_ref[...]
        df_ref[...] = df.astype(BF16)
        dbf_ref[...] += _rows8(df)

    rev = lambda i: nsteps - 1 - i
    return pl.pallas_call(
        body, name="fgate_bwd", grid=(nsteps,),
        in_specs=[pl.BlockSpec((tm, ATTN_W), lambda i: (rev(i), 0)), pl.BlockSpec((tm, ATTN_W), lambda i: (rev(i), 0)),
                  pl.BlockSpec((tm, F_PAD), lambda i: (rev(i), 0))],
        out_specs=[pl.BlockSpec((tm, F_PAD), lambda i: (rev(i), 0)), pl.BlockSpec((8, F_PAD), lambda i: (0, 0))],
        out_shape=[jax.ShapeDtypeStruct((t, F_PAD), BF16), jax.ShapeDtypeStruct((8, F_PAD), F32)],
        scratch_shapes=[pltpu.VMEM((1, F_PAD), F32)],
        compiler_params=_cp(("arbitrary",)),
    )(dqx, dkx, sgate)


def _inproj_bwd(dq, dk, dv, df, dbcx, dh, x, g, win_p, *, tm=512):
    t, d = x.shape
    tm = min(tm, t)
    n_qkv = 3 * ATTN_W

    def body(dq_ref, dk_ref, dv_ref, df_ref, db_ref, dh_ref, x_ref, g_ref, w_ref, gx_ref, dg_ref):
        @pl.when(pl.program_id(0) == 0)
        def _():
            dg_ref[...] = jnp.zeros_like(dg_ref)
        dn = _dot_nt(df_ref[...], w_ref[:, n_qkv:n_qkv + F_PAD])
        for k, r in enumerate((dq_ref, dk_ref, dv_ref)):
            dn = dn + _dot_nt(r[...], w_ref[:, k * ATTN_W:(k + 1) * ATTN_W])
        for k in range(3):
            c0 = n_qkv + F_PAD + k * CONV_CH
            dn = dn + _dot_nt(db_ref[:, k * CONV_CH:(k + 1) * CONV_CH], w_ref[:, c0:c0 + CONV_CH])
        dx, dg = _norm_bwd(dn, x_ref[...], g_ref[...])
        gx_ref[...] = dh_ref[...] + dx
        dg_ref[...] += dg

    row = lambda n_: pl.BlockSpec((tm, n_), lambda i: (i, 0))
    return pl.pallas_call(
        body, name="inproj_bwd", grid=(t // tm,),
        in_specs=[row(ATTN_W), row(ATTN_W), row(ATTN_W), row(F_PAD), row(3 * CONV_CH), row(d), row(d),
                  _const_spec((1, d)), _const_spec(win_p.shape)],
        out_specs=[row(d), pl.BlockSpec((8, d), lambda i: (0, 0))],
        out_shape=[jax.ShapeDtypeStruct((t, d), F32), jax.ShapeDtypeStruct((8, d), F32)],
        compiler_params=_cp(("arbitrary",)),
    )(dq, dk, dv, df, dbcx, dh, x, g, win_p)


LATE = ("w_out_0", "w_up_0", "w_down_0", "pool_w_1", "w_up_1", "w_down_1")


def _local_step(x, target, gains, b_f, conv_w, pool_scale, win_p, shards):
    d = x.shape[1]
    n0, qkv, flog, bcx = _norm_inproj(x, gains["mix0"], win_p)
    q_aug, k_aug, v_aug, sgate = _fgate_prep(flog, b_f, qkv)
    att, q_bwd, q_bwd_t, gathered = _attn_fwd(q_aug, k_aug, v_aug, [shards[n] for n in LATE])
    g = dict(zip(LATE, gathered))
    wout = g["w_out_0"].reshape(d, d)
    wup0, wup1 = g["w_up_0"], g["w_up_1"]
    wdown0, wdown1 = g["w_down_0"].reshape(-1, d), g["w_down_1"].reshape(-1, d)
    n_grp = len(POOL_WINDOWS)
    cg = d // n_grp
    poolw = g["pool_w_1"].reshape(N_DEV, n_grp, cg // N_DEV, cg).transpose(1, 0, 2, 3).reshape(n_grp, cg, cg)
    cv = _conv_fwd(bcx, conv_w)
    h1 = _outproj(att, cv, x, wout)
    h2, n1, a0, z0 = _mlp_fwd(h1, gains["ffn0"], wup0, wdown0, name="mlp_fwd0")
    h3, pooled = _pool_fwd(h2, gains["mix1"], poolw, pool_scale)
    h4, n3, a1, z1 = _mlp_fwd(h3, gains["ffn1"], wup1, wdown1, name="mlp_fwd1")
    loss, dh4, dg_final = _loss_bwd(h4, gains["final"], target)
    f = a1.shape[1]
    fb = f // N_DEV
    dh3, da1, dg_ffn1 = _mlp_bwd(dh4, h3, a1, gains["ffn1"], wup1, wdown1, name="mlp_bwd1")
    dwdown1 = _mm_tn(z1, dh4, name="dwdown1", ta=1024, tb=1024, tt=1024, out_dtype=BF16)
    dwup1 = _mm_tn(n3, da1, name="dwup1", ta=d, tb=fb, tt=2048, blocked_out=True, out_dtype=BF16)
    dh2, dpoolw, dscale, dg_mix1 = _pool_bwd(dh3, h2, pooled, gains["mix1"], poolw, pool_scale)
    dh1, da0, dg_ffn0 = _mlp_bwd(dh2, h1, a0, gains["ffn0"], wup0, wdown0, name="mlp_bwd0")
    dwdown0 = _mm_tn(z0, dh2, name="dwdown0", ta=1024, tb=1024, tt=1024, out_dtype=BF16)
    dwup0 = _mm_tn(n1, da0, name="dwup0", ta=d, tb=fb, tt=2048, blocked_out=True, out_dtype=BF16)
    do_aug, do_aug_t, dcv = _outproj_bwd(dh1, att, wout)
    dwout = jnp.concatenate([_mm_tn(att, dh1, name="dwout_att", ta=512, tb=1024, tt=2048, out_dtype=BF16),
                             _mm_tn(cv, dh1, name="dwout_conv", ta=512, tb=1024, tt=2048, out_dtype=BF16)], axis=0)
    dbcx, dconvw = _conv_bwd(bcx, dcv, conv_w)
    gblocks = {
        "w_out_0": dwout.reshape(N_DEV, d // N_DEV, d), "w_up_0": dwup0, "w_up_1": dwup1,
        "w_down_0": dwdown0.reshape(N_DEV, -1, d), "w_down_1": dwdown1.reshape(N_DEV, -1, d),
        "pool_w_1": dpoolw.astype(BF16).reshape(n_grp, N_DEV, cg // N_DEV, cg).transpose(1, 0, 2, 3).reshape(
            N_DEV, n_grp * (cg // N_DEV), cg),
    }
    (dq, dqx, dk, dkx, dv), landed = _attn_bwd(q_bwd, do_aug, q_bwd_t, do_aug_t, k_aug, v_aug,
                                               [gblocks[n] for n in LATE])
    df, dbf = _fgate_bwd(dqx, dkx, sgate)
    grad_x, dg_mix0 = _inproj_bwd(dq, dk, dv, df, dbcx, dh1, x, gains["mix0"], win_p)
    dwin_p = jnp.concatenate(
        [_mm_tn(n0, dq, name="dwin_q", ta=d, tb=512, tt=2048), _mm_tn(n0, dk, name="dwin_k", ta=d, tb=512, tt=2048),
         _mm_tn(n0, dv, name="dwin_v", ta=d, tb=512, tt=2048), _mm_tn(n0, df, name="dwin_f", ta=d, tb=128, tt=2048),
         _mm_tn(n0, dbcx, name="dwin_bcx", ta=d, tb=512, tt=2048)], axis=1)
    small = dict(mix0=dg_mix0, ffn0=dg_ffn0, mix1=dg_mix1, pool_scale=dscale, ffn1=dg_ffn1, final=dg_final,
                 b_f=dbf, conv_w=dconvw)
    return loss, grad_x, dwin_p, dict(zip(LATE, landed)), small


def _mesh_places():
    x, y, c = lax.axis_index("x"), lax.axis_index("y"), lax.axis_index("c")
    chips = [(1 - x, y), (x, 1 - y), (1 - x, 1 - y)]
    return (x, y, c), (x, y, 1 - c), chips


def _all_gather(shards):
    n = len(shards)

    def body(*refs):
        ins, outs = refs[:n], refs[n:2 * n]
        send_sems, recv_sems, local_sems = refs[2 * n:]
        me, sib, chips = _mesh_places()
        c = me[2]

        def copy(ai, k, block, to, src=None):
            dst = outs[ai].at[_slot(*block)]
            return pltpu.make_async_remote_copy(
                src_ref=dst if src is None else src, dst_ref=dst, send_sem=send_sems.at[7 * ai + k],
                recv_sem=recv_sems.at[7 * ai + k], device_id=to, device_id_type=MESH)

        mine = [pltpu.make_async_copy(ins[ai], outs[ai].at[_slot(*me)], local_sems.at[ai]) for ai in range(n)]
        for cp in mine:
            cp.start()
        first = []
        for ai in range(n):
            first.append(copy(ai, 0, me, sib, src=ins[ai]))
            first += [copy(ai, 1 + j, me, (*chip, c), src=ins[ai]) for j, chip in enumerate(chips)]
        for cp in first:
            cp.start()
        passed = []
        for ai in range(n):
            for j, chip in enumerate(chips):
                copy(ai, 1 + j, (*chip, c), me).wait_recv()
                cp = copy(ai, 4 + j, (*chip, c), sib)
                cp.start()
                passed.append(cp)
        for ai in range(n):
            copy(ai, 0, sib, me).wait_recv()
            for j, chip in enumerate(chips):
                copy(ai, 4 + j, (*chip, 1 - c), me).wait_recv()
        for cp in first + passed:
            cp.wait_send()
        for cp in mine:
            cp.wait()

    return pl.pallas_call(
        body, name="all_gather",
        in_specs=[HBM_SPEC] * n, out_specs=[HBM_SPEC] * n,
        out_shape=[jax.ShapeDtypeStruct((N_DEV,) + s.shape, s.dtype) for s in shards],
        scratch_shapes=[pltpu.SemaphoreType.DMA((7 * n,)), pltpu.SemaphoreType.DMA((7 * n,)),
                        pltpu.SemaphoreType.DMA((n,))],
    )(*shards)


def _reduce_scatter(grads):
    n = len(grads)
    shapes = [g.shape[1:] for g in grads]

    def body(*refs):
        gs = refs[:n]
        outs, land_a, land_b = refs[n:2 * n], refs[2 * n:3 * n], refs[3 * n:4 * n]
        sendbufs = refs[4 * n:5 * n]
        sa_send, sa_recv, sb_send, sb_recv, lsem = refs[5 * n:]
        me, sib, chips = _mesh_places()
        x, y, c = me

        def copy_a(ai, q):
            qx, qy = q // 2, q % 2
            return pltpu.make_async_remote_copy(
                src_ref=gs[ai].at[_slot(qx, qy, 1 - c)], dst_ref=land_a[ai].at[q], send_sem=sa_send.at[4 * ai + q],
                recv_sem=sa_recv.at[4 * ai + q], device_id=sib, device_id_type=MESH)

        def copy_b(ai, j):
            return pltpu.make_async_remote_copy(
                src_ref=sendbufs[ai].at[j], dst_ref=land_b[ai].at[j], send_sem=sb_send.at[3 * ai + j],
                recv_sem=sb_recv.at[3 * ai + j], device_id=(*chips[j], c), device_id_type=MESH)

        def load(src, dst):
            cp = pltpu.make_async_copy(src, dst, lsem)
            cp.start()
            cp.wait()

        for ai in range(n):
            for q in range(4):
                copy_a(ai, q).start()
        for ai in range(n):
            def stage(buf_a, buf_b, ai=ai):
                for j, (qx, qy) in enumerate(chips):
                    q = 2 * qx + qy
                    pltpu.make_async_remote_copy(
                        src_ref=land_a[ai].at[0], dst_ref=land_a[ai].at[0], send_sem=sa_send.at[0],
                        recv_sem=sa_recv.at[4 * ai + q], device_id=sib, device_id_type=MESH).wait_recv()
                    load(gs[ai].at[_slot(qx, qy, c)], buf_a)
                    load(land_a[ai].at[q], buf_b)
                    sendbufs[ai][j] = (buf_a[...] + buf_b[...]).astype(BF16)
                    copy_b(ai, j).start()
            pl.run_scoped(stage, pltpu.VMEM(shapes[ai], F32), pltpu.VMEM(shapes[ai], F32))
        for ai in range(n):
            def final(buf_a, buf_b, buf_c, ai=ai):
                q = 2 * x + y
                pltpu.make_async_remote_copy(
                    src_ref=land_a[ai].at[0], dst_ref=land_a[ai].at[0], send_sem=sa_send.at[0],
                    recv_sem=sa_recv.at[4 * ai + q], device_id=sib, device_id_type=MESH).wait_recv()
                load(gs[ai].at[_slot(x, y, c)], buf_a)
                load(land_a[ai].at[q], buf_b)
                acc = buf_a[...] + buf_b[...]
                for j in range(3):
                    copy_b(ai, j).wait_recv()
                    load(land_b[ai].at[j], buf_c)
                    acc = acc + buf_c[...].astype(F32)
                buf_a[...] = acc
                load(buf_a, outs[ai])
            pl.run_scoped(final, pltpu.VMEM(shapes[ai], F32), pltpu.VMEM(shapes[ai], F32), pltpu.VMEM(shapes[ai], BF16))
        for ai in range(n):
            for q in range(4):
                copy_a(ai, q).wait_send()
            for j in range(3):
                copy_b(ai, j).wait_send()

    res = pl.pallas_call(
        body, name="reduce_scatter",
        in_specs=[HBM_SPEC] * n, out_specs=[HBM_SPEC] * (3 * n),
        out_shape=([jax.ShapeDtypeStruct(s, F32) for s in shapes]
                   + [jax.ShapeDtypeStruct((4,) + s, F32) for s in shapes]
                   + [jax.ShapeDtypeStruct((3,) + s, BF16) for s in shapes]),
        scratch_shapes=([pltpu.VMEM((3,) + s, BF16) for s in shapes]
                        + [pltpu.SemaphoreType.DMA((4 * n,)), pltpu.SemaphoreType.DMA((4 * n,)),
                           pltpu.SemaphoreType.DMA((3 * n,)), pltpu.SemaphoreType.DMA((3 * n,)),
                           pltpu.SemaphoreType.DMA(())]),
        compiler_params=pltpu.CompilerParams(vmem_limit_bytes=VMEM_LIMIT),
    )(*grads)
    return res[:n]


SMALL_ROWS = 16


def _small_allreduce(parts):
    n, _, w = parts.shape
    assert n <= SMALL_ROWS

    def body(p_ref, o_ref, gath, send_sems, recv_sems):
        x, y, c = lax.axis_index("x"), lax.axis_index("y"), lax.axis_index("c")
        my = _slot(x, y, c)
        rows = [jnp.sum(p_ref[i], axis=0, keepdims=True) for i in range(n)]
        rows.append(jnp.zeros((SMALL_ROWS - n, w), F32))
        gath[my] = jnp.concatenate(rows, axis=0)
        copies = []
        for k in range(1, N_DEV):
            px, py, pc = x ^ (k >> 2), y ^ ((k >> 1) & 1), c ^ (k & 1)
            cp = pltpu.make_async_remote_copy(
                src_ref=gath.at[my], dst_ref=gath.at[my], send_sem=send_sems.at[k - 1], recv_sem=recv_sems.at[k - 1],
                device_id=(px, py, pc), device_id_type=MESH)
            cp.start()
            copies.append(cp)
        for cp in copies:
            cp.wait()
        acc = gath[0]
        for d in range(1, N_DEV):
            acc = acc + gath[d]
        o_ref[...] = acc

    return pl.pallas_call(
        body, name="small_allreduce",
        in_specs=[VMEM_SPEC], out_specs=VMEM_SPEC,
        out_shape=jax.ShapeDtypeStruct((SMALL_ROWS, w), F32),
        scratch_shapes=[pltpu.VMEM((N_DEV, SMALL_ROWS, w), F32), pltpu.SemaphoreType.DMA((N_DEV - 1,)),
                        pltpu.SemaphoreType.DMA((N_DEV - 1,))],
    )(parts)


def _adamw(g, w, m, v, *, name, tm=256):
    r, c = g.shape
    tm = tm if r % tm == 0 else r
    bc1 = 1.0 - ADAM_B1 ** ADAM_STEP
    bc2 = 1.0 - ADAM_B2 ** ADAM_STEP

    def body(g_ref, w_ref, m_ref, v_ref, d_ref, nm_ref, nv_ref):
        gv = g_ref[...]
        nm = ADAM_B1 * m_ref[...] + (1.0 - ADAM_B1) * gv
        nv = ADAM_B2 * v_ref[...] + (1.0 - ADAM_B2) * jnp.square(gv)
        nm_ref[...] = nm
        nv_ref[...] = nv
        d_ref[...] = -ADAM_LR * ((nm / bc1) / (jnp.sqrt(nv / bc2) + ADAM_EPS) + ADAM_WD * w_ref[...])

    blk = pl.BlockSpec((tm, c), lambda i: (i, 0))
    shp = jax.ShapeDtypeStruct((r, c), F32)
    return pl.pallas_call(
        body, name=name, grid=(r // tm,), in_specs=[blk] * 4, out_specs=[blk] * 3, out_shape=[shp] * 3,
        compiler_params=_cp(("parallel",)),
    )(g, w, m, v)


def _adamw_sum(parts, w, m, v, *, name, tm=128):
    _, r, c = parts.shape
    tm = tm if r % tm == 0 else r
    bc1 = 1.0 - ADAM_B1 ** ADAM_STEP
    bc2 = 1.0 - ADAM_B2 ** ADAM_STEP

    def body(p_ref, w_ref, m_ref, v_ref, g_ref, d_ref, nm_ref, nv_ref):
        gv = p_ref[0].astype(F32)
        for k in range(1, N_DEV):
            gv = gv + p_ref[k].astype(F32)
        g_ref[...] = gv
        nm = ADAM_B1 * m_ref[...] + (1.0 - ADAM_B1) * gv
        nv = ADAM_B2 * v_ref[...] + (1.0 - ADAM_B2) * jnp.square(gv)
        nm_ref[...] = nm
        nv_ref[...] = nv
        d_ref[...] = -ADAM_LR * ((nm / bc1) / (jnp.sqrt(nv / bc2) + ADAM_EPS) + ADAM_WD * w_ref[...])

    blk = pl.BlockSpec((tm, c), lambda i: (i, 0))
    shp = jax.ShapeDtypeStruct((r, c), F32)
    return pl.pallas_call(
        body, name=name, grid=(r // tm,), in_specs=[pl.BlockSpec((N_DEV, tm, c), lambda i: (0, i, 0))] + [blk] * 3,
        out_specs=[blk] * 4, out_shape=[shp] * 4, compiler_params=_cp(("parallel",)),
    )(parts, w, m, v)


BIG = ("w_in_0", "w_out_0", "w_up_0", "w_down_0", "pool_w_1", "w_up_1", "w_down_1")
SMALL = ("norm_mix_0", "norm_ffn_0", "norm_mix_1", "pool_scale_1", "norm_ffn_1", "final_norm", "b_f_0", "conv_w_0")
WEIGHTS = ("norm_mix_0", "w_in_0", "b_f_0", "conv_w_0", "w_out_0", "norm_ffn_0", "w_up_0", "w_down_0", "norm_mix_1",
           "pool_w_1", "pool_scale_1", "norm_ffn_1", "w_up_1", "w_down_1", "final_norm")


def _pad_to(a, rows, cols):
    return jnp.pad(a, ((0, rows - a.shape[0]), (0, cols - a.shape[1])))


def _pack_small(p, width):
    rows = [p[n].reshape(1, -1) for n in SMALL[:6]]
    rows.append(_pad_to(p["b_f_0"].reshape(1, -1), 1, width))
    rows.append(_pad_to(p["conv_w_0"], 3, width))
    return _pad_to(jnp.concatenate(rows, axis=0), SMALL_ROWS, width)


def _unpack_small(a, like):
    out = {n: a[i] for i, n in enumerate(SMALL[:6])}
    out["b_f_0"] = a[6, :like["b_f_0"].shape[0]]
    out["conv_w_0"] = a[7:10, :like["conv_w_0"].shape[1]]
    return out


def kernel(x, norm_mix_0, w_in_0, b_f_0, conv_w_0, w_out_0, norm_ffn_0, w_up_0, w_down_0, norm_mix_1, pool_w_1, pool_scale_1, norm_ffn_1, w_up_1, w_down_1, final_norm, loss_target, m_norm_mix_0, m_w_in_0, m_b_f_0, m_conv_w_0, m_w_out_0, m_norm_ffn_0, m_w_up_0, m_w_down_0, m_norm_mix_1, m_pool_w_1, m_pool_scale_1, m_norm_ffn_1, m_w_up_1, m_w_down_1, m_final_norm, v_norm_mix_0, v_w_in_0, v_b_f_0, v_conv_w_0, v_w_out_0, v_norm_ffn_0, v_w_up_0, v_w_down_0, v_norm_mix_1, v_pool_w_1, v_pool_scale_1, v_norm_ffn_1, v_w_up_1, v_w_down_1, v_final_norm):
    w = dict(norm_mix_0=norm_mix_0, w_in_0=w_in_0, b_f_0=b_f_0, conv_w_0=conv_w_0, w_out_0=w_out_0,
             norm_ffn_0=norm_ffn_0, w_up_0=w_up_0, w_down_0=w_down_0, norm_mix_1=norm_mix_1, pool_w_1=pool_w_1,
             pool_scale_1=pool_scale_1, norm_ffn_1=norm_ffn_1, w_up_1=w_up_1, w_down_1=w_down_1, final_norm=final_norm)
    m = dict(norm_mix_0=m_norm_mix_0, w_in_0=m_w_in_0, b_f_0=m_b_f_0, conv_w_0=m_conv_w_0, w_out_0=m_w_out_0,
             norm_ffn_0=m_norm_ffn_0, w_up_0=m_w_up_0, w_down_0=m_w_down_0, norm_mix_1=m_norm_mix_1,
             pool_w_1=m_pool_w_1, pool_scale_1=m_pool_scale_1, norm_ffn_1=m_norm_ffn_1, w_up_1=m_w_up_1,
             w_down_1=m_w_down_1, final_norm=m_final_norm)
    v = dict(norm_mix_0=v_norm_mix_0, w_in_0=v_w_in_0, b_f_0=v_b_f_0, conv_w_0=v_conv_w_0, w_out_0=v_w_out_0,
             norm_ffn_0=v_norm_ffn_0, w_up_0=v_w_up_0, w_down_0=v_w_down_0, norm_mix_1=v_norm_mix_1,
             pool_w_1=v_pool_w_1, pool_scale_1=v_pool_scale_1, norm_ffn_1=v_norm_ffn_1, w_up_1=v_w_up_1,
             w_down_1=v_w_down_1, final_norm=v_final_norm)
    d = x.shape[-1]
    n_in = w_in_0.shape[1] * N_DEV
    n_qkv = 3 * ATTN_W
    pool_g, pool_rows, pool_c = pool_w_1.shape

    def shard2d(p):
        return {n: (p[n].reshape(pool_g * pool_rows, pool_c) if n == "pool_w_1" else p[n]) for n in BIG}
    w2, m2, v2 = shard2d(w), shard2d(m), shard2d(v)

    conv_cols = conv_w_0.shape[1]
    win_g8, conv_g8 = _all_gather([w_in_0.astype(BF16), _pad_to(conv_w_0, 8, 128)])
    conv_full = conv_g8[:, :, :conv_cols].transpose(1, 0, 2).reshape(8, N_DEV * conv_cols)
    win = win_g8.transpose(1, 0, 2).reshape(d, n_in)
    win_p = jnp.concatenate([win[:, :n_qkv], _pad_to(win[:, n_qkv:n_qkv + N_HEADS], d, F_PAD),
                             win[:, n_qkv + N_HEADS:]], axis=1)

    gains = dict(mix0=norm_mix_0.reshape(1, d), ffn0=norm_ffn_0.reshape(1, d), mix1=norm_mix_1.reshape(1, d),
                 ffn1=norm_ffn_1.reshape(1, d), final=final_norm.reshape(1, d))
    dev = _slot(lax.axis_index("x"), lax.axis_index("y"), lax.axis_index("c"))
    loss8, grad_x, dwin_p, landed, small = _local_step(
        x[0], loss_target[0], gains, _pad_to(b_f_0.reshape(1, -1), 1, F_PAD), conv_full, pool_scale_1.reshape(1, d),
        win_p, {n: w2[n].astype(BF16) for n in LATE})
    loss = lax.psum(loss8[0, 0], ("x", "y", "c"))

    dwin = jnp.concatenate([dwin_p[:, :n_qkv + N_HEADS], dwin_p[:, n_qkv + F_PAD:]], axis=1)
    g_win, = _reduce_scatter([dwin.reshape(d, N_DEV, n_in // N_DEV).transpose(1, 0, 2)])

    parts = jnp.concatenate(
        [small[k][None] for k in ("mix0", "ffn0", "mix1", "pool_scale", "ffn1", "final")]
        + [_pad_to(small["b_f"], 8, d)[None], jnp.pad(small["conv_w"], ((0, 0), (0, 0), (0, d - CONV_CH)))], axis=0)
    tot = _small_allreduce(parts)
    conv_g = lax.dynamic_slice(tot, (7, dev * conv_cols), (3, conv_cols))
    gs = tot.at[7:10].set(_pad_to(conv_g, 3, d))

    grads, deltas, new_m, new_v = {}, {}, {}, {}
    for n in BIG:
        if n in LATE:
            gr, dl, nm, nv = _adamw_sum(landed[n], w2[n], m2[n], v2[n], name="adamw_" + n)
        else:
            gr = g_win
            dl, nm, nv = _adamw(gr, w2[n], m2[n], v2[n], name="adamw_" + n)
        for dst, val in ((grads, gr), (deltas, dl), (new_m, nm), (new_v, nv)):
            dst[n] = val.reshape(w[n].shape)
    dl, nm, nv = _adamw(gs, _pack_small(w, d), _pack_small(m, d), _pack_small(v, d), name="adamw_small")
    for dst, val in ((grads, gs), (deltas, dl), (new_m, nm), (new_v, nv)):
        dst.update(_unpack_small(val, w))
    return (loss, grad_x[None], *[grads[n] for n in WEIGHTS], *[deltas[n] for n in WEIGHTS],
            *[new_m[n] for n in WEIGHTS], *[new_v[n] for n in WEIGHTS])
```

```python
import functools

import jax
import jax.numpy as jnp
from jax import lax
from jax.experimental import pallas as pl
from jax.experimental.pallas import tpu as pltpu

F32 = jnp.float32
BF16 = jnp.bfloat16

N_DEV = 8
N_HEADS = 8
HEAD_DIM = 64
PAIR = 2 * HEAD_DIM
ATTN_W = N_HEADS * HEAD_DIM
CONV_CH = 512
F_PAD = 128
POOL_WINDOWS = (2, 4, 8, 16)
POOL_HALO = 16
CONV_HALO = 8
RMS_EPS = 1e-6
Q_SCALE = HEAD_DIM ** -0.5
LOG2E = 1.4426950408889634
NEG = -1e30
AUX_BIAS = 0
AUX_LSE = 3
AUX_ROWSUM = 6
ADAM_LR, ADAM_B1, ADAM_B2, ADAM_EPS, ADAM_WD, ADAM_STEP = 0.001, 0.9, 0.999, 1e-08, 0.01, 10
MESH = pl.DeviceIdType.MESH
VMEM_LIMIT = 56 * 2**20


def _cp(sem=None, vmem=VMEM_LIMIT, **kw):
    return pltpu.CompilerParams(dimension_semantics=sem, vmem_limit_bytes=vmem, **kw)


def _dot(a, b):
    return jnp.dot(a, b, preferred_element_type=F32)


def _dot_nt(a, b):
    return lax.dot_general(a, b, (((1,), (1,)), ((), ())), preferred_element_type=F32)


def _dot_tn(a, b):
    return lax.dot_general(a, b, (((0,), (0,)), ((), ())), preferred_element_type=F32)


def _rstd(h):
    return lax.rsqrt(jnp.mean(h * h, axis=-1, keepdims=True) + RMS_EPS)


def _rows8(x):
    r, n = x.shape
    return jnp.sum(x.reshape(r // 8, 8, n), axis=0)


def _norm_bwd(dn, h, g):
    r = _rstd(h)
    xhat = h * r
    dy = dn * g
    dh = r * (dy - xhat * jnp.mean(dy * xhat, axis=-1, keepdims=True))
    return dh, _rows8(dn * xhat)


def _const_spec(shape):
    nd = len(shape)
    return pl.BlockSpec(shape, lambda *_: (0,) * nd, pipeline_mode=pl.Buffered(1))


HBM_SPEC = pl.BlockSpec(memory_space=pltpu.HBM)
VMEM_SPEC = pl.BlockSpec(memory_space=pltpu.VMEM)


def _slot(px, py, pc):
    return 4 * px + 2 * py + pc


class _Exchange:
    def __init__(self, srcs, dsts, send_sems, recv_sems, local_sems, gather):
        x, y, c = lax.axis_index("x"), lax.axis_index("y"), lax.axis_index("c")
        me = _slot(x, y, c)
        self.copies = []
        for a, (src, dst) in enumerate(zip(srcs, dsts)):
            self.copies.append(pltpu.make_async_copy(src if gather else src.at[me], dst.at[me], local_sems.at[a]))
            for k in range(1, N_DEV):
                px, py, pc = x ^ (k >> 2), y ^ ((k >> 1) & 1), c ^ (k & 1)
                self.copies.append(pltpu.make_async_remote_copy(
                    src_ref=src if gather else src.at[_slot(px, py, pc)], dst_ref=dst.at[me],
                    send_sem=send_sems.at[(N_DEV - 1) * a + k - 1], recv_sem=recv_sems.at[(N_DEV - 1) * a + k - 1],
                    device_id=(px, py, pc), device_id_type=MESH))

    def start(self):
        for cp in self.copies:
            cp.start()

    def wait(self):
        for cp in self.copies:
            cp.wait()

    @staticmethod
    def scratch(n):
        return [pltpu.SemaphoreType.DMA(((N_DEV - 1) * n,)), pltpu.SemaphoreType.DMA(((N_DEV - 1) * n,)),
                pltpu.SemaphoreType.DMA((n,))]


def _norm_inproj(x, g, win_p, *, tm=512):
    t, d = x.shape
    n_all = win_p.shape[1]
    n_qkv = 3 * ATTN_W
    n_bcx = 3 * CONV_CH
    assert n_all == n_qkv + F_PAD + n_bcx
    tm = min(tm, t)

    def body(x_ref, g_ref, w_ref, n_ref, qkv_ref, f_ref, bcx_ref):
        h = x_ref[...]
        n = (h * _rstd(h) * g_ref[...]).astype(BF16)
        n_ref[...] = n
        for c0 in range(0, n_qkv, 512):
            acc = _dot(n, w_ref[:, c0:c0 + 512])
            if c0 < ATTN_W:
                acc = acc * (Q_SCALE * LOG2E)
            qkv_ref[:, c0:c0 + 512] = acc.astype(BF16)
        f_ref[...] = _dot(n, w_ref[:, n_qkv:n_qkv + F_PAD])
        for c0 in range(0, n_bcx, 512):
            bcx_ref[:, c0:c0 + 512] = _dot(n, w_ref[:, n_qkv + F_PAD + c0:n_qkv + F_PAD + c0 + 512])

    return pl.pallas_call(
        body, name="norm_inproj", grid=(t // tm,),
        in_specs=[pl.BlockSpec((tm, d), lambda i: (i, 0)), _const_spec((1, d)), _const_spec((d, n_all))],
        out_specs=[pl.BlockSpec((tm, d), lambda i: (i, 0)), pl.BlockSpec((tm, n_qkv), lambda i: (i, 0)),
                   pl.BlockSpec((tm, F_PAD), lambda i: (i, 0)), pl.BlockSpec((tm, n_bcx), lambda i: (i, 0))],
        out_shape=[jax.ShapeDtypeStruct((t, d), BF16), jax.ShapeDtypeStruct((t, n_qkv), BF16),
                   jax.ShapeDtypeStruct((t, F_PAD), F32), jax.ShapeDtypeStruct((t, n_bcx), F32)],
        compiler_params=_cp(("parallel",)),
    )(x, g, win_p)


def _head_lanes(h):
    lane = lax.broadcasted_iota(jnp.int32, (1, PAIR), 1)
    hh = h % 2
    return lane, lane // HEAD_DIM == hh, HEAD_DIM * (1 - hh)


def _pieces(col):
    hi = col.astype(BF16).astype(F32)
    r1 = col - hi
    mid = r1.astype(BF16).astype(F32)
    lo = (r1 - mid).astype(BF16).astype(F32)
    return hi, mid, lo


def _put_pieces(lane, first, col, other):
    hi, mid, lo = _pieces(col)
    return jnp.where(lane == first, hi, jnp.where(lane == first + 1, mid, jnp.where(lane == first + 2, lo, other)))


def _fgate_prep(flog, b_f, qkv, *, tm=256):
    t = flog.shape[0]
    tm = min(tm, t)

    def body(f_ref, b_ref, qkv_ref, qat_ref, ka_ref, va_ref, vat_ref, sg_ref, carry):
        @pl.when(pl.program_id(0) == 0)
        def _():
            carry[...] = jnp.zeros_like(carry)
        z = f_ref[...] + b_ref[...]
        e = jnp.exp(-jnp.abs(z))
        logf = jnp.minimum(z, 0.0) - jnp.log(1.0 + e)
        sg_ref[...] = jnp.where(z >= 0, e, 1.0) / (1.0 + e)
        r = lax.broadcasted_iota(jnp.int32, (tm, tm), 0)
        c = lax.broadcasted_iota(jnp.int32, (tm, tm), 1)
        tri = (c <= r).astype(F32)
        cs = jnp.dot(tri, logf, preferred_element_type=F32, precision=lax.Precision.HIGHEST) + carry[...]
        carry[...] = cs[tm - 1:tm, :]
        cs2 = cs * LOG2E
        for h in range(N_HEADS):
            lane, head, aux = _head_lanes(h)
            p0 = (h // 2) * PAIR
            ones = ((lane >= aux + AUX_LSE) & (lane <= aux + AUX_ROWSUM)).astype(F32)
            bias = (lane >= aux + AUX_BIAS) & (lane < aux + AUX_BIAS + 3)
            k_aux = _put_pieces(lane, aux + AUX_BIAS, cs2[:, h:h + 1], ones)
            q_aug = jnp.where(head, qkv_ref[:, p0:p0 + PAIR].astype(F32), jnp.where(bias, -1.0, 0.0))
            v_aug = jnp.where(head, qkv_ref[:, 2 * ATTN_W + p0:2 * ATTN_W + p0 + PAIR].astype(F32),
                              jnp.where(bias, 1.0, 0.0))
            qat_ref[h] = q_aug.T.astype(BF16)
            ka_ref[h] = jnp.where(head, qkv_ref[:, ATTN_W + p0:ATTN_W + p0 + PAIR], k_aux.astype(BF16))
            va_ref[h] = v_aug.astype(BF16)
            vat_ref[h] = v_aug.T.astype(BF16)

    aug = lambda: pl.BlockSpec((N_HEADS, tm, PAIR), lambda i: (0, i, 0))
    aug_t = lambda: pl.BlockSpec((N_HEADS, PAIR, tm), lambda i: (0, 0, i))
    aug_shape = jax.ShapeDtypeStruct((N_HEADS, t, PAIR), BF16)
    aug_t_shape = jax.ShapeDtypeStruct((N_HEADS, PAIR, t), BF16)
    return pl.pallas_call(
        body, name="fgate_prep", grid=(t // tm,),
        in_specs=[pl.BlockSpec((tm, F_PAD), lambda i: (i, 0)), _const_spec((1, F_PAD)),
                  pl.BlockSpec((tm, 3 * ATTN_W), lambda i: (i, 0))],
        out_specs=[aug_t(), aug(), aug(), aug_t(), pl.BlockSpec((tm, F_PAD), lambda i: (i, 0))],
        out_shape=[aug_t_shape, aug_shape, aug_shape, aug_t_shape, jax.ShapeDtypeStruct((t, F_PAD), F32)],
        scratch_shapes=[pltpu.VMEM((1, F_PAD), F32)],
        compiler_params=_cp(("arbitrary",)),
    )(flog, b_f, qkv)


def _put_pieces_t(row, first, vec, other):
    hi, mid, lo = _pieces(vec)
    return jnp.where(row == first, hi, jnp.where(row == first + 1, mid, jnp.where(row == first + 2, lo, other)))


def _attn_fwd(q_aug_t, k_aug, v_aug_t, shards, *, tq=512):
    t = k_aug.shape[1]
    tq = min(tq, t)
    tk = tq
    nq = t // tq
    n_pairs = ATTN_W // PAIR
    n_sh = len(shards)

    def body(qt_ref, k_ref, vt_ref, *rest):
        o_ref, qb_ref, qbt_ref = rest[n_sh:n_sh + 3]
        s_scr = rest[2 * n_sh + 3]
        gather = _Exchange(rest[:n_sh], rest[n_sh + 3:2 * n_sh + 3], *rest[2 * n_sh + 4:], gather=True)
        i = pl.program_id(1)

        @pl.when((pl.program_id(0) == 0) & (i == 0))
        def _():
            gather.start()
        key = lax.broadcasted_iota(jnp.int32, (tk, tq), 0)
        qry = lax.broadcasted_iota(jnp.int32, (tk, tq), 1)
        qt = [qt_ref[0], qt_ref[1]]

        def logits(hh, tile, slot, diag):
            s = _dot(k_ref[hh, pl.ds(pl.multiple_of(tile * tk, tk), tk), :], qt[hh])
            if diag:
                s = jnp.where(key <= qry, s, NEG)
            s_scr[hh, slot] = s
            return jnp.max(s, axis=0, keepdims=True)

        def probs(hh, tile, slot, m, acc, tmax):
            mn = jnp.maximum(m, tmax)
            p = jnp.exp2(s_scr[hh, slot] - mn).astype(BF16)
            acc = jnp.exp2(m - mn) * acc + _dot(vt_ref[hh, :, pl.ds(pl.multiple_of(tile * tk, tk), tk)], p)
            return mn, acc

        def advance(carry, prev, slot, nxt):
            out = []
            for hh in range(2):
                m, acc, tmax = carry[hh]
                m, acc = probs(hh, prev, slot, m, acc, tmax)
                out.append((m, acc, logits(hh, nxt, 1 - slot, False)))
            return tuple(out)

        def two_tiles(jj, carry):
            carry = advance(carry, jnp.where(jj == 0, i, 2 * jj - 1), 0, 2 * jj)
            return advance(carry, 2 * jj, 1, 2 * jj + 1)

        init = tuple((jnp.full((1, tq), NEG, F32), jnp.zeros((PAIR, tq), F32), logits(hh, i, 0, True))
                     for hh in range(2))
        carry = lax.fori_loop(0, i // 2, two_tiles, init)
        odd = i % 2 == 1
        carry = lax.cond(odd, lambda c: advance(c, jnp.where(i == 1, i, i - 2), 0, i - 1), lambda c: c, carry)
        last = jnp.where(i == 0, i, i - 1)
        row = lax.broadcasted_iota(jnp.int32, (PAIR, 1), 0)
        res = []
        for hh in range(2):
            aux = HEAD_DIM * (1 - hh)
            m, acc, tmax = carry[hh]
            m, acc = lax.cond(odd, lambda a: probs(hh, last, 1, *a), lambda a: probs(hh, last, 0, *a), (m, acc, tmax))
            l = acc[aux + AUX_BIAS:aux + AUX_BIAS + 1, :]
            qbt = _put_pieces_t(row, aux + AUX_LSE, -(m + jnp.log2(l)), qt[hh].astype(F32))
            qbt_ref[hh] = qbt.astype(BF16)
            qb_ref[hh] = qbt.T.astype(BF16)
            res.append(acc / l)
        o_ref[...] = jnp.where(row < HEAD_DIM, res[0], res[1]).T.astype(BF16)

        @pl.when((pl.program_id(0) == n_pairs - 1) & (i == nq - 1))
        def _():
            gather.wait()

    res = pl.pallas_call(
        body, name="attn_fwd", grid=(n_pairs, nq),
        in_specs=[pl.BlockSpec((2, PAIR, tq), lambda p, i: (p, 0, i)),
                  pl.BlockSpec((2, t, PAIR), lambda p, i: (p, 0, 0), pipeline_mode=pl.Buffered(1)),
                  pl.BlockSpec((2, PAIR, t), lambda p, i: (p, 0, 0), pipeline_mode=pl.Buffered(1))] + [HBM_SPEC] * n_sh,
        out_specs=[pl.BlockSpec((tq, PAIR), lambda p, i: (i, p)),
                   pl.BlockSpec((2, tq, PAIR), lambda p, i: (p, i, 0)),
                   pl.BlockSpec((2, PAIR, tq), lambda p, i: (p, 0, i))] + [HBM_SPEC] * n_sh,
        out_shape=[jax.ShapeDtypeStruct((t, ATTN_W), BF16), jax.ShapeDtypeStruct((N_HEADS, t, PAIR), BF16),
                   jax.ShapeDtypeStruct((N_HEADS, PAIR, t), BF16)]
        + [jax.ShapeDtypeStruct((N_DEV,) + s.shape, s.dtype) for s in shards],
        scratch_shapes=[pltpu.VMEM((2, 2, tk, tq), F32)] + _Exchange.scratch(n_sh),
        compiler_params=_cp(("arbitrary", "arbitrary")),
    )(q_aug_t, k_aug, v_aug_t, *shards)
    return res[0], res[1], res[2], res[3:]


def _prev_halo(tm, halo):
    return lambda i: (jnp.maximum(i * (tm // halo) - 1, 0), 0)


def _next_halo(tm, halo, t):
    return lambda i: (jnp.minimum((i + 1) * (tm // halo), t // halo - 1), 0)


def _conv_fwd(bcx, conv_w, *, tm=512):
    t = bcx.shape[0]
    tm = min(tm, t)
    ch = CONV_CH

    def body(b_ref, c_ref, x_ref, hc_ref, hx_ref, w_ref, cv_ref, ext):
        first = pl.program_id(0) == 0
        ext[0:CONV_HALO, :] = jnp.where(first, 0.0, hc_ref[...] * hx_ref[...])
        ext[CONV_HALO:CONV_HALO + tm, :] = c_ref[...] * x_ref[...]
        conv = (w_ref[0:1, :] * ext[CONV_HALO - 2:CONV_HALO - 2 + tm, :]
                + w_ref[1:2, :] * ext[CONV_HALO - 1:CONV_HALO - 1 + tm, :]
                + w_ref[2:3, :] * ext[CONV_HALO:CONV_HALO + tm, :])
        cv_ref[...] = (b_ref[...] * conv).astype(BF16)

    col = lambda k: pl.BlockSpec((tm, ch), lambda i: (i, k))
    halo = lambda k: pl.BlockSpec((CONV_HALO, ch), lambda i: (_prev_halo(tm, CONV_HALO)(i)[0], k))
    return pl.pallas_call(
        body, name="conv_fwd", grid=(t // tm,),
        in_specs=[col(0), col(1), col(2), halo(1), halo(2), _const_spec((8, ch))],
        out_specs=pl.BlockSpec((tm, ch), lambda i: (i, 0)),
        out_shape=jax.ShapeDtypeStruct((t, ch), BF16),
        scratch_shapes=[pltpu.VMEM((CONV_HALO + tm, ch), F32)],
        compiler_params=_cp(("parallel",)),
    )(bcx, bcx, bcx, bcx, bcx, conv_w)


def _outproj(att, cv, x, wout, *, tm=512):
    t, d = x.shape
    tm = min(tm, t)

    def body(a_ref, c_ref, x_ref, w_ref, h_ref):
        h_ref[...] = x_ref[...] + _dot(a_ref[...], w_ref[0:ATTN_W, :]) + _dot(c_ref[...], w_ref[ATTN_W:, :])

    return pl.pallas_call(
        body, name="outproj", grid=(t // tm,),
        in_specs=[pl.BlockSpec((tm, ATTN_W), lambda i: (i, 0)), pl.BlockSpec((tm, CONV_CH), lambda i: (i, 0)),
                  pl.BlockSpec((tm, d), lambda i: (i, 0)), _const_spec(wout.shape)],
        out_specs=pl.BlockSpec((tm, d), lambda i: (i, 0)),
        out_shape=jax.ShapeDtypeStruct((t, d), F32),
        compiler_params=_cp(("parallel",)),
    )(att, cv, x, wout)


def _mlp_fwd(h, g, wup, wdown, *, name, tm=256):
    t, d = h.shape
    n_blk, _, fb = wup.shape
    f = n_blk * fb
    tm = min(tm, t)

    def body(h_ref, g_ref, wu_ref, wd_ref, ho_ref, n_ref, a_ref, z_ref):
        hh = h_ref[...]
        n = (hh * _rstd(hh) * g_ref[...]).astype(BF16)
        n_ref[...] = n
        acc = hh
        for k in range(n_blk):
            a = _dot(n, wu_ref[k])
            zz = jnp.square(jnp.maximum(a, 0.0)).astype(BF16)
            a_ref[:, k * fb:(k + 1) * fb] = a.astype(BF16)
            z_ref[:, k * fb:(k + 1) * fb] = zz
            acc = acc + _dot(zz, wd_ref[k * fb:(k + 1) * fb, :])
        ho_ref[...] = acc

    row = lambda n_: pl.BlockSpec((tm, n_), lambda i: (i, 0))
    return pl.pallas_call(
        body, name=name, grid=(t // tm,),
        in_specs=[row(d), _const_spec((1, d)), _const_spec(wup.shape), _const_spec(wdown.shape)],
        out_specs=[row(d), row(d), row(f), row(f)],
        out_shape=[jax.ShapeDtypeStruct((t, d), F32), jax.ShapeDtypeStruct((t, d), BF16),
                   jax.ShapeDtypeStruct((t, f), BF16), jax.ShapeDtypeStruct((t, f), BF16)],
        compiler_params=_cp(("parallel",)),
    )(h, g, wup, wdown)


def _pool_inv_count(i, tm):
    tglob = (i * tm + lax.broadcasted_iota(jnp.int32, (tm, 1), 0) + 1).astype(F32)
    return [1.0 / jnp.minimum(tglob, float(w)) for w in POOL_WINDOWS]


def _pool_fwd(h, g, poolw, scale, *, tm=256):
    t, d = h.shape
    tm = min(tm, t)
    cg = d // len(POOL_WINDOWS)

    def body(h_ref, hh_ref, g_ref, w_ref, s_ref, ho_ref, p_ref, ext):
        i = pl.program_id(0)
        hv = h_ref[...]
        halo = hh_ref[...]
        n = hv * _rstd(hv) * g_ref[...]
        ext[0:POOL_HALO, :] = jnp.where(i == 0, 0.0, halo * _rstd(halo) * g_ref[...])
        ext[POOL_HALO:POOL_HALO + tm, :] = n
        inv = _pool_inv_count(i, tm)
        for gi, w in enumerate(POOL_WINDOWS):
            cs = slice(gi * cg, (gi + 1) * cg)
            s = ext[POOL_HALO:POOL_HALO + tm, cs]
            for j in range(1, w):
                s = s + ext[POOL_HALO - j:POOL_HALO - j + tm, cs]
            pooled = (s * inv[gi] - n[:, cs]).astype(BF16)
            p_ref[:, cs] = pooled
            ho_ref[:, cs] = hv[:, cs] + _dot(pooled, w_ref[gi]) * s_ref[:, cs]

    row = lambda: pl.BlockSpec((tm, d), lambda i: (i, 0))
    return pl.pallas_call(
        body, name="pool_fwd", grid=(t // tm,),
        in_specs=[row(), pl.BlockSpec((POOL_HALO, d), _prev_halo(tm, POOL_HALO)), _const_spec((1, d)),
                  _const_spec(poolw.shape), _const_spec((1, d))],
        out_specs=[row(), row()],
        out_shape=[jax.ShapeDtypeStruct((t, d), F32), jax.ShapeDtypeStruct((t, d), BF16)],
        scratch_shapes=[pltpu.VMEM((POOL_HALO + tm, d), F32)],
        compiler_params=_cp(("parallel",)),
    )(h, h, g, poolw, scale)


def _loss_bwd(h, g, target, *, tm=512):
    t, d = h.shape
    tm = min(tm, t)
    nsteps = t // tm

    def body(h_ref, g_ref, y_ref, loss_ref, dh_ref, dg_ref, lacc):
        i = pl.program_id(0)

        @pl.when(i == 0)
        def _():
            lacc[...] = jnp.zeros_like(lacc)
            dg_ref[...] = jnp.zeros_like(dg_ref)
        hv = h_ref[...]
        gv = g_ref[...]
        r = _rstd(hv)
        xhat = hv * r
        err = xhat * gv - y_ref[...]
        lacc[...] += _rows8(err * err)
        dout = err * (1.0 / d)
        dy = dout * gv
        dg_ref[...] += _rows8(dout * xhat)
        dh_ref[...] = r * (dy - xhat * jnp.mean(dy * xhat, axis=-1, keepdims=True))

        @pl.when(i == nsteps - 1)
        def _():
            loss_ref[...] = jnp.full(loss_ref.shape, (0.5 / d) * jnp.sum(lacc[...]), F32)

    row = lambda: pl.BlockSpec((tm, d), lambda i: (i, 0))
    return pl.pallas_call(
        body, name="loss_bwd", grid=(nsteps,),
        in_specs=[row(), _const_spec((1, d)), row()],
        out_specs=[pl.BlockSpec((8, 128), lambda i: (0, 0)), row(), pl.BlockSpec((8, d), lambda i: (0, 0))],
        out_shape=[jax.ShapeDtypeStruct((8, 128), F32), jax.ShapeDtypeStruct((t, d), F32),
                   jax.ShapeDtypeStruct((8, d), F32)],
        scratch_shapes=[pltpu.VMEM((8, d), F32)],
        compiler_params=_cp(("arbitrary",)),
    )(h, g, target)


def _mm_tn(a, b, *, name, ta, tb, tt, blocked_out=False, out_dtype=F32):
    t, ka = a.shape
    n = b.shape[1]
    ta, tb, tt = min(ta, ka), min(tb, n), min(tt, t)
    nt = t // tt

    def body(a_ref, b_ref, o_ref, acc):
        @pl.when(pl.program_id(2) == 0)
        def _():
            acc[...] = jnp.zeros_like(acc)
        acc[...] += _dot_tn(a_ref[...].astype(BF16), b_ref[...].astype(BF16))

        @pl.when(pl.program_id(2) == nt - 1)
        def _():
            o_ref[...] = acc[...].astype(out_dtype)

    if blocked_out:
        assert ta == ka
        out_shape = jax.ShapeDtypeStruct((n // tb, ka, tb), out_dtype)
        out_spec = pl.BlockSpec((None, ta, tb), lambda i, j, k: (j, i, 0))
    else:
        out_shape = jax.ShapeDtypeStruct((ka, n), out_dtype)
        out_spec = pl.BlockSpec((ta, tb), lambda i, j, k: (i, j))
    return pl.pallas_call(
        body, name=name, grid=(ka // ta, n // tb, nt),
        in_specs=[pl.BlockSpec((tt, ta), lambda i, j, k: (k, i)), pl.BlockSpec((tt, tb), lambda i, j, k: (k, j))],
        out_specs=out_spec, out_shape=out_shape, scratch_shapes=[pltpu.VMEM((ta, tb), F32)],
        compiler_params=_cp(("parallel", "parallel", "arbitrary")),
    )(a, b)


def _mlp_bwd(dho, h, a, g, wup, wdown, *, name, tm=256):
    t, d = h.shape
    n_blk, _, fb = wup.shape
    f = n_blk * fb
    tm = min(tm, t)

    def body(do_ref, h_ref, a_ref, g_ref, wu_ref, wd_ref, dh_ref, da_ref, dg_ref):
        @pl.when(pl.program_id(0) == 0)
        def _():
            dg_ref[...] = jnp.zeros_like(dg_ref)
        dho_v = do_ref[...]
        dob = dho_v.astype(BF16)
        dn = jnp.zeros((tm, d), F32)
        for k in range(n_blk):
            dz = _dot_nt(dob, wd_ref[k * fb:(k + 1) * fb, :])
            da = (dz * (2.0 * jnp.maximum(a_ref[:, k * fb:(k + 1) * fb].astype(F32), 0.0))).astype(BF16)
            da_ref[:, k * fb:(k + 1) * fb] = da
            dn = dn + _dot_nt(da, wu_ref[k])
        dh, dg = _norm_bwd(dn, h_ref[...], g_ref[...])
        dh_ref[...] = dho_v + dh
        dg_ref[...] += dg

    row = lambda n_: pl.BlockSpec((tm, n_), lambda i: (i, 0))
    return pl.pallas_call(
        body, name=name, grid=(t // tm,),
        in_specs=[row(d), row(d), row(f), _const_spec((1, d)), _const_spec(wup.shape), _const_spec(wdown.shape)],
        out_specs=[row(d), row(f), pl.BlockSpec((8, d), lambda i: (0, 0))],
        out_shape=[jax.ShapeDtypeStruct((t, d), F32), jax.ShapeDtypeStruct((t, f), BF16),
                   jax.ShapeDtypeStruct((8, d), F32)],
        compiler_params=_cp(("arbitrary",)),
    )(dho, h, a, g, wup, wdown)


def _pool_bwd(dho, h, pooled, g, poolw, scale, *, tm=256):
    t, d = h.shape
    tm = min(tm, t)
    ng = len(POOL_WINDOWS)
    cg = d // ng
    nsteps = t // tm

    def body(do_ref, dn_ref, h_ref, p_ref, g_ref, w_ref, s_ref, dh_ref, dw_ref, ds_ref, dg_ref, ext):
        i = pl.program_id(0)

        @pl.when(i == 0)
        def _():
            dw_ref[...] = jnp.zeros_like(dw_ref)
            ds_ref[...] = jnp.zeros_like(ds_ref)
            dg_ref[...] = jnp.zeros_like(dg_ref)
        dho_v = do_ref[...]
        sv = s_ref[...]
        dyp = (dho_v * sv).astype(BF16)
        dyp_halo = (dn_ref[...] * sv).astype(BF16)
        inv = _pool_inv_count(i, tm)
        tnext = ((i + 1) * tm + lax.broadcasted_iota(jnp.int32, (POOL_HALO, 1), 0) + 1).astype(F32)
        last = i == nsteps - 1
        ypre_parts, dpooled_parts = [], []
        for gi, w in enumerate(POOL_WINDOWS):
            cs = slice(gi * cg, (gi + 1) * cg)
            pg = p_ref[:, cs]
            ypre_parts.append(_dot(pg, w_ref[gi]))
            dw_ref[gi] += _dot_tn(pg, dyp[:, cs])
            dpool = _dot_nt(dyp[:, cs], w_ref[gi])
            dpooled_parts.append(dpool)
            ext[0:tm, cs] = dpool * inv[gi]
            dpool_halo = _dot_nt(dyp_halo[:, cs], w_ref[gi]) * (1.0 / jnp.minimum(tnext, float(w)))
            ext[tm:tm + POOL_HALO, cs] = jnp.where(last, 0.0, dpool_halo)
        ds_ref[...] += _rows8(dho_v * jnp.concatenate(ypre_parts, axis=1))
        dn_parts = []
        for gi, w in enumerate(POOL_WINDOWS):
            cs = slice(gi * cg, (gi + 1) * cg)
            s = ext[0:tm, cs]
            for j in range(1, w):
                s = s + ext[j:j + tm, cs]
            dn_parts.append(s - dpooled_parts[gi])
        dh, dg = _norm_bwd(jnp.concatenate(dn_parts, axis=1), h_ref[...], g_ref[...])
        dh_ref[...] = dho_v + dh
        dg_ref[...] += dg

    row = lambda: pl.BlockSpec((tm, d), lambda i: (i, 0))
    acc8 = lambda: pl.BlockSpec((8, d), lambda i: (0, 0))
    return pl.pallas_call(
        body, name="pool_bwd", grid=(nsteps,),
        in_specs=[row(), pl.BlockSpec((POOL_HALO, d), _next_halo(tm, POOL_HALO, t)), row(), row(),
                  _const_spec((1, d)), _const_spec(poolw.shape), _const_spec((1, d))],
        out_specs=[row(), pl.BlockSpec((ng, cg, cg), lambda i: (0, 0, 0)), acc8(), acc8()],
        out_shape=[jax.ShapeDtypeStruct((t, d), F32), jax.ShapeDtypeStruct((ng, cg, cg), F32),
                   jax.ShapeDtypeStruct((8, d), F32), jax.ShapeDtypeStruct((8, d), F32)],
        scratch_shapes=[pltpu.VMEM((tm + POOL_HALO, d), F32)],
        compiler_params=_cp(("arbitrary",)),
    )(dho, dho, h, pooled, g, poolw, scale)


def _outproj_bwd(dh, o, wout, *, tm=512):
    t, d = dh.shape
    tm = min(tm, t)

    def body(dh_ref, o_ref, w_ref, da_ref, dat_ref, dc_ref):
        dhb = dh_ref[...].astype(BF16)
        dc_ref[...] = _dot_nt(dhb, w_ref[ATTN_W:, :])
        for p in range(ATTN_W // PAIR):
            datt = _dot_nt(dhb, w_ref[p * PAIR:(p + 1) * PAIR, :])
            prod = datt * o_ref[:, p * PAIR:(p + 1) * PAIR].astype(F32)
            for hh in range(2):
                lane, head, aux = _head_lanes(hh)
                delta = jnp.sum(jnp.where(head, prod, 0.0), axis=1, keepdims=True)
                aug = _put_pieces(lane, aux + AUX_BIAS, -delta, jnp.where(head, datt, 0.0))
                da_ref[2 * p + hh] = aug.astype(BF16)
                dat_ref[2 * p + hh] = aug.T.astype(BF16)

    row = lambda n_: pl.BlockSpec((tm, n_), lambda i: (i, 0))
    return pl.pallas_call(
        body, name="outproj_bwd", grid=(t // tm,),
        in_specs=[row(d), row(ATTN_W), _const_spec(wout.shape)],
        out_specs=[pl.BlockSpec((N_HEADS, tm, PAIR), lambda i: (0, i, 0)),
                   pl.BlockSpec((N_HEADS, PAIR, tm), lambda i: (0, 0, i)), row(CONV_CH)],
        out_shape=[jax.ShapeDtypeStruct((N_HEADS, t, PAIR), BF16), jax.ShapeDtypeStruct((N_HEADS, PAIR, t), BF16),
                   jax.ShapeDtypeStruct((t, CONV_CH), F32)],
        compiler_params=_cp(("parallel",)),
    )(dh, o, wout)


def _conv_bwd(bcx, dcv, conv_w, *, tm=512):
    t = bcx.shape[0]
    tm = min(tm, t)
    ch = CONV_CH
    nsteps = t // tm

    def body(b_ref, c_ref, x_ref, hc_ref, hx_ref, d_ref, nb_ref, nd_ref, w_ref, o_ref, dw_ref, ext_u, ext_d):
        i = pl.program_id(0)

        @pl.when(i == 0)
        def _():
            dw_ref[...] = jnp.zeros_like(dw_ref)
        b, c, x, dcv_v = b_ref[...], c_ref[...], x_ref[...], d_ref[...]
        ext_u[0:CONV_HALO, :] = jnp.where(i == 0, 0.0, hc_ref[...] * hx_ref[...])
        ext_u[CONV_HALO:CONV_HALO + tm, :] = c * x
        dconv = dcv_v * b
        ext_d[0:tm, :] = dconv
        ext_d[tm:tm + CONV_HALO, :] = jnp.where(i == nsteps - 1, 0.0, nd_ref[...] * nb_ref[...])
        u = [ext_u[CONV_HALO - 2 + k:CONV_HALO - 2 + k + tm, :] for k in range(3)]
        conv = w_ref[0:1, :] * u[0] + w_ref[1:2, :] * u[1] + w_ref[2:3, :] * u[2]
        du = (w_ref[2:3, :] * dconv + w_ref[1:2, :] * ext_d[1:1 + tm, :] + w_ref[0:1, :] * ext_d[2:2 + tm, :])
        o_ref[:, 0:ch] = (dcv_v * conv).astype(BF16)
        o_ref[:, ch:2 * ch] = (du * x).astype(BF16)
        o_ref[:, 2 * ch:3 * ch] = (du * c).astype(BF16)
        for k in range(3):
            dw_ref[k] += _rows8(dconv * u[k])

    col = lambda k: pl.BlockSpec((tm, ch), lambda i: (i, k))
    prev = lambda k: pl.BlockSpec((CONV_HALO, ch), lambda i: (_prev_halo(tm, CONV_HALO)(i)[0], k))
    nxt = lambda k: pl.BlockSpec((CONV_HALO, ch), lambda i: (_next_halo(tm, CONV_HALO, t)(i)[0], k))
    return pl.pallas_call(
        body, name="conv_bwd", grid=(nsteps,),
        in_specs=[col(0), col(1), col(2), prev(1), prev(2), col(0), nxt(0), nxt(0), _const_spec((8, ch))],
        out_specs=[pl.BlockSpec((tm, 3 * ch), lambda i: (i, 0)), pl.BlockSpec((3, 8, ch), lambda i: (0, 0, 0))],
        out_shape=[jax.ShapeDtypeStruct((t, 3 * ch), BF16), jax.ShapeDtypeStruct((3, 8, ch), F32)],
        scratch_shapes=[pltpu.VMEM((CONV_HALO + tm, ch), F32), pltpu.VMEM((tm + CONV_HALO, ch), F32)],
        compiler_params=_cp(("arbitrary",)),
    )(bcx, bcx, bcx, bcx, bcx, dcv, bcx, dcv, conv_w)


def _attn_bwd(q_bwd, do_aug, q_bwd_t, do_aug_t, k_aug, v_aug, gblocks, *, tq=512):
    t = q_bwd.shape[1]
    tq = min(tq, t)
    tk = tq
    nq = t // tq
    n_pairs = ATTN_W // PAIR
    n_g = len(gblocks)

    def body(q_ref, do_ref, qt_ref, dot_ref, k_ref, v_ref, *rest):
        dq_ref, dqx_ref, dk_ref, dkx_ref, dv_ref = rest[n_g:n_g + 5]
        dq_scr = rest[2 * n_g + 5]
        scatter = _Exchange(rest[:n_g], rest[n_g + 5:2 * n_g + 5], *rest[2 * n_g + 6:], gather=False)
        j = pl.program_id(1)

        @pl.when((pl.program_id(0) == 0) & (j == 0))
        def _():
            scatter.start()

        @pl.when(j == 0)
        def _():
            dq_scr[...] = jnp.zeros_like(dq_scr)
        row = lax.broadcasted_iota(jnp.int32, (tq, tk), 0)
        col = lax.broadcasted_iota(jnp.int32, (tq, tk), 1)
        k = [k_ref[0], k_ref[1]]
        v = [v_ref[0], v_ref[1]]

        def step(i, carry, diag):
            qs = pl.multiple_of(i * tq, tq)
            out = []
            for hh in range(2):
                dk_a, dv_a = carry[hh]
                q = q_ref[hh, pl.ds(qs, tq), :]
                dov = do_ref[hh, pl.ds(qs, tq), :]
                p = jnp.exp2(_dot_nt(q, k[hh]))
                if diag:
                    p = jnp.where(col <= row, p, 0.0)
                ds = (p * _dot_nt(dov, v[hh])).astype(BF16)
                dv_a = dv_a + _dot(dot_ref[hh, :, pl.ds(qs, tq)], p.astype(BF16))
                dk_a = dk_a + _dot(qt_ref[hh, :, pl.ds(qs, tq)], ds)
                dq_scr[hh, pl.ds(qs, tq), :] += _dot(ds, k[hh])
                out.append((dk_a, dv_a))
            return tuple(out)

        zero = (jnp.zeros((PAIR, tk), F32), jnp.zeros((PAIR, tk), F32))
        carry = step(j, (zero, zero), True)
        (dk0, dv0), (dk1, dv1) = lax.fori_loop(j + 1, nq, functools.partial(step, diag=False), carry)
        first_t = lax.broadcasted_iota(jnp.int32, (PAIR, 1), 0) < HEAD_DIM
        first = lax.broadcasted_iota(jnp.int32, (1, PAIR), 1) < HEAD_DIM
        dk_ref[...] = (jnp.where(first_t, dk0, dk1).T * (1.0 / LOG2E)).astype(BF16)
        dkx_ref[...] = jnp.where(first_t, dk1, dk0).T
        dv_ref[...] = jnp.where(first_t, dv0, dv1).T.astype(BF16)

        @pl.when(j == nq - 1)
        def _():
            dq_ref[...] = (jnp.where(first, dq_scr[0], dq_scr[1]) * Q_SCALE).astype(BF16)
            dqx_ref[...] = jnp.where(first, dq_scr[1], dq_scr[0])

        @pl.when((pl.program_id(0) == n_pairs - 1) & (j == nq - 1))
        def _():
            scatter.wait()

    resident = lambda: pl.BlockSpec((2, t, PAIR), lambda p, j: (p, 0, 0), pipeline_mode=pl.Buffered(1))
    resident_t = lambda: pl.BlockSpec((2, PAIR, t), lambda p, j: (p, 0, 0), pipeline_mode=pl.Buffered(1))
    kv_in = lambda: pl.BlockSpec((2, tk, PAIR), lambda p, j: (p, j, 0))
    whole = lambda: pl.BlockSpec((t, PAIR), lambda p, j: (0, p))
    tile = lambda: pl.BlockSpec((tk, PAIR), lambda p, j: (j, p))
    b16 = jax.ShapeDtypeStruct((t, ATTN_W), BF16)
    f32 = jax.ShapeDtypeStruct((t, ATTN_W), F32)
    res = pl.pallas_call(
        body, name="attn_bwd", grid=(n_pairs, nq),
        in_specs=[resident(), resident(), resident_t(), resident_t(), kv_in(), kv_in()] + [HBM_SPEC] * n_g,
        out_specs=[whole(), whole(), tile(), tile(), tile()] + [HBM_SPEC] * n_g,
        out_shape=[b16, f32, b16, f32, b16] + [jax.ShapeDtypeStruct(g.shape, g.dtype) for g in gblocks],
        scratch_shapes=[pltpu.VMEM((2, t, PAIR), F32)] + _Exchange.scratch(n_g),
        compiler_params=_cp(("arbitrary", "arbitrary")),
    )(q_bwd, do_aug, q_bwd_t, do_aug_t, k_aug, v_aug, *gblocks)
    return res[:5], res[5:]


def _fgate_bwd(dqx, dkx, sgate, *, tm=256):
    t = sgate.shape[0]
    tm = min(tm, t)
    nsteps = t // tm

    def body(dq_ref, dk_ref, sg_ref, df_ref, dbf_ref, carry):
        @pl.when(pl.program_id(0) == 0)
        def _():
            carry[...] = jnp.zeros_like(carry)
            dbf_ref[...] = jnp.zeros_like(dbf_ref)
        lane = lax.broadcasted_iota(jnp.int32, (ATTN_W, F_PAD), 0)
        head = lax.broadcasted_iota(jnp.int32, (ATTN_W, F_PAD), 1)
        aux = (head // 2) * PAIR + HEAD_DIM * (1 - head % 2)
        valid = head < N_HEADS
        pick_r = (valid & (lane == aux + AUX_ROWSUM)).astype(F32)
        pick_c = (valid & (lane == aux + AUX_BIAS)).astype(F32)
        hp = lax.Precision.HIGHEST
        dcum = (jnp.dot(dq_ref[...], pick_r, preferred_element_type=F32, precision=hp)
                + jnp.dot(dk_ref[...], pick_c, preferred_element_type=F32, precision=hp))
        r = lax.broadcasted_iota(jnp.int32, (tm, tm), 0)
        c = lax.broadcasted_iota(jnp.int32, (tm, tm), 1)
        tri = (c >= r).astype(F32)
        rc = jnp.dot(tri, dcum, preferred_element_type=F32, precision=hp) + carry[...]
        carry[...] = rc[0:1, :]
        df = rc * sg_ref[...]
        df_ref[...] = df.astype(BF16)
        dbf_ref[...] += _rows8(df)

    rev = lambda i: nsteps - 1 - i
    return pl.pallas_call(
        body, name="fgate_bwd", grid=(nsteps,),
        in_specs=[pl.BlockSpec((tm, ATTN_W), lambda i: (rev(i), 0)), pl.BlockSpec((tm, ATTN_W), lambda i: (rev(i), 0)),
                  pl.BlockSpec((tm, F_PAD), lambda i: (rev(i), 0))],
        out_specs=[pl.BlockSpec((tm, F_PAD), lambda i: (rev(i), 0)), pl.BlockSpec((8, F_PAD), lambda i: (0, 0))],
        out_shape=[jax.ShapeDtypeStruct((t, F_PAD), BF16), jax.ShapeDtypeStruct((8, F_PAD), F32)],
        scratch_shapes=[pltpu.VMEM((1, F_PAD), F32)],
        compiler_params=_cp(("arbitrary",)),
    )(dqx, dkx, sgate)


def _inproj_bwd(dq, dk, dv, df, dbcx, dh, x, g, win_p, *, tm=512):
    t, d = x.shape
    tm = min(tm, t)
    n_qkv = 3 * ATTN_W

    def body(dq_ref, dk_ref, dv_ref, df_ref, db_ref, dh_ref, x_ref, g_ref, w_ref, gx_ref, dg_ref):
        @pl.when(pl.program_id(0) == 0)
        def _():
            dg_ref[...] = jnp.zeros_like(dg_ref)
        dn = _dot_nt(df_ref[...], w_ref[:, n_qkv:n_qkv + F_PAD])
        for k, r in enumerate((dq_ref, dk_ref, dv_ref)):
            dn = dn + _dot_nt(r[...], w_ref[:, k * ATTN_W:(k + 1) * ATTN_W])
        for k in range(3):
            c0 = n_qkv + F_PAD + k * CONV_CH
            dn = dn + _dot_nt(db_ref[:, k * CONV_CH:(k + 1) * CONV_CH], w_ref[:, c0:c0 + CONV_CH])
        dx, dg = _norm_bwd(dn, x_ref[...], g_ref[...])
        gx_ref[...] = dh_ref[...] + dx
        dg_ref[...] += dg

    row = lambda n_: pl.BlockSpec((tm, n_), lambda i: (i, 0))
    return pl.pallas_call(
        body, name="inproj_bwd", grid=(t // tm,),
        in_specs=[row(ATTN_W), row(ATTN_W), row(ATTN_W), row(F_PAD), row(3 * CONV_CH), row(d), row(d),
                  _const_spec((1, d)), _const_spec(win_p.shape)],
        out_specs=[row(d), pl.BlockSpec((8, d), lambda i: (0, 0))],
        out_shape=[jax.ShapeDtypeStruct((t, d), F32), jax.ShapeDtypeStruct((8, d), F32)],
        compiler_params=_cp(("arbitrary",)),
    )(dq, dk, dv, df, dbcx, dh, x, g, win_p)


LATE = ("w_out_0", "w_up_0", "w_down_0", "pool_w_1", "w_up_1", "w_down_1")


def _local_step(x, target, gains, b_f, conv_w, pool_scale, win_p, shards):
    d = x.shape[1]
    n0, qkv, flog, bcx = _norm_inproj(x, gains["mix0"], win_p)
    q_aug_t, k_aug, v_aug, v_aug_t, sgate = _fgate_prep(flog, b_f, qkv)
    att, q_bwd, q_bwd_t, gathered = _attn_fwd(q_aug_t, k_aug, v_aug_t, [shards[n] for n in LATE])
    g = dict(zip(LATE, gathered))
    wout = g["w_out_0"].reshape(d, d)
    wup0, wup1 = g["w_up_0"], g["w_up_1"]
    wdown0, wdown1 = g["w_down_0"].reshape(-1, d), g["w_down_1"].reshape(-1, d)
    n_grp = len(POOL_WINDOWS)
    cg = d // n_grp
    poolw = g["pool_w_1"].reshape(N_DEV, n_grp, cg // N_DEV, cg).transpose(1, 0, 2, 3).reshape(n_grp, cg, cg)
    cv = _conv_fwd(bcx, conv_w)
    h1 = _outproj(att, cv, x, wout)
    h2, n1, a0, z0 = _mlp_fwd(h1, gains["ffn0"], wup0, wdown0, name="mlp_fwd0")
    h3, pooled = _pool_fwd(h2, gains["mix1"], poolw, pool_scale)
    h4, n3, a1, z1 = _mlp_fwd(h3, gains["ffn1"], wup1, wdown1, name="mlp_fwd1")
    loss, dh4, dg_final = _loss_bwd(h4, gains["final"], target)
    f = a1.shape[1]
    fb = f // N_DEV
    dh3, da1, dg_ffn1 = _mlp_bwd(dh4, h3, a1, gains["ffn1"], wup1, wdown1, name="mlp_bwd1")
    dwdown1 = _mm_tn(z1, dh4, name="dwdown1", ta=1024, tb=1024, tt=1024, out_dtype=BF16)
    dwup1 = _mm_tn(n3, da1, name="dwup1", ta=d, tb=fb, tt=2048, blocked_out=True, out_dtype=BF16)
    dh2, dpoolw, dscale, dg_mix1 = _pool_bwd(dh3, h2, pooled, gains["mix1"], poolw, pool_scale)
    dh1, da0, dg_ffn0 = _mlp_bwd(dh2, h1, a0, gains["ffn0"], wup0, wdown0, name="mlp_bwd0")
    dwdown0 = _mm_tn(z0, dh2, name="dwdown0", ta=1024, tb=1024, tt=1024, out_dtype=BF16)
    dwup0 = _mm_tn(n1, da0, name="dwup0", ta=d, tb=fb, tt=2048, blocked_out=True, out_dtype=BF16)
    do_aug, do_aug_t, dcv = _outproj_bwd(dh1, att, wout)
    dwout = jnp.concatenate([_mm_tn(att, dh1, name="dwout_att", ta=512, tb=1024, tt=2048, out_dtype=BF16),
                             _mm_tn(cv, dh1, name="dwout_conv", ta=512, tb=1024, tt=2048, out_dtype=BF16)], axis=0)
    dbcx, dconvw = _conv_bwd(bcx, dcv, conv_w)
    gblocks = {
        "w_out_0": dwout.reshape(N_DEV, d // N_DEV, d), "w_up_0": dwup0, "w_up_1": dwup1,
        "w_down_0": dwdown0.reshape(N_DEV, -1, d), "w_down_1": dwdown1.reshape(N_DEV, -1, d),
        "pool_w_1": dpoolw.astype(BF16).reshape(n_grp, N_DEV, cg // N_DEV, cg).transpose(1, 0, 2, 3).reshape(
            N_DEV, n_grp * (cg // N_DEV), cg),
    }
    (dq, dqx, dk, dkx, dv), landed = _attn_bwd(q_bwd, do_aug, q_bwd_t, do_aug_t, k_aug, v_aug,
                                               [gblocks[n] for n in LATE])
    df, dbf = _fgate_bwd(dqx, dkx, sgate)
    grad_x, dg_mix0 = _inproj_bwd(dq, dk, dv, df, dbcx, dh1, x, gains["mix0"], win_p)
    dwin_p = jnp.concatenate(
        [_mm_tn(n0, dq, name="dwin_q", ta=d, tb=512, tt=2048), _mm_tn(n0, dk, name="dwin_k", ta=d, tb=512, tt=2048),
         _mm_tn(n0, dv, name="dwin_v", ta=d, tb=512, tt=2048), _mm_tn(n0, df, name="dwin_f", ta=d, tb=128, tt=2048),
         _mm_tn(n0, dbcx, name="dwin_bcx", ta=d, tb=512, tt=2048)], axis=1)
    small = dict(mix0=dg_mix0, ffn0=dg_ffn0, mix1=dg_mix1, pool_scale=dscale, ffn1=dg_ffn1, final=dg_final,
                 b_f=dbf, conv_w=dconvw)
    return loss, grad_x, dwin_p, dict(zip(LATE, landed)), small


def _mesh_places():
    x, y, c = lax.axis_index("x"), lax.axis_index("y"), lax.axis_index("c")
    chips = [(1 - x, y), (x, 1 - y), (1 - x, 1 - y)]
    return (x, y, c), (x, y, 1 - c), chips


def _all_gather(shards):
    n = len(shards)

    def body(*refs):
        ins, outs = refs[:n], refs[n:2 * n]
        send_sems, recv_sems, local_sems = refs[2 * n:]
        me, sib, chips = _mesh_places()
        c = me[2]

        def copy(ai, k, block, to, src=None):
            dst = outs[ai].at[_slot(*block)]
            return pltpu.make_async_remote_copy(
                src_ref=dst if src is None else src, dst_ref=dst, send_sem=send_sems.at[7 * ai + k],
                recv_sem=recv_sems.at[7 * ai + k], device_id=to, device_id_type=MESH)

        mine = [pltpu.make_async_copy(ins[ai], outs[ai].at[_slot(*me)], local_sems.at[ai]) for ai in range(n)]
        for cp in mine:
            cp.start()
        first = []
        for ai in range(n):
            first.append(copy(ai, 0, me, sib, src=ins[ai]))
            first += [copy(ai, 1 + j, me, (*chip, c), src=ins[ai]) for j, chip in enumerate(chips)]
        for cp in first:
            cp.start()
        passed = []
        for ai in range(n):
            for j, chip in enumerate(chips):
                copy(ai, 1 + j, (*chip, c), me).wait_recv()
                cp = copy(ai, 4 + j, (*chip, c), sib)
                cp.start()
                passed.append(cp)
        for ai in range(n):
            copy(ai, 0, sib, me).wait_recv()
            for j, chip in enumerate(chips):
                copy(ai, 4 + j, (*chip, 1 - c), me).wait_recv()
        for cp in first + passed:
            cp.wait_send()
        for cp in mine:
            cp.wait()

    return pl.pallas_call(
        body, name="all_gather",
        in_specs=[HBM_SPEC] * n, out_specs=[HBM_SPEC] * n,
        out_shape=[jax.ShapeDtypeStruct((N_DEV,) + s.shape, s.dtype) for s in shards],
        scratch_shapes=[pltpu.SemaphoreType.DMA((7 * n,)), pltpu.SemaphoreType.DMA((7 * n,)),
                        pltpu.SemaphoreType.DMA((n,))],
    )(*shards)


def _reduce_scatter(grads):
    n = len(grads)
    shapes = [g.shape[1:] for g in grads]

    def body(*refs):
        gs = refs[:n]
        outs, land_a, land_b = refs[n:2 * n], refs[2 * n:3 * n], refs[3 * n:4 * n]
        sendbufs = refs[4 * n:5 * n]
        sa_send, sa_recv, sb_send, sb_recv, lsem = refs[5 * n:]
        me, sib, chips = _mesh_places()
        x, y, c = me

        def copy_a(ai, q):
            qx, qy = q // 2, q % 2
            return pltpu.make_async_remote_copy(
                src_ref=gs[ai].at[_slot(qx, qy, 1 - c)], dst_ref=land_a[ai].at[q], send_sem=sa_send.at[4 * ai + q],
                recv_sem=sa_recv.at[4 * ai + q], device_id=sib, device_id_type=MESH)

        def copy_b(ai, j):
            return pltpu.make_async_remote_copy(
                src_ref=sendbufs[ai].at[j], dst_ref=land_b[ai].at[j], send_sem=sb_send.at[3 * ai + j],
                recv_sem=sb_recv.at[3 * ai + j], device_id=(*chips[j], c), device_id_type=MESH)

        def load(src, dst):
            cp = pltpu.make_async_copy(src, dst, lsem)
            cp.start()
            cp.wait()

        for ai in range(n):
            for q in range(4):
                copy_a(ai, q).start()
        for ai in range(n):
            def stage(buf_a, buf_b, ai=ai):
                for j, (qx, qy) in enumerate(chips):
                    q = 2 * qx + qy
                    pltpu.make_async_remote_copy(
                        src_ref=land_a[ai].at[0], dst_ref=land_a[ai].at[0], send_sem=sa_send.at[0],
                        recv_sem=sa_recv.at[4 * ai + q], device_id=sib, device_id_type=MESH).wait_recv()
                    load(gs[ai].at[_slot(qx, qy, c)], buf_a)
                    load(land_a[ai].at[q], buf_b)
                    sendbufs[ai][j] = (buf_a[...] + buf_b[...]).astype(BF16)
                    copy_b(ai, j).start()
            pl.run_scoped(stage, pltpu.VMEM(shapes[ai], F32), pltpu.VMEM(shapes[ai], F32))
        for ai in range(n):
            def final(buf_a, buf_b, buf_c, ai=ai):
                q = 2 * x + y
                pltpu.make_async_remote_copy(
                    src_ref=land_a[ai].at[0], dst_ref=land_a[ai].at[0], send_sem=sa_send.at[0],
                    recv_sem=sa_recv.at[4 * ai + q], device_id=sib, device_id_type=MESH).wait_recv()
                load(gs[ai].at[_slot(x, y, c)], buf_a)
                load(land_a[ai].at[q], buf_b)
                acc = buf_a[...] + buf_b[...]
                for j in range(3):
                    copy_b(ai, j).wait_recv()
                    load(land_b[ai].at[j], buf_c)
                    acc = acc + buf_c[...].astype(F32)
                buf_a[...] = acc
                load(buf_a, outs[ai])
            pl.run_scoped(final, pltpu.VMEM(shapes[ai], F32), pltpu.VMEM(shapes[ai], F32), pltpu.VMEM(shapes[ai], BF16))
        for ai in range(n):
            for q in range(4):
                copy_a(ai, q).wait_send()
            for j in range(3):
                copy_b(ai, j).wait_send()

    res = pl.pallas_call(
        body, name="reduce_scatter",
        in_specs=[HBM_SPEC] * n, out_specs=[HBM_SPEC] * (3 * n),
        out_shape=([jax.ShapeDtypeStruct(s, F32) for s in shapes]
                   + [jax.ShapeDtypeStruct((4,) + s, F32) for s in shapes]
                   + [jax.ShapeDtypeStruct((3,) + s, BF16) for s in shapes]),
        scratch_shapes=([pltpu.VMEM((3,) + s, BF16) for s in shapes]
                        + [pltpu.SemaphoreType.DMA((4 * n,)), pltpu.SemaphoreType.DMA((4 * n,)),
                           pltpu.SemaphoreType.DMA((3 * n,)), pltpu.SemaphoreType.DMA((3 * n,)),
                           pltpu.SemaphoreType.DMA(())]),
        compiler_params=pltpu.CompilerParams(vmem_limit_bytes=VMEM_LIMIT),
    )(*grads)
    return res[:n]


SMALL_ROWS = 16


def _small_allreduce(parts):
    n, _, w = parts.shape
    assert n <= SMALL_ROWS

    def body(p_ref, o_ref, gath, send_sems, recv_sems):
        x, y, c = lax.axis_index("x"), lax.axis_index("y"), lax.axis_index("c")
        my = _slot(x, y, c)
        rows = [jnp.sum(p_ref[i], axis=0, keepdims=True) for i in range(n)]
        rows.append(jnp.zeros((SMALL_ROWS - n, w), F32))
        gath[my] = jnp.concatenate(rows, axis=0)
        copies = []
        for k in range(1, N_DEV):
            px, py, pc = x ^ (k >> 2), y ^ ((k >> 1) & 1), c ^ (k & 1)
            cp = pltpu.make_async_remote_copy(
                src_ref=gath.at[my], dst_ref=gath.at[my], send_sem=send_sems.at[k - 1], recv_sem=recv_sems.at[k - 1],
                device_id=(px, py, pc), device_id_type=MESH)
            cp.start()
            copies.append(cp)
        for cp in copies:
            cp.wait()
        acc = gath[0]
        for d in range(1, N_DEV):
            acc = acc + gath[d]
        o_ref[...] = acc

    return pl.pallas_call(
        body, name="small_allreduce",
        in_specs=[VMEM_SPEC], out_specs=VMEM_SPEC,
        out_shape=jax.ShapeDtypeStruct((SMALL_ROWS, w), F32),
        scratch_shapes=[pltpu.VMEM((N_DEV, SMALL_ROWS, w), F32), pltpu.SemaphoreType.DMA((N_DEV - 1,)),
                        pltpu.SemaphoreType.DMA((N_DEV - 1,))],
    )(parts)


def _adamw(g, w, m, v, *, name, tm=256):
    r, c = g.shape
    tm = tm if r % tm == 0 else r
    bc1 = 1.0 - ADAM_B1 ** ADAM_STEP
    bc2 = 1.0 - ADAM_B2 ** ADAM_STEP

    def body(g_ref, w_ref, m_ref, v_ref, d_ref, nm_ref, nv_ref):
        gv = g_ref[...]
        nm = ADAM_B1 * m_ref[...] + (1.0 - ADAM_B1) * gv
        nv = ADAM_B2 * v_ref[...] + (1.0 - ADAM_B2) * jnp.square(gv)
        nm_ref[...] = nm
        nv_ref[...] = nv
        d_ref[...] = -ADAM_LR * ((nm / bc1) / (jnp.sqrt(nv / bc2) + ADAM_EPS) + ADAM_WD * w_ref[...])

    blk = pl.BlockSpec((tm, c), lambda i: (i, 0))
    shp = jax.ShapeDtypeStruct((r, c), F32)
    return pl.pallas_call(
        body, name=name, grid=(r // tm,), in_specs=[blk] * 4, out_specs=[blk] * 3, out_shape=[shp] * 3,
        compiler_params=_cp(("parallel",)),
    )(g, w, m, v)


def _adamw_sum(parts, w, m, v, *, name, tm=128):
    _, r, c = parts.shape
    tm = tm if r % tm == 0 else r
    bc1 = 1.0 - ADAM_B1 ** ADAM_STEP
    bc2 = 1.0 - ADAM_B2 ** ADAM_STEP

    def body(p_ref, w_ref, m_ref, v_ref, g_ref, d_ref, nm_ref, nv_ref):
        gv = p_ref[0].astype(F32)
        for k in range(1, N_DEV):
            gv = gv + p_ref[k].astype(F32)
        g_ref[...] = gv
        nm = ADAM_B1 * m_ref[...] + (1.0 - ADAM_B1) * gv
        nv = ADAM_B2 * v_ref[...] + (1.0 - ADAM_B2) * jnp.square(gv)
        nm_ref[...] = nm
        nv_ref[...] = nv
        d_ref[...] = -ADAM_LR * ((nm / bc1) / (jnp.sqrt(nv / bc2) + ADAM_EPS) + ADAM_WD * w_ref[...])

    blk = pl.BlockSpec((tm, c), lambda i: (i, 0))
    shp = jax.ShapeDtypeStruct((r, c), F32)
    return pl.pallas_call(
        body, name=name, grid=(r // tm,), in_specs=[pl.BlockSpec((N_DEV, tm, c), lambda i: (0, i, 0))] + [blk] * 3,
        out_specs=[blk] * 4, out_shape=[shp] * 4, compiler_params=_cp(("parallel",)),
    )(parts, w, m, v)


BIG = ("w_in_0", "w_out_0", "w_up_0", "w_down_0", "pool_w_1", "w_up_1", "w_down_1")
SMALL = ("norm_mix_0", "norm_ffn_0", "norm_mix_1", "pool_scale_1", "norm_ffn_1", "final_norm", "b_f_0", "conv_w_0")
WEIGHTS = ("norm_mix_0", "w_in_0", "b_f_0", "conv_w_0", "w_out_0", "norm_ffn_0", "w_up_0", "w_down_0", "norm_mix_1",
           "pool_w_1", "pool_scale_1", "norm_ffn_1", "w_up_1", "w_down_1", "final_norm")


def _pad_to(a, rows, cols):
    return jnp.pad(a, ((0, rows - a.shape[0]), (0, cols - a.shape[1])))


def _pack_small(p, width):
    rows = [p[n].reshape(1, -1) for n in SMALL[:6]]
    rows.append(_pad_to(p["b_f_0"].reshape(1, -1), 1, width))
    rows.append(_pad_to(p["conv_w_0"], 3, width))
    return _pad_to(jnp.concatenate(rows, axis=0), SMALL_ROWS, width)


def _unpack_small(a, like):
    out = {n: a[i] for i, n in enumerate(SMALL[:6])}
    out["b_f_0"] = a[6, :like["b_f_0"].shape[0]]
    out["conv_w_0"] = a[7:10, :like["conv_w_0"].shape[1]]
    return out


def kernel(x, norm_mix_0, w_in_0, b_f_0, conv_w_0, w_out_0, norm_ffn_0, w_up_0, w_down_0, norm_mix_1, pool_w_1, pool_scale_1, norm_ffn_1, w_up_1, w_down_1, final_norm, loss_target, m_norm_mix_0, m_w_in_0, m_b_f_0, m_conv_w_0, m_w_out_0, m_norm_ffn_0, m_w_up_0, m_w_down_0, m_norm_mix_1, m_pool_w_1, m_pool_scale_1, m_norm_ffn_1, m_w_up_1, m_w_down_1, m_final_norm, v_norm_mix_0, v_w_in_0, v_b_f_0, v_conv_w_0, v_w_out_0, v_norm_ffn_0, v_w_up_0, v_w_down_0, v_norm_mix_1, v_pool_w_1, v_pool_scale_1, v_norm_ffn_1, v_w_up_1, v_w_down_1, v_final_norm):
    w = dict(norm_mix_0=norm_mix_0, w_in_0=w_in_0, b_f_0=b_f_0, conv_w_0=conv_w_0, w_out_0=w_out_0,
             norm_ffn_0=norm_ffn_0, w_up_0=w_up_0, w_down_0=w_down_0, norm_mix_1=norm_mix_1, pool_w_1=pool_w_1,
             pool_scale_1=pool_scale_1, norm_ffn_1=norm_ffn_1, w_up_1=w_up_1, w_down_1=w_down_1, final_norm=final_norm)
    m = dict(norm_mix_0=m_norm_mix_0, w_in_0=m_w_in_0, b_f_0=m_b_f_0, conv_w_0=m_conv_w_0, w_out_0=m_w_out_0,
             norm_ffn_0=m_norm_ffn_0, w_up_0=m_w_up_0, w_down_0=m_w_down_0, norm_mix_1=m_norm_mix_1,
             pool_w_1=m_pool_w_1, pool_scale_1=m_pool_scale_1, norm_ffn_1=m_norm_ffn_1, w_up_1=m_w_up_1,
             w_down_1=m_w_down_1, final_norm=m_final_norm)
    v = dict(norm_mix_0=v_norm_mix_0, w_in_0=v_w_in_0, b_f_0=v_b_f_0, conv_w_0=v_conv_w_0, w_out_0=v_w_out_0,
             norm_ffn_0=v_norm_ffn_0, w_up_0=v_w_up_0, w_down_0=v_w_down_0, norm_mix_1=v_norm_mix_1,
             pool_w_1=v_pool_w_1, pool_scale_1=v_pool_scale_1, norm_ffn_1=v_norm_ffn_1, w_up_1=v_w_up_1,
             w_down_1=v_w_down_1, final_norm=v_final_norm)
    d = x.shape[-1]
    n_in = w_in_0.shape[1] * N_DEV
    n_qkv = 3 * ATTN_W
    pool_g, pool_rows, pool_c = pool_w_1.shape

    def shard2d(p):
        return {n: (p[n].reshape(pool_g * pool_rows, pool_c) if n == "pool_w_1" else p[n]) for n in BIG}
    w2, m2, v2 = shard2d(w), shard2d(m), shard2d(v)

    conv_cols = conv_w_0.shape[1]
    win_g8, conv_g8 = _all_gather([w_in_0.astype(BF16), _pad_to(conv_w_0, 8, 128)])
    conv_full = conv_g8[:, :, :conv_cols].transpose(1, 0, 2).reshape(8, N_DEV * conv_cols)
    win = win_g8.transpose(1, 0, 2).reshape(d, n_in)
    win_p = jnp.concatenate([win[:, :n_qkv], _pad_to(win[:, n_qkv:n_qkv + N_HEADS], d, F_PAD),
                             win[:, n_qkv + N_HEADS:]], axis=1)

    gains = dict(mix0=norm_mix_0.reshape(1, d), ffn0=norm_ffn_0.reshape(1, d), mix1=norm_mix_1.reshape(1, d),
                 ffn1=norm_ffn_1.reshape(1, d), final=final_norm.reshape(1, d))
    dev = _slot(lax.axis_index("x"), lax.axis_index("y"), lax.axis_index("c"))
    loss8, grad_x, dwin_p, landed, small = _local_step(
        x[0], loss_target[0], gains, _pad_to(b_f_0.reshape(1, -1), 1, F_PAD), conv_full, pool_scale_1.reshape(1, d),
        win_p, {n: w2[n].astype(BF16) for n in LATE})
    loss = lax.psum(loss8[0, 0], ("x", "y", "c"))

    dwin = jnp.concatenate([dwin_p[:, :n_qkv + N_HEADS], dwin_p[:, n_qkv + F_PAD:]], axis=1)
    g_win, = _reduce_scatter([dwin.reshape(d, N_DEV, n_in // N_DEV).transpose(1, 0, 2)])

    parts = jnp.concatenate(
        [small[k][None] for k in ("mix0", "ffn0", "mix1", "pool_scale", "ffn1", "final")]
        + [_pad_to(small["b_f"], 8, d)[None], jnp.pad(small["conv_w"], ((0, 0), (0, 0), (0, d - CONV_CH)))], axis=0)
    tot = _small_allreduce(parts)
    conv_g = lax.dynamic_slice(tot, (7, dev * conv_cols), (3, conv_cols))
    gs = tot.at[7:10].set(_pad_to(conv_g, 3, d))

    grads, deltas, new_m, new_v = {}, {}, {}, {}
    for n in BIG:
        if n in LATE:
            gr, dl, nm, nv = _adamw_sum(landed[n], w2[n], m2[n], v2[n], name="adamw_" + n)
        else:
            gr = g_win
            dl, nm, nv = _adamw(gr, w2[n], m2[n], v2[n], name="adamw_" + n)
        for dst, val in ((grads, gr), (deltas, dl), (new_m, nm), (new_v, nv)):
            dst[n] = val.reshape(w[n].shape)
    dl, nm, nv = _adamw(gs, _pack_small(w, d), _pack_small(m, d), _pack_small(v, d), name="adamw_small")
    for dst, val in ((grads, gs), (deltas, dl), (new_m, nm), (new_v, nv)):
        dst.update(_unpack_small(val, w))
    return (loss, grad_x[None], *[grads[n] for n in WEIGHTS], *[deltas[n] for n in WEIGHTS],
            *[new_m[n] for n in WEIGHTS], *[new_v[n] for n in WEIGHTS])
```

```python
import functools

import jax
import jax.numpy as jnp
from jax import lax
from jax.experimental import pallas as pl
from jax.experimental.pallas import tpu as pltpu

F32 = jnp.float32
BF16 = jnp.bfloat16

N_DEV = 8
N_HEADS = 8
HEAD_DIM = 64
PAIR = 2 * HEAD_DIM
ATTN_W = N_HEADS * HEAD_DIM
CONV_CH = 512
F_PAD = 128
POOL_WINDOWS = (2, 4, 8, 16)
POOL_HALO = 16
CONV_HALO = 8
RMS_EPS = 1e-6
Q_SCALE = HEAD_DIM ** -0.5
LOG2E = 1.4426950408889634
NEG = -1e30
AUX_BIAS = 0
AUX_LSE = 3
AUX_ROWSUM = 6
ADAM_LR, ADAM_B1, ADAM_B2, ADAM_EPS, ADAM_WD, ADAM_STEP = 0.001, 0.9, 0.999, 1e-08, 0.01, 10
MESH = pl.DeviceIdType.MESH
VMEM_LIMIT = 56 * 2**20


def _cp(sem=None, vmem=VMEM_LIMIT, **kw):
    return pltpu.CompilerParams(dimension_semantics=sem, vmem_limit_bytes=vmem, **kw)


def _dot(a, b):
    return jnp.dot(a, b, preferred_element_type=F32)


def _dot_nt(a, b):
    return lax.dot_general(a, b, (((1,), (1,)), ((), ())), preferred_element_type=F32)


def _dot_tn(a, b):
    return lax.dot_general(a, b, (((0,), (0,)), ((), ())), preferred_element_type=F32)


def _rstd(h):
    return lax.rsqrt(jnp.mean(h * h, axis=-1, keepdims=True) + RMS_EPS)


def _rows8(x):
    r, n = x.shape
    return jnp.sum(x.reshape(r // 8, 8, n), axis=0)


def _norm_bwd(dn, h, g):
    r = _rstd(h)
    xhat = h * r
    dy = dn * g
    dh = r * (dy - xhat * jnp.mean(dy * xhat, axis=-1, keepdims=True))
    return dh, _rows8(dn * xhat)


def _const_spec(shape):
    nd = len(shape)
    return pl.BlockSpec(shape, lambda *_: (0,) * nd, pipeline_mode=pl.Buffered(1))


HBM_SPEC = pl.BlockSpec(memory_space=pltpu.HBM)
VMEM_SPEC = pl.BlockSpec(memory_space=pltpu.VMEM)


def _slot(px, py, pc):
    return 4 * px + 2 * py + pc


class _Exchange:
    def __init__(self, srcs, dsts, send_sems, recv_sems, local_sems, gather):
        x, y, c = lax.axis_index("x"), lax.axis_index("y"), lax.axis_index("c")
        me = _slot(x, y, c)
        self.copies = []
        for a, (src, dst) in enumerate(zip(srcs, dsts)):
            self.copies.append(pltpu.make_async_copy(src if gather else src.at[me], dst.at[me], local_sems.at[a]))
            for k in range(1, N_DEV):
                px, py, pc = x ^ (k >> 2), y ^ ((k >> 1) & 1), c ^ (k & 1)
                self.copies.append(pltpu.make_async_remote_copy(
                    src_ref=src if gather else src.at[_slot(px, py, pc)], dst_ref=dst.at[me],
                    send_sem=send_sems.at[(N_DEV - 1) * a + k - 1], recv_sem=recv_sems.at[(N_DEV - 1) * a + k - 1],
                    device_id=(px, py, pc), device_id_type=MESH))

    def start(self):
        for cp in self.copies:
            cp.start()

    def wait(self):
        for cp in self.copies:
            cp.wait()

    @staticmethod
    def scratch(n):
        return [pltpu.SemaphoreType.DMA(((N_DEV - 1) * n,)), pltpu.SemaphoreType.DMA(((N_DEV - 1) * n,)),
                pltpu.SemaphoreType.DMA((n,))]


def _norm_inproj(x, g, win_p, *, tm=512):
    t, d = x.shape
    n_all = win_p.shape[1]
    n_qkv = 3 * ATTN_W
    n_bcx = 3 * CONV_CH
    assert n_all == n_qkv + F_PAD + n_bcx
    tm = min(tm, t)

    def body(x_ref, g_ref, w_ref, n_ref, qkv_ref, f_ref, bcx_ref):
        h = x_ref[...]
        n = (h * _rstd(h) * g_ref[...]).astype(BF16)
        n_ref[...] = n
        for c0 in range(0, n_qkv, 512):
            acc = _dot(n, w_ref[:, c0:c0 + 512])
            if c0 < ATTN_W:
                acc = acc * (Q_SCALE * LOG2E)
            qkv_ref[:, c0:c0 + 512] = acc.astype(BF16)
        f_ref[...] = _dot(n, w_ref[:, n_qkv:n_qkv + F_PAD])
        for c0 in range(0, n_bcx, 512):
            bcx_ref[:, c0:c0 + 512] = _dot(n, w_ref[:, n_qkv + F_PAD + c0:n_qkv + F_PAD + c0 + 512])

    return pl.pallas_call(
        body, name="norm_inproj", grid=(t // tm,),
        in_specs=[pl.BlockSpec((tm, d), lambda i: (i, 0)), _const_spec((1, d)), _const_spec((d, n_all))],
        out_specs=[pl.BlockSpec((tm, d), lambda i: (i, 0)), pl.BlockSpec((tm, n_qkv), lambda i: (i, 0)),
                   pl.BlockSpec((tm, F_PAD), lambda i: (i, 0)), pl.BlockSpec((tm, n_bcx), lambda i: (i, 0))],
        out_shape=[jax.ShapeDtypeStruct((t, d), BF16), jax.ShapeDtypeStruct((t, n_qkv), BF16),
                   jax.ShapeDtypeStruct((t, F_PAD), F32), jax.ShapeDtypeStruct((t, n_bcx), F32)],
        compiler_params=_cp(("parallel",)),
    )(x, g, win_p)


def _head_lanes(h):
    lane = lax.broadcasted_iota(jnp.int32, (1, PAIR), 1)
    hh = h % 2
    return lane, lane // HEAD_DIM == hh, HEAD_DIM * (1 - hh)


def _pieces(col):
    hi = col.astype(BF16).astype(F32)
    r1 = col - hi
    mid = r1.astype(BF16).astype(F32)
    lo = (r1 - mid).astype(BF16).astype(F32)
    return hi, mid, lo


def _put_pieces(lane, first, col, other):
    hi, mid, lo = _pieces(col)
    return jnp.where(lane == first, hi, jnp.where(lane == first + 1, mid, jnp.where(lane == first + 2, lo, other)))


def _fgate_prep(flog, b_f, qkv, *, tm=512):
    t = flog.shape[0]
    tm = min(tm, t)

    def body(f_ref, b_ref, qkv_ref, qat_ref, ka_ref, va_ref, vat_ref, sg_ref, carry):
        @pl.when(pl.program_id(0) == 0)
        def _():
            carry[...] = jnp.zeros_like(carry)
        z = f_ref[...] + b_ref[...]
        e = jnp.exp(-jnp.abs(z))
        logf = jnp.minimum(z, 0.0) - jnp.log(1.0 + e)
        sg_ref[...] = jnp.where(z >= 0, e, 1.0) / (1.0 + e)
        r = lax.broadcasted_iota(jnp.int32, (tm, tm), 0)
        c = lax.broadcasted_iota(jnp.int32, (tm, tm), 1)
        tri = (c <= r).astype(F32)
        cs = jnp.dot(tri, logf, preferred_element_type=F32, precision=lax.Precision.HIGHEST) + carry[...]
        carry[...] = cs[tm - 1:tm, :]
        cs2 = cs * LOG2E
        for h in range(N_HEADS):
            lane, head, aux = _head_lanes(h)
            p0 = (h // 2) * PAIR
            ones = ((lane >= aux + AUX_LSE) & (lane <= aux + AUX_ROWSUM)).astype(F32)
            bias = (lane >= aux + AUX_BIAS) & (lane < aux + AUX_BIAS + 3)
            k_aux = _put_pieces(lane, aux + AUX_BIAS, cs2[:, h:h + 1], ones)
            q_aug = jnp.where(head, qkv_ref[:, p0:p0 + PAIR].astype(F32), jnp.where(bias, -1.0, 0.0))
            v_aug = jnp.where(head, qkv_ref[:, 2 * ATTN_W + p0:2 * ATTN_W + p0 + PAIR].astype(F32),
                              jnp.where(bias, 1.0, 0.0))
            qat_ref[h] = q_aug.T.astype(BF16)
            ka_ref[h] = jnp.where(head, qkv_ref[:, ATTN_W + p0:ATTN_W + p0 + PAIR], k_aux.astype(BF16))
            va_ref[h] = v_aug.astype(BF16)
            vat_ref[h] = v_aug.T.astype(BF16)

    aug = lambda: pl.BlockSpec((N_HEADS, tm, PAIR), lambda i: (0, i, 0))
    aug_t = lambda: pl.BlockSpec((N_HEADS, PAIR, tm), lambda i: (0, 0, i))
    aug_shape = jax.ShapeDtypeStruct((N_HEADS, t, PAIR), BF16)
    aug_t_shape = jax.ShapeDtypeStruct((N_HEADS, PAIR, t), BF16)
    return pl.pallas_call(
        body, name="fgate_prep", grid=(t // tm,),
        in_specs=[pl.BlockSpec((tm, F_PAD), lambda i: (i, 0)), _const_spec((1, F_PAD)),
                  pl.BlockSpec((tm, 3 * ATTN_W), lambda i: (i, 0))],
        out_specs=[aug_t(), aug(), aug(), aug_t(), pl.BlockSpec((tm, F_PAD), lambda i: (i, 0))],
        out_shape=[aug_t_shape, aug_shape, aug_shape, aug_t_shape, jax.ShapeDtypeStruct((t, F_PAD), F32)],
        scratch_shapes=[pltpu.VMEM((1, F_PAD), F32)],
        compiler_params=_cp(("arbitrary",)),
    )(flog, b_f, qkv)


def _put_pieces_t(row, first, vec, other):
    hi, mid, lo = _pieces(vec)
    return jnp.where(row == first, hi, jnp.where(row == first + 1, mid, jnp.where(row == first + 2, lo, other)))


def _attn_fwd(q_aug_t, k_aug, v_aug_t, shards, *, tq=512):
    t = k_aug.shape[1]
    tq = min(tq, t)
    tk = tq
    nq = t // tq
    n_pairs = ATTN_W // PAIR
    n_sh = len(shards)

    def body(qt_ref, k_ref, vt_ref, *rest):
        o_ref, qb_ref, qbt_ref = rest[n_sh:n_sh + 3]
        s_scr = rest[2 * n_sh + 3]
        gather = _Exchange(rest[:n_sh], rest[n_sh + 3:2 * n_sh + 3], *rest[2 * n_sh + 4:], gather=True)
        i = pl.program_id(1)

        @pl.when((pl.program_id(0) == 0) & (i == 0))
        def _():
            gather.start()
        key = lax.broadcasted_iota(jnp.int32, (tk, tq), 0)
        qry = lax.broadcasted_iota(jnp.int32, (tk, tq), 1)
        qt = [qt_ref[0], qt_ref[1]]

        def logits(hh, tile, slot, diag):
            s = _dot(k_ref[hh, pl.ds(pl.multiple_of(tile * tk, tk), tk), :], qt[hh])
            if diag:
                s = jnp.where(key <= qry, s, NEG)
            s_scr[hh, slot] = s
            return jnp.max(s, axis=0, keepdims=True)

        def probs(hh, tile, slot, m, acc, tmax):
            mn = jnp.maximum(m, tmax)
            p = jnp.exp2(s_scr[hh, slot] - mn).astype(BF16)
            acc = jnp.exp2(m - mn) * acc + _dot(vt_ref[hh, :, pl.ds(pl.multiple_of(tile * tk, tk), tk)], p)
            return mn, acc

        def advance(carry, prev, slot, nxt):
            out = []
            for hh in range(2):
                m, acc, tmax = carry[hh]
                m, acc = probs(hh, prev, slot, m, acc, tmax)
                out.append((m, acc, logits(hh, nxt, 1 - slot, False)))
            return tuple(out)

        def two_tiles(jj, carry):
            carry = advance(carry, jnp.where(jj == 0, i, 2 * jj - 1), 0, 2 * jj)
            return advance(carry, 2 * jj, 1, 2 * jj + 1)

        init = tuple((jnp.full((1, tq), NEG, F32), jnp.zeros((PAIR, tq), F32), logits(hh, i, 0, True))
                     for hh in range(2))
        carry = lax.fori_loop(0, i // 2, two_tiles, init)
        odd = i % 2 == 1
        carry = lax.cond(odd, lambda c: advance(c, jnp.where(i == 1, i, i - 2), 0, i - 1), lambda c: c, carry)
        last = jnp.where(i == 0, i, i - 1)
        row = lax.broadcasted_iota(jnp.int32, (PAIR, 1), 0)
        res = []
        for hh in range(2):
            aux = HEAD_DIM * (1 - hh)
            m, acc, tmax = carry[hh]
            m, acc = lax.cond(odd, lambda a: probs(hh, last, 1, *a), lambda a: probs(hh, last, 0, *a), (m, acc, tmax))
            l = acc[aux + AUX_BIAS:aux + AUX_BIAS + 1, :]
            qbt = _put_pieces_t(row, aux + AUX_LSE, -(m + jnp.log2(l)), qt[hh].astype(F32))
            qbt_ref[hh] = qbt.astype(BF16)
            qb_ref[hh] = qbt.T.astype(BF16)
            res.append(acc * (1.0 / l))
        o_ref[...] = jnp.where(row < HEAD_DIM, res[0], res[1]).T.astype(BF16)

        @pl.when((pl.program_id(0) == n_pairs - 1) & (i == nq - 1))
        def _():
            gather.wait()

    res = pl.pallas_call(
        body, name="attn_fwd", grid=(n_pairs, nq),
        in_specs=[pl.BlockSpec((2, PAIR, tq), lambda p, i: (p, 0, i)),
                  pl.BlockSpec((2, t, PAIR), lambda p, i: (p, 0, 0), pipeline_mode=pl.Buffered(1)),
                  pl.BlockSpec((2, PAIR, t), lambda p, i: (p, 0, 0), pipeline_mode=pl.Buffered(1))] + [HBM_SPEC] * n_sh,
        out_specs=[pl.BlockSpec((tq, PAIR), lambda p, i: (i, p)),
                   pl.BlockSpec((2, tq, PAIR), lambda p, i: (p, i, 0)),
                   pl.BlockSpec((2, PAIR, tq), lambda p, i: (p, 0, i))] + [HBM_SPEC] * n_sh,
        out_shape=[jax.ShapeDtypeStruct((t, ATTN_W), BF16), jax.ShapeDtypeStruct((N_HEADS, t, PAIR), BF16),
                   jax.ShapeDtypeStruct((N_HEADS, PAIR, t), BF16)]
        + [jax.ShapeDtypeStruct((N_DEV,) + s.shape, s.dtype) for s in shards],
        scratch_shapes=[pltpu.VMEM((2, 2, tk, tq), F32)] + _Exchange.scratch(n_sh),
        compiler_params=_cp(("arbitrary", "arbitrary")),
    )(q_aug_t, k_aug, v_aug_t, *shards)
    return res[0], res[1], res[2], res[3:]


def _prev_halo(tm, halo):
    return lambda i: (jnp.maximum(i * (tm // halo) - 1, 0), 0)


def _next_halo(tm, halo, t):
    return lambda i: (jnp.minimum((i + 1) * (tm // halo), t // halo - 1), 0)


def _conv_fwd(bcx, conv_w, *, tm=512):
    t = bcx.shape[0]
    tm = min(tm, t)
    ch = CONV_CH

    def body(b_ref, c_ref, x_ref, hc_ref, hx_ref, w_ref, cv_ref, ext):
        first = pl.program_id(0) == 0
        ext[0:CONV_HALO, :] = jnp.where(first, 0.0, hc_ref[...] * hx_ref[...])
        ext[CONV_HALO:CONV_HALO + tm, :] = c_ref[...] * x_ref[...]
        conv = (w_ref[0:1, :] * ext[CONV_HALO - 2:CONV_HALO - 2 + tm, :]
                + w_ref[1:2, :] * ext[CONV_HALO - 1:CONV_HALO - 1 + tm, :]
                + w_ref[2:3, :] * ext[CONV_HALO:CONV_HALO + tm, :])
        cv_ref[...] = (b_ref[...] * conv).astype(BF16)

    col = lambda k: pl.BlockSpec((tm, ch), lambda i: (i, k))
    halo = lambda k: pl.BlockSpec((CONV_HALO, ch), lambda i: (_prev_halo(tm, CONV_HALO)(i)[0], k))
    return pl.pallas_call(
        body, name="conv_fwd", grid=(t // tm,),
        in_specs=[col(0), col(1), col(2), halo(1), halo(2), _const_spec((8, ch))],
        out_specs=pl.BlockSpec((tm, ch), lambda i: (i, 0)),
        out_shape=jax.ShapeDtypeStruct((t, ch), BF16),
        scratch_shapes=[pltpu.VMEM((CONV_HALO + tm, ch), F32)],
        compiler_params=_cp(("parallel",)),
    )(bcx, bcx, bcx, bcx, bcx, conv_w)


def _outproj(att, cv, x, wout, *, tm=512):
    t, d = x.shape
    tm = min(tm, t)

    def body(a_ref, c_ref, x_ref, w_ref, h_ref):
        h_ref[...] = x_ref[...] + _dot(a_ref[...], w_ref[0:ATTN_W, :]) + _dot(c_ref[...], w_ref[ATTN_W:, :])

    return pl.pallas_call(
        body, name="outproj", grid=(t // tm,),
        in_specs=[pl.BlockSpec((tm, ATTN_W), lambda i: (i, 0)), pl.BlockSpec((tm, CONV_CH), lambda i: (i, 0)),
                  pl.BlockSpec((tm, d), lambda i: (i, 0)), _const_spec(wout.shape)],
        out_specs=pl.BlockSpec((tm, d), lambda i: (i, 0)),
        out_shape=jax.ShapeDtypeStruct((t, d), F32),
        compiler_params=_cp(("parallel",)),
    )(att, cv, x, wout)


def _mlp_tile(hh, g_ref, wu_ref, wd_ref, n_ref, a_ref, z_ref):
    n_blk, _, fb = wu_ref.shape
    n = (hh * _rstd(hh) * g_ref[...]).astype(BF16)
    n_ref[...] = n
    acc = hh
    for k in range(n_blk):
        a = _dot(n, wu_ref[k])
        zz = jnp.square(jnp.maximum(a, 0.0)).astype(BF16)
        a_ref[:, k * fb:(k + 1) * fb] = a.astype(BF16)
        z_ref[:, k * fb:(k + 1) * fb] = zz
        acc = acc + _dot(zz, wd_ref[k * fb:(k + 1) * fb, :])
    return acc


def _mlp_fwd(h, g, wup, wdown, *, name, tm=256):
    t, d = h.shape
    n_blk, _, fb = wup.shape
    f = n_blk * fb
    tm = min(tm, t)

    def body(h_ref, g_ref, wu_ref, wd_ref, ho_ref, n_ref, a_ref, z_ref):
        ho_ref[...] = _mlp_tile(h_ref[...], g_ref, wu_ref, wd_ref, n_ref, a_ref, z_ref)

    row = lambda n_: pl.BlockSpec((tm, n_), lambda i: (i, 0))
    return pl.pallas_call(
        body, name=name, grid=(t // tm,),
        in_specs=[row(d), _const_spec((1, d)), _const_spec(wup.shape), _const_spec(wdown.shape)],
        out_specs=[row(d), row(d), row(f), row(f)],
        out_shape=[jax.ShapeDtypeStruct((t, d), F32), jax.ShapeDtypeStruct((t, d), BF16),
                   jax.ShapeDtypeStruct((t, f), BF16), jax.ShapeDtypeStruct((t, f), BF16)],
        compiler_params=_cp(("parallel",)),
    )(h, g, wup, wdown)


def _mlp_fwd_loss(h, g, wup, wdown, g_out, target, *, name, tm=256):
    t, d = h.shape
    n_blk, _, fb = wup.shape
    f = n_blk * fb
    tm = min(tm, t)
    nsteps = t // tm

    def body(h_ref, g_ref, wu_ref, wd_ref, go_ref, y_ref, loss_ref, dh_ref, dg_ref, n_ref, a_ref, z_ref, lacc):
        i = pl.program_id(0)

        @pl.when(i == 0)
        def _():
            lacc[...] = jnp.zeros_like(lacc)
            dg_ref[...] = jnp.zeros_like(dg_ref)
        hv = _mlp_tile(h_ref[...], g_ref, wu_ref, wd_ref, n_ref, a_ref, z_ref)
        gv = go_ref[...]
        r = _rstd(hv)
        xhat = hv * r
        err = xhat * gv - y_ref[...]
        lacc[...] += _rows8(err * err)
        dout = err * (1.0 / d)
        dy = dout * gv
        dg_ref[...] += _rows8(dout * xhat)
        dh_ref[...] = r * (dy - xhat * jnp.mean(dy * xhat, axis=-1, keepdims=True))

        @pl.when(i == nsteps - 1)
        def _():
            loss_ref[...] = jnp.full(loss_ref.shape, (0.5 / d) * jnp.sum(lacc[...]), F32)

    row = lambda n_: pl.BlockSpec((tm, n_), lambda i: (i, 0))
    return pl.pallas_call(
        body, name=name, grid=(nsteps,),
        in_specs=[row(d), _const_spec((1, d)), _const_spec(wup.shape), _const_spec(wdown.shape), _const_spec((1, d)),
                  row(d)],
        out_specs=[pl.BlockSpec((8, 128), lambda i: (0, 0)), row(d), pl.BlockSpec((8, d), lambda i: (0, 0)),
                   row(d), row(f), row(f)],
        out_shape=[jax.ShapeDtypeStruct((8, 128), F32), jax.ShapeDtypeStruct((t, d), F32),
                   jax.ShapeDtypeStruct((8, d), F32), jax.ShapeDtypeStruct((t, d), BF16),
                   jax.ShapeDtypeStruct((t, f), BF16), jax.ShapeDtypeStruct((t, f), BF16)],
        scratch_shapes=[pltpu.VMEM((8, d), F32)],
        compiler_params=_cp(("arbitrary",)),
    )(h, g, wup, wdown, g_out, target)


def _pool_inv_count(i, tm):
    tglob = (i * tm + lax.broadcasted_iota(jnp.int32, (tm, 1), 0) + 1).astype(F32)
    return [1.0 / jnp.minimum(tglob, float(w)) for w in POOL_WINDOWS]


def _pool_fwd(h, g, poolw, scale, *, tm=256):
    t, d = h.shape
    tm = min(tm, t)
    cg = d // len(POOL_WINDOWS)

    def body(h_ref, hh_ref, g_ref, w_ref, s_ref, ho_ref, p_ref, ext):
        i = pl.program_id(0)
        hv = h_ref[...]
        halo = hh_ref[...]
        n = hv * _rstd(hv) * g_ref[...]
        ext[0:POOL_HALO, :] = jnp.where(i == 0, 0.0, halo * _rstd(halo) * g_ref[...])
        ext[POOL_HALO:POOL_HALO + tm, :] = n
        inv = _pool_inv_count(i, tm)
        for gi, w in enumerate(POOL_WINDOWS):
            cs = slice(gi * cg, (gi + 1) * cg)
            s = ext[POOL_HALO:POOL_HALO + tm, cs]
            for j in range(1, w):
                s = s + ext[POOL_HALO - j:POOL_HALO - j + tm, cs]
            pooled = (s * inv[gi] - n[:, cs]).astype(BF16)
            p_ref[:, cs] = pooled
            ho_ref[:, cs] = hv[:, cs] + _dot(pooled, w_ref[gi]) * s_ref[:, cs]

    row = lambda: pl.BlockSpec((tm, d), lambda i: (i, 0))
    return pl.pallas_call(
        body, name="pool_fwd", grid=(t // tm,),
        in_specs=[row(), pl.BlockSpec((POOL_HALO, d), _prev_halo(tm, POOL_HALO)), _const_spec((1, d)),
                  _const_spec(poolw.shape), _const_spec((1, d))],
        out_specs=[row(), row()],
        out_shape=[jax.ShapeDtypeStruct((t, d), F32), jax.ShapeDtypeStruct((t, d), BF16)],
        scratch_shapes=[pltpu.VMEM((POOL_HALO + tm, d), F32)],
        compiler_params=_cp(("parallel",)),
    )(h, h, g, poolw, scale)


def _mm_tn(a, b, *, name, ta, tb, tt, blocked_out=False, out_dtype=F32):
    t, ka = a.shape
    n = b.shape[1]
    ta, tb, tt = min(ta, ka), min(tb, n), min(tt, t)
    nt = t // tt

    def body(a_ref, b_ref, o_ref, acc):
        @pl.when(pl.program_id(2) == 0)
        def _():
            acc[...] = jnp.zeros_like(acc)
        acc[...] += _dot_tn(a_ref[...].astype(BF16), b_ref[...].astype(BF16))

        @pl.when(pl.program_id(2) == nt - 1)
        def _():
            o_ref[...] = acc[...].astype(out_dtype)

    if blocked_out:
        assert ta == ka
        out_shape = jax.ShapeDtypeStruct((n // tb, ka, tb), out_dtype)
        out_spec = pl.BlockSpec((None, ta, tb), lambda i, j, k: (j, i, 0))
    else:
        out_shape = jax.ShapeDtypeStruct((ka, n), out_dtype)
        out_spec = pl.BlockSpec((ta, tb), lambda i, j, k: (i, j))
    return pl.pallas_call(
        body, name=name, grid=(ka // ta, n // tb, nt),
        in_specs=[pl.BlockSpec((tt, ta), lambda i, j, k: (k, i)), pl.BlockSpec((tt, tb), lambda i, j, k: (k, j))],
        out_specs=out_spec, out_shape=out_shape, scratch_shapes=[pltpu.VMEM((ta, tb), F32)],
        compiler_params=_cp(("parallel", "parallel", "arbitrary")),
    )(a, b)


def _mlp_bwd(dho, h, a, g, wup, wdown, *, name, tm=256):
    t, d = h.shape
    n_blk, _, fb = wup.shape
    f = n_blk * fb
    tm = min(tm, t)

    def body(do_ref, h_ref, a_ref, g_ref, wu_ref, wd_ref, dh_ref, da_ref, dg_ref):
        @pl.when(pl.program_id(0) == 0)
        def _():
            dg_ref[...] = jnp.zeros_like(dg_ref)
        dho_v = do_ref[...]
        dob = dho_v.astype(BF16)
        dn = jnp.zeros((tm, d), F32)
        for k in range(n_blk):
            dz = _dot_nt(dob, wd_ref[k * fb:(k + 1) * fb, :])
            da = (dz * (2.0 * jnp.maximum(a_ref[:, k * fb:(k + 1) * fb].astype(F32), 0.0))).astype(BF16)
            da_ref[:, k * fb:(k + 1) * fb] = da
            dn = dn + _dot_nt(da, wu_ref[k])
        dh, dg = _norm_bwd(dn, h_ref[...], g_ref[...])
        dh_ref[...] = dho_v + dh
        dg_ref[...] += dg

    row = lambda n_: pl.BlockSpec((tm, n_), lambda i: (i, 0))
    return pl.pallas_call(
        body, name=name, grid=(t // tm,),
        in_specs=[row(d), row(d), row(f), _const_spec((1, d)), _const_spec(wup.shape), _const_spec(wdown.shape)],
        out_specs=[row(d), row(f), pl.BlockSpec((8, d), lambda i: (0, 0))],
        out_shape=[jax.ShapeDtypeStruct((t, d), F32), jax.ShapeDtypeStruct((t, f), BF16),
                   jax.ShapeDtypeStruct((8, d), F32)],
        compiler_params=_cp(("arbitrary",)),
    )(dho, h, a, g, wup, wdown)


def _pool_bwd(dho, h, pooled, g, poolw, scale, *, tm=256):
    t, d = h.shape
    tm = min(tm, t)
    ng = len(POOL_WINDOWS)
    cg = d // ng
    nsteps = t // tm

    def body(do_ref, dn_ref, h_ref, p_ref, g_ref, w_ref, s_ref, dh_ref, dw_ref, ds_ref, dg_ref, ext):
        i = pl.program_id(0)

        @pl.when(i == 0)
        def _():
            dw_ref[...] = jnp.zeros_like(dw_ref)
            ds_ref[...] = jnp.zeros_like(ds_ref)
            dg_ref[...] = jnp.zeros_like(dg_ref)
        dho_v = do_ref[...]
        sv = s_ref[...]
        dyp = (dho_v * sv).astype(BF16)
        dyp_halo = (dn_ref[...] * sv).astype(BF16)
        inv = _pool_inv_count(i, tm)
        tnext = ((i + 1) * tm + lax.broadcasted_iota(jnp.int32, (POOL_HALO, 1), 0) + 1).astype(F32)
        last = i == nsteps - 1
        ypre_parts, dpooled_parts = [], []
        for gi, w in enumerate(POOL_WINDOWS):
            cs = slice(gi * cg, (gi + 1) * cg)
            pg = p_ref[:, cs]
            ypre_parts.append(_dot(pg, w_ref[gi]))
            dw_ref[gi] += _dot_tn(pg, dyp[:, cs])
            dpool = _dot_nt(dyp[:, cs], w_ref[gi])
            dpooled_parts.append(dpool)
            ext[0:tm, cs] = dpool * inv[gi]
            dpool_halo = _dot_nt(dyp_halo[:, cs], w_ref[gi]) * (1.0 / jnp.minimum(tnext, float(w)))
            ext[tm:tm + POOL_HALO, cs] = jnp.where(last, 0.0, dpool_halo)
        ds_ref[...] += _rows8(dho_v * jnp.concatenate(ypre_parts, axis=1))
        dn_parts = []
        for gi, w in enumerate(POOL_WINDOWS):
            cs = slice(gi * cg, (gi + 1) * cg)
            s = ext[0:tm, cs]
            for j in range(1, w):
                s = s + ext[j:j + tm, cs]
            dn_parts.append(s - dpooled_parts[gi])
        dh, dg = _norm_bwd(jnp.concatenate(dn_parts, axis=1), h_ref[...], g_ref[...])
        dh_ref[...] = dho_v + dh
        dg_ref[...] += dg

    row = lambda: pl.BlockSpec((tm, d), lambda i: (i, 0))
    acc8 = lambda: pl.BlockSpec((8, d), lambda i: (0, 0))
    return pl.pallas_call(
        body, name="pool_bwd", grid=(nsteps,),
        in_specs=[row(), pl.BlockSpec((POOL_HALO, d), _next_halo(tm, POOL_HALO, t)), row(), row(),
                  _const_spec((1, d)), _const_spec(poolw.shape), _const_spec((1, d))],
        out_specs=[row(), pl.BlockSpec((ng, cg, cg), lambda i: (0, 0, 0)), acc8(), acc8()],
        out_shape=[jax.ShapeDtypeStruct((t, d), F32), jax.ShapeDtypeStruct((ng, cg, cg), F32),
                   jax.ShapeDtypeStruct((8, d), F32), jax.ShapeDtypeStruct((8, d), F32)],
        scratch_shapes=[pltpu.VMEM((tm + POOL_HALO, d), F32)],
        compiler_params=_cp(("arbitrary",)),
    )(dho, dho, h, pooled, g, poolw, scale)


def _outproj_bwd(dh, o, wout, *, tm=512):
    t, d = dh.shape
    tm = min(tm, t)

    def body(dh_ref, o_ref, w_ref, da_ref, dat_ref, dc_ref):
        dhb = dh_ref[...].astype(BF16)
        dc_ref[...] = _dot_nt(dhb, w_ref[ATTN_W:, :])
        for p in range(ATTN_W // PAIR):
            datt = _dot_nt(dhb, w_ref[p * PAIR:(p + 1) * PAIR, :])
            prod = datt * o_ref[:, p * PAIR:(p + 1) * PAIR].astype(F32)
            for hh in range(2):
                lane, head, aux = _head_lanes(hh)
                delta = jnp.sum(jnp.where(head, prod, 0.0), axis=1, keepdims=True)
                aug = _put_pieces(lane, aux + AUX_BIAS, -delta, jnp.where(head, datt, 0.0))
                da_ref[2 * p + hh] = aug.astype(BF16)
                dat_ref[2 * p + hh] = aug.T.astype(BF16)

    row = lambda n_: pl.BlockSpec((tm, n_), lambda i: (i, 0))
    return pl.pallas_call(
        body, name="outproj_bwd", grid=(t // tm,),
        in_specs=[row(d), row(ATTN_W), _const_spec(wout.shape)],
        out_specs=[pl.BlockSpec((N_HEADS, tm, PAIR), lambda i: (0, i, 0)),
                   pl.BlockSpec((N_HEADS, PAIR, tm), lambda i: (0, 0, i)), row(CONV_CH)],
        out_shape=[jax.ShapeDtypeStruct((N_HEADS, t, PAIR), BF16), jax.ShapeDtypeStruct((N_HEADS, PAIR, t), BF16),
                   jax.ShapeDtypeStruct((t, CONV_CH), F32)],
        compiler_params=_cp(("parallel",)),
    )(dh, o, wout)


def _conv_bwd(bcx, dcv, conv_w, *, tm=512):
    t = bcx.shape[0]
    tm = min(tm, t)
    ch = CONV_CH
    nsteps = t // tm

    def body(b_ref, c_ref, x_ref, hc_ref, hx_ref, d_ref, nb_ref, nd_ref, w_ref, o_ref, dw_ref, ext_u, ext_d):
        i = pl.program_id(0)

        @pl.when(i == 0)
        def _():
            dw_ref[...] = jnp.zeros_like(dw_ref)
        b, c, x, dcv_v = b_ref[...], c_ref[...], x_ref[...], d_ref[...]
        ext_u[0:CONV_HALO, :] = jnp.where(i == 0, 0.0, hc_ref[...] * hx_ref[...])
        ext_u[CONV_HALO:CONV_HALO + tm, :] = c * x
        dconv = dcv_v * b
        ext_d[0:tm, :] = dconv
        ext_d[tm:tm + CONV_HALO, :] = jnp.where(i == nsteps - 1, 0.0, nd_ref[...] * nb_ref[...])
        u = [ext_u[CONV_HALO - 2 + k:CONV_HALO - 2 + k + tm, :] for k in range(3)]
        conv = w_ref[0:1, :] * u[0] + w_ref[1:2, :] * u[1] + w_ref[2:3, :] * u[2]
        du = (w_ref[2:3, :] * dconv + w_ref[1:2, :] * ext_d[1:1 + tm, :] + w_ref[0:1, :] * ext_d[2:2 + tm, :])
        o_ref[:, 0:ch] = (dcv_v * conv).astype(BF16)
        o_ref[:, ch:2 * ch] = (du * x).astype(BF16)
        o_ref[:, 2 * ch:3 * ch] = (du * c).astype(BF16)
        for k in range(3):
            dw_ref[k] += _rows8(dconv * u[k])

    col = lambda k: pl.BlockSpec((tm, ch), lambda i: (i, k))
    prev = lambda k: pl.BlockSpec((CONV_HALO, ch), lambda i: (_prev_halo(tm, CONV_HALO)(i)[0], k))
    nxt = lambda k: pl.BlockSpec((CONV_HALO, ch), lambda i: (_next_halo(tm, CONV_HALO, t)(i)[0], k))
    return pl.pallas_call(
        body, name="conv_bwd", grid=(nsteps,),
        in_specs=[col(0), col(1), col(2), prev(1), prev(2), col(0), nxt(0), nxt(0), _const_spec((8, ch))],
        out_specs=[pl.BlockSpec((tm, 3 * ch), lambda i: (i, 0)), pl.BlockSpec((3, 8, ch), lambda i: (0, 0, 0))],
        out_shape=[jax.ShapeDtypeStruct((t, 3 * ch), BF16), jax.ShapeDtypeStruct((3, 8, ch), F32)],
        scratch_shapes=[pltpu.VMEM((CONV_HALO + tm, ch), F32), pltpu.VMEM((tm + CONV_HALO, ch), F32)],
        compiler_params=_cp(("arbitrary",)),
    )(bcx, bcx, bcx, bcx, bcx, dcv, bcx, dcv, conv_w)


def _attn_bwd(q_bwd, do_aug, q_bwd_t, do_aug_t, k_aug, v_aug, gblocks, *, tq=512):
    t = q_bwd.shape[1]
    tq = min(tq, t)
    tk = tq
    nq = t // tq
    n_pairs = ATTN_W // PAIR
    n_g = len(gblocks)

    def body(q_ref, do_ref, qt_ref, dot_ref, k_ref, v_ref, *rest):
        dq_ref, dqx_ref, dk_ref, dkx_ref, dv_ref = rest[n_g:n_g + 5]
        dq_scr = rest[2 * n_g + 5]
        scatter = _Exchange(rest[:n_g], rest[n_g + 5:2 * n_g + 5], *rest[2 * n_g + 6:], gather=False)
        j = pl.program_id(1)

        @pl.when((pl.program_id(0) == 0) & (j == 0))
        def _():
            scatter.start()

        @pl.when(j == 0)
        def _():
            dq_scr[...] = jnp.zeros_like(dq_scr)
        row = lax.broadcasted_iota(jnp.int32, (tq, tk), 0)
        col = lax.broadcasted_iota(jnp.int32, (tq, tk), 1)
        k = [k_ref[0], k_ref[1]]
        v = [v_ref[0], v_ref[1]]

        def step(i, carry, diag):
            qs = pl.multiple_of(i * tq, tq)
            out = []
            for hh in range(2):
                dk_a, dv_a = carry[hh]
                q = q_ref[hh, pl.ds(qs, tq), :]
                dov = do_ref[hh, pl.ds(qs, tq), :]
                p = jnp.exp2(_dot_nt(q, k[hh]))
                if diag:
                    p = jnp.where(col <= row, p, 0.0)
                ds = (p * _dot_nt(dov, v[hh])).astype(BF16)
                dv_a = dv_a + _dot(dot_ref[hh, :, pl.ds(qs, tq)], p.astype(BF16))
                dk_a = dk_a + _dot(qt_ref[hh, :, pl.ds(qs, tq)], ds)
                dq_scr[hh, pl.ds(qs, tq), :] += _dot(ds, k[hh])
                out.append((dk_a, dv_a))
            return tuple(out)

        zero = (jnp.zeros((PAIR, tk), F32), jnp.zeros((PAIR, tk), F32))
        carry = step(j, (zero, zero), True)
        (dk0, dv0), (dk1, dv1) = lax.fori_loop(j + 1, nq, functools.partial(step, diag=False), carry)
        first_t = lax.broadcasted_iota(jnp.int32, (PAIR, 1), 0) < HEAD_DIM
        first = lax.broadcasted_iota(jnp.int32, (1, PAIR), 1) < HEAD_DIM
        dk_ref[...] = (jnp.where(first_t, dk0, dk1).T * (1.0 / LOG2E)).astype(BF16)
        dkx_ref[...] = jnp.where(first_t, dk1, dk0).T
        dv_ref[...] = jnp.where(first_t, dv0, dv1).T.astype(BF16)

        @pl.when(j == nq - 1)
        def _():
            dq_ref[...] = (jnp.where(first, dq_scr[0], dq_scr[1]) * Q_SCALE).astype(BF16)
            dqx_ref[...] = jnp.where(first, dq_scr[1], dq_scr[0])

        @pl.when((pl.program_id(0) == n_pairs - 1) & (j == nq - 1))
        def _():
            scatter.wait()

    resident = lambda: pl.BlockSpec((2, t, PAIR), lambda p, j: (p, 0, 0), pipeline_mode=pl.Buffered(1))
    resident_t = lambda: pl.BlockSpec((2, PAIR, t), lambda p, j: (p, 0, 0), pipeline_mode=pl.Buffered(1))
    kv_in = lambda: pl.BlockSpec((2, tk, PAIR), lambda p, j: (p, j, 0))
    whole = lambda: pl.BlockSpec((t, PAIR), lambda p, j: (0, p))
    tile = lambda: pl.BlockSpec((tk, PAIR), lambda p, j: (j, p))
    b16 = jax.ShapeDtypeStruct((t, ATTN_W), BF16)
    f32 = jax.ShapeDtypeStruct((t, ATTN_W), F32)
    res = pl.pallas_call(
        body, name="attn_bwd", grid=(n_pairs, nq),
        in_specs=[resident(), resident(), resident_t(), resident_t(), kv_in(), kv_in()] + [HBM_SPEC] * n_g,
        out_specs=[whole(), whole(), tile(), tile(), tile()] + [HBM_SPEC] * n_g,
        out_shape=[b16, f32, b16, f32, b16] + [jax.ShapeDtypeStruct(g.shape, g.dtype) for g in gblocks],
        scratch_shapes=[pltpu.VMEM((2, t, PAIR), F32)] + _Exchange.scratch(n_g),
        compiler_params=_cp(("arbitrary", "arbitrary")),
    )(q_bwd, do_aug, q_bwd_t, do_aug_t, k_aug, v_aug, *gblocks)
    return res[:5], res[5:]


def _fgate_bwd(dqx, dkx, sgate, *, tm=512):
    t = sgate.shape[0]
    tm = min(tm, t)
    nsteps = t // tm

    def body(dq_ref, dk_ref, sg_ref, df_ref, dbf_ref, carry):
        @pl.when(pl.program_id(0) == 0)
        def _():
            carry[...] = jnp.zeros_like(carry)
            dbf_ref[...] = jnp.zeros_like(dbf_ref)
        lane = lax.broadcasted_iota(jnp.int32, (ATTN_W, F_PAD), 0)
        head = lax.broadcasted_iota(jnp.int32, (ATTN_W, F_PAD), 1)
        aux = (head // 2) * PAIR + HEAD_DIM * (1 - head % 2)
        valid = head < N_HEADS
        pick_r = (valid & (lane == aux + AUX_ROWSUM)).astype(F32)
        pick_c = (valid & (lane == aux + AUX_BIAS)).astype(F32)
        hp = lax.Precision.HIGHEST
        dcum = (jnp.dot(dq_ref[...], pick_r, preferred_element_type=F32, precision=hp)
                + jnp.dot(dk_ref[...], pick_c, preferred_element_type=F32, precision=hp))
        r = lax.broadcasted_iota(jnp.int32, (tm, tm), 0)
        c = lax.broadcasted_iota(jnp.int32, (tm, tm), 1)
        tri = (c >= r).astype(F32)
        rc = jnp.dot(tri, dcum, preferred_element_type=F32, precision=hp) + carry[...]
        carry[...] = rc[0:1, :]
        df = rc * sg_ref[...]
        df_ref[...] = df.astype(BF16)
        dbf_ref[...] += _rows8(df)

    rev = lambda i: nsteps - 1 - i
    return pl.pallas_call(
        body, name="fgate_bwd", grid=(nsteps,),
        in_specs=[pl.BlockSpec((tm, ATTN_W), lambda i: (rev(i), 0)), pl.BlockSpec((tm, ATTN_W), lambda i: (rev(i), 0)),
                  pl.BlockSpec((tm, F_PAD), lambda i: (rev(i), 0))],
        out_specs=[pl.BlockSpec((tm, F_PAD), lambda i: (rev(i), 0)), pl.BlockSpec((8, F_PAD), lambda i: (0, 0))],
        out_shape=[jax.ShapeDtypeStruct((t, F_PAD), BF16), jax.ShapeDtypeStruct((8, F_PAD), F32)],
        scratch_shapes=[pltpu.VMEM((1, F_PAD), F32)],
        compiler_params=_cp(("arbitrary",)),
    )(dqx, dkx, sgate)


def _inproj_bwd(dq, dk, dv, df, dbcx, dh, x, g, win_p, gblock, *, tm=512):
    t, d = x.shape
    tm = min(tm, t)
    nsteps = t // tm
    n_qkv = 3 * ATTN_W

    def body(dq_ref, dk_ref, dv_ref, df_ref, db_ref, dh_ref, x_ref, g_ref, w_ref, gb_ref, gx_ref, dg_ref, land_ref,
             *sems):
        scatter = _Exchange([gb_ref], [land_ref], *sems, gather=False)

        @pl.when(pl.program_id(0) == 0)
        def _():
            scatter.start()
            dg_ref[...] = jnp.zeros_like(dg_ref)
        dn = _dot_nt(df_ref[...], w_ref[:, n_qkv:n_qkv + F_PAD])
        for k, r in enumerate((dq_ref, dk_ref, dv_ref)):
            dn = dn + _dot_nt(r[...], w_ref[:, k * ATTN_W:(k + 1) * ATTN_W])
        for k in range(3):
            c0 = n_qkv + F_PAD + k * CONV_CH
            dn = dn + _dot_nt(db_ref[:, k * CONV_CH:(k + 1) * CONV_CH], w_ref[:, c0:c0 + CONV_CH])
        dx, dg = _norm_bwd(dn, x_ref[...], g_ref[...])
        gx_ref[...] = dh_ref[...] + dx
        dg_ref[...] += dg

        @pl.when(pl.program_id(0) == nsteps - 1)
        def _():
            scatter.wait()

    row = lambda n_: pl.BlockSpec((tm, n_), lambda i: (i, 0))
    return pl.pallas_call(
        body, name="inproj_bwd", grid=(nsteps,),
        in_specs=[row(ATTN_W), row(ATTN_W), row(ATTN_W), row(F_PAD), row(3 * CONV_CH), row(d), row(d),
                  _const_spec((1, d)), _const_spec(win_p.shape), HBM_SPEC],
        out_specs=[row(d), pl.BlockSpec((8, d), lambda i: (0, 0)), HBM_SPEC],
        out_shape=[jax.ShapeDtypeStruct((t, d), F32), jax.ShapeDtypeStruct((8, d), F32),
                   jax.ShapeDtypeStruct(gblock.shape, gblock.dtype)],
        scratch_shapes=_Exchange.scratch(1),
        compiler_params=_cp(("arbitrary",)),
    )(dq, dk, dv, df, dbcx, dh, x, g, win_p, gblock)


LATE = ("w_out_0", "w_up_0", "w_down_0", "pool_w_1", "w_up_1", "w_down_1")


def _local_step(x, target, gains, b_f, conv_w, pool_scale, win_p, shards):
    d = x.shape[1]
    n0, qkv, flog, bcx = _norm_inproj(x, gains["mix0"], win_p)
    q_aug_t, k_aug, v_aug, v_aug_t, sgate = _fgate_prep(flog, b_f, qkv)
    att, q_bwd, q_bwd_t, gathered = _attn_fwd(q_aug_t, k_aug, v_aug_t, [shards[n] for n in LATE])
    g = dict(zip(LATE, gathered))
    wout = g["w_out_0"].reshape(d, d)
    wup0, wup1 = g["w_up_0"], g["w_up_1"]
    wdown0, wdown1 = g["w_down_0"].reshape(-1, d), g["w_down_1"].reshape(-1, d)
    n_grp = len(POOL_WINDOWS)
    cg = d // n_grp
    poolw = g["pool_w_1"].reshape(N_DEV, n_grp, cg // N_DEV, cg).transpose(1, 0, 2, 3).reshape(n_grp, cg, cg)
    cv = _conv_fwd(bcx, conv_w)
    h1 = _outproj(att, cv, x, wout)
    h2, n1, a0, z0 = _mlp_fwd(h1, gains["ffn0"], wup0, wdown0, name="mlp_fwd0")
    h3, pooled = _pool_fwd(h2, gains["mix1"], poolw, pool_scale)
    loss, dh4, dg_final, n3, a1, z1 = _mlp_fwd_loss(h3, gains["ffn1"], wup1, wdown1, gains["final"], target,
                                                    name="mlp_fwd1")
    f = a1.shape[1]
    fb = f // N_DEV
    dh3, da1, dg_ffn1 = _mlp_bwd(dh4, h3, a1, gains["ffn1"], wup1, wdown1, name="mlp_bwd1")
    dwdown1 = _mm_tn(z1, dh4, name="dwdown1", ta=1024, tb=1024, tt=1024, out_dtype=BF16)
    dwup1 = _mm_tn(n3, da1, name="dwup1", ta=d, tb=fb, tt=2048, blocked_out=True, out_dtype=BF16)
    dh2, dpoolw, dscale, dg_mix1 = _pool_bwd(dh3, h2, pooled, gains["mix1"], poolw, pool_scale)
    dh1, da0, dg_ffn0 = _mlp_bwd(dh2, h1, a0, gains["ffn0"], wup0, wdown0, name="mlp_bwd0")
    dwdown0 = _mm_tn(z0, dh2, name="dwdown0", ta=1024, tb=1024, tt=1024, out_dtype=BF16)
    dwup0 = _mm_tn(n1, da0, name="dwup0", ta=d, tb=fb, tt=2048, blocked_out=True, out_dtype=BF16)
    do_aug, do_aug_t, dcv = _outproj_bwd(dh1, att, wout)
    dwout = jnp.concatenate([_mm_tn(att, dh1, name="dwout_att", ta=512, tb=1024, tt=2048, out_dtype=BF16),
                             _mm_tn(cv, dh1, name="dwout_conv", ta=512, tb=1024, tt=2048, out_dtype=BF16)], axis=0)
    dbcx, dconvw = _conv_bwd(bcx, dcv, conv_w)
    gblocks = {
        "w_out_0": dwout.reshape(N_DEV, d // N_DEV, d), "w_up_0": dwup0, "w_up_1": dwup1,
        "w_down_0": dwdown0.reshape(N_DEV, -1, d), "w_down_1": dwdown1.reshape(N_DEV, -1, d),
        "pool_w_1": dpoolw.astype(BF16).reshape(n_grp, N_DEV, cg // N_DEV, cg).transpose(1, 0, 2, 3).reshape(
            N_DEV, n_grp * (cg // N_DEV), cg),
    }
    (dq, dqx, dk, dkx, dv), landed = _attn_bwd(q_bwd, do_aug, q_bwd_t, do_aug_t, k_aug, v_aug,
                                               [gblocks[n] for n in LATE])
    df, dbf = _fgate_bwd(dqx, dkx, sgate)
    dwin = jnp.concatenate(
        [_mm_tn(n0, dq, name="dwin_q", ta=d, tb=512, tt=2048, out_dtype=BF16),
         _mm_tn(n0, dk, name="dwin_k", ta=d, tb=512, tt=2048, out_dtype=BF16),
         _mm_tn(n0, dv, name="dwin_v", ta=d, tb=512, tt=2048, out_dtype=BF16),
         _mm_tn(n0, df, name="dwin_f", ta=d, tb=128, tt=2048, out_dtype=BF16)[:, :N_HEADS],
         _mm_tn(n0, dbcx, name="dwin_bcx", ta=d, tb=512, tt=2048, out_dtype=BF16)], axis=1)
    dwin_blocks = dwin.reshape(d, N_DEV, dwin.shape[1] // N_DEV).transpose(1, 0, 2)
    grad_x, dg_mix0, landed_win = _inproj_bwd(dq, dk, dv, df, dbcx, dh1, x, gains["mix0"], win_p, dwin_blocks)
    small = dict(mix0=dg_mix0, ffn0=dg_ffn0, mix1=dg_mix1, pool_scale=dscale, ffn1=dg_ffn1, final=dg_final,
                 b_f=dbf, conv_w=dconvw)
    return loss, grad_x, dict(zip(LATE + ("w_in_0",), tuple(landed) + (landed_win,))), small


def _mesh_places():
    x, y, c = lax.axis_index("x"), lax.axis_index("y"), lax.axis_index("c")
    chips = [(1 - x, y), (x, 1 - y), (1 - x, 1 - y)]
    return (x, y, c), (x, y, 1 - c), chips


def _all_gather(shards):
    n = len(shards)

    def body(*refs):
        ins, outs = refs[:n], refs[n:2 * n]
        send_sems, recv_sems, local_sems = refs[2 * n:]
        me, sib, chips = _mesh_places()
        c = me[2]

        def copy(ai, k, block, to, src=None):
            dst = outs[ai].at[_slot(*block)]
            return pltpu.make_async_remote_copy(
                src_ref=dst if src is None else src, dst_ref=dst, send_sem=send_sems.at[7 * ai + k],
                recv_sem=recv_sems.at[7 * ai + k], device_id=to, device_id_type=MESH)

        mine = [pltpu.make_async_copy(ins[ai], outs[ai].at[_slot(*me)], local_sems.at[ai]) for ai in range(n)]
        for cp in mine:
            cp.start()
        first = []
        for ai in range(n):
            first.append(copy(ai, 0, me, sib, src=ins[ai]))
            first += [copy(ai, 1 + j, me, (*chip, c), src=ins[ai]) for j, chip in enumerate(chips)]
        for cp in first:
            cp.start()
        passed = []
        for ai in range(n):
            for j, chip in enumerate(chips):
                copy(ai, 1 + j, (*chip, c), me).wait_recv()
                cp = copy(ai, 4 + j, (*chip, c), sib)
                cp.start()
                passed.append(cp)
        for ai in range(n):
            copy(ai, 0, sib, me).wait_recv()
            for j, chip in enumerate(chips):
                copy(ai, 4 + j, (*chip, 1 - c), me).wait_recv()
        for cp in first + passed:
            cp.wait_send()
        for cp in mine:
            cp.wait()

    return pl.pallas_call(
        body, name="all_gather",
        in_specs=[HBM_SPEC] * n, out_specs=[HBM_SPEC] * n,
        out_shape=[jax.ShapeDtypeStruct((N_DEV,) + s.shape, s.dtype) for s in shards],
        scratch_shapes=[pltpu.SemaphoreType.DMA((7 * n,)), pltpu.SemaphoreType.DMA((7 * n,)),
                        pltpu.SemaphoreType.DMA((n,))],
    )(*shards)


SMALL_ROWS = 16


def _small_allreduce(parts):
    n, _, w = parts.shape
    assert n <= SMALL_ROWS

    def body(p_ref, o_ref, gath, send_sems, recv_sems):
        x, y, c = lax.axis_index("x"), lax.axis_index("y"), lax.axis_index("c")
        my = _slot(x, y, c)
        rows = [jnp.sum(p_ref[i], axis=0, keepdims=True) for i in range(n)]
        rows.append(jnp.zeros((SMALL_ROWS - n, w), F32))
        gath[my] = jnp.concatenate(rows, axis=0)
        copies = []
        for k in range(1, N_DEV):
            px, py, pc = x ^ (k >> 2), y ^ ((k >> 1) & 1), c ^ (k & 1)
            cp = pltpu.make_async_remote_copy(
                src_ref=gath.at[my], dst_ref=gath.at[my], send_sem=send_sems.at[k - 1], recv_sem=recv_sems.at[k - 1],
                device_id=(px, py, pc), device_id_type=MESH)
            cp.start()
            copies.append(cp)
        for cp in copies:
            cp.wait()
        acc = gath[0]
        for d in range(1, N_DEV):
            acc = acc + gath[d]
        o_ref[...] = acc

    return pl.pallas_call(
        body, name="small_allreduce",
        in_specs=[VMEM_SPEC], out_specs=VMEM_SPEC,
        out_shape=jax.ShapeDtypeStruct((SMALL_ROWS, w), F32),
        scratch_shapes=[pltpu.VMEM((N_DEV, SMALL_ROWS, w), F32), pltpu.SemaphoreType.DMA((N_DEV - 1,)),
                        pltpu.SemaphoreType.DMA((N_DEV - 1,))],
    )(parts)


def _adamw(g, w, m, v, *, name, tm=256):
    r, c = g.shape
    tm = tm if r % tm == 0 else r
    bc1 = 1.0 - ADAM_B1 ** ADAM_STEP
    bc2 = 1.0 - ADAM_B2 ** ADAM_STEP

    def body(g_ref, w_ref, m_ref, v_ref, d_ref, nm_ref, nv_ref):
        gv = g_ref[...]
        nm = ADAM_B1 * m_ref[...] + (1.0 - ADAM_B1) * gv
        nv = ADAM_B2 * v_ref[...] + (1.0 - ADAM_B2) * jnp.square(gv)
        nm_ref[...] = nm
        nv_ref[...] = nv
        d_ref[...] = -ADAM_LR * ((nm / bc1) / (jnp.sqrt(nv / bc2) + ADAM_EPS) + ADAM_WD * w_ref[...])

    blk = pl.BlockSpec((tm, c), lambda i: (i, 0))
    shp = jax.ShapeDtypeStruct((r, c), F32)
    return pl.pallas_call(
        body, name=name, grid=(r // tm,), in_specs=[blk] * 4, out_specs=[blk] * 3, out_shape=[shp] * 3,
        compiler_params=_cp(("parallel",)),
    )(g, w, m, v)


def _adamw_sum(parts, w, m, v, *, name, tm=128):
    _, r, c = parts.shape
    tm = tm if r % tm == 0 else r
    bc1 = 1.0 - ADAM_B1 ** ADAM_STEP
    bc2 = 1.0 - ADAM_B2 ** ADAM_STEP

    def body(p_ref, w_ref, m_ref, v_ref, g_ref, d_ref, nm_ref, nv_ref):
        gv = p_ref[0].astype(F32)
        for k in range(1, N_DEV):
            gv = gv + p_ref[k].astype(F32)
        g_ref[...] = gv
        nm = ADAM_B1 * m_ref[...] + (1.0 - ADAM_B1) * gv
        nv = ADAM_B2 * v_ref[...] + (1.0 - ADAM_B2) * jnp.square(gv)
        nm_ref[...] = nm
        nv_ref[...] = nv
        d_ref[...] = -ADAM_LR * ((nm / bc1) / (jnp.sqrt(nv / bc2) + ADAM_EPS) + ADAM_WD * w_ref[...])

    blk = pl.BlockSpec((tm, c), lambda i: (i, 0))
    shp = jax.ShapeDtypeStruct((r, c), F32)
    return pl.pallas_call(
        body, name=name, grid=(r // tm,), in_specs=[pl.BlockSpec((N_DEV, tm, c), lambda i: (0, i, 0))] + [blk] * 3,
        out_specs=[blk] * 4, out_shape=[shp] * 4, compiler_params=_cp(("parallel",)),
    )(parts, w, m, v)


BIG = ("w_in_0", "w_out_0", "w_up_0", "w_down_0", "pool_w_1", "w_up_1", "w_down_1")
SMALL = ("norm_mix_0", "norm_ffn_0", "norm_mix_1", "pool_scale_1", "norm_ffn_1", "final_norm", "b_f_0", "conv_w_0")
WEIGHTS = ("norm_mix_0", "w_in_0", "b_f_0", "conv_w_0", "w_out_0", "norm_ffn_0", "w_up_0", "w_down_0", "norm_mix_1",
           "pool_w_1", "pool_scale_1", "norm_ffn_1", "w_up_1", "w_down_1", "final_norm")


def _pad_to(a, rows, cols):
    return jnp.pad(a, ((0, rows - a.shape[0]), (0, cols - a.shape[1])))


def _pack_small(p, width):
    rows = [p[n].reshape(1, -1) for n in SMALL[:6]]
    rows.append(_pad_to(p["b_f_0"].reshape(1, -1), 1, width))
    rows.append(_pad_to(p["conv_w_0"], 3, width))
    return _pad_to(jnp.concatenate(rows, axis=0), SMALL_ROWS, width)


def _unpack_small(a, like):
    out = {n: a[i] for i, n in enumerate(SMALL[:6])}
    out["b_f_0"] = a[6, :like["b_f_0"].shape[0]]
    out["conv_w_0"] = a[7:10, :like["conv_w_0"].shape[1]]
    return out


def kernel(x, norm_mix_0, w_in_0, b_f_0, conv_w_0, w_out_0, norm_ffn_0, w_up_0, w_down_0, norm_mix_1, pool_w_1, pool_scale_1, norm_ffn_1, w_up_1, w_down_1, final_norm, loss_target, m_norm_mix_0, m_w_in_0, m_b_f_0, m_conv_w_0, m_w_out_0, m_norm_ffn_0, m_w_up_0, m_w_down_0, m_norm_mix_1, m_pool_w_1, m_pool_scale_1, m_norm_ffn_1, m_w_up_1, m_w_down_1, m_final_norm, v_norm_mix_0, v_w_in_0, v_b_f_0, v_conv_w_0, v_w_out_0, v_norm_ffn_0, v_w_up_0, v_w_down_0, v_norm_mix_1, v_pool_w_1, v_pool_scale_1, v_norm_ffn_1, v_w_up_1, v_w_down_1, v_final_norm):
    w = dict(norm_mix_0=norm_mix_0, w_in_0=w_in_0, b_f_0=b_f_0, conv_w_0=conv_w_0, w_out_0=w_out_0,
             norm_ffn_0=norm_ffn_0, w_up_0=w_up_0, w_down_0=w_down_0, norm_mix_1=norm_mix_1, pool_w_1=pool_w_1,
             pool_scale_1=pool_scale_1, norm_ffn_1=norm_ffn_1, w_up_1=w_up_1, w_down_1=w_down_1, final_norm=final_norm)
    m = dict(norm_mix_0=m_norm_mix_0, w_in_0=m_w_in_0, b_f_0=m_b_f_0, conv_w_0=m_conv_w_0, w_out_0=m_w_out_0,
             norm_ffn_0=m_norm_ffn_0, w_up_0=m_w_up_0, w_down_0=m_w_down_0, norm_mix_1=m_norm_mix_1,
             pool_w_1=m_pool_w_1, pool_scale_1=m_pool_scale_1, norm_ffn_1=m_norm_ffn_1, w_up_1=m_w_up_1,
             w_down_1=m_w_down_1, final_norm=m_final_norm)
    v = dict(norm_mix_0=v_norm_mix_0, w_in_0=v_w_in_0, b_f_0=v_b_f_0, conv_w_0=v_conv_w_0, w_out_0=v_w_out_0,
             norm_ffn_0=v_norm_ffn_0, w_up_0=v_w_up_0, w_down_0=v_w_down_0, norm_mix_1=v_norm_mix_1,
             pool_w_1=v_pool_w_1, pool_scale_1=v_pool_scale_1, norm_ffn_1=v_norm_ffn_1, w_up_1=v_w_up_1,
             w_down_1=v_w_down_1, final_norm=v_final_norm)
    d = x.shape[-1]
    n_in = w_in_0.shape[1] * N_DEV
    n_qkv = 3 * ATTN_W
    pool_g, pool_rows, pool_c = pool_w_1.shape

    def shard2d(p):
        return {n: (p[n].reshape(pool_g * pool_rows, pool_c) if n == "pool_w_1" else p[n]) for n in BIG}
    w2, m2, v2 = shard2d(w), shard2d(m), shard2d(v)

    conv_cols = conv_w_0.shape[1]
    win_g8, conv_g8 = _all_gather([w_in_0.astype(BF16), _pad_to(conv_w_0, 8, 128)])
    conv_full = conv_g8[:, :, :conv_cols].transpose(1, 0, 2).reshape(8, N_DEV * conv_cols)
    win = win_g8.transpose(1, 0, 2).reshape(d, n_in)
    win_p = jnp.concatenate([win[:, :n_qkv], _pad_to(win[:, n_qkv:n_qkv + N_HEADS], d, F_PAD),
                             win[:, n_qkv + N_HEADS:]], axis=1)

    gains = dict(mix0=norm_mix_0.reshape(1, d), ffn0=norm_ffn_0.reshape(1, d), mix1=norm_mix_1.reshape(1, d),
                 ffn1=norm_ffn_1.reshape(1, d), final=final_norm.reshape(1, d))
    dev = _slot(lax.axis_index("x"), lax.axis_index("y"), lax.axis_index("c"))
    loss8, grad_x, landed, small = _local_step(
        x[0], loss_target[0], gains, _pad_to(b_f_0.reshape(1, -1), 1, F_PAD), conv_full, pool_scale_1.reshape(1, d),
        win_p, {n: w2[n].astype(BF16) for n in LATE})
    loss = lax.psum(loss8[0, 0], ("x", "y", "c"))

    parts = jnp.concatenate(
        [small[k][None] for k in ("mix0", "ffn0", "mix1", "pool_scale", "ffn1", "final")]
        + [_pad_to(small["b_f"], 8, d)[None], jnp.pad(small["conv_w"], ((0, 0), (0, 0), (0, d - CONV_CH)))], axis=0)
    tot = _small_allreduce(parts)
    conv_g = lax.dynamic_slice(tot, (7, dev * conv_cols), (3, conv_cols))
    gs = tot.at[7:10].set(_pad_to(conv_g, 3, d))

    grads, deltas, new_m, new_v = {}, {}, {}, {}
    for n in BIG:
        gr, dl, nm, nv = _adamw_sum(landed[n], w2[n], m2[n], v2[n], name="adamw_" + n)
        for dst, val in ((grads, gr), (deltas, dl), (new_m, nm), (new_v, nv)):
            dst[n] = val.reshape(w[n].shape)
    dl, nm, nv = _adamw(gs, _pack_small(w, d), _pack_small(m, d), _pack_small(v, d), name="adamw_small")
    for dst, val in ((grads, gs), (deltas, dl), (new_m, nm), (new_v, nv)):
        dst.update(_unpack_small(val, w))
    return (loss, grad_x[None], *[grads[n] for n in WEIGHTS], *[deltas[n] for n in WEIGHTS],
            *[new_m[n] for n in WEIGHTS], *[new_v[n] for n in WEIGHTS])
```

```python
import functools

import jax
import jax.numpy as jnp
from jax import lax
from jax.experimental import pallas as pl
from jax.experimental.pallas import tpu as pltpu

F32 = jnp.float32
BF16 = jnp.bfloat16

N_DEV = 8
N_HEADS = 8
HEAD_DIM = 64
PAIR = 2 * HEAD_DIM
ATTN_W = N_HEADS * HEAD_DIM
CONV_CH = 512
F_PAD = 128
POOL_WINDOWS = (2, 4, 8, 16)
POOL_HALO = 16
CONV_HALO = 8
RMS_EPS = 1e-6
Q_SCALE = HEAD_DIM ** -0.5
LOG2E = 1.4426950408889634
NEG = -1e30
AUX_BIAS = 0
AUX_LSE = 3
AUX_ROWSUM = 6
ADAM_LR, ADAM_B1, ADAM_B2, ADAM_EPS, ADAM_WD, ADAM_STEP = 0.001, 0.9, 0.999, 1e-08, 0.01, 10
MESH = pl.DeviceIdType.MESH
VMEM_LIMIT = 56 * 2**20


def _cp(sem=None, vmem=VMEM_LIMIT, **kw):
    return pltpu.CompilerParams(dimension_semantics=sem, vmem_limit_bytes=vmem, **kw)


def _dot(a, b):
    return jnp.dot(a, b, preferred_element_type=F32)


def _dot_nt(a, b):
    return lax.dot_general(a, b, (((1,), (1,)), ((), ())), preferred_element_type=F32)


def _dot_tn(a, b):
    return lax.dot_general(a, b, (((0,), (0,)), ((), ())), preferred_element_type=F32)


def _rstd(h):
    return lax.rsqrt(jnp.mean(h * h, axis=-1, keepdims=True) + RMS_EPS)


def _rows8(x):
    r, n = x.shape
    return jnp.sum(x.reshape(r // 8, 8, n), axis=0)


def _norm_bwd(dn, h, g):
    r = _rstd(h)
    xhat = h * r
    dy = dn * g
    dh = r * (dy - xhat * jnp.mean(dy * xhat, axis=-1, keepdims=True))
    return dh, _rows8(dn * xhat)


def _const_spec(shape):
    nd = len(shape)
    return pl.BlockSpec(shape, lambda *_: (0,) * nd, pipeline_mode=pl.Buffered(1))


HBM_SPEC = pl.BlockSpec(memory_space=pltpu.HBM)
VMEM_SPEC = pl.BlockSpec(memory_space=pltpu.VMEM)


def _slot(px, py, pc):
    return 4 * px + 2 * py + pc


class _Exchange:
    def __init__(self, srcs, dsts, send_sems, recv_sems, local_sems, gather):
        x, y, c = lax.axis_index("x"), lax.axis_index("y"), lax.axis_index("c")
        me = _slot(x, y, c)
        self.copies = []
        for a, (src, dst) in enumerate(zip(srcs, dsts)):
            self.copies.append(pltpu.make_async_copy(src if gather else src.at[me], dst.at[me], local_sems.at[a]))
            for k in range(1, N_DEV):
                px, py, pc = x ^ (k >> 2), y ^ ((k >> 1) & 1), c ^ (k & 1)
                self.copies.append(pltpu.make_async_remote_copy(
                    src_ref=src if gather else src.at[_slot(px, py, pc)], dst_ref=dst.at[me],
                    send_sem=send_sems.at[(N_DEV - 1) * a + k - 1], recv_sem=recv_sems.at[(N_DEV - 1) * a + k - 1],
                    device_id=(px, py, pc), device_id_type=MESH))

    def start(self):
        for cp in self.copies:
            cp.start()

    def wait(self):
        for cp in self.copies:
            cp.wait()

    @staticmethod
    def scratch(n):
        return [pltpu.SemaphoreType.DMA(((N_DEV - 1) * n,)), pltpu.SemaphoreType.DMA(((N_DEV - 1) * n,)),
                pltpu.SemaphoreType.DMA((n,))]


def _norm_inproj(x, g, win_p, *, tm=512):
    t, d = x.shape
    n_all = win_p.shape[1]
    n_qkv = 3 * ATTN_W
    n_bcx = 3 * CONV_CH
    assert n_all == n_qkv + F_PAD + n_bcx
    tm = min(tm, t)

    def body(x_ref, g_ref, w_ref, n_ref, qkv_ref, f_ref, bcx_ref):
        h = x_ref[...]
        n = (h * _rstd(h) * g_ref[...]).astype(BF16)
        n_ref[...] = n
        for c0 in range(0, n_qkv, 512):
            acc = _dot(n, w_ref[:, c0:c0 + 512])
            if c0 < ATTN_W:
                acc = acc * (Q_SCALE * LOG2E)
            qkv_ref[:, c0:c0 + 512] = acc.astype(BF16)
        f_ref[...] = _dot(n, w_ref[:, n_qkv:n_qkv + F_PAD])
        for c0 in range(0, n_bcx, 512):
            bcx_ref[:, c0:c0 + 512] = _dot(n, w_ref[:, n_qkv + F_PAD + c0:n_qkv + F_PAD + c0 + 512])

    return pl.pallas_call(
        body, name="norm_inproj", grid=(t // tm,),
        in_specs=[pl.BlockSpec((tm, d), lambda i: (i, 0)), _const_spec((1, d)), _const_spec((d, n_all))],
        out_specs=[pl.BlockSpec((tm, d), lambda i: (i, 0)), pl.BlockSpec((tm, n_qkv), lambda i: (i, 0)),
                   pl.BlockSpec((tm, F_PAD), lambda i: (i, 0)), pl.BlockSpec((tm, n_bcx), lambda i: (i, 0))],
        out_shape=[jax.ShapeDtypeStruct((t, d), BF16), jax.ShapeDtypeStruct((t, n_qkv), BF16),
                   jax.ShapeDtypeStruct((t, F_PAD), F32), jax.ShapeDtypeStruct((t, n_bcx), F32)],
        compiler_params=_cp(("parallel",)),
    )(x, g, win_p)


def _head_lanes(h):
    lane = lax.broadcasted_iota(jnp.int32, (1, PAIR), 1)
    hh = h % 2
    return lane, lane // HEAD_DIM == hh, HEAD_DIM * (1 - hh)


def _pieces(col):
    hi = col.astype(BF16).astype(F32)
    r1 = col - hi
    mid = r1.astype(BF16).astype(F32)
    lo = (r1 - mid).astype(BF16).astype(F32)
    return hi, mid, lo


def _put_pieces(lane, first, col, other):
    hi, mid, lo = _pieces(col)
    return jnp.where(lane == first, hi, jnp.where(lane == first + 1, mid, jnp.where(lane == first + 2, lo, other)))


def _fgate_prep(flog, b_f, qkv, *, tm=512):
    t = flog.shape[0]
    tm = min(tm, t)

    def body(f_ref, b_ref, qkv_ref, qat_ref, ka_ref, va_ref, vat_ref, sg_ref, carry):
        @pl.when(pl.program_id(0) == 0)
        def _():
            carry[...] = jnp.zeros_like(carry)
        z = f_ref[...] + b_ref[...]
        e = jnp.exp(-jnp.abs(z))
        logf = jnp.minimum(z, 0.0) - jnp.log(1.0 + e)
        sg_ref[...] = jnp.where(z >= 0, e, 1.0) / (1.0 + e)
        r = lax.broadcasted_iota(jnp.int32, (tm, tm), 0)
        c = lax.broadcasted_iota(jnp.int32, (tm, tm), 1)
        tri = (c <= r).astype(F32)
        cs = jnp.dot(tri, logf, preferred_element_type=F32, precision=lax.Precision.HIGHEST) + carry[...]
        carry[...] = cs[tm - 1:tm, :]
        cs2 = cs * LOG2E
        for h in range(N_HEADS):
            lane, head, aux = _head_lanes(h)
            p0 = (h // 2) * PAIR
            ones = ((lane >= aux + AUX_LSE) & (lane <= aux + AUX_ROWSUM)).astype(F32)
            bias = (lane >= aux + AUX_BIAS) & (lane < aux + AUX_BIAS + 3)
            k_aux = _put_pieces(lane, aux + AUX_BIAS, cs2[:, h:h + 1], ones)
            q_aug = jnp.where(head, qkv_ref[:, p0:p0 + PAIR].astype(F32), jnp.where(bias, -1.0, 0.0))
            v_aug = jnp.where(head, qkv_ref[:, 2 * ATTN_W + p0:2 * ATTN_W + p0 + PAIR].astype(F32),
                              jnp.where(bias, 1.0, 0.0))
            qat_ref[h] = q_aug.T.astype(BF16)
            ka_ref[h] = jnp.where(head, qkv_ref[:, ATTN_W + p0:ATTN_W + p0 + PAIR], k_aux.astype(BF16))
            va_ref[h] = v_aug.astype(BF16)
            vat_ref[h] = v_aug.T.astype(BF16)

    aug = lambda: pl.BlockSpec((N_HEADS, tm, PAIR), lambda i: (0, i, 0))
    aug_t = lambda: pl.BlockSpec((N_HEADS, PAIR, tm), lambda i: (0, 0, i))
    aug_shape = jax.ShapeDtypeStruct((N_HEADS, t, PAIR), BF16)
    aug_t_shape = jax.ShapeDtypeStruct((N_HEADS, PAIR, t), BF16)
    return pl.pallas_call(
        body, name="fgate_prep", grid=(t // tm,),
        in_specs=[pl.BlockSpec((tm, F_PAD), lambda i: (i, 0)), _const_spec((1, F_PAD)),
                  pl.BlockSpec((tm, 3 * ATTN_W), lambda i: (i, 0))],
        out_specs=[aug_t(), aug(), aug(), aug_t(), pl.BlockSpec((tm, F_PAD), lambda i: (i, 0))],
        out_shape=[aug_t_shape, aug_shape, aug_shape, aug_t_shape, jax.ShapeDtypeStruct((t, F_PAD), F32)],
        scratch_shapes=[pltpu.VMEM((1, F_PAD), F32)],
        compiler_params=_cp(("arbitrary",)),
    )(flog, b_f, qkv)


def _put_pieces_t(row, first, vec, other):
    hi, mid, lo = _pieces(vec)
    return jnp.where(row == first, hi, jnp.where(row == first + 1, mid, jnp.where(row == first + 2, lo, other)))


def _attn_fwd(q_aug_t, k_aug, v_aug_t, shards, *, tq=1024):
    t = k_aug.shape[1]
    tq = min(tq, t)
    tk = tq // 2
    nq = t // tq
    n_pairs = ATTN_W // PAIR
    n_sh = len(shards)

    def body(qt_ref, k_ref, vt_ref, *rest):
        o_ref, qb_ref, qbt_ref = rest[n_sh:n_sh + 3]
        s_scr = rest[2 * n_sh + 3]
        gather = _Exchange(rest[:n_sh], rest[n_sh + 3:2 * n_sh + 3], *rest[2 * n_sh + 4:], gather=True)
        i = pl.program_id(1)

        @pl.when((pl.program_id(0) == 0) & (i == 0))
        def _():
            gather.start()
        key = lax.broadcasted_iota(jnp.int32, (tk, tq), 0)
        qry = lax.broadcasted_iota(jnp.int32, (tk, tq), 1)
        qt = [qt_ref[0], qt_ref[1]]

        def logits(hh, tile, slot, diag):
            s = _dot(k_ref[hh, pl.ds(pl.multiple_of(tile * tk, tk), tk), :], qt[hh])
            if diag:
                s = jnp.where(key + (tile * tk - i * tq) <= qry, s, NEG)
            s_scr[hh, slot] = s
            return jnp.max(s, axis=0, keepdims=True)

        def probs(hh, tile, slot, m, acc, tmax):
            mn = jnp.maximum(m, tmax)
            p = jnp.exp2(s_scr[hh, slot] - mn).astype(BF16)
            acc = jnp.exp2(m - mn) * acc + _dot(vt_ref[hh, :, pl.ds(pl.multiple_of(tile * tk, tk), tk)], p)
            return mn, acc

        def advance(carry, prev, slot, nxt, diag=False):
            out = []
            for hh in range(2):
                m, acc, tmax = carry[hh]
                m, acc = probs(hh, prev, slot, m, acc, tmax)
                out.append((m, acc, logits(hh, nxt, 1 - slot, diag)))
            return tuple(out)

        def two_tiles(jj, carry):
            carry = advance(carry, jnp.where(jj == 0, 2 * i, 2 * jj - 1), 1, 2 * jj)
            return advance(carry, 2 * jj, 0, 2 * jj + 1)

        init = tuple((jnp.full((1, tq), NEG, F32), jnp.zeros((PAIR, tq), F32), logits(hh, 2 * i + 1, 0, True))
                     for hh in range(2))
        carry = advance(init, 2 * i + 1, 0, 2 * i, diag=True)
        carry = lax.fori_loop(0, i, two_tiles, carry)
        last = jnp.where(i == 0, 2 * i, 2 * i - 1)
        row = lax.broadcasted_iota(jnp.int32, (PAIR, 1), 0)
        res = []
        for hh in range(2):
            aux = HEAD_DIM * (1 - hh)
            m, acc, tmax = carry[hh]
            m, acc = probs(hh, last, 1, m, acc, tmax)
            l = acc[aux + AUX_BIAS:aux + AUX_BIAS + 1, :]
            qbt = _put_pieces_t(row, aux + AUX_LSE, -(m + jnp.log2(l)), qt[hh].astype(F32))
            qbt_ref[hh] = qbt.astype(BF16)
            qb_ref[hh] = qbt.T.astype(BF16)
            res.append(acc * (1.0 / l))
        o_ref[...] = jnp.where(row < HEAD_DIM, res[0], res[1]).T.astype(BF16)

        @pl.when((pl.program_id(0) == n_pairs - 1) & (i == nq - 1))
        def _():
            gather.wait()

    res = pl.pallas_call(
        body, name="attn_fwd", grid=(n_pairs, nq),
        in_specs=[pl.BlockSpec((2, PAIR, tq), lambda p, i: (p, 0, i)),
                  pl.BlockSpec((2, t, PAIR), lambda p, i: (p, 0, 0), pipeline_mode=pl.Buffered(1)),
                  pl.BlockSpec((2, PAIR, t), lambda p, i: (p, 0, 0), pipeline_mode=pl.Buffered(1))] + [HBM_SPEC] * n_sh,
        out_specs=[pl.BlockSpec((tq, PAIR), lambda p, i: (i, p)),
                   pl.BlockSpec((2, tq, PAIR), lambda p, i: (p, i, 0)),
                   pl.BlockSpec((2, PAIR, tq), lambda p, i: (p, 0, i))] + [HBM_SPEC] * n_sh,
        out_shape=[jax.ShapeDtypeStruct((t, ATTN_W), BF16), jax.ShapeDtypeStruct((N_HEADS, t, PAIR), BF16),
                   jax.ShapeDtypeStruct((N_HEADS, PAIR, t), BF16)]
        + [jax.ShapeDtypeStruct((N_DEV,) + s.shape, s.dtype) for s in shards],
        scratch_shapes=[pltpu.VMEM((2, 2, tk, tq), F32)] + _Exchange.scratch(n_sh),
        compiler_params=_cp(("arbitrary", "arbitrary")),
    )(q_aug_t, k_aug, v_aug_t, *shards)
    return res[0], res[1], res[2], res[3:]


def _prev_halo(tm, halo):
    return lambda i: (jnp.maximum(i * (tm // halo) - 1, 0), 0)


def _next_halo(tm, halo, t):
    return lambda i: (jnp.minimum((i + 1) * (tm // halo), t // halo - 1), 0)


def _conv_fwd(bcx, conv_w, *, tm=512):
    t = bcx.shape[0]
    tm = min(tm, t)
    ch = CONV_CH

    def body(b_ref, c_ref, x_ref, hc_ref, hx_ref, w_ref, cv_ref, ext):
        first = pl.program_id(0) == 0
        ext[0:CONV_HALO, :] = jnp.where(first, 0.0, hc_ref[...] * hx_ref[...])
        ext[CONV_HALO:CONV_HALO + tm, :] = c_ref[...] * x_ref[...]
        conv = (w_ref[0:1, :] * ext[CONV_HALO - 2:CONV_HALO - 2 + tm, :]
                + w_ref[1:2, :] * ext[CONV_HALO - 1:CONV_HALO - 1 + tm, :]
                + w_ref[2:3, :] * ext[CONV_HALO:CONV_HALO + tm, :])
        cv_ref[...] = (b_ref[...] * conv).astype(BF16)

    col = lambda k: pl.BlockSpec((tm, ch), lambda i: (i, k))
    halo = lambda k: pl.BlockSpec((CONV_HALO, ch), lambda i: (_prev_halo(tm, CONV_HALO)(i)[0], k))
    return pl.pallas_call(
        body, name="conv_fwd", grid=(t // tm,),
        in_specs=[col(0), col(1), col(2), halo(1), halo(2), _const_spec((8, ch))],
        out_specs=pl.BlockSpec((tm, ch), lambda i: (i, 0)),
        out_shape=jax.ShapeDtypeStruct((t, ch), BF16),
        scratch_shapes=[pltpu.VMEM((CONV_HALO + tm, ch), F32)],
        compiler_params=_cp(("parallel",)),
    )(bcx, bcx, bcx, bcx, bcx, conv_w)


def _outproj(att, cv, x, wout, *, tm=512):
    t, d = x.shape
    tm = min(tm, t)

    def body(a_ref, c_ref, x_ref, w_ref, h_ref):
        h_ref[...] = x_ref[...] + _dot(a_ref[...], w_ref[0:ATTN_W, :]) + _dot(c_ref[...], w_ref[ATTN_W:, :])

    return pl.pallas_call(
        body, name="outproj", grid=(t // tm,),
        in_specs=[pl.BlockSpec((tm, ATTN_W), lambda i: (i, 0)), pl.BlockSpec((tm, CONV_CH), lambda i: (i, 0)),
                  pl.BlockSpec((tm, d), lambda i: (i, 0)), _const_spec(wout.shape)],
        out_specs=pl.BlockSpec((tm, d), lambda i: (i, 0)),
        out_shape=jax.ShapeDtypeStruct((t, d), F32),
        compiler_params=_cp(("parallel",)),
    )(att, cv, x, wout)


def _mlp_tile(hh, g_ref, wu_ref, wd_ref, n_ref, a_ref, z_ref):
    n_blk, _, fb = wu_ref.shape
    n = (hh * _rstd(hh) * g_ref[...]).astype(BF16)
    n_ref[...] = n
    acc = hh
    for k in range(n_blk):
        a = _dot(n, wu_ref[k])
        zz = jnp.square(jnp.maximum(a, 0.0)).astype(BF16)
        a_ref[:, k * fb:(k + 1) * fb] = a.astype(BF16)
        z_ref[:, k * fb:(k + 1) * fb] = zz
        acc = acc + _dot(zz, wd_ref[k * fb:(k + 1) * fb, :])
    return acc


def _mlp_fwd(h, g, wup, wdown, *, name, tm=256):
    t, d = h.shape
    n_blk, _, fb = wup.shape
    f = n_blk * fb
    tm = min(tm, t)

    def body(h_ref, g_ref, wu_ref, wd_ref, ho_ref, n_ref, a_ref, z_ref):
        ho_ref[...] = _mlp_tile(h_ref[...], g_ref, wu_ref, wd_ref, n_ref, a_ref, z_ref)

    row = lambda n_: pl.BlockSpec((tm, n_), lambda i: (i, 0))
    return pl.pallas_call(
        body, name=name, grid=(t // tm,),
        in_specs=[row(d), _const_spec((1, d)), _const_spec(wup.shape), _const_spec(wdown.shape)],
        out_specs=[row(d), row(d), row(f), row(f)],
        out_shape=[jax.ShapeDtypeStruct((t, d), F32), jax.ShapeDtypeStruct((t, d), BF16),
                   jax.ShapeDtypeStruct((t, f), BF16), jax.ShapeDtypeStruct((t, f), BF16)],
        compiler_params=_cp(("parallel",)),
    )(h, g, wup, wdown)


def _mlp_fwd_loss(h, g, wup, wdown, g_out, target, *, name, tm=256):
    t, d = h.shape
    n_blk, _, fb = wup.shape
    f = n_blk * fb
    tm = min(tm, t)
    nsteps = t // tm

    def body(h_ref, g_ref, wu_ref, wd_ref, go_ref, y_ref, loss_ref, dh_ref, dg_ref, n_ref, a_ref, z_ref, lacc):
        i = pl.program_id(0)

        @pl.when(i == 0)
        def _():
            lacc[...] = jnp.zeros_like(lacc)
            dg_ref[...] = jnp.zeros_like(dg_ref)
        hv = _mlp_tile(h_ref[...], g_ref, wu_ref, wd_ref, n_ref, a_ref, z_ref)
        gv = go_ref[...]
        r = _rstd(hv)
        xhat = hv * r
        err = xhat * gv - y_ref[...]
        lacc[...] += _rows8(err * err)
        dout = err * (1.0 / d)
        dy = dout * gv
        dg_ref[...] += _rows8(dout * xhat)
        dh_ref[...] = r * (dy - xhat * jnp.mean(dy * xhat, axis=-1, keepdims=True))

        @pl.when(i == nsteps - 1)
        def _():
            loss_ref[...] = jnp.full(loss_ref.shape, (0.5 / d) * jnp.sum(lacc[...]), F32)

    row = lambda n_: pl.BlockSpec((tm, n_), lambda i: (i, 0))
    return pl.pallas_call(
        body, name=name, grid=(nsteps,),
        in_specs=[row(d), _const_spec((1, d)), _const_spec(wup.shape), _const_spec(wdown.shape), _const_spec((1, d)),
                  row(d)],
        out_specs=[pl.BlockSpec((8, 128), lambda i: (0, 0)), row(d), pl.BlockSpec((8, d), lambda i: (0, 0)),
                   row(d), row(f), row(f)],
        out_shape=[jax.ShapeDtypeStruct((8, 128), F32), jax.ShapeDtypeStruct((t, d), F32),
                   jax.ShapeDtypeStruct((8, d), F32), jax.ShapeDtypeStruct((t, d), BF16),
                   jax.ShapeDtypeStruct((t, f), BF16), jax.ShapeDtypeStruct((t, f), BF16)],
        scratch_shapes=[pltpu.VMEM((8, d), F32)],
        compiler_params=_cp(("arbitrary",)),
    )(h, g, wup, wdown, g_out, target)


def _pool_inv_count(i, tm):
    tglob = (i * tm + lax.broadcasted_iota(jnp.int32, (tm, 1), 0) + 1).astype(F32)
    return [1.0 / jnp.minimum(tglob, float(w)) for w in POOL_WINDOWS]


def _pool_fwd(h, g, poolw, scale, *, tm=256):
    t, d = h.shape
    tm = min(tm, t)
    cg = d // len(POOL_WINDOWS)

    def body(h_ref, hh_ref, g_ref, w_ref, s_ref, ho_ref, p_ref, ext):
        i = pl.program_id(0)
        hv = h_ref[...]
        halo = hh_ref[...]
        n = hv * _rstd(hv) * g_ref[...]
        ext[0:POOL_HALO, :] = jnp.where(i == 0, 0.0, halo * _rstd(halo) * g_ref[...])
        ext[POOL_HALO:POOL_HALO + tm, :] = n
        inv = _pool_inv_count(i, tm)
        for gi, w in enumerate(POOL_WINDOWS):
            cs = slice(gi * cg, (gi + 1) * cg)
            s = ext[POOL_HALO:POOL_HALO + tm, cs]
            for j in range(1, w):
                s = s + ext[POOL_HALO - j:POOL_HALO - j + tm, cs]
            pooled = (s * inv[gi] - n[:, cs]).astype(BF16)
            p_ref[:, cs] = pooled
            ho_ref[:, cs] = hv[:, cs] + _dot(pooled, w_ref[gi]) * s_ref[:, cs]

    row = lambda: pl.BlockSpec((tm, d), lambda i: (i, 0))
    return pl.pallas_call(
        body, name="pool_fwd", grid=(t // tm,),
        in_specs=[row(), pl.BlockSpec((POOL_HALO, d), _prev_halo(tm, POOL_HALO)), _const_spec((1, d)),
                  _const_spec(poolw.shape), _const_spec((1, d))],
        out_specs=[row(), row()],
        out_shape=[jax.ShapeDtypeStruct((t, d), F32), jax.ShapeDtypeStruct((t, d), BF16)],
        scratch_shapes=[pltpu.VMEM((POOL_HALO + tm, d), F32)],
        compiler_params=_cp(("parallel",)),
    )(h, h, g, poolw, scale)


def _mm_tn(a, b, *, name, ta, tb, tt, blocked_out=False, out_dtype=F32):
    t, ka = a.shape
    n = b.shape[1]
    ta, tb, tt = min(ta, ka), min(tb, n), min(tt, t)
    nt = t // tt

    def body(a_ref, b_ref, o_ref, acc):
        @pl.when(pl.program_id(2) == 0)
        def _():
            acc[...] = jnp.zeros_like(acc)
        acc[...] += _dot_tn(a_ref[...].astype(BF16), b_ref[...].astype(BF16))

        @pl.when(pl.program_id(2) == nt - 1)
        def _():
            o_ref[...] = acc[...].astype(out_dtype)

    if blocked_out:
        assert ta == ka
        out_shape = jax.ShapeDtypeStruct((n // tb, ka, tb), out_dtype)
        out_spec = pl.BlockSpec((None, ta, tb), lambda i, j, k: (j, i, 0))
    else:
        out_shape = jax.ShapeDtypeStruct((ka, n), out_dtype)
        out_spec = pl.BlockSpec((ta, tb), lambda i, j, k: (i, j))
    return pl.pallas_call(
        body, name=name, grid=(ka // ta, n // tb, nt),
        in_specs=[pl.BlockSpec((tt, ta), lambda i, j, k: (k, i)), pl.BlockSpec((tt, tb), lambda i, j, k: (k, j))],
        out_specs=out_spec, out_shape=out_shape, scratch_shapes=[pltpu.VMEM((ta, tb), F32)],
        compiler_params=_cp(("parallel", "parallel", "arbitrary")),
    )(a, b)


def _mlp_bwd(dho, h, a, g, wup, wdown, *, name, tm=256):
    t, d = h.shape
    n_blk, _, fb = wup.shape
    f = n_blk * fb
    tm = min(tm, t)

    def body(do_ref, h_ref, a_ref, g_ref, wu_ref, wd_ref, dh_ref, da_ref, dg_ref):
        @pl.when(pl.program_id(0) == 0)
        def _():
            dg_ref[...] = jnp.zeros_like(dg_ref)
        dho_v = do_ref[...]
        dob = dho_v.astype(BF16)
        dn = jnp.zeros((tm, d), F32)
        for k in range(n_blk):
            dz = _dot_nt(dob, wd_ref[k * fb:(k + 1) * fb, :])
            da = (dz * (2.0 * jnp.maximum(a_ref[:, k * fb:(k + 1) * fb].astype(F32), 0.0))).astype(BF16)
            da_ref[:, k * fb:(k + 1) * fb] = da
            dn = dn + _dot_nt(da, wu_ref[k])
        dh, dg = _norm_bwd(dn, h_ref[...], g_ref[...])
        dh_ref[...] = dho_v + dh
        dg_ref[...] += dg

    row = lambda n_: pl.BlockSpec((tm, n_), lambda i: (i, 0))
    return pl.pallas_call(
        body, name=name, grid=(t // tm,),
        in_specs=[row(d), row(d), row(f), _const_spec((1, d)), _const_spec(wup.shape), _const_spec(wdown.shape)],
        out_specs=[row(d), row(f), pl.BlockSpec((8, d), lambda i: (0, 0))],
        out_shape=[jax.ShapeDtypeStruct((t, d), F32), jax.ShapeDtypeStruct((t, f), BF16),
                   jax.ShapeDtypeStruct((8, d), F32)],
        compiler_params=_cp(("arbitrary",)),
    )(dho, h, a, g, wup, wdown)


def _pool_bwd(dho, h, pooled, g, poolw, scale, *, tm=256):
    t, d = h.shape
    tm = min(tm, t)
    ng = len(POOL_WINDOWS)
    cg = d // ng
    nsteps = t // tm

    def body(do_ref, dn_ref, h_ref, p_ref, g_ref, w_ref, s_ref, dh_ref, dw_ref, ds_ref, dg_ref, ext):
        i = pl.program_id(0)

        @pl.when(i == 0)
        def _():
            dw_ref[...] = jnp.zeros_like(dw_ref)
            ds_ref[...] = jnp.zeros_like(ds_ref)
            dg_ref[...] = jnp.zeros_like(dg_ref)
        dho_v = do_ref[...]
        sv = s_ref[...]
        dyp = (dho_v * sv).astype(BF16)
        dyp_halo = (dn_ref[...] * sv).astype(BF16)
        inv = _pool_inv_count(i, tm)
        tnext = ((i + 1) * tm + lax.broadcasted_iota(jnp.int32, (POOL_HALO, 1), 0) + 1).astype(F32)
        last = i == nsteps - 1
        ypre_parts, dpooled_parts = [], []
        for gi, w in enumerate(POOL_WINDOWS):
            cs = slice(gi * cg, (gi + 1) * cg)
            pg = p_ref[:, cs]
            ypre_parts.append(_dot(pg, w_ref[gi]))
            dw_ref[gi] += _dot_tn(pg, dyp[:, cs])
            dpool = _dot_nt(dyp[:, cs], w_ref[gi])
            dpooled_parts.append(dpool)
            ext[0:tm, cs] = dpool * inv[gi]
            dpool_halo = _dot_nt(dyp_halo[:, cs], w_ref[gi]) * (1.0 / jnp.minimum(tnext, float(w)))
            ext[tm:tm + POOL_HALO, cs] = jnp.where(last, 0.0, dpool_halo)
        ds_ref[...] += _rows8(dho_v * jnp.concatenate(ypre_parts, axis=1))
        dn_parts = []
        for gi, w in enumerate(POOL_WINDOWS):
            cs = slice(gi * cg, (gi + 1) * cg)
            s = ext[0:tm, cs]
            for j in range(1, w):
                s = s + ext[j:j + tm, cs]
            dn_parts.append(s - dpooled_parts[gi])
        dh, dg = _norm_bwd(jnp.concatenate(dn_parts, axis=1), h_ref[...], g_ref[...])
        dh_ref[...] = dho_v + dh
        dg_ref[...] += dg

    row = lambda: pl.BlockSpec((tm, d), lambda i: (i, 0))
    acc8 = lambda: pl.BlockSpec((8, d), lambda i: (0, 0))
    return pl.pallas_call(
        body, name="pool_bwd", grid=(nsteps,),
        in_specs=[row(), pl.BlockSpec((POOL_HALO, d), _next_halo(tm, POOL_HALO, t)), row(), row(),
                  _const_spec((1, d)), _const_spec(poolw.shape), _const_spec((1, d))],
        out_specs=[row(), pl.BlockSpec((ng, cg, cg), lambda i: (0, 0, 0)), acc8(), acc8()],
        out_shape=[jax.ShapeDtypeStruct((t, d), F32), jax.ShapeDtypeStruct((ng, cg, cg), F32),
                   jax.ShapeDtypeStruct((8, d), F32), jax.ShapeDtypeStruct((8, d), F32)],
        scratch_shapes=[pltpu.VMEM((tm + POOL_HALO, d), F32)],
        compiler_params=_cp(("arbitrary",)),
    )(dho, dho, h, pooled, g, poolw, scale)


def _outproj_bwd(dh, o, wout, *, tm=512):
    t, d = dh.shape
    tm = min(tm, t)

    def body(dh_ref, o_ref, w_ref, da_ref, dat_ref, dc_ref):
        dhb = dh_ref[...].astype(BF16)
        dc_ref[...] = _dot_nt(dhb, w_ref[ATTN_W:, :])
        for p in range(ATTN_W // PAIR):
            datt = _dot_nt(dhb, w_ref[p * PAIR:(p + 1) * PAIR, :])
            prod = datt * o_ref[:, p * PAIR:(p + 1) * PAIR].astype(F32)
            for hh in range(2):
                lane, head, aux = _head_lanes(hh)
                delta = jnp.sum(jnp.where(head, prod, 0.0), axis=1, keepdims=True)
                aug = _put_pieces(lane, aux + AUX_BIAS, -delta, jnp.where(head, datt, 0.0))
                da_ref[2 * p + hh] = aug.astype(BF16)
                dat_ref[2 * p + hh] = aug.T.astype(BF16)

    row = lambda n_: pl.BlockSpec((tm, n_), lambda i: (i, 0))
    return pl.pallas_call(
        body, name="outproj_bwd", grid=(t // tm,),
        in_specs=[row(d), row(ATTN_W), _const_spec(wout.shape)],
        out_specs=[pl.BlockSpec((N_HEADS, tm, PAIR), lambda i: (0, i, 0)),
                   pl.BlockSpec((N_HEADS, PAIR, tm), lambda i: (0, 0, i)), row(CONV_CH)],
        out_shape=[jax.ShapeDtypeStruct((N_HEADS, t, PAIR), BF16), jax.ShapeDtypeStruct((N_HEADS, PAIR, t), BF16),
                   jax.ShapeDtypeStruct((t, CONV_CH), F32)],
        compiler_params=_cp(("parallel",)),
    )(dh, o, wout)


def _conv_bwd(bcx, dcv, conv_w, *, tm=512):
    t = bcx.shape[0]
    tm = min(tm, t)
    ch = CONV_CH
    nsteps = t // tm

    def body(b_ref, c_ref, x_ref, hc_ref, hx_ref, d_ref, nb_ref, nd_ref, w_ref, o_ref, dw_ref, ext_u, ext_d):
        i = pl.program_id(0)

        @pl.when(i == 0)
        def _():
            dw_ref[...] = jnp.zeros_like(dw_ref)
        b, c, x, dcv_v = b_ref[...], c_ref[...], x_ref[...], d_ref[...]
        ext_u[0:CONV_HALO, :] = jnp.where(i == 0, 0.0, hc_ref[...] * hx_ref[...])
        ext_u[CONV_HALO:CONV_HALO + tm, :] = c * x
        dconv = dcv_v * b
        ext_d[0:tm, :] = dconv
        ext_d[tm:tm + CONV_HALO, :] = jnp.where(i == nsteps - 1, 0.0, nd_ref[...] * nb_ref[...])
        u = [ext_u[CONV_HALO - 2 + k:CONV_HALO - 2 + k + tm, :] for k in range(3)]
        conv = w_ref[0:1, :] * u[0] + w_ref[1:2, :] * u[1] + w_ref[2:3, :] * u[2]
        du = (w_ref[2:3, :] * dconv + w_ref[1:2, :] * ext_d[1:1 + tm, :] + w_ref[0:1, :] * ext_d[2:2 + tm, :])
        o_ref[:, 0:ch] = (dcv_v * conv).astype(BF16)
        o_ref[:, ch:2 * ch] = (du * x).astype(BF16)
        o_ref[:, 2 * ch:3 * ch] = (du * c).astype(BF16)
        for k in range(3):
            dw_ref[k] += _rows8(dconv * u[k])

    col = lambda k: pl.BlockSpec((tm, ch), lambda i: (i, k))
    prev = lambda k: pl.BlockSpec((CONV_HALO, ch), lambda i: (_prev_halo(tm, CONV_HALO)(i)[0], k))
    nxt = lambda k: pl.BlockSpec((CONV_HALO, ch), lambda i: (_next_halo(tm, CONV_HALO, t)(i)[0], k))
    return pl.pallas_call(
        body, name="conv_bwd", grid=(nsteps,),
        in_specs=[col(0), col(1), col(2), prev(1), prev(2), col(0), nxt(0), nxt(0), _const_spec((8, ch))],
        out_specs=[pl.BlockSpec((tm, 3 * ch), lambda i: (i, 0)), pl.BlockSpec((3, 8, ch), lambda i: (0, 0, 0))],
        out_shape=[jax.ShapeDtypeStruct((t, 3 * ch), BF16), jax.ShapeDtypeStruct((3, 8, ch), F32)],
        scratch_shapes=[pltpu.VMEM((CONV_HALO + tm, ch), F32), pltpu.VMEM((tm + CONV_HALO, ch), F32)],
        compiler_params=_cp(("arbitrary",)),
    )(bcx, bcx, bcx, bcx, bcx, dcv, bcx, dcv, conv_w)


def _attn_bwd(q_bwd, do_aug, q_bwd_t, do_aug_t, k_aug, v_aug, gblocks, *, tq=1024):
    t = q_bwd.shape[1]
    tq = min(tq, t)
    tk = tq // 2
    nq, nk = t // tq, t // tk
    n_pairs = ATTN_W // PAIR
    n_g = len(gblocks)

    def body(q_ref, do_ref, qt_ref, dot_ref, k_ref, v_ref, *rest):
        dq_ref, dqx_ref, dk_ref, dkx_ref, dv_ref = rest[n_g:n_g + 5]
        dq_scr = rest[2 * n_g + 5]
        scatter = _Exchange(rest[:n_g], rest[n_g + 5:2 * n_g + 5], *rest[2 * n_g + 6:], gather=False)
        j = pl.program_id(1)

        @pl.when((pl.program_id(0) == 0) & (j == 0))
        def _():
            scatter.start()

        @pl.when(j == 0)
        def _():
            dq_scr[...] = jnp.zeros_like(dq_scr)
        row = lax.broadcasted_iota(jnp.int32, (tq, tk), 0)
        col = lax.broadcasted_iota(jnp.int32, (tq, tk), 1)
        k = [k_ref[0], k_ref[1]]
        v = [v_ref[0], v_ref[1]]

        def step(i, carry, diag):
            qs = pl.multiple_of(i * tq, tq)
            out = []
            for hh in range(2):
                dk_a, dv_a = carry[hh]
                q = q_ref[hh, pl.ds(qs, tq), :]
                dov = do_ref[hh, pl.ds(qs, tq), :]
                p = jnp.exp2(_dot_nt(q, k[hh]))
                if diag:
                    p = jnp.where(col + (j * tk - i * tq) <= row, p, 0.0)
                ds = (p * _dot_nt(dov, v[hh])).astype(BF16)
                dv_a = dv_a + _dot(dot_ref[hh, :, pl.ds(qs, tq)], p.astype(BF16))
                dk_a = dk_a + _dot(qt_ref[hh, :, pl.ds(qs, tq)], ds)
                dq_scr[hh, pl.ds(qs, tq), :] += _dot(ds, k[hh])
                out.append((dk_a, dv_a))
            return tuple(out)

        zero = (jnp.zeros((PAIR, tk), F32), jnp.zeros((PAIR, tk), F32))
        carry = step(j // 2, (zero, zero), True)
        (dk0, dv0), (dk1, dv1) = lax.fori_loop(j // 2 + 1, nq, functools.partial(step, diag=False), carry)
        first_t = lax.broadcasted_iota(jnp.int32, (PAIR, 1), 0) < HEAD_DIM
        first = lax.broadcasted_iota(jnp.int32, (1, PAIR), 1) < HEAD_DIM
        dk_ref[...] = (jnp.where(first_t, dk0, dk1).T * (1.0 / LOG2E)).astype(BF16)
        dkx_ref[...] = jnp.where(first_t, dk1, dk0).T
        dv_ref[...] = jnp.where(first_t, dv0, dv1).T.astype(BF16)

        @pl.when(j == nk - 1)
        def _():
            dq_ref[...] = (jnp.where(first, dq_scr[0], dq_scr[1]) * Q_SCALE).astype(BF16)
            dqx_ref[...] = jnp.where(first, dq_scr[1], dq_scr[0])

        @pl.when((pl.program_id(0) == n_pairs - 1) & (j == nk - 1))
        def _():
            scatter.wait()

    resident = lambda: pl.BlockSpec((2, t, PAIR), lambda p, j: (p, 0, 0), pipeline_mode=pl.Buffered(1))
    resident_t = lambda: pl.BlockSpec((2, PAIR, t), lambda p, j: (p, 0, 0), pipeline_mode=pl.Buffered(1))
    kv_in = lambda: pl.BlockSpec((2, tk, PAIR), lambda p, j: (p, j, 0))
    whole = lambda: pl.BlockSpec((t, PAIR), lambda p, j: (0, p))
    tile = lambda: pl.BlockSpec((tk, PAIR), lambda p, j: (j, p))
    b16 = jax.ShapeDtypeStruct((t, ATTN_W), BF16)
    f32 = jax.ShapeDtypeStruct((t, ATTN_W), F32)
    res = pl.pallas_call(
        body, name="attn_bwd", grid=(n_pairs, nk),
        in_specs=[resident(), resident(), resident_t(), resident_t(), kv_in(), kv_in()] + [HBM_SPEC] * n_g,
        out_specs=[whole(), whole(), tile(), tile(), tile()] + [HBM_SPEC] * n_g,
        out_shape=[b16, f32, b16, f32, b16] + [jax.ShapeDtypeStruct(g.shape, g.dtype) for g in gblocks],
        scratch_shapes=[pltpu.VMEM((2, t, PAIR), F32)] + _Exchange.scratch(n_g),
        compiler_params=_cp(("arbitrary", "arbitrary")),
    )(q_bwd, do_aug, q_bwd_t, do_aug_t, k_aug, v_aug, *gblocks)
    return res[:5], res[5:]


def _fgate_bwd(dqx, dkx, sgate, *, tm=256):
    t = sgate.shape[0]
    tm = min(tm, t)
    nsteps = t // tm

    def body(dq_ref, dk_ref, sg_ref, df_ref, dbf_ref, carry):
        @pl.when(pl.program_id(0) == 0)
        def _():
            carry[...] = jnp.zeros_like(carry)
            dbf_ref[...] = jnp.zeros_like(dbf_ref)
        lane = lax.broadcasted_iota(jnp.int32, (ATTN_W, F_PAD), 0)
        head = lax.broadcasted_iota(jnp.int32, (ATTN_W, F_PAD), 1)
        aux = (head // 2) * PAIR + HEAD_DIM * (1 - head % 2)
        valid = head < N_HEADS
        pick_r = (valid & (lane == aux + AUX_ROWSUM)).astype(F32)
        pick_c = (valid & (lane == aux + AUX_BIAS)).astype(F32)
        hp = lax.Precision.HIGHEST
        dcum = (jnp.dot(dq_ref[...], pick_r, preferred_element_type=F32, precision=hp)
                + jnp.dot(dk_ref[...], pick_c, preferred_element_type=F32, precision=hp))
        r = lax.broadcasted_iota(jnp.int32, (tm, tm), 0)
        c = lax.broadcasted_iota(jnp.int32, (tm, tm), 1)
        tri = (c >= r).astype(F32)
        rc = jnp.dot(tri, dcum, preferred_element_type=F32, precision=hp) + carry[...]
        carry[...] = rc[0:1, :]
        df = rc * sg_ref[...]
        df_ref[...] = df.astype(BF16)
        dbf_ref[...] += _rows8(df)

    rev = lambda i: nsteps - 1 - i
    return pl.pallas_call(
        body, name="fgate_bwd", grid=(nsteps,),
        in_specs=[pl.BlockSpec((tm, ATTN_W), lambda i: (rev(i), 0)), pl.BlockSpec((tm, ATTN_W), lambda i: (rev(i), 0)),
                  pl.BlockSpec((tm, F_PAD), lambda i: (rev(i), 0))],
        out_specs=[pl.BlockSpec((tm, F_PAD), lambda i: (rev(i), 0)), pl.BlockSpec((8, F_PAD), lambda i: (0, 0))],
        out_shape=[jax.ShapeDtypeStruct((t, F_PAD), BF16), jax.ShapeDtypeStruct((8, F_PAD), F32)],
        scratch_shapes=[pltpu.VMEM((1, F_PAD), F32)],
        compiler_params=_cp(("arbitrary",)),
    )(dqx, dkx, sgate)


def _inproj_bwd(dq, dk, dv, df, dbcx, dh, x, g, win_p, gblock, *, tm=512):
    t, d = x.shape
    tm = min(tm, t)
    nsteps = t // tm
    n_qkv = 3 * ATTN_W

    def body(dq_ref, dk_ref, dv_ref, df_ref, db_ref, dh_ref, x_ref, g_ref, w_ref, gb_ref, gx_ref, dg_ref, land_ref,
             *sems):
        scatter = _Exchange([gb_ref], [land_ref], *sems, gather=False)

        @pl.when(pl.program_id(0) == 0)
        def _():
            scatter.start()
            dg_ref[...] = jnp.zeros_like(dg_ref)
        dn = _dot_nt(df_ref[...], w_ref[:, n_qkv:n_qkv + F_PAD])
        for k, r in enumerate((dq_ref, dk_ref, dv_ref)):
            dn = dn + _dot_nt(r[...], w_ref[:, k * ATTN_W:(k + 1) * ATTN_W])
        for k in range(3):
            c0 = n_qkv + F_PAD + k * CONV_CH
            dn = dn + _dot_nt(db_ref[:, k * CONV_CH:(k + 1) * CONV_CH], w_ref[:, c0:c0 + CONV_CH])
        dx, dg = _norm_bwd(dn, x_ref[...], g_ref[...])
        gx_ref[...] = dh_ref[...] + dx
        dg_ref[...] += dg

        @pl.when(pl.program_id(0) == nsteps - 1)
        def _():
            scatter.wait()

    row = lambda n_: pl.BlockSpec((tm, n_), lambda i: (i, 0))
    return pl.pallas_call(
        body, name="inproj_bwd", grid=(nsteps,),
        in_specs=[row(ATTN_W), row(ATTN_W), row(ATTN_W), row(F_PAD), row(3 * CONV_CH), row(d), row(d),
                  _const_spec((1, d)), _const_spec(win_p.shape), HBM_SPEC],
        out_specs=[row(d), pl.BlockSpec((8, d), lambda i: (0, 0)), HBM_SPEC],
        out_shape=[jax.ShapeDtypeStruct((t, d), F32), jax.ShapeDtypeStruct((8, d), F32),
                   jax.ShapeDtypeStruct(gblock.shape, gblock.dtype)],
        scratch_shapes=_Exchange.scratch(1),
        compiler_params=_cp(("arbitrary",)),
    )(dq, dk, dv, df, dbcx, dh, x, g, win_p, gblock)


LATE = ("w_out_0", "w_up_0", "w_down_0", "pool_w_1", "w_up_1", "w_down_1")


def _local_step(x, target, gains, b_f, conv_w, pool_scale, win_p, shards):
    d = x.shape[1]
    n0, qkv, flog, bcx = _norm_inproj(x, gains["mix0"], win_p)
    q_aug_t, k_aug, v_aug, v_aug_t, sgate = _fgate_prep(flog, b_f, qkv)
    att, q_bwd, q_bwd_t, gathered = _attn_fwd(q_aug_t, k_aug, v_aug_t, [shards[n] for n in LATE])
    g = dict(zip(LATE, gathered))
    wout = g["w_out_0"].reshape(d, d)
    wup0, wup1 = g["w_up_0"], g["w_up_1"]
    wdown0, wdown1 = g["w_down_0"].reshape(-1, d), g["w_down_1"].reshape(-1, d)
    n_grp = len(POOL_WINDOWS)
    cg = d // n_grp
    poolw = g["pool_w_1"].reshape(N_DEV, n_grp, cg // N_DEV, cg).transpose(1, 0, 2, 3).reshape(n_grp, cg, cg)
    cv = _conv_fwd(bcx, conv_w)
    h1 = _outproj(att, cv, x, wout)
    h2, n1, a0, z0 = _mlp_fwd(h1, gains["ffn0"], wup0, wdown0, name="mlp_fwd0")
    h3, pooled = _pool_fwd(h2, gains["mix1"], poolw, pool_scale)
    loss, dh4, dg_final, n3, a1, z1 = _mlp_fwd_loss(h3, gains["ffn1"], wup1, wdown1, gains["final"], target,
                                                    name="mlp_fwd1")
    f = a1.shape[1]
    fb = f // N_DEV
    dh3, da1, dg_ffn1 = _mlp_bwd(dh4, h3, a1, gains["ffn1"], wup1, wdown1, name="mlp_bwd1")
    dwdown1 = _mm_tn(z1, dh4, name="dwdown1", ta=1024, tb=1024, tt=1024, out_dtype=BF16)
    dwup1 = _mm_tn(n3, da1, name="dwup1", ta=d, tb=fb, tt=2048, blocked_out=True, out_dtype=BF16)
    dh2, dpoolw, dscale, dg_mix1 = _pool_bwd(dh3, h2, pooled, gains["mix1"], poolw, pool_scale)
    dh1, da0, dg_ffn0 = _mlp_bwd(dh2, h1, a0, gains["ffn0"], wup0, wdown0, name="mlp_bwd0")
    dwdown0 = _mm_tn(z0, dh2, name="dwdown0", ta=1024, tb=1024, tt=1024, out_dtype=BF16)
    dwup0 = _mm_tn(n1, da0, name="dwup0", ta=d, tb=fb, tt=2048, blocked_out=True, out_dtype=BF16)
    do_aug, do_aug_t, dcv = _outproj_bwd(dh1, att, wout)
    dwout = jnp.concatenate([_mm_tn(att, dh1, name="dwout_att", ta=512, tb=1024, tt=2048, out_dtype=BF16),
                             _mm_tn(cv, dh1, name="dwout_conv", ta=512, tb=1024, tt=2048, out_dtype=BF16)], axis=0)
    dbcx, dconvw = _conv_bwd(bcx, dcv, conv_w)
    gblocks = {
        "w_out_0": dwout.reshape(N_DEV, d // N_DEV, d), "w_up_0": dwup0, "w_up_1": dwup1,
        "w_down_0": dwdown0.reshape(N_DEV, -1, d), "w_down_1": dwdown1.reshape(N_DEV, -1, d),
        "pool_w_1": dpoolw.astype(BF16).reshape(n_grp, N_DEV, cg // N_DEV, cg).transpose(1, 0, 2, 3).reshape(
            N_DEV, n_grp * (cg // N_DEV), cg),
    }
    (dq, dqx, dk, dkx, dv), landed = _attn_bwd(q_bwd, do_aug, q_bwd_t, do_aug_t, k_aug, v_aug,
                                               [gblocks[n] for n in LATE])
    df, dbf = _fgate_bwd(dqx, dkx, sgate)
    dwin = jnp.concatenate(
        [_mm_tn(n0, dq, name="dwin_q", ta=d, tb=512, tt=2048, out_dtype=BF16),
         _mm_tn(n0, dk, name="dwin_k", ta=d, tb=512, tt=2048, out_dtype=BF16),
         _mm_tn(n0, dv, name="dwin_v", ta=d, tb=512, tt=2048, out_dtype=BF16),
         _mm_tn(n0, df, name="dwin_f", ta=d, tb=128, tt=2048, out_dtype=BF16)[:, :N_HEADS],
         _mm_tn(n0, dbcx, name="dwin_bcx", ta=d, tb=512, tt=2048, out_dtype=BF16)], axis=1)
    dwin_blocks = dwin.reshape(d, N_DEV, dwin.shape[1] // N_DEV).transpose(1, 0, 2)
    grad_x, dg_mix0, landed_win = _inproj_bwd(dq, dk, dv, df, dbcx, dh1, x, gains["mix0"], win_p, dwin_blocks)
    small = dict(mix0=dg_mix0, ffn0=dg_ffn0, mix1=dg_mix1, pool_scale=dscale, ffn1=dg_ffn1, final=dg_final,
                 b_f=dbf, conv_w=dconvw)
    return loss, grad_x, dict(zip(LATE + ("w_in_0",), tuple(landed) + (landed_win,))), small


def _mesh_places():
    x, y, c = lax.axis_index("x"), lax.axis_index("y"), lax.axis_index("c")
    chips = [(1 - x, y), (x, 1 - y), (1 - x, 1 - y)]
    return (x, y, c), (x, y, 1 - c), chips


def _all_gather(shards):
    n = len(shards)

    def body(*refs):
        ins, outs = refs[:n], refs[n:2 * n]
        send_sems, recv_sems, local_sems = refs[2 * n:]
        me, sib, chips = _mesh_places()
        c = me[2]

        def copy(ai, k, block, to, src=None):
            dst = outs[ai].at[_slot(*block)]
            return pltpu.make_async_remote_copy(
                src_ref=dst if src is None else src, dst_ref=dst, send_sem=send_sems.at[7 * ai + k],
                recv_sem=recv_sems.at[7 * ai + k], device_id=to, device_id_type=MESH)

        mine = [pltpu.make_async_copy(ins[ai], outs[ai].at[_slot(*me)], local_sems.at[ai]) for ai in range(n)]
        for cp in mine:
            cp.start()
        first = []
        for ai in range(n):
            first.append(copy(ai, 0, me, sib, src=ins[ai]))
            first += [copy(ai, 1 + j, me, (*chip, c), src=ins[ai]) for j, chip in enumerate(chips)]
        for cp in first:
            cp.start()
        passed = []
        for ai in range(n):
            for j, chip in enumerate(chips):
                copy(ai, 1 + j, (*chip, c), me).wait_recv()
                cp = copy(ai, 4 + j, (*chip, c), sib)
                cp.start()
                passed.append(cp)
        for ai in range(n):
            copy(ai, 0, sib, me).wait_recv()
            for j, chip in enumerate(chips):
                copy(ai, 4 + j, (*chip, 1 - c), me).wait_recv()
        for cp in first + passed:
            cp.wait_send()
        for cp in mine:
            cp.wait()

    return pl.pallas_call(
        body, name="all_gather",
        in_specs=[HBM_SPEC] * n, out_specs=[HBM_SPEC] * n,
        out_shape=[jax.ShapeDtypeStruct((N_DEV,) + s.shape, s.dtype) for s in shards],
        scratch_shapes=[pltpu.SemaphoreType.DMA((7 * n,)), pltpu.SemaphoreType.DMA((7 * n,)),
                        pltpu.SemaphoreType.DMA((n,))],
    )(*shards)


SMALL_ROWS = 16


def _small_allreduce(parts):
    n, _, w = parts.shape
    assert n <= SMALL_ROWS

    def body(p_ref, o_ref, gath, send_sems, recv_sems):
        x, y, c = lax.axis_index("x"), lax.axis_index("y"), lax.axis_index("c")
        my = _slot(x, y, c)
        rows = [jnp.sum(p_ref[i], axis=0, keepdims=True) for i in range(n)]
        rows.append(jnp.zeros((SMALL_ROWS - n, w), F32))
        gath[my] = jnp.concatenate(rows, axis=0)
        copies = []
        for k in range(1, N_DEV):
            px, py, pc = x ^ (k >> 2), y ^ ((k >> 1) & 1), c ^ (k & 1)
            cp = pltpu.make_async_remote_copy(
                src_ref=gath.at[my], dst_ref=gath.at[my], send_sem=send_sems.at[k - 1], recv_sem=recv_sems.at[k - 1],
                device_id=(px, py, pc), device_id_type=MESH)
            cp.start()
            copies.append(cp)
        for cp in copies:
            cp.wait()
        acc = gath[0]
        for d in range(1, N_DEV):
            acc = acc + gath[d]
        o_ref[...] = acc

    return pl.pallas_call(
        body, name="small_allreduce",
        in_specs=[VMEM_SPEC], out_specs=VMEM_SPEC,
        out_shape=jax.ShapeDtypeStruct((SMALL_ROWS, w), F32),
        scratch_shapes=[pltpu.VMEM((N_DEV, SMALL_ROWS, w), F32), pltpu.SemaphoreType.DMA((N_DEV - 1,)),
                        pltpu.SemaphoreType.DMA((N_DEV - 1,))],
    )(parts)


def _adamw(g, w, m, v, *, name, tm=256):
    r, c = g.shape
    tm = tm if r % tm == 0 else r
    bc1 = 1.0 - ADAM_B1 ** ADAM_STEP
    bc2 = 1.0 - ADAM_B2 ** ADAM_STEP

    def body(g_ref, w_ref, m_ref, v_ref, d_ref, nm_ref, nv_ref):
        gv = g_ref[...]
        nm = ADAM_B1 * m_ref[...] + (1.0 - ADAM_B1) * gv
        nv = ADAM_B2 * v_ref[...] + (1.0 - ADAM_B2) * jnp.square(gv)
        nm_ref[...] = nm
        nv_ref[...] = nv
        d_ref[...] = -ADAM_LR * ((nm / bc1) / (jnp.sqrt(nv / bc2) + ADAM_EPS) + ADAM_WD * w_ref[...])

    blk = pl.BlockSpec((tm, c), lambda i: (i, 0))
    shp = jax.ShapeDtypeStruct((r, c), F32)
    return pl.pallas_call(
        body, name=name, grid=(r // tm,), in_specs=[blk] * 4, out_specs=[blk] * 3, out_shape=[shp] * 3,
        compiler_params=_cp(("parallel",)),
    )(g, w, m, v)


def _adamw_sum(parts, w, m, v, *, name, tm=128):
    _, r, c = parts.shape
    tm = tm if r % tm == 0 else r
    bc1 = 1.0 - ADAM_B1 ** ADAM_STEP
    bc2 = 1.0 - ADAM_B2 ** ADAM_STEP

    def body(p_ref, w_ref, m_ref, v_ref, g_ref, d_ref, nm_ref, nv_ref):
        gv = p_ref[0].astype(F32)
        for k in range(1, N_DEV):
            gv = gv + p_ref[k].astype(F32)
        g_ref[...] = gv
        nm = ADAM_B1 * m_ref[...] + (1.0 - ADAM_B1) * gv
        nv = ADAM_B2 * v_ref[...] + (1.0 - ADAM_B2) * jnp.square(gv)
        nm_ref[...] = nm
        nv_ref[...] = nv
        d_ref[...] = -ADAM_LR * ((nm / bc1) / (jnp.sqrt(nv / bc2) + ADAM_EPS) + ADAM_WD * w_ref[...])

    blk = pl.BlockSpec((tm, c), lambda i: (i, 0))
    shp = jax.ShapeDtypeStruct((r, c), F32)
    return pl.pallas_call(
        body, name=name, grid=(r // tm,), in_specs=[pl.BlockSpec((N_DEV, tm, c), lambda i: (0, i, 0))] + [blk] * 3,
        out_specs=[blk] * 4, out_shape=[shp] * 4, compiler_params=_cp(("parallel",)),
    )(parts, w, m, v)


BIG = ("w_in_0", "w_out_0", "w_up_0", "w_down_0", "pool_w_1", "w_up_1", "w_down_1")
SMALL = ("norm_mix_0", "norm_ffn_0", "norm_mix_1", "pool_scale_1", "norm_ffn_1", "final_norm", "b_f_0", "conv_w_0")
WEIGHTS = ("norm_mix_0", "w_in_0", "b_f_0", "conv_w_0", "w_out_0", "norm_ffn_0", "w_up_0", "w_down_0", "norm_mix_1",
           "pool_w_1", "pool_scale_1", "norm_ffn_1", "w_up_1", "w_down_1", "final_norm")


def _pad_to(a, rows, cols):
    return jnp.pad(a, ((0, rows - a.shape[0]), (0, cols - a.shape[1])))


def _pack_small(p, width):
    rows = [p[n].reshape(1, -1) for n in SMALL[:6]]
    rows.append(_pad_to(p["b_f_0"].reshape(1, -1), 1, width))
    rows.append(_pad_to(p["conv_w_0"], 3, width))
    return _pad_to(jnp.concatenate(rows, axis=0), SMALL_ROWS, width)


def _unpack_small(a, like):
    out = {n: a[i] for i, n in enumerate(SMALL[:6])}
    out["b_f_0"] = a[6, :like["b_f_0"].shape[0]]
    out["conv_w_0"] = a[7:10, :like["conv_w_0"].shape[1]]
    return out


def kernel(x, norm_mix_0, w_in_0, b_f_0, conv_w_0, w_out_0, norm_ffn_0, w_up_0, w_down_0, norm_mix_1, pool_w_1, pool_scale_1, norm_ffn_1, w_up_1, w_down_1, final_norm, loss_target, m_norm_mix_0, m_w_in_0, m_b_f_0, m_conv_w_0, m_w_out_0, m_norm_ffn_0, m_w_up_0, m_w_down_0, m_norm_mix_1, m_pool_w_1, m_pool_scale_1, m_norm_ffn_1, m_w_up_1, m_w_down_1, m_final_norm, v_norm_mix_0, v_w_in_0, v_b_f_0, v_conv_w_0, v_w_out_0, v_norm_ffn_0, v_w_up_0, v_w_down_0, v_norm_mix_1, v_pool_w_1, v_pool_scale_1, v_norm_ffn_1, v_w_up_1, v_w_down_1, v_final_norm):
    w = dict(norm_mix_0=norm_mix_0, w_in_0=w_in_0, b_f_0=b_f_0, conv_w_0=conv_w_0, w_out_0=w_out_0,
             norm_ffn_0=norm_ffn_0, w_up_0=w_up_0, w_down_0=w_down_0, norm_mix_1=norm_mix_1, pool_w_1=pool_w_1,
             pool_scale_1=pool_scale_1, norm_ffn_1=norm_ffn_1, w_up_1=w_up_1, w_down_1=w_down_1, final_norm=final_norm)
    m = dict(norm_mix_0=m_norm_mix_0, w_in_0=m_w_in_0, b_f_0=m_b_f_0, conv_w_0=m_conv_w_0, w_out_0=m_w_out_0,
             norm_ffn_0=m_norm_ffn_0, w_up_0=m_w_up_0, w_down_0=m_w_down_0, norm_mix_1=m_norm_mix_1,
             pool_w_1=m_pool_w_1, pool_scale_1=m_pool_scale_1, norm_ffn_1=m_norm_ffn_1, w_up_1=m_w_up_1,
             w_down_1=m_w_down_1, final_norm=m_final_norm)
    v = dict(norm_mix_0=v_norm_mix_0, w_in_0=v_w_in_0, b_f_0=v_b_f_0, conv_w_0=v_conv_w_0, w_out_0=v_w_out_0,
             norm_ffn_0=v_norm_ffn_0, w_up_0=v_w_up_0, w_down_0=v_w_down_0, norm_mix_1=v_norm_mix_1,
             pool_w_1=v_pool_w_1, pool_scale_1=v_pool_scale_1, norm_ffn_1=v_norm_ffn_1, w_up_1=v_w_up_1,
             w_down_1=v_w_down_1, final_norm=v_final_norm)
    d = x.shape[-1]
    n_in = w_in_0.shape[1] * N_DEV
    n_qkv = 3 * ATTN_W
    pool_g, pool_rows, pool_c = pool_w_1.shape

    def shard2d(p):
        return {n: (p[n].reshape(pool_g * pool_rows, pool_c) if n == "pool_w_1" else p[n]) for n in BIG}
    w2, m2, v2 = shard2d(w), shard2d(m), shard2d(v)

    conv_cols = conv_w_0.shape[1]
    win_g8, conv_g8 = _all_gather([w_in_0.astype(BF16), _pad_to(conv_w_0, 8, 128)])
    conv_full = conv_g8[:, :, :conv_cols].transpose(1, 0, 2).reshape(8, N_DEV * conv_cols)
    win = win_g8.transpose(1, 0, 2).reshape(d, n_in)
    win_p = jnp.concatenate([win[:, :n_qkv], _pad_to(win[:, n_qkv:n_qkv + N_HEADS], d, F_PAD),
                             win[:, n_qkv + N_HEADS:]], axis=1)

    gains = dict(mix0=norm_mix_0.reshape(1, d), ffn0=norm_ffn_0.reshape(1, d), mix1=norm_mix_1.reshape(1, d),
                 ffn1=norm_ffn_1.reshape(1, d), final=final_norm.reshape(1, d))
    dev = _slot(lax.axis_index("x"), lax.axis_index("y"), lax.axis_index("c"))
    loss8, grad_x, landed, small = _local_step(
        x[0], loss_target[0], gains, _pad_to(b_f_0.reshape(1, -1), 1, F_PAD), conv_full, pool_scale_1.reshape(1, d),
        win_p, {n: w2[n].astype(BF16) for n in LATE})
    loss = lax.psum(loss8[0, 0], ("x", "y", "c"))

    parts = jnp.concatenate(
        [small[k][None] for k in ("mix0", "ffn0", "mix1", "pool_scale", "ffn1", "final")]
        + [_pad_to(small["b_f"], 8, d)[None], jnp.pad(small["conv_w"], ((0, 0), (0, 0), (0, d - CONV_CH)))], axis=0)
    tot = _small_allreduce(parts)
    conv_g = lax.dynamic_slice(tot, (7, dev * conv_cols), (3, conv_cols))
    gs = tot.at[7:10].set(_pad_to(conv_g, 3, d))

    grads, deltas, new_m, new_v = {}, {}, {}, {}
    for n in BIG:
        gr, dl, nm, nv = _adamw_sum(landed[n], w2[n], m2[n], v2[n], name="adamw_" + n)
        for dst, val in ((grads, gr), (deltas, dl), (new_m, nm), (new_v, nv)):
            dst[n] = val.reshape(w[n].shape)
    dl, nm, nv = _adamw(gs, _pack_small(w, d), _pack_small(m, d), _pack_small(v, d), name="adamw_small")
    for dst, val in ((grads, gs), (deltas, dl), (new_m, nm), (new_v, nv)):
        dst.update(_unpack_small(val, w))
    return (loss, grad_x[None], *[grads[n] for n in WEIGHTS], *[deltas[n] for n in WEIGHTS],
            *[new_m[n] for n in WEIGHTS], *[new_v[n] for n in WEIGHTS])
```

```python
import functools

import jax
import jax.numpy as jnp
from jax import lax
from jax.experimental import pallas as pl
from jax.experimental.pallas import tpu as pltpu

F32 = jnp.float32
BF16 = jnp.bfloat16

N_DEV = 8
N_HEADS = 8
HEAD_DIM = 64
PAIR = 2 * HEAD_DIM
ATTN_W = N_HEADS * HEAD_DIM
CONV_CH = 512
F_PAD = 128
POOL_WINDOWS = (2, 4, 8, 16)
POOL_HALO = 16
CONV_HALO = 8
RMS_EPS = 1e-6
Q_SCALE = HEAD_DIM ** -0.5
LOG2E = 1.4426950408889634
NEG = -1e30
AUX_BIAS = 0
AUX_LSE = 3
AUX_ROWSUM = 6
ADAM_LR, ADAM_B1, ADAM_B2, ADAM_EPS, ADAM_WD, ADAM_STEP = 0.001, 0.9, 0.999, 1e-08, 0.01, 10
MESH = pl.DeviceIdType.MESH
VMEM_LIMIT = 56 * 2**20


def _cp(sem=None, vmem=VMEM_LIMIT, **kw):
    return pltpu.CompilerParams(dimension_semantics=sem, vmem_limit_bytes=vmem, **kw)


def _dot(a, b):
    return jnp.dot(a, b, preferred_element_type=F32)


def _dot_nt(a, b):
    return lax.dot_general(a, b, (((1,), (1,)), ((), ())), preferred_element_type=F32)


def _dot_tn(a, b):
    return lax.dot_general(a, b, (((0,), (0,)), ((), ())), preferred_element_type=F32)


def _rstd(h):
    return lax.rsqrt(jnp.mean(h * h, axis=-1, keepdims=True) + RMS_EPS)


def _rows8(x):
    r, n = x.shape
    return jnp.sum(x.reshape(r // 8, 8, n), axis=0)


def _norm_bwd(dn, h, g):
    r = _rstd(h)
    xhat = h * r
    dy = dn * g
    dh = r * (dy - xhat * jnp.mean(dy * xhat, axis=-1, keepdims=True))
    return dh, _rows8(dn * xhat)


def _const_spec(shape):
    nd = len(shape)
    return pl.BlockSpec(shape, lambda *_: (0,) * nd, pipeline_mode=pl.Buffered(1))


HBM_SPEC = pl.BlockSpec(memory_space=pltpu.HBM)
VMEM_SPEC = pl.BlockSpec(memory_space=pltpu.VMEM)


def _slot(px, py, pc):
    return 4 * px + 2 * py + pc


class _Exchange:
    def __init__(self, srcs, dsts, send_sems, recv_sems, local_sems, gather):
        x, y, c = lax.axis_index("x"), lax.axis_index("y"), lax.axis_index("c")
        me = _slot(x, y, c)
        self.copies = []
        for a, (src, dst) in enumerate(zip(srcs, dsts)):
            self.copies.append(pltpu.make_async_copy(src if gather else src.at[me], dst.at[me], local_sems.at[a]))
            for k in range(1, N_DEV):
                px, py, pc = x ^ (k >> 2), y ^ ((k >> 1) & 1), c ^ (k & 1)
                self.copies.append(pltpu.make_async_remote_copy(
                    src_ref=src if gather else src.at[_slot(px, py, pc)], dst_ref=dst.at[me],
                    send_sem=send_sems.at[(N_DEV - 1) * a + k - 1], recv_sem=recv_sems.at[(N_DEV - 1) * a + k - 1],
                    device_id=(px, py, pc), device_id_type=MESH))

    def start(self):
        for cp in self.copies:
            cp.start()

    def wait(self):
        for cp in self.copies:
            cp.wait()

    @staticmethod
    def scratch(n):
        return [pltpu.SemaphoreType.DMA(((N_DEV - 1) * n,)), pltpu.SemaphoreType.DMA(((N_DEV - 1) * n,)),
                pltpu.SemaphoreType.DMA((n,))]


def _norm_inproj(x, g, win_p, *, tm=512):
    t, d = x.shape
    n_all = win_p.shape[1]
    n_qkv = 3 * ATTN_W
    n_bcx = 3 * CONV_CH
    assert n_all == n_qkv + F_PAD + n_bcx
    tm = min(tm, t)

    def body(x_ref, g_ref, w_ref, n_ref, qkv_ref, f_ref, bcx_ref):
        h = x_ref[...]
        n = (h * _rstd(h) * g_ref[...]).astype(BF16)
        n_ref[...] = n
        for c0 in range(0, n_qkv, 512):
            acc = _dot(n, w_ref[:, c0:c0 + 512])
            if c0 < ATTN_W:
                acc = acc * (Q_SCALE * LOG2E)
            qkv_ref[:, c0:c0 + 512] = acc.astype(BF16)
        f_ref[...] = _dot(n, w_ref[:, n_qkv:n_qkv + F_PAD])
        for c0 in range(0, n_bcx, 512):
            bcx_ref[:, c0:c0 + 512] = _dot(n, w_ref[:, n_qkv + F_PAD + c0:n_qkv + F_PAD + c0 + 512])

    return pl.pallas_call(
        body, name="norm_inproj", grid=(t // tm,),
        in_specs=[pl.BlockSpec((tm, d), lambda i: (i, 0)), _const_spec((1, d)), _const_spec((d, n_all))],
        out_specs=[pl.BlockSpec((tm, d), lambda i: (i, 0)), pl.BlockSpec((tm, n_qkv), lambda i: (i, 0)),
                   pl.BlockSpec((tm, F_PAD), lambda i: (i, 0)), pl.BlockSpec((tm, n_bcx), lambda i: (i, 0))],
        out_shape=[jax.ShapeDtypeStruct((t, d), BF16), jax.ShapeDtypeStruct((t, n_qkv), BF16),
                   jax.ShapeDtypeStruct((t, F_PAD), F32), jax.ShapeDtypeStruct((t, n_bcx), F32)],
        compiler_params=_cp(("parallel",)),
    )(x, g, win_p)


def _head_lanes(h):
    lane = lax.broadcasted_iota(jnp.int32, (1, PAIR), 1)
    hh = h % 2
    return lane, lane // HEAD_DIM == hh, HEAD_DIM * (1 - hh)


def _pieces(col):
    hi = col.astype(BF16).astype(F32)
    r1 = col - hi
    mid = r1.astype(BF16).astype(F32)
    lo = (r1 - mid).astype(BF16).astype(F32)
    return hi, mid, lo


def _put_pieces(lane, first, col, other):
    hi, mid, lo = _pieces(col)
    return jnp.where(lane == first, hi, jnp.where(lane == first + 1, mid, jnp.where(lane == first + 2, lo, other)))


def _fgate_prep(flog, b_f, qkv, *, tm=512):
    t = flog.shape[0]
    tm = min(tm, t)

    def body(f_ref, b_ref, qkv_ref, qat_ref, ka_ref, va_ref, vat_ref, sg_ref, carry):
        @pl.when(pl.program_id(0) == 0)
        def _():
            carry[...] = jnp.zeros_like(carry)
        z = f_ref[...] + b_ref[...]
        e = jnp.exp(-jnp.abs(z))
        logf = jnp.minimum(z, 0.0) - jnp.log(1.0 + e)
        sg_ref[...] = jnp.where(z >= 0, e, 1.0) / (1.0 + e)
        r = lax.broadcasted_iota(jnp.int32, (tm, tm), 0)
        c = lax.broadcasted_iota(jnp.int32, (tm, tm), 1)
        tri = (c <= r).astype(F32)
        cs = jnp.dot(tri, logf, preferred_element_type=F32, precision=lax.Precision.HIGHEST) + carry[...]
        carry[...] = cs[tm - 1:tm, :]
        cs2 = cs * LOG2E
        for h in range(N_HEADS):
            lane, head, aux = _head_lanes(h)
            p0 = (h // 2) * PAIR
            ones = ((lane >= aux + AUX_LSE) & (lane <= aux + AUX_ROWSUM)).astype(F32)
            bias = (lane >= aux + AUX_BIAS) & (lane < aux + AUX_BIAS + 3)
            k_aux = _put_pieces(lane, aux + AUX_BIAS, cs2[:, h:h + 1], ones)
            q_aug = jnp.where(head, qkv_ref[:, p0:p0 + PAIR].astype(F32), jnp.where(bias, -1.0, 0.0))
            v_aug = jnp.where(head, qkv_ref[:, 2 * ATTN_W + p0:2 * ATTN_W + p0 + PAIR].astype(F32),
                              jnp.where(bias, 1.0, 0.0))
            qat_ref[h] = q_aug.T.astype(BF16)
            ka_ref[h] = jnp.where(head, qkv_ref[:, ATTN_W + p0:ATTN_W + p0 + PAIR], k_aux.astype(BF16))
            va_ref[h] = v_aug.astype(BF16)
            vat_ref[h] = v_aug.T.astype(BF16)

    aug = lambda: pl.BlockSpec((N_HEADS, tm, PAIR), lambda i: (0, i, 0))
    aug_t = lambda: pl.BlockSpec((N_HEADS, PAIR, tm), lambda i: (0, 0, i))
    aug_shape = jax.ShapeDtypeStruct((N_HEADS, t, PAIR), BF16)
    aug_t_shape = jax.ShapeDtypeStruct((N_HEADS, PAIR, t), BF16)
    return pl.pallas_call(
        body, name="fgate_prep", grid=(t // tm,),
        in_specs=[pl.BlockSpec((tm, F_PAD), lambda i: (i, 0)), _const_spec((1, F_PAD)),
                  pl.BlockSpec((tm, 3 * ATTN_W), lambda i: (i, 0))],
        out_specs=[aug_t(), aug(), aug(), aug_t(), pl.BlockSpec((tm, F_PAD), lambda i: (i, 0))],
        out_shape=[aug_t_shape, aug_shape, aug_shape, aug_t_shape, jax.ShapeDtypeStruct((t, F_PAD), F32)],
        scratch_shapes=[pltpu.VMEM((1, F_PAD), F32)],
        compiler_params=_cp(("arbitrary",)),
    )(flog, b_f, qkv)


def _put_pieces_t(row, first, vec, other):
    hi, mid, lo = _pieces(vec)
    return jnp.where(row == first, hi, jnp.where(row == first + 1, mid, jnp.where(row == first + 2, lo, other)))


def _attn_fwd(q_aug_t, k_aug, v_aug_t, shards, *, tq=1024):
    t = k_aug.shape[1]
    tq = min(tq, t)
    tk = tq // 2
    nq = t // tq
    n_pairs = ATTN_W // PAIR
    n_sh = len(shards)

    def body(qt_ref, k_ref, vt_ref, *rest):
        o_ref, qb_ref, qbt_ref = rest[n_sh:n_sh + 3]
        s_scr = rest[2 * n_sh + 3]
        gather = _Exchange(rest[:n_sh], rest[n_sh + 3:2 * n_sh + 3], *rest[2 * n_sh + 4:], gather=True)
        i = pl.program_id(1)

        @pl.when((pl.program_id(0) == 0) & (i == 0))
        def _():
            gather.start()
        key = lax.broadcasted_iota(jnp.int32, (tk, tq), 0)
        qry = lax.broadcasted_iota(jnp.int32, (tk, tq), 1)
        qt = [qt_ref[0], qt_ref[1]]

        def logits(hh, tile, slot, diag):
            s = _dot(k_ref[hh, pl.ds(pl.multiple_of(tile * tk, tk), tk), :], qt[hh])
            if diag:
                s = jnp.where(key + (tile * tk - i * tq) <= qry, s, NEG)
            s_scr[hh, slot] = s
            return jnp.max(s, axis=0, keepdims=True)

        def probs(hh, tile, slot, m, acc, tmax):
            mn = jnp.maximum(m, tmax)
            p = jnp.exp2(s_scr[hh, slot] - mn).astype(BF16)
            acc = jnp.exp2(m - mn) * acc + _dot(vt_ref[hh, :, pl.ds(pl.multiple_of(tile * tk, tk), tk)], p)
            return mn, acc

        def advance(carry, prev, slot, nxt, diag=False):
            out = []
            for hh in range(2):
                m, acc, tmax = carry[hh]
                m, acc = probs(hh, prev, slot, m, acc, tmax)
                out.append((m, acc, logits(hh, nxt, 1 - slot, diag)))
            return tuple(out)

        def two_tiles(jj, carry):
            carry = advance(carry, jnp.where(jj == 0, 2 * i, 2 * jj - 1), 1, 2 * jj)
            return advance(carry, 2 * jj, 0, 2 * jj + 1)

        init = tuple((jnp.full((1, tq), NEG, F32), jnp.zeros((PAIR, tq), F32), logits(hh, 2 * i + 1, 0, True))
                     for hh in range(2))
        carry = advance(init, 2 * i + 1, 0, 2 * i, diag=True)
        carry = lax.fori_loop(0, i, two_tiles, carry)
        last = jnp.where(i == 0, 2 * i, 2 * i - 1)
        row = lax.broadcasted_iota(jnp.int32, (PAIR, 1), 0)
        res = []
        for hh in range(2):
            aux = HEAD_DIM * (1 - hh)
            m, acc, tmax = carry[hh]
            m, acc = probs(hh, last, 1, m, acc, tmax)
            l = acc[aux + AUX_BIAS:aux + AUX_BIAS + 1, :]
            qbt = _put_pieces_t(row, aux + AUX_LSE, -(m + jnp.log2(l)), qt[hh].astype(F32))
            qbt_ref[hh] = qbt.astype(BF16)
            qb_ref[hh] = qbt.T.astype(BF16)
            res.append(acc * (1.0 / l))
        o_ref[...] = jnp.where(row < HEAD_DIM, res[0], res[1]).T.astype(BF16)

        @pl.when((pl.program_id(0) == n_pairs - 1) & (i == nq - 1))
        def _():
            gather.wait()

    res = pl.pallas_call(
        body, name="attn_fwd", grid=(n_pairs, nq),
        in_specs=[pl.BlockSpec((2, PAIR, tq), lambda p, i: (p, 0, i)),
                  pl.BlockSpec((2, t, PAIR), lambda p, i: (p, 0, 0), pipeline_mode=pl.Buffered(1)),
                  pl.BlockSpec((2, PAIR, t), lambda p, i: (p, 0, 0), pipeline_mode=pl.Buffered(1))] + [HBM_SPEC] * n_sh,
        out_specs=[pl.BlockSpec((tq, PAIR), lambda p, i: (i, p)),
                   pl.BlockSpec((2, tq, PAIR), lambda p, i: (p, i, 0)),
                   pl.BlockSpec((2, PAIR, tq), lambda p, i: (p, 0, i))] + [HBM_SPEC] * n_sh,
        out_shape=[jax.ShapeDtypeStruct((t, ATTN_W), BF16), jax.ShapeDtypeStruct((N_HEADS, t, PAIR), BF16),
                   jax.ShapeDtypeStruct((N_HEADS, PAIR, t), BF16)]
        + [jax.ShapeDtypeStruct((N_DEV,) + s.shape, s.dtype) for s in shards],
        scratch_shapes=[pltpu.VMEM((2, 2, tk, tq), F32)] + _Exchange.scratch(n_sh),
        compiler_params=_cp(("arbitrary", "arbitrary")),
    )(q_aug_t, k_aug, v_aug_t, *shards)
    return res[0], res[1], res[2], res[3:]


def _prev_halo(tm, halo):
    return lambda i: (jnp.maximum(i * (tm // halo) - 1, 0), 0)


def _next_halo(tm, halo, t):
    return lambda i: (jnp.minimum((i + 1) * (tm // halo), t // halo - 1), 0)


def _conv_fwd(bcx, conv_w, *, tm=512):
    t = bcx.shape[0]
    tm = min(tm, t)
    ch = CONV_CH

    def body(b_ref, c_ref, x_ref, hc_ref, hx_ref, w_ref, cv_ref, ext):
        first = pl.program_id(0) == 0
        ext[0:CONV_HALO, :] = jnp.where(first, 0.0, hc_ref[...] * hx_ref[...])
        ext[CONV_HALO:CONV_HALO + tm, :] = c_ref[...] * x_ref[...]
        conv = (w_ref[0:1, :] * ext[CONV_HALO - 2:CONV_HALO - 2 + tm, :]
                + w_ref[1:2, :] * ext[CONV_HALO - 1:CONV_HALO - 1 + tm, :]
                + w_ref[2:3, :] * ext[CONV_HALO:CONV_HALO + tm, :])
        cv_ref[...] = (b_ref[...] * conv).astype(BF16)

    col = lambda k: pl.BlockSpec((tm, ch), lambda i: (i, k))
    halo = lambda k: pl.BlockSpec((CONV_HALO, ch), lambda i: (_prev_halo(tm, CONV_HALO)(i)[0], k))
    return pl.pallas_call(
        body, name="conv_fwd", grid=(t // tm,),
        in_specs=[col(0), col(1), col(2), halo(1), halo(2), _const_spec((8, ch))],
        out_specs=pl.BlockSpec((tm, ch), lambda i: (i, 0)),
        out_shape=jax.ShapeDtypeStruct((t, ch), BF16),
        scratch_shapes=[pltpu.VMEM((CONV_HALO + tm, ch), F32)],
        compiler_params=_cp(("parallel",)),
    )(bcx, bcx, bcx, bcx, bcx, conv_w)


def _outproj(att, cv, x, wout, *, tm=512):
    t, d = x.shape
    tm = min(tm, t)

    def body(a_ref, c_ref, x_ref, w_ref, h_ref):
        h_ref[...] = x_ref[...] + _dot(a_ref[...], w_ref[0:ATTN_W, :]) + _dot(c_ref[...], w_ref[ATTN_W:, :])

    return pl.pallas_call(
        body, name="outproj", grid=(t // tm,),
        in_specs=[pl.BlockSpec((tm, ATTN_W), lambda i: (i, 0)), pl.BlockSpec((tm, CONV_CH), lambda i: (i, 0)),
                  pl.BlockSpec((tm, d), lambda i: (i, 0)), _const_spec(wout.shape)],
        out_specs=pl.BlockSpec((tm, d), lambda i: (i, 0)),
        out_shape=jax.ShapeDtypeStruct((t, d), F32),
        compiler_params=_cp(("parallel",)),
    )(att, cv, x, wout)


def _mlp_tile(hh, g_ref, wu_ref, wd_ref, n_ref, a_ref, z_ref):
    n_blk, _, fb = wu_ref.shape
    n = (hh * _rstd(hh) * g_ref[...]).astype(BF16)
    n_ref[...] = n
    acc = hh
    for k in range(n_blk):
        a = _dot(n, wu_ref[k])
        zz = jnp.square(jnp.maximum(a, 0.0)).astype(BF16)
        a_ref[:, k * fb:(k + 1) * fb] = a.astype(BF16)
        z_ref[:, k * fb:(k + 1) * fb] = zz
        acc = acc + _dot(zz, wd_ref[k * fb:(k + 1) * fb, :])
    return acc


def _mlp_fwd(h, g, wup, wdown, *, name, tm=512):
    t, d = h.shape
    n_blk, _, fb = wup.shape
    f = n_blk * fb
    tm = min(tm, t)

    def body(h_ref, g_ref, wu_ref, wd_ref, ho_ref, n_ref, a_ref, z_ref):
        ho_ref[...] = _mlp_tile(h_ref[...], g_ref, wu_ref, wd_ref, n_ref, a_ref, z_ref)

    row = lambda n_: pl.BlockSpec((tm, n_), lambda i: (i, 0))
    return pl.pallas_call(
        body, name=name, grid=(t // tm,),
        in_specs=[row(d), _const_spec((1, d)), _const_spec(wup.shape), _const_spec(wdown.shape)],
        out_specs=[row(d), row(d), row(f), row(f)],
        out_shape=[jax.ShapeDtypeStruct((t, d), F32), jax.ShapeDtypeStruct((t, d), BF16),
                   jax.ShapeDtypeStruct((t, f), BF16), jax.ShapeDtypeStruct((t, f), BF16)],
        compiler_params=_cp(("parallel",)),
    )(h, g, wup, wdown)


def _mlp_fwd_loss(h, g, wup, wdown, g_out, target, *, name, tm=512):
    t, d = h.shape
    n_blk, _, fb = wup.shape
    f = n_blk * fb
    tm = min(tm, t)
    nsteps = t // tm

    def body(h_ref, g_ref, wu_ref, wd_ref, go_ref, y_ref, loss_ref, dh_ref, dg_ref, n_ref, a_ref, z_ref, lacc):
        i = pl.program_id(0)

        @pl.when(i == 0)
        def _():
            lacc[...] = jnp.zeros_like(lacc)
            dg_ref[...] = jnp.zeros_like(dg_ref)
        hv = _mlp_tile(h_ref[...], g_ref, wu_ref, wd_ref, n_ref, a_ref, z_ref)
        gv = go_ref[...]
        r = _rstd(hv)
        xhat = hv * r
        err = xhat * gv - y_ref[...]
        lacc[...] += _rows8(err * err)
        dout = err * (1.0 / d)
        dy = dout * gv
        dg_ref[...] += _rows8(dout * xhat)
        dh_ref[...] = r * (dy - xhat * jnp.mean(dy * xhat, axis=-1, keepdims=True))

        @pl.when(i == nsteps - 1)
        def _():
            loss_ref[...] = jnp.full(loss_ref.shape, (0.5 / d) * jnp.sum(lacc[...]), F32)

    row = lambda n_: pl.BlockSpec((tm, n_), lambda i: (i, 0))
    return pl.pallas_call(
        body, name=name, grid=(nsteps,),
        in_specs=[row(d), _const_spec((1, d)), _const_spec(wup.shape), _const_spec(wdown.shape), _const_spec((1, d)),
                  row(d)],
        out_specs=[pl.BlockSpec((8, 128), lambda i: (0, 0)), row(d), pl.BlockSpec((8, d), lambda i: (0, 0)),
                   row(d), row(f), row(f)],
        out_shape=[jax.ShapeDtypeStruct((8, 128), F32), jax.ShapeDtypeStruct((t, d), F32),
                   jax.ShapeDtypeStruct((8, d), F32), jax.ShapeDtypeStruct((t, d), BF16),
                   jax.ShapeDtypeStruct((t, f), BF16), jax.ShapeDtypeStruct((t, f), BF16)],
        scratch_shapes=[pltpu.VMEM((8, d), F32)],
        compiler_params=_cp(("arbitrary",)),
    )(h, g, wup, wdown, g_out, target)


def _pool_inv_count(i, tm):
    tglob = (i * tm + lax.broadcasted_iota(jnp.int32, (tm, 1), 0) + 1).astype(F32)
    return [1.0 / jnp.minimum(tglob, float(w)) for w in POOL_WINDOWS]


def _pool_fwd(h, g, poolw, scale, *, tm=256):
    t, d = h.shape
    tm = min(tm, t)
    cg = d // len(POOL_WINDOWS)

    def body(h_ref, hh_ref, g_ref, w_ref, s_ref, ho_ref, p_ref, ext):
        i = pl.program_id(0)
        hv = h_ref[...]
        halo = hh_ref[...]
        n = hv * _rstd(hv) * g_ref[...]
        ext[0:POOL_HALO, :] = jnp.where(i == 0, 0.0, halo * _rstd(halo) * g_ref[...])
        ext[POOL_HALO:POOL_HALO + tm, :] = n
        inv = _pool_inv_count(i, tm)
        for gi, w in enumerate(POOL_WINDOWS):
            cs = slice(gi * cg, (gi + 1) * cg)
            s = ext[POOL_HALO:POOL_HALO + tm, cs]
            for j in range(1, w):
                s = s + ext[POOL_HALO - j:POOL_HALO - j + tm, cs]
            pooled = (s * inv[gi] - n[:, cs]).astype(BF16)
            p_ref[:, cs] = pooled
            ho_ref[:, cs] = hv[:, cs] + _dot(pooled, w_ref[gi]) * s_ref[:, cs]

    row = lambda: pl.BlockSpec((tm, d), lambda i: (i, 0))
    return pl.pallas_call(
        body, name="pool_fwd", grid=(t // tm,),
        in_specs=[row(), pl.BlockSpec((POOL_HALO, d), _prev_halo(tm, POOL_HALO)), _const_spec((1, d)),
                  _const_spec(poolw.shape), _const_spec((1, d))],
        out_specs=[row(), row()],
        out_shape=[jax.ShapeDtypeStruct((t, d), F32), jax.ShapeDtypeStruct((t, d), BF16)],
        scratch_shapes=[pltpu.VMEM((POOL_HALO + tm, d), F32)],
        compiler_params=_cp(("parallel",)),
    )(h, h, g, poolw, scale)


def _mm_tn(a, b, *, name, ta, tb, tt, blocked_out=False, out_dtype=F32):
    t, ka = a.shape
    n = b.shape[1]
    ta, tb, tt = min(ta, ka), min(tb, n), min(tt, t)
    nt = t // tt

    def body(a_ref, b_ref, o_ref, acc):
        @pl.when(pl.program_id(2) == 0)
        def _():
            acc[...] = jnp.zeros_like(acc)
        acc[...] += _dot_tn(a_ref[...].astype(BF16), b_ref[...].astype(BF16))

        @pl.when(pl.program_id(2) == nt - 1)
        def _():
            o_ref[...] = acc[...].astype(out_dtype)

    if blocked_out:
        assert ta == ka
        out_shape = jax.ShapeDtypeStruct((n // tb, ka, tb), out_dtype)
        out_spec = pl.BlockSpec((None, ta, tb), lambda i, j, k: (j, i, 0))
    else:
        out_shape = jax.ShapeDtypeStruct((ka, n), out_dtype)
        out_spec = pl.BlockSpec((ta, tb), lambda i, j, k: (i, j))
    return pl.pallas_call(
        body, name=name, grid=(ka // ta, n // tb, nt),
        in_specs=[pl.BlockSpec((tt, ta), lambda i, j, k: (k, i)), pl.BlockSpec((tt, tb), lambda i, j, k: (k, j))],
        out_specs=out_spec, out_shape=out_shape, scratch_shapes=[pltpu.VMEM((ta, tb), F32)],
        compiler_params=_cp(("parallel", "parallel", "arbitrary")),
    )(a, b)


def _mlp_bwd(dho, h, a, g, wup, wdown, *, name, tm=512):
    t, d = h.shape
    n_blk, _, fb = wup.shape
    f = n_blk * fb
    tm = min(tm, t)

    def body(do_ref, h_ref, a_ref, g_ref, wu_ref, wd_ref, dh_ref, da_ref, dg_ref):
        @pl.when(pl.program_id(0) == 0)
        def _():
            dg_ref[...] = jnp.zeros_like(dg_ref)
        dho_v = do_ref[...]
        dob = dho_v.astype(BF16)
        dn = jnp.zeros((tm, d), F32)
        for k in range(n_blk):
            dz = _dot_nt(dob, wd_ref[k * fb:(k + 1) * fb, :])
            da = (dz * (2.0 * jnp.maximum(a_ref[:, k * fb:(k + 1) * fb].astype(F32), 0.0))).astype(BF16)
            da_ref[:, k * fb:(k + 1) * fb] = da
            dn = dn + _dot_nt(da, wu_ref[k])
        dh, dg = _norm_bwd(dn, h_ref[...], g_ref[...])
        dh_ref[...] = dho_v + dh
        dg_ref[...] += dg

    row = lambda n_: pl.BlockSpec((tm, n_), lambda i: (i, 0))
    return pl.pallas_call(
        body, name=name, grid=(t // tm,),
        in_specs=[row(d), row(d), row(f), _const_spec((1, d)), _const_spec(wup.shape), _const_spec(wdown.shape)],
        out_specs=[row(d), row(f), pl.BlockSpec((8, d), lambda i: (0, 0))],
        out_shape=[jax.ShapeDtypeStruct((t, d), F32), jax.ShapeDtypeStruct((t, f), BF16),
                   jax.ShapeDtypeStruct((8, d), F32)],
        compiler_params=_cp(("arbitrary",)),
    )(dho, h, a, g, wup, wdown)


def _pool_bwd(dho, h, pooled, g, poolw, scale, *, tm=256):
    t, d = h.shape
    tm = min(tm, t)
    ng = len(POOL_WINDOWS)
    cg = d // ng
    nsteps = t // tm

    def body(do_ref, dn_ref, h_ref, p_ref, g_ref, w_ref, s_ref, dh_ref, dw_ref, ds_ref, dg_ref, ext):
        i = pl.program_id(0)

        @pl.when(i == 0)
        def _():
            dw_ref[...] = jnp.zeros_like(dw_ref)
            ds_ref[...] = jnp.zeros_like(ds_ref)
            dg_ref[...] = jnp.zeros_like(dg_ref)
        dho_v = do_ref[...]
        sv = s_ref[...]
        dyp = (dho_v * sv).astype(BF16)
        dyp_halo = (dn_ref[...] * sv).astype(BF16)
        inv = _pool_inv_count(i, tm)
        tnext = ((i + 1) * tm + lax.broadcasted_iota(jnp.int32, (POOL_HALO, 1), 0) + 1).astype(F32)
        last = i == nsteps - 1
        ypre_parts, dpooled_parts = [], []
        for gi, w in enumerate(POOL_WINDOWS):
            cs = slice(gi * cg, (gi + 1) * cg)
            pg = p_ref[:, cs]
            ypre_parts.append(_dot(pg, w_ref[gi]))
            dw_ref[gi] += _dot_tn(pg, dyp[:, cs])
            dpool = _dot_nt(dyp[:, cs], w_ref[gi])
            dpooled_parts.append(dpool)
            ext[0:tm, cs] = dpool * inv[gi]
            dpool_halo = _dot_nt(dyp_halo[:, cs], w_ref[gi]) * (1.0 / jnp.minimum(tnext, float(w)))
            ext[tm:tm + POOL_HALO, cs] = jnp.where(last, 0.0, dpool_halo)
        ds_ref[...] += _rows8(dho_v * jnp.concatenate(ypre_parts, axis=1))
        dn_parts = []
        for gi, w in enumerate(POOL_WINDOWS):
            cs = slice(gi * cg, (gi + 1) * cg)
            s = ext[0:tm, cs]
            for j in range(1, w):
                s = s + ext[j:j + tm, cs]
            dn_parts.append(s - dpooled_parts[gi])
        dh, dg = _norm_bwd(jnp.concatenate(dn_parts, axis=1), h_ref[...], g_ref[...])
        dh_ref[...] = dho_v + dh
        dg_ref[...] += dg

    row = lambda: pl.BlockSpec((tm, d), lambda i: (i, 0))
    acc8 = lambda: pl.BlockSpec((8, d), lambda i: (0, 0))
    return pl.pallas_call(
        body, name="pool_bwd", grid=(nsteps,),
        in_specs=[row(), pl.BlockSpec((POOL_HALO, d), _next_halo(tm, POOL_HALO, t)), row(), row(),
                  _const_spec((1, d)), _const_spec(poolw.shape), _const_spec((1, d))],
        out_specs=[row(), pl.BlockSpec((ng, cg, cg), lambda i: (0, 0, 0)), acc8(), acc8()],
        out_shape=[jax.ShapeDtypeStruct((t, d), F32), jax.ShapeDtypeStruct((ng, cg, cg), F32),
                   jax.ShapeDtypeStruct((8, d), F32), jax.ShapeDtypeStruct((8, d), F32)],
        scratch_shapes=[pltpu.VMEM((tm + POOL_HALO, d), F32)],
        compiler_params=_cp(("arbitrary",)),
    )(dho, dho, h, pooled, g, poolw, scale)


def _outproj_bwd(dh, o, wout, *, tm=512):
    t, d = dh.shape
    tm = min(tm, t)

    def body(dh_ref, o_ref, w_ref, da_ref, dat_ref, dc_ref):
        dhb = dh_ref[...].astype(BF16)
        dc_ref[...] = _dot_nt(dhb, w_ref[ATTN_W:, :])
        for p in range(ATTN_W // PAIR):
            datt = _dot_nt(dhb, w_ref[p * PAIR:(p + 1) * PAIR, :])
            prod = datt * o_ref[:, p * PAIR:(p + 1) * PAIR].astype(F32)
            for hh in range(2):
                lane, head, aux = _head_lanes(hh)
                delta = jnp.sum(jnp.where(head, prod, 0.0), axis=1, keepdims=True)
                aug = _put_pieces(lane, aux + AUX_BIAS, -delta, jnp.where(head, datt, 0.0))
                da_ref[2 * p + hh] = aug.astype(BF16)
                dat_ref[2 * p + hh] = aug.T.astype(BF16)

    row = lambda n_: pl.BlockSpec((tm, n_), lambda i: (i, 0))
    return pl.pallas_call(
        body, name="outproj_bwd", grid=(t // tm,),
        in_specs=[row(d), row(ATTN_W), _const_spec(wout.shape)],
        out_specs=[pl.BlockSpec((N_HEADS, tm, PAIR), lambda i: (0, i, 0)),
                   pl.BlockSpec((N_HEADS, PAIR, tm), lambda i: (0, 0, i)), row(CONV_CH)],
        out_shape=[jax.ShapeDtypeStruct((N_HEADS, t, PAIR), BF16), jax.ShapeDtypeStruct((N_HEADS, PAIR, t), BF16),
                   jax.ShapeDtypeStruct((t, CONV_CH), F32)],
        compiler_params=_cp(("parallel",)),
    )(dh, o, wout)


def _conv_bwd(bcx, dcv, conv_w, *, tm=512):
    t = bcx.shape[0]
    tm = min(tm, t)
    ch = CONV_CH
    nsteps = t // tm

    def body(b_ref, c_ref, x_ref, hc_ref, hx_ref, d_ref, nb_ref, nd_ref, w_ref, o_ref, dw_ref, ext_u, ext_d):
        i = pl.program_id(0)

        @pl.when(i == 0)
        def _():
            dw_ref[...] = jnp.zeros_like(dw_ref)
        b, c, x, dcv_v = b_ref[...], c_ref[...], x_ref[...], d_ref[...]
        ext_u[0:CONV_HALO, :] = jnp.where(i == 0, 0.0, hc_ref[...] * hx_ref[...])
        ext_u[CONV_HALO:CONV_HALO + tm, :] = c * x
        dconv = dcv_v * b
        ext_d[0:tm, :] = dconv
        ext_d[tm:tm + CONV_HALO, :] = jnp.where(i == nsteps - 1, 0.0, nd_ref[...] * nb_ref[...])
        u = [ext_u[CONV_HALO - 2 + k:CONV_HALO - 2 + k + tm, :] for k in range(3)]
        conv = w_ref[0:1, :] * u[0] + w_ref[1:2, :] * u[1] + w_ref[2:3, :] * u[2]
        du = (w_ref[2:3, :] * dconv + w_ref[1:2, :] * ext_d[1:1 + tm, :] + w_ref[0:1, :] * ext_d[2:2 + tm, :])
        o_ref[:, 0:ch] = (dcv_v * conv).astype(BF16)
        o_ref[:, ch:2 * ch] = (du * x).astype(BF16)
        o_ref[:, 2 * ch:3 * ch] = (du * c).astype(BF16)
        for k in range(3):
            dw_ref[k] += _rows8(dconv * u[k])

    col = lambda k: pl.BlockSpec((tm, ch), lambda i: (i, k))
    prev = lambda k: pl.BlockSpec((CONV_HALO, ch), lambda i: (_prev_halo(tm, CONV_HALO)(i)[0], k))
    nxt = lambda k: pl.BlockSpec((CONV_HALO, ch), lambda i: (_next_halo(tm, CONV_HALO, t)(i)[0], k))
    return pl.pallas_call(
        body, name="conv_bwd", grid=(nsteps,),
        in_specs=[col(0), col(1), col(2), prev(1), prev(2), col(0), nxt(0), nxt(0), _const_spec((8, ch))],
        out_specs=[pl.BlockSpec((tm, 3 * ch), lambda i: (i, 0)), pl.BlockSpec((3, 8, ch), lambda i: (0, 0, 0))],
        out_shape=[jax.ShapeDtypeStruct((t, 3 * ch), BF16), jax.ShapeDtypeStruct((3, 8, ch), F32)],
        scratch_shapes=[pltpu.VMEM((CONV_HALO + tm, ch), F32), pltpu.VMEM((tm + CONV_HALO, ch), F32)],
        compiler_params=_cp(("arbitrary",)),
    )(bcx, bcx, bcx, bcx, bcx, dcv, bcx, dcv, conv_w)


def _attn_bwd(q_bwd, do_aug, q_bwd_t, do_aug_t, k_aug, v_aug, gblocks, *, tq=1024):
    t = q_bwd.shape[1]
    tq = min(tq, t)
    tk = tq // 2
    nq, nk = t // tq, t // tk
    n_pairs = ATTN_W // PAIR
    n_g = len(gblocks)

    def body(q_ref, do_ref, qt_ref, dot_ref, k_ref, v_ref, *rest):
        dq_ref, dqx_ref, dk_ref, dkx_ref, dv_ref = rest[n_g:n_g + 5]
        dq_scr = rest[2 * n_g + 5]
        scatter = _Exchange(rest[:n_g], rest[n_g + 5:2 * n_g + 5], *rest[2 * n_g + 6:], gather=False)
        j = pl.program_id(1)

        @pl.when((pl.program_id(0) == 0) & (j == 0))
        def _():
            scatter.start()

        @pl.when(j == 0)
        def _():
            dq_scr[...] = jnp.zeros_like(dq_scr)
        row = lax.broadcasted_iota(jnp.int32, (tq, tk), 0)
        col = lax.broadcasted_iota(jnp.int32, (tq, tk), 1)
        k = [k_ref[0], k_ref[1]]
        v = [v_ref[0], v_ref[1]]

        def step(i, carry, diag):
            qs = pl.multiple_of(i * tq, tq)
            out = []
            for hh in range(2):
                dk_a, dv_a = carry[hh]
                q = q_ref[hh, pl.ds(qs, tq), :]
                dov = do_ref[hh, pl.ds(qs, tq), :]
                p = jnp.exp2(_dot_nt(q, k[hh]))
                if diag:
                    p = jnp.where(col + (j * tk - i * tq) <= row, p, 0.0)
                ds = (p * _dot_nt(dov, v[hh])).astype(BF16)
                dv_a = dv_a + _dot(dot_ref[hh, :, pl.ds(qs, tq)], p.astype(BF16))
                dk_a = dk_a + _dot(qt_ref[hh, :, pl.ds(qs, tq)], ds)
                dq_scr[hh, pl.ds(qs, tq), :] += _dot(ds, k[hh])
                out.append((dk_a, dv_a))
            return tuple(out)

        zero = (jnp.zeros((PAIR, tk), F32), jnp.zeros((PAIR, tk), F32))
        carry = step(j // 2, (zero, zero), True)
        (dk0, dv0), (dk1, dv1) = lax.fori_loop(j // 2 + 1, nq, functools.partial(step, diag=False), carry)
        first_t = lax.broadcasted_iota(jnp.int32, (PAIR, 1), 0) < HEAD_DIM
        first = lax.broadcasted_iota(jnp.int32, (1, PAIR), 1) < HEAD_DIM
        dk_ref[...] = (jnp.where(first_t, dk0, dk1).T * (1.0 / LOG2E)).astype(BF16)
        dkx_ref[...] = jnp.where(first_t, dk1, dk0).T
        dv_ref[...] = jnp.where(first_t, dv0, dv1).T.astype(BF16)

        @pl.when(j == nk - 1)
        def _():
            dq_ref[...] = (jnp.where(first, dq_scr[0], dq_scr[1]) * Q_SCALE).astype(BF16)
            dqx_ref[...] = jnp.where(first, dq_scr[1], dq_scr[0])

        @pl.when((pl.program_id(0) == n_pairs - 1) & (j == nk - 1))
        def _():
            scatter.wait()

    resident = lambda: pl.BlockSpec((2, t, PAIR), lambda p, j: (p, 0, 0), pipeline_mode=pl.Buffered(1))
    resident_t = lambda: pl.BlockSpec((2, PAIR, t), lambda p, j: (p, 0, 0), pipeline_mode=pl.Buffered(1))
    kv_in = lambda: pl.BlockSpec((2, tk, PAIR), lambda p, j: (p, j, 0))
    whole = lambda: pl.BlockSpec((t, PAIR), lambda p, j: (0, p))
    tile = lambda: pl.BlockSpec((tk, PAIR), lambda p, j: (j, p))
    b16 = jax.ShapeDtypeStruct((t, ATTN_W), BF16)
    f32 = jax.ShapeDtypeStruct((t, ATTN_W), F32)
    res = pl.pallas_call(
        body, name="attn_bwd", grid=(n_pairs, nk),
        in_specs=[resident(), resident(), resident_t(), resident_t(), kv_in(), kv_in()] + [HBM_SPEC] * n_g,
        out_specs=[whole(), whole(), tile(), tile(), tile()] + [HBM_SPEC] * n_g,
        out_shape=[b16, f32, b16, f32, b16] + [jax.ShapeDtypeStruct(g.shape, g.dtype) for g in gblocks],
        scratch_shapes=[pltpu.VMEM((2, t, PAIR), F32)] + _Exchange.scratch(n_g),
        compiler_params=_cp(("arbitrary", "arbitrary")),
    )(q_bwd, do_aug, q_bwd_t, do_aug_t, k_aug, v_aug, *gblocks)
    return res[:5], res[5:]


def _fgate_bwd(dqx, dkx, sgate, *, tm=256):
    t = sgate.shape[0]
    tm = min(tm, t)
    nsteps = t // tm

    def body(dq_ref, dk_ref, sg_ref, df_ref, dbf_ref, carry):
        @pl.when(pl.program_id(0) == 0)
        def _():
            carry[...] = jnp.zeros_like(carry)
            dbf_ref[...] = jnp.zeros_like(dbf_ref)
        lane = lax.broadcasted_iota(jnp.int32, (ATTN_W, F_PAD), 0)
        head = lax.broadcasted_iota(jnp.int32, (ATTN_W, F_PAD), 1)
        aux = (head // 2) * PAIR + HEAD_DIM * (1 - head % 2)
        valid = head < N_HEADS
        pick_r = (valid & (lane == aux + AUX_ROWSUM)).astype(F32)
        pick_c = (valid & (lane == aux + AUX_BIAS)).astype(F32)
        hp = lax.Precision.HIGHEST
        dcum = (jnp.dot(dq_ref[...], pick_r, preferred_element_type=F32, precision=hp)
                + jnp.dot(dk_ref[...], pick_c, preferred_element_type=F32, precision=hp))
        r = lax.broadcasted_iota(jnp.int32, (tm, tm), 0)
        c = lax.broadcasted_iota(jnp.int32, (tm, tm), 1)
        tri = (c >= r).astype(F32)
        rc = jnp.dot(tri, dcum, preferred_element_type=F32, precision=hp) + carry[...]
        carry[...] = rc[0:1, :]
        df = rc * sg_ref[...]
        df_ref[...] = df.astype(BF16)
        dbf_ref[...] += _rows8(df)

    rev = lambda i: nsteps - 1 - i
    return pl.pallas_call(
        body, name="fgate_bwd", grid=(nsteps,),
        in_specs=[pl.BlockSpec((tm, ATTN_W), lambda i: (rev(i), 0)), pl.BlockSpec((tm, ATTN_W), lambda i: (rev(i), 0)),
                  pl.BlockSpec((tm, F_PAD), lambda i: (rev(i), 0))],
        out_specs=[pl.BlockSpec((tm, F_PAD), lambda i: (rev(i), 0)), pl.BlockSpec((8, F_PAD), lambda i: (0, 0))],
        out_shape=[jax.ShapeDtypeStruct((t, F_PAD), BF16), jax.ShapeDtypeStruct((8, F_PAD), F32)],
        scratch_shapes=[pltpu.VMEM((1, F_PAD), F32)],
        compiler_params=_cp(("arbitrary",)),
    )(dqx, dkx, sgate)


def _inproj_bwd(dq, dk, dv, df, dbcx, dh, x, g, win_p, gblock, *, tm=512):
    t, d = x.shape
    tm = min(tm, t)
    nsteps = t // tm
    n_qkv = 3 * ATTN_W

    def body(dq_ref, dk_ref, dv_ref, df_ref, db_ref, dh_ref, x_ref, g_ref, w_ref, gb_ref, gx_ref, dg_ref, land_ref,
             *sems):
        scatter = _Exchange([gb_ref], [land_ref], *sems, gather=False)

        @pl.when(pl.program_id(0) == 0)
        def _():
            scatter.start()
            dg_ref[...] = jnp.zeros_like(dg_ref)
        dn = _dot_nt(df_ref[...], w_ref[:, n_qkv:n_qkv + F_PAD])
        for k, r in enumerate((dq_ref, dk_ref, dv_ref)):
            dn = dn + _dot_nt(r[...], w_ref[:, k * ATTN_W:(k + 1) * ATTN_W])
        for k in range(3):
            c0 = n_qkv + F_PAD + k * CONV_CH
            dn = dn + _dot_nt(db_ref[:, k * CONV_CH:(k + 1) * CONV_CH], w_ref[:, c0:c0 + CONV_CH])
        dx, dg = _norm_bwd(dn, x_ref[...], g_ref[...])
        gx_ref[...] = dh_ref[...] + dx
        dg_ref[...] += dg

        @pl.when(pl.program_id(0) == nsteps - 1)
        def _():
            scatter.wait()

    row = lambda n_: pl.BlockSpec((tm, n_), lambda i: (i, 0))
    return pl.pallas_call(
        body, name="inproj_bwd", grid=(nsteps,),
        in_specs=[row(ATTN_W), row(ATTN_W), row(ATTN_W), row(F_PAD), row(3 * CONV_CH), row(d), row(d),
                  _const_spec((1, d)), _const_spec(win_p.shape), HBM_SPEC],
        out_specs=[row(d), pl.BlockSpec((8, d), lambda i: (0, 0)), HBM_SPEC],
        out_shape=[jax.ShapeDtypeStruct((t, d), F32), jax.ShapeDtypeStruct((8, d), F32),
                   jax.ShapeDtypeStruct(gblock.shape, gblock.dtype)],
        scratch_shapes=_Exchange.scratch(1),
        compiler_params=_cp(("arbitrary",)),
    )(dq, dk, dv, df, dbcx, dh, x, g, win_p, gblock)


LATE = ("w_out_0", "w_up_0", "w_down_0", "pool_w_1", "w_up_1", "w_down_1")


def _local_step(x, target, gains, b_f, conv_w, pool_scale, win_p, shards):
    d = x.shape[1]
    n0, qkv, flog, bcx = _norm_inproj(x, gains["mix0"], win_p)
    q_aug_t, k_aug, v_aug, v_aug_t, sgate = _fgate_prep(flog, b_f, qkv)
    att, q_bwd, q_bwd_t, gathered = _attn_fwd(q_aug_t, k_aug, v_aug_t, [shards[n] for n in LATE])
    g = dict(zip(LATE, gathered))
    wout = g["w_out_0"].reshape(d, d)
    wup0, wup1 = g["w_up_0"], g["w_up_1"]
    wdown0, wdown1 = g["w_down_0"].reshape(-1, d), g["w_down_1"].reshape(-1, d)
    n_grp = len(POOL_WINDOWS)
    cg = d // n_grp
    poolw = g["pool_w_1"].reshape(N_DEV, n_grp, cg // N_DEV, cg).transpose(1, 0, 2, 3).reshape(n_grp, cg, cg)
    cv = _conv_fwd(bcx, conv_w)
    h1 = _outproj(att, cv, x, wout)
    h2, n1, a0, z0 = _mlp_fwd(h1, gains["ffn0"], wup0, wdown0, name="mlp_fwd0")
    h3, pooled = _pool_fwd(h2, gains["mix1"], poolw, pool_scale)
    loss, dh4, dg_final, n3, a1, z1 = _mlp_fwd_loss(h3, gains["ffn1"], wup1, wdown1, gains["final"], target,
                                                    name="mlp_fwd1")
    f = a1.shape[1]
    fb = f // N_DEV
    dh3, da1, dg_ffn1 = _mlp_bwd(dh4, h3, a1, gains["ffn1"], wup1, wdown1, name="mlp_bwd1")
    dwdown1 = _mm_tn(z1, dh4, name="dwdown1", ta=1024, tb=1024, tt=1024, out_dtype=BF16)
    dwup1 = _mm_tn(n3, da1, name="dwup1", ta=d, tb=fb, tt=2048, blocked_out=True, out_dtype=BF16)
    dh2, dpoolw, dscale, dg_mix1 = _pool_bwd(dh3, h2, pooled, gains["mix1"], poolw, pool_scale)
    dh1, da0, dg_ffn0 = _mlp_bwd(dh2, h1, a0, gains["ffn0"], wup0, wdown0, name="mlp_bwd0")
    dwdown0 = _mm_tn(z0, dh2, name="dwdown0", ta=1024, tb=1024, tt=1024, out_dtype=BF16)
    dwup0 = _mm_tn(n1, da0, name="dwup0", ta=d, tb=fb, tt=2048, blocked_out=True, out_dtype=BF16)
    do_aug, do_aug_t, dcv = _outproj_bwd(dh1, att, wout)
    dwout = jnp.concatenate([_mm_tn(att, dh1, name="dwout_att", ta=512, tb=1024, tt=2048, out_dtype=BF16),
                             _mm_tn(cv, dh1, name="dwout_conv", ta=512, tb=1024, tt=2048, out_dtype=BF16)], axis=0)
    dbcx, dconvw = _conv_bwd(bcx, dcv, conv_w)
    gblocks = {
        "w_out_0": dwout.reshape(N_DEV, d // N_DEV, d), "w_up_0": dwup0, "w_up_1": dwup1,
        "w_down_0": dwdown0.reshape(N_DEV, -1, d), "w_down_1": dwdown1.reshape(N_DEV, -1, d),
        "pool_w_1": dpoolw.astype(BF16).reshape(n_grp, N_DEV, cg // N_DEV, cg).transpose(1, 0, 2, 3).reshape(
            N_DEV, n_grp * (cg // N_DEV), cg),
    }
    (dq, dqx, dk, dkx, dv), landed = _attn_bwd(q_bwd, do_aug, q_bwd_t, do_aug_t, k_aug, v_aug,
                                               [gblocks[n] for n in LATE])
    df, dbf = _fgate_bwd(dqx, dkx, sgate)
    dwin = jnp.concatenate(
        [_mm_tn(n0, dq, name="dwin_q", ta=d, tb=512, tt=2048, out_dtype=BF16),
         _mm_tn(n0, dk, name="dwin_k", ta=d, tb=512, tt=2048, out_dtype=BF16),
         _mm_tn(n0, dv, name="dwin_v", ta=d, tb=512, tt=2048, out_dtype=BF16),
         _mm_tn(n0, df, name="dwin_f", ta=d, tb=128, tt=2048, out_dtype=BF16)[:, :N_HEADS],
         _mm_tn(n0, dbcx, name="dwin_bcx", ta=d, tb=512, tt=2048, out_dtype=BF16)], axis=1)
    dwin_blocks = dwin.reshape(d, N_DEV, dwin.shape[1] // N_DEV).transpose(1, 0, 2)
    grad_x, dg_mix0, landed_win = _inproj_bwd(dq, dk, dv, df, dbcx, dh1, x, gains["mix0"], win_p, dwin_blocks)
    small = dict(mix0=dg_mix0, ffn0=dg_ffn0, mix1=dg_mix1, pool_scale=dscale, ffn1=dg_ffn1, final=dg_final,
                 b_f=dbf, conv_w=dconvw)
    return loss, grad_x, dict(zip(LATE + ("w_in_0",), tuple(landed) + (landed_win,))), small


def _mesh_places():
    x, y, c = lax.axis_index("x"), lax.axis_index("y"), lax.axis_index("c")
    chips = [(1 - x, y), (x, 1 - y), (1 - x, 1 - y)]
    return (x, y, c), (x, y, 1 - c), chips


def _all_gather(shards):
    n = len(shards)

    def body(*refs):
        ins, outs = refs[:n], refs[n:2 * n]
        send_sems, recv_sems, local_sems = refs[2 * n:]
        me, sib, chips = _mesh_places()
        c = me[2]

        def copy(ai, k, block, to, src=None):
            dst = outs[ai].at[_slot(*block)]
            return pltpu.make_async_remote_copy(
                src_ref=dst if src is None else src, dst_ref=dst, send_sem=send_sems.at[7 * ai + k],
                recv_sem=recv_sems.at[7 * ai + k], device_id=to, device_id_type=MESH)

        mine = [pltpu.make_async_copy(ins[ai], outs[ai].at[_slot(*me)], local_sems.at[ai]) for ai in range(n)]
        for cp in mine:
            cp.start()
        first = []
        for ai in range(n):
            first.append(copy(ai, 0, me, sib, src=ins[ai]))
            first += [copy(ai, 1 + j, me, (*chip, c), src=ins[ai]) for j, chip in enumerate(chips)]
        for cp in first:
            cp.start()
        passed = []
        for ai in range(n):
            for j, chip in enumerate(chips):
                copy(ai, 1 + j, (*chip, c), me).wait_recv()
                cp = copy(ai, 4 + j, (*chip, c), sib)
                cp.start()
                passed.append(cp)
        for ai in range(n):
            copy(ai, 0, sib, me).wait_recv()
            for j, chip in enumerate(chips):
                copy(ai, 4 + j, (*chip, 1 - c), me).wait_recv()
        for cp in first + passed:
            cp.wait_send()
        for cp in mine:
            cp.wait()

    return pl.pallas_call(
        body, name="all_gather",
        in_specs=[HBM_SPEC] * n, out_specs=[HBM_SPEC] * n,
        out_shape=[jax.ShapeDtypeStruct((N_DEV,) + s.shape, s.dtype) for s in shards],
        scratch_shapes=[pltpu.SemaphoreType.DMA((7 * n,)), pltpu.SemaphoreType.DMA((7 * n,)),
                        pltpu.SemaphoreType.DMA((n,))],
    )(*shards)


SMALL_ROWS = 16


def _small_allreduce(parts):
    n, _, w = parts.shape
    assert n <= SMALL_ROWS

    def body(p_ref, o_ref, gath, send_sems, recv_sems):
        x, y, c = lax.axis_index("x"), lax.axis_index("y"), lax.axis_index("c")
        my = _slot(x, y, c)
        rows = [jnp.sum(p_ref[i], axis=0, keepdims=True) for i in range(n)]
        rows.append(jnp.zeros((SMALL_ROWS - n, w), F32))
        gath[my] = jnp.concatenate(rows, axis=0)
        copies = []
        for k in range(1, N_DEV):
            px, py, pc = x ^ (k >> 2), y ^ ((k >> 1) & 1), c ^ (k & 1)
            cp = pltpu.make_async_remote_copy(
                src_ref=gath.at[my], dst_ref=gath.at[my], send_sem=send_sems.at[k - 1], recv_sem=recv_sems.at[k - 1],
                device_id=(px, py, pc), device_id_type=MESH)
            cp.start()
            copies.append(cp)
        for cp in copies:
            cp.wait()
        acc = gath[0]
        for d in range(1, N_DEV):
            acc = acc + gath[d]
        o_ref[...] = acc

    return pl.pallas_call(
        body, name="small_allreduce",
        in_specs=[VMEM_SPEC], out_specs=VMEM_SPEC,
        out_shape=jax.ShapeDtypeStruct((SMALL_ROWS, w), F32),
        scratch_shapes=[pltpu.VMEM((N_DEV, SMALL_ROWS, w), F32), pltpu.SemaphoreType.DMA((N_DEV - 1,)),
                        pltpu.SemaphoreType.DMA((N_DEV - 1,))],
    )(parts)


def _adamw(g, w, m, v, *, name, tm=256):
    r, c = g.shape
    tm = tm if r % tm == 0 else r
    bc1 = 1.0 - ADAM_B1 ** ADAM_STEP
    bc2 = 1.0 - ADAM_B2 ** ADAM_STEP

    def body(g_ref, w_ref, m_ref, v_ref, d_ref, nm_ref, nv_ref):
        gv = g_ref[...]
        nm = ADAM_B1 * m_ref[...] + (1.0 - ADAM_B1) * gv
        nv = ADAM_B2 * v_ref[...] + (1.0 - ADAM_B2) * jnp.square(gv)
        nm_ref[...] = nm
        nv_ref[...] = nv
        d_ref[...] = -ADAM_LR * ((nm / bc1) / (jnp.sqrt(nv / bc2) + ADAM_EPS) + ADAM_WD * w_ref[...])

    blk = pl.BlockSpec((tm, c), lambda i: (i, 0))
    shp = jax.ShapeDtypeStruct((r, c), F32)
    return pl.pallas_call(
        body, name=name, grid=(r // tm,), in_specs=[blk] * 4, out_specs=[blk] * 3, out_shape=[shp] * 3,
        compiler_params=_cp(("parallel",)),
    )(g, w, m, v)


def _adamw_sum(parts, w, m, v, *, name, tm=128):
    _, r, c = parts.shape
    tm = tm if r % tm == 0 else r
    bc1 = 1.0 - ADAM_B1 ** ADAM_STEP
    bc2 = 1.0 - ADAM_B2 ** ADAM_STEP

    def body(p_ref, w_ref, m_ref, v_ref, g_ref, d_ref, nm_ref, nv_ref):
        gv = p_ref[0].astype(F32)
        for k in range(1, N_DEV):
            gv = gv + p_ref[k].astype(F32)
        g_ref[...] = gv
        nm = ADAM_B1 * m_ref[...] + (1.0 - ADAM_B1) * gv
        nv = ADAM_B2 * v_ref[...] + (1.0 - ADAM_B2) * jnp.square(gv)
        nm_ref[...] = nm
        nv_ref[...] = nv
        d_ref[...] = -ADAM_LR * ((nm / bc1) / (jnp.sqrt(nv / bc2) + ADAM_EPS) + ADAM_WD * w_ref[...])

    blk = pl.BlockSpec((tm, c), lambda i: (i, 0))
    shp = jax.ShapeDtypeStruct((r, c), F32)
    return pl.pallas_call(
        body, name=name, grid=(r // tm,), in_specs=[pl.BlockSpec((N_DEV, tm, c), lambda i: (0, i, 0))] + [blk] * 3,
        out_specs=[blk] * 4, out_shape=[shp] * 4, compiler_params=_cp(("parallel",)),
    )(parts, w, m, v)


BIG = ("w_in_0", "w_out_0", "w_up_0", "w_down_0", "pool_w_1", "w_up_1", "w_down_1")
SMALL = ("norm_mix_0", "norm_ffn_0", "norm_mix_1", "pool_scale_1", "norm_ffn_1", "final_norm", "b_f_0", "conv_w_0")
WEIGHTS = ("norm_mix_0", "w_in_0", "b_f_0", "conv_w_0", "w_out_0", "norm_ffn_0", "w_up_0", "w_down_0", "norm_mix_1",
           "pool_w_1", "pool_scale_1", "norm_ffn_1", "w_up_1", "w_down_1", "final_norm")


def _pad_to(a, rows, cols):
    return jnp.pad(a, ((0, rows - a.shape[0]), (0, cols - a.shape[1])))


def _pack_small(p, width):
    rows = [p[n].reshape(1, -1) for n in SMALL[:6]]
    rows.append(_pad_to(p["b_f_0"].reshape(1, -1), 1, width))
    rows.append(_pad_to(p["conv_w_0"], 3, width))
    return _pad_to(jnp.concatenate(rows, axis=0), SMALL_ROWS, width)


def _unpack_small(a, like):
    out = {n: a[i] for i, n in enumerate(SMALL[:6])}
    out["b_f_0"] = a[6, :like["b_f_0"].shape[0]]
    out["conv_w_0"] = a[7:10, :like["conv_w_0"].shape[1]]
    return out


def kernel(x, norm_mix_0, w_in_0, b_f_0, conv_w_0, w_out_0, norm_ffn_0, w_up_0, w_down_0, norm_mix_1, pool_w_1, pool_scale_1, norm_ffn_1, w_up_1, w_down_1, final_norm, loss_target, m_norm_mix_0, m_w_in_0, m_b_f_0, m_conv_w_0, m_w_out_0, m_norm_ffn_0, m_w_up_0, m_w_down_0, m_norm_mix_1, m_pool_w_1, m_pool_scale_1, m_norm_ffn_1, m_w_up_1, m_w_down_1, m_final_norm, v_norm_mix_0, v_w_in_0, v_b_f_0, v_conv_w_0, v_w_out_0, v_norm_ffn_0, v_w_up_0, v_w_down_0, v_norm_mix_1, v_pool_w_1, v_pool_scale_1, v_norm_ffn_1, v_w_up_1, v_w_down_1, v_final_norm):
    w = dict(norm_mix_0=norm_mix_0, w_in_0=w_in_0, b_f_0=b_f_0, conv_w_0=conv_w_0, w_out_0=w_out_0,
             norm_ffn_0=norm_ffn_0, w_up_0=w_up_0, w_down_0=w_down_0, norm_mix_1=norm_mix_1, pool_w_1=pool_w_1,
             pool_scale_1=pool_scale_1, norm_ffn_1=norm_ffn_1, w_up_1=w_up_1, w_down_1=w_down_1, final_norm=final_norm)
    m = dict(norm_mix_0=m_norm_mix_0, w_in_0=m_w_in_0, b_f_0=m_b_f_0, conv_w_0=m_conv_w_0, w_out_0=m_w_out_0,
             norm_ffn_0=m_norm_ffn_0, w_up_0=m_w_up_0, w_down_0=m_w_down_0, norm_mix_1=m_norm_mix_1,
             pool_w_1=m_pool_w_1, pool_scale_1=m_pool_scale_1, norm_ffn_1=m_norm_ffn_1, w_up_1=m_w_up_1,
             w_down_1=m_w_down_1, final_norm=m_final_norm)
    v = dict(norm_mix_0=v_norm_mix_0, w_in_0=v_w_in_0, b_f_0=v_b_f_0, conv_w_0=v_conv_w_0, w_out_0=v_w_out_0,
             norm_ffn_0=v_norm_ffn_0, w_up_0=v_w_up_0, w_down_0=v_w_down_0, norm_mix_1=v_norm_mix_1,
             pool_w_1=v_pool_w_1, pool_scale_1=v_pool_scale_1, norm_ffn_1=v_norm_ffn_1, w_up_1=v_w_up_1,
             w_down_1=v_w_down_1, final_norm=v_final_norm)
    d = x.shape[-1]
    n_in = w_in_0.shape[1] * N_DEV
    n_qkv = 3 * ATTN_W
    pool_g, pool_rows, pool_c = pool_w_1.shape

    def shard2d(p):
        return {n: (p[n].reshape(pool_g * pool_rows, pool_c) if n == "pool_w_1" else p[n]) for n in BIG}
    w2, m2, v2 = shard2d(w), shard2d(m), shard2d(v)

    conv_cols = conv_w_0.shape[1]
    win_g8, conv_g8 = _all_gather([w_in_0.astype(BF16), _pad_to(conv_w_0, 8, 128)])
    conv_full = conv_g8[:, :, :conv_cols].transpose(1, 0, 2).reshape(8, N_DEV * conv_cols)
    win = win_g8.transpose(1, 0, 2).reshape(d, n_in)
    win_p = jnp.concatenate([win[:, :n_qkv], _pad_to(win[:, n_qkv:n_qkv + N_HEADS], d, F_PAD),
                             win[:, n_qkv + N_HEADS:]], axis=1)

    gains = dict(mix0=norm_mix_0.reshape(1, d), ffn0=norm_ffn_0.reshape(1, d), mix1=norm_mix_1.reshape(1, d),
                 ffn1=norm_ffn_1.reshape(1, d), final=final_norm.reshape(1, d))
    dev = _slot(lax.axis_index("x"), lax.axis_index("y"), lax.axis_index("c"))
    loss8, grad_x, landed, small = _local_step(
        x[0], loss_target[0], gains, _pad_to(b_f_0.reshape(1, -1), 1, F_PAD), conv_full, pool_scale_1.reshape(1, d),
        win_p, {n: w2[n].astype(BF16) for n in LATE})
    loss = lax.psum(loss8[0, 0], ("x", "y", "c"))

    parts = jnp.concatenate(
        [small[k][None] for k in ("mix0", "ffn0", "mix1", "pool_scale", "ffn1", "final")]
        + [_pad_to(small["b_f"], 8, d)[None], jnp.pad(small["conv_w"], ((0, 0), (0, 0), (0, d - CONV_CH)))], axis=0)
    tot = _small_allreduce(parts)
    conv_g = lax.dynamic_slice(tot, (7, dev * conv_cols), (3, conv_cols))
    gs = tot.at[7:10].set(_pad_to(conv_g, 3, d))

    grads, deltas, new_m, new_v = {}, {}, {}, {}
    for n in BIG:
        gr, dl, nm, nv = _adamw_sum(landed[n], w2[n], m2[n], v2[n], name="adamw_" + n)
        for dst, val in ((grads, gr), (deltas, dl), (new_m, nm), (new_v, nv)):
            dst[n] = val.reshape(w[n].shape)
    dl, nm, nv = _adamw(gs, _pack_small(w, d), _pack_small(m, d), _pack_small(v, d), name="adamw_small")
    for dst, val in ((grads, gs), (deltas, dl), (new_m, nm), (new_v, nv)):
        dst.update(_unpack_small(val, w))
    return (loss, grad_x[None], *[grads[n] for n in WEIGHTS], *[deltas[n] for n in WEIGHTS],
            *[new_m[n] for n in WEIGHTS], *[new_v[n] for n in WEIGHTS])
```

```python
import functools

import jax
import jax.numpy as jnp
from jax import lax
from jax.experimental import pallas as pl
from jax.experimental.pallas import tpu as pltpu

F32 = jnp.float32
BF16 = jnp.bfloat16

N_DEV = 8
N_HEADS = 8
HEAD_DIM = 64
PAIR = 2 * HEAD_DIM
ATTN_W = N_HEADS * HEAD_DIM
CONV_CH = 512
F_PAD = 128
POOL_WINDOWS = (2, 4, 8, 16)
POOL_HALO = 16
CONV_HALO = 8
RMS_EPS = 1e-6
Q_SCALE = HEAD_DIM ** -0.5
LOG2E = 1.4426950408889634
NEG = -1e30
AUX_BIAS = 0
AUX_LSE = 3
AUX_ROWSUM = 6
ADAM_LR, ADAM_B1, ADAM_B2, ADAM_EPS, ADAM_WD, ADAM_STEP = 0.001, 0.9, 0.999, 1e-08, 0.01, 10
MESH = pl.DeviceIdType.MESH
VMEM_LIMIT = 56 * 2**20


def _cp(sem=None, vmem=VMEM_LIMIT, **kw):
    return pltpu.CompilerParams(dimension_semantics=sem, vmem_limit_bytes=vmem, **kw)


def _dot(a, b):
    return jnp.dot(a, b, preferred_element_type=F32)


def _dot_nt(a, b):
    return lax.dot_general(a, b, (((1,), (1,)), ((), ())), preferred_element_type=F32)


def _dot_tn(a, b):
    return lax.dot_general(a, b, (((0,), (0,)), ((), ())), preferred_element_type=F32)


def _rstd(h):
    return lax.rsqrt(jnp.mean(h * h, axis=-1, keepdims=True) + RMS_EPS)


def _rows8(x):
    r, n = x.shape
    return jnp.sum(x.reshape(r // 8, 8, n), axis=0)


def _norm_bwd(dn, h, g):
    r = _rstd(h)
    xhat = h * r
    dy = dn * g
    dh = r * (dy - xhat * jnp.mean(dy * xhat, axis=-1, keepdims=True))
    return dh, _rows8(dn * xhat)


def _const_spec(shape):
    nd = len(shape)
    return pl.BlockSpec(shape, lambda *_: (0,) * nd, pipeline_mode=pl.Buffered(1))


HBM_SPEC = pl.BlockSpec(memory_space=pltpu.HBM)
VMEM_SPEC = pl.BlockSpec(memory_space=pltpu.VMEM)


def _slot(px, py, pc):
    return 4 * px + 2 * py + pc


class _Exchange:
    def __init__(self, srcs, dsts, send_sems, recv_sems, local_sems, gather):
        x, y, c = lax.axis_index("x"), lax.axis_index("y"), lax.axis_index("c")
        me = _slot(x, y, c)
        self.copies = []
        for a, (src, dst) in enumerate(zip(srcs, dsts)):
            self.copies.append(pltpu.make_async_copy(src if gather else src.at[me], dst.at[me], local_sems.at[a]))
            for k in range(1, N_DEV):
                px, py, pc = x ^ (k >> 2), y ^ ((k >> 1) & 1), c ^ (k & 1)
                self.copies.append(pltpu.make_async_remote_copy(
                    src_ref=src if gather else src.at[_slot(px, py, pc)], dst_ref=dst.at[me],
                    send_sem=send_sems.at[(N_DEV - 1) * a + k - 1], recv_sem=recv_sems.at[(N_DEV - 1) * a + k - 1],
                    device_id=(px, py, pc), device_id_type=MESH))

    def start(self):
        for cp in self.copies:
            cp.start()

    def wait(self):
        for cp in self.copies:
            cp.wait()

    @staticmethod
    def scratch(n):
        return [pltpu.SemaphoreType.DMA(((N_DEV - 1) * n,)), pltpu.SemaphoreType.DMA(((N_DEV - 1) * n,)),
                pltpu.SemaphoreType.DMA((n,))]


def _norm_inproj(x, g, win_p, *, tm=512):
    t, d = x.shape
    n_all = win_p.shape[1]
    n_qkv = 3 * ATTN_W
    n_bcx = 3 * CONV_CH
    assert n_all == n_qkv + F_PAD + n_bcx
    tm = min(tm, t)

    def body(x_ref, g_ref, w_ref, n_ref, qkv_ref, f_ref, bcx_ref):
        h = x_ref[...]
        n = (h * _rstd(h) * g_ref[...]).astype(BF16)
        n_ref[...] = n
        for c0 in range(0, n_qkv, 512):
            acc = _dot(n, w_ref[:, c0:c0 + 512])
            if c0 < ATTN_W:
                acc = acc * (Q_SCALE * LOG2E)
            qkv_ref[:, c0:c0 + 512] = acc.astype(BF16)
        f_ref[...] = _dot(n, w_ref[:, n_qkv:n_qkv + F_PAD])
        for c0 in range(0, n_bcx, 512):
            bcx_ref[:, c0:c0 + 512] = _dot(n, w_ref[:, n_qkv + F_PAD + c0:n_qkv + F_PAD + c0 + 512])

    return pl.pallas_call(
        body, name="norm_inproj", grid=(t // tm,),
        in_specs=[pl.BlockSpec((tm, d), lambda i: (i, 0)), _const_spec((1, d)), _const_spec((d, n_all))],
        out_specs=[pl.BlockSpec((tm, d), lambda i: (i, 0)), pl.BlockSpec((tm, n_qkv), lambda i: (i, 0)),
                   pl.BlockSpec((tm, F_PAD), lambda i: (i, 0)), pl.BlockSpec((tm, n_bcx), lambda i: (i, 0))],
        out_shape=[jax.ShapeDtypeStruct((t, d), BF16), jax.ShapeDtypeStruct((t, n_qkv), BF16),
                   jax.ShapeDtypeStruct((t, F_PAD), F32), jax.ShapeDtypeStruct((t, n_bcx), F32)],
        compiler_params=_cp(("parallel",)),
    )(x, g, win_p)


def _head_lanes(h):
    lane = lax.broadcasted_iota(jnp.int32, (1, PAIR), 1)
    hh = h % 2
    return lane, lane // HEAD_DIM == hh, HEAD_DIM * (1 - hh)


def _pieces(col):
    hi = col.astype(BF16).astype(F32)
    r1 = col - hi
    mid = r1.astype(BF16).astype(F32)
    lo = (r1 - mid).astype(BF16).astype(F32)
    return hi, mid, lo


def _put_pieces(lane, first, col, other):
    hi, mid, lo = _pieces(col)
    return jnp.where(lane == first, hi, jnp.where(lane == first + 1, mid, jnp.where(lane == first + 2, lo, other)))


def _fgate_prep(flog, b_f, qkv, *, tm=512):
    t = flog.shape[0]
    tm = min(tm, t)

    def body(f_ref, b_ref, qkv_ref, qat_ref, ka_ref, va_ref, vat_ref, sg_ref, carry):
        @pl.when(pl.program_id(0) == 0)
        def _():
            carry[...] = jnp.zeros_like(carry)
        z = f_ref[...] + b_ref[...]
        e = jnp.exp(-jnp.abs(z))
        logf = jnp.minimum(z, 0.0) - jnp.log(1.0 + e)
        sg_ref[...] = jnp.where(z >= 0, e, 1.0) / (1.0 + e)
        r = lax.broadcasted_iota(jnp.int32, (tm, tm), 0)
        c = lax.broadcasted_iota(jnp.int32, (tm, tm), 1)
        tri = (c <= r).astype(F32)
        cs = jnp.dot(tri, logf, preferred_element_type=F32, precision=lax.Precision.HIGHEST) + carry[...]
        carry[...] = cs[tm - 1:tm, :]
        cs2 = cs * LOG2E
        for h in range(N_HEADS):
            lane, head, aux = _head_lanes(h)
            p0 = (h // 2) * PAIR
            ones = ((lane >= aux + AUX_LSE) & (lane <= aux + AUX_ROWSUM)).astype(F32)
            bias = (lane >= aux + AUX_BIAS) & (lane < aux + AUX_BIAS + 3)
            k_aux = _put_pieces(lane, aux + AUX_BIAS, cs2[:, h:h + 1], ones)
            q_aug = jnp.where(head, qkv_ref[:, p0:p0 + PAIR].astype(F32), jnp.where(bias, -1.0, 0.0))
            v_aug = jnp.where(head, qkv_ref[:, 2 * ATTN_W + p0:2 * ATTN_W + p0 + PAIR].astype(F32),
                              jnp.where(bias, 1.0, 0.0))
            qat_ref[h] = q_aug.T.astype(BF16)
            ka_ref[h] = jnp.where(head, qkv_ref[:, ATTN_W + p0:ATTN_W + p0 + PAIR], k_aux.astype(BF16))
            va_ref[h] = v_aug.astype(BF16)
            vat_ref[h] = v_aug.T.astype(BF16)

    aug = lambda: pl.BlockSpec((N_HEADS, tm, PAIR), lambda i: (0, i, 0))
    aug_t = lambda: pl.BlockSpec((N_HEADS, PAIR, tm), lambda i: (0, 0, i))
    aug_shape = jax.ShapeDtypeStruct((N_HEADS, t, PAIR), BF16)
    aug_t_shape = jax.ShapeDtypeStruct((N_HEADS, PAIR, t), BF16)
    return pl.pallas_call(
        body, name="fgate_prep", grid=(t // tm,),
        in_specs=[pl.BlockSpec((tm, F_PAD), lambda i: (i, 0)), _const_spec((1, F_PAD)),
                  pl.BlockSpec((tm, 3 * ATTN_W), lambda i: (i, 0))],
        out_specs=[aug_t(), aug(), aug(), aug_t(), pl.BlockSpec((tm, F_PAD), lambda i: (i, 0))],
        out_shape=[aug_t_shape, aug_shape, aug_shape, aug_t_shape, jax.ShapeDtypeStruct((t, F_PAD), F32)],
        scratch_shapes=[pltpu.VMEM((1, F_PAD), F32)],
        compiler_params=_cp(("arbitrary",)),
    )(flog, b_f, qkv)


def _put_pieces_t(row, first, vec, other):
    hi, mid, lo = _pieces(vec)
    return jnp.where(row == first, hi, jnp.where(row == first + 1, mid, jnp.where(row == first + 2, lo, other)))


def _attn_fwd(q_aug_t, k_aug, v_aug_t, shards, *, tq=1024):
    t = k_aug.shape[1]
    tq = min(tq, t)
    tk = tq // 2
    nq = t // tq
    n_pairs = ATTN_W // PAIR
    n_sh = len(shards)

    def body(qt_ref, k_ref, vt_ref, *rest):
        o_ref, qb_ref, qbt_ref = rest[n_sh:n_sh + 3]
        s_scr = rest[2 * n_sh + 3]
        gather = _Exchange(rest[:n_sh], rest[n_sh + 3:2 * n_sh + 3], *rest[2 * n_sh + 4:], gather=True)
        i = pl.program_id(1)

        @pl.when((pl.program_id(0) == 0) & (i == 0))
        def _():
            gather.start()
        key = lax.broadcasted_iota(jnp.int32, (tk, tq), 0)
        qry = lax.broadcasted_iota(jnp.int32, (tk, tq), 1)
        qt = [qt_ref[0], qt_ref[1]]

        def logits(hh, tile, slot, diag):
            s = _dot(k_ref[hh, pl.ds(pl.multiple_of(tile * tk, tk), tk), :], qt[hh])
            if diag:
                s = jnp.where(key + (tile * tk - i * tq) <= qry, s, NEG)
            s_scr[hh, slot] = s
            return jnp.max(s, axis=0, keepdims=True)

        def probs(hh, tile, slot, m, acc, tmax):
            mn = jnp.maximum(m, tmax)
            p = jnp.exp2(s_scr[hh, slot] - mn).astype(BF16)
            acc = jnp.exp2(m - mn) * acc + _dot(vt_ref[hh, :, pl.ds(pl.multiple_of(tile * tk, tk), tk)], p)
            return mn, acc

        def advance(carry, prev, slot, nxt, diag=False):
            out = []
            for hh in range(2):
                m, acc, tmax = carry[hh]
                m, acc = probs(hh, prev, slot, m, acc, tmax)
                out.append((m, acc, logits(hh, nxt, 1 - slot, diag)))
            return tuple(out)

        def two_tiles(jj, carry):
            carry = advance(carry, jnp.where(jj == 0, 2 * i, 2 * jj - 1), 1, 2 * jj)
            return advance(carry, 2 * jj, 0, 2 * jj + 1)

        init = tuple((jnp.full((1, tq), NEG, F32), jnp.zeros((PAIR, tq), F32), logits(hh, 2 * i + 1, 0, True))
                     for hh in range(2))
        carry = advance(init, 2 * i + 1, 0, 2 * i, diag=True)
        carry = lax.fori_loop(0, i, two_tiles, carry)
        last = jnp.where(i == 0, 2 * i, 2 * i - 1)
        row = lax.broadcasted_iota(jnp.int32, (PAIR, 1), 0)
        res = []
        for hh in range(2):
            aux = HEAD_DIM * (1 - hh)
            m, acc, tmax = carry[hh]
            m, acc = probs(hh, last, 1, m, acc, tmax)
            l = acc[aux + AUX_BIAS:aux + AUX_BIAS + 1, :]
            qbt = _put_pieces_t(row, aux + AUX_LSE, -(m + jnp.log2(l)), qt[hh].astype(F32))
            qbt_ref[hh] = qbt.astype(BF16)
            qb_ref[hh] = qbt.T.astype(BF16)
            res.append(acc * (1.0 / l))
        o_ref[...] = jnp.where(row < HEAD_DIM, res[0], res[1]).T.astype(BF16)

        @pl.when((pl.program_id(0) == n_pairs - 1) & (i == nq - 1))
        def _():
            gather.wait()

    res = pl.pallas_call(
        body, name="attn_fwd", grid=(n_pairs, nq),
        in_specs=[pl.BlockSpec((2, PAIR, tq), lambda p, i: (p, 0, i)),
                  pl.BlockSpec((2, t, PAIR), lambda p, i: (p, 0, 0), pipeline_mode=pl.Buffered(1)),
                  pl.BlockSpec((2, PAIR, t), lambda p, i: (p, 0, 0), pipeline_mode=pl.Buffered(1))] + [HBM_SPEC] * n_sh,
        out_specs=[pl.BlockSpec((tq, PAIR), lambda p, i: (i, p)),
                   pl.BlockSpec((2, tq, PAIR), lambda p, i: (p, i, 0)),
                   pl.BlockSpec((2, PAIR, tq), lambda p, i: (p, 0, i))] + [HBM_SPEC] * n_sh,
        out_shape=[jax.ShapeDtypeStruct((t, ATTN_W), BF16), jax.ShapeDtypeStruct((N_HEADS, t, PAIR), BF16),
                   jax.ShapeDtypeStruct((N_HEADS, PAIR, t), BF16)]
        + [jax.ShapeDtypeStruct((N_DEV,) + s.shape, s.dtype) for s in shards],
        scratch_shapes=[pltpu.VMEM((2, 2, tk, tq), F32)] + _Exchange.scratch(n_sh),
        compiler_params=_cp(("arbitrary", "arbitrary")),
    )(q_aug_t, k_aug, v_aug_t, *shards)
    return res[0], res[1], res[2], res[3:]


def _prev_halo(tm, halo):
    return lambda i: (jnp.maximum(i * (tm // halo) - 1, 0), 0)


def _next_halo(tm, halo, t):
    return lambda i: (jnp.minimum((i + 1) * (tm // halo), t // halo - 1), 0)


def _conv_fwd(bcx, conv_w, *, tm=512):
    t = bcx.shape[0]
    tm = min(tm, t)
    ch = CONV_CH

    def body(b_ref, c_ref, x_ref, hc_ref, hx_ref, w_ref, cv_ref, ext):
        first = pl.program_id(0) == 0
        ext[0:CONV_HALO, :] = jnp.where(first, 0.0, hc_ref[...] * hx_ref[...])
        ext[CONV_HALO:CONV_HALO + tm, :] = c_ref[...] * x_ref[...]
        conv = (w_ref[0:1, :] * ext[CONV_HALO - 2:CONV_HALO - 2 + tm, :]
                + w_ref[1:2, :] * ext[CONV_HALO - 1:CONV_HALO - 1 + tm, :]
                + w_ref[2:3, :] * ext[CONV_HALO:CONV_HALO + tm, :])
        cv_ref[...] = (b_ref[...] * conv).astype(BF16)

    col = lambda k: pl.BlockSpec((tm, ch), lambda i: (i, k))
    halo = lambda k: pl.BlockSpec((CONV_HALO, ch), lambda i: (_prev_halo(tm, CONV_HALO)(i)[0], k))
    return pl.pallas_call(
        body, name="conv_fwd", grid=(t // tm,),
        in_specs=[col(0), col(1), col(2), halo(1), halo(2), _const_spec((8, ch))],
        out_specs=pl.BlockSpec((tm, ch), lambda i: (i, 0)),
        out_shape=jax.ShapeDtypeStruct((t, ch), BF16),
        scratch_shapes=[pltpu.VMEM((CONV_HALO + tm, ch), F32)],
        compiler_params=_cp(("parallel",)),
    )(bcx, bcx, bcx, bcx, bcx, conv_w)


def _outproj(att, cv, x, wout, *, tm=512):
    t, d = x.shape
    tm = min(tm, t)

    def body(a_ref, c_ref, x_ref, w_ref, h_ref):
        h_ref[...] = x_ref[...] + _dot(a_ref[...], w_ref[0:ATTN_W, :]) + _dot(c_ref[...], w_ref[ATTN_W:, :])

    return pl.pallas_call(
        body, name="outproj", grid=(t // tm,),
        in_specs=[pl.BlockSpec((tm, ATTN_W), lambda i: (i, 0)), pl.BlockSpec((tm, CONV_CH), lambda i: (i, 0)),
                  pl.BlockSpec((tm, d), lambda i: (i, 0)), _const_spec(wout.shape)],
        out_specs=pl.BlockSpec((tm, d), lambda i: (i, 0)),
        out_shape=jax.ShapeDtypeStruct((t, d), F32),
        compiler_params=_cp(("parallel",)),
    )(att, cv, x, wout)


def _mlp_tile(hh, g_ref, wu_ref, wd_ref, n_ref, a_ref, z_ref):
    n_blk, _, fb = wu_ref.shape
    n = (hh * _rstd(hh) * g_ref[...]).astype(BF16)
    n_ref[...] = n
    acc = hh
    for k in range(n_blk):
        a = _dot(n, wu_ref[k])
        zz = jnp.square(jnp.maximum(a, 0.0)).astype(BF16)
        a_ref[:, k * fb:(k + 1) * fb] = a.astype(BF16)
        z_ref[:, k * fb:(k + 1) * fb] = zz
        acc = acc + _dot(zz, wd_ref[k * fb:(k + 1) * fb, :])
    return acc


def _mlp_fwd(h, g, wup, wdown, *, name, tm=512):
    t, d = h.shape
    n_blk, _, fb = wup.shape
    f = n_blk * fb
    tm = min(tm, t)

    def body(h_ref, g_ref, wu_ref, wd_ref, ho_ref, n_ref, a_ref, z_ref):
        ho_ref[...] = _mlp_tile(h_ref[...], g_ref, wu_ref, wd_ref, n_ref, a_ref, z_ref)

    row = lambda n_: pl.BlockSpec((tm, n_), lambda i: (i, 0))
    return pl.pallas_call(
        body, name=name, grid=(t // tm,),
        in_specs=[row(d), _const_spec((1, d)), _const_spec(wup.shape), _const_spec(wdown.shape)],
        out_specs=[row(d), row(d), row(f), row(f)],
        out_shape=[jax.ShapeDtypeStruct((t, d), F32), jax.ShapeDtypeStruct((t, d), BF16),
                   jax.ShapeDtypeStruct((t, f), BF16), jax.ShapeDtypeStruct((t, f), BF16)],
        compiler_params=_cp(("parallel",)),
    )(h, g, wup, wdown)


def _mlp_fwd_loss(h, g, wup, wdown, g_out, target, *, name, tm=512):
    t, d = h.shape
    n_blk, _, fb = wup.shape
    f = n_blk * fb
    tm = min(tm, t)
    nsteps = t // tm

    def body(h_ref, g_ref, wu_ref, wd_ref, go_ref, y_ref, loss_ref, dh_ref, dg_ref, n_ref, a_ref, z_ref, lacc):
        i = pl.program_id(0)

        @pl.when(i == 0)
        def _():
            lacc[...] = jnp.zeros_like(lacc)
            dg_ref[...] = jnp.zeros_like(dg_ref)
        hv = _mlp_tile(h_ref[...], g_ref, wu_ref, wd_ref, n_ref, a_ref, z_ref)
        gv = go_ref[...]
        r = _rstd(hv)
        xhat = hv * r
        err = xhat * gv - y_ref[...]
        lacc[...] += _rows8(err * err)
        dout = err * (1.0 / d)
        dy = dout * gv
        dg_ref[...] += _rows8(dout * xhat)
        dh_ref[...] = r * (dy - xhat * jnp.mean(dy * xhat, axis=-1, keepdims=True))

        @pl.when(i == nsteps - 1)
        def _():
            loss_ref[...] = jnp.full(loss_ref.shape, (0.5 / d) * jnp.sum(lacc[...]), F32)

    row = lambda n_: pl.BlockSpec((tm, n_), lambda i: (i, 0))
    return pl.pallas_call(
        body, name=name, grid=(nsteps,),
        in_specs=[row(d), _const_spec((1, d)), _const_spec(wup.shape), _const_spec(wdown.shape), _const_spec((1, d)),
                  row(d)],
        out_specs=[pl.BlockSpec((8, 128), lambda i: (0, 0)), row(d), pl.BlockSpec((8, d), lambda i: (0, 0)),
                   row(d), row(f), row(f)],
        out_shape=[jax.ShapeDtypeStruct((8, 128), F32), jax.ShapeDtypeStruct((t, d), F32),
                   jax.ShapeDtypeStruct((8, d), F32), jax.ShapeDtypeStruct((t, d), BF16),
                   jax.ShapeDtypeStruct((t, f), BF16), jax.ShapeDtypeStruct((t, f), BF16)],
        scratch_shapes=[pltpu.VMEM((8, d), F32)],
        compiler_params=_cp(("arbitrary",)),
    )(h, g, wup, wdown, g_out, target)


def _pool_inv_count(i, tm):
    tglob = (i * tm + lax.broadcasted_iota(jnp.int32, (tm, 1), 0) + 1).astype(F32)
    return [1.0 / jnp.minimum(tglob, float(w)) for w in POOL_WINDOWS]


def _pool_fwd(h, g, poolw, scale, *, tm=256):
    t, d = h.shape
    tm = min(tm, t)
    cg = d // len(POOL_WINDOWS)

    def body(h_ref, hh_ref, g_ref, w_ref, s_ref, ho_ref, p_ref, ext):
        i = pl.program_id(0)
        hv = h_ref[...]
        halo = hh_ref[...]
        n = hv * _rstd(hv) * g_ref[...]
        ext[0:POOL_HALO, :] = jnp.where(i == 0, 0.0, halo * _rstd(halo) * g_ref[...])
        ext[POOL_HALO:POOL_HALO + tm, :] = n
        inv = _pool_inv_count(i, tm)
        for gi, w in enumerate(POOL_WINDOWS):
            cs = slice(gi * cg, (gi + 1) * cg)
            s = ext[POOL_HALO:POOL_HALO + tm, cs]
            for j in range(1, w):
                s = s + ext[POOL_HALO - j:POOL_HALO - j + tm, cs]
            pooled = (s * inv[gi] - n[:, cs]).astype(BF16)
            p_ref[:, cs] = pooled
            ho_ref[:, cs] = hv[:, cs] + _dot(pooled, w_ref[gi]) * s_ref[:, cs]

    row = lambda: pl.BlockSpec((tm, d), lambda i: (i, 0))
    return pl.pallas_call(
        body, name="pool_fwd", grid=(t // tm,),
        in_specs=[row(), pl.BlockSpec((POOL_HALO, d), _prev_halo(tm, POOL_HALO)), _const_spec((1, d)),
                  _const_spec(poolw.shape), _const_spec((1, d))],
        out_specs=[row(), row()],
        out_shape=[jax.ShapeDtypeStruct((t, d), F32), jax.ShapeDtypeStruct((t, d), BF16)],
        scratch_shapes=[pltpu.VMEM((POOL_HALO + tm, d), F32)],
        compiler_params=_cp(("parallel",)),
    )(h, h, g, poolw, scale)


def _mm_tn(a, b, *, name, ta, tb, tt, blocked_out=False, out_dtype=F32):
    t, ka = a.shape
    n = b.shape[1]
    ta, tb, tt = min(ta, ka), min(tb, n), min(tt, t)
    nt = t // tt

    def body(a_ref, b_ref, o_ref, acc):
        @pl.when(pl.program_id(2) == 0)
        def _():
            acc[...] = jnp.zeros_like(acc)
        acc[...] += _dot_tn(a_ref[...].astype(BF16), b_ref[...].astype(BF16))

        @pl.when(pl.program_id(2) == nt - 1)
        def _():
            o_ref[...] = acc[...].astype(out_dtype)

    if blocked_out:
        assert ta == ka
        out_shape = jax.ShapeDtypeStruct((n // tb, ka, tb), out_dtype)
        out_spec = pl.BlockSpec((None, ta, tb), lambda i, j, k: (j, i, 0))
    else:
        out_shape = jax.ShapeDtypeStruct((ka, n), out_dtype)
        out_spec = pl.BlockSpec((ta, tb), lambda i, j, k: (i, j))
    return pl.pallas_call(
        body, name=name, grid=(ka // ta, n // tb, nt),
        in_specs=[pl.BlockSpec((tt, ta), lambda i, j, k: (k, i)), pl.BlockSpec((tt, tb), lambda i, j, k: (k, j))],
        out_specs=out_spec, out_shape=out_shape, scratch_shapes=[pltpu.VMEM((ta, tb), F32)],
        compiler_params=_cp(("parallel", "parallel", "arbitrary")),
    )(a, b)


def _mlp_bwd(dho, h, a, g, wup, wdown, *, name, tm=512):
    t, d = h.shape
    n_blk, _, fb = wup.shape
    f = n_blk * fb
    tm = min(tm, t)

    def body(do_ref, h_ref, a_ref, g_ref, wu_ref, wd_ref, dh_ref, da_ref, dg_ref):
        @pl.when(pl.program_id(0) == 0)
        def _():
            dg_ref[...] = jnp.zeros_like(dg_ref)
        dho_v = do_ref[...]
        dob = dho_v.astype(BF16)
        dn = jnp.zeros((tm, d), F32)
        for k in range(n_blk):
            dz = _dot_nt(dob, wd_ref[k * fb:(k + 1) * fb, :])
            da = (dz * (2.0 * jnp.maximum(a_ref[:, k * fb:(k + 1) * fb].astype(F32), 0.0))).astype(BF16)
            da_ref[:, k * fb:(k + 1) * fb] = da
            dn = dn + _dot_nt(da, wu_ref[k])
        dh, dg = _norm_bwd(dn, h_ref[...], g_ref[...])
        dh_ref[...] = dho_v + dh
        dg_ref[...] += dg

    row = lambda n_: pl.BlockSpec((tm, n_), lambda i: (i, 0))
    return pl.pallas_call(
        body, name=name, grid=(t // tm,),
        in_specs=[row(d), row(d), row(f), _const_spec((1, d)), _const_spec(wup.shape), _const_spec(wdown.shape)],
        out_specs=[row(d), row(f), pl.BlockSpec((8, d), lambda i: (0, 0))],
        out_shape=[jax.ShapeDtypeStruct((t, d), F32), jax.ShapeDtypeStruct((t, f), BF16),
                   jax.ShapeDtypeStruct((8, d), F32)],
        compiler_params=_cp(("arbitrary",)),
    )(dho, h, a, g, wup, wdown)


def _pool_bwd(dho, h, pooled, g, poolw, scale, *, tm=256):
    t, d = h.shape
    tm = min(tm, t)
    ng = len(POOL_WINDOWS)
    cg = d // ng
    nsteps = t // tm

    def body(do_ref, dn_ref, h_ref, p_ref, g_ref, w_ref, s_ref, dh_ref, dw_ref, ds_ref, dg_ref, ext):
        i = pl.program_id(0)

        @pl.when(i == 0)
        def _():
            dw_ref[...] = jnp.zeros_like(dw_ref)
            ds_ref[...] = jnp.zeros_like(ds_ref)
            dg_ref[...] = jnp.zeros_like(dg_ref)
        dho_v = do_ref[...]
        sv = s_ref[...]
        dyp = (dho_v * sv).astype(BF16)
        dyp_halo = (dn_ref[...] * sv).astype(BF16)
        inv = _pool_inv_count(i, tm)
        tnext = ((i + 1) * tm + lax.broadcasted_iota(jnp.int32, (POOL_HALO, 1), 0) + 1).astype(F32)
        last = i == nsteps - 1
        ypre_parts, dpooled_parts = [], []
        for gi, w in enumerate(POOL_WINDOWS):
            cs = slice(gi * cg, (gi + 1) * cg)
            pg = p_ref[:, cs]
            ypre_parts.append(_dot(pg, w_ref[gi]))
            dw_ref[gi] += _dot_tn(pg, dyp[:, cs])
            dpool = _dot_nt(dyp[:, cs], w_ref[gi])
            dpooled_parts.append(dpool)
            ext[0:tm, cs] = dpool * inv[gi]
            dpool_halo = _dot_nt(dyp_halo[:, cs], w_ref[gi]) * (1.0 / jnp.minimum(tnext, float(w)))
            ext[tm:tm + POOL_HALO, cs] = jnp.where(last, 0.0, dpool_halo)
        ds_ref[...] += _rows8(dho_v * jnp.concatenate(ypre_parts, axis=1))
        dn_parts = []
        for gi, w in enumerate(POOL_WINDOWS):
            cs = slice(gi * cg, (gi + 1) * cg)
            s = ext[0:tm, cs]
            for j in range(1, w):
                s = s + ext[j:j + tm, cs]
            dn_parts.append(s - dpooled_parts[gi])
        dh, dg = _norm_bwd(jnp.concatenate(dn_parts, axis=1), h_ref[...], g_ref[...])
        dh_ref[...] = dho_v + dh
        dg_ref[...] += dg

    row = lambda: pl.BlockSpec((tm, d), lambda i: (i, 0))
    acc8 = lambda: pl.BlockSpec((8, d), lambda i: (0, 0))
    return pl.pallas_call(
        body, name="pool_bwd", grid=(nsteps,),
        in_specs=[row(), pl.BlockSpec((POOL_HALO, d), _next_halo(tm, POOL_HALO, t)), row(), row(),
                  _const_spec((1, d)), _const_spec(poolw.shape), _const_spec((1, d))],
        out_specs=[row(), pl.BlockSpec((ng, cg, cg), lambda i: (0, 0, 0)), acc8(), acc8()],
        out_shape=[jax.ShapeDtypeStruct((t, d), F32), jax.ShapeDtypeStruct((ng, cg, cg), F32),
                   jax.ShapeDtypeStruct((8, d), F32), jax.ShapeDtypeStruct((8, d), F32)],
        scratch_shapes=[pltpu.VMEM((tm + POOL_HALO, d), F32)],
        compiler_params=_cp(("arbitrary",)),
    )(dho, dho, h, pooled, g, poolw, scale)


def _outproj_bwd(dh, o, wout, *, tm=512):
    t, d = dh.shape
    tm = min(tm, t)

    def body(dh_ref, o_ref, w_ref, da_ref, dat_ref, dc_ref):
        dhb = dh_ref[...].astype(BF16)
        dc_ref[...] = _dot_nt(dhb, w_ref[ATTN_W:, :])
        for p in range(ATTN_W // PAIR):
            datt = _dot_nt(dhb, w_ref[p * PAIR:(p + 1) * PAIR, :])
            prod = datt * o_ref[:, p * PAIR:(p + 1) * PAIR].astype(F32)
            for hh in range(2):
                lane, head, aux = _head_lanes(hh)
                delta = jnp.sum(jnp.where(head, prod, 0.0), axis=1, keepdims=True)
                aug = _put_pieces(lane, aux + AUX_BIAS, -delta, jnp.where(head, datt, 0.0))
                da_ref[2 * p + hh] = aug.astype(BF16)
                dat_ref[2 * p + hh] = aug.T.astype(BF16)

    row = lambda n_: pl.BlockSpec((tm, n_), lambda i: (i, 0))
    return pl.pallas_call(
        body, name="outproj_bwd", grid=(t // tm,),
        in_specs=[row(d), row(ATTN_W), _const_spec(wout.shape)],
        out_specs=[pl.BlockSpec((N_HEADS, tm, PAIR), lambda i: (0, i, 0)),
                   pl.BlockSpec((N_HEADS, PAIR, tm), lambda i: (0, 0, i)), row(CONV_CH)],
        out_shape=[jax.ShapeDtypeStruct((N_HEADS, t, PAIR), BF16), jax.ShapeDtypeStruct((N_HEADS, PAIR, t), BF16),
                   jax.ShapeDtypeStruct((t, CONV_CH), F32)],
        compiler_params=_cp(("parallel",)),
    )(dh, o, wout)


def _conv_bwd(bcx, dcv, conv_w, *, tm=512):
    t = bcx.shape[0]
    tm = min(tm, t)
    ch = CONV_CH
    nsteps = t // tm

    def body(b_ref, c_ref, x_ref, hc_ref, hx_ref, d_ref, nb_ref, nd_ref, w_ref, o_ref, dw_ref, ext_u, ext_d):
        i = pl.program_id(0)

        @pl.when(i == 0)
        def _():
            dw_ref[...] = jnp.zeros_like(dw_ref)
        b, c, x, dcv_v = b_ref[...], c_ref[...], x_ref[...], d_ref[...]
        ext_u[0:CONV_HALO, :] = jnp.where(i == 0, 0.0, hc_ref[...] * hx_ref[...])
        ext_u[CONV_HALO:CONV_HALO + tm, :] = c * x
        dconv = dcv_v * b
        ext_d[0:tm, :] = dconv
        ext_d[tm:tm + CONV_HALO, :] = jnp.where(i == nsteps - 1, 0.0, nd_ref[...] * nb_ref[...])
        u = [ext_u[CONV_HALO - 2 + k:CONV_HALO - 2 + k + tm, :] for k in range(3)]
        conv = w_ref[0:1, :] * u[0] + w_ref[1:2, :] * u[1] + w_ref[2:3, :] * u[2]
        du = (w_ref[2:3, :] * dconv + w_ref[1:2, :] * ext_d[1:1 + tm, :] + w_ref[0:1, :] * ext_d[2:2 + tm, :])
        o_ref[:, 0:ch] = (dcv_v * conv).astype(BF16)
        o_ref[:, ch:2 * ch] = (du * x).astype(BF16)
        o_ref[:, 2 * ch:3 * ch] = (du * c).astype(BF16)
        for k in range(3):
            dw_ref[k] += _rows8(dconv * u[k])

    col = lambda k: pl.BlockSpec((tm, ch), lambda i: (i, k))
    prev = lambda k: pl.BlockSpec((CONV_HALO, ch), lambda i: (_prev_halo(tm, CONV_HALO)(i)[0], k))
    nxt = lambda k: pl.BlockSpec((CONV_HALO, ch), lambda i: (_next_halo(tm, CONV_HALO, t)(i)[0], k))
    return pl.pallas_call(
        body, name="conv_bwd", grid=(nsteps,),
        in_specs=[col(0), col(1), col(2), prev(1), prev(2), col(0), nxt(0), nxt(0), _const_spec((8, ch))],
        out_specs=[pl.BlockSpec((tm, 3 * ch), lambda i: (i, 0)), pl.BlockSpec((3, 8, ch), lambda i: (0, 0, 0))],
        out_shape=[jax.ShapeDtypeStruct((t, 3 * ch), BF16), jax.ShapeDtypeStruct((3, 8, ch), F32)],
        scratch_shapes=[pltpu.VMEM((CONV_HALO + tm, ch), F32), pltpu.VMEM((tm + CONV_HALO, ch), F32)],
        compiler_params=_cp(("arbitrary",)),
    )(bcx, bcx, bcx, bcx, bcx, dcv, bcx, dcv, conv_w)


def _attn_bwd(q_bwd, do_aug, q_bwd_t, do_aug_t, k_aug, v_aug, gblocks, *, tq=1024):
    t = q_bwd.shape[1]
    tq = min(tq, t)
    tk = tq // 2
    nq, nk = t // tq, t // tk
    n_pairs = ATTN_W // PAIR
    n_g = len(gblocks)

    def body(q_ref, do_ref, qt_ref, dot_ref, k_ref, v_ref, *rest):
        dq_ref, dqx_ref, dk_ref, dkx_ref, dv_ref = rest[n_g:n_g + 5]
        dq_scr = rest[2 * n_g + 5]
        scatter = _Exchange(rest[:n_g], rest[n_g + 5:2 * n_g + 5], *rest[2 * n_g + 6:], gather=False)
        j = pl.program_id(1)

        @pl.when((pl.program_id(0) == 0) & (j == 0))
        def _():
            scatter.start()

        @pl.when(j == 0)
        def _():
            dq_scr[...] = jnp.zeros_like(dq_scr)
        k = [k_ref[0], k_ref[1]]
        v = [v_ref[0], v_ref[1]]

        def step(i, carry, diag, rows=tq, row0=0):
            qs = pl.multiple_of(i * tq + row0, tk)
            if diag:
                row = lax.broadcasted_iota(jnp.int32, (rows, tk), 0)
                col = lax.broadcasted_iota(jnp.int32, (rows, tk), 1)
            out = []
            for hh in range(2):
                dk_a, dv_a = carry[hh]
                q = q_ref[hh, pl.ds(qs, rows), :]
                dov = do_ref[hh, pl.ds(qs, rows), :]
                p = jnp.exp2(_dot_nt(q, k[hh]))
                if diag:
                    p = jnp.where(col + (j * tk - i * tq - row0) <= row, p, 0.0)
                ds = (p * _dot_nt(dov, v[hh])).astype(BF16)
                dv_a = dv_a + _dot(dot_ref[hh, :, pl.ds(qs, rows)], p.astype(BF16))
                dk_a = dk_a + _dot(qt_ref[hh, :, pl.ds(qs, rows)], ds)
                dq_scr[hh, pl.ds(qs, rows), :] += _dot(ds, k[hh])
                out.append((dk_a, dv_a))
            return tuple(out)

        zero = (jnp.zeros((PAIR, tk), F32), jnp.zeros((PAIR, tk), F32))
        carry = lax.cond(j % 2 == 0, lambda c: step(j // 2, c, True),
                         lambda c: step(j // 2, c, True, rows=tk, row0=tk), (zero, zero))
        (dk0, dv0), (dk1, dv1) = lax.fori_loop(j // 2 + 1, nq, functools.partial(step, diag=False), carry)
        first_t = lax.broadcasted_iota(jnp.int32, (PAIR, 1), 0) < HEAD_DIM
        first = lax.broadcasted_iota(jnp.int32, (1, PAIR), 1) < HEAD_DIM
        dk_ref[...] = (jnp.where(first_t, dk0, dk1).T * (1.0 / LOG2E)).astype(BF16)
        dkx_ref[...] = jnp.where(first_t, dk1, dk0).T
        dv_ref[...] = jnp.where(first_t, dv0, dv1).T.astype(BF16)

        @pl.when(j == nk - 1)
        def _():
            dq_ref[...] = (jnp.where(first, dq_scr[0], dq_scr[1]) * Q_SCALE).astype(BF16)
            dqx_ref[...] = jnp.where(first, dq_scr[1], dq_scr[0])

        @pl.when((pl.program_id(0) == n_pairs - 1) & (j == nk - 1))
        def _():
            scatter.wait()

    resident = lambda: pl.BlockSpec((2, t, PAIR), lambda p, j: (p, 0, 0), pipeline_mode=pl.Buffered(1))
    resident_t = lambda: pl.BlockSpec((2, PAIR, t), lambda p, j: (p, 0, 0), pipeline_mode=pl.Buffered(1))
    kv_in = lambda: pl.BlockSpec((2, tk, PAIR), lambda p, j: (p, j, 0))
    whole = lambda: pl.BlockSpec((t, PAIR), lambda p, j: (0, p))
    tile = lambda: pl.BlockSpec((tk, PAIR), lambda p, j: (j, p))
    b16 = jax.ShapeDtypeStruct((t, ATTN_W), BF16)
    f32 = jax.ShapeDtypeStruct((t, ATTN_W), F32)
    res = pl.pallas_call(
        body, name="attn_bwd", grid=(n_pairs, nk),
        in_specs=[resident(), resident(), resident_t(), resident_t(), kv_in(), kv_in()] + [HBM_SPEC] * n_g,
        out_specs=[whole(), whole(), tile(), tile(), tile()] + [HBM_SPEC] * n_g,
        out_shape=[b16, f32, b16, f32, b16] + [jax.ShapeDtypeStruct(g.shape, g.dtype) for g in gblocks],
        scratch_shapes=[pltpu.VMEM((2, t, PAIR), F32)] + _Exchange.scratch(n_g),
        compiler_params=_cp(("arbitrary", "arbitrary")),
    )(q_bwd, do_aug, q_bwd_t, do_aug_t, k_aug, v_aug, *gblocks)
    return res[:5], res[5:]


def _fgate_bwd(dqx, dkx, sgate, *, tm=256):
    t = sgate.shape[0]
    tm = min(tm, t)
    nsteps = t // tm

    def body(dq_ref, dk_ref, sg_ref, df_ref, dbf_ref, carry):
        @pl.when(pl.program_id(0) == 0)
        def _():
            carry[...] = jnp.zeros_like(carry)
            dbf_ref[...] = jnp.zeros_like(dbf_ref)
        lane = lax.broadcasted_iota(jnp.int32, (ATTN_W, F_PAD), 0)
        head = lax.broadcasted_iota(jnp.int32, (ATTN_W, F_PAD), 1)
        aux = (head // 2) * PAIR + HEAD_DIM * (1 - head % 2)
        valid = head < N_HEADS
        pick_r = (valid & (lane == aux + AUX_ROWSUM)).astype(F32)
        pick_c = (valid & (lane == aux + AUX_BIAS)).astype(F32)
        hp = lax.Precision.HIGHEST
        dcum = (jnp.dot(dq_ref[...], pick_r, preferred_element_type=F32, precision=hp)
                + jnp.dot(dk_ref[...], pick_c, preferred_element_type=F32, precision=hp))
        r = lax.broadcasted_iota(jnp.int32, (tm, tm), 0)
        c = lax.broadcasted_iota(jnp.int32, (tm, tm), 1)
        tri = (c >= r).astype(F32)
        rc = jnp.dot(tri, dcum, preferred_element_type=F32, precision=hp) + carry[...]
        carry[...] = rc[0:1, :]
        df = rc * sg_ref[...]
        df_ref[...] = df.astype(BF16)
        dbf_ref[...] += _rows8(df)

    rev = lambda i: nsteps - 1 - i
    return pl.pallas_call(
        body, name="fgate_bwd", grid=(nsteps,),
        in_specs=[pl.BlockSpec((tm, ATTN_W), lambda i: (rev(i), 0)), pl.BlockSpec((tm, ATTN_W), lambda i: (rev(i), 0)),
                  pl.BlockSpec((tm, F_PAD), lambda i: (rev(i), 0))],
        out_specs=[pl.BlockSpec((tm, F_PAD), lambda i: (rev(i), 0)), pl.BlockSpec((8, F_PAD), lambda i: (0, 0))],
        out_shape=[jax.ShapeDtypeStruct((t, F_PAD), BF16), jax.ShapeDtypeStruct((8, F_PAD), F32)],
        scratch_shapes=[pltpu.VMEM((1, F_PAD), F32)],
        compiler_params=_cp(("arbitrary",)),
    )(dqx, dkx, sgate)


def _inproj_bwd(dq, dk, dv, df, dbcx, dh, x, g, win_p, gblock, *, tm=512):
    t, d = x.shape
    tm = min(tm, t)
    nsteps = t // tm
    n_qkv = 3 * ATTN_W

    def body(dq_ref, dk_ref, dv_ref, df_ref, db_ref, dh_ref, x_ref, g_ref, w_ref, gb_ref, gx_ref, dg_ref, land_ref,
             *sems):
        scatter = _Exchange([gb_ref], [land_ref], *sems, gather=False)

        @pl.when(pl.program_id(0) == 0)
        def _():
            scatter.start()
            dg_ref[...] = jnp.zeros_like(dg_ref)
        dn = _dot_nt(df_ref[...], w_ref[:, n_qkv:n_qkv + F_PAD])
        for k, r in enumerate((dq_ref, dk_ref, dv_ref)):
            dn = dn + _dot_nt(r[...], w_ref[:, k * ATTN_W:(k + 1) * ATTN_W])
        for k in range(3):
            c0 = n_qkv + F_PAD + k * CONV_CH
            dn = dn + _dot_nt(db_ref[:, k * CONV_CH:(k + 1) * CONV_CH], w_ref[:, c0:c0 + CONV_CH])
        dx, dg = _norm_bwd(dn, x_ref[...], g_ref[...])
        gx_ref[...] = dh_ref[...] + dx
        dg_ref[...] += dg

        @pl.when(pl.program_id(0) == nsteps - 1)
        def _():
            scatter.wait()

    row = lambda n_: pl.BlockSpec((tm, n_), lambda i: (i, 0))
    return pl.pallas_call(
        body, name="inproj_bwd", grid=(nsteps,),
        in_specs=[row(ATTN_W), row(ATTN_W), row(ATTN_W), row(F_PAD), row(3 * CONV_CH), row(d), row(d),
                  _const_spec((1, d)), _const_spec(win_p.shape), HBM_SPEC],
        out_specs=[row(d), pl.BlockSpec((8, d), lambda i: (0, 0)), HBM_SPEC],
        out_shape=[jax.ShapeDtypeStruct((t, d), F32), jax.ShapeDtypeStruct((8, d), F32),
                   jax.ShapeDtypeStruct(gblock.shape, gblock.dtype)],
        scratch_shapes=_Exchange.scratch(1),
        compiler_params=_cp(("arbitrary",)),
    )(dq, dk, dv, df, dbcx, dh, x, g, win_p, gblock)


LATE = ("w_out_0", "w_up_0", "w_down_0", "pool_w_1", "w_up_1", "w_down_1")


def _local_step(x, target, gains, b_f, conv_w, pool_scale, win_p, shards):
    d = x.shape[1]
    n0, qkv, flog, bcx = _norm_inproj(x, gains["mix0"], win_p)
    q_aug_t, k_aug, v_aug, v_aug_t, sgate = _fgate_prep(flog, b_f, qkv)
    att, q_bwd, q_bwd_t, gathered = _attn_fwd(q_aug_t, k_aug, v_aug_t, [shards[n] for n in LATE])
    g = dict(zip(LATE, gathered))
    wout = g["w_out_0"].reshape(d, d)
    wup0, wup1 = g["w_up_0"], g["w_up_1"]
    wdown0, wdown1 = g["w_down_0"].reshape(-1, d), g["w_down_1"].reshape(-1, d)
    n_grp = len(POOL_WINDOWS)
    cg = d // n_grp
    poolw = g["pool_w_1"].reshape(N_DEV, n_grp, cg // N_DEV, cg).transpose(1, 0, 2, 3).reshape(n_grp, cg, cg)
    cv = _conv_fwd(bcx, conv_w)
    h1 = _outproj(att, cv, x, wout)
    h2, n1, a0, z0 = _mlp_fwd(h1, gains["ffn0"], wup0, wdown0, name="mlp_fwd0")
    h3, pooled = _pool_fwd(h2, gains["mix1"], poolw, pool_scale)
    loss, dh4, dg_final, n3, a1, z1 = _mlp_fwd_loss(h3, gains["ffn1"], wup1, wdown1, gains["final"], target,
                                                    name="mlp_fwd1")
    f = a1.shape[1]
    fb = f // N_DEV
    dh3, da1, dg_ffn1 = _mlp_bwd(dh4, h3, a1, gains["ffn1"], wup1, wdown1, name="mlp_bwd1")
    dwdown1 = _mm_tn(z1, dh4, name="dwdown1", ta=1024, tb=1024, tt=2048, out_dtype=BF16)
    dwup1 = _mm_tn(n3, da1, name="dwup1", ta=d, tb=fb, tt=4096, blocked_out=True, out_dtype=BF16)
    dh2, dpoolw, dscale, dg_mix1 = _pool_bwd(dh3, h2, pooled, gains["mix1"], poolw, pool_scale)
    dh1, da0, dg_ffn0 = _mlp_bwd(dh2, h1, a0, gains["ffn0"], wup0, wdown0, name="mlp_bwd0")
    dwdown0 = _mm_tn(z0, dh2, name="dwdown0", ta=1024, tb=1024, tt=2048, out_dtype=BF16)
    dwup0 = _mm_tn(n1, da0, name="dwup0", ta=d, tb=fb, tt=4096, blocked_out=True, out_dtype=BF16)
    do_aug, do_aug_t, dcv = _outproj_bwd(dh1, att, wout)
    dwout = jnp.concatenate([_mm_tn(att, dh1, name="dwout_att", ta=512, tb=1024, tt=2048, out_dtype=BF16),
                             _mm_tn(cv, dh1, name="dwout_conv", ta=512, tb=1024, tt=2048, out_dtype=BF16)], axis=0)
    dbcx, dconvw = _conv_bwd(bcx, dcv, conv_w)
    gblocks = {
        "w_out_0": dwout.reshape(N_DEV, d // N_DEV, d), "w_up_0": dwup0, "w_up_1": dwup1,
        "w_down_0": dwdown0.reshape(N_DEV, -1, d), "w_down_1": dwdown1.reshape(N_DEV, -1, d),
        "pool_w_1": dpoolw.astype(BF16).reshape(n_grp, N_DEV, cg // N_DEV, cg).transpose(1, 0, 2, 3).reshape(
            N_DEV, n_grp * (cg // N_DEV), cg),
    }
    (dq, dqx, dk, dkx, dv), landed = _attn_bwd(q_bwd, do_aug, q_bwd_t, do_aug_t, k_aug, v_aug,
                                               [gblocks[n] for n in LATE])
    df, dbf = _fgate_bwd(dqx, dkx, sgate)
    dwin = jnp.concatenate(
        [_mm_tn(n0, dq, name="dwin_q", ta=d, tb=512, tt=4096, out_dtype=BF16),
         _mm_tn(n0, dk, name="dwin_k", ta=d, tb=512, tt=4096, out_dtype=BF16),
         _mm_tn(n0, dv, name="dwin_v", ta=d, tb=512, tt=4096, out_dtype=BF16),
         _mm_tn(n0, df, name="dwin_f", ta=d, tb=128, tt=2048, out_dtype=BF16)[:, :N_HEADS],
         _mm_tn(n0, dbcx, name="dwin_bcx", ta=d, tb=512, tt=4096, out_dtype=BF16)], axis=1)
    dwin_blocks = dwin.reshape(d, N_DEV, dwin.shape[1] // N_DEV).transpose(1, 0, 2)
    grad_x, dg_mix0, landed_win = _inproj_bwd(dq, dk, dv, df, dbcx, dh1, x, gains["mix0"], win_p, dwin_blocks)
    small = dict(mix0=dg_mix0, ffn0=dg_ffn0, mix1=dg_mix1, pool_scale=dscale, ffn1=dg_ffn1, final=dg_final,
                 b_f=dbf, conv_w=dconvw)
    return loss, grad_x, dict(zip(LATE + ("w_in_0",), tuple(landed) + (landed_win,))), small


def _mesh_places():
    x, y, c = lax.axis_index("x"), lax.axis_index("y"), lax.axis_index("c")
    chips = [(1 - x, y), (x, 1 - y), (1 - x, 1 - y)]
    return (x, y, c), (x, y, 1 - c), chips


def _all_gather(shards):
    n = len(shards)

    def body(*refs):
        ins, outs = refs[:n], refs[n:2 * n]
        send_sems, recv_sems, local_sems = refs[2 * n:]
        me, sib, chips = _mesh_places()
        c = me[2]

        def copy(ai, k, block, to, src=None):
            dst = outs[ai].at[_slot(*block)]
            return pltpu.make_async_remote_copy(
                src_ref=dst if src is None else src, dst_ref=dst, send_sem=send_sems.at[7 * ai + k],
                recv_sem=recv_sems.at[7 * ai + k], device_id=to, device_id_type=MESH)

        mine = [pltpu.make_async_copy(ins[ai], outs[ai].at[_slot(*me)], local_sems.at[ai]) for ai in range(n)]
        for cp in mine:
            cp.start()
        first = []
        for ai in range(n):
            first.append(copy(ai, 0, me, sib, src=ins[ai]))
            first += [copy(ai, 1 + j, me, (*chip, c), src=ins[ai]) for j, chip in enumerate(chips)]
        for cp in first:
            cp.start()
        passed = []
        for ai in range(n):
            for j, chip in enumerate(chips):
                copy(ai, 1 + j, (*chip, c), me).wait_recv()
                cp = copy(ai, 4 + j, (*chip, c), sib)
                cp.start()
                passed.append(cp)
        for ai in range(n):
            copy(ai, 0, sib, me).wait_recv()
            for j, chip in enumerate(chips):
                copy(ai, 4 + j, (*chip, 1 - c), me).wait_recv()
        for cp in first + passed:
            cp.wait_send()
        for cp in mine:
            cp.wait()

    return pl.pallas_call(
        body, name="all_gather",
        in_specs=[HBM_SPEC] * n, out_specs=[HBM_SPEC] * n,
        out_shape=[jax.ShapeDtypeStruct((N_DEV,) + s.shape, s.dtype) for s in shards],
        scratch_shapes=[pltpu.SemaphoreType.DMA((7 * n,)), pltpu.SemaphoreType.DMA((7 * n,)),
                        pltpu.SemaphoreType.DMA((n,))],
    )(*shards)


SMALL_ROWS = 16


def _small_allreduce(parts):
    n, _, w = parts.shape
    assert n <= SMALL_ROWS

    def body(p_ref, o_ref, gath, send_sems, recv_sems):
        x, y, c = lax.axis_index("x"), lax.axis_index("y"), lax.axis_index("c")
        my = _slot(x, y, c)
        rows = [jnp.sum(p_ref[i], axis=0, keepdims=True) for i in range(n)]
        rows.append(jnp.zeros((SMALL_ROWS - n, w), F32))
        gath[my] = jnp.concatenate(rows, axis=0)
        copies = []
        for k in range(1, N_DEV):
            px, py, pc = x ^ (k >> 2), y ^ ((k >> 1) & 1), c ^ (k & 1)
            cp = pltpu.make_async_remote_copy(
                src_ref=gath.at[my], dst_ref=gath.at[my], send_sem=send_sems.at[k - 1], recv_sem=recv_sems.at[k - 1],
                device_id=(px, py, pc), device_id_type=MESH)
            cp.start()
            copies.append(cp)
        for cp in copies:
            cp.wait()
        acc = gath[0]
        for d in range(1, N_DEV):
            acc = acc + gath[d]
        o_ref[...] = acc

    return pl.pallas_call(
        body, name="small_allreduce",
        in_specs=[VMEM_SPEC], out_specs=VMEM_SPEC,
        out_shape=jax.ShapeDtypeStruct((SMALL_ROWS, w), F32),
        scratch_shapes=[pltpu.VMEM((N_DEV, SMALL_ROWS, w), F32), pltpu.SemaphoreType.DMA((N_DEV - 1,)),
                        pltpu.SemaphoreType.DMA((N_DEV - 1,))],
    )(parts)


def _adamw(g, w, m, v, *, name, tm=256):
    r, c = g.shape
    tm = tm if r % tm == 0 else r
    bc1 = 1.0 - ADAM_B1 ** ADAM_STEP
    bc2 = 1.0 - ADAM_B2 ** ADAM_STEP

    def body(g_ref, w_ref, m_ref, v_ref, d_ref, nm_ref, nv_ref):
        gv = g_ref[...]
        nm = ADAM_B1 * m_ref[...] + (1.0 - ADAM_B1) * gv
        nv = ADAM_B2 * v_ref[...] + (1.0 - ADAM_B2) * jnp.square(gv)
        nm_ref[...] = nm
        nv_ref[...] = nv
        d_ref[...] = -ADAM_LR * ((nm / bc1) / (jnp.sqrt(nv / bc2) + ADAM_EPS) + ADAM_WD * w_ref[...])

    blk = pl.BlockSpec((tm, c), lambda i: (i, 0))
    shp = jax.ShapeDtypeStruct((r, c), F32)
    return pl.pallas_call(
        body, name=name, grid=(r // tm,), in_specs=[blk] * 4, out_specs=[blk] * 3, out_shape=[shp] * 3,
        compiler_params=_cp(("parallel",)),
    )(g, w, m, v)


def _adamw_sum(parts, w, m, v, *, name, tm=128):
    _, r, c = parts.shape
    tm = tm if r % tm == 0 else r
    bc1 = 1.0 - ADAM_B1 ** ADAM_STEP
    bc2 = 1.0 - ADAM_B2 ** ADAM_STEP

    def body(p_ref, w_ref, m_ref, v_ref, g_ref, d_ref, nm_ref, nv_ref):
        gv = p_ref[0].astype(F32)
        for k in range(1, N_DEV):
            gv = gv + p_ref[k].astype(F32)
        g_ref[...] = gv
        nm = ADAM_B1 * m_ref[...] + (1.0 - ADAM_B1) * gv
        nv = ADAM_B2 * v_ref[...] + (1.0 - ADAM_B2) * jnp.square(gv)
        nm_ref[...] = nm
        nv_ref[...] = nv
        d_ref[...] = -ADAM_LR * ((nm / bc1) / (jnp.sqrt(nv / bc2) + ADAM_EPS) + ADAM_WD * w_ref[...])

    blk = pl.BlockSpec((tm, c), lambda i: (i, 0))
    shp = jax.ShapeDtypeStruct((r, c), F32)
    return pl.pallas_call(
        body, name=name, grid=(r // tm,), in_specs=[pl.BlockSpec((N_DEV, tm, c), lambda i: (0, i, 0))] + [blk] * 3,
        out_specs=[blk] * 4, out_shape=[shp] * 4, compiler_params=_cp(("parallel",)),
    )(parts, w, m, v)


BIG = ("w_in_0", "w_out_0", "w_up_0", "w_down_0", "pool_w_1", "w_up_1", "w_down_1")
SMALL = ("norm_mix_0", "norm_ffn_0", "norm_mix_1", "pool_scale_1", "norm_ffn_1", "final_norm", "b_f_0", "conv_w_0")
WEIGHTS = ("norm_mix_0", "w_in_0", "b_f_0", "conv_w_0", "w_out_0", "norm_ffn_0", "w_up_0", "w_down_0", "norm_mix_1",
           "pool_w_1", "pool_scale_1", "norm_ffn_1", "w_up_1", "w_down_1", "final_norm")


def _pad_to(a, rows, cols):
    return jnp.pad(a, ((0, rows - a.shape[0]), (0, cols - a.shape[1])))


def _pack_small(p, width):
    rows = [p[n].reshape(1, -1) for n in SMALL[:6]]
    rows.append(_pad_to(p["b_f_0"].reshape(1, -1), 1, width))
    rows.append(_pad_to(p["conv_w_0"], 3, width))
    return _pad_to(jnp.concatenate(rows, axis=0), SMALL_ROWS, width)


def _unpack_small(a, like):
    out = {n: a[i] for i, n in enumerate(SMALL[:6])}
    out["b_f_0"] = a[6, :like["b_f_0"].shape[0]]
    out["conv_w_0"] = a[7:10, :like["conv_w_0"].shape[1]]
    return out


def kernel(x, norm_mix_0, w_in_0, b_f_0, conv_w_0, w_out_0, norm_ffn_0, w_up_0, w_down_0, norm_mix_1, pool_w_1, pool_scale_1, norm_ffn_1, w_up_1, w_down_1, final_norm, loss_target, m_norm_mix_0, m_w_in_0, m_b_f_0, m_conv_w_0, m_w_out_0, m_norm_ffn_0, m_w_up_0, m_w_down_0, m_norm_mix_1, m_pool_w_1, m_pool_scale_1, m_norm_ffn_1, m_w_up_1, m_w_down_1, m_final_norm, v_norm_mix_0, v_w_in_0, v_b_f_0, v_conv_w_0, v_w_out_0, v_norm_ffn_0, v_w_up_0, v_w_down_0, v_norm_mix_1, v_pool_w_1, v_pool_scale_1, v_norm_ffn_1, v_w_up_1, v_w_down_1, v_final_norm):
    w = dict(norm_mix_0=norm_mix_0, w_in_0=w_in_0, b_f_0=b_f_0, conv_w_0=conv_w_0, w_out_0=w_out_0,
             norm_ffn_0=norm_ffn_0, w_up_0=w_up_0, w_down_0=w_down_0, norm_mix_1=norm_mix_1, pool_w_1=pool_w_1,
             pool_scale_1=pool_scale_1, norm_ffn_1=norm_ffn_1, w_up_1=w_up_1, w_down_1=w_down_1, final_norm=final_norm)
    m = dict(norm_mix_0=m_norm_mix_0, w_in_0=m_w_in_0, b_f_0=m_b_f_0, conv_w_0=m_conv_w_0, w_out_0=m_w_out_0,
             norm_ffn_0=m_norm_ffn_0, w_up_0=m_w_up_0, w_down_0=m_w_down_0, norm_mix_1=m_norm_mix_1,
             pool_w_1=m_pool_w_1, pool_scale_1=m_pool_scale_1, norm_ffn_1=m_norm_ffn_1, w_up_1=m_w_up_1,
             w_down_1=m_w_down_1, final_norm=m_final_norm)
    v = dict(norm_mix_0=v_norm_mix_0, w_in_0=v_w_in_0, b_f_0=v_b_f_0, conv_w_0=v_conv_w_0, w_out_0=v_w_out_0,
             norm_ffn_0=v_norm_ffn_0, w_up_0=v_w_up_0, w_down_0=v_w_down_0, norm_mix_1=v_norm_mix_1,
             pool_w_1=v_pool_w_1, pool_scale_1=v_pool_scale_1, norm_ffn_1=v_norm_ffn_1, w_up_1=v_w_up_1,
             w_down_1=v_w_down_1, final_norm=v_final_norm)
    d = x.shape[-1]
    n_in = w_in_0.shape[1] * N_DEV
    n_qkv = 3 * ATTN_W
    pool_g, pool_rows, pool_c = pool_w_1.shape

    def shard2d(p):
        return {n: (p[n].reshape(pool_g * pool_rows, pool_c) if n == "pool_w_1" else p[n]) for n in BIG}
    w2, m2, v2 = shard2d(w), shard2d(m), shard2d(v)

    conv_cols = conv_w_0.shape[1]
    win_g8, conv_g8 = _all_gather([w_in_0.astype(BF16), _pad_to(conv_w_0, 8, 128)])
    conv_full = conv_g8[:, :, :conv_cols].transpose(1, 0, 2).reshape(8, N_DEV * conv_cols)
    win = win_g8.transpose(1, 0, 2).reshape(d, n_in)
    win_p = jnp.concatenate([win[:, :n_qkv], _pad_to(win[:, n_qkv:n_qkv + N_HEADS], d, F_PAD),
                             win[:, n_qkv + N_HEADS:]], axis=1)

    gains = dict(mix0=norm_mix_0.reshape(1, d), ffn0=norm_ffn_0.reshape(1, d), mix1=norm_mix_1.reshape(1, d),
                 ffn1=norm_ffn_1.reshape(1, d), final=final_norm.reshape(1, d))
    dev = _slot(lax.axis_index("x"), lax.axis_index("y"), lax.axis_index("c"))
    loss8, grad_x, landed, small = _local_step(
        x[0], loss_target[0], gains, _pad_to(b_f_0.reshape(1, -1), 1, F_PAD), conv_full, pool_scale_1.reshape(1, d),
        win_p, {n: w2[n].astype(BF16) for n in LATE})
    loss = lax.psum(loss8[0, 0], ("x", "y", "c"))

    parts = jnp.concatenate(
        [small[k][None] for k in ("mix0", "ffn0", "mix1", "pool_scale", "ffn1", "final")]
        + [_pad_to(small["b_f"], 8, d)[None], jnp.pad(small["conv_w"], ((0, 0), (0, 0), (0, d - CONV_CH)))], axis=0)
    tot = _small_allreduce(parts)
    conv_g = lax.dynamic_slice(tot, (7, dev * conv_cols), (3, conv_cols))
    gs = tot.at[7:10].set(_pad_to(conv_g, 3, d))

    grads, deltas, new_m, new_v = {}, {}, {}, {}
    for n in BIG:
        gr, dl, nm, nv = _adamw_sum(landed[n], w2[n], m2[n], v2[n], name="adamw_" + n)
        for dst, val in ((grads, gr), (deltas, dl), (new_m, nm), (new_v, nv)):
            dst[n] = val.reshape(w[n].shape)
    dl, nm, nv = _adamw(gs, _pack_small(w, d), _pack_small(m, d), _pack_small(v, d), name="adamw_small")
    for dst, val in ((grads, gs), (deltas, dl), (new_m, nm), (new_v, nv)):
        dst.update(_unpack_small(val, w))
    return (loss, grad_x[None], *[grads[n] for n in WEIGHTS], *[deltas[n] for n in WEIGHTS],
            *[new_m[n] for n in WEIGHTS], *[new_v[n] for n in WEIGHTS])
```

```python
import functools

import jax
import jax.numpy as jnp
from jax import lax
from jax.experimental import pallas as pl
from jax.experimental.pallas import tpu as pltpu

F32 = jnp.float32
BF16 = jnp.bfloat16

N_DEV = 8
N_HEADS = 8
HEAD_DIM = 64
PAIR = 2 * HEAD_DIM
ATTN_W = N_HEADS * HEAD_DIM
CONV_CH = 512
F_PAD = 128
POOL_WINDOWS = (2, 4, 8, 16)
POOL_HALO = 16
CONV_HALO = 16
RMS_EPS = 1e-6
Q_SCALE = HEAD_DIM ** -0.5
LOG2E = 1.4426950408889634
NEG = -1e30
AUX_BIAS = 0
AUX_LSE = 3
AUX_ROWSUM = 6
ADAM_LR, ADAM_B1, ADAM_B2, ADAM_EPS, ADAM_WD, ADAM_STEP = 0.001, 0.9, 0.999, 1e-08, 0.01, 10
MESH = pl.DeviceIdType.MESH
VMEM_LIMIT = 56 * 2**20


def _cp(sem=None, vmem=VMEM_LIMIT, **kw):
    return pltpu.CompilerParams(dimension_semantics=sem, vmem_limit_bytes=vmem, **kw)


def _dot(a, b):
    return jnp.dot(a, b, preferred_element_type=F32)


def _dot_nt(a, b):
    return lax.dot_general(a, b, (((1,), (1,)), ((), ())), preferred_element_type=F32)


def _dot_tn(a, b):
    return lax.dot_general(a, b, (((0,), (0,)), ((), ())), preferred_element_type=F32)


def _rstd(h):
    return lax.rsqrt(jnp.mean(h * h, axis=-1, keepdims=True) + RMS_EPS)


def _rows8(x):
    r, n = x.shape
    return jnp.sum(x.reshape(r // 8, 8, n), axis=0)


def _norm_bwd(dn, h, g):
    r = _rstd(h)
    xhat = h * r
    dy = dn * g
    dh = r * (dy - xhat * jnp.mean(dy * xhat, axis=-1, keepdims=True))
    return dh, _rows8(dn * xhat)


def _const_spec(shape):
    nd = len(shape)
    return pl.BlockSpec(shape, lambda *_: (0,) * nd, pipeline_mode=pl.Buffered(1))


HBM_SPEC = pl.BlockSpec(memory_space=pltpu.HBM)
VMEM_SPEC = pl.BlockSpec(memory_space=pltpu.VMEM)


def _slot(px, py, pc):
    return 4 * px + 2 * py + pc


class _Exchange:
    def __init__(self, srcs, dsts, send_sems, recv_sems, local_sems, gather):
        x, y, c = lax.axis_index("x"), lax.axis_index("y"), lax.axis_index("c")
        me = _slot(x, y, c)
        self.copies = []
        for a, (src, dst) in enumerate(zip(srcs, dsts)):
            self.copies.append(pltpu.make_async_copy(src if gather else src.at[me], dst.at[me], local_sems.at[a]))
            for k in range(1, N_DEV):
                px, py, pc = x ^ (k >> 2), y ^ ((k >> 1) & 1), c ^ (k & 1)
                self.copies.append(pltpu.make_async_remote_copy(
                    src_ref=src if gather else src.at[_slot(px, py, pc)], dst_ref=dst.at[me],
                    send_sem=send_sems.at[(N_DEV - 1) * a + k - 1], recv_sem=recv_sems.at[(N_DEV - 1) * a + k - 1],
                    device_id=(px, py, pc), device_id_type=MESH))

    def start(self):
        for cp in self.copies:
            cp.start()

    def wait(self):
        for cp in self.copies:
            cp.wait()

    @staticmethod
    def scratch(n):
        return [pltpu.SemaphoreType.DMA(((N_DEV - 1) * n,)), pltpu.SemaphoreType.DMA(((N_DEV - 1) * n,)),
                pltpu.SemaphoreType.DMA((n,))]


def _norm_inproj(x, g, win_p, *, tm=512):
    t, d = x.shape
    n_all = win_p.shape[1]
    n_qkv = 3 * ATTN_W
    n_bcx = 3 * CONV_CH
    assert n_all == n_qkv + F_PAD + n_bcx
    tm = min(tm, t)

    def body(x_ref, g_ref, w_ref, n_ref, qkv_ref, f_ref, bcx_ref):
        h = x_ref[...]
        n = (h * _rstd(h) * g_ref[...]).astype(BF16)
        n_ref[...] = n
        for c0 in range(0, n_qkv, 512):
            acc = _dot(n, w_ref[:, c0:c0 + 512])
            if c0 < ATTN_W:
                acc = acc * (Q_SCALE * LOG2E)
            qkv_ref[:, c0:c0 + 512] = acc.astype(BF16)
        f_ref[...] = _dot(n, w_ref[:, n_qkv:n_qkv + F_PAD])
        for c0 in range(0, n_bcx, 512):
            bcx_ref[:, c0:c0 + 512] = _dot(n, w_ref[:, n_qkv + F_PAD + c0:n_qkv + F_PAD + c0 + 512]).astype(BF16)

    return pl.pallas_call(
        body, name="norm_inproj", grid=(t // tm,),
        in_specs=[pl.BlockSpec((tm, d), lambda i: (i, 0)), _const_spec((1, d)), _const_spec((d, n_all))],
        out_specs=[pl.BlockSpec((tm, d), lambda i: (i, 0)), pl.BlockSpec((tm, n_qkv), lambda i: (i, 0)),
                   pl.BlockSpec((tm, F_PAD), lambda i: (i, 0)), pl.BlockSpec((tm, n_bcx), lambda i: (i, 0))],
        out_shape=[jax.ShapeDtypeStruct((t, d), BF16), jax.ShapeDtypeStruct((t, n_qkv), BF16),
                   jax.ShapeDtypeStruct((t, F_PAD), F32), jax.ShapeDtypeStruct((t, n_bcx), BF16)],
        compiler_params=_cp(("parallel",)),
    )(x, g, win_p)


def _head_lanes(h):
    lane = lax.broadcasted_iota(jnp.int32, (1, PAIR), 1)
    hh = h % 2
    return lane, lane // HEAD_DIM == hh, HEAD_DIM * (1 - hh)


def _pieces(col):
    hi = col.astype(BF16).astype(F32)
    r1 = col - hi
    mid = r1.astype(BF16).astype(F32)
    lo = (r1 - mid).astype(BF16).astype(F32)
    return hi, mid, lo


def _put_pieces(lane, first, col, other):
    hi, mid, lo = _pieces(col)
    return jnp.where(lane == first, hi, jnp.where(lane == first + 1, mid, jnp.where(lane == first + 2, lo, other)))


def _fgate_prep(flog, b_f, qkv, *, tm=512):
    t = flog.shape[0]
    tm = min(tm, t)

    def body(f_ref, b_ref, qkv_ref, qat_ref, ka_ref, va_ref, vat_ref, sg_ref, carry):
        @pl.when(pl.program_id(0) == 0)
        def _():
            carry[...] = jnp.zeros_like(carry)
        z = f_ref[...] + b_ref[...]
        e = jnp.exp(-jnp.abs(z))
        logf = jnp.minimum(z, 0.0) - jnp.log(1.0 + e)
        sg_ref[...] = jnp.where(z >= 0, e, 1.0) / (1.0 + e)
        r = lax.broadcasted_iota(jnp.int32, (tm, tm), 0)
        c = lax.broadcasted_iota(jnp.int32, (tm, tm), 1)
        tri = (c <= r).astype(F32)
        cs = jnp.dot(tri, logf, preferred_element_type=F32, precision=lax.Precision.HIGHEST) + carry[...]
        carry[...] = cs[tm - 1:tm, :]
        cs2 = cs * LOG2E
        for h in range(N_HEADS):
            lane, head, aux = _head_lanes(h)
            p0 = (h // 2) * PAIR
            ones = ((lane >= aux + AUX_LSE) & (lane <= aux + AUX_ROWSUM)).astype(F32)
            bias = (lane >= aux + AUX_BIAS) & (lane < aux + AUX_BIAS + 3)
            k_aux = _put_pieces(lane, aux + AUX_BIAS, cs2[:, h:h + 1], ones)
            q_aug = jnp.where(head, qkv_ref[:, p0:p0 + PAIR].astype(F32), jnp.where(bias, -1.0, 0.0))
            v_aug = jnp.where(head, qkv_ref[:, 2 * ATTN_W + p0:2 * ATTN_W + p0 + PAIR].astype(F32),
                              jnp.where(bias, 1.0, 0.0))
            qat_ref[h] = q_aug.T.astype(BF16)
            ka_ref[h] = jnp.where(head, qkv_ref[:, ATTN_W + p0:ATTN_W + p0 + PAIR], k_aux.astype(BF16))
            va_ref[h] = v_aug.astype(BF16)
            vat_ref[h] = v_aug.T.astype(BF16)

    aug = lambda: pl.BlockSpec((N_HEADS, tm, PAIR), lambda i: (0, i, 0))
    aug_t = lambda: pl.BlockSpec((N_HEADS, PAIR, tm), lambda i: (0, 0, i))
    aug_shape = jax.ShapeDtypeStruct((N_HEADS, t, PAIR), BF16)
    aug_t_shape = jax.ShapeDtypeStruct((N_HEADS, PAIR, t), BF16)
    return pl.pallas_call(
        body, name="fgate_prep", grid=(t // tm,),
        in_specs=[pl.BlockSpec((tm, F_PAD), lambda i: (i, 0)), _const_spec((1, F_PAD)),
                  pl.BlockSpec((tm, 3 * ATTN_W), lambda i: (i, 0))],
        out_specs=[aug_t(), aug(), aug(), aug_t(), pl.BlockSpec((tm, F_PAD), lambda i: (i, 0))],
        out_shape=[aug_t_shape, aug_shape, aug_shape, aug_t_shape, jax.ShapeDtypeStruct((t, F_PAD), F32)],
        scratch_shapes=[pltpu.VMEM((1, F_PAD), F32)],
        compiler_params=_cp(("arbitrary",)),
    )(flog, b_f, qkv)


def _put_pieces_t(row, first, vec, other):
    hi, mid, lo = _pieces(vec)
    return jnp.where(row == first, hi, jnp.where(row == first + 1, mid, jnp.where(row == first + 2, lo, other)))


def _attn_fwd(q_aug_t, k_aug, v_aug_t, shards, *, tq=1024):
    t = k_aug.shape[1]
    tq = min(tq, t)
    tk = tq // 2
    nq = t // tq
    n_pairs = ATTN_W // PAIR
    n_sh = len(shards)

    def body(qt_ref, k_ref, vt_ref, *rest):
        o_ref, qb_ref, qbt_ref = rest[n_sh:n_sh + 3]
        s_scr = rest[2 * n_sh + 3]
        gather = _Exchange(rest[:n_sh], rest[n_sh + 3:2 * n_sh + 3], *rest[2 * n_sh + 4:], gather=True)
        i = pl.program_id(1)

        @pl.when((pl.program_id(0) == 0) & (i == 0))
        def _():
            gather.start()
        key = lax.broadcasted_iota(jnp.int32, (tk, tq), 0)
        qry = lax.broadcasted_iota(jnp.int32, (tk, tq), 1)
        qt = [qt_ref[0], qt_ref[1]]

        def logits(hh, tile, slot, diag):
            s = _dot(k_ref[hh, pl.ds(pl.multiple_of(tile * tk, tk), tk), :], qt[hh])
            if diag:
                s = jnp.where(key + (tile * tk - i * tq) <= qry, s, NEG)
            s_scr[hh, slot] = s
            return jnp.max(s, axis=0, keepdims=True)

        def probs(hh, tile, slot, m, acc, tmax):
            mn = jnp.maximum(m, tmax)
            p = jnp.exp2(s_scr[hh, slot] - mn).astype(BF16)
            acc = jnp.exp2(m - mn) * acc + _dot(vt_ref[hh, :, pl.ds(pl.multiple_of(tile * tk, tk), tk)], p)
            return mn, acc

        def advance(carry, prev, slot, nxt, diag=False):
            out = []
            for hh in range(2):
                m, acc, tmax = carry[hh]
                m, acc = probs(hh, prev, slot, m, acc, tmax)
                out.append((m, acc, logits(hh, nxt, 1 - slot, diag)))
            return tuple(out)

        def two_tiles(jj, carry):
            carry = advance(carry, jnp.where(jj == 0, 2 * i, 2 * jj - 1), 1, 2 * jj)
            return advance(carry, 2 * jj, 0, 2 * jj + 1)

        init = tuple((jnp.full((1, tq), NEG, F32), jnp.zeros((PAIR, tq), F32), logits(hh, 2 * i + 1, 0, True))
                     for hh in range(2))
        carry = advance(init, 2 * i + 1, 0, 2 * i, diag=True)
        carry = lax.fori_loop(0, i, two_tiles, carry)
        last = jnp.where(i == 0, 2 * i, 2 * i - 1)
        row = lax.broadcasted_iota(jnp.int32, (PAIR, 1), 0)
        res = []
        for hh in range(2):
            aux = HEAD_DIM * (1 - hh)
            m, acc, tmax = carry[hh]
            m, acc = probs(hh, last, 1, m, acc, tmax)
            l = acc[aux + AUX_BIAS:aux + AUX_BIAS + 1, :]
            qbt = _put_pieces_t(row, aux + AUX_LSE, -(m + jnp.log2(l)), qt[hh].astype(F32))
            qbt_ref[hh] = qbt.astype(BF16)
            qb_ref[hh] = qbt.T.astype(BF16)
            res.append(acc * (1.0 / l))
        o_ref[...] = jnp.where(row < HEAD_DIM, res[0], res[1]).T.astype(BF16)

        @pl.when((pl.program_id(0) == n_pairs - 1) & (i == nq - 1))
        def _():
            gather.wait()

    res = pl.pallas_call(
        body, name="attn_fwd", grid=(n_pairs, nq),
        in_specs=[pl.BlockSpec((2, PAIR, tq), lambda p, i: (p, 0, i)),
                  pl.BlockSpec((2, t, PAIR), lambda p, i: (p, 0, 0), pipeline_mode=pl.Buffered(1)),
                  pl.BlockSpec((2, PAIR, t), lambda p, i: (p, 0, 0), pipeline_mode=pl.Buffered(1))] + [HBM_SPEC] * n_sh,
        out_specs=[pl.BlockSpec((tq, PAIR), lambda p, i: (i, p)),
                   pl.BlockSpec((2, tq, PAIR), lambda p, i: (p, i, 0)),
                   pl.BlockSpec((2, PAIR, tq), lambda p, i: (p, 0, i))] + [HBM_SPEC] * n_sh,
        out_shape=[jax.ShapeDtypeStruct((t, ATTN_W), BF16), jax.ShapeDtypeStruct((N_HEADS, t, PAIR), BF16),
                   jax.ShapeDtypeStruct((N_HEADS, PAIR, t), BF16)]
        + [jax.ShapeDtypeStruct((N_DEV,) + s.shape, s.dtype) for s in shards],
        scratch_shapes=[pltpu.VMEM((2, 2, tk, tq), F32)] + _Exchange.scratch(n_sh),
        compiler_params=_cp(("arbitrary", "arbitrary")),
    )(q_aug_t, k_aug, v_aug_t, *shards)
    return res[0], res[1], res[2], res[3:]


def _prev_halo(tm, halo):
    return lambda i: (jnp.maximum(i * (tm // halo) - 1, 0), 0)


def _next_halo(tm, halo, t):
    return lambda i: (jnp.minimum((i + 1) * (tm // halo), t // halo - 1), 0)


def _conv_fwd(bcx, conv_w, *, tm=512):
    t = bcx.shape[0]
    tm = min(tm, t)
    ch = CONV_CH

    def body(b_ref, c_ref, x_ref, hc_ref, hx_ref, w_ref, cv_ref, ext):
        first = pl.program_id(0) == 0
        ext[0:CONV_HALO, :] = jnp.where(first, 0.0, hc_ref[...].astype(F32) * hx_ref[...].astype(F32))
        ext[CONV_HALO:CONV_HALO + tm, :] = c_ref[...].astype(F32) * x_ref[...].astype(F32)
        conv = (w_ref[0:1, :] * ext[CONV_HALO - 2:CONV_HALO - 2 + tm, :]
                + w_ref[1:2, :] * ext[CONV_HALO - 1:CONV_HALO - 1 + tm, :]
                + w_ref[2:3, :] * ext[CONV_HALO:CONV_HALO + tm, :])
        cv_ref[...] = (b_ref[...].astype(F32) * conv).astype(BF16)

    col = lambda k: pl.BlockSpec((tm, ch), lambda i: (i, k))
    halo = lambda k: pl.BlockSpec((CONV_HALO, ch), lambda i: (_prev_halo(tm, CONV_HALO)(i)[0], k))
    return pl.pallas_call(
        body, name="conv_fwd", grid=(t // tm,),
        in_specs=[col(0), col(1), col(2), halo(1), halo(2), _const_spec((8, ch))],
        out_specs=pl.BlockSpec((tm, ch), lambda i: (i, 0)),
        out_shape=jax.ShapeDtypeStruct((t, ch), BF16),
        scratch_shapes=[pltpu.VMEM((CONV_HALO + tm, ch), F32)],
        compiler_params=_cp(("parallel",)),
    )(bcx, bcx, bcx, bcx, bcx, conv_w)


def _mlp_tile(hh, g_ref, wu_ref, wd_ref, n_ref, a_ref, z_ref):
    n_blk, _, fb = wu_ref.shape
    n = (hh * _rstd(hh) * g_ref[...]).astype(BF16)
    n_ref[...] = n
    acc = hh
    for k in range(n_blk):
        a = _dot(n, wu_ref[k])
        zz = jnp.square(jnp.maximum(a, 0.0)).astype(BF16)
        a_ref[:, k * fb:(k + 1) * fb] = a.astype(BF16)
        z_ref[:, k * fb:(k + 1) * fb] = zz
        acc = acc + _dot(zz, wd_ref[k * fb:(k + 1) * fb, :])
    return acc


def _outproj(att, cv, x, wout, *, tm=512):
    t, d = x.shape
    tm = min(tm, t)

    def body(a_ref, c_ref, x_ref, w_ref, h_ref):
        h_ref[...] = x_ref[...] + _dot(a_ref[...], w_ref[0:ATTN_W, :]) + _dot(c_ref[...], w_ref[ATTN_W:, :])

    return pl.pallas_call(
        body, name="outproj", grid=(t // tm,),
        in_specs=[pl.BlockSpec((tm, ATTN_W), lambda i: (i, 0)), pl.BlockSpec((tm, CONV_CH), lambda i: (i, 0)),
                  pl.BlockSpec((tm, d), lambda i: (i, 0)), _const_spec(wout.shape)],
        out_specs=pl.BlockSpec((tm, d), lambda i: (i, 0)),
        out_shape=jax.ShapeDtypeStruct((t, d), F32),
        compiler_params=_cp(("parallel",)),
    )(att, cv, x, wout)


def _mlp_fwd(h, g, wup, wdown, *, name, tm=512):
    t, d = h.shape
    n_blk, _, fb = wup.shape
    f = n_blk * fb
    tm = min(tm, t)

    def body(h_ref, g_ref, wu_ref, wd_ref, ho_ref, n_ref, a_ref, z_ref):
        ho_ref[...] = _mlp_tile(h_ref[...], g_ref, wu_ref, wd_ref, n_ref, a_ref, z_ref)

    row = lambda n_: pl.BlockSpec((tm, n_), lambda i: (i, 0))
    return pl.pallas_call(
        body, name=name, grid=(t // tm,),
        in_specs=[row(d), _const_spec((1, d)), _const_spec(wup.shape), _const_spec(wdown.shape)],
        out_specs=[row(d), row(d), row(f), row(f)],
        out_shape=[jax.ShapeDtypeStruct((t, d), F32), jax.ShapeDtypeStruct((t, d), BF16),
                   jax.ShapeDtypeStruct((t, f), BF16), jax.ShapeDtypeStruct((t, f), BF16)],
        compiler_params=_cp(("parallel",)),
    )(h, g, wup, wdown)


def _mlp_fwd_loss(h, g, wup, wdown, g_out, target, *, name, tm=512):
    t, d = h.shape
    n_blk, _, fb = wup.shape
    f = n_blk * fb
    tm = min(tm, t)
    nsteps = t // tm

    def body(h_ref, g_ref, wu_ref, wd_ref, go_ref, y_ref, loss_ref, dh_ref, dg_ref, n_ref, a_ref, z_ref, lacc):
        i = pl.program_id(0)

        @pl.when(i == 0)
        def _():
            lacc[...] = jnp.zeros_like(lacc)
            dg_ref[...] = jnp.zeros_like(dg_ref)
        hv = _mlp_tile(h_ref[...], g_ref, wu_ref, wd_ref, n_ref, a_ref, z_ref)
        gv = go_ref[...]
        r = _rstd(hv)
        xhat = hv * r
        err = xhat * gv - y_ref[...]
        lacc[...] += _rows8(err * err)
        dout = err * (1.0 / d)
        dy = dout * gv
        dg_ref[...] += _rows8(dout * xhat)
        dh_ref[...] = r * (dy - xhat * jnp.mean(dy * xhat, axis=-1, keepdims=True))

        @pl.when(i == nsteps - 1)
        def _():
            loss_ref[...] = jnp.full(loss_ref.shape, (0.5 / d) * jnp.sum(lacc[...]), F32)

    row = lambda n_: pl.BlockSpec((tm, n_), lambda i: (i, 0))
    return pl.pallas_call(
        body, name=name, grid=(nsteps,),
        in_specs=[row(d), _const_spec((1, d)), _const_spec(wup.shape), _const_spec(wdown.shape), _const_spec((1, d)),
                  row(d)],
        out_specs=[pl.BlockSpec((8, 128), lambda i: (0, 0)), row(d), pl.BlockSpec((8, d), lambda i: (0, 0)),
                   row(d), row(f), row(f)],
        out_shape=[jax.ShapeDtypeStruct((8, 128), F32), jax.ShapeDtypeStruct((t, d), F32),
                   jax.ShapeDtypeStruct((8, d), F32), jax.ShapeDtypeStruct((t, d), BF16),
                   jax.ShapeDtypeStruct((t, f), BF16), jax.ShapeDtypeStruct((t, f), BF16)],
        scratch_shapes=[pltpu.VMEM((8, d), F32)],
        compiler_params=_cp(("arbitrary",)),
    )(h, g, wup, wdown, g_out, target)


def _pool_inv_count(i, tm):
    tglob = (i * tm + lax.broadcasted_iota(jnp.int32, (tm, 1), 0) + 1).astype(F32)
    return [1.0 / jnp.minimum(tglob, float(w)) for w in POOL_WINDOWS]


def _pool_fwd(h, g, poolw, scale, *, tm=256):
    t, d = h.shape
    tm = min(tm, t)
    cg = d // len(POOL_WINDOWS)

    def body(h_ref, hh_ref, g_ref, w_ref, s_ref, ho_ref, p_ref, ext):
        i = pl.program_id(0)
        hv = h_ref[...]
        halo = hh_ref[...]
        n = hv * _rstd(hv) * g_ref[...]
        ext[0:POOL_HALO, :] = jnp.where(i == 0, 0.0, halo * _rstd(halo) * g_ref[...])
        ext[POOL_HALO:POOL_HALO + tm, :] = n
        inv = _pool_inv_count(i, tm)
        for gi, w in enumerate(POOL_WINDOWS):
            cs = slice(gi * cg, (gi + 1) * cg)
            s = ext[POOL_HALO:POOL_HALO + tm, cs]
            for j in range(1, w):
                s = s + ext[POOL_HALO - j:POOL_HALO - j + tm, cs]
            pooled = (s * inv[gi] - n[:, cs]).astype(BF16)
            p_ref[:, cs] = pooled
            ho_ref[:, cs] = hv[:, cs] + _dot(pooled, w_ref[gi]) * s_ref[:, cs]

    row = lambda: pl.BlockSpec((tm, d), lambda i: (i, 0))
    return pl.pallas_call(
        body, name="pool_fwd", grid=(t // tm,),
        in_specs=[row(), pl.BlockSpec((POOL_HALO, d), _prev_halo(tm, POOL_HALO)), _const_spec((1, d)),
                  _const_spec(poolw.shape), _const_spec((1, d))],
        out_specs=[row(), row()],
        out_shape=[jax.ShapeDtypeStruct((t, d), F32), jax.ShapeDtypeStruct((t, d), BF16)],
        scratch_shapes=[pltpu.VMEM((POOL_HALO + tm, d), F32)],
        compiler_params=_cp(("parallel",)),
    )(h, h, g, poolw, scale)


def _mm_tn(a, b, *, name, ta, tb, tt, blocked_out=False, out_dtype=F32):
    t, ka = a.shape
    n = b.shape[1]
    ta, tb, tt = min(ta, ka), min(tb, n), min(tt, t)
    nt = t // tt

    def body(a_ref, b_ref, o_ref, acc):
        @pl.when(pl.program_id(2) == 0)
        def _():
            acc[...] = jnp.zeros_like(acc)
        acc[...] += _dot_tn(a_ref[...].astype(BF16), b_ref[...].astype(BF16))

        @pl.when(pl.program_id(2) == nt - 1)
        def _():
            o_ref[...] = acc[...].astype(out_dtype)

    if blocked_out:
        assert ta == ka
        out_shape = jax.ShapeDtypeStruct((n // tb, ka, tb), out_dtype)
        out_spec = pl.BlockSpec((None, ta, tb), lambda i, j, k: (j, i, 0))
    else:
        out_shape = jax.ShapeDtypeStruct((ka, n), out_dtype)
        out_spec = pl.BlockSpec((ta, tb), lambda i, j, k: (i, j))
    return pl.pallas_call(
        body, name=name, grid=(ka // ta, n // tb, nt),
        in_specs=[pl.BlockSpec((tt, ta), lambda i, j, k: (k, i)), pl.BlockSpec((tt, tb), lambda i, j, k: (k, j))],
        out_specs=out_spec, out_shape=out_shape, scratch_shapes=[pltpu.VMEM((ta, tb), F32)],
        compiler_params=_cp(("parallel", "parallel", "arbitrary")),
    )(a, b)


def _mlp_bwd(dho, h, a, g, wup, wdown, *, name, tm=512):
    t, d = h.shape
    n_blk, _, fb = wup.shape
    f = n_blk * fb
    tm = min(tm, t)

    def body(do_ref, h_ref, a_ref, g_ref, wu_ref, wd_ref, dh_ref, da_ref, dg_ref):
        @pl.when(pl.program_id(0) == 0)
        def _():
            dg_ref[...] = jnp.zeros_like(dg_ref)
        dho_v = do_ref[...]
        dob = dho_v.astype(BF16)
        dn = jnp.zeros((tm, d), F32)
        for k in range(n_blk):
            dz = _dot_nt(dob, wd_ref[k * fb:(k + 1) * fb, :])
            da = (dz * (2.0 * jnp.maximum(a_ref[:, k * fb:(k + 1) * fb].astype(F32), 0.0))).astype(BF16)
            da_ref[:, k * fb:(k + 1) * fb] = da
            dn = dn + _dot_nt(da, wu_ref[k])
        dh, dg = _norm_bwd(dn, h_ref[...], g_ref[...])
        dh_ref[...] = dho_v + dh
        dg_ref[...] += dg

    row = lambda n_: pl.BlockSpec((tm, n_), lambda i: (i, 0))
    return pl.pallas_call(
        body, name=name, grid=(t // tm,),
        in_specs=[row(d), row(d), row(f), _const_spec((1, d)), _const_spec(wup.shape), _const_spec(wdown.shape)],
        out_specs=[row(d), row(f), pl.BlockSpec((8, d), lambda i: (0, 0))],
        out_shape=[jax.ShapeDtypeStruct((t, d), F32), jax.ShapeDtypeStruct((t, f), BF16),
                   jax.ShapeDtypeStruct((8, d), F32)],
        compiler_params=_cp(("arbitrary",)),
    )(dho, h, a, g, wup, wdown)


def _pool_bwd(dho, h, pooled, g, poolw, scale, *, tm=256):
    t, d = h.shape
    tm = min(tm, t)
    ng = len(POOL_WINDOWS)
    cg = d // ng
    nsteps = t // tm

    def body(do_ref, dn_ref, h_ref, p_ref, g_ref, w_ref, s_ref, dh_ref, dw_ref, ds_ref, dg_ref, ext):
        i = pl.program_id(0)

        @pl.when(i == 0)
        def _():
            dw_ref[...] = jnp.zeros_like(dw_ref)
            ds_ref[...] = jnp.zeros_like(ds_ref)
            dg_ref[...] = jnp.zeros_like(dg_ref)
        dho_v = do_ref[...]
        sv = s_ref[...]
        dyp = (dho_v * sv).astype(BF16)
        dyp_halo = (dn_ref[...] * sv).astype(BF16)
        inv = _pool_inv_count(i, tm)
        tnext = ((i + 1) * tm + lax.broadcasted_iota(jnp.int32, (POOL_HALO, 1), 0) + 1).astype(F32)
        last = i == nsteps - 1
        ypre_parts, dpooled_parts = [], []
        for gi, w in enumerate(POOL_WINDOWS):
            cs = slice(gi * cg, (gi + 1) * cg)
            pg = p_ref[:, cs]
            ypre_parts.append(_dot(pg, w_ref[gi]))
            dw_ref[gi] += _dot_tn(pg, dyp[:, cs])
            dpool = _dot_nt(dyp[:, cs], w_ref[gi])
            dpooled_parts.append(dpool)
            ext[0:tm, cs] = dpool * inv[gi]
            dpool_halo = _dot_nt(dyp_halo[:, cs], w_ref[gi]) * (1.0 / jnp.minimum(tnext, float(w)))
            ext[tm:tm + POOL_HALO, cs] = jnp.where(last, 0.0, dpool_halo)
        ds_ref[...] += _rows8(dho_v * jnp.concatenate(ypre_parts, axis=1))
        dn_parts = []
        for gi, w in enumerate(POOL_WINDOWS):
            cs = slice(gi * cg, (gi + 1) * cg)
            s = ext[0:tm, cs]
            for j in range(1, w):
                s = s + ext[j:j + tm, cs]
            dn_parts.append(s - dpooled_parts[gi])
        dh, dg = _norm_bwd(jnp.concatenate(dn_parts, axis=1), h_ref[...], g_ref[...])
        dh_ref[...] = dho_v + dh
        dg_ref[...] += dg

    row = lambda: pl.BlockSpec((tm, d), lambda i: (i, 0))
    acc8 = lambda: pl.BlockSpec((8, d), lambda i: (0, 0))
    return pl.pallas_call(
        body, name="pool_bwd", grid=(nsteps,),
        in_specs=[row(), pl.BlockSpec((POOL_HALO, d), _next_halo(tm, POOL_HALO, t)), row(), row(),
                  _const_spec((1, d)), _const_spec(poolw.shape), _const_spec((1, d))],
        out_specs=[row(), pl.BlockSpec((ng, cg, cg), lambda i: (0, 0, 0)), acc8(), acc8()],
        out_shape=[jax.ShapeDtypeStruct((t, d), F32), jax.ShapeDtypeStruct((ng, cg, cg), F32),
                   jax.ShapeDtypeStruct((8, d), F32), jax.ShapeDtypeStruct((8, d), F32)],
        scratch_shapes=[pltpu.VMEM((tm + POOL_HALO, d), F32)],
        compiler_params=_cp(("arbitrary",)),
    )(dho, dho, h, pooled, g, poolw, scale)


def _outproj_bwd(dh, o, wout, *, tm=512):
    t, d = dh.shape
    tm = min(tm, t)

    def body(dh_ref, o_ref, w_ref, da_ref, dat_ref, dc_ref):
        dhb = dh_ref[...].astype(BF16)
        dc_ref[...] = _dot_nt(dhb, w_ref[ATTN_W:, :])
        for p in range(ATTN_W // PAIR):
            datt = _dot_nt(dhb, w_ref[p * PAIR:(p + 1) * PAIR, :])
            prod = datt * o_ref[:, p * PAIR:(p + 1) * PAIR].astype(F32)
            for hh in range(2):
                lane, head, aux = _head_lanes(hh)
                delta = jnp.sum(jnp.where(head, prod, 0.0), axis=1, keepdims=True)
                aug = _put_pieces(lane, aux + AUX_BIAS, -delta, jnp.where(head, datt, 0.0))
                da_ref[2 * p + hh] = aug.astype(BF16)
                dat_ref[2 * p + hh] = aug.T.astype(BF16)

    row = lambda n_: pl.BlockSpec((tm, n_), lambda i: (i, 0))
    return pl.pallas_call(
        body, name="outproj_bwd", grid=(t // tm,),
        in_specs=[row(d), row(ATTN_W), _const_spec(wout.shape)],
        out_specs=[pl.BlockSpec((N_HEADS, tm, PAIR), lambda i: (0, i, 0)),
                   pl.BlockSpec((N_HEADS, PAIR, tm), lambda i: (0, 0, i)), row(CONV_CH)],
        out_shape=[jax.ShapeDtypeStruct((N_HEADS, t, PAIR), BF16), jax.ShapeDtypeStruct((N_HEADS, PAIR, t), BF16),
                   jax.ShapeDtypeStruct((t, CONV_CH), F32)],
        compiler_params=_cp(("parallel",)),
    )(dh, o, wout)


def _conv_bwd(bcx, dcv, conv_w, *, tm=512):
    t = bcx.shape[0]
    tm = min(tm, t)
    ch = CONV_CH
    nsteps = t // tm

    def body(b_ref, c_ref, x_ref, hc_ref, hx_ref, d_ref, nb_ref, nd_ref, w_ref, o_ref, dw_ref, ext_u, ext_d):
        i = pl.program_id(0)

        @pl.when(i == 0)
        def _():
            dw_ref[...] = jnp.zeros_like(dw_ref)
        b, c, x, dcv_v = b_ref[...].astype(F32), c_ref[...].astype(F32), x_ref[...].astype(F32), d_ref[...]
        ext_u[0:CONV_HALO, :] = jnp.where(i == 0, 0.0, hc_ref[...].astype(F32) * hx_ref[...].astype(F32))
        ext_u[CONV_HALO:CONV_HALO + tm, :] = c * x
        dconv = dcv_v * b
        ext_d[0:tm, :] = dconv
        ext_d[tm:tm + CONV_HALO, :] = jnp.where(i == nsteps - 1, 0.0, nd_ref[...] * nb_ref[...].astype(F32))
        u = [ext_u[CONV_HALO - 2 + k:CONV_HALO - 2 + k + tm, :] for k in range(3)]
        conv = w_ref[0:1, :] * u[0] + w_ref[1:2, :] * u[1] + w_ref[2:3, :] * u[2]
        du = (w_ref[2:3, :] * dconv + w_ref[1:2, :] * ext_d[1:1 + tm, :] + w_ref[0:1, :] * ext_d[2:2 + tm, :])
        o_ref[:, 0:ch] = (dcv_v * conv).astype(BF16)
        o_ref[:, ch:2 * ch] = (du * x).astype(BF16)
        o_ref[:, 2 * ch:3 * ch] = (du * c).astype(BF16)
        for k in range(3):
            dw_ref[k] += _rows8(dconv * u[k])

    col = lambda k: pl.BlockSpec((tm, ch), lambda i: (i, k))
    prev = lambda k: pl.BlockSpec((CONV_HALO, ch), lambda i: (_prev_halo(tm, CONV_HALO)(i)[0], k))
    nxt = lambda k: pl.BlockSpec((CONV_HALO, ch), lambda i: (_next_halo(tm, CONV_HALO, t)(i)[0], k))
    return pl.pallas_call(
        body, name="conv_bwd", grid=(nsteps,),
        in_specs=[col(0), col(1), col(2), prev(1), prev(2), col(0), nxt(0), nxt(0), _const_spec((8, ch))],
        out_specs=[pl.BlockSpec((tm, 3 * ch), lambda i: (i, 0)), pl.BlockSpec((3, 8, ch), lambda i: (0, 0, 0))],
        out_shape=[jax.ShapeDtypeStruct((t, 3 * ch), BF16), jax.ShapeDtypeStruct((3, 8, ch), F32)],
        scratch_shapes=[pltpu.VMEM((CONV_HALO + tm, ch), F32), pltpu.VMEM((tm + CONV_HALO, ch), F32)],
        compiler_params=_cp(("arbitrary",)),
    )(bcx, bcx, bcx, bcx, bcx, dcv, bcx, dcv, conv_w)


def _attn_bwd(q_bwd, do_aug, q_bwd_t, do_aug_t, k_aug, v_aug, gblocks, *, tq=1024):
    t = q_bwd.shape[1]
    tq = min(tq, t)
    tk = tq // 2
    nq, nk = t // tq, t // tk
    n_pairs = ATTN_W // PAIR
    n_g = len(gblocks)

    def body(q_ref, do_ref, qt_ref, dot_ref, k_ref, v_ref, *rest):
        dq_ref, dqx_ref, dk_ref, dkx_ref, dv_ref = rest[n_g:n_g + 5]
        dq_scr = rest[2 * n_g + 5]
        scatter = _Exchange(rest[:n_g], rest[n_g + 5:2 * n_g + 5], *rest[2 * n_g + 6:], gather=False)
        j = pl.program_id(1)

        @pl.when((pl.program_id(0) == 0) & (j == 0))
        def _():
            scatter.start()

        @pl.when(j == 0)
        def _():
            dq_scr[...] = jnp.zeros_like(dq_scr)
        k = [k_ref[0], k_ref[1]]
        v = [v_ref[0], v_ref[1]]

        def step(i, carry, diag, rows=tq, row0=0):
            qs = pl.multiple_of(i * tq + row0, tk)
            if diag:
                row = lax.broadcasted_iota(jnp.int32, (rows, tk), 0)
                col = lax.broadcasted_iota(jnp.int32, (rows, tk), 1)
            out = []
            for hh in range(2):
                dk_a, dv_a = carry[hh]
                q = q_ref[hh, pl.ds(qs, rows), :]
                dov = do_ref[hh, pl.ds(qs, rows), :]
                p = jnp.exp2(_dot_nt(q, k[hh]))
                if diag:
                    p = jnp.where(col + (j * tk - i * tq - row0) <= row, p, 0.0)
                ds = (p * _dot_nt(dov, v[hh])).astype(BF16)
                dv_a = dv_a + _dot(dot_ref[hh, :, pl.ds(qs, rows)], p.astype(BF16))
                dk_a = dk_a + _dot(qt_ref[hh, :, pl.ds(qs, rows)], ds)
                dq_scr[hh, pl.ds(qs, rows), :] += _dot(ds, k[hh])
                out.append((dk_a, dv_a))
            return tuple(out)

        zero = (jnp.zeros((PAIR, tk), F32), jnp.zeros((PAIR, tk), F32))
        carry = lax.cond(j % 2 == 0, lambda c: step(j // 2, c, True),
                         lambda c: step(j // 2, c, True, rows=tk, row0=tk), (zero, zero))
        (dk0, dv0), (dk1, dv1) = lax.fori_loop(j // 2 + 1, nq, functools.partial(step, diag=False), carry)
        first_t = lax.broadcasted_iota(jnp.int32, (PAIR, 1), 0) < HEAD_DIM
        first = lax.broadcasted_iota(jnp.int32, (1, PAIR), 1) < HEAD_DIM
        dk_ref[...] = (jnp.where(first_t, dk0, dk1).T * (1.0 / LOG2E)).astype(BF16)
        dkx_ref[...] = jnp.where(first_t, dk1, dk0).T
        dv_ref[...] = jnp.where(first_t, dv0, dv1).T.astype(BF16)

        @pl.when(j == nk - 1)
        def _():
            dq_ref[...] = (jnp.where(first, dq_scr[0], dq_scr[1]) * Q_SCALE).astype(BF16)
            dqx_ref[...] = jnp.where(first, dq_scr[1], dq_scr[0])

        @pl.when((pl.program_id(0) == n_pairs - 1) & (j == nk - 1))
        def _():
            scatter.wait()

    resident = lambda: pl.BlockSpec((2, t, PAIR), lambda p, j: (p, 0, 0), pipeline_mode=pl.Buffered(1))
    resident_t = lambda: pl.BlockSpec((2, PAIR, t), lambda p, j: (p, 0, 0), pipeline_mode=pl.Buffered(1))
    kv_in = lambda: pl.BlockSpec((2, tk, PAIR), lambda p, j: (p, j, 0))
    whole = lambda: pl.BlockSpec((t, PAIR), lambda p, j: (0, p))
    tile = lambda: pl.BlockSpec((tk, PAIR), lambda p, j: (j, p))
    b16 = jax.ShapeDtypeStruct((t, ATTN_W), BF16)
    f32 = jax.ShapeDtypeStruct((t, ATTN_W), F32)
    res = pl.pallas_call(
        body, name="attn_bwd", grid=(n_pairs, nk),
        in_specs=[resident(), resident(), resident_t(), resident_t(), kv_in(), kv_in()] + [HBM_SPEC] * n_g,
        out_specs=[whole(), whole(), tile(), tile(), tile()] + [HBM_SPEC] * n_g,
        out_shape=[b16, f32, b16, f32, b16] + [jax.ShapeDtypeStruct(g.shape, g.dtype) for g in gblocks],
        scratch_shapes=[pltpu.VMEM((2, t, PAIR), F32)] + _Exchange.scratch(n_g),
        compiler_params=_cp(("arbitrary", "arbitrary")),
    )(q_bwd, do_aug, q_bwd_t, do_aug_t, k_aug, v_aug, *gblocks)
    return res[:5], res[5:]


def _fgate_bwd(dqx, dkx, sgate, *, tm=256):
    t = sgate.shape[0]
    tm = min(tm, t)
    nsteps = t // tm

    def body(dq_ref, dk_ref, sg_ref, df_ref, dbf_ref, carry):
        @pl.when(pl.program_id(0) == 0)
        def _():
            carry[...] = jnp.zeros_like(carry)
            dbf_ref[...] = jnp.zeros_like(dbf_ref)
        lane = lax.broadcasted_iota(jnp.int32, (ATTN_W, F_PAD), 0)
        head = lax.broadcasted_iota(jnp.int32, (ATTN_W, F_PAD), 1)
        aux = (head // 2) * PAIR + HEAD_DIM * (1 - head % 2)
        valid = head < N_HEADS
        pick_r = (valid & (lane == aux + AUX_ROWSUM)).astype(F32)
        pick_c = (valid & (lane == aux + AUX_BIAS)).astype(F32)
        hp = lax.Precision.HIGHEST
        dcum = (jnp.dot(dq_ref[...], pick_r, preferred_element_type=F32, precision=hp)
                + jnp.dot(dk_ref[...], pick_c, preferred_element_type=F32, precision=hp))
        r = lax.broadcasted_iota(jnp.int32, (tm, tm), 0)
        c = lax.broadcasted_iota(jnp.int32, (tm, tm), 1)
        tri = (c >= r).astype(F32)
        rc = jnp.dot(tri, dcum, preferred_element_type=F32, precision=hp) + carry[...]
        carry[...] = rc[0:1, :]
        df = rc * sg_ref[...]
        df_ref[...] = df.astype(BF16)
        dbf_ref[...] += _rows8(df)

    rev = lambda i: nsteps - 1 - i
    return pl.pallas_call(
        body, name="fgate_bwd", grid=(nsteps,),
        in_specs=[pl.BlockSpec((tm, ATTN_W), lambda i: (rev(i), 0)), pl.BlockSpec((tm, ATTN_W), lambda i: (rev(i), 0)),
                  pl.BlockSpec((tm, F_PAD), lambda i: (rev(i), 0))],
        out_specs=[pl.BlockSpec((tm, F_PAD), lambda i: (rev(i), 0)), pl.BlockSpec((8, F_PAD), lambda i: (0, 0))],
        out_shape=[jax.ShapeDtypeStruct((t, F_PAD), BF16), jax.ShapeDtypeStruct((8, F_PAD), F32)],
        scratch_shapes=[pltpu.VMEM((1, F_PAD), F32)],
        compiler_params=_cp(("arbitrary",)),
    )(dqx, dkx, sgate)


def _inproj_bwd(dq, dk, dv, df, dbcx, dh, x, g, win_p, gblock, *, tm=512):
    t, d = x.shape
    tm = min(tm, t)
    nsteps = t // tm
    n_qkv = 3 * ATTN_W

    def body(dq_ref, dk_ref, dv_ref, df_ref, db_ref, dh_ref, x_ref, g_ref, w_ref, gb_ref, gx_ref, dg_ref, land_ref,
             *sems):
        scatter = _Exchange([gb_ref], [land_ref], *sems, gather=False)

        @pl.when(pl.program_id(0) == 0)
        def _():
            scatter.start()
            dg_ref[...] = jnp.zeros_like(dg_ref)
        dn = _dot_nt(df_ref[...], w_ref[:, n_qkv:n_qkv + F_PAD])
        for k, r in enumerate((dq_ref, dk_ref, dv_ref)):
            dn = dn + _dot_nt(r[...], w_ref[:, k * ATTN_W:(k + 1) * ATTN_W])
        for k in range(3):
            c0 = n_qkv + F_PAD + k * CONV_CH
            dn = dn + _dot_nt(db_ref[:, k * CONV_CH:(k + 1) * CONV_CH], w_ref[:, c0:c0 + CONV_CH])
        dx, dg = _norm_bwd(dn, x_ref[...], g_ref[...])
        gx_ref[...] = dh_ref[...] + dx
        dg_ref[...] += dg

        @pl.when(pl.program_id(0) == nsteps - 1)
        def _():
            scatter.wait()

    row = lambda n_: pl.BlockSpec((tm, n_), lambda i: (i, 0))
    return pl.pallas_call(
        body, name="inproj_bwd", grid=(nsteps,),
        in_specs=[row(ATTN_W), row(ATTN_W), row(ATTN_W), row(F_PAD), row(3 * CONV_CH), row(d), row(d),
                  _const_spec((1, d)), _const_spec(win_p.shape), HBM_SPEC],
        out_specs=[row(d), pl.BlockSpec((8, d), lambda i: (0, 0)), HBM_SPEC],
        out_shape=[jax.ShapeDtypeStruct((t, d), F32), jax.ShapeDtypeStruct((8, d), F32),
                   jax.ShapeDtypeStruct(gblock.shape, gblock.dtype)],
        scratch_shapes=_Exchange.scratch(1),
        compiler_params=_cp(("arbitrary",)),
    )(dq, dk, dv, df, dbcx, dh, x, g, win_p, gblock)


LATE = ("w_out_0", "w_up_0", "w_down_0", "pool_w_1", "w_up_1", "w_down_1")


def _local_step(x, target, gains, b_f, conv_w, pool_scale, win_p, shards):
    d = x.shape[1]
    n0, qkv, flog, bcx = _norm_inproj(x, gains["mix0"], win_p)
    q_aug_t, k_aug, v_aug, v_aug_t, sgate = _fgate_prep(flog, b_f, qkv)
    att, q_bwd, q_bwd_t, gathered = _attn_fwd(q_aug_t, k_aug, v_aug_t, [shards[n] for n in LATE])
    g = dict(zip(LATE, gathered))
    wout = g["w_out_0"].reshape(d, d)
    wup0, wup1 = g["w_up_0"], g["w_up_1"]
    wdown0, wdown1 = g["w_down_0"].reshape(-1, d), g["w_down_1"].reshape(-1, d)
    n_grp = len(POOL_WINDOWS)
    cg = d // n_grp
    poolw = g["pool_w_1"].reshape(N_DEV, n_grp, cg // N_DEV, cg).transpose(1, 0, 2, 3).reshape(n_grp, cg, cg)
    cv = _conv_fwd(bcx, conv_w)
    h1 = _outproj(att, cv, x, wout)
    h2, n1, a0, z0 = _mlp_fwd(h1, gains["ffn0"], wup0, wdown0, name="mlp_fwd0")
    h3, pooled = _pool_fwd(h2, gains["mix1"], poolw, pool_scale)
    loss, dh4, dg_final, n3, a1, z1 = _mlp_fwd_loss(h3, gains["ffn1"], wup1, wdown1, gains["final"], target,
                                                    name="mlp_fwd1")
    f = a1.shape[1]
    fb = f // N_DEV
    dh3, da1, dg_ffn1 = _mlp_bwd(dh4, h3, a1, gains["ffn1"], wup1, wdown1, name="mlp_bwd1")
    dwdown1 = _mm_tn(z1, dh4, name="dwdown1", ta=1024, tb=1024, tt=2048, out_dtype=BF16)
    dwup1 = _mm_tn(n3, da1, name="dwup1", ta=d, tb=fb, tt=4096, blocked_out=True, out_dtype=BF16)
    dh2, dpoolw, dscale, dg_mix1 = _pool_bwd(dh3, h2, pooled, gains["mix1"], poolw, pool_scale)
    dh1, da0, dg_ffn0 = _mlp_bwd(dh2, h1, a0, gains["ffn0"], wup0, wdown0, name="mlp_bwd0")
    dwdown0 = _mm_tn(z0, dh2, name="dwdown0", ta=1024, tb=1024, tt=2048, out_dtype=BF16)
    dwup0 = _mm_tn(n1, da0, name="dwup0", ta=d, tb=fb, tt=4096, blocked_out=True, out_dtype=BF16)
    do_aug, do_aug_t, dcv = _outproj_bwd(dh1, att, wout)
    dwout = jnp.concatenate([_mm_tn(att, dh1, name="dwout_att", ta=512, tb=1024, tt=2048, out_dtype=BF16),
                             _mm_tn(cv, dh1, name="dwout_conv", ta=512, tb=1024, tt=2048, out_dtype=BF16)], axis=0)
    dbcx, dconvw = _conv_bwd(bcx, dcv, conv_w)
    gblocks = {
        "w_out_0": dwout.reshape(N_DEV, d // N_DEV, d), "w_up_0": dwup0, "w_up_1": dwup1,
        "w_down_0": dwdown0.reshape(N_DEV, -1, d), "w_down_1": dwdown1.reshape(N_DEV, -1, d),
        "pool_w_1": dpoolw.astype(BF16).reshape(n_grp, N_DEV, cg // N_DEV, cg).transpose(1, 0, 2, 3).reshape(
            N_DEV, n_grp * (cg // N_DEV), cg),
    }
    (dq, dqx, dk, dkx, dv), landed = _attn_bwd(q_bwd, do_aug, q_bwd_t, do_aug_t, k_aug, v_aug,
                                               [gblocks[n] for n in LATE])
    df, dbf = _fgate_bwd(dqx, dkx, sgate)
    dwin = jnp.concatenate(
        [_mm_tn(n0, dq, name="dwin_q", ta=d, tb=512, tt=4096, out_dtype=BF16),
         _mm_tn(n0, dk, name="dwin_k", ta=d, tb=512, tt=4096, out_dtype=BF16),
         _mm_tn(n0, dv, name="dwin_v", ta=d, tb=512, tt=4096, out_dtype=BF16),
         _mm_tn(n0, df, name="dwin_f", ta=d, tb=128, tt=2048, out_dtype=BF16)[:, :N_HEADS],
         _mm_tn(n0, dbcx, name="dwin_bcx", ta=d, tb=512, tt=4096, out_dtype=BF16)], axis=1)
    dwin_blocks = dwin.reshape(d, N_DEV, dwin.shape[1] // N_DEV).transpose(1, 0, 2)
    grad_x, dg_mix0, landed_win = _inproj_bwd(dq, dk, dv, df, dbcx, dh1, x, gains["mix0"], win_p, dwin_blocks)
    small = dict(mix0=dg_mix0, ffn0=dg_ffn0, mix1=dg_mix1, pool_scale=dscale, ffn1=dg_ffn1, final=dg_final,
                 b_f=dbf, conv_w=dconvw)
    return loss, grad_x, dict(zip(LATE + ("w_in_0",), tuple(landed) + (landed_win,))), small


def _mesh_places():
    x, y, c = lax.axis_index("x"), lax.axis_index("y"), lax.axis_index("c")
    chips = [(1 - x, y), (x, 1 - y), (1 - x, 1 - y)]
    return (x, y, c), (x, y, 1 - c), chips


def _all_gather(shards):
    n = len(shards)

    def body(*refs):
        ins, outs = refs[:n], refs[n:2 * n]
        send_sems, recv_sems, local_sems = refs[2 * n:]
        me, sib, chips = _mesh_places()
        c = me[2]

        def copy(ai, k, block, to, src=None):
            dst = outs[ai].at[_slot(*block)]
            return pltpu.make_async_remote_copy(
                src_ref=dst if src is None else src, dst_ref=dst, send_sem=send_sems.at[7 * ai + k],
                recv_sem=recv_sems.at[7 * ai + k], device_id=to, device_id_type=MESH)

        mine = [pltpu.make_async_copy(ins[ai], outs[ai].at[_slot(*me)], local_sems.at[ai]) for ai in range(n)]
        for cp in mine:
            cp.start()
        first = []
        for ai in range(n):
            first.append(copy(ai, 0, me, sib, src=ins[ai]))
            first += [copy(ai, 1 + j, me, (*chip, c), src=ins[ai]) for j, chip in enumerate(chips)]
        for cp in first:
            cp.start()
        passed = []
        for ai in range(n):
            for j, chip in enumerate(chips):
                copy(ai, 1 + j, (*chip, c), me).wait_recv()
                cp = copy(ai, 4 + j, (*chip, c), sib)
                cp.start()
                passed.append(cp)
        for ai in range(n):
            copy(ai, 0, sib, me).wait_recv()
            for j, chip in enumerate(chips):
                copy(ai, 4 + j, (*chip, 1 - c), me).wait_recv()
        for cp in first + passed:
            cp.wait_send()
        for cp in mine:
            cp.wait()

    return pl.pallas_call(
        body, name="all_gather",
        in_specs=[HBM_SPEC] * n, out_specs=[HBM_SPEC] * n,
        out_shape=[jax.ShapeDtypeStruct((N_DEV,) + s.shape, s.dtype) for s in shards],
        scratch_shapes=[pltpu.SemaphoreType.DMA((7 * n,)), pltpu.SemaphoreType.DMA((7 * n,)),
                        pltpu.SemaphoreType.DMA((n,))],
    )(*shards)


SMALL_ROWS = 16


def _small_allreduce(parts):
    n, _, w = parts.shape
    assert n <= SMALL_ROWS

    def body(p_ref, o_ref, gath, send_sems, recv_sems):
        x, y, c = lax.axis_index("x"), lax.axis_index("y"), lax.axis_index("c")
        my = _slot(x, y, c)
        rows = [jnp.sum(p_ref[i], axis=0, keepdims=True) for i in range(n)]
        rows.append(jnp.zeros((SMALL_ROWS - n, w), F32))
        gath[my] = jnp.concatenate(rows, axis=0)
        copies = []
        for k in range(1, N_DEV):
            px, py, pc = x ^ (k >> 2), y ^ ((k >> 1) & 1), c ^ (k & 1)
            cp = pltpu.make_async_remote_copy(
                src_ref=gath.at[my], dst_ref=gath.at[my], send_sem=send_sems.at[k - 1], recv_sem=recv_sems.at[k - 1],
                device_id=(px, py, pc), device_id_type=MESH)
            cp.start()
            copies.append(cp)
        for cp in copies:
            cp.wait()
        acc = gath[0]
        for d in range(1, N_DEV):
            acc = acc + gath[d]
        o_ref[...] = acc

    return pl.pallas_call(
        body, name="small_allreduce",
        in_specs=[VMEM_SPEC], out_specs=VMEM_SPEC,
        out_shape=jax.ShapeDtypeStruct((SMALL_ROWS, w), F32),
        scratch_shapes=[pltpu.VMEM((N_DEV, SMALL_ROWS, w), F32), pltpu.SemaphoreType.DMA((N_DEV - 1,)),
                        pltpu.SemaphoreType.DMA((N_DEV - 1,))],
    )(parts)


def _adamw(g, w, m, v, *, name, tm=256):
    r, c = g.shape
    tm = tm if r % tm == 0 else r
    bc1 = 1.0 - ADAM_B1 ** ADAM_STEP
    bc2 = 1.0 - ADAM_B2 ** ADAM_STEP

    def body(g_ref, w_ref, m_ref, v_ref, d_ref, nm_ref, nv_ref):
        gv = g_ref[...]
        nm = ADAM_B1 * m_ref[...] + (1.0 - ADAM_B1) * gv
        nv = ADAM_B2 * v_ref[...] + (1.0 - ADAM_B2) * jnp.square(gv)
        nm_ref[...] = nm
        nv_ref[...] = nv
        d_ref[...] = -ADAM_LR * ((nm / bc1) / (jnp.sqrt(nv / bc2) + ADAM_EPS) + ADAM_WD * w_ref[...])

    blk = pl.BlockSpec((tm, c), lambda i: (i, 0))
    shp = jax.ShapeDtypeStruct((r, c), F32)
    return pl.pallas_call(
        body, name=name, grid=(r // tm,), in_specs=[blk] * 4, out_specs=[blk] * 3, out_shape=[shp] * 3,
        compiler_params=_cp(("parallel",)),
    )(g, w, m, v)


def _adamw_sum(parts, w, m, v, *, name, tm=128):
    _, r, c = parts.shape
    tm = tm if r % tm == 0 else r
    bc1 = 1.0 - ADAM_B1 ** ADAM_STEP
    bc2 = 1.0 - ADAM_B2 ** ADAM_STEP

    def body(p_ref, w_ref, m_ref, v_ref, g_ref, d_ref, nm_ref, nv_ref):
        gv = p_ref[0].astype(F32)
        for k in range(1, N_DEV):
            gv = gv + p_ref[k].astype(F32)
        g_ref[...] = gv
        nm = ADAM_B1 * m_ref[...] + (1.0 - ADAM_B1) * gv
        nv = ADAM_B2 * v_ref[...] + (1.0 - ADAM_B2) * jnp.square(gv)
        nm_ref[...] = nm
        nv_ref[...] = nv
        d_ref[...] = -ADAM_LR * ((nm / bc1) / (jnp.sqrt(nv / bc2) + ADAM_EPS) + ADAM_WD * w_ref[...])

    blk = pl.BlockSpec((tm, c), lambda i: (i, 0))
    shp = jax.ShapeDtypeStruct((r, c), F32)
    return pl.pallas_call(
        body, name=name, grid=(r // tm,), in_specs=[pl.BlockSpec((N_DEV, tm, c), lambda i: (0, i, 0))] + [blk] * 3,
        out_specs=[blk] * 4, out_shape=[shp] * 4, compiler_params=_cp(("parallel",)),
    )(parts, w, m, v)


BIG = ("w_in_0", "w_out_0", "w_up_0", "w_down_0", "pool_w_1", "w_up_1", "w_down_1")
SMALL = ("norm_mix_0", "norm_ffn_0", "norm_mix_1", "pool_scale_1", "norm_ffn_1", "final_norm", "b_f_0", "conv_w_0")
WEIGHTS = ("norm_mix_0", "w_in_0", "b_f_0", "conv_w_0", "w_out_0", "norm_ffn_0", "w_up_0", "w_down_0", "norm_mix_1",
           "pool_w_1", "pool_scale_1", "norm_ffn_1", "w_up_1", "w_down_1", "final_norm")


def _pad_to(a, rows, cols):
    return jnp.pad(a, ((0, rows - a.shape[0]), (0, cols - a.shape[1])))


def _pack_small(p, width):
    rows = [p[n].reshape(1, -1) for n in SMALL[:6]]
    rows.append(_pad_to(p["b_f_0"].reshape(1, -1), 1, width))
    rows.append(_pad_to(p["conv_w_0"], 3, width))
    return _pad_to(jnp.concatenate(rows, axis=0), SMALL_ROWS, width)


def _unpack_small(a, like):
    out = {n: a[i] for i, n in enumerate(SMALL[:6])}
    out["b_f_0"] = a[6, :like["b_f_0"].shape[0]]
    out["conv_w_0"] = a[7:10, :like["conv_w_0"].shape[1]]
    return out


def kernel(x, norm_mix_0, w_in_0, b_f_0, conv_w_0, w_out_0, norm_ffn_0, w_up_0, w_down_0, norm_mix_1, pool_w_1, pool_scale_1, norm_ffn_1, w_up_1, w_down_1, final_norm, loss_target, m_norm_mix_0, m_w_in_0, m_b_f_0, m_conv_w_0, m_w_out_0, m_norm_ffn_0, m_w_up_0, m_w_down_0, m_norm_mix_1, m_pool_w_1, m_pool_scale_1, m_norm_ffn_1, m_w_up_1, m_w_down_1, m_final_norm, v_norm_mix_0, v_w_in_0, v_b_f_0, v_conv_w_0, v_w_out_0, v_norm_ffn_0, v_w_up_0, v_w_down_0, v_norm_mix_1, v_pool_w_1, v_pool_scale_1, v_norm_ffn_1, v_w_up_1, v_w_down_1, v_final_norm):
    w = dict(norm_mix_0=norm_mix_0, w_in_0=w_in_0, b_f_0=b_f_0, conv_w_0=conv_w_0, w_out_0=w_out_0,
             norm_ffn_0=norm_ffn_0, w_up_0=w_up_0, w_down_0=w_down_0, norm_mix_1=norm_mix_1, pool_w_1=pool_w_1,
             pool_scale_1=pool_scale_1, norm_ffn_1=norm_ffn_1, w_up_1=w_up_1, w_down_1=w_down_1, final_norm=final_norm)
    m = dict(norm_mix_0=m_norm_mix_0, w_in_0=m_w_in_0, b_f_0=m_b_f_0, conv_w_0=m_conv_w_0, w_out_0=m_w_out_0,
             norm_ffn_0=m_norm_ffn_0, w_up_0=m_w_up_0, w_down_0=m_w_down_0, norm_mix_1=m_norm_mix_1,
             pool_w_1=m_pool_w_1, pool_scale_1=m_pool_scale_1, norm_ffn_1=m_norm_ffn_1, w_up_1=m_w_up_1,
             w_down_1=m_w_down_1, final_norm=m_final_norm)
    v = dict(norm_mix_0=v_norm_mix_0, w_in_0=v_w_in_0, b_f_0=v_b_f_0, conv_w_0=v_conv_w_0, w_out_0=v_w_out_0,
             norm_ffn_0=v_norm_ffn_0, w_up_0=v_w_up_0, w_down_0=v_w_down_0, norm_mix_1=v_norm_mix_1,
             pool_w_1=v_pool_w_1, pool_scale_1=v_pool_scale_1, norm_ffn_1=v_norm_ffn_1, w_up_1=v_w_up_1,
             w_down_1=v_w_down_1, final_norm=v_final_norm)
    d = x.shape[-1]
    n_in = w_in_0.shape[1] * N_DEV
    n_qkv = 3 * ATTN_W
    pool_g, pool_rows, pool_c = pool_w_1.shape

    def shard2d(p):
        return {n: (p[n].reshape(pool_g * pool_rows, pool_c) if n == "pool_w_1" else p[n]) for n in BIG}
    w2, m2, v2 = shard2d(w), shard2d(m), shard2d(v)

    conv_cols = conv_w_0.shape[1]
    win_g8, conv_g8 = _all_gather([w_in_0.astype(BF16), _pad_to(conv_w_0, 8, 128)])
    conv_full = conv_g8[:, :, :conv_cols].transpose(1, 0, 2).reshape(8, N_DEV * conv_cols)
    win = win_g8.transpose(1, 0, 2).reshape(d, n_in)
    win_p = jnp.concatenate([win[:, :n_qkv], _pad_to(win[:, n_qkv:n_qkv + N_HEADS], d, F_PAD),
                             win[:, n_qkv + N_HEADS:]], axis=1)

    gains = dict(mix0=norm_mix_0.reshape(1, d), ffn0=norm_ffn_0.reshape(1, d), mix1=norm_mix_1.reshape(1, d),
                 ffn1=norm_ffn_1.reshape(1, d), final=final_norm.reshape(1, d))
    dev = _slot(lax.axis_index("x"), lax.axis_index("y"), lax.axis_index("c"))
    loss8, grad_x, landed, small = _local_step(
        x[0], loss_target[0], gains, _pad_to(b_f_0.reshape(1, -1), 1, F_PAD), conv_full, pool_scale_1.reshape(1, d),
        win_p, {n: w2[n].astype(BF16) for n in LATE})
    loss = lax.psum(loss8[0, 0], ("x", "y", "c"))

    parts = jnp.concatenate(
        [small[k][None] for k in ("mix0", "ffn0", "mix1", "pool_scale", "ffn1", "final")]
        + [_pad_to(small["b_f"], 8, d)[None], jnp.pad(small["conv_w"], ((0, 0), (0, 0), (0, d - CONV_CH)))], axis=0)
    tot = _small_allreduce(parts)
    conv_g = lax.dynamic_slice(tot, (7, dev * conv_cols), (3, conv_cols))
    gs = tot.at[7:10].set(_pad_to(conv_g, 3, d))

    grads, deltas, new_m, new_v = {}, {}, {}, {}
    for n in BIG:
        gr, dl, nm, nv = _adamw_sum(landed[n], w2[n], m2[n], v2[n], name="adamw_" + n)
        for dst, val in ((grads, gr), (deltas, dl), (new_m, nm), (new_v, nv)):
            dst[n] = val.reshape(w[n].shape)
    dl, nm, nv = _adamw(gs, _pack_small(w, d), _pack_small(m, d), _pack_small(v, d), name="adamw_small")
    for dst, val in ((grads, gs), (deltas, dl), (new_m, nm), (new_v, nv)):
        dst.update(_unpack_small(val, w))
    return (loss, grad_x[None], *[grads[n] for n in WEIGHTS], *[deltas[n] for n in WEIGHTS],
            *[new_m[n] for n in WEIGHTS], *[new_v[n] for n in WEIGHTS])
```

```python
import functools

import jax
import jax.numpy as jnp
from jax import lax
from jax.experimental import pallas as pl
from jax.experimental.pallas import tpu as pltpu

F32 = jnp.float32
BF16 = jnp.bfloat16

N_DEV = 8
N_HEADS = 8
HEAD_DIM = 64
PAIR = 2 * HEAD_DIM
ATTN_W = N_HEADS * HEAD_DIM
CONV_CH = 512
F_PAD = 128
POOL_WINDOWS = (2, 4, 8, 16)
POOL_HALO = 16
CONV_HALO = 16
RMS_EPS = 1e-6
Q_SCALE = HEAD_DIM ** -0.5
LOG2E = 1.4426950408889634
NEG = -1e30
AUX_BIAS = 0
AUX_LSE = 3
AUX_ROWSUM = 6
ADAM_LR, ADAM_B1, ADAM_B2, ADAM_EPS, ADAM_WD, ADAM_STEP = 0.001, 0.9, 0.999, 1e-08, 0.01, 10
MESH = pl.DeviceIdType.MESH
VMEM_LIMIT = 56 * 2**20


def _cp(sem=None, vmem=VMEM_LIMIT, **kw):
    return pltpu.CompilerParams(dimension_semantics=sem, vmem_limit_bytes=vmem, **kw)


def _dot(a, b):
    return jnp.dot(a, b, preferred_element_type=F32)


def _dot_nt(a, b):
    return lax.dot_general(a, b, (((1,), (1,)), ((), ())), preferred_element_type=F32)


def _dot_tn(a, b):
    return lax.dot_general(a, b, (((0,), (0,)), ((), ())), preferred_element_type=F32)


def _rstd(h):
    return lax.rsqrt(jnp.mean(h * h, axis=-1, keepdims=True) + RMS_EPS)


def _rows8(x):
    r, n = x.shape
    return jnp.sum(x.reshape(r // 8, 8, n), axis=0)


def _norm_bwd(dn, h, g):
    r = _rstd(h)
    xhat = h * r
    dy = dn * g
    dh = r * (dy - xhat * jnp.mean(dy * xhat, axis=-1, keepdims=True))
    return dh, _rows8(dn * xhat)


def _const_spec(shape):
    nd = len(shape)
    return pl.BlockSpec(shape, lambda *_: (0,) * nd, pipeline_mode=pl.Buffered(1))


HBM_SPEC = pl.BlockSpec(memory_space=pltpu.HBM)
VMEM_SPEC = pl.BlockSpec(memory_space=pltpu.VMEM)


def _slot(px, py, pc):
    return 4 * px + 2 * py + pc


class _Exchange:
    def __init__(self, srcs, dsts, send_sems, recv_sems, local_sems, gather):
        x, y, c = lax.axis_index("x"), lax.axis_index("y"), lax.axis_index("c")
        me = _slot(x, y, c)
        self.copies = []
        for a, (src, dst) in enumerate(zip(srcs, dsts)):
            self.copies.append(pltpu.make_async_copy(src if gather else src.at[me], dst.at[me], local_sems.at[a]))
            for k in range(1, N_DEV):
                px, py, pc = x ^ (k >> 2), y ^ ((k >> 1) & 1), c ^ (k & 1)
                self.copies.append(pltpu.make_async_remote_copy(
                    src_ref=src if gather else src.at[_slot(px, py, pc)], dst_ref=dst.at[me],
                    send_sem=send_sems.at[(N_DEV - 1) * a + k - 1], recv_sem=recv_sems.at[(N_DEV - 1) * a + k - 1],
                    device_id=(px, py, pc), device_id_type=MESH))

    def start(self):
        for cp in self.copies:
            cp.start()

    def wait(self):
        for cp in self.copies:
            cp.wait()

    @staticmethod
    def scratch(n):
        return [pltpu.SemaphoreType.DMA(((N_DEV - 1) * n,)), pltpu.SemaphoreType.DMA(((N_DEV - 1) * n,)),
                pltpu.SemaphoreType.DMA((n,))]


def _norm_inproj(x, g, win_p, conv_w, *, tm=512):
    t, d = x.shape
    n_all = win_p.shape[1]
    n_qkv = 3 * ATTN_W
    n_bcx = 3 * CONV_CH
    assert n_all == n_qkv + F_PAD + n_bcx
    tm = min(tm, t)
    ch = CONV_CH

    def body(x_ref, g_ref, w_ref, cw_ref, n_ref, qkv_ref, f_ref, bcx_ref, cv_ref, ext):
        h = x_ref[...]
        n = (h * _rstd(h) * g_ref[...]).astype(BF16)
        n_ref[...] = n
        for c0 in range(0, n_qkv, 512):
            acc = _dot(n, w_ref[:, c0:c0 + 512])
            if c0 < ATTN_W:
                acc = acc * (Q_SCALE * LOG2E)
            qkv_ref[:, c0:c0 + 512] = acc.astype(BF16)
        f_ref[...] = _dot(n, w_ref[:, n_qkv:n_qkv + F_PAD])
        bcx = []
        for k in range(3):
            c0 = n_qkv + F_PAD + k * ch
            v = _dot(n, w_ref[:, c0:c0 + ch]).astype(BF16)
            bcx_ref[:, k * ch:(k + 1) * ch] = v
            bcx.append(v.astype(F32))
        @pl.when(pl.program_id(0) == 0)
        def _():
            ext[tm:tm + CONV_HALO, :] = jnp.zeros((CONV_HALO, ch), F32)
        ext[0:CONV_HALO, :] = ext[tm:tm + CONV_HALO, :]
        ext[CONV_HALO:CONV_HALO + tm, :] = bcx[1] * bcx[2]
        conv = (cw_ref[0:1, :] * ext[CONV_HALO - 2:CONV_HALO - 2 + tm, :]
                + cw_ref[1:2, :] * ext[CONV_HALO - 1:CONV_HALO - 1 + tm, :]
                + cw_ref[2:3, :] * ext[CONV_HALO:CONV_HALO + tm, :])
        cv_ref[...] = (bcx[0] * conv).astype(BF16)

    return pl.pallas_call(
        body, name="norm_inproj", grid=(t // tm,),
        in_specs=[pl.BlockSpec((tm, d), lambda i: (i, 0)), _const_spec((1, d)), _const_spec((d, n_all)),
                  _const_spec((8, ch))],
        out_specs=[pl.BlockSpec((tm, d), lambda i: (i, 0)), pl.BlockSpec((tm, n_qkv), lambda i: (i, 0)),
                   pl.BlockSpec((tm, F_PAD), lambda i: (i, 0)), pl.BlockSpec((tm, n_bcx), lambda i: (i, 0)),
                   pl.BlockSpec((tm, ch), lambda i: (i, 0))],
        out_shape=[jax.ShapeDtypeStruct((t, d), BF16), jax.ShapeDtypeStruct((t, n_qkv), BF16),
                   jax.ShapeDtypeStruct((t, F_PAD), F32), jax.ShapeDtypeStruct((t, n_bcx), BF16),
                   jax.ShapeDtypeStruct((t, ch), BF16)],
        scratch_shapes=[pltpu.VMEM((CONV_HALO + tm, ch), F32)],
        compiler_params=_cp(("arbitrary",)),
    )(x, g, win_p, conv_w)


def _head_lanes(h):
    lane = lax.broadcasted_iota(jnp.int32, (1, PAIR), 1)
    hh = h % 2
    return lane, lane // HEAD_DIM == hh, HEAD_DIM * (1 - hh)


def _pieces(col):
    hi = col.astype(BF16).astype(F32)
    r1 = col - hi
    mid = r1.astype(BF16).astype(F32)
    lo = (r1 - mid).astype(BF16).astype(F32)
    return hi, mid, lo


def _put_pieces(lane, first, col, other):
    hi, mid, lo = _pieces(col)
    return jnp.where(lane == first, hi, jnp.where(lane == first + 1, mid, jnp.where(lane == first + 2, lo, other)))


def _fgate_prep(flog, b_f, qkv, *, tm=512):
    t = flog.shape[0]
    tm = min(tm, t)

    def body(f_ref, b_ref, qkv_ref, qat_ref, ka_ref, va_ref, vat_ref, sg_ref, carry):
        @pl.when(pl.program_id(0) == 0)
        def _():
            carry[...] = jnp.zeros_like(carry)
        z = f_ref[...] + b_ref[...]
        e = jnp.exp(-jnp.abs(z))
        logf = jnp.minimum(z, 0.0) - jnp.log(1.0 + e)
        sg_ref[...] = jnp.where(z >= 0, e, 1.0) / (1.0 + e)
        r = lax.broadcasted_iota(jnp.int32, (tm, tm), 0)
        c = lax.broadcasted_iota(jnp.int32, (tm, tm), 1)
        tri = (c <= r).astype(F32)
        cs = jnp.dot(tri, logf, preferred_element_type=F32, precision=lax.Precision.HIGHEST) + carry[...]
        carry[...] = cs[tm - 1:tm, :]
        cs2 = cs * LOG2E
        for h in range(N_HEADS):
            lane, head, aux = _head_lanes(h)
            p0 = (h // 2) * PAIR
            ones = ((lane >= aux + AUX_LSE) & (lane <= aux + AUX_ROWSUM)).astype(F32)
            bias = (lane >= aux + AUX_BIAS) & (lane < aux + AUX_BIAS + 3)
            k_aux = _put_pieces(lane, aux + AUX_BIAS, cs2[:, h:h + 1], ones)
            q_aug = jnp.where(head, qkv_ref[:, p0:p0 + PAIR].astype(F32), jnp.where(bias, -1.0, 0.0))
            v_aug = jnp.where(head, qkv_ref[:, 2 * ATTN_W + p0:2 * ATTN_W + p0 + PAIR].astype(F32),
                              jnp.where(bias, 1.0, 0.0))
            qat_ref[h] = q_aug.T.astype(BF16)
            ka_ref[h] = jnp.where(head, qkv_ref[:, ATTN_W + p0:ATTN_W + p0 + PAIR], k_aux.astype(BF16))
            va_ref[h] = v_aug.astype(BF16)
            vat_ref[h] = v_aug.T.astype(BF16)

    aug = lambda: pl.BlockSpec((N_HEADS, tm, PAIR), lambda i: (0, i, 0))
    aug_t = lambda: pl.BlockSpec((N_HEADS, PAIR, tm), lambda i: (0, 0, i))
    aug_shape = jax.ShapeDtypeStruct((N_HEADS, t, PAIR), BF16)
    aug_t_shape = jax.ShapeDtypeStruct((N_HEADS, PAIR, t), BF16)
    return pl.pallas_call(
        body, name="fgate_prep", grid=(t // tm,),
        in_specs=[pl.BlockSpec((tm, F_PAD), lambda i: (i, 0)), _const_spec((1, F_PAD)),
                  pl.BlockSpec((tm, 3 * ATTN_W), lambda i: (i, 0))],
        out_specs=[aug_t(), aug(), aug(), aug_t(), pl.BlockSpec((tm, F_PAD), lambda i: (i, 0))],
        out_shape=[aug_t_shape, aug_shape, aug_shape, aug_t_shape, jax.ShapeDtypeStruct((t, F_PAD), F32)],
        scratch_shapes=[pltpu.VMEM((1, F_PAD), F32)],
        compiler_params=_cp(("arbitrary",)),
    )(flog, b_f, qkv)


def _put_pieces_t(row, first, vec, other):
    hi, mid, lo = _pieces(vec)
    return jnp.where(row == first, hi, jnp.where(row == first + 1, mid, jnp.where(row == first + 2, lo, other)))


def _attn_fwd(q_aug_t, k_aug, v_aug_t, shards, *, tq=1024):
    t = k_aug.shape[1]
    tq = min(tq, t)
    tk = tq // 2
    nq = t // tq
    n_pairs = ATTN_W // PAIR
    n_sh = len(shards)

    def body(qt_ref, k_ref, vt_ref, *rest):
        o_ref, qb_ref, qbt_ref = rest[n_sh:n_sh + 3]
        s_scr = rest[2 * n_sh + 3]
        gather = _Exchange(rest[:n_sh], rest[n_sh + 3:2 * n_sh + 3], *rest[2 * n_sh + 4:], gather=True)
        i = pl.program_id(1)

        @pl.when((pl.program_id(0) == 0) & (i == 0))
        def _():
            gather.start()
        key = lax.broadcasted_iota(jnp.int32, (tk, tq), 0)
        qry = lax.broadcasted_iota(jnp.int32, (tk, tq), 1)
        qt = [qt_ref[0], qt_ref[1]]

        def logits(hh, tile, slot, diag):
            s = _dot(k_ref[hh, pl.ds(pl.multiple_of(tile * tk, tk), tk), :], qt[hh])
            if diag:
                s = jnp.where(key + (tile * tk - i * tq) <= qry, s, NEG)
            s_scr[hh, slot] = s
            return jnp.max(s, axis=0, keepdims=True)

        def probs(hh, tile, slot, m, acc, tmax):
            mn = jnp.maximum(m, tmax)
            p = jnp.exp2(s_scr[hh, slot] - mn).astype(BF16)
            acc = jnp.exp2(m - mn) * acc + _dot(vt_ref[hh, :, pl.ds(pl.multiple_of(tile * tk, tk), tk)], p)
            return mn, acc

        def advance(carry, prev, slot, nxt, diag=False):
            out = []
            for hh in range(2):
                m, acc, tmax = carry[hh]
                m, acc = probs(hh, prev, slot, m, acc, tmax)
                out.append((m, acc, logits(hh, nxt, 1 - slot, diag)))
            return tuple(out)

        def two_tiles(jj, carry):
            carry = advance(carry, jnp.where(jj == 0, 2 * i, 2 * jj - 1), 1, 2 * jj)
            return advance(carry, 2 * jj, 0, 2 * jj + 1)

        init = tuple((jnp.full((1, tq), NEG, F32), jnp.zeros((PAIR, tq), F32), logits(hh, 2 * i + 1, 0, True))
                     for hh in range(2))
        carry = advance(init, 2 * i + 1, 0, 2 * i, diag=True)
        carry = lax.fori_loop(0, i, two_tiles, carry)
        last = jnp.where(i == 0, 2 * i, 2 * i - 1)
        row = lax.broadcasted_iota(jnp.int32, (PAIR, 1), 0)
        res = []
        for hh in range(2):
            aux = HEAD_DIM * (1 - hh)
            m, acc, tmax = carry[hh]
            m, acc = probs(hh, last, 1, m, acc, tmax)
            l = acc[aux + AUX_BIAS:aux + AUX_BIAS + 1, :]
            qbt = _put_pieces_t(row, aux + AUX_LSE, -(m + jnp.log2(l)), qt[hh].astype(F32))
            qbt_ref[hh] = qbt.astype(BF16)
            qb_ref[hh] = qbt.T.astype(BF16)
            res.append(acc * (1.0 / l))
        o_ref[...] = jnp.where(row < HEAD_DIM, res[0], res[1]).T.astype(BF16)

        @pl.when((pl.program_id(0) == n_pairs - 1) & (i == nq - 1))
        def _():
            gather.wait()

    res = pl.pallas_call(
        body, name="attn_fwd", grid=(n_pairs, nq),
        in_specs=[pl.BlockSpec((2, PAIR, tq), lambda p, i: (p, 0, i)),
                  pl.BlockSpec((2, t, PAIR), lambda p, i: (p, 0, 0), pipeline_mode=pl.Buffered(1)),
                  pl.BlockSpec((2, PAIR, t), lambda p, i: (p, 0, 0), pipeline_mode=pl.Buffered(1))] + [HBM_SPEC] * n_sh,
        out_specs=[pl.BlockSpec((tq, PAIR), lambda p, i: (i, p)),
                   pl.BlockSpec((2, tq, PAIR), lambda p, i: (p, i, 0)),
                   pl.BlockSpec((2, PAIR, tq), lambda p, i: (p, 0, i))] + [HBM_SPEC] * n_sh,
        out_shape=[jax.ShapeDtypeStruct((t, ATTN_W), BF16), jax.ShapeDtypeStruct((N_HEADS, t, PAIR), BF16),
                   jax.ShapeDtypeStruct((N_HEADS, PAIR, t), BF16)]
        + [jax.ShapeDtypeStruct((N_DEV,) + s.shape, s.dtype) for s in shards],
        scratch_shapes=[pltpu.VMEM((2, 2, tk, tq), F32)] + _Exchange.scratch(n_sh),
        compiler_params=_cp(("arbitrary", "arbitrary")),
    )(q_aug_t, k_aug, v_aug_t, *shards)
    return res[0], res[1], res[2], res[3:]


def _prev_halo(tm, halo):
    return lambda i: (jnp.maximum(i * (tm // halo) - 1, 0), 0)


def _next_halo(tm, halo, t):
    return lambda i: (jnp.minimum((i + 1) * (tm // halo), t // halo - 1), 0)


def _mlp_tile(hh, g_ref, wu_ref, wd_ref, n_ref, a_ref, z_ref):
    n_blk, _, fb = wu_ref.shape
    n = (hh * _rstd(hh) * g_ref[...]).astype(BF16)
    n_ref[...] = n
    acc = hh
    for k in range(n_blk):
        a = _dot(n, wu_ref[k])
        zz = jnp.square(jnp.maximum(a, 0.0)).astype(BF16)
        a_ref[:, k * fb:(k + 1) * fb] = a.astype(BF16)
        z_ref[:, k * fb:(k + 1) * fb] = zz
        acc = acc + _dot(zz, wd_ref[k * fb:(k + 1) * fb, :])
    return acc


def _outproj(att, cv, x, wout, *, tm=512):
    t, d = x.shape
    tm = min(tm, t)

    def body(a_ref, c_ref, x_ref, w_ref, h_ref):
        h_ref[...] = x_ref[...] + _dot(a_ref[...], w_ref[0:ATTN_W, :]) + _dot(c_ref[...], w_ref[ATTN_W:, :])

    return pl.pallas_call(
        body, name="outproj", grid=(t // tm,),
        in_specs=[pl.BlockSpec((tm, ATTN_W), lambda i: (i, 0)), pl.BlockSpec((tm, CONV_CH), lambda i: (i, 0)),
                  pl.BlockSpec((tm, d), lambda i: (i, 0)), _const_spec(wout.shape)],
        out_specs=pl.BlockSpec((tm, d), lambda i: (i, 0)),
        out_shape=jax.ShapeDtypeStruct((t, d), F32),
        compiler_params=_cp(("parallel",)),
    )(att, cv, x, wout)


def _mlp_fwd(h, g, wup, wdown, *, name, tm=512):
    t, d = h.shape
    n_blk, _, fb = wup.shape
    f = n_blk * fb
    tm = min(tm, t)

    def body(h_ref, g_ref, wu_ref, wd_ref, ho_ref, n_ref, a_ref, z_ref):
        ho_ref[...] = _mlp_tile(h_ref[...], g_ref, wu_ref, wd_ref, n_ref, a_ref, z_ref)

    row = lambda n_: pl.BlockSpec((tm, n_), lambda i: (i, 0))
    return pl.pallas_call(
        body, name=name, grid=(t // tm,),
        in_specs=[row(d), _const_spec((1, d)), _const_spec(wup.shape), _const_spec(wdown.shape)],
        out_specs=[row(d), row(d), row(f), row(f)],
        out_shape=[jax.ShapeDtypeStruct((t, d), F32), jax.ShapeDtypeStruct((t, d), BF16),
                   jax.ShapeDtypeStruct((t, f), BF16), jax.ShapeDtypeStruct((t, f), BF16)],
        compiler_params=_cp(("parallel",)),
    )(h, g, wup, wdown)


def _mlp_fwd_loss(h, g, wup, wdown, g_out, target, *, name, tm=512):
    t, d = h.shape
    n_blk, _, fb = wup.shape
    f = n_blk * fb
    tm = min(tm, t)
    nsteps = t // tm

    def body(h_ref, g_ref, wu_ref, wd_ref, go_ref, y_ref, loss_ref, dh_ref, dg_ref, n_ref, a_ref, z_ref, lacc):
        i = pl.program_id(0)

        @pl.when(i == 0)
        def _():
            lacc[...] = jnp.zeros_like(lacc)
            dg_ref[...] = jnp.zeros_like(dg_ref)
        hv = _mlp_tile(h_ref[...], g_ref, wu_ref, wd_ref, n_ref, a_ref, z_ref)
        gv = go_ref[...]
        r = _rstd(hv)
        xhat = hv * r
        err = xhat * gv - y_ref[...]
        lacc[...] += _rows8(err * err)
        dout = err * (1.0 / d)
        dy = dout * gv
        dg_ref[...] += _rows8(dout * xhat)
        dh_ref[...] = r * (dy - xhat * jnp.mean(dy * xhat, axis=-1, keepdims=True))

        @pl.when(i == nsteps - 1)
        def _():
            loss_ref[...] = jnp.full(loss_ref.shape, (0.5 / d) * jnp.sum(lacc[...]), F32)

    row = lambda n_: pl.BlockSpec((tm, n_), lambda i: (i, 0))
    return pl.pallas_call(
        body, name=name, grid=(nsteps,),
        in_specs=[row(d), _const_spec((1, d)), _const_spec(wup.shape), _const_spec(wdown.shape), _const_spec((1, d)),
                  row(d)],
        out_specs=[pl.BlockSpec((8, 128), lambda i: (0, 0)), row(d), pl.BlockSpec((8, d), lambda i: (0, 0)),
                   row(d), row(f), row(f)],
        out_shape=[jax.ShapeDtypeStruct((8, 128), F32), jax.ShapeDtypeStruct((t, d), F32),
                   jax.ShapeDtypeStruct((8, d), F32), jax.ShapeDtypeStruct((t, d), BF16),
                   jax.ShapeDtypeStruct((t, f), BF16), jax.ShapeDtypeStruct((t, f), BF16)],
        scratch_shapes=[pltpu.VMEM((8, d), F32)],
        compiler_params=_cp(("arbitrary",)),
    )(h, g, wup, wdown, g_out, target)


def _pool_inv_count(i, tm):
    tglob = (i * tm + lax.broadcasted_iota(jnp.int32, (tm, 1), 0) + 1).astype(F32)
    return [1.0 / jnp.minimum(tglob, float(w)) for w in POOL_WINDOWS]


def _pool_fwd(h, g, poolw, scale, *, tm=256):
    t, d = h.shape
    tm = min(tm, t)
    cg = d // len(POOL_WINDOWS)

    def body(h_ref, hh_ref, g_ref, w_ref, s_ref, ho_ref, p_ref, ext):
        i = pl.program_id(0)
        hv = h_ref[...]
        halo = hh_ref[...]
        n = hv * _rstd(hv) * g_ref[...]
        ext[0:POOL_HALO, :] = jnp.where(i == 0, 0.0, halo * _rstd(halo) * g_ref[...])
        ext[POOL_HALO:POOL_HALO + tm, :] = n
        inv = _pool_inv_count(i, tm)
        for gi, w in enumerate(POOL_WINDOWS):
            cs = slice(gi * cg, (gi + 1) * cg)
            s = ext[POOL_HALO:POOL_HALO + tm, cs]
            for j in range(1, w):
                s = s + ext[POOL_HALO - j:POOL_HALO - j + tm, cs]
            pooled = (s * inv[gi] - n[:, cs]).astype(BF16)
            p_ref[:, cs] = pooled
            ho_ref[:, cs] = hv[:, cs] + _dot(pooled, w_ref[gi]) * s_ref[:, cs]

    row = lambda: pl.BlockSpec((tm, d), lambda i: (i, 0))
    return pl.pallas_call(
        body, name="pool_fwd", grid=(t // tm,),
        in_specs=[row(), pl.BlockSpec((POOL_HALO, d), _prev_halo(tm, POOL_HALO)), _const_spec((1, d)),
                  _const_spec(poolw.shape), _const_spec((1, d))],
        out_specs=[row(), row()],
        out_shape=[jax.ShapeDtypeStruct((t, d), F32), jax.ShapeDtypeStruct((t, d), BF16)],
        scratch_shapes=[pltpu.VMEM((POOL_HALO + tm, d), F32)],
        compiler_params=_cp(("parallel",)),
    )(h, h, g, poolw, scale)


def _mm_tn(a, b, *, name, ta, tb, tt, blocked_out=False, out_dtype=F32):
    t, ka = a.shape
    n = b.shape[1]
    ta, tb, tt = min(ta, ka), min(tb, n), min(tt, t)
    nt = t // tt

    def body(a_ref, b_ref, o_ref, acc):
        @pl.when(pl.program_id(2) == 0)
        def _():
            acc[...] = jnp.zeros_like(acc)
        acc[...] += _dot_tn(a_ref[...].astype(BF16), b_ref[...].astype(BF16))

        @pl.when(pl.program_id(2) == nt - 1)
        def _():
            o_ref[...] = acc[...].astype(out_dtype)

    if blocked_out:
        assert ta == ka
        out_shape = jax.ShapeDtypeStruct((n // tb, ka, tb), out_dtype)
        out_spec = pl.BlockSpec((None, ta, tb), lambda i, j, k: (j, i, 0))
    else:
        out_shape = jax.ShapeDtypeStruct((ka, n), out_dtype)
        out_spec = pl.BlockSpec((ta, tb), lambda i, j, k: (i, j))
    return pl.pallas_call(
        body, name=name, grid=(ka // ta, n // tb, nt),
        in_specs=[pl.BlockSpec((tt, ta), lambda i, j, k: (k, i)), pl.BlockSpec((tt, tb), lambda i, j, k: (k, j))],
        out_specs=out_spec, out_shape=out_shape, scratch_shapes=[pltpu.VMEM((ta, tb), F32)],
        compiler_params=_cp(("parallel", "parallel", "arbitrary")),
    )(a, b)


def _mlp_bwd(dho, h, a, g, wup, wdown, *, name, tm=512):
    t, d = h.shape
    n_blk, _, fb = wup.shape
    f = n_blk * fb
    tm = min(tm, t)

    def body(do_ref, h_ref, a_ref, g_ref, wu_ref, wd_ref, dh_ref, da_ref, dg_ref):
        @pl.when(pl.program_id(0) == 0)
        def _():
            dg_ref[...] = jnp.zeros_like(dg_ref)
        dho_v = do_ref[...]
        dob = dho_v.astype(BF16)
        dn = jnp.zeros((tm, d), F32)
        for k in range(n_blk):
            dz = _dot_nt(dob, wd_ref[k * fb:(k + 1) * fb, :])
            da = (dz * (2.0 * jnp.maximum(a_ref[:, k * fb:(k + 1) * fb].astype(F32), 0.0))).astype(BF16)
            da_ref[:, k * fb:(k + 1) * fb] = da
            dn = dn + _dot_nt(da, wu_ref[k])
        dh, dg = _norm_bwd(dn, h_ref[...], g_ref[...])
        dh_ref[...] = dho_v + dh
        dg_ref[...] += dg

    row = lambda n_: pl.BlockSpec((tm, n_), lambda i: (i, 0))
    return pl.pallas_call(
        body, name=name, grid=(t // tm,),
        in_specs=[row(d), row(d), row(f), _const_spec((1, d)), _const_spec(wup.shape), _const_spec(wdown.shape)],
        out_specs=[row(d), row(f), pl.BlockSpec((8, d), lambda i: (0, 0))],
        out_shape=[jax.ShapeDtypeStruct((t, d), F32), jax.ShapeDtypeStruct((t, f), BF16),
                   jax.ShapeDtypeStruct((8, d), F32)],
        compiler_params=_cp(("arbitrary",)),
    )(dho, h, a, g, wup, wdown)


def _pool_bwd(dho, h, pooled, g, poolw, scale, *, tm=256):
    t, d = h.shape
    tm = min(tm, t)
    ng = len(POOL_WINDOWS)
    cg = d // ng
    nsteps = t // tm

    def body(do_ref, dn_ref, h_ref, p_ref, g_ref, w_ref, s_ref, dh_ref, dw_ref, ds_ref, dg_ref, ext):
        i = pl.program_id(0)

        @pl.when(i == 0)
        def _():
            dw_ref[...] = jnp.zeros_like(dw_ref)
            ds_ref[...] = jnp.zeros_like(ds_ref)
            dg_ref[...] = jnp.zeros_like(dg_ref)
        dho_v = do_ref[...]
        sv = s_ref[...]
        dyp = (dho_v * sv).astype(BF16)
        dyp_halo = (dn_ref[...] * sv).astype(BF16)
        inv = _pool_inv_count(i, tm)
        tnext = ((i + 1) * tm + lax.broadcasted_iota(jnp.int32, (POOL_HALO, 1), 0) + 1).astype(F32)
        last = i == nsteps - 1
        ypre_parts, dpooled_parts = [], []
        for gi, w in enumerate(POOL_WINDOWS):
            cs = slice(gi * cg, (gi + 1) * cg)
            pg = p_ref[:, cs]
            ypre_parts.append(_dot(pg, w_ref[gi]))
            dw_ref[gi] += _dot_tn(pg, dyp[:, cs])
            dpool = _dot_nt(dyp[:, cs], w_ref[gi])
            dpooled_parts.append(dpool)
            ext[0:tm, cs] = dpool * inv[gi]
            dpool_halo = _dot_nt(dyp_halo[:, cs], w_ref[gi]) * (1.0 / jnp.minimum(tnext, float(w)))
            ext[tm:tm + POOL_HALO, cs] = jnp.where(last, 0.0, dpool_halo)
        ds_ref[...] += _rows8(dho_v * jnp.concatenate(ypre_parts, axis=1))
        dn_parts = []
        for gi, w in enumerate(POOL_WINDOWS):
            cs = slice(gi * cg, (gi + 1) * cg)
            s = ext[0:tm, cs]
            for j in range(1, w):
                s = s + ext[j:j + tm, cs]
            dn_parts.append(s - dpooled_parts[gi])
        dh, dg = _norm_bwd(jnp.concatenate(dn_parts, axis=1), h_ref[...], g_ref[...])
        dh_ref[...] = dho_v + dh
        dg_ref[...] += dg

    row = lambda: pl.BlockSpec((tm, d), lambda i: (i, 0))
    acc8 = lambda: pl.BlockSpec((8, d), lambda i: (0, 0))
    return pl.pallas_call(
        body, name="pool_bwd", grid=(nsteps,),
        in_specs=[row(), pl.BlockSpec((POOL_HALO, d), _next_halo(tm, POOL_HALO, t)), row(), row(),
                  _const_spec((1, d)), _const_spec(poolw.shape), _const_spec((1, d))],
        out_specs=[row(), pl.BlockSpec((ng, cg, cg), lambda i: (0, 0, 0)), acc8(), acc8()],
        out_shape=[jax.ShapeDtypeStruct((t, d), F32), jax.ShapeDtypeStruct((ng, cg, cg), F32),
                   jax.ShapeDtypeStruct((8, d), F32), jax.ShapeDtypeStruct((8, d), F32)],
        scratch_shapes=[pltpu.VMEM((tm + POOL_HALO, d), F32)],
        compiler_params=_cp(("arbitrary",)),
    )(dho, dho, h, pooled, g, poolw, scale)


def _outproj_bwd(dh, o, wout, *, tm=512):
    t, d = dh.shape
    tm = min(tm, t)

    def body(dh_ref, o_ref, w_ref, da_ref, dat_ref, dc_ref):
        dhb = dh_ref[...].astype(BF16)
        dc_ref[...] = _dot_nt(dhb, w_ref[ATTN_W:, :])
        for p in range(ATTN_W // PAIR):
            datt = _dot_nt(dhb, w_ref[p * PAIR:(p + 1) * PAIR, :])
            prod = datt * o_ref[:, p * PAIR:(p + 1) * PAIR].astype(F32)
            for hh in range(2):
                lane, head, aux = _head_lanes(hh)
                delta = jnp.sum(jnp.where(head, prod, 0.0), axis=1, keepdims=True)
                aug = _put_pieces(lane, aux + AUX_BIAS, -delta, jnp.where(head, datt, 0.0))
                da_ref[2 * p + hh] = aug.astype(BF16)
                dat_ref[2 * p + hh] = aug.T.astype(BF16)

    row = lambda n_: pl.BlockSpec((tm, n_), lambda i: (i, 0))
    return pl.pallas_call(
        body, name="outproj_bwd", grid=(t // tm,),
        in_specs=[row(d), row(ATTN_W), _const_spec(wout.shape)],
        out_specs=[pl.BlockSpec((N_HEADS, tm, PAIR), lambda i: (0, i, 0)),
                   pl.BlockSpec((N_HEADS, PAIR, tm), lambda i: (0, 0, i)), row(CONV_CH)],
        out_shape=[jax.ShapeDtypeStruct((N_HEADS, t, PAIR), BF16), jax.ShapeDtypeStruct((N_HEADS, PAIR, t), BF16),
                   jax.ShapeDtypeStruct((t, CONV_CH), F32)],
        compiler_params=_cp(("parallel",)),
    )(dh, o, wout)


def _conv_bwd(bcx, dcv, conv_w, *, tm=512):
    t = bcx.shape[0]
    tm = min(tm, t)
    ch = CONV_CH
    nsteps = t // tm

    def body(b_ref, c_ref, x_ref, hc_ref, hx_ref, d_ref, nb_ref, nd_ref, w_ref, o_ref, dw_ref, ext_u, ext_d):
        i = pl.program_id(0)

        @pl.when(i == 0)
        def _():
            dw_ref[...] = jnp.zeros_like(dw_ref)
        b, c, x, dcv_v = b_ref[...].astype(F32), c_ref[...].astype(F32), x_ref[...].astype(F32), d_ref[...]
        ext_u[0:CONV_HALO, :] = jnp.where(i == 0, 0.0, hc_ref[...].astype(F32) * hx_ref[...].astype(F32))
        ext_u[CONV_HALO:CONV_HALO + tm, :] = c * x
        dconv = dcv_v * b
        ext_d[0:tm, :] = dconv
        ext_d[tm:tm + CONV_HALO, :] = jnp.where(i == nsteps - 1, 0.0, nd_ref[...] * nb_ref[...].astype(F32))
        u = [ext_u[CONV_HALO - 2 + k:CONV_HALO - 2 + k + tm, :] for k in range(3)]
        conv = w_ref[0:1, :] * u[0] + w_ref[1:2, :] * u[1] + w_ref[2:3, :] * u[2]
        du = (w_ref[2:3, :] * dconv + w_ref[1:2, :] * ext_d[1:1 + tm, :] + w_ref[0:1, :] * ext_d[2:2 + tm, :])
        o_ref[:, 0:ch] = (dcv_v * conv).astype(BF16)
        o_ref[:, ch:2 * ch] = (du * x).astype(BF16)
        o_ref[:, 2 * ch:3 * ch] = (du * c).astype(BF16)
        for k in range(3):
            dw_ref[k] += _rows8(dconv * u[k])

    col = lambda k: pl.BlockSpec((tm, ch), lambda i: (i, k))
    prev = lambda k: pl.BlockSpec((CONV_HALO, ch), lambda i: (_prev_halo(tm, CONV_HALO)(i)[0], k))
    nxt = lambda k: pl.BlockSpec((CONV_HALO, ch), lambda i: (_next_halo(tm, CONV_HALO, t)(i)[0], k))
    return pl.pallas_call(
        body, name="conv_bwd", grid=(nsteps,),
        in_specs=[col(0), col(1), col(2), prev(1), prev(2), col(0), nxt(0), nxt(0), _const_spec((8, ch))],
        out_specs=[pl.BlockSpec((tm, 3 * ch), lambda i: (i, 0)), pl.BlockSpec((3, 8, ch), lambda i: (0, 0, 0))],
        out_shape=[jax.ShapeDtypeStruct((t, 3 * ch), BF16), jax.ShapeDtypeStruct((3, 8, ch), F32)],
        scratch_shapes=[pltpu.VMEM((CONV_HALO + tm, ch), F32), pltpu.VMEM((tm + CONV_HALO, ch), F32)],
        compiler_params=_cp(("arbitrary",)),
    )(bcx, bcx, bcx, bcx, bcx, dcv, bcx, dcv, conv_w)


def _attn_bwd(q_bwd, do_aug, q_bwd_t, do_aug_t, k_aug, v_aug, gblocks, *, tq=1024):
    t = q_bwd.shape[1]
    tq = min(tq, t)
    tk = tq // 2
    nq, nk = t // tq, t // tk
    n_pairs = ATTN_W // PAIR
    n_g = len(gblocks)

    def body(q_ref, do_ref, qt_ref, dot_ref, k_ref, v_ref, *rest):
        dq_ref, dqx_ref, dk_ref, dkx_ref, dv_ref = rest[n_g:n_g + 5]
        dq_scr = rest[2 * n_g + 5]
        scatter = _Exchange(rest[:n_g], rest[n_g + 5:2 * n_g + 5], *rest[2 * n_g + 6:], gather=False)
        j = pl.program_id(1)

        @pl.when((pl.program_id(0) == 0) & (j == 0))
        def _():
            scatter.start()

        @pl.when(j == 0)
        def _():
            dq_scr[...] = jnp.zeros_like(dq_scr)
        k = [k_ref[0], k_ref[1]]
        v = [v_ref[0], v_ref[1]]

        def step(i, carry, diag, rows=tq, row0=0):
            qs = pl.multiple_of(i * tq + row0, tk)
            if diag:
                row = lax.broadcasted_iota(jnp.int32, (rows, tk), 0)
                col = lax.broadcasted_iota(jnp.int32, (rows, tk), 1)
            out = []
            for hh in range(2):
                dk_a, dv_a = carry[hh]
                q = q_ref[hh, pl.ds(qs, rows), :]
                dov = do_ref[hh, pl.ds(qs, rows), :]
                p = jnp.exp2(_dot_nt(q, k[hh]))
                if diag:
                    p = jnp.where(col + (j * tk - i * tq - row0) <= row, p, 0.0)
                ds = (p * _dot_nt(dov, v[hh])).astype(BF16)
                dv_a = dv_a + _dot(dot_ref[hh, :, pl.ds(qs, rows)], p.astype(BF16))
                dk_a = dk_a + _dot(qt_ref[hh, :, pl.ds(qs, rows)], ds)
                dq_scr[hh, pl.ds(qs, rows), :] += _dot(ds, k[hh])
                out.append((dk_a, dv_a))
            return tuple(out)

        zero = (jnp.zeros((PAIR, tk), F32), jnp.zeros((PAIR, tk), F32))
        carry = lax.cond(j % 2 == 0, lambda c: step(j // 2, c, True),
                         lambda c: step(j // 2, c, True, rows=tk, row0=tk), (zero, zero))
        (dk0, dv0), (dk1, dv1) = lax.fori_loop(j // 2 + 1, nq, functools.partial(step, diag=False), carry)
        first_t = lax.broadcasted_iota(jnp.int32, (PAIR, 1), 0) < HEAD_DIM
        first = lax.broadcasted_iota(jnp.int32, (1, PAIR), 1) < HEAD_DIM
        dk_ref[...] = (jnp.where(first_t, dk0, dk1).T * (1.0 / LOG2E)).astype(BF16)
        dkx_ref[...] = jnp.where(first_t, dk1, dk0).T
        dv_ref[...] = jnp.where(first_t, dv0, dv1).T.astype(BF16)

        @pl.when(j == nk - 1)
        def _():
            dq_ref[...] = (jnp.where(first, dq_scr[0], dq_scr[1]) * Q_SCALE).astype(BF16)
            dqx_ref[...] = jnp.where(first, dq_scr[1], dq_scr[0])

        @pl.when((pl.program_id(0) == n_pairs - 1) & (j == nk - 1))
        def _():
            scatter.wait()

    resident = lambda: pl.BlockSpec((2, t, PAIR), lambda p, j: (p, 0, 0), pipeline_mode=pl.Buffered(1))
    resident_t = lambda: pl.BlockSpec((2, PAIR, t), lambda p, j: (p, 0, 0), pipeline_mode=pl.Buffered(1))
    kv_in = lambda: pl.BlockSpec((2, tk, PAIR), lambda p, j: (p, j, 0))
    whole = lambda: pl.BlockSpec((t, PAIR), lambda p, j: (0, p))
    tile = lambda: pl.BlockSpec((tk, PAIR), lambda p, j: (j, p))
    b16 = jax.ShapeDtypeStruct((t, ATTN_W), BF16)
    f32 = jax.ShapeDtypeStruct((t, ATTN_W), F32)
    res = pl.pallas_call(
        body, name="attn_bwd", grid=(n_pairs, nk),
        in_specs=[resident(), resident(), resident_t(), resident_t(), kv_in(), kv_in()] + [HBM_SPEC] * n_g,
        out_specs=[whole(), whole(), tile(), tile(), tile()] + [HBM_SPEC] * n_g,
        out_shape=[b16, f32, b16, f32, b16] + [jax.ShapeDtypeStruct(g.shape, g.dtype) for g in gblocks],
        scratch_shapes=[pltpu.VMEM((2, t, PAIR), F32)] + _Exchange.scratch(n_g),
        compiler_params=_cp(("arbitrary", "arbitrary")),
    )(q_bwd, do_aug, q_bwd_t, do_aug_t, k_aug, v_aug, *gblocks)
    return res[:5], res[5:]


def _fgate_bwd(dqx, dkx, sgate, *, tm=256):
    t = sgate.shape[0]
    tm = min(tm, t)
    nsteps = t // tm

    def body(dq_ref, dk_ref, sg_ref, df_ref, dbf_ref, carry):
        @pl.when(pl.program_id(0) == 0)
        def _():
            carry[...] = jnp.zeros_like(carry)
            dbf_ref[...] = jnp.zeros_like(dbf_ref)
        lane = lax.broadcasted_iota(jnp.int32, (ATTN_W, F_PAD), 0)
        head = lax.broadcasted_iota(jnp.int32, (ATTN_W, F_PAD), 1)
        aux = (head // 2) * PAIR + HEAD_DIM * (1 - head % 2)
        valid = head < N_HEADS
        pick_r = (valid & (lane == aux + AUX_ROWSUM)).astype(F32)
        pick_c = (valid & (lane == aux + AUX_BIAS)).astype(F32)
        hp = lax.Precision.HIGHEST
        dcum = (jnp.dot(dq_ref[...], pick_r, preferred_element_type=F32, precision=hp)
                + jnp.dot(dk_ref[...], pick_c, preferred_element_type=F32, precision=hp))
        r = lax.broadcasted_iota(jnp.int32, (tm, tm), 0)
        c = lax.broadcasted_iota(jnp.int32, (tm, tm), 1)
        tri = (c >= r).astype(F32)
        rc = jnp.dot(tri, dcum, preferred_element_type=F32, precision=hp) + carry[...]
        carry[...] = rc[0:1, :]
        df = rc * sg_ref[...]
        df_ref[...] = df.astype(BF16)
        dbf_ref[...] += _rows8(df)

    rev = lambda i: nsteps - 1 - i
    return pl.pallas_call(
        body, name="fgate_bwd", grid=(nsteps,),
        in_specs=[pl.BlockSpec((tm, ATTN_W), lambda i: (rev(i), 0)), pl.BlockSpec((tm, ATTN_W), lambda i: (rev(i), 0)),
                  pl.BlockSpec((tm, F_PAD), lambda i: (rev(i), 0))],
        out_specs=[pl.BlockSpec((tm, F_PAD), lambda i: (rev(i), 0)), pl.BlockSpec((8, F_PAD), lambda i: (0, 0))],
        out_shape=[jax.ShapeDtypeStruct((t, F_PAD), BF16), jax.ShapeDtypeStruct((8, F_PAD), F32)],
        scratch_shapes=[pltpu.VMEM((1, F_PAD), F32)],
        compiler_params=_cp(("arbitrary",)),
    )(dqx, dkx, sgate)


def _inproj_bwd(dq, dk, dv, df, dbcx, dh, x, g, win_p, gblock, *, tm=512):
    t, d = x.shape
    tm = min(tm, t)
    nsteps = t // tm
    n_qkv = 3 * ATTN_W

    def body(dq_ref, dk_ref, dv_ref, df_ref, db_ref, dh_ref, x_ref, g_ref, w_ref, gb_ref, gx_ref, dg_ref, land_ref,
             *sems):
        scatter = _Exchange([gb_ref], [land_ref], *sems, gather=False)

        @pl.when(pl.program_id(0) == 0)
        def _():
            scatter.start()
            dg_ref[...] = jnp.zeros_like(dg_ref)
        dn = _dot_nt(df_ref[...], w_ref[:, n_qkv:n_qkv + F_PAD])
        for k, r in enumerate((dq_ref, dk_ref, dv_ref)):
            dn = dn + _dot_nt(r[...], w_ref[:, k * ATTN_W:(k + 1) * ATTN_W])
        for k in range(3):
            c0 = n_qkv + F_PAD + k * CONV_CH
            dn = dn + _dot_nt(db_ref[:, k * CONV_CH:(k + 1) * CONV_CH], w_ref[:, c0:c0 + CONV_CH])
        dx, dg = _norm_bwd(dn, x_ref[...], g_ref[...])
        gx_ref[...] = dh_ref[...] + dx
        dg_ref[...] += dg

        @pl.when(pl.program_id(0) == nsteps - 1)
        def _():
            scatter.wait()

    row = lambda n_: pl.BlockSpec((tm, n_), lambda i: (i, 0))
    return pl.pallas_call(
        body, name="inproj_bwd", grid=(nsteps,),
        in_specs=[row(ATTN_W), row(ATTN_W), row(ATTN_W), row(F_PAD), row(3 * CONV_CH), row(d), row(d),
                  _const_spec((1, d)), _const_spec(win_p.shape), HBM_SPEC],
        out_specs=[row(d), pl.BlockSpec((8, d), lambda i: (0, 0)), HBM_SPEC],
        out_shape=[jax.ShapeDtypeStruct((t, d), F32), jax.ShapeDtypeStruct((8, d), F32),
                   jax.ShapeDtypeStruct(gblock.shape, gblock.dtype)],
        scratch_shapes=_Exchange.scratch(1),
        compiler_params=_cp(("arbitrary",)),
    )(dq, dk, dv, df, dbcx, dh, x, g, win_p, gblock)


LATE = ("w_out_0", "w_up_0", "w_down_0", "pool_w_1", "w_up_1", "w_down_1")


def _local_step(x, target, gains, b_f, conv_w, pool_scale, win_p, shards):
    d = x.shape[1]
    n0, qkv, flog, bcx, cv = _norm_inproj(x, gains["mix0"], win_p, conv_w)
    q_aug_t, k_aug, v_aug, v_aug_t, sgate = _fgate_prep(flog, b_f, qkv)
    att, q_bwd, q_bwd_t, gathered = _attn_fwd(q_aug_t, k_aug, v_aug_t, [shards[n] for n in LATE])
    g = dict(zip(LATE, gathered))
    wout = g["w_out_0"].reshape(d, d)
    wup0, wup1 = g["w_up_0"], g["w_up_1"]
    wdown0, wdown1 = g["w_down_0"].reshape(-1, d), g["w_down_1"].reshape(-1, d)
    n_grp = len(POOL_WINDOWS)
    cg = d // n_grp
    poolw = g["pool_w_1"].reshape(N_DEV, n_grp, cg // N_DEV, cg).transpose(1, 0, 2, 3).reshape(n_grp, cg, cg)
    h1 = _outproj(att, cv, x, wout)
    h2, n1, a0, z0 = _mlp_fwd(h1, gains["ffn0"], wup0, wdown0, name="mlp_fwd0")
    h3, pooled = _pool_fwd(h2, gains["mix1"], poolw, pool_scale)
    loss, dh4, dg_final, n3, a1, z1 = _mlp_fwd_loss(h3, gains["ffn1"], wup1, wdown1, gains["final"], target,
                                                    name="mlp_fwd1")
    f = a1.shape[1]
    fb = f // N_DEV
    dh3, da1, dg_ffn1 = _mlp_bwd(dh4, h3, a1, gains["ffn1"], wup1, wdown1, name="mlp_bwd1")
    dwdown1 = _mm_tn(z1, dh4, name="dwdown1", ta=1024, tb=1024, tt=2048, out_dtype=BF16)
    dwup1 = _mm_tn(n3, da1, name="dwup1", ta=d, tb=fb, tt=4096, blocked_out=True, out_dtype=BF16)
    dh2, dpoolw, dscale, dg_mix1 = _pool_bwd(dh3, h2, pooled, gains["mix1"], poolw, pool_scale)
    dh1, da0, dg_ffn0 = _mlp_bwd(dh2, h1, a0, gains["ffn0"], wup0, wdown0, name="mlp_bwd0")
    dwdown0 = _mm_tn(z0, dh2, name="dwdown0", ta=1024, tb=1024, tt=2048, out_dtype=BF16)
    dwup0 = _mm_tn(n1, da0, name="dwup0", ta=d, tb=fb, tt=4096, blocked_out=True, out_dtype=BF16)
    do_aug, do_aug_t, dcv = _outproj_bwd(dh1, att, wout)
    dwout = jnp.concatenate([_mm_tn(att, dh1, name="dwout_att", ta=512, tb=1024, tt=2048, out_dtype=BF16),
                             _mm_tn(cv, dh1, name="dwout_conv", ta=512, tb=1024, tt=2048, out_dtype=BF16)], axis=0)
    dbcx, dconvw = _conv_bwd(bcx, dcv, conv_w)
    gblocks = {
        "w_out_0": dwout.reshape(N_DEV, d // N_DEV, d), "w_up_0": dwup0, "w_up_1": dwup1,
        "w_down_0": dwdown0.reshape(N_DEV, -1, d), "w_down_1": dwdown1.reshape(N_DEV, -1, d),
        "pool_w_1": dpoolw.astype(BF16).reshape(n_grp, N_DEV, cg // N_DEV, cg).transpose(1, 0, 2, 3).reshape(
            N_DEV, n_grp * (cg // N_DEV), cg),
    }
    (dq, dqx, dk, dkx, dv), landed = _attn_bwd(q_bwd, do_aug, q_bwd_t, do_aug_t, k_aug, v_aug,
                                               [gblocks[n] for n in LATE])
    df, dbf = _fgate_bwd(dqx, dkx, sgate)
    dwin = jnp.concatenate(
        [_mm_tn(n0, dq, name="dwin_q", ta=d, tb=512, tt=4096, out_dtype=BF16),
         _mm_tn(n0, dk, name="dwin_k", ta=d, tb=512, tt=4096, out_dtype=BF16),
         _mm_tn(n0, dv, name="dwin_v", ta=d, tb=512, tt=4096, out_dtype=BF16),
         _mm_tn(n0, df, name="dwin_f", ta=d, tb=128, tt=2048, out_dtype=BF16)[:, :N_HEADS],
         _mm_tn(n0, dbcx, name="dwin_bcx", ta=d, tb=512, tt=4096, out_dtype=BF16)], axis=1)
    dwin_blocks = dwin.reshape(d, N_DEV, dwin.shape[1] // N_DEV).transpose(1, 0, 2)
    grad_x, dg_mix0, landed_win = _inproj_bwd(dq, dk, dv, df, dbcx, dh1, x, gains["mix0"], win_p, dwin_blocks)
    small = dict(mix0=dg_mix0, ffn0=dg_ffn0, mix1=dg_mix1, pool_scale=dscale, ffn1=dg_ffn1, final=dg_final,
                 b_f=dbf, conv_w=dconvw)
    return loss, grad_x, dict(zip(LATE + ("w_in_0",), tuple(landed) + (landed_win,))), small


def _mesh_places():
    x, y, c = lax.axis_index("x"), lax.axis_index("y"), lax.axis_index("c")
    chips = [(1 - x, y), (x, 1 - y), (1 - x, 1 - y)]
    return (x, y, c), (x, y, 1 - c), chips


def _all_gather(shards):
    n = len(shards)

    def body(*refs):
        ins, outs = refs[:n], refs[n:2 * n]
        send_sems, recv_sems, local_sems = refs[2 * n:]
        me, sib, chips = _mesh_places()
        c = me[2]

        def copy(ai, k, block, to, src=None):
            dst = outs[ai].at[_slot(*block)]
            return pltpu.make_async_remote_copy(
                src_ref=dst if src is None else src, dst_ref=dst, send_sem=send_sems.at[7 * ai + k],
                recv_sem=recv_sems.at[7 * ai + k], device_id=to, device_id_type=MESH)

        mine = [pltpu.make_async_copy(ins[ai], outs[ai].at[_slot(*me)], local_sems.at[ai]) for ai in range(n)]
        for cp in mine:
            cp.start()
        first = []
        for ai in range(n):
            first.append(copy(ai, 0, me, sib, src=ins[ai]))
            first += [copy(ai, 1 + j, me, (*chip, c), src=ins[ai]) for j, chip in enumerate(chips)]
        for cp in first:
            cp.start()
        passed = []
        for ai in range(n):
            for j, chip in enumerate(chips):
                copy(ai, 1 + j, (*chip, c), me).wait_recv()
                cp = copy(ai, 4 + j, (*chip, c), sib)
                cp.start()
                passed.append(cp)
        for ai in range(n):
            copy(ai, 0, sib, me).wait_recv()
            for j, chip in enumerate(chips):
                copy(ai, 4 + j, (*chip, 1 - c), me).wait_recv()
        for cp in first + passed:
            cp.wait_send()
        for cp in mine:
            cp.wait()

    return pl.pallas_call(
        body, name="all_gather",
        in_specs=[HBM_SPEC] * n, out_specs=[HBM_SPEC] * n,
        out_shape=[jax.ShapeDtypeStruct((N_DEV,) + s.shape, s.dtype) for s in shards],
        scratch_shapes=[pltpu.SemaphoreType.DMA((7 * n,)), pltpu.SemaphoreType.DMA((7 * n,)),
                        pltpu.SemaphoreType.DMA((n,))],
    )(*shards)


SMALL_ROWS = 16


def _small_allreduce(parts):
    n, _, w = parts.shape
    assert n <= SMALL_ROWS

    def body(p_ref, o_ref, gath, send_sems, recv_sems):
        x, y, c = lax.axis_index("x"), lax.axis_index("y"), lax.axis_index("c")
        my = _slot(x, y, c)
        rows = [jnp.sum(p_ref[i], axis=0, keepdims=True) for i in range(n)]
        rows.append(jnp.zeros((SMALL_ROWS - n, w), F32))
        gath[my] = jnp.concatenate(rows, axis=0)
        copies = []
        for k in range(1, N_DEV):
            px, py, pc = x ^ (k >> 2), y ^ ((k >> 1) & 1), c ^ (k & 1)
            cp = pltpu.make_async_remote_copy(
                src_ref=gath.at[my], dst_ref=gath.at[my], send_sem=send_sems.at[k - 1], recv_sem=recv_sems.at[k - 1],
                device_id=(px, py, pc), device_id_type=MESH)
            cp.start()
            copies.append(cp)
        for cp in copies:
            cp.wait()
        acc = gath[0]
        for d in range(1, N_DEV):
            acc = acc + gath[d]
        o_ref[...] = acc

    return pl.pallas_call(
        body, name="small_allreduce",
        in_specs=[VMEM_SPEC], out_specs=VMEM_SPEC,
        out_shape=jax.ShapeDtypeStruct((SMALL_ROWS, w), F32),
        scratch_shapes=[pltpu.VMEM((N_DEV, SMALL_ROWS, w), F32), pltpu.SemaphoreType.DMA((N_DEV - 1,)),
                        pltpu.SemaphoreType.DMA((N_DEV - 1,))],
    )(parts)


def _adamw(g, w, m, v, *, name, tm=256):
    r, c = g.shape
    tm = tm if r % tm == 0 else r
    bc1 = 1.0 - ADAM_B1 ** ADAM_STEP
    bc2 = 1.0 - ADAM_B2 ** ADAM_STEP

    def body(g_ref, w_ref, m_ref, v_ref, d_ref, nm_ref, nv_ref):
        gv = g_ref[...]
        nm = ADAM_B1 * m_ref[...] + (1.0 - ADAM_B1) * gv
        nv = ADAM_B2 * v_ref[...] + (1.0 - ADAM_B2) * jnp.square(gv)
        nm_ref[...] = nm
        nv_ref[...] = nv
        d_ref[...] = -ADAM_LR * ((nm / bc1) / (jnp.sqrt(nv / bc2) + ADAM_EPS) + ADAM_WD * w_ref[...])

    blk = pl.BlockSpec((tm, c), lambda i: (i, 0))
    shp = jax.ShapeDtypeStruct((r, c), F32)
    return pl.pallas_call(
        body, name=name, grid=(r // tm,), in_specs=[blk] * 4, out_specs=[blk] * 3, out_shape=[shp] * 3,
        compiler_params=_cp(("parallel",)),
    )(g, w, m, v)


def _adamw_sum(parts, w, m, v, *, name, tm=128):
    _, r, c = parts.shape
    tm = tm if r % tm == 0 else r
    bc1 = 1.0 - ADAM_B1 ** ADAM_STEP
    bc2 = 1.0 - ADAM_B2 ** ADAM_STEP

    def body(p_ref, w_ref, m_ref, v_ref, g_ref, d_ref, nm_ref, nv_ref):
        gv = p_ref[0].astype(F32)
        for k in range(1, N_DEV):
            gv = gv + p_ref[k].astype(F32)
        g_ref[...] = gv
        nm = ADAM_B1 * m_ref[...] + (1.0 - ADAM_B1) * gv
        nv = ADAM_B2 * v_ref[...] + (1.0 - ADAM_B2) * jnp.square(gv)
        nm_ref[...] = nm
        nv_ref[...] = nv
        d_ref[...] = -ADAM_LR * ((nm / bc1) / (jnp.sqrt(nv / bc2) + ADAM_EPS) + ADAM_WD * w_ref[...])

    blk = pl.BlockSpec((tm, c), lambda i: (i, 0))
    shp = jax.ShapeDtypeStruct((r, c), F32)
    return pl.pallas_call(
        body, name=name, grid=(r // tm,), in_specs=[pl.BlockSpec((N_DEV, tm, c), lambda i: (0, i, 0))] + [blk] * 3,
        out_specs=[blk] * 4, out_shape=[shp] * 4, compiler_params=_cp(("parallel",)),
    )(parts, w, m, v)


BIG = ("w_in_0", "w_out_0", "w_up_0", "w_down_0", "pool_w_1", "w_up_1", "w_down_1")
SMALL = ("norm_mix_0", "norm_ffn_0", "norm_mix_1", "pool_scale_1", "norm_ffn_1", "final_norm", "b_f_0", "conv_w_0")
WEIGHTS = ("norm_mix_0", "w_in_0", "b_f_0", "conv_w_0", "w_out_0", "norm_ffn_0", "w_up_0", "w_down_0", "norm_mix_1",
           "pool_w_1", "pool_scale_1", "norm_ffn_1", "w_up_1", "w_down_1", "final_norm")


def _pad_to(a, rows, cols):
    return jnp.pad(a, ((0, rows - a.shape[0]), (0, cols - a.shape[1])))


def _pack_small(p, width):
    rows = [p[n].reshape(1, -1) for n in SMALL[:6]]
    rows.append(_pad_to(p["b_f_0"].reshape(1, -1), 1, width))
    rows.append(_pad_to(p["conv_w_0"], 3, width))
    return _pad_to(jnp.concatenate(rows, axis=0), SMALL_ROWS, width)


def _unpack_small(a, like):
    out = {n: a[i] for i, n in enumerate(SMALL[:6])}
    out["b_f_0"] = a[6, :like["b_f_0"].shape[0]]
    out["conv_w_0"] = a[7:10, :like["conv_w_0"].shape[1]]
    return out


def kernel(x, norm_mix_0, w_in_0, b_f_0, conv_w_0, w_out_0, norm_ffn_0, w_up_0, w_down_0, norm_mix_1, pool_w_1, pool_scale_1, norm_ffn_1, w_up_1, w_down_1, final_norm, loss_target, m_norm_mix_0, m_w_in_0, m_b_f_0, m_conv_w_0, m_w_out_0, m_norm_ffn_0, m_w_up_0, m_w_down_0, m_norm_mix_1, m_pool_w_1, m_pool_scale_1, m_norm_ffn_1, m_w_up_1, m_w_down_1, m_final_norm, v_norm_mix_0, v_w_in_0, v_b_f_0, v_conv_w_0, v_w_out_0, v_norm_ffn_0, v_w_up_0, v_w_down_0, v_norm_mix_1, v_pool_w_1, v_pool_scale_1, v_norm_ffn_1, v_w_up_1, v_w_down_1, v_final_norm):
    w = dict(norm_mix_0=norm_mix_0, w_in_0=w_in_0, b_f_0=b_f_0, conv_w_0=conv_w_0, w_out_0=w_out_0,
             norm_ffn_0=norm_ffn_0, w_up_0=w_up_0, w_down_0=w_down_0, norm_mix_1=norm_mix_1, pool_w_1=pool_w_1,
             pool_scale_1=pool_scale_1, norm_ffn_1=norm_ffn_1, w_up_1=w_up_1, w_down_1=w_down_1, final_norm=final_norm)
    m = dict(norm_mix_0=m_norm_mix_0, w_in_0=m_w_in_0, b_f_0=m_b_f_0, conv_w_0=m_conv_w_0, w_out_0=m_w_out_0,
             norm_ffn_0=m_norm_ffn_0, w_up_0=m_w_up_0, w_down_0=m_w_down_0, norm_mix_1=m_norm_mix_1,
             pool_w_1=m_pool_w_1, pool_scale_1=m_pool_scale_1, norm_ffn_1=m_norm_ffn_1, w_up_1=m_w_up_1,
             w_down_1=m_w_down_1, final_norm=m_final_norm)
    v = dict(norm_mix_0=v_norm_mix_0, w_in_0=v_w_in_0, b_f_0=v_b_f_0, conv_w_0=v_conv_w_0, w_out_0=v_w_out_0,
             norm_ffn_0=v_norm_ffn_0, w_up_0=v_w_up_0, w_down_0=v_w_down_0, norm_mix_1=v_norm_mix_1,
             pool_w_1=v_pool_w_1, pool_scale_1=v_pool_scale_1, norm_ffn_1=v_norm_ffn_1, w_up_1=v_w_up_1,
             w_down_1=v_w_down_1, final_norm=v_final_norm)
    d = x.shape[-1]
    n_in = w_in_0.shape[1] * N_DEV
    n_qkv = 3 * ATTN_W
    pool_g, pool_rows, pool_c = pool_w_1.shape

    def shard2d(p):
        return {n: (p[n].reshape(pool_g * pool_rows, pool_c) if n == "pool_w_1" else p[n]) for n in BIG}
    w2, m2, v2 = shard2d(w), shard2d(m), shard2d(v)

    conv_cols = conv_w_0.shape[1]
    win_g8, conv_g8 = _all_gather([w_in_0.astype(BF16), _pad_to(conv_w_0, 8, 128)])
    conv_full = conv_g8[:, :, :conv_cols].transpose(1, 0, 2).reshape(8, N_DEV * conv_cols)
    win = win_g8.transpose(1, 0, 2).reshape(d, n_in)
    win_p = jnp.concatenate([win[:, :n_qkv], _pad_to(win[:, n_qkv:n_qkv + N_HEADS], d, F_PAD),
                             win[:, n_qkv + N_HEADS:]], axis=1)

    gains = dict(mix0=norm_mix_0.reshape(1, d), ffn0=norm_ffn_0.reshape(1, d), mix1=norm_mix_1.reshape(1, d),
                 ffn1=norm_ffn_1.reshape(1, d), final=final_norm.reshape(1, d))
    dev = _slot(lax.axis_index("x"), lax.axis_index("y"), lax.axis_index("c"))
    loss8, grad_x, landed, small = _local_step(
        x[0], loss_target[0], gains, _pad_to(b_f_0.reshape(1, -1), 1, F_PAD), conv_full, pool_scale_1.reshape(1, d),
        win_p, {n: w2[n].astype(BF16) for n in LATE})
    parts = jnp.concatenate(
        [small[k][None] for k in ("mix0", "ffn0", "mix1", "pool_scale", "ffn1", "final")]
        + [_pad_to(small["b_f"], 8, d)[None], jnp.pad(small["conv_w"], ((0, 0), (0, 0), (0, d - CONV_CH))),
           _pad_to(loss8[0:1, 0:1], 8, d)[None]], axis=0)
    tot = _small_allreduce(parts)
    loss = tot[10, 0]
    conv_g = lax.dynamic_slice(tot, (7, dev * conv_cols), (3, conv_cols))
    gs = tot.at[7:10].set(_pad_to(conv_g, 3, d))

    grads, deltas, new_m, new_v = {}, {}, {}, {}
    for n in BIG:
        gr, dl, nm, nv = _adamw_sum(landed[n], w2[n], m2[n], v2[n], name="adamw_" + n)
        for dst, val in ((grads, gr), (deltas, dl), (new_m, nm), (new_v, nv)):
            dst[n] = val.reshape(w[n].shape)
    dl, nm, nv = _adamw(gs, _pack_small(w, d), _pack_small(m, d), _pack_small(v, d), name="adamw_small")
    for dst, val in ((grads, gs), (deltas, dl), (new_m, nm), (new_v, nv)):
        dst.update(_unpack_small(val, w))
    return (loss, grad_x[None], *[grads[n] for n in WEIGHTS], *[deltas[n] for n in WEIGHTS],
            *[new_m[n] for n in WEIGHTS], *[new_v[n] for n in WEIGHTS])
```

```python
import functools

import jax
import jax.numpy as jnp
from jax import lax
from jax.experimental import pallas as pl
from jax.experimental.pallas import tpu as pltpu

F32 = jnp.float32
BF16 = jnp.bfloat16

N_DEV = 8
N_HEADS = 8
HEAD_DIM = 64
PAIR = 2 * HEAD_DIM
ATTN_W = N_HEADS * HEAD_DIM
CONV_CH = 512
F_PAD = 128
POOL_WINDOWS = (2, 4, 8, 16)
POOL_HALO = 16
CONV_HALO = 16
RMS_EPS = 1e-6
Q_SCALE = HEAD_DIM ** -0.5
LOG2E = 1.4426950408889634
NEG = -1e30
AUX_BIAS = 0
AUX_LSE = 3
AUX_ROWSUM = 6
ADAM_LR, ADAM_B1, ADAM_B2, ADAM_EPS, ADAM_WD, ADAM_STEP = 0.001, 0.9, 0.999, 1e-08, 0.01, 10
MESH = pl.DeviceIdType.MESH
VMEM_LIMIT = 56 * 2**20


def _cp(sem=None, vmem=VMEM_LIMIT, **kw):
    return pltpu.CompilerParams(dimension_semantics=sem, vmem_limit_bytes=vmem, **kw)


def _dot(a, b):
    return jnp.dot(a, b, preferred_element_type=F32)


def _dot_nt(a, b):
    return lax.dot_general(a, b, (((1,), (1,)), ((), ())), preferred_element_type=F32)


def _dot_tn(a, b):
    return lax.dot_general(a, b, (((0,), (0,)), ((), ())), preferred_element_type=F32)


def _rstd(h):
    return lax.rsqrt(jnp.mean(h * h, axis=-1, keepdims=True) + RMS_EPS)


def _rows8(x):
    r, n = x.shape
    return jnp.sum(x.reshape(r // 8, 8, n), axis=0)


def _norm_bwd(dn, h, g):
    r = _rstd(h)
    xhat = h * r
    dy = dn * g
    dh = r * (dy - xhat * jnp.mean(dy * xhat, axis=-1, keepdims=True))
    return dh, _rows8(dn * xhat)


def _const_spec(shape):
    nd = len(shape)
    return pl.BlockSpec(shape, lambda *_: (0,) * nd, pipeline_mode=pl.Buffered(1))


HBM_SPEC = pl.BlockSpec(memory_space=pltpu.HBM)
VMEM_SPEC = pl.BlockSpec(memory_space=pltpu.VMEM)


def _slot(px, py, pc):
    return 4 * px + 2 * py + pc


class _Exchange:
    def __init__(self, srcs, dsts, send_sems, recv_sems, local_sems, gather):
        x, y, c = lax.axis_index("x"), lax.axis_index("y"), lax.axis_index("c")
        me = _slot(x, y, c)
        self.copies = []
        for a, (src, dst) in enumerate(zip(srcs, dsts)):
            self.copies.append(pltpu.make_async_copy(src if gather else src.at[me], dst.at[me], local_sems.at[a]))
            for k in range(1, N_DEV):
                px, py, pc = x ^ (k >> 2), y ^ ((k >> 1) & 1), c ^ (k & 1)
                self.copies.append(pltpu.make_async_remote_copy(
                    src_ref=src if gather else src.at[_slot(px, py, pc)], dst_ref=dst.at[me],
                    send_sem=send_sems.at[(N_DEV - 1) * a + k - 1], recv_sem=recv_sems.at[(N_DEV - 1) * a + k - 1],
                    device_id=(px, py, pc), device_id_type=MESH))

    def start(self):
        for cp in self.copies:
            cp.start()

    def wait(self):
        for cp in self.copies:
            cp.wait()

    @staticmethod
    def scratch(n):
        return [pltpu.SemaphoreType.DMA(((N_DEV - 1) * n,)), pltpu.SemaphoreType.DMA(((N_DEV - 1) * n,)),
                pltpu.SemaphoreType.DMA((n,))]


def _norm_inproj(x, g, win_p, conv_w, *, tm=512):
    t, d = x.shape
    n_all = win_p.shape[1]
    n_qkv = 3 * ATTN_W
    n_bcx = 3 * CONV_CH
    assert n_all == n_qkv + F_PAD + n_bcx
    tm = min(tm, t)
    ch = CONV_CH

    def body(x_ref, g_ref, w_ref, cw_ref, n_ref, qkv_ref, f_ref, bcx_ref, cv_ref, ext):
        h = x_ref[...]
        n = (h * _rstd(h) * g_ref[...]).astype(BF16)
        n_ref[...] = n
        for c0 in range(0, n_qkv, 512):
            acc = _dot(n, w_ref[:, c0:c0 + 512])
            if c0 < ATTN_W:
                acc = acc * (Q_SCALE * LOG2E)
            qkv_ref[:, c0:c0 + 512] = acc.astype(BF16)
        f_ref[...] = _dot(n, w_ref[:, n_qkv:n_qkv + F_PAD])
        bcx = []
        for k in range(3):
            c0 = n_qkv + F_PAD + k * ch
            v = _dot(n, w_ref[:, c0:c0 + ch]).astype(BF16)
            bcx_ref[:, k * ch:(k + 1) * ch] = v
            bcx.append(v.astype(F32))
        @pl.when(pl.program_id(0) == 0)
        def _():
            ext[tm:tm + CONV_HALO, :] = jnp.zeros((CONV_HALO, ch), F32)
        ext[0:CONV_HALO, :] = ext[tm:tm + CONV_HALO, :]
        ext[CONV_HALO:CONV_HALO + tm, :] = bcx[1] * bcx[2]
        conv = (cw_ref[0:1, :] * ext[CONV_HALO - 2:CONV_HALO - 2 + tm, :]
                + cw_ref[1:2, :] * ext[CONV_HALO - 1:CONV_HALO - 1 + tm, :]
                + cw_ref[2:3, :] * ext[CONV_HALO:CONV_HALO + tm, :])
        cv_ref[...] = (bcx[0] * conv).astype(BF16)

    return pl.pallas_call(
        body, name="norm_inproj", grid=(t // tm,),
        in_specs=[pl.BlockSpec((tm, d), lambda i: (i, 0)), _const_spec((1, d)), _const_spec((d, n_all)),
                  _const_spec((8, ch))],
        out_specs=[pl.BlockSpec((tm, d), lambda i: (i, 0)), pl.BlockSpec((tm, n_qkv), lambda i: (i, 0)),
                   pl.BlockSpec((tm, F_PAD), lambda i: (i, 0)), pl.BlockSpec((tm, n_bcx), lambda i: (i, 0)),
                   pl.BlockSpec((tm, ch), lambda i: (i, 0))],
        out_shape=[jax.ShapeDtypeStruct((t, d), BF16), jax.ShapeDtypeStruct((t, n_qkv), BF16),
                   jax.ShapeDtypeStruct((t, F_PAD), F32), jax.ShapeDtypeStruct((t, n_bcx), BF16),
                   jax.ShapeDtypeStruct((t, ch), BF16)],
        scratch_shapes=[pltpu.VMEM((CONV_HALO + tm, ch), F32)],
        compiler_params=_cp(("arbitrary",)),
    )(x, g, win_p, conv_w)


def _head_lanes(h):
    lane = lax.broadcasted_iota(jnp.int32, (1, PAIR), 1)
    hh = h % 2
    return lane, lane // HEAD_DIM == hh, HEAD_DIM * (1 - hh)


def _pieces(col):
    hi = col.astype(BF16).astype(F32)
    r1 = col - hi
    mid = r1.astype(BF16).astype(F32)
    lo = (r1 - mid).astype(BF16).astype(F32)
    return hi, mid, lo


def _put_pieces(lane, first, col, other):
    hi, mid, lo = _pieces(col)
    return jnp.where(lane == first, hi, jnp.where(lane == first + 1, mid, jnp.where(lane == first + 2, lo, other)))


def _fgate_prep(flog, b_f, qkv, *, tm=512):
    t = flog.shape[0]
    tm = min(tm, t)

    def body(f_ref, b_ref, qkv_ref, qat_ref, ka_ref, va_ref, vat_ref, sg_ref, carry):
        @pl.when(pl.program_id(0) == 0)
        def _():
            carry[...] = jnp.zeros_like(carry)
        z = f_ref[...] + b_ref[...]
        e = jnp.exp(-jnp.abs(z))
        logf = jnp.minimum(z, 0.0) - jnp.log(1.0 + e)
        sg_ref[...] = jnp.where(z >= 0, e, 1.0) / (1.0 + e)
        r = lax.broadcasted_iota(jnp.int32, (tm, tm), 0)
        c = lax.broadcasted_iota(jnp.int32, (tm, tm), 1)
        tri = (c <= r).astype(F32)
        cs = jnp.dot(tri, logf, preferred_element_type=F32, precision=lax.Precision.HIGHEST) + carry[...]
        carry[...] = cs[tm - 1:tm, :]
        cs2 = cs * LOG2E
        for h in range(N_HEADS):
            lane, head, aux = _head_lanes(h)
            p0 = (h // 2) * PAIR
            ones = ((lane >= aux + AUX_LSE) & (lane <= aux + AUX_ROWSUM)).astype(F32)
            bias = (lane >= aux + AUX_BIAS) & (lane < aux + AUX_BIAS + 3)
            k_aux = _put_pieces(lane, aux + AUX_BIAS, cs2[:, h:h + 1], ones)
            q_aug = jnp.where(head, qkv_ref[:, p0:p0 + PAIR].astype(F32), jnp.where(bias, -1.0, 0.0))
            v_aug = jnp.where(head, qkv_ref[:, 2 * ATTN_W + p0:2 * ATTN_W + p0 + PAIR].astype(F32),
                              jnp.where(bias, 1.0, 0.0))
            qat_ref[h] = q_aug.astype(BF16).T
            ka_ref[h] = jnp.where(head, qkv_ref[:, ATTN_W + p0:ATTN_W + p0 + PAIR], k_aux.astype(BF16))
            va_ref[h] = v_aug.astype(BF16)
            vat_ref[h] = v_aug.astype(BF16).T

    aug = lambda: pl.BlockSpec((N_HEADS, tm, PAIR), lambda i: (0, i, 0))
    aug_t = lambda: pl.BlockSpec((N_HEADS, PAIR, tm), lambda i: (0, 0, i))
    aug_shape = jax.ShapeDtypeStruct((N_HEADS, t, PAIR), BF16)
    aug_t_shape = jax.ShapeDtypeStruct((N_HEADS, PAIR, t), BF16)
    return pl.pallas_call(
        body, name="fgate_prep", grid=(t // tm,),
        in_specs=[pl.BlockSpec((tm, F_PAD), lambda i: (i, 0)), _const_spec((1, F_PAD)),
                  pl.BlockSpec((tm, 3 * ATTN_W), lambda i: (i, 0))],
        out_specs=[aug_t(), aug(), aug(), aug_t(), pl.BlockSpec((tm, F_PAD), lambda i: (i, 0))],
        out_shape=[aug_t_shape, aug_shape, aug_shape, aug_t_shape, jax.ShapeDtypeStruct((t, F_PAD), F32)],
        scratch_shapes=[pltpu.VMEM((1, F_PAD), F32)],
        compiler_params=_cp(("arbitrary",)),
    )(flog, b_f, qkv)


def _put_pieces_t(row, first, vec, other):
    hi, mid, lo = _pieces(vec)
    return jnp.where(row == first, hi, jnp.where(row == first + 1, mid, jnp.where(row == first + 2, lo, other)))


def _attn_fwd(q_aug_t, k_aug, v_aug_t, shards, *, tq=1024):
    t = k_aug.shape[1]
    tq = min(tq, t)
    tk = tq // 2
    nq = t // tq
    n_pairs = ATTN_W // PAIR
    n_sh = len(shards)

    def body(qt_ref, k_ref, vt_ref, *rest):
        o_ref, qb_ref, qbt_ref = rest[n_sh:n_sh + 3]
        s_scr = rest[2 * n_sh + 3]
        gather = _Exchange(rest[:n_sh], rest[n_sh + 3:2 * n_sh + 3], *rest[2 * n_sh + 4:], gather=True)
        i = pl.program_id(1)

        @pl.when((pl.program_id(0) == 0) & (i == 0))
        def _():
            gather.start()
        key = lax.broadcasted_iota(jnp.int32, (tk, tq), 0)
        qry = lax.broadcasted_iota(jnp.int32, (tk, tq), 1)
        qt = [qt_ref[0], qt_ref[1]]

        def logits(hh, tile, slot, diag):
            s = _dot(k_ref[hh, pl.ds(pl.multiple_of(tile * tk, tk), tk), :], qt[hh])
            if diag:
                s = jnp.where(key + (tile * tk - i * tq) <= qry, s, NEG)
            s_scr[hh, slot] = s
            return jnp.max(s, axis=0, keepdims=True)

        def probs(hh, tile, slot, m, acc, tmax):
            mn = jnp.maximum(m, tmax)
            p = jnp.exp2(s_scr[hh, slot] - mn).astype(BF16)
            acc = jnp.exp2(m - mn) * acc + _dot(vt_ref[hh, :, pl.ds(pl.multiple_of(tile * tk, tk), tk)], p)
            return mn, acc

        def advance(carry, prev, slot, nxt, diag=False):
            out = []
            for hh in range(2):
                m, acc, tmax = carry[hh]
                m, acc = probs(hh, prev, slot, m, acc, tmax)
                out.append((m, acc, logits(hh, nxt, 1 - slot, diag)))
            return tuple(out)

        def two_tiles(jj, carry):
            carry = advance(carry, jnp.where(jj == 0, 2 * i, 2 * jj - 1), 1, 2 * jj)
            return advance(carry, 2 * jj, 0, 2 * jj + 1)

        init = tuple((jnp.full((1, tq), NEG, F32), jnp.zeros((PAIR, tq), F32), logits(hh, 2 * i + 1, 0, True))
                     for hh in range(2))
        carry = advance(init, 2 * i + 1, 0, 2 * i, diag=True)
        carry = lax.fori_loop(0, i, two_tiles, carry)
        last = jnp.where(i == 0, 2 * i, 2 * i - 1)
        row = lax.broadcasted_iota(jnp.int32, (PAIR, 1), 0)
        res = []
        for hh in range(2):
            aux = HEAD_DIM * (1 - hh)
            m, acc, tmax = carry[hh]
            m, acc = probs(hh, last, 1, m, acc, tmax)
            l = acc[aux + AUX_BIAS:aux + AUX_BIAS + 1, :]
            qbt = _put_pieces_t(row, aux + AUX_LSE, -(m + jnp.log2(l)), qt[hh].astype(F32))
            qbt_ref[hh] = qbt.astype(BF16)
            qb_ref[hh] = qbt.astype(BF16).T
            res.append(acc * (1.0 / l))
        o_ref[...] = jnp.where(row < HEAD_DIM, res[0], res[1]).astype(BF16).T

        @pl.when((pl.program_id(0) == n_pairs - 1) & (i == nq - 1))
        def _():
            gather.wait()

    res = pl.pallas_call(
        body, name="attn_fwd", grid=(n_pairs, nq),
        in_specs=[pl.BlockSpec((2, PAIR, tq), lambda p, i: (p, 0, i)),
                  pl.BlockSpec((2, t, PAIR), lambda p, i: (p, 0, 0), pipeline_mode=pl.Buffered(1)),
                  pl.BlockSpec((2, PAIR, t), lambda p, i: (p, 0, 0), pipeline_mode=pl.Buffered(1))] + [HBM_SPEC] * n_sh,
        out_specs=[pl.BlockSpec((tq, PAIR), lambda p, i: (i, p)),
                   pl.BlockSpec((2, tq, PAIR), lambda p, i: (p, i, 0)),
                   pl.BlockSpec((2, PAIR, tq), lambda p, i: (p, 0, i))] + [HBM_SPEC] * n_sh,
        out_shape=[jax.ShapeDtypeStruct((t, ATTN_W), BF16), jax.ShapeDtypeStruct((N_HEADS, t, PAIR), BF16),
                   jax.ShapeDtypeStruct((N_HEADS, PAIR, t), BF16)]
        + [jax.ShapeDtypeStruct((N_DEV,) + s.shape, s.dtype) for s in shards],
        scratch_shapes=[pltpu.VMEM((2, 2, tk, tq), F32)] + _Exchange.scratch(n_sh),
        compiler_params=_cp(("arbitrary", "arbitrary")),
    )(q_aug_t, k_aug, v_aug_t, *shards)
    return res[0], res[1], res[2], res[3:]


def _prev_halo(tm, halo):
    return lambda i: (jnp.maximum(i * (tm // halo) - 1, 0), 0)


def _next_halo(tm, halo, t):
    return lambda i: (jnp.minimum((i + 1) * (tm // halo), t // halo - 1), 0)


def _mlp_tile(hh, g_ref, wu_ref, wd_ref, n_ref, a_ref, z_ref):
    n_blk, _, fb = wu_ref.shape
    n = (hh * _rstd(hh) * g_ref[...]).astype(BF16)
    n_ref[...] = n
    acc = hh
    for k in range(n_blk):
        a = _dot(n, wu_ref[k])
        zz = jnp.square(jnp.maximum(a, 0.0)).astype(BF16)
        a_ref[:, k * fb:(k + 1) * fb] = a.astype(BF16)
        z_ref[:, k * fb:(k + 1) * fb] = zz
        acc = acc + _dot(zz, wd_ref[k * fb:(k + 1) * fb, :])
    return acc


def _outproj(att, cv, x, wout, *, tm=512):
    t, d = x.shape
    tm = min(tm, t)

    def body(a_ref, c_ref, x_ref, w_ref, h_ref):
        h_ref[...] = x_ref[...] + _dot(a_ref[...], w_ref[0:ATTN_W, :]) + _dot(c_ref[...], w_ref[ATTN_W:, :])

    return pl.pallas_call(
        body, name="outproj", grid=(t // tm,),
        in_specs=[pl.BlockSpec((tm, ATTN_W), lambda i: (i, 0)), pl.BlockSpec((tm, CONV_CH), lambda i: (i, 0)),
                  pl.BlockSpec((tm, d), lambda i: (i, 0)), _const_spec(wout.shape)],
        out_specs=pl.BlockSpec((tm, d), lambda i: (i, 0)),
        out_shape=jax.ShapeDtypeStruct((t, d), F32),
        compiler_params=_cp(("parallel",)),
    )(att, cv, x, wout)


def _mlp_fwd(h, g, wup, wdown, *, name, tm=512):
    t, d = h.shape
    n_blk, _, fb = wup.shape
    f = n_blk * fb
    tm = min(tm, t)

    def body(h_ref, g_ref, wu_ref, wd_ref, ho_ref, n_ref, a_ref, z_ref):
        ho_ref[...] = _mlp_tile(h_ref[...], g_ref, wu_ref, wd_ref, n_ref, a_ref, z_ref)

    row = lambda n_: pl.BlockSpec((tm, n_), lambda i: (i, 0))
    return pl.pallas_call(
        body, name=name, grid=(t // tm,),
        in_specs=[row(d), _const_spec((1, d)), _const_spec(wup.shape), _const_spec(wdown.shape)],
        out_specs=[row(d), row(d), row(f), row(f)],
        out_shape=[jax.ShapeDtypeStruct((t, d), F32), jax.ShapeDtypeStruct((t, d), BF16),
                   jax.ShapeDtypeStruct((t, f), BF16), jax.ShapeDtypeStruct((t, f), BF16)],
        compiler_params=_cp(("parallel",)),
    )(h, g, wup, wdown)


def _mlp_fwd_loss(h, g, wup, wdown, g_out, target, *, name, tm=512):
    t, d = h.shape
    n_blk, _, fb = wup.shape
    f = n_blk * fb
    tm = min(tm, t)
    nsteps = t // tm

    def body(h_ref, g_ref, wu_ref, wd_ref, go_ref, y_ref, loss_ref, dh_ref, dg_ref, n_ref, a_ref, z_ref, lacc):
        i = pl.program_id(0)

        @pl.when(i == 0)
        def _():
            lacc[...] = jnp.zeros_like(lacc)
            dg_ref[...] = jnp.zeros_like(dg_ref)
        hv = _mlp_tile(h_ref[...], g_ref, wu_ref, wd_ref, n_ref, a_ref, z_ref)
        gv = go_ref[...]
        r = _rstd(hv)
        xhat = hv * r
        err = xhat * gv - y_ref[...]
        lacc[...] += _rows8(err * err)
        dout = err * (1.0 / d)
        dy = dout * gv
        dg_ref[...] += _rows8(dout * xhat)
        dh_ref[...] = r * (dy - xhat * jnp.mean(dy * xhat, axis=-1, keepdims=True))

        @pl.when(i == nsteps - 1)
        def _():
            loss_ref[...] = jnp.full(loss_ref.shape, (0.5 / d) * jnp.sum(lacc[...]), F32)

    row = lambda n_: pl.BlockSpec((tm, n_), lambda i: (i, 0))
    return pl.pallas_call(
        body, name=name, grid=(nsteps,),
        in_specs=[row(d), _const_spec((1, d)), _const_spec(wup.shape), _const_spec(wdown.shape), _const_spec((1, d)),
                  row(d)],
        out_specs=[pl.BlockSpec((8, 128), lambda i: (0, 0)), row(d), pl.BlockSpec((8, d), lambda i: (0, 0)),
                   row(d), row(f), row(f)],
        out_shape=[jax.ShapeDtypeStruct((8, 128), F32), jax.ShapeDtypeStruct((t, d), F32),
                   jax.ShapeDtypeStruct((8, d), F32), jax.ShapeDtypeStruct((t, d), BF16),
                   jax.ShapeDtypeStruct((t, f), BF16), jax.ShapeDtypeStruct((t, f), BF16)],
        scratch_shapes=[pltpu.VMEM((8, d), F32)],
        compiler_params=_cp(("arbitrary",)),
    )(h, g, wup, wdown, g_out, target)


def _pool_inv_count(i, tm):
    tglob = (i * tm + lax.broadcasted_iota(jnp.int32, (tm, 1), 0) + 1).astype(F32)
    return [1.0 / jnp.minimum(tglob, float(w)) for w in POOL_WINDOWS]


def _pool_fwd(h, g, poolw, scale, *, tm=512):
    t, d = h.shape
    tm = min(tm, t)
    cg = d // len(POOL_WINDOWS)

    def body(h_ref, hh_ref, g_ref, w_ref, s_ref, ho_ref, p_ref, ext):
        i = pl.program_id(0)
        hv = h_ref[...]
        halo = hh_ref[...]
        n = hv * _rstd(hv) * g_ref[...]
        ext[0:POOL_HALO, :] = jnp.where(i == 0, 0.0, halo * _rstd(halo) * g_ref[...])
        ext[POOL_HALO:POOL_HALO + tm, :] = n
        inv = _pool_inv_count(i, tm)
        for gi, w in enumerate(POOL_WINDOWS):
            cs = slice(gi * cg, (gi + 1) * cg)
            s = ext[POOL_HALO:POOL_HALO + tm, cs]
            for j in range(1, w):
                s = s + ext[POOL_HALO - j:POOL_HALO - j + tm, cs]
            pooled = (s * inv[gi] - n[:, cs]).astype(BF16)
            p_ref[:, cs] = pooled
            ho_ref[:, cs] = hv[:, cs] + _dot(pooled, w_ref[gi]) * s_ref[:, cs]

    row = lambda: pl.BlockSpec((tm, d), lambda i: (i, 0))
    return pl.pallas_call(
        body, name="pool_fwd", grid=(t // tm,),
        in_specs=[row(), pl.BlockSpec((POOL_HALO, d), _prev_halo(tm, POOL_HALO)), _const_spec((1, d)),
                  _const_spec(poolw.shape), _const_spec((1, d))],
        out_specs=[row(), row()],
        out_shape=[jax.ShapeDtypeStruct((t, d), F32), jax.ShapeDtypeStruct((t, d), BF16)],
        scratch_shapes=[pltpu.VMEM((POOL_HALO + tm, d), F32)],
        compiler_params=_cp(("parallel",)),
    )(h, h, g, poolw, scale)


def _mm_tn(a, b, *, name, ta, tb, tt, blocked_out=False, out_dtype=F32):
    t, ka = a.shape
    n = b.shape[1]
    ta, tb, tt = min(ta, ka), min(tb, n), min(tt, t)
    nt = t // tt

    def body(a_ref, b_ref, o_ref, acc):
        @pl.when(pl.program_id(2) == 0)
        def _():
            acc[...] = jnp.zeros_like(acc)
        acc[...] += _dot_tn(a_ref[...].astype(BF16), b_ref[...].astype(BF16))

        @pl.when(pl.program_id(2) == nt - 1)
        def _():
            o_ref[...] = acc[...].astype(out_dtype)

    if blocked_out:
        assert ta == ka
        out_shape = jax.ShapeDtypeStruct((n // tb, ka, tb), out_dtype)
        out_spec = pl.BlockSpec((None, ta, tb), lambda i, j, k: (j, i, 0))
    else:
        out_shape = jax.ShapeDtypeStruct((ka, n), out_dtype)
        out_spec = pl.BlockSpec((ta, tb), lambda i, j, k: (i, j))
    return pl.pallas_call(
        body, name=name, grid=(ka // ta, n // tb, nt),
        in_specs=[pl.BlockSpec((tt, ta), lambda i, j, k: (k, i)), pl.BlockSpec((tt, tb), lambda i, j, k: (k, j))],
        out_specs=out_spec, out_shape=out_shape, scratch_shapes=[pltpu.VMEM((ta, tb), F32)],
        compiler_params=_cp(("parallel", "parallel", "arbitrary")),
    )(a, b)


def _mlp_bwd(dho, h, a, g, wup, wdown, *, name, tm=512):
    t, d = h.shape
    n_blk, _, fb = wup.shape
    f = n_blk * fb
    tm = min(tm, t)

    def body(do_ref, h_ref, a_ref, g_ref, wu_ref, wd_ref, dh_ref, da_ref, dg_ref):
        @pl.when(pl.program_id(0) == 0)
        def _():
            dg_ref[...] = jnp.zeros_like(dg_ref)
        dho_v = do_ref[...]
        dob = dho_v.astype(BF16)
        dn = jnp.zeros((tm, d), F32)
        for k in range(n_blk):
            dz = _dot_nt(dob, wd_ref[k * fb:(k + 1) * fb, :])
            da = (dz * (2.0 * jnp.maximum(a_ref[:, k * fb:(k + 1) * fb].astype(F32), 0.0))).astype(BF16)
            da_ref[:, k * fb:(k + 1) * fb] = da
            dn = dn + _dot_nt(da, wu_ref[k])
        dh, dg = _norm_bwd(dn, h_ref[...], g_ref[...])
        dh_ref[...] = dho_v + dh
        dg_ref[...] += dg

    row = lambda n_: pl.BlockSpec((tm, n_), lambda i: (i, 0))
    return pl.pallas_call(
        body, name=name, grid=(t // tm,),
        in_specs=[row(d), row(d), row(f), _const_spec((1, d)), _const_spec(wup.shape), _const_spec(wdown.shape)],
        out_specs=[row(d), row(f), pl.BlockSpec((8, d), lambda i: (0, 0))],
        out_shape=[jax.ShapeDtypeStruct((t, d), F32), jax.ShapeDtypeStruct((t, f), BF16),
                   jax.ShapeDtypeStruct((8, d), F32)],
        compiler_params=_cp(("arbitrary",)),
    )(dho, h, a, g, wup, wdown)


def _pool_bwd(dho, h, pooled, g, poolw, scale, *, tm=512):
    t, d = h.shape
    tm = min(tm, t)
    ng = len(POOL_WINDOWS)
    cg = d // ng
    nsteps = t // tm

    def body(do_ref, dn_ref, h_ref, p_ref, g_ref, w_ref, s_ref, dh_ref, dw_ref, ds_ref, dg_ref, ext):
        i = pl.program_id(0)

        @pl.when(i == 0)
        def _():
            dw_ref[...] = jnp.zeros_like(dw_ref)
            ds_ref[...] = jnp.zeros_like(ds_ref)
            dg_ref[...] = jnp.zeros_like(dg_ref)
        dho_v = do_ref[...]
        sv = s_ref[...]
        dyp = (dho_v * sv).astype(BF16)
        dyp_halo = (dn_ref[...] * sv).astype(BF16)
        inv = _pool_inv_count(i, tm)
        tnext = ((i + 1) * tm + lax.broadcasted_iota(jnp.int32, (POOL_HALO, 1), 0) + 1).astype(F32)
        last = i == nsteps - 1
        ypre_parts, dpooled_parts = [], []
        for gi, w in enumerate(POOL_WINDOWS):
            cs = slice(gi * cg, (gi + 1) * cg)
            pg = p_ref[:, cs]
            ypre_parts.append(_dot(pg, w_ref[gi]))
            dw_ref[gi] += _dot_tn(pg, dyp[:, cs])
            dpool = _dot_nt(dyp[:, cs], w_ref[gi])
            dpooled_parts.append(dpool)
            ext[0:tm, cs] = dpool * inv[gi]
            dpool_halo = _dot_nt(dyp_halo[:, cs], w_ref[gi]) * (1.0 / jnp.minimum(tnext, float(w)))
            ext[tm:tm + POOL_HALO, cs] = jnp.where(last, 0.0, dpool_halo)
        ds_ref[...] += _rows8(dho_v * jnp.concatenate(ypre_parts, axis=1))
        dn_parts = []
        for gi, w in enumerate(POOL_WINDOWS):
            cs = slice(gi * cg, (gi + 1) * cg)
            s = ext[0:tm, cs]
            for j in range(1, w):
                s = s + ext[j:j + tm, cs]
            dn_parts.append(s - dpooled_parts[gi])
        dh, dg = _norm_bwd(jnp.concatenate(dn_parts, axis=1), h_ref[...], g_ref[...])
        dh_ref[...] = dho_v + dh
        dg_ref[...] += dg

    row = lambda: pl.BlockSpec((tm, d), lambda i: (i, 0))
    acc8 = lambda: pl.BlockSpec((8, d), lambda i: (0, 0))
    return pl.pallas_call(
        body, name="pool_bwd", grid=(nsteps,),
        in_specs=[row(), pl.BlockSpec((POOL_HALO, d), _next_halo(tm, POOL_HALO, t)), row(), row(),
                  _const_spec((1, d)), _const_spec(poolw.shape), _const_spec((1, d))],
        out_specs=[row(), pl.BlockSpec((ng, cg, cg), lambda i: (0, 0, 0)), acc8(), acc8()],
        out_shape=[jax.ShapeDtypeStruct((t, d), F32), jax.ShapeDtypeStruct((ng, cg, cg), F32),
                   jax.ShapeDtypeStruct((8, d), F32), jax.ShapeDtypeStruct((8, d), F32)],
        scratch_shapes=[pltpu.VMEM((tm + POOL_HALO, d), F32)],
        compiler_params=_cp(("arbitrary",)),
    )(dho, dho, h, pooled, g, poolw, scale)


def _outproj_bwd(dh, o, wout, *, tm=512):
    t, d = dh.shape
    tm = min(tm, t)

    def body(dh_ref, o_ref, w_ref, da_ref, dat_ref, dc_ref):
        dhb = dh_ref[...].astype(BF16)
        dc_ref[...] = _dot_nt(dhb, w_ref[ATTN_W:, :])
        for p in range(ATTN_W // PAIR):
            datt = _dot_nt(dhb, w_ref[p * PAIR:(p + 1) * PAIR, :])
            prod = datt * o_ref[:, p * PAIR:(p + 1) * PAIR].astype(F32)
            for hh in range(2):
                lane, head, aux = _head_lanes(hh)
                delta = jnp.sum(jnp.where(head, prod, 0.0), axis=1, keepdims=True)
                aug = _put_pieces(lane, aux + AUX_BIAS, -delta, jnp.where(head, datt, 0.0))
                da_ref[2 * p + hh] = aug.astype(BF16)
                dat_ref[2 * p + hh] = aug.astype(BF16).T

    row = lambda n_: pl.BlockSpec((tm, n_), lambda i: (i, 0))
    return pl.pallas_call(
        body, name="outproj_bwd", grid=(t // tm,),
        in_specs=[row(d), row(ATTN_W), _const_spec(wout.shape)],
        out_specs=[pl.BlockSpec((N_HEADS, tm, PAIR), lambda i: (0, i, 0)),
                   pl.BlockSpec((N_HEADS, PAIR, tm), lambda i: (0, 0, i)), row(CONV_CH)],
        out_shape=[jax.ShapeDtypeStruct((N_HEADS, t, PAIR), BF16), jax.ShapeDtypeStruct((N_HEADS, PAIR, t), BF16),
                   jax.ShapeDtypeStruct((t, CONV_CH), F32)],
        compiler_params=_cp(("parallel",)),
    )(dh, o, wout)


def _conv_bwd(bcx, dcv, conv_w, *, tm=512):
    t = bcx.shape[0]
    tm = min(tm, t)
    ch = CONV_CH
    nsteps = t // tm

    def body(b_ref, c_ref, x_ref, hc_ref, hx_ref, d_ref, nb_ref, nd_ref, w_ref, o_ref, dw_ref, ext_u, ext_d):
        i = pl.program_id(0)

        @pl.when(i == 0)
        def _():
            dw_ref[...] = jnp.zeros_like(dw_ref)
        b, c, x, dcv_v = b_ref[...].astype(F32), c_ref[...].astype(F32), x_ref[...].astype(F32), d_ref[...]
        ext_u[0:CONV_HALO, :] = jnp.where(i == 0, 0.0, hc_ref[...].astype(F32) * hx_ref[...].astype(F32))
        ext_u[CONV_HALO:CONV_HALO + tm, :] = c * x
        dconv = dcv_v * b
        ext_d[0:tm, :] = dconv
        ext_d[tm:tm + CONV_HALO, :] = jnp.where(i == nsteps - 1, 0.0, nd_ref[...] * nb_ref[...].astype(F32))
        u = [ext_u[CONV_HALO - 2 + k:CONV_HALO - 2 + k + tm, :] for k in range(3)]
        conv = w_ref[0:1, :] * u[0] + w_ref[1:2, :] * u[1] + w_ref[2:3, :] * u[2]
        du = (w_ref[2:3, :] * dconv + w_ref[1:2, :] * ext_d[1:1 + tm, :] + w_ref[0:1, :] * ext_d[2:2 + tm, :])
        o_ref[:, 0:ch] = (dcv_v * conv).astype(BF16)
        o_ref[:, ch:2 * ch] = (du * x).astype(BF16)
        o_ref[:, 2 * ch:3 * ch] = (du * c).astype(BF16)
        for k in range(3):
            dw_ref[k] += _rows8(dconv * u[k])

    col = lambda k: pl.BlockSpec((tm, ch), lambda i: (i, k))
    prev = lambda k: pl.BlockSpec((CONV_HALO, ch), lambda i: (_prev_halo(tm, CONV_HALO)(i)[0], k))
    nxt = lambda k: pl.BlockSpec((CONV_HALO, ch), lambda i: (_next_halo(tm, CONV_HALO, t)(i)[0], k))
    return pl.pallas_call(
        body, name="conv_bwd", grid=(nsteps,),
        in_specs=[col(0), col(1), col(2), prev(1), prev(2), col(0), nxt(0), nxt(0), _const_spec((8, ch))],
        out_specs=[pl.BlockSpec((tm, 3 * ch), lambda i: (i, 0)), pl.BlockSpec((3, 8, ch), lambda i: (0, 0, 0))],
        out_shape=[jax.ShapeDtypeStruct((t, 3 * ch), BF16), jax.ShapeDtypeStruct((3, 8, ch), F32)],
        scratch_shapes=[pltpu.VMEM((CONV_HALO + tm, ch), F32), pltpu.VMEM((tm + CONV_HALO, ch), F32)],
        compiler_params=_cp(("arbitrary",)),
    )(bcx, bcx, bcx, bcx, bcx, dcv, bcx, dcv, conv_w)


def _attn_bwd(q_bwd, do_aug, q_bwd_t, do_aug_t, k_aug, v_aug, gblocks, *, tq=1024):
    t = q_bwd.shape[1]
    tq = min(tq, t)
    tk = tq // 2
    nq, nk = t // tq, t // tk
    n_pairs = ATTN_W // PAIR
    n_g = len(gblocks)

    def body(q_ref, do_ref, qt_ref, dot_ref, k_ref, v_ref, *rest):
        dq_ref, dqx_ref, dk_ref, dkx_ref, dv_ref = rest[n_g:n_g + 5]
        dq_scr = rest[2 * n_g + 5]
        scatter = _Exchange(rest[:n_g], rest[n_g + 5:2 * n_g + 5], *rest[2 * n_g + 6:], gather=False)
        j = pl.program_id(1)

        @pl.when((pl.program_id(0) == 0) & (j == 0))
        def _():
            scatter.start()

        @pl.when(j == 0)
        def _():
            dq_scr[...] = jnp.zeros_like(dq_scr)
        k = [k_ref[0], k_ref[1]]
        v = [v_ref[0], v_ref[1]]

        def step(i, carry, diag, rows=tq, row0=0):
            qs = pl.multiple_of(i * tq + row0, tk)
            if diag:
                row = lax.broadcasted_iota(jnp.int32, (rows, tk), 0)
                col = lax.broadcasted_iota(jnp.int32, (rows, tk), 1)
            out = []
            for hh in range(2):
                dk_a, dv_a = carry[hh]
                q = q_ref[hh, pl.ds(qs, rows), :]
                dov = do_ref[hh, pl.ds(qs, rows), :]
                p = jnp.exp2(_dot_nt(q, k[hh]))
                if diag:
                    p = jnp.where(col + (j * tk - i * tq - row0) <= row, p, 0.0)
                ds = (p * _dot_nt(dov, v[hh])).astype(BF16)
                dv_a = dv_a + _dot(dot_ref[hh, :, pl.ds(qs, rows)], p.astype(BF16))
                dk_a = dk_a + _dot(qt_ref[hh, :, pl.ds(qs, rows)], ds)
                dq_scr[hh, pl.ds(qs, rows), :] += _dot(ds, k[hh])
                out.append((dk_a, dv_a))
            return tuple(out)

        zero = (jnp.zeros((PAIR, tk), F32), jnp.zeros((PAIR, tk), F32))
        carry = lax.cond(j % 2 == 0, lambda c: step(j // 2, c, True),
                         lambda c: step(j // 2, c, True, rows=tk, row0=tk), (zero, zero))
        (dk0, dv0), (dk1, dv1) = lax.fori_loop(j // 2 + 1, nq, functools.partial(step, diag=False), carry)
        first_t = lax.broadcasted_iota(jnp.int32, (PAIR, 1), 0) < HEAD_DIM
        first = lax.broadcasted_iota(jnp.int32, (1, PAIR), 1) < HEAD_DIM
        dk_ref[...] = (jnp.where(first_t, dk0, dk1) * (1.0 / LOG2E)).astype(BF16).T
        dkx_ref[...] = jnp.where(first_t, dk1, dk0).T
        dv_ref[...] = jnp.where(first_t, dv0, dv1).astype(BF16).T

        @pl.when(j == nk - 1)
        def _():
            dq_ref[...] = (jnp.where(first, dq_scr[0], dq_scr[1]) * Q_SCALE).astype(BF16)
            dqx_ref[...] = jnp.where(first, dq_scr[1], dq_scr[0])

        @pl.when((pl.program_id(0) == n_pairs - 1) & (j == nk - 1))
        def _():
            scatter.wait()

    resident = lambda: pl.BlockSpec((2, t, PAIR), lambda p, j: (p, 0, 0), pipeline_mode=pl.Buffered(1))
    resident_t = lambda: pl.BlockSpec((2, PAIR, t), lambda p, j: (p, 0, 0), pipeline_mode=pl.Buffered(1))
    kv_in = lambda: pl.BlockSpec((2, tk, PAIR), lambda p, j: (p, j, 0))
    whole = lambda: pl.BlockSpec((t, PAIR), lambda p, j: (0, p))
    tile = lambda: pl.BlockSpec((tk, PAIR), lambda p, j: (j, p))
    b16 = jax.ShapeDtypeStruct((t, ATTN_W), BF16)
    f32 = jax.ShapeDtypeStruct((t, ATTN_W), F32)
    res = pl.pallas_call(
        body, name="attn_bwd", grid=(n_pairs, nk),
        in_specs=[resident(), resident(), resident_t(), resident_t(), kv_in(), kv_in()] + [HBM_SPEC] * n_g,
        out_specs=[whole(), whole(), tile(), tile(), tile()] + [HBM_SPEC] * n_g,
        out_shape=[b16, f32, b16, f32, b16] + [jax.ShapeDtypeStruct(g.shape, g.dtype) for g in gblocks],
        scratch_shapes=[pltpu.VMEM((2, t, PAIR), F32)] + _Exchange.scratch(n_g),
        compiler_params=_cp(("arbitrary", "arbitrary")),
    )(q_bwd, do_aug, q_bwd_t, do_aug_t, k_aug, v_aug, *gblocks)
    return res[:5], res[5:]


def _fgate_bwd(dqx, dkx, sgate, *, tm=256):
    t = sgate.shape[0]
    tm = min(tm, t)
    nsteps = t // tm

    def body(dq_ref, dk_ref, sg_ref, df_ref, dbf_ref, carry):
        @pl.when(pl.program_id(0) == 0)
        def _():
            carry[...] = jnp.zeros_like(carry)
            dbf_ref[...] = jnp.zeros_like(dbf_ref)
        lane = lax.broadcasted_iota(jnp.int32, (ATTN_W, F_PAD), 0)
        head = lax.broadcasted_iota(jnp.int32, (ATTN_W, F_PAD), 1)
        aux = (head // 2) * PAIR + HEAD_DIM * (1 - head % 2)
        valid = head < N_HEADS
        pick_r = (valid & (lane == aux + AUX_ROWSUM)).astype(F32)
        pick_c = (valid & (lane == aux + AUX_BIAS)).astype(F32)
        hp = lax.Precision.HIGHEST
        dcum = (jnp.dot(dq_ref[...], pick_r, preferred_element_type=F32, precision=lax.Precision.HIGH)
                + jnp.dot(dk_ref[...], pick_c, preferred_element_type=F32, precision=lax.Precision.HIGH))
        r = lax.broadcasted_iota(jnp.int32, (tm, tm), 0)
        c = lax.broadcasted_iota(jnp.int32, (tm, tm), 1)
        tri = (c >= r).astype(F32)
        rc = jnp.dot(tri, dcum, preferred_element_type=F32, precision=hp) + carry[...]
        carry[...] = rc[0:1, :]
        df = rc * sg_ref[...]
        df_ref[...] = df.astype(BF16)
        dbf_ref[...] += _rows8(df)

    rev = lambda i: nsteps - 1 - i
    return pl.pallas_call(
        body, name="fgate_bwd", grid=(nsteps,),
        in_specs=[pl.BlockSpec((tm, ATTN_W), lambda i: (rev(i), 0)), pl.BlockSpec((tm, ATTN_W), lambda i: (rev(i), 0)),
                  pl.BlockSpec((tm, F_PAD), lambda i: (rev(i), 0))],
        out_specs=[pl.BlockSpec((tm, F_PAD), lambda i: (rev(i), 0)), pl.BlockSpec((8, F_PAD), lambda i: (0, 0))],
        out_shape=[jax.ShapeDtypeStruct((t, F_PAD), BF16), jax.ShapeDtypeStruct((8, F_PAD), F32)],
        scratch_shapes=[pltpu.VMEM((1, F_PAD), F32)],
        compiler_params=_cp(("arbitrary",)),
    )(dqx, dkx, sgate)


def _inproj_bwd(dq, dk, dv, df, dbcx, dh, x, g, win_p, gblock, *, tm=512):
    t, d = x.shape
    tm = min(tm, t)
    nsteps = t // tm
    n_qkv = 3 * ATTN_W

    def body(dq_ref, dk_ref, dv_ref, df_ref, db_ref, dh_ref, x_ref, g_ref, w_ref, gb_ref, gx_ref, dg_ref, land_ref,
             *sems):
        scatter = _Exchange([gb_ref], [land_ref], *sems, gather=False)

        @pl.when(pl.program_id(0) == 0)
        def _():
            scatter.start()
            dg_ref[...] = jnp.zeros_like(dg_ref)
        dn = _dot_nt(df_ref[...], w_ref[:, n_qkv:n_qkv + F_PAD])
        for k, r in enumerate((dq_ref, dk_ref, dv_ref)):
            dn = dn + _dot_nt(r[...], w_ref[:, k * ATTN_W:(k + 1) * ATTN_W])
        for k in range(3):
            c0 = n_qkv + F_PAD + k * CONV_CH
            dn = dn + _dot_nt(db_ref[:, k * CONV_CH:(k + 1) * CONV_CH], w_ref[:, c0:c0 + CONV_CH])
        dx, dg = _norm_bwd(dn, x_ref[...], g_ref[...])
        gx_ref[...] = dh_ref[...] + dx
        dg_ref[...] += dg

        @pl.when(pl.program_id(0) == nsteps - 1)
        def _():
            scatter.wait()

    row = lambda n_: pl.BlockSpec((tm, n_), lambda i: (i, 0))
    return pl.pallas_call(
        body, name="inproj_bwd", grid=(nsteps,),
        in_specs=[row(ATTN_W), row(ATTN_W), row(ATTN_W), row(F_PAD), row(3 * CONV_CH), row(d), row(d),
                  _const_spec((1, d)), _const_spec(win_p.shape), HBM_SPEC],
        out_specs=[row(d), pl.BlockSpec((8, d), lambda i: (0, 0)), HBM_SPEC],
        out_shape=[jax.ShapeDtypeStruct((t, d), F32), jax.ShapeDtypeStruct((8, d), F32),
                   jax.ShapeDtypeStruct(gblock.shape, gblock.dtype)],
        scratch_shapes=_Exchange.scratch(1),
        compiler_params=_cp(("arbitrary",)),
    )(dq, dk, dv, df, dbcx, dh, x, g, win_p, gblock)


LATE = ("w_out_0", "w_up_0", "w_down_0", "pool_w_1", "w_up_1", "w_down_1")


def _local_step(x, target, gains, b_f, conv_w, pool_scale, win_p, shards):
    d = x.shape[1]
    n0, qkv, flog, bcx, cv = _norm_inproj(x, gains["mix0"], win_p, conv_w)
    q_aug_t, k_aug, v_aug, v_aug_t, sgate = _fgate_prep(flog, b_f, qkv)
    att, q_bwd, q_bwd_t, gathered = _attn_fwd(q_aug_t, k_aug, v_aug_t, [shards[n] for n in LATE])
    g = dict(zip(LATE, gathered))
    wout = g["w_out_0"].reshape(d, d)
    wup0, wup1 = g["w_up_0"], g["w_up_1"]
    wdown0, wdown1 = g["w_down_0"].reshape(-1, d), g["w_down_1"].reshape(-1, d)
    n_grp = len(POOL_WINDOWS)
    cg = d // n_grp
    poolw = g["pool_w_1"].reshape(N_DEV, n_grp, cg // N_DEV, cg).transpose(1, 0, 2, 3).reshape(n_grp, cg, cg)
    h1 = _outproj(att, cv, x, wout)
    h2, n1, a0, z0 = _mlp_fwd(h1, gains["ffn0"], wup0, wdown0, name="mlp_fwd0")
    h3, pooled = _pool_fwd(h2, gains["mix1"], poolw, pool_scale)
    loss, dh4, dg_final, n3, a1, z1 = _mlp_fwd_loss(h3, gains["ffn1"], wup1, wdown1, gains["final"], target,
                                                    name="mlp_fwd1")
    f = a1.shape[1]
    fb = f // N_DEV
    dh3, da1, dg_ffn1 = _mlp_bwd(dh4, h3, a1, gains["ffn1"], wup1, wdown1, name="mlp_bwd1")
    dwdown1 = _mm_tn(z1, dh4, name="dwdown1", ta=1024, tb=1024, tt=2048, out_dtype=BF16)
    dwup1 = _mm_tn(n3, da1, name="dwup1", ta=d, tb=fb, tt=4096, blocked_out=True, out_dtype=BF16)
    dh2, dpoolw, dscale, dg_mix1 = _pool_bwd(dh3, h2, pooled, gains["mix1"], poolw, pool_scale)
    dh1, da0, dg_ffn0 = _mlp_bwd(dh2, h1, a0, gains["ffn0"], wup0, wdown0, name="mlp_bwd0")
    dwdown0 = _mm_tn(z0, dh2, name="dwdown0", ta=1024, tb=1024, tt=2048, out_dtype=BF16)
    dwup0 = _mm_tn(n1, da0, name="dwup0", ta=d, tb=fb, tt=4096, blocked_out=True, out_dtype=BF16)
    do_aug, do_aug_t, dcv = _outproj_bwd(dh1, att, wout)
    dwout = jnp.concatenate([_mm_tn(att, dh1, name="dwout_att", ta=512, tb=1024, tt=2048, out_dtype=BF16),
                             _mm_tn(cv, dh1, name="dwout_conv", ta=512, tb=1024, tt=2048, out_dtype=BF16)], axis=0)
    dbcx, dconvw = _conv_bwd(bcx, dcv, conv_w)
    gblocks = {
        "w_out_0": dwout.reshape(N_DEV, d // N_DEV, d), "w_up_0": dwup0, "w_up_1": dwup1,
        "w_down_0": dwdown0.reshape(N_DEV, -1, d), "w_down_1": dwdown1.reshape(N_DEV, -1, d),
        "pool_w_1": dpoolw.astype(BF16).reshape(n_grp, N_DEV, cg // N_DEV, cg).transpose(1, 0, 2, 3).reshape(
            N_DEV, n_grp * (cg // N_DEV), cg),
    }
    (dq, dqx, dk, dkx, dv), landed = _attn_bwd(q_bwd, do_aug, q_bwd_t, do_aug_t, k_aug, v_aug,
                                               [gblocks[n] for n in LATE])
    df, dbf = _fgate_bwd(dqx, dkx, sgate)
    dwin = jnp.concatenate(
        [_mm_tn(n0, dq, name="dwin_q", ta=d, tb=512, tt=4096, out_dtype=BF16),
         _mm_tn(n0, dk, name="dwin_k", ta=d, tb=512, tt=4096, out_dtype=BF16),
         _mm_tn(n0, dv, name="dwin_v", ta=d, tb=512, tt=4096, out_dtype=BF16),
         _mm_tn(n0, df, name="dwin_f", ta=d, tb=128, tt=2048, out_dtype=BF16)[:, :N_HEADS],
         _mm_tn(n0, dbcx, name="dwin_bcx", ta=d, tb=512, tt=4096, out_dtype=BF16)], axis=1)
    dwin_blocks = dwin.reshape(d, N_DEV, dwin.shape[1] // N_DEV).transpose(1, 0, 2)
    grad_x, dg_mix0, landed_win = _inproj_bwd(dq, dk, dv, df, dbcx, dh1, x, gains["mix0"], win_p, dwin_blocks)
    small = dict(mix0=dg_mix0, ffn0=dg_ffn0, mix1=dg_mix1, pool_scale=dscale, ffn1=dg_ffn1, final=dg_final,
                 b_f=dbf, conv_w=dconvw)
    return loss, grad_x, dict(zip(LATE + ("w_in_0",), tuple(landed) + (landed_win,))), small


def _mesh_places():
    x, y, c = lax.axis_index("x"), lax.axis_index("y"), lax.axis_index("c")
    chips = [(1 - x, y), (x, 1 - y), (1 - x, 1 - y)]
    return (x, y, c), (x, y, 1 - c), chips


def _all_gather(shards):
    n = len(shards)

    def body(*refs):
        ins, outs = refs[:n], refs[n:2 * n]
        send_sems, recv_sems, local_sems = refs[2 * n:]
        me, sib, chips = _mesh_places()
        c = me[2]

        def copy(ai, k, block, to, src=None):
            dst = outs[ai].at[_slot(*block)]
            return pltpu.make_async_remote_copy(
                src_ref=dst if src is None else src, dst_ref=dst, send_sem=send_sems.at[7 * ai + k],
                recv_sem=recv_sems.at[7 * ai + k], device_id=to, device_id_type=MESH)

        mine = [pltpu.make_async_copy(ins[ai], outs[ai].at[_slot(*me)], local_sems.at[ai]) for ai in range(n)]
        for cp in mine:
            cp.start()
        first = []
        for ai in range(n):
            first.append(copy(ai, 0, me, sib, src=ins[ai]))
            first += [copy(ai, 1 + j, me, (*chip, c), src=ins[ai]) for j, chip in enumerate(chips)]
        for cp in first:
            cp.start()
        passed = []
        for ai in range(n):
            for j, chip in enumerate(chips):
                copy(ai, 1 + j, (*chip, c), me).wait_recv()
                cp = copy(ai, 4 + j, (*chip, c), sib)
                cp.start()
                passed.append(cp)
        for ai in range(n):
            copy(ai, 0, sib, me).wait_recv()
            for j, chip in enumerate(chips):
                copy(ai, 4 + j, (*chip, 1 - c), me).wait_recv()
        for cp in first + passed:
            cp.wait_send()
        for cp in mine:
            cp.wait()

    return pl.pallas_call(
        body, name="all_gather",
        in_specs=[HBM_SPEC] * n, out_specs=[HBM_SPEC] * n,
        out_shape=[jax.ShapeDtypeStruct((N_DEV,) + s.shape, s.dtype) for s in shards],
        scratch_shapes=[pltpu.SemaphoreType.DMA((7 * n,)), pltpu.SemaphoreType.DMA((7 * n,)),
                        pltpu.SemaphoreType.DMA((n,))],
    )(*shards)


SMALL_ROWS = 16


def _small_allreduce(parts):
    n, _, w = parts.shape
    assert n <= SMALL_ROWS

    def body(p_ref, o_ref, gath, send_sems, recv_sems):
        x, y, c = lax.axis_index("x"), lax.axis_index("y"), lax.axis_index("c")
        my = _slot(x, y, c)
        rows = [jnp.sum(p_ref[i], axis=0, keepdims=True) for i in range(n)]
        rows.append(jnp.zeros((SMALL_ROWS - n, w), F32))
        gath[my] = jnp.concatenate(rows, axis=0)
        copies = []
        for k in range(1, N_DEV):
            px, py, pc = x ^ (k >> 2), y ^ ((k >> 1) & 1), c ^ (k & 1)
            cp = pltpu.make_async_remote_copy(
                src_ref=gath.at[my], dst_ref=gath.at[my], send_sem=send_sems.at[k - 1], recv_sem=recv_sems.at[k - 1],
                device_id=(px, py, pc), device_id_type=MESH)
            cp.start()
            copies.append(cp)
        for cp in copies:
            cp.wait()
        acc = gath[0]
        for d in range(1, N_DEV):
            acc = acc + gath[d]
        o_ref[...] = acc

    return pl.pallas_call(
        body, name="small_allreduce",
        in_specs=[VMEM_SPEC], out_specs=VMEM_SPEC,
        out_shape=jax.ShapeDtypeStruct((SMALL_ROWS, w), F32),
        scratch_shapes=[pltpu.VMEM((N_DEV, SMALL_ROWS, w), F32), pltpu.SemaphoreType.DMA((N_DEV - 1,)),
                        pltpu.SemaphoreType.DMA((N_DEV - 1,))],
    )(parts)


def _adamw(g, w, m, v, *, name, tm=256):
    r, c = g.shape
    tm = tm if r % tm == 0 else r
    bc1 = 1.0 - ADAM_B1 ** ADAM_STEP
    bc2 = 1.0 - ADAM_B2 ** ADAM_STEP

    def body(g_ref, w_ref, m_ref, v_ref, d_ref, nm_ref, nv_ref):
        gv = g_ref[...]
        nm = ADAM_B1 * m_ref[...] + (1.0 - ADAM_B1) * gv
        nv = ADAM_B2 * v_ref[...] + (1.0 - ADAM_B2) * jnp.square(gv)
        nm_ref[...] = nm
        nv_ref[...] = nv
        d_ref[...] = -ADAM_LR * ((nm / bc1) / (jnp.sqrt(nv / bc2) + ADAM_EPS) + ADAM_WD * w_ref[...])

    blk = pl.BlockSpec((tm, c), lambda i: (i, 0))
    shp = jax.ShapeDtypeStruct((r, c), F32)
    return pl.pallas_call(
        body, name=name, grid=(r // tm,), in_specs=[blk] * 4, out_specs=[blk] * 3, out_shape=[shp] * 3,
        compiler_params=_cp(("parallel",)),
    )(g, w, m, v)


def _adamw_sum(parts, w, m, v, *, name, tm=128):
    _, r, c = parts.shape
    tm = tm if r % tm == 0 else r
    bc1 = 1.0 - ADAM_B1 ** ADAM_STEP
    bc2 = 1.0 - ADAM_B2 ** ADAM_STEP

    def body(p_ref, w_ref, m_ref, v_ref, g_ref, d_ref, nm_ref, nv_ref):
        gv = p_ref[0].astype(F32)
        for k in range(1, N_DEV):
            gv = gv + p_ref[k].astype(F32)
        g_ref[...] = gv
        nm = ADAM_B1 * m_ref[...] + (1.0 - ADAM_B1) * gv
        nv = ADAM_B2 * v_ref[...] + (1.0 - ADAM_B2) * jnp.square(gv)
        nm_ref[...] = nm
        nv_ref[...] = nv
        d_ref[...] = -ADAM_LR * ((nm / bc1) / (jnp.sqrt(nv / bc2) + ADAM_EPS) + ADAM_WD * w_ref[...])

    blk = pl.BlockSpec((tm, c), lambda i: (i, 0))
    shp = jax.ShapeDtypeStruct((r, c), F32)
    return pl.pallas_call(
        body, name=name, grid=(r // tm,), in_specs=[pl.BlockSpec((N_DEV, tm, c), lambda i: (0, i, 0))] + [blk] * 3,
        out_specs=[blk] * 4, out_shape=[shp] * 4, compiler_params=_cp(("parallel",)),
    )(parts, w, m, v)


BIG = ("w_in_0", "w_out_0", "w_up_0", "w_down_0", "pool_w_1", "w_up_1", "w_down_1")
SMALL = ("norm_mix_0", "norm_ffn_0", "norm_mix_1", "pool_scale_1", "norm_ffn_1", "final_norm", "b_f_0", "conv_w_0")
WEIGHTS = ("norm_mix_0", "w_in_0", "b_f_0", "conv_w_0", "w_out_0", "norm_ffn_0", "w_up_0", "w_down_0", "norm_mix_1",
           "pool_w_1", "pool_scale_1", "norm_ffn_1", "w_up_1", "w_down_1", "final_norm")


def _pad_to(a, rows, cols):
    return jnp.pad(a, ((0, rows - a.shape[0]), (0, cols - a.shape[1])))


def _pack_small(p, width):
    rows = [p[n].reshape(1, -1) for n in SMALL[:6]]
    rows.append(_pad_to(p["b_f_0"].reshape(1, -1), 1, width))
    rows.append(_pad_to(p["conv_w_0"], 3, width))
    return _pad_to(jnp.concatenate(rows, axis=0), SMALL_ROWS, width)


def _unpack_small(a, like):
    out = {n: a[i] for i, n in enumerate(SMALL[:6])}
    out["b_f_0"] = a[6, :like["b_f_0"].shape[0]]
    out["conv_w_0"] = a[7:10, :like["conv_w_0"].shape[1]]
    return out


def kernel(x, norm_mix_0, w_in_0, b_f_0, conv_w_0, w_out_0, norm_ffn_0, w_up_0, w_down_0, norm_mix_1, pool_w_1, pool_scale_1, norm_ffn_1, w_up_1, w_down_1, final_norm, loss_target, m_norm_mix_0, m_w_in_0, m_b_f_0, m_conv_w_0, m_w_out_0, m_norm_ffn_0, m_w_up_0, m_w_down_0, m_norm_mix_1, m_pool_w_1, m_pool_scale_1, m_norm_ffn_1, m_w_up_1, m_w_down_1, m_final_norm, v_norm_mix_0, v_w_in_0, v_b_f_0, v_conv_w_0, v_w_out_0, v_norm_ffn_0, v_w_up_0, v_w_down_0, v_norm_mix_1, v_pool_w_1, v_pool_scale_1, v_norm_ffn_1, v_w_up_1, v_w_down_1, v_final_norm):
    w = dict(norm_mix_0=norm_mix_0, w_in_0=w_in_0, b_f_0=b_f_0, conv_w_0=conv_w_0, w_out_0=w_out_0,
             norm_ffn_0=norm_ffn_0, w_up_0=w_up_0, w_down_0=w_down_0, norm_mix_1=norm_mix_1, pool_w_1=pool_w_1,
             pool_scale_1=pool_scale_1, norm_ffn_1=norm_ffn_1, w_up_1=w_up_1, w_down_1=w_down_1, final_norm=final_norm)
    m = dict(norm_mix_0=m_norm_mix_0, w_in_0=m_w_in_0, b_f_0=m_b_f_0, conv_w_0=m_conv_w_0, w_out_0=m_w_out_0,
             norm_ffn_0=m_norm_ffn_0, w_up_0=m_w_up_0, w_down_0=m_w_down_0, norm_mix_1=m_norm_mix_1,
             pool_w_1=m_pool_w_1, pool_scale_1=m_pool_scale_1, norm_ffn_1=m_norm_ffn_1, w_up_1=m_w_up_1,
             w_down_1=m_w_down_1, final_norm=m_final_norm)
    v = dict(norm_mix_0=v_norm_mix_0, w_in_0=v_w_in_0, b_f_0=v_b_f_0, conv_w_0=v_conv_w_0, w_out_0=v_w_out_0,
             norm_ffn_0=v_norm_ffn_0, w_up_0=v_w_up_0, w_down_0=v_w_down_0, norm_mix_1=v_norm_mix_1,
             pool_w_1=v_pool_w_1, pool_scale_1=v_pool_scale_1, norm_ffn_1=v_norm_ffn_1, w_up_1=v_w_up_1,
             w_down_1=v_w_down_1, final_norm=v_final_norm)
    d = x.shape[-1]
    n_in = w_in_0.shape[1] * N_DEV
    n_qkv = 3 * ATTN_W
    pool_g, pool_rows, pool_c = pool_w_1.shape

    def shard2d(p):
        return {n: (p[n].reshape(pool_g * pool_rows, pool_c) if n == "pool_w_1" else p[n]) for n in BIG}
    w2, m2, v2 = shard2d(w), shard2d(m), shard2d(v)

    conv_cols = conv_w_0.shape[1]
    win_g8, conv_g8 = _all_gather([w_in_0.astype(BF16), _pad_to(conv_w_0, 8, 128)])
    conv_full = conv_g8[:, :, :conv_cols].transpose(1, 0, 2).reshape(8, N_DEV * conv_cols)
    win = win_g8.transpose(1, 0, 2).reshape(d, n_in)
    win_p = jnp.concatenate([win[:, :n_qkv], _pad_to(win[:, n_qkv:n_qkv + N_HEADS], d, F_PAD),
                             win[:, n_qkv + N_HEADS:]], axis=1)

    gains = dict(mix0=norm_mix_0.reshape(1, d), ffn0=norm_ffn_0.reshape(1, d), mix1=norm_mix_1.reshape(1, d),
                 ffn1=norm_ffn_1.reshape(1, d), final=final_norm.reshape(1, d))
    dev = _slot(lax.axis_index("x"), lax.axis_index("y"), lax.axis_index("c"))
    loss8, grad_x, landed, small = _local_step(
        x[0], loss_target[0], gains, _pad_to(b_f_0.reshape(1, -1), 1, F_PAD), conv_full, pool_scale_1.reshape(1, d),
        win_p, {n: w2[n].astype(BF16) for n in LATE})
    parts = jnp.concatenate(
        [small[k][None] for k in ("mix0", "ffn0", "mix1", "pool_scale", "ffn1", "final")]
        + [_pad_to(small["b_f"], 8, d)[None], jnp.pad(small["conv_w"], ((0, 0), (0, 0), (0, d - CONV_CH))),
           _pad_to(loss8[0:1, 0:1], 8, d)[None]], axis=0)
    tot = _small_allreduce(parts)
    loss = tot[10, 0]
    conv_g = lax.dynamic_slice(tot, (7, dev * conv_cols), (3, conv_cols))
    gs = tot.at[7:10].set(_pad_to(conv_g, 3, d))

    grads, deltas, new_m, new_v = {}, {}, {}, {}
    for n in BIG:
        gr, dl, nm, nv = _adamw_sum(landed[n], w2[n], m2[n], v2[n], name="adamw_" + n)
        for dst, val in ((grads, gr), (deltas, dl), (new_m, nm), (new_v, nv)):
            dst[n] = val.reshape(w[n].shape)
    dl, nm, nv = _adamw(gs, _pack_small(w, d), _pack_small(m, d), _pack_small(v, d), name="adamw_small")
    for dst, val in ((grads, gs), (deltas, dl), (new_m, nm), (new_v, nv)):
        dst.update(_unpack_small(val, w))
    return (loss, grad_x[None], *[grads[n] for n in WEIGHTS], *[deltas[n] for n in WEIGHTS],
            *[new_m[n] for n in WEIGHTS], *[new_v[n] for n in WEIGHTS])
```

```python
import functools

import jax
import jax.numpy as jnp
from jax import lax
from jax.experimental import pallas as pl
from jax.experimental.pallas import tpu as pltpu

F32 = jnp.float32
BF16 = jnp.bfloat16

N_DEV = 8
N_HEADS = 8
HEAD_DIM = 64
PAIR = 2 * HEAD_DIM
ATTN_W = N_HEADS * HEAD_DIM
CONV_CH = 512
F_PAD = 128
POOL_WINDOWS = (2, 4, 8, 16)
POOL_HALO = 16
CONV_HALO = 16
RMS_EPS = 1e-6
Q_SCALE = HEAD_DIM ** -0.5
LOG2E = 1.4426950408889634
NEG = -1e30
AUX_BIAS = 0
AUX_LSE = 3
AUX_ROWSUM = 6
ADAM_LR, ADAM_B1, ADAM_B2, ADAM_EPS, ADAM_WD, ADAM_STEP = 0.001, 0.9, 0.999, 1e-08, 0.01, 10
MESH = pl.DeviceIdType.MESH
VMEM_LIMIT = 56 * 2**20


def _cp(sem=None, vmem=VMEM_LIMIT, **kw):
    return pltpu.CompilerParams(dimension_semantics=sem, vmem_limit_bytes=vmem, **kw)


def _dot(a, b):
    return jnp.dot(a, b, preferred_element_type=F32)


def _dot_nt(a, b):
    return lax.dot_general(a, b, (((1,), (1,)), ((), ())), preferred_element_type=F32)


def _dot_tn(a, b):
    return lax.dot_general(a, b, (((0,), (0,)), ((), ())), preferred_element_type=F32)


def _rstd(h):
    return lax.rsqrt(jnp.mean(h * h, axis=-1, keepdims=True) + RMS_EPS)


def _rows8(x):
    r, n = x.shape
    return jnp.sum(x.reshape(r // 8, 8, n), axis=0)


def _norm_bwd(dn, h, g):
    r = _rstd(h)
    xhat = h * r
    dy = dn * g
    dh = r * (dy - xhat * jnp.mean(dy * xhat, axis=-1, keepdims=True))
    return dh, _rows8(dn * xhat)


def _const_spec(shape):
    nd = len(shape)
    return pl.BlockSpec(shape, lambda *_: (0,) * nd, pipeline_mode=pl.Buffered(1))


HBM_SPEC = pl.BlockSpec(memory_space=pltpu.HBM)
VMEM_SPEC = pl.BlockSpec(memory_space=pltpu.VMEM)


def _slot(px, py, pc):
    return 4 * px + 2 * py + pc


class _Exchange:
    def __init__(self, srcs, dsts, send_sems, recv_sems, local_sems, gather):
        x, y, c = lax.axis_index("x"), lax.axis_index("y"), lax.axis_index("c")
        me = _slot(x, y, c)
        self.copies = []
        for a, (src, dst) in enumerate(zip(srcs, dsts)):
            self.copies.append(pltpu.make_async_copy(src if gather else src.at[me], dst.at[me], local_sems.at[a]))
            for k in range(1, N_DEV):
                px, py, pc = x ^ (k >> 2), y ^ ((k >> 1) & 1), c ^ (k & 1)
                self.copies.append(pltpu.make_async_remote_copy(
                    src_ref=src if gather else src.at[_slot(px, py, pc)], dst_ref=dst.at[me],
                    send_sem=send_sems.at[(N_DEV - 1) * a + k - 1], recv_sem=recv_sems.at[(N_DEV - 1) * a + k - 1],
                    device_id=(px, py, pc), device_id_type=MESH))

    def start(self):
        for cp in self.copies:
            cp.start()

    def wait(self):
        for cp in self.copies:
            cp.wait()

    @staticmethod
    def scratch(n):
        return [pltpu.SemaphoreType.DMA(((N_DEV - 1) * n,)), pltpu.SemaphoreType.DMA(((N_DEV - 1) * n,)),
                pltpu.SemaphoreType.DMA((n,))]


def _norm_inproj(x, g, win_p, conv_w, *, tm=512):
    t, d = x.shape
    n_all = win_p.shape[1]
    n_qkv = 3 * ATTN_W
    n_bcx = 3 * CONV_CH
    assert n_all == n_qkv + F_PAD + n_bcx
    tm = min(tm, t)
    ch = CONV_CH

    def body(x_ref, g_ref, w_ref, cw_ref, n_ref, qkv_ref, f_ref, bcx_ref, cv_ref, ext):
        h = x_ref[...]
        n = (h * _rstd(h) * g_ref[...]).astype(BF16)
        n_ref[...] = n
        for c0 in range(0, n_qkv, 512):
            acc = _dot(n, w_ref[:, c0:c0 + 512])
            if c0 < ATTN_W:
                acc = acc * (Q_SCALE * LOG2E)
            qkv_ref[:, c0:c0 + 512] = acc.astype(BF16)
        f_ref[...] = _dot(n, w_ref[:, n_qkv:n_qkv + F_PAD])
        bcx = []
        for k in range(3):
            c0 = n_qkv + F_PAD + k * ch
            v = _dot(n, w_ref[:, c0:c0 + ch]).astype(BF16)
            bcx_ref[:, k * ch:(k + 1) * ch] = v
            bcx.append(v.astype(F32))
        @pl.when(pl.program_id(0) == 0)
        def _():
            ext[tm:tm + CONV_HALO, :] = jnp.zeros((CONV_HALO, ch), F32)
        ext[0:CONV_HALO, :] = ext[tm:tm + CONV_HALO, :]
        ext[CONV_HALO:CONV_HALO + tm, :] = bcx[1] * bcx[2]
        conv = (cw_ref[0:1, :] * ext[CONV_HALO - 2:CONV_HALO - 2 + tm, :]
                + cw_ref[1:2, :] * ext[CONV_HALO - 1:CONV_HALO - 1 + tm, :]
                + cw_ref[2:3, :] * ext[CONV_HALO:CONV_HALO + tm, :])
        cv_ref[...] = (bcx[0] * conv).astype(BF16)

    return pl.pallas_call(
        body, name="norm_inproj", grid=(t // tm,),
        in_specs=[pl.BlockSpec((tm, d), lambda i: (i, 0)), _const_spec((1, d)), _const_spec((d, n_all)),
                  _const_spec((8, ch))],
        out_specs=[pl.BlockSpec((tm, d), lambda i: (i, 0)), pl.BlockSpec((tm, n_qkv), lambda i: (i, 0)),
                   pl.BlockSpec((tm, F_PAD), lambda i: (i, 0)), pl.BlockSpec((tm, n_bcx), lambda i: (i, 0)),
                   pl.BlockSpec((tm, ch), lambda i: (i, 0))],
        out_shape=[jax.ShapeDtypeStruct((t, d), BF16), jax.ShapeDtypeStruct((t, n_qkv), BF16),
                   jax.ShapeDtypeStruct((t, F_PAD), F32), jax.ShapeDtypeStruct((t, n_bcx), BF16),
                   jax.ShapeDtypeStruct((t, ch), BF16)],
        scratch_shapes=[pltpu.VMEM((CONV_HALO + tm, ch), F32)],
        compiler_params=_cp(("arbitrary",)),
    )(x, g, win_p, conv_w)


def _head_lanes(h):
    lane = lax.broadcasted_iota(jnp.int32, (1, PAIR), 1)
    hh = h % 2
    return lane, lane // HEAD_DIM == hh, HEAD_DIM * (1 - hh)


def _pieces(col):
    hi = col.astype(BF16).astype(F32)
    r1 = col - hi
    mid = r1.astype(BF16).astype(F32)
    lo = (r1 - mid).astype(BF16).astype(F32)
    return hi, mid, lo


def _put_pieces(lane, first, col, other):
    hi, mid, lo = _pieces(col)
    return jnp.where(lane == first, hi, jnp.where(lane == first + 1, mid, jnp.where(lane == first + 2, lo, other)))


def _fgate_prep(flog, b_f, qkv, *, tm=512):
    t = flog.shape[0]
    tm = min(tm, t)

    def body(f_ref, b_ref, qkv_ref, qat_ref, ka_ref, va_ref, vat_ref, sg_ref, carry):
        @pl.when(pl.program_id(0) == 0)
        def _():
            carry[...] = jnp.zeros_like(carry)
        z = f_ref[...] + b_ref[...]
        e = jnp.exp(-jnp.abs(z))
        logf = jnp.minimum(z, 0.0) - jnp.log(1.0 + e)
        sg_ref[...] = jnp.where(z >= 0, e, 1.0) / (1.0 + e)
        r = lax.broadcasted_iota(jnp.int32, (tm, tm), 0)
        c = lax.broadcasted_iota(jnp.int32, (tm, tm), 1)
        tri = (c <= r).astype(F32)
        cs = jnp.dot(tri, logf, preferred_element_type=F32, precision=lax.Precision.HIGHEST) + carry[...]
        carry[...] = cs[tm - 1:tm, :]
        cs2 = cs * LOG2E
        for h in range(N_HEADS):
            lane, head, aux = _head_lanes(h)
            p0 = (h // 2) * PAIR
            ones = ((lane >= aux + AUX_LSE) & (lane <= aux + AUX_ROWSUM)).astype(F32)
            bias = (lane >= aux + AUX_BIAS) & (lane < aux + AUX_BIAS + 3)
            k_aux = _put_pieces(lane, aux + AUX_BIAS, cs2[:, h:h + 1], ones)
            q_aug = jnp.where(head, qkv_ref[:, p0:p0 + PAIR].astype(F32), jnp.where(bias, -1.0, 0.0))
            v_aug = jnp.where(head, qkv_ref[:, 2 * ATTN_W + p0:2 * ATTN_W + p0 + PAIR].astype(F32),
                              jnp.where(bias, 1.0, 0.0))
            qat_ref[h] = q_aug.astype(BF16).T
            ka_ref[h] = jnp.where(head, qkv_ref[:, ATTN_W + p0:ATTN_W + p0 + PAIR], k_aux.astype(BF16))
            va_ref[h] = v_aug.astype(BF16)
            vat_ref[h] = v_aug.astype(BF16).T

    aug = lambda: pl.BlockSpec((N_HEADS, tm, PAIR), lambda i: (0, i, 0))
    aug_t = lambda: pl.BlockSpec((N_HEADS, PAIR, tm), lambda i: (0, 0, i))
    aug_shape = jax.ShapeDtypeStruct((N_HEADS, t, PAIR), BF16)
    aug_t_shape = jax.ShapeDtypeStruct((N_HEADS, PAIR, t), BF16)
    return pl.pallas_call(
        body, name="fgate_prep", grid=(t // tm,),
        in_specs=[pl.BlockSpec((tm, F_PAD), lambda i: (i, 0)), _const_spec((1, F_PAD)),
                  pl.BlockSpec((tm, 3 * ATTN_W), lambda i: (i, 0))],
        out_specs=[aug_t(), aug(), aug(), aug_t(), pl.BlockSpec((tm, F_PAD), lambda i: (i, 0))],
        out_shape=[aug_t_shape, aug_shape, aug_shape, aug_t_shape, jax.ShapeDtypeStruct((t, F_PAD), F32)],
        scratch_shapes=[pltpu.VMEM((1, F_PAD), F32)],
        compiler_params=_cp(("arbitrary",)),
    )(flog, b_f, qkv)


def _put_pieces_t(row, first, vec, other):
    hi, mid, lo = _pieces(vec)
    return jnp.where(row == first, hi, jnp.where(row == first + 1, mid, jnp.where(row == first + 2, lo, other)))


def _attn_fwd(q_aug_t, k_aug, v_aug_t, shards, *, tq=1024):
    t = k_aug.shape[1]
    tq = min(tq, t)
    tk = tq // 2
    nq = t // tq
    n_pairs = ATTN_W // PAIR
    n_sh = len(shards)

    def body(qt_ref, k_ref, vt_ref, *rest):
        o_ref, qb_ref, qbt_ref = rest[n_sh:n_sh + 3]
        s_scr = rest[2 * n_sh + 3]
        gather = _Exchange(rest[:n_sh], rest[n_sh + 3:2 * n_sh + 3], *rest[2 * n_sh + 4:], gather=True)
        i = pl.program_id(1)

        @pl.when((pl.program_id(0) == 0) & (i == 0))
        def _():
            gather.start()
        key = lax.broadcasted_iota(jnp.int32, (tk, tq), 0)
        qry = lax.broadcasted_iota(jnp.int32, (tk, tq), 1)
        qt = [qt_ref[0], qt_ref[1]]

        def logits(hh, tile, slot, diag):
            s = _dot(k_ref[hh, pl.ds(pl.multiple_of(tile * tk, tk), tk), :], qt[hh])
            if diag:
                s = jnp.where(key + (tile * tk - i * tq) <= qry, s, NEG)
            s_scr[hh, slot] = s
            return jnp.max(s, axis=0, keepdims=True)

        def probs(hh, tile, slot, m, acc, tmax):
            mn = jnp.maximum(m, tmax)
            p = jnp.exp2(s_scr[hh, slot] - mn).astype(BF16)
            acc = jnp.exp2(m - mn) * acc + _dot(vt_ref[hh, :, pl.ds(pl.multiple_of(tile * tk, tk), tk)], p)
            return mn, acc

        def advance(carry, prev, slot, nxt, diag=False):
            out = []
            for hh in range(2):
                m, acc, tmax = carry[hh]
                m, acc = probs(hh, prev, slot, m, acc, tmax)
                out.append((m, acc, logits(hh, nxt, 1 - slot, diag)))
            return tuple(out)

        def two_tiles(jj, carry):
            carry = advance(carry, jnp.where(jj == 0, 2 * i, 2 * jj - 1), 1, 2 * jj)
            return advance(carry, 2 * jj, 0, 2 * jj + 1)

        init = tuple((jnp.full((1, tq), NEG, F32), jnp.zeros((PAIR, tq), F32), logits(hh, 2 * i + 1, 0, True))
                     for hh in range(2))
        carry = advance(init, 2 * i + 1, 0, 2 * i, diag=True)
        carry = lax.fori_loop(0, i, two_tiles, carry)
        last = jnp.where(i == 0, 2 * i, 2 * i - 1)
        row = lax.broadcasted_iota(jnp.int32, (PAIR, 1), 0)
        res = []
        for hh in range(2):
            aux = HEAD_DIM * (1 - hh)
            m, acc, tmax = carry[hh]
            m, acc = probs(hh, last, 1, m, acc, tmax)
            l = acc[aux + AUX_BIAS:aux + AUX_BIAS + 1, :]
            qbt = _put_pieces_t(row, aux + AUX_LSE, -(m + jnp.log2(l)), qt[hh].astype(F32))
            qbt_ref[hh] = qbt.astype(BF16)
            qb_ref[hh] = qbt.astype(BF16).T
            res.append(acc * (1.0 / l))
        o_ref[...] = jnp.where(row < HEAD_DIM, res[0], res[1]).astype(BF16).T

        @pl.when((pl.program_id(0) == n_pairs - 1) & (i == nq - 1))
        def _():
            gather.wait()

    res = pl.pallas_call(
        body, name="attn_fwd", grid=(n_pairs, nq),
        in_specs=[pl.BlockSpec((2, PAIR, tq), lambda p, i: (p, 0, i)),
                  pl.BlockSpec((2, t, PAIR), lambda p, i: (p, 0, 0), pipeline_mode=pl.Buffered(1)),
                  pl.BlockSpec((2, PAIR, t), lambda p, i: (p, 0, 0), pipeline_mode=pl.Buffered(1))] + [HBM_SPEC] * n_sh,
        out_specs=[pl.BlockSpec((tq, PAIR), lambda p, i: (i, p)),
                   pl.BlockSpec((2, tq, PAIR), lambda p, i: (p, i, 0)),
                   pl.BlockSpec((2, PAIR, tq), lambda p, i: (p, 0, i))] + [HBM_SPEC] * n_sh,
        out_shape=[jax.ShapeDtypeStruct((t, ATTN_W), BF16), jax.ShapeDtypeStruct((N_HEADS, t, PAIR), BF16),
                   jax.ShapeDtypeStruct((N_HEADS, PAIR, t), BF16)]
        + [jax.ShapeDtypeStruct((N_DEV,) + s.shape, s.dtype) for s in shards],
        scratch_shapes=[pltpu.VMEM((2, 2, tk, tq), F32)] + _Exchange.scratch(n_sh),
        compiler_params=_cp(("arbitrary", "arbitrary")),
    )(q_aug_t, k_aug, v_aug_t, *shards)
    return res[0], res[1], res[2], res[3:]


def _prev_halo(tm, halo):
    return lambda i: (jnp.maximum(i * (tm // halo) - 1, 0), 0)


def _next_halo(tm, halo, t):
    return lambda i: (jnp.minimum((i + 1) * (tm // halo), t // halo - 1), 0)


def _mlp_tile(hh, g_ref, wu_ref, wd_ref, n_ref, a_ref, z_ref):
    n_blk, _, fb = wu_ref.shape
    n = (hh * _rstd(hh) * g_ref[...]).astype(BF16)
    n_ref[...] = n
    acc = hh
    for k in range(n_blk):
        a = _dot(n, wu_ref[k])
        zz = jnp.square(jnp.maximum(a, 0.0)).astype(BF16)
        a_ref[:, k * fb:(k + 1) * fb] = a.astype(BF16)
        z_ref[:, k * fb:(k + 1) * fb] = zz
        acc = acc + _dot(zz, wd_ref[k * fb:(k + 1) * fb, :])
    return acc


def _outproj(att, cv, x, wout, *, tm=512):
    t, d = x.shape
    tm = min(tm, t)

    def body(a_ref, c_ref, x_ref, w_ref, h_ref):
        h_ref[...] = x_ref[...] + _dot(a_ref[...], w_ref[0:ATTN_W, :]) + _dot(c_ref[...], w_ref[ATTN_W:, :])

    return pl.pallas_call(
        body, name="outproj", grid=(t // tm,),
        in_specs=[pl.BlockSpec((tm, ATTN_W), lambda i: (i, 0)), pl.BlockSpec((tm, CONV_CH), lambda i: (i, 0)),
                  pl.BlockSpec((tm, d), lambda i: (i, 0)), _const_spec(wout.shape)],
        out_specs=pl.BlockSpec((tm, d), lambda i: (i, 0)),
        out_shape=jax.ShapeDtypeStruct((t, d), F32),
        compiler_params=_cp(("parallel",)),
    )(att, cv, x, wout)


def _mlp_fwd(h, g, wup, wdown, *, name, tm=512):
    t, d = h.shape
    n_blk, _, fb = wup.shape
    f = n_blk * fb
    tm = min(tm, t)

    def body(h_ref, g_ref, wu_ref, wd_ref, ho_ref, n_ref, a_ref, z_ref):
        ho_ref[...] = _mlp_tile(h_ref[...], g_ref, wu_ref, wd_ref, n_ref, a_ref, z_ref)

    row = lambda n_: pl.BlockSpec((tm, n_), lambda i: (i, 0))
    return pl.pallas_call(
        body, name=name, grid=(t // tm,),
        in_specs=[row(d), _const_spec((1, d)), _const_spec(wup.shape), _const_spec(wdown.shape)],
        out_specs=[row(d), row(d), row(f), row(f)],
        out_shape=[jax.ShapeDtypeStruct((t, d), F32), jax.ShapeDtypeStruct((t, d), BF16),
                   jax.ShapeDtypeStruct((t, f), BF16), jax.ShapeDtypeStruct((t, f), BF16)],
        compiler_params=_cp(("parallel",)),
    )(h, g, wup, wdown)


def _mlp_fwd_loss(h, g, wup, wdown, g_out, target, *, name, tm=512):
    t, d = h.shape
    n_blk, _, fb = wup.shape
    f = n_blk * fb
    tm = min(tm, t)
    nsteps = t // tm

    def body(h_ref, g_ref, wu_ref, wd_ref, go_ref, y_ref, loss_ref, dh_ref, dg_ref, n_ref, a_ref, z_ref, lacc):
        i = pl.program_id(0)

        @pl.when(i == 0)
        def _():
            lacc[...] = jnp.zeros_like(lacc)
            dg_ref[...] = jnp.zeros_like(dg_ref)
        hv = _mlp_tile(h_ref[...], g_ref, wu_ref, wd_ref, n_ref, a_ref, z_ref)
        gv = go_ref[...]
        r = _rstd(hv)
        xhat = hv * r
        err = xhat * gv - y_ref[...]
        lacc[...] += _rows8(err * err)
        dout = err * (1.0 / d)
        dy = dout * gv
        dg_ref[...] += _rows8(dout * xhat)
        dh_ref[...] = r * (dy - xhat * jnp.mean(dy * xhat, axis=-1, keepdims=True))

        @pl.when(i == nsteps - 1)
        def _():
            loss_ref[...] = jnp.full(loss_ref.shape, (0.5 / d) * jnp.sum(lacc[...]), F32)

    row = lambda n_: pl.BlockSpec((tm, n_), lambda i: (i, 0))
    return pl.pallas_call(
        body, name=name, grid=(nsteps,),
        in_specs=[row(d), _const_spec((1, d)), _const_spec(wup.shape), _const_spec(wdown.shape), _const_spec((1, d)),
                  row(d)],
        out_specs=[pl.BlockSpec((8, 128), lambda i: (0, 0)), row(d), pl.BlockSpec((8, d), lambda i: (0, 0)),
                   row(d), row(f), row(f)],
        out_shape=[jax.ShapeDtypeStruct((8, 128), F32), jax.ShapeDtypeStruct((t, d), F32),
                   jax.ShapeDtypeStruct((8, d), F32), jax.ShapeDtypeStruct((t, d), BF16),
                   jax.ShapeDtypeStruct((t, f), BF16), jax.ShapeDtypeStruct((t, f), BF16)],
        scratch_shapes=[pltpu.VMEM((8, d), F32)],
        compiler_params=_cp(("arbitrary",)),
    )(h, g, wup, wdown, g_out, target)


def _pool_inv_count(i, tm):
    tglob = (i * tm + lax.broadcasted_iota(jnp.int32, (tm, 1), 0) + 1).astype(F32)
    return [1.0 / jnp.minimum(tglob, float(w)) for w in POOL_WINDOWS]


def _pool_fwd(h, g, poolw, scale, *, tm=512):
    t, d = h.shape
    tm = min(tm, t)
    cg = d // len(POOL_WINDOWS)

    def body(h_ref, hh_ref, g_ref, w_ref, s_ref, ho_ref, p_ref, ext):
        i = pl.program_id(0)
        hv = h_ref[...]
        halo = hh_ref[...]
        n = hv * _rstd(hv) * g_ref[...]
        ext[0:POOL_HALO, :] = jnp.where(i == 0, 0.0, halo * _rstd(halo) * g_ref[...])
        ext[POOL_HALO:POOL_HALO + tm, :] = n
        inv = _pool_inv_count(i, tm)
        for gi, w in enumerate(POOL_WINDOWS):
            cs = slice(gi * cg, (gi + 1) * cg)
            s = ext[POOL_HALO:POOL_HALO + tm, cs]
            for j in range(1, w):
                s = s + ext[POOL_HALO - j:POOL_HALO - j + tm, cs]
            pooled = (s * inv[gi] - n[:, cs]).astype(BF16)
            p_ref[:, cs] = pooled
            ho_ref[:, cs] = hv[:, cs] + _dot(pooled, w_ref[gi]) * s_ref[:, cs]

    row = lambda: pl.BlockSpec((tm, d), lambda i: (i, 0))
    return pl.pallas_call(
        body, name="pool_fwd", grid=(t // tm,),
        in_specs=[row(), pl.BlockSpec((POOL_HALO, d), _prev_halo(tm, POOL_HALO)), _const_spec((1, d)),
                  _const_spec(poolw.shape), _const_spec((1, d))],
        out_specs=[row(), row()],
        out_shape=[jax.ShapeDtypeStruct((t, d), F32), jax.ShapeDtypeStruct((t, d), BF16)],
        scratch_shapes=[pltpu.VMEM((POOL_HALO + tm, d), F32)],
        compiler_params=_cp(("parallel",)),
    )(h, h, g, poolw, scale)


def _mm_tn(a, b, *, name, ta, tb, tt, blocked_out=False, out_dtype=F32):
    t, ka = a.shape
    n = b.shape[1]
    ta, tb, tt = min(ta, ka), min(tb, n), min(tt, t)
    nt = t // tt

    def body(a_ref, b_ref, o_ref, acc):
        @pl.when(pl.program_id(2) == 0)
        def _():
            acc[...] = jnp.zeros_like(acc)
        acc[...] += _dot_tn(a_ref[...].astype(BF16), b_ref[...].astype(BF16))

        @pl.when(pl.program_id(2) == nt - 1)
        def _():
            o_ref[...] = acc[...].astype(out_dtype)

    if blocked_out:
        assert ta == ka
        out_shape = jax.ShapeDtypeStruct((n // tb, ka, tb), out_dtype)
        out_spec = pl.BlockSpec((None, ta, tb), lambda i, j, k: (j, i, 0))
    else:
        out_shape = jax.ShapeDtypeStruct((ka, n), out_dtype)
        out_spec = pl.BlockSpec((ta, tb), lambda i, j, k: (i, j))
    return pl.pallas_call(
        body, name=name, grid=(ka // ta, n // tb, nt),
        in_specs=[pl.BlockSpec((tt, ta), lambda i, j, k: (k, i)), pl.BlockSpec((tt, tb), lambda i, j, k: (k, j))],
        out_specs=out_spec, out_shape=out_shape, scratch_shapes=[pltpu.VMEM((ta, tb), F32)],
        compiler_params=_cp(("parallel", "parallel", "arbitrary")),
    )(a, b)


def _mm_tn_cat(a_list, b_list, *, name, tt, out_dtype=BF16):
    t = a_list[0].shape[0]
    ta, tb = a_list[0].shape[1], b_list[0].shape[1]
    na, nb = len(a_list), len(b_list)
    tt = min(tt, t)
    nt = t // tt

    def body(*refs):
        a_refs, b_refs, o_ref, acc = refs[:na], refs[na:na + nb], refs[na + nb], refs[na + nb + 1]
        i, j, k = pl.program_id(0), pl.program_id(1), pl.program_id(2)

        @pl.when(k == 0)
        def _():
            acc[...] = jnp.zeros_like(acc)
        for ia in range(na):
            for ib in range(nb):
                @pl.when((i == ia) & (j == ib))
                def _(ia=ia, ib=ib):
                    acc[...] += _dot_tn(a_refs[ia][...].astype(BF16), b_refs[ib][...].astype(BF16))

        @pl.when(k == nt - 1)
        def _():
            o_ref[...] = acc[...].astype(out_dtype)

    def held(m, axis):
        def index(i, j, k):
            cur = (i, j)[axis]
            return (jnp.where(cur == m, k, jnp.where(cur < m, 0, nt - 1)), 0)
        return index

    return pl.pallas_call(
        body, name=name, grid=(na, nb, nt),
        in_specs=[pl.BlockSpec((tt, ta), held(m, 0)) for m in range(na)]
        + [pl.BlockSpec((tt, tb), held(m, 1)) for m in range(nb)],
        out_specs=pl.BlockSpec((ta, tb), lambda i, j, k: (i, j)),
        out_shape=jax.ShapeDtypeStruct((na * ta, nb * tb), out_dtype), scratch_shapes=[pltpu.VMEM((ta, tb), F32)],
        compiler_params=_cp(("arbitrary", "arbitrary", "arbitrary")),
    )(*a_list, *b_list)


def _mlp_bwd(dho, h, a, g, wup, wdown, *, name, tm=512):
    t, d = h.shape
    n_blk, _, fb = wup.shape
    f = n_blk * fb
    tm = min(tm, t)

    def body(do_ref, h_ref, a_ref, g_ref, wu_ref, wd_ref, dh_ref, da_ref, dg_ref):
        @pl.when(pl.program_id(0) == 0)
        def _():
            dg_ref[...] = jnp.zeros_like(dg_ref)
        dho_v = do_ref[...]
        dob = dho_v.astype(BF16)
        dn = jnp.zeros((tm, d), F32)
        for k in range(n_blk):
            dz = _dot_nt(dob, wd_ref[k * fb:(k + 1) * fb, :])
            da = (dz * (2.0 * jnp.maximum(a_ref[:, k * fb:(k + 1) * fb].astype(F32), 0.0))).astype(BF16)
            da_ref[:, k * fb:(k + 1) * fb] = da
            dn = dn + _dot_nt(da, wu_ref[k])
        dh, dg = _norm_bwd(dn, h_ref[...], g_ref[...])
        dh_ref[...] = dho_v + dh
        dg_ref[...] += dg

    row = lambda n_: pl.BlockSpec((tm, n_), lambda i: (i, 0))
    return pl.pallas_call(
        body, name=name, grid=(t // tm,),
        in_specs=[row(d), row(d), row(f), _const_spec((1, d)), _const_spec(wup.shape), _const_spec(wdown.shape)],
        out_specs=[row(d), row(f), pl.BlockSpec((8, d), lambda i: (0, 0))],
        out_shape=[jax.ShapeDtypeStruct((t, d), F32), jax.ShapeDtypeStruct((t, f), BF16),
                   jax.ShapeDtypeStruct((8, d), F32)],
        compiler_params=_cp(("arbitrary",)),
    )(dho, h, a, g, wup, wdown)


def _pool_bwd(dho, h, pooled, g, poolw, scale, *, tm=512):
    t, d = h.shape
    tm = min(tm, t)
    ng = len(POOL_WINDOWS)
    cg = d // ng
    nsteps = t // tm

    def body(do_ref, dn_ref, h_ref, p_ref, g_ref, w_ref, s_ref, dh_ref, dw_ref, ds_ref, dg_ref, ext):
        i = pl.program_id(0)

        @pl.when(i == 0)
        def _():
            dw_ref[...] = jnp.zeros_like(dw_ref)
            ds_ref[...] = jnp.zeros_like(ds_ref)
            dg_ref[...] = jnp.zeros_like(dg_ref)
        dho_v = do_ref[...]
        sv = s_ref[...]
        dyp = (dho_v * sv).astype(BF16)
        dyp_halo = (dn_ref[...] * sv).astype(BF16)
        inv = _pool_inv_count(i, tm)
        tnext = ((i + 1) * tm + lax.broadcasted_iota(jnp.int32, (POOL_HALO, 1), 0) + 1).astype(F32)
        last = i == nsteps - 1
        ypre_parts, dpooled_parts = [], []
        for gi, w in enumerate(POOL_WINDOWS):
            cs = slice(gi * cg, (gi + 1) * cg)
            pg = p_ref[:, cs]
            ypre_parts.append(_dot(pg, w_ref[gi]))
            dw_ref[gi] += _dot_tn(pg, dyp[:, cs])
            dpool = _dot_nt(dyp[:, cs], w_ref[gi])
            dpooled_parts.append(dpool)
            ext[0:tm, cs] = dpool * inv[gi]
            dpool_halo = _dot_nt(dyp_halo[:, cs], w_ref[gi]) * (1.0 / jnp.minimum(tnext, float(w)))
            ext[tm:tm + POOL_HALO, cs] = jnp.where(last, 0.0, dpool_halo)
        ds_ref[...] += _rows8(dho_v * jnp.concatenate(ypre_parts, axis=1))
        dn_parts = []
        for gi, w in enumerate(POOL_WINDOWS):
            cs = slice(gi * cg, (gi + 1) * cg)
            s = ext[0:tm, cs]
            for j in range(1, w):
                s = s + ext[j:j + tm, cs]
            dn_parts.append(s - dpooled_parts[gi])
        dh, dg = _norm_bwd(jnp.concatenate(dn_parts, axis=1), h_ref[...], g_ref[...])
        dh_ref[...] = dho_v + dh
        dg_ref[...] += dg

    row = lambda: pl.BlockSpec((tm, d), lambda i: (i, 0))
    acc8 = lambda: pl.BlockSpec((8, d), lambda i: (0, 0))
    return pl.pallas_call(
        body, name="pool_bwd", grid=(nsteps,),
        in_specs=[row(), pl.BlockSpec((POOL_HALO, d), _next_halo(tm, POOL_HALO, t)), row(), row(),
                  _const_spec((1, d)), _const_spec(poolw.shape), _const_spec((1, d))],
        out_specs=[row(), pl.BlockSpec((ng, cg, cg), lambda i: (0, 0, 0)), acc8(), acc8()],
        out_shape=[jax.ShapeDtypeStruct((t, d), F32), jax.ShapeDtypeStruct((ng, cg, cg), F32),
                   jax.ShapeDtypeStruct((8, d), F32), jax.ShapeDtypeStruct((8, d), F32)],
        scratch_shapes=[pltpu.VMEM((tm + POOL_HALO, d), F32)],
        compiler_params=_cp(("arbitrary",)),
    )(dho, dho, h, pooled, g, poolw, scale)


def _outproj_bwd(dh, o, wout, *, tm=512):
    t, d = dh.shape
    tm = min(tm, t)

    def body(dh_ref, o_ref, w_ref, da_ref, dat_ref, dc_ref):
        dhb = dh_ref[...].astype(BF16)
        dc_ref[...] = _dot_nt(dhb, w_ref[ATTN_W:, :])
        for p in range(ATTN_W // PAIR):
            datt = _dot_nt(dhb, w_ref[p * PAIR:(p + 1) * PAIR, :])
            prod = datt * o_ref[:, p * PAIR:(p + 1) * PAIR].astype(F32)
            for hh in range(2):
                lane, head, aux = _head_lanes(hh)
                delta = jnp.sum(jnp.where(head, prod, 0.0), axis=1, keepdims=True)
                aug = _put_pieces(lane, aux + AUX_BIAS, -delta, jnp.where(head, datt, 0.0))
                da_ref[2 * p + hh] = aug.astype(BF16)
                dat_ref[2 * p + hh] = aug.astype(BF16).T

    row = lambda n_: pl.BlockSpec((tm, n_), lambda i: (i, 0))
    return pl.pallas_call(
        body, name="outproj_bwd", grid=(t // tm,),
        in_specs=[row(d), row(ATTN_W), _const_spec(wout.shape)],
        out_specs=[pl.BlockSpec((N_HEADS, tm, PAIR), lambda i: (0, i, 0)),
                   pl.BlockSpec((N_HEADS, PAIR, tm), lambda i: (0, 0, i)), row(CONV_CH)],
        out_shape=[jax.ShapeDtypeStruct((N_HEADS, t, PAIR), BF16), jax.ShapeDtypeStruct((N_HEADS, PAIR, t), BF16),
                   jax.ShapeDtypeStruct((t, CONV_CH), F32)],
        compiler_params=_cp(("parallel",)),
    )(dh, o, wout)


def _conv_bwd(bcx, dcv, conv_w, *, tm=512):
    t = bcx.shape[0]
    tm = min(tm, t)
    ch = CONV_CH
    nsteps = t // tm

    def body(b_ref, c_ref, x_ref, hc_ref, hx_ref, d_ref, nb_ref, nd_ref, w_ref, o_ref, dw_ref, ext_u, ext_d):
        i = pl.program_id(0)

        @pl.when(i == 0)
        def _():
            dw_ref[...] = jnp.zeros_like(dw_ref)
        b, c, x, dcv_v = b_ref[...].astype(F32), c_ref[...].astype(F32), x_ref[...].astype(F32), d_ref[...]
        ext_u[0:CONV_HALO, :] = jnp.where(i == 0, 0.0, hc_ref[...].astype(F32) * hx_ref[...].astype(F32))
        ext_u[CONV_HALO:CONV_HALO + tm, :] = c * x
        dconv = dcv_v * b
        ext_d[0:tm, :] = dconv
        ext_d[tm:tm + CONV_HALO, :] = jnp.where(i == nsteps - 1, 0.0, nd_ref[...] * nb_ref[...].astype(F32))
        u = [ext_u[CONV_HALO - 2 + k:CONV_HALO - 2 + k + tm, :] for k in range(3)]
        conv = w_ref[0:1, :] * u[0] + w_ref[1:2, :] * u[1] + w_ref[2:3, :] * u[2]
        du = (w_ref[2:3, :] * dconv + w_ref[1:2, :] * ext_d[1:1 + tm, :] + w_ref[0:1, :] * ext_d[2:2 + tm, :])
        o_ref[:, 0:ch] = (dcv_v * conv).astype(BF16)
        o_ref[:, ch:2 * ch] = (du * x).astype(BF16)
        o_ref[:, 2 * ch:3 * ch] = (du * c).astype(BF16)
        for k in range(3):
            dw_ref[k] += _rows8(dconv * u[k])

    col = lambda k: pl.BlockSpec((tm, ch), lambda i: (i, k))
    prev = lambda k: pl.BlockSpec((CONV_HALO, ch), lambda i: (_prev_halo(tm, CONV_HALO)(i)[0], k))
    nxt = lambda k: pl.BlockSpec((CONV_HALO, ch), lambda i: (_next_halo(tm, CONV_HALO, t)(i)[0], k))
    return pl.pallas_call(
        body, name="conv_bwd", grid=(nsteps,),
        in_specs=[col(0), col(1), col(2), prev(1), prev(2), col(0), nxt(0), nxt(0), _const_spec((8, ch))],
        out_specs=[pl.BlockSpec((tm, 3 * ch), lambda i: (i, 0)), pl.BlockSpec((3, 8, ch), lambda i: (0, 0, 0))],
        out_shape=[jax.ShapeDtypeStruct((t, 3 * ch), BF16), jax.ShapeDtypeStruct((3, 8, ch), F32)],
        scratch_shapes=[pltpu.VMEM((CONV_HALO + tm, ch), F32), pltpu.VMEM((tm + CONV_HALO, ch), F32)],
        compiler_params=_cp(("arbitrary",)),
    )(bcx, bcx, bcx, bcx, bcx, dcv, bcx, dcv, conv_w)


def _attn_bwd(q_bwd, do_aug, q_bwd_t, do_aug_t, k_aug, v_aug, gblocks, *, tq=1024):
    t = q_bwd.shape[1]
    tq = min(tq, t)
    tk = tq // 2
    nq, nk = t // tq, t // tk
    n_pairs = ATTN_W // PAIR
    n_g = len(gblocks)

    def body(q_ref, do_ref, qt_ref, dot_ref, k_ref, v_ref, *rest):
        dq_ref, dqx_ref, dk_ref, dkx_ref, dv_ref = rest[n_g:n_g + 5]
        dq_scr = rest[2 * n_g + 5]
        scatter = _Exchange(rest[:n_g], rest[n_g + 5:2 * n_g + 5], *rest[2 * n_g + 6:], gather=False)
        j = pl.program_id(1)

        @pl.when((pl.program_id(0) == 0) & (j == 0))
        def _():
            scatter.start()

        @pl.when(j == 0)
        def _():
            dq_scr[...] = jnp.zeros_like(dq_scr)
        k = [k_ref[0], k_ref[1]]
        v = [v_ref[0], v_ref[1]]

        def step(i, carry, diag, rows=tq, row0=0):
            qs = pl.multiple_of(i * tq + row0, tk)
            if diag:
                row = lax.broadcasted_iota(jnp.int32, (rows, tk), 0)
                col = lax.broadcasted_iota(jnp.int32, (rows, tk), 1)
            out = []
            for hh in range(2):
                dk_a, dv_a = carry[hh]
                q = q_ref[hh, pl.ds(qs, rows), :]
                dov = do_ref[hh, pl.ds(qs, rows), :]
                p = jnp.exp2(_dot_nt(q, k[hh]))
                if diag:
                    p = jnp.where(col + (j * tk - i * tq - row0) <= row, p, 0.0)
                ds = (p * _dot_nt(dov, v[hh])).astype(BF16)
                dv_a = dv_a + _dot(dot_ref[hh, :, pl.ds(qs, rows)], p.astype(BF16))
                dk_a = dk_a + _dot(qt_ref[hh, :, pl.ds(qs, rows)], ds)
                dq_scr[hh, pl.ds(qs, rows), :] += _dot(ds, k[hh])
                out.append((dk_a, dv_a))
            return tuple(out)

        zero = (jnp.zeros((PAIR, tk), F32), jnp.zeros((PAIR, tk), F32))
        carry = lax.cond(j % 2 == 0, lambda c: step(j // 2, c, True),
                         lambda c: step(j // 2, c, True, rows=tk, row0=tk), (zero, zero))
        (dk0, dv0), (dk1, dv1) = lax.fori_loop(j // 2 + 1, nq, functools.partial(step, diag=False), carry)
        first_t = lax.broadcasted_iota(jnp.int32, (PAIR, 1), 0) < HEAD_DIM
        first = lax.broadcasted_iota(jnp.int32, (1, PAIR), 1) < HEAD_DIM
        dk_ref[...] = (jnp.where(first_t, dk0, dk1) * (1.0 / LOG2E)).astype(BF16).T
        dkx_ref[...] = jnp.where(first_t, dk1, dk0).T
        dv_ref[...] = jnp.where(first_t, dv0, dv1).astype(BF16).T

        @pl.when(j == nk - 1)
        def _():
            dq_ref[...] = (jnp.where(first, dq_scr[0], dq_scr[1]) * Q_SCALE).astype(BF16)
            dqx_ref[...] = jnp.where(first, dq_scr[1], dq_scr[0])

        @pl.when((pl.program_id(0) == n_pairs - 1) & (j == nk - 1))
        def _():
            scatter.wait()

    resident = lambda: pl.BlockSpec((2, t, PAIR), lambda p, j: (p, 0, 0), pipeline_mode=pl.Buffered(1))
    resident_t = lambda: pl.BlockSpec((2, PAIR, t), lambda p, j: (p, 0, 0), pipeline_mode=pl.Buffered(1))
    kv_in = lambda: pl.BlockSpec((2, tk, PAIR), lambda p, j: (p, j, 0))
    whole = lambda: pl.BlockSpec((t, PAIR), lambda p, j: (0, p))
    tile = lambda: pl.BlockSpec((tk, PAIR), lambda p, j: (j, p))
    b16 = jax.ShapeDtypeStruct((t, ATTN_W), BF16)
    f32 = jax.ShapeDtypeStruct((t, ATTN_W), F32)
    res = pl.pallas_call(
        body, name="attn_bwd", grid=(n_pairs, nk),
        in_specs=[resident(), resident(), resident_t(), resident_t(), kv_in(), kv_in()] + [HBM_SPEC] * n_g,
        out_specs=[whole(), whole(), tile(), tile(), tile()] + [HBM_SPEC] * n_g,
        out_shape=[b16, f32, b16, f32, b16] + [jax.ShapeDtypeStruct(g.shape, g.dtype) for g in gblocks],
        scratch_shapes=[pltpu.VMEM((2, t, PAIR), F32)] + _Exchange.scratch(n_g),
        compiler_params=_cp(("arbitrary", "arbitrary")),
    )(q_bwd, do_aug, q_bwd_t, do_aug_t, k_aug, v_aug, *gblocks)
    return res[:5], res[5:]


def _fgate_bwd(dqx, dkx, sgate, *, tm=256):
    t = sgate.shape[0]
    tm = min(tm, t)
    nsteps = t // tm

    def body(dq_ref, dk_ref, sg_ref, df_ref, dbf_ref, carry):
        @pl.when(pl.program_id(0) == 0)
        def _():
            carry[...] = jnp.zeros_like(carry)
            dbf_ref[...] = jnp.zeros_like(dbf_ref)
        lane = lax.broadcasted_iota(jnp.int32, (ATTN_W, F_PAD), 0)
        head = lax.broadcasted_iota(jnp.int32, (ATTN_W, F_PAD), 1)
        aux = (head // 2) * PAIR + HEAD_DIM * (1 - head % 2)
        valid = head < N_HEADS
        pick_r = (valid & (lane == aux + AUX_ROWSUM)).astype(F32)
        pick_c = (valid & (lane == aux + AUX_BIAS)).astype(F32)
        hp = lax.Precision.HIGHEST
        dcum = (jnp.dot(dq_ref[...], pick_r, preferred_element_type=F32, precision=lax.Precision.HIGH)
                + jnp.dot(dk_ref[...], pick_c, preferred_element_type=F32, precision=lax.Precision.HIGH))
        r = lax.broadcasted_iota(jnp.int32, (tm, tm), 0)
        c = lax.broadcasted_iota(jnp.int32, (tm, tm), 1)
        tri = (c >= r).astype(F32)
        rc = jnp.dot(tri, dcum, preferred_element_type=F32, precision=hp) + carry[...]
        carry[...] = rc[0:1, :]
        df = rc * sg_ref[...]
        df_ref[...] = df.astype(BF16)
        dbf_ref[...] += _rows8(df)

    rev = lambda i: nsteps - 1 - i
    return pl.pallas_call(
        body, name="fgate_bwd", grid=(nsteps,),
        in_specs=[pl.BlockSpec((tm, ATTN_W), lambda i: (rev(i), 0)), pl.BlockSpec((tm, ATTN_W), lambda i: (rev(i), 0)),
                  pl.BlockSpec((tm, F_PAD), lambda i: (rev(i), 0))],
        out_specs=[pl.BlockSpec((tm, F_PAD), lambda i: (rev(i), 0)), pl.BlockSpec((8, F_PAD), lambda i: (0, 0))],
        out_shape=[jax.ShapeDtypeStruct((t, F_PAD), BF16), jax.ShapeDtypeStruct((8, F_PAD), F32)],
        scratch_shapes=[pltpu.VMEM((1, F_PAD), F32)],
        compiler_params=_cp(("arbitrary",)),
    )(dqx, dkx, sgate)


def _inproj_bwd(dq, dk, dv, df, dbcx, dh, x, g, win_p, gblock, *, tm=512):
    t, d = x.shape
    tm = min(tm, t)
    nsteps = t // tm
    n_qkv = 3 * ATTN_W

    def body(dq_ref, dk_ref, dv_ref, df_ref, db_ref, dh_ref, x_ref, g_ref, w_ref, gb_ref, gx_ref, dg_ref, land_ref,
             *sems):
        scatter = _Exchange([gb_ref], [land_ref], *sems, gather=False)

        @pl.when(pl.program_id(0) == 0)
        def _():
            scatter.start()
            dg_ref[...] = jnp.zeros_like(dg_ref)
        dn = _dot_nt(df_ref[...], w_ref[:, n_qkv:n_qkv + F_PAD])
        for k, r in enumerate((dq_ref, dk_ref, dv_ref)):
            dn = dn + _dot_nt(r[...], w_ref[:, k * ATTN_W:(k + 1) * ATTN_W])
        for k in range(3):
            c0 = n_qkv + F_PAD + k * CONV_CH
            dn = dn + _dot_nt(db_ref[:, k * CONV_CH:(k + 1) * CONV_CH], w_ref[:, c0:c0 + CONV_CH])
        dx, dg = _norm_bwd(dn, x_ref[...], g_ref[...])
        gx_ref[...] = dh_ref[...] + dx
        dg_ref[...] += dg

        @pl.when(pl.program_id(0) == nsteps - 1)
        def _():
            scatter.wait()

    row = lambda n_: pl.BlockSpec((tm, n_), lambda i: (i, 0))
    return pl.pallas_call(
        body, name="inproj_bwd", grid=(nsteps,),
        in_specs=[row(ATTN_W), row(ATTN_W), row(ATTN_W), row(F_PAD), row(3 * CONV_CH), row(d), row(d),
                  _const_spec((1, d)), _const_spec(win_p.shape), HBM_SPEC],
        out_specs=[row(d), pl.BlockSpec((8, d), lambda i: (0, 0)), HBM_SPEC],
        out_shape=[jax.ShapeDtypeStruct((t, d), F32), jax.ShapeDtypeStruct((8, d), F32),
                   jax.ShapeDtypeStruct(gblock.shape, gblock.dtype)],
        scratch_shapes=_Exchange.scratch(1),
        compiler_params=_cp(("arbitrary",)),
    )(dq, dk, dv, df, dbcx, dh, x, g, win_p, gblock)


LATE = ("w_out_0", "w_up_0", "w_down_0", "pool_w_1", "w_up_1", "w_down_1")


def _local_step(x, target, gains, b_f, conv_w, pool_scale, win_p, shards):
    d = x.shape[1]
    n0, qkv, flog, bcx, cv = _norm_inproj(x, gains["mix0"], win_p, conv_w)
    q_aug_t, k_aug, v_aug, v_aug_t, sgate = _fgate_prep(flog, b_f, qkv)
    att, q_bwd, q_bwd_t, gathered = _attn_fwd(q_aug_t, k_aug, v_aug_t, [shards[n] for n in LATE])
    g = dict(zip(LATE, gathered))
    wout = g["w_out_0"].reshape(d, d)
    wup0, wup1 = g["w_up_0"], g["w_up_1"]
    wdown0, wdown1 = g["w_down_0"].reshape(-1, d), g["w_down_1"].reshape(-1, d)
    n_grp = len(POOL_WINDOWS)
    cg = d // n_grp
    poolw = g["pool_w_1"].reshape(N_DEV, n_grp, cg // N_DEV, cg).transpose(1, 0, 2, 3).reshape(n_grp, cg, cg)
    h1 = _outproj(att, cv, x, wout)
    h2, n1, a0, z0 = _mlp_fwd(h1, gains["ffn0"], wup0, wdown0, name="mlp_fwd0")
    h3, pooled = _pool_fwd(h2, gains["mix1"], poolw, pool_scale)
    loss, dh4, dg_final, n3, a1, z1 = _mlp_fwd_loss(h3, gains["ffn1"], wup1, wdown1, gains["final"], target,
                                                    name="mlp_fwd1")
    f = a1.shape[1]
    fb = f // N_DEV
    dh3, da1, dg_ffn1 = _mlp_bwd(dh4, h3, a1, gains["ffn1"], wup1, wdown1, name="mlp_bwd1")
    dwdown1 = _mm_tn(z1, dh4, name="dwdown1", ta=1024, tb=1024, tt=2048, out_dtype=BF16)
    dwup1 = _mm_tn(n3, da1, name="dwup1", ta=d, tb=fb, tt=4096, blocked_out=True, out_dtype=BF16)
    dh2, dpoolw, dscale, dg_mix1 = _pool_bwd(dh3, h2, pooled, gains["mix1"], poolw, pool_scale)
    dh1, da0, dg_ffn0 = _mlp_bwd(dh2, h1, a0, gains["ffn0"], wup0, wdown0, name="mlp_bwd0")
    dwdown0 = _mm_tn(z0, dh2, name="dwdown0", ta=1024, tb=1024, tt=2048, out_dtype=BF16)
    dwup0 = _mm_tn(n1, da0, name="dwup0", ta=d, tb=fb, tt=4096, blocked_out=True, out_dtype=BF16)
    do_aug, do_aug_t, dcv = _outproj_bwd(dh1, att, wout)
    dwout = _mm_tn_cat([att, cv], [dh1], name="dwout", tt=2048)
    dbcx, dconvw = _conv_bwd(bcx, dcv, conv_w)
    gblocks = {
        "w_out_0": dwout.reshape(N_DEV, d // N_DEV, d), "w_up_0": dwup0, "w_up_1": dwup1,
        "w_down_0": dwdown0.reshape(N_DEV, -1, d), "w_down_1": dwdown1.reshape(N_DEV, -1, d),
        "pool_w_1": dpoolw.astype(BF16).reshape(n_grp, N_DEV, cg // N_DEV, cg).transpose(1, 0, 2, 3).reshape(
            N_DEV, n_grp * (cg // N_DEV), cg),
    }
    (dq, dqx, dk, dkx, dv), landed = _attn_bwd(q_bwd, do_aug, q_bwd_t, do_aug_t, k_aug, v_aug,
                                               [gblocks[n] for n in LATE])
    df, dbf = _fgate_bwd(dqx, dkx, sgate)
    dwin = jnp.concatenate(
        [_mm_tn_cat([n0], [dq, dk, dv], name="dwin_qkv", tt=2048),
         _mm_tn(n0, df, name="dwin_f", ta=d, tb=128, tt=2048, out_dtype=BF16)[:, :N_HEADS],
         _mm_tn(n0, dbcx, name="dwin_bcx", ta=d, tb=512, tt=4096, out_dtype=BF16)], axis=1)
    dwin_blocks = dwin.reshape(d, N_DEV, dwin.shape[1] // N_DEV).transpose(1, 0, 2)
    grad_x, dg_mix0, landed_win = _inproj_bwd(dq, dk, dv, df, dbcx, dh1, x, gains["mix0"], win_p, dwin_blocks)
    small = dict(mix0=dg_mix0, ffn0=dg_ffn0, mix1=dg_mix1, pool_scale=dscale, ffn1=dg_ffn1, final=dg_final,
                 b_f=dbf, conv_w=dconvw)
    return loss, grad_x, dict(zip(LATE + ("w_in_0",), tuple(landed) + (landed_win,))), small


def _mesh_places():
    x, y, c = lax.axis_index("x"), lax.axis_index("y"), lax.axis_index("c")
    chips = [(1 - x, y), (x, 1 - y), (1 - x, 1 - y)]
    return (x, y, c), (x, y, 1 - c), chips


def _all_gather(shards):
    n = len(shards)

    def body(*refs):
        ins, outs = refs[:n], refs[n:2 * n]
        send_sems, recv_sems, local_sems = refs[2 * n:]
        me, sib, chips = _mesh_places()
        c = me[2]

        def copy(ai, k, block, to, src=None):
            dst = outs[ai].at[_slot(*block)]
            return pltpu.make_async_remote_copy(
                src_ref=dst if src is None else src, dst_ref=dst, send_sem=send_sems.at[7 * ai + k],
                recv_sem=recv_sems.at[7 * ai + k], device_id=to, device_id_type=MESH)

        mine = [pltpu.make_async_copy(ins[ai], outs[ai].at[_slot(*me)], local_sems.at[ai]) for ai in range(n)]
        for cp in mine:
            cp.start()
        first = []
        for ai in range(n):
            first.append(copy(ai, 0, me, sib, src=ins[ai]))
            first += [copy(ai, 1 + j, me, (*chip, c), src=ins[ai]) for j, chip in enumerate(chips)]
        for cp in first:
            cp.start()
        passed = []
        for ai in range(n):
            for j, chip in enumerate(chips):
                copy(ai, 1 + j, (*chip, c), me).wait_recv()
                cp = copy(ai, 4 + j, (*chip, c), sib)
                cp.start()
                passed.append(cp)
        for ai in range(n):
            copy(ai, 0, sib, me).wait_recv()
            for j, chip in enumerate(chips):
                copy(ai, 4 + j, (*chip, 1 - c), me).wait_recv()
        for cp in first + passed:
            cp.wait_send()
        for cp in mine:
            cp.wait()

    return pl.pallas_call(
        body, name="all_gather",
        in_specs=[HBM_SPEC] * n, out_specs=[HBM_SPEC] * n,
        out_shape=[jax.ShapeDtypeStruct((N_DEV,) + s.shape, s.dtype) for s in shards],
        scratch_shapes=[pltpu.SemaphoreType.DMA((7 * n,)), pltpu.SemaphoreType.DMA((7 * n,)),
                        pltpu.SemaphoreType.DMA((n,))],
    )(*shards)


SMALL_ROWS = 16


def _small_allreduce(parts):
    n, _, w = parts.shape
    assert n <= SMALL_ROWS

    def body(p_ref, o_ref, gath, send_sems, recv_sems):
        x, y, c = lax.axis_index("x"), lax.axis_index("y"), lax.axis_index("c")
        my = _slot(x, y, c)
        rows = [jnp.sum(p_ref[i], axis=0, keepdims=True) for i in range(n)]
        rows.append(jnp.zeros((SMALL_ROWS - n, w), F32))
        gath[my] = jnp.concatenate(rows, axis=0)
        copies = []
        for k in range(1, N_DEV):
            px, py, pc = x ^ (k >> 2), y ^ ((k >> 1) & 1), c ^ (k & 1)
            cp = pltpu.make_async_remote_copy(
                src_ref=gath.at[my], dst_ref=gath.at[my], send_sem=send_sems.at[k - 1], recv_sem=recv_sems.at[k - 1],
                device_id=(px, py, pc), device_id_type=MESH)
            cp.start()
            copies.append(cp)
        for cp in copies:
            cp.wait()
        acc = gath[0]
        for d in range(1, N_DEV):
            acc = acc + gath[d]
        o_ref[...] = acc

    return pl.pallas_call(
        body, name="small_allreduce",
        in_specs=[VMEM_SPEC], out_specs=VMEM_SPEC,
        out_shape=jax.ShapeDtypeStruct((SMALL_ROWS, w), F32),
        scratch_shapes=[pltpu.VMEM((N_DEV, SMALL_ROWS, w), F32), pltpu.SemaphoreType.DMA((N_DEV - 1,)),
                        pltpu.SemaphoreType.DMA((N_DEV - 1,))],
    )(parts)


def _adamw(g, w, m, v, *, name, tm=256):
    r, c = g.shape
    tm = tm if r % tm == 0 else r
    bc1 = 1.0 - ADAM_B1 ** ADAM_STEP
    bc2 = 1.0 - ADAM_B2 ** ADAM_STEP

    def body(g_ref, w_ref, m_ref, v_ref, d_ref, nm_ref, nv_ref):
        gv = g_ref[...]
        nm = ADAM_B1 * m_ref[...] + (1.0 - ADAM_B1) * gv
        nv = ADAM_B2 * v_ref[...] + (1.0 - ADAM_B2) * jnp.square(gv)
        nm_ref[...] = nm
        nv_ref[...] = nv
        d_ref[...] = -ADAM_LR * ((nm / bc1) / (jnp.sqrt(nv / bc2) + ADAM_EPS) + ADAM_WD * w_ref[...])

    blk = pl.BlockSpec((tm, c), lambda i: (i, 0))
    shp = jax.ShapeDtypeStruct((r, c), F32)
    return pl.pallas_call(
        body, name=name, grid=(r // tm,), in_specs=[blk] * 4, out_specs=[blk] * 3, out_shape=[shp] * 3,
        compiler_params=_cp(("parallel",)),
    )(g, w, m, v)


def _adamw_sum(parts, w, m, v, *, name, tm=128):
    _, r, c = parts.shape
    tm = tm if r % tm == 0 else r
    bc1 = 1.0 - ADAM_B1 ** ADAM_STEP
    bc2 = 1.0 - ADAM_B2 ** ADAM_STEP

    def body(p_ref, w_ref, m_ref, v_ref, g_ref, d_ref, nm_ref, nv_ref):
        gv = p_ref[0].astype(F32)
        for k in range(1, N_DEV):
            gv = gv + p_ref[k].astype(F32)
        g_ref[...] = gv
        nm = ADAM_B1 * m_ref[...] + (1.0 - ADAM_B1) * gv
        nv = ADAM_B2 * v_ref[...] + (1.0 - ADAM_B2) * jnp.square(gv)
        nm_ref[...] = nm
        nv_ref[...] = nv
        d_ref[...] = -ADAM_LR * ((nm / bc1) / (jnp.sqrt(nv / bc2) + ADAM_EPS) + ADAM_WD * w_ref[...])

    blk = pl.BlockSpec((tm, c), lambda i: (i, 0))
    shp = jax.ShapeDtypeStruct((r, c), F32)
    return pl.pallas_call(
        body, name=name, grid=(r // tm,), in_specs=[pl.BlockSpec((N_DEV, tm, c), lambda i: (0, i, 0))] + [blk] * 3,
        out_specs=[blk] * 4, out_shape=[shp] * 4, compiler_params=_cp(("parallel",)),
    )(parts, w, m, v)


BIG = ("w_in_0", "w_out_0", "w_up_0", "w_down_0", "pool_w_1", "w_up_1", "w_down_1")
SMALL = ("norm_mix_0", "norm_ffn_0", "norm_mix_1", "pool_scale_1", "norm_ffn_1", "final_norm", "b_f_0", "conv_w_0")
WEIGHTS = ("norm_mix_0", "w_in_0", "b_f_0", "conv_w_0", "w_out_0", "norm_ffn_0", "w_up_0", "w_down_0", "norm_mix_1",
           "pool_w_1", "pool_scale_1", "norm_ffn_1", "w_up_1", "w_down_1", "final_norm")


def _pad_to(a, rows, cols):
    return jnp.pad(a, ((0, rows - a.shape[0]), (0, cols - a.shape[1])))


def _pack_small(p, width):
    rows = [p[n].reshape(1, -1) for n in SMALL[:6]]
    rows.append(_pad_to(p["b_f_0"].reshape(1, -1), 1, width))
    rows.append(_pad_to(p["conv_w_0"], 3, width))
    return _pad_to(jnp.concatenate(rows, axis=0), SMALL_ROWS, width)


def _unpack_small(a, like):
    out = {n: a[i] for i, n in enumerate(SMALL[:6])}
    out["b_f_0"] = a[6, :like["b_f_0"].shape[0]]
    out["conv_w_0"] = a[7:10, :like["conv_w_0"].shape[1]]
    return out


def kernel(x, norm_mix_0, w_in_0, b_f_0, conv_w_0, w_out_0, norm_ffn_0, w_up_0, w_down_0, norm_mix_1, pool_w_1, pool_scale_1, norm_ffn_1, w_up_1, w_down_1, final_norm, loss_target, m_norm_mix_0, m_w_in_0, m_b_f_0, m_conv_w_0, m_w_out_0, m_norm_ffn_0, m_w_up_0, m_w_down_0, m_norm_mix_1, m_pool_w_1, m_pool_scale_1, m_norm_ffn_1, m_w_up_1, m_w_down_1, m_final_norm, v_norm_mix_0, v_w_in_0, v_b_f_0, v_conv_w_0, v_w_out_0, v_norm_ffn_0, v_w_up_0, v_w_down_0, v_norm_mix_1, v_pool_w_1, v_pool_scale_1, v_norm_ffn_1, v_w_up_1, v_w_down_1, v_final_norm):
    w = dict(norm_mix_0=norm_mix_0, w_in_0=w_in_0, b_f_0=b_f_0, conv_w_0=conv_w_0, w_out_0=w_out_0,
             norm_ffn_0=norm_ffn_0, w_up_0=w_up_0, w_down_0=w_down_0, norm_mix_1=norm_mix_1, pool_w_1=pool_w_1,
             pool_scale_1=pool_scale_1, norm_ffn_1=norm_ffn_1, w_up_1=w_up_1, w_down_1=w_down_1, final_norm=final_norm)
    m = dict(norm_mix_0=m_norm_mix_0, w_in_0=m_w_in_0, b_f_0=m_b_f_0, conv_w_0=m_conv_w_0, w_out_0=m_w_out_0,
             norm_ffn_0=m_norm_ffn_0, w_up_0=m_w_up_0, w_down_0=m_w_down_0, norm_mix_1=m_norm_mix_1,
             pool_w_1=m_pool_w_1, pool_scale_1=m_pool_scale_1, norm_ffn_1=m_norm_ffn_1, w_up_1=m_w_up_1,
             w_down_1=m_w_down_1, final_norm=m_final_norm)
    v = dict(norm_mix_0=v_norm_mix_0, w_in_0=v_w_in_0, b_f_0=v_b_f_0, conv_w_0=v_conv_w_0, w_out_0=v_w_out_0,
             norm_ffn_0=v_norm_ffn_0, w_up_0=v_w_up_0, w_down_0=v_w_down_0, norm_mix_1=v_norm_mix_1,
             pool_w_1=v_pool_w_1, pool_scale_1=v_pool_scale_1, norm_ffn_1=v_norm_ffn_1, w_up_1=v_w_up_1,
             w_down_1=v_w_down_1, final_norm=v_final_norm)
    d = x.shape[-1]
    n_in = w_in_0.shape[1] * N_DEV
    n_qkv = 3 * ATTN_W
    pool_g, pool_rows, pool_c = pool_w_1.shape

    def shard2d(p):
        return {n: (p[n].reshape(pool_g * pool_rows, pool_c) if n == "pool_w_1" else p[n]) for n in BIG}
    w2, m2, v2 = shard2d(w), shard2d(m), shard2d(v)

    conv_cols = conv_w_0.shape[1]
    win_g8, conv_g8 = _all_gather([w_in_0.astype(BF16), _pad_to(conv_w_0, 8, 128)])
    conv_full = conv_g8[:, :, :conv_cols].transpose(1, 0, 2).reshape(8, N_DEV * conv_cols)
    win = win_g8.transpose(1, 0, 2).reshape(d, n_in)
    win_p = jnp.concatenate([win[:, :n_qkv], _pad_to(win[:, n_qkv:n_qkv + N_HEADS], d, F_PAD),
                             win[:, n_qkv + N_HEADS:]], axis=1)

    gains = dict(mix0=norm_mix_0.reshape(1, d), ffn0=norm_ffn_0.reshape(1, d), mix1=norm_mix_1.reshape(1, d),
                 ffn1=norm_ffn_1.reshape(1, d), final=final_norm.reshape(1, d))
    dev = _slot(lax.axis_index("x"), lax.axis_index("y"), lax.axis_index("c"))
    loss8, grad_x, landed, small = _local_step(
        x[0], loss_target[0], gains, _pad_to(b_f_0.reshape(1, -1), 1, F_PAD), conv_full, pool_scale_1.reshape(1, d),
        win_p, {n: w2[n].astype(BF16) for n in LATE})
    parts = jnp.concatenate(
        [small[k][None] for k in ("mix0", "ffn0", "mix1", "pool_scale", "ffn1", "final")]
        + [_pad_to(small["b_f"], 8, d)[None], jnp.pad(small["conv_w"], ((0, 0), (0, 0), (0, d - CONV_CH))),
           _pad_to(loss8[0:1, 0:1], 8, d)[None]], axis=0)
    tot = _small_allreduce(parts)
    loss = tot[10, 0]
    conv_g = lax.dynamic_slice(tot, (7, dev * conv_cols), (3, conv_cols))
    gs = tot.at[7:10].set(_pad_to(conv_g, 3, d))

    grads, deltas, new_m, new_v = {}, {}, {}, {}
    for n in BIG:
        gr, dl, nm, nv = _adamw_sum(landed[n], w2[n], m2[n], v2[n], name="adamw_" + n)
        for dst, val in ((grads, gr), (deltas, dl), (new_m, nm), (new_v, nv)):
            dst[n] = val.reshape(w[n].shape)
    dl, nm, nv = _adamw(gs, _pack_small(w, d), _pack_small(m, d), _pack_small(v, d), name="adamw_small")
    for dst, val in ((grads, gs), (deltas, dl), (new_m, nm), (new_v, nv)):
        dst.update(_unpack_small(val, w))
    return (loss, grad_x[None], *[grads[n] for n in WEIGHTS], *[deltas[n] for n in WEIGHTS],
            *[new_m[n] for n in WEIGHTS], *[new_v[n] for n in WEIGHTS])
```

```python
import functools

import jax
import jax.numpy as jnp
from jax import lax
from jax.experimental import pallas as pl
from jax.experimental.pallas import tpu as pltpu

F32 = jnp.float32
BF16 = jnp.bfloat16

N_DEV = 8
N_HEADS = 8
HEAD_DIM = 64
PAIR = 2 * HEAD_DIM
ATTN_W = N_HEADS * HEAD_DIM
CONV_CH = 512
F_PAD = 128
POOL_WINDOWS = (2, 4, 8, 16)
POOL_HALO = 16
CONV_HALO = 16
RMS_EPS = 1e-6
Q_SCALE = HEAD_DIM ** -0.5
LOG2E = 1.4426950408889634
NEG = -1e30
AUX_BIAS = 0
AUX_LSE = 3
AUX_ROWSUM = 6
ADAM_LR, ADAM_B1, ADAM_B2, ADAM_EPS, ADAM_WD, ADAM_STEP = 0.001, 0.9, 0.999, 1e-08, 0.01, 10
MESH = pl.DeviceIdType.MESH
VMEM_LIMIT = 56 * 2**20


def _cp(sem=None, vmem=VMEM_LIMIT, **kw):
    return pltpu.CompilerParams(dimension_semantics=sem, vmem_limit_bytes=vmem, **kw)


def _dot(a, b):
    return jnp.dot(a, b, preferred_element_type=F32)


def _dot_nt(a, b):
    return lax.dot_general(a, b, (((1,), (1,)), ((), ())), preferred_element_type=F32)


def _dot_tn(a, b):
    return lax.dot_general(a, b, (((0,), (0,)), ((), ())), preferred_element_type=F32)


def _rstd(h):
    return lax.rsqrt(jnp.mean(h * h, axis=-1, keepdims=True) + RMS_EPS)


def _rows8(x):
    r, n = x.shape
    return jnp.sum(x.reshape(r // 8, 8, n), axis=0)


def _norm_bwd(dn, h, g):
    r = _rstd(h)
    xhat = h * r
    dy = dn * g
    dh = r * (dy - xhat * jnp.mean(dy * xhat, axis=-1, keepdims=True))
    return dh, _rows8(dn * xhat)


def _const_spec(shape):
    nd = len(shape)
    return pl.BlockSpec(shape, lambda *_: (0,) * nd, pipeline_mode=pl.Buffered(1))


HBM_SPEC = pl.BlockSpec(memory_space=pltpu.HBM)
VMEM_SPEC = pl.BlockSpec(memory_space=pltpu.VMEM)


def _slot(px, py, pc):
    return 4 * px + 2 * py + pc


class _Exchange:
    def __init__(self, srcs, dsts, send_sems, recv_sems, local_sems, gather):
        x, y, c = lax.axis_index("x"), lax.axis_index("y"), lax.axis_index("c")
        me = _slot(x, y, c)
        self.copies = []
        for a, (src, dst) in enumerate(zip(srcs, dsts)):
            self.copies.append(pltpu.make_async_copy(src if gather else src.at[me], dst.at[me], local_sems.at[a]))
            for k in range(1, N_DEV):
                px, py, pc = x ^ (k >> 2), y ^ ((k >> 1) & 1), c ^ (k & 1)
                self.copies.append(pltpu.make_async_remote_copy(
                    src_ref=src if gather else src.at[_slot(px, py, pc)], dst_ref=dst.at[me],
                    send_sem=send_sems.at[(N_DEV - 1) * a + k - 1], recv_sem=recv_sems.at[(N_DEV - 1) * a + k - 1],
                    device_id=(px, py, pc), device_id_type=MESH))

    def start(self):
        for cp in self.copies:
            cp.start()

    def wait(self):
        for cp in self.copies:
            cp.wait()

    @staticmethod
    def scratch(n):
        return [pltpu.SemaphoreType.DMA(((N_DEV - 1) * n,)), pltpu.SemaphoreType.DMA(((N_DEV - 1) * n,)),
                pltpu.SemaphoreType.DMA((n,))]


def _norm_inproj(x, g, win_p, conv_w, *, tm=512):
    t, d = x.shape
    n_all = win_p.shape[1]
    n_qkv = 3 * ATTN_W
    n_bcx = 3 * CONV_CH
    assert n_all == n_qkv + F_PAD + n_bcx
    tm = min(tm, t)
    ch = CONV_CH

    def body(x_ref, g_ref, w_ref, cw_ref, n_ref, qkv_ref, f_ref, bcx_ref, cv_ref, ext):
        h = x_ref[...]
        n = (h * _rstd(h) * g_ref[...]).astype(BF16)
        n_ref[...] = n
        for c0 in range(0, n_qkv, 512):
            acc = _dot(n, w_ref[:, c0:c0 + 512])
            if c0 < ATTN_W:
                acc = acc * (Q_SCALE * LOG2E)
            qkv_ref[:, c0:c0 + 512] = acc.astype(BF16)
        f_ref[...] = _dot(n, w_ref[:, n_qkv:n_qkv + F_PAD])
        bcx = []
        for k in range(3):
            c0 = n_qkv + F_PAD + k * ch
            v = _dot(n, w_ref[:, c0:c0 + ch]).astype(BF16)
            bcx_ref[:, k * ch:(k + 1) * ch] = v
            bcx.append(v.astype(F32))
        @pl.when(pl.program_id(0) == 0)
        def _():
            ext[tm:tm + CONV_HALO, :] = jnp.zeros((CONV_HALO, ch), F32)
        ext[0:CONV_HALO, :] = ext[tm:tm + CONV_HALO, :]
        ext[CONV_HALO:CONV_HALO + tm, :] = bcx[1] * bcx[2]
        conv = (cw_ref[0:1, :] * ext[CONV_HALO - 2:CONV_HALO - 2 + tm, :]
                + cw_ref[1:2, :] * ext[CONV_HALO - 1:CONV_HALO - 1 + tm, :]
                + cw_ref[2:3, :] * ext[CONV_HALO:CONV_HALO + tm, :])
        cv_ref[...] = (bcx[0] * conv).astype(BF16)

    return pl.pallas_call(
        body, name="norm_inproj", grid=(t // tm,),
        in_specs=[pl.BlockSpec((tm, d), lambda i: (i, 0)), _const_spec((1, d)), _const_spec((d, n_all)),
                  _const_spec((8, ch))],
        out_specs=[pl.BlockSpec((tm, d), lambda i: (i, 0)), pl.BlockSpec((tm, n_qkv), lambda i: (i, 0)),
                   pl.BlockSpec((tm, F_PAD), lambda i: (i, 0)), pl.BlockSpec((tm, n_bcx), lambda i: (i, 0)),
                   pl.BlockSpec((tm, ch), lambda i: (i, 0))],
        out_shape=[jax.ShapeDtypeStruct((t, d), BF16), jax.ShapeDtypeStruct((t, n_qkv), BF16),
                   jax.ShapeDtypeStruct((t, F_PAD), F32), jax.ShapeDtypeStruct((t, n_bcx), BF16),
                   jax.ShapeDtypeStruct((t, ch), BF16)],
        scratch_shapes=[pltpu.VMEM((CONV_HALO + tm, ch), F32)],
        compiler_params=_cp(("arbitrary",)),
    )(x, g, win_p, conv_w)


def _head_lanes(h):
    lane = lax.broadcasted_iota(jnp.int32, (1, PAIR), 1)
    hh = h % 2
    return lane, lane // HEAD_DIM == hh, HEAD_DIM * (1 - hh)


def _pieces(col):
    hi = col.astype(BF16).astype(F32)
    r1 = col - hi
    mid = r1.astype(BF16).astype(F32)
    lo = (r1 - mid).astype(BF16).astype(F32)
    return hi, mid, lo


def _put_pieces(lane, first, col, other):
    hi, mid, lo = _pieces(col)
    return jnp.where(lane == first, hi, jnp.where(lane == first + 1, mid, jnp.where(lane == first + 2, lo, other)))


def _fgate_prep(flog, b_f, qkv, *, tm=512):
    t = flog.shape[0]
    tm = min(tm, t)

    def body(f_ref, b_ref, qkv_ref, qat_ref, ka_ref, va_ref, vat_ref, sg_ref, carry):
        @pl.when(pl.program_id(0) == 0)
        def _():
            carry[...] = jnp.zeros_like(carry)
        z = f_ref[...] + b_ref[...]
        e = jnp.exp(-jnp.abs(z))
        logf = jnp.minimum(z, 0.0) - jnp.log(1.0 + e)
        sg_ref[...] = jnp.where(z >= 0, e, 1.0) / (1.0 + e)
        r = lax.broadcasted_iota(jnp.int32, (tm, tm), 0)
        c = lax.broadcasted_iota(jnp.int32, (tm, tm), 1)
        tri = (c <= r).astype(F32)
        cs = jnp.dot(tri, logf, preferred_element_type=F32, precision=lax.Precision.HIGHEST) + carry[...]
        carry[...] = cs[tm - 1:tm, :]
        cs2 = cs * LOG2E
        for h in range(N_HEADS):
            lane, head, aux = _head_lanes(h)
            p0 = (h // 2) * PAIR
            ones = ((lane >= aux + AUX_LSE) & (lane <= aux + AUX_ROWSUM)).astype(F32)
            bias = (lane >= aux + AUX_BIAS) & (lane < aux + AUX_BIAS + 3)
            k_aux = _put_pieces(lane, aux + AUX_BIAS, cs2[:, h:h + 1], ones)
            q_aug = jnp.where(head, qkv_ref[:, p0:p0 + PAIR].astype(F32), jnp.where(bias, -1.0, 0.0))
            v_aug = jnp.where(head, qkv_ref[:, 2 * ATTN_W + p0:2 * ATTN_W + p0 + PAIR].astype(F32),
                              jnp.where(bias, 1.0, 0.0))
            qat_ref[h] = q_aug.astype(BF16).T
            ka_ref[h] = jnp.where(head, qkv_ref[:, ATTN_W + p0:ATTN_W + p0 + PAIR], k_aux.astype(BF16))
            va_ref[h] = v_aug.astype(BF16)
            vat_ref[h] = v_aug.astype(BF16).T

    aug = lambda: pl.BlockSpec((N_HEADS, tm, PAIR), lambda i: (0, i, 0))
    aug_t = lambda: pl.BlockSpec((N_HEADS, PAIR, tm), lambda i: (0, 0, i))
    aug_shape = jax.ShapeDtypeStruct((N_HEADS, t, PAIR), BF16)
    aug_t_shape = jax.ShapeDtypeStruct((N_HEADS, PAIR, t), BF16)
    return pl.pallas_call(
        body, name="fgate_prep", grid=(t // tm,),
        in_specs=[pl.BlockSpec((tm, F_PAD), lambda i: (i, 0)), _const_spec((1, F_PAD)),
                  pl.BlockSpec((tm, 3 * ATTN_W), lambda i: (i, 0))],
        out_specs=[aug_t(), aug(), aug(), aug_t(), pl.BlockSpec((tm, F_PAD), lambda i: (i, 0))],
        out_shape=[aug_t_shape, aug_shape, aug_shape, aug_t_shape, jax.ShapeDtypeStruct((t, F_PAD), F32)],
        scratch_shapes=[pltpu.VMEM((1, F_PAD), F32)],
        compiler_params=_cp(("arbitrary",)),
    )(flog, b_f, qkv)


def _put_pieces_t(row, first, vec, other):
    hi, mid, lo = _pieces(vec)
    return jnp.where(row == first, hi, jnp.where(row == first + 1, mid, jnp.where(row == first + 2, lo, other)))


def _attn_fwd(q_aug_t, k_aug, v_aug_t, shards, *, tq=1024):
    t = k_aug.shape[1]
    tq = min(tq, t)
    tk = tq // 2
    nq = t // tq
    n_pairs = ATTN_W // PAIR
    n_sh = len(shards)

    def body(qt_ref, k_ref, vt_ref, *rest):
        o_ref, qb_ref, qbt_ref = rest[n_sh:n_sh + 3]
        s_scr = rest[2 * n_sh + 3]
        gather = _Exchange(rest[:n_sh], rest[n_sh + 3:2 * n_sh + 3], *rest[2 * n_sh + 4:], gather=True)
        i = pl.program_id(1)

        @pl.when((pl.program_id(0) == 0) & (i == 0))
        def _():
            gather.start()
        key = lax.broadcasted_iota(jnp.int32, (tk, tq), 0)
        qry = lax.broadcasted_iota(jnp.int32, (tk, tq), 1)
        qt = [qt_ref[0], qt_ref[1]]

        def logits(hh, tile, slot, diag):
            s = _dot(k_ref[hh, pl.ds(pl.multiple_of(tile * tk, tk), tk), :], qt[hh])
            if diag:
                s = jnp.where(key + (tile * tk - i * tq) <= qry, s, NEG)
            s_scr[hh, slot] = s
            return jnp.max(s, axis=0, keepdims=True)

        def probs(hh, tile, slot, m, acc, tmax):
            mn = jnp.maximum(m, tmax)
            p = jnp.exp2(s_scr[hh, slot] - mn).astype(BF16)
            acc = jnp.exp2(m - mn) * acc + _dot(vt_ref[hh, :, pl.ds(pl.multiple_of(tile * tk, tk), tk)], p)
            return mn, acc

        def advance(carry, prev, slot, nxt, diag=False):
            out = []
            for hh in range(2):
                m, acc, tmax = carry[hh]
                m, acc = probs(hh, prev, slot, m, acc, tmax)
                out.append((m, acc, logits(hh, nxt, 1 - slot, diag)))
            return tuple(out)

        def two_tiles(jj, carry):
            carry = advance(carry, jnp.where(jj == 0, 2 * i, 2 * jj - 1), 1, 2 * jj)
            return advance(carry, 2 * jj, 0, 2 * jj + 1)

        init = tuple((jnp.full((1, tq), NEG, F32), jnp.zeros((PAIR, tq), F32), logits(hh, 2 * i + 1, 0, True))
                     for hh in range(2))
        carry = advance(init, 2 * i + 1, 0, 2 * i, diag=True)
        carry = lax.fori_loop(0, i, two_tiles, carry)
        last = jnp.where(i == 0, 2 * i, 2 * i - 1)
        row = lax.broadcasted_iota(jnp.int32, (PAIR, 1), 0)
        res = []
        for hh in range(2):
            aux = HEAD_DIM * (1 - hh)
            m, acc, tmax = carry[hh]
            m, acc = probs(hh, last, 1, m, acc, tmax)
            l = acc[aux + AUX_BIAS:aux + AUX_BIAS + 1, :]
            qbt = _put_pieces_t(row, aux + AUX_LSE, -(m + jnp.log2(l)), qt[hh].astype(F32))
            qbt_ref[hh] = qbt.astype(BF16)
            qb_ref[hh] = qbt.astype(BF16).T
            res.append(acc * (1.0 / l))
        o_ref[...] = jnp.where(row < HEAD_DIM, res[0], res[1]).astype(BF16).T

        @pl.when((pl.program_id(0) == n_pairs - 1) & (i == nq - 1))
        def _():
            gather.wait()

    res = pl.pallas_call(
        body, name="attn_fwd", grid=(n_pairs, nq),
        in_specs=[pl.BlockSpec((2, PAIR, tq), lambda p, i: (p, 0, i)),
                  pl.BlockSpec((2, t, PAIR), lambda p, i: (p, 0, 0), pipeline_mode=pl.Buffered(1)),
                  pl.BlockSpec((2, PAIR, t), lambda p, i: (p, 0, 0), pipeline_mode=pl.Buffered(1))] + [HBM_SPEC] * n_sh,
        out_specs=[pl.BlockSpec((tq, PAIR), lambda p, i: (i, p)),
                   pl.BlockSpec((2, tq, PAIR), lambda p, i: (p, i, 0)),
                   pl.BlockSpec((2, PAIR, tq), lambda p, i: (p, 0, i))] + [HBM_SPEC] * n_sh,
        out_shape=[jax.ShapeDtypeStruct((t, ATTN_W), BF16), jax.ShapeDtypeStruct((N_HEADS, t, PAIR), BF16),
                   jax.ShapeDtypeStruct((N_HEADS, PAIR, t), BF16)]
        + [jax.ShapeDtypeStruct((N_DEV,) + s.shape, s.dtype) for s in shards],
        scratch_shapes=[pltpu.VMEM((2, 2, tk, tq), F32)] + _Exchange.scratch(n_sh),
        compiler_params=_cp(("arbitrary", "arbitrary")),
    )(q_aug_t, k_aug, v_aug_t, *shards)
    return res[0], res[1], res[2], res[3:]


def _prev_halo(tm, halo):
    return lambda i: (jnp.maximum(i * (tm // halo) - 1, 0), 0)


def _next_halo(tm, halo, t):
    return lambda i: (jnp.minimum((i + 1) * (tm // halo), t // halo - 1), 0)


def _mlp_tile(hh, g_ref, wu_ref, wd_ref, n_ref, a_ref, z_ref):
    n_blk, _, fb = wu_ref.shape
    n = (hh * _rstd(hh) * g_ref[...]).astype(BF16)
    n_ref[...] = n
    acc = hh
    for k in range(n_blk):
        a = _dot(n, wu_ref[k])
        zz = jnp.square(jnp.maximum(a, 0.0)).astype(BF16)
        a_ref[:, k * fb:(k + 1) * fb] = a.astype(BF16)
        z_ref[:, k * fb:(k + 1) * fb] = zz
        acc = acc + _dot(zz, wd_ref[k * fb:(k + 1) * fb, :])
    return acc


def _outproj(att, cv, x, wout, *, tm=512):
    t, d = x.shape
    tm = min(tm, t)

    def body(a_ref, c_ref, x_ref, w_ref, h_ref):
        h_ref[...] = x_ref[...] + _dot(a_ref[...], w_ref[0:ATTN_W, :]) + _dot(c_ref[...], w_ref[ATTN_W:, :])

    return pl.pallas_call(
        body, name="outproj", grid=(t // tm,),
        in_specs=[pl.BlockSpec((tm, ATTN_W), lambda i: (i, 0)), pl.BlockSpec((tm, CONV_CH), lambda i: (i, 0)),
                  pl.BlockSpec((tm, d), lambda i: (i, 0)), _const_spec(wout.shape)],
        out_specs=pl.BlockSpec((tm, d), lambda i: (i, 0)),
        out_shape=jax.ShapeDtypeStruct((t, d), F32),
        compiler_params=_cp(("parallel",)),
    )(att, cv, x, wout)


def _mlp_fwd(h, g, wup, wdown, *, name, tm=512):
    t, d = h.shape
    n_blk, _, fb = wup.shape
    f = n_blk * fb
    tm = min(tm, t)

    def body(h_ref, g_ref, wu_ref, wd_ref, ho_ref, n_ref, a_ref, z_ref):
        ho_ref[...] = _mlp_tile(h_ref[...], g_ref, wu_ref, wd_ref, n_ref, a_ref, z_ref)

    row = lambda n_: pl.BlockSpec((tm, n_), lambda i: (i, 0))
    return pl.pallas_call(
        body, name=name, grid=(t // tm,),
        in_specs=[row(d), _const_spec((1, d)), _const_spec(wup.shape), _const_spec(wdown.shape)],
        out_specs=[row(d), row(d), row(f), row(f)],
        out_shape=[jax.ShapeDtypeStruct((t, d), F32), jax.ShapeDtypeStruct((t, d), BF16),
                   jax.ShapeDtypeStruct((t, f), BF16), jax.ShapeDtypeStruct((t, f), BF16)],
        compiler_params=_cp(("parallel",)),
    )(h, g, wup, wdown)


def _mlp_fwd_loss(h, g, wup, wdown, g_out, target, *, name, tm=512):
    t, d = h.shape
    n_blk, _, fb = wup.shape
    f = n_blk * fb
    tm = min(tm, t)
    nsteps = t // tm

    def body(h_ref, g_ref, wu_ref, wd_ref, go_ref, y_ref, loss_ref, dh_ref, dg_ref, n_ref, a_ref, z_ref, lacc):
        i = pl.program_id(0)

        @pl.when(i == 0)
        def _():
            lacc[...] = jnp.zeros_like(lacc)
            dg_ref[...] = jnp.zeros_like(dg_ref)
        hv = _mlp_tile(h_ref[...], g_ref, wu_ref, wd_ref, n_ref, a_ref, z_ref)
        gv = go_ref[...]
        r = _rstd(hv)
        xhat = hv * r
        err = xhat * gv - y_ref[...]
        lacc[...] += _rows8(err * err)
        dout = err * (1.0 / d)
        dy = dout * gv
        dg_ref[...] += _rows8(dout * xhat)
        dh_ref[...] = r * (dy - xhat * jnp.mean(dy * xhat, axis=-1, keepdims=True))

        @pl.when(i == nsteps - 1)
        def _():
            loss_ref[...] = jnp.full(loss_ref.shape, (0.5 / d) * jnp.sum(lacc[...]), F32)

    row = lambda n_: pl.BlockSpec((tm, n_), lambda i: (i, 0))
    return pl.pallas_call(
        body, name=name, grid=(nsteps,),
        in_specs=[row(d), _const_spec((1, d)), _const_spec(wup.shape), _const_spec(wdown.shape), _const_spec((1, d)),
                  row(d)],
        out_specs=[pl.BlockSpec((8, 128), lambda i: (0, 0)), row(d), pl.BlockSpec((8, d), lambda i: (0, 0)),
                   row(d), row(f), row(f)],
        out_shape=[jax.ShapeDtypeStruct((8, 128), F32), jax.ShapeDtypeStruct((t, d), F32),
                   jax.ShapeDtypeStruct((8, d), F32), jax.ShapeDtypeStruct((t, d), BF16),
                   jax.ShapeDtypeStruct((t, f), BF16), jax.ShapeDtypeStruct((t, f), BF16)],
        scratch_shapes=[pltpu.VMEM((8, d), F32)],
        compiler_params=_cp(("arbitrary",)),
    )(h, g, wup, wdown, g_out, target)


def _pool_inv_count(i, tm):
    tglob = (i * tm + lax.broadcasted_iota(jnp.int32, (tm, 1), 0) + 1).astype(F32)
    return [1.0 / jnp.minimum(tglob, float(w)) for w in POOL_WINDOWS]


def _pool_fwd(h, g, poolw, scale, *, tm=512):
    t, d = h.shape
    tm = min(tm, t)
    cg = d // len(POOL_WINDOWS)

    def body(h_ref, hh_ref, g_ref, w_ref, s_ref, ho_ref, p_ref, ext):
        i = pl.program_id(0)
        hv = h_ref[...]
        halo = hh_ref[...]
        n = hv * _rstd(hv) * g_ref[...]
        ext[0:POOL_HALO, :] = jnp.where(i == 0, 0.0, halo * _rstd(halo) * g_ref[...])
        ext[POOL_HALO:POOL_HALO + tm, :] = n
        inv = _pool_inv_count(i, tm)
        for gi, w in enumerate(POOL_WINDOWS):
            cs = slice(gi * cg, (gi + 1) * cg)
            s = ext[POOL_HALO:POOL_HALO + tm, cs]
            for j in range(1, w):
                s = s + ext[POOL_HALO - j:POOL_HALO - j + tm, cs]
            pooled = (s * inv[gi] - n[:, cs]).astype(BF16)
            p_ref[:, cs] = pooled
            ho_ref[:, cs] = hv[:, cs] + _dot(pooled, w_ref[gi]) * s_ref[:, cs]

    row = lambda: pl.BlockSpec((tm, d), lambda i: (i, 0))
    return pl.pallas_call(
        body, name="pool_fwd", grid=(t // tm,),
        in_specs=[row(), pl.BlockSpec((POOL_HALO, d), _prev_halo(tm, POOL_HALO)), _const_spec((1, d)),
                  _const_spec(poolw.shape), _const_spec((1, d))],
        out_specs=[row(), row()],
        out_shape=[jax.ShapeDtypeStruct((t, d), F32), jax.ShapeDtypeStruct((t, d), BF16)],
        scratch_shapes=[pltpu.VMEM((POOL_HALO + tm, d), F32)],
        compiler_params=_cp(("parallel",)),
    )(h, h, g, poolw, scale)


def _mm_tn(a, b, *, name, ta, tb, tt, blocked_out=False, out_dtype=F32):
    t, ka = a.shape
    n = b.shape[1]
    ta, tb, tt = min(ta, ka), min(tb, n), min(tt, t)
    nt = t // tt

    def body(a_ref, b_ref, o_ref, acc):
        @pl.when(pl.program_id(2) == 0)
        def _():
            acc[...] = jnp.zeros_like(acc)
        acc[...] += _dot_tn(a_ref[...].astype(BF16), b_ref[...].astype(BF16))

        @pl.when(pl.program_id(2) == nt - 1)
        def _():
            o_ref[...] = acc[...].astype(out_dtype)

    if blocked_out:
        assert ta == ka
        out_shape = jax.ShapeDtypeStruct((n // tb, ka, tb), out_dtype)
        out_spec = pl.BlockSpec((None, ta, tb), lambda i, j, k: (j, i, 0))
    else:
        out_shape = jax.ShapeDtypeStruct((ka, n), out_dtype)
        out_spec = pl.BlockSpec((ta, tb), lambda i, j, k: (i, j))
    return pl.pallas_call(
        body, name=name, grid=(ka // ta, n // tb, nt),
        in_specs=[pl.BlockSpec((tt, ta), lambda i, j, k: (k, i)), pl.BlockSpec((tt, tb), lambda i, j, k: (k, j))],
        out_specs=out_spec, out_shape=out_shape, scratch_shapes=[pltpu.VMEM((ta, tb), F32)],
        compiler_params=_cp(("parallel", "parallel", "arbitrary")),
    )(a, b)


def _mm_tn_cat(a_list, b_list, *, name, tt, out_dtype=BF16):
    t = a_list[0].shape[0]
    ta, tb = a_list[0].shape[1], b_list[0].shape[1]
    na, nb = len(a_list), len(b_list)
    tt = min(tt, t)
    nt = t // tt

    def body(*refs):
        a_refs, b_refs, o_ref, acc = refs[:na], refs[na:na + nb], refs[na + nb], refs[na + nb + 1]
        i, j, k = pl.program_id(0), pl.program_id(1), pl.program_id(2)

        @pl.when(k == 0)
        def _():
            acc[...] = jnp.zeros_like(acc)
        for ia in range(na):
            for ib in range(nb):
                @pl.when((i == ia) & (j == ib))
                def _(ia=ia, ib=ib):
                    acc[...] += _dot_tn(a_refs[ia][...].astype(BF16), b_refs[ib][...].astype(BF16))

        @pl.when(k == nt - 1)
        def _():
            o_ref[...] = acc[...].astype(out_dtype)

    def held(m, axis):
        def index(i, j, k):
            cur = (i, j)[axis]
            return (jnp.where(cur == m, k, jnp.where(cur < m, 0, nt - 1)), 0)
        return index

    return pl.pallas_call(
        body, name=name, grid=(na, nb, nt),
        in_specs=[pl.BlockSpec((tt, ta), held(m, 0)) for m in range(na)]
        + [pl.BlockSpec((tt, tb), held(m, 1)) for m in range(nb)],
        out_specs=pl.BlockSpec((ta, tb), lambda i, j, k: (i, j)),
        out_shape=jax.ShapeDtypeStruct((na * ta, nb * tb), out_dtype), scratch_shapes=[pltpu.VMEM((ta, tb), F32)],
        compiler_params=_cp(("arbitrary", "arbitrary", "arbitrary")),
    )(*a_list, *b_list)


def _mlp_bwd(dho, h, a, g, wup, wdown, *, name, tm=512):
    t, d = h.shape
    n_blk, _, fb = wup.shape
    f = n_blk * fb
    tm = min(tm, t)

    def body(do_ref, h_ref, a_ref, g_ref, wu_ref, wd_ref, dh_ref, da_ref, dg_ref):
        @pl.when(pl.program_id(0) == 0)
        def _():
            dg_ref[...] = jnp.zeros_like(dg_ref)
        dho_v = do_ref[...]
        dob = dho_v.astype(BF16)
        dn = jnp.zeros((tm, d), F32)
        for k in range(n_blk):
            dz = _dot_nt(dob, wd_ref[k * fb:(k + 1) * fb, :])
            da = (dz * (2.0 * jnp.maximum(a_ref[:, k * fb:(k + 1) * fb].astype(F32), 0.0))).astype(BF16)
            da_ref[:, k * fb:(k + 1) * fb] = da
            dn = dn + _dot_nt(da, wu_ref[k])
        dh, dg = _norm_bwd(dn, h_ref[...], g_ref[...])
        dh_ref[...] = dho_v + dh
        dg_ref[...] += dg

    row = lambda n_: pl.BlockSpec((tm, n_), lambda i: (i, 0))
    return pl.pallas_call(
        body, name=name, grid=(t // tm,),
        in_specs=[row(d), row(d), row(f), _const_spec((1, d)), _const_spec(wup.shape), _const_spec(wdown.shape)],
        out_specs=[row(d), row(f), pl.BlockSpec((8, d), lambda i: (0, 0))],
        out_shape=[jax.ShapeDtypeStruct((t, d), F32), jax.ShapeDtypeStruct((t, f), BF16),
                   jax.ShapeDtypeStruct((8, d), F32)],
        compiler_params=_cp(("arbitrary",)),
    )(dho, h, a, g, wup, wdown)


def _pool_bwd(dho, h, pooled, g, poolw, scale, *, tm=512):
    t, d = h.shape
    tm = min(tm, t)
    ng = len(POOL_WINDOWS)
    cg = d // ng
    nsteps = t // tm

    def body(do_ref, dn_ref, h_ref, p_ref, g_ref, w_ref, s_ref, dh_ref, dw_ref, ds_ref, dg_ref, ext):
        i = pl.program_id(0)

        @pl.when(i == 0)
        def _():
            dw_ref[...] = jnp.zeros_like(dw_ref)
            ds_ref[...] = jnp.zeros_like(ds_ref)
            dg_ref[...] = jnp.zeros_like(dg_ref)
        dho_v = do_ref[...]
        sv = s_ref[...]
        dyp = (dho_v * sv).astype(BF16)
        dyp_halo = (dn_ref[...] * sv).astype(BF16)
        inv = _pool_inv_count(i, tm)
        tnext = ((i + 1) * tm + lax.broadcasted_iota(jnp.int32, (POOL_HALO, 1), 0) + 1).astype(F32)
        last = i == nsteps - 1
        ypre_parts, dpooled_parts = [], []
        for gi, w in enumerate(POOL_WINDOWS):
            cs = slice(gi * cg, (gi + 1) * cg)
            pg = p_ref[:, cs]
            ypre_parts.append(_dot(pg, w_ref[gi]))
            dw_ref[gi] += _dot_tn(pg, dyp[:, cs])
            dpool = _dot_nt(dyp[:, cs], w_ref[gi])
            dpooled_parts.append(dpool)
            ext[0:tm, cs] = dpool * inv[gi]
            dpool_halo = _dot_nt(dyp_halo[:, cs], w_ref[gi]) * (1.0 / jnp.minimum(tnext, float(w)))
            ext[tm:tm + POOL_HALO, cs] = jnp.where(last, 0.0, dpool_halo)
        ds_ref[...] += _rows8(dho_v * jnp.concatenate(ypre_parts, axis=1))
        dn_parts = []
        for gi, w in enumerate(POOL_WINDOWS):
            cs = slice(gi * cg, (gi + 1) * cg)
            s = ext[0:tm, cs]
            for j in range(1, w):
                s = s + ext[j:j + tm, cs]
            dn_parts.append(s - dpooled_parts[gi])
        dh, dg = _norm_bwd(jnp.concatenate(dn_parts, axis=1), h_ref[...], g_ref[...])
        dh_ref[...] = dho_v + dh
        dg_ref[...] += dg

    row = lambda: pl.BlockSpec((tm, d), lambda i: (i, 0))
    acc8 = lambda: pl.BlockSpec((8, d), lambda i: (0, 0))
    return pl.pallas_call(
        body, name="pool_bwd", grid=(nsteps,),
        in_specs=[row(), pl.BlockSpec((POOL_HALO, d), _next_halo(tm, POOL_HALO, t)), row(), row(),
                  _const_spec((1, d)), _const_spec(poolw.shape), _const_spec((1, d))],
        out_specs=[row(), pl.BlockSpec((ng, cg, cg), lambda i: (0, 0, 0)), acc8(), acc8()],
        out_shape=[jax.ShapeDtypeStruct((t, d), F32), jax.ShapeDtypeStruct((ng, cg, cg), F32),
                   jax.ShapeDtypeStruct((8, d), F32), jax.ShapeDtypeStruct((8, d), F32)],
        scratch_shapes=[pltpu.VMEM((tm + POOL_HALO, d), F32)],
        compiler_params=_cp(("arbitrary",)),
    )(dho, dho, h, pooled, g, poolw, scale)


def _outproj_bwd(dh, o, wout, *, tm=512):
    t, d = dh.shape
    tm = min(tm, t)

    def body(dh_ref, o_ref, w_ref, da_ref, dat_ref, dc_ref):
        dhb = dh_ref[...].astype(BF16)
        dc_ref[...] = _dot_nt(dhb, w_ref[ATTN_W:, :])
        for p in range(ATTN_W // PAIR):
            datt = _dot_nt(dhb, w_ref[p * PAIR:(p + 1) * PAIR, :])
            prod = datt * o_ref[:, p * PAIR:(p + 1) * PAIR].astype(F32)
            for hh in range(2):
                lane, head, aux = _head_lanes(hh)
                delta = jnp.sum(jnp.where(head, prod, 0.0), axis=1, keepdims=True)
                aug = _put_pieces(lane, aux + AUX_BIAS, -delta, jnp.where(head, datt, 0.0))
                da_ref[2 * p + hh] = aug.astype(BF16)
                dat_ref[2 * p + hh] = aug.astype(BF16).T

    row = lambda n_: pl.BlockSpec((tm, n_), lambda i: (i, 0))
    return pl.pallas_call(
        body, name="outproj_bwd", grid=(t // tm,),
        in_specs=[row(d), row(ATTN_W), _const_spec(wout.shape)],
        out_specs=[pl.BlockSpec((N_HEADS, tm, PAIR), lambda i: (0, i, 0)),
                   pl.BlockSpec((N_HEADS, PAIR, tm), lambda i: (0, 0, i)), row(CONV_CH)],
        out_shape=[jax.ShapeDtypeStruct((N_HEADS, t, PAIR), BF16), jax.ShapeDtypeStruct((N_HEADS, PAIR, t), BF16),
                   jax.ShapeDtypeStruct((t, CONV_CH), F32)],
        compiler_params=_cp(("parallel",)),
    )(dh, o, wout)


def _conv_bwd(bcx, dcv, conv_w, *, tm=512):
    t = bcx.shape[0]
    tm = min(tm, t)
    ch = CONV_CH
    nsteps = t // tm

    def body(b_ref, c_ref, x_ref, hc_ref, hx_ref, d_ref, nb_ref, nd_ref, w_ref, o_ref, dw_ref, ext_u, ext_d):
        i = pl.program_id(0)

        @pl.when(i == 0)
        def _():
            dw_ref[...] = jnp.zeros_like(dw_ref)
        b, c, x, dcv_v = b_ref[...].astype(F32), c_ref[...].astype(F32), x_ref[...].astype(F32), d_ref[...]
        ext_u[0:CONV_HALO, :] = jnp.where(i == 0, 0.0, hc_ref[...].astype(F32) * hx_ref[...].astype(F32))
        ext_u[CONV_HALO:CONV_HALO + tm, :] = c * x
        dconv = dcv_v * b
        ext_d[0:tm, :] = dconv
        ext_d[tm:tm + CONV_HALO, :] = jnp.where(i == nsteps - 1, 0.0, nd_ref[...] * nb_ref[...].astype(F32))
        u = [ext_u[CONV_HALO - 2 + k:CONV_HALO - 2 + k + tm, :] for k in range(3)]
        conv = w_ref[0:1, :] * u[0] + w_ref[1:2, :] * u[1] + w_ref[2:3, :] * u[2]
        du = (w_ref[2:3, :] * dconv + w_ref[1:2, :] * ext_d[1:1 + tm, :] + w_ref[0:1, :] * ext_d[2:2 + tm, :])
        o_ref[:, 0:ch] = (dcv_v * conv).astype(BF16)
        o_ref[:, ch:2 * ch] = (du * x).astype(BF16)
        o_ref[:, 2 * ch:3 * ch] = (du * c).astype(BF16)
        for k in range(3):
            dw_ref[k] += _rows8(dconv * u[k])

    col = lambda k: pl.BlockSpec((tm, ch), lambda i: (i, k))
    prev = lambda k: pl.BlockSpec((CONV_HALO, ch), lambda i: (_prev_halo(tm, CONV_HALO)(i)[0], k))
    nxt = lambda k: pl.BlockSpec((CONV_HALO, ch), lambda i: (_next_halo(tm, CONV_HALO, t)(i)[0], k))
    return pl.pallas_call(
        body, name="conv_bwd", grid=(nsteps,),
        in_specs=[col(0), col(1), col(2), prev(1), prev(2), col(0), nxt(0), nxt(0), _const_spec((8, ch))],
        out_specs=[pl.BlockSpec((tm, 3 * ch), lambda i: (i, 0)), pl.BlockSpec((3, 8, ch), lambda i: (0, 0, 0))],
        out_shape=[jax.ShapeDtypeStruct((t, 3 * ch), BF16), jax.ShapeDtypeStruct((3, 8, ch), F32)],
        scratch_shapes=[pltpu.VMEM((CONV_HALO + tm, ch), F32), pltpu.VMEM((tm + CONV_HALO, ch), F32)],
        compiler_params=_cp(("arbitrary",)),
    )(bcx, bcx, bcx, bcx, bcx, dcv, bcx, dcv, conv_w)


def _attn_bwd(q_bwd, do_aug, q_bwd_t, do_aug_t, k_aug, v_aug, gblocks, *, tq=1024):
    t = q_bwd.shape[1]
    tq = min(tq, t)
    tk = tq // 2
    nq, nk = t // tq, t // tk
    n_pairs = ATTN_W // PAIR
    n_g = len(gblocks)

    def body(q_ref, do_ref, qt_ref, dot_ref, k_ref, v_ref, *rest):
        dq_ref, dqx_ref, dk_ref, dkx_ref, dv_ref = rest[n_g:n_g + 5]
        dq_scr = rest[2 * n_g + 5]
        scatter = _Exchange(rest[:n_g], rest[n_g + 5:2 * n_g + 5], *rest[2 * n_g + 6:], gather=False)
        j = pl.program_id(1)

        @pl.when((pl.program_id(0) == 0) & (j == 0))
        def _():
            scatter.start()

        @pl.when(j == 0)
        def _():
            dq_scr[...] = jnp.zeros_like(dq_scr)
        k = [k_ref[0], k_ref[1]]
        v = [v_ref[0], v_ref[1]]

        def step(i, carry, diag, rows=tq, row0=0):
            qs = pl.multiple_of(i * tq + row0, tk)
            if diag:
                row = lax.broadcasted_iota(jnp.int32, (rows, tk), 0)
                col = lax.broadcasted_iota(jnp.int32, (rows, tk), 1)
            out = []
            for hh in range(2):
                dk_a, dv_a = carry[hh]
                q = q_ref[hh, pl.ds(qs, rows), :]
                dov = do_ref[hh, pl.ds(qs, rows), :]
                p = jnp.exp2(_dot_nt(q, k[hh]))
                if diag:
                    p = jnp.where(col + (j * tk - i * tq - row0) <= row, p, 0.0)
                ds = (p * _dot_nt(dov, v[hh])).astype(BF16)
                dv_a = dv_a + _dot(dot_ref[hh, :, pl.ds(qs, rows)], p.astype(BF16))
                dk_a = dk_a + _dot(qt_ref[hh, :, pl.ds(qs, rows)], ds)
                dq_scr[hh, pl.ds(qs, rows), :] += _dot(ds, k[hh])
                out.append((dk_a, dv_a))
            return tuple(out)

        zero = (jnp.zeros((PAIR, tk), F32), jnp.zeros((PAIR, tk), F32))
        carry = lax.cond(j % 2 == 0, lambda c: step(j // 2, c, True),
                         lambda c: step(j // 2, c, True, rows=tk, row0=tk), (zero, zero))
        full0 = j // 2 + 1
        odd = (nq - full0) % 2
        carry = lax.cond(odd == 1, lambda c: step(full0, c, False), lambda c: c, carry)
        (dk0, dv0), (dk1, dv1) = lax.fori_loop(
            0, (nq - full0) // 2, lambda ii, c: step(full0 + odd + 2 * ii, c, False, rows=2 * tq), carry)
        first_t = lax.broadcasted_iota(jnp.int32, (PAIR, 1), 0) < HEAD_DIM
        first = lax.broadcasted_iota(jnp.int32, (1, PAIR), 1) < HEAD_DIM
        dk_ref[...] = (jnp.where(first_t, dk0, dk1) * (1.0 / LOG2E)).astype(BF16).T
        dkx_ref[...] = jnp.where(first_t, dk1, dk0).T
        dv_ref[...] = jnp.where(first_t, dv0, dv1).astype(BF16).T

        @pl.when(j == nk - 1)
        def _():
            dq_ref[...] = (jnp.where(first, dq_scr[0], dq_scr[1]) * Q_SCALE).astype(BF16)
            dqx_ref[...] = jnp.where(first, dq_scr[1], dq_scr[0])

        @pl.when((pl.program_id(0) == n_pairs - 1) & (j == nk - 1))
        def _():
            scatter.wait()

    resident = lambda: pl.BlockSpec((2, t, PAIR), lambda p, j: (p, 0, 0), pipeline_mode=pl.Buffered(1))
    resident_t = lambda: pl.BlockSpec((2, PAIR, t), lambda p, j: (p, 0, 0), pipeline_mode=pl.Buffered(1))
    kv_in = lambda: pl.BlockSpec((2, tk, PAIR), lambda p, j: (p, j, 0))
    whole = lambda: pl.BlockSpec((t, PAIR), lambda p, j: (0, p))
    tile = lambda: pl.BlockSpec((tk, PAIR), lambda p, j: (j, p))
    b16 = jax.ShapeDtypeStruct((t, ATTN_W), BF16)
    f32 = jax.ShapeDtypeStruct((t, ATTN_W), F32)
    res = pl.pallas_call(
        body, name="attn_bwd", grid=(n_pairs, nk),
        in_specs=[resident(), resident(), resident_t(), resident_t(), kv_in(), kv_in()] + [HBM_SPEC] * n_g,
        out_specs=[whole(), whole(), tile(), tile(), tile()] + [HBM_SPEC] * n_g,
        out_shape=[b16, f32, b16, f32, b16] + [jax.ShapeDtypeStruct(g.shape, g.dtype) for g in gblocks],
        scratch_shapes=[pltpu.VMEM((2, t, PAIR), F32)] + _Exchange.scratch(n_g),
        compiler_params=_cp(("arbitrary", "arbitrary")),
    )(q_bwd, do_aug, q_bwd_t, do_aug_t, k_aug, v_aug, *gblocks)
    return res[:5], res[5:]


def _fgate_bwd(dqx, dkx, sgate, *, tm=256):
    t = sgate.shape[0]
    tm = min(tm, t)
    nsteps = t // tm

    def body(dq_ref, dk_ref, sg_ref, df_ref, dbf_ref, carry):
        @pl.when(pl.program_id(0) == 0)
        def _():
            carry[...] = jnp.zeros_like(carry)
            dbf_ref[...] = jnp.zeros_like(dbf_ref)
        lane = lax.broadcasted_iota(jnp.int32, (ATTN_W, F_PAD), 0)
        head = lax.broadcasted_iota(jnp.int32, (ATTN_W, F_PAD), 1)
        aux = (head // 2) * PAIR + HEAD_DIM * (1 - head % 2)
        valid = head < N_HEADS
        pick_r = (valid & (lane == aux + AUX_ROWSUM)).astype(F32)
        pick_c = (valid & (lane == aux + AUX_BIAS)).astype(F32)
        hp = lax.Precision.HIGHEST
        dcum = (jnp.dot(dq_ref[...], pick_r, preferred_element_type=F32, precision=lax.Precision.HIGH)
                + jnp.dot(dk_ref[...], pick_c, preferred_element_type=F32, precision=lax.Precision.HIGH))
        r = lax.broadcasted_iota(jnp.int32, (tm, tm), 0)
        c = lax.broadcasted_iota(jnp.int32, (tm, tm), 1)
        tri = (c >= r).astype(F32)
        rc = jnp.dot(tri, dcum, preferred_element_type=F32, precision=hp) + carry[...]
        carry[...] = rc[0:1, :]
        df = rc * sg_ref[...]
        df_ref[...] = df.astype(BF16)
        dbf_ref[...] += _rows8(df)

    rev = lambda i: nsteps - 1 - i
    return pl.pallas_call(
        body, name="fgate_bwd", grid=(nsteps,),
        in_specs=[pl.BlockSpec((tm, ATTN_W), lambda i: (rev(i), 0)), pl.BlockSpec((tm, ATTN_W), lambda i: (rev(i), 0)),
                  pl.BlockSpec((tm, F_PAD), lambda i: (rev(i), 0))],
        out_specs=[pl.BlockSpec((tm, F_PAD), lambda i: (rev(i), 0)), pl.BlockSpec((8, F_PAD), lambda i: (0, 0))],
        out_shape=[jax.ShapeDtypeStruct((t, F_PAD), BF16), jax.ShapeDtypeStruct((8, F_PAD), F32)],
        scratch_shapes=[pltpu.VMEM((1, F_PAD), F32)],
        compiler_params=_cp(("arbitrary",)),
    )(dqx, dkx, sgate)


def _inproj_bwd(dq, dk, dv, df, dbcx, dh, x, g, win_p, gblock, *, tm=512):
    t, d = x.shape
    tm = min(tm, t)
    nsteps = t // tm
    n_qkv = 3 * ATTN_W

    def body(dq_ref, dk_ref, dv_ref, df_ref, db_ref, dh_ref, x_ref, g_ref, w_ref, gb_ref, gx_ref, dg_ref, land_ref,
             *sems):
        scatter = _Exchange([gb_ref], [land_ref], *sems, gather=False)

        @pl.when(pl.program_id(0) == 0)
        def _():
            scatter.start()
            dg_ref[...] = jnp.zeros_like(dg_ref)
        dn = _dot_nt(df_ref[...], w_ref[:, n_qkv:n_qkv + F_PAD])
        for k, r in enumerate((dq_ref, dk_ref, dv_ref)):
            dn = dn + _dot_nt(r[...], w_ref[:, k * ATTN_W:(k + 1) * ATTN_W])
        for k in range(3):
            c0 = n_qkv + F_PAD + k * CONV_CH
            dn = dn + _dot_nt(db_ref[:, k * CONV_CH:(k + 1) * CONV_CH], w_ref[:, c0:c0 + CONV_CH])
        dx, dg = _norm_bwd(dn, x_ref[...], g_ref[...])
        gx_ref[...] = dh_ref[...] + dx
        dg_ref[...] += dg

        @pl.when(pl.program_id(0) == nsteps - 1)
        def _():
            scatter.wait()

    row = lambda n_: pl.BlockSpec((tm, n_), lambda i: (i, 0))
    return pl.pallas_call(
        body, name="inproj_bwd", grid=(nsteps,),
        in_specs=[row(ATTN_W), row(ATTN_W), row(ATTN_W), row(F_PAD), row(3 * CONV_CH), row(d), row(d),
                  _const_spec((1, d)), _const_spec(win_p.shape), HBM_SPEC],
        out_specs=[row(d), pl.BlockSpec((8, d), lambda i: (0, 0)), HBM_SPEC],
        out_shape=[jax.ShapeDtypeStruct((t, d), F32), jax.ShapeDtypeStruct((8, d), F32),
                   jax.ShapeDtypeStruct(gblock.shape, gblock.dtype)],
        scratch_shapes=_Exchange.scratch(1),
        compiler_params=_cp(("arbitrary",)),
    )(dq, dk, dv, df, dbcx, dh, x, g, win_p, gblock)


LATE = ("w_out_0", "w_up_0", "w_down_0", "pool_w_1", "w_up_1", "w_down_1")


def _local_step(x, target, gains, b_f, conv_w, pool_scale, win_p, shards):
    d = x.shape[1]
    n0, qkv, flog, bcx, cv = _norm_inproj(x, gains["mix0"], win_p, conv_w)
    q_aug_t, k_aug, v_aug, v_aug_t, sgate = _fgate_prep(flog, b_f, qkv)
    att, q_bwd, q_bwd_t, gathered = _attn_fwd(q_aug_t, k_aug, v_aug_t, [shards[n] for n in LATE])
    g = dict(zip(LATE, gathered))
    wout = g["w_out_0"].reshape(d, d)
    wup0, wup1 = g["w_up_0"], g["w_up_1"]
    wdown0, wdown1 = g["w_down_0"].reshape(-1, d), g["w_down_1"].reshape(-1, d)
    n_grp = len(POOL_WINDOWS)
    cg = d // n_grp
    poolw = g["pool_w_1"].reshape(N_DEV, n_grp, cg // N_DEV, cg).transpose(1, 0, 2, 3).reshape(n_grp, cg, cg)
    h1 = _outproj(att, cv, x, wout)
    h2, n1, a0, z0 = _mlp_fwd(h1, gains["ffn0"], wup0, wdown0, name="mlp_fwd0")
    h3, pooled = _pool_fwd(h2, gains["mix1"], poolw, pool_scale)
    loss, dh4, dg_final, n3, a1, z1 = _mlp_fwd_loss(h3, gains["ffn1"], wup1, wdown1, gains["final"], target,
                                                    name="mlp_fwd1")
    f = a1.shape[1]
    fb = f // N_DEV
    dh3, da1, dg_ffn1 = _mlp_bwd(dh4, h3, a1, gains["ffn1"], wup1, wdown1, name="mlp_bwd1")
    dwdown1 = _mm_tn(z1, dh4, name="dwdown1", ta=1024, tb=1024, tt=2048, out_dtype=BF16)
    dwup1 = _mm_tn(n3, da1, name="dwup1", ta=d, tb=fb, tt=4096, blocked_out=True, out_dtype=BF16)
    dh2, dpoolw, dscale, dg_mix1 = _pool_bwd(dh3, h2, pooled, gains["mix1"], poolw, pool_scale)
    dh1, da0, dg_ffn0 = _mlp_bwd(dh2, h1, a0, gains["ffn0"], wup0, wdown0, name="mlp_bwd0")
    dwdown0 = _mm_tn(z0, dh2, name="dwdown0", ta=1024, tb=1024, tt=2048, out_dtype=BF16)
    dwup0 = _mm_tn(n1, da0, name="dwup0", ta=d, tb=fb, tt=4096, blocked_out=True, out_dtype=BF16)
    do_aug, do_aug_t, dcv = _outproj_bwd(dh1, att, wout)
    dwout = _mm_tn_cat([att, cv], [dh1], name="dwout", tt=2048)
    dbcx, dconvw = _conv_bwd(bcx, dcv, conv_w)
    gblocks = {
        "w_out_0": dwout.reshape(N_DEV, d // N_DEV, d), "w_up_0": dwup0, "w_up_1": dwup1,
        "w_down_0": dwdown0.reshape(N_DEV, -1, d), "w_down_1": dwdown1.reshape(N_DEV, -1, d),
        "pool_w_1": dpoolw.astype(BF16).reshape(n_grp, N_DEV, cg // N_DEV, cg).transpose(1, 0, 2, 3).reshape(
            N_DEV, n_grp * (cg // N_DEV), cg),
    }
    (dq, dqx, dk, dkx, dv), landed = _attn_bwd(q_bwd, do_aug, q_bwd_t, do_aug_t, k_aug, v_aug,
                                               [gblocks[n] for n in LATE])
    df, dbf = _fgate_bwd(dqx, dkx, sgate)
    dwin = jnp.concatenate(
        [_mm_tn_cat([n0], [dq, dk, dv], name="dwin_qkv", tt=2048),
         _mm_tn(n0, df, name="dwin_f", ta=d, tb=128, tt=2048, out_dtype=BF16)[:, :N_HEADS],
         _mm_tn(n0, dbcx, name="dwin_bcx", ta=d, tb=512, tt=4096, out_dtype=BF16)], axis=1)
    dwin_blocks = dwin.reshape(d, N_DEV, dwin.shape[1] // N_DEV).transpose(1, 0, 2)
    grad_x, dg_mix0, landed_win = _inproj_bwd(dq, dk, dv, df, dbcx, dh1, x, gains["mix0"], win_p, dwin_blocks)
    small = dict(mix0=dg_mix0, ffn0=dg_ffn0, mix1=dg_mix1, pool_scale=dscale, ffn1=dg_ffn1, final=dg_final,
                 b_f=dbf, conv_w=dconvw)
    return loss, grad_x, dict(zip(LATE + ("w_in_0",), tuple(landed) + (landed_win,))), small


def _mesh_places():
    x, y, c = lax.axis_index("x"), lax.axis_index("y"), lax.axis_index("c")
    chips = [(1 - x, y), (x, 1 - y), (1 - x, 1 - y)]
    return (x, y, c), (x, y, 1 - c), chips


def _all_gather(shards):
    n = len(shards)

    def body(*refs):
        ins, outs = refs[:n], refs[n:2 * n]
        send_sems, recv_sems, local_sems = refs[2 * n:]
        me, sib, chips = _mesh_places()
        c = me[2]

        def copy(ai, k, block, to, src=None):
            dst = outs[ai].at[_slot(*block)]
            return pltpu.make_async_remote_copy(
                src_ref=dst if src is None else src, dst_ref=dst, send_sem=send_sems.at[7 * ai + k],
                recv_sem=recv_sems.at[7 * ai + k], device_id=to, device_id_type=MESH)

        mine = [pltpu.make_async_copy(ins[ai], outs[ai].at[_slot(*me)], local_sems.at[ai]) for ai in range(n)]
        for cp in mine:
            cp.start()
        first = []
        for ai in range(n):
            first.append(copy(ai, 0, me, sib, src=ins[ai]))
            first += [copy(ai, 1 + j, me, (*chip, c), src=ins[ai]) for j, chip in enumerate(chips)]
        for cp in first:
            cp.start()
        passed = []
        for ai in range(n):
            for j, chip in enumerate(chips):
                copy(ai, 1 + j, (*chip, c), me).wait_recv()
                cp = copy(ai, 4 + j, (*chip, c), sib)
                cp.start()
                passed.append(cp)
        for ai in range(n):
            copy(ai, 0, sib, me).wait_recv()
            for j, chip in enumerate(chips):
                copy(ai, 4 + j, (*chip, 1 - c), me).wait_recv()
        for cp in first + passed:
            cp.wait_send()
        for cp in mine:
            cp.wait()

    return pl.pallas_call(
        body, name="all_gather",
        in_specs=[HBM_SPEC] * n, out_specs=[HBM_SPEC] * n,
        out_shape=[jax.ShapeDtypeStruct((N_DEV,) + s.shape, s.dtype) for s in shards],
        scratch_shapes=[pltpu.SemaphoreType.DMA((7 * n,)), pltpu.SemaphoreType.DMA((7 * n,)),
                        pltpu.SemaphoreType.DMA((n,))],
    )(*shards)


SMALL_ROWS = 16


def _small_allreduce(parts):
    n, _, w = parts.shape
    assert n <= SMALL_ROWS

    def body(p_ref, o_ref, gath, send_sems, recv_sems):
        x, y, c = lax.axis_index("x"), lax.axis_index("y"), lax.axis_index("c")
        my = _slot(x, y, c)
        rows = [jnp.sum(p_ref[i], axis=0, keepdims=True) for i in range(n)]
        rows.append(jnp.zeros((SMALL_ROWS - n, w), F32))
        gath[my] = jnp.concatenate(rows, axis=0)
        copies = []
        for k in range(1, N_DEV):
            px, py, pc = x ^ (k >> 2), y ^ ((k >> 1) & 1), c ^ (k & 1)
            cp = pltpu.make_async_remote_copy(
                src_ref=gath.at[my], dst_ref=gath.at[my], send_sem=send_sems.at[k - 1], recv_sem=recv_sems.at[k - 1],
                device_id=(px, py, pc), device_id_type=MESH)
            cp.start()
            copies.append(cp)
        for cp in copies:
            cp.wait()
        acc = gath[0]
        for d in range(1, N_DEV):
            acc = acc + gath[d]
        o_ref[...] = acc

    return pl.pallas_call(
        body, name="small_allreduce",
        in_specs=[VMEM_SPEC], out_specs=VMEM_SPEC,
        out_shape=jax.ShapeDtypeStruct((SMALL_ROWS, w), F32),
        scratch_shapes=[pltpu.VMEM((N_DEV, SMALL_ROWS, w), F32), pltpu.SemaphoreType.DMA((N_DEV - 1,)),
                        pltpu.SemaphoreType.DMA((N_DEV - 1,))],
    )(parts)


def _adamw(g, w, m, v, *, name, tm=256):
    r, c = g.shape
    tm = tm if r % tm == 0 else r
    bc1 = 1.0 - ADAM_B1 ** ADAM_STEP
    bc2 = 1.0 - ADAM_B2 ** ADAM_STEP

    def body(g_ref, w_ref, m_ref, v_ref, d_ref, nm_ref, nv_ref):
        gv = g_ref[...]
        nm = ADAM_B1 * m_ref[...] + (1.0 - ADAM_B1) * gv
        nv = ADAM_B2 * v_ref[...] + (1.0 - ADAM_B2) * jnp.square(gv)
        nm_ref[...] = nm
        nv_ref[...] = nv
        d_ref[...] = -ADAM_LR * ((nm / bc1) / (jnp.sqrt(nv / bc2) + ADAM_EPS) + ADAM_WD * w_ref[...])

    blk = pl.BlockSpec((tm, c), lambda i: (i, 0))
    shp = jax.ShapeDtypeStruct((r, c), F32)
    return pl.pallas_call(
        body, name=name, grid=(r // tm,), in_specs=[blk] * 4, out_specs=[blk] * 3, out_shape=[shp] * 3,
        compiler_params=_cp(("parallel",)),
    )(g, w, m, v)


def _adamw_sum(parts, w, m, v, *, name, tm=128):
    _, r, c = parts.shape
    tm = tm if r % tm == 0 else r
    bc1 = 1.0 - ADAM_B1 ** ADAM_STEP
    bc2 = 1.0 - ADAM_B2 ** ADAM_STEP

    def body(p_ref, w_ref, m_ref, v_ref, g_ref, d_ref, nm_ref, nv_ref):
        gv = p_ref[0].astype(F32)
        for k in range(1, N_DEV):
            gv = gv + p_ref[k].astype(F32)
        g_ref[...] = gv
        nm = ADAM_B1 * m_ref[...] + (1.0 - ADAM_B1) * gv
        nv = ADAM_B2 * v_ref[...] + (1.0 - ADAM_B2) * jnp.square(gv)
        nm_ref[...] = nm
        nv_ref[...] = nv
        d_ref[...] = -ADAM_LR * ((nm / bc1) / (jnp.sqrt(nv / bc2) + ADAM_EPS) + ADAM_WD * w_ref[...])

    blk = pl.BlockSpec((tm, c), lambda i: (i, 0))
    shp = jax.ShapeDtypeStruct((r, c), F32)
    return pl.pallas_call(
        body, name=name, grid=(r // tm,), in_specs=[pl.BlockSpec((N_DEV, tm, c), lambda i: (0, i, 0))] + [blk] * 3,
        out_specs=[blk] * 4, out_shape=[shp] * 4, compiler_params=_cp(("parallel",)),
    )(parts, w, m, v)


BIG = ("w_in_0", "w_out_0", "w_up_0", "w_down_0", "pool_w_1", "w_up_1", "w_down_1")
SMALL = ("norm_mix_0", "norm_ffn_0", "norm_mix_1", "pool_scale_1", "norm_ffn_1", "final_norm", "b_f_0", "conv_w_0")
WEIGHTS = ("norm_mix_0", "w_in_0", "b_f_0", "conv_w_0", "w_out_0", "norm_ffn_0", "w_up_0", "w_down_0", "norm_mix_1",
           "pool_w_1", "pool_scale_1", "norm_ffn_1", "w_up_1", "w_down_1", "final_norm")


def _pad_to(a, rows, cols):
    return jnp.pad(a, ((0, rows - a.shape[0]), (0, cols - a.shape[1])))


def _pack_small(p, width):
    rows = [p[n].reshape(1, -1) for n in SMALL[:6]]
    rows.append(_pad_to(p["b_f_0"].reshape(1, -1), 1, width))
    rows.append(_pad_to(p["conv_w_0"], 3, width))
    return _pad_to(jnp.concatenate(rows, axis=0), SMALL_ROWS, width)


def _unpack_small(a, like):
    out = {n: a[i] for i, n in enumerate(SMALL[:6])}
    out["b_f_0"] = a[6, :like["b_f_0"].shape[0]]
    out["conv_w_0"] = a[7:10, :like["conv_w_0"].shape[1]]
    return out


def kernel(x, norm_mix_0, w_in_0, b_f_0, conv_w_0, w_out_0, norm_ffn_0, w_up_0, w_down_0, norm_mix_1, pool_w_1, pool_scale_1, norm_ffn_1, w_up_1, w_down_1, final_norm, loss_target, m_norm_mix_0, m_w_in_0, m_b_f_0, m_conv_w_0, m_w_out_0, m_norm_ffn_0, m_w_up_0, m_w_down_0, m_norm_mix_1, m_pool_w_1, m_pool_scale_1, m_norm_ffn_1, m_w_up_1, m_w_down_1, m_final_norm, v_norm_mix_0, v_w_in_0, v_b_f_0, v_conv_w_0, v_w_out_0, v_norm_ffn_0, v_w_up_0, v_w_down_0, v_norm_mix_1, v_pool_w_1, v_pool_scale_1, v_norm_ffn_1, v_w_up_1, v_w_down_1, v_final_norm):
    w = dict(norm_mix_0=norm_mix_0, w_in_0=w_in_0, b_f_0=b_f_0, conv_w_0=conv_w_0, w_out_0=w_out_0,
             norm_ffn_0=norm_ffn_0, w_up_0=w_up_0, w_down_0=w_down_0, norm_mix_1=norm_mix_1, pool_w_1=pool_w_1,
             pool_scale_1=pool_scale_1, norm_ffn_1=norm_ffn_1, w_up_1=w_up_1, w_down_1=w_down_1, final_norm=final_norm)
    m = dict(norm_mix_0=m_norm_mix_0, w_in_0=m_w_in_0, b_f_0=m_b_f_0, conv_w_0=m_conv_w_0, w_out_0=m_w_out_0,
             norm_ffn_0=m_norm_ffn_0, w_up_0=m_w_up_0, w_down_0=m_w_down_0, norm_mix_1=m_norm_mix_1,
             pool_w_1=m_pool_w_1, pool_scale_1=m_pool_scale_1, norm_ffn_1=m_norm_ffn_1, w_up_1=m_w_up_1,
             w_down_1=m_w_down_1, final_norm=m_final_norm)
    v = dict(norm_mix_0=v_norm_mix_0, w_in_0=v_w_in_0, b_f_0=v_b_f_0, conv_w_0=v_conv_w_0, w_out_0=v_w_out_0,
             norm_ffn_0=v_norm_ffn_0, w_up_0=v_w_up_0, w_down_0=v_w_down_0, norm_mix_1=v_norm_mix_1,
             pool_w_1=v_pool_w_1, pool_scale_1=v_pool_scale_1, norm_ffn_1=v_norm_ffn_1, w_up_1=v_w_up_1,
             w_down_1=v_w_down_1, final_norm=v_final_norm)
    d = x.shape[-1]
    n_in = w_in_0.shape[1] * N_DEV
    n_qkv = 3 * ATTN_W
    pool_g, pool_rows, pool_c = pool_w_1.shape

    def shard2d(p):
        return {n: (p[n].reshape(pool_g * pool_rows, pool_c) if n == "pool_w_1" else p[n]) for n in BIG}
    w2, m2, v2 = shard2d(w), shard2d(m), shard2d(v)

    conv_cols = conv_w_0.shape[1]
    win_g8, conv_g8 = _all_gather([w_in_0.astype(BF16), _pad_to(conv_w_0, 8, 128)])
    conv_full = conv_g8[:, :, :conv_cols].transpose(1, 0, 2).reshape(8, N_DEV * conv_cols)
    win = win_g8.transpose(1, 0, 2).reshape(d, n_in)
    win_p = jnp.concatenate([win[:, :n_qkv], _pad_to(win[:, n_qkv:n_qkv + N_HEADS], d, F_PAD),
                             win[:, n_qkv + N_HEADS:]], axis=1)

    gains = dict(mix0=norm_mix_0.reshape(1, d), ffn0=norm_ffn_0.reshape(1, d), mix1=norm_mix_1.reshape(1, d),
                 ffn1=norm_ffn_1.reshape(1, d), final=final_norm.reshape(1, d))
    dev = _slot(lax.axis_index("x"), lax.axis_index("y"), lax.axis_index("c"))
    loss8, grad_x, landed, small = _local_step(
        x[0], loss_target[0], gains, _pad_to(b_f_0.reshape(1, -1), 1, F_PAD), conv_full, pool_scale_1.reshape(1, d),
        win_p, {n: w2[n].astype(BF16) for n in LATE})
    parts = jnp.concatenate(
        [small[k][None] for k in ("mix0", "ffn0", "mix1", "pool_scale", "ffn1", "final")]
        + [_pad_to(small["b_f"], 8, d)[None], jnp.pad(small["conv_w"], ((0, 0), (0, 0), (0, d - CONV_CH))),
           _pad_to(loss8[0:1, 0:1], 8, d)[None]], axis=0)
    tot = _small_allreduce(parts)
    loss = tot[10, 0]
    conv_g = lax.dynamic_slice(tot, (7, dev * conv_cols), (3, conv_cols))
    gs = tot.at[7:10].set(_pad_to(conv_g, 3, d))

    grads, deltas, new_m, new_v = {}, {}, {}, {}
    for n in BIG:
        gr, dl, nm, nv = _adamw_sum(landed[n], w2[n], m2[n], v2[n], name="adamw_" + n)
        for dst, val in ((grads, gr), (deltas, dl), (new_m, nm), (new_v, nv)):
            dst[n] = val.reshape(w[n].shape)
    dl, nm, nv = _adamw(gs, _pack_small(w, d), _pack_small(m, d), _pack_small(v, d), name="adamw_small")
    for dst, val in ((grads, gs), (deltas, dl), (new_m, nm), (new_v, nv)):
        dst.update(_unpack_small(val, w))
    return (loss, grad_x[None], *[grads[n] for n in WEIGHTS], *[deltas[n] for n in WEIGHTS],
            *[new_m[n] for n in WEIGHTS], *[new_v[n] for n in WEIGHTS])
```

```python
import functools

import jax
import jax.numpy as jnp
from jax import lax
from jax.experimental import pallas as pl
from jax.experimental.pallas import tpu as pltpu

F32 = jnp.float32
BF16 = jnp.bfloat16

N_DEV = 8
N_HEADS = 8
HEAD_DIM = 64
PAIR = 2 * HEAD_DIM
ATTN_W = N_HEADS * HEAD_DIM
CONV_CH = 512
F_PAD = 128
POOL_WINDOWS = (2, 4, 8, 16)
POOL_HALO = 16
CONV_HALO = 16
RMS_EPS = 1e-6
Q_SCALE = HEAD_DIM ** -0.5
LOG2E = 1.4426950408889634
NEG = -1e30
AUX_BIAS = 0
AUX_LSE = 3
AUX_ROWSUM = 6
ADAM_LR, ADAM_B1, ADAM_B2, ADAM_EPS, ADAM_WD, ADAM_STEP = 0.001, 0.9, 0.999, 1e-08, 0.01, 10
MESH = pl.DeviceIdType.MESH
VMEM_LIMIT = 56 * 2**20


def _cp(sem=None, vmem=VMEM_LIMIT, **kw):
    return pltpu.CompilerParams(dimension_semantics=sem, vmem_limit_bytes=vmem, **kw)


def _dot(a, b):
    return jnp.dot(a, b, preferred_element_type=F32)


def _dot_nt(a, b):
    return lax.dot_general(a, b, (((1,), (1,)), ((), ())), preferred_element_type=F32)


def _dot_tn(a, b):
    return lax.dot_general(a, b, (((0,), (0,)), ((), ())), preferred_element_type=F32)


def _rstd(h):
    return lax.rsqrt(jnp.mean(h * h, axis=-1, keepdims=True) + RMS_EPS)


def _rows8(x):
    r, n = x.shape
    return jnp.sum(x.reshape(r // 8, 8, n), axis=0)


def _norm_bwd(dn, h, g):
    r = _rstd(h)
    xhat = h * r
    dy = dn * g
    dh = r * (dy - xhat * jnp.mean(dy * xhat, axis=-1, keepdims=True))
    return dh, _rows8(dn * xhat)


def _const_spec(shape):
    nd = len(shape)
    return pl.BlockSpec(shape, lambda *_: (0,) * nd, pipeline_mode=pl.Buffered(1))


HBM_SPEC = pl.BlockSpec(memory_space=pltpu.HBM)
VMEM_SPEC = pl.BlockSpec(memory_space=pltpu.VMEM)


def _slot(px, py, pc):
    return 4 * px + 2 * py + pc


class _Exchange:
    def __init__(self, srcs, dsts, send_sems, recv_sems, local_sems, gather):
        x, y, c = lax.axis_index("x"), lax.axis_index("y"), lax.axis_index("c")
        me = _slot(x, y, c)
        self.copies = []
        for a, (src, dst) in enumerate(zip(srcs, dsts)):
            self.copies.append(pltpu.make_async_copy(src if gather else src.at[me], dst.at[me], local_sems.at[a]))
            for k in range(1, N_DEV):
                px, py, pc = x ^ (k >> 2), y ^ ((k >> 1) & 1), c ^ (k & 1)
                self.copies.append(pltpu.make_async_remote_copy(
                    src_ref=src if gather else src.at[_slot(px, py, pc)], dst_ref=dst.at[me],
                    send_sem=send_sems.at[(N_DEV - 1) * a + k - 1], recv_sem=recv_sems.at[(N_DEV - 1) * a + k - 1],
                    device_id=(px, py, pc), device_id_type=MESH))

    def start(self):
        for cp in self.copies:
            cp.start()

    def wait(self):
        for cp in self.copies:
            cp.wait()

    @staticmethod
    def scratch(n):
        return [pltpu.SemaphoreType.DMA(((N_DEV - 1) * n,)), pltpu.SemaphoreType.DMA(((N_DEV - 1) * n,)),
                pltpu.SemaphoreType.DMA((n,))]


def _norm_inproj(x, g, win_p, conv_w, *, tm=512):
    t, d = x.shape
    n_all = win_p.shape[1]
    n_qkv = 3 * ATTN_W
    n_bcx = 3 * CONV_CH
    assert n_all == n_qkv + F_PAD + n_bcx
    tm = min(tm, t)
    ch = CONV_CH

    def body(x_ref, g_ref, w_ref, cw_ref, n_ref, qkv_ref, f_ref, bcx_ref, cv_ref, ext):
        h = x_ref[...]
        n = (h * _rstd(h) * g_ref[...]).astype(BF16)
        n_ref[...] = n
        for c0 in range(0, n_qkv, 512):
            acc = _dot(n, w_ref[:, c0:c0 + 512])
            if c0 < ATTN_W:
                acc = acc * (Q_SCALE * LOG2E)
            qkv_ref[:, c0:c0 + 512] = acc.astype(BF16)
        f_ref[...] = _dot(n, w_ref[:, n_qkv:n_qkv + F_PAD])
        bcx = []
        for k in range(3):
            c0 = n_qkv + F_PAD + k * ch
            v = _dot(n, w_ref[:, c0:c0 + ch]).astype(BF16)
            bcx_ref[:, k * ch:(k + 1) * ch] = v
            bcx.append(v.astype(F32))
        @pl.when(pl.program_id(0) == 0)
        def _():
            ext[tm:tm + CONV_HALO, :] = jnp.zeros((CONV_HALO, ch), F32)
        ext[0:CONV_HALO, :] = ext[tm:tm + CONV_HALO, :]
        ext[CONV_HALO:CONV_HALO + tm, :] = bcx[1] * bcx[2]
        conv = (cw_ref[0:1, :] * ext[CONV_HALO - 2:CONV_HALO - 2 + tm, :]
                + cw_ref[1:2, :] * ext[CONV_HALO - 1:CONV_HALO - 1 + tm, :]
                + cw_ref[2:3, :] * ext[CONV_HALO:CONV_HALO + tm, :])
        cv_ref[...] = (bcx[0] * conv).astype(BF16)

    return pl.pallas_call(
        body, name="norm_inproj", grid=(t // tm,),
        in_specs=[pl.BlockSpec((tm, d), lambda i: (i, 0)), _const_spec((1, d)), _const_spec((d, n_all)),
                  _const_spec((8, ch))],
        out_specs=[pl.BlockSpec((tm, d), lambda i: (i, 0)), pl.BlockSpec((tm, n_qkv), lambda i: (i, 0)),
                   pl.BlockSpec((tm, F_PAD), lambda i: (i, 0)), pl.BlockSpec((tm, n_bcx), lambda i: (i, 0)),
                   pl.BlockSpec((tm, ch), lambda i: (i, 0))],
        out_shape=[jax.ShapeDtypeStruct((t, d), BF16), jax.ShapeDtypeStruct((t, n_qkv), BF16),
                   jax.ShapeDtypeStruct((t, F_PAD), F32), jax.ShapeDtypeStruct((t, n_bcx), BF16),
                   jax.ShapeDtypeStruct((t, ch), BF16)],
        scratch_shapes=[pltpu.VMEM((CONV_HALO + tm, ch), F32)],
        compiler_params=_cp(("arbitrary",)),
    )(x, g, win_p, conv_w)


def _head_lanes(h):
    lane = lax.broadcasted_iota(jnp.int32, (1, PAIR), 1)
    hh = h % 2
    return lane, lane // HEAD_DIM == hh, HEAD_DIM * (1 - hh)


def _pieces(col):
    hi = col.astype(BF16).astype(F32)
    r1 = col - hi
    mid = r1.astype(BF16).astype(F32)
    lo = (r1 - mid).astype(BF16).astype(F32)
    return hi, mid, lo


def _put_pieces(lane, first, col, other):
    hi, mid, lo = _pieces(col)
    return jnp.where(lane == first, hi, jnp.where(lane == first + 1, mid, jnp.where(lane == first + 2, lo, other)))


def _fgate_prep(flog, b_f, qkv, *, tm=512):
    t = flog.shape[0]
    tm = min(tm, t)

    def body(f_ref, b_ref, qkv_ref, qat_ref, ka_ref, va_ref, vat_ref, sg_ref, carry):
        @pl.when(pl.program_id(0) == 0)
        def _():
            carry[...] = jnp.zeros_like(carry)
        z = f_ref[...] + b_ref[...]
        e = jnp.exp(-jnp.abs(z))
        logf = jnp.minimum(z, 0.0) - jnp.log(1.0 + e)
        sg_ref[...] = jnp.where(z >= 0, e, 1.0) / (1.0 + e)
        r = lax.broadcasted_iota(jnp.int32, (tm, tm), 0)
        c = lax.broadcasted_iota(jnp.int32, (tm, tm), 1)
        tri = (c <= r).astype(F32)
        cs = jnp.dot(tri, logf, preferred_element_type=F32, precision=lax.Precision.HIGHEST) + carry[...]
        carry[...] = cs[tm - 1:tm, :]
        cs2 = cs * LOG2E
        for h in range(N_HEADS):
            lane, head, aux = _head_lanes(h)
            p0 = (h // 2) * PAIR
            ones = ((lane >= aux + AUX_LSE) & (lane <= aux + AUX_ROWSUM)).astype(F32)
            bias = (lane >= aux + AUX_BIAS) & (lane < aux + AUX_BIAS + 3)
            k_aux = _put_pieces(lane, aux + AUX_BIAS, cs2[:, h:h + 1], ones)
            q_aug = jnp.where(head, qkv_ref[:, p0:p0 + PAIR].astype(F32), jnp.where(bias, -1.0, 0.0))
            v_aug = jnp.where(head, qkv_ref[:, 2 * ATTN_W + p0:2 * ATTN_W + p0 + PAIR].astype(F32),
                              jnp.where(bias, 1.0, 0.0))
            qat_ref[h] = q_aug.astype(BF16).T
            ka_ref[h] = jnp.where(head, qkv_ref[:, ATTN_W + p0:ATTN_W + p0 + PAIR], k_aux.astype(BF16))
            va_ref[h] = v_aug.astype(BF16)
            vat_ref[h] = v_aug.astype(BF16).T

    aug = lambda: pl.BlockSpec((N_HEADS, tm, PAIR), lambda i: (0, i, 0))
    aug_t = lambda: pl.BlockSpec((N_HEADS, PAIR, tm), lambda i: (0, 0, i))
    aug_shape = jax.ShapeDtypeStruct((N_HEADS, t, PAIR), BF16)
    aug_t_shape = jax.ShapeDtypeStruct((N_HEADS, PAIR, t), BF16)
    return pl.pallas_call(
        body, name="fgate_prep", grid=(t // tm,),
        in_specs=[pl.BlockSpec((tm, F_PAD), lambda i: (i, 0)), _const_spec((1, F_PAD)),
                  pl.BlockSpec((tm, 3 * ATTN_W), lambda i: (i, 0))],
        out_specs=[aug_t(), aug(), aug(), aug_t(), pl.BlockSpec((tm, F_PAD), lambda i: (i, 0))],
        out_shape=[aug_t_shape, aug_shape, aug_shape, aug_t_shape, jax.ShapeDtypeStruct((t, F_PAD), F32)],
        scratch_shapes=[pltpu.VMEM((1, F_PAD), F32)],
        compiler_params=_cp(("arbitrary",)),
    )(flog, b_f, qkv)


def _put_pieces_t(row, first, vec, other):
    hi, mid, lo = _pieces(vec)
    return jnp.where(row == first, hi, jnp.where(row == first + 1, mid, jnp.where(row == first + 2, lo, other)))


def _attn_fwd(q_aug_t, k_aug, v_aug_t, shards, *, tq=1024):
    t = k_aug.shape[1]
    tq = min(tq, t)
    tk = tq // 2
    nq = t // tq
    n_pairs = ATTN_W // PAIR
    n_sh = len(shards)

    def body(qt_ref, k_ref, vt_ref, *rest):
        o_ref, qb_ref, qbt_ref = rest[n_sh:n_sh + 3]
        s_scr = rest[2 * n_sh + 3]
        gather = _Exchange(rest[:n_sh], rest[n_sh + 3:2 * n_sh + 3], *rest[2 * n_sh + 4:], gather=True)
        i = pl.program_id(1)

        @pl.when((pl.program_id(0) == 0) & (i == 0))
        def _():
            gather.start()
        key = lax.broadcasted_iota(jnp.int32, (tk, tq), 0)
        qry = lax.broadcasted_iota(jnp.int32, (tk, tq), 1)
        qt = [qt_ref[0], qt_ref[1]]

        def logits(hh, tile, slot, diag):
            s = _dot(k_ref[hh, pl.ds(pl.multiple_of(tile * tk, tk), tk), :], qt[hh])
            if diag:
                s = jnp.where(key + (tile * tk - i * tq) <= qry, s, NEG)
            s_scr[hh, slot] = s
            return jnp.max(s, axis=0, keepdims=True)

        def probs(hh, tile, slot, m, acc, tmax):
            mn = jnp.maximum(m, tmax)
            p = jnp.exp2(s_scr[hh, slot] - mn).astype(BF16)
            acc = jnp.exp2(m - mn) * acc + _dot(vt_ref[hh, :, pl.ds(pl.multiple_of(tile * tk, tk), tk)], p)
            return mn, acc

        def advance(carry, prev, slot, nxt, diag=False):
            out = []
            for hh in range(2):
                m, acc, tmax = carry[hh]
                m, acc = probs(hh, prev, slot, m, acc, tmax)
                out.append((m, acc, logits(hh, nxt, 1 - slot, diag)))
            return tuple(out)

        def two_tiles(jj, carry):
            carry = advance(carry, jnp.where(jj == 0, 2 * i, 2 * jj - 1), 1, 2 * jj)
            return advance(carry, 2 * jj, 0, 2 * jj + 1)

        init = tuple((jnp.full((1, tq), NEG, F32), jnp.zeros((PAIR, tq), F32), logits(hh, 2 * i + 1, 0, True))
                     for hh in range(2))
        carry = advance(init, 2 * i + 1, 0, 2 * i, diag=True)
        carry = lax.fori_loop(0, i, two_tiles, carry)
        last = jnp.where(i == 0, 2 * i, 2 * i - 1)
        row = lax.broadcasted_iota(jnp.int32, (PAIR, 1), 0)
        res = []
        for hh in range(2):
            aux = HEAD_DIM * (1 - hh)
            m, acc, tmax = carry[hh]
            m, acc = probs(hh, last, 1, m, acc, tmax)
            l = acc[aux + AUX_BIAS:aux + AUX_BIAS + 1, :]
            qbt = _put_pieces_t(row, aux + AUX_LSE, -(m + jnp.log2(l)), qt[hh].astype(F32))
            qbt_ref[hh] = qbt.astype(BF16)
            qb_ref[hh] = qbt.astype(BF16).T
            res.append(acc * (1.0 / l))
        o_ref[...] = jnp.where(row < HEAD_DIM, res[0], res[1]).astype(BF16).T

        @pl.when((pl.program_id(0) == n_pairs - 1) & (i == nq - 1))
        def _():
            gather.wait()

    res = pl.pallas_call(
        body, name="attn_fwd", grid=(n_pairs, nq),
        in_specs=[pl.BlockSpec((2, PAIR, tq), lambda p, i: (p, 0, i)),
                  pl.BlockSpec((2, t, PAIR), lambda p, i: (p, 0, 0), pipeline_mode=pl.Buffered(1)),
                  pl.BlockSpec((2, PAIR, t), lambda p, i: (p, 0, 0), pipeline_mode=pl.Buffered(1))] + [HBM_SPEC] * n_sh,
        out_specs=[pl.BlockSpec((tq, PAIR), lambda p, i: (i, p)),
                   pl.BlockSpec((2, tq, PAIR), lambda p, i: (p, i, 0)),
                   pl.BlockSpec((2, PAIR, tq), lambda p, i: (p, 0, i))] + [HBM_SPEC] * n_sh,
        out_shape=[jax.ShapeDtypeStruct((t, ATTN_W), BF16), jax.ShapeDtypeStruct((N_HEADS, t, PAIR), BF16),
                   jax.ShapeDtypeStruct((N_HEADS, PAIR, t), BF16)]
        + [jax.ShapeDtypeStruct((N_DEV,) + s.shape, s.dtype) for s in shards],
        scratch_shapes=[pltpu.VMEM((2, 2, tk, tq), F32)] + _Exchange.scratch(n_sh),
        compiler_params=_cp(("arbitrary", "arbitrary")),
    )(q_aug_t, k_aug, v_aug_t, *shards)
    return res[0], res[1], res[2], res[3:]


def _prev_halo(tm, halo):
    return lambda i: (jnp.maximum(i * (tm // halo) - 1, 0), 0)


def _next_halo(tm, halo, t):
    return lambda i: (jnp.minimum((i + 1) * (tm // halo), t // halo - 1), 0)


def _mlp_tile(hh, g_ref, wu_ref, wd_ref, n_ref, a_ref, z_ref):
    n_blk, _, fb = wu_ref.shape
    n = (hh * _rstd(hh) * g_ref[...]).astype(BF16)
    n_ref[...] = n
    acc = hh
    for k in range(n_blk):
        a = _dot(n, wu_ref[k])
        zz = jnp.square(jnp.maximum(a, 0.0)).astype(BF16)
        a_ref[:, k * fb:(k + 1) * fb] = a.astype(BF16)
        z_ref[:, k * fb:(k + 1) * fb] = zz
        acc = acc + _dot(zz, wd_ref[k * fb:(k + 1) * fb, :])
    return acc


def _outproj(att, cv, x, wout, *, tm=512):
    t, d = x.shape
    tm = min(tm, t)

    def body(a_ref, c_ref, x_ref, w_ref, h_ref):
        h_ref[...] = x_ref[...] + _dot(a_ref[...], w_ref[0:ATTN_W, :]) + _dot(c_ref[...], w_ref[ATTN_W:, :])

    return pl.pallas_call(
        body, name="outproj", grid=(t // tm,),
        in_specs=[pl.BlockSpec((tm, ATTN_W), lambda i: (i, 0)), pl.BlockSpec((tm, CONV_CH), lambda i: (i, 0)),
                  pl.BlockSpec((tm, d), lambda i: (i, 0)), _const_spec(wout.shape)],
        out_specs=pl.BlockSpec((tm, d), lambda i: (i, 0)),
        out_shape=jax.ShapeDtypeStruct((t, d), F32),
        compiler_params=_cp(("parallel",)),
    )(att, cv, x, wout)


def _mlp_fwd(h, g, wup, wdown, *, name, tm=512):
    t, d = h.shape
    n_blk, _, fb = wup.shape
    f = n_blk * fb
    tm = min(tm, t)

    def body(h_ref, g_ref, wu_ref, wd_ref, ho_ref, n_ref, a_ref, z_ref):
        ho_ref[...] = _mlp_tile(h_ref[...], g_ref, wu_ref, wd_ref, n_ref, a_ref, z_ref)

    row = lambda n_: pl.BlockSpec((tm, n_), lambda i: (i, 0))
    return pl.pallas_call(
        body, name=name, grid=(t // tm,),
        in_specs=[row(d), _const_spec((1, d)), _const_spec(wup.shape), _const_spec(wdown.shape)],
        out_specs=[row(d), row(d), row(f), row(f)],
        out_shape=[jax.ShapeDtypeStruct((t, d), F32), jax.ShapeDtypeStruct((t, d), BF16),
                   jax.ShapeDtypeStruct((t, f), BF16), jax.ShapeDtypeStruct((t, f), BF16)],
        compiler_params=_cp(("parallel",)),
    )(h, g, wup, wdown)


def _mlp_fwd_loss(h, g, wup, wdown, g_out, target, *, name, tm=512):
    t, d = h.shape
    n_blk, _, fb = wup.shape
    f = n_blk * fb
    tm = min(tm, t)
    nsteps = t // tm

    def body(h_ref, g_ref, wu_ref, wd_ref, go_ref, y_ref, loss_ref, dh_ref, dg_ref, n_ref, a_ref, z_ref, lacc):
        i = pl.program_id(0)

        @pl.when(i == 0)
        def _():
            lacc[...] = jnp.zeros_like(lacc)
            dg_ref[...] = jnp.zeros_like(dg_ref)
        hv = _mlp_tile(h_ref[...], g_ref, wu_ref, wd_ref, n_ref, a_ref, z_ref)
        gv = go_ref[...]
        r = _rstd(hv)
        xhat = hv * r
        err = xhat * gv - y_ref[...]
        lacc[...] += _rows8(err * err)
        dout = err * (1.0 / d)
        dy = dout * gv
        dg_ref[...] += _rows8(dout * xhat)
        dh_ref[...] = r * (dy - xhat * jnp.mean(dy * xhat, axis=-1, keepdims=True))

        @pl.when(i == nsteps - 1)
        def _():
            loss_ref[...] = jnp.full(loss_ref.shape, (0.5 / d) * jnp.sum(lacc[...]), F32)

    row = lambda n_: pl.BlockSpec((tm, n_), lambda i: (i, 0))
    return pl.pallas_call(
        body, name=name, grid=(nsteps,),
        in_specs=[row(d), _const_spec((1, d)), _const_spec(wup.shape), _const_spec(wdown.shape), _const_spec((1, d)),
                  row(d)],
        out_specs=[pl.BlockSpec((8, 128), lambda i: (0, 0)), row(d), pl.BlockSpec((8, d), lambda i: (0, 0)),
                   row(d), row(f), row(f)],
        out_shape=[jax.ShapeDtypeStruct((8, 128), F32), jax.ShapeDtypeStruct((t, d), F32),
                   jax.ShapeDtypeStruct((8, d), F32), jax.ShapeDtypeStruct((t, d), BF16),
                   jax.ShapeDtypeStruct((t, f), BF16), jax.ShapeDtypeStruct((t, f), BF16)],
        scratch_shapes=[pltpu.VMEM((8, d), F32)],
        compiler_params=_cp(("arbitrary",)),
    )(h, g, wup, wdown, g_out, target)


def _pool_inv_count(i, tm):
    tglob = (i * tm + lax.broadcasted_iota(jnp.int32, (tm, 1), 0) + 1).astype(F32)
    return [1.0 / jnp.minimum(tglob, float(w)) for w in POOL_WINDOWS]


def _pool_fwd(h, g, poolw, scale, *, tm=512):
    t, d = h.shape
    tm = min(tm, t)
    cg = d // len(POOL_WINDOWS)

    def body(h_ref, hh_ref, g_ref, w_ref, s_ref, ho_ref, p_ref, ext):
        i = pl.program_id(0)
        hv = h_ref[...]
        halo = hh_ref[...]
        n = hv * _rstd(hv) * g_ref[...]
        ext[0:POOL_HALO, :] = jnp.where(i == 0, 0.0, halo * _rstd(halo) * g_ref[...])
        ext[POOL_HALO:POOL_HALO + tm, :] = n
        inv = _pool_inv_count(i, tm)
        for gi, w in enumerate(POOL_WINDOWS):
            cs = slice(gi * cg, (gi + 1) * cg)
            s = ext[POOL_HALO:POOL_HALO + tm, cs]
            for j in range(1, w):
                s = s + ext[POOL_HALO - j:POOL_HALO - j + tm, cs]
            pooled = (s * inv[gi] - n[:, cs]).astype(BF16)
            p_ref[:, cs] = pooled
            ho_ref[:, cs] = hv[:, cs] + _dot(pooled, w_ref[gi]) * s_ref[:, cs]

    row = lambda: pl.BlockSpec((tm, d), lambda i: (i, 0))
    return pl.pallas_call(
        body, name="pool_fwd", grid=(t // tm,),
        in_specs=[row(), pl.BlockSpec((POOL_HALO, d), _prev_halo(tm, POOL_HALO)), _const_spec((1, d)),
                  _const_spec(poolw.shape), _const_spec((1, d))],
        out_specs=[row(), row()],
        out_shape=[jax.ShapeDtypeStruct((t, d), F32), jax.ShapeDtypeStruct((t, d), BF16)],
        scratch_shapes=[pltpu.VMEM((POOL_HALO + tm, d), F32)],
        compiler_params=_cp(("parallel",)),
    )(h, h, g, poolw, scale)


def _mm_tn(a, b, *, name, ta, tb, tt, blocked_out=False, out_dtype=F32):
    t, ka = a.shape
    n = b.shape[1]
    ta, tb, tt = min(ta, ka), min(tb, n), min(tt, t)
    nt = t // tt

    def body(a_ref, b_ref, o_ref, acc):
        @pl.when(pl.program_id(2) == 0)
        def _():
            acc[...] = jnp.zeros_like(acc)
        acc[...] += _dot_tn(a_ref[...].astype(BF16), b_ref[...].astype(BF16))

        @pl.when(pl.program_id(2) == nt - 1)
        def _():
            o_ref[...] = acc[...].astype(out_dtype)

    if blocked_out:
        assert ta == ka
        out_shape = jax.ShapeDtypeStruct((n // tb, ka, tb), out_dtype)
        out_spec = pl.BlockSpec((None, ta, tb), lambda i, j, k: (j, i, 0))
    else:
        out_shape = jax.ShapeDtypeStruct((ka, n), out_dtype)
        out_spec = pl.BlockSpec((ta, tb), lambda i, j, k: (i, j))
    return pl.pallas_call(
        body, name=name, grid=(ka // ta, n // tb, nt),
        in_specs=[pl.BlockSpec((tt, ta), lambda i, j, k: (k, i)), pl.BlockSpec((tt, tb), lambda i, j, k: (k, j))],
        out_specs=out_spec, out_shape=out_shape, scratch_shapes=[pltpu.VMEM((ta, tb), F32)],
        compiler_params=_cp(("parallel", "parallel", "arbitrary")),
    )(a, b)


def _mm_tn_cat(a_list, b_list, *, name, tt, out_dtype=BF16):
    t = a_list[0].shape[0]
    ta, tb = a_list[0].shape[1], b_list[0].shape[1]
    na, nb = len(a_list), len(b_list)
    tt = min(tt, t)
    nt = t // tt

    def body(*refs):
        a_refs, b_refs, o_ref, acc = refs[:na], refs[na:na + nb], refs[na + nb], refs[na + nb + 1]
        i, j, k = pl.program_id(0), pl.program_id(1), pl.program_id(2)

        @pl.when(k == 0)
        def _():
            acc[...] = jnp.zeros_like(acc)
        for ia in range(na):
            for ib in range(nb):
                @pl.when((i == ia) & (j == ib))
                def _(ia=ia, ib=ib):
                    acc[...] += _dot_tn(a_refs[ia][...].astype(BF16), b_refs[ib][...].astype(BF16))

        @pl.when(k == nt - 1)
        def _():
            o_ref[...] = acc[...].astype(out_dtype)

    def held(m, axis):
        def index(i, j, k):
            cur = (i, j)[axis]
            return (jnp.where(cur == m, k, jnp.where(cur < m, 0, nt - 1)), 0)
        return index

    return pl.pallas_call(
        body, name=name, grid=(na, nb, nt),
        in_specs=[pl.BlockSpec((tt, ta), held(m, 0)) for m in range(na)]
        + [pl.BlockSpec((tt, tb), held(m, 1)) for m in range(nb)],
        out_specs=pl.BlockSpec((ta, tb), lambda i, j, k: (i, j)),
        out_shape=jax.ShapeDtypeStruct((na * ta, nb * tb), out_dtype), scratch_shapes=[pltpu.VMEM((ta, tb), F32)],
        compiler_params=_cp(("arbitrary", "arbitrary", "arbitrary")),
    )(*a_list, *b_list)


def _mlp_bwd(dho, h, a, g, wup, wdown, *, name, tm=512):
    t, d = h.shape
    n_blk, _, fb = wup.shape
    f = n_blk * fb
    tm = min(tm, t)

    def body(do_ref, h_ref, a_ref, g_ref, wu_ref, wd_ref, dh_ref, da_ref, dg_ref):
        @pl.when(pl.program_id(0) == 0)
        def _():
            dg_ref[...] = jnp.zeros_like(dg_ref)
        dho_v = do_ref[...]
        dob = dho_v.astype(BF16)
        dn = jnp.zeros((tm, d), F32)
        for k in range(n_blk):
            dz = _dot_nt(dob, wd_ref[k * fb:(k + 1) * fb, :])
            da = (dz * (2.0 * jnp.maximum(a_ref[:, k * fb:(k + 1) * fb].astype(F32), 0.0))).astype(BF16)
            da_ref[:, k * fb:(k + 1) * fb] = da
            dn = dn + _dot_nt(da, wu_ref[k])
        dh, dg = _norm_bwd(dn, h_ref[...], g_ref[...])
        dh_ref[...] = dho_v + dh
        dg_ref[...] += dg

    row = lambda n_: pl.BlockSpec((tm, n_), lambda i: (i, 0))
    return pl.pallas_call(
        body, name=name, grid=(t // tm,),
        in_specs=[row(d), row(d), row(f), _const_spec((1, d)), _const_spec(wup.shape), _const_spec(wdown.shape)],
        out_specs=[row(d), row(f), pl.BlockSpec((8, d), lambda i: (0, 0))],
        out_shape=[jax.ShapeDtypeStruct((t, d), F32), jax.ShapeDtypeStruct((t, f), BF16),
                   jax.ShapeDtypeStruct((8, d), F32)],
        compiler_params=_cp(("arbitrary",)),
    )(dho, h, a, g, wup, wdown)


def _pool_bwd(dho, h, pooled, g, poolw, scale, *, tm=512):
    t, d = h.shape
    tm = min(tm, t)
    ng = len(POOL_WINDOWS)
    cg = d // ng
    nsteps = t // tm

    def body(do_ref, dn_ref, h_ref, p_ref, g_ref, w_ref, s_ref, dh_ref, dw_ref, ds_ref, dg_ref, ext):
        i = pl.program_id(0)

        @pl.when(i == 0)
        def _():
            dw_ref[...] = jnp.zeros_like(dw_ref)
            ds_ref[...] = jnp.zeros_like(ds_ref)
            dg_ref[...] = jnp.zeros_like(dg_ref)
        dho_v = do_ref[...]
        sv = s_ref[...]
        dyp = (dho_v * sv).astype(BF16)
        dyp_halo = (dn_ref[...] * sv).astype(BF16)
        inv = _pool_inv_count(i, tm)
        tnext = ((i + 1) * tm + lax.broadcasted_iota(jnp.int32, (POOL_HALO, 1), 0) + 1).astype(F32)
        last = i == nsteps - 1
        ypre_parts, dpooled_parts = [], []
        for gi, w in enumerate(POOL_WINDOWS):
            cs = slice(gi * cg, (gi + 1) * cg)
            pg = p_ref[:, cs]
            ypre_parts.append(_dot(pg, w_ref[gi]))
            dw_ref[gi] += _dot_tn(pg, dyp[:, cs])
            dpool = _dot_nt(dyp[:, cs], w_ref[gi])
            dpooled_parts.append(dpool)
            ext[0:tm, cs] = dpool * inv[gi]
            dpool_halo = _dot_nt(dyp_halo[:, cs], w_ref[gi]) * (1.0 / jnp.minimum(tnext, float(w)))
            ext[tm:tm + POOL_HALO, cs] = jnp.where(last, 0.0, dpool_halo)
        ds_ref[...] += _rows8(dho_v * jnp.concatenate(ypre_parts, axis=1))
        dn_parts = []
        for gi, w in enumerate(POOL_WINDOWS):
            cs = slice(gi * cg, (gi + 1) * cg)
            s = ext[0:tm, cs]
            for j in range(1, w):
                s = s + ext[j:j + tm, cs]
            dn_parts.append(s - dpooled_parts[gi])
        dh, dg = _norm_bwd(jnp.concatenate(dn_parts, axis=1), h_ref[...], g_ref[...])
        dh_ref[...] = dho_v + dh
        dg_ref[...] += dg

    row = lambda: pl.BlockSpec((tm, d), lambda i: (i, 0))
    acc8 = lambda: pl.BlockSpec((8, d), lambda i: (0, 0))
    return pl.pallas_call(
        body, name="pool_bwd", grid=(nsteps,),
        in_specs=[row(), pl.BlockSpec((POOL_HALO, d), _next_halo(tm, POOL_HALO, t)), row(), row(),
                  _const_spec((1, d)), _const_spec(poolw.shape), _const_spec((1, d))],
        out_specs=[row(), pl.BlockSpec((ng, cg, cg), lambda i: (0, 0, 0)), acc8(), acc8()],
        out_shape=[jax.ShapeDtypeStruct((t, d), F32), jax.ShapeDtypeStruct((ng, cg, cg), F32),
                   jax.ShapeDtypeStruct((8, d), F32), jax.ShapeDtypeStruct((8, d), F32)],
        scratch_shapes=[pltpu.VMEM((tm + POOL_HALO, d), F32)],
        compiler_params=_cp(("arbitrary",)),
    )(dho, dho, h, pooled, g, poolw, scale)


def _outproj_conv_bwd(dh, o, wout, bcx, conv_w, *, tm=512):
    t, d = dh.shape
    tm = min(tm, t)
    ch = CONV_CH
    nsteps = t // tm

    def body(dh_ref, o_ref, w_ref, b_ref, c_ref, x_ref, hc_ref, hx_ref, cw_ref,
             da_ref, dat_ref, db_ref, dw_ref, ext_u, ext_d):
        s = pl.program_id(0)

        @pl.when(s == 0)
        def _():
            dw_ref[...] = jnp.zeros_like(dw_ref)
            ext_d[0:CONV_HALO, :] = jnp.zeros((CONV_HALO, ch), F32)
        dhb = dh_ref[...].astype(BF16)
        for p in range(ATTN_W // PAIR):
            datt = _dot_nt(dhb, w_ref[p * PAIR:(p + 1) * PAIR, :])
            prod = datt * o_ref[:, p * PAIR:(p + 1) * PAIR].astype(F32)
            for hh in range(2):
                lane, head, aux = _head_lanes(hh)
                delta = jnp.sum(jnp.where(head, prod, 0.0), axis=1, keepdims=True)
                aug = _put_pieces(lane, aux + AUX_BIAS, -delta, jnp.where(head, datt, 0.0))
                da_ref[2 * p + hh] = aug.astype(BF16)
                dat_ref[2 * p + hh] = aug.astype(BF16).T
        dcv = _dot_nt(dhb, w_ref[ATTN_W:, :])
        b, c, x = b_ref[...].astype(F32), c_ref[...].astype(F32), x_ref[...].astype(F32)
        ext_u[0:CONV_HALO, :] = jnp.where(s == nsteps - 1, 0.0, hc_ref[...].astype(F32) * hx_ref[...].astype(F32))
        ext_u[CONV_HALO:CONV_HALO + tm, :] = c * x
        dconv = dcv * b
        ext_d[tm:tm + CONV_HALO, :] = ext_d[0:CONV_HALO, :]
        ext_d[0:tm, :] = dconv
        u = [ext_u[CONV_HALO - 2 + k:CONV_HALO - 2 + k + tm, :] for k in range(3)]
        conv = cw_ref[0:1, :] * u[0] + cw_ref[1:2, :] * u[1] + cw_ref[2:3, :] * u[2]
        du = (cw_ref[2:3, :] * dconv + cw_ref[1:2, :] * ext_d[1:1 + tm, :] + cw_ref[0:1, :] * ext_d[2:2 + tm, :])
        db_ref[:, 0:ch] = (dcv * conv).astype(BF16)
        db_ref[:, ch:2 * ch] = (du * x).astype(BF16)
        db_ref[:, 2 * ch:3 * ch] = (du * c).astype(BF16)
        for k in range(3):
            dw_ref[k] += _rows8(dconv * u[k])

    rev = lambda s: nsteps - 1 - s
    row = lambda n_: pl.BlockSpec((tm, n_), lambda s: (rev(s), 0))
    col = lambda k: pl.BlockSpec((tm, ch), lambda s: (rev(s), k))
    prev = lambda k: pl.BlockSpec((CONV_HALO, ch), lambda s: (_prev_halo(tm, CONV_HALO)(rev(s))[0], k))
    return pl.pallas_call(
        body, name="outproj_conv_bwd", grid=(nsteps,),
        in_specs=[row(d), row(ATTN_W), _const_spec(wout.shape), col(0), col(1), col(2), prev(1), prev(2),
                  _const_spec((8, ch))],
        out_specs=[pl.BlockSpec((N_HEADS, tm, PAIR), lambda s: (0, rev(s), 0)),
                   pl.BlockSpec((N_HEADS, PAIR, tm), lambda s: (0, 0, rev(s))),
                   row(3 * ch), pl.BlockSpec((3, 8, ch), lambda s: (0, 0, 0))],
        out_shape=[jax.ShapeDtypeStruct((N_HEADS, t, PAIR), BF16), jax.ShapeDtypeStruct((N_HEADS, PAIR, t), BF16),
                   jax.ShapeDtypeStruct((t, 3 * ch), BF16), jax.ShapeDtypeStruct((3, 8, ch), F32)],
        scratch_shapes=[pltpu.VMEM((CONV_HALO + tm, ch), F32), pltpu.VMEM((tm + CONV_HALO, ch), F32)],
        compiler_params=_cp(("arbitrary",)),
    )(dh, o, wout, bcx, bcx, bcx, bcx, bcx, conv_w)


def _attn_bwd(q_bwd, do_aug, q_bwd_t, do_aug_t, k_aug, v_aug, gblocks, *, tq=1024):
    t = q_bwd.shape[1]
    tq = min(tq, t)
    tk = tq // 2
    nq, nk = t // tq, t // tk
    n_pairs = ATTN_W // PAIR
    n_g = len(gblocks)

    def body(q_ref, do_ref, qt_ref, dot_ref, k_ref, v_ref, *rest):
        dq_ref, dqx_ref, dk_ref, dkx_ref, dv_ref = rest[n_g:n_g + 5]
        dq_scr = rest[2 * n_g + 5]
        scatter = _Exchange(rest[:n_g], rest[n_g + 5:2 * n_g + 5], *rest[2 * n_g + 6:], gather=False)
        j = pl.program_id(1)

        @pl.when((pl.program_id(0) == 0) & (j == 0))
        def _():
            scatter.start()

        @pl.when(j == 0)
        def _():
            dq_scr[...] = jnp.zeros_like(dq_scr)
        k = [k_ref[0], k_ref[1]]
        v = [v_ref[0], v_ref[1]]

        def step(i, carry, diag, rows=tq, row0=0):
            qs = pl.multiple_of(i * tq + row0, tk)
            if diag:
                row = lax.broadcasted_iota(jnp.int32, (rows, tk), 0)
                col = lax.broadcasted_iota(jnp.int32, (rows, tk), 1)
            out = []
            for hh in range(2):
                dk_a, dv_a = carry[hh]
                q = q_ref[hh, pl.ds(qs, rows), :]
                dov = do_ref[hh, pl.ds(qs, rows), :]
                p = jnp.exp2(_dot_nt(q, k[hh]))
                if diag:
                    p = jnp.where(col + (j * tk - i * tq - row0) <= row, p, 0.0)
                ds = (p * _dot_nt(dov, v[hh])).astype(BF16)
                dv_a = dv_a + _dot(dot_ref[hh, :, pl.ds(qs, rows)], p.astype(BF16))
                dk_a = dk_a + _dot(qt_ref[hh, :, pl.ds(qs, rows)], ds)
                dq_scr[hh, pl.ds(qs, rows), :] += _dot(ds, k[hh])
                out.append((dk_a, dv_a))
            return tuple(out)

        zero = (jnp.zeros((PAIR, tk), F32), jnp.zeros((PAIR, tk), F32))
        carry = lax.cond(j % 2 == 0, lambda c: step(j // 2, c, True),
                         lambda c: step(j // 2, c, True, rows=tk, row0=tk), (zero, zero))
        full0 = j // 2 + 1
        odd = (nq - full0) % 2
        carry = lax.cond(odd == 1, lambda c: step(full0, c, False), lambda c: c, carry)
        (dk0, dv0), (dk1, dv1) = lax.fori_loop(
            0, (nq - full0) // 2, lambda ii, c: step(full0 + odd + 2 * ii, c, False, rows=2 * tq), carry)
        first_t = lax.broadcasted_iota(jnp.int32, (PAIR, 1), 0) < HEAD_DIM
        first = lax.broadcasted_iota(jnp.int32, (1, PAIR), 1) < HEAD_DIM
        dk_ref[...] = (jnp.where(first_t, dk0, dk1) * (1.0 / LOG2E)).astype(BF16).T
        dkx_ref[...] = jnp.where(first_t, dk1, dk0).T
        dv_ref[...] = jnp.where(first_t, dv0, dv1).astype(BF16).T

        @pl.when(j == nk - 1)
        def _():
            dq_ref[...] = (jnp.where(first, dq_scr[0], dq_scr[1]) * Q_SCALE).astype(BF16)
            dqx_ref[...] = jnp.where(first, dq_scr[1], dq_scr[0])

        @pl.when((pl.program_id(0) == n_pairs - 1) & (j == nk - 1))
        def _():
            scatter.wait()

    resident = lambda: pl.BlockSpec((2, t, PAIR), lambda p, j: (p, 0, 0), pipeline_mode=pl.Buffered(1))
    resident_t = lambda: pl.BlockSpec((2, PAIR, t), lambda p, j: (p, 0, 0), pipeline_mode=pl.Buffered(1))
    kv_in = lambda: pl.BlockSpec((2, tk, PAIR), lambda p, j: (p, j, 0))
    whole = lambda: pl.BlockSpec((t, PAIR), lambda p, j: (0, p))
    tile = lambda: pl.BlockSpec((tk, PAIR), lambda p, j: (j, p))
    b16 = jax.ShapeDtypeStruct((t, ATTN_W), BF16)
    f32 = jax.ShapeDtypeStruct((t, ATTN_W), F32)
    res = pl.pallas_call(
        body, name="attn_bwd", grid=(n_pairs, nk),
        in_specs=[resident(), resident(), resident_t(), resident_t(), kv_in(), kv_in()] + [HBM_SPEC] * n_g,
        out_specs=[whole(), whole(), tile(), tile(), tile()] + [HBM_SPEC] * n_g,
        out_shape=[b16, f32, b16, f32, b16] + [jax.ShapeDtypeStruct(g.shape, g.dtype) for g in gblocks],
        scratch_shapes=[pltpu.VMEM((2, t, PAIR), F32)] + _Exchange.scratch(n_g),
        compiler_params=_cp(("arbitrary", "arbitrary")),
    )(q_bwd, do_aug, q_bwd_t, do_aug_t, k_aug, v_aug, *gblocks)
    return res[:5], res[5:]


def _fgate_bwd(dqx, dkx, sgate, *, tm=256):
    t = sgate.shape[0]
    tm = min(tm, t)
    nsteps = t // tm

    def body(dq_ref, dk_ref, sg_ref, df_ref, dbf_ref, carry):
        @pl.when(pl.program_id(0) == 0)
        def _():
            carry[...] = jnp.zeros_like(carry)
            dbf_ref[...] = jnp.zeros_like(dbf_ref)
        lane = lax.broadcasted_iota(jnp.int32, (ATTN_W, F_PAD), 0)
        head = lax.broadcasted_iota(jnp.int32, (ATTN_W, F_PAD), 1)
        aux = (head // 2) * PAIR + HEAD_DIM * (1 - head % 2)
        valid = head < N_HEADS
        pick_r = (valid & (lane == aux + AUX_ROWSUM)).astype(F32)
        pick_c = (valid & (lane == aux + AUX_BIAS)).astype(F32)
        hp = lax.Precision.HIGHEST
        dcum = (jnp.dot(dq_ref[...], pick_r, preferred_element_type=F32, precision=lax.Precision.HIGH)
                + jnp.dot(dk_ref[...], pick_c, preferred_element_type=F32, precision=lax.Precision.HIGH))
        r = lax.broadcasted_iota(jnp.int32, (tm, tm), 0)
        c = lax.broadcasted_iota(jnp.int32, (tm, tm), 1)
        tri = (c >= r).astype(F32)
        rc = jnp.dot(tri, dcum, preferred_element_type=F32, precision=hp) + carry[...]
        carry[...] = rc[0:1, :]
        df = rc * sg_ref[...]
        df_ref[...] = df.astype(BF16)
        dbf_ref[...] += _rows8(df)

    rev = lambda i: nsteps - 1 - i
    return pl.pallas_call(
        body, name="fgate_bwd", grid=(nsteps,),
        in_specs=[pl.BlockSpec((tm, ATTN_W), lambda i: (rev(i), 0)), pl.BlockSpec((tm, ATTN_W), lambda i: (rev(i), 0)),
                  pl.BlockSpec((tm, F_PAD), lambda i: (rev(i), 0))],
        out_specs=[pl.BlockSpec((tm, F_PAD), lambda i: (rev(i), 0)), pl.BlockSpec((8, F_PAD), lambda i: (0, 0))],
        out_shape=[jax.ShapeDtypeStruct((t, F_PAD), BF16), jax.ShapeDtypeStruct((8, F_PAD), F32)],
        scratch_shapes=[pltpu.VMEM((1, F_PAD), F32)],
        compiler_params=_cp(("arbitrary",)),
    )(dqx, dkx, sgate)


def _inproj_bwd(dq, dk, dv, df, dbcx, dh, x, g, win_p, gblock, *, tm=512):
    t, d = x.shape
    tm = min(tm, t)
    nsteps = t // tm
    n_qkv = 3 * ATTN_W

    def body(dq_ref, dk_ref, dv_ref, df_ref, db_ref, dh_ref, x_ref, g_ref, w_ref, gb_ref, gx_ref, dg_ref, land_ref,
             *sems):
        scatter = _Exchange([gb_ref], [land_ref], *sems, gather=False)

        @pl.when(pl.program_id(0) == 0)
        def _():
            scatter.start()
            dg_ref[...] = jnp.zeros_like(dg_ref)
        dn = _dot_nt(df_ref[...], w_ref[:, n_qkv:n_qkv + F_PAD])
        for k, r in enumerate((dq_ref, dk_ref, dv_ref)):
            dn = dn + _dot_nt(r[...], w_ref[:, k * ATTN_W:(k + 1) * ATTN_W])
        for k in range(3):
            c0 = n_qkv + F_PAD + k * CONV_CH
            dn = dn + _dot_nt(db_ref[:, k * CONV_CH:(k + 1) * CONV_CH], w_ref[:, c0:c0 + CONV_CH])
        dx, dg = _norm_bwd(dn, x_ref[...], g_ref[...])
        gx_ref[...] = dh_ref[...] + dx
        dg_ref[...] += dg

        @pl.when(pl.program_id(0) == nsteps - 1)
        def _():
            scatter.wait()

    row = lambda n_: pl.BlockSpec((tm, n_), lambda i: (i, 0))
    return pl.pallas_call(
        body, name="inproj_bwd", grid=(nsteps,),
        in_specs=[row(ATTN_W), row(ATTN_W), row(ATTN_W), row(F_PAD), row(3 * CONV_CH), row(d), row(d),
                  _const_spec((1, d)), _const_spec(win_p.shape), HBM_SPEC],
        out_specs=[row(d), pl.BlockSpec((8, d), lambda i: (0, 0)), HBM_SPEC],
        out_shape=[jax.ShapeDtypeStruct((t, d), F32), jax.ShapeDtypeStruct((8, d), F32),
                   jax.ShapeDtypeStruct(gblock.shape, gblock.dtype)],
        scratch_shapes=_Exchange.scratch(1),
        compiler_params=_cp(("arbitrary",)),
    )(dq, dk, dv, df, dbcx, dh, x, g, win_p, gblock)


LATE = ("w_out_0", "w_up_0", "w_down_0", "pool_w_1", "w_up_1", "w_down_1")


def _local_step(x, target, gains, b_f, conv_w, pool_scale, win_p, shards):
    d = x.shape[1]
    n0, qkv, flog, bcx, cv = _norm_inproj(x, gains["mix0"], win_p, conv_w)
    q_aug_t, k_aug, v_aug, v_aug_t, sgate = _fgate_prep(flog, b_f, qkv)
    att, q_bwd, q_bwd_t, gathered = _attn_fwd(q_aug_t, k_aug, v_aug_t, [shards[n] for n in LATE])
    g = dict(zip(LATE, gathered))
    wout = g["w_out_0"].reshape(d, d)
    wup0, wup1 = g["w_up_0"], g["w_up_1"]
    wdown0, wdown1 = g["w_down_0"].reshape(-1, d), g["w_down_1"].reshape(-1, d)
    n_grp = len(POOL_WINDOWS)
    cg = d // n_grp
    poolw = g["pool_w_1"].reshape(N_DEV, n_grp, cg // N_DEV, cg).transpose(1, 0, 2, 3).reshape(n_grp, cg, cg)
    h1 = _outproj(att, cv, x, wout)
    h2, n1, a0, z0 = _mlp_fwd(h1, gains["ffn0"], wup0, wdown0, name="mlp_fwd0")
    h3, pooled = _pool_fwd(h2, gains["mix1"], poolw, pool_scale)
    loss, dh4, dg_final, n3, a1, z1 = _mlp_fwd_loss(h3, gains["ffn1"], wup1, wdown1, gains["final"], target,
                                                    name="mlp_fwd1")
    f = a1.shape[1]
    fb = f // N_DEV
    dh3, da1, dg_ffn1 = _mlp_bwd(dh4, h3, a1, gains["ffn1"], wup1, wdown1, name="mlp_bwd1")
    dwdown1 = _mm_tn(z1, dh4, name="dwdown1", ta=1024, tb=1024, tt=2048, out_dtype=BF16)
    dwup1 = _mm_tn(n3, da1, name="dwup1", ta=d, tb=fb, tt=4096, blocked_out=True, out_dtype=BF16)
    dh2, dpoolw, dscale, dg_mix1 = _pool_bwd(dh3, h2, pooled, gains["mix1"], poolw, pool_scale)
    dh1, da0, dg_ffn0 = _mlp_bwd(dh2, h1, a0, gains["ffn0"], wup0, wdown0, name="mlp_bwd0")
    dwdown0 = _mm_tn(z0, dh2, name="dwdown0", ta=1024, tb=1024, tt=2048, out_dtype=BF16)
    dwup0 = _mm_tn(n1, da0, name="dwup0", ta=d, tb=fb, tt=4096, blocked_out=True, out_dtype=BF16)
    do_aug, do_aug_t, dbcx, dconvw = _outproj_conv_bwd(dh1, att, wout, bcx, conv_w)
    dwout = _mm_tn_cat([att, cv], [dh1], name="dwout", tt=2048)
    gblocks = {
        "w_out_0": dwout.reshape(N_DEV, d // N_DEV, d), "w_up_0": dwup0, "w_up_1": dwup1,
        "w_down_0": dwdown0.reshape(N_DEV, -1, d), "w_down_1": dwdown1.reshape(N_DEV, -1, d),
        "pool_w_1": dpoolw.astype(BF16).reshape(n_grp, N_DEV, cg // N_DEV, cg).transpose(1, 0, 2, 3).reshape(
            N_DEV, n_grp * (cg // N_DEV), cg),
    }
    (dq, dqx, dk, dkx, dv), landed = _attn_bwd(q_bwd, do_aug, q_bwd_t, do_aug_t, k_aug, v_aug,
                                               [gblocks[n] for n in LATE])
    df, dbf = _fgate_bwd(dqx, dkx, sgate)
    dwin = jnp.concatenate(
        [_mm_tn_cat([n0], [dq, dk, dv], name="dwin_qkv", tt=2048),
         _mm_tn(n0, df, name="dwin_f", ta=d, tb=128, tt=2048, out_dtype=BF16)[:, :N_HEADS],
         _mm_tn(n0, dbcx, name="dwin_bcx", ta=d, tb=512, tt=4096, out_dtype=BF16)], axis=1)
    dwin_blocks = dwin.reshape(d, N_DEV, dwin.shape[1] // N_DEV).transpose(1, 0, 2)
    grad_x, dg_mix0, landed_win = _inproj_bwd(dq, dk, dv, df, dbcx, dh1, x, gains["mix0"], win_p, dwin_blocks)
    small = dict(mix0=dg_mix0, ffn0=dg_ffn0, mix1=dg_mix1, pool_scale=dscale, ffn1=dg_ffn1, final=dg_final,
                 b_f=dbf, conv_w=dconvw)
    return loss, grad_x, dict(zip(LATE + ("w_in_0",), tuple(landed) + (landed_win,))), small


def _mesh_places():
    x, y, c = lax.axis_index("x"), lax.axis_index("y"), lax.axis_index("c")
    chips = [(1 - x, y), (x, 1 - y), (1 - x, 1 - y)]
    return (x, y, c), (x, y, 1 - c), chips


def _all_gather(shards):
    n = len(shards)

    def body(*refs):
        ins, outs = refs[:n], refs[n:2 * n]
        send_sems, recv_sems, local_sems = refs[2 * n:]
        me, sib, chips = _mesh_places()
        c = me[2]

        def copy(ai, k, block, to, src=None):
            dst = outs[ai].at[_slot(*block)]
            return pltpu.make_async_remote_copy(
                src_ref=dst if src is None else src, dst_ref=dst, send_sem=send_sems.at[7 * ai + k],
                recv_sem=recv_sems.at[7 * ai + k], device_id=to, device_id_type=MESH)

        mine = [pltpu.make_async_copy(ins[ai], outs[ai].at[_slot(*me)], local_sems.at[ai]) for ai in range(n)]
        for cp in mine:
            cp.start()
        first = []
        for ai in range(n):
            first.append(copy(ai, 0, me, sib, src=ins[ai]))
            first += [copy(ai, 1 + j, me, (*chip, c), src=ins[ai]) for j, chip in enumerate(chips)]
        for cp in first:
            cp.start()
        passed = []
        for ai in range(n):
            for j, chip in enumerate(chips):
                copy(ai, 1 + j, (*chip, c), me).wait_recv()
                cp = copy(ai, 4 + j, (*chip, c), sib)
                cp.start()
                passed.append(cp)
        for ai in range(n):
            copy(ai, 0, sib, me).wait_recv()
            for j, chip in enumerate(chips):
                copy(ai, 4 + j, (*chip, 1 - c), me).wait_recv()
        for cp in first + passed:
            cp.wait_send()
        for cp in mine:
            cp.wait()

    return pl.pallas_call(
        body, name="all_gather",
        in_specs=[HBM_SPEC] * n, out_specs=[HBM_SPEC] * n,
        out_shape=[jax.ShapeDtypeStruct((N_DEV,) + s.shape, s.dtype) for s in shards],
        scratch_shapes=[pltpu.SemaphoreType.DMA((7 * n,)), pltpu.SemaphoreType.DMA((7 * n,)),
                        pltpu.SemaphoreType.DMA((n,))],
    )(*shards)


SMALL_ROWS = 16


def _small_allreduce(parts):
    n, _, w = parts.shape
    assert n <= SMALL_ROWS

    def body(p_ref, o_ref, gath, send_sems, recv_sems):
        x, y, c = lax.axis_index("x"), lax.axis_index("y"), lax.axis_index("c")
        my = _slot(x, y, c)
        rows = [jnp.sum(p_ref[i], axis=0, keepdims=True) for i in range(n)]
        rows.append(jnp.zeros((SMALL_ROWS - n, w), F32))
        gath[my] = jnp.concatenate(rows, axis=0)
        copies = []
        for k in range(1, N_DEV):
            px, py, pc = x ^ (k >> 2), y ^ ((k >> 1) & 1), c ^ (k & 1)
            cp = pltpu.make_async_remote_copy(
                src_ref=gath.at[my], dst_ref=gath.at[my], send_sem=send_sems.at[k - 1], recv_sem=recv_sems.at[k - 1],
                device_id=(px, py, pc), device_id_type=MESH)
            cp.start()
            copies.append(cp)
        for cp in copies:
            cp.wait()
        acc = gath[0]
        for d in range(1, N_DEV):
            acc = acc + gath[d]
        o_ref[...] = acc

    return pl.pallas_call(
        body, name="small_allreduce",
        in_specs=[VMEM_SPEC], out_specs=VMEM_SPEC,
        out_shape=jax.ShapeDtypeStruct((SMALL_ROWS, w), F32),
        scratch_shapes=[pltpu.VMEM((N_DEV, SMALL_ROWS, w), F32), pltpu.SemaphoreType.DMA((N_DEV - 1,)),
                        pltpu.SemaphoreType.DMA((N_DEV - 1,))],
    )(parts)


def _adamw(g, w, m, v, *, name, tm=256):
    r, c = g.shape
    tm = tm if r % tm == 0 else r
    bc1 = 1.0 - ADAM_B1 ** ADAM_STEP
    bc2 = 1.0 - ADAM_B2 ** ADAM_STEP

    def body(g_ref, w_ref, m_ref, v_ref, d_ref, nm_ref, nv_ref):
        gv = g_ref[...]
        nm = ADAM_B1 * m_ref[...] + (1.0 - ADAM_B1) * gv
        nv = ADAM_B2 * v_ref[...] + (1.0 - ADAM_B2) * jnp.square(gv)
        nm_ref[...] = nm
        nv_ref[...] = nv
        d_ref[...] = -ADAM_LR * ((nm / bc1) / (jnp.sqrt(nv / bc2) + ADAM_EPS) + ADAM_WD * w_ref[...])

    blk = pl.BlockSpec((tm, c), lambda i: (i, 0))
    shp = jax.ShapeDtypeStruct((r, c), F32)
    return pl.pallas_call(
        body, name=name, grid=(r // tm,), in_specs=[blk] * 4, out_specs=[blk] * 3, out_shape=[shp] * 3,
        compiler_params=_cp(("parallel",)),
    )(g, w, m, v)


def _adamw_sum(parts, w, m, v, *, name, tm=128):
    _, r, c = parts.shape
    tm = tm if r % tm == 0 else r
    bc1 = 1.0 - ADAM_B1 ** ADAM_STEP
    bc2 = 1.0 - ADAM_B2 ** ADAM_STEP

    def body(p_ref, w_ref, m_ref, v_ref, g_ref, d_ref, nm_ref, nv_ref):
        gv = p_ref[0].astype(F32)
        for k in range(1, N_DEV):
            gv = gv + p_ref[k].astype(F32)
        g_ref[...] = gv
        nm = ADAM_B1 * m_ref[...] + (1.0 - ADAM_B1) * gv
        nv = ADAM_B2 * v_ref[...] + (1.0 - ADAM_B2) * jnp.square(gv)
        nm_ref[...] = nm
        nv_ref[...] = nv
        d_ref[...] = -ADAM_LR * ((nm / bc1) / (jnp.sqrt(nv / bc2) + ADAM_EPS) + ADAM_WD * w_ref[...])

    blk = pl.BlockSpec((tm, c), lambda i: (i, 0))
    shp = jax.ShapeDtypeStruct((r, c), F32)
    return pl.pallas_call(
        body, name=name, grid=(r // tm,), in_specs=[pl.BlockSpec((N_DEV, tm, c), lambda i: (0, i, 0))] + [blk] * 3,
        out_specs=[blk] * 4, out_shape=[shp] * 4, compiler_params=_cp(("parallel",)),
    )(parts, w, m, v)


BIG = ("w_in_0", "w_out_0", "w_up_0", "w_down_0", "pool_w_1", "w_up_1", "w_down_1")
SMALL = ("norm_mix_0", "norm_ffn_0", "norm_mix_1", "pool_scale_1", "norm_ffn_1", "final_norm", "b_f_0", "conv_w_0")
WEIGHTS = ("norm_mix_0", "w_in_0", "b_f_0", "conv_w_0", "w_out_0", "norm_ffn_0", "w_up_0", "w_down_0", "norm_mix_1",
           "pool_w_1", "pool_scale_1", "norm_ffn_1", "w_up_1", "w_down_1", "final_norm")


def _pad_to(a, rows, cols):
    return jnp.pad(a, ((0, rows - a.shape[0]), (0, cols - a.shape[1])))


def _pack_small(p, width):
    rows = [p[n].reshape(1, -1) for n in SMALL[:6]]
    rows.append(_pad_to(p["b_f_0"].reshape(1, -1), 1, width))
    rows.append(_pad_to(p["conv_w_0"], 3, width))
    return _pad_to(jnp.concatenate(rows, axis=0), SMALL_ROWS, width)


def _unpack_small(a, like):
    out = {n: a[i] for i, n in enumerate(SMALL[:6])}
    out["b_f_0"] = a[6, :like["b_f_0"].shape[0]]
    out["conv_w_0"] = a[7:10, :like["conv_w_0"].shape[1]]
    return out


def kernel(x, norm_mix_0, w_in_0, b_f_0, conv_w_0, w_out_0, norm_ffn_0, w_up_0, w_down_0, norm_mix_1, pool_w_1, pool_scale_1, norm_ffn_1, w_up_1, w_down_1, final_norm, loss_target, m_norm_mix_0, m_w_in_0, m_b_f_0, m_conv_w_0, m_w_out_0, m_norm_ffn_0, m_w_up_0, m_w_down_0, m_norm_mix_1, m_pool_w_1, m_pool_scale_1, m_norm_ffn_1, m_w_up_1, m_w_down_1, m_final_norm, v_norm_mix_0, v_w_in_0, v_b_f_0, v_conv_w_0, v_w_out_0, v_norm_ffn_0, v_w_up_0, v_w_down_0, v_norm_mix_1, v_pool_w_1, v_pool_scale_1, v_norm_ffn_1, v_w_up_1, v_w_down_1, v_final_norm):
    w = dict(norm_mix_0=norm_mix_0, w_in_0=w_in_0, b_f_0=b_f_0, conv_w_0=conv_w_0, w_out_0=w_out_0,
             norm_ffn_0=norm_ffn_0, w_up_0=w_up_0, w_down_0=w_down_0, norm_mix_1=norm_mix_1, pool_w_1=pool_w_1,
             pool_scale_1=pool_scale_1, norm_ffn_1=norm_ffn_1, w_up_1=w_up_1, w_down_1=w_down_1, final_norm=final_norm)
    m = dict(norm_mix_0=m_norm_mix_0, w_in_0=m_w_in_0, b_f_0=m_b_f_0, conv_w_0=m_conv_w_0, w_out_0=m_w_out_0,
             norm_ffn_0=m_norm_ffn_0, w_up_0=m_w_up_0, w_down_0=m_w_down_0, norm_mix_1=m_norm_mix_1,
             pool_w_1=m_pool_w_1, pool_scale_1=m_pool_scale_1, norm_ffn_1=m_norm_ffn_1, w_up_1=m_w_up_1,
             w_down_1=m_w_down_1, final_norm=m_final_norm)
    v = dict(norm_mix_0=v_norm_mix_0, w_in_0=v_w_in_0, b_f_0=v_b_f_0, conv_w_0=v_conv_w_0, w_out_0=v_w_out_0,
             norm_ffn_0=v_norm_ffn_0, w_up_0=v_w_up_0, w_down_0=v_w_down_0, norm_mix_1=v_norm_mix_1,
             pool_w_1=v_pool_w_1, pool_scale_1=v_pool_scale_1, norm_ffn_1=v_norm_ffn_1, w_up_1=v_w_up_1,
             w_down_1=v_w_down_1, final_norm=v_final_norm)
    d = x.shape[-1]
    n_in = w_in_0.shape[1] * N_DEV
    n_qkv = 3 * ATTN_W
    pool_g, pool_rows, pool_c = pool_w_1.shape

    def shard2d(p):
        return {n: (p[n].reshape(pool_g * pool_rows, pool_c) if n == "pool_w_1" else p[n]) for n in BIG}
    w2, m2, v2 = shard2d(w), shard2d(m), shard2d(v)

    conv_cols = conv_w_0.shape[1]
    win_g8, conv_g8 = _all_gather([w_in_0.astype(BF16), _pad_to(conv_w_0, 8, 128)])
    conv_full = conv_g8[:, :, :conv_cols].transpose(1, 0, 2).reshape(8, N_DEV * conv_cols)
    win = win_g8.transpose(1, 0, 2).reshape(d, n_in)
    win_p = jnp.concatenate([win[:, :n_qkv], _pad_to(win[:, n_qkv:n_qkv + N_HEADS], d, F_PAD),
                             win[:, n_qkv + N_HEADS:]], axis=1)

    gains = dict(mix0=norm_mix_0.reshape(1, d), ffn0=norm_ffn_0.reshape(1, d), mix1=norm_mix_1.reshape(1, d),
                 ffn1=norm_ffn_1.reshape(1, d), final=final_norm.reshape(1, d))
    dev = _slot(lax.axis_index("x"), lax.axis_index("y"), lax.axis_index("c"))
    loss8, grad_x, landed, small = _local_step(
        x[0], loss_target[0], gains, _pad_to(b_f_0.reshape(1, -1), 1, F_PAD), conv_full, pool_scale_1.reshape(1, d),
        win_p, {n: w2[n].astype(BF16) for n in LATE})
    parts = jnp.concatenate(
        [small[k][None] for k in ("mix0", "ffn0", "mix1", "pool_scale", "ffn1", "final")]
        + [_pad_to(small["b_f"], 8, d)[None], jnp.pad(small["conv_w"], ((0, 0), (0, 0), (0, d - CONV_CH))),
           _pad_to(loss8[0:1, 0:1], 8, d)[None]], axis=0)
    tot = _small_allreduce(parts)
    loss = tot[10, 0]
    conv_g = lax.dynamic_slice(tot, (7, dev * conv_cols), (3, conv_cols))
    gs = tot.at[7:10].set(_pad_to(conv_g, 3, d))

    grads, deltas, new_m, new_v = {}, {}, {}, {}
    for n in BIG:
        gr, dl, nm, nv = _adamw_sum(landed[n], w2[n], m2[n], v2[n], name="adamw_" + n)
        for dst, val in ((grads, gr), (deltas, dl), (new_m, nm), (new_v, nv)):
            dst[n] = val.reshape(w[n].shape)
    dl, nm, nv = _adamw(gs, _pack_small(w, d), _pack_small(m, d), _pack_small(v, d), name="adamw_small")
    for dst, val in ((grads, gs), (deltas, dl), (new_m, nm), (new_v, nv)):
        dst.update(_unpack_small(val, w))
    return (loss, grad_x[None], *[grads[n] for n in WEIGHTS], *[deltas[n] for n in WEIGHTS],
            *[new_m[n] for n in WEIGHTS], *[new_v[n] for n in WEIGHTS])
```

```python
import functools

import jax
import jax.numpy as jnp
from jax import lax
from jax.experimental import pallas as pl
from jax.experimental.pallas import tpu as pltpu

F32 = jnp.float32
BF16 = jnp.bfloat16

N_DEV = 8
N_HEADS = 8
HEAD_DIM = 64
PAIR = 2 * HEAD_DIM
ATTN_W = N_HEADS * HEAD_DIM
CONV_CH = 512
F_PAD = 128
POOL_WINDOWS = (2, 4, 8, 16)
POOL_HALO = 16
CONV_HALO = 16
RMS_EPS = 1e-6
Q_SCALE = HEAD_DIM ** -0.5
LOG2E = 1.4426950408889634
NEG = -1e30
AUX_BIAS = 0
AUX_LSE = 3
AUX_ROWSUM = 6
ADAM_LR, ADAM_B1, ADAM_B2, ADAM_EPS, ADAM_WD, ADAM_STEP = 0.001, 0.9, 0.999, 1e-08, 0.01, 10
MESH = pl.DeviceIdType.MESH
VMEM_LIMIT = 56 * 2**20


def _cp(sem=None, vmem=VMEM_LIMIT, **kw):
    return pltpu.CompilerParams(dimension_semantics=sem, vmem_limit_bytes=vmem, **kw)


def _dot(a, b):
    return jnp.dot(a, b, preferred_element_type=F32)


def _dot_nt(a, b):
    return lax.dot_general(a, b, (((1,), (1,)), ((), ())), preferred_element_type=F32)


def _dot_tn(a, b):
    return lax.dot_general(a, b, (((0,), (0,)), ((), ())), preferred_element_type=F32)


def _rstd(h):
    return lax.rsqrt(jnp.mean(h * h, axis=-1, keepdims=True) + RMS_EPS)


def _rows8(x):
    r, n = x.shape
    return jnp.sum(x.reshape(r // 8, 8, n), axis=0)


def _norm_bwd(dn, h, g):
    r = _rstd(h)
    xhat = h * r
    dy = dn * g
    dh = r * (dy - xhat * jnp.mean(dy * xhat, axis=-1, keepdims=True))
    return dh, _rows8(dn * xhat)


def _const_spec(shape):
    nd = len(shape)
    return pl.BlockSpec(shape, lambda *_: (0,) * nd, pipeline_mode=pl.Buffered(1))


HBM_SPEC = pl.BlockSpec(memory_space=pltpu.HBM)
VMEM_SPEC = pl.BlockSpec(memory_space=pltpu.VMEM)


def _slot(px, py, pc):
    return 4 * px + 2 * py + pc


class _Exchange:
    def __init__(self, srcs, dsts, send_sems, recv_sems, local_sems, gather):
        x, y, c = lax.axis_index("x"), lax.axis_index("y"), lax.axis_index("c")
        me = _slot(x, y, c)
        self.copies = []
        for a, (src, dst) in enumerate(zip(srcs, dsts)):
            self.copies.append(pltpu.make_async_copy(src if gather else src.at[me], dst.at[me], local_sems.at[a]))
            for k in range(1, N_DEV):
                px, py, pc = x ^ (k >> 2), y ^ ((k >> 1) & 1), c ^ (k & 1)
                self.copies.append(pltpu.make_async_remote_copy(
                    src_ref=src if gather else src.at[_slot(px, py, pc)], dst_ref=dst.at[me],
                    send_sem=send_sems.at[(N_DEV - 1) * a + k - 1], recv_sem=recv_sems.at[(N_DEV - 1) * a + k - 1],
                    device_id=(px, py, pc), device_id_type=MESH))

    def start(self):
        for cp in self.copies:
            cp.start()

    def wait(self):
        for cp in self.copies:
            cp.wait()

    @staticmethod
    def scratch(n):
        return [pltpu.SemaphoreType.DMA(((N_DEV - 1) * n,)), pltpu.SemaphoreType.DMA(((N_DEV - 1) * n,)),
                pltpu.SemaphoreType.DMA((n,))]


def _norm_inproj(x, g, win_p, conv_w, *, tm=512):
    t, d = x.shape
    n_all = win_p.shape[1]
    n_qkv = 3 * ATTN_W
    n_bcx = 3 * CONV_CH
    assert n_all == n_qkv + F_PAD + n_bcx
    tm = min(tm, t)
    ch = CONV_CH

    def body(x_ref, g_ref, w_ref, cw_ref, n_ref, qkv_ref, f_ref, bcx_ref, cv_ref, ext):
        h = x_ref[...]
        n = (h * _rstd(h) * g_ref[...]).astype(BF16)
        n_ref[...] = n
        for c0 in range(0, n_qkv, 512):
            acc = _dot(n, w_ref[:, c0:c0 + 512])
            if c0 < ATTN_W:
                acc = acc * (Q_SCALE * LOG2E)
            qkv_ref[:, c0:c0 + 512] = acc.astype(BF16)
        f_ref[...] = _dot(n, w_ref[:, n_qkv:n_qkv + F_PAD])
        bcx = []
        for k in range(3):
            c0 = n_qkv + F_PAD + k * ch
            v = _dot(n, w_ref[:, c0:c0 + ch]).astype(BF16)
            bcx_ref[:, k * ch:(k + 1) * ch] = v
            bcx.append(v.astype(F32))
        @pl.when(pl.program_id(0) == 0)
        def _():
            ext[tm:tm + CONV_HALO, :] = jnp.zeros((CONV_HALO, ch), F32)
        ext[0:CONV_HALO, :] = ext[tm:tm + CONV_HALO, :]
        ext[CONV_HALO:CONV_HALO + tm, :] = bcx[1] * bcx[2]
        conv = (cw_ref[0:1, :] * ext[CONV_HALO - 2:CONV_HALO - 2 + tm, :]
                + cw_ref[1:2, :] * ext[CONV_HALO - 1:CONV_HALO - 1 + tm, :]
                + cw_ref[2:3, :] * ext[CONV_HALO:CONV_HALO + tm, :])
        cv_ref[...] = (bcx[0] * conv).astype(BF16)

    return pl.pallas_call(
        body, name="norm_inproj", grid=(t // tm,),
        in_specs=[pl.BlockSpec((tm, d), lambda i: (i, 0)), _const_spec((1, d)), _const_spec((d, n_all)),
                  _const_spec((8, ch))],
        out_specs=[pl.BlockSpec((tm, d), lambda i: (i, 0)), pl.BlockSpec((tm, n_qkv), lambda i: (i, 0)),
                   pl.BlockSpec((tm, F_PAD), lambda i: (i, 0)), pl.BlockSpec((tm, n_bcx), lambda i: (i, 0)),
                   pl.BlockSpec((tm, ch), lambda i: (i, 0))],
        out_shape=[jax.ShapeDtypeStruct((t, d), BF16), jax.ShapeDtypeStruct((t, n_qkv), BF16),
                   jax.ShapeDtypeStruct((t, F_PAD), F32), jax.ShapeDtypeStruct((t, n_bcx), BF16),
                   jax.ShapeDtypeStruct((t, ch), BF16)],
        scratch_shapes=[pltpu.VMEM((CONV_HALO + tm, ch), F32)],
        compiler_params=_cp(("arbitrary",)),
    )(x, g, win_p, conv_w)


def _head_lanes(h):
    lane = lax.broadcasted_iota(jnp.int32, (1, PAIR), 1)
    hh = h % 2
    return lane, lane // HEAD_DIM == hh, HEAD_DIM * (1 - hh)


def _pieces(col):
    hi = col.astype(BF16).astype(F32)
    r1 = col - hi
    mid = r1.astype(BF16).astype(F32)
    lo = (r1 - mid).astype(BF16).astype(F32)
    return hi, mid, lo


def _put_pieces(lane, first, col, other):
    hi, mid, lo = _pieces(col)
    return jnp.where(lane == first, hi, jnp.where(lane == first + 1, mid, jnp.where(lane == first + 2, lo, other)))


def _fgate_prep(flog, b_f, qkv, *, tm=512):
    t = flog.shape[0]
    tm = min(tm, t)

    def body(f_ref, b_ref, qkv_ref, qat_ref, ka_ref, va_ref, vat_ref, sg_ref, carry):
        @pl.when(pl.program_id(0) == 0)
        def _():
            carry[...] = jnp.zeros_like(carry)
        z = f_ref[...] + b_ref[...]
        e = jnp.exp(-jnp.abs(z))
        logf = jnp.minimum(z, 0.0) - jnp.log(1.0 + e)
        sg_ref[...] = jnp.where(z >= 0, e, 1.0) / (1.0 + e)
        r = lax.broadcasted_iota(jnp.int32, (tm, tm), 0)
        c = lax.broadcasted_iota(jnp.int32, (tm, tm), 1)
        tri = (c <= r).astype(F32)
        cs = jnp.dot(tri, logf, preferred_element_type=F32, precision=lax.Precision.HIGHEST) + carry[...]
        carry[...] = cs[tm - 1:tm, :]
        cs2 = cs * LOG2E
        for h in range(N_HEADS):
            lane, head, aux = _head_lanes(h)
            p0 = (h // 2) * PAIR
            ones = ((lane >= aux + AUX_LSE) & (lane <= aux + AUX_ROWSUM)).astype(F32)
            bias = (lane >= aux + AUX_BIAS) & (lane < aux + AUX_BIAS + 3)
            k_aux = _put_pieces(lane, aux + AUX_BIAS, cs2[:, h:h + 1], ones)
            q_aug = jnp.where(head, qkv_ref[:, p0:p0 + PAIR], jnp.where(bias, -1.0, 0.0).astype(BF16))
            v_aug = jnp.where(head, qkv_ref[:, 2 * ATTN_W + p0:2 * ATTN_W + p0 + PAIR],
                              jnp.where(bias, 1.0, 0.0).astype(BF16))
            qat_ref[h] = q_aug.T
            ka_ref[h] = jnp.where(head, qkv_ref[:, ATTN_W + p0:ATTN_W + p0 + PAIR], k_aux.astype(BF16))
            va_ref[h] = v_aug
            vat_ref[h] = v_aug.T

    aug = lambda: pl.BlockSpec((N_HEADS, tm, PAIR), lambda i: (0, i, 0))
    aug_t = lambda: pl.BlockSpec((N_HEADS, PAIR, tm), lambda i: (0, 0, i))
    aug_shape = jax.ShapeDtypeStruct((N_HEADS, t, PAIR), BF16)
    aug_t_shape = jax.ShapeDtypeStruct((N_HEADS, PAIR, t), BF16)
    return pl.pallas_call(
        body, name="fgate_prep", grid=(t // tm,),
        in_specs=[pl.BlockSpec((tm, F_PAD), lambda i: (i, 0)), _const_spec((1, F_PAD)),
                  pl.BlockSpec((tm, 3 * ATTN_W), lambda i: (i, 0))],
        out_specs=[aug_t(), aug(), aug(), aug_t(), pl.BlockSpec((tm, F_PAD), lambda i: (i, 0))],
        out_shape=[aug_t_shape, aug_shape, aug_shape, aug_t_shape, jax.ShapeDtypeStruct((t, F_PAD), F32)],
        scratch_shapes=[pltpu.VMEM((1, F_PAD), F32)],
        compiler_params=_cp(("arbitrary",)),
    )(flog, b_f, qkv)


def _put_pieces_t(row, first, vec, other):
    hi, mid, lo = _pieces(vec)
    return jnp.where(row == first, hi, jnp.where(row == first + 1, mid, jnp.where(row == first + 2, lo, other)))


def _attn_fwd(q_aug_t, k_aug, v_aug_t, shards, *, tq=1024):
    t = k_aug.shape[1]
    tq = min(tq, t)
    tk = tq // 2
    nq = t // tq
    n_pairs = ATTN_W // PAIR
    n_sh = len(shards)

    def body(qt_ref, k_ref, vt_ref, *rest):
        o_ref, qb_ref, qbt_ref = rest[n_sh:n_sh + 3]
        s_scr = rest[2 * n_sh + 3]
        gather = _Exchange(rest[:n_sh], rest[n_sh + 3:2 * n_sh + 3], *rest[2 * n_sh + 4:], gather=True)
        i = pl.program_id(1)

        @pl.when((pl.program_id(0) == 0) & (i == 0))
        def _():
            gather.start()
        key = lax.broadcasted_iota(jnp.int32, (tk, tq), 0)
        qry = lax.broadcasted_iota(jnp.int32, (tk, tq), 1)
        qt = [qt_ref[0], qt_ref[1]]

        def logits(hh, tile, slot, diag):
            s = _dot(k_ref[hh, pl.ds(pl.multiple_of(tile * tk, tk), tk), :], qt[hh])
            if diag:
                s = jnp.where(key + (tile * tk - i * tq) <= qry, s, NEG)
            s_scr[hh, slot] = s
            return jnp.max(s, axis=0, keepdims=True)

        def probs(hh, tile, slot, m, acc, tmax):
            mn = jnp.maximum(m, tmax)
            p = jnp.exp2(s_scr[hh, slot] - mn).astype(BF16)
            acc = jnp.exp2(m - mn) * acc + _dot(vt_ref[hh, :, pl.ds(pl.multiple_of(tile * tk, tk), tk)], p)
            return mn, acc

        def advance(carry, prev, slot, nxt, diag=False):
            out = []
            for hh in range(2):
                m, acc, tmax = carry[hh]
                m, acc = probs(hh, prev, slot, m, acc, tmax)
                out.append((m, acc, logits(hh, nxt, 1 - slot, diag)))
            return tuple(out)

        def two_tiles(jj, carry):
            carry = advance(carry, jnp.where(jj == 0, 2 * i, 2 * jj - 1), 1, 2 * jj)
            return advance(carry, 2 * jj, 0, 2 * jj + 1)

        init = tuple((jnp.full((1, tq), NEG, F32), jnp.zeros((PAIR, tq), F32), logits(hh, 2 * i + 1, 0, True))
                     for hh in range(2))
        carry = advance(init, 2 * i + 1, 0, 2 * i, diag=True)
        carry = lax.fori_loop(0, i, two_tiles, carry)
        last = jnp.where(i == 0, 2 * i, 2 * i - 1)
        row = lax.broadcasted_iota(jnp.int32, (PAIR, 1), 0)
        res = []
        for hh in range(2):
            aux = HEAD_DIM * (1 - hh)
            m, acc, tmax = carry[hh]
            m, acc = probs(hh, last, 1, m, acc, tmax)
            l = acc[aux + AUX_BIAS:aux + AUX_BIAS + 1, :]
            qbt = _put_pieces_t(row, aux + AUX_LSE, -(m + jnp.log2(l)), qt[hh].astype(F32))
            qbt_ref[hh] = qbt.astype(BF16)
            qb_ref[hh] = qbt.astype(BF16).T
            res.append(acc * (1.0 / l))
        o_ref[...] = jnp.where(row < HEAD_DIM, res[0], res[1]).astype(BF16).T

        @pl.when((pl.program_id(0) == n_pairs - 1) & (i == nq - 1))
        def _():
            gather.wait()

    res = pl.pallas_call(
        body, name="attn_fwd", grid=(n_pairs, nq),
        in_specs=[pl.BlockSpec((2, PAIR, tq), lambda p, i: (p, 0, i)),
                  pl.BlockSpec((2, t, PAIR), lambda p, i: (p, 0, 0), pipeline_mode=pl.Buffered(1)),
                  pl.BlockSpec((2, PAIR, t), lambda p, i: (p, 0, 0), pipeline_mode=pl.Buffered(1))] + [HBM_SPEC] * n_sh,
        out_specs=[pl.BlockSpec((tq, PAIR), lambda p, i: (i, p)),
                   pl.BlockSpec((2, tq, PAIR), lambda p, i: (p, i, 0)),
                   pl.BlockSpec((2, PAIR, tq), lambda p, i: (p, 0, i))] + [HBM_SPEC] * n_sh,
        out_shape=[jax.ShapeDtypeStruct((t, ATTN_W), BF16), jax.ShapeDtypeStruct((N_HEADS, t, PAIR), BF16),
                   jax.ShapeDtypeStruct((N_HEADS, PAIR, t), BF16)]
        + [jax.ShapeDtypeStruct((N_DEV,) + s.shape, s.dtype) for s in shards],
        scratch_shapes=[pltpu.VMEM((2, 2, tk, tq), F32)] + _Exchange.scratch(n_sh),
        compiler_params=_cp(("arbitrary", "arbitrary")),
    )(q_aug_t, k_aug, v_aug_t, *shards)
    return res[0], res[1], res[2], res[3:]


def _prev_halo(tm, halo):
    return lambda i: (jnp.maximum(i * (tm // halo) - 1, 0), 0)


def _next_halo(tm, halo, t):
    return lambda i: (jnp.minimum((i + 1) * (tm // halo), t // halo - 1), 0)


def _mlp_tile(hh, g_ref, wu_ref, wd_ref, n_ref, a_ref, z_ref):
    n_blk, _, fb = wu_ref.shape
    n = (hh * _rstd(hh) * g_ref[...]).astype(BF16)
    n_ref[...] = n
    acc = hh
    for k in range(n_blk):
        a = _dot(n, wu_ref[k])
        zz = jnp.square(jnp.maximum(a, 0.0)).astype(BF16)
        a_ref[:, k * fb:(k + 1) * fb] = a.astype(BF16)
        z_ref[:, k * fb:(k + 1) * fb] = zz
        acc = acc + _dot(zz, wd_ref[k * fb:(k + 1) * fb, :])
    return acc


def _outproj(att, cv, x, wout, *, tm=512):
    t, d = x.shape
    tm = min(tm, t)

    def body(a_ref, c_ref, x_ref, w_ref, h_ref):
        h_ref[...] = x_ref[...] + _dot(a_ref[...], w_ref[0:ATTN_W, :]) + _dot(c_ref[...], w_ref[ATTN_W:, :])

    return pl.pallas_call(
        body, name="outproj", grid=(t // tm,),
        in_specs=[pl.BlockSpec((tm, ATTN_W), lambda i: (i, 0)), pl.BlockSpec((tm, CONV_CH), lambda i: (i, 0)),
                  pl.BlockSpec((tm, d), lambda i: (i, 0)), _const_spec(wout.shape)],
        out_specs=pl.BlockSpec((tm, d), lambda i: (i, 0)),
        out_shape=jax.ShapeDtypeStruct((t, d), F32),
        compiler_params=_cp(("parallel",)),
    )(att, cv, x, wout)


def _mlp_fwd(h, g, wup, wdown, *, name, tm=512):
    t, d = h.shape
    n_blk, _, fb = wup.shape
    f = n_blk * fb
    tm = min(tm, t)

    def body(h_ref, g_ref, wu_ref, wd_ref, ho_ref, n_ref, a_ref, z_ref):
        ho_ref[...] = _mlp_tile(h_ref[...], g_ref, wu_ref, wd_ref, n_ref, a_ref, z_ref)

    row = lambda n_: pl.BlockSpec((tm, n_), lambda i: (i, 0))
    return pl.pallas_call(
        body, name=name, grid=(t // tm,),
        in_specs=[row(d), _const_spec((1, d)), _const_spec(wup.shape), _const_spec(wdown.shape)],
        out_specs=[row(d), row(d), row(f), row(f)],
        out_shape=[jax.ShapeDtypeStruct((t, d), F32), jax.ShapeDtypeStruct((t, d), BF16),
                   jax.ShapeDtypeStruct((t, f), BF16), jax.ShapeDtypeStruct((t, f), BF16)],
        compiler_params=_cp(("parallel",)),
    )(h, g, wup, wdown)


def _mlp_fwd_loss(h, g, wup, wdown, g_out, target, *, name, tm=512):
    t, d = h.shape
    n_blk, _, fb = wup.shape
    f = n_blk * fb
    tm = min(tm, t)
    nsteps = t // tm

    def body(h_ref, g_ref, wu_ref, wd_ref, go_ref, y_ref, loss_ref, dh_ref, dg_ref, n_ref, a_ref, z_ref, lacc):
        i = pl.program_id(0)

        @pl.when(i == 0)
        def _():
            lacc[...] = jnp.zeros_like(lacc)
            dg_ref[...] = jnp.zeros_like(dg_ref)
        hv = _mlp_tile(h_ref[...], g_ref, wu_ref, wd_ref, n_ref, a_ref, z_ref)
        gv = go_ref[...]
        r = _rstd(hv)
        xhat = hv * r
        err = xhat * gv - y_ref[...]
        lacc[...] += _rows8(err * err)
        dout = err * (1.0 / d)
        dy = dout * gv
        dg_ref[...] += _rows8(dout * xhat)
        dh_ref[...] = r * (dy - xhat * jnp.mean(dy * xhat, axis=-1, keepdims=True))

        @pl.when(i == nsteps - 1)
        def _():
            loss_ref[...] = jnp.full(loss_ref.shape, (0.5 / d) * jnp.sum(lacc[...]), F32)

    row = lambda n_: pl.BlockSpec((tm, n_), lambda i: (i, 0))
    return pl.pallas_call(
        body, name=name, grid=(nsteps,),
        in_specs=[row(d), _const_spec((1, d)), _const_spec(wup.shape), _const_spec(wdown.shape), _const_spec((1, d)),
                  row(d)],
        out_specs=[pl.BlockSpec((8, 128), lambda i: (0, 0)), row(d), pl.BlockSpec((8, d), lambda i: (0, 0)),
                   row(d), row(f), row(f)],
        out_shape=[jax.ShapeDtypeStruct((8, 128), F32), jax.ShapeDtypeStruct((t, d), F32),
                   jax.ShapeDtypeStruct((8, d), F32), jax.ShapeDtypeStruct((t, d), BF16),
                   jax.ShapeDtypeStruct((t, f), BF16), jax.ShapeDtypeStruct((t, f), BF16)],
        scratch_shapes=[pltpu.VMEM((8, d), F32)],
        compiler_params=_cp(("arbitrary",)),
    )(h, g, wup, wdown, g_out, target)


def _pool_inv_count(i, tm):
    tglob = (i * tm + lax.broadcasted_iota(jnp.int32, (tm, 1), 0) + 1).astype(F32)
    return [1.0 / jnp.minimum(tglob, float(w)) for w in POOL_WINDOWS]


def _pool_fwd(h, g, poolw, scale, *, tm=512):
    t, d = h.shape
    tm = min(tm, t)
    cg = d // len(POOL_WINDOWS)

    def body(h_ref, hh_ref, g_ref, w_ref, s_ref, ho_ref, p_ref, ext):
        i = pl.program_id(0)
        hv = h_ref[...]
        halo = hh_ref[...]
        n = hv * _rstd(hv) * g_ref[...]
        ext[0:POOL_HALO, :] = jnp.where(i == 0, 0.0, halo * _rstd(halo) * g_ref[...])
        ext[POOL_HALO:POOL_HALO + tm, :] = n
        inv = _pool_inv_count(i, tm)
        for gi, w in enumerate(POOL_WINDOWS):
            cs = slice(gi * cg, (gi + 1) * cg)
            s = ext[POOL_HALO:POOL_HALO + tm, cs]
            for j in range(1, w):
                s = s + ext[POOL_HALO - j:POOL_HALO - j + tm, cs]
            pooled = (s * inv[gi] - n[:, cs]).astype(BF16)
            p_ref[:, cs] = pooled
            ho_ref[:, cs] = hv[:, cs] + _dot(pooled, w_ref[gi]) * s_ref[:, cs]

    row = lambda: pl.BlockSpec((tm, d), lambda i: (i, 0))
    return pl.pallas_call(
        body, name="pool_fwd", grid=(t // tm,),
        in_specs=[row(), pl.BlockSpec((POOL_HALO, d), _prev_halo(tm, POOL_HALO)), _const_spec((1, d)),
                  _const_spec(poolw.shape), _const_spec((1, d))],
        out_specs=[row(), row()],
        out_shape=[jax.ShapeDtypeStruct((t, d), F32), jax.ShapeDtypeStruct((t, d), BF16)],
        scratch_shapes=[pltpu.VMEM((POOL_HALO + tm, d), F32)],
        compiler_params=_cp(("parallel",)),
    )(h, h, g, poolw, scale)


def _mm_tn(a, b, *, name, ta, tb, tt, blocked_out=False, out_dtype=F32):
    t, ka = a.shape
    n = b.shape[1]
    ta, tb, tt = min(ta, ka), min(tb, n), min(tt, t)
    nt = t // tt

    def body(a_ref, b_ref, o_ref, acc):
        @pl.when(pl.program_id(2) == 0)
        def _():
            acc[...] = jnp.zeros_like(acc)
        acc[...] += _dot_tn(a_ref[...].astype(BF16), b_ref[...].astype(BF16))

        @pl.when(pl.program_id(2) == nt - 1)
        def _():
            o_ref[...] = acc[...].astype(out_dtype)

    if blocked_out:
        assert ta == ka
        out_shape = jax.ShapeDtypeStruct((n // tb, ka, tb), out_dtype)
        out_spec = pl.BlockSpec((None, ta, tb), lambda i, j, k: (j, i, 0))
    else:
        out_shape = jax.ShapeDtypeStruct((ka, n), out_dtype)
        out_spec = pl.BlockSpec((ta, tb), lambda i, j, k: (i, j))
    return pl.pallas_call(
        body, name=name, grid=(ka // ta, n // tb, nt),
        in_specs=[pl.BlockSpec((tt, ta), lambda i, j, k: (k, i)), pl.BlockSpec((tt, tb), lambda i, j, k: (k, j))],
        out_specs=out_spec, out_shape=out_shape, scratch_shapes=[pltpu.VMEM((ta, tb), F32)],
        compiler_params=_cp(("parallel", "parallel", "arbitrary")),
    )(a, b)


def _mm_tn_cat(a_list, b_list, *, name, tt, out_dtype=BF16):
    t = a_list[0].shape[0]
    ta, tb = a_list[0].shape[1], b_list[0].shape[1]
    na, nb = len(a_list), len(b_list)
    tt = min(tt, t)
    nt = t // tt

    def body(*refs):
        a_refs, b_refs, o_ref, acc = refs[:na], refs[na:na + nb], refs[na + nb], refs[na + nb + 1]
        i, j, k = pl.program_id(0), pl.program_id(1), pl.program_id(2)

        @pl.when(k == 0)
        def _():
            acc[...] = jnp.zeros_like(acc)
        for ia in range(na):
            for ib in range(nb):
                @pl.when((i == ia) & (j == ib))
                def _(ia=ia, ib=ib):
                    acc[...] += _dot_tn(a_refs[ia][...].astype(BF16), b_refs[ib][...].astype(BF16))

        @pl.when(k == nt - 1)
        def _():
            o_ref[...] = acc[...].astype(out_dtype)

    def held(m, axis):
        def index(i, j, k):
            cur = (i, j)[axis]
            return (jnp.where(cur == m, k, jnp.where(cur < m, 0, nt - 1)), 0)
        return index

    return pl.pallas_call(
        body, name=name, grid=(na, nb, nt),
        in_specs=[pl.BlockSpec((tt, ta), held(m, 0)) for m in range(na)]
        + [pl.BlockSpec((tt, tb), held(m, 1)) for m in range(nb)],
        out_specs=pl.BlockSpec((ta, tb), lambda i, j, k: (i, j)),
        out_shape=jax.ShapeDtypeStruct((na * ta, nb * tb), out_dtype), scratch_shapes=[pltpu.VMEM((ta, tb), F32)],
        compiler_params=_cp(("arbitrary", "arbitrary", "arbitrary")),
    )(*a_list, *b_list)


def _mlp_bwd(dho, h, a, g, wup, wdown, *, name, tm=512):
    t, d = h.shape
    n_blk, _, fb = wup.shape
    f = n_blk * fb
    tm = min(tm, t)

    def body(do_ref, h_ref, a_ref, g_ref, wu_ref, wd_ref, dh_ref, da_ref, dg_ref):
        @pl.when(pl.program_id(0) == 0)
        def _():
            dg_ref[...] = jnp.zeros_like(dg_ref)
        dho_v = do_ref[...]
        dob = dho_v.astype(BF16)
        dn = jnp.zeros((tm, d), F32)
        for k in range(n_blk):
            dz = _dot_nt(dob, wd_ref[k * fb:(k + 1) * fb, :])
            da = (dz * (2.0 * jnp.maximum(a_ref[:, k * fb:(k + 1) * fb].astype(F32), 0.0))).astype(BF16)
            da_ref[:, k * fb:(k + 1) * fb] = da
            dn = dn + _dot_nt(da, wu_ref[k])
        dh, dg = _norm_bwd(dn, h_ref[...], g_ref[...])
        dh_ref[...] = dho_v + dh
        dg_ref[...] += dg

    row = lambda n_: pl.BlockSpec((tm, n_), lambda i: (i, 0))
    return pl.pallas_call(
        body, name=name, grid=(t // tm,),
        in_specs=[row(d), row(d), row(f), _const_spec((1, d)), _const_spec(wup.shape), _const_spec(wdown.shape)],
        out_specs=[row(d), row(f), pl.BlockSpec((8, d), lambda i: (0, 0))],
        out_shape=[jax.ShapeDtypeStruct((t, d), F32), jax.ShapeDtypeStruct((t, f), BF16),
                   jax.ShapeDtypeStruct((8, d), F32)],
        compiler_params=_cp(("arbitrary",)),
    )(dho, h, a, g, wup, wdown)


def _pool_bwd(dho, h, pooled, g, poolw, scale, *, tm=512):
    t, d = h.shape
    tm = min(tm, t)
    ng = len(POOL_WINDOWS)
    cg = d // ng
    nsteps = t // tm

    def body(do_ref, dn_ref, h_ref, p_ref, g_ref, w_ref, s_ref, dh_ref, dw_ref, ds_ref, dg_ref, ext):
        i = pl.program_id(0)

        @pl.when(i == 0)
        def _():
            dw_ref[...] = jnp.zeros_like(dw_ref)
            ds_ref[...] = jnp.zeros_like(ds_ref)
            dg_ref[...] = jnp.zeros_like(dg_ref)
        dho_v = do_ref[...]
        sv = s_ref[...]
        dyp = (dho_v * sv).astype(BF16)
        dyp_halo = (dn_ref[...] * sv).astype(BF16)
        inv = _pool_inv_count(i, tm)
        tnext = ((i + 1) * tm + lax.broadcasted_iota(jnp.int32, (POOL_HALO, 1), 0) + 1).astype(F32)
        last = i == nsteps - 1
        ypre_parts, dpooled_parts = [], []
        for gi, w in enumerate(POOL_WINDOWS):
            cs = slice(gi * cg, (gi + 1) * cg)
            pg = p_ref[:, cs]
            ypre_parts.append(_dot(pg, w_ref[gi]))
            dw_ref[gi] += _dot_tn(pg, dyp[:, cs])
            dpool = _dot_nt(dyp[:, cs], w_ref[gi])
            dpooled_parts.append(dpool)
            ext[0:tm, cs] = dpool * inv[gi]
            dpool_halo = _dot_nt(dyp_halo[:, cs], w_ref[gi]) * (1.0 / jnp.minimum(tnext, float(w)))
            ext[tm:tm + POOL_HALO, cs] = jnp.where(last, 0.0, dpool_halo)
        ds_ref[...] += _rows8(dho_v * jnp.concatenate(ypre_parts, axis=1))
        dn_parts = []
        for gi, w in enumerate(POOL_WINDOWS):
            cs = slice(gi * cg, (gi + 1) * cg)
            s = ext[0:tm, cs]
            for j in range(1, w):
                s = s + ext[j:j + tm, cs]
            dn_parts.append(s - dpooled_parts[gi])
        dh, dg = _norm_bwd(jnp.concatenate(dn_parts, axis=1), h_ref[...], g_ref[...])
        dh_ref[...] = dho_v + dh
        dg_ref[...] += dg

    row = lambda: pl.BlockSpec((tm, d), lambda i: (i, 0))
    acc8 = lambda: pl.BlockSpec((8, d), lambda i: (0, 0))
    return pl.pallas_call(
        body, name="pool_bwd", grid=(nsteps,),
        in_specs=[row(), pl.BlockSpec((POOL_HALO, d), _next_halo(tm, POOL_HALO, t)), row(), row(),
                  _const_spec((1, d)), _const_spec(poolw.shape), _const_spec((1, d))],
        out_specs=[row(), pl.BlockSpec((ng, cg, cg), lambda i: (0, 0, 0)), acc8(), acc8()],
        out_shape=[jax.ShapeDtypeStruct((t, d), F32), jax.ShapeDtypeStruct((ng, cg, cg), F32),
                   jax.ShapeDtypeStruct((8, d), F32), jax.ShapeDtypeStruct((8, d), F32)],
        scratch_shapes=[pltpu.VMEM((tm + POOL_HALO, d), F32)],
        compiler_params=_cp(("arbitrary",)),
    )(dho, dho, h, pooled, g, poolw, scale)


def _outproj_conv_bwd(dh, o, wout, bcx, conv_w, *, tm=512):
    t, d = dh.shape
    tm = min(tm, t)
    ch = CONV_CH
    nsteps = t // tm

    def body(dh_ref, o_ref, w_ref, b_ref, c_ref, x_ref, hc_ref, hx_ref, cw_ref,
             da_ref, dat_ref, db_ref, dw_ref, ext_u, ext_d):
        s = pl.program_id(0)

        @pl.when(s == 0)
        def _():
            dw_ref[...] = jnp.zeros_like(dw_ref)
            ext_d[0:CONV_HALO, :] = jnp.zeros((CONV_HALO, ch), F32)
        dhb = dh_ref[...].astype(BF16)
        for p in range(ATTN_W // PAIR):
            datt = _dot_nt(dhb, w_ref[p * PAIR:(p + 1) * PAIR, :])
            prod = datt * o_ref[:, p * PAIR:(p + 1) * PAIR].astype(F32)
            for hh in range(2):
                lane, head, aux = _head_lanes(hh)
                delta = jnp.sum(jnp.where(head, prod, 0.0), axis=1, keepdims=True)
                aug = _put_pieces(lane, aux + AUX_BIAS, -delta, jnp.where(head, datt, 0.0))
                da_ref[2 * p + hh] = aug.astype(BF16)
                dat_ref[2 * p + hh] = aug.astype(BF16).T
        dcv = _dot_nt(dhb, w_ref[ATTN_W:, :])
        b, c, x = b_ref[...].astype(F32), c_ref[...].astype(F32), x_ref[...].astype(F32)
        ext_u[0:CONV_HALO, :] = jnp.where(s == nsteps - 1, 0.0, hc_ref[...].astype(F32) * hx_ref[...].astype(F32))
        ext_u[CONV_HALO:CONV_HALO + tm, :] = c * x
        dconv = dcv * b
        ext_d[tm:tm + CONV_HALO, :] = ext_d[0:CONV_HALO, :]
        ext_d[0:tm, :] = dconv
        u = [ext_u[CONV_HALO - 2 + k:CONV_HALO - 2 + k + tm, :] for k in range(3)]
        conv = cw_ref[0:1, :] * u[0] + cw_ref[1:2, :] * u[1] + cw_ref[2:3, :] * u[2]
        du = (cw_ref[2:3, :] * dconv + cw_ref[1:2, :] * ext_d[1:1 + tm, :] + cw_ref[0:1, :] * ext_d[2:2 + tm, :])
        db_ref[:, 0:ch] = (dcv * conv).astype(BF16)
        db_ref[:, ch:2 * ch] = (du * x).astype(BF16)
        db_ref[:, 2 * ch:3 * ch] = (du * c).astype(BF16)
        for k in range(3):
            dw_ref[k] += _rows8(dconv * u[k])

    rev = lambda s: nsteps - 1 - s
    row = lambda n_: pl.BlockSpec((tm, n_), lambda s: (rev(s), 0))
    col = lambda k: pl.BlockSpec((tm, ch), lambda s: (rev(s), k))
    prev = lambda k: pl.BlockSpec((CONV_HALO, ch), lambda s: (_prev_halo(tm, CONV_HALO)(rev(s))[0], k))
    return pl.pallas_call(
        body, name="outproj_conv_bwd", grid=(nsteps,),
        in_specs=[row(d), row(ATTN_W), _const_spec(wout.shape), col(0), col(1), col(2), prev(1), prev(2),
                  _const_spec((8, ch))],
        out_specs=[pl.BlockSpec((N_HEADS, tm, PAIR), lambda s: (0, rev(s), 0)),
                   pl.BlockSpec((N_HEADS, PAIR, tm), lambda s: (0, 0, rev(s))),
                   row(3 * ch), pl.BlockSpec((3, 8, ch), lambda s: (0, 0, 0))],
        out_shape=[jax.ShapeDtypeStruct((N_HEADS, t, PAIR), BF16), jax.ShapeDtypeStruct((N_HEADS, PAIR, t), BF16),
                   jax.ShapeDtypeStruct((t, 3 * ch), BF16), jax.ShapeDtypeStruct((3, 8, ch), F32)],
        scratch_shapes=[pltpu.VMEM((CONV_HALO + tm, ch), F32), pltpu.VMEM((tm + CONV_HALO, ch), F32)],
        compiler_params=_cp(("arbitrary",)),
    )(dh, o, wout, bcx, bcx, bcx, bcx, bcx, conv_w)


def _attn_bwd(q_bwd, do_aug, q_bwd_t, do_aug_t, k_aug, v_aug, gblocks, *, tq=1024):
    t = q_bwd.shape[1]
    tq = min(tq, t)
    tk = tq // 2
    nq, nk = t // tq, t // tk
    n_pairs = ATTN_W // PAIR
    n_g = len(gblocks)

    def body(q_ref, do_ref, qt_ref, dot_ref, k_ref, v_ref, *rest):
        dq_ref, dqx_ref, dk_ref, dkx_ref, dv_ref = rest[n_g:n_g + 5]
        dq_scr = rest[2 * n_g + 5]
        scatter = _Exchange(rest[:n_g], rest[n_g + 5:2 * n_g + 5], *rest[2 * n_g + 6:], gather=False)
        j = pl.program_id(1)

        @pl.when((pl.program_id(0) == 0) & (j == 0))
        def _():
            scatter.start()

        @pl.when(j == 0)
        def _():
            dq_scr[...] = jnp.zeros_like(dq_scr)
        k = [k_ref[0], k_ref[1]]
        v = [v_ref[0], v_ref[1]]

        def step(i, carry, diag, rows=tq, row0=0):
            qs = pl.multiple_of(i * tq + row0, tk)
            if diag:
                row = lax.broadcasted_iota(jnp.int32, (rows, tk), 0)
                col = lax.broadcasted_iota(jnp.int32, (rows, tk), 1)
            out = []
            for hh in range(2):
                dk_a, dv_a = carry[hh]
                q = q_ref[hh, pl.ds(qs, rows), :]
                dov = do_ref[hh, pl.ds(qs, rows), :]
                p = jnp.exp2(_dot_nt(q, k[hh]))
                if diag:
                    p = jnp.where(col + (j * tk - i * tq - row0) <= row, p, 0.0)
                ds = (p * _dot_nt(dov, v[hh])).astype(BF16)
                dv_a = dv_a + _dot(dot_ref[hh, :, pl.ds(qs, rows)], p.astype(BF16))
                dk_a = dk_a + _dot(qt_ref[hh, :, pl.ds(qs, rows)], ds)
                dq_scr[hh, pl.ds(qs, rows), :] += _dot(ds, k[hh])
                out.append((dk_a, dv_a))
            return tuple(out)

        zero = (jnp.zeros((PAIR, tk), F32), jnp.zeros((PAIR, tk), F32))
        carry = lax.cond(j % 2 == 0, lambda c: step(j // 2, c, True),
                         lambda c: step(j // 2, c, True, rows=tk, row0=tk), (zero, zero))
        full0 = j // 2 + 1
        odd = (nq - full0) % 2
        carry = lax.cond(odd == 1, lambda c: step(full0, c, False), lambda c: c, carry)
        (dk0, dv0), (dk1, dv1) = lax.fori_loop(
            0, (nq - full0) // 2, lambda ii, c: step(full0 + odd + 2 * ii, c, False, rows=2 * tq), carry)
        first_t = lax.broadcasted_iota(jnp.int32, (PAIR, 1), 0) < HEAD_DIM
        first = lax.broadcasted_iota(jnp.int32, (1, PAIR), 1) < HEAD_DIM
        dk_ref[...] = (jnp.where(first_t, dk0, dk1) * (1.0 / LOG2E)).astype(BF16).T
        dkx_ref[...] = jnp.where(first_t, dk1, dk0).T
        dv_ref[...] = jnp.where(first_t, dv0, dv1).astype(BF16).T

        @pl.when(j == nk - 1)
        def _():
            dq_ref[...] = (jnp.where(first, dq_scr[0], dq_scr[1]) * Q_SCALE).astype(BF16)
            dqx_ref[...] = jnp.where(first, dq_scr[1], dq_scr[0])

        @pl.when((pl.program_id(0) == n_pairs - 1) & (j == nk - 1))
        def _():
            scatter.wait()

    resident = lambda: pl.BlockSpec((2, t, PAIR), lambda p, j: (p, 0, 0), pipeline_mode=pl.Buffered(1))
    resident_t = lambda: pl.BlockSpec((2, PAIR, t), lambda p, j: (p, 0, 0), pipeline_mode=pl.Buffered(1))
    kv_in = lambda: pl.BlockSpec((2, tk, PAIR), lambda p, j: (p, j, 0))
    whole = lambda: pl.BlockSpec((t, PAIR), lambda p, j: (0, p))
    tile = lambda: pl.BlockSpec((tk, PAIR), lambda p, j: (j, p))
    b16 = jax.ShapeDtypeStruct((t, ATTN_W), BF16)
    f32 = jax.ShapeDtypeStruct((t, ATTN_W), F32)
    res = pl.pallas_call(
        body, name="attn_bwd", grid=(n_pairs, nk),
        in_specs=[resident(), resident(), resident_t(), resident_t(), kv_in(), kv_in()] + [HBM_SPEC] * n_g,
        out_specs=[whole(), whole(), tile(), tile(), tile()] + [HBM_SPEC] * n_g,
        out_shape=[b16, f32, b16, f32, b16] + [jax.ShapeDtypeStruct(g.shape, g.dtype) for g in gblocks],
        scratch_shapes=[pltpu.VMEM((2, t, PAIR), F32)] + _Exchange.scratch(n_g),
        compiler_params=_cp(("arbitrary", "arbitrary")),
    )(q_bwd, do_aug, q_bwd_t, do_aug_t, k_aug, v_aug, *gblocks)
    return res[:5], res[5:]


def _fgate_bwd(dqx, dkx, sgate, *, tm=256):
    t = sgate.shape[0]
    tm = min(tm, t)
    nsteps = t // tm

    def body(dq_ref, dk_ref, sg_ref, df_ref, dbf_ref, carry):
        @pl.when(pl.program_id(0) == 0)
        def _():
            carry[...] = jnp.zeros_like(carry)
            dbf_ref[...] = jnp.zeros_like(dbf_ref)
        lane = lax.broadcasted_iota(jnp.int32, (ATTN_W, F_PAD), 0)
        head = lax.broadcasted_iota(jnp.int32, (ATTN_W, F_PAD), 1)
        aux = (head // 2) * PAIR + HEAD_DIM * (1 - head % 2)
        valid = head < N_HEADS
        pick_r = (valid & (lane == aux + AUX_ROWSUM)).astype(F32)
        pick_c = (valid & (lane == aux + AUX_BIAS)).astype(F32)
        hp = lax.Precision.HIGHEST
        dcum = (jnp.dot(dq_ref[...], pick_r, preferred_element_type=F32, precision=lax.Precision.HIGH)
                + jnp.dot(dk_ref[...], pick_c, preferred_element_type=F32, precision=lax.Precision.HIGH))
        r = lax.broadcasted_iota(jnp.int32, (tm, tm), 0)
        c = lax.broadcasted_iota(jnp.int32, (tm, tm), 1)
        tri = (c >= r).astype(F32)
        rc = jnp.dot(tri, dcum, preferred_element_type=F32, precision=hp) + carry[...]
        carry[...] = rc[0:1, :]
        df = rc * sg_ref[...]
        df_ref[...] = df.astype(BF16)
        dbf_ref[...] += _rows8(df)

    rev = lambda i: nsteps - 1 - i
    return pl.pallas_call(
        body, name="fgate_bwd", grid=(nsteps,),
        in_specs=[pl.BlockSpec((tm, ATTN_W), lambda i: (rev(i), 0)), pl.BlockSpec((tm, ATTN_W), lambda i: (rev(i), 0)),
                  pl.BlockSpec((tm, F_PAD), lambda i: (rev(i), 0))],
        out_specs=[pl.BlockSpec((tm, F_PAD), lambda i: (rev(i), 0)), pl.BlockSpec((8, F_PAD), lambda i: (0, 0))],
        out_shape=[jax.ShapeDtypeStruct((t, F_PAD), BF16), jax.ShapeDtypeStruct((8, F_PAD), F32)],
        scratch_shapes=[pltpu.VMEM((1, F_PAD), F32)],
        compiler_params=_cp(("arbitrary",)),
    )(dqx, dkx, sgate)


def _inproj_bwd(dq, dk, dv, df, dbcx, dh, x, g, win_p, gblock, *, tm=512):
    t, d = x.shape
    tm = min(tm, t)
    nsteps = t // tm
    n_qkv = 3 * ATTN_W

    def body(dq_ref, dk_ref, dv_ref, df_ref, db_ref, dh_ref, x_ref, g_ref, w_ref, gb_ref, gx_ref, dg_ref, land_ref,
             *sems):
        scatter = _Exchange([gb_ref], [land_ref], *sems, gather=False)

        @pl.when(pl.program_id(0) == 0)
        def _():
            scatter.start()
            dg_ref[...] = jnp.zeros_like(dg_ref)
        dn = _dot_nt(df_ref[...], w_ref[:, n_qkv:n_qkv + F_PAD])
        for k, r in enumerate((dq_ref, dk_ref, dv_ref)):
            dn = dn + _dot_nt(r[...], w_ref[:, k * ATTN_W:(k + 1) * ATTN_W])
        for k in range(3):
            c0 = n_qkv + F_PAD + k * CONV_CH
            dn = dn + _dot_nt(db_ref[:, k * CONV_CH:(k + 1) * CONV_CH], w_ref[:, c0:c0 + CONV_CH])
        dx, dg = _norm_bwd(dn, x_ref[...], g_ref[...])
        gx_ref[...] = dh_ref[...] + dx
        dg_ref[...] += dg

        @pl.when(pl.program_id(0) == nsteps - 1)
        def _():
            scatter.wait()

    row = lambda n_: pl.BlockSpec((tm, n_), lambda i: (i, 0))
    return pl.pallas_call(
        body, name="inproj_bwd", grid=(nsteps,),
        in_specs=[row(ATTN_W), row(ATTN_W), row(ATTN_W), row(F_PAD), row(3 * CONV_CH), row(d), row(d),
                  _const_spec((1, d)), _const_spec(win_p.shape), HBM_SPEC],
        out_specs=[row(d), pl.BlockSpec((8, d), lambda i: (0, 0)), HBM_SPEC],
        out_shape=[jax.ShapeDtypeStruct((t, d), F32), jax.ShapeDtypeStruct((8, d), F32),
                   jax.ShapeDtypeStruct(gblock.shape, gblock.dtype)],
        scratch_shapes=_Exchange.scratch(1),
        compiler_params=_cp(("arbitrary",)),
    )(dq, dk, dv, df, dbcx, dh, x, g, win_p, gblock)


LATE = ("w_out_0", "w_up_0", "w_down_0", "pool_w_1", "w_up_1", "w_down_1")


def _local_step(x, target, gains, b_f, conv_w, pool_scale, win_p, shards):
    d = x.shape[1]
    n0, qkv, flog, bcx, cv = _norm_inproj(x, gains["mix0"], win_p, conv_w)
    q_aug_t, k_aug, v_aug, v_aug_t, sgate = _fgate_prep(flog, b_f, qkv)
    att, q_bwd, q_bwd_t, gathered = _attn_fwd(q_aug_t, k_aug, v_aug_t, [shards[n] for n in LATE])
    g = dict(zip(LATE, gathered))
    wout = g["w_out_0"].reshape(d, d)
    wup0, wup1 = g["w_up_0"], g["w_up_1"]
    wdown0, wdown1 = g["w_down_0"].reshape(-1, d), g["w_down_1"].reshape(-1, d)
    n_grp = len(POOL_WINDOWS)
    cg = d // n_grp
    poolw = g["pool_w_1"].reshape(N_DEV, n_grp, cg // N_DEV, cg).transpose(1, 0, 2, 3).reshape(n_grp, cg, cg)
    h1 = _outproj(att, cv, x, wout)
    h2, n1, a0, z0 = _mlp_fwd(h1, gains["ffn0"], wup0, wdown0, name="mlp_fwd0")
    h3, pooled = _pool_fwd(h2, gains["mix1"], poolw, pool_scale)
    loss, dh4, dg_final, n3, a1, z1 = _mlp_fwd_loss(h3, gains["ffn1"], wup1, wdown1, gains["final"], target,
                                                    name="mlp_fwd1")
    f = a1.shape[1]
    fb = f // N_DEV
    dh3, da1, dg_ffn1 = _mlp_bwd(dh4, h3, a1, gains["ffn1"], wup1, wdown1, name="mlp_bwd1")
    dwdown1 = _mm_tn(z1, dh4, name="dwdown1", ta=1024, tb=1024, tt=2048, out_dtype=BF16)
    dwup1 = _mm_tn(n3, da1, name="dwup1", ta=d, tb=fb, tt=4096, blocked_out=True, out_dtype=BF16)
    dh2, dpoolw, dscale, dg_mix1 = _pool_bwd(dh3, h2, pooled, gains["mix1"], poolw, pool_scale)
    dh1, da0, dg_ffn0 = _mlp_bwd(dh2, h1, a0, gains["ffn0"], wup0, wdown0, name="mlp_bwd0")
    dwdown0 = _mm_tn(z0, dh2, name="dwdown0", ta=1024, tb=1024, tt=2048, out_dtype=BF16)
    dwup0 = _mm_tn(n1, da0, name="dwup0", ta=d, tb=fb, tt=4096, blocked_out=True, out_dtype=BF16)
    do_aug, do_aug_t, dbcx, dconvw = _outproj_conv_bwd(dh1, att, wout, bcx, conv_w)
    dwout = _mm_tn_cat([att, cv], [dh1], name="dwout", tt=2048)
    gblocks = {
        "w_out_0": dwout.reshape(N_DEV, d // N_DEV, d), "w_up_0": dwup0, "w_up_1": dwup1,
        "w_down_0": dwdown0.reshape(N_DEV, -1, d), "w_down_1": dwdown1.reshape(N_DEV, -1, d),
        "pool_w_1": dpoolw.astype(BF16).reshape(n_grp, N_DEV, cg // N_DEV, cg).transpose(1, 0, 2, 3).reshape(
            N_DEV, n_grp * (cg // N_DEV), cg),
    }
    (dq, dqx, dk, dkx, dv), landed = _attn_bwd(q_bwd, do_aug, q_bwd_t, do_aug_t, k_aug, v_aug,
                                               [gblocks[n] for n in LATE])
    df, dbf = _fgate_bwd(dqx, dkx, sgate)
    dwin = jnp.concatenate(
        [_mm_tn_cat([n0], [dq, dk, dv], name="dwin_qkv", tt=2048),
         _mm_tn(n0, df, name="dwin_f", ta=d, tb=128, tt=2048, out_dtype=BF16)[:, :N_HEADS],
         _mm_tn(n0, dbcx, name="dwin_bcx", ta=d, tb=512, tt=4096, out_dtype=BF16)], axis=1)
    dwin_blocks = dwin.reshape(d, N_DEV, dwin.shape[1] // N_DEV).transpose(1, 0, 2)
    grad_x, dg_mix0, landed_win = _inproj_bwd(dq, dk, dv, df, dbcx, dh1, x, gains["mix0"], win_p, dwin_blocks)
    small = dict(mix0=dg_mix0, ffn0=dg_ffn0, mix1=dg_mix1, pool_scale=dscale, ffn1=dg_ffn1, final=dg_final,
                 b_f=dbf, conv_w=dconvw)
    return loss, grad_x, dict(zip(LATE + ("w_in_0",), tuple(landed) + (landed_win,))), small


def _mesh_places():
    x, y, c = lax.axis_index("x"), lax.axis_index("y"), lax.axis_index("c")
    chips = [(1 - x, y), (x, 1 - y), (1 - x, 1 - y)]
    return (x, y, c), (x, y, 1 - c), chips


def _all_gather(shards):
    n = len(shards)

    def body(*refs):
        ins, outs = refs[:n], refs[n:2 * n]
        send_sems, recv_sems, local_sems = refs[2 * n:]
        me, sib, chips = _mesh_places()
        c = me[2]

        def copy(ai, k, block, to, src=None):
            dst = outs[ai].at[_slot(*block)]
            return pltpu.make_async_remote_copy(
                src_ref=dst if src is None else src, dst_ref=dst, send_sem=send_sems.at[7 * ai + k],
                recv_sem=recv_sems.at[7 * ai + k], device_id=to, device_id_type=MESH)

        mine = [pltpu.make_async_copy(ins[ai], outs[ai].at[_slot(*me)], local_sems.at[ai]) for ai in range(n)]
        for cp in mine:
            cp.start()
        first = []
        for ai in range(n):
            first.append(copy(ai, 0, me, sib, src=ins[ai]))
            first += [copy(ai, 1 + j, me, (*chip, c), src=ins[ai]) for j, chip in enumerate(chips)]
        for cp in first:
            cp.start()
        passed = []
        for ai in range(n):
            for j, chip in enumerate(chips):
                copy(ai, 1 + j, (*chip, c), me).wait_recv()
                cp = copy(ai, 4 + j, (*chip, c), sib)
                cp.start()
                passed.append(cp)
        for ai in range(n):
            copy(ai, 0, sib, me).wait_recv()
            for j, chip in enumerate(chips):
                copy(ai, 4 + j, (*chip, 1 - c), me).wait_recv()
        for cp in first + passed:
            cp.wait_send()
        for cp in mine:
            cp.wait()

    return pl.pallas_call(
        body, name="all_gather",
        in_specs=[HBM_SPEC] * n, out_specs=[HBM_SPEC] * n,
        out_shape=[jax.ShapeDtypeStruct((N_DEV,) + s.shape, s.dtype) for s in shards],
        scratch_shapes=[pltpu.SemaphoreType.DMA((7 * n,)), pltpu.SemaphoreType.DMA((7 * n,)),
                        pltpu.SemaphoreType.DMA((n,))],
    )(*shards)


SMALL_ROWS = 16


def _small_allreduce(parts):
    n, _, w = parts.shape
    assert n <= SMALL_ROWS

    def body(p_ref, o_ref, gath, send_sems, recv_sems):
        x, y, c = lax.axis_index("x"), lax.axis_index("y"), lax.axis_index("c")
        my = _slot(x, y, c)
        rows = [jnp.sum(p_ref[i], axis=0, keepdims=True) for i in range(n)]
        rows.append(jnp.zeros((SMALL_ROWS - n, w), F32))
        gath[my] = jnp.concatenate(rows, axis=0)
        copies = []
        for k in range(1, N_DEV):
            px, py, pc = x ^ (k >> 2), y ^ ((k >> 1) & 1), c ^ (k & 1)
            cp = pltpu.make_async_remote_copy(
                src_ref=gath.at[my], dst_ref=gath.at[my], send_sem=send_sems.at[k - 1], recv_sem=recv_sems.at[k - 1],
                device_id=(px, py, pc), device_id_type=MESH)
            cp.start()
            copies.append(cp)
        for cp in copies:
            cp.wait()
        acc = gath[0]
        for d in range(1, N_DEV):
            acc = acc + gath[d]
        o_ref[...] = acc

    return pl.pallas_call(
        body, name="small_allreduce",
        in_specs=[VMEM_SPEC], out_specs=VMEM_SPEC,
        out_shape=jax.ShapeDtypeStruct((SMALL_ROWS, w), F32),
        scratch_shapes=[pltpu.VMEM((N_DEV, SMALL_ROWS, w), F32), pltpu.SemaphoreType.DMA((N_DEV - 1,)),
                        pltpu.SemaphoreType.DMA((N_DEV - 1,))],
    )(parts)


def _adamw(g, w, m, v, *, name, tm=256):
    r, c = g.shape
    tm = tm if r % tm == 0 else r
    bc1 = 1.0 - ADAM_B1 ** ADAM_STEP
    bc2 = 1.0 - ADAM_B2 ** ADAM_STEP

    def body(g_ref, w_ref, m_ref, v_ref, d_ref, nm_ref, nv_ref):
        gv = g_ref[...]
        nm = ADAM_B1 * m_ref[...] + (1.0 - ADAM_B1) * gv
        nv = ADAM_B2 * v_ref[...] + (1.0 - ADAM_B2) * jnp.square(gv)
        nm_ref[...] = nm
        nv_ref[...] = nv
        d_ref[...] = -ADAM_LR * ((nm / bc1) / (jnp.sqrt(nv / bc2) + ADAM_EPS) + ADAM_WD * w_ref[...])

    blk = pl.BlockSpec((tm, c), lambda i: (i, 0))
    shp = jax.ShapeDtypeStruct((r, c), F32)
    return pl.pallas_call(
        body, name=name, grid=(r // tm,), in_specs=[blk] * 4, out_specs=[blk] * 3, out_shape=[shp] * 3,
        compiler_params=_cp(("parallel",)),
    )(g, w, m, v)


def _adamw_sum(parts, w, m, v, *, name, tm=256):
    _, r, c = parts.shape
    tm = tm if r % tm == 0 else r
    bc1 = 1.0 - ADAM_B1 ** ADAM_STEP
    bc2 = 1.0 - ADAM_B2 ** ADAM_STEP

    def body(p_ref, w_ref, m_ref, v_ref, g_ref, d_ref, nm_ref, nv_ref):
        gv = p_ref[0].astype(F32)
        for k in range(1, N_DEV):
            gv = gv + p_ref[k].astype(F32)
        g_ref[...] = gv
        nm = ADAM_B1 * m_ref[...] + (1.0 - ADAM_B1) * gv
        nv = ADAM_B2 * v_ref[...] + (1.0 - ADAM_B2) * jnp.square(gv)
        nm_ref[...] = nm
        nv_ref[...] = nv
        d_ref[...] = -ADAM_LR * ((nm / bc1) / (jnp.sqrt(nv / bc2) + ADAM_EPS) + ADAM_WD * w_ref[...])

    blk = pl.BlockSpec((tm, c), lambda i: (i, 0))
    shp = jax.ShapeDtypeStruct((r, c), F32)
    return pl.pallas_call(
        body, name=name, grid=(r // tm,), in_specs=[pl.BlockSpec((N_DEV, tm, c), lambda i: (0, i, 0))] + [blk] * 3,
        out_specs=[blk] * 4, out_shape=[shp] * 4, compiler_params=_cp(("parallel",)),
    )(parts, w, m, v)


BIG = ("w_in_0", "w_out_0", "w_up_0", "w_down_0", "pool_w_1", "w_up_1", "w_down_1")
SMALL = ("norm_mix_0", "norm_ffn_0", "norm_mix_1", "pool_scale_1", "norm_ffn_1", "final_norm", "b_f_0", "conv_w_0")
WEIGHTS = ("norm_mix_0", "w_in_0", "b_f_0", "conv_w_0", "w_out_0", "norm_ffn_0", "w_up_0", "w_down_0", "norm_mix_1",
           "pool_w_1", "pool_scale_1", "norm_ffn_1", "w_up_1", "w_down_1", "final_norm")


def _pad_to(a, rows, cols):
    return jnp.pad(a, ((0, rows - a.shape[0]), (0, cols - a.shape[1])))


def _pack_small(p, width):
    rows = [p[n].reshape(1, -1) for n in SMALL[:6]]
    rows.append(_pad_to(p["b_f_0"].reshape(1, -1), 1, width))
    rows.append(_pad_to(p["conv_w_0"], 3, width))
    return _pad_to(jnp.concatenate(rows, axis=0), SMALL_ROWS, width)


def _unpack_small(a, like):
    out = {n: a[i] for i, n in enumerate(SMALL[:6])}
    out["b_f_0"] = a[6, :like["b_f_0"].shape[0]]
    out["conv_w_0"] = a[7:10, :like["conv_w_0"].shape[1]]
    return out


def kernel(x, norm_mix_0, w_in_0, b_f_0, conv_w_0, w_out_0, norm_ffn_0, w_up_0, w_down_0, norm_mix_1, pool_w_1, pool_scale_1, norm_ffn_1, w_up_1, w_down_1, final_norm, loss_target, m_norm_mix_0, m_w_in_0, m_b_f_0, m_conv_w_0, m_w_out_0, m_norm_ffn_0, m_w_up_0, m_w_down_0, m_norm_mix_1, m_pool_w_1, m_pool_scale_1, m_norm_ffn_1, m_w_up_1, m_w_down_1, m_final_norm, v_norm_mix_0, v_w_in_0, v_b_f_0, v_conv_w_0, v_w_out_0, v_norm_ffn_0, v_w_up_0, v_w_down_0, v_norm_mix_1, v_pool_w_1, v_pool_scale_1, v_norm_ffn_1, v_w_up_1, v_w_down_1, v_final_norm):
    w = dict(norm_mix_0=norm_mix_0, w_in_0=w_in_0, b_f_0=b_f_0, conv_w_0=conv_w_0, w_out_0=w_out_0,
             norm_ffn_0=norm_ffn_0, w_up_0=w_up_0, w_down_0=w_down_0, norm_mix_1=norm_mix_1, pool_w_1=pool_w_1,
             pool_scale_1=pool_scale_1, norm_ffn_1=norm_ffn_1, w_up_1=w_up_1, w_down_1=w_down_1, final_norm=final_norm)
    m = dict(norm_mix_0=m_norm_mix_0, w_in_0=m_w_in_0, b_f_0=m_b_f_0, conv_w_0=m_conv_w_0, w_out_0=m_w_out_0,
             norm_ffn_0=m_norm_ffn_0, w_up_0=m_w_up_0, w_down_0=m_w_down_0, norm_mix_1=m_norm_mix_1,
             pool_w_1=m_pool_w_1, pool_scale_1=m_pool_scale_1, norm_ffn_1=m_norm_ffn_1, w_up_1=m_w_up_1,
             w_down_1=m_w_down_1, final_norm=m_final_norm)
    v = dict(norm_mix_0=v_norm_mix_0, w_in_0=v_w_in_0, b_f_0=v_b_f_0, conv_w_0=v_conv_w_0, w_out_0=v_w_out_0,
             norm_ffn_0=v_norm_ffn_0, w_up_0=v_w_up_0, w_down_0=v_w_down_0, norm_mix_1=v_norm_mix_1,
             pool_w_1=v_pool_w_1, pool_scale_1=v_pool_scale_1, norm_ffn_1=v_norm_ffn_1, w_up_1=v_w_up_1,
             w_down_1=v_w_down_1, final_norm=v_final_norm)
    d = x.shape[-1]
    n_in = w_in_0.shape[1] * N_DEV
    n_qkv = 3 * ATTN_W
    pool_g, pool_rows, pool_c = pool_w_1.shape

    def shard2d(p):
        return {n: (p[n].reshape(pool_g * pool_rows, pool_c) if n == "pool_w_1" else p[n]) for n in BIG}
    w2, m2, v2 = shard2d(w), shard2d(m), shard2d(v)

    conv_cols = conv_w_0.shape[1]
    win_g8, conv_g8 = _all_gather([w_in_0.astype(BF16), _pad_to(conv_w_0, 8, 128)])
    conv_full = conv_g8[:, :, :conv_cols].transpose(1, 0, 2).reshape(8, N_DEV * conv_cols)
    win = win_g8.transpose(1, 0, 2).reshape(d, n_in)
    win_p = jnp.concatenate([win[:, :n_qkv], _pad_to(win[:, n_qkv:n_qkv + N_HEADS], d, F_PAD),
                             win[:, n_qkv + N_HEADS:]], axis=1)

    gains = dict(mix0=norm_mix_0.reshape(1, d), ffn0=norm_ffn_0.reshape(1, d), mix1=norm_mix_1.reshape(1, d),
                 ffn1=norm_ffn_1.reshape(1, d), final=final_norm.reshape(1, d))
    dev = _slot(lax.axis_index("x"), lax.axis_index("y"), lax.axis_index("c"))
    loss8, grad_x, landed, small = _local_step(
        x[0], loss_target[0], gains, _pad_to(b_f_0.reshape(1, -1), 1, F_PAD), conv_full, pool_scale_1.reshape(1, d),
        win_p, {n: w2[n].astype(BF16) for n in LATE})
    parts = jnp.concatenate(
        [small[k][None] for k in ("mix0", "ffn0", "mix1", "pool_scale", "ffn1", "final")]
        + [_pad_to(small["b_f"], 8, d)[None], jnp.pad(small["conv_w"], ((0, 0), (0, 0), (0, d - CONV_CH))),
           _pad_to(loss8[0:1, 0:1], 8, d)[None]], axis=0)
    tot = _small_allreduce(parts)
    loss = tot[10, 0]
    conv_g = lax.dynamic_slice(tot, (7, dev * conv_cols), (3, conv_cols))
    gs = tot.at[7:10].set(_pad_to(conv_g, 3, d))

    grads, deltas, new_m, new_v = {}, {}, {}, {}
    for n in BIG:
        gr, dl, nm, nv = _adamw_sum(landed[n], w2[n], m2[n], v2[n], name="adamw_" + n)
        for dst, val in ((grads, gr), (deltas, dl), (new_m, nm), (new_v, nv)):
            dst[n] = val.reshape(w[n].shape)
    dl, nm, nv = _adamw(gs, _pack_small(w, d), _pack_small(m, d), _pack_small(v, d), name="adamw_small")
    for dst, val in ((grads, gs), (deltas, dl), (new_m, nm), (new_v, nv)):
        dst.update(_unpack_small(val, w))
    return (loss, grad_x[None], *[grads[n] for n in WEIGHTS], *[deltas[n] for n in WEIGHTS],
            *[new_m[n] for n in WEIGHTS], *[new_v[n] for n in WEIGHTS])
```

```python
import functools

import jax
import jax.numpy as jnp
from jax import lax
from jax.experimental import pallas as pl
from jax.experimental.pallas import tpu as pltpu

F32 = jnp.float32
BF16 = jnp.bfloat16

N_DEV = 8
N_HEADS = 8
HEAD_DIM = 64
PAIR = 2 * HEAD_DIM
ATTN_W = N_HEADS * HEAD_DIM
CONV_CH = 512
F_PAD = 128
POOL_WINDOWS = (2, 4, 8, 16)
POOL_HALO = 16
CONV_HALO = 16
RMS_EPS = 1e-6
Q_SCALE = HEAD_DIM ** -0.5
LOG2E = 1.4426950408889634
NEG = -1e30
AUX_BIAS = 0
AUX_LSE = 3
AUX_ROWSUM = 6
ADAM_LR, ADAM_B1, ADAM_B2, ADAM_EPS, ADAM_WD, ADAM_STEP = 0.001, 0.9, 0.999, 1e-08, 0.01, 10
MESH = pl.DeviceIdType.MESH
VMEM_LIMIT = 56 * 2**20


def _cp(sem=None, vmem=VMEM_LIMIT, **kw):
    return pltpu.CompilerParams(dimension_semantics=sem, vmem_limit_bytes=vmem, **kw)


def _dot(a, b):
    return jnp.dot(a, b, preferred_element_type=F32)


def _dot_nt(a, b):
    return lax.dot_general(a, b, (((1,), (1,)), ((), ())), preferred_element_type=F32)


def _dot_tn(a, b):
    return lax.dot_general(a, b, (((0,), (0,)), ((), ())), preferred_element_type=F32)


def _rstd(h):
    return lax.rsqrt(jnp.mean(h * h, axis=-1, keepdims=True) + RMS_EPS)


def _rows8(x):
    r, n = x.shape
    return jnp.sum(x.reshape(r // 8, 8, n), axis=0)


def _norm_bwd(dn, h, g):
    r = _rstd(h)
    xhat = h * r
    dy = dn * g
    dh = r * (dy - xhat * jnp.mean(dy * xhat, axis=-1, keepdims=True))
    return dh, _rows8(dn * xhat)


def _const_spec(shape):
    nd = len(shape)
    return pl.BlockSpec(shape, lambda *_: (0,) * nd, pipeline_mode=pl.Buffered(1))


HBM_SPEC = pl.BlockSpec(memory_space=pltpu.HBM)
VMEM_SPEC = pl.BlockSpec(memory_space=pltpu.VMEM)


def _slot(px, py, pc):
    return 4 * px + 2 * py + pc


class _Exchange:
    def __init__(self, srcs, dsts, send_sems, recv_sems, local_sems, gather):
        x, y, c = lax.axis_index("x"), lax.axis_index("y"), lax.axis_index("c")
        me = _slot(x, y, c)
        self.copies = []
        for a, (src, dst) in enumerate(zip(srcs, dsts)):
            self.copies.append(pltpu.make_async_copy(src if gather else src.at[me], dst.at[me], local_sems.at[a]))
            for k in range(1, N_DEV):
                px, py, pc = x ^ (k >> 2), y ^ ((k >> 1) & 1), c ^ (k & 1)
                self.copies.append(pltpu.make_async_remote_copy(
                    src_ref=src if gather else src.at[_slot(px, py, pc)], dst_ref=dst.at[me],
                    send_sem=send_sems.at[(N_DEV - 1) * a + k - 1], recv_sem=recv_sems.at[(N_DEV - 1) * a + k - 1],
                    device_id=(px, py, pc), device_id_type=MESH))

    def start(self):
        for cp in self.copies:
            cp.start()

    def wait(self):
        for cp in self.copies:
            cp.wait()

    @staticmethod
    def scratch(n):
        return [pltpu.SemaphoreType.DMA(((N_DEV - 1) * n,)), pltpu.SemaphoreType.DMA(((N_DEV - 1) * n,)),
                pltpu.SemaphoreType.DMA((n,))]


def _norm_inproj(x, g, win_pt, conv_w, *, tm=512):
    t, d = x.shape
    n_all = win_pt.shape[0]
    n_qkv = 3 * ATTN_W
    n_bcx = 3 * CONV_CH
    assert n_all == n_qkv + F_PAD + n_bcx
    tm = min(tm, t)
    ch = CONV_CH

    def body(x_ref, g_ref, w_ref, cw_ref, n_ref, qkv_ref, f_ref, bcx_ref, cv_ref, ext):
        h = x_ref[...]
        n = (h * _rstd(h) * g_ref[...]).astype(BF16)
        n_ref[...] = n
        for c0 in range(0, n_qkv, 512):
            acc = _dot_nt(n, w_ref[c0:c0 + 512, :])
            if c0 < ATTN_W:
                acc = acc * (Q_SCALE * LOG2E)
            qkv_ref[:, c0:c0 + 512] = acc.astype(BF16)
        f_ref[...] = _dot_nt(n, w_ref[n_qkv:n_qkv + F_PAD, :])
        bcx = []
        for k in range(3):
            c0 = n_qkv + F_PAD + k * ch
            v = _dot_nt(n, w_ref[c0:c0 + ch, :]).astype(BF16)
            bcx_ref[:, k * ch:(k + 1) * ch] = v
            bcx.append(v.astype(F32))
        @pl.when(pl.program_id(0) == 0)
        def _():
            ext[tm:tm + CONV_HALO, :] = jnp.zeros((CONV_HALO, ch), F32)
        ext[0:CONV_HALO, :] = ext[tm:tm + CONV_HALO, :]
        ext[CONV_HALO:CONV_HALO + tm, :] = bcx[1] * bcx[2]
        conv = (cw_ref[0:1, :] * ext[CONV_HALO - 2:CONV_HALO - 2 + tm, :]
                + cw_ref[1:2, :] * ext[CONV_HALO - 1:CONV_HALO - 1 + tm, :]
                + cw_ref[2:3, :] * ext[CONV_HALO:CONV_HALO + tm, :])
        cv_ref[...] = (bcx[0] * conv).astype(BF16)

    return pl.pallas_call(
        body, name="norm_inproj", grid=(t // tm,),
        in_specs=[pl.BlockSpec((tm, d), lambda i: (i, 0)), _const_spec((1, d)), _const_spec((n_all, d)),
                  _const_spec((8, ch))],
        out_specs=[pl.BlockSpec((tm, d), lambda i: (i, 0)), pl.BlockSpec((tm, n_qkv), lambda i: (i, 0)),
                   pl.BlockSpec((tm, F_PAD), lambda i: (i, 0)), pl.BlockSpec((tm, n_bcx), lambda i: (i, 0)),
                   pl.BlockSpec((tm, ch), lambda i: (i, 0))],
        out_shape=[jax.ShapeDtypeStruct((t, d), BF16), jax.ShapeDtypeStruct((t, n_qkv), BF16),
                   jax.ShapeDtypeStruct((t, F_PAD), F32), jax.ShapeDtypeStruct((t, n_bcx), BF16),
                   jax.ShapeDtypeStruct((t, ch), BF16)],
        scratch_shapes=[pltpu.VMEM((CONV_HALO + tm, ch), F32)],
        compiler_params=_cp(("arbitrary",)),
    )(x, g, win_pt, conv_w)


def _head_lanes(h):
    lane = lax.broadcasted_iota(jnp.int32, (1, PAIR), 1)
    hh = h % 2
    return lane, lane // HEAD_DIM == hh, HEAD_DIM * (1 - hh)


def _pieces(col):
    hi = col.astype(BF16).astype(F32)
    r1 = col - hi
    mid = r1.astype(BF16).astype(F32)
    lo = (r1 - mid).astype(BF16).astype(F32)
    return hi, mid, lo


def _put_pieces(lane, first, col, other):
    hi, mid, lo = _pieces(col)
    return jnp.where(lane == first, hi, jnp.where(lane == first + 1, mid, jnp.where(lane == first + 2, lo, other)))


def _fgate_prep(flog, b_f, qkv, *, tm=512):
    t = flog.shape[0]
    tm = min(tm, t)

    def body(f_ref, b_ref, qkv_ref, qat_ref, ka_ref, va_ref, vat_ref, sg_ref, carry):
        @pl.when(pl.program_id(0) == 0)
        def _():
            carry[...] = jnp.zeros_like(carry)
        z = f_ref[...] + b_ref[...]
        e = jnp.exp(-jnp.abs(z))
        logf = jnp.minimum(z, 0.0) - jnp.log(1.0 + e)
        sg_ref[...] = jnp.where(z >= 0, e, 1.0) / (1.0 + e)
        r = lax.broadcasted_iota(jnp.int32, (tm, tm), 0)
        c = lax.broadcasted_iota(jnp.int32, (tm, tm), 1)
        tri = (c <= r).astype(F32)
        cs = jnp.dot(tri, logf, preferred_element_type=F32, precision=lax.Precision.HIGHEST) + carry[...]
        carry[...] = cs[tm - 1:tm, :]
        cs2 = cs * LOG2E
        for h in range(N_HEADS):
            lane, head, aux = _head_lanes(h)
            p0 = (h // 2) * PAIR
            ones = ((lane >= aux + AUX_LSE) & (lane <= aux + AUX_ROWSUM)).astype(F32)
            bias = (lane >= aux + AUX_BIAS) & (lane < aux + AUX_BIAS + 3)
            k_aux = _put_pieces(lane, aux + AUX_BIAS, cs2[:, h:h + 1], ones)
            q_aug = jnp.where(head, qkv_ref[:, p0:p0 + PAIR], jnp.where(bias, -1.0, 0.0).astype(BF16))
            v_aug = jnp.where(head, qkv_ref[:, 2 * ATTN_W + p0:2 * ATTN_W + p0 + PAIR],
                              jnp.where(bias, 1.0, 0.0).astype(BF16))
            qat_ref[h] = q_aug.T
            ka_ref[h] = jnp.where(head, qkv_ref[:, ATTN_W + p0:ATTN_W + p0 + PAIR], k_aux.astype(BF16))
            va_ref[h] = v_aug
            vat_ref[h] = v_aug.T

    aug = lambda: pl.BlockSpec((N_HEADS, tm, PAIR), lambda i: (0, i, 0))
    aug_t = lambda: pl.BlockSpec((N_HEADS, PAIR, tm), lambda i: (0, 0, i))
    aug_shape = jax.ShapeDtypeStruct((N_HEADS, t, PAIR), BF16)
    aug_t_shape = jax.ShapeDtypeStruct((N_HEADS, PAIR, t), BF16)
    return pl.pallas_call(
        body, name="fgate_prep", grid=(t // tm,),
        in_specs=[pl.BlockSpec((tm, F_PAD), lambda i: (i, 0)), _const_spec((1, F_PAD)),
                  pl.BlockSpec((tm, 3 * ATTN_W), lambda i: (i, 0))],
        out_specs=[aug_t(), aug(), aug(), aug_t(), pl.BlockSpec((tm, F_PAD), lambda i: (i, 0))],
        out_shape=[aug_t_shape, aug_shape, aug_shape, aug_t_shape, jax.ShapeDtypeStruct((t, F_PAD), F32)],
        scratch_shapes=[pltpu.VMEM((1, F_PAD), F32)],
        compiler_params=_cp(("arbitrary",)),
    )(flog, b_f, qkv)


def _put_pieces_t(row, first, vec, other):
    hi, mid, lo = _pieces(vec)
    return jnp.where(row == first, hi, jnp.where(row == first + 1, mid, jnp.where(row == first + 2, lo, other)))


def _attn_fwd(q_aug_t, k_aug, v_aug_t, shards, *, tq=1024):
    t = k_aug.shape[1]
    tq = min(tq, t)
    tk = tq // 2
    nq = t // tq
    n_pairs = ATTN_W // PAIR
    n_sh = len(shards)

    def body(qt_ref, k_ref, vt_ref, *rest):
        o_ref, qb_ref, qbt_ref = rest[n_sh:n_sh + 3]
        s_scr = rest[2 * n_sh + 3]
        gather = _Exchange(rest[:n_sh], rest[n_sh + 3:2 * n_sh + 3], *rest[2 * n_sh + 4:], gather=True)
        i = pl.program_id(1)

        @pl.when((pl.program_id(0) == 0) & (i == 0))
        def _():
            gather.start()
        key = lax.broadcasted_iota(jnp.int32, (tk, tq), 0)
        qry = lax.broadcasted_iota(jnp.int32, (tk, tq), 1)
        qt = [qt_ref[0], qt_ref[1]]

        def logits(hh, tile, slot, diag):
            s = _dot(k_ref[hh, pl.ds(pl.multiple_of(tile * tk, tk), tk), :], qt[hh])
            if diag:
                s = jnp.where(key + (tile * tk - i * tq) <= qry, s, NEG)
            s_scr[hh, slot] = s
            return jnp.max(s, axis=0, keepdims=True)

        def probs(hh, tile, slot, m, acc, tmax):
            mn = jnp.maximum(m, tmax)
            p = jnp.exp2(s_scr[hh, slot] - mn).astype(BF16)
            acc = jnp.exp2(m - mn) * acc + _dot(vt_ref[hh, :, pl.ds(pl.multiple_of(tile * tk, tk), tk)], p)
            return mn, acc

        def advance(carry, prev, slot, nxt, diag=False):
            out = []
            for hh in range(2):
                m, acc, tmax = carry[hh]
                m, acc = probs(hh, prev, slot, m, acc, tmax)
                out.append((m, acc, logits(hh, nxt, 1 - slot, diag)))
            return tuple(out)

        def two_tiles(jj, carry):
            carry = advance(carry, jnp.where(jj == 0, 2 * i, 2 * jj - 1), 1, 2 * jj)
            return advance(carry, 2 * jj, 0, 2 * jj + 1)

        init = tuple((jnp.full((1, tq), NEG, F32), jnp.zeros((PAIR, tq), F32), logits(hh, 2 * i + 1, 0, True))
                     for hh in range(2))
        carry = advance(init, 2 * i + 1, 0, 2 * i, diag=True)
        carry = lax.fori_loop(0, i, two_tiles, carry)
        last = jnp.where(i == 0, 2 * i, 2 * i - 1)
        row = lax.broadcasted_iota(jnp.int32, (PAIR, 1), 0)
        res = []
        for hh in range(2):
            aux = HEAD_DIM * (1 - hh)
            m, acc, tmax = carry[hh]
            m, acc = probs(hh, last, 1, m, acc, tmax)
            l = acc[aux + AUX_BIAS:aux + AUX_BIAS + 1, :]
            qbt = _put_pieces_t(row, aux + AUX_LSE, -(m + jnp.log2(l)), qt[hh].astype(F32))
            qbt_ref[hh] = qbt.astype(BF16)
            qb_ref[hh] = qbt.astype(BF16).T
            res.append(acc * (1.0 / l))
        o_ref[...] = jnp.where(row < HEAD_DIM, res[0], res[1]).astype(BF16).T

        @pl.when((pl.program_id(0) == n_pairs - 1) & (i == nq - 1))
        def _():
            gather.wait()

    res = pl.pallas_call(
        body, name="attn_fwd", grid=(n_pairs, nq),
        in_specs=[pl.BlockSpec((2, PAIR, tq), lambda p, i: (p, 0, i)),
                  pl.BlockSpec((2, t, PAIR), lambda p, i: (p, 0, 0), pipeline_mode=pl.Buffered(1)),
                  pl.BlockSpec((2, PAIR, t), lambda p, i: (p, 0, 0), pipeline_mode=pl.Buffered(1))] + [HBM_SPEC] * n_sh,
        out_specs=[pl.BlockSpec((tq, PAIR), lambda p, i: (i, p)),
                   pl.BlockSpec((2, tq, PAIR), lambda p, i: (p, i, 0)),
                   pl.BlockSpec((2, PAIR, tq), lambda p, i: (p, 0, i))] + [HBM_SPEC] * n_sh,
        out_shape=[jax.ShapeDtypeStruct((t, ATTN_W), BF16), jax.ShapeDtypeStruct((N_HEADS, t, PAIR), BF16),
                   jax.ShapeDtypeStruct((N_HEADS, PAIR, t), BF16)]
        + [jax.ShapeDtypeStruct((N_DEV,) + s.shape, s.dtype) for s in shards],
        scratch_shapes=[pltpu.VMEM((2, 2, tk, tq), F32)] + _Exchange.scratch(n_sh),
        compiler_params=_cp(("arbitrary", "arbitrary")),
    )(q_aug_t, k_aug, v_aug_t, *shards)
    return res[0], res[1], res[2], res[3:]


def _prev_halo(tm, halo):
    return lambda i: (jnp.maximum(i * (tm // halo) - 1, 0), 0)


def _next_halo(tm, halo, t):
    return lambda i: (jnp.minimum((i + 1) * (tm // halo), t // halo - 1), 0)


def _mlp_tile(hh, g_ref, wu_ref, wd_ref, n_ref, a_ref, z_ref):
    n_blk, _, fb = wu_ref.shape
    n = (hh * _rstd(hh) * g_ref[...]).astype(BF16)
    n_ref[...] = n
    acc = hh
    for k in range(n_blk):
        a = _dot(n, wu_ref[k])
        zz = jnp.square(jnp.maximum(a, 0.0)).astype(BF16)
        a_ref[:, k * fb:(k + 1) * fb] = a.astype(BF16)
        z_ref[:, k * fb:(k + 1) * fb] = zz
        acc = acc + _dot(zz, wd_ref[k * fb:(k + 1) * fb, :])
    return acc


def _outproj(att, cv, x, wout, *, tm=512):
    t, d = x.shape
    tm = min(tm, t)

    def body(a_ref, c_ref, x_ref, w_ref, h_ref):
        h_ref[...] = x_ref[...] + _dot(a_ref[...], w_ref[0:ATTN_W, :]) + _dot(c_ref[...], w_ref[ATTN_W:, :])

    return pl.pallas_call(
        body, name="outproj", grid=(t // tm,),
        in_specs=[pl.BlockSpec((tm, ATTN_W), lambda i: (i, 0)), pl.BlockSpec((tm, CONV_CH), lambda i: (i, 0)),
                  pl.BlockSpec((tm, d), lambda i: (i, 0)), _const_spec(wout.shape)],
        out_specs=pl.BlockSpec((tm, d), lambda i: (i, 0)),
        out_shape=jax.ShapeDtypeStruct((t, d), F32),
        compiler_params=_cp(("parallel",)),
    )(att, cv, x, wout)


def _mlp_fwd(h, g, wup, wdown, *, name, tm=512):
    t, d = h.shape
    n_blk, _, fb = wup.shape
    f = n_blk * fb
    tm = min(tm, t)

    def body(h_ref, g_ref, wu_ref, wd_ref, ho_ref, n_ref, a_ref, z_ref):
        ho_ref[...] = _mlp_tile(h_ref[...], g_ref, wu_ref, wd_ref, n_ref, a_ref, z_ref)

    row = lambda n_: pl.BlockSpec((tm, n_), lambda i: (i, 0))
    return pl.pallas_call(
        body, name=name, grid=(t // tm,),
        in_specs=[row(d), _const_spec((1, d)), _const_spec(wup.shape), _const_spec(wdown.shape)],
        out_specs=[row(d), row(d), row(f), row(f)],
        out_shape=[jax.ShapeDtypeStruct((t, d), F32), jax.ShapeDtypeStruct((t, d), BF16),
                   jax.ShapeDtypeStruct((t, f), BF16), jax.ShapeDtypeStruct((t, f), BF16)],
        compiler_params=_cp(("parallel",)),
    )(h, g, wup, wdown)


def _mlp_fwd_loss(h, g, wup, wdown, g_out, target, *, name, tm=512):
    t, d = h.shape
    n_blk, _, fb = wup.shape
    f = n_blk * fb
    tm = min(tm, t)
    nsteps = t // tm

    def body(h_ref, g_ref, wu_ref, wd_ref, go_ref, y_ref, loss_ref, dh_ref, dg_ref, n_ref, a_ref, z_ref, lacc):
        i = pl.program_id(0)

        @pl.when(i == 0)
        def _():
            lacc[...] = jnp.zeros_like(lacc)
            dg_ref[...] = jnp.zeros_like(dg_ref)
        hv = _mlp_tile(h_ref[...], g_ref, wu_ref, wd_ref, n_ref, a_ref, z_ref)
        gv = go_ref[...]
        r = _rstd(hv)
        xhat = hv * r
        err = xhat * gv - y_ref[...]
        lacc[...] += _rows8(err * err)
        dout = err * (1.0 / d)
        dy = dout * gv
        dg_ref[...] += _rows8(dout * xhat)
        dh_ref[...] = r * (dy - xhat * jnp.mean(dy * xhat, axis=-1, keepdims=True))

        @pl.when(i == nsteps - 1)
        def _():
            loss_ref[...] = jnp.full(loss_ref.shape, (0.5 / d) * jnp.sum(lacc[...]), F32)

    row = lambda n_: pl.BlockSpec((tm, n_), lambda i: (i, 0))
    return pl.pallas_call(
        body, name=name, grid=(nsteps,),
        in_specs=[row(d), _const_spec((1, d)), _const_spec(wup.shape), _const_spec(wdown.shape), _const_spec((1, d)),
                  row(d)],
        out_specs=[pl.BlockSpec((8, 128), lambda i: (0, 0)), row(d), pl.BlockSpec((8, d), lambda i: (0, 0)),
                   row(d), row(f), row(f)],
        out_shape=[jax.ShapeDtypeStruct((8, 128), F32), jax.ShapeDtypeStruct((t, d), F32),
                   jax.ShapeDtypeStruct((8, d), F32), jax.ShapeDtypeStruct((t, d), BF16),
                   jax.ShapeDtypeStruct((t, f), BF16), jax.ShapeDtypeStruct((t, f), BF16)],
        scratch_shapes=[pltpu.VMEM((8, d), F32)],
        compiler_params=_cp(("arbitrary",)),
    )(h, g, wup, wdown, g_out, target)


def _pool_inv_count(i, tm):
    tglob = (i * tm + lax.broadcasted_iota(jnp.int32, (tm, 1), 0) + 1).astype(F32)
    return [1.0 / jnp.minimum(tglob, float(w)) for w in POOL_WINDOWS]


def _pool_fwd(h, g, poolw, scale, *, tm=512):
    t, d = h.shape
    tm = min(tm, t)
    cg = d // len(POOL_WINDOWS)

    def body(h_ref, hh_ref, g_ref, w_ref, s_ref, ho_ref, p_ref, ext):
        i = pl.program_id(0)
        hv = h_ref[...]
        halo = hh_ref[...]
        n = hv * _rstd(hv) * g_ref[...]
        ext[0:POOL_HALO, :] = jnp.where(i == 0, 0.0, halo * _rstd(halo) * g_ref[...])
        ext[POOL_HALO:POOL_HALO + tm, :] = n
        inv = _pool_inv_count(i, tm)
        for gi, w in enumerate(POOL_WINDOWS):
            cs = slice(gi * cg, (gi + 1) * cg)
            s = ext[POOL_HALO:POOL_HALO + tm, cs]
            for j in range(1, w):
                s = s + ext[POOL_HALO - j:POOL_HALO - j + tm, cs]
            pooled = (s * inv[gi] - n[:, cs]).astype(BF16)
            p_ref[:, cs] = pooled
            ho_ref[:, cs] = hv[:, cs] + _dot(pooled, w_ref[gi]) * s_ref[:, cs]

    row = lambda: pl.BlockSpec((tm, d), lambda i: (i, 0))
    return pl.pallas_call(
        body, name="pool_fwd", grid=(t // tm,),
        in_specs=[row(), pl.BlockSpec((POOL_HALO, d), _prev_halo(tm, POOL_HALO)), _const_spec((1, d)),
                  _const_spec(poolw.shape), _const_spec((1, d))],
        out_specs=[row(), row()],
        out_shape=[jax.ShapeDtypeStruct((t, d), F32), jax.ShapeDtypeStruct((t, d), BF16)],
        scratch_shapes=[pltpu.VMEM((POOL_HALO + tm, d), F32)],
        compiler_params=_cp(("parallel",)),
    )(h, h, g, poolw, scale)


def _mm_tn(a, b, *, name, ta, tb, tt, blocked_out=False, out_dtype=F32):
    t, ka = a.shape
    n = b.shape[1]
    ta, tb, tt = min(ta, ka), min(tb, n), min(tt, t)
    nt = t // tt

    def body(a_ref, b_ref, o_ref, acc):
        @pl.when(pl.program_id(2) == 0)
        def _():
            acc[...] = jnp.zeros_like(acc)
        acc[...] += _dot_tn(a_ref[...].astype(BF16), b_ref[...].astype(BF16))

        @pl.when(pl.program_id(2) == nt - 1)
        def _():
            o_ref[...] = acc[...].astype(out_dtype)

    if blocked_out:
        assert ta == ka
        out_shape = jax.ShapeDtypeStruct((n // tb, ka, tb), out_dtype)
        out_spec = pl.BlockSpec((None, ta, tb), lambda i, j, k: (j, i, 0))
    else:
        out_shape = jax.ShapeDtypeStruct((ka, n), out_dtype)
        out_spec = pl.BlockSpec((ta, tb), lambda i, j, k: (i, j))
    return pl.pallas_call(
        body, name=name, grid=(ka // ta, n // tb, nt),
        in_specs=[pl.BlockSpec((tt, ta), lambda i, j, k: (k, i)), pl.BlockSpec((tt, tb), lambda i, j, k: (k, j))],
        out_specs=out_spec, out_shape=out_shape, scratch_shapes=[pltpu.VMEM((ta, tb), F32)],
        compiler_params=_cp(("parallel", "parallel", "arbitrary")),
    )(a, b)


def _mm_tn_cat(a_list, b_list, *, name, tt, out_dtype=BF16):
    t = a_list[0].shape[0]
    ta, tb = a_list[0].shape[1], b_list[0].shape[1]
    na, nb = len(a_list), len(b_list)
    tt = min(tt, t)
    nt = t // tt

    def body(*refs):
        a_refs, b_refs, o_ref, acc = refs[:na], refs[na:na + nb], refs[na + nb], refs[na + nb + 1]
        i, j, k = pl.program_id(0), pl.program_id(1), pl.program_id(2)

        @pl.when(k == 0)
        def _():
            acc[...] = jnp.zeros_like(acc)
        for ia in range(na):
            for ib in range(nb):
                @pl.when((i == ia) & (j == ib))
                def _(ia=ia, ib=ib):
                    acc[...] += _dot_tn(a_refs[ia][...].astype(BF16), b_refs[ib][...].astype(BF16))

        @pl.when(k == nt - 1)
        def _():
            o_ref[...] = acc[...].astype(out_dtype)

    def held(m, axis):
        def index(i, j, k):
            cur = (i, j)[axis]
            return (jnp.where(cur == m, k, jnp.where(cur < m, 0, nt - 1)), 0)
        return index

    return pl.pallas_call(
        body, name=name, grid=(na, nb, nt),
        in_specs=[pl.BlockSpec((tt, ta), held(m, 0)) for m in range(na)]
        + [pl.BlockSpec((tt, tb), held(m, 1)) for m in range(nb)],
        out_specs=pl.BlockSpec((ta, tb), lambda i, j, k: (i, j)),
        out_shape=jax.ShapeDtypeStruct((na * ta, nb * tb), out_dtype), scratch_shapes=[pltpu.VMEM((ta, tb), F32)],
        compiler_params=_cp(("arbitrary", "arbitrary", "arbitrary")),
    )(*a_list, *b_list)


def _mlp_bwd(dho, h, a, g, wup, wdown, *, name, tm=512):
    t, d = h.shape
    n_blk, _, fb = wup.shape
    f = n_blk * fb
    tm = min(tm, t)

    def body(do_ref, h_ref, a_ref, g_ref, wu_ref, wd_ref, dh_ref, da_ref, dg_ref):
        @pl.when(pl.program_id(0) == 0)
        def _():
            dg_ref[...] = jnp.zeros_like(dg_ref)
        dho_v = do_ref[...]
        dob = dho_v.astype(BF16)
        dn = jnp.zeros((tm, d), F32)
        for k in range(n_blk):
            dz = _dot_nt(dob, wd_ref[k * fb:(k + 1) * fb, :])
            da = (dz * (2.0 * jnp.maximum(a_ref[:, k * fb:(k + 1) * fb].astype(F32), 0.0))).astype(BF16)
            da_ref[:, k * fb:(k + 1) * fb] = da
            dn = dn + _dot_nt(da, wu_ref[k])
        dh, dg = _norm_bwd(dn, h_ref[...], g_ref[...])
        dh_ref[...] = dho_v + dh
        dg_ref[...] += dg

    row = lambda n_: pl.BlockSpec((tm, n_), lambda i: (i, 0))
    return pl.pallas_call(
        body, name=name, grid=(t // tm,),
        in_specs=[row(d), row(d), row(f), _const_spec((1, d)), _const_spec(wup.shape), _const_spec(wdown.shape)],
        out_specs=[row(d), row(f), pl.BlockSpec((8, d), lambda i: (0, 0))],
        out_shape=[jax.ShapeDtypeStruct((t, d), F32), jax.ShapeDtypeStruct((t, f), BF16),
                   jax.ShapeDtypeStruct((8, d), F32)],
        compiler_params=_cp(("arbitrary",)),
    )(dho, h, a, g, wup, wdown)


def _pool_bwd(dho, h, pooled, g, poolw, scale, *, tm=512):
    t, d = h.shape
    tm = min(tm, t)
    ng = len(POOL_WINDOWS)
    cg = d // ng
    nsteps = t // tm

    def body(do_ref, dn_ref, h_ref, p_ref, g_ref, w_ref, s_ref, dh_ref, dw_ref, ds_ref, dg_ref, ext):
        i = pl.program_id(0)

        @pl.when(i == 0)
        def _():
            dw_ref[...] = jnp.zeros_like(dw_ref)
            ds_ref[...] = jnp.zeros_like(ds_ref)
            dg_ref[...] = jnp.zeros_like(dg_ref)
        dho_v = do_ref[...]
        sv = s_ref[...]
        dyp = (dho_v * sv).astype(BF16)
        dyp_halo = (dn_ref[...] * sv).astype(BF16)
        inv = _pool_inv_count(i, tm)
        tnext = ((i + 1) * tm + lax.broadcasted_iota(jnp.int32, (POOL_HALO, 1), 0) + 1).astype(F32)
        last = i == nsteps - 1
        ypre_parts, dpooled_parts = [], []
        for gi, w in enumerate(POOL_WINDOWS):
            cs = slice(gi * cg, (gi + 1) * cg)
            pg = p_ref[:, cs]
            ypre_parts.append(_dot(pg, w_ref[gi]))
            dw_ref[gi] += _dot_tn(pg, dyp[:, cs])
            dpool = _dot_nt(dyp[:, cs], w_ref[gi])
            dpooled_parts.append(dpool)
            ext[0:tm, cs] = dpool * inv[gi]
            dpool_halo = _dot_nt(dyp_halo[:, cs], w_ref[gi]) * (1.0 / jnp.minimum(tnext, float(w)))
            ext[tm:tm + POOL_HALO, cs] = jnp.where(last, 0.0, dpool_halo)
        ds_ref[...] += _rows8(dho_v * jnp.concatenate(ypre_parts, axis=1))
        dn_parts = []
        for gi, w in enumerate(POOL_WINDOWS):
            cs = slice(gi * cg, (gi + 1) * cg)
            s = ext[0:tm, cs]
            for j in range(1, w):
                s = s + ext[j:j + tm, cs]
            dn_parts.append(s - dpooled_parts[gi])
        dh, dg = _norm_bwd(jnp.concatenate(dn_parts, axis=1), h_ref[...], g_ref[...])
        dh_ref[...] = dho_v + dh
        dg_ref[...] += dg

    row = lambda: pl.BlockSpec((tm, d), lambda i: (i, 0))
    acc8 = lambda: pl.BlockSpec((8, d), lambda i: (0, 0))
    return pl.pallas_call(
        body, name="pool_bwd", grid=(nsteps,),
        in_specs=[row(), pl.BlockSpec((POOL_HALO, d), _next_halo(tm, POOL_HALO, t)), row(), row(),
                  _const_spec((1, d)), _const_spec(poolw.shape), _const_spec((1, d))],
        out_specs=[row(), pl.BlockSpec((ng, cg, cg), lambda i: (0, 0, 0)), acc8(), acc8()],
        out_shape=[jax.ShapeDtypeStruct((t, d), F32), jax.ShapeDtypeStruct((ng, cg, cg), F32),
                   jax.ShapeDtypeStruct((8, d), F32), jax.ShapeDtypeStruct((8, d), F32)],
        scratch_shapes=[pltpu.VMEM((tm + POOL_HALO, d), F32)],
        compiler_params=_cp(("arbitrary",)),
    )(dho, dho, h, pooled, g, poolw, scale)


def _outproj_conv_bwd(dh, o, wout, bcx, conv_w, *, tm=512):
    t, d = dh.shape
    tm = min(tm, t)
    ch = CONV_CH
    nsteps = t // tm

    def body(dh_ref, o_ref, w_ref, b_ref, c_ref, x_ref, hc_ref, hx_ref, cw_ref,
             da_ref, dat_ref, db_ref, dw_ref, ext_u, ext_d):
        s = pl.program_id(0)

        @pl.when(s == 0)
        def _():
            dw_ref[...] = jnp.zeros_like(dw_ref)
            ext_d[0:CONV_HALO, :] = jnp.zeros((CONV_HALO, ch), F32)
        dhb = dh_ref[...].astype(BF16)
        for p in range(ATTN_W // PAIR):
            datt = _dot_nt(dhb, w_ref[p * PAIR:(p + 1) * PAIR, :])
            prod = datt * o_ref[:, p * PAIR:(p + 1) * PAIR].astype(F32)
            for hh in range(2):
                lane, head, aux = _head_lanes(hh)
                delta = jnp.sum(jnp.where(head, prod, 0.0), axis=1, keepdims=True)
                aug = _put_pieces(lane, aux + AUX_BIAS, -delta, jnp.where(head, datt, 0.0))
                da_ref[2 * p + hh] = aug.astype(BF16)
                dat_ref[2 * p + hh] = aug.astype(BF16).T
        dcv = _dot_nt(dhb, w_ref[ATTN_W:, :])
        b, c, x = b_ref[...].astype(F32), c_ref[...].astype(F32), x_ref[...].astype(F32)
        ext_u[0:CONV_HALO, :] = jnp.where(s == nsteps - 1, 0.0, hc_ref[...].astype(F32) * hx_ref[...].astype(F32))
        ext_u[CONV_HALO:CONV_HALO + tm, :] = c * x
        dconv = dcv * b
        ext_d[tm:tm + CONV_HALO, :] = ext_d[0:CONV_HALO, :]
        ext_d[0:tm, :] = dconv
        u = [ext_u[CONV_HALO - 2 + k:CONV_HALO - 2 + k + tm, :] for k in range(3)]
        conv = cw_ref[0:1, :] * u[0] + cw_ref[1:2, :] * u[1] + cw_ref[2:3, :] * u[2]
        du = (cw_ref[2:3, :] * dconv + cw_ref[1:2, :] * ext_d[1:1 + tm, :] + cw_ref[0:1, :] * ext_d[2:2 + tm, :])
        db_ref[:, 0:ch] = (dcv * conv).astype(BF16)
        db_ref[:, ch:2 * ch] = (du * x).astype(BF16)
        db_ref[:, 2 * ch:3 * ch] = (du * c).astype(BF16)
        for k in range(3):
            dw_ref[k] += _rows8(dconv * u[k])

    rev = lambda s: nsteps - 1 - s
    row = lambda n_: pl.BlockSpec((tm, n_), lambda s: (rev(s), 0))
    col = lambda k: pl.BlockSpec((tm, ch), lambda s: (rev(s), k))
    prev = lambda k: pl.BlockSpec((CONV_HALO, ch), lambda s: (_prev_halo(tm, CONV_HALO)(rev(s))[0], k))
    return pl.pallas_call(
        body, name="outproj_conv_bwd", grid=(nsteps,),
        in_specs=[row(d), row(ATTN_W), _const_spec(wout.shape), col(0), col(1), col(2), prev(1), prev(2),
                  _const_spec((8, ch))],
        out_specs=[pl.BlockSpec((N_HEADS, tm, PAIR), lambda s: (0, rev(s), 0)),
                   pl.BlockSpec((N_HEADS, PAIR, tm), lambda s: (0, 0, rev(s))),
                   row(3 * ch), pl.BlockSpec((3, 8, ch), lambda s: (0, 0, 0))],
        out_shape=[jax.ShapeDtypeStruct((N_HEADS, t, PAIR), BF16), jax.ShapeDtypeStruct((N_HEADS, PAIR, t), BF16),
                   jax.ShapeDtypeStruct((t, 3 * ch), BF16), jax.ShapeDtypeStruct((3, 8, ch), F32)],
        scratch_shapes=[pltpu.VMEM((CONV_HALO + tm, ch), F32), pltpu.VMEM((tm + CONV_HALO, ch), F32)],
        compiler_params=_cp(("arbitrary",)),
    )(dh, o, wout, bcx, bcx, bcx, bcx, bcx, conv_w)


def _attn_bwd(q_bwd, do_aug, q_bwd_t, do_aug_t, k_aug, v_aug, gblocks, *, tq=1024):
    t = q_bwd.shape[1]
    tq = min(tq, t)
    tk = tq // 2
    nq, nk = t // tq, t // tk
    n_pairs = ATTN_W // PAIR
    n_g = len(gblocks)

    def body(q_ref, do_ref, qt_ref, dot_ref, k_ref, v_ref, *rest):
        dq_ref, dqx_ref, dk_ref, dkx_ref, dv_ref = rest[n_g:n_g + 5]
        dq_scr = rest[2 * n_g + 5]
        scatter = _Exchange(rest[:n_g], rest[n_g + 5:2 * n_g + 5], *rest[2 * n_g + 6:], gather=False)
        j = pl.program_id(1)

        @pl.when((pl.program_id(0) == 0) & (j == 0))
        def _():
            scatter.start()

        @pl.when(j == 0)
        def _():
            dq_scr[...] = jnp.zeros_like(dq_scr)
        k = [k_ref[0], k_ref[1]]
        v = [v_ref[0], v_ref[1]]

        def step(i, carry, diag, rows=tq, row0=0):
            qs = pl.multiple_of(i * tq + row0, tk)
            if diag:
                row = lax.broadcasted_iota(jnp.int32, (rows, tk), 0)
                col = lax.broadcasted_iota(jnp.int32, (rows, tk), 1)
            out = []
            for hh in range(2):
                dk_a, dv_a = carry[hh]
                q = q_ref[hh, pl.ds(qs, rows), :]
                dov = do_ref[hh, pl.ds(qs, rows), :]
                p = jnp.exp2(_dot_nt(q, k[hh]))
                if diag:
                    p = jnp.where(col + (j * tk - i * tq - row0) <= row, p, 0.0)
                ds = (p * _dot_nt(dov, v[hh])).astype(BF16)
                dv_a = dv_a + _dot(dot_ref[hh, :, pl.ds(qs, rows)], p.astype(BF16))
                dk_a = dk_a + _dot(qt_ref[hh, :, pl.ds(qs, rows)], ds)
                dq_scr[hh, pl.ds(qs, rows), :] += _dot(ds, k[hh])
                out.append((dk_a, dv_a))
            return tuple(out)

        zero = (jnp.zeros((PAIR, tk), F32), jnp.zeros((PAIR, tk), F32))
        carry = lax.cond(j % 2 == 0, lambda c: step(j // 2, c, True),
                         lambda c: step(j // 2, c, True, rows=tk, row0=tk), (zero, zero))
        full0 = j // 2 + 1
        odd = (nq - full0) % 2
        carry = lax.cond(odd == 1, lambda c: step(full0, c, False), lambda c: c, carry)
        (dk0, dv0), (dk1, dv1) = lax.fori_loop(
            0, (nq - full0) // 2, lambda ii, c: step(full0 + odd + 2 * ii, c, False, rows=2 * tq), carry)
        first_t = lax.broadcasted_iota(jnp.int32, (PAIR, 1), 0) < HEAD_DIM
        first = lax.broadcasted_iota(jnp.int32, (1, PAIR), 1) < HEAD_DIM
        dk_ref[...] = (jnp.where(first_t, dk0, dk1) * (1.0 / LOG2E)).astype(BF16).T
        dkx_ref[...] = jnp.where(first_t, dk1, dk0).T
        dv_ref[...] = jnp.where(first_t, dv0, dv1).astype(BF16).T

        @pl.when(j == nk - 1)
        def _():
            dq_ref[...] = (jnp.where(first, dq_scr[0], dq_scr[1]) * Q_SCALE).astype(BF16)
            dqx_ref[...] = jnp.where(first, dq_scr[1], dq_scr[0])

        @pl.when((pl.program_id(0) == n_pairs - 1) & (j == nk - 1))
        def _():
            scatter.wait()

    resident = lambda: pl.BlockSpec((2, t, PAIR), lambda p, j: (p, 0, 0), pipeline_mode=pl.Buffered(1))
    resident_t = lambda: pl.BlockSpec((2, PAIR, t), lambda p, j: (p, 0, 0), pipeline_mode=pl.Buffered(1))
    kv_in = lambda: pl.BlockSpec((2, tk, PAIR), lambda p, j: (p, j, 0))
    whole = lambda: pl.BlockSpec((t, PAIR), lambda p, j: (0, p))
    tile = lambda: pl.BlockSpec((tk, PAIR), lambda p, j: (j, p))
    b16 = jax.ShapeDtypeStruct((t, ATTN_W), BF16)
    f32 = jax.ShapeDtypeStruct((t, ATTN_W), F32)
    res = pl.pallas_call(
        body, name="attn_bwd", grid=(n_pairs, nk),
        in_specs=[resident(), resident(), resident_t(), resident_t(), kv_in(), kv_in()] + [HBM_SPEC] * n_g,
        out_specs=[whole(), whole(), tile(), tile(), tile()] + [HBM_SPEC] * n_g,
        out_shape=[b16, f32, b16, f32, b16] + [jax.ShapeDtypeStruct(g.shape, g.dtype) for g in gblocks],
        scratch_shapes=[pltpu.VMEM((2, t, PAIR), F32)] + _Exchange.scratch(n_g),
        compiler_params=_cp(("arbitrary", "arbitrary")),
    )(q_bwd, do_aug, q_bwd_t, do_aug_t, k_aug, v_aug, *gblocks)
    return res[:5], res[5:]


def _fgate_bwd(dqx, dkx, sgate, *, tm=256):
    t = sgate.shape[0]
    tm = min(tm, t)
    nsteps = t // tm

    def body(dq_ref, dk_ref, sg_ref, df_ref, dbf_ref, carry):
        @pl.when(pl.program_id(0) == 0)
        def _():
            carry[...] = jnp.zeros_like(carry)
            dbf_ref[...] = jnp.zeros_like(dbf_ref)
        lane = lax.broadcasted_iota(jnp.int32, (ATTN_W, F_PAD), 0)
        head = lax.broadcasted_iota(jnp.int32, (ATTN_W, F_PAD), 1)
        aux = (head // 2) * PAIR + HEAD_DIM * (1 - head % 2)
        valid = head < N_HEADS
        pick_r = (valid & (lane == aux + AUX_ROWSUM)).astype(F32)
        pick_c = (valid & (lane == aux + AUX_BIAS)).astype(F32)
        hp = lax.Precision.HIGHEST
        dcum = (jnp.dot(dq_ref[...], pick_r, preferred_element_type=F32, precision=lax.Precision.HIGH)
                + jnp.dot(dk_ref[...], pick_c, preferred_element_type=F32, precision=lax.Precision.HIGH))
        r = lax.broadcasted_iota(jnp.int32, (tm, tm), 0)
        c = lax.broadcasted_iota(jnp.int32, (tm, tm), 1)
        tri = (c >= r).astype(F32)
        rc = jnp.dot(tri, dcum, preferred_element_type=F32, precision=hp) + carry[...]
        carry[...] = rc[0:1, :]
        df = rc * sg_ref[...]
        df_ref[...] = df.astype(BF16)
        dbf_ref[...] += _rows8(df)

    rev = lambda i: nsteps - 1 - i
    return pl.pallas_call(
        body, name="fgate_bwd", grid=(nsteps,),
        in_specs=[pl.BlockSpec((tm, ATTN_W), lambda i: (rev(i), 0)), pl.BlockSpec((tm, ATTN_W), lambda i: (rev(i), 0)),
                  pl.BlockSpec((tm, F_PAD), lambda i: (rev(i), 0))],
        out_specs=[pl.BlockSpec((tm, F_PAD), lambda i: (rev(i), 0)), pl.BlockSpec((8, F_PAD), lambda i: (0, 0))],
        out_shape=[jax.ShapeDtypeStruct((t, F_PAD), BF16), jax.ShapeDtypeStruct((8, F_PAD), F32)],
        scratch_shapes=[pltpu.VMEM((1, F_PAD), F32)],
        compiler_params=_cp(("arbitrary",)),
    )(dqx, dkx, sgate)


def _inproj_bwd(dq, dk, dv, df, dbcx, dh, x, g, win_pt, gblock, *, tm=512):
    t, d = x.shape
    tm = min(tm, t)
    nsteps = t // tm
    n_qkv = 3 * ATTN_W

    def body(dq_ref, dk_ref, dv_ref, df_ref, db_ref, dh_ref, x_ref, g_ref, w_ref, gb_ref, gx_ref, dg_ref, land_ref,
             *sems):
        scatter = _Exchange([gb_ref], [land_ref], *sems, gather=False)

        @pl.when(pl.program_id(0) == 0)
        def _():
            scatter.start()
            dg_ref[...] = jnp.zeros_like(dg_ref)
        dn = _dot(df_ref[...], w_ref[n_qkv:n_qkv + F_PAD, :])
        for k, r in enumerate((dq_ref, dk_ref, dv_ref)):
            dn = dn + _dot(r[...], w_ref[k * ATTN_W:(k + 1) * ATTN_W, :])
        for k in range(3):
            c0 = n_qkv + F_PAD + k * CONV_CH
            dn = dn + _dot(db_ref[:, k * CONV_CH:(k + 1) * CONV_CH], w_ref[c0:c0 + CONV_CH, :])
        dx, dg = _norm_bwd(dn, x_ref[...], g_ref[...])
        gx_ref[...] = dh_ref[...] + dx
        dg_ref[...] += dg

        @pl.when(pl.program_id(0) == nsteps - 1)
        def _():
            scatter.wait()

    row = lambda n_: pl.BlockSpec((tm, n_), lambda i: (i, 0))
    return pl.pallas_call(
        body, name="inproj_bwd", grid=(nsteps,),
        in_specs=[row(ATTN_W), row(ATTN_W), row(ATTN_W), row(F_PAD), row(3 * CONV_CH), row(d), row(d),
                  _const_spec((1, d)), _const_spec(win_pt.shape), HBM_SPEC],
        out_specs=[row(d), pl.BlockSpec((8, d), lambda i: (0, 0)), HBM_SPEC],
        out_shape=[jax.ShapeDtypeStruct((t, d), F32), jax.ShapeDtypeStruct((8, d), F32),
                   jax.ShapeDtypeStruct(gblock.shape, gblock.dtype)],
        scratch_shapes=_Exchange.scratch(1),
        compiler_params=_cp(("arbitrary",)),
    )(dq, dk, dv, df, dbcx, dh, x, g, win_pt, gblock)


LATE = ("w_out_0", "w_up_0", "w_down_0", "pool_w_1", "w_up_1", "w_down_1")


def _local_step(x, target, gains, b_f, conv_w, pool_scale, win_pt, shards):
    d = x.shape[1]
    n0, qkv, flog, bcx, cv = _norm_inproj(x, gains["mix0"], win_pt, conv_w)
    q_aug_t, k_aug, v_aug, v_aug_t, sgate = _fgate_prep(flog, b_f, qkv)
    att, q_bwd, q_bwd_t, gathered = _attn_fwd(q_aug_t, k_aug, v_aug_t, [shards[n] for n in LATE])
    g = dict(zip(LATE, gathered))
    wout = g["w_out_0"].reshape(d, d)
    wup0, wup1 = g["w_up_0"], g["w_up_1"]
    wdown0, wdown1 = g["w_down_0"].reshape(-1, d), g["w_down_1"].reshape(-1, d)
    n_grp = len(POOL_WINDOWS)
    cg = d // n_grp
    poolw = g["pool_w_1"].reshape(N_DEV, n_grp, cg // N_DEV, cg).transpose(1, 0, 2, 3).reshape(n_grp, cg, cg)
    h1 = _outproj(att, cv, x, wout)
    h2, n1, a0, z0 = _mlp_fwd(h1, gains["ffn0"], wup0, wdown0, name="mlp_fwd0")
    h3, pooled = _pool_fwd(h2, gains["mix1"], poolw, pool_scale)
    loss, dh4, dg_final, n3, a1, z1 = _mlp_fwd_loss(h3, gains["ffn1"], wup1, wdown1, gains["final"], target,
                                                    name="mlp_fwd1")
    f = a1.shape[1]
    fb = f // N_DEV
    dh3, da1, dg_ffn1 = _mlp_bwd(dh4, h3, a1, gains["ffn1"], wup1, wdown1, name="mlp_bwd1")
    dwdown1 = _mm_tn(z1, dh4, name="dwdown1", ta=1024, tb=1024, tt=2048, out_dtype=BF16)
    dwup1 = _mm_tn(n3, da1, name="dwup1", ta=d, tb=fb, tt=4096, blocked_out=True, out_dtype=BF16)
    dh2, dpoolw, dscale, dg_mix1 = _pool_bwd(dh3, h2, pooled, gains["mix1"], poolw, pool_scale)
    dh1, da0, dg_ffn0 = _mlp_bwd(dh2, h1, a0, gains["ffn0"], wup0, wdown0, name="mlp_bwd0")
    dwdown0 = _mm_tn(z0, dh2, name="dwdown0", ta=1024, tb=1024, tt=2048, out_dtype=BF16)
    dwup0 = _mm_tn(n1, da0, name="dwup0", ta=d, tb=fb, tt=4096, blocked_out=True, out_dtype=BF16)
    do_aug, do_aug_t, dbcx, dconvw = _outproj_conv_bwd(dh1, att, wout, bcx, conv_w)
    dwout = _mm_tn_cat([att, cv], [dh1], name="dwout", tt=2048)
    gblocks = {
        "w_out_0": dwout.reshape(N_DEV, d // N_DEV, d), "w_up_0": dwup0, "w_up_1": dwup1,
        "w_down_0": dwdown0.reshape(N_DEV, -1, d), "w_down_1": dwdown1.reshape(N_DEV, -1, d),
        "pool_w_1": dpoolw.astype(BF16).reshape(n_grp, N_DEV, cg // N_DEV, cg).transpose(1, 0, 2, 3).reshape(
            N_DEV, n_grp * (cg // N_DEV), cg),
    }
    (dq, dqx, dk, dkx, dv), landed = _attn_bwd(q_bwd, do_aug, q_bwd_t, do_aug_t, k_aug, v_aug,
                                               [gblocks[n] for n in LATE])
    df, dbf = _fgate_bwd(dqx, dkx, sgate)
    dwin_t = jnp.concatenate(
        [_mm_tn_cat([dq, dk, dv], [n0], name="dwin_qkv", tt=2048),
         _mm_tn(df, n0, name="dwin_f", ta=F_PAD, tb=d, tt=2048, out_dtype=BF16)[:N_HEADS],
         _mm_tn(dbcx, n0, name="dwin_bcx", ta=512, tb=d, tt=4096, out_dtype=BF16)], axis=0)
    dwin_blocks = dwin_t.reshape(N_DEV, dwin_t.shape[0] // N_DEV, d)
    grad_x, dg_mix0, landed_win = _inproj_bwd(dq, dk, dv, df, dbcx, dh1, x, gains["mix0"], win_pt, dwin_blocks)
    small = dict(mix0=dg_mix0, ffn0=dg_ffn0, mix1=dg_mix1, pool_scale=dscale, ffn1=dg_ffn1, final=dg_final,
                 b_f=dbf, conv_w=dconvw)
    return loss, grad_x, dict(zip(LATE + ("w_in_0",), tuple(landed) + (landed_win,))), small


def _mesh_places():
    x, y, c = lax.axis_index("x"), lax.axis_index("y"), lax.axis_index("c")
    chips = [(1 - x, y), (x, 1 - y), (1 - x, 1 - y)]
    return (x, y, c), (x, y, 1 - c), chips


def _all_gather(shards):
    n = len(shards)

    def body(*refs):
        ins, outs = refs[:n], refs[n:2 * n]
        send_sems, recv_sems, local_sems = refs[2 * n:]
        me, sib, chips = _mesh_places()
        c = me[2]

        def copy(ai, k, block, to, src=None):
            dst = outs[ai].at[_slot(*block)]
            return pltpu.make_async_remote_copy(
                src_ref=dst if src is None else src, dst_ref=dst, send_sem=send_sems.at[7 * ai + k],
                recv_sem=recv_sems.at[7 * ai + k], device_id=to, device_id_type=MESH)

        mine = [pltpu.make_async_copy(ins[ai], outs[ai].at[_slot(*me)], local_sems.at[ai]) for ai in range(n)]
        for cp in mine:
            cp.start()
        first = []
        for ai in range(n):
            first.append(copy(ai, 0, me, sib, src=ins[ai]))
            first += [copy(ai, 1 + j, me, (*chip, c), src=ins[ai]) for j, chip in enumerate(chips)]
        for cp in first:
            cp.start()
        passed = []
        for ai in range(n):
            for j, chip in enumerate(chips):
                copy(ai, 1 + j, (*chip, c), me).wait_recv()
                cp = copy(ai, 4 + j, (*chip, c), sib)
                cp.start()
                passed.append(cp)
        for ai in range(n):
            copy(ai, 0, sib, me).wait_recv()
            for j, chip in enumerate(chips):
                copy(ai, 4 + j, (*chip, 1 - c), me).wait_recv()
        for cp in first + passed:
            cp.wait_send()
        for cp in mine:
            cp.wait()

    return pl.pallas_call(
        body, name="all_gather",
        in_specs=[HBM_SPEC] * n, out_specs=[HBM_SPEC] * n,
        out_shape=[jax.ShapeDtypeStruct((N_DEV,) + s.shape, s.dtype) for s in shards],
        scratch_shapes=[pltpu.SemaphoreType.DMA((7 * n,)), pltpu.SemaphoreType.DMA((7 * n,)),
                        pltpu.SemaphoreType.DMA((n,))],
    )(*shards)


SMALL_ROWS = 16


def _small_allreduce(parts):
    n, _, w = parts.shape
    assert n <= SMALL_ROWS

    def body(p_ref, o_ref, gath, send_sems, recv_sems):
        x, y, c = lax.axis_index("x"), lax.axis_index("y"), lax.axis_index("c")
        my = _slot(x, y, c)
        rows = [jnp.sum(p_ref[i], axis=0, keepdims=True) for i in range(n)]
        rows.append(jnp.zeros((SMALL_ROWS - n, w), F32))
        gath[my] = jnp.concatenate(rows, axis=0)
        copies = []
        for k in range(1, N_DEV):
            px, py, pc = x ^ (k >> 2), y ^ ((k >> 1) & 1), c ^ (k & 1)
            cp = pltpu.make_async_remote_copy(
                src_ref=gath.at[my], dst_ref=gath.at[my], send_sem=send_sems.at[k - 1], recv_sem=recv_sems.at[k - 1],
                device_id=(px, py, pc), device_id_type=MESH)
            cp.start()
            copies.append(cp)
        for cp in copies:
            cp.wait()
        acc = gath[0]
        for d in range(1, N_DEV):
            acc = acc + gath[d]
        o_ref[...] = acc

    return pl.pallas_call(
        body, name="small_allreduce",
        in_specs=[VMEM_SPEC], out_specs=VMEM_SPEC,
        out_shape=jax.ShapeDtypeStruct((SMALL_ROWS, w), F32),
        scratch_shapes=[pltpu.VMEM((N_DEV, SMALL_ROWS, w), F32), pltpu.SemaphoreType.DMA((N_DEV - 1,)),
                        pltpu.SemaphoreType.DMA((N_DEV - 1,))],
    )(parts)


def _adamw(g, w, m, v, *, name, tm=256):
    r, c = g.shape
    tm = tm if r % tm == 0 else r
    bc1 = 1.0 - ADAM_B1 ** ADAM_STEP
    bc2 = 1.0 - ADAM_B2 ** ADAM_STEP

    def body(g_ref, w_ref, m_ref, v_ref, d_ref, nm_ref, nv_ref):
        gv = g_ref[...]
        nm = ADAM_B1 * m_ref[...] + (1.0 - ADAM_B1) * gv
        nv = ADAM_B2 * v_ref[...] + (1.0 - ADAM_B2) * jnp.square(gv)
        nm_ref[...] = nm
        nv_ref[...] = nv
        d_ref[...] = -ADAM_LR * ((nm / bc1) / (jnp.sqrt(nv / bc2) + ADAM_EPS) + ADAM_WD * w_ref[...])

    blk = pl.BlockSpec((tm, c), lambda i: (i, 0))
    shp = jax.ShapeDtypeStruct((r, c), F32)
    return pl.pallas_call(
        body, name=name, grid=(r // tm,), in_specs=[blk] * 4, out_specs=[blk] * 3, out_shape=[shp] * 3,
        compiler_params=_cp(("parallel",)),
    )(g, w, m, v)


def _sum_blocks(parts, *, name):
    def body(p_ref, o_ref):
        acc = p_ref[0].astype(F32)
        for k in range(1, N_DEV):
            acc = acc + p_ref[k].astype(F32)
        o_ref[...] = acc

    return pl.pallas_call(body, name=name, out_shape=jax.ShapeDtypeStruct(parts.shape[1:], F32),
                          compiler_params=_cp())(parts)


def _adamw_sum(parts, w, m, v, *, name, tm=256):
    _, r, c = parts.shape
    tm = tm if r % tm == 0 else r
    bc1 = 1.0 - ADAM_B1 ** ADAM_STEP
    bc2 = 1.0 - ADAM_B2 ** ADAM_STEP

    def body(p_ref, w_ref, m_ref, v_ref, g_ref, d_ref, nm_ref, nv_ref):
        gv = p_ref[0].astype(F32)
        for k in range(1, N_DEV):
            gv = gv + p_ref[k].astype(F32)
        g_ref[...] = gv
        nm = ADAM_B1 * m_ref[...] + (1.0 - ADAM_B1) * gv
        nv = ADAM_B2 * v_ref[...] + (1.0 - ADAM_B2) * jnp.square(gv)
        nm_ref[...] = nm
        nv_ref[...] = nv
        d_ref[...] = -ADAM_LR * ((nm / bc1) / (jnp.sqrt(nv / bc2) + ADAM_EPS) + ADAM_WD * w_ref[...])

    blk = pl.BlockSpec((tm, c), lambda i: (i, 0))
    shp = jax.ShapeDtypeStruct((r, c), F32)
    return pl.pallas_call(
        body, name=name, grid=(r // tm,), in_specs=[pl.BlockSpec((N_DEV, tm, c), lambda i: (0, i, 0))] + [blk] * 3,
        out_specs=[blk] * 4, out_shape=[shp] * 4, compiler_params=_cp(("parallel",)),
    )(parts, w, m, v)


BIG = ("w_in_0", "w_out_0", "w_up_0", "w_down_0", "pool_w_1", "w_up_1", "w_down_1")
SMALL = ("norm_mix_0", "norm_ffn_0", "norm_mix_1", "pool_scale_1", "norm_ffn_1", "final_norm", "b_f_0", "conv_w_0")
WEIGHTS = ("norm_mix_0", "w_in_0", "b_f_0", "conv_w_0", "w_out_0", "norm_ffn_0", "w_up_0", "w_down_0", "norm_mix_1",
           "pool_w_1", "pool_scale_1", "norm_ffn_1", "w_up_1", "w_down_1", "final_norm")


def _pad_to(a, rows, cols):
    return jnp.pad(a, ((0, rows - a.shape[0]), (0, cols - a.shape[1])))


def _pack_small(p, width):
    rows = [p[n].reshape(1, -1) for n in SMALL[:6]]
    rows.append(_pad_to(p["b_f_0"].reshape(1, -1), 1, width))
    rows.append(_pad_to(p["conv_w_0"], 3, width))
    return _pad_to(jnp.concatenate(rows, axis=0), SMALL_ROWS, width)


def _unpack_small(a, like):
    out = {n: a[i] for i, n in enumerate(SMALL[:6])}
    out["b_f_0"] = a[6, :like["b_f_0"].shape[0]]
    out["conv_w_0"] = a[7:10, :like["conv_w_0"].shape[1]]
    return out


def kernel(x, norm_mix_0, w_in_0, b_f_0, conv_w_0, w_out_0, norm_ffn_0, w_up_0, w_down_0, norm_mix_1, pool_w_1, pool_scale_1, norm_ffn_1, w_up_1, w_down_1, final_norm, loss_target, m_norm_mix_0, m_w_in_0, m_b_f_0, m_conv_w_0, m_w_out_0, m_norm_ffn_0, m_w_up_0, m_w_down_0, m_norm_mix_1, m_pool_w_1, m_pool_scale_1, m_norm_ffn_1, m_w_up_1, m_w_down_1, m_final_norm, v_norm_mix_0, v_w_in_0, v_b_f_0, v_conv_w_0, v_w_out_0, v_norm_ffn_0, v_w_up_0, v_w_down_0, v_norm_mix_1, v_pool_w_1, v_pool_scale_1, v_norm_ffn_1, v_w_up_1, v_w_down_1, v_final_norm):
    w = dict(norm_mix_0=norm_mix_0, w_in_0=w_in_0, b_f_0=b_f_0, conv_w_0=conv_w_0, w_out_0=w_out_0,
             norm_ffn_0=norm_ffn_0, w_up_0=w_up_0, w_down_0=w_down_0, norm_mix_1=norm_mix_1, pool_w_1=pool_w_1,
             pool_scale_1=pool_scale_1, norm_ffn_1=norm_ffn_1, w_up_1=w_up_1, w_down_1=w_down_1, final_norm=final_norm)
    m = dict(norm_mix_0=m_norm_mix_0, w_in_0=m_w_in_0, b_f_0=m_b_f_0, conv_w_0=m_conv_w_0, w_out_0=m_w_out_0,
             norm_ffn_0=m_norm_ffn_0, w_up_0=m_w_up_0, w_down_0=m_w_down_0, norm_mix_1=m_norm_mix_1,
             pool_w_1=m_pool_w_1, pool_scale_1=m_pool_scale_1, norm_ffn_1=m_norm_ffn_1, w_up_1=m_w_up_1,
             w_down_1=m_w_down_1, final_norm=m_final_norm)
    v = dict(norm_mix_0=v_norm_mix_0, w_in_0=v_w_in_0, b_f_0=v_b_f_0, conv_w_0=v_conv_w_0, w_out_0=v_w_out_0,
             norm_ffn_0=v_norm_ffn_0, w_up_0=v_w_up_0, w_down_0=v_w_down_0, norm_mix_1=v_norm_mix_1,
             pool_w_1=v_pool_w_1, pool_scale_1=v_pool_scale_1, norm_ffn_1=v_norm_ffn_1, w_up_1=v_w_up_1,
             w_down_1=v_w_down_1, final_norm=v_final_norm)
    d = x.shape[-1]
    n_in = w_in_0.shape[1] * N_DEV
    n_qkv = 3 * ATTN_W
    pool_g, pool_rows, pool_c = pool_w_1.shape

    def shard2d(p):
        return {n: (p[n].reshape(pool_g * pool_rows, pool_c) if n == "pool_w_1" else p[n]) for n in BIG}
    w2, m2, v2 = shard2d(w), shard2d(m), shard2d(v)

    conv_cols = conv_w_0.shape[1]
    win_g8, conv_g8 = _all_gather([w_in_0.T.astype(BF16), _pad_to(conv_w_0, 8, 128)])
    conv_full = conv_g8[:, :, :conv_cols].transpose(1, 0, 2).reshape(8, N_DEV * conv_cols)
    win_t = win_g8.reshape(n_in, d)
    win_pt = jnp.concatenate([win_t[:n_qkv], _pad_to(win_t[n_qkv:n_qkv + N_HEADS], F_PAD, d),
                              win_t[n_qkv + N_HEADS:]], axis=0)

    gains = dict(mix0=norm_mix_0.reshape(1, d), ffn0=norm_ffn_0.reshape(1, d), mix1=norm_mix_1.reshape(1, d),
                 ffn1=norm_ffn_1.reshape(1, d), final=final_norm.reshape(1, d))
    dev = _slot(lax.axis_index("x"), lax.axis_index("y"), lax.axis_index("c"))
    loss8, grad_x, landed, small = _local_step(
        x[0], loss_target[0], gains, _pad_to(b_f_0.reshape(1, -1), 1, F_PAD), conv_full, pool_scale_1.reshape(1, d),
        win_pt, {n: w2[n].astype(BF16) for n in LATE})
    parts = jnp.concatenate(
        [small[k][None] for k in ("mix0", "ffn0", "mix1", "pool_scale", "ffn1", "final")]
        + [_pad_to(small["b_f"], 8, d)[None], jnp.pad(small["conv_w"], ((0, 0), (0, 0), (0, d - CONV_CH))),
           _pad_to(loss8[0:1, 0:1], 8, d)[None]], axis=0)
    tot = _small_allreduce(parts)
    loss = tot[10, 0]
    conv_g = lax.dynamic_slice(tot, (7, dev * conv_cols), (3, conv_cols))
    gs = tot.at[7:10].set(_pad_to(conv_g, 3, d))

    grads, deltas, new_m, new_v = {}, {}, {}, {}
    for n in BIG:
        if n in LATE:
            gr, dl, nm, nv = _adamw_sum(landed[n], w2[n], m2[n], v2[n], name="adamw_" + n)
        else:
            gr = _sum_blocks(landed[n], name="gsum_" + n).T
            dl, nm, nv = _adamw(gr, w2[n], m2[n], v2[n], name="adamw_" + n)
        for dst, val in ((grads, gr), (deltas, dl), (new_m, nm), (new_v, nv)):
            dst[n] = val.reshape(w[n].shape)
    dl, nm, nv = _adamw(gs, _pack_small(w, d), _pack_small(m, d), _pack_small(v, d), name="adamw_small")
    for dst, val in ((grads, gs), (deltas, dl), (new_m, nm), (new_v, nv)):
        dst.update(_unpack_small(val, w))
    return (loss, grad_x[None], *[grads[n] for n in WEIGHTS], *[deltas[n] for n in WEIGHTS],
            *[new_m[n] for n in WEIGHTS], *[new_v[n] for n in WEIGHTS])
```

```python
import functools

import jax
import jax.numpy as jnp
from jax import lax
from jax.experimental import pallas as pl
from jax.experimental.pallas import tpu as pltpu

F32 = jnp.float32
BF16 = jnp.bfloat16

N_DEV = 8
N_HEADS = 8
HEAD_DIM = 64
PAIR = 2 * HEAD_DIM
ATTN_W = N_HEADS * HEAD_DIM
CONV_CH = 512
F_PAD = 128
POOL_WINDOWS = (2, 4, 8, 16)
POOL_HALO = 16
CONV_HALO = 16
RMS_EPS = 1e-6
Q_SCALE = HEAD_DIM ** -0.5
LOG2E = 1.4426950408889634
NEG = -1e30
AUX_BIAS = 0
AUX_LSE = 3
AUX_ROWSUM = 6
ADAM_LR, ADAM_B1, ADAM_B2, ADAM_EPS, ADAM_WD, ADAM_STEP = 0.001, 0.9, 0.999, 1e-08, 0.01, 10
MESH = pl.DeviceIdType.MESH
VMEM_LIMIT = 56 * 2**20


def _cp(sem=None, vmem=VMEM_LIMIT, **kw):
    return pltpu.CompilerParams(dimension_semantics=sem, vmem_limit_bytes=vmem, **kw)


def _dot(a, b):
    return jnp.dot(a, b, preferred_element_type=F32)


def _dot_nt(a, b):
    return lax.dot_general(a, b, (((1,), (1,)), ((), ())), preferred_element_type=F32)


def _dot_tn(a, b):
    return lax.dot_general(a, b, (((0,), (0,)), ((), ())), preferred_element_type=F32)


def _rstd(h):
    return lax.rsqrt(jnp.mean(h * h, axis=-1, keepdims=True) + RMS_EPS)


def _rows8(x):
    r, n = x.shape
    return jnp.sum(x.reshape(r // 8, 8, n), axis=0)


def _norm_bwd(dn, h, g):
    r = _rstd(h)
    xhat = h * r
    dy = dn * g
    dh = r * (dy - xhat * jnp.mean(dy * xhat, axis=-1, keepdims=True))
    return dh, _rows8(dn * xhat)


def _const_spec(shape):
    nd = len(shape)
    return pl.BlockSpec(shape, lambda *_: (0,) * nd, pipeline_mode=pl.Buffered(1))


HBM_SPEC = pl.BlockSpec(memory_space=pltpu.HBM)
VMEM_SPEC = pl.BlockSpec(memory_space=pltpu.VMEM)


def _slot(px, py, pc):
    return 4 * px + 2 * py + pc


class _Exchange:
    def __init__(self, srcs, dsts, send_sems, recv_sems, local_sems, gather):
        x, y, c = lax.axis_index("x"), lax.axis_index("y"), lax.axis_index("c")
        me = _slot(x, y, c)
        self.copies = []
        for a, (src, dst) in enumerate(zip(srcs, dsts)):
            self.copies.append(pltpu.make_async_copy(src if gather else src.at[me], dst.at[me], local_sems.at[a]))
            for k in range(1, N_DEV):
                px, py, pc = x ^ (k >> 2), y ^ ((k >> 1) & 1), c ^ (k & 1)
                self.copies.append(pltpu.make_async_remote_copy(
                    src_ref=src if gather else src.at[_slot(px, py, pc)], dst_ref=dst.at[me],
                    send_sem=send_sems.at[(N_DEV - 1) * a + k - 1], recv_sem=recv_sems.at[(N_DEV - 1) * a + k - 1],
                    device_id=(px, py, pc), device_id_type=MESH))

    def start(self):
        for cp in self.copies:
            cp.start()

    def wait(self):
        for cp in self.copies:
            cp.wait()

    @staticmethod
    def scratch(n):
        return [pltpu.SemaphoreType.DMA(((N_DEV - 1) * n,)), pltpu.SemaphoreType.DMA(((N_DEV - 1) * n,)),
                pltpu.SemaphoreType.DMA((n,))]


def _norm_inproj(x, g, win_pt, conv_w, *, tm=512):
    t, d = x.shape
    n_all = win_pt.shape[0]
    n_qkv = 3 * ATTN_W
    n_bcx = 3 * CONV_CH
    assert n_all == n_qkv + F_PAD + n_bcx
    tm = min(tm, t)
    ch = CONV_CH

    def body(x_ref, g_ref, w_ref, cw_ref, n_ref, qkv_ref, f_ref, bcx_ref, cv_ref, ext):
        h = x_ref[...]
        n = (h * _rstd(h) * g_ref[...]).astype(BF16)
        n_ref[...] = n
        for c0 in range(0, n_qkv, 512):
            acc = _dot_nt(n, w_ref[c0:c0 + 512, :])
            if c0 < ATTN_W:
                acc = acc * (Q_SCALE * LOG2E)
            qkv_ref[:, c0:c0 + 512] = acc.astype(BF16)
        f_ref[...] = _dot_nt(n, w_ref[n_qkv:n_qkv + F_PAD, :])
        bcx = []
        for k in range(3):
            c0 = n_qkv + F_PAD + k * ch
            v = _dot_nt(n, w_ref[c0:c0 + ch, :]).astype(BF16)
            bcx_ref[:, k * ch:(k + 1) * ch] = v
            bcx.append(v.astype(F32))
        @pl.when(pl.program_id(0) == 0)
        def _():
            ext[tm:tm + CONV_HALO, :] = jnp.zeros((CONV_HALO, ch), F32)
        ext[0:CONV_HALO, :] = ext[tm:tm + CONV_HALO, :]
        ext[CONV_HALO:CONV_HALO + tm, :] = bcx[1] * bcx[2]
        conv = (cw_ref[0:1, :] * ext[CONV_HALO - 2:CONV_HALO - 2 + tm, :]
                + cw_ref[1:2, :] * ext[CONV_HALO - 1:CONV_HALO - 1 + tm, :]
                + cw_ref[2:3, :] * ext[CONV_HALO:CONV_HALO + tm, :])
        cv_ref[...] = (bcx[0] * conv).astype(BF16)

    return pl.pallas_call(
        body, name="norm_inproj", grid=(t // tm,),
        in_specs=[pl.BlockSpec((tm, d), lambda i: (i, 0)), _const_spec((1, d)), _const_spec((n_all, d)),
                  _const_spec((8, ch))],
        out_specs=[pl.BlockSpec((tm, d), lambda i: (i, 0)), pl.BlockSpec((tm, n_qkv), lambda i: (i, 0)),
                   pl.BlockSpec((tm, F_PAD), lambda i: (i, 0)), pl.BlockSpec((tm, n_bcx), lambda i: (i, 0)),
                   pl.BlockSpec((tm, ch), lambda i: (i, 0))],
        out_shape=[jax.ShapeDtypeStruct((t, d), BF16), jax.ShapeDtypeStruct((t, n_qkv), BF16),
                   jax.ShapeDtypeStruct((t, F_PAD), F32), jax.ShapeDtypeStruct((t, n_bcx), BF16),
                   jax.ShapeDtypeStruct((t, ch), BF16)],
        scratch_shapes=[pltpu.VMEM((CONV_HALO + tm, ch), F32)],
        compiler_params=_cp(("arbitrary",)),
    )(x, g, win_pt, conv_w)


def _head_lanes(h):
    lane = lax.broadcasted_iota(jnp.int32, (1, PAIR), 1)
    hh = h % 2
    return lane, lane // HEAD_DIM == hh, HEAD_DIM * (1 - hh)


def _pieces(col):
    hi = col.astype(BF16).astype(F32)
    r1 = col - hi
    mid = r1.astype(BF16).astype(F32)
    lo = (r1 - mid).astype(BF16).astype(F32)
    return hi, mid, lo


def _put_pieces(lane, first, col, other):
    hi, mid, lo = _pieces(col)
    return jnp.where(lane == first, hi, jnp.where(lane == first + 1, mid, jnp.where(lane == first + 2, lo, other)))


def _fgate_prep(flog, b_f, qkv, *, tm=512):
    t = flog.shape[0]
    tm = min(tm, t)

    def body(f_ref, b_ref, qkv_ref, qat_ref, ka_ref, va_ref, vat_ref, sg_ref, carry):
        @pl.when(pl.program_id(0) == 0)
        def _():
            carry[...] = jnp.zeros_like(carry)
        z = f_ref[...] + b_ref[...]
        e = jnp.exp(-jnp.abs(z))
        logf = jnp.minimum(z, 0.0) - jnp.log(1.0 + e)
        sg_ref[...] = jnp.where(z >= 0, e, 1.0) / (1.0 + e)
        r = lax.broadcasted_iota(jnp.int32, (tm, tm), 0)
        c = lax.broadcasted_iota(jnp.int32, (tm, tm), 1)
        tri = (c <= r).astype(F32)
        cs = jnp.dot(tri, logf, preferred_element_type=F32, precision=lax.Precision.HIGHEST) + carry[...]
        carry[...] = cs[tm - 1:tm, :]
        cs2 = cs * LOG2E
        for h in range(N_HEADS):
            lane, head, aux = _head_lanes(h)
            p0 = (h // 2) * PAIR
            ones = ((lane >= aux + AUX_LSE) & (lane <= aux + AUX_ROWSUM)).astype(F32)
            bias = (lane >= aux + AUX_BIAS) & (lane < aux + AUX_BIAS + 3)
            k_aux = _put_pieces(lane, aux + AUX_BIAS, cs2[:, h:h + 1], ones)
            q_aug = jnp.where(head, qkv_ref[:, p0:p0 + PAIR], jnp.where(bias, -1.0, 0.0).astype(BF16))
            v_aug = jnp.where(head, qkv_ref[:, 2 * ATTN_W + p0:2 * ATTN_W + p0 + PAIR],
                              jnp.where(bias, 1.0, 0.0).astype(BF16))
            qat_ref[h] = q_aug.T
            ka_ref[h] = jnp.where(head, qkv_ref[:, ATTN_W + p0:ATTN_W + p0 + PAIR], k_aux.astype(BF16))
            va_ref[h] = v_aug
            vat_ref[h] = v_aug.T

    aug = lambda: pl.BlockSpec((N_HEADS, tm, PAIR), lambda i: (0, i, 0))
    aug_t = lambda: pl.BlockSpec((N_HEADS, PAIR, tm), lambda i: (0, 0, i))
    aug_shape = jax.ShapeDtypeStruct((N_HEADS, t, PAIR), BF16)
    aug_t_shape = jax.ShapeDtypeStruct((N_HEADS, PAIR, t), BF16)
    return pl.pallas_call(
        body, name="fgate_prep", grid=(t // tm,),
        in_specs=[pl.BlockSpec((tm, F_PAD), lambda i: (i, 0)), _const_spec((1, F_PAD)),
                  pl.BlockSpec((tm, 3 * ATTN_W), lambda i: (i, 0))],
        out_specs=[aug_t(), aug(), aug(), aug_t(), pl.BlockSpec((tm, F_PAD), lambda i: (i, 0))],
        out_shape=[aug_t_shape, aug_shape, aug_shape, aug_t_shape, jax.ShapeDtypeStruct((t, F_PAD), F32)],
        scratch_shapes=[pltpu.VMEM((1, F_PAD), F32)],
        compiler_params=_cp(("arbitrary",)),
    )(flog, b_f, qkv)


def _put_pieces_t(row, first, vec, other):
    hi, mid, lo = _pieces(vec)
    return jnp.where(row == first, hi, jnp.where(row == first + 1, mid, jnp.where(row == first + 2, lo, other)))


def _attn_fwd(q_aug_t, k_aug, v_aug_t, shards, *, tq=1024):
    t = k_aug.shape[1]
    tq = min(tq, t)
    tk = tq // 2
    nq = t // tq
    n_pairs = ATTN_W // PAIR
    n_sh = len(shards)

    def body(qt_ref, k_ref, vt_ref, *rest):
        o_ref, qb_ref, qbt_ref = rest[n_sh:n_sh + 3]
        s_scr = rest[2 * n_sh + 3]
        gather = _Exchange(rest[:n_sh], rest[n_sh + 3:2 * n_sh + 3], *rest[2 * n_sh + 4:], gather=True)
        i = pl.program_id(1)

        @pl.when((pl.program_id(0) == 0) & (i == 0))
        def _():
            gather.start()
        key = lax.broadcasted_iota(jnp.int32, (tk, tq), 0)
        qry = lax.broadcasted_iota(jnp.int32, (tk, tq), 1)
        qt = [qt_ref[0], qt_ref[1]]

        def logits(hh, tile, slot, diag):
            s = _dot(k_ref[hh, pl.ds(pl.multiple_of(tile * tk, tk), tk), :], qt[hh])
            if diag:
                s = jnp.where(key + (tile * tk - i * tq) <= qry, s, NEG)
            s_scr[hh, slot] = s
            return jnp.max(s, axis=0, keepdims=True)

        def probs(hh, tile, slot, m, acc, tmax):
            mn = jnp.maximum(m, tmax)
            p = jnp.exp2(s_scr[hh, slot] - mn).astype(BF16)
            acc = jnp.exp2(m - mn) * acc + _dot(vt_ref[hh, :, pl.ds(pl.multiple_of(tile * tk, tk), tk)], p)
            return mn, acc

        def advance(carry, prev, slot, nxt, diag=False):
            out = []
            for hh in range(2):
                m, acc, tmax = carry[hh]
                m, acc = probs(hh, prev, slot, m, acc, tmax)
                out.append((m, acc, logits(hh, nxt, 1 - slot, diag)))
            return tuple(out)

        def two_tiles(jj, carry):
            carry = advance(carry, jnp.where(jj == 0, 2 * i, 2 * jj - 1), 1, 2 * jj)
            return advance(carry, 2 * jj, 0, 2 * jj + 1)

        init = tuple((jnp.full((1, tq), NEG, F32), jnp.zeros((PAIR, tq), F32), logits(hh, 2 * i + 1, 0, True))
                     for hh in range(2))
        carry = advance(init, 2 * i + 1, 0, 2 * i, diag=True)
        carry = lax.fori_loop(0, i, two_tiles, carry)
        last = jnp.where(i == 0, 2 * i, 2 * i - 1)
        row = lax.broadcasted_iota(jnp.int32, (PAIR, 1), 0)
        res = []
        for hh in range(2):
            aux = HEAD_DIM * (1 - hh)
            m, acc, tmax = carry[hh]
            m, acc = probs(hh, last, 1, m, acc, tmax)
            l = acc[aux + AUX_BIAS:aux + AUX_BIAS + 1, :]
            qbt = _put_pieces_t(row, aux + AUX_LSE, -(m + jnp.log2(l)), qt[hh].astype(F32))
            qbt_ref[hh] = qbt.astype(BF16)
            qb_ref[hh] = qbt.astype(BF16).T
            res.append(acc * (1.0 / l))
        o_ref[...] = jnp.where(row < HEAD_DIM, res[0], res[1]).astype(BF16).T

        @pl.when((pl.program_id(0) == n_pairs - 1) & (i == nq - 1))
        def _():
            gather.wait()

    res = pl.pallas_call(
        body, name="attn_fwd", grid=(n_pairs, nq),
        in_specs=[pl.BlockSpec((2, PAIR, tq), lambda p, i: (p, 0, i)),
                  pl.BlockSpec((2, t, PAIR), lambda p, i: (p, 0, 0), pipeline_mode=pl.Buffered(1)),
                  pl.BlockSpec((2, PAIR, t), lambda p, i: (p, 0, 0), pipeline_mode=pl.Buffered(1))] + [HBM_SPEC] * n_sh,
        out_specs=[pl.BlockSpec((tq, PAIR), lambda p, i: (i, p)),
                   pl.BlockSpec((2, tq, PAIR), lambda p, i: (p, i, 0)),
                   pl.BlockSpec((2, PAIR, tq), lambda p, i: (p, 0, i))] + [HBM_SPEC] * n_sh,
        out_shape=[jax.ShapeDtypeStruct((t, ATTN_W), BF16), jax.ShapeDtypeStruct((N_HEADS, t, PAIR), BF16),
                   jax.ShapeDtypeStruct((N_HEADS, PAIR, t), BF16)]
        + [jax.ShapeDtypeStruct((N_DEV,) + s.shape, s.dtype) for s in shards],
        scratch_shapes=[pltpu.VMEM((2, 2, tk, tq), F32)] + _Exchange.scratch(n_sh),
        compiler_params=_cp(("arbitrary", "arbitrary")),
    )(q_aug_t, k_aug, v_aug_t, *shards)
    return res[0], res[1], res[2], res[3:]


def _prev_halo(tm, halo):
    return lambda i: (jnp.maximum(i * (tm // halo) - 1, 0), 0)


def _next_halo(tm, halo, t):
    return lambda i: (jnp.minimum((i + 1) * (tm // halo), t // halo - 1), 0)


def _mlp_tile(hh, g_ref, wu_ref, wd_ref, n_ref, a_ref, z_ref):
    n_blk, _, fb = wu_ref.shape
    n = (hh * _rstd(hh) * g_ref[...]).astype(BF16)
    n_ref[...] = n
    acc = hh
    for k in range(n_blk):
        a = _dot(n, wu_ref[k])
        zz = jnp.square(jnp.maximum(a, 0.0)).astype(BF16)
        a_ref[:, k * fb:(k + 1) * fb] = a.astype(BF16)
        z_ref[:, k * fb:(k + 1) * fb] = zz
        acc = acc + _dot(zz, wd_ref[k * fb:(k + 1) * fb, :])
    return acc


def _outproj(att, cv, x, wout, *, tm=512):
    t, d = x.shape
    tm = min(tm, t)

    def body(a_ref, c_ref, x_ref, w_ref, h_ref):
        h_ref[...] = x_ref[...] + _dot(a_ref[...], w_ref[0:ATTN_W, :]) + _dot(c_ref[...], w_ref[ATTN_W:, :])

    return pl.pallas_call(
        body, name="outproj", grid=(t // tm,),
        in_specs=[pl.BlockSpec((tm, ATTN_W), lambda i: (i, 0)), pl.BlockSpec((tm, CONV_CH), lambda i: (i, 0)),
                  pl.BlockSpec((tm, d), lambda i: (i, 0)), _const_spec(wout.shape)],
        out_specs=pl.BlockSpec((tm, d), lambda i: (i, 0)),
        out_shape=jax.ShapeDtypeStruct((t, d), F32),
        compiler_params=_cp(("parallel",)),
    )(att, cv, x, wout)


def _mlp_fwd(h, g, wup, wdown, *, name, tm=512):
    t, d = h.shape
    n_blk, _, fb = wup.shape
    f = n_blk * fb
    tm = min(tm, t)

    def body(h_ref, g_ref, wu_ref, wd_ref, ho_ref, n_ref, a_ref, z_ref):
        ho_ref[...] = _mlp_tile(h_ref[...], g_ref, wu_ref, wd_ref, n_ref, a_ref, z_ref)

    row = lambda n_: pl.BlockSpec((tm, n_), lambda i: (i, 0))
    return pl.pallas_call(
        body, name=name, grid=(t // tm,),
        in_specs=[row(d), _const_spec((1, d)), _const_spec(wup.shape), _const_spec(wdown.shape)],
        out_specs=[row(d), row(d), row(f), row(f)],
        out_shape=[jax.ShapeDtypeStruct((t, d), F32), jax.ShapeDtypeStruct((t, d), BF16),
                   jax.ShapeDtypeStruct((t, f), BF16), jax.ShapeDtypeStruct((t, f), BF16)],
        compiler_params=_cp(("parallel",)),
    )(h, g, wup, wdown)


def _mlp_fwd_loss(h, g, wup, wdown, g_out, target, *, name, tm=512):
    t, d = h.shape
    n_blk, _, fb = wup.shape
    f = n_blk * fb
    tm = min(tm, t)
    nsteps = t // tm

    def body(h_ref, g_ref, wu_ref, wd_ref, go_ref, y_ref, loss_ref, dh_ref, dg_ref, n_ref, a_ref, z_ref, lacc):
        i = pl.program_id(0)

        @pl.when(i == 0)
        def _():
            lacc[...] = jnp.zeros_like(lacc)
            dg_ref[...] = jnp.zeros_like(dg_ref)
        hv = _mlp_tile(h_ref[...], g_ref, wu_ref, wd_ref, n_ref, a_ref, z_ref)
        gv = go_ref[...]
        r = _rstd(hv)
        xhat = hv * r
        err = xhat * gv - y_ref[...]
        lacc[...] += _rows8(err * err)
        dout = err * (1.0 / d)
        dy = dout * gv
        dg_ref[...] += _rows8(dout * xhat)
        dh_ref[...] = r * (dy - xhat * jnp.mean(dy * xhat, axis=-1, keepdims=True))

        @pl.when(i == nsteps - 1)
        def _():
            loss_ref[...] = jnp.full(loss_ref.shape, (0.5 / d) * jnp.sum(lacc[...]), F32)

    row = lambda n_: pl.BlockSpec((tm, n_), lambda i: (i, 0))
    return pl.pallas_call(
        body, name=name, grid=(nsteps,),
        in_specs=[row(d), _const_spec((1, d)), _const_spec(wup.shape), _const_spec(wdown.shape), _const_spec((1, d)),
                  row(d)],
        out_specs=[pl.BlockSpec((8, 128), lambda i: (0, 0)), row(d), pl.BlockSpec((8, d), lambda i: (0, 0)),
                   row(d), row(f), row(f)],
        out_shape=[jax.ShapeDtypeStruct((8, 128), F32), jax.ShapeDtypeStruct((t, d), F32),
                   jax.ShapeDtypeStruct((8, d), F32), jax.ShapeDtypeStruct((t, d), BF16),
                   jax.ShapeDtypeStruct((t, f), BF16), jax.ShapeDtypeStruct((t, f), BF16)],
        scratch_shapes=[pltpu.VMEM((8, d), F32)],
        compiler_params=_cp(("arbitrary",)),
    )(h, g, wup, wdown, g_out, target)


def _pool_inv_count(i, tm):
    tglob = (i * tm + lax.broadcasted_iota(jnp.int32, (tm, 1), 0) + 1).astype(F32)
    return [1.0 / jnp.minimum(tglob, float(w)) for w in POOL_WINDOWS]


def _pool_fwd(h, g, poolw, scale, *, tm=512):
    t, d = h.shape
    tm = min(tm, t)
    cg = d // len(POOL_WINDOWS)

    def body(h_ref, hh_ref, g_ref, w_ref, s_ref, ho_ref, p_ref, ext):
        i = pl.program_id(0)
        hv = h_ref[...]
        halo = hh_ref[...]
        n = hv * _rstd(hv) * g_ref[...]
        ext[0:POOL_HALO, :] = jnp.where(i == 0, 0.0, halo * _rstd(halo) * g_ref[...])
        ext[POOL_HALO:POOL_HALO + tm, :] = n
        inv = _pool_inv_count(i, tm)
        for gi, w in enumerate(POOL_WINDOWS):
            cs = slice(gi * cg, (gi + 1) * cg)
            s = ext[POOL_HALO:POOL_HALO + tm, cs]
            for j in range(1, w):
                s = s + ext[POOL_HALO - j:POOL_HALO - j + tm, cs]
            pooled = (s * inv[gi] - n[:, cs]).astype(BF16)
            p_ref[:, cs] = pooled
            ho_ref[:, cs] = hv[:, cs] + _dot(pooled, w_ref[gi]) * s_ref[:, cs]

    row = lambda: pl.BlockSpec((tm, d), lambda i: (i, 0))
    return pl.pallas_call(
        body, name="pool_fwd", grid=(t // tm,),
        in_specs=[row(), pl.BlockSpec((POOL_HALO, d), _prev_halo(tm, POOL_HALO)), _const_spec((1, d)),
                  _const_spec(poolw.shape), _const_spec((1, d))],
        out_specs=[row(), row()],
        out_shape=[jax.ShapeDtypeStruct((t, d), F32), jax.ShapeDtypeStruct((t, d), BF16)],
        scratch_shapes=[pltpu.VMEM((POOL_HALO + tm, d), F32)],
        compiler_params=_cp(("parallel",)),
    )(h, h, g, poolw, scale)


def _mm_tn(a, b, *, name, ta, tb, tt, blocked_out=False, out_dtype=F32):
    t, ka = a.shape
    n = b.shape[1]
    ta, tb, tt = min(ta, ka), min(tb, n), min(tt, t)
    nt = t // tt

    def body(a_ref, b_ref, o_ref, acc):
        @pl.when(pl.program_id(2) == 0)
        def _():
            acc[...] = jnp.zeros_like(acc)
        acc[...] += _dot_tn(a_ref[...].astype(BF16), b_ref[...].astype(BF16))

        @pl.when(pl.program_id(2) == nt - 1)
        def _():
            o_ref[...] = acc[...].astype(out_dtype)

    if blocked_out:
        assert ta == ka
        out_shape = jax.ShapeDtypeStruct((n // tb, ka, tb), out_dtype)
        out_spec = pl.BlockSpec((None, ta, tb), lambda i, j, k: (j, i, 0))
    else:
        out_shape = jax.ShapeDtypeStruct((ka, n), out_dtype)
        out_spec = pl.BlockSpec((ta, tb), lambda i, j, k: (i, j))
    return pl.pallas_call(
        body, name=name, grid=(ka // ta, n // tb, nt),
        in_specs=[pl.BlockSpec((tt, ta), lambda i, j, k: (k, i)), pl.BlockSpec((tt, tb), lambda i, j, k: (k, j))],
        out_specs=out_spec, out_shape=out_shape, scratch_shapes=[pltpu.VMEM((ta, tb), F32)],
        compiler_params=_cp(("parallel", "parallel", "arbitrary")),
    )(a, b)


def _mm_tn_cat(a_list, b_list, *, name, tt, out_dtype=BF16):
    t = a_list[0].shape[0]
    ta, tb = a_list[0].shape[1], b_list[0].shape[1]
    na, nb = len(a_list), len(b_list)
    tt = min(tt, t)
    nt = t // tt

    def body(*refs):
        a_refs, b_refs, o_ref, acc = refs[:na], refs[na:na + nb], refs[na + nb], refs[na + nb + 1]
        i, j, k = pl.program_id(0), pl.program_id(1), pl.program_id(2)

        @pl.when(k == 0)
        def _():
            acc[...] = jnp.zeros_like(acc)
        for ia in range(na):
            for ib in range(nb):
                @pl.when((i == ia) & (j == ib))
                def _(ia=ia, ib=ib):
                    acc[...] += _dot_tn(a_refs[ia][...].astype(BF16), b_refs[ib][...].astype(BF16))

        @pl.when(k == nt - 1)
        def _():
            o_ref[...] = acc[...].astype(out_dtype)

    def held(m, axis):
        def index(i, j, k):
            cur = (i, j)[axis]
            return (jnp.where(cur == m, k, jnp.where(cur < m, 0, nt - 1)), 0)
        return index

    return pl.pallas_call(
        body, name=name, grid=(na, nb, nt),
        in_specs=[pl.BlockSpec((tt, ta), held(m, 0)) for m in range(na)]
        + [pl.BlockSpec((tt, tb), held(m, 1)) for m in range(nb)],
        out_specs=pl.BlockSpec((ta, tb), lambda i, j, k: (i, j)),
        out_shape=jax.ShapeDtypeStruct((na * ta, nb * tb), out_dtype), scratch_shapes=[pltpu.VMEM((ta, tb), F32)],
        compiler_params=_cp(("arbitrary", "arbitrary", "arbitrary")),
    )(*a_list, *b_list)


def _mlp_bwd(dho, h, a, g, wup, wdown, *, name, tm=512):
    t, d = h.shape
    n_blk, _, fb = wup.shape
    f = n_blk * fb
    tm = min(tm, t)

    def body(do_ref, h_ref, a_ref, g_ref, wu_ref, wd_ref, dh_ref, da_ref, dg_ref):
        @pl.when(pl.program_id(0) == 0)
        def _():
            dg_ref[...] = jnp.zeros_like(dg_ref)
        dho_v = do_ref[...]
        dob = dho_v.astype(BF16)
        dn = jnp.zeros((tm, d), F32)
        for k in range(n_blk):
            dz = _dot_nt(dob, wd_ref[k * fb:(k + 1) * fb, :])
            da = (dz * (2.0 * jnp.maximum(a_ref[:, k * fb:(k + 1) * fb].astype(F32), 0.0))).astype(BF16)
            da_ref[:, k * fb:(k + 1) * fb] = da
            dn = dn + _dot_nt(da, wu_ref[k])
        dh, dg = _norm_bwd(dn, h_ref[...], g_ref[...])
        dh_ref[...] = dho_v + dh
        dg_ref[...] += dg

    row = lambda n_: pl.BlockSpec((tm, n_), lambda i: (i, 0))
    return pl.pallas_call(
        body, name=name, grid=(t // tm,),
        in_specs=[row(d), row(d), row(f), _const_spec((1, d)), _const_spec(wup.shape), _const_spec(wdown.shape)],
        out_specs=[row(d), row(f), pl.BlockSpec((8, d), lambda i: (0, 0))],
        out_shape=[jax.ShapeDtypeStruct((t, d), F32), jax.ShapeDtypeStruct((t, f), BF16),
                   jax.ShapeDtypeStruct((8, d), F32)],
        compiler_params=_cp(("arbitrary",)),
    )(dho, h, a, g, wup, wdown)


def _pool_bwd(dho, h, pooled, g, poolw, scale, *, tm=512):
    t, d = h.shape
    tm = min(tm, t)
    ng = len(POOL_WINDOWS)
    cg = d // ng
    nsteps = t // tm

    def body(do_ref, dn_ref, h_ref, p_ref, g_ref, w_ref, s_ref, dh_ref, dw_ref, ds_ref, dg_ref, ext):
        i = pl.program_id(0)

        @pl.when(i == 0)
        def _():
            dw_ref[...] = jnp.zeros_like(dw_ref)
            ds_ref[...] = jnp.zeros_like(ds_ref)
            dg_ref[...] = jnp.zeros_like(dg_ref)
        dho_v = do_ref[...]
        sv = s_ref[...]
        dyp = (dho_v * sv).astype(BF16)
        dyp_halo = (dn_ref[...] * sv).astype(BF16)
        inv = _pool_inv_count(i, tm)
        tnext = ((i + 1) * tm + lax.broadcasted_iota(jnp.int32, (POOL_HALO, 1), 0) + 1).astype(F32)
        last = i == nsteps - 1
        ypre_parts, dpooled_parts = [], []
        for gi, w in enumerate(POOL_WINDOWS):
            cs = slice(gi * cg, (gi + 1) * cg)
            pg = p_ref[:, cs]
            ypre_parts.append(_dot(pg, w_ref[gi]))
            dw_ref[gi] += _dot_tn(pg, dyp[:, cs])
            dpool = _dot_nt(dyp[:, cs], w_ref[gi])
            dpooled_parts.append(dpool)
            ext[0:tm, cs] = dpool * inv[gi]
            dpool_halo = _dot_nt(dyp_halo[:, cs], w_ref[gi]) * (1.0 / jnp.minimum(tnext, float(w)))
            ext[tm:tm + POOL_HALO, cs] = jnp.where(last, 0.0, dpool_halo)
        ds_ref[...] += _rows8(dho_v * jnp.concatenate(ypre_parts, axis=1))
        dn_parts = []
        for gi, w in enumerate(POOL_WINDOWS):
            cs = slice(gi * cg, (gi + 1) * cg)
            s = ext[0:tm, cs]
            for j in range(1, w):
                s = s + ext[j:j + tm, cs]
            dn_parts.append(s - dpooled_parts[gi])
        dh, dg = _norm_bwd(jnp.concatenate(dn_parts, axis=1), h_ref[...], g_ref[...])
        dh_ref[...] = dho_v + dh
        dg_ref[...] += dg

    row = lambda: pl.BlockSpec((tm, d), lambda i: (i, 0))
    acc8 = lambda: pl.BlockSpec((8, d), lambda i: (0, 0))
    return pl.pallas_call(
        body, name="pool_bwd", grid=(nsteps,),
        in_specs=[row(), pl.BlockSpec((POOL_HALO, d), _next_halo(tm, POOL_HALO, t)), row(), row(),
                  _const_spec((1, d)), _const_spec(poolw.shape), _const_spec((1, d))],
        out_specs=[row(), pl.BlockSpec((ng, cg, cg), lambda i: (0, 0, 0)), acc8(), acc8()],
        out_shape=[jax.ShapeDtypeStruct((t, d), F32), jax.ShapeDtypeStruct((ng, cg, cg), F32),
                   jax.ShapeDtypeStruct((8, d), F32), jax.ShapeDtypeStruct((8, d), F32)],
        scratch_shapes=[pltpu.VMEM((tm + POOL_HALO, d), F32)],
        compiler_params=_cp(("arbitrary",)),
    )(dho, dho, h, pooled, g, poolw, scale)


def _outproj_conv_bwd(dh, o, wout, bcx, conv_w, *, tm=512):
    t, d = dh.shape
    tm = min(tm, t)
    ch = CONV_CH
    nsteps = t // tm

    def body(dh_ref, o_ref, w_ref, b_ref, c_ref, x_ref, hc_ref, hx_ref, cw_ref,
             da_ref, dat_ref, db_ref, dw_ref, ext_u, ext_d):
        s = pl.program_id(0)

        @pl.when(s == 0)
        def _():
            dw_ref[...] = jnp.zeros_like(dw_ref)
            ext_d[0:CONV_HALO, :] = jnp.zeros((CONV_HALO, ch), F32)
        dhb = dh_ref[...].astype(BF16)
        for p in range(ATTN_W // PAIR):
            datt = _dot_nt(dhb, w_ref[p * PAIR:(p + 1) * PAIR, :])
            prod = datt * o_ref[:, p * PAIR:(p + 1) * PAIR].astype(F32)
            for hh in range(2):
                lane, head, aux = _head_lanes(hh)
                delta = jnp.sum(jnp.where(head, prod, 0.0), axis=1, keepdims=True)
                aug = _put_pieces(lane, aux + AUX_BIAS, -delta, jnp.where(head, datt, 0.0))
                da_ref[2 * p + hh] = aug.astype(BF16)
                dat_ref[2 * p + hh] = aug.astype(BF16).T
        dcv = _dot_nt(dhb, w_ref[ATTN_W:, :])
        b, c, x = b_ref[...].astype(F32), c_ref[...].astype(F32), x_ref[...].astype(F32)
        ext_u[0:CONV_HALO, :] = jnp.where(s == nsteps - 1, 0.0, hc_ref[...].astype(F32) * hx_ref[...].astype(F32))
        ext_u[CONV_HALO:CONV_HALO + tm, :] = c * x
        dconv = dcv * b
        ext_d[tm:tm + CONV_HALO, :] = ext_d[0:CONV_HALO, :]
        ext_d[0:tm, :] = dconv
        u = [ext_u[CONV_HALO - 2 + k:CONV_HALO - 2 + k + tm, :] for k in range(3)]
        conv = cw_ref[0:1, :] * u[0] + cw_ref[1:2, :] * u[1] + cw_ref[2:3, :] * u[2]
        du = (cw_ref[2:3, :] * dconv + cw_ref[1:2, :] * ext_d[1:1 + tm, :] + cw_ref[0:1, :] * ext_d[2:2 + tm, :])
        db_ref[:, 0:ch] = (dcv * conv).astype(BF16)
        db_ref[:, ch:2 * ch] = (du * x).astype(BF16)
        db_ref[:, 2 * ch:3 * ch] = (du * c).astype(BF16)
        for k in range(3):
            dw_ref[k] += _rows8(dconv * u[k])

    rev = lambda s: nsteps - 1 - s
    row = lambda n_: pl.BlockSpec((tm, n_), lambda s: (rev(s), 0))
    col = lambda k: pl.BlockSpec((tm, ch), lambda s: (rev(s), k))
    prev = lambda k: pl.BlockSpec((CONV_HALO, ch), lambda s: (_prev_halo(tm, CONV_HALO)(rev(s))[0], k))
    return pl.pallas_call(
        body, name="outproj_conv_bwd", grid=(nsteps,),
        in_specs=[row(d), row(ATTN_W), _const_spec(wout.shape), col(0), col(1), col(2), prev(1), prev(2),
                  _const_spec((8, ch))],
        out_specs=[pl.BlockSpec((N_HEADS, tm, PAIR), lambda s: (0, rev(s), 0)),
                   pl.BlockSpec((N_HEADS, PAIR, tm), lambda s: (0, 0, rev(s))),
                   row(3 * ch), pl.BlockSpec((3, 8, ch), lambda s: (0, 0, 0))],
        out_shape=[jax.ShapeDtypeStruct((N_HEADS, t, PAIR), BF16), jax.ShapeDtypeStruct((N_HEADS, PAIR, t), BF16),
                   jax.ShapeDtypeStruct((t, 3 * ch), BF16), jax.ShapeDtypeStruct((3, 8, ch), F32)],
        scratch_shapes=[pltpu.VMEM((CONV_HALO + tm, ch), F32), pltpu.VMEM((tm + CONV_HALO, ch), F32)],
        compiler_params=_cp(("arbitrary",)),
    )(dh, o, wout, bcx, bcx, bcx, bcx, bcx, conv_w)


def _attn_bwd(q_bwd, do_aug, q_bwd_t, do_aug_t, k_aug, v_aug, gblocks, *, tq=1024):
    t = q_bwd.shape[1]
    tq = min(tq, t)
    tk = tq // 2
    nq, nk = t // tq, t // tk
    n_pairs = ATTN_W // PAIR
    n_g = len(gblocks)

    def body(q_ref, do_ref, qt_ref, dot_ref, k_ref, v_ref, *rest):
        dq_ref, dqx_ref, dk_ref, dkx_ref, dv_ref = rest[n_g:n_g + 5]
        dq_scr = rest[2 * n_g + 5]
        scatter = _Exchange(rest[:n_g], rest[n_g + 5:2 * n_g + 5], *rest[2 * n_g + 6:], gather=False)
        j = pl.program_id(1)

        @pl.when((pl.program_id(0) == 0) & (j == 0))
        def _():
            scatter.start()

        @pl.when(j == 0)
        def _():
            dq_scr[...] = jnp.zeros_like(dq_scr)
        k = [k_ref[0], k_ref[1]]
        v = [v_ref[0], v_ref[1]]

        def step(i, carry, diag, rows=tq, row0=0):
            qs = pl.multiple_of(i * tq + row0, tk)
            if diag:
                row = lax.broadcasted_iota(jnp.int32, (rows, tk), 0)
                col = lax.broadcasted_iota(jnp.int32, (rows, tk), 1)
            out = []
            for hh in range(2):
                dk_a, dv_a = carry[hh]
                q = q_ref[hh, pl.ds(qs, rows), :]
                dov = do_ref[hh, pl.ds(qs, rows), :]
                p = jnp.exp2(_dot_nt(q, k[hh]))
                if diag:
                    p = jnp.where(col + (j * tk - i * tq - row0) <= row, p, 0.0)
                ds = (p * _dot_nt(dov, v[hh])).astype(BF16)
                dv_a = dv_a + _dot(dot_ref[hh, :, pl.ds(qs, rows)], p.astype(BF16))
                dk_a = dk_a + _dot(qt_ref[hh, :, pl.ds(qs, rows)], ds)
                dq_scr[hh, pl.ds(qs, rows), :] += _dot(ds, k[hh])
                out.append((dk_a, dv_a))
            return tuple(out)

        zero = (jnp.zeros((PAIR, tk), F32), jnp.zeros((PAIR, tk), F32))
        carry = lax.cond(j % 2 == 0, lambda c: step(j // 2, c, True),
                         lambda c: step(j // 2, c, True, rows=tk, row0=tk), (zero, zero))
        full0 = j // 2 + 1
        odd = (nq - full0) % 2
        carry = lax.cond(odd == 1, lambda c: step(full0, c, False), lambda c: c, carry)
        (dk0, dv0), (dk1, dv1) = lax.fori_loop(
            0, (nq - full0) // 2, lambda ii, c: step(full0 + odd + 2 * ii, c, False, rows=2 * tq), carry)
        first_t = lax.broadcasted_iota(jnp.int32, (PAIR, 1), 0) < HEAD_DIM
        first = lax.broadcasted_iota(jnp.int32, (1, PAIR), 1) < HEAD_DIM
        dk_ref[...] = (jnp.where(first_t, dk0, dk1) * (1.0 / LOG2E)).astype(BF16).T
        dkx_ref[...] = jnp.where(first_t, dk1, dk0).T
        dv_ref[...] = jnp.where(first_t, dv0, dv1).astype(BF16).T

        @pl.when(j == nk - 1)
        def _():
            dq_ref[...] = (jnp.where(first, dq_scr[0], dq_scr[1]) * Q_SCALE).astype(BF16)
            dqx_ref[...] = jnp.where(first, dq_scr[1], dq_scr[0])

        @pl.when((pl.program_id(0) == n_pairs - 1) & (j == nk - 1))
        def _():
            scatter.wait()

    resident = lambda: pl.BlockSpec((2, t, PAIR), lambda p, j: (p, 0, 0), pipeline_mode=pl.Buffered(1))
    resident_t = lambda: pl.BlockSpec((2, PAIR, t), lambda p, j: (p, 0, 0), pipeline_mode=pl.Buffered(1))
    kv_in = lambda: pl.BlockSpec((2, tk, PAIR), lambda p, j: (p, j, 0))
    whole = lambda: pl.BlockSpec((t, PAIR), lambda p, j: (0, p))
    tile = lambda: pl.BlockSpec((tk, PAIR), lambda p, j: (j, p))
    b16 = jax.ShapeDtypeStruct((t, ATTN_W), BF16)
    f32 = jax.ShapeDtypeStruct((t, ATTN_W), F32)
    res = pl.pallas_call(
        body, name="attn_bwd", grid=(n_pairs, nk),
        in_specs=[resident(), resident(), resident_t(), resident_t(), kv_in(), kv_in()] + [HBM_SPEC] * n_g,
        out_specs=[whole(), whole(), tile(), tile(), tile()] + [HBM_SPEC] * n_g,
        out_shape=[b16, f32, b16, f32, b16] + [jax.ShapeDtypeStruct(g.shape, g.dtype) for g in gblocks],
        scratch_shapes=[pltpu.VMEM((2, t, PAIR), F32)] + _Exchange.scratch(n_g),
        compiler_params=_cp(("arbitrary", "arbitrary")),
    )(q_bwd, do_aug, q_bwd_t, do_aug_t, k_aug, v_aug, *gblocks)
    return res[:5], res[5:]


def _fgate_bwd(dqx, dkx, sgate, *, tm=256):
    t = sgate.shape[0]
    tm = min(tm, t)
    nsteps = t // tm

    def body(dq_ref, dk_ref, sg_ref, df_ref, dbf_ref, carry):
        @pl.when(pl.program_id(0) == 0)
        def _():
            carry[...] = jnp.zeros_like(carry)
            dbf_ref[...] = jnp.zeros_like(dbf_ref)
        lane = lax.broadcasted_iota(jnp.int32, (ATTN_W, F_PAD), 0)
        head = lax.broadcasted_iota(jnp.int32, (ATTN_W, F_PAD), 1)
        aux = (head // 2) * PAIR + HEAD_DIM * (1 - head % 2)
        valid = head < N_HEADS
        pick_r = (valid & (lane == aux + AUX_ROWSUM)).astype(F32)
        pick_c = (valid & (lane == aux + AUX_BIAS)).astype(F32)
        hp = lax.Precision.HIGHEST
        dcum = (jnp.dot(dq_ref[...], pick_r, preferred_element_type=F32, precision=lax.Precision.HIGH)
                + jnp.dot(dk_ref[...], pick_c, preferred_element_type=F32, precision=lax.Precision.HIGH))
        r = lax.broadcasted_iota(jnp.int32, (tm, tm), 0)
        c = lax.broadcasted_iota(jnp.int32, (tm, tm), 1)
        tri = (c >= r).astype(F32)
        rc = jnp.dot(tri, dcum, preferred_element_type=F32, precision=hp) + carry[...]
        carry[...] = rc[0:1, :]
        df = rc * sg_ref[...]
        df_ref[...] = df.astype(BF16)
        dbf_ref[...] += _rows8(df)

    rev = lambda i: nsteps - 1 - i
    return pl.pallas_call(
        body, name="fgate_bwd", grid=(nsteps,),
        in_specs=[pl.BlockSpec((tm, ATTN_W), lambda i: (rev(i), 0)), pl.BlockSpec((tm, ATTN_W), lambda i: (rev(i), 0)),
                  pl.BlockSpec((tm, F_PAD), lambda i: (rev(i), 0))],
        out_specs=[pl.BlockSpec((tm, F_PAD), lambda i: (rev(i), 0)), pl.BlockSpec((8, F_PAD), lambda i: (0, 0))],
        out_shape=[jax.ShapeDtypeStruct((t, F_PAD), BF16), jax.ShapeDtypeStruct((8, F_PAD), F32)],
        scratch_shapes=[pltpu.VMEM((1, F_PAD), F32)],
        compiler_params=_cp(("arbitrary",)),
    )(dqx, dkx, sgate)


def _inproj_bwd(dq, dk, dv, df, dbcx, dh, x, g, win_pt, gblock, *, tm=512):
    t, d = x.shape
    tm = min(tm, t)
    nsteps = t // tm
    n_qkv = 3 * ATTN_W

    def body(dq_ref, dk_ref, dv_ref, df_ref, db_ref, dh_ref, x_ref, g_ref, w_ref, gb_ref, gx_ref, dg_ref, land_ref,
             *sems):
        scatter = _Exchange([gb_ref], [land_ref], *sems, gather=False)

        @pl.when(pl.program_id(0) == 0)
        def _():
            scatter.start()
            dg_ref[...] = jnp.zeros_like(dg_ref)
        dn = _dot(df_ref[...], w_ref[n_qkv:n_qkv + F_PAD, :])
        for k, r in enumerate((dq_ref, dk_ref, dv_ref)):
            dn = dn + _dot(r[...], w_ref[k * ATTN_W:(k + 1) * ATTN_W, :])
        for k in range(3):
            c0 = n_qkv + F_PAD + k * CONV_CH
            dn = dn + _dot(db_ref[:, k * CONV_CH:(k + 1) * CONV_CH], w_ref[c0:c0 + CONV_CH, :])
        dx, dg = _norm_bwd(dn, x_ref[...], g_ref[...])
        gx_ref[...] = dh_ref[...] + dx
        dg_ref[...] += dg

        @pl.when(pl.program_id(0) == nsteps - 1)
        def _():
            scatter.wait()

    row = lambda n_: pl.BlockSpec((tm, n_), lambda i: (i, 0))
    return pl.pallas_call(
        body, name="inproj_bwd", grid=(nsteps,),
        in_specs=[row(ATTN_W), row(ATTN_W), row(ATTN_W), row(F_PAD), row(3 * CONV_CH), row(d), row(d),
                  _const_spec((1, d)), _const_spec(win_pt.shape), HBM_SPEC],
        out_specs=[row(d), pl.BlockSpec((8, d), lambda i: (0, 0)), HBM_SPEC],
        out_shape=[jax.ShapeDtypeStruct((t, d), F32), jax.ShapeDtypeStruct((8, d), F32),
                   jax.ShapeDtypeStruct(gblock.shape, gblock.dtype)],
        scratch_shapes=_Exchange.scratch(1),
        compiler_params=_cp(("arbitrary",)),
    )(dq, dk, dv, df, dbcx, dh, x, g, win_pt, gblock)


LATE = ("w_out_0", "w_up_0", "w_down_0", "pool_w_1", "w_up_1", "w_down_1")


def _local_step(x, target, gains, b_f, conv_w, pool_scale, win_pt, shards):
    d = x.shape[1]
    n0, qkv, flog, bcx, cv = _norm_inproj(x, gains["mix0"], win_pt, conv_w)
    q_aug_t, k_aug, v_aug, v_aug_t, sgate = _fgate_prep(flog, b_f, qkv)
    att, q_bwd, q_bwd_t, gathered = _attn_fwd(q_aug_t, k_aug, v_aug_t, [shards[n] for n in LATE])
    g = dict(zip(LATE, gathered))
    wout = g["w_out_0"].reshape(d, d)
    wup0, wup1 = g["w_up_0"], g["w_up_1"]
    wdown0, wdown1 = g["w_down_0"].reshape(-1, d), g["w_down_1"].reshape(-1, d)
    n_grp = len(POOL_WINDOWS)
    cg = d // n_grp
    poolw = g["pool_w_1"].reshape(N_DEV, n_grp, cg // N_DEV, cg).transpose(1, 0, 2, 3).reshape(n_grp, cg, cg)
    h1 = _outproj(att, cv, x, wout)
    h2, n1, a0, z0 = _mlp_fwd(h1, gains["ffn0"], wup0, wdown0, name="mlp_fwd0")
    h3, pooled = _pool_fwd(h2, gains["mix1"], poolw, pool_scale)
    loss, dh4, dg_final, n3, a1, z1 = _mlp_fwd_loss(h3, gains["ffn1"], wup1, wdown1, gains["final"], target,
                                                    name="mlp_fwd1")
    f = a1.shape[1]
    fb = f // N_DEV
    dh3, da1, dg_ffn1 = _mlp_bwd(dh4, h3, a1, gains["ffn1"], wup1, wdown1, name="mlp_bwd1")
    dwdown1 = _mm_tn(z1, dh4, name="dwdown1", ta=1024, tb=1024, tt=2048, out_dtype=BF16)
    dwup1 = _mm_tn(n3, da1, name="dwup1", ta=d, tb=fb, tt=4096, blocked_out=True, out_dtype=BF16)
    dh2, dpoolw, dscale, dg_mix1 = _pool_bwd(dh3, h2, pooled, gains["mix1"], poolw, pool_scale)
    dh1, da0, dg_ffn0 = _mlp_bwd(dh2, h1, a0, gains["ffn0"], wup0, wdown0, name="mlp_bwd0")
    dwdown0 = _mm_tn(z0, dh2, name="dwdown0", ta=1024, tb=1024, tt=2048, out_dtype=BF16)
    dwup0 = _mm_tn(n1, da0, name="dwup0", ta=d, tb=fb, tt=4096, blocked_out=True, out_dtype=BF16)
    do_aug, do_aug_t, dbcx, dconvw = _outproj_conv_bwd(dh1, att, wout, bcx, conv_w)
    dwout = _mm_tn_cat([att, cv], [dh1], name="dwout", tt=2048)
    gblocks = {
        "w_out_0": dwout.reshape(N_DEV, d // N_DEV, d), "w_up_0": dwup0, "w_up_1": dwup1,
        "w_down_0": dwdown0.reshape(N_DEV, -1, d), "w_down_1": dwdown1.reshape(N_DEV, -1, d),
        "pool_w_1": dpoolw.astype(BF16).reshape(n_grp, N_DEV, cg // N_DEV, cg).transpose(1, 0, 2, 3).reshape(
            N_DEV, n_grp * (cg // N_DEV), cg),
    }
    (dq, dqx, dk, dkx, dv), landed = _attn_bwd(q_bwd, do_aug, q_bwd_t, do_aug_t, k_aug, v_aug,
                                               [gblocks[n] for n in LATE])
    df, dbf = _fgate_bwd(dqx, dkx, sgate)
    dwin_t = jnp.concatenate(
        [_mm_tn_cat([dq, dk, dv], [n0], name="dwin_qkv", tt=2048),
         _mm_tn(df, n0, name="dwin_f", ta=F_PAD, tb=d, tt=2048, out_dtype=BF16)[:N_HEADS],
         _mm_tn(dbcx, n0, name="dwin_bcx", ta=512, tb=d, tt=4096, out_dtype=BF16)], axis=0)
    dwin_blocks = dwin_t.reshape(N_DEV, dwin_t.shape[0] // N_DEV, d)
    grad_x, dg_mix0, landed_win = _inproj_bwd(dq, dk, dv, df, dbcx, dh1, x, gains["mix0"], win_pt, dwin_blocks)
    small = dict(mix0=dg_mix0, ffn0=dg_ffn0, mix1=dg_mix1, pool_scale=dscale, ffn1=dg_ffn1, final=dg_final,
                 b_f=dbf, conv_w=dconvw)
    return loss, grad_x, dict(zip(LATE + ("w_in_0",), tuple(landed) + (landed_win,))), small


def _mesh_places():
    x, y, c = lax.axis_index("x"), lax.axis_index("y"), lax.axis_index("c")
    chips = [(1 - x, y), (x, 1 - y), (1 - x, 1 - y)]
    return (x, y, c), (x, y, 1 - c), chips


def _all_gather(shards):
    n = len(shards)

    def body(*refs):
        ins, outs = refs[:n], refs[n:2 * n]
        send_sems, recv_sems, local_sems = refs[2 * n:]
        me, sib, chips = _mesh_places()
        c = me[2]

        def copy(ai, k, block, to, src=None):
            dst = outs[ai].at[_slot(*block)]
            return pltpu.make_async_remote_copy(
                src_ref=dst if src is None else src, dst_ref=dst, send_sem=send_sems.at[7 * ai + k],
                recv_sem=recv_sems.at[7 * ai + k], device_id=to, device_id_type=MESH)

        mine = [pltpu.make_async_copy(ins[ai], outs[ai].at[_slot(*me)], local_sems.at[ai]) for ai in range(n)]
        for cp in mine:
            cp.start()
        first = []
        for ai in range(n):
            first.append(copy(ai, 0, me, sib, src=ins[ai]))
            first += [copy(ai, 1 + j, me, (*chip, c), src=ins[ai]) for j, chip in enumerate(chips)]
        for cp in first:
            cp.start()
        passed = []
        for ai in range(n):
            for j, chip in enumerate(chips):
                copy(ai, 1 + j, (*chip, c), me).wait_recv()
                cp = copy(ai, 4 + j, (*chip, c), sib)
                cp.start()
                passed.append(cp)
        for ai in range(n):
            copy(ai, 0, sib, me).wait_recv()
            for j, chip in enumerate(chips):
                copy(ai, 4 + j, (*chip, 1 - c), me).wait_recv()
        for cp in first + passed:
            cp.wait_send()
        for cp in mine:
            cp.wait()

    return pl.pallas_call(
        body, name="all_gather",
        in_specs=[HBM_SPEC] * n, out_specs=[HBM_SPEC] * n,
        out_shape=[jax.ShapeDtypeStruct((N_DEV,) + s.shape, s.dtype) for s in shards],
        scratch_shapes=[pltpu.SemaphoreType.DMA((7 * n,)), pltpu.SemaphoreType.DMA((7 * n,)),
                        pltpu.SemaphoreType.DMA((n,))],
    )(*shards)


SMALL_ROWS = 16


def _small_allreduce(parts):
    n, _, w = parts.shape
    assert n <= SMALL_ROWS

    def body(p_ref, o_ref, gath, send_sems, recv_sems):
        x, y, c = lax.axis_index("x"), lax.axis_index("y"), lax.axis_index("c")
        my = _slot(x, y, c)
        rows = [jnp.sum(p_ref[i], axis=0, keepdims=True) for i in range(n)]
        rows.append(jnp.zeros((SMALL_ROWS - n, w), F32))
        gath[my] = jnp.concatenate(rows, axis=0)
        copies = []
        for k in range(1, N_DEV):
            px, py, pc = x ^ (k >> 2), y ^ ((k >> 1) & 1), c ^ (k & 1)
            cp = pltpu.make_async_remote_copy(
                src_ref=gath.at[my], dst_ref=gath.at[my], send_sem=send_sems.at[k - 1], recv_sem=recv_sems.at[k - 1],
                device_id=(px, py, pc), device_id_type=MESH)
            cp.start()
            copies.append(cp)
        for cp in copies:
            cp.wait()
        acc = gath[0]
        for d in range(1, N_DEV):
            acc = acc + gath[d]
        o_ref[...] = acc

    return pl.pallas_call(
        body, name="small_allreduce",
        in_specs=[VMEM_SPEC], out_specs=VMEM_SPEC,
        out_shape=jax.ShapeDtypeStruct((SMALL_ROWS, w), F32),
        scratch_shapes=[pltpu.VMEM((N_DEV, SMALL_ROWS, w), F32), pltpu.SemaphoreType.DMA((N_DEV - 1,)),
                        pltpu.SemaphoreType.DMA((N_DEV - 1,))],
    )(parts)


def _adamw(g, w, m, v, *, name, tm=256):
    r, c = g.shape
    tm = tm if r % tm == 0 else r
    bc1 = 1.0 - ADAM_B1 ** ADAM_STEP
    bc2 = 1.0 - ADAM_B2 ** ADAM_STEP

    def body(g_ref, w_ref, m_ref, v_ref, d_ref, nm_ref, nv_ref):
        gv = g_ref[...]
        nm = ADAM_B1 * m_ref[...] + (1.0 - ADAM_B1) * gv
        nv = ADAM_B2 * v_ref[...] + (1.0 - ADAM_B2) * jnp.square(gv)
        nm_ref[...] = nm
        nv_ref[...] = nv
        d_ref[...] = -ADAM_LR * ((nm / bc1) / (jnp.sqrt(nv / bc2) + ADAM_EPS) + ADAM_WD * w_ref[...])

    blk = pl.BlockSpec((tm, c), lambda i: (i, 0))
    shp = jax.ShapeDtypeStruct((r, c), F32)
    return pl.pallas_call(
        body, name=name, grid=(r // tm,), in_specs=[blk] * 4, out_specs=[blk] * 3, out_shape=[shp] * 3,
        compiler_params=_cp(("parallel",)),
    )(g, w, m, v)


def _transpose_cast(a, *, name):
    def body(a_ref, o_ref):
        o_ref[...] = a_ref[...].T.astype(BF16)

    return pl.pallas_call(body, name=name, out_shape=jax.ShapeDtypeStruct(a.shape[::-1], BF16),
                          compiler_params=_cp())(a)


def _adamw_sum_t(parts, w, m, v, *, name):
    bc1 = 1.0 - ADAM_B1 ** ADAM_STEP
    bc2 = 1.0 - ADAM_B2 ** ADAM_STEP

    def body(p_ref, w_ref, m_ref, v_ref, g_ref, d_ref, nm_ref, nv_ref):
        acc = p_ref[0].astype(F32)
        for k in range(1, N_DEV):
            acc = acc + p_ref[k].astype(F32)
        gv = acc.T
        g_ref[...] = gv
        nm = ADAM_B1 * m_ref[...] + (1.0 - ADAM_B1) * gv
        nv = ADAM_B2 * v_ref[...] + (1.0 - ADAM_B2) * jnp.square(gv)
        nm_ref[...] = nm
        nv_ref[...] = nv
        d_ref[...] = -ADAM_LR * ((nm / bc1) / (jnp.sqrt(nv / bc2) + ADAM_EPS) + ADAM_WD * w_ref[...])

    shp = jax.ShapeDtypeStruct(w.shape, F32)
    return pl.pallas_call(body, name=name, out_shape=[shp] * 4, compiler_params=_cp())(parts, w, m, v)


def _adamw_sum(parts, w, m, v, *, name, tm=256):
    _, r, c = parts.shape
    tm = tm if r % tm == 0 else r
    bc1 = 1.0 - ADAM_B1 ** ADAM_STEP
    bc2 = 1.0 - ADAM_B2 ** ADAM_STEP

    def body(p_ref, w_ref, m_ref, v_ref, g_ref, d_ref, nm_ref, nv_ref):
        gv = p_ref[0].astype(F32)
        for k in range(1, N_DEV):
            gv = gv + p_ref[k].astype(F32)
        g_ref[...] = gv
        nm = ADAM_B1 * m_ref[...] + (1.0 - ADAM_B1) * gv
        nv = ADAM_B2 * v_ref[...] + (1.0 - ADAM_B2) * jnp.square(gv)
        nm_ref[...] = nm
        nv_ref[...] = nv
        d_ref[...] = -ADAM_LR * ((nm / bc1) / (jnp.sqrt(nv / bc2) + ADAM_EPS) + ADAM_WD * w_ref[...])

    blk = pl.BlockSpec((tm, c), lambda i: (i, 0))
    shp = jax.ShapeDtypeStruct((r, c), F32)
    return pl.pallas_call(
        body, name=name, grid=(r // tm,), in_specs=[pl.BlockSpec((N_DEV, tm, c), lambda i: (0, i, 0))] + [blk] * 3,
        out_specs=[blk] * 4, out_shape=[shp] * 4, compiler_params=_cp(("parallel",)),
    )(parts, w, m, v)


BIG = ("w_in_0", "w_out_0", "w_up_0", "w_down_0", "pool_w_1", "w_up_1", "w_down_1")
SMALL = ("norm_mix_0", "norm_ffn_0", "norm_mix_1", "pool_scale_1", "norm_ffn_1", "final_norm", "b_f_0", "conv_w_0")
WEIGHTS = ("norm_mix_0", "w_in_0", "b_f_0", "conv_w_0", "w_out_0", "norm_ffn_0", "w_up_0", "w_down_0", "norm_mix_1",
           "pool_w_1", "pool_scale_1", "norm_ffn_1", "w_up_1", "w_down_1", "final_norm")


def _pad_to(a, rows, cols):
    return jnp.pad(a, ((0, rows - a.shape[0]), (0, cols - a.shape[1])))


def _pack_small(p, width):
    rows = [p[n].reshape(1, -1) for n in SMALL[:6]]
    rows.append(_pad_to(p["b_f_0"].reshape(1, -1), 1, width))
    rows.append(_pad_to(p["conv_w_0"], 3, width))
    return _pad_to(jnp.concatenate(rows, axis=0), SMALL_ROWS, width)


def _unpack_small(a, like):
    out = {n: a[i] for i, n in enumerate(SMALL[:6])}
    out["b_f_0"] = a[6, :like["b_f_0"].shape[0]]
    out["conv_w_0"] = a[7:10, :like["conv_w_0"].shape[1]]
    return out


def kernel(x, norm_mix_0, w_in_0, b_f_0, conv_w_0, w_out_0, norm_ffn_0, w_up_0, w_down_0, norm_mix_1, pool_w_1, pool_scale_1, norm_ffn_1, w_up_1, w_down_1, final_norm, loss_target, m_norm_mix_0, m_w_in_0, m_b_f_0, m_conv_w_0, m_w_out_0, m_norm_ffn_0, m_w_up_0, m_w_down_0, m_norm_mix_1, m_pool_w_1, m_pool_scale_1, m_norm_ffn_1, m_w_up_1, m_w_down_1, m_final_norm, v_norm_mix_0, v_w_in_0, v_b_f_0, v_conv_w_0, v_w_out_0, v_norm_ffn_0, v_w_up_0, v_w_down_0, v_norm_mix_1, v_pool_w_1, v_pool_scale_1, v_norm_ffn_1, v_w_up_1, v_w_down_1, v_final_norm):
    w = dict(norm_mix_0=norm_mix_0, w_in_0=w_in_0, b_f_0=b_f_0, conv_w_0=conv_w_0, w_out_0=w_out_0,
             norm_ffn_0=norm_ffn_0, w_up_0=w_up_0, w_down_0=w_down_0, norm_mix_1=norm_mix_1, pool_w_1=pool_w_1,
             pool_scale_1=pool_scale_1, norm_ffn_1=norm_ffn_1, w_up_1=w_up_1, w_down_1=w_down_1, final_norm=final_norm)
    m = dict(norm_mix_0=m_norm_mix_0, w_in_0=m_w_in_0, b_f_0=m_b_f_0, conv_w_0=m_conv_w_0, w_out_0=m_w_out_0,
             norm_ffn_0=m_norm_ffn_0, w_up_0=m_w_up_0, w_down_0=m_w_down_0, norm_mix_1=m_norm_mix_1,
             pool_w_1=m_pool_w_1, pool_scale_1=m_pool_scale_1, norm_ffn_1=m_norm_ffn_1, w_up_1=m_w_up_1,
             w_down_1=m_w_down_1, final_norm=m_final_norm)
    v = dict(norm_mix_0=v_norm_mix_0, w_in_0=v_w_in_0, b_f_0=v_b_f_0, conv_w_0=v_conv_w_0, w_out_0=v_w_out_0,
             norm_ffn_0=v_norm_ffn_0, w_up_0=v_w_up_0, w_down_0=v_w_down_0, norm_mix_1=v_norm_mix_1,
             pool_w_1=v_pool_w_1, pool_scale_1=v_pool_scale_1, norm_ffn_1=v_norm_ffn_1, w_up_1=v_w_up_1,
             w_down_1=v_w_down_1, final_norm=v_final_norm)
    d = x.shape[-1]
    n_in = w_in_0.shape[1] * N_DEV
    n_qkv = 3 * ATTN_W
    pool_g, pool_rows, pool_c = pool_w_1.shape

    def shard2d(p):
        return {n: (p[n].reshape(pool_g * pool_rows, pool_c) if n == "pool_w_1" else p[n]) for n in BIG}
    w2, m2, v2 = shard2d(w), shard2d(m), shard2d(v)

    conv_cols = conv_w_0.shape[1]
    win_g8, conv_g8 = _all_gather([_transpose_cast(w_in_0, name="w_in_t"), _pad_to(conv_w_0, 8, 128)])
    conv_full = conv_g8[:, :, :conv_cols].transpose(1, 0, 2).reshape(8, N_DEV * conv_cols)
    win_t = win_g8.reshape(n_in, d)
    win_pt = jnp.concatenate([win_t[:n_qkv], _pad_to(win_t[n_qkv:n_qkv + N_HEADS], F_PAD, d),
                              win_t[n_qkv + N_HEADS:]], axis=0)

    gains = dict(mix0=norm_mix_0.reshape(1, d), ffn0=norm_ffn_0.reshape(1, d), mix1=norm_mix_1.reshape(1, d),
                 ffn1=norm_ffn_1.reshape(1, d), final=final_norm.reshape(1, d))
    dev = _slot(lax.axis_index("x"), lax.axis_index("y"), lax.axis_index("c"))
    loss8, grad_x, landed, small = _local_step(
        x[0], loss_target[0], gains, _pad_to(b_f_0.reshape(1, -1), 1, F_PAD), conv_full, pool_scale_1.reshape(1, d),
        win_pt, {n: w2[n].astype(BF16) for n in LATE})
    parts = jnp.concatenate(
        [small[k][None] for k in ("mix0", "ffn0", "mix1", "pool_scale", "ffn1", "final")]
        + [_pad_to(small["b_f"], 8, d)[None], jnp.pad(small["conv_w"], ((0, 0), (0, 0), (0, d - CONV_CH))),
           _pad_to(loss8[0:1, 0:1], 8, d)[None]], axis=0)
    tot = _small_allreduce(parts)
    loss = tot[10, 0]
    conv_g = lax.dynamic_slice(tot, (7, dev * conv_cols), (3, conv_cols))
    gs = tot.at[7:10].set(_pad_to(conv_g, 3, d))

    grads, deltas, new_m, new_v = {}, {}, {}, {}
    for n in BIG:
        if n in LATE:
            gr, dl, nm, nv = _adamw_sum(landed[n], w2[n], m2[n], v2[n], name="adamw_" + n)
        else:
            gr, dl, nm, nv = _adamw_sum_t(landed[n], w2[n], m2[n], v2[n], name="adamw_" + n)
        for dst, val in ((grads, gr), (deltas, dl), (new_m, nm), (new_v, nv)):
            dst[n] = val.reshape(w[n].shape)
    dl, nm, nv = _adamw(gs, _pack_small(w, d), _pack_small(m, d), _pack_small(v, d), name="adamw_small")
    for dst, val in ((grads, gs), (deltas, dl), (new_m, nm), (new_v, nv)):
        dst.update(_unpack_small(val, w))
    return (loss, grad_x[None], *[grads[n] for n in WEIGHTS], *[deltas[n] for n in WEIGHTS],
            *[new_m[n] for n in WEIGHTS], *[new_v[n] for n in WEIGHTS])
```

```python
import functools

import jax
import jax.numpy as jnp
from jax import lax
from jax.experimental import pallas as pl
from jax.experimental.pallas import tpu as pltpu

F32 = jnp.float32
BF16 = jnp.bfloat16

N_DEV = 8
N_HEADS = 8
HEAD_DIM = 64
PAIR = 2 * HEAD_DIM
ATTN_W = N_HEADS * HEAD_DIM
CONV_CH = 512
F_PAD = 128
POOL_WINDOWS = (2, 4, 8, 16)
POOL_HALO = 16
CONV_HALO = 16
RMS_EPS = 1e-6
Q_SCALE = HEAD_DIM ** -0.5
LOG2E = 1.4426950408889634
NEG = -1e30
AUX_BIAS = 0
AUX_LSE = 3
AUX_ROWSUM = 6
ADAM_LR, ADAM_B1, ADAM_B2, ADAM_EPS, ADAM_WD, ADAM_STEP = 0.001, 0.9, 0.999, 1e-08, 0.01, 10
MESH = pl.DeviceIdType.MESH
VMEM_LIMIT = 56 * 2**20


def _cp(sem=None, vmem=VMEM_LIMIT, **kw):
    return pltpu.CompilerParams(dimension_semantics=sem, vmem_limit_bytes=vmem, **kw)


def _dot(a, b):
    return jnp.dot(a, b, preferred_element_type=F32)


def _dot_nt(a, b):
    return lax.dot_general(a, b, (((1,), (1,)), ((), ())), preferred_element_type=F32)


def _dot_tn(a, b):
    return lax.dot_general(a, b, (((0,), (0,)), ((), ())), preferred_element_type=F32)


def _rstd(h):
    return lax.rsqrt(jnp.mean(h * h, axis=-1, keepdims=True) + RMS_EPS)


def _rows8(x):
    r, n = x.shape
    return jnp.sum(x.reshape(r // 8, 8, n), axis=0)


def _norm_bwd(dn, h, g):
    r = _rstd(h)
    xhat = h * r
    dy = dn * g
    dh = r * (dy - xhat * jnp.mean(dy * xhat, axis=-1, keepdims=True))
    return dh, _rows8(dn * xhat)


def _const_spec(shape):
    nd = len(shape)
    return pl.BlockSpec(shape, lambda *_: (0,) * nd, pipeline_mode=pl.Buffered(1))


HBM_SPEC = pl.BlockSpec(memory_space=pltpu.HBM)
VMEM_SPEC = pl.BlockSpec(memory_space=pltpu.VMEM)


def _slot(px, py, pc):
    return 4 * px + 2 * py + pc


class _Exchange:
    def __init__(self, srcs, dsts, send_sems, recv_sems, local_sems, gather):
        x, y, c = lax.axis_index("x"), lax.axis_index("y"), lax.axis_index("c")
        me = _slot(x, y, c)
        self.copies = []
        for a, (src, dst) in enumerate(zip(srcs, dsts)):
            self.copies.append(pltpu.make_async_copy(src if gather else src.at[me], dst.at[me], local_sems.at[a]))
            for k in range(1, N_DEV):
                px, py, pc = x ^ (k >> 2), y ^ ((k >> 1) & 1), c ^ (k & 1)
                self.copies.append(pltpu.make_async_remote_copy(
                    src_ref=src if gather else src.at[_slot(px, py, pc)], dst_ref=dst.at[me],
                    send_sem=send_sems.at[(N_DEV - 1) * a + k - 1], recv_sem=recv_sems.at[(N_DEV - 1) * a + k - 1],
                    device_id=(px, py, pc), device_id_type=MESH))

    def start(self):
        for cp in self.copies:
            cp.start()

    def wait(self):
        for cp in self.copies:
            cp.wait()

    @staticmethod
    def scratch(n):
        return [pltpu.SemaphoreType.DMA(((N_DEV - 1) * n,)), pltpu.SemaphoreType.DMA(((N_DEV - 1) * n,)),
                pltpu.SemaphoreType.DMA((n,))]


def _norm_inproj(x, g, win_pt, conv_w, *, tm=512):
    t, d = x.shape
    n_all = win_pt.shape[0]
    n_qkv = 3 * ATTN_W
    n_bcx = 3 * CONV_CH
    assert n_all == n_qkv + F_PAD + n_bcx
    tm = min(tm, t)
    ch = CONV_CH

    def body(x_ref, g_ref, w_ref, cw_ref, n_ref, qkv_ref, f_ref, bcx_ref, cv_ref, ext):
        h = x_ref[...]
        n = (h * _rstd(h) * g_ref[...]).astype(BF16)
        n_ref[...] = n
        for c0 in range(0, n_qkv, 512):
            acc = _dot_nt(n, w_ref[c0:c0 + 512, :])
            if c0 < ATTN_W:
                acc = acc * (Q_SCALE * LOG2E)
            qkv_ref[:, c0:c0 + 512] = acc.astype(BF16)
        f_ref[...] = _dot_nt(n, w_ref[n_qkv:n_qkv + F_PAD, :])
        bcx = []
        for k in range(3):
            c0 = n_qkv + F_PAD + k * ch
            v = _dot_nt(n, w_ref[c0:c0 + ch, :]).astype(BF16)
            bcx_ref[:, k * ch:(k + 1) * ch] = v
            bcx.append(v.astype(F32))
        @pl.when(pl.program_id(0) == 0)
        def _():
            ext[tm:tm + CONV_HALO, :] = jnp.zeros((CONV_HALO, ch), F32)
        ext[0:CONV_HALO, :] = ext[tm:tm + CONV_HALO, :]
        ext[CONV_HALO:CONV_HALO + tm, :] = bcx[1] * bcx[2]
        conv = (cw_ref[0:1, :] * ext[CONV_HALO - 2:CONV_HALO - 2 + tm, :]
                + cw_ref[1:2, :] * ext[CONV_HALO - 1:CONV_HALO - 1 + tm, :]
                + cw_ref[2:3, :] * ext[CONV_HALO:CONV_HALO + tm, :])
        cv_ref[...] = (bcx[0] * conv).astype(BF16)

    return pl.pallas_call(
        body, name="norm_inproj", grid=(t // tm,),
        in_specs=[pl.BlockSpec((tm, d), lambda i: (i, 0)), _const_spec((1, d)), _const_spec((n_all, d)),
                  _const_spec((8, ch))],
        out_specs=[pl.BlockSpec((tm, d), lambda i: (i, 0)), pl.BlockSpec((tm, n_qkv), lambda i: (i, 0)),
                   pl.BlockSpec((tm, F_PAD), lambda i: (i, 0)), pl.BlockSpec((tm, n_bcx), lambda i: (i, 0)),
                   pl.BlockSpec((tm, ch), lambda i: (i, 0))],
        out_shape=[jax.ShapeDtypeStruct((t, d), BF16), jax.ShapeDtypeStruct((t, n_qkv), BF16),
                   jax.ShapeDtypeStruct((t, F_PAD), F32), jax.ShapeDtypeStruct((t, n_bcx), BF16),
                   jax.ShapeDtypeStruct((t, ch), BF16)],
        scratch_shapes=[pltpu.VMEM((CONV_HALO + tm, ch), F32)],
        compiler_params=_cp(("arbitrary",)),
    )(x, g, win_pt, conv_w)


def _head_lanes(h):
    lane = lax.broadcasted_iota(jnp.int32, (1, PAIR), 1)
    hh = h % 2
    return lane, lane // HEAD_DIM == hh, HEAD_DIM * (1 - hh)


def _pieces(col):
    hi = col.astype(BF16).astype(F32)
    r1 = col - hi
    mid = r1.astype(BF16).astype(F32)
    lo = (r1 - mid).astype(BF16).astype(F32)
    return hi, mid, lo


def _put_pieces(lane, first, col, other):
    hi, mid, lo = _pieces(col)
    return jnp.where(lane == first, hi, jnp.where(lane == first + 1, mid, jnp.where(lane == first + 2, lo, other)))


def _fgate_prep(flog, b_f, qkv, *, tm=512):
    t = flog.shape[0]
    tm = min(tm, t)

    def body(f_ref, b_ref, qkv_ref, qat_ref, ka_ref, va_ref, vat_ref, sg_ref, carry):
        @pl.when(pl.program_id(0) == 0)
        def _():
            carry[...] = jnp.zeros_like(carry)
        z = f_ref[...] + b_ref[...]
        e = jnp.exp(-jnp.abs(z))
        logf = jnp.minimum(z, 0.0) - jnp.log(1.0 + e)
        sg_ref[...] = jnp.where(z >= 0, e, 1.0) / (1.0 + e)
        r = lax.broadcasted_iota(jnp.int32, (tm, tm), 0)
        c = lax.broadcasted_iota(jnp.int32, (tm, tm), 1)
        tri = (c <= r).astype(F32)
        cs = jnp.dot(tri, logf, preferred_element_type=F32, precision=lax.Precision.HIGHEST) + carry[...]
        carry[...] = cs[tm - 1:tm, :]
        cs2 = cs * LOG2E
        for h in range(N_HEADS):
            lane, head, aux = _head_lanes(h)
            p0 = (h // 2) * PAIR
            ones = ((lane >= aux + AUX_LSE) & (lane <= aux + AUX_ROWSUM)).astype(F32)
            bias = (lane >= aux + AUX_BIAS) & (lane < aux + AUX_BIAS + 3)
            k_aux = _put_pieces(lane, aux + AUX_BIAS, cs2[:, h:h + 1], ones)
            q_aug = jnp.where(head, qkv_ref[:, p0:p0 + PAIR], jnp.where(bias, -1.0, 0.0).astype(BF16))
            v_aug = jnp.where(head, qkv_ref[:, 2 * ATTN_W + p0:2 * ATTN_W + p0 + PAIR],
                              jnp.where(bias, 1.0, 0.0).astype(BF16))
            qat_ref[h] = q_aug.T
            ka_ref[h] = jnp.where(head, qkv_ref[:, ATTN_W + p0:ATTN_W + p0 + PAIR], k_aux.astype(BF16))
            va_ref[h] = v_aug
            vat_ref[h] = v_aug.T

    aug = lambda: pl.BlockSpec((N_HEADS, tm, PAIR), lambda i: (0, i, 0))
    aug_t = lambda: pl.BlockSpec((N_HEADS, PAIR, tm), lambda i: (0, 0, i))
    aug_shape = jax.ShapeDtypeStruct((N_HEADS, t, PAIR), BF16)
    aug_t_shape = jax.ShapeDtypeStruct((N_HEADS, PAIR, t), BF16)
    return pl.pallas_call(
        body, name="fgate_prep", grid=(t // tm,),
        in_specs=[pl.BlockSpec((tm, F_PAD), lambda i: (i, 0)), _const_spec((1, F_PAD)),
                  pl.BlockSpec((tm, 3 * ATTN_W), lambda i: (i, 0))],
        out_specs=[aug_t(), aug(), aug(), aug_t(), pl.BlockSpec((tm, F_PAD), lambda i: (i, 0))],
        out_shape=[aug_t_shape, aug_shape, aug_shape, aug_t_shape, jax.ShapeDtypeStruct((t, F_PAD), F32)],
        scratch_shapes=[pltpu.VMEM((1, F_PAD), F32)],
        compiler_params=_cp(("arbitrary",)),
    )(flog, b_f, qkv)


def _put_pieces_t(row, first, vec, other):
    hi, mid, lo = _pieces(vec)
    return jnp.where(row == first, hi, jnp.where(row == first + 1, mid, jnp.where(row == first + 2, lo, other)))


def _attn_fwd(q_aug_t, k_aug, v_aug_t, shards, *, tq=1024):
    t = k_aug.shape[1]
    tq = min(tq, t)
    tk = tq // 2
    nq = t // tq
    n_pairs = ATTN_W // PAIR
    n_sh = len(shards)

    def body(qt_ref, k_ref, vt_ref, *rest):
        o_ref, qb_ref, qbt_ref = rest[n_sh:n_sh + 3]
        s_scr = rest[2 * n_sh + 3]
        gather = _Exchange(rest[:n_sh], rest[n_sh + 3:2 * n_sh + 3], *rest[2 * n_sh + 4:], gather=True)
        i = pl.program_id(1)

        @pl.when((pl.program_id(0) == 0) & (i == 0))
        def _():
            gather.start()
        key = lax.broadcasted_iota(jnp.int32, (tk, tq), 0)
        qry = lax.broadcasted_iota(jnp.int32, (tk, tq), 1)
        qt = [qt_ref[0], qt_ref[1]]

        def logits(hh, tile, slot, diag):
            s = _dot(k_ref[hh, pl.ds(pl.multiple_of(tile * tk, tk), tk), :], qt[hh])
            if diag:
                s = jnp.where(key + (tile * tk - i * tq) <= qry, s, NEG)
            s_scr[hh, slot] = s
            return jnp.max(s, axis=0, keepdims=True)

        def probs(hh, tile, slot, m, acc, tmax):
            mn = jnp.maximum(m, tmax)
            p = jnp.exp2(s_scr[hh, slot] - mn).astype(BF16)
            acc = jnp.exp2(m - mn) * acc + _dot(vt_ref[hh, :, pl.ds(pl.multiple_of(tile * tk, tk), tk)], p)
            return mn, acc

        def advance(carry, prev, slot, nxt, diag=False):
            out = []
            for hh in range(2):
                m, acc, tmax = carry[hh]
                m, acc = probs(hh, prev, slot, m, acc, tmax)
                out.append((m, acc, logits(hh, nxt, 1 - slot, diag)))
            return tuple(out)

        def two_tiles(jj, carry):
            carry = advance(carry, jnp.where(jj == 0, 2 * i, 2 * jj - 1), 1, 2 * jj)
            return advance(carry, 2 * jj, 0, 2 * jj + 1)

        init = tuple((jnp.full((1, tq), NEG, F32), jnp.zeros((PAIR, tq), F32), logits(hh, 2 * i + 1, 0, True))
                     for hh in range(2))
        carry = advance(init, 2 * i + 1, 0, 2 * i, diag=True)
        carry = lax.fori_loop(0, i // 2, lambda jj, c: two_tiles(2 * jj + 1, two_tiles(2 * jj, c)), carry)
        carry = lax.cond(i % 2 == 1, lambda c: two_tiles(i - 1, c), lambda c: c, carry)
        last = jnp.where(i == 0, 2 * i, 2 * i - 1)
        row = lax.broadcasted_iota(jnp.int32, (PAIR, 1), 0)
        res = []
        for hh in range(2):
            aux = HEAD_DIM * (1 - hh)
            m, acc, tmax = carry[hh]
            m, acc = probs(hh, last, 1, m, acc, tmax)
            l = acc[aux + AUX_BIAS:aux + AUX_BIAS + 1, :]
            qbt = _put_pieces_t(row, aux + AUX_LSE, -(m + jnp.log2(l)), qt[hh].astype(F32))
            qbt_ref[hh] = qbt.astype(BF16)
            qb_ref[hh] = qbt.astype(BF16).T
            res.append(acc * (1.0 / l))
        o_ref[...] = jnp.where(row < HEAD_DIM, res[0], res[1]).astype(BF16).T

        @pl.when((pl.program_id(0) == n_pairs - 1) & (i == nq - 1))
        def _():
            gather.wait()

    res = pl.pallas_call(
        body, name="attn_fwd", grid=(n_pairs, nq),
        in_specs=[pl.BlockSpec((2, PAIR, tq), lambda p, i: (p, 0, i)),
                  pl.BlockSpec((2, t, PAIR), lambda p, i: (p, 0, 0), pipeline_mode=pl.Buffered(1)),
                  pl.BlockSpec((2, PAIR, t), lambda p, i: (p, 0, 0), pipeline_mode=pl.Buffered(1))] + [HBM_SPEC] * n_sh,
        out_specs=[pl.BlockSpec((tq, PAIR), lambda p, i: (i, p)),
                   pl.BlockSpec((2, tq, PAIR), lambda p, i: (p, i, 0)),
                   pl.BlockSpec((2, PAIR, tq), lambda p, i: (p, 0, i))] + [HBM_SPEC] * n_sh,
        out_shape=[jax.ShapeDtypeStruct((t, ATTN_W), BF16), jax.ShapeDtypeStruct((N_HEADS, t, PAIR), BF16),
                   jax.ShapeDtypeStruct((N_HEADS, PAIR, t), BF16)]
        + [jax.ShapeDtypeStruct((N_DEV,) + s.shape, s.dtype) for s in shards],
        scratch_shapes=[pltpu.VMEM((2, 2, tk, tq), F32)] + _Exchange.scratch(n_sh),
        compiler_params=_cp(("arbitrary", "arbitrary")),
    )(q_aug_t, k_aug, v_aug_t, *shards)
    return res[0], res[1], res[2], res[3:]


def _prev_halo(tm, halo):
    return lambda i: (jnp.maximum(i * (tm // halo) - 1, 0), 0)


def _next_halo(tm, halo, t):
    return lambda i: (jnp.minimum((i + 1) * (tm // halo), t // halo - 1), 0)


def _mlp_tile(hh, g_ref, wu_ref, wd_ref, n_ref, a_ref, z_ref):
    n_blk, _, fb = wu_ref.shape
    n = (hh * _rstd(hh) * g_ref[...]).astype(BF16)
    n_ref[...] = n
    acc = hh
    for k in range(n_blk):
        a = _dot(n, wu_ref[k])
        zz = jnp.square(jnp.maximum(a, 0.0)).astype(BF16)
        a_ref[:, k * fb:(k + 1) * fb] = a.astype(BF16)
        z_ref[:, k * fb:(k + 1) * fb] = zz
        acc = acc + _dot(zz, wd_ref[k * fb:(k + 1) * fb, :])
    return acc


def _outproj(att, cv, x, wout, *, tm=512):
    t, d = x.shape
    tm = min(tm, t)

    def body(a_ref, c_ref, x_ref, w_ref, h_ref):
        h_ref[...] = x_ref[...] + _dot(a_ref[...], w_ref[0:ATTN_W, :]) + _dot(c_ref[...], w_ref[ATTN_W:, :])

    return pl.pallas_call(
        body, name="outproj", grid=(t // tm,),
        in_specs=[pl.BlockSpec((tm, ATTN_W), lambda i: (i, 0)), pl.BlockSpec((tm, CONV_CH), lambda i: (i, 0)),
                  pl.BlockSpec((tm, d), lambda i: (i, 0)), _const_spec(wout.shape)],
        out_specs=pl.BlockSpec((tm, d), lambda i: (i, 0)),
        out_shape=jax.ShapeDtypeStruct((t, d), F32),
        compiler_params=_cp(("parallel",)),
    )(att, cv, x, wout)


def _mlp_fwd(h, g, wup, wdown, *, name, tm=512):
    t, d = h.shape
    n_blk, _, fb = wup.shape
    f = n_blk * fb
    tm = min(tm, t)

    def body(h_ref, g_ref, wu_ref, wd_ref, ho_ref, n_ref, a_ref, z_ref):
        ho_ref[...] = _mlp_tile(h_ref[...], g_ref, wu_ref, wd_ref, n_ref, a_ref, z_ref)

    row = lambda n_: pl.BlockSpec((tm, n_), lambda i: (i, 0))
    return pl.pallas_call(
        body, name=name, grid=(t // tm,),
        in_specs=[row(d), _const_spec((1, d)), _const_spec(wup.shape), _const_spec(wdown.shape)],
        out_specs=[row(d), row(d), row(f), row(f)],
        out_shape=[jax.ShapeDtypeStruct((t, d), F32), jax.ShapeDtypeStruct((t, d), BF16),
                   jax.ShapeDtypeStruct((t, f), BF16), jax.ShapeDtypeStruct((t, f), BF16)],
        compiler_params=_cp(("parallel",)),
    )(h, g, wup, wdown)


def _mlp_fwd_loss(h, g, wup, wdown, g_out, target, *, name, tm=512):
    t, d = h.shape
    n_blk, _, fb = wup.shape
    f = n_blk * fb
    tm = min(tm, t)
    nsteps = t // tm

    def body(h_ref, g_ref, wu_ref, wd_ref, go_ref, y_ref, loss_ref, dh_ref, dg_ref, n_ref, a_ref, z_ref, lacc):
        i = pl.program_id(0)

        @pl.when(i == 0)
        def _():
            lacc[...] = jnp.zeros_like(lacc)
            dg_ref[...] = jnp.zeros_like(dg_ref)
        hv = _mlp_tile(h_ref[...], g_ref, wu_ref, wd_ref, n_ref, a_ref, z_ref)
        gv = go_ref[...]
        r = _rstd(hv)
        xhat = hv * r
        err = xhat * gv - y_ref[...]
        lacc[...] += _rows8(err * err)
        dout = err * (1.0 / d)
        dy = dout * gv
        dg_ref[...] += _rows8(dout * xhat)
        dh_ref[...] = r * (dy - xhat * jnp.mean(dy * xhat, axis=-1, keepdims=True))

        @pl.when(i == nsteps - 1)
        def _():
            loss_ref[...] = jnp.full(loss_ref.shape, (0.5 / d) * jnp.sum(lacc[...]), F32)

    row = lambda n_: pl.BlockSpec((tm, n_), lambda i: (i, 0))
    return pl.pallas_call(
        body, name=name, grid=(nsteps,),
        in_specs=[row(d), _const_spec((1, d)), _const_spec(wup.shape), _const_spec(wdown.shape), _const_spec((1, d)),
                  row(d)],
        out_specs=[pl.BlockSpec((8, 128), lambda i: (0, 0)), row(d), pl.BlockSpec((8, d), lambda i: (0, 0)),
                   row(d), row(f), row(f)],
        out_shape=[jax.ShapeDtypeStruct((8, 128), F32), jax.ShapeDtypeStruct((t, d), F32),
                   jax.ShapeDtypeStruct((8, d), F32), jax.ShapeDtypeStruct((t, d), BF16),
                   jax.ShapeDtypeStruct((t, f), BF16), jax.ShapeDtypeStruct((t, f), BF16)],
        scratch_shapes=[pltpu.VMEM((8, d), F32)],
        compiler_params=_cp(("arbitrary",)),
    )(h, g, wup, wdown, g_out, target)


def _pool_inv_count(i, tm):
    tglob = (i * tm + lax.broadcasted_iota(jnp.int32, (tm, 1), 0) + 1).astype(F32)
    return [1.0 / jnp.minimum(tglob, float(w)) for w in POOL_WINDOWS]


def _pool_fwd(h, g, poolw, scale, *, tm=512):
    t, d = h.shape
    tm = min(tm, t)
    cg = d // len(POOL_WINDOWS)

    def body(h_ref, hh_ref, g_ref, w_ref, s_ref, ho_ref, p_ref, ext):
        i = pl.program_id(0)
        hv = h_ref[...]
        halo = hh_ref[...]
        n = hv * _rstd(hv) * g_ref[...]
        ext[0:POOL_HALO, :] = jnp.where(i == 0, 0.0, halo * _rstd(halo) * g_ref[...])
        ext[POOL_HALO:POOL_HALO + tm, :] = n
        inv = _pool_inv_count(i, tm)
        for gi, w in enumerate(POOL_WINDOWS):
            cs = slice(gi * cg, (gi + 1) * cg)
            s = ext[POOL_HALO:POOL_HALO + tm, cs]
            for j in range(1, w):
                s = s + ext[POOL_HALO - j:POOL_HALO - j + tm, cs]
            pooled = (s * inv[gi] - n[:, cs]).astype(BF16)
            p_ref[:, cs] = pooled
            ho_ref[:, cs] = hv[:, cs] + _dot(pooled, w_ref[gi]) * s_ref[:, cs]

    row = lambda: pl.BlockSpec((tm, d), lambda i: (i, 0))
    return pl.pallas_call(
        body, name="pool_fwd", grid=(t // tm,),
        in_specs=[row(), pl.BlockSpec((POOL_HALO, d), _prev_halo(tm, POOL_HALO)), _const_spec((1, d)),
                  _const_spec(poolw.shape), _const_spec((1, d))],
        out_specs=[row(), row()],
        out_shape=[jax.ShapeDtypeStruct((t, d), F32), jax.ShapeDtypeStruct((t, d), BF16)],
        scratch_shapes=[pltpu.VMEM((POOL_HALO + tm, d), F32)],
        compiler_params=_cp(("parallel",)),
    )(h, h, g, poolw, scale)


def _mm_tn(a, b, *, name, ta, tb, tt, blocked_out=False, out_dtype=F32):
    t, ka = a.shape
    n = b.shape[1]
    ta, tb, tt = min(ta, ka), min(tb, n), min(tt, t)
    nt = t // tt

    def body(a_ref, b_ref, o_ref, acc):
        @pl.when(pl.program_id(2) == 0)
        def _():
            acc[...] = jnp.zeros_like(acc)
        acc[...] += _dot_tn(a_ref[...].astype(BF16), b_ref[...].astype(BF16))

        @pl.when(pl.program_id(2) == nt - 1)
        def _():
            o_ref[...] = acc[...].astype(out_dtype)

    if blocked_out:
        assert ta == ka
        out_shape = jax.ShapeDtypeStruct((n // tb, ka, tb), out_dtype)
        out_spec = pl.BlockSpec((None, ta, tb), lambda i, j, k: (j, i, 0))
    else:
        out_shape = jax.ShapeDtypeStruct((ka, n), out_dtype)
        out_spec = pl.BlockSpec((ta, tb), lambda i, j, k: (i, j))
    return pl.pallas_call(
        body, name=name, grid=(ka // ta, n // tb, nt),
        in_specs=[pl.BlockSpec((tt, ta), lambda i, j, k: (k, i)), pl.BlockSpec((tt, tb), lambda i, j, k: (k, j))],
        out_specs=out_spec, out_shape=out_shape, scratch_shapes=[pltpu.VMEM((ta, tb), F32)],
        compiler_params=_cp(("parallel", "parallel", "arbitrary")),
    )(a, b)


def _mm_tn_cat(a_list, b_list, *, name, tt, out_dtype=BF16):
    t = a_list[0].shape[0]
    ta, tb = a_list[0].shape[1], b_list[0].shape[1]
    na, nb = len(a_list), len(b_list)
    tt = min(tt, t)
    nt = t // tt

    def body(*refs):
        a_refs, b_refs, o_ref, acc = refs[:na], refs[na:na + nb], refs[na + nb], refs[na + nb + 1]
        i, j, k = pl.program_id(0), pl.program_id(1), pl.program_id(2)

        @pl.when(k == 0)
        def _():
            acc[...] = jnp.zeros_like(acc)
        for ia in range(na):
            for ib in range(nb):
                @pl.when((i == ia) & (j == ib))
                def _(ia=ia, ib=ib):
                    acc[...] += _dot_tn(a_refs[ia][...].astype(BF16), b_refs[ib][...].astype(BF16))

        @pl.when(k == nt - 1)
        def _():
            o_ref[...] = acc[...].astype(out_dtype)

    def held(m, axis):
        def index(i, j, k):
            cur = (i, j)[axis]
            return (jnp.where(cur == m, k, jnp.where(cur < m, 0, nt - 1)), 0)
        return index

    return pl.pallas_call(
        body, name=name, grid=(na, nb, nt),
        in_specs=[pl.BlockSpec((tt, ta), held(m, 0)) for m in range(na)]
        + [pl.BlockSpec((tt, tb), held(m, 1)) for m in range(nb)],
        out_specs=pl.BlockSpec((ta, tb), lambda i, j, k: (i, j)),
        out_shape=jax.ShapeDtypeStruct((na * ta, nb * tb), out_dtype), scratch_shapes=[pltpu.VMEM((ta, tb), F32)],
        compiler_params=_cp(("arbitrary", "arbitrary", "arbitrary")),
    )(*a_list, *b_list)


def _mlp_bwd(dho, h, a, g, wup, wdown, *, name, tm=512):
    t, d = h.shape
    n_blk, _, fb = wup.shape
    f = n_blk * fb
    tm = min(tm, t)

    def body(do_ref, h_ref, a_ref, g_ref, wu_ref, wd_ref, dh_ref, da_ref, dg_ref):
        @pl.when(pl.program_id(0) == 0)
        def _():
            dg_ref[...] = jnp.zeros_like(dg_ref)
        dho_v = do_ref[...]
        dob = dho_v.astype(BF16)
        dn = jnp.zeros((tm, d), F32)
        for k in range(n_blk):
            dz = _dot_nt(dob, wd_ref[k * fb:(k + 1) * fb, :])
            da = (dz * (2.0 * jnp.maximum(a_ref[:, k * fb:(k + 1) * fb].astype(F32), 0.0))).astype(BF16)
            da_ref[:, k * fb:(k + 1) * fb] = da
            dn = dn + _dot_nt(da, wu_ref[k])
        dh, dg = _norm_bwd(dn, h_ref[...], g_ref[...])
        dh_ref[...] = dho_v + dh
        dg_ref[...] += dg

    row = lambda n_: pl.BlockSpec((tm, n_), lambda i: (i, 0))
    return pl.pallas_call(
        body, name=name, grid=(t // tm,),
        in_specs=[row(d), row(d), row(f), _const_spec((1, d)), _const_spec(wup.shape), _const_spec(wdown.shape)],
        out_specs=[row(d), row(f), pl.BlockSpec((8, d), lambda i: (0, 0))],
        out_shape=[jax.ShapeDtypeStruct((t, d), F32), jax.ShapeDtypeStruct((t, f), BF16),
                   jax.ShapeDtypeStruct((8, d), F32)],
        compiler_params=_cp(("arbitrary",)),
    )(dho, h, a, g, wup, wdown)


def _pool_bwd(dho, h, pooled, g, poolw, scale, *, tm=512):
    t, d = h.shape
    tm = min(tm, t)
    ng = len(POOL_WINDOWS)
    cg = d // ng
    nsteps = t // tm

    def body(do_ref, dn_ref, h_ref, p_ref, g_ref, w_ref, s_ref, dh_ref, dw_ref, ds_ref, dg_ref, ext):
        i = pl.program_id(0)

        @pl.when(i == 0)
        def _():
            dw_ref[...] = jnp.zeros_like(dw_ref)
            ds_ref[...] = jnp.zeros_like(ds_ref)
            dg_ref[...] = jnp.zeros_like(dg_ref)
        dho_v = do_ref[...]
        sv = s_ref[...]
        dyp = (dho_v * sv).astype(BF16)
        dyp_halo = (dn_ref[...] * sv).astype(BF16)
        inv = _pool_inv_count(i, tm)
        tnext = ((i + 1) * tm + lax.broadcasted_iota(jnp.int32, (POOL_HALO, 1), 0) + 1).astype(F32)
        last = i == nsteps - 1
        ypre_parts, dpooled_parts = [], []
        for gi, w in enumerate(POOL_WINDOWS):
            cs = slice(gi * cg, (gi + 1) * cg)
            pg = p_ref[:, cs]
            ypre_parts.append(_dot(pg, w_ref[gi]))
            dw_ref[gi] += _dot_tn(pg, dyp[:, cs])
            dpool = _dot_nt(dyp[:, cs], w_ref[gi])
            dpooled_parts.append(dpool)
            ext[0:tm, cs] = dpool * inv[gi]
            dpool_halo = _dot_nt(dyp_halo[:, cs], w_ref[gi]) * (1.0 / jnp.minimum(tnext, float(w)))
            ext[tm:tm + POOL_HALO, cs] = jnp.where(last, 0.0, dpool_halo)
        ds_ref[...] += _rows8(dho_v * jnp.concatenate(ypre_parts, axis=1))
        dn_parts = []
        for gi, w in enumerate(POOL_WINDOWS):
            cs = slice(gi * cg, (gi + 1) * cg)
            s = ext[0:tm, cs]
            for j in range(1, w):
                s = s + ext[j:j + tm, cs]
            dn_parts.append(s - dpooled_parts[gi])
        dh, dg = _norm_bwd(jnp.concatenate(dn_parts, axis=1), h_ref[...], g_ref[...])
        dh_ref[...] = dho_v + dh
        dg_ref[...] += dg

    row = lambda: pl.BlockSpec((tm, d), lambda i: (i, 0))
    acc8 = lambda: pl.BlockSpec((8, d), lambda i: (0, 0))
    return pl.pallas_call(
        body, name="pool_bwd", grid=(nsteps,),
        in_specs=[row(), pl.BlockSpec((POOL_HALO, d), _next_halo(tm, POOL_HALO, t)), row(), row(),
                  _const_spec((1, d)), _const_spec(poolw.shape), _const_spec((1, d))],
        out_specs=[row(), pl.BlockSpec((ng, cg, cg), lambda i: (0, 0, 0)), acc8(), acc8()],
        out_shape=[jax.ShapeDtypeStruct((t, d), F32), jax.ShapeDtypeStruct((ng, cg, cg), F32),
                   jax.ShapeDtypeStruct((8, d), F32), jax.ShapeDtypeStruct((8, d), F32)],
        scratch_shapes=[pltpu.VMEM((tm + POOL_HALO, d), F32)],
        compiler_params=_cp(("arbitrary",)),
    )(dho, dho, h, pooled, g, poolw, scale)


def _outproj_conv_bwd(dh, o, wout, bcx, conv_w, *, tm=512):
    t, d = dh.shape
    tm = min(tm, t)
    ch = CONV_CH
    nsteps = t // tm

    def body(dh_ref, o_ref, w_ref, b_ref, c_ref, x_ref, hc_ref, hx_ref, cw_ref,
             da_ref, dat_ref, db_ref, dw_ref, ext_u, ext_d):
        s = pl.program_id(0)

        @pl.when(s == 0)
        def _():
            dw_ref[...] = jnp.zeros_like(dw_ref)
            ext_d[0:CONV_HALO, :] = jnp.zeros((CONV_HALO, ch), F32)
        dhb = dh_ref[...].astype(BF16)
        for p in range(ATTN_W // PAIR):
            datt = _dot_nt(dhb, w_ref[p * PAIR:(p + 1) * PAIR, :])
            prod = datt * o_ref[:, p * PAIR:(p + 1) * PAIR].astype(F32)
            for hh in range(2):
                lane, head, aux = _head_lanes(hh)
                delta = jnp.sum(jnp.where(head, prod, 0.0), axis=1, keepdims=True)
                aug = _put_pieces(lane, aux + AUX_BIAS, -delta, jnp.where(head, datt, 0.0))
                da_ref[2 * p + hh] = aug.astype(BF16)
                dat_ref[2 * p + hh] = aug.astype(BF16).T
        dcv = _dot_nt(dhb, w_ref[ATTN_W:, :])
        b, c, x = b_ref[...].astype(F32), c_ref[...].astype(F32), x_ref[...].astype(F32)
        ext_u[0:CONV_HALO, :] = jnp.where(s == nsteps - 1, 0.0, hc_ref[...].astype(F32) * hx_ref[...].astype(F32))
        ext_u[CONV_HALO:CONV_HALO + tm, :] = c * x
        dconv = dcv * b
        ext_d[tm:tm + CONV_HALO, :] = ext_d[0:CONV_HALO, :]
        ext_d[0:tm, :] = dconv
        u = [ext_u[CONV_HALO - 2 + k:CONV_HALO - 2 + k + tm, :] for k in range(3)]
        conv = cw_ref[0:1, :] * u[0] + cw_ref[1:2, :] * u[1] + cw_ref[2:3, :] * u[2]
        du = (cw_ref[2:3, :] * dconv + cw_ref[1:2, :] * ext_d[1:1 + tm, :] + cw_ref[0:1, :] * ext_d[2:2 + tm, :])
        db_ref[:, 0:ch] = (dcv * conv).astype(BF16)
        db_ref[:, ch:2 * ch] = (du * x).astype(BF16)
        db_ref[:, 2 * ch:3 * ch] = (du * c).astype(BF16)
        for k in range(3):
            dw_ref[k] += _rows8(dconv * u[k])

    rev = lambda s: nsteps - 1 - s
    row = lambda n_: pl.BlockSpec((tm, n_), lambda s: (rev(s), 0))
    col = lambda k: pl.BlockSpec((tm, ch), lambda s: (rev(s), k))
    prev = lambda k: pl.BlockSpec((CONV_HALO, ch), lambda s: (_prev_halo(tm, CONV_HALO)(rev(s))[0], k))
    return pl.pallas_call(
        body, name="outproj_conv_bwd", grid=(nsteps,),
        in_specs=[row(d), row(ATTN_W), _const_spec(wout.shape), col(0), col(1), col(2), prev(1), prev(2),
                  _const_spec((8, ch))],
        out_specs=[pl.BlockSpec((N_HEADS, tm, PAIR), lambda s: (0, rev(s), 0)),
                   pl.BlockSpec((N_HEADS, PAIR, tm), lambda s: (0, 0, rev(s))),
                   row(3 * ch), pl.BlockSpec((3, 8, ch), lambda s: (0, 0, 0))],
        out_shape=[jax.ShapeDtypeStruct((N_HEADS, t, PAIR), BF16), jax.ShapeDtypeStruct((N_HEADS, PAIR, t), BF16),
                   jax.ShapeDtypeStruct((t, 3 * ch), BF16), jax.ShapeDtypeStruct((3, 8, ch), F32)],
        scratch_shapes=[pltpu.VMEM((CONV_HALO + tm, ch), F32), pltpu.VMEM((tm + CONV_HALO, ch), F32)],
        compiler_params=_cp(("arbitrary",)),
    )(dh, o, wout, bcx, bcx, bcx, bcx, bcx, conv_w)


def _attn_bwd(q_bwd, do_aug, q_bwd_t, do_aug_t, k_aug, v_aug, gblocks, *, tq=1024):
    t = q_bwd.shape[1]
    tq = min(tq, t)
    tk = tq // 2
    nq, nk = t // tq, t // tk
    n_pairs = ATTN_W // PAIR
    n_g = len(gblocks)

    def body(q_ref, do_ref, qt_ref, dot_ref, k_ref, v_ref, *rest):
        dq_ref, dqx_ref, dk_ref, dkx_ref, dv_ref = rest[n_g:n_g + 5]
        dq_scr = rest[2 * n_g + 5]
        scatter = _Exchange(rest[:n_g], rest[n_g + 5:2 * n_g + 5], *rest[2 * n_g + 6:], gather=False)
        j = pl.program_id(1)

        @pl.when((pl.program_id(0) == 0) & (j == 0))
        def _():
            scatter.start()

        @pl.when(j == 0)
        def _():
            dq_scr[...] = jnp.zeros_like(dq_scr)
        k = [k_ref[0], k_ref[1]]
        v = [v_ref[0], v_ref[1]]

        def step(i, carry, diag, rows=tq, row0=0):
            qs = pl.multiple_of(i * tq + row0, tk)
            if diag:
                row = lax.broadcasted_iota(jnp.int32, (rows, tk), 0)
                col = lax.broadcasted_iota(jnp.int32, (rows, tk), 1)
            out = []
            for hh in range(2):
                dk_a, dv_a = carry[hh]
                q = q_ref[hh, pl.ds(qs, rows), :]
                dov = do_ref[hh, pl.ds(qs, rows), :]
                p = jnp.exp2(_dot_nt(q, k[hh]))
                if diag:
                    p = jnp.where(col + (j * tk - i * tq - row0) <= row, p, 0.0)
                ds = (p * _dot_nt(dov, v[hh])).astype(BF16)
                dv_a = dv_a + _dot(dot_ref[hh, :, pl.ds(qs, rows)], p.astype(BF16))
                dk_a = dk_a + _dot(qt_ref[hh, :, pl.ds(qs, rows)], ds)
                dq_scr[hh, pl.ds(qs, rows), :] += _dot(ds, k[hh])
                out.append((dk_a, dv_a))
            return tuple(out)

        zero = (jnp.zeros((PAIR, tk), F32), jnp.zeros((PAIR, tk), F32))
        carry = lax.cond(j % 2 == 0, lambda c: step(j // 2, c, True),
                         lambda c: step(j // 2, c, True, rows=tk, row0=tk), (zero, zero))
        full0 = j // 2 + 1
        odd = (nq - full0) % 2
        carry = lax.cond(odd == 1, lambda c: step(full0, c, False), lambda c: c, carry)
        (dk0, dv0), (dk1, dv1) = lax.fori_loop(
            0, (nq - full0) // 2, lambda ii, c: step(full0 + odd + 2 * ii, c, False, rows=2 * tq), carry)
        first_t = lax.broadcasted_iota(jnp.int32, (PAIR, 1), 0) < HEAD_DIM
        first = lax.broadcasted_iota(jnp.int32, (1, PAIR), 1) < HEAD_DIM
        dk_ref[...] = (jnp.where(first_t, dk0, dk1) * (1.0 / LOG2E)).astype(BF16).T
        dkx_ref[...] = jnp.where(first_t, dk1, dk0).T
        dv_ref[...] = jnp.where(first_t, dv0, dv1).astype(BF16).T

        @pl.when(j == nk - 1)
        def _():
            dq_ref[...] = (jnp.where(first, dq_scr[0], dq_scr[1]) * Q_SCALE).astype(BF16)
            dqx_ref[...] = jnp.where(first, dq_scr[1], dq_scr[0])

        @pl.when((pl.program_id(0) == n_pairs - 1) & (j == nk - 1))
        def _():
            scatter.wait()

    resident = lambda: pl.BlockSpec((2, t, PAIR), lambda p, j: (p, 0, 0), pipeline_mode=pl.Buffered(1))
    resident_t = lambda: pl.BlockSpec((2, PAIR, t), lambda p, j: (p, 0, 0), pipeline_mode=pl.Buffered(1))
    kv_in = lambda: pl.BlockSpec((2, tk, PAIR), lambda p, j: (p, j, 0))
    whole = lambda: pl.BlockSpec((t, PAIR), lambda p, j: (0, p))
    tile = lambda: pl.BlockSpec((tk, PAIR), lambda p, j: (j, p))
    b16 = jax.ShapeDtypeStruct((t, ATTN_W), BF16)
    f32 = jax.ShapeDtypeStruct((t, ATTN_W), F32)
    res = pl.pallas_call(
        body, name="attn_bwd", grid=(n_pairs, nk),
        in_specs=[resident(), resident(), resident_t(), resident_t(), kv_in(), kv_in()] + [HBM_SPEC] * n_g,
        out_specs=[whole(), whole(), tile(), tile(), tile()] + [HBM_SPEC] * n_g,
        out_shape=[b16, f32, b16, f32, b16] + [jax.ShapeDtypeStruct(g.shape, g.dtype) for g in gblocks],
        scratch_shapes=[pltpu.VMEM((2, t, PAIR), F32)] + _Exchange.scratch(n_g),
        compiler_params=_cp(("arbitrary", "arbitrary")),
    )(q_bwd, do_aug, q_bwd_t, do_aug_t, k_aug, v_aug, *gblocks)
    return res[:5], res[5:]


def _fgate_bwd(dqx, dkx, sgate, *, tm=256):
    t = sgate.shape[0]
    tm = min(tm, t)
    nsteps = t // tm

    def body(dq_ref, dk_ref, sg_ref, df_ref, dbf_ref, carry):
        @pl.when(pl.program_id(0) == 0)
        def _():
            carry[...] = jnp.zeros_like(carry)
            dbf_ref[...] = jnp.zeros_like(dbf_ref)
        lane = lax.broadcasted_iota(jnp.int32, (ATTN_W, F_PAD), 0)
        head = lax.broadcasted_iota(jnp.int32, (ATTN_W, F_PAD), 1)
        aux = (head // 2) * PAIR + HEAD_DIM * (1 - head % 2)
        valid = head < N_HEADS
        pick_r = (valid & (lane == aux + AUX_ROWSUM)).astype(F32)
        pick_c = (valid & (lane == aux + AUX_BIAS)).astype(F32)
        hp = lax.Precision.HIGHEST
        dcum = (jnp.dot(dq_ref[...], pick_r, preferred_element_type=F32, precision=lax.Precision.HIGH)
                + jnp.dot(dk_ref[...], pick_c, preferred_element_type=F32, precision=lax.Precision.HIGH))
        r = lax.broadcasted_iota(jnp.int32, (tm, tm), 0)
        c = lax.broadcasted_iota(jnp.int32, (tm, tm), 1)
        tri = (c >= r).astype(F32)
        rc = jnp.dot(tri, dcum, preferred_element_type=F32, precision=hp) + carry[...]
        carry[...] = rc[0:1, :]
        df = rc * sg_ref[...]
        df_ref[...] = df.astype(BF16)
        dbf_ref[...] += _rows8(df)

    rev = lambda i: nsteps - 1 - i
    return pl.pallas_call(
        body, name="fgate_bwd", grid=(nsteps,),
        in_specs=[pl.BlockSpec((tm, ATTN_W), lambda i: (rev(i), 0)), pl.BlockSpec((tm, ATTN_W), lambda i: (rev(i), 0)),
                  pl.BlockSpec((tm, F_PAD), lambda i: (rev(i), 0))],
        out_specs=[pl.BlockSpec((tm, F_PAD), lambda i: (rev(i), 0)), pl.BlockSpec((8, F_PAD), lambda i: (0, 0))],
        out_shape=[jax.ShapeDtypeStruct((t, F_PAD), BF16), jax.ShapeDtypeStruct((8, F_PAD), F32)],
        scratch_shapes=[pltpu.VMEM((1, F_PAD), F32)],
        compiler_params=_cp(("arbitrary",)),
    )(dqx, dkx, sgate)


def _inproj_bwd(dq, dk, dv, df, dbcx, dh, x, g, win_pt, gblock, *, tm=512):
    t, d = x.shape
    tm = min(tm, t)
    nsteps = t // tm
    n_qkv = 3 * ATTN_W

    def body(dq_ref, dk_ref, dv_ref, df_ref, db_ref, dh_ref, x_ref, g_ref, w_ref, gb_ref, gx_ref, dg_ref, land_ref,
             *sems):
        scatter = _Exchange([gb_ref], [land_ref], *sems, gather=False)

        @pl.when(pl.program_id(0) == 0)
        def _():
            scatter.start()
            dg_ref[...] = jnp.zeros_like(dg_ref)
        dn = _dot(df_ref[...], w_ref[n_qkv:n_qkv + F_PAD, :])
        for k, r in enumerate((dq_ref, dk_ref, dv_ref)):
            dn = dn + _dot(r[...], w_ref[k * ATTN_W:(k + 1) * ATTN_W, :])
        for k in range(3):
            c0 = n_qkv + F_PAD + k * CONV_CH
            dn = dn + _dot(db_ref[:, k * CONV_CH:(k + 1) * CONV_CH], w_ref[c0:c0 + CONV_CH, :])
        dx, dg = _norm_bwd(dn, x_ref[...], g_ref[...])
        gx_ref[...] = dh_ref[...] + dx
        dg_ref[...] += dg

        @pl.when(pl.program_id(0) == nsteps - 1)
        def _():
            scatter.wait()

    row = lambda n_: pl.BlockSpec((tm, n_), lambda i: (i, 0))
    return pl.pallas_call(
        body, name="inproj_bwd", grid=(nsteps,),
        in_specs=[row(ATTN_W), row(ATTN_W), row(ATTN_W), row(F_PAD), row(3 * CONV_CH), row(d), row(d),
                  _const_spec((1, d)), _const_spec(win_pt.shape), HBM_SPEC],
        out_specs=[row(d), pl.BlockSpec((8, d), lambda i: (0, 0)), HBM_SPEC],
        out_shape=[jax.ShapeDtypeStruct((t, d), F32), jax.ShapeDtypeStruct((8, d), F32),
                   jax.ShapeDtypeStruct(gblock.shape, gblock.dtype)],
        scratch_shapes=_Exchange.scratch(1),
        compiler_params=_cp(("arbitrary",)),
    )(dq, dk, dv, df, dbcx, dh, x, g, win_pt, gblock)


LATE = ("w_out_0", "w_up_0", "w_down_0", "pool_w_1", "w_up_1", "w_down_1")


def _local_step(x, target, gains, b_f, conv_w, pool_scale, win_pt, shards):
    d = x.shape[1]
    n0, qkv, flog, bcx, cv = _norm_inproj(x, gains["mix0"], win_pt, conv_w)
    q_aug_t, k_aug, v_aug, v_aug_t, sgate = _fgate_prep(flog, b_f, qkv)
    att, q_bwd, q_bwd_t, gathered = _attn_fwd(q_aug_t, k_aug, v_aug_t, [shards[n] for n in LATE])
    g = dict(zip(LATE, gathered))
    wout = g["w_out_0"].reshape(d, d)
    wup0, wup1 = g["w_up_0"], g["w_up_1"]
    wdown0, wdown1 = g["w_down_0"].reshape(-1, d), g["w_down_1"].reshape(-1, d)
    n_grp = len(POOL_WINDOWS)
    cg = d // n_grp
    poolw = g["pool_w_1"].reshape(N_DEV, n_grp, cg // N_DEV, cg).transpose(1, 0, 2, 3).reshape(n_grp, cg, cg)
    h1 = _outproj(att, cv, x, wout)
    h2, n1, a0, z0 = _mlp_fwd(h1, gains["ffn0"], wup0, wdown0, name="mlp_fwd0")
    h3, pooled = _pool_fwd(h2, gains["mix1"], poolw, pool_scale)
    loss, dh4, dg_final, n3, a1, z1 = _mlp_fwd_loss(h3, gains["ffn1"], wup1, wdown1, gains["final"], target,
                                                    name="mlp_fwd1")
    f = a1.shape[1]
    fb = f // N_DEV
    dh3, da1, dg_ffn1 = _mlp_bwd(dh4, h3, a1, gains["ffn1"], wup1, wdown1, name="mlp_bwd1")
    dwdown1 = _mm_tn(z1, dh4, name="dwdown1", ta=1024, tb=1024, tt=2048, out_dtype=BF16)
    dwup1 = _mm_tn(n3, da1, name="dwup1", ta=d, tb=fb, tt=4096, blocked_out=True, out_dtype=BF16)
    dh2, dpoolw, dscale, dg_mix1 = _pool_bwd(dh3, h2, pooled, gains["mix1"], poolw, pool_scale)
    dh1, da0, dg_ffn0 = _mlp_bwd(dh2, h1, a0, gains["ffn0"], wup0, wdown0, name="mlp_bwd0")
    dwdown0 = _mm_tn(z0, dh2, name="dwdown0", ta=1024, tb=1024, tt=2048, out_dtype=BF16)
    dwup0 = _mm_tn(n1, da0, name="dwup0", ta=d, tb=fb, tt=4096, blocked_out=True, out_dtype=BF16)
    do_aug, do_aug_t, dbcx, dconvw = _outproj_conv_bwd(dh1, att, wout, bcx, conv_w)
    dwout = _mm_tn_cat([att, cv], [dh1], name="dwout", tt=2048)
    gblocks = {
        "w_out_0": dwout.reshape(N_DEV, d // N_DEV, d), "w_up_0": dwup0, "w_up_1": dwup1,
        "w_down_0": dwdown0.reshape(N_DEV, -1, d), "w_down_1": dwdown1.reshape(N_DEV, -1, d),
        "pool_w_1": dpoolw.astype(BF16).reshape(n_grp, N_DEV, cg // N_DEV, cg).transpose(1, 0, 2, 3).reshape(
            N_DEV, n_grp * (cg // N_DEV), cg),
    }
    (dq, dqx, dk, dkx, dv), landed = _attn_bwd(q_bwd, do_aug, q_bwd_t, do_aug_t, k_aug, v_aug,
                                               [gblocks[n] for n in LATE])
    df, dbf = _fgate_bwd(dqx, dkx, sgate)
    dwin_t = jnp.concatenate(
        [_mm_tn_cat([dq, dk, dv], [n0], name="dwin_qkv", tt=2048),
         _mm_tn(df, n0, name="dwin_f", ta=F_PAD, tb=d, tt=2048, out_dtype=BF16)[:N_HEADS],
         _mm_tn(dbcx, n0, name="dwin_bcx", ta=512, tb=d, tt=4096, out_dtype=BF16)], axis=0)
    dwin_blocks = dwin_t.reshape(N_DEV, dwin_t.shape[0] // N_DEV, d)
    grad_x, dg_mix0, landed_win = _inproj_bwd(dq, dk, dv, df, dbcx, dh1, x, gains["mix0"], win_pt, dwin_blocks)
    small = dict(mix0=dg_mix0, ffn0=dg_ffn0, mix1=dg_mix1, pool_scale=dscale, ffn1=dg_ffn1, final=dg_final,
                 b_f=dbf, conv_w=dconvw)
    return loss, grad_x, dict(zip(LATE + ("w_in_0",), tuple(landed) + (landed_win,))), small


def _mesh_places():
    x, y, c = lax.axis_index("x"), lax.axis_index("y"), lax.axis_index("c")
    chips = [(1 - x, y), (x, 1 - y), (1 - x, 1 - y)]
    return (x, y, c), (x, y, 1 - c), chips


def _all_gather(shards):
    n = len(shards)

    def body(*refs):
        ins, outs = refs[:n], refs[n:2 * n]
        send_sems, recv_sems, local_sems = refs[2 * n:]
        me, sib, chips = _mesh_places()
        c = me[2]

        def copy(ai, k, block, to, src=None):
            dst = outs[ai].at[_slot(*block)]
            return pltpu.make_async_remote_copy(
                src_ref=dst if src is None else src, dst_ref=dst, send_sem=send_sems.at[7 * ai + k],
                recv_sem=recv_sems.at[7 * ai + k], device_id=to, device_id_type=MESH)

        mine = [pltpu.make_async_copy(ins[ai], outs[ai].at[_slot(*me)], local_sems.at[ai]) for ai in range(n)]
        for cp in mine:
            cp.start()
        first = []
        for ai in range(n):
            first.append(copy(ai, 0, me, sib, src=ins[ai]))
            first += [copy(ai, 1 + j, me, (*chip, c), src=ins[ai]) for j, chip in enumerate(chips)]
        for cp in first:
            cp.start()
        passed = []
        for ai in range(n):
            for j, chip in enumerate(chips):
                copy(ai, 1 + j, (*chip, c), me).wait_recv()
                cp = copy(ai, 4 + j, (*chip, c), sib)
                cp.start()
                passed.append(cp)
        for ai in range(n):
            copy(ai, 0, sib, me).wait_recv()
            for j, chip in enumerate(chips):
                copy(ai, 4 + j, (*chip, 1 - c), me).wait_recv()
        for cp in first + passed:
            cp.wait_send()
        for cp in mine:
            cp.wait()

    return pl.pallas_call(
        body, name="all_gather",
        in_specs=[HBM_SPEC] * n, out_specs=[HBM_SPEC] * n,
        out_shape=[jax.ShapeDtypeStruct((N_DEV,) + s.shape, s.dtype) for s in shards],
        scratch_shapes=[pltpu.SemaphoreType.DMA((7 * n,)), pltpu.SemaphoreType.DMA((7 * n,)),
                        pltpu.SemaphoreType.DMA((n,))],
    )(*shards)


SMALL_ROWS = 16


def _small_allreduce(parts):
    n, _, w = parts.shape
    assert n <= SMALL_ROWS

    def body(p_ref, o_ref, gath, send_sems, recv_sems):
        x, y, c = lax.axis_index("x"), lax.axis_index("y"), lax.axis_index("c")
        my = _slot(x, y, c)
        rows = [jnp.sum(p_ref[i], axis=0, keepdims=True) for i in range(n)]
        rows.append(jnp.zeros((SMALL_ROWS - n, w), F32))
        gath[my] = jnp.concatenate(rows, axis=0)
        copies = []
        for k in range(1, N_DEV):
            px, py, pc = x ^ (k >> 2), y ^ ((k >> 1) & 1), c ^ (k & 1)
            cp = pltpu.make_async_remote_copy(
                src_ref=gath.at[my], dst_ref=gath.at[my], send_sem=send_sems.at[k - 1], recv_sem=recv_sems.at[k - 1],
                device_id=(px, py, pc), device_id_type=MESH)
            cp.start()
            copies.append(cp)
        for cp in copies:
            cp.wait()
        acc = gath[0]
        for d in range(1, N_DEV):
            acc = acc + gath[d]
        o_ref[...] = acc

    return pl.pallas_call(
        body, name="small_allreduce",
        in_specs=[VMEM_SPEC], out_specs=VMEM_SPEC,
        out_shape=jax.ShapeDtypeStruct((SMALL_ROWS, w), F32),
        scratch_shapes=[pltpu.VMEM((N_DEV, SMALL_ROWS, w), F32), pltpu.SemaphoreType.DMA((N_DEV - 1,)),
                        pltpu.SemaphoreType.DMA((N_DEV - 1,))],
    )(parts)


def _adamw(g, w, m, v, *, name, tm=256):
    r, c = g.shape
    tm = tm if r % tm == 0 else r
    bc1 = 1.0 - ADAM_B1 ** ADAM_STEP
    bc2 = 1.0 - ADAM_B2 ** ADAM_STEP

    def body(g_ref, w_ref, m_ref, v_ref, d_ref, nm_ref, nv_ref):
        gv = g_ref[...]
        nm = ADAM_B1 * m_ref[...] + (1.0 - ADAM_B1) * gv
        nv = ADAM_B2 * v_ref[...] + (1.0 - ADAM_B2) * jnp.square(gv)
        nm_ref[...] = nm
        nv_ref[...] = nv
        d_ref[...] = -ADAM_LR * ((nm / bc1) / (jnp.sqrt(nv / bc2) + ADAM_EPS) + ADAM_WD * w_ref[...])

    blk = pl.BlockSpec((tm, c), lambda i: (i, 0))
    shp = jax.ShapeDtypeStruct((r, c), F32)
    return pl.pallas_call(
        body, name=name, grid=(r // tm,), in_specs=[blk] * 4, out_specs=[blk] * 3, out_shape=[shp] * 3,
        compiler_params=_cp(("parallel",)),
    )(g, w, m, v)


def _transpose_cast(a, *, name):
    def body(a_ref, o_ref):
        o_ref[...] = a_ref[...].T.astype(BF16)

    return pl.pallas_call(body, name=name, out_shape=jax.ShapeDtypeStruct(a.shape[::-1], BF16),
                          compiler_params=_cp())(a)


def _adamw_sum_t(parts, w, m, v, *, name):
    bc1 = 1.0 - ADAM_B1 ** ADAM_STEP
    bc2 = 1.0 - ADAM_B2 ** ADAM_STEP

    def body(p_ref, w_ref, m_ref, v_ref, g_ref, d_ref, nm_ref, nv_ref):
        acc = p_ref[0].astype(F32)
        for k in range(1, N_DEV):
            acc = acc + p_ref[k].astype(F32)
        gv = acc.T
        g_ref[...] = gv
        nm = ADAM_B1 * m_ref[...] + (1.0 - ADAM_B1) * gv
        nv = ADAM_B2 * v_ref[...] + (1.0 - ADAM_B2) * jnp.square(gv)
        nm_ref[...] = nm
        nv_ref[...] = nv
        d_ref[...] = -ADAM_LR * ((nm / bc1) / (jnp.sqrt(nv / bc2) + ADAM_EPS) + ADAM_WD * w_ref[...])

    shp = jax.ShapeDtypeStruct(w.shape, F32)
    return pl.pallas_call(body, name=name, out_shape=[shp] * 4, compiler_params=_cp())(parts, w, m, v)


def _adamw_sum(parts, w, m, v, *, name, tm=256):
    _, r, c = parts.shape
    tm = tm if r % tm == 0 else r
    bc1 = 1.0 - ADAM_B1 ** ADAM_STEP
    bc2 = 1.0 - ADAM_B2 ** ADAM_STEP

    def body(p_ref, w_ref, m_ref, v_ref, g_ref, d_ref, nm_ref, nv_ref):
        gv = p_ref[0].astype(F32)
        for k in range(1, N_DEV):
            gv = gv + p_ref[k].astype(F32)
        g_ref[...] = gv
        nm = ADAM_B1 * m_ref[...] + (1.0 - ADAM_B1) * gv
        nv = ADAM_B2 * v_ref[...] + (1.0 - ADAM_B2) * jnp.square(gv)
        nm_ref[...] = nm
        nv_ref[...] = nv
        d_ref[...] = -ADAM_LR * ((nm / bc1) / (jnp.sqrt(nv / bc2) + ADAM_EPS) + ADAM_WD * w_ref[...])

    blk = pl.BlockSpec((tm, c), lambda i: (i, 0))
    shp = jax.ShapeDtypeStruct((r, c), F32)
    return pl.pallas_call(
        body, name=name, grid=(r // tm,), in_specs=[pl.BlockSpec((N_DEV, tm, c), lambda i: (0, i, 0))] + [blk] * 3,
        out_specs=[blk] * 4, out_shape=[shp] * 4, compiler_params=_cp(("parallel",)),
    )(parts, w, m, v)


BIG = ("w_in_0", "w_out_0", "w_up_0", "w_down_0", "pool_w_1", "w_up_1", "w_down_1")
SMALL = ("norm_mix_0", "norm_ffn_0", "norm_mix_1", "pool_scale_1", "norm_ffn_1", "final_norm", "b_f_0", "conv_w_0")
WEIGHTS = ("norm_mix_0", "w_in_0", "b_f_0", "conv_w_0", "w_out_0", "norm_ffn_0", "w_up_0", "w_down_0", "norm_mix_1",
           "pool_w_1", "pool_scale_1", "norm_ffn_1", "w_up_1", "w_down_1", "final_norm")


def _pad_to(a, rows, cols):
    return jnp.pad(a, ((0, rows - a.shape[0]), (0, cols - a.shape[1])))


def _pack_small(p, width):
    rows = [p[n].reshape(1, -1) for n in SMALL[:6]]
    rows.append(_pad_to(p["b_f_0"].reshape(1, -1), 1, width))
    rows.append(_pad_to(p["conv_w_0"], 3, width))
    return _pad_to(jnp.concatenate(rows, axis=0), SMALL_ROWS, width)


def _unpack_small(a, like):
    out = {n: a[i] for i, n in enumerate(SMALL[:6])}
    out["b_f_0"] = a[6, :like["b_f_0"].shape[0]]
    out["conv_w_0"] = a[7:10, :like["conv_w_0"].shape[1]]
    return out


def kernel(x, norm_mix_0, w_in_0, b_f_0, conv_w_0, w_out_0, norm_ffn_0, w_up_0, w_down_0, norm_mix_1, pool_w_1, pool_scale_1, norm_ffn_1, w_up_1, w_down_1, final_norm, loss_target, m_norm_mix_0, m_w_in_0, m_b_f_0, m_conv_w_0, m_w_out_0, m_norm_ffn_0, m_w_up_0, m_w_down_0, m_norm_mix_1, m_pool_w_1, m_pool_scale_1, m_norm_ffn_1, m_w_up_1, m_w_down_1, m_final_norm, v_norm_mix_0, v_w_in_0, v_b_f_0, v_conv_w_0, v_w_out_0, v_norm_ffn_0, v_w_up_0, v_w_down_0, v_norm_mix_1, v_pool_w_1, v_pool_scale_1, v_norm_ffn_1, v_w_up_1, v_w_down_1, v_final_norm):
    w = dict(norm_mix_0=norm_mix_0, w_in_0=w_in_0, b_f_0=b_f_0, conv_w_0=conv_w_0, w_out_0=w_out_0,
             norm_ffn_0=norm_ffn_0, w_up_0=w_up_0, w_down_0=w_down_0, norm_mix_1=norm_mix_1, pool_w_1=pool_w_1,
             pool_scale_1=pool_scale_1, norm_ffn_1=norm_ffn_1, w_up_1=w_up_1, w_down_1=w_down_1, final_norm=final_norm)
    m = dict(norm_mix_0=m_norm_mix_0, w_in_0=m_w_in_0, b_f_0=m_b_f_0, conv_w_0=m_conv_w_0, w_out_0=m_w_out_0,
             norm_ffn_0=m_norm_ffn_0, w_up_0=m_w_up_0, w_down_0=m_w_down_0, norm_mix_1=m_norm_mix_1,
             pool_w_1=m_pool_w_1, pool_scale_1=m_pool_scale_1, norm_ffn_1=m_norm_ffn_1, w_up_1=m_w_up_1,
             w_down_1=m_w_down_1, final_norm=m_final_norm)
    v = dict(norm_mix_0=v_norm_mix_0, w_in_0=v_w_in_0, b_f_0=v_b_f_0, conv_w_0=v_conv_w_0, w_out_0=v_w_out_0,
             norm_ffn_0=v_norm_ffn_0, w_up_0=v_w_up_0, w_down_0=v_w_down_0, norm_mix_1=v_norm_mix_1,
             pool_w_1=v_pool_w_1, pool_scale_1=v_pool_scale_1, norm_ffn_1=v_norm_ffn_1, w_up_1=v_w_up_1,
             w_down_1=v_w_down_1, final_norm=v_final_norm)
    d = x.shape[-1]
    n_in = w_in_0.shape[1] * N_DEV
    n_qkv = 3 * ATTN_W
    pool_g, pool_rows, pool_c = pool_w_1.shape

    def shard2d(p):
        return {n: (p[n].reshape(pool_g * pool_rows, pool_c) if n == "pool_w_1" else p[n]) for n in BIG}
    w2, m2, v2 = shard2d(w), shard2d(m), shard2d(v)

    conv_cols = conv_w_0.shape[1]
    win_g8, conv_g8 = _all_gather([_transpose_cast(w_in_0, name="w_in_t"), _pad_to(conv_w_0, 8, 128)])
    conv_full = conv_g8[:, :, :conv_cols].transpose(1, 0, 2).reshape(8, N_DEV * conv_cols)
    win_t = win_g8.reshape(n_in, d)
    win_pt = jnp.concatenate([win_t[:n_qkv], _pad_to(win_t[n_qkv:n_qkv + N_HEADS], F_PAD, d),
                              win_t[n_qkv + N_HEADS:]], axis=0)

    gains = dict(mix0=norm_mix_0.reshape(1, d), ffn0=norm_ffn_0.reshape(1, d), mix1=norm_mix_1.reshape(1, d),
                 ffn1=norm_ffn_1.reshape(1, d), final=final_norm.reshape(1, d))
    dev = _slot(lax.axis_index("x"), lax.axis_index("y"), lax.axis_index("c"))
    loss8, grad_x, landed, small = _local_step(
        x[0], loss_target[0], gains, _pad_to(b_f_0.reshape(1, -1), 1, F_PAD), conv_full, pool_scale_1.reshape(1, d),
        win_pt, {n: w2[n].astype(BF16) for n in LATE})
    parts = jnp.concatenate(
        [small[k][None] for k in ("mix0", "ffn0", "mix1", "pool_scale", "ffn1", "final")]
        + [_pad_to(small["b_f"], 8, d)[None], jnp.pad(small["conv_w"], ((0, 0), (0, 0), (0, d - CONV_CH))),
           _pad_to(loss8[0:1, 0:1], 8, d)[None]], axis=0)
    tot = _small_allreduce(parts)
    loss = tot[10, 0]
    conv_g = lax.dynamic_slice(tot, (7, dev * conv_cols), (3, conv_cols))
    gs = tot.at[7:10].set(_pad_to(conv_g, 3, d))

    grads, deltas, new_m, new_v = {}, {}, {}, {}
    for n in BIG:
        if n in LATE:
            gr, dl, nm, nv = _adamw_sum(landed[n], w2[n], m2[n], v2[n], name="adamw_" + n)
        else:
            gr, dl, nm, nv = _adamw_sum_t(landed[n], w2[n], m2[n], v2[n], name="adamw_" + n)
        for dst, val in ((grads, gr), (deltas, dl), (new_m, nm), (new_v, nv)):
            dst[n] = val.reshape(w[n].shape)
    dl, nm, nv = _adamw(gs, _pack_small(w, d), _pack_small(m, d), _pack_small(v, d), name="adamw_small")
    for dst, val in ((grads, gs), (deltas, dl), (new_m, nm), (new_v, nv)):
        dst.update(_unpack_small(val, w))
    return (loss, grad_x[None], *[grads[n] for n in WEIGHTS], *[deltas[n] for n in WEIGHTS],
            *[new_m[n] for n in WEIGHTS], *[new_v[n] for n in WEIGHTS])
```

```python
import functools

import jax
import jax.numpy as jnp
from jax import lax
from jax.experimental import pallas as pl
from jax.experimental.pallas import tpu as pltpu

F32 = jnp.float32
BF16 = jnp.bfloat16

N_DEV = 8
N_HEADS = 8
HEAD_DIM = 64
PAIR = 2 * HEAD_DIM
ATTN_W = N_HEADS * HEAD_DIM
CONV_CH = 512
F_PAD = 128
POOL_WINDOWS = (2, 4, 8, 16)
POOL_HALO = 16
CONV_HALO = 16
RMS_EPS = 1e-6
Q_SCALE = HEAD_DIM ** -0.5
LOG2E = 1.4426950408889634
NEG = -1e30
AUX_BIAS = 0
AUX_LSE = 3
AUX_ROWSUM = 6
ADAM_LR, ADAM_B1, ADAM_B2, ADAM_EPS, ADAM_WD, ADAM_STEP = 0.001, 0.9, 0.999, 1e-08, 0.01, 10
MESH = pl.DeviceIdType.MESH
VMEM_LIMIT = 56 * 2**20


def _cp(sem=None, vmem=VMEM_LIMIT, **kw):
    return pltpu.CompilerParams(dimension_semantics=sem, vmem_limit_bytes=vmem, **kw)


def _dot(a, b):
    return jnp.dot(a, b, preferred_element_type=F32)


def _dot_nt(a, b):
    return lax.dot_general(a, b, (((1,), (1,)), ((), ())), preferred_element_type=F32)


def _dot_tn(a, b):
    return lax.dot_general(a, b, (((0,), (0,)), ((), ())), preferred_element_type=F32)


def _rstd(h):
    return lax.rsqrt(jnp.mean(h * h, axis=-1, keepdims=True) + RMS_EPS)


def _rows8(x):
    r, n = x.shape
    return jnp.sum(x.reshape(r // 8, 8, n), axis=0)


def _norm_bwd(dn, h, g):
    r = _rstd(h)
    xhat = h * r
    dy = dn * g
    dh = r * (dy - xhat * jnp.mean(dy * xhat, axis=-1, keepdims=True))
    return dh, _rows8(dn * xhat)


def _const_spec(shape):
    nd = len(shape)
    return pl.BlockSpec(shape, lambda *_: (0,) * nd, pipeline_mode=pl.Buffered(1))


HBM_SPEC = pl.BlockSpec(memory_space=pltpu.HBM)
VMEM_SPEC = pl.BlockSpec(memory_space=pltpu.VMEM)


def _slot(px, py, pc):
    return 4 * px + 2 * py + pc


class _Exchange:
    def __init__(self, srcs, dsts, send_sems, recv_sems, local_sems, gather):
        x, y, c = lax.axis_index("x"), lax.axis_index("y"), lax.axis_index("c")
        me = _slot(x, y, c)
        self.copies = []
        for a, (src, dst) in enumerate(zip(srcs, dsts)):
            self.copies.append(pltpu.make_async_copy(src if gather else src.at[me], dst.at[me], local_sems.at[a]))
            for k in range(1, N_DEV):
                px, py, pc = x ^ (k >> 2), y ^ ((k >> 1) & 1), c ^ (k & 1)
                self.copies.append(pltpu.make_async_remote_copy(
                    src_ref=src if gather else src.at[_slot(px, py, pc)], dst_ref=dst.at[me],
                    send_sem=send_sems.at[(N_DEV - 1) * a + k - 1], recv_sem=recv_sems.at[(N_DEV - 1) * a + k - 1],
                    device_id=(px, py, pc), device_id_type=MESH))

    def start(self):
        for cp in self.copies:
            cp.start()

    def wait(self):
        for cp in self.copies:
            cp.wait()

    @staticmethod
    def scratch(n):
        return [pltpu.SemaphoreType.DMA(((N_DEV - 1) * n,)), pltpu.SemaphoreType.DMA(((N_DEV - 1) * n,)),
                pltpu.SemaphoreType.DMA((n,))]


def _mesh_places():
    x, y, c = lax.axis_index("x"), lax.axis_index("y"), lax.axis_index("c")
    chips = [(1 - x, y), (x, 1 - y), (1 - x, 1 - y)]
    return (x, y, c), (x, y, 1 - c), chips


class _TwoLevelGather:
    def __init__(self, srcs, dsts, send_sems, recv_sems, local_sems):
        me, sib, chips = _mesh_places()
        c = me[2]
        n = len(srcs)

        def copy(a, k, block, to, src=None):
            dst = dsts[a].at[_slot(*block)]
            return pltpu.make_async_remote_copy(
                src_ref=dst if src is None else src, dst_ref=dst, send_sem=send_sems.at[7 * a + k],
                recv_sem=recv_sems.at[7 * a + k], device_id=to, device_id_type=MESH)

        self.mine = [pltpu.make_async_copy(srcs[a], dsts[a].at[_slot(*me)], local_sems.at[a]) for a in range(n)]
        self.first, self.landed, self.passed, self.rest = [], [], [], []
        for a in range(n):
            self.first.append(copy(a, 0, me, sib, src=srcs[a]))
            self.first += [copy(a, 1 + j, me, (*chip, c), src=srcs[a]) for j, chip in enumerate(chips)]
            self.landed += [copy(a, 1 + j, (*chip, c), me) for j, chip in enumerate(chips)]
            self.passed += [copy(a, 4 + j, (*chip, c), sib) for j, chip in enumerate(chips)]
            self.rest.append(copy(a, 0, sib, me))
            self.rest += [copy(a, 4 + j, (*chip, 1 - c), me) for j, chip in enumerate(chips)]

    def start(self):
        for cp in self.mine + self.first:
            cp.start()

    def forward(self):
        for arrived, onward in zip(self.landed, self.passed):
            arrived.wait_recv()
            onward.start()

    def wait(self):
        for cp in self.rest:
            cp.wait_recv()
        for cp in self.first + self.passed:
            cp.wait_send()
        for cp in self.mine:
            cp.wait()


def _norm_inproj(x, g, win_pt, conv_w, *, tm=512):
    t, d = x.shape
    n_all = win_pt.shape[0]
    n_qkv = 3 * ATTN_W
    n_bcx = 3 * CONV_CH
    assert n_all == n_qkv + F_PAD + n_bcx
    tm = min(tm, t)
    ch = CONV_CH

    def body(x_ref, g_ref, w_ref, cw_ref, n_ref, qkv_ref, f_ref, bcx_ref, cv_ref, ext):
        h = x_ref[...]
        n = (h * _rstd(h) * g_ref[...]).astype(BF16)
        n_ref[...] = n
        for c0 in range(0, n_qkv, 512):
            acc = _dot_nt(n, w_ref[c0:c0 + 512, :])
            if c0 < ATTN_W:
                acc = acc * (Q_SCALE * LOG2E)
            qkv_ref[:, c0:c0 + 512] = acc.astype(BF16)
        f_ref[...] = _dot_nt(n, w_ref[n_qkv:n_qkv + F_PAD, :])
        bcx = []
        for k in range(3):
            c0 = n_qkv + F_PAD + k * ch
            v = _dot_nt(n, w_ref[c0:c0 + ch, :]).astype(BF16)
            bcx_ref[:, k * ch:(k + 1) * ch] = v
            bcx.append(v.astype(F32))
        @pl.when(pl.program_id(0) == 0)
        def _():
            ext[tm:tm + CONV_HALO, :] = jnp.zeros((CONV_HALO, ch), F32)
        ext[0:CONV_HALO, :] = ext[tm:tm + CONV_HALO, :]
        ext[CONV_HALO:CONV_HALO + tm, :] = bcx[1] * bcx[2]
        conv = (cw_ref[0:1, :] * ext[CONV_HALO - 2:CONV_HALO - 2 + tm, :]
                + cw_ref[1:2, :] * ext[CONV_HALO - 1:CONV_HALO - 1 + tm, :]
                + cw_ref[2:3, :] * ext[CONV_HALO:CONV_HALO + tm, :])
        cv_ref[...] = (bcx[0] * conv).astype(BF16)

    return pl.pallas_call(
        body, name="norm_inproj", grid=(t // tm,),
        in_specs=[pl.BlockSpec((tm, d), lambda i: (i, 0)), _const_spec((1, d)), _const_spec((n_all, d)),
                  _const_spec((8, ch))],
        out_specs=[pl.BlockSpec((tm, d), lambda i: (i, 0)), pl.BlockSpec((tm, n_qkv), lambda i: (i, 0)),
                   pl.BlockSpec((tm, F_PAD), lambda i: (i, 0)), pl.BlockSpec((tm, n_bcx), lambda i: (i, 0)),
                   pl.BlockSpec((tm, ch), lambda i: (i, 0))],
        out_shape=[jax.ShapeDtypeStruct((t, d), BF16), jax.ShapeDtypeStruct((t, n_qkv), BF16),
                   jax.ShapeDtypeStruct((t, F_PAD), F32), jax.ShapeDtypeStruct((t, n_bcx), BF16),
                   jax.ShapeDtypeStruct((t, ch), BF16)],
        scratch_shapes=[pltpu.VMEM((CONV_HALO + tm, ch), F32)],
        compiler_params=_cp(("arbitrary",)),
    )(x, g, win_pt, conv_w)


def _head_lanes(h):
    lane = lax.broadcasted_iota(jnp.int32, (1, PAIR), 1)
    hh = h % 2
    return lane, lane // HEAD_DIM == hh, HEAD_DIM * (1 - hh)


def _pieces(col):
    hi = col.astype(BF16).astype(F32)
    r1 = col - hi
    mid = r1.astype(BF16).astype(F32)
    lo = (r1 - mid).astype(BF16).astype(F32)
    return hi, mid, lo


def _put_pieces(lane, first, col, other):
    hi, mid, lo = _pieces(col)
    return jnp.where(lane == first, hi, jnp.where(lane == first + 1, mid, jnp.where(lane == first + 2, lo, other)))


def _fgate_prep(flog, b_f, qkv, *, tm=512):
    t = flog.shape[0]
    tm = min(tm, t)

    def body(f_ref, b_ref, qkv_ref, qat_ref, ka_ref, va_ref, vat_ref, sg_ref, carry):
        @pl.when(pl.program_id(0) == 0)
        def _():
            carry[...] = jnp.zeros_like(carry)
        z = f_ref[...] + b_ref[...]
        e = jnp.exp(-jnp.abs(z))
        logf = jnp.minimum(z, 0.0) - jnp.log(1.0 + e)
        sg_ref[...] = jnp.where(z >= 0, e, 1.0) / (1.0 + e)
        r = lax.broadcasted_iota(jnp.int32, (tm, tm), 0)
        c = lax.broadcasted_iota(jnp.int32, (tm, tm), 1)
        tri = (c <= r).astype(F32)
        cs = jnp.dot(tri, logf, preferred_element_type=F32, precision=lax.Precision.HIGHEST) + carry[...]
        carry[...] = cs[tm - 1:tm, :]
        cs2 = cs * LOG2E
        for h in range(N_HEADS):
            lane, head, aux = _head_lanes(h)
            p0 = (h // 2) * PAIR
            ones = ((lane >= aux + AUX_LSE) & (lane <= aux + AUX_ROWSUM)).astype(F32)
            bias = (lane >= aux + AUX_BIAS) & (lane < aux + AUX_BIAS + 3)
            k_aux = _put_pieces(lane, aux + AUX_BIAS, cs2[:, h:h + 1], ones)
            q_aug = jnp.where(head, qkv_ref[:, p0:p0 + PAIR], jnp.where(bias, -1.0, 0.0).astype(BF16))
            v_aug = jnp.where(head, qkv_ref[:, 2 * ATTN_W + p0:2 * ATTN_W + p0 + PAIR],
                              jnp.where(bias, 1.0, 0.0).astype(BF16))
            qat_ref[h] = q_aug.T
            ka_ref[h] = jnp.where(head, qkv_ref[:, ATTN_W + p0:ATTN_W + p0 + PAIR], k_aux.astype(BF16))
            va_ref[h] = v_aug
            vat_ref[h] = v_aug.T

    aug = lambda: pl.BlockSpec((N_HEADS, tm, PAIR), lambda i: (0, i, 0))
    aug_t = lambda: pl.BlockSpec((N_HEADS, PAIR, tm), lambda i: (0, 0, i))
    aug_shape = jax.ShapeDtypeStruct((N_HEADS, t, PAIR), BF16)
    aug_t_shape = jax.ShapeDtypeStruct((N_HEADS, PAIR, t), BF16)
    return pl.pallas_call(
        body, name="fgate_prep", grid=(t // tm,),
        in_specs=[pl.BlockSpec((tm, F_PAD), lambda i: (i, 0)), _const_spec((1, F_PAD)),
                  pl.BlockSpec((tm, 3 * ATTN_W), lambda i: (i, 0))],
        out_specs=[aug_t(), aug(), aug(), aug_t(), pl.BlockSpec((tm, F_PAD), lambda i: (i, 0))],
        out_shape=[aug_t_shape, aug_shape, aug_shape, aug_t_shape, jax.ShapeDtypeStruct((t, F_PAD), F32)],
        scratch_shapes=[pltpu.VMEM((1, F_PAD), F32)],
        compiler_params=_cp(("arbitrary",)),
    )(flog, b_f, qkv)


def _put_pieces_t(row, first, vec, other):
    hi, mid, lo = _pieces(vec)
    return jnp.where(row == first, hi, jnp.where(row == first + 1, mid, jnp.where(row == first + 2, lo, other)))


def _attn_fwd(q_aug_t, k_aug, v_aug_t, shards, *, tq=1024):
    t = k_aug.shape[1]
    tq = min(tq, t)
    tk = tq // 2
    nq = t // tq
    n_pairs = ATTN_W // PAIR
    n_sh = len(shards)
    forward_step = (11 * n_pairs * nq) // 16

    def body(qt_ref, k_ref, vt_ref, *rest):
        o_ref, qb_ref, qbt_ref = rest[n_sh:n_sh + 3]
        s_scr = rest[2 * n_sh + 3]
        gather = _TwoLevelGather(rest[:n_sh], rest[n_sh + 3:2 * n_sh + 3], *rest[2 * n_sh + 4:])
        i = pl.program_id(1)
        step = pl.program_id(0) * nq + i

        @pl.when(step == 0)
        def _():
            gather.start()

        @pl.when(step == forward_step)
        def _():
            gather.forward()
        key = lax.broadcasted_iota(jnp.int32, (tk, tq), 0)
        qry = lax.broadcasted_iota(jnp.int32, (tk, tq), 1)
        qt = [qt_ref[0], qt_ref[1]]

        def logits(hh, tile, slot, diag):
            s = _dot(k_ref[hh, pl.ds(pl.multiple_of(tile * tk, tk), tk), :], qt[hh])
            if diag:
                s = jnp.where(key + (tile * tk - i * tq) <= qry, s, NEG)
            s_scr[hh, slot] = s
            return jnp.max(s, axis=0, keepdims=True)

        def probs(hh, tile, slot, m, acc, tmax):
            mn = jnp.maximum(m, tmax)
            p = jnp.exp2(s_scr[hh, slot] - mn).astype(BF16)
            acc = jnp.exp2(m - mn) * acc + _dot(vt_ref[hh, :, pl.ds(pl.multiple_of(tile * tk, tk), tk)], p)
            return mn, acc

        def advance(carry, prev, slot, nxt, diag=False):
            out = []
            for hh in range(2):
                m, acc, tmax = carry[hh]
                m, acc = probs(hh, prev, slot, m, acc, tmax)
                out.append((m, acc, logits(hh, nxt, 1 - slot, diag)))
            return tuple(out)

        def two_tiles(jj, carry):
            carry = advance(carry, jnp.where(jj == 0, 2 * i, 2 * jj - 1), 1, 2 * jj)
            return advance(carry, 2 * jj, 0, 2 * jj + 1)

        init = tuple((jnp.full((1, tq), NEG, F32), jnp.zeros((PAIR, tq), F32), logits(hh, 2 * i + 1, 0, True))
                     for hh in range(2))
        carry = advance(init, 2 * i + 1, 0, 2 * i, diag=True)
        carry = lax.fori_loop(0, i // 2, lambda jj, c: two_tiles(2 * jj + 1, two_tiles(2 * jj, c)), carry)
        carry = lax.cond(i % 2 == 1, lambda c: two_tiles(i - 1, c), lambda c: c, carry)
        last = jnp.where(i == 0, 2 * i, 2 * i - 1)
        row = lax.broadcasted_iota(jnp.int32, (PAIR, 1), 0)
        res = []
        for hh in range(2):
            aux = HEAD_DIM * (1 - hh)
            m, acc, tmax = carry[hh]
            m, acc = probs(hh, last, 1, m, acc, tmax)
            l = acc[aux + AUX_BIAS:aux + AUX_BIAS + 1, :]
            qbt = _put_pieces_t(row, aux + AUX_LSE, -(m + jnp.log2(l)), qt[hh].astype(F32))
            qbt_ref[hh] = qbt.astype(BF16)
            qb_ref[hh] = qbt.astype(BF16).T
            res.append(acc * (1.0 / l))
        o_ref[...] = jnp.where(row < HEAD_DIM, res[0], res[1]).astype(BF16).T

        @pl.when((pl.program_id(0) == n_pairs - 1) & (i == nq - 1))
        def _():
            gather.wait()

    res = pl.pallas_call(
        body, name="attn_fwd", grid=(n_pairs, nq),
        in_specs=[pl.BlockSpec((2, PAIR, tq), lambda p, i: (p, 0, i)),
                  pl.BlockSpec((2, t, PAIR), lambda p, i: (p, 0, 0), pipeline_mode=pl.Buffered(1)),
                  pl.BlockSpec((2, PAIR, t), lambda p, i: (p, 0, 0), pipeline_mode=pl.Buffered(1))] + [HBM_SPEC] * n_sh,
        out_specs=[pl.BlockSpec((tq, PAIR), lambda p, i: (i, p)),
                   pl.BlockSpec((2, tq, PAIR), lambda p, i: (p, i, 0)),
                   pl.BlockSpec((2, PAIR, tq), lambda p, i: (p, 0, i))] + [HBM_SPEC] * n_sh,
        out_shape=[jax.ShapeDtypeStruct((t, ATTN_W), BF16), jax.ShapeDtypeStruct((N_HEADS, t, PAIR), BF16),
                   jax.ShapeDtypeStruct((N_HEADS, PAIR, t), BF16)]
        + [jax.ShapeDtypeStruct((N_DEV,) + s.shape, s.dtype) for s in shards],
        scratch_shapes=[pltpu.VMEM((2, 2, tk, tq), F32)] + _Exchange.scratch(n_sh),
        compiler_params=_cp(("arbitrary", "arbitrary")),
    )(q_aug_t, k_aug, v_aug_t, *shards)
    return res[0], res[1], res[2], res[3:]


def _prev_halo(tm, halo):
    return lambda i: (jnp.maximum(i * (tm // halo) - 1, 0), 0)


def _next_halo(tm, halo, t):
    return lambda i: (jnp.minimum((i + 1) * (tm // halo), t // halo - 1), 0)


def _mlp_tile(hh, g_ref, wu_ref, wd_ref, n_ref, a_ref, z_ref):
    n_blk, _, fb = wu_ref.shape
    n = (hh * _rstd(hh) * g_ref[...]).astype(BF16)
    n_ref[...] = n
    acc = hh
    for k in range(n_blk):
        a = _dot(n, wu_ref[k])
        zz = jnp.square(jnp.maximum(a, 0.0)).astype(BF16)
        a_ref[:, k * fb:(k + 1) * fb] = a.astype(BF16)
        z_ref[:, k * fb:(k + 1) * fb] = zz
        acc = acc + _dot(zz, wd_ref[k * fb:(k + 1) * fb, :])
    return acc


def _outproj(att, cv, x, wout, *, tm=512):
    t, d = x.shape
    tm = min(tm, t)

    def body(a_ref, c_ref, x_ref, w_ref, h_ref):
        h_ref[...] = x_ref[...] + _dot(a_ref[...], w_ref[0:ATTN_W, :]) + _dot(c_ref[...], w_ref[ATTN_W:, :])

    return pl.pallas_call(
        body, name="outproj", grid=(t // tm,),
        in_specs=[pl.BlockSpec((tm, ATTN_W), lambda i: (i, 0)), pl.BlockSpec((tm, CONV_CH), lambda i: (i, 0)),
                  pl.BlockSpec((tm, d), lambda i: (i, 0)), _const_spec(wout.shape)],
        out_specs=pl.BlockSpec((tm, d), lambda i: (i, 0)),
        out_shape=jax.ShapeDtypeStruct((t, d), F32),
        compiler_params=_cp(("parallel",)),
    )(att, cv, x, wout)


def _mlp_fwd(h, g, wup, wdown, *, name, tm=512):
    t, d = h.shape
    n_blk, _, fb = wup.shape
    f = n_blk * fb
    tm = min(tm, t)

    def body(h_ref, g_ref, wu_ref, wd_ref, ho_ref, n_ref, a_ref, z_ref):
        ho_ref[...] = _mlp_tile(h_ref[...], g_ref, wu_ref, wd_ref, n_ref, a_ref, z_ref)

    row = lambda n_: pl.BlockSpec((tm, n_), lambda i: (i, 0))
    return pl.pallas_call(
        body, name=name, grid=(t // tm,),
        in_specs=[row(d), _const_spec((1, d)), _const_spec(wup.shape), _const_spec(wdown.shape)],
        out_specs=[row(d), row(d), row(f), row(f)],
        out_shape=[jax.ShapeDtypeStruct((t, d), F32), jax.ShapeDtypeStruct((t, d), BF16),
                   jax.ShapeDtypeStruct((t, f), BF16), jax.ShapeDtypeStruct((t, f), BF16)],
        compiler_params=_cp(("parallel",)),
    )(h, g, wup, wdown)


def _mlp_fwd_loss(h, g, wup, wdown, g_out, target, *, name, tm=512):
    t, d = h.shape
    n_blk, _, fb = wup.shape
    f = n_blk * fb
    tm = min(tm, t)
    nsteps = t // tm

    def body(h_ref, g_ref, wu_ref, wd_ref, go_ref, y_ref, loss_ref, dh_ref, dg_ref, n_ref, a_ref, z_ref, lacc):
        i = pl.program_id(0)

        @pl.when(i == 0)
        def _():
            lacc[...] = jnp.zeros_like(lacc)
            dg_ref[...] = jnp.zeros_like(dg_ref)
        hv = _mlp_tile(h_ref[...], g_ref, wu_ref, wd_ref, n_ref, a_ref, z_ref)
        gv = go_ref[...]
        r = _rstd(hv)
        xhat = hv * r
        err = xhat * gv - y_ref[...]
        lacc[...] += _rows8(err * err)
        dout = err * (1.0 / d)
        dy = dout * gv
        dg_ref[...] += _rows8(dout * xhat)
        dh_ref[...] = r * (dy - xhat * jnp.mean(dy * xhat, axis=-1, keepdims=True))

        @pl.when(i == nsteps - 1)
        def _():
            loss_ref[...] = jnp.full(loss_ref.shape, (0.5 / d) * jnp.sum(lacc[...]), F32)

    row = lambda n_: pl.BlockSpec((tm, n_), lambda i: (i, 0))
    return pl.pallas_call(
        body, name=name, grid=(nsteps,),
        in_specs=[row(d), _const_spec((1, d)), _const_spec(wup.shape), _const_spec(wdown.shape), _const_spec((1, d)),
                  row(d)],
        out_specs=[pl.BlockSpec((8, 128), lambda i: (0, 0)), row(d), pl.BlockSpec((8, d), lambda i: (0, 0)),
                   row(d), row(f), row(f)],
        out_shape=[jax.ShapeDtypeStruct((8, 128), F32), jax.ShapeDtypeStruct((t, d), F32),
                   jax.ShapeDtypeStruct((8, d), F32), jax.ShapeDtypeStruct((t, d), BF16),
                   jax.ShapeDtypeStruct((t, f), BF16), jax.ShapeDtypeStruct((t, f), BF16)],
        scratch_shapes=[pltpu.VMEM((8, d), F32)],
        compiler_params=_cp(("arbitrary",)),
    )(h, g, wup, wdown, g_out, target)


def _pool_inv_count(i, tm):
    tglob = (i * tm + lax.broadcasted_iota(jnp.int32, (tm, 1), 0) + 1).astype(F32)
    return [1.0 / jnp.minimum(tglob, float(w)) for w in POOL_WINDOWS]


def _pool_fwd(h, g, poolw, scale, *, tm=512):
    t, d = h.shape
    tm = min(tm, t)
    cg = d // len(POOL_WINDOWS)

    def body(h_ref, hh_ref, g_ref, w_ref, s_ref, ho_ref, p_ref, ext):
        i = pl.program_id(0)
        hv = h_ref[...]
        halo = hh_ref[...]
        n = hv * _rstd(hv) * g_ref[...]
        ext[0:POOL_HALO, :] = jnp.where(i == 0, 0.0, halo * _rstd(halo) * g_ref[...])
        ext[POOL_HALO:POOL_HALO + tm, :] = n
        inv = _pool_inv_count(i, tm)
        for gi, w in enumerate(POOL_WINDOWS):
            cs = slice(gi * cg, (gi + 1) * cg)
            s = ext[POOL_HALO:POOL_HALO + tm, cs]
            for j in range(1, w):
                s = s + ext[POOL_HALO - j:POOL_HALO - j + tm, cs]
            pooled = (s * inv[gi] - n[:, cs]).astype(BF16)
            p_ref[:, cs] = pooled
            ho_ref[:, cs] = hv[:, cs] + _dot(pooled, w_ref[gi]) * s_ref[:, cs]

    row = lambda: pl.BlockSpec((tm, d), lambda i: (i, 0))
    return pl.pallas_call(
        body, name="pool_fwd", grid=(t // tm,),
        in_specs=[row(), pl.BlockSpec((POOL_HALO, d), _prev_halo(tm, POOL_HALO)), _const_spec((1, d)),
                  _const_spec(poolw.shape), _const_spec((1, d))],
        out_specs=[row(), row()],
        out_shape=[jax.ShapeDtypeStruct((t, d), F32), jax.ShapeDtypeStruct((t, d), BF16)],
        scratch_shapes=[pltpu.VMEM((POOL_HALO + tm, d), F32)],
        compiler_params=_cp(("parallel",)),
    )(h, h, g, poolw, scale)


def _mm_tn(a, b, *, name, ta, tb, tt, blocked_out=False, out_dtype=F32):
    t, ka = a.shape
    n = b.shape[1]
    ta, tb, tt = min(ta, ka), min(tb, n), min(tt, t)
    nt = t // tt

    def body(a_ref, b_ref, o_ref, acc):
        @pl.when(pl.program_id(2) == 0)
        def _():
            acc[...] = jnp.zeros_like(acc)
        acc[...] += _dot_tn(a_ref[...].astype(BF16), b_ref[...].astype(BF16))

        @pl.when(pl.program_id(2) == nt - 1)
        def _():
            o_ref[...] = acc[...].astype(out_dtype)

    if blocked_out:
        assert ta == ka
        out_shape = jax.ShapeDtypeStruct((n // tb, ka, tb), out_dtype)
        out_spec = pl.BlockSpec((None, ta, tb), lambda i, j, k: (j, i, 0))
    else:
        out_shape = jax.ShapeDtypeStruct((ka, n), out_dtype)
        out_spec = pl.BlockSpec((ta, tb), lambda i, j, k: (i, j))
    return pl.pallas_call(
        body, name=name, grid=(ka // ta, n // tb, nt),
        in_specs=[pl.BlockSpec((tt, ta), lambda i, j, k: (k, i)), pl.BlockSpec((tt, tb), lambda i, j, k: (k, j))],
        out_specs=out_spec, out_shape=out_shape, scratch_shapes=[pltpu.VMEM((ta, tb), F32)],
        compiler_params=_cp(("parallel", "parallel", "arbitrary")),
    )(a, b)


def _mm_tn_cat(a_list, b_list, *, name, tt, out_dtype=BF16):
    t = a_list[0].shape[0]
    ta, tb = a_list[0].shape[1], b_list[0].shape[1]
    na, nb = len(a_list), len(b_list)
    tt = min(tt, t)
    nt = t // tt

    def body(*refs):
        a_refs, b_refs, o_ref, acc = refs[:na], refs[na:na + nb], refs[na + nb], refs[na + nb + 1]
        i, j, k = pl.program_id(0), pl.program_id(1), pl.program_id(2)

        @pl.when(k == 0)
        def _():
            acc[...] = jnp.zeros_like(acc)
        for ia in range(na):
            for ib in range(nb):
                @pl.when((i == ia) & (j == ib))
                def _(ia=ia, ib=ib):
                    acc[...] += _dot_tn(a_refs[ia][...].astype(BF16), b_refs[ib][...].astype(BF16))

        @pl.when(k == nt - 1)
        def _():
            o_ref[...] = acc[...].astype(out_dtype)

    def held(m, axis):
        def index(i, j, k):
            cur = (i, j)[axis]
            return (jnp.where(cur == m, k, jnp.where(cur < m, 0, nt - 1)), 0)
        return index

    return pl.pallas_call(
        body, name=name, grid=(na, nb, nt),
        in_specs=[pl.BlockSpec((tt, ta), held(m, 0)) for m in range(na)]
        + [pl.BlockSpec((tt, tb), held(m, 1)) for m in range(nb)],
        out_specs=pl.BlockSpec((ta, tb), lambda i, j, k: (i, j)),
        out_shape=jax.ShapeDtypeStruct((na * ta, nb * tb), out_dtype), scratch_shapes=[pltpu.VMEM((ta, tb), F32)],
        compiler_params=_cp(("arbitrary", "arbitrary", "arbitrary")),
    )(*a_list, *b_list)


def _mlp_bwd(dho, h, a, g, wup, wdown, *, name, tm=512):
    t, d = h.shape
    n_blk, _, fb = wup.shape
    f = n_blk * fb
    tm = min(tm, t)

    def body(do_ref, h_ref, a_ref, g_ref, wu_ref, wd_ref, dh_ref, da_ref, dg_ref):
        @pl.when(pl.program_id(0) == 0)
        def _():
            dg_ref[...] = jnp.zeros_like(dg_ref)
        dho_v = do_ref[...]
        dob = dho_v.astype(BF16)
        dn = jnp.zeros((tm, d), F32)
        for k in range(n_blk):
            dz = _dot_nt(dob, wd_ref[k * fb:(k + 1) * fb, :])
            da = (dz * (2.0 * jnp.maximum(a_ref[:, k * fb:(k + 1) * fb].astype(F32), 0.0))).astype(BF16)
            da_ref[:, k * fb:(k + 1) * fb] = da
            dn = dn + _dot_nt(da, wu_ref[k])
        dh, dg = _norm_bwd(dn, h_ref[...], g_ref[...])
        dh_ref[...] = dho_v + dh
        dg_ref[...] += dg

    row = lambda n_: pl.BlockSpec((tm, n_), lambda i: (i, 0))
    return pl.pallas_call(
        body, name=name, grid=(t // tm,),
        in_specs=[row(d), row(d), row(f), _const_spec((1, d)), _const_spec(wup.shape), _const_spec(wdown.shape)],
        out_specs=[row(d), row(f), pl.BlockSpec((8, d), lambda i: (0, 0))],
        out_shape=[jax.ShapeDtypeStruct((t, d), F32), jax.ShapeDtypeStruct((t, f), BF16),
                   jax.ShapeDtypeStruct((8, d), F32)],
        compiler_params=_cp(("arbitrary",)),
    )(dho, h, a, g, wup, wdown)


def _pool_bwd(dho, h, pooled, g, poolw, scale, *, tm=512):
    t, d = h.shape
    tm = min(tm, t)
    ng = len(POOL_WINDOWS)
    cg = d // ng
    nsteps = t // tm

    def body(do_ref, dn_ref, h_ref, p_ref, g_ref, w_ref, s_ref, dh_ref, dw_ref, ds_ref, dg_ref, ext):
        i = pl.program_id(0)

        @pl.when(i == 0)
        def _():
            dw_ref[...] = jnp.zeros_like(dw_ref)
            ds_ref[...] = jnp.zeros_like(ds_ref)
            dg_ref[...] = jnp.zeros_like(dg_ref)
        dho_v = do_ref[...]
        sv = s_ref[...]
        dyp = (dho_v * sv).astype(BF16)
        dyp_halo = (dn_ref[...] * sv).astype(BF16)
        inv = _pool_inv_count(i, tm)
        tnext = ((i + 1) * tm + lax.broadcasted_iota(jnp.int32, (POOL_HALO, 1), 0) + 1).astype(F32)
        last = i == nsteps - 1
        ypre_parts, dpooled_parts = [], []
        for gi, w in enumerate(POOL_WINDOWS):
            cs = slice(gi * cg, (gi + 1) * cg)
            pg = p_ref[:, cs]
            ypre_parts.append(_dot(pg, w_ref[gi]))
            dw_ref[gi] += _dot_tn(pg, dyp[:, cs])
            dpool = _dot_nt(dyp[:, cs], w_ref[gi])
            dpooled_parts.append(dpool)
            ext[0:tm, cs] = dpool * inv[gi]
            dpool_halo = _dot_nt(dyp_halo[:, cs], w_ref[gi]) * (1.0 / jnp.minimum(tnext, float(w)))
            ext[tm:tm + POOL_HALO, cs] = jnp.where(last, 0.0, dpool_halo)
        ds_ref[...] += _rows8(dho_v * jnp.concatenate(ypre_parts, axis=1))
        dn_parts = []
        for gi, w in enumerate(POOL_WINDOWS):
            cs = slice(gi * cg, (gi + 1) * cg)
            s = ext[0:tm, cs]
            for j in range(1, w):
                s = s + ext[j:j + tm, cs]
            dn_parts.append(s - dpooled_parts[gi])
        dh, dg = _norm_bwd(jnp.concatenate(dn_parts, axis=1), h_ref[...], g_ref[...])
        dh_ref[...] = dho_v + dh
        dg_ref[...] += dg

    row = lambda: pl.BlockSpec((tm, d), lambda i: (i, 0))
    acc8 = lambda: pl.BlockSpec((8, d), lambda i: (0, 0))
    return pl.pallas_call(
        body, name="pool_bwd", grid=(nsteps,),
        in_specs=[row(), pl.BlockSpec((POOL_HALO, d), _next_halo(tm, POOL_HALO, t)), row(), row(),
                  _const_spec((1, d)), _const_spec(poolw.shape), _const_spec((1, d))],
        out_specs=[row(), pl.BlockSpec((ng, cg, cg), lambda i: (0, 0, 0)), acc8(), acc8()],
        out_shape=[jax.ShapeDtypeStruct((t, d), F32), jax.ShapeDtypeStruct((ng, cg, cg), F32),
                   jax.ShapeDtypeStruct((8, d), F32), jax.ShapeDtypeStruct((8, d), F32)],
        scratch_shapes=[pltpu.VMEM((tm + POOL_HALO, d), F32)],
        compiler_params=_cp(("arbitrary",)),
    )(dho, dho, h, pooled, g, poolw, scale)


def _outproj_conv_bwd(dh, o, wout, bcx, conv_w, *, tm=512):
    t, d = dh.shape
    tm = min(tm, t)
    ch = CONV_CH
    nsteps = t // tm

    def body(dh_ref, o_ref, w_ref, b_ref, c_ref, x_ref, hc_ref, hx_ref, cw_ref,
             da_ref, dat_ref, db_ref, dw_ref, ext_u, ext_d):
        s = pl.program_id(0)

        @pl.when(s == 0)
        def _():
            dw_ref[...] = jnp.zeros_like(dw_ref)
            ext_d[0:CONV_HALO, :] = jnp.zeros((CONV_HALO, ch), F32)
        dhb = dh_ref[...].astype(BF16)
        for p in range(ATTN_W // PAIR):
            datt = _dot_nt(dhb, w_ref[p * PAIR:(p + 1) * PAIR, :])
            prod = datt * o_ref[:, p * PAIR:(p + 1) * PAIR].astype(F32)
            for hh in range(2):
                lane, head, aux = _head_lanes(hh)
                delta = jnp.sum(jnp.where(head, prod, 0.0), axis=1, keepdims=True)
                aug = _put_pieces(lane, aux + AUX_BIAS, -delta, jnp.where(head, datt, 0.0))
                da_ref[2 * p + hh] = aug.astype(BF16)
                dat_ref[2 * p + hh] = aug.astype(BF16).T
        dcv = _dot_nt(dhb, w_ref[ATTN_W:, :])
        b, c, x = b_ref[...].astype(F32), c_ref[...].astype(F32), x_ref[...].astype(F32)
        ext_u[0:CONV_HALO, :] = jnp.where(s == nsteps - 1, 0.0, hc_ref[...].astype(F32) * hx_ref[...].astype(F32))
        ext_u[CONV_HALO:CONV_HALO + tm, :] = c * x
        dconv = dcv * b
        ext_d[tm:tm + CONV_HALO, :] = ext_d[0:CONV_HALO, :]
        ext_d[0:tm, :] = dconv
        u = [ext_u[CONV_HALO - 2 + k:CONV_HALO - 2 + k + tm, :] for k in range(3)]
        conv = cw_ref[0:1, :] * u[0] + cw_ref[1:2, :] * u[1] + cw_ref[2:3, :] * u[2]
        du = (cw_ref[2:3, :] * dconv + cw_ref[1:2, :] * ext_d[1:1 + tm, :] + cw_ref[0:1, :] * ext_d[2:2 + tm, :])
        db_ref[:, 0:ch] = (dcv * conv).astype(BF16)
        db_ref[:, ch:2 * ch] = (du * x).astype(BF16)
        db_ref[:, 2 * ch:3 * ch] = (du * c).astype(BF16)
        for k in range(3):
            dw_ref[k] += _rows8(dconv * u[k])

    rev = lambda s: nsteps - 1 - s
    row = lambda n_: pl.BlockSpec((tm, n_), lambda s: (rev(s), 0))
    col = lambda k: pl.BlockSpec((tm, ch), lambda s: (rev(s), k))
    prev = lambda k: pl.BlockSpec((CONV_HALO, ch), lambda s: (_prev_halo(tm, CONV_HALO)(rev(s))[0], k))
    return pl.pallas_call(
        body, name="outproj_conv_bwd", grid=(nsteps,),
        in_specs=[row(d), row(ATTN_W), _const_spec(wout.shape), col(0), col(1), col(2), prev(1), prev(2),
                  _const_spec((8, ch))],
        out_specs=[pl.BlockSpec((N_HEADS, tm, PAIR), lambda s: (0, rev(s), 0)),
                   pl.BlockSpec((N_HEADS, PAIR, tm), lambda s: (0, 0, rev(s))),
                   row(3 * ch), pl.BlockSpec((3, 8, ch), lambda s: (0, 0, 0))],
        out_shape=[jax.ShapeDtypeStruct((N_HEADS, t, PAIR), BF16), jax.ShapeDtypeStruct((N_HEADS, PAIR, t), BF16),
                   jax.ShapeDtypeStruct((t, 3 * ch), BF16), jax.ShapeDtypeStruct((3, 8, ch), F32)],
        scratch_shapes=[pltpu.VMEM((CONV_HALO + tm, ch), F32), pltpu.VMEM((tm + CONV_HALO, ch), F32)],
        compiler_params=_cp(("arbitrary",)),
    )(dh, o, wout, bcx, bcx, bcx, bcx, bcx, conv_w)


def _attn_bwd(q_bwd, do_aug, q_bwd_t, do_aug_t, k_aug, v_aug, gblocks, *, tq=1024):
    t = q_bwd.shape[1]
    tq = min(tq, t)
    tk = tq // 2
    nq, nk = t // tq, t // tk
    n_pairs = ATTN_W // PAIR
    n_g = len(gblocks)

    def body(q_ref, do_ref, qt_ref, dot_ref, k_ref, v_ref, *rest):
        dq_ref, dqx_ref, dk_ref, dkx_ref, dv_ref = rest[n_g:n_g + 5]
        dq_scr = rest[2 * n_g + 5]
        scatter = _Exchange(rest[:n_g], rest[n_g + 5:2 * n_g + 5], *rest[2 * n_g + 6:], gather=False)
        j = pl.program_id(1)

        @pl.when((pl.program_id(0) == 0) & (j == 0))
        def _():
            scatter.start()

        @pl.when(j == 0)
        def _():
            dq_scr[...] = jnp.zeros_like(dq_scr)
        k = [k_ref[0], k_ref[1]]
        v = [v_ref[0], v_ref[1]]

        def step(i, carry, diag, rows=tq, row0=0):
            qs = pl.multiple_of(i * tq + row0, tk)
            if diag:
                row = lax.broadcasted_iota(jnp.int32, (rows, tk), 0)
                col = lax.broadcasted_iota(jnp.int32, (rows, tk), 1)
            out = []
            for hh in range(2):
                dk_a, dv_a = carry[hh]
                q = q_ref[hh, pl.ds(qs, rows), :]
                dov = do_ref[hh, pl.ds(qs, rows), :]
                p = jnp.exp2(_dot_nt(q, k[hh]))
                if diag:
                    p = jnp.where(col + (j * tk - i * tq - row0) <= row, p, 0.0)
                ds = (p * _dot_nt(dov, v[hh])).astype(BF16)
                dv_a = dv_a + _dot(dot_ref[hh, :, pl.ds(qs, rows)], p.astype(BF16))
                dk_a = dk_a + _dot(qt_ref[hh, :, pl.ds(qs, rows)], ds)
                dq_scr[hh, pl.ds(qs, rows), :] += _dot(ds, k[hh])
                out.append((dk_a, dv_a))
            return tuple(out)

        zero = (jnp.zeros((PAIR, tk), F32), jnp.zeros((PAIR, tk), F32))
        carry = lax.cond(j % 2 == 0, lambda c: step(j // 2, c, True),
                         lambda c: step(j // 2, c, True, rows=tk, row0=tk), (zero, zero))
        full0 = j // 2 + 1
        odd = (nq - full0) % 2
        carry = lax.cond(odd == 1, lambda c: step(full0, c, False), lambda c: c, carry)
        (dk0, dv0), (dk1, dv1) = lax.fori_loop(
            0, (nq - full0) // 2, lambda ii, c: step(full0 + odd + 2 * ii, c, False, rows=2 * tq), carry)
        first_t = lax.broadcasted_iota(jnp.int32, (PAIR, 1), 0) < HEAD_DIM
        first = lax.broadcasted_iota(jnp.int32, (1, PAIR), 1) < HEAD_DIM
        dk_ref[...] = (jnp.where(first_t, dk0, dk1) * (1.0 / LOG2E)).astype(BF16).T
        dkx_ref[...] = jnp.where(first_t, dk1, dk0).T
        dv_ref[...] = jnp.where(first_t, dv0, dv1).astype(BF16).T

        @pl.when(j == nk - 1)
        def _():
            dq_ref[...] = (jnp.where(first, dq_scr[0], dq_scr[1]) * Q_SCALE).astype(BF16)
            dqx_ref[...] = jnp.where(first, dq_scr[1], dq_scr[0])

        @pl.when((pl.program_id(0) == n_pairs - 1) & (j == nk - 1))
        def _():
            scatter.wait()

    resident = lambda: pl.BlockSpec((2, t, PAIR), lambda p, j: (p, 0, 0), pipeline_mode=pl.Buffered(1))
    resident_t = lambda: pl.BlockSpec((2, PAIR, t), lambda p, j: (p, 0, 0), pipeline_mode=pl.Buffered(1))
    kv_in = lambda: pl.BlockSpec((2, tk, PAIR), lambda p, j: (p, j, 0))
    whole = lambda: pl.BlockSpec((t, PAIR), lambda p, j: (0, p))
    tile = lambda: pl.BlockSpec((tk, PAIR), lambda p, j: (j, p))
    b16 = jax.ShapeDtypeStruct((t, ATTN_W), BF16)
    f32 = jax.ShapeDtypeStruct((t, ATTN_W), F32)
    res = pl.pallas_call(
        body, name="attn_bwd", grid=(n_pairs, nk),
        in_specs=[resident(), resident(), resident_t(), resident_t(), kv_in(), kv_in()] + [HBM_SPEC] * n_g,
        out_specs=[whole(), whole(), tile(), tile(), tile()] + [HBM_SPEC] * n_g,
        out_shape=[b16, f32, b16, f32, b16] + [jax.ShapeDtypeStruct(g.shape, g.dtype) for g in gblocks],
        scratch_shapes=[pltpu.VMEM((2, t, PAIR), F32)] + _Exchange.scratch(n_g),
        compiler_params=_cp(("arbitrary", "arbitrary")),
    )(q_bwd, do_aug, q_bwd_t, do_aug_t, k_aug, v_aug, *gblocks)
    return res[:5], res[5:]


def _fgate_bwd(dqx, dkx, sgate, *, tm=256):
    t = sgate.shape[0]
    tm = min(tm, t)
    nsteps = t // tm

    def body(dq_ref, dk_ref, sg_ref, df_ref, dbf_ref, carry):
        @pl.when(pl.program_id(0) == 0)
        def _():
            carry[...] = jnp.zeros_like(carry)
            dbf_ref[...] = jnp.zeros_like(dbf_ref)
        lane = lax.broadcasted_iota(jnp.int32, (ATTN_W, F_PAD), 0)
        head = lax.broadcasted_iota(jnp.int32, (ATTN_W, F_PAD), 1)
        aux = (head // 2) * PAIR + HEAD_DIM * (1 - head % 2)
        valid = head < N_HEADS
        pick_r = (valid & (lane == aux + AUX_ROWSUM)).astype(F32)
        pick_c = (valid & (lane == aux + AUX_BIAS)).astype(F32)
        hp = lax.Precision.HIGHEST
        dcum = (jnp.dot(dq_ref[...], pick_r, preferred_element_type=F32, precision=lax.Precision.HIGH)
                + jnp.dot(dk_ref[...], pick_c, preferred_element_type=F32, precision=lax.Precision.HIGH))
        r = lax.broadcasted_iota(jnp.int32, (tm, tm), 0)
        c = lax.broadcasted_iota(jnp.int32, (tm, tm), 1)
        tri = (c >= r).astype(F32)
        rc = jnp.dot(tri, dcum, preferred_element_type=F32, precision=hp) + carry[...]
        carry[...] = rc[0:1, :]
        df = rc * sg_ref[...]
        df_ref[...] = df.astype(BF16)
        dbf_ref[...] += _rows8(df)

    rev = lambda i: nsteps - 1 - i
    return pl.pallas_call(
        body, name="fgate_bwd", grid=(nsteps,),
        in_specs=[pl.BlockSpec((tm, ATTN_W), lambda i: (rev(i), 0)), pl.BlockSpec((tm, ATTN_W), lambda i: (rev(i), 0)),
                  pl.BlockSpec((tm, F_PAD), lambda i: (rev(i), 0))],
        out_specs=[pl.BlockSpec((tm, F_PAD), lambda i: (rev(i), 0)), pl.BlockSpec((8, F_PAD), lambda i: (0, 0))],
        out_shape=[jax.ShapeDtypeStruct((t, F_PAD), BF16), jax.ShapeDtypeStruct((8, F_PAD), F32)],
        scratch_shapes=[pltpu.VMEM((1, F_PAD), F32)],
        compiler_params=_cp(("arbitrary",)),
    )(dqx, dkx, sgate)


def _inproj_bwd(dq, dk, dv, df, dbcx, dh, x, g, win_pt, gblock, *, tm=512):
    t, d = x.shape
    tm = min(tm, t)
    nsteps = t // tm
    n_qkv = 3 * ATTN_W

    def body(dq_ref, dk_ref, dv_ref, df_ref, db_ref, dh_ref, x_ref, g_ref, w_ref, gb_ref, gx_ref, dg_ref, land_ref,
             *sems):
        scatter = _Exchange([gb_ref], [land_ref], *sems, gather=False)

        @pl.when(pl.program_id(0) == 0)
        def _():
            scatter.start()
            dg_ref[...] = jnp.zeros_like(dg_ref)
        dn = _dot(df_ref[...], w_ref[n_qkv:n_qkv + F_PAD, :])
        for k, r in enumerate((dq_ref, dk_ref, dv_ref)):
            dn = dn + _dot(r[...], w_ref[k * ATTN_W:(k + 1) * ATTN_W, :])
        for k in range(3):
            c0 = n_qkv + F_PAD + k * CONV_CH
            dn = dn + _dot(db_ref[:, k * CONV_CH:(k + 1) * CONV_CH], w_ref[c0:c0 + CONV_CH, :])
        dx, dg = _norm_bwd(dn, x_ref[...], g_ref[...])
        gx_ref[...] = dh_ref[...] + dx
        dg_ref[...] += dg

        @pl.when(pl.program_id(0) == nsteps - 1)
        def _():
            scatter.wait()

    row = lambda n_: pl.BlockSpec((tm, n_), lambda i: (i, 0))
    return pl.pallas_call(
        body, name="inproj_bwd", grid=(nsteps,),
        in_specs=[row(ATTN_W), row(ATTN_W), row(ATTN_W), row(F_PAD), row(3 * CONV_CH), row(d), row(d),
                  _const_spec((1, d)), _const_spec(win_pt.shape), HBM_SPEC],
        out_specs=[row(d), pl.BlockSpec((8, d), lambda i: (0, 0)), HBM_SPEC],
        out_shape=[jax.ShapeDtypeStruct((t, d), F32), jax.ShapeDtypeStruct((8, d), F32),
                   jax.ShapeDtypeStruct(gblock.shape, gblock.dtype)],
        scratch_shapes=_Exchange.scratch(1),
        compiler_params=_cp(("arbitrary",)),
    )(dq, dk, dv, df, dbcx, dh, x, g, win_pt, gblock)


LATE = ("w_out_0", "w_up_0", "w_down_0", "pool_w_1", "w_up_1", "w_down_1")


def _local_step(x, target, gains, b_f, conv_w, pool_scale, win_pt, shards):
    d = x.shape[1]
    n0, qkv, flog, bcx, cv = _norm_inproj(x, gains["mix0"], win_pt, conv_w)
    q_aug_t, k_aug, v_aug, v_aug_t, sgate = _fgate_prep(flog, b_f, qkv)
    att, q_bwd, q_bwd_t, gathered = _attn_fwd(q_aug_t, k_aug, v_aug_t, [shards[n] for n in LATE])
    g = dict(zip(LATE, gathered))
    wout = g["w_out_0"].reshape(d, d)
    wup0, wup1 = g["w_up_0"], g["w_up_1"]
    wdown0, wdown1 = g["w_down_0"].reshape(-1, d), g["w_down_1"].reshape(-1, d)
    n_grp = len(POOL_WINDOWS)
    cg = d // n_grp
    poolw = g["pool_w_1"].reshape(N_DEV, n_grp, cg // N_DEV, cg).transpose(1, 0, 2, 3).reshape(n_grp, cg, cg)
    h1 = _outproj(att, cv, x, wout)
    h2, n1, a0, z0 = _mlp_fwd(h1, gains["ffn0"], wup0, wdown0, name="mlp_fwd0")
    h3, pooled = _pool_fwd(h2, gains["mix1"], poolw, pool_scale)
    loss, dh4, dg_final, n3, a1, z1 = _mlp_fwd_loss(h3, gains["ffn1"], wup1, wdown1, gains["final"], target,
                                                    name="mlp_fwd1")
    f = a1.shape[1]
    fb = f // N_DEV
    dh3, da1, dg_ffn1 = _mlp_bwd(dh4, h3, a1, gains["ffn1"], wup1, wdown1, name="mlp_bwd1")
    dwdown1 = _mm_tn(z1, dh4, name="dwdown1", ta=1024, tb=1024, tt=2048, out_dtype=BF16)
    dwup1 = _mm_tn(n3, da1, name="dwup1", ta=d, tb=fb, tt=4096, blocked_out=True, out_dtype=BF16)
    dh2, dpoolw, dscale, dg_mix1 = _pool_bwd(dh3, h2, pooled, gains["mix1"], poolw, pool_scale)
    dh1, da0, dg_ffn0 = _mlp_bwd(dh2, h1, a0, gains["ffn0"], wup0, wdown0, name="mlp_bwd0")
    dwdown0 = _mm_tn(z0, dh2, name="dwdown0", ta=1024, tb=1024, tt=2048, out_dtype=BF16)
    dwup0 = _mm_tn(n1, da0, name="dwup0", ta=d, tb=fb, tt=4096, blocked_out=True, out_dtype=BF16)
    do_aug, do_aug_t, dbcx, dconvw = _outproj_conv_bwd(dh1, att, wout, bcx, conv_w)
    dwout = _mm_tn_cat([att, cv], [dh1], name="dwout", tt=2048)
    gblocks = {
        "w_out_0": dwout.reshape(N_DEV, d // N_DEV, d), "w_up_0": dwup0, "w_up_1": dwup1,
        "w_down_0": dwdown0.reshape(N_DEV, -1, d), "w_down_1": dwdown1.reshape(N_DEV, -1, d),
        "pool_w_1": dpoolw.astype(BF16).reshape(n_grp, N_DEV, cg // N_DEV, cg).transpose(1, 0, 2, 3).reshape(
            N_DEV, n_grp * (cg // N_DEV), cg),
    }
    (dq, dqx, dk, dkx, dv), landed = _attn_bwd(q_bwd, do_aug, q_bwd_t, do_aug_t, k_aug, v_aug,
                                               [gblocks[n] for n in LATE])
    df, dbf = _fgate_bwd(dqx, dkx, sgate)
    dwin_t = jnp.concatenate(
        [_mm_tn_cat([dq, dk, dv], [n0], name="dwin_qkv", tt=2048),
         _mm_tn(df, n0, name="dwin_f", ta=F_PAD, tb=d, tt=2048, out_dtype=BF16)[:N_HEADS],
         _mm_tn(dbcx, n0, name="dwin_bcx", ta=512, tb=d, tt=4096, out_dtype=BF16)], axis=0)
    dwin_blocks = dwin_t.reshape(N_DEV, dwin_t.shape[0] // N_DEV, d)
    grad_x, dg_mix0, landed_win = _inproj_bwd(dq, dk, dv, df, dbcx, dh1, x, gains["mix0"], win_pt, dwin_blocks)
    small = dict(mix0=dg_mix0, ffn0=dg_ffn0, mix1=dg_mix1, pool_scale=dscale, ffn1=dg_ffn1, final=dg_final,
                 b_f=dbf, conv_w=dconvw)
    return loss, grad_x, dict(zip(LATE + ("w_in_0",), tuple(landed) + (landed_win,))), small


def _all_gather(shards):
    n = len(shards)

    def body(*refs):
        gather = _TwoLevelGather(refs[:n], refs[n:2 * n], *refs[2 * n:])
        gather.start()
        gather.forward()
        gather.wait()

    return pl.pallas_call(
        body, name="all_gather",
        in_specs=[HBM_SPEC] * n, out_specs=[HBM_SPEC] * n,
        out_shape=[jax.ShapeDtypeStruct((N_DEV,) + s.shape, s.dtype) for s in shards],
        scratch_shapes=[pltpu.SemaphoreType.DMA((7 * n,)), pltpu.SemaphoreType.DMA((7 * n,)),
                        pltpu.SemaphoreType.DMA((n,))],
    )(*shards)


SMALL_ROWS = 16


def _small_allreduce(parts):
    n, _, w = parts.shape
    assert n <= SMALL_ROWS

    def body(p_ref, o_ref, gath, send_sems, recv_sems):
        x, y, c = lax.axis_index("x"), lax.axis_index("y"), lax.axis_index("c")
        my = _slot(x, y, c)
        rows = [jnp.sum(p_ref[i], axis=0, keepdims=True) for i in range(n)]
        rows.append(jnp.zeros((SMALL_ROWS - n, w), F32))
        gath[my] = jnp.concatenate(rows, axis=0)
        copies = []
        for k in range(1, N_DEV):
            px, py, pc = x ^ (k >> 2), y ^ ((k >> 1) & 1), c ^ (k & 1)
            cp = pltpu.make_async_remote_copy(
                src_ref=gath.at[my], dst_ref=gath.at[my], send_sem=send_sems.at[k - 1], recv_sem=recv_sems.at[k - 1],
                device_id=(px, py, pc), device_id_type=MESH)
            cp.start()
            copies.append(cp)
        for cp in copies:
            cp.wait()
        acc = gath[0]
        for d in range(1, N_DEV):
            acc = acc + gath[d]
        o_ref[...] = acc

    return pl.pallas_call(
        body, name="small_allreduce",
        in_specs=[VMEM_SPEC], out_specs=VMEM_SPEC,
        out_shape=jax.ShapeDtypeStruct((SMALL_ROWS, w), F32),
        scratch_shapes=[pltpu.VMEM((N_DEV, SMALL_ROWS, w), F32), pltpu.SemaphoreType.DMA((N_DEV - 1,)),
                        pltpu.SemaphoreType.DMA((N_DEV - 1,))],
    )(parts)


def _adamw(g, w, m, v, *, name, tm=256):
    r, c = g.shape
    tm = tm if r % tm == 0 else r
    bc1 = 1.0 - ADAM_B1 ** ADAM_STEP
    bc2 = 1.0 - ADAM_B2 ** ADAM_STEP

    def body(g_ref, w_ref, m_ref, v_ref, d_ref, nm_ref, nv_ref):
        gv = g_ref[...]
        nm = ADAM_B1 * m_ref[...] + (1.0 - ADAM_B1) * gv
        nv = ADAM_B2 * v_ref[...] + (1.0 - ADAM_B2) * jnp.square(gv)
        nm_ref[...] = nm
        nv_ref[...] = nv
        d_ref[...] = -ADAM_LR * ((nm / bc1) / (jnp.sqrt(nv / bc2) + ADAM_EPS) + ADAM_WD * w_ref[...])

    blk = pl.BlockSpec((tm, c), lambda i: (i, 0))
    shp = jax.ShapeDtypeStruct((r, c), F32)
    return pl.pallas_call(
        body, name=name, grid=(r // tm,), in_specs=[blk] * 4, out_specs=[blk] * 3, out_shape=[shp] * 3,
        compiler_params=_cp(("parallel",)),
    )(g, w, m, v)


def _transpose_cast(a, *, name):
    def body(a_ref, o_ref):
        o_ref[...] = a_ref[...].T.astype(BF16)

    return pl.pallas_call(body, name=name, out_shape=jax.ShapeDtypeStruct(a.shape[::-1], BF16),
                          compiler_params=_cp())(a)


def _adamw_sum_t(parts, w, m, v, *, name):
    bc1 = 1.0 - ADAM_B1 ** ADAM_STEP
    bc2 = 1.0 - ADAM_B2 ** ADAM_STEP

    def body(p_ref, w_ref, m_ref, v_ref, g_ref, d_ref, nm_ref, nv_ref):
        acc = p_ref[0].astype(F32)
        for k in range(1, N_DEV):
            acc = acc + p_ref[k].astype(F32)
        gv = acc.T
        g_ref[...] = gv
        nm = ADAM_B1 * m_ref[...] + (1.0 - ADAM_B1) * gv
        nv = ADAM_B2 * v_ref[...] + (1.0 - ADAM_B2) * jnp.square(gv)
        nm_ref[...] = nm
        nv_ref[...] = nv
        d_ref[...] = -ADAM_LR * ((nm / bc1) / (jnp.sqrt(nv / bc2) + ADAM_EPS) + ADAM_WD * w_ref[...])

    shp = jax.ShapeDtypeStruct(w.shape, F32)
    return pl.pallas_call(body, name=name, out_shape=[shp] * 4, compiler_params=_cp())(parts, w, m, v)


def _adamw_sum(parts, w, m, v, *, name, tm=256):
    _, r, c = parts.shape
    tm = tm if r % tm == 0 else r
    bc1 = 1.0 - ADAM_B1 ** ADAM_STEP
    bc2 = 1.0 - ADAM_B2 ** ADAM_STEP

    def body(p_ref, w_ref, m_ref, v_ref, g_ref, d_ref, nm_ref, nv_ref):
        gv = p_ref[0].astype(F32)
        for k in range(1, N_DEV):
            gv = gv + p_ref[k].astype(F32)
        g_ref[...] = gv
        nm = ADAM_B1 * m_ref[...] + (1.0 - ADAM_B1) * gv
        nv = ADAM_B2 * v_ref[...] + (1.0 - ADAM_B2) * jnp.square(gv)
        nm_ref[...] = nm
        nv_ref[...] = nv
        d_ref[...] = -ADAM_LR * ((nm / bc1) / (jnp.sqrt(nv / bc2) + ADAM_EPS) + ADAM_WD * w_ref[...])

    blk = pl.BlockSpec((tm, c), lambda i: (i, 0))
    shp = jax.ShapeDtypeStruct((r, c), F32)
    return pl.pallas_call(
        body, name=name, grid=(r // tm,), in_specs=[pl.BlockSpec((N_DEV, tm, c), lambda i: (0, i, 0))] + [blk] * 3,
        out_specs=[blk] * 4, out_shape=[shp] * 4, compiler_params=_cp(("parallel",)),
    )(parts, w, m, v)


BIG = ("w_in_0", "w_out_0", "w_up_0", "w_down_0", "pool_w_1", "w_up_1", "w_down_1")
SMALL = ("norm_mix_0", "norm_ffn_0", "norm_mix_1", "pool_scale_1", "norm_ffn_1", "final_norm", "b_f_0", "conv_w_0")
WEIGHTS = ("norm_mix_0", "w_in_0", "b_f_0", "conv_w_0", "w_out_0", "norm_ffn_0", "w_up_0", "w_down_0", "norm_mix_1",
           "pool_w_1", "pool_scale_1", "norm_ffn_1", "w_up_1", "w_down_1", "final_norm")


def _pad_to(a, rows, cols):
    return jnp.pad(a, ((0, rows - a.shape[0]), (0, cols - a.shape[1])))


def _pack_small(p, width):
    rows = [p[n].reshape(1, -1) for n in SMALL[:6]]
    rows.append(_pad_to(p["b_f_0"].reshape(1, -1), 1, width))
    rows.append(_pad_to(p["conv_w_0"], 3, width))
    return _pad_to(jnp.concatenate(rows, axis=0), SMALL_ROWS, width)


def _unpack_small(a, like):
    out = {n: a[i] for i, n in enumerate(SMALL[:6])}
    out["b_f_0"] = a[6, :like["b_f_0"].shape[0]]
    out["conv_w_0"] = a[7:10, :like["conv_w_0"].shape[1]]
    return out


def kernel(x, norm_mix_0, w_in_0, b_f_0, conv_w_0, w_out_0, norm_ffn_0, w_up_0, w_down_0, norm_mix_1, pool_w_1, pool_scale_1, norm_ffn_1, w_up_1, w_down_1, final_norm, loss_target, m_norm_mix_0, m_w_in_0, m_b_f_0, m_conv_w_0, m_w_out_0, m_norm_ffn_0, m_w_up_0, m_w_down_0, m_norm_mix_1, m_pool_w_1, m_pool_scale_1, m_norm_ffn_1, m_w_up_1, m_w_down_1, m_final_norm, v_norm_mix_0, v_w_in_0, v_b_f_0, v_conv_w_0, v_w_out_0, v_norm_ffn_0, v_w_up_0, v_w_down_0, v_norm_mix_1, v_pool_w_1, v_pool_scale_1, v_norm_ffn_1, v_w_up_1, v_w_down_1, v_final_norm):
    w = dict(norm_mix_0=norm_mix_0, w_in_0=w_in_0, b_f_0=b_f_0, conv_w_0=conv_w_0, w_out_0=w_out_0,
             norm_ffn_0=norm_ffn_0, w_up_0=w_up_0, w_down_0=w_down_0, norm_mix_1=norm_mix_1, pool_w_1=pool_w_1,
             pool_scale_1=pool_scale_1, norm_ffn_1=norm_ffn_1, w_up_1=w_up_1, w_down_1=w_down_1, final_norm=final_norm)
    m = dict(norm_mix_0=m_norm_mix_0, w_in_0=m_w_in_0, b_f_0=m_b_f_0, conv_w_0=m_conv_w_0, w_out_0=m_w_out_0,
             norm_ffn_0=m_norm_ffn_0, w_up_0=m_w_up_0, w_down_0=m_w_down_0, norm_mix_1=m_norm_mix_1,
             pool_w_1=m_pool_w_1, pool_scale_1=m_pool_scale_1, norm_ffn_1=m_norm_ffn_1, w_up_1=m_w_up_1,
             w_down_1=m_w_down_1, final_norm=m_final_norm)
    v = dict(norm_mix_0=v_norm_mix_0, w_in_0=v_w_in_0, b_f_0=v_b_f_0, conv_w_0=v_conv_w_0, w_out_0=v_w_out_0,
             norm_ffn_0=v_norm_ffn_0, w_up_0=v_w_up_0, w_down_0=v_w_down_0, norm_mix_1=v_norm_mix_1,
             pool_w_1=v_pool_w_1, pool_scale_1=v_pool_scale_1, norm_ffn_1=v_norm_ffn_1, w_up_1=v_w_up_1,
             w_down_1=v_w_down_1, final_norm=v_final_norm)
    d = x.shape[-1]
    n_in = w_in_0.shape[1] * N_DEV
    n_qkv = 3 * ATTN_W
    pool_g, pool_rows, pool_c = pool_w_1.shape

    def shard2d(p):
        return {n: (p[n].reshape(pool_g * pool_rows, pool_c) if n == "pool_w_1" else p[n]) for n in BIG}
    w2, m2, v2 = shard2d(w), shard2d(m), shard2d(v)

    conv_cols = conv_w_0.shape[1]
    win_g8, conv_g8 = _all_gather([_transpose_cast(w_in_0, name="w_in_t"), _pad_to(conv_w_0, 8, 128)])
    conv_full = conv_g8[:, :, :conv_cols].transpose(1, 0, 2).reshape(8, N_DEV * conv_cols)
    win_t = win_g8.reshape(n_in, d)
    win_pt = jnp.concatenate([win_t[:n_qkv], _pad_to(win_t[n_qkv:n_qkv + N_HEADS], F_PAD, d),
                              win_t[n_qkv + N_HEADS:]], axis=0)

    gains = dict(mix0=norm_mix_0.reshape(1, d), ffn0=norm_ffn_0.reshape(1, d), mix1=norm_mix_1.reshape(1, d),
                 ffn1=norm_ffn_1.reshape(1, d), final=final_norm.reshape(1, d))
    dev = _slot(lax.axis_index("x"), lax.axis_index("y"), lax.axis_index("c"))
    loss8, grad_x, landed, small = _local_step(
        x[0], loss_target[0], gains, _pad_to(b_f_0.reshape(1, -1), 1, F_PAD), conv_full, pool_scale_1.reshape(1, d),
        win_pt, {n: w2[n].astype(BF16) for n in LATE})
    parts = jnp.concatenate(
        [small[k][None] for k in ("mix0", "ffn0", "mix1", "pool_scale", "ffn1", "final")]
        + [_pad_to(small["b_f"], 8, d)[None], jnp.pad(small["conv_w"], ((0, 0), (0, 0), (0, d - CONV_CH))),
           _pad_to(loss8[0:1, 0:1], 8, d)[None]], axis=0)
    tot = _small_allreduce(parts)
    loss = tot[10, 0]
    conv_g = lax.dynamic_slice(tot, (7, dev * conv_cols), (3, conv_cols))
    gs = tot.at[7:10].set(_pad_to(conv_g, 3, d))

    grads, deltas, new_m, new_v = {}, {}, {}, {}
    for n in BIG:
        if n in LATE:
            gr, dl, nm, nv = _adamw_sum(landed[n], w2[n], m2[n], v2[n], name="adamw_" + n)
        else:
            gr, dl, nm, nv = _adamw_sum_t(landed[n], w2[n], m2[n], v2[n], name="adamw_" + n)
        for dst, val in ((grads, gr), (deltas, dl), (new_m, nm), (new_v, nv)):
            dst[n] = val.reshape(w[n].shape)
    dl, nm, nv = _adamw(gs, _pack_small(w, d), _pack_small(m, d), _pack_small(v, d), name="adamw_small")
    for dst, val in ((grads, gs), (deltas, dl), (new_m, nm), (new_v, nv)):
        dst.update(_unpack_small(val, w))
    return (loss, grad_x[None], *[grads[n] for n in WEIGHTS], *[deltas[n] for n in WEIGHTS],
            *[new_m[n] for n in WEIGHTS], *[new_v[n] for n in WEIGHTS])
```

```python
import functools

import jax
import jax.numpy as jnp
from jax import lax
from jax.experimental import pallas as pl
from jax.experimental.pallas import tpu as pltpu

F32 = jnp.float32
BF16 = jnp.bfloat16

N_DEV = 8
N_HEADS = 8
HEAD_DIM = 64
PAIR = 2 * HEAD_DIM
ATTN_W = N_HEADS * HEAD_DIM
CONV_CH = 512
F_PAD = 128
POOL_WINDOWS = (2, 4, 8, 16)
POOL_HALO = 16
CONV_HALO = 16
RMS_EPS = 1e-6
Q_SCALE = HEAD_DIM ** -0.5
LOG2E = 1.4426950408889634
NEG = -1e30
AUX_BIAS = 0
AUX_LSE = 3
AUX_ROWSUM = 6
ADAM_LR, ADAM_B1, ADAM_B2, ADAM_EPS, ADAM_WD, ADAM_STEP = 0.001, 0.9, 0.999, 1e-08, 0.01, 10
MESH = pl.DeviceIdType.MESH
VMEM_LIMIT = 56 * 2**20


def _cp(sem=None, vmem=VMEM_LIMIT, **kw):
    return pltpu.CompilerParams(dimension_semantics=sem, vmem_limit_bytes=vmem, **kw)


def _dot(a, b):
    return jnp.dot(a, b, preferred_element_type=F32)


def _dot_nt(a, b):
    return lax.dot_general(a, b, (((1,), (1,)), ((), ())), preferred_element_type=F32)


def _dot_tn(a, b):
    return lax.dot_general(a, b, (((0,), (0,)), ((), ())), preferred_element_type=F32)


def _rstd(h):
    return lax.rsqrt(jnp.mean(h * h, axis=-1, keepdims=True) + RMS_EPS)


def _rows8(x):
    r, n = x.shape
    return jnp.sum(x.reshape(r // 8, 8, n), axis=0)


def _norm_bwd(dn, h, g):
    r = _rstd(h)
    xhat = h * r
    dy = dn * g
    dh = r * (dy - xhat * jnp.mean(dy * xhat, axis=-1, keepdims=True))
    return dh, _rows8(dn * xhat)


def _const_spec(shape):
    nd = len(shape)
    return pl.BlockSpec(shape, lambda *_: (0,) * nd, pipeline_mode=pl.Buffered(1))


HBM_SPEC = pl.BlockSpec(memory_space=pltpu.HBM)
VMEM_SPEC = pl.BlockSpec(memory_space=pltpu.VMEM)


def _slot(px, py, pc):
    return 4 * px + 2 * py + pc


class _Exchange:
    def __init__(self, srcs, dsts, send_sems, recv_sems, local_sems, gather):
        x, y, c = lax.axis_index("x"), lax.axis_index("y"), lax.axis_index("c")
        me = _slot(x, y, c)
        self.copies = []
        for a, (src, dst) in enumerate(zip(srcs, dsts)):
            self.copies.append(pltpu.make_async_copy(src if gather else src.at[me], dst.at[me], local_sems.at[a]))
            for k in range(1, N_DEV):
                px, py, pc = x ^ (k >> 2), y ^ ((k >> 1) & 1), c ^ (k & 1)
                self.copies.append(pltpu.make_async_remote_copy(
                    src_ref=src if gather else src.at[_slot(px, py, pc)], dst_ref=dst.at[me],
                    send_sem=send_sems.at[(N_DEV - 1) * a + k - 1], recv_sem=recv_sems.at[(N_DEV - 1) * a + k - 1],
                    device_id=(px, py, pc), device_id_type=MESH))

    def start(self):
        for cp in self.copies:
            cp.start()

    def wait(self):
        for cp in self.copies:
            cp.wait()

    @staticmethod
    def scratch(n):
        return [pltpu.SemaphoreType.DMA(((N_DEV - 1) * n,)), pltpu.SemaphoreType.DMA(((N_DEV - 1) * n,)),
                pltpu.SemaphoreType.DMA((n,))]


def _mesh_places():
    x, y, c = lax.axis_index("x"), lax.axis_index("y"), lax.axis_index("c")
    chips = [(1 - x, y), (x, 1 - y), (1 - x, 1 - y)]
    return (x, y, c), (x, y, 1 - c), chips


class _TwoLevelGather:
    def __init__(self, srcs, dsts, send_sems, recv_sems, local_sems):
        me, sib, chips = _mesh_places()
        c = me[2]
        n = len(srcs)

        def copy(a, k, block, to, src=None):
            dst = dsts[a].at[_slot(*block)]
            return pltpu.make_async_remote_copy(
                src_ref=dst if src is None else src, dst_ref=dst, send_sem=send_sems.at[7 * a + k],
                recv_sem=recv_sems.at[7 * a + k], device_id=to, device_id_type=MESH)

        self.mine = [pltpu.make_async_copy(srcs[a], dsts[a].at[_slot(*me)], local_sems.at[a]) for a in range(n)]
        self.first, self.landed, self.passed, self.rest = [], [], [], []
        for a in range(n):
            self.first.append(copy(a, 0, me, sib, src=srcs[a]))
            self.first += [copy(a, 1 + j, me, (*chip, c), src=srcs[a]) for j, chip in enumerate(chips)]
            self.landed += [copy(a, 1 + j, (*chip, c), me) for j, chip in enumerate(chips)]
            self.passed += [copy(a, 4 + j, (*chip, c), sib) for j, chip in enumerate(chips)]
            self.rest.append(copy(a, 0, sib, me))
            self.rest += [copy(a, 4 + j, (*chip, 1 - c), me) for j, chip in enumerate(chips)]

    def start(self):
        for cp in self.mine + self.first:
            cp.start()

    def forward(self):
        for arrived, onward in zip(self.landed, self.passed):
            arrived.wait_recv()
            onward.start()

    def wait(self):
        for cp in self.rest:
            cp.wait_recv()
        for cp in self.first + self.passed:
            cp.wait_send()
        for cp in self.mine:
            cp.wait()


def _norm_inproj(x, g, win_pt, conv_w, *, tm=512):
    t, d = x.shape
    n_all = win_pt.shape[0]
    n_qkv = 3 * ATTN_W
    n_bcx = 3 * CONV_CH
    assert n_all == n_qkv + F_PAD + n_bcx
    tm = min(tm, t)
    ch = CONV_CH

    def body(x_ref, g_ref, w_ref, cw_ref, n_ref, qkv_ref, f_ref, bcx_ref, cv_ref, ext):
        h = x_ref[...]
        n = (h * _rstd(h) * g_ref[...]).astype(BF16)
        n_ref[...] = n
        for c0 in range(0, n_qkv, 512):
            acc = _dot_nt(n, w_ref[c0:c0 + 512, :])
            if c0 < ATTN_W:
                acc = acc * (Q_SCALE * LOG2E)
            qkv_ref[:, c0:c0 + 512] = acc.astype(BF16)
        f_ref[...] = _dot_nt(n, w_ref[n_qkv:n_qkv + F_PAD, :])
        bcx = []
        for k in range(3):
            c0 = n_qkv + F_PAD + k * ch
            v = _dot_nt(n, w_ref[c0:c0 + ch, :]).astype(BF16)
            bcx_ref[:, k * ch:(k + 1) * ch] = v
            bcx.append(v.astype(F32))
        @pl.when(pl.program_id(0) == 0)
        def _():
            ext[tm:tm + CONV_HALO, :] = jnp.zeros((CONV_HALO, ch), F32)
        ext[0:CONV_HALO, :] = ext[tm:tm + CONV_HALO, :]
        ext[CONV_HALO:CONV_HALO + tm, :] = bcx[1] * bcx[2]
        conv = (cw_ref[0:1, :] * ext[CONV_HALO - 2:CONV_HALO - 2 + tm, :]
                + cw_ref[1:2, :] * ext[CONV_HALO - 1:CONV_HALO - 1 + tm, :]
                + cw_ref[2:3, :] * ext[CONV_HALO:CONV_HALO + tm, :])
        cv_ref[...] = (bcx[0] * conv).astype(BF16)

    return pl.pallas_call(
        body, name="norm_inproj", grid=(t // tm,),
        in_specs=[pl.BlockSpec((tm, d), lambda i: (i, 0)), _const_spec((1, d)), _const_spec((n_all, d)),
                  _const_spec((8, ch))],
        out_specs=[pl.BlockSpec((tm, d), lambda i: (i, 0)), pl.BlockSpec((tm, n_qkv), lambda i: (i, 0)),
                   pl.BlockSpec((tm, F_PAD), lambda i: (i, 0)), pl.BlockSpec((tm, n_bcx), lambda i: (i, 0)),
                   pl.BlockSpec((tm, ch), lambda i: (i, 0))],
        out_shape=[jax.ShapeDtypeStruct((t, d), BF16), jax.ShapeDtypeStruct((t, n_qkv), BF16),
                   jax.ShapeDtypeStruct((t, F_PAD), F32), jax.ShapeDtypeStruct((t, n_bcx), BF16),
                   jax.ShapeDtypeStruct((t, ch), BF16)],
        scratch_shapes=[pltpu.VMEM((CONV_HALO + tm, ch), F32)],
        compiler_params=_cp(("arbitrary",)),
    )(x, g, win_pt, conv_w)


def _head_lanes(h):
    lane = lax.broadcasted_iota(jnp.int32, (1, PAIR), 1)
    hh = h % 2
    return lane, lane // HEAD_DIM == hh, HEAD_DIM * (1 - hh)


def _pieces(col):
    hi = col.astype(BF16).astype(F32)
    r1 = col - hi
    mid = r1.astype(BF16).astype(F32)
    lo = (r1 - mid).astype(BF16).astype(F32)
    return hi, mid, lo


def _put_pieces(lane, first, col, other):
    hi, mid, lo = _pieces(col)
    return jnp.where(lane == first, hi, jnp.where(lane == first + 1, mid, jnp.where(lane == first + 2, lo, other)))


def _fgate_prep(flog, b_f, qkv, *, tm=512):
    t = flog.shape[0]
    tm = min(tm, t)

    def body(f_ref, b_ref, qkv_ref, qat_ref, ka_ref, vat_ref, sg_ref, carry):
        @pl.when(pl.program_id(0) == 0)
        def _():
            carry[...] = jnp.zeros_like(carry)
        z = f_ref[...] + b_ref[...]
        e = jnp.exp(-jnp.abs(z))
        logf = jnp.minimum(z, 0.0) - jnp.log(1.0 + e)
        sg_ref[...] = jnp.where(z >= 0, e, 1.0) / (1.0 + e)
        r = lax.broadcasted_iota(jnp.int32, (tm, tm), 0)
        c = lax.broadcasted_iota(jnp.int32, (tm, tm), 1)
        tri = (c <= r).astype(F32)
        cs = jnp.dot(tri, logf, preferred_element_type=F32, precision=lax.Precision.HIGHEST) + carry[...]
        carry[...] = cs[tm - 1:tm, :]
        cs2 = cs * LOG2E
        for h in range(N_HEADS):
            lane, head, aux = _head_lanes(h)
            p0 = (h // 2) * PAIR
            ones = ((lane >= aux + AUX_LSE) & (lane <= aux + AUX_ROWSUM)).astype(F32)
            bias = (lane >= aux + AUX_BIAS) & (lane < aux + AUX_BIAS + 3)
            k_aux = _put_pieces(lane, aux + AUX_BIAS, cs2[:, h:h + 1], ones)
            q_aug = jnp.where(head, qkv_ref[:, p0:p0 + PAIR], jnp.where(bias, -1.0, 0.0).astype(BF16))
            v_aug = jnp.where(head, qkv_ref[:, 2 * ATTN_W + p0:2 * ATTN_W + p0 + PAIR],
                              jnp.where(bias, 1.0, 0.0).astype(BF16))
            qat_ref[h] = q_aug.T
            ka_ref[h] = jnp.where(head, qkv_ref[:, ATTN_W + p0:ATTN_W + p0 + PAIR], k_aux.astype(BF16))
            vat_ref[h] = v_aug.T

    aug = lambda: pl.BlockSpec((N_HEADS, tm, PAIR), lambda i: (0, i, 0))
    aug_t = lambda: pl.BlockSpec((N_HEADS, PAIR, tm), lambda i: (0, 0, i))
    aug_shape = jax.ShapeDtypeStruct((N_HEADS, t, PAIR), BF16)
    aug_t_shape = jax.ShapeDtypeStruct((N_HEADS, PAIR, t), BF16)
    return pl.pallas_call(
        body, name="fgate_prep", grid=(t // tm,),
        in_specs=[pl.BlockSpec((tm, F_PAD), lambda i: (i, 0)), _const_spec((1, F_PAD)),
                  pl.BlockSpec((tm, 3 * ATTN_W), lambda i: (i, 0))],
        out_specs=[aug_t(), aug(), aug_t(), pl.BlockSpec((tm, F_PAD), lambda i: (i, 0))],
        out_shape=[aug_t_shape, aug_shape, aug_t_shape, jax.ShapeDtypeStruct((t, F_PAD), F32)],
        scratch_shapes=[pltpu.VMEM((1, F_PAD), F32)],
        compiler_params=_cp(("arbitrary",)),
    )(flog, b_f, qkv)


def _put_pieces_t(row, first, vec, other):
    hi, mid, lo = _pieces(vec)
    return jnp.where(row == first, hi, jnp.where(row == first + 1, mid, jnp.where(row == first + 2, lo, other)))


def _attn_fwd(q_aug_t, k_aug, v_aug_t, shards, *, tq=1024):
    t = k_aug.shape[1]
    tq = min(tq, t)
    tk = tq // 2
    nq = t // tq
    n_pairs = ATTN_W // PAIR
    n_sh = len(shards)
    forward_step = (11 * n_pairs * nq) // 16

    def body(qt_ref, k_ref, vt_ref, *rest):
        o_ref, qb_ref, qbt_ref = rest[n_sh:n_sh + 3]
        s_scr = rest[2 * n_sh + 3]
        gather = _TwoLevelGather(rest[:n_sh], rest[n_sh + 3:2 * n_sh + 3], *rest[2 * n_sh + 4:])
        i = pl.program_id(1)
        step = pl.program_id(0) * nq + i

        @pl.when(step == 0)
        def _():
            gather.start()

        @pl.when(step == forward_step)
        def _():
            gather.forward()
        key = lax.broadcasted_iota(jnp.int32, (tk, tq), 0)
        qry = lax.broadcasted_iota(jnp.int32, (tk, tq), 1)
        qt = [qt_ref[0], qt_ref[1]]

        def logits(hh, tile, slot, diag):
            s = _dot(k_ref[hh, pl.ds(pl.multiple_of(tile * tk, tk), tk), :], qt[hh])
            if diag:
                s = jnp.where(key + (tile * tk - i * tq) <= qry, s, NEG)
            s_scr[hh, slot] = s
            return jnp.max(s, axis=0, keepdims=True)

        def probs(hh, tile, slot, m, acc, tmax):
            mn = jnp.maximum(m, tmax)
            p = jnp.exp2(s_scr[hh, slot] - mn).astype(BF16)
            acc = jnp.exp2(m - mn) * acc + _dot(vt_ref[hh, :, pl.ds(pl.multiple_of(tile * tk, tk), tk)], p)
            return mn, acc

        def advance(carry, prev, slot, nxt, diag=False):
            out = []
            for hh in range(2):
                m, acc, tmax = carry[hh]
                m, acc = probs(hh, prev, slot, m, acc, tmax)
                out.append((m, acc, logits(hh, nxt, 1 - slot, diag)))
            return tuple(out)

        def two_tiles(jj, carry):
            carry = advance(carry, jnp.where(jj == 0, 2 * i, 2 * jj - 1), 1, 2 * jj)
            return advance(carry, 2 * jj, 0, 2 * jj + 1)

        init = tuple((jnp.full((1, tq), NEG, F32), jnp.zeros((PAIR, tq), F32), logits(hh, 2 * i + 1, 0, True))
                     for hh in range(2))
        carry = advance(init, 2 * i + 1, 0, 2 * i, diag=True)
        carry = lax.fori_loop(0, i // 2, lambda jj, c: two_tiles(2 * jj + 1, two_tiles(2 * jj, c)), carry)
        carry = lax.cond(i % 2 == 1, lambda c: two_tiles(i - 1, c), lambda c: c, carry)
        last = jnp.where(i == 0, 2 * i, 2 * i - 1)
        row = lax.broadcasted_iota(jnp.int32, (PAIR, 1), 0)
        res = []
        for hh in range(2):
            aux = HEAD_DIM * (1 - hh)
            m, acc, tmax = carry[hh]
            m, acc = probs(hh, last, 1, m, acc, tmax)
            l = acc[aux + AUX_BIAS:aux + AUX_BIAS + 1, :]
            qbt = _put_pieces_t(row, aux + AUX_LSE, -(m + jnp.log2(l)), qt[hh].astype(F32))
            qbt_ref[hh] = qbt.astype(BF16)
            qb_ref[hh] = qbt.astype(BF16).T
            res.append(acc * (1.0 / l))
        o_ref[...] = jnp.where(row < HEAD_DIM, res[0], res[1]).astype(BF16).T

        @pl.when((pl.program_id(0) == n_pairs - 1) & (i == nq - 1))
        def _():
            gather.wait()

    res = pl.pallas_call(
        body, name="attn_fwd", grid=(n_pairs, nq),
        in_specs=[pl.BlockSpec((2, PAIR, tq), lambda p, i: (p, 0, i)),
                  pl.BlockSpec((2, t, PAIR), lambda p, i: (p, 0, 0), pipeline_mode=pl.Buffered(1)),
                  pl.BlockSpec((2, PAIR, t), lambda p, i: (p, 0, 0), pipeline_mode=pl.Buffered(1))] + [HBM_SPEC] * n_sh,
        out_specs=[pl.BlockSpec((tq, PAIR), lambda p, i: (i, p)),
                   pl.BlockSpec((2, tq, PAIR), lambda p, i: (p, i, 0)),
                   pl.BlockSpec((2, PAIR, tq), lambda p, i: (p, 0, i))] + [HBM_SPEC] * n_sh,
        out_shape=[jax.ShapeDtypeStruct((t, ATTN_W), BF16), jax.ShapeDtypeStruct((N_HEADS, t, PAIR), BF16),
                   jax.ShapeDtypeStruct((N_HEADS, PAIR, t), BF16)]
        + [jax.ShapeDtypeStruct((N_DEV,) + s.shape, s.dtype) for s in shards],
        scratch_shapes=[pltpu.VMEM((2, 2, tk, tq), F32)] + _Exchange.scratch(n_sh),
        compiler_params=_cp(("arbitrary", "arbitrary")),
    )(q_aug_t, k_aug, v_aug_t, *shards)
    return res[0], res[1], res[2], res[3:]


def _prev_halo(tm, halo):
    return lambda i: (jnp.maximum(i * (tm // halo) - 1, 0), 0)


def _next_halo(tm, halo, t):
    return lambda i: (jnp.minimum((i + 1) * (tm // halo), t // halo - 1), 0)


def _mlp_tile(hh, g_ref, wu_ref, wd_ref, n_ref, a_ref, z_ref):
    n_blk, _, fb = wu_ref.shape
    n = (hh * _rstd(hh) * g_ref[...]).astype(BF16)
    n_ref[...] = n
    acc = hh
    for k in range(n_blk):
        a = _dot(n, wu_ref[k])
        zz = jnp.square(jnp.maximum(a, 0.0)).astype(BF16)
        a_ref[:, k * fb:(k + 1) * fb] = a.astype(BF16)
        z_ref[:, k * fb:(k + 1) * fb] = zz
        acc = acc + _dot(zz, wd_ref[k * fb:(k + 1) * fb, :])
    return acc


def _outproj(att, cv, x, wout, *, tm=512):
    t, d = x.shape
    tm = min(tm, t)

    def body(a_ref, c_ref, x_ref, w_ref, h_ref):
        h_ref[...] = x_ref[...] + _dot(a_ref[...], w_ref[0:ATTN_W, :]) + _dot(c_ref[...], w_ref[ATTN_W:, :])

    return pl.pallas_call(
        body, name="outproj", grid=(t // tm,),
        in_specs=[pl.BlockSpec((tm, ATTN_W), lambda i: (i, 0)), pl.BlockSpec((tm, CONV_CH), lambda i: (i, 0)),
                  pl.BlockSpec((tm, d), lambda i: (i, 0)), _const_spec(wout.shape)],
        out_specs=pl.BlockSpec((tm, d), lambda i: (i, 0)),
        out_shape=jax.ShapeDtypeStruct((t, d), F32),
        compiler_params=_cp(("parallel",)),
    )(att, cv, x, wout)


def _mlp_fwd(h, g, wup, wdown, *, name, tm=512):
    t, d = h.shape
    n_blk, _, fb = wup.shape
    f = n_blk * fb
    tm = min(tm, t)

    def body(h_ref, g_ref, wu_ref, wd_ref, ho_ref, n_ref, a_ref, z_ref):
        ho_ref[...] = _mlp_tile(h_ref[...], g_ref, wu_ref, wd_ref, n_ref, a_ref, z_ref)

    row = lambda n_: pl.BlockSpec((tm, n_), lambda i: (i, 0))
    return pl.pallas_call(
        body, name=name, grid=(t // tm,),
        in_specs=[row(d), _const_spec((1, d)), _const_spec(wup.shape), _const_spec(wdown.shape)],
        out_specs=[row(d), row(d), row(f), row(f)],
        out_shape=[jax.ShapeDtypeStruct((t, d), F32), jax.ShapeDtypeStruct((t, d), BF16),
                   jax.ShapeDtypeStruct((t, f), BF16), jax.ShapeDtypeStruct((t, f), BF16)],
        compiler_params=_cp(("parallel",)),
    )(h, g, wup, wdown)


def _mlp_fwd_loss(h, g, wup, wdown, g_out, target, *, name, tm=512):
    t, d = h.shape
    n_blk, _, fb = wup.shape
    f = n_blk * fb
    tm = min(tm, t)
    nsteps = t // tm

    def body(h_ref, g_ref, wu_ref, wd_ref, go_ref, y_ref, loss_ref, dh_ref, dg_ref, n_ref, a_ref, z_ref, lacc):
        i = pl.program_id(0)

        @pl.when(i == 0)
        def _():
            lacc[...] = jnp.zeros_like(lacc)
            dg_ref[...] = jnp.zeros_like(dg_ref)
        hv = _mlp_tile(h_ref[...], g_ref, wu_ref, wd_ref, n_ref, a_ref, z_ref)
        gv = go_ref[...]
        r = _rstd(hv)
        xhat = hv * r
        err = xhat * gv - y_ref[...]
        lacc[...] += _rows8(err * err)
        dout = err * (1.0 / d)
        dy = dout * gv
        dg_ref[...] += _rows8(dout * xhat)
        dh_ref[...] = r * (dy - xhat * jnp.mean(dy * xhat, axis=-1, keepdims=True))

        @pl.when(i == nsteps - 1)
        def _():
            loss_ref[...] = jnp.full(loss_ref.shape, (0.5 / d) * jnp.sum(lacc[...]), F32)

    row = lambda n_: pl.BlockSpec((tm, n_), lambda i: (i, 0))
    return pl.pallas_call(
        body, name=name, grid=(nsteps,),
        in_specs=[row(d), _const_spec((1, d)), _const_spec(wup.shape), _const_spec(wdown.shape), _const_spec((1, d)),
                  row(d)],
        out_specs=[pl.BlockSpec((8, 128), lambda i: (0, 0)), row(d), pl.BlockSpec((8, d), lambda i: (0, 0)),
                   row(d), row(f), row(f)],
        out_shape=[jax.ShapeDtypeStruct((8, 128), F32), jax.ShapeDtypeStruct((t, d), F32),
                   jax.ShapeDtypeStruct((8, d), F32), jax.ShapeDtypeStruct((t, d), BF16),
                   jax.ShapeDtypeStruct((t, f), BF16), jax.ShapeDtypeStruct((t, f), BF16)],
        scratch_shapes=[pltpu.VMEM((8, d), F32)],
        compiler_params=_cp(("arbitrary",)),
    )(h, g, wup, wdown, g_out, target)


def _pool_inv_count(i, tm):
    tglob = (i * tm + lax.broadcasted_iota(jnp.int32, (tm, 1), 0) + 1).astype(F32)
    return [1.0 / jnp.minimum(tglob, float(w)) for w in POOL_WINDOWS]


def _pool_fwd(h, g, poolw, scale, *, tm=512):
    t, d = h.shape
    tm = min(tm, t)
    cg = d // len(POOL_WINDOWS)

    def body(h_ref, hh_ref, g_ref, w_ref, s_ref, ho_ref, p_ref, ext):
        i = pl.program_id(0)
        hv = h_ref[...]
        halo = hh_ref[...]
        n = hv * _rstd(hv) * g_ref[...]
        ext[0:POOL_HALO, :] = jnp.where(i == 0, 0.0, halo * _rstd(halo) * g_ref[...])
        ext[POOL_HALO:POOL_HALO + tm, :] = n
        inv = _pool_inv_count(i, tm)
        for gi, w in enumerate(POOL_WINDOWS):
            cs = slice(gi * cg, (gi + 1) * cg)
            s = ext[POOL_HALO:POOL_HALO + tm, cs]
            for j in range(1, w):
                s = s + ext[POOL_HALO - j:POOL_HALO - j + tm, cs]
            pooled = (s * inv[gi] - n[:, cs]).astype(BF16)
            p_ref[:, cs] = pooled
            ho_ref[:, cs] = hv[:, cs] + _dot(pooled, w_ref[gi]) * s_ref[:, cs]

    row = lambda: pl.BlockSpec((tm, d), lambda i: (i, 0))
    return pl.pallas_call(
        body, name="pool_fwd", grid=(t // tm,),
        in_specs=[row(), pl.BlockSpec((POOL_HALO, d), _prev_halo(tm, POOL_HALO)), _const_spec((1, d)),
                  _const_spec(poolw.shape), _const_spec((1, d))],
        out_specs=[row(), row()],
        out_shape=[jax.ShapeDtypeStruct((t, d), F32), jax.ShapeDtypeStruct((t, d), BF16)],
        scratch_shapes=[pltpu.VMEM((POOL_HALO + tm, d), F32)],
        compiler_params=_cp(("parallel",)),
    )(h, h, g, poolw, scale)


def _mm_tn(a, b, *, name, ta, tb, tt, blocked_out=False, out_dtype=F32):
    t, ka = a.shape
    n = b.shape[1]
    ta, tb, tt = min(ta, ka), min(tb, n), min(tt, t)
    nt = t // tt

    def body(a_ref, b_ref, o_ref, acc):
        @pl.when(pl.program_id(2) == 0)
        def _():
            acc[...] = jnp.zeros_like(acc)
        acc[...] += _dot_tn(a_ref[...].astype(BF16), b_ref[...].astype(BF16))

        @pl.when(pl.program_id(2) == nt - 1)
        def _():
            o_ref[...] = acc[...].astype(out_dtype)

    if blocked_out:
        assert ta == ka
        out_shape = jax.ShapeDtypeStruct((n // tb, ka, tb), out_dtype)
        out_spec = pl.BlockSpec((None, ta, tb), lambda i, j, k: (j, i, 0))
    else:
        out_shape = jax.ShapeDtypeStruct((ka, n), out_dtype)
        out_spec = pl.BlockSpec((ta, tb), lambda i, j, k: (i, j))
    return pl.pallas_call(
        body, name=name, grid=(ka // ta, n // tb, nt),
        in_specs=[pl.BlockSpec((tt, ta), lambda i, j, k: (k, i)), pl.BlockSpec((tt, tb), lambda i, j, k: (k, j))],
        out_specs=out_spec, out_shape=out_shape, scratch_shapes=[pltpu.VMEM((ta, tb), F32)],
        compiler_params=_cp(("parallel", "parallel", "arbitrary")),
    )(a, b)


def _mm_tn_cat(a_list, b_list, *, name, tt, out_dtype=BF16):
    t = a_list[0].shape[0]
    ta, tb = a_list[0].shape[1], b_list[0].shape[1]
    na, nb = len(a_list), len(b_list)
    tt = min(tt, t)
    nt = t // tt

    def body(*refs):
        a_refs, b_refs, o_ref, acc = refs[:na], refs[na:na + nb], refs[na + nb], refs[na + nb + 1]
        i, j, k = pl.program_id(0), pl.program_id(1), pl.program_id(2)

        @pl.when(k == 0)
        def _():
            acc[...] = jnp.zeros_like(acc)
        for ia in range(na):
            for ib in range(nb):
                @pl.when((i == ia) & (j == ib))
                def _(ia=ia, ib=ib):
                    acc[...] += _dot_tn(a_refs[ia][...].astype(BF16), b_refs[ib][...].astype(BF16))

        @pl.when(k == nt - 1)
        def _():
            o_ref[...] = acc[...].astype(out_dtype)

    def held(m, axis):
        def index(i, j, k):
            cur = (i, j)[axis]
            return (jnp.where(cur == m, k, jnp.where(cur < m, 0, nt - 1)), 0)
        return index

    return pl.pallas_call(
        body, name=name, grid=(na, nb, nt),
        in_specs=[pl.BlockSpec((tt, ta), held(m, 0)) for m in range(na)]
        + [pl.BlockSpec((tt, tb), held(m, 1)) for m in range(nb)],
        out_specs=pl.BlockSpec((ta, tb), lambda i, j, k: (i, j)),
        out_shape=jax.ShapeDtypeStruct((na * ta, nb * tb), out_dtype), scratch_shapes=[pltpu.VMEM((ta, tb), F32)],
        compiler_params=_cp(("arbitrary", "arbitrary", "arbitrary")),
    )(*a_list, *b_list)


def _mlp_bwd(dho, h, a, g, wup, wdown, *, name, tm=512):
    t, d = h.shape
    n_blk, _, fb = wup.shape
    f = n_blk * fb
    tm = min(tm, t)

    def body(do_ref, h_ref, a_ref, g_ref, wu_ref, wd_ref, dh_ref, da_ref, dg_ref):
        @pl.when(pl.program_id(0) == 0)
        def _():
            dg_ref[...] = jnp.zeros_like(dg_ref)
        dho_v = do_ref[...]
        dob = dho_v.astype(BF16)
        dn = jnp.zeros((tm, d), F32)
        for k in range(n_blk):
            dz = _dot_nt(dob, wd_ref[k * fb:(k + 1) * fb, :])
            da = (dz * (2.0 * jnp.maximum(a_ref[:, k * fb:(k + 1) * fb].astype(F32), 0.0))).astype(BF16)
            da_ref[:, k * fb:(k + 1) * fb] = da
            dn = dn + _dot_nt(da, wu_ref[k])
        dh, dg = _norm_bwd(dn, h_ref[...], g_ref[...])
        dh_ref[...] = dho_v + dh
        dg_ref[...] += dg

    row = lambda n_: pl.BlockSpec((tm, n_), lambda i: (i, 0))
    return pl.pallas_call(
        body, name=name, grid=(t // tm,),
        in_specs=[row(d), row(d), row(f), _const_spec((1, d)), _const_spec(wup.shape), _const_spec(wdown.shape)],
        out_specs=[row(d), row(f), pl.BlockSpec((8, d), lambda i: (0, 0))],
        out_shape=[jax.ShapeDtypeStruct((t, d), F32), jax.ShapeDtypeStruct((t, f), BF16),
                   jax.ShapeDtypeStruct((8, d), F32)],
        compiler_params=_cp(("arbitrary",)),
    )(dho, h, a, g, wup, wdown)


def _pool_bwd(dho, h, pooled, g, poolw, scale, *, tm=512):
    t, d = h.shape
    tm = min(tm, t)
    ng = len(POOL_WINDOWS)
    cg = d // ng
    nsteps = t // tm

    def body(do_ref, dn_ref, h_ref, p_ref, g_ref, w_ref, s_ref, dh_ref, dw_ref, ds_ref, dg_ref, ext):
        i = pl.program_id(0)

        @pl.when(i == 0)
        def _():
            dw_ref[...] = jnp.zeros_like(dw_ref)
            ds_ref[...] = jnp.zeros_like(ds_ref)
            dg_ref[...] = jnp.zeros_like(dg_ref)
        dho_v = do_ref[...]
        sv = s_ref[...]
        dyp = (dho_v * sv).astype(BF16)
        dyp_halo = (dn_ref[...] * sv).astype(BF16)
        inv = _pool_inv_count(i, tm)
        tnext = ((i + 1) * tm + lax.broadcasted_iota(jnp.int32, (POOL_HALO, 1), 0) + 1).astype(F32)
        last = i == nsteps - 1
        ypre_parts, dpooled_parts = [], []
        for gi, w in enumerate(POOL_WINDOWS):
            cs = slice(gi * cg, (gi + 1) * cg)
            pg = p_ref[:, cs]
            ypre_parts.append(_dot(pg, w_ref[gi]))
            dw_ref[gi] += _dot_tn(pg, dyp[:, cs])
            dpool = _dot_nt(dyp[:, cs], w_ref[gi])
            dpooled_parts.append(dpool)
            ext[0:tm, cs] = dpool * inv[gi]
            dpool_halo = _dot_nt(dyp_halo[:, cs], w_ref[gi]) * (1.0 / jnp.minimum(tnext, float(w)))
            ext[tm:tm + POOL_HALO, cs] = jnp.where(last, 0.0, dpool_halo)
        ds_ref[...] += _rows8(dho_v * jnp.concatenate(ypre_parts, axis=1))
        dn_parts = []
        for gi, w in enumerate(POOL_WINDOWS):
            cs = slice(gi * cg, (gi + 1) * cg)
            s = ext[0:tm, cs]
            for j in range(1, w):
                s = s + ext[j:j + tm, cs]
            dn_parts.append(s - dpooled_parts[gi])
        dh, dg = _norm_bwd(jnp.concatenate(dn_parts, axis=1), h_ref[...], g_ref[...])
        dh_ref[...] = dho_v + dh
        dg_ref[...] += dg

    row = lambda: pl.BlockSpec((tm, d), lambda i: (i, 0))
    acc8 = lambda: pl.BlockSpec((8, d), lambda i: (0, 0))
    return pl.pallas_call(
        body, name="pool_bwd", grid=(nsteps,),
        in_specs=[row(), pl.BlockSpec((POOL_HALO, d), _next_halo(tm, POOL_HALO, t)), row(), row(),
                  _const_spec((1, d)), _const_spec(poolw.shape), _const_spec((1, d))],
        out_specs=[row(), pl.BlockSpec((ng, cg, cg), lambda i: (0, 0, 0)), acc8(), acc8()],
        out_shape=[jax.ShapeDtypeStruct((t, d), F32), jax.ShapeDtypeStruct((ng, cg, cg), F32),
                   jax.ShapeDtypeStruct((8, d), F32), jax.ShapeDtypeStruct((8, d), F32)],
        scratch_shapes=[pltpu.VMEM((tm + POOL_HALO, d), F32)],
        compiler_params=_cp(("arbitrary",)),
    )(dho, dho, h, pooled, g, poolw, scale)


def _outproj_conv_bwd(dh, o, wout, bcx, conv_w, *, tm=512):
    t, d = dh.shape
    tm = min(tm, t)
    ch = CONV_CH
    nsteps = t // tm

    def body(dh_ref, o_ref, w_ref, b_ref, c_ref, x_ref, hc_ref, hx_ref, cw_ref,
             da_ref, dat_ref, db_ref, dw_ref, ext_u, ext_d):
        s = pl.program_id(0)

        @pl.when(s == 0)
        def _():
            dw_ref[...] = jnp.zeros_like(dw_ref)
            ext_d[0:CONV_HALO, :] = jnp.zeros((CONV_HALO, ch), F32)
        dhb = dh_ref[...].astype(BF16)
        for p in range(ATTN_W // PAIR):
            datt = _dot_nt(dhb, w_ref[p * PAIR:(p + 1) * PAIR, :])
            prod = datt * o_ref[:, p * PAIR:(p + 1) * PAIR].astype(F32)
            for hh in range(2):
                lane, head, aux = _head_lanes(hh)
                delta = jnp.sum(jnp.where(head, prod, 0.0), axis=1, keepdims=True)
                aug = _put_pieces(lane, aux + AUX_BIAS, -delta, jnp.where(head, datt, 0.0))
                da_ref[2 * p + hh] = aug.astype(BF16)
                dat_ref[2 * p + hh] = aug.astype(BF16).T
        dcv = _dot_nt(dhb, w_ref[ATTN_W:, :])
        b, c, x = b_ref[...].astype(F32), c_ref[...].astype(F32), x_ref[...].astype(F32)
        ext_u[0:CONV_HALO, :] = jnp.where(s == nsteps - 1, 0.0, hc_ref[...].astype(F32) * hx_ref[...].astype(F32))
        ext_u[CONV_HALO:CONV_HALO + tm, :] = c * x
        dconv = dcv * b
        ext_d[tm:tm + CONV_HALO, :] = ext_d[0:CONV_HALO, :]
        ext_d[0:tm, :] = dconv
        u = [ext_u[CONV_HALO - 2 + k:CONV_HALO - 2 + k + tm, :] for k in range(3)]
        conv = cw_ref[0:1, :] * u[0] + cw_ref[1:2, :] * u[1] + cw_ref[2:3, :] * u[2]
        du = (cw_ref[2:3, :] * dconv + cw_ref[1:2, :] * ext_d[1:1 + tm, :] + cw_ref[0:1, :] * ext_d[2:2 + tm, :])
        db_ref[:, 0:ch] = (dcv * conv).astype(BF16)
        db_ref[:, ch:2 * ch] = (du * x).astype(BF16)
        db_ref[:, 2 * ch:3 * ch] = (du * c).astype(BF16)
        for k in range(3):
            dw_ref[k] += _rows8(dconv * u[k])

    rev = lambda s: nsteps - 1 - s
    row = lambda n_: pl.BlockSpec((tm, n_), lambda s: (rev(s), 0))
    col = lambda k: pl.BlockSpec((tm, ch), lambda s: (rev(s), k))
    prev = lambda k: pl.BlockSpec((CONV_HALO, ch), lambda s: (_prev_halo(tm, CONV_HALO)(rev(s))[0], k))
    return pl.pallas_call(
        body, name="outproj_conv_bwd", grid=(nsteps,),
        in_specs=[row(d), row(ATTN_W), _const_spec(wout.shape), col(0), col(1), col(2), prev(1), prev(2),
                  _const_spec((8, ch))],
        out_specs=[pl.BlockSpec((N_HEADS, tm, PAIR), lambda s: (0, rev(s), 0)),
                   pl.BlockSpec((N_HEADS, PAIR, tm), lambda s: (0, 0, rev(s))),
                   row(3 * ch), pl.BlockSpec((3, 8, ch), lambda s: (0, 0, 0))],
        out_shape=[jax.ShapeDtypeStruct((N_HEADS, t, PAIR), BF16), jax.ShapeDtypeStruct((N_HEADS, PAIR, t), BF16),
                   jax.ShapeDtypeStruct((t, 3 * ch), BF16), jax.ShapeDtypeStruct((3, 8, ch), F32)],
        scratch_shapes=[pltpu.VMEM((CONV_HALO + tm, ch), F32), pltpu.VMEM((tm + CONV_HALO, ch), F32)],
        compiler_params=_cp(("arbitrary",)),
    )(dh, o, wout, bcx, bcx, bcx, bcx, bcx, conv_w)


def _attn_bwd(q_bwd, do_aug, q_bwd_t, do_aug_t, k_aug, v_aug_t, gblocks, *, tq=1024):
    t = q_bwd.shape[1]
    tq = min(tq, t)
    tk = tq // 2
    nq, nk = t // tq, t // tk
    n_pairs = ATTN_W // PAIR
    n_g = len(gblocks)

    def body(q_ref, do_ref, qt_ref, dot_ref, k_ref, vt_ref, *rest):
        dq_ref, dqx_ref, dk_ref, dkx_ref, dv_ref = rest[n_g:n_g + 5]
        dq_scr = rest[2 * n_g + 5]
        scatter = _Exchange(rest[:n_g], rest[n_g + 5:2 * n_g + 5], *rest[2 * n_g + 6:], gather=False)
        j = pl.program_id(1)

        @pl.when((pl.program_id(0) == 0) & (j == 0))
        def _():
            scatter.start()

        @pl.when(j == 0)
        def _():
            dq_scr[...] = jnp.zeros_like(dq_scr)
        k = [k_ref[0], k_ref[1]]
        vt = [vt_ref[0], vt_ref[1]]

        def step(i, carry, diag, rows=tq, row0=0):
            qs = pl.multiple_of(i * tq + row0, tk)
            if diag:
                row = lax.broadcasted_iota(jnp.int32, (rows, tk), 0)
                col = lax.broadcasted_iota(jnp.int32, (rows, tk), 1)
            out = []
            for hh in range(2):
                dk_a, dv_a = carry[hh]
                q = q_ref[hh, pl.ds(qs, rows), :]
                dov = do_ref[hh, pl.ds(qs, rows), :]
                p = jnp.exp2(_dot_nt(q, k[hh]))
                if diag:
                    p = jnp.where(col + (j * tk - i * tq - row0) <= row, p, 0.0)
                ds = (p * _dot(dov, vt[hh])).astype(BF16)
                dv_a = dv_a + _dot(dot_ref[hh, :, pl.ds(qs, rows)], p.astype(BF16))
                dk_a = dk_a + _dot(qt_ref[hh, :, pl.ds(qs, rows)], ds)
                dq_scr[hh, pl.ds(qs, rows), :] += _dot(ds, k[hh])
                out.append((dk_a, dv_a))
            return tuple(out)

        zero = (jnp.zeros((PAIR, tk), F32), jnp.zeros((PAIR, tk), F32))
        carry = lax.cond(j % 2 == 0, lambda c: step(j // 2, c, True),
                         lambda c: step(j // 2, c, True, rows=tk, row0=tk), (zero, zero))
        full0 = j // 2 + 1
        odd = (nq - full0) % 2
        carry = lax.cond(odd == 1, lambda c: step(full0, c, False), lambda c: c, carry)
        (dk0, dv0), (dk1, dv1) = lax.fori_loop(
            0, (nq - full0) // 2, lambda ii, c: step(full0 + odd + 2 * ii, c, False, rows=2 * tq), carry)
        first_t = lax.broadcasted_iota(jnp.int32, (PAIR, 1), 0) < HEAD_DIM
        first = lax.broadcasted_iota(jnp.int32, (1, PAIR), 1) < HEAD_DIM
        dk_ref[...] = (jnp.where(first_t, dk0, dk1) * (1.0 / LOG2E)).astype(BF16).T
        dkx_ref[...] = jnp.where(first_t, dk1, dk0).T
        dv_ref[...] = jnp.where(first_t, dv0, dv1).astype(BF16).T

        @pl.when(j == nk - 1)
        def _():
            dq_ref[...] = (jnp.where(first, dq_scr[0], dq_scr[1]) * Q_SCALE).astype(BF16)
            dqx_ref[...] = jnp.where(first, dq_scr[1], dq_scr[0])

        @pl.when((pl.program_id(0) == n_pairs - 1) & (j == nk - 1))
        def _():
            scatter.wait()

    resident = lambda: pl.BlockSpec((2, t, PAIR), lambda p, j: (p, 0, 0), pipeline_mode=pl.Buffered(1))
    resident_t = lambda: pl.BlockSpec((2, PAIR, t), lambda p, j: (p, 0, 0), pipeline_mode=pl.Buffered(1))
    kv_in = lambda: pl.BlockSpec((2, tk, PAIR), lambda p, j: (p, j, 0))
    whole = lambda: pl.BlockSpec((t, PAIR), lambda p, j: (0, p))
    tile = lambda: pl.BlockSpec((tk, PAIR), lambda p, j: (j, p))
    b16 = jax.ShapeDtypeStruct((t, ATTN_W), BF16)
    f32 = jax.ShapeDtypeStruct((t, ATTN_W), F32)
    res = pl.pallas_call(
        body, name="attn_bwd", grid=(n_pairs, nk),
        in_specs=[resident(), resident(), resident_t(), resident_t(), kv_in(),
                  pl.BlockSpec((2, PAIR, tk), lambda p, j: (p, 0, j))] + [HBM_SPEC] * n_g,
        out_specs=[whole(), whole(), tile(), tile(), tile()] + [HBM_SPEC] * n_g,
        out_shape=[b16, f32, b16, f32, b16] + [jax.ShapeDtypeStruct(g.shape, g.dtype) for g in gblocks],
        scratch_shapes=[pltpu.VMEM((2, t, PAIR), F32)] + _Exchange.scratch(n_g),
        compiler_params=_cp(("arbitrary", "arbitrary")),
    )(q_bwd, do_aug, q_bwd_t, do_aug_t, k_aug, v_aug_t, *gblocks)
    return res[:5], res[5:]


def _fgate_bwd(dqx, dkx, sgate, *, tm=256):
    t = sgate.shape[0]
    tm = min(tm, t)
    nsteps = t // tm

    def body(dq_ref, dk_ref, sg_ref, df_ref, dbf_ref, carry):
        @pl.when(pl.program_id(0) == 0)
        def _():
            carry[...] = jnp.zeros_like(carry)
            dbf_ref[...] = jnp.zeros_like(dbf_ref)
        lane = lax.broadcasted_iota(jnp.int32, (ATTN_W, F_PAD), 0)
        head = lax.broadcasted_iota(jnp.int32, (ATTN_W, F_PAD), 1)
        aux = (head // 2) * PAIR + HEAD_DIM * (1 - head % 2)
        valid = head < N_HEADS
        pick_r = (valid & (lane == aux + AUX_ROWSUM)).astype(F32)
        pick_c = (valid & (lane == aux + AUX_BIAS)).astype(F32)
        hp = lax.Precision.HIGHEST
        dcum = (jnp.dot(dq_ref[...], pick_r, preferred_element_type=F32, precision=lax.Precision.HIGH)
                + jnp.dot(dk_ref[...], pick_c, preferred_element_type=F32, precision=lax.Precision.HIGH))
        r = lax.broadcasted_iota(jnp.int32, (tm, tm), 0)
        c = lax.broadcasted_iota(jnp.int32, (tm, tm), 1)
        tri = (c >= r).astype(F32)
        rc = jnp.dot(tri, dcum, preferred_element_type=F32, precision=hp) + carry[...]
        carry[...] = rc[0:1, :]
        df = rc * sg_ref[...]
        df_ref[...] = df.astype(BF16)
        dbf_ref[...] += _rows8(df)

    rev = lambda i: nsteps - 1 - i
    return pl.pallas_call(
        body, name="fgate_bwd", grid=(nsteps,),
        in_specs=[pl.BlockSpec((tm, ATTN_W), lambda i: (rev(i), 0)), pl.BlockSpec((tm, ATTN_W), lambda i: (rev(i), 0)),
                  pl.BlockSpec((tm, F_PAD), lambda i: (rev(i), 0))],
        out_specs=[pl.BlockSpec((tm, F_PAD), lambda i: (rev(i), 0)), pl.BlockSpec((8, F_PAD), lambda i: (0, 0))],
        out_shape=[jax.ShapeDtypeStruct((t, F_PAD), BF16), jax.ShapeDtypeStruct((8, F_PAD), F32)],
        scratch_shapes=[pltpu.VMEM((1, F_PAD), F32)],
        compiler_params=_cp(("arbitrary",)),
    )(dqx, dkx, sgate)


def _inproj_bwd(dq, dk, dv, df, dbcx, dh, x, g, win_pt, gblock, *, tm=512):
    t, d = x.shape
    tm = min(tm, t)
    nsteps = t // tm
    n_qkv = 3 * ATTN_W

    def body(dq_ref, dk_ref, dv_ref, df_ref, db_ref, dh_ref, x_ref, g_ref, w_ref, gb_ref, gx_ref, dg_ref, land_ref,
             *sems):
        scatter = _Exchange([gb_ref], [land_ref], *sems, gather=False)

        @pl.when(pl.program_id(0) == 0)
        def _():
            scatter.start()
            dg_ref[...] = jnp.zeros_like(dg_ref)
        dn = _dot(df_ref[...], w_ref[n_qkv:n_qkv + F_PAD, :])
        for k, r in enumerate((dq_ref, dk_ref, dv_ref)):
            dn = dn + _dot(r[...], w_ref[k * ATTN_W:(k + 1) * ATTN_W, :])
        for k in range(3):
            c0 = n_qkv + F_PAD + k * CONV_CH
            dn = dn + _dot(db_ref[:, k * CONV_CH:(k + 1) * CONV_CH], w_ref[c0:c0 + CONV_CH, :])
        dx, dg = _norm_bwd(dn, x_ref[...], g_ref[...])
        gx_ref[...] = dh_ref[...] + dx
        dg_ref[...] += dg

        @pl.when(pl.program_id(0) == nsteps - 1)
        def _():
            scatter.wait()

    row = lambda n_: pl.BlockSpec((tm, n_), lambda i: (i, 0))
    return pl.pallas_call(
        body, name="inproj_bwd", grid=(nsteps,),
        in_specs=[row(ATTN_W), row(ATTN_W), row(ATTN_W), row(F_PAD), row(3 * CONV_CH), row(d), row(d),
                  _const_spec((1, d)), _const_spec(win_pt.shape), HBM_SPEC],
        out_specs=[row(d), pl.BlockSpec((8, d), lambda i: (0, 0)), HBM_SPEC],
        out_shape=[jax.ShapeDtypeStruct((t, d), F32), jax.ShapeDtypeStruct((8, d), F32),
                   jax.ShapeDtypeStruct(gblock.shape, gblock.dtype)],
        scratch_shapes=_Exchange.scratch(1),
        compiler_params=_cp(("arbitrary",)),
    )(dq, dk, dv, df, dbcx, dh, x, g, win_pt, gblock)


LATE = ("w_out_0", "w_up_0", "w_down_0", "pool_w_1", "w_up_1", "w_down_1")


def _local_step(x, target, gains, b_f, conv_w, pool_scale, win_pt, shards):
    d = x.shape[1]
    n0, qkv, flog, bcx, cv = _norm_inproj(x, gains["mix0"], win_pt, conv_w)
    q_aug_t, k_aug, v_aug_t, sgate = _fgate_prep(flog, b_f, qkv)
    att, q_bwd, q_bwd_t, gathered = _attn_fwd(q_aug_t, k_aug, v_aug_t, [shards[n] for n in LATE])
    g = dict(zip(LATE, gathered))
    wout = g["w_out_0"].reshape(d, d)
    wup0, wup1 = g["w_up_0"], g["w_up_1"]
    wdown0, wdown1 = g["w_down_0"].reshape(-1, d), g["w_down_1"].reshape(-1, d)
    n_grp = len(POOL_WINDOWS)
    cg = d // n_grp
    poolw = g["pool_w_1"].reshape(N_DEV, n_grp, cg // N_DEV, cg).transpose(1, 0, 2, 3).reshape(n_grp, cg, cg)
    h1 = _outproj(att, cv, x, wout)
    h2, n1, a0, z0 = _mlp_fwd(h1, gains["ffn0"], wup0, wdown0, name="mlp_fwd0")
    h3, pooled = _pool_fwd(h2, gains["mix1"], poolw, pool_scale)
    loss, dh4, dg_final, n3, a1, z1 = _mlp_fwd_loss(h3, gains["ffn1"], wup1, wdown1, gains["final"], target,
                                                    name="mlp_fwd1")
    f = a1.shape[1]
    fb = f // N_DEV
    dh3, da1, dg_ffn1 = _mlp_bwd(dh4, h3, a1, gains["ffn1"], wup1, wdown1, name="mlp_bwd1")
    dwdown1 = _mm_tn(z1, dh4, name="dwdown1", ta=1024, tb=1024, tt=2048, out_dtype=BF16)
    dwup1 = _mm_tn(n3, da1, name="dwup1", ta=d, tb=fb, tt=4096, blocked_out=True, out_dtype=BF16)
    dh2, dpoolw, dscale, dg_mix1 = _pool_bwd(dh3, h2, pooled, gains["mix1"], poolw, pool_scale)
    dh1, da0, dg_ffn0 = _mlp_bwd(dh2, h1, a0, gains["ffn0"], wup0, wdown0, name="mlp_bwd0")
    dwdown0 = _mm_tn(z0, dh2, name="dwdown0", ta=1024, tb=1024, tt=2048, out_dtype=BF16)
    dwup0 = _mm_tn(n1, da0, name="dwup0", ta=d, tb=fb, tt=4096, blocked_out=True, out_dtype=BF16)
    do_aug, do_aug_t, dbcx, dconvw = _outproj_conv_bwd(dh1, att, wout, bcx, conv_w)
    dwout = _mm_tn_cat([att, cv], [dh1], name="dwout", tt=2048)
    gblocks = {
        "w_out_0": dwout.reshape(N_DEV, d // N_DEV, d), "w_up_0": dwup0, "w_up_1": dwup1,
        "w_down_0": dwdown0.reshape(N_DEV, -1, d), "w_down_1": dwdown1.reshape(N_DEV, -1, d),
        "pool_w_1": dpoolw.astype(BF16).reshape(n_grp, N_DEV, cg // N_DEV, cg).transpose(1, 0, 2, 3).reshape(
            N_DEV, n_grp * (cg // N_DEV), cg),
    }
    (dq, dqx, dk, dkx, dv), landed = _attn_bwd(q_bwd, do_aug, q_bwd_t, do_aug_t, k_aug, v_aug_t,
                                               [gblocks[n] for n in LATE])
    df, dbf = _fgate_bwd(dqx, dkx, sgate)
    dwin_t = jnp.concatenate(
        [_mm_tn_cat([dq, dk, dv], [n0], name="dwin_qkv", tt=2048),
         _mm_tn(df, n0, name="dwin_f", ta=F_PAD, tb=d, tt=2048, out_dtype=BF16)[:N_HEADS],
         _mm_tn(dbcx, n0, name="dwin_bcx", ta=512, tb=d, tt=4096, out_dtype=BF16)], axis=0)
    dwin_blocks = dwin_t.reshape(N_DEV, dwin_t.shape[0] // N_DEV, d)
    grad_x, dg_mix0, landed_win = _inproj_bwd(dq, dk, dv, df, dbcx, dh1, x, gains["mix0"], win_pt, dwin_blocks)
    small = dict(mix0=dg_mix0, ffn0=dg_ffn0, mix1=dg_mix1, pool_scale=dscale, ffn1=dg_ffn1, final=dg_final,
                 b_f=dbf, conv_w=dconvw)
    return loss, grad_x, dict(zip(LATE + ("w_in_0",), tuple(landed) + (landed_win,))), small


def _all_gather(shards):
    n = len(shards)

    def body(*refs):
        gather = _TwoLevelGather(refs[:n], refs[n:2 * n], *refs[2 * n:])
        gather.start()
        gather.forward()
        gather.wait()

    return pl.pallas_call(
        body, name="all_gather",
        in_specs=[HBM_SPEC] * n, out_specs=[HBM_SPEC] * n,
        out_shape=[jax.ShapeDtypeStruct((N_DEV,) + s.shape, s.dtype) for s in shards],
        scratch_shapes=[pltpu.SemaphoreType.DMA((7 * n,)), pltpu.SemaphoreType.DMA((7 * n,)),
                        pltpu.SemaphoreType.DMA((n,))],
    )(*shards)


SMALL_ROWS = 16


def _small_allreduce(parts):
    n, _, w = parts.shape
    assert n <= SMALL_ROWS

    def body(p_ref, o_ref, gath, send_sems, recv_sems):
        x, y, c = lax.axis_index("x"), lax.axis_index("y"), lax.axis_index("c")
        my = _slot(x, y, c)
        rows = [jnp.sum(p_ref[i], axis=0, keepdims=True) for i in range(n)]
        rows.append(jnp.zeros((SMALL_ROWS - n, w), F32))
        gath[my] = jnp.concatenate(rows, axis=0)
        copies = []
        for k in range(1, N_DEV):
            px, py, pc = x ^ (k >> 2), y ^ ((k >> 1) & 1), c ^ (k & 1)
            cp = pltpu.make_async_remote_copy(
                src_ref=gath.at[my], dst_ref=gath.at[my], send_sem=send_sems.at[k - 1], recv_sem=recv_sems.at[k - 1],
                device_id=(px, py, pc), device_id_type=MESH)
            cp.start()
            copies.append(cp)
        for cp in copies:
            cp.wait()
        acc = gath[0]
        for d in range(1, N_DEV):
            acc = acc + gath[d]
        o_ref[...] = acc

    return pl.pallas_call(
        body, name="small_allreduce",
        in_specs=[VMEM_SPEC], out_specs=VMEM_SPEC,
        out_shape=jax.ShapeDtypeStruct((SMALL_ROWS, w), F32),
        scratch_shapes=[pltpu.VMEM((N_DEV, SMALL_ROWS, w), F32), pltpu.SemaphoreType.DMA((N_DEV - 1,)),
                        pltpu.SemaphoreType.DMA((N_DEV - 1,))],
    )(parts)


def _adamw(g, w, m, v, *, name, tm=256):
    r, c = g.shape
    tm = tm if r % tm == 0 else r
    bc1 = 1.0 - ADAM_B1 ** ADAM_STEP
    bc2 = 1.0 - ADAM_B2 ** ADAM_STEP

    def body(g_ref, w_ref, m_ref, v_ref, d_ref, nm_ref, nv_ref):
        gv = g_ref[...]
        nm = ADAM_B1 * m_ref[...] + (1.0 - ADAM_B1) * gv
        nv = ADAM_B2 * v_ref[...] + (1.0 - ADAM_B2) * jnp.square(gv)
        nm_ref[...] = nm
        nv_ref[...] = nv
        d_ref[...] = -ADAM_LR * ((nm / bc1) / (jnp.sqrt(nv / bc2) + ADAM_EPS) + ADAM_WD * w_ref[...])

    blk = pl.BlockSpec((tm, c), lambda i: (i, 0))
    shp = jax.ShapeDtypeStruct((r, c), F32)
    return pl.pallas_call(
        body, name=name, grid=(r // tm,), in_specs=[blk] * 4, out_specs=[blk] * 3, out_shape=[shp] * 3,
        compiler_params=_cp(("parallel",)),
    )(g, w, m, v)


def _transpose_cast(a, *, name):
    def body(a_ref, o_ref):
        o_ref[...] = a_ref[...].T.astype(BF16)

    return pl.pallas_call(body, name=name, out_shape=jax.ShapeDtypeStruct(a.shape[::-1], BF16),
                          compiler_params=_cp())(a)


def _adamw_sum_t(parts, w, m, v, *, name):
    bc1 = 1.0 - ADAM_B1 ** ADAM_STEP
    bc2 = 1.0 - ADAM_B2 ** ADAM_STEP

    def body(p_ref, w_ref, m_ref, v_ref, g_ref, d_ref, nm_ref, nv_ref):
        acc = p_ref[0].astype(F32)
        for k in range(1, N_DEV):
            acc = acc + p_ref[k].astype(F32)
        gv = acc.T
        g_ref[...] = gv
        nm = ADAM_B1 * m_ref[...] + (1.0 - ADAM_B1) * gv
        nv = ADAM_B2 * v_ref[...] + (1.0 - ADAM_B2) * jnp.square(gv)
        nm_ref[...] = nm
        nv_ref[...] = nv
        d_ref[...] = -ADAM_LR * ((nm / bc1) / (jnp.sqrt(nv / bc2) + ADAM_EPS) + ADAM_WD * w_ref[...])

    shp = jax.ShapeDtypeStruct(w.shape, F32)
    return pl.pallas_call(body, name=name, out_shape=[shp] * 4, compiler_params=_cp())(parts, w, m, v)


def _adamw_sum(parts, w, m, v, *, name, tm=256):
    _, r, c = parts.shape
    tm = tm if r % tm == 0 else r
    bc1 = 1.0 - ADAM_B1 ** ADAM_STEP
    bc2 = 1.0 - ADAM_B2 ** ADAM_STEP

    def body(p_ref, w_ref, m_ref, v_ref, g_ref, d_ref, nm_ref, nv_ref):
        gv = p_ref[0].astype(F32)
        for k in range(1, N_DEV):
            gv = gv + p_ref[k].astype(F32)
        g_ref[...] = gv
        nm = ADAM_B1 * m_ref[...] + (1.0 - ADAM_B1) * gv
        nv = ADAM_B2 * v_ref[...] + (1.0 - ADAM_B2) * jnp.square(gv)
        nm_ref[...] = nm
        nv_ref[...] = nv
        d_ref[...] = -ADAM_LR * ((nm / bc1) / (jnp.sqrt(nv / bc2) + ADAM_EPS) + ADAM_WD * w_ref[...])

    blk = pl.BlockSpec((tm, c), lambda i: (i, 0))
    shp = jax.ShapeDtypeStruct((r, c), F32)
    return pl.pallas_call(
        body, name=name, grid=(r // tm,), in_specs=[pl.BlockSpec((N_DEV, tm, c), lambda i: (0, i, 0))] + [blk] * 3,
        out_specs=[blk] * 4, out_shape=[shp] * 4, compiler_params=_cp(("parallel",)),
    )(parts, w, m, v)


BIG = ("w_in_0", "w_out_0", "w_up_0", "w_down_0", "pool_w_1", "w_up_1", "w_down_1")
SMALL = ("norm_mix_0", "norm_ffn_0", "norm_mix_1", "pool_scale_1", "norm_ffn_1", "final_norm", "b_f_0", "conv_w_0")
WEIGHTS = ("norm_mix_0", "w_in_0", "b_f_0", "conv_w_0", "w_out_0", "norm_ffn_0", "w_up_0", "w_down_0", "norm_mix_1",
           "pool_w_1", "pool_scale_1", "norm_ffn_1", "w_up_1", "w_down_1", "final_norm")


def _pad_to(a, rows, cols):
    return jnp.pad(a, ((0, rows - a.shape[0]), (0, cols - a.shape[1])))


def _pack_small(p, width):
    rows = [p[n].reshape(1, -1) for n in SMALL[:6]]
    rows.append(_pad_to(p["b_f_0"].reshape(1, -1), 1, width))
    rows.append(_pad_to(p["conv_w_0"], 3, width))
    return _pad_to(jnp.concatenate(rows, axis=0), SMALL_ROWS, width)


def _unpack_small(a, like):
    out = {n: a[i] for i, n in enumerate(SMALL[:6])}
    out["b_f_0"] = a[6, :like["b_f_0"].shape[0]]
    out["conv_w_0"] = a[7:10, :like["conv_w_0"].shape[1]]
    return out


def kernel(x, norm_mix_0, w_in_0, b_f_0, conv_w_0, w_out_0, norm_ffn_0, w_up_0, w_down_0, norm_mix_1, pool_w_1, pool_scale_1, norm_ffn_1, w_up_1, w_down_1, final_norm, loss_target, m_norm_mix_0, m_w_in_0, m_b_f_0, m_conv_w_0, m_w_out_0, m_norm_ffn_0, m_w_up_0, m_w_down_0, m_norm_mix_1, m_pool_w_1, m_pool_scale_1, m_norm_ffn_1, m_w_up_1, m_w_down_1, m_final_norm, v_norm_mix_0, v_w_in_0, v_b_f_0, v_conv_w_0, v_w_out_0, v_norm_ffn_0, v_w_up_0, v_w_down_0, v_norm_mix_1, v_pool_w_1, v_pool_scale_1, v_norm_ffn_1, v_w_up_1, v_w_down_1, v_final_norm):
    w = dict(norm_mix_0=norm_mix_0, w_in_0=w_in_0, b_f_0=b_f_0, conv_w_0=conv_w_0, w_out_0=w_out_0,
             norm_ffn_0=norm_ffn_0, w_up_0=w_up_0, w_down_0=w_down_0, norm_mix_1=norm_mix_1, pool_w_1=pool_w_1,
             pool_scale_1=pool_scale_1, norm_ffn_1=norm_ffn_1, w_up_1=w_up_1, w_down_1=w_down_1, final_norm=final_norm)
    m = dict(norm_mix_0=m_norm_mix_0, w_in_0=m_w_in_0, b_f_0=m_b_f_0, conv_w_0=m_conv_w_0, w_out_0=m_w_out_0,
             norm_ffn_0=m_norm_ffn_0, w_up_0=m_w_up_0, w_down_0=m_w_down_0, norm_mix_1=m_norm_mix_1,
             pool_w_1=m_pool_w_1, pool_scale_1=m_pool_scale_1, norm_ffn_1=m_norm_ffn_1, w_up_1=m_w_up_1,
             w_down_1=m_w_down_1, final_norm=m_final_norm)
    v = dict(norm_mix_0=v_norm_mix_0, w_in_0=v_w_in_0, b_f_0=v_b_f_0, conv_w_0=v_conv_w_0, w_out_0=v_w_out_0,
             norm_ffn_0=v_norm_ffn_0, w_up_0=v_w_up_0, w_down_0=v_w_down_0, norm_mix_1=v_norm_mix_1,
             pool_w_1=v_pool_w_1, pool_scale_1=v_pool_scale_1, norm_ffn_1=v_norm_ffn_1, w_up_1=v_w_up_1,
             w_down_1=v_w_down_1, final_norm=v_final_norm)
    d = x.shape[-1]
    n_in = w_in_0.shape[1] * N_DEV
    n_qkv = 3 * ATTN_W
    pool_g, pool_rows, pool_c = pool_w_1.shape

    def shard2d(p):
        return {n: (p[n].reshape(pool_g * pool_rows, pool_c) if n == "pool_w_1" else p[n]) for n in BIG}
    w2, m2, v2 = shard2d(w), shard2d(m), shard2d(v)

    conv_cols = conv_w_0.shape[1]
    win_g8, conv_g8 = _all_gather([_transpose_cast(w_in_0, name="w_in_t"), _pad_to(conv_w_0, 8, 128)])
    conv_full = conv_g8[:, :, :conv_cols].transpose(1, 0, 2).reshape(8, N_DEV * conv_cols)
    win_t = win_g8.reshape(n_in, d)
    win_pt = jnp.concatenate([win_t[:n_qkv], _pad_to(win_t[n_qkv:n_qkv + N_HEADS], F_PAD, d),
                              win_t[n_qkv + N_HEADS:]], axis=0)

    gains = dict(mix0=norm_mix_0.reshape(1, d), ffn0=norm_ffn_0.reshape(1, d), mix1=norm_mix_1.reshape(1, d),
                 ffn1=norm_ffn_1.reshape(1, d), final=final_norm.reshape(1, d))
    dev = _slot(lax.axis_index("x"), lax.axis_index("y"), lax.axis_index("c"))
    loss8, grad_x, landed, small = _local_step(
        x[0], loss_target[0], gains, _pad_to(b_f_0.reshape(1, -1), 1, F_PAD), conv_full, pool_scale_1.reshape(1, d),
        win_pt, {n: w2[n].astype(BF16) for n in LATE})
    parts = jnp.concatenate(
        [small[k][None] for k in ("mix0", "ffn0", "mix1", "pool_scale", "ffn1", "final")]
        + [_pad_to(small["b_f"], 8, d)[None], jnp.pad(small["conv_w"], ((0, 0), (0, 0), (0, d - CONV_CH))),
           _pad_to(loss8[0:1, 0:1], 8, d)[None]], axis=0)
    tot = _small_allreduce(parts)
    loss = tot[10, 0]
    conv_g = lax.dynamic_slice(tot, (7, dev * conv_cols), (3, conv_cols))
    gs = tot.at[7:10].set(_pad_to(conv_g, 3, d))

    grads, deltas, new_m, new_v = {}, {}, {}, {}
    for n in BIG:
        if n in LATE:
            gr, dl, nm, nv = _adamw_sum(landed[n], w2[n], m2[n], v2[n], name="adamw_" + n)
        else:
            gr, dl, nm, nv = _adamw_sum_t(landed[n], w2[n], m2[n], v2[n], name="adamw_" + n)
        for dst, val in ((grads, gr), (deltas, dl), (new_m, nm), (new_v, nv)):
            dst[n] = val.reshape(w[n].shape)
    dl, nm, nv = _adamw(gs, _pack_small(w, d), _pack_small(m, d), _pack_small(v, d), name="adamw_small")
    for dst, val in ((grads, gs), (deltas, dl), (new_m, nm), (new_v, nv)):
        dst.update(_unpack_small(val, w))
    return (loss, grad_x[None], *[grads[n] for n in WEIGHTS], *[deltas[n] for n in WEIGHTS],
            *[new_m[n] for n in WEIGHTS], *[new_v[n] for n in WEIGHTS])
```

```python
import functools

import jax
import jax.numpy as jnp
from jax import lax
from jax.experimental import pallas as pl
from jax.experimental.pallas import tpu as pltpu

F32 = jnp.float32
BF16 = jnp.bfloat16

N_DEV = 8
N_HEADS = 8
HEAD_DIM = 64
PAIR = 2 * HEAD_DIM
ATTN_W = N_HEADS * HEAD_DIM
CONV_CH = 512
F_PAD = 128
POOL_WINDOWS = (2, 4, 8, 16)
POOL_HALO = 16
CONV_HALO = 16
RMS_EPS = 1e-6
Q_SCALE = HEAD_DIM ** -0.5
LOG2E = 1.4426950408889634
NEG = -1e30
AUX_BIAS = 0
AUX_LSE = 3
AUX_ROWSUM = 6
ADAM_LR, ADAM_B1, ADAM_B2, ADAM_EPS, ADAM_WD, ADAM_STEP = 0.001, 0.9, 0.999, 1e-08, 0.01, 10
MESH = pl.DeviceIdType.MESH
VMEM_LIMIT = 56 * 2**20


def _cp(sem=None, vmem=VMEM_LIMIT, **kw):
    return pltpu.CompilerParams(dimension_semantics=sem, vmem_limit_bytes=vmem, **kw)


def _dot(a, b):
    return jnp.dot(a, b, preferred_element_type=F32)


def _dot_nt(a, b):
    return lax.dot_general(a, b, (((1,), (1,)), ((), ())), preferred_element_type=F32)


def _dot_tn(a, b):
    return lax.dot_general(a, b, (((0,), (0,)), ((), ())), preferred_element_type=F32)


def _rstd(h):
    return lax.rsqrt(jnp.mean(h * h, axis=-1, keepdims=True) + RMS_EPS)


def _rows8(x):
    r, n = x.shape
    return jnp.sum(x.reshape(r // 8, 8, n), axis=0)


def _norm_bwd(dn, h, g):
    r = _rstd(h)
    xhat = h * r
    dy = dn * g
    dh = r * (dy - xhat * jnp.mean(dy * xhat, axis=-1, keepdims=True))
    return dh, _rows8(dn * xhat)


def _const_spec(shape):
    nd = len(shape)
    return pl.BlockSpec(shape, lambda *_: (0,) * nd, pipeline_mode=pl.Buffered(1))


HBM_SPEC = pl.BlockSpec(memory_space=pltpu.HBM)
VMEM_SPEC = pl.BlockSpec(memory_space=pltpu.VMEM)


def _slot(px, py, pc):
    return 4 * px + 2 * py + pc


class _Exchange:
    def __init__(self, srcs, dsts, send_sems, recv_sems, local_sems, gather):
        x, y, c = lax.axis_index("x"), lax.axis_index("y"), lax.axis_index("c")
        me = _slot(x, y, c)
        self.copies = []
        for a, (src, dst) in enumerate(zip(srcs, dsts)):
            self.copies.append(pltpu.make_async_copy(src if gather else src.at[me], dst.at[me], local_sems.at[a]))
            for k in range(1, N_DEV):
                px, py, pc = x ^ (k >> 2), y ^ ((k >> 1) & 1), c ^ (k & 1)
                self.copies.append(pltpu.make_async_remote_copy(
                    src_ref=src if gather else src.at[_slot(px, py, pc)], dst_ref=dst.at[me],
                    send_sem=send_sems.at[(N_DEV - 1) * a + k - 1], recv_sem=recv_sems.at[(N_DEV - 1) * a + k - 1],
                    device_id=(px, py, pc), device_id_type=MESH))

    def start(self):
        for cp in self.copies:
            cp.start()

    def wait(self):
        for cp in self.copies:
            cp.wait()

    @staticmethod
    def scratch(n):
        return [pltpu.SemaphoreType.DMA(((N_DEV - 1) * n,)), pltpu.SemaphoreType.DMA(((N_DEV - 1) * n,)),
                pltpu.SemaphoreType.DMA((n,))]


def _mesh_places():
    x, y, c = lax.axis_index("x"), lax.axis_index("y"), lax.axis_index("c")
    chips = [(1 - x, y), (x, 1 - y), (1 - x, 1 - y)]
    return (x, y, c), (x, y, 1 - c), chips


class _TwoLevelGather:
    def __init__(self, srcs, dsts, send_sems, recv_sems, local_sems):
        me, sib, chips = _mesh_places()
        c = me[2]
        n = len(srcs)

        def copy(a, k, block, to, src=None):
            dst = dsts[a].at[_slot(*block)]
            return pltpu.make_async_remote_copy(
                src_ref=dst if src is None else src, dst_ref=dst, send_sem=send_sems.at[7 * a + k],
                recv_sem=recv_sems.at[7 * a + k], device_id=to, device_id_type=MESH)

        self.mine = [pltpu.make_async_copy(srcs[a], dsts[a].at[_slot(*me)], local_sems.at[a]) for a in range(n)]
        self.first, self.landed, self.passed, self.rest = [], [], [], []
        for a in range(n):
            self.first.append(copy(a, 0, me, sib, src=srcs[a]))
            self.first += [copy(a, 1 + j, me, (*chip, c), src=srcs[a]) for j, chip in enumerate(chips)]
            self.landed += [copy(a, 1 + j, (*chip, c), me) for j, chip in enumerate(chips)]
            self.passed += [copy(a, 4 + j, (*chip, c), sib) for j, chip in enumerate(chips)]
            self.rest.append(copy(a, 0, sib, me))
            self.rest += [copy(a, 4 + j, (*chip, 1 - c), me) for j, chip in enumerate(chips)]

    def start(self):
        for cp in self.mine + self.first:
            cp.start()

    def forward(self):
        for arrived, onward in zip(self.landed, self.passed):
            arrived.wait_recv()
            onward.start()

    def wait(self):
        for cp in self.rest:
            cp.wait_recv()
        for cp in self.first + self.passed:
            cp.wait_send()
        for cp in self.mine:
            cp.wait()


def _norm_inproj(x, g, win_pt, conv_w, *, tm=512):
    t, d = x.shape
    n_all = win_pt.shape[0]
    n_qkv = 3 * ATTN_W
    n_bcx = 3 * CONV_CH
    assert n_all == n_qkv + F_PAD + n_bcx
    tm = min(tm, t)
    ch = CONV_CH

    def body(x_ref, g_ref, w_ref, cw_ref, n_ref, qkv_ref, f_ref, bcx_ref, cv_ref, ext):
        h = x_ref[...]
        n = (h * _rstd(h) * g_ref[...]).astype(BF16)
        n_ref[...] = n
        for c0 in range(0, n_qkv, 512):
            acc = _dot_nt(n, w_ref[c0:c0 + 512, :])
            if c0 < ATTN_W:
                acc = acc * (Q_SCALE * LOG2E)
            qkv_ref[:, c0:c0 + 512] = acc.astype(BF16)
        f_ref[...] = _dot_nt(n, w_ref[n_qkv:n_qkv + F_PAD, :])
        bcx = []
        for k in range(3):
            c0 = n_qkv + F_PAD + k * ch
            v = _dot_nt(n, w_ref[c0:c0 + ch, :]).astype(BF16)
            bcx_ref[:, k * ch:(k + 1) * ch] = v
            bcx.append(v.astype(F32))
        @pl.when(pl.program_id(0) == 0)
        def _():
            ext[tm:tm + CONV_HALO, :] = jnp.zeros((CONV_HALO, ch), F32)
        ext[0:CONV_HALO, :] = ext[tm:tm + CONV_HALO, :]
        ext[CONV_HALO:CONV_HALO + tm, :] = bcx[1] * bcx[2]
        conv = (cw_ref[0:1, :] * ext[CONV_HALO - 2:CONV_HALO - 2 + tm, :]
                + cw_ref[1:2, :] * ext[CONV_HALO - 1:CONV_HALO - 1 + tm, :]
                + cw_ref[2:3, :] * ext[CONV_HALO:CONV_HALO + tm, :])
        cv_ref[...] = (bcx[0] * conv).astype(BF16)

    return pl.pallas_call(
        body, name="norm_inproj", grid=(t // tm,),
        in_specs=[pl.BlockSpec((tm, d), lambda i: (i, 0)), _const_spec((1, d)), _const_spec((n_all, d)),
                  _const_spec((8, ch))],
        out_specs=[pl.BlockSpec((tm, d), lambda i: (i, 0)), pl.BlockSpec((tm, n_qkv), lambda i: (i, 0)),
                   pl.BlockSpec((tm, F_PAD), lambda i: (i, 0)), pl.BlockSpec((tm, n_bcx), lambda i: (i, 0)),
                   pl.BlockSpec((tm, ch), lambda i: (i, 0))],
        out_shape=[jax.ShapeDtypeStruct((t, d), BF16), jax.ShapeDtypeStruct((t, n_qkv), BF16),
                   jax.ShapeDtypeStruct((t, F_PAD), F32), jax.ShapeDtypeStruct((t, n_bcx), BF16),
                   jax.ShapeDtypeStruct((t, ch), BF16)],
        scratch_shapes=[pltpu.VMEM((CONV_HALO + tm, ch), F32)],
        compiler_params=_cp(("arbitrary",)),
    )(x, g, win_pt, conv_w)


def _head_lanes(h):
    lane = lax.broadcasted_iota(jnp.int32, (1, PAIR), 1)
    hh = h % 2
    return lane, lane // HEAD_DIM == hh, HEAD_DIM * (1 - hh)


def _pieces(col):
    hi = col.astype(BF16).astype(F32)
    r1 = col - hi
    mid = r1.astype(BF16).astype(F32)
    lo = (r1 - mid).astype(BF16).astype(F32)
    return hi, mid, lo


def _put_pieces(lane, first, col, other):
    hi, mid, lo = _pieces(col)
    return jnp.where(lane == first, hi, jnp.where(lane == first + 1, mid, jnp.where(lane == first + 2, lo, other)))


def _fgate_prep(flog, b_f, qkv, *, tm=512):
    t = flog.shape[0]
    tm = min(tm, t)

    def body(f_ref, b_ref, qkv_ref, qat_ref, ka_ref, vat_ref, sg_ref, carry):
        @pl.when(pl.program_id(0) == 0)
        def _():
            carry[...] = jnp.zeros_like(carry)
        z = f_ref[...] + b_ref[...]
        e = jnp.exp(-jnp.abs(z))
        logf = jnp.minimum(z, 0.0) - jnp.log(1.0 + e)
        sg_ref[...] = jnp.where(z >= 0, e, 1.0) / (1.0 + e)
        r = lax.broadcasted_iota(jnp.int32, (tm, tm), 0)
        c = lax.broadcasted_iota(jnp.int32, (tm, tm), 1)
        tri = (c <= r).astype(F32)
        cs = jnp.dot(tri, logf, preferred_element_type=F32, precision=lax.Precision.HIGHEST) + carry[...]
        carry[...] = cs[tm - 1:tm, :]
        cs2 = cs * LOG2E
        for h in range(N_HEADS):
            lane, head, aux = _head_lanes(h)
            p0 = (h // 2) * PAIR
            ones = ((lane >= aux + AUX_LSE) & (lane <= aux + AUX_ROWSUM)).astype(F32)
            bias = (lane >= aux + AUX_BIAS) & (lane < aux + AUX_BIAS + 3)
            k_aux = _put_pieces(lane, aux + AUX_BIAS, cs2[:, h:h + 1], ones)
            q_aug = jnp.where(head, qkv_ref[:, p0:p0 + PAIR], jnp.where(bias, -1.0, 0.0).astype(BF16))
            v_aug = jnp.where(head, qkv_ref[:, 2 * ATTN_W + p0:2 * ATTN_W + p0 + PAIR],
                              jnp.where(bias, 1.0, 0.0).astype(BF16))
            qat_ref[h] = q_aug.T
            ka_ref[h] = jnp.where(head, qkv_ref[:, ATTN_W + p0:ATTN_W + p0 + PAIR], k_aux.astype(BF16))
            vat_ref[h] = v_aug.T

    aug = lambda: pl.BlockSpec((N_HEADS, tm, PAIR), lambda i: (0, i, 0))
    aug_t = lambda: pl.BlockSpec((N_HEADS, PAIR, tm), lambda i: (0, 0, i))
    aug_shape = jax.ShapeDtypeStruct((N_HEADS, t, PAIR), BF16)
    aug_t_shape = jax.ShapeDtypeStruct((N_HEADS, PAIR, t), BF16)
    return pl.pallas_call(
        body, name="fgate_prep", grid=(t // tm,),
        in_specs=[pl.BlockSpec((tm, F_PAD), lambda i: (i, 0)), _const_spec((1, F_PAD)),
                  pl.BlockSpec((tm, 3 * ATTN_W), lambda i: (i, 0))],
        out_specs=[aug_t(), aug(), aug_t(), pl.BlockSpec((tm, F_PAD), lambda i: (i, 0))],
        out_shape=[aug_t_shape, aug_shape, aug_t_shape, jax.ShapeDtypeStruct((t, F_PAD), F32)],
        scratch_shapes=[pltpu.VMEM((1, F_PAD), F32)],
        compiler_params=_cp(("arbitrary",)),
    )(flog, b_f, qkv)


def _put_pieces_t(row, first, vec, other):
    hi, mid, lo = _pieces(vec)
    return jnp.where(row == first, hi, jnp.where(row == first + 1, mid, jnp.where(row == first + 2, lo, other)))


def _attn_fwd(q_aug_t, k_aug, v_aug_t, shards, *, tq=1024):
    t = k_aug.shape[1]
    tq = min(tq, t)
    tk = tq // 2
    nq = t // tq
    n_pairs = ATTN_W // PAIR
    n_sh = len(shards)
    forward_step = (11 * n_pairs * nq) // 16

    def body(qt_ref, k_ref, vt_ref, *rest):
        o_ref, qb_ref, qbt_ref = rest[n_sh:n_sh + 3]
        s_scr = rest[2 * n_sh + 3]
        gather = _TwoLevelGather(rest[:n_sh], rest[n_sh + 3:2 * n_sh + 3], *rest[2 * n_sh + 4:])
        i = pl.program_id(1)
        step = pl.program_id(0) * nq + i

        @pl.when(step == 0)
        def _():
            gather.start()

        @pl.when(step == forward_step)
        def _():
            gather.forward()
        key = lax.broadcasted_iota(jnp.int32, (tk, tq), 0)
        qry = lax.broadcasted_iota(jnp.int32, (tk, tq), 1)
        qt = [qt_ref[0], qt_ref[1]]

        def logits(hh, tile, slot, diag):
            s = _dot(k_ref[hh, pl.ds(pl.multiple_of(tile * tk, tk), tk), :], qt[hh])
            if diag:
                s = jnp.where(key + (tile * tk - i * tq) <= qry, s, NEG)
            s_scr[hh, slot] = s
            return jnp.max(s, axis=0, keepdims=True)

        def probs(hh, tile, slot, m, acc, tmax):
            mn = jnp.maximum(m, tmax)
            p = jnp.exp2(s_scr[hh, slot] - mn).astype(BF16)
            acc = jnp.exp2(m - mn) * acc + _dot(vt_ref[hh, :, pl.ds(pl.multiple_of(tile * tk, tk), tk)], p)
            return mn, acc

        def advance(carry, prev, slot, nxt, diag=False):
            out = []
            for hh in range(2):
                m, acc, tmax = carry[hh]
                m, acc = probs(hh, prev, slot, m, acc, tmax)
                out.append((m, acc, logits(hh, nxt, 1 - slot, diag)))
            return tuple(out)

        def two_tiles(jj, carry):
            carry = advance(carry, jnp.where(jj == 0, 2 * i, 2 * jj - 1), 1, 2 * jj)
            return advance(carry, 2 * jj, 0, 2 * jj + 1)

        init = tuple((jnp.full((1, tq), NEG, F32), jnp.zeros((PAIR, tq), F32), logits(hh, 2 * i + 1, 0, True))
                     for hh in range(2))
        carry = advance(init, 2 * i + 1, 0, 2 * i, diag=True)
        carry = lax.fori_loop(0, i // 2, lambda jj, c: two_tiles(2 * jj + 1, two_tiles(2 * jj, c)), carry)
        carry = lax.cond(i % 2 == 1, lambda c: two_tiles(i - 1, c), lambda c: c, carry)
        last = jnp.where(i == 0, 2 * i, 2 * i - 1)
        row = lax.broadcasted_iota(jnp.int32, (PAIR, 1), 0)
        res = []
        for hh in range(2):
            aux = HEAD_DIM * (1 - hh)
            m, acc, tmax = carry[hh]
            m, acc = probs(hh, last, 1, m, acc, tmax)
            l = acc[aux + AUX_BIAS:aux + AUX_BIAS + 1, :]
            qbt = _put_pieces_t(row, aux + AUX_LSE, -(m + jnp.log2(l)), qt[hh].astype(F32))
            qbt_ref[hh] = qbt.astype(BF16)
            qb_ref[hh] = qbt.astype(BF16).T
            res.append(acc * (1.0 / l))
        o_ref[...] = jnp.where(row < HEAD_DIM, res[0], res[1]).astype(BF16).T

        @pl.when((pl.program_id(0) == n_pairs - 1) & (i == nq - 1))
        def _():
            gather.wait()

    res = pl.pallas_call(
        body, name="attn_fwd", grid=(n_pairs, nq),
        in_specs=[pl.BlockSpec((2, PAIR, tq), lambda p, i: (p, 0, i)),
                  pl.BlockSpec((2, t, PAIR), lambda p, i: (p, 0, 0), pipeline_mode=pl.Buffered(1)),
                  pl.BlockSpec((2, PAIR, t), lambda p, i: (p, 0, 0), pipeline_mode=pl.Buffered(1))] + [HBM_SPEC] * n_sh,
        out_specs=[pl.BlockSpec((tq, PAIR), lambda p, i: (i, p)),
                   pl.BlockSpec((2, tq, PAIR), lambda p, i: (p, i, 0)),
                   pl.BlockSpec((2, PAIR, tq), lambda p, i: (p, 0, i))] + [HBM_SPEC] * n_sh,
        out_shape=[jax.ShapeDtypeStruct((t, ATTN_W), BF16), jax.ShapeDtypeStruct((N_HEADS, t, PAIR), BF16),
                   jax.ShapeDtypeStruct((N_HEADS, PAIR, t), BF16)]
        + [jax.ShapeDtypeStruct((N_DEV,) + s.shape, s.dtype) for s in shards],
        scratch_shapes=[pltpu.VMEM((2, 2, tk, tq), F32)] + _Exchange.scratch(n_sh),
        compiler_params=_cp(("arbitrary", "arbitrary")),
    )(q_aug_t, k_aug, v_aug_t, *shards)
    return res[0], res[1], res[2], res[3:]


def _prev_halo(tm, halo):
    return lambda i: (jnp.maximum(i * (tm // halo) - 1, 0), 0)


def _next_halo(tm, halo, t):
    return lambda i: (jnp.minimum((i + 1) * (tm // halo), t // halo - 1), 0)


def _mlp_tile(hh, g_ref, wu_ref, wd_ref, n_ref, a_ref, z_ref):
    n_blk, _, fb = wu_ref.shape
    n = (hh * _rstd(hh) * g_ref[...]).astype(BF16)
    n_ref[...] = n
    acc = hh
    for k in range(n_blk):
        a = _dot(n, wu_ref[k])
        zz = jnp.square(jnp.maximum(a, 0.0)).astype(BF16)
        a_ref[:, k * fb:(k + 1) * fb] = a.astype(BF16)
        z_ref[:, k * fb:(k + 1) * fb] = zz
        acc = acc + _dot(zz, wd_ref[k * fb:(k + 1) * fb, :])
    return acc


def _outproj(att, cv, x, wout, *, tm=512):
    t, d = x.shape
    tm = min(tm, t)

    def body(a_ref, c_ref, x_ref, w_ref, h_ref):
        h_ref[...] = x_ref[...] + _dot(a_ref[...], w_ref[0:ATTN_W, :]) + _dot(c_ref[...], w_ref[ATTN_W:, :])

    return pl.pallas_call(
        body, name="outproj", grid=(t // tm,),
        in_specs=[pl.BlockSpec((tm, ATTN_W), lambda i: (i, 0)), pl.BlockSpec((tm, CONV_CH), lambda i: (i, 0)),
                  pl.BlockSpec((tm, d), lambda i: (i, 0)), _const_spec(wout.shape)],
        out_specs=pl.BlockSpec((tm, d), lambda i: (i, 0)),
        out_shape=jax.ShapeDtypeStruct((t, d), F32),
        compiler_params=_cp(("parallel",)),
    )(att, cv, x, wout)


def _mlp_fwd(h, g, wup, wdown, *, name, tm=512):
    t, d = h.shape
    n_blk, _, fb = wup.shape
    f = n_blk * fb
    tm = min(tm, t)

    def body(h_ref, g_ref, wu_ref, wd_ref, ho_ref, n_ref, a_ref, z_ref):
        ho_ref[...] = _mlp_tile(h_ref[...], g_ref, wu_ref, wd_ref, n_ref, a_ref, z_ref)

    row = lambda n_: pl.BlockSpec((tm, n_), lambda i: (i, 0))
    return pl.pallas_call(
        body, name=name, grid=(t // tm,),
        in_specs=[row(d), _const_spec((1, d)), _const_spec(wup.shape), _const_spec(wdown.shape)],
        out_specs=[row(d), row(d), row(f), row(f)],
        out_shape=[jax.ShapeDtypeStruct((t, d), F32), jax.ShapeDtypeStruct((t, d), BF16),
                   jax.ShapeDtypeStruct((t, f), BF16), jax.ShapeDtypeStruct((t, f), BF16)],
        compiler_params=_cp(("parallel",)),
    )(h, g, wup, wdown)


def _mlp_fwd_loss(h, g, wup, wdown, g_out, target, *, name, tm=512):
    t, d = h.shape
    n_blk, _, fb = wup.shape
    f = n_blk * fb
    tm = min(tm, t)
    nsteps = t // tm

    def body(h_ref, g_ref, wu_ref, wd_ref, go_ref, y_ref, loss_ref, dh_ref, dhb_ref, dg_ref, n_ref, a_ref, z_ref, lacc):
        i = pl.program_id(0)

        @pl.when(i == 0)
        def _():
            lacc[...] = jnp.zeros_like(lacc)
            dg_ref[...] = jnp.zeros_like(dg_ref)
        hv = _mlp_tile(h_ref[...], g_ref, wu_ref, wd_ref, n_ref, a_ref, z_ref)
        gv = go_ref[...]
        r = _rstd(hv)
        xhat = hv * r
        err = xhat * gv - y_ref[...]
        lacc[...] += _rows8(err * err)
        dout = err * (1.0 / d)
        dy = dout * gv
        dg_ref[...] += _rows8(dout * xhat)
        dh = r * (dy - xhat * jnp.mean(dy * xhat, axis=-1, keepdims=True))
        dh_ref[...] = dh
        dhb_ref[...] = dh.astype(BF16)

        @pl.when(i == nsteps - 1)
        def _():
            loss_ref[...] = jnp.full(loss_ref.shape, (0.5 / d) * jnp.sum(lacc[...]), F32)

    row = lambda n_: pl.BlockSpec((tm, n_), lambda i: (i, 0))
    return pl.pallas_call(
        body, name=name, grid=(nsteps,),
        in_specs=[row(d), _const_spec((1, d)), _const_spec(wup.shape), _const_spec(wdown.shape), _const_spec((1, d)),
                  row(d)],
        out_specs=[pl.BlockSpec((8, 128), lambda i: (0, 0)), row(d), row(d), pl.BlockSpec((8, d), lambda i: (0, 0)),
                   row(d), row(f), row(f)],
        out_shape=[jax.ShapeDtypeStruct((8, 128), F32), jax.ShapeDtypeStruct((t, d), F32),
                   jax.ShapeDtypeStruct((t, d), BF16),
                   jax.ShapeDtypeStruct((8, d), F32), jax.ShapeDtypeStruct((t, d), BF16),
                   jax.ShapeDtypeStruct((t, f), BF16), jax.ShapeDtypeStruct((t, f), BF16)],
        scratch_shapes=[pltpu.VMEM((8, d), F32)],
        compiler_params=_cp(("arbitrary",)),
    )(h, g, wup, wdown, g_out, target)


def _pool_inv_count(i, tm):
    tglob = (i * tm + lax.broadcasted_iota(jnp.int32, (tm, 1), 0) + 1).astype(F32)
    return [1.0 / jnp.minimum(tglob, float(w)) for w in POOL_WINDOWS]


def _pool_fwd(h, g, poolw, scale, *, tm=512):
    t, d = h.shape
    tm = min(tm, t)
    cg = d // len(POOL_WINDOWS)

    def body(h_ref, hh_ref, g_ref, w_ref, s_ref, ho_ref, p_ref, ext):
        i = pl.program_id(0)
        hv = h_ref[...]
        halo = hh_ref[...]
        n = hv * _rstd(hv) * g_ref[...]
        ext[0:POOL_HALO, :] = jnp.where(i == 0, 0.0, halo * _rstd(halo) * g_ref[...])
        ext[POOL_HALO:POOL_HALO + tm, :] = n
        inv = _pool_inv_count(i, tm)
        for gi, w in enumerate(POOL_WINDOWS):
            cs = slice(gi * cg, (gi + 1) * cg)
            s = ext[POOL_HALO:POOL_HALO + tm, cs]
            for j in range(1, w):
                s = s + ext[POOL_HALO - j:POOL_HALO - j + tm, cs]
            pooled = (s * inv[gi] - n[:, cs]).astype(BF16)
            p_ref[:, cs] = pooled
            ho_ref[:, cs] = hv[:, cs] + _dot(pooled, w_ref[gi]) * s_ref[:, cs]

    row = lambda: pl.BlockSpec((tm, d), lambda i: (i, 0))
    return pl.pallas_call(
        body, name="pool_fwd", grid=(t // tm,),
        in_specs=[row(), pl.BlockSpec((POOL_HALO, d), _prev_halo(tm, POOL_HALO)), _const_spec((1, d)),
                  _const_spec(poolw.shape), _const_spec((1, d))],
        out_specs=[row(), row()],
        out_shape=[jax.ShapeDtypeStruct((t, d), F32), jax.ShapeDtypeStruct((t, d), BF16)],
        scratch_shapes=[pltpu.VMEM((POOL_HALO + tm, d), F32)],
        compiler_params=_cp(("parallel",)),
    )(h, h, g, poolw, scale)


def _mm_tn(a, b, *, name, ta, tb, tt, blocked_out=False, out_dtype=F32):
    t, ka = a.shape
    n = b.shape[1]
    ta, tb, tt = min(ta, ka), min(tb, n), min(tt, t)
    nt = t // tt

    def body(a_ref, b_ref, o_ref, acc):
        @pl.when(pl.program_id(2) == 0)
        def _():
            acc[...] = jnp.zeros_like(acc)
        acc[...] += _dot_tn(a_ref[...].astype(BF16), b_ref[...].astype(BF16))

        @pl.when(pl.program_id(2) == nt - 1)
        def _():
            o_ref[...] = acc[...].astype(out_dtype)

    if blocked_out:
        assert ta == ka
        out_shape = jax.ShapeDtypeStruct((n // tb, ka, tb), out_dtype)
        out_spec = pl.BlockSpec((None, ta, tb), lambda i, j, k: (j, i, 0))
    else:
        out_shape = jax.ShapeDtypeStruct((ka, n), out_dtype)
        out_spec = pl.BlockSpec((ta, tb), lambda i, j, k: (i, j))
    return pl.pallas_call(
        body, name=name, grid=(ka // ta, n // tb, nt),
        in_specs=[pl.BlockSpec((tt, ta), lambda i, j, k: (k, i)), pl.BlockSpec((tt, tb), lambda i, j, k: (k, j))],
        out_specs=out_spec, out_shape=out_shape, scratch_shapes=[pltpu.VMEM((ta, tb), F32)],
        compiler_params=_cp(("parallel", "parallel", "arbitrary")),
    )(a, b)


def _mm_tn_cat(a_list, b_list, *, name, tt, out_dtype=BF16):
    t = a_list[0].shape[0]
    ta, tb = a_list[0].shape[1], b_list[0].shape[1]
    na, nb = len(a_list), len(b_list)
    tt = min(tt, t)
    nt = t // tt

    def body(*refs):
        a_refs, b_refs, o_ref, acc = refs[:na], refs[na:na + nb], refs[na + nb], refs[na + nb + 1]
        i, j, k = pl.program_id(0), pl.program_id(1), pl.program_id(2)

        @pl.when(k == 0)
        def _():
            acc[...] = jnp.zeros_like(acc)
        for ia in range(na):
            for ib in range(nb):
                @pl.when((i == ia) & (j == ib))
                def _(ia=ia, ib=ib):
                    acc[...] += _dot_tn(a_refs[ia][...].astype(BF16), b_refs[ib][...].astype(BF16))

        @pl.when(k == nt - 1)
        def _():
            o_ref[...] = acc[...].astype(out_dtype)

    def held(m, axis):
        def index(i, j, k):
            cur = (i, j)[axis]
            return (jnp.where(cur == m, k, jnp.where(cur < m, 0, nt - 1)), 0)
        return index

    return pl.pallas_call(
        body, name=name, grid=(na, nb, nt),
        in_specs=[pl.BlockSpec((tt, ta), held(m, 0)) for m in range(na)]
        + [pl.BlockSpec((tt, tb), held(m, 1)) for m in range(nb)],
        out_specs=pl.BlockSpec((ta, tb), lambda i, j, k: (i, j)),
        out_shape=jax.ShapeDtypeStruct((na * ta, nb * tb), out_dtype), scratch_shapes=[pltpu.VMEM((ta, tb), F32)],
        compiler_params=_cp(("arbitrary", "arbitrary", "arbitrary")),
    )(*a_list, *b_list)


def _mlp_bwd(dho, h, a, g, wup, wdown, *, name, tm=512):
    t, d = h.shape
    n_blk, _, fb = wup.shape
    f = n_blk * fb
    tm = min(tm, t)

    def body(do_ref, h_ref, a_ref, g_ref, wu_ref, wd_ref, dh_ref, da_ref, dg_ref):
        @pl.when(pl.program_id(0) == 0)
        def _():
            dg_ref[...] = jnp.zeros_like(dg_ref)
        dho_v = do_ref[...]
        dob = dho_v.astype(BF16)
        dn = jnp.zeros((tm, d), F32)
        for k in range(n_blk):
            dz = _dot_nt(dob, wd_ref[k * fb:(k + 1) * fb, :])
            da = (dz * (2.0 * jnp.maximum(a_ref[:, k * fb:(k + 1) * fb].astype(F32), 0.0))).astype(BF16)
            da_ref[:, k * fb:(k + 1) * fb] = da
            dn = dn + _dot_nt(da, wu_ref[k])
        dh, dg = _norm_bwd(dn, h_ref[...], g_ref[...])
        dh_ref[...] = dho_v + dh
        dg_ref[...] += dg

    row = lambda n_: pl.BlockSpec((tm, n_), lambda i: (i, 0))
    return pl.pallas_call(
        body, name=name, grid=(t // tm,),
        in_specs=[row(d), row(d), row(f), _const_spec((1, d)), _const_spec(wup.shape), _const_spec(wdown.shape)],
        out_specs=[row(d), row(f), pl.BlockSpec((8, d), lambda i: (0, 0))],
        out_shape=[jax.ShapeDtypeStruct((t, d), F32), jax.ShapeDtypeStruct((t, f), BF16),
                   jax.ShapeDtypeStruct((8, d), F32)],
        compiler_params=_cp(("arbitrary",)),
    )(dho, h, a, g, wup, wdown)


def _pool_bwd(dho, h, pooled, g, poolw, scale, *, tm=512):
    t, d = h.shape
    tm = min(tm, t)
    ng = len(POOL_WINDOWS)
    cg = d // ng
    nsteps = t // tm

    def body(do_ref, dn_ref, h_ref, p_ref, g_ref, w_ref, s_ref, dh_ref, dhb_ref, dw_ref, ds_ref, dg_ref, ext):
        i = pl.program_id(0)

        @pl.when(i == 0)
        def _():
            dw_ref[...] = jnp.zeros_like(dw_ref)
            ds_ref[...] = jnp.zeros_like(ds_ref)
            dg_ref[...] = jnp.zeros_like(dg_ref)
        dho_v = do_ref[...]
        sv = s_ref[...]
        dyp = (dho_v * sv).astype(BF16)
        dyp_halo = (dn_ref[...] * sv).astype(BF16)
        inv = _pool_inv_count(i, tm)
        tnext = ((i + 1) * tm + lax.broadcasted_iota(jnp.int32, (POOL_HALO, 1), 0) + 1).astype(F32)
        last = i == nsteps - 1
        ypre_parts, dpooled_parts = [], []
        for gi, w in enumerate(POOL_WINDOWS):
            cs = slice(gi * cg, (gi + 1) * cg)
            pg = p_ref[:, cs]
            ypre_parts.append(_dot(pg, w_ref[gi]))
            dw_ref[gi] += _dot_tn(pg, dyp[:, cs])
            dpool = _dot_nt(dyp[:, cs], w_ref[gi])
            dpooled_parts.append(dpool)
            ext[0:tm, cs] = dpool * inv[gi]
            dpool_halo = _dot_nt(dyp_halo[:, cs], w_ref[gi]) * (1.0 / jnp.minimum(tnext, float(w)))
            ext[tm:tm + POOL_HALO, cs] = jnp.where(last, 0.0, dpool_halo)
        ds_ref[...] += _rows8(dho_v * jnp.concatenate(ypre_parts, axis=1))
        dn_parts = []
        for gi, w in enumerate(POOL_WINDOWS):
            cs = slice(gi * cg, (gi + 1) * cg)
            s = ext[0:tm, cs]
            for j in range(1, w):
                s = s + ext[j:j + tm, cs]
            dn_parts.append(s - dpooled_parts[gi])
        dh, dg = _norm_bwd(jnp.concatenate(dn_parts, axis=1), h_ref[...], g_ref[...])
        dh = dho_v + dh
        dh_ref[...] = dh
        dhb_ref[...] = dh.astype(BF16)
        dg_ref[...] += dg

    row = lambda: pl.BlockSpec((tm, d), lambda i: (i, 0))
    acc8 = lambda: pl.BlockSpec((8, d), lambda i: (0, 0))
    return pl.pallas_call(
        body, name="pool_bwd", grid=(nsteps,),
        in_specs=[row(), pl.BlockSpec((POOL_HALO, d), _next_halo(tm, POOL_HALO, t)), row(), row(),
                  _const_spec((1, d)), _const_spec(poolw.shape), _const_spec((1, d))],
        out_specs=[row(), row(), pl.BlockSpec((ng, cg, cg), lambda i: (0, 0, 0)), acc8(), acc8()],
        out_shape=[jax.ShapeDtypeStruct((t, d), F32), jax.ShapeDtypeStruct((t, d), BF16),
                   jax.ShapeDtypeStruct((ng, cg, cg), F32),
                   jax.ShapeDtypeStruct((8, d), F32), jax.ShapeDtypeStruct((8, d), F32)],
        scratch_shapes=[pltpu.VMEM((tm + POOL_HALO, d), F32)],
        compiler_params=_cp(("arbitrary",)),
    )(dho, dho, h, pooled, g, poolw, scale)


def _outproj_conv_bwd(dh, o, wout, bcx, conv_w, *, tm=512):
    t, d = dh.shape
    tm = min(tm, t)
    ch = CONV_CH
    nsteps = t // tm

    def body(dh_ref, o_ref, w_ref, b_ref, c_ref, x_ref, hc_ref, hx_ref, cw_ref,
             da_ref, dat_ref, db_ref, dw_ref, ext_u, ext_d):
        s = pl.program_id(0)

        @pl.when(s == 0)
        def _():
            dw_ref[...] = jnp.zeros_like(dw_ref)
            ext_d[0:CONV_HALO, :] = jnp.zeros((CONV_HALO, ch), F32)
        dhb = dh_ref[...].astype(BF16)
        for p in range(ATTN_W // PAIR):
            datt = _dot_nt(dhb, w_ref[p * PAIR:(p + 1) * PAIR, :])
            prod = datt * o_ref[:, p * PAIR:(p + 1) * PAIR].astype(F32)
            for hh in range(2):
                lane, head, aux = _head_lanes(hh)
                delta = jnp.sum(jnp.where(head, prod, 0.0), axis=1, keepdims=True)
                aug = _put_pieces(lane, aux + AUX_BIAS, -delta, jnp.where(head, datt, 0.0))
                da_ref[2 * p + hh] = aug.astype(BF16)
                dat_ref[2 * p + hh] = aug.astype(BF16).T
        dcv = _dot_nt(dhb, w_ref[ATTN_W:, :])
        b, c, x = b_ref[...].astype(F32), c_ref[...].astype(F32), x_ref[...].astype(F32)
        ext_u[0:CONV_HALO, :] = jnp.where(s == nsteps - 1, 0.0, hc_ref[...].astype(F32) * hx_ref[...].astype(F32))
        ext_u[CONV_HALO:CONV_HALO + tm, :] = c * x
        dconv = dcv * b
        ext_d[tm:tm + CONV_HALO, :] = ext_d[0:CONV_HALO, :]
        ext_d[0:tm, :] = dconv
        u = [ext_u[CONV_HALO - 2 + k:CONV_HALO - 2 + k + tm, :] for k in range(3)]
        conv = cw_ref[0:1, :] * u[0] + cw_ref[1:2, :] * u[1] + cw_ref[2:3, :] * u[2]
        du = (cw_ref[2:3, :] * dconv + cw_ref[1:2, :] * ext_d[1:1 + tm, :] + cw_ref[0:1, :] * ext_d[2:2 + tm, :])
        db_ref[:, 0:ch] = (dcv * conv).astype(BF16)
        db_ref[:, ch:2 * ch] = (du * x).astype(BF16)
        db_ref[:, 2 * ch:3 * ch] = (du * c).astype(BF16)
        for k in range(3):
            dw_ref[k] += _rows8(dconv * u[k])

    rev = lambda s: nsteps - 1 - s
    row = lambda n_: pl.BlockSpec((tm, n_), lambda s: (rev(s), 0))
    col = lambda k: pl.BlockSpec((tm, ch), lambda s: (rev(s), k))
    prev = lambda k: pl.BlockSpec((CONV_HALO, ch), lambda s: (_prev_halo(tm, CONV_HALO)(rev(s))[0], k))
    return pl.pallas_call(
        body, name="outproj_conv_bwd", grid=(nsteps,),
        in_specs=[row(d), row(ATTN_W), _const_spec(wout.shape), col(0), col(1), col(2), prev(1), prev(2),
                  _const_spec((8, ch))],
        out_specs=[pl.BlockSpec((N_HEADS, tm, PAIR), lambda s: (0, rev(s), 0)),
                   pl.BlockSpec((N_HEADS, PAIR, tm), lambda s: (0, 0, rev(s))),
                   row(3 * ch), pl.BlockSpec((3, 8, ch), lambda s: (0, 0, 0))],
        out_shape=[jax.ShapeDtypeStruct((N_HEADS, t, PAIR), BF16), jax.ShapeDtypeStruct((N_HEADS, PAIR, t), BF16),
                   jax.ShapeDtypeStruct((t, 3 * ch), BF16), jax.ShapeDtypeStruct((3, 8, ch), F32)],
        scratch_shapes=[pltpu.VMEM((CONV_HALO + tm, ch), F32), pltpu.VMEM((tm + CONV_HALO, ch), F32)],
        compiler_params=_cp(("arbitrary",)),
    )(dh, o, wout, bcx, bcx, bcx, bcx, bcx, conv_w)


def _attn_bwd(q_bwd, do_aug, q_bwd_t, do_aug_t, k_aug, v_aug_t, gblocks, *, tq=1024):
    t = q_bwd.shape[1]
    tq = min(tq, t)
    tk = tq // 2
    nq, nk = t // tq, t // tk
    n_pairs = ATTN_W // PAIR
    n_g = len(gblocks)

    def body(q_ref, do_ref, qt_ref, dot_ref, k_ref, vt_ref, *rest):
        dq_ref, dqx_ref, dk_ref, dkx_ref, dv_ref = rest[n_g:n_g + 5]
        dq_scr = rest[2 * n_g + 5]
        scatter = _Exchange(rest[:n_g], rest[n_g + 5:2 * n_g + 5], *rest[2 * n_g + 6:], gather=False)
        j = pl.program_id(1)

        @pl.when((pl.program_id(0) == 0) & (j == 0))
        def _():
            scatter.start()

        @pl.when(j == 0)
        def _():
            dq_scr[...] = jnp.zeros_like(dq_scr)
        k = [k_ref[0], k_ref[1]]
        vt = [vt_ref[0], vt_ref[1]]

        def step(i, carry, diag, rows=tq, row0=0):
            qs = pl.multiple_of(i * tq + row0, tk)
            if diag:
                row = lax.broadcasted_iota(jnp.int32, (rows, tk), 0)
                col = lax.broadcasted_iota(jnp.int32, (rows, tk), 1)
            out = []
            for hh in range(2):
                dk_a, dv_a = carry[hh]
                q = q_ref[hh, pl.ds(qs, rows), :]
                dov = do_ref[hh, pl.ds(qs, rows), :]
                p = jnp.exp2(_dot_nt(q, k[hh]))
                if diag:
                    p = jnp.where(col + (j * tk - i * tq - row0) <= row, p, 0.0)
                ds = (p * _dot(dov, vt[hh])).astype(BF16)
                dv_a = dv_a + _dot(dot_ref[hh, :, pl.ds(qs, rows)], p.astype(BF16))
                dk_a = dk_a + _dot(qt_ref[hh, :, pl.ds(qs, rows)], ds)
                dq_scr[hh, pl.ds(qs, rows), :] += _dot(ds, k[hh])
                out.append((dk_a, dv_a))
            return tuple(out)

        zero = (jnp.zeros((PAIR, tk), F32), jnp.zeros((PAIR, tk), F32))
        carry = lax.cond(j % 2 == 0, lambda c: step(j // 2, c, True),
                         lambda c: step(j // 2, c, True, rows=tk, row0=tk), (zero, zero))
        full0 = j // 2 + 1
        odd = (nq - full0) % 2
        carry = lax.cond(odd == 1, lambda c: step(full0, c, False), lambda c: c, carry)
        (dk0, dv0), (dk1, dv1) = lax.fori_loop(
            0, (nq - full0) // 2, lambda ii, c: step(full0 + odd + 2 * ii, c, False, rows=2 * tq), carry)
        first_t = lax.broadcasted_iota(jnp.int32, (PAIR, 1), 0) < HEAD_DIM
        first = lax.broadcasted_iota(jnp.int32, (1, PAIR), 1) < HEAD_DIM
        dk_ref[...] = (jnp.where(first_t, dk0, dk1) * (1.0 / LOG2E)).astype(BF16).T
        dkx_ref[...] = jnp.where(first_t, dk1, dk0).T
        dv_ref[...] = jnp.where(first_t, dv0, dv1).astype(BF16).T

        @pl.when(j == nk - 1)
        def _():
            dq_ref[...] = (jnp.where(first, dq_scr[0], dq_scr[1]) * Q_SCALE).astype(BF16)
            dqx_ref[...] = jnp.where(first, dq_scr[1], dq_scr[0])

        @pl.when((pl.program_id(0) == n_pairs - 1) & (j == nk - 1))
        def _():
            scatter.wait()

    resident = lambda: pl.BlockSpec((2, t, PAIR), lambda p, j: (p, 0, 0), pipeline_mode=pl.Buffered(1))
    resident_t = lambda: pl.BlockSpec((2, PAIR, t), lambda p, j: (p, 0, 0), pipeline_mode=pl.Buffered(1))
    kv_in = lambda: pl.BlockSpec((2, tk, PAIR), lambda p, j: (p, j, 0))
    whole = lambda: pl.BlockSpec((t, PAIR), lambda p, j: (0, p))
    tile = lambda: pl.BlockSpec((tk, PAIR), lambda p, j: (j, p))
    b16 = jax.ShapeDtypeStruct((t, ATTN_W), BF16)
    f32 = jax.ShapeDtypeStruct((t, ATTN_W), F32)
    res = pl.pallas_call(
        body, name="attn_bwd", grid=(n_pairs, nk),
        in_specs=[resident(), resident(), resident_t(), resident_t(), kv_in(),
                  pl.BlockSpec((2, PAIR, tk), lambda p, j: (p, 0, j))] + [HBM_SPEC] * n_g,
        out_specs=[whole(), whole(), tile(), tile(), tile()] + [HBM_SPEC] * n_g,
        out_shape=[b16, f32, b16, f32, b16] + [jax.ShapeDtypeStruct(g.shape, g.dtype) for g in gblocks],
        scratch_shapes=[pltpu.VMEM((2, t, PAIR), F32)] + _Exchange.scratch(n_g),
        compiler_params=_cp(("arbitrary", "arbitrary")),
    )(q_bwd, do_aug, q_bwd_t, do_aug_t, k_aug, v_aug_t, *gblocks)
    return res[:5], res[5:]


def _fgate_bwd(dqx, dkx, sgate, *, tm=256):
    t = sgate.shape[0]
    tm = min(tm, t)
    nsteps = t // tm

    def body(dq_ref, dk_ref, sg_ref, df_ref, dbf_ref, carry):
        @pl.when(pl.program_id(0) == 0)
        def _():
            carry[...] = jnp.zeros_like(carry)
            dbf_ref[...] = jnp.zeros_like(dbf_ref)
        lane = lax.broadcasted_iota(jnp.int32, (ATTN_W, F_PAD), 0)
        head = lax.broadcasted_iota(jnp.int32, (ATTN_W, F_PAD), 1)
        aux = (head // 2) * PAIR + HEAD_DIM * (1 - head % 2)
        valid = head < N_HEADS
        pick_r = (valid & (lane == aux + AUX_ROWSUM)).astype(F32)
        pick_c = (valid & (lane == aux + AUX_BIAS)).astype(F32)
        hp = lax.Precision.HIGHEST
        dcum = (jnp.dot(dq_ref[...], pick_r, preferred_element_type=F32, precision=lax.Precision.HIGH)
                + jnp.dot(dk_ref[...], pick_c, preferred_element_type=F32, precision=lax.Precision.HIGH))
        r = lax.broadcasted_iota(jnp.int32, (tm, tm), 0)
        c = lax.broadcasted_iota(jnp.int32, (tm, tm), 1)
        tri = (c >= r).astype(F32)
        rc = jnp.dot(tri, dcum, preferred_element_type=F32, precision=hp) + carry[...]
        carry[...] = rc[0:1, :]
        df = rc * sg_ref[...]
        df_ref[...] = df.astype(BF16)
        dbf_ref[...] += _rows8(df)

    rev = lambda i: nsteps - 1 - i
    return pl.pallas_call(
        body, name="fgate_bwd", grid=(nsteps,),
        in_specs=[pl.BlockSpec((tm, ATTN_W), lambda i: (rev(i), 0)), pl.BlockSpec((tm, ATTN_W), lambda i: (rev(i), 0)),
                  pl.BlockSpec((tm, F_PAD), lambda i: (rev(i), 0))],
        out_specs=[pl.BlockSpec((tm, F_PAD), lambda i: (rev(i), 0)), pl.BlockSpec((8, F_PAD), lambda i: (0, 0))],
        out_shape=[jax.ShapeDtypeStruct((t, F_PAD), BF16), jax.ShapeDtypeStruct((8, F_PAD), F32)],
        scratch_shapes=[pltpu.VMEM((1, F_PAD), F32)],
        compiler_params=_cp(("arbitrary",)),
    )(dqx, dkx, sgate)


def _inproj_bwd(dq, dk, dv, df, dbcx, dh, x, g, win_pt, gblock, *, tm=512):
    t, d = x.shape
    tm = min(tm, t)
    nsteps = t // tm
    n_qkv = 3 * ATTN_W

    def body(dq_ref, dk_ref, dv_ref, df_ref, db_ref, dh_ref, x_ref, g_ref, w_ref, gb_ref, gx_ref, dg_ref, land_ref,
             *sems):
        scatter = _Exchange([gb_ref], [land_ref], *sems, gather=False)

        @pl.when(pl.program_id(0) == 0)
        def _():
            scatter.start()
            dg_ref[...] = jnp.zeros_like(dg_ref)
        dn = _dot(df_ref[...], w_ref[n_qkv:n_qkv + F_PAD, :])
        for k, r in enumerate((dq_ref, dk_ref, dv_ref)):
            dn = dn + _dot(r[...], w_ref[k * ATTN_W:(k + 1) * ATTN_W, :])
        for k in range(3):
            c0 = n_qkv + F_PAD + k * CONV_CH
            dn = dn + _dot(db_ref[:, k * CONV_CH:(k + 1) * CONV_CH], w_ref[c0:c0 + CONV_CH, :])
        dx, dg = _norm_bwd(dn, x_ref[...], g_ref[...])
        gx_ref[...] = dh_ref[...] + dx
        dg_ref[...] += dg

        @pl.when(pl.program_id(0) == nsteps - 1)
        def _():
            scatter.wait()

    row = lambda n_: pl.BlockSpec((tm, n_), lambda i: (i, 0))
    return pl.pallas_call(
        body, name="inproj_bwd", grid=(nsteps,),
        in_specs=[row(ATTN_W), row(ATTN_W), row(ATTN_W), row(F_PAD), row(3 * CONV_CH), row(d), row(d),
                  _const_spec((1, d)), _const_spec(win_pt.shape), HBM_SPEC],
        out_specs=[row(d), pl.BlockSpec((8, d), lambda i: (0, 0)), HBM_SPEC],
        out_shape=[jax.ShapeDtypeStruct((t, d), F32), jax.ShapeDtypeStruct((8, d), F32),
                   jax.ShapeDtypeStruct(gblock.shape, gblock.dtype)],
        scratch_shapes=_Exchange.scratch(1),
        compiler_params=_cp(("arbitrary",)),
    )(dq, dk, dv, df, dbcx, dh, x, g, win_pt, gblock)


LATE = ("w_out_0", "w_up_0", "w_down_0", "pool_w_1", "w_up_1", "w_down_1")


def _local_step(x, target, gains, b_f, conv_w, pool_scale, win_pt, shards):
    d = x.shape[1]
    n0, qkv, flog, bcx, cv = _norm_inproj(x, gains["mix0"], win_pt, conv_w)
    q_aug_t, k_aug, v_aug_t, sgate = _fgate_prep(flog, b_f, qkv)
    att, q_bwd, q_bwd_t, gathered = _attn_fwd(q_aug_t, k_aug, v_aug_t, [shards[n] for n in LATE])
    g = dict(zip(LATE, gathered))
    wout = g["w_out_0"].reshape(d, d)
    wup0, wup1 = g["w_up_0"], g["w_up_1"]
    wdown0, wdown1 = g["w_down_0"].reshape(-1, d), g["w_down_1"].reshape(-1, d)
    n_grp = len(POOL_WINDOWS)
    cg = d // n_grp
    poolw = g["pool_w_1"].reshape(N_DEV, n_grp, cg // N_DEV, cg).transpose(1, 0, 2, 3).reshape(n_grp, cg, cg)
    h1 = _outproj(att, cv, x, wout)
    h2, n1, a0, z0 = _mlp_fwd(h1, gains["ffn0"], wup0, wdown0, name="mlp_fwd0")
    h3, pooled = _pool_fwd(h2, gains["mix1"], poolw, pool_scale)
    loss, dh4, dh4_b, dg_final, n3, a1, z1 = _mlp_fwd_loss(h3, gains["ffn1"], wup1, wdown1, gains["final"], target,
                                                           name="mlp_fwd1")
    f = a1.shape[1]
    fb = f // N_DEV
    dh3, da1, dg_ffn1 = _mlp_bwd(dh4, h3, a1, gains["ffn1"], wup1, wdown1, name="mlp_bwd1")
    dwdown1 = _mm_tn(z1, dh4_b, name="dwdown1", ta=1024, tb=1024, tt=4096, out_dtype=BF16)
    dwup1 = _mm_tn(n3, da1, name="dwup1", ta=d, tb=fb, tt=4096, blocked_out=True, out_dtype=BF16)
    dh2, dh2_b, dpoolw, dscale, dg_mix1 = _pool_bwd(dh3, h2, pooled, gains["mix1"], poolw, pool_scale)
    dh1, da0, dg_ffn0 = _mlp_bwd(dh2, h1, a0, gains["ffn0"], wup0, wdown0, name="mlp_bwd0")
    dwdown0 = _mm_tn(z0, dh2_b, name="dwdown0", ta=1024, tb=1024, tt=4096, out_dtype=BF16)
    dwup0 = _mm_tn(n1, da0, name="dwup0", ta=d, tb=fb, tt=4096, blocked_out=True, out_dtype=BF16)
    do_aug, do_aug_t, dbcx, dconvw = _outproj_conv_bwd(dh1, att, wout, bcx, conv_w)
    dwout = _mm_tn_cat([att, cv], [dh1], name="dwout", tt=2048)
    gblocks = {
        "w_out_0": dwout.reshape(N_DEV, d // N_DEV, d), "w_up_0": dwup0, "w_up_1": dwup1,
        "w_down_0": dwdown0.reshape(N_DEV, -1, d), "w_down_1": dwdown1.reshape(N_DEV, -1, d),
        "pool_w_1": dpoolw.astype(BF16).reshape(n_grp, N_DEV, cg // N_DEV, cg).transpose(1, 0, 2, 3).reshape(
            N_DEV, n_grp * (cg // N_DEV), cg),
    }
    (dq, dqx, dk, dkx, dv), landed = _attn_bwd(q_bwd, do_aug, q_bwd_t, do_aug_t, k_aug, v_aug_t,
                                               [gblocks[n] for n in LATE])
    df, dbf = _fgate_bwd(dqx, dkx, sgate)
    dwin_t = jnp.concatenate(
        [_mm_tn_cat([dq, dk, dv], [n0], name="dwin_qkv", tt=2048),
         _mm_tn(df, n0, name="dwin_f", ta=F_PAD, tb=d, tt=2048, out_dtype=BF16)[:N_HEADS],
         _mm_tn(dbcx, n0, name="dwin_bcx", ta=512, tb=d, tt=4096, out_dtype=BF16)], axis=0)
    dwin_blocks = dwin_t.reshape(N_DEV, dwin_t.shape[0] // N_DEV, d)
    grad_x, dg_mix0, landed_win = _inproj_bwd(dq, dk, dv, df, dbcx, dh1, x, gains["mix0"], win_pt, dwin_blocks)
    small = dict(mix0=dg_mix0, ffn0=dg_ffn0, mix1=dg_mix1, pool_scale=dscale, ffn1=dg_ffn1, final=dg_final,
                 b_f=dbf, conv_w=dconvw)
    return loss, grad_x, dict(zip(LATE + ("w_in_0",), tuple(landed) + (landed_win,))), small


def _all_gather(shards):
    n = len(shards)

    def body(*refs):
        gather = _TwoLevelGather(refs[:n], refs[n:2 * n], *refs[2 * n:])
        gather.start()
        gather.forward()
        gather.wait()

    return pl.pallas_call(
        body, name="all_gather",
        in_specs=[HBM_SPEC] * n, out_specs=[HBM_SPEC] * n,
        out_shape=[jax.ShapeDtypeStruct((N_DEV,) + s.shape, s.dtype) for s in shards],
        scratch_shapes=[pltpu.SemaphoreType.DMA((7 * n,)), pltpu.SemaphoreType.DMA((7 * n,)),
                        pltpu.SemaphoreType.DMA((n,))],
    )(*shards)


SMALL_ROWS = 16


def _small_allreduce(parts):
    n, _, w = parts.shape
    assert n <= SMALL_ROWS

    def body(p_ref, o_ref, gath, send_sems, recv_sems):
        x, y, c = lax.axis_index("x"), lax.axis_index("y"), lax.axis_index("c")
        my = _slot(x, y, c)
        rows = [jnp.sum(p_ref[i], axis=0, keepdims=True) for i in range(n)]
        rows.append(jnp.zeros((SMALL_ROWS - n, w), F32))
        gath[my] = jnp.concatenate(rows, axis=0)
        copies = []
        for k in range(1, N_DEV):
            px, py, pc = x ^ (k >> 2), y ^ ((k >> 1) & 1), c ^ (k & 1)
            cp = pltpu.make_async_remote_copy(
                src_ref=gath.at[my], dst_ref=gath.at[my], send_sem=send_sems.at[k - 1], recv_sem=recv_sems.at[k - 1],
                device_id=(px, py, pc), device_id_type=MESH)
            cp.start()
            copies.append(cp)
        for cp in copies:
            cp.wait()
        acc = gath[0]
        for d in range(1, N_DEV):
            acc = acc + gath[d]
        o_ref[...] = acc

    return pl.pallas_call(
        body, name="small_allreduce",
        in_specs=[VMEM_SPEC], out_specs=VMEM_SPEC,
        out_shape=jax.ShapeDtypeStruct((SMALL_ROWS, w), F32),
        scratch_shapes=[pltpu.VMEM((N_DEV, SMALL_ROWS, w), F32), pltpu.SemaphoreType.DMA((N_DEV - 1,)),
                        pltpu.SemaphoreType.DMA((N_DEV - 1,))],
    )(parts)


def _adamw(g, w, m, v, *, name, tm=256):
    r, c = g.shape
    tm = tm if r % tm == 0 else r
    bc1 = 1.0 - ADAM_B1 ** ADAM_STEP
    bc2 = 1.0 - ADAM_B2 ** ADAM_STEP

    def body(g_ref, w_ref, m_ref, v_ref, d_ref, nm_ref, nv_ref):
        gv = g_ref[...]
        nm = ADAM_B1 * m_ref[...] + (1.0 - ADAM_B1) * gv
        nv = ADAM_B2 * v_ref[...] + (1.0 - ADAM_B2) * jnp.square(gv)
        nm_ref[...] = nm
        nv_ref[...] = nv
        d_ref[...] = -ADAM_LR * ((nm / bc1) / (jnp.sqrt(nv / bc2) + ADAM_EPS) + ADAM_WD * w_ref[...])

    blk = pl.BlockSpec((tm, c), lambda i: (i, 0))
    shp = jax.ShapeDtypeStruct((r, c), F32)
    return pl.pallas_call(
        body, name=name, grid=(r // tm,), in_specs=[blk] * 4, out_specs=[blk] * 3, out_shape=[shp] * 3,
        compiler_params=_cp(("parallel",)),
    )(g, w, m, v)


def _transpose_cast(a, *, name):
    def body(a_ref, o_ref):
        o_ref[...] = a_ref[...].T.astype(BF16)

    return pl.pallas_call(body, name=name, out_shape=jax.ShapeDtypeStruct(a.shape[::-1], BF16),
                          compiler_params=_cp())(a)


def _adamw_sum_t(parts, w, m, v, *, name):
    bc1 = 1.0 - ADAM_B1 ** ADAM_STEP
    bc2 = 1.0 - ADAM_B2 ** ADAM_STEP

    def body(p_ref, w_ref, m_ref, v_ref, g_ref, d_ref, nm_ref, nv_ref):
        acc = p_ref[0].astype(F32)
        for k in range(1, N_DEV):
            acc = acc + p_ref[k].astype(F32)
        gv = acc.T
        g_ref[...] = gv
        nm = ADAM_B1 * m_ref[...] + (1.0 - ADAM_B1) * gv
        nv = ADAM_B2 * v_ref[...] + (1.0 - ADAM_B2) * jnp.square(gv)
        nm_ref[...] = nm
        nv_ref[...] = nv
        d_ref[...] = -ADAM_LR * ((nm / bc1) / (jnp.sqrt(nv / bc2) + ADAM_EPS) + ADAM_WD * w_ref[...])

    shp = jax.ShapeDtypeStruct(w.shape, F32)
    return pl.pallas_call(body, name=name, out_shape=[shp] * 4, compiler_params=_cp())(parts, w, m, v)


def _adamw_sum(parts, w, m, v, *, name, tm=256):
    _, r, c = parts.shape
    tm = tm if r % tm == 0 else r
    bc1 = 1.0 - ADAM_B1 ** ADAM_STEP
    bc2 = 1.0 - ADAM_B2 ** ADAM_STEP

    def body(p_ref, w_ref, m_ref, v_ref, g_ref, d_ref, nm_ref, nv_ref):
        gv = p_ref[0].astype(F32)
        for k in range(1, N_DEV):
            gv = gv + p_ref[k].astype(F32)
        g_ref[...] = gv
        nm = ADAM_B1 * m_ref[...] + (1.0 - ADAM_B1) * gv
        nv = ADAM_B2 * v_ref[...] + (1.0 - ADAM_B2) * jnp.square(gv)
        nm_ref[...] = nm
        nv_ref[...] = nv
        d_ref[...] = -ADAM_LR * ((nm / bc1) / (jnp.sqrt(nv / bc2) + ADAM_EPS) + ADAM_WD * w_ref[...])

    blk = pl.BlockSpec((tm, c), lambda i: (i, 0))
    shp = jax.ShapeDtypeStruct((r, c), F32)
    return pl.pallas_call(
        body, name=name, grid=(r // tm,), in_specs=[pl.BlockSpec((N_DEV, tm, c), lambda i: (0, i, 0))] + [blk] * 3,
        out_specs=[blk] * 4, out_shape=[shp] * 4, compiler_params=_cp(("parallel",)),
    )(parts, w, m, v)


BIG = ("w_in_0", "w_out_0", "w_up_0", "w_down_0", "pool_w_1", "w_up_1", "w_down_1")
SMALL = ("norm_mix_0", "norm_ffn_0", "norm_mix_1", "pool_scale_1", "norm_ffn_1", "final_norm", "b_f_0", "conv_w_0")
WEIGHTS = ("norm_mix_0", "w_in_0", "b_f_0", "conv_w_0", "w_out_0", "norm_ffn_0", "w_up_0", "w_down_0", "norm_mix_1",
           "pool_w_1", "pool_scale_1", "norm_ffn_1", "w_up_1", "w_down_1", "final_norm")


def _pad_to(a, rows, cols):
    return jnp.pad(a, ((0, rows - a.shape[0]), (0, cols - a.shape[1])))


def _pack_small(p, width):
    rows = [p[n].reshape(1, -1) for n in SMALL[:6]]
    rows.append(_pad_to(p["b_f_0"].reshape(1, -1), 1, width))
    rows.append(_pad_to(p["conv_w_0"], 3, width))
    return _pad_to(jnp.concatenate(rows, axis=0), SMALL_ROWS, width)


def _unpack_small(a, like):
    out = {n: a[i] for i, n in enumerate(SMALL[:6])}
    out["b_f_0"] = a[6, :like["b_f_0"].shape[0]]
    out["conv_w_0"] = a[7:10, :like["conv_w_0"].shape[1]]
    return out


def kernel(x, norm_mix_0, w_in_0, b_f_0, conv_w_0, w_out_0, norm_ffn_0, w_up_0, w_down_0, norm_mix_1, pool_w_1, pool_scale_1, norm_ffn_1, w_up_1, w_down_1, final_norm, loss_target, m_norm_mix_0, m_w_in_0, m_b_f_0, m_conv_w_0, m_w_out_0, m_norm_ffn_0, m_w_up_0, m_w_down_0, m_norm_mix_1, m_pool_w_1, m_pool_scale_1, m_norm_ffn_1, m_w_up_1, m_w_down_1, m_final_norm, v_norm_mix_0, v_w_in_0, v_b_f_0, v_conv_w_0, v_w_out_0, v_norm_ffn_0, v_w_up_0, v_w_down_0, v_norm_mix_1, v_pool_w_1, v_pool_scale_1, v_norm_ffn_1, v_w_up_1, v_w_down_1, v_final_norm):
    w = dict(norm_mix_0=norm_mix_0, w_in_0=w_in_0, b_f_0=b_f_0, conv_w_0=conv_w_0, w_out_0=w_out_0,
             norm_ffn_0=norm_ffn_0, w_up_0=w_up_0, w_down_0=w_down_0, norm_mix_1=norm_mix_1, pool_w_1=pool_w_1,
             pool_scale_1=pool_scale_1, norm_ffn_1=norm_ffn_1, w_up_1=w_up_1, w_down_1=w_down_1, final_norm=final_norm)
    m = dict(norm_mix_0=m_norm_mix_0, w_in_0=m_w_in_0, b_f_0=m_b_f_0, conv_w_0=m_conv_w_0, w_out_0=m_w_out_0,
             norm_ffn_0=m_norm_ffn_0, w_up_0=m_w_up_0, w_down_0=m_w_down_0, norm_mix_1=m_norm_mix_1,
             pool_w_1=m_pool_w_1, pool_scale_1=m_pool_scale_1, norm_ffn_1=m_norm_ffn_1, w_up_1=m_w_up_1,
             w_down_1=m_w_down_1, final_norm=m_final_norm)
    v = dict(norm_mix_0=v_norm_mix_0, w_in_0=v_w_in_0, b_f_0=v_b_f_0, conv_w_0=v_conv_w_0, w_out_0=v_w_out_0,
             norm_ffn_0=v_norm_ffn_0, w_up_0=v_w_up_0, w_down_0=v_w_down_0, norm_mix_1=v_norm_mix_1,
             pool_w_1=v_pool_w_1, pool_scale_1=v_pool_scale_1, norm_ffn_1=v_norm_ffn_1, w_up_1=v_w_up_1,
             w_down_1=v_w_down_1, final_norm=v_final_norm)
    d = x.shape[-1]
    n_in = w_in_0.shape[1] * N_DEV
    n_qkv = 3 * ATTN_W
    pool_g, pool_rows, pool_c = pool_w_1.shape

    def shard2d(p):
        return {n: (p[n].reshape(pool_g * pool_rows, pool_c) if n == "pool_w_1" else p[n]) for n in BIG}
    w2, m2, v2 = shard2d(w), shard2d(m), shard2d(v)

    conv_cols = conv_w_0.shape[1]
    win_g8, conv_g8 = _all_gather([_transpose_cast(w_in_0, name="w_in_t"), _pad_to(conv_w_0, 8, 128)])
    conv_full = conv_g8[:, :, :conv_cols].transpose(1, 0, 2).reshape(8, N_DEV * conv_cols)
    win_t = win_g8.reshape(n_in, d)
    win_pt = jnp.concatenate([win_t[:n_qkv], _pad_to(win_t[n_qkv:n_qkv + N_HEADS], F_PAD, d),
                              win_t[n_qkv + N_HEADS:]], axis=0)

    gains = dict(mix0=norm_mix_0.reshape(1, d), ffn0=norm_ffn_0.reshape(1, d), mix1=norm_mix_1.reshape(1, d),
                 ffn1=norm_ffn_1.reshape(1, d), final=final_norm.reshape(1, d))
    dev = _slot(lax.axis_index("x"), lax.axis_index("y"), lax.axis_index("c"))
    loss8, grad_x, landed, small = _local_step(
        x[0], loss_target[0], gains, _pad_to(b_f_0.reshape(1, -1), 1, F_PAD), conv_full, pool_scale_1.reshape(1, d),
        win_pt, {n: w2[n].astype(BF16) for n in LATE})
    parts = jnp.concatenate(
        [small[k][None] for k in ("mix0", "ffn0", "mix1", "pool_scale", "ffn1", "final")]
        + [_pad_to(small["b_f"], 8, d)[None], jnp.pad(small["conv_w"], ((0, 0), (0, 0), (0, d - CONV_CH))),
           _pad_to(loss8[0:1, 0:1], 8, d)[None]], axis=0)
    tot = _small_allreduce(parts)
    loss = tot[10, 0]
    conv_g = lax.dynamic_slice(tot, (7, dev * conv_cols), (3, conv_cols))
    gs = tot.at[7:10].set(_pad_to(conv_g, 3, d))

    grads, deltas, new_m, new_v = {}, {}, {}, {}
    for n in BIG:
        if n in LATE:
            gr, dl, nm, nv = _adamw_sum(landed[n], w2[n], m2[n], v2[n], name="adamw_" + n)
        else:
            gr, dl, nm, nv = _adamw_sum_t(landed[n], w2[n], m2[n], v2[n], name="adamw_" + n)
        for dst, val in ((grads, gr), (deltas, dl), (new_m, nm), (new_v, nv)):
            dst[n] = val.reshape(w[n].shape)
    dl, nm, nv = _adamw(gs, _pack_small(w, d), _pack_small(m, d), _pack_small(v, d), name="adamw_small")
    for dst, val in ((grads, gs), (deltas, dl), (new_m, nm), (new_v, nv)):
        dst.update(_unpack_small(val, w))
    return (loss, grad_x[None], *[grads[n] for n in WEIGHTS], *[deltas[n] for n in WEIGHTS],
            *[new_m[n] for n in WEIGHTS], *[new_v[n] for n in WEIGHTS])
```

```python
import functools

import jax
import jax.numpy as jnp
from jax import lax
from jax.experimental import pallas as pl
from jax.experimental.pallas import tpu as pltpu

F32 = jnp.float32
BF16 = jnp.bfloat16

N_DEV = 8
N_HEADS = 8
HEAD_DIM = 64
PAIR = 2 * HEAD_DIM
ATTN_W = N_HEADS * HEAD_DIM
CONV_CH = 512
F_PAD = 128
POOL_WINDOWS = (2, 4, 8, 16)
POOL_HALO = 16
CONV_HALO = 16
RMS_EPS = 1e-6
Q_SCALE = HEAD_DIM ** -0.5
LOG2E = 1.4426950408889634
NEG = -1e30
AUX_BIAS = 0
AUX_LSE = 3
AUX_ROWSUM = 6
ADAM_LR, ADAM_B1, ADAM_B2, ADAM_EPS, ADAM_WD, ADAM_STEP = 0.001, 0.9, 0.999, 1e-08, 0.01, 10
MESH = pl.DeviceIdType.MESH
VMEM_LIMIT = 56 * 2**20


def _cp(sem=None, vmem=VMEM_LIMIT, **kw):
    return pltpu.CompilerParams(dimension_semantics=sem, vmem_limit_bytes=vmem, **kw)


def _dot(a, b):
    return jnp.dot(a, b, preferred_element_type=F32)


def _dot_nt(a, b):
    return lax.dot_general(a, b, (((1,), (1,)), ((), ())), preferred_element_type=F32)


def _dot_tn(a, b):
    return lax.dot_general(a, b, (((0,), (0,)), ((), ())), preferred_element_type=F32)


def _rstd(h):
    return lax.rsqrt(jnp.mean(h * h, axis=-1, keepdims=True) + RMS_EPS)


def _rows8(x):
    r, n = x.shape
    return jnp.sum(x.reshape(r // 8, 8, n), axis=0)


def _norm_bwd(dn, h, g):
    r = _rstd(h)
    xhat = h * r
    dy = dn * g
    dh = r * (dy - xhat * jnp.mean(dy * xhat, axis=-1, keepdims=True))
    return dh, _rows8(dn * xhat)


def _const_spec(shape):
    nd = len(shape)
    return pl.BlockSpec(shape, lambda *_: (0,) * nd, pipeline_mode=pl.Buffered(1))


HBM_SPEC = pl.BlockSpec(memory_space=pltpu.HBM)
VMEM_SPEC = pl.BlockSpec(memory_space=pltpu.VMEM)


def _slot(px, py, pc):
    return 4 * px + 2 * py + pc


class _Exchange:
    def __init__(self, srcs, dsts, send_sems, recv_sems, local_sems, gather):
        x, y, c = lax.axis_index("x"), lax.axis_index("y"), lax.axis_index("c")
        me = _slot(x, y, c)
        self.copies = []
        for a, (src, dst) in enumerate(zip(srcs, dsts)):
            self.copies.append(pltpu.make_async_copy(src if gather else src.at[me], dst.at[me], local_sems.at[a]))
            for k in range(1, N_DEV):
                px, py, pc = x ^ (k >> 2), y ^ ((k >> 1) & 1), c ^ (k & 1)
                self.copies.append(pltpu.make_async_remote_copy(
                    src_ref=src if gather else src.at[_slot(px, py, pc)], dst_ref=dst.at[me],
                    send_sem=send_sems.at[(N_DEV - 1) * a + k - 1], recv_sem=recv_sems.at[(N_DEV - 1) * a + k - 1],
                    device_id=(px, py, pc), device_id_type=MESH))

    def start(self):
        for cp in self.copies:
            cp.start()

    def wait(self):
        for cp in self.copies:
            cp.wait()

    @staticmethod
    def scratch(n):
        return [pltpu.SemaphoreType.DMA(((N_DEV - 1) * n,)), pltpu.SemaphoreType.DMA(((N_DEV - 1) * n,)),
                pltpu.SemaphoreType.DMA((n,))]


def _mesh_places():
    x, y, c = lax.axis_index("x"), lax.axis_index("y"), lax.axis_index("c")
    chips = [(1 - x, y), (x, 1 - y), (1 - x, 1 - y)]
    return (x, y, c), (x, y, 1 - c), chips


class _TwoLevelGather:
    def __init__(self, srcs, dsts, send_sems, recv_sems, local_sems):
        me, sib, chips = _mesh_places()
        c = me[2]
        n = len(srcs)

        def copy(a, k, block, to, src=None):
            dst = dsts[a].at[_slot(*block)]
            return pltpu.make_async_remote_copy(
                src_ref=dst if src is None else src, dst_ref=dst, send_sem=send_sems.at[7 * a + k],
                recv_sem=recv_sems.at[7 * a + k], device_id=to, device_id_type=MESH)

        self.mine = [pltpu.make_async_copy(srcs[a], dsts[a].at[_slot(*me)], local_sems.at[a]) for a in range(n)]
        self.first, self.landed, self.passed, self.rest = [], [], [], []
        for a in range(n):
            self.first.append(copy(a, 0, me, sib, src=srcs[a]))
            self.first += [copy(a, 1 + j, me, (*chip, c), src=srcs[a]) for j, chip in enumerate(chips)]
            self.landed += [copy(a, 1 + j, (*chip, c), me) for j, chip in enumerate(chips)]
            self.passed += [copy(a, 4 + j, (*chip, c), sib) for j, chip in enumerate(chips)]
            self.rest.append(copy(a, 0, sib, me))
            self.rest += [copy(a, 4 + j, (*chip, 1 - c), me) for j, chip in enumerate(chips)]

    def start(self):
        for cp in self.mine + self.first:
            cp.start()

    def forward(self):
        for arrived, onward in zip(self.landed, self.passed):
            arrived.wait_recv()
            onward.start()

    def wait(self):
        for cp in self.rest:
            cp.wait_recv()
        for cp in self.first + self.passed:
            cp.wait_send()
        for cp in self.mine:
            cp.wait()


def _norm_inproj(x, g, win_pt, conv_w, *, tm=512):
    t, d = x.shape
    n_all = win_pt.shape[0]
    n_qkv = 3 * ATTN_W
    n_bcx = 3 * CONV_CH
    assert n_all == n_qkv + F_PAD + n_bcx
    tm = min(tm, t)
    ch = CONV_CH

    def body(x_ref, g_ref, w_ref, cw_ref, n_ref, qkv_ref, f_ref, bcx_ref, cv_ref, ext):
        h = x_ref[...]
        n = (h * _rstd(h) * g_ref[...]).astype(BF16)
        n_ref[...] = n
        for c0 in range(0, n_qkv, 512):
            acc = _dot_nt(n, w_ref[c0:c0 + 512, :])
            if c0 < ATTN_W:
                acc = acc * (Q_SCALE * LOG2E)
            qkv_ref[:, c0:c0 + 512] = acc.astype(BF16)
        f_ref[...] = _dot_nt(n, w_ref[n_qkv:n_qkv + F_PAD, :])
        bcx = []
        for k in range(3):
            c0 = n_qkv + F_PAD + k * ch
            v = _dot_nt(n, w_ref[c0:c0 + ch, :]).astype(BF16)
            bcx_ref[:, k * ch:(k + 1) * ch] = v
            bcx.append(v.astype(F32))
        @pl.when(pl.program_id(0) == 0)
        def _():
            ext[tm:tm + CONV_HALO, :] = jnp.zeros((CONV_HALO, ch), F32)
        ext[0:CONV_HALO, :] = ext[tm:tm + CONV_HALO, :]
        ext[CONV_HALO:CONV_HALO + tm, :] = bcx[1] * bcx[2]
        conv = (cw_ref[0:1, :] * ext[CONV_HALO - 2:CONV_HALO - 2 + tm, :]
                + cw_ref[1:2, :] * ext[CONV_HALO - 1:CONV_HALO - 1 + tm, :]
                + cw_ref[2:3, :] * ext[CONV_HALO:CONV_HALO + tm, :])
        cv_ref[...] = (bcx[0] * conv).astype(BF16)

    return pl.pallas_call(
        body, name="norm_inproj", grid=(t // tm,),
        in_specs=[pl.BlockSpec((tm, d), lambda i: (i, 0)), _const_spec((1, d)), _const_spec((n_all, d)),
                  _const_spec((8, ch))],
        out_specs=[pl.BlockSpec((tm, d), lambda i: (i, 0)), pl.BlockSpec((tm, n_qkv), lambda i: (i, 0)),
                   pl.BlockSpec((tm, F_PAD), lambda i: (i, 0)), pl.BlockSpec((tm, n_bcx), lambda i: (i, 0)),
                   pl.BlockSpec((tm, ch), lambda i: (i, 0))],
        out_shape=[jax.ShapeDtypeStruct((t, d), BF16), jax.ShapeDtypeStruct((t, n_qkv), BF16),
                   jax.ShapeDtypeStruct((t, F_PAD), F32), jax.ShapeDtypeStruct((t, n_bcx), BF16),
                   jax.ShapeDtypeStruct((t, ch), BF16)],
        scratch_shapes=[pltpu.VMEM((CONV_HALO + tm, ch), F32)],
        compiler_params=_cp(("arbitrary",)),
    )(x, g, win_pt, conv_w)


def _head_lanes(h):
    lane = lax.broadcasted_iota(jnp.int32, (1, PAIR), 1)
    hh = h % 2
    return lane, lane // HEAD_DIM == hh, HEAD_DIM * (1 - hh)


def _pieces(col):
    hi = col.astype(BF16).astype(F32)
    r1 = col - hi
    mid = r1.astype(BF16).astype(F32)
    lo = (r1 - mid).astype(BF16).astype(F32)
    return hi, mid, lo


def _put_pieces(lane, first, col, other):
    hi, mid, lo = _pieces(col)
    return jnp.where(lane == first, hi, jnp.where(lane == first + 1, mid, jnp.where(lane == first + 2, lo, other)))


def _fgate_prep(flog, b_f, qkv, *, tm=512):
    t = flog.shape[0]
    tm = min(tm, t)

    def body(f_ref, b_ref, qkv_ref, qat_ref, ka_ref, vat_ref, sg_ref, carry):
        @pl.when(pl.program_id(0) == 0)
        def _():
            carry[...] = jnp.zeros_like(carry)
        z = f_ref[...] + b_ref[...]
        e = jnp.exp(-jnp.abs(z))
        logf = jnp.minimum(z, 0.0) - jnp.log(1.0 + e)
        sg_ref[...] = jnp.where(z >= 0, e, 1.0) / (1.0 + e)
        r = lax.broadcasted_iota(jnp.int32, (tm, tm), 0)
        c = lax.broadcasted_iota(jnp.int32, (tm, tm), 1)
        tri = (c <= r).astype(F32)
        cs = jnp.dot(tri, logf, preferred_element_type=F32, precision=lax.Precision.HIGHEST) + carry[...]
        carry[...] = cs[tm - 1:tm, :]
        cs2 = cs * LOG2E
        for h in range(N_HEADS):
            lane, head, aux = _head_lanes(h)
            p0 = (h // 2) * PAIR
            ones = ((lane >= aux + AUX_LSE) & (lane <= aux + AUX_ROWSUM)).astype(F32)
            bias = (lane >= aux + AUX_BIAS) & (lane < aux + AUX_BIAS + 3)
            k_aux = _put_pieces(lane, aux + AUX_BIAS, cs2[:, h:h + 1], ones)
            q_aug = jnp.where(head, qkv_ref[:, p0:p0 + PAIR], jnp.where(bias, -1.0, 0.0).astype(BF16))
            v_aug = jnp.where(head, qkv_ref[:, 2 * ATTN_W + p0:2 * ATTN_W + p0 + PAIR],
                              jnp.where(bias, 1.0, 0.0).astype(BF16))
            qat_ref[h] = q_aug.T
            ka_ref[h] = jnp.where(head, qkv_ref[:, ATTN_W + p0:ATTN_W + p0 + PAIR], k_aux.astype(BF16))
            vat_ref[h] = v_aug.T

    aug = lambda: pl.BlockSpec((N_HEADS, tm, PAIR), lambda i: (0, i, 0))
    aug_t = lambda: pl.BlockSpec((N_HEADS, PAIR, tm), lambda i: (0, 0, i))
    aug_shape = jax.ShapeDtypeStruct((N_HEADS, t, PAIR), BF16)
    aug_t_shape = jax.ShapeDtypeStruct((N_HEADS, PAIR, t), BF16)
    return pl.pallas_call(
        body, name="fgate_prep", grid=(t // tm,),
        in_specs=[pl.BlockSpec((tm, F_PAD), lambda i: (i, 0)), _const_spec((1, F_PAD)),
                  pl.BlockSpec((tm, 3 * ATTN_W), lambda i: (i, 0))],
        out_specs=[aug_t(), aug(), aug_t(), pl.BlockSpec((tm, F_PAD), lambda i: (i, 0))],
        out_shape=[aug_t_shape, aug_shape, aug_t_shape, jax.ShapeDtypeStruct((t, F_PAD), F32)],
        scratch_shapes=[pltpu.VMEM((1, F_PAD), F32)],
        compiler_params=_cp(("arbitrary",)),
    )(flog, b_f, qkv)


def _put_pieces_t(row, first, vec, other):
    hi, mid, lo = _pieces(vec)
    return jnp.where(row == first, hi, jnp.where(row == first + 1, mid, jnp.where(row == first + 2, lo, other)))


def _attn_fwd(q_aug_t, k_aug, v_aug_t, shards, *, tq=1024):
    t = k_aug.shape[1]
    tq = min(tq, t)
    tk = tq // 2
    nq = t // tq
    n_pairs = ATTN_W // PAIR
    n_sh = len(shards)
    forward_step = (11 * n_pairs * nq) // 16

    def body(qt_ref, k_ref, vt_ref, *rest):
        o_ref, qb_ref, qbt_ref = rest[n_sh:n_sh + 3]
        s_scr = rest[2 * n_sh + 3]
        gather = _TwoLevelGather(rest[:n_sh], rest[n_sh + 3:2 * n_sh + 3], *rest[2 * n_sh + 4:])
        i = pl.program_id(1)
        step = pl.program_id(0) * nq + i

        @pl.when(step == 0)
        def _():
            gather.start()

        @pl.when(step == forward_step)
        def _():
            gather.forward()
        key = lax.broadcasted_iota(jnp.int32, (tk, tq), 0)
        qry = lax.broadcasted_iota(jnp.int32, (tk, tq), 1)
        qt = [qt_ref[0], qt_ref[1]]

        def logits(hh, tile, slot, diag):
            s = _dot(k_ref[hh, pl.ds(pl.multiple_of(tile * tk, tk), tk), :], qt[hh])
            if diag:
                s = jnp.where(key + (tile * tk - i * tq) <= qry, s, NEG)
            s_scr[hh, slot] = s
            return jnp.max(s, axis=0, keepdims=True)

        def probs(hh, tile, slot, m, acc, tmax):
            mn = jnp.maximum(m, tmax)
            p = jnp.exp2(s_scr[hh, slot] - mn).astype(BF16)
            acc = jnp.exp2(m - mn) * acc + _dot(vt_ref[hh, :, pl.ds(pl.multiple_of(tile * tk, tk), tk)], p)
            return mn, acc

        def advance(carry, prev, slot, nxt, diag=False):
            out = []
            for hh in range(2):
                m, acc, tmax = carry[hh]
                m, acc = probs(hh, prev, slot, m, acc, tmax)
                out.append((m, acc, logits(hh, nxt, 1 - slot, diag)))
            return tuple(out)

        def two_tiles(jj, carry):
            carry = advance(carry, jnp.where(jj == 0, 2 * i, 2 * jj - 1), 1, 2 * jj)
            return advance(carry, 2 * jj, 0, 2 * jj + 1)

        init = tuple((jnp.full((1, tq), NEG, F32), jnp.zeros((PAIR, tq), F32), logits(hh, 2 * i + 1, 0, True))
                     for hh in range(2))
        carry = advance(init, 2 * i + 1, 0, 2 * i, diag=True)
        carry = lax.fori_loop(0, i // 2, lambda jj, c: two_tiles(2 * jj + 1, two_tiles(2 * jj, c)), carry)
        carry = lax.cond(i % 2 == 1, lambda c: two_tiles(i - 1, c), lambda c: c, carry)
        last = jnp.where(i == 0, 2 * i, 2 * i - 1)
        row = lax.broadcasted_iota(jnp.int32, (PAIR, 1), 0)
        res = []
        for hh in range(2):
            aux = HEAD_DIM * (1 - hh)
            m, acc, tmax = carry[hh]
            m, acc = probs(hh, last, 1, m, acc, tmax)
            l = acc[aux + AUX_BIAS:aux + AUX_BIAS + 1, :]
            qbt = _put_pieces_t(row, aux + AUX_LSE, -(m + jnp.log2(l)), qt[hh].astype(F32))
            qbt_ref[hh] = qbt.astype(BF16)
            qb_ref[hh] = qbt.astype(BF16).T
            res.append(acc * (1.0 / l))
        o_ref[...] = jnp.where(row < HEAD_DIM, res[0], res[1]).astype(BF16).T

        @pl.when((pl.program_id(0) == n_pairs - 1) & (i == nq - 1))
        def _():
            gather.wait()

    res = pl.pallas_call(
        body, name="attn_fwd", grid=(n_pairs, nq),
        in_specs=[pl.BlockSpec((2, PAIR, tq), lambda p, i: (p, 0, i)),
                  pl.BlockSpec((2, t, PAIR), lambda p, i: (p, 0, 0), pipeline_mode=pl.Buffered(1)),
                  pl.BlockSpec((2, PAIR, t), lambda p, i: (p, 0, 0), pipeline_mode=pl.Buffered(1))] + [HBM_SPEC] * n_sh,
        out_specs=[pl.BlockSpec((tq, PAIR), lambda p, i: (i, p)),
                   pl.BlockSpec((2, tq, PAIR), lambda p, i: (p, i, 0)),
                   pl.BlockSpec((2, PAIR, tq), lambda p, i: (p, 0, i))] + [HBM_SPEC] * n_sh,
        out_shape=[jax.ShapeDtypeStruct((t, ATTN_W), BF16), jax.ShapeDtypeStruct((N_HEADS, t, PAIR), BF16),
                   jax.ShapeDtypeStruct((N_HEADS, PAIR, t), BF16)]
        + [jax.ShapeDtypeStruct((N_DEV,) + s.shape, s.dtype) for s in shards],
        scratch_shapes=[pltpu.VMEM((2, 2, tk, tq), F32)] + _Exchange.scratch(n_sh),
        compiler_params=_cp(("arbitrary", "arbitrary")),
    )(q_aug_t, k_aug, v_aug_t, *shards)
    return res[0], res[1], res[2], res[3:]


def _prev_halo(tm, halo):
    return lambda i: (jnp.maximum(i * (tm // halo) - 1, 0), 0)


def _next_halo(tm, halo, t):
    return lambda i: (jnp.minimum((i + 1) * (tm // halo), t // halo - 1), 0)


def _mlp_tile(hh, g_ref, wu_ref, wd_ref, n_ref, a_ref, z_ref):
    n_blk, _, fb = wu_ref.shape
    n = (hh * _rstd(hh) * g_ref[...]).astype(BF16)
    n_ref[...] = n
    acc = hh
    for k in range(n_blk):
        a = _dot(n, wu_ref[k])
        zz = jnp.square(jnp.maximum(a, 0.0)).astype(BF16)
        a_ref[:, k * fb:(k + 1) * fb] = a.astype(BF16)
        z_ref[:, k * fb:(k + 1) * fb] = zz
        acc = acc + _dot(zz, wd_ref[k * fb:(k + 1) * fb, :])
    return acc


def _outproj(att, cv, x, wout, *, tm=512):
    t, d = x.shape
    tm = min(tm, t)

    def body(a_ref, c_ref, x_ref, w_ref, h_ref):
        h_ref[...] = x_ref[...] + _dot(a_ref[...], w_ref[0:ATTN_W, :]) + _dot(c_ref[...], w_ref[ATTN_W:, :])

    return pl.pallas_call(
        body, name="outproj", grid=(t // tm,),
        in_specs=[pl.BlockSpec((tm, ATTN_W), lambda i: (i, 0)), pl.BlockSpec((tm, CONV_CH), lambda i: (i, 0)),
                  pl.BlockSpec((tm, d), lambda i: (i, 0)), _const_spec(wout.shape)],
        out_specs=pl.BlockSpec((tm, d), lambda i: (i, 0)),
        out_shape=jax.ShapeDtypeStruct((t, d), F32),
        compiler_params=_cp(("parallel",)),
    )(att, cv, x, wout)


def _mlp_fwd(h, g, wup, wdown, *, name, tm=512):
    t, d = h.shape
    n_blk, _, fb = wup.shape
    f = n_blk * fb
    tm = min(tm, t)

    def body(h_ref, g_ref, wu_ref, wd_ref, ho_ref, n_ref, a_ref, z_ref):
        ho_ref[...] = _mlp_tile(h_ref[...], g_ref, wu_ref, wd_ref, n_ref, a_ref, z_ref)

    row = lambda n_: pl.BlockSpec((tm, n_), lambda i: (i, 0))
    return pl.pallas_call(
        body, name=name, grid=(t // tm,),
        in_specs=[row(d), _const_spec((1, d)), _const_spec(wup.shape), _const_spec(wdown.shape)],
        out_specs=[row(d), row(d), row(f), row(f)],
        out_shape=[jax.ShapeDtypeStruct((t, d), F32), jax.ShapeDtypeStruct((t, d), BF16),
                   jax.ShapeDtypeStruct((t, f), BF16), jax.ShapeDtypeStruct((t, f), BF16)],
        compiler_params=_cp(("parallel",)),
    )(h, g, wup, wdown)


def _mlp_fwd_loss(h, g, wup, wdown, g_out, target, *, name, tm=512):
    t, d = h.shape
    n_blk, _, fb = wup.shape
    f = n_blk * fb
    tm = min(tm, t)
    nsteps = t // tm

    def body(h_ref, g_ref, wu_ref, wd_ref, go_ref, y_ref, loss_ref, dh_ref, dhb_ref, dg_ref, n_ref, a_ref, z_ref, lacc):
        i = pl.program_id(0)

        @pl.when(i == 0)
        def _():
            lacc[...] = jnp.zeros_like(lacc)
            dg_ref[...] = jnp.zeros_like(dg_ref)
        hv = _mlp_tile(h_ref[...], g_ref, wu_ref, wd_ref, n_ref, a_ref, z_ref)
        gv = go_ref[...]
        r = _rstd(hv)
        xhat = hv * r
        err = xhat * gv - y_ref[...]
        lacc[...] += _rows8(err * err)
        dout = err * (1.0 / d)
        dy = dout * gv
        dg_ref[...] += _rows8(dout * xhat)
        dh = r * (dy - xhat * jnp.mean(dy * xhat, axis=-1, keepdims=True))
        dh_ref[...] = dh
        dhb_ref[...] = dh.astype(BF16)

        @pl.when(i == nsteps - 1)
        def _():
            loss_ref[...] = jnp.full(loss_ref.shape, (0.5 / d) * jnp.sum(lacc[...]), F32)

    row = lambda n_: pl.BlockSpec((tm, n_), lambda i: (i, 0))
    return pl.pallas_call(
        body, name=name, grid=(nsteps,),
        in_specs=[row(d), _const_spec((1, d)), _const_spec(wup.shape), _const_spec(wdown.shape), _const_spec((1, d)),
                  row(d)],
        out_specs=[pl.BlockSpec((8, 128), lambda i: (0, 0)), row(d), row(d), pl.BlockSpec((8, d), lambda i: (0, 0)),
                   row(d), row(f), row(f)],
        out_shape=[jax.ShapeDtypeStruct((8, 128), F32), jax.ShapeDtypeStruct((t, d), F32),
                   jax.ShapeDtypeStruct((t, d), BF16),
                   jax.ShapeDtypeStruct((8, d), F32), jax.ShapeDtypeStruct((t, d), BF16),
                   jax.ShapeDtypeStruct((t, f), BF16), jax.ShapeDtypeStruct((t, f), BF16)],
        scratch_shapes=[pltpu.VMEM((8, d), F32)],
        compiler_params=_cp(("arbitrary",)),
    )(h, g, wup, wdown, g_out, target)


def _pool_inv_count(i, tm):
    tglob = (i * tm + lax.broadcasted_iota(jnp.int32, (tm, 1), 0) + 1).astype(F32)
    return [1.0 / jnp.minimum(tglob, float(w)) for w in POOL_WINDOWS]


def _pool_fwd(h, g, poolw, scale, *, tm=512):
    t, d = h.shape
    tm = min(tm, t)
    cg = d // len(POOL_WINDOWS)

    def body(h_ref, hh_ref, g_ref, w_ref, s_ref, ho_ref, p_ref, ext):
        i = pl.program_id(0)
        hv = h_ref[...]
        halo = hh_ref[...]
        n = hv * _rstd(hv) * g_ref[...]
        ext[0:POOL_HALO, :] = jnp.where(i == 0, 0.0, halo * _rstd(halo) * g_ref[...])
        ext[POOL_HALO:POOL_HALO + tm, :] = n
        inv = _pool_inv_count(i, tm)
        for gi, w in enumerate(POOL_WINDOWS):
            cs = slice(gi * cg, (gi + 1) * cg)
            s = ext[POOL_HALO:POOL_HALO + tm, cs]
            for j in range(1, w):
                s = s + ext[POOL_HALO - j:POOL_HALO - j + tm, cs]
            pooled = (s * inv[gi] - n[:, cs]).astype(BF16)
            p_ref[:, cs] = pooled
            ho_ref[:, cs] = hv[:, cs] + _dot(pooled, w_ref[gi]) * s_ref[:, cs]

    row = lambda: pl.BlockSpec((tm, d), lambda i: (i, 0))
    return pl.pallas_call(
        body, name="pool_fwd", grid=(t // tm,),
        in_specs=[row(), pl.BlockSpec((POOL_HALO, d), _prev_halo(tm, POOL_HALO)), _const_spec((1, d)),
                  _const_spec(poolw.shape), _const_spec((1, d))],
        out_specs=[row(), row()],
        out_shape=[jax.ShapeDtypeStruct((t, d), F32), jax.ShapeDtypeStruct((t, d), BF16)],
        scratch_shapes=[pltpu.VMEM((POOL_HALO + tm, d), F32)],
        compiler_params=_cp(("parallel",)),
    )(h, h, g, poolw, scale)


def _mm_tn(a, b, *, name, ta, tb, tt, blocked_out=False, out_dtype=F32):
    t, ka = a.shape
    n = b.shape[1]
    ta, tb, tt = min(ta, ka), min(tb, n), min(tt, t)
    nt = t // tt

    def body(a_ref, b_ref, o_ref, acc):
        @pl.when(pl.program_id(2) == 0)
        def _():
            acc[...] = jnp.zeros_like(acc)
        acc[...] += _dot_tn(a_ref[...].astype(BF16), b_ref[...].astype(BF16))

        @pl.when(pl.program_id(2) == nt - 1)
        def _():
            o_ref[...] = acc[...].astype(out_dtype)

    if blocked_out:
        assert ta == ka
        out_shape = jax.ShapeDtypeStruct((n // tb, ka, tb), out_dtype)
        out_spec = pl.BlockSpec((None, ta, tb), lambda i, j, k: (j, i, 0))
    else:
        out_shape = jax.ShapeDtypeStruct((ka, n), out_dtype)
        out_spec = pl.BlockSpec((ta, tb), lambda i, j, k: (i, j))
    return pl.pallas_call(
        body, name=name, grid=(ka // ta, n // tb, nt),
        in_specs=[pl.BlockSpec((tt, ta), lambda i, j, k: (k, i)), pl.BlockSpec((tt, tb), lambda i, j, k: (k, j))],
        out_specs=out_spec, out_shape=out_shape, scratch_shapes=[pltpu.VMEM((ta, tb), F32)],
        compiler_params=_cp(("parallel", "parallel", "arbitrary")),
    )(a, b)


def _mm_tn_cat(a_list, b_list, *, name, tt, out_dtype=BF16):
    t = a_list[0].shape[0]
    ta, tb = a_list[0].shape[1], b_list[0].shape[1]
    na, nb = len(a_list), len(b_list)
    tt = min(tt, t)
    nt = t // tt

    def body(*refs):
        a_refs, b_refs, o_ref, acc = refs[:na], refs[na:na + nb], refs[na + nb], refs[na + nb + 1]
        i, j, k = pl.program_id(0), pl.program_id(1), pl.program_id(2)

        @pl.when(k == 0)
        def _():
            acc[...] = jnp.zeros_like(acc)
        for ia in range(na):
            for ib in range(nb):
                @pl.when((i == ia) & (j == ib))
                def _(ia=ia, ib=ib):
                    acc[...] += _dot_tn(a_refs[ia][...].astype(BF16), b_refs[ib][...].astype(BF16))

        @pl.when(k == nt - 1)
        def _():
            o_ref[...] = acc[...].astype(out_dtype)

    def held(m, axis):
        def index(i, j, k):
            cur = (i, j)[axis]
            return (jnp.where(cur == m, k, jnp.where(cur < m, 0, nt - 1)), 0)
        return index

    return pl.pallas_call(
        body, name=name, grid=(na, nb, nt),
        in_specs=[pl.BlockSpec((tt, ta), held(m, 0)) for m in range(na)]
        + [pl.BlockSpec((tt, tb), held(m, 1)) for m in range(nb)],
        out_specs=pl.BlockSpec((ta, tb), lambda i, j, k: (i, j)),
        out_shape=jax.ShapeDtypeStruct((na * ta, nb * tb), out_dtype), scratch_shapes=[pltpu.VMEM((ta, tb), F32)],
        compiler_params=_cp(("arbitrary", "arbitrary", "arbitrary")),
    )(*a_list, *b_list)


def _mlp_bwd(dho, h, a, g, wup, wdown, *, name, tm=512):
    t, d = h.shape
    n_blk, _, fb = wup.shape
    f = n_blk * fb
    tm = min(tm, t)

    def body(do_ref, h_ref, a_ref, g_ref, wu_ref, wd_ref, dh_ref, dhb_ref, da_ref, dg_ref):
        @pl.when(pl.program_id(0) == 0)
        def _():
            dg_ref[...] = jnp.zeros_like(dg_ref)
        dho_v = do_ref[...]
        dob = dho_v.astype(BF16)
        dn = jnp.zeros((tm, d), F32)
        for k in range(n_blk):
            dz = _dot_nt(dob, wd_ref[k * fb:(k + 1) * fb, :])
            da = (dz * (2.0 * jnp.maximum(a_ref[:, k * fb:(k + 1) * fb].astype(F32), 0.0))).astype(BF16)
            da_ref[:, k * fb:(k + 1) * fb] = da
            dn = dn + _dot_nt(da, wu_ref[k])
        dh, dg = _norm_bwd(dn, h_ref[...], g_ref[...])
        dh = dho_v + dh
        dh_ref[...] = dh
        dhb_ref[...] = dh.astype(BF16)
        dg_ref[...] += dg

    row = lambda n_: pl.BlockSpec((tm, n_), lambda i: (i, 0))
    return pl.pallas_call(
        body, name=name, grid=(t // tm,),
        in_specs=[row(d), row(d), row(f), _const_spec((1, d)), _const_spec(wup.shape), _const_spec(wdown.shape)],
        out_specs=[row(d), row(d), row(f), pl.BlockSpec((8, d), lambda i: (0, 0))],
        out_shape=[jax.ShapeDtypeStruct((t, d), F32), jax.ShapeDtypeStruct((t, d), BF16),
                   jax.ShapeDtypeStruct((t, f), BF16), jax.ShapeDtypeStruct((8, d), F32)],
        compiler_params=_cp(("arbitrary",)),
    )(dho, h, a, g, wup, wdown)


def _pool_bwd(dho, h, pooled, g, poolw, scale, *, tm=512):
    t, d = h.shape
    tm = min(tm, t)
    ng = len(POOL_WINDOWS)
    cg = d // ng
    nsteps = t // tm

    def body(do_ref, dn_ref, h_ref, p_ref, g_ref, w_ref, s_ref, dh_ref, dhb_ref, dw_ref, ds_ref, dg_ref, ext):
        i = pl.program_id(0)

        @pl.when(i == 0)
        def _():
            dw_ref[...] = jnp.zeros_like(dw_ref)
            ds_ref[...] = jnp.zeros_like(ds_ref)
            dg_ref[...] = jnp.zeros_like(dg_ref)
        dho_v = do_ref[...]
        sv = s_ref[...]
        dyp = (dho_v * sv).astype(BF16)
        dyp_halo = (dn_ref[...] * sv).astype(BF16)
        inv = _pool_inv_count(i, tm)
        tnext = ((i + 1) * tm + lax.broadcasted_iota(jnp.int32, (POOL_HALO, 1), 0) + 1).astype(F32)
        last = i == nsteps - 1
        ypre_parts, dpooled_parts = [], []
        for gi, w in enumerate(POOL_WINDOWS):
            cs = slice(gi * cg, (gi + 1) * cg)
            pg = p_ref[:, cs]
            ypre_parts.append(_dot(pg, w_ref[gi]))
            dw_ref[gi] += _dot_tn(pg, dyp[:, cs])
            dpool = _dot_nt(dyp[:, cs], w_ref[gi])
            dpooled_parts.append(dpool)
            ext[0:tm, cs] = dpool * inv[gi]
            dpool_halo = _dot_nt(dyp_halo[:, cs], w_ref[gi]) * (1.0 / jnp.minimum(tnext, float(w)))
            ext[tm:tm + POOL_HALO, cs] = jnp.where(last, 0.0, dpool_halo)
        ds_ref[...] += _rows8(dho_v * jnp.concatenate(ypre_parts, axis=1))
        dn_parts = []
        for gi, w in enumerate(POOL_WINDOWS):
            cs = slice(gi * cg, (gi + 1) * cg)
            s = ext[0:tm, cs]
            for j in range(1, w):
                s = s + ext[j:j + tm, cs]
            dn_parts.append(s - dpooled_parts[gi])
        dh, dg = _norm_bwd(jnp.concatenate(dn_parts, axis=1), h_ref[...], g_ref[...])
        dh = dho_v + dh
        dh_ref[...] = dh
        dhb_ref[...] = dh.astype(BF16)
        dg_ref[...] += dg

    row = lambda: pl.BlockSpec((tm, d), lambda i: (i, 0))
    acc8 = lambda: pl.BlockSpec((8, d), lambda i: (0, 0))
    return pl.pallas_call(
        body, name="pool_bwd", grid=(nsteps,),
        in_specs=[row(), pl.BlockSpec((POOL_HALO, d), _next_halo(tm, POOL_HALO, t)), row(), row(),
                  _const_spec((1, d)), _const_spec(poolw.shape), _const_spec((1, d))],
        out_specs=[row(), row(), pl.BlockSpec((ng, cg, cg), lambda i: (0, 0, 0)), acc8(), acc8()],
        out_shape=[jax.ShapeDtypeStruct((t, d), F32), jax.ShapeDtypeStruct((t, d), BF16),
                   jax.ShapeDtypeStruct((ng, cg, cg), F32),
                   jax.ShapeDtypeStruct((8, d), F32), jax.ShapeDtypeStruct((8, d), F32)],
        scratch_shapes=[pltpu.VMEM((tm + POOL_HALO, d), F32)],
        compiler_params=_cp(("arbitrary",)),
    )(dho, dho, h, pooled, g, poolw, scale)


def _outproj_conv_bwd(dh, o, wout, bcx, conv_w, *, tm=512):
    t, d = dh.shape
    tm = min(tm, t)
    ch = CONV_CH
    nsteps = t // tm

    def body(dh_ref, o_ref, w_ref, b_ref, c_ref, x_ref, hc_ref, hx_ref, cw_ref,
             da_ref, dat_ref, db_ref, dw_ref, ext_u, ext_d):
        s = pl.program_id(0)

        @pl.when(s == 0)
        def _():
            dw_ref[...] = jnp.zeros_like(dw_ref)
            ext_d[0:CONV_HALO, :] = jnp.zeros((CONV_HALO, ch), F32)
        dhb = dh_ref[...].astype(BF16)
        for p in range(ATTN_W // PAIR):
            datt = _dot_nt(dhb, w_ref[p * PAIR:(p + 1) * PAIR, :])
            prod = datt * o_ref[:, p * PAIR:(p + 1) * PAIR].astype(F32)
            for hh in range(2):
                lane, head, aux = _head_lanes(hh)
                delta = jnp.sum(jnp.where(head, prod, 0.0), axis=1, keepdims=True)
                aug = _put_pieces(lane, aux + AUX_BIAS, -delta, jnp.where(head, datt, 0.0))
                da_ref[2 * p + hh] = aug.astype(BF16)
                dat_ref[2 * p + hh] = aug.astype(BF16).T
        dcv = _dot_nt(dhb, w_ref[ATTN_W:, :])
        b, c, x = b_ref[...].astype(F32), c_ref[...].astype(F32), x_ref[...].astype(F32)
        ext_u[0:CONV_HALO, :] = jnp.where(s == nsteps - 1, 0.0, hc_ref[...].astype(F32) * hx_ref[...].astype(F32))
        ext_u[CONV_HALO:CONV_HALO + tm, :] = c * x
        dconv = dcv * b
        ext_d[tm:tm + CONV_HALO, :] = ext_d[0:CONV_HALO, :]
        ext_d[0:tm, :] = dconv
        u = [ext_u[CONV_HALO - 2 + k:CONV_HALO - 2 + k + tm, :] for k in range(3)]
        conv = cw_ref[0:1, :] * u[0] + cw_ref[1:2, :] * u[1] + cw_ref[2:3, :] * u[2]
        du = (cw_ref[2:3, :] * dconv + cw_ref[1:2, :] * ext_d[1:1 + tm, :] + cw_ref[0:1, :] * ext_d[2:2 + tm, :])
        db_ref[:, 0:ch] = (dcv * conv).astype(BF16)
        db_ref[:, ch:2 * ch] = (du * x).astype(BF16)
        db_ref[:, 2 * ch:3 * ch] = (du * c).astype(BF16)
        for k in range(3):
            dw_ref[k] += _rows8(dconv * u[k])

    rev = lambda s: nsteps - 1 - s
    row = lambda n_: pl.BlockSpec((tm, n_), lambda s: (rev(s), 0))
    col = lambda k: pl.BlockSpec((tm, ch), lambda s: (rev(s), k))
    prev = lambda k: pl.BlockSpec((CONV_HALO, ch), lambda s: (_prev_halo(tm, CONV_HALO)(rev(s))[0], k))
    return pl.pallas_call(
        body, name="outproj_conv_bwd", grid=(nsteps,),
        in_specs=[row(d), row(ATTN_W), _const_spec(wout.shape), col(0), col(1), col(2), prev(1), prev(2),
                  _const_spec((8, ch))],
        out_specs=[pl.BlockSpec((N_HEADS, tm, PAIR), lambda s: (0, rev(s), 0)),
                   pl.BlockSpec((N_HEADS, PAIR, tm), lambda s: (0, 0, rev(s))),
                   row(3 * ch), pl.BlockSpec((3, 8, ch), lambda s: (0, 0, 0))],
        out_shape=[jax.ShapeDtypeStruct((N_HEADS, t, PAIR), BF16), jax.ShapeDtypeStruct((N_HEADS, PAIR, t), BF16),
                   jax.ShapeDtypeStruct((t, 3 * ch), BF16), jax.ShapeDtypeStruct((3, 8, ch), F32)],
        scratch_shapes=[pltpu.VMEM((CONV_HALO + tm, ch), F32), pltpu.VMEM((tm + CONV_HALO, ch), F32)],
        compiler_params=_cp(("arbitrary",)),
    )(dh, o, wout, bcx, bcx, bcx, bcx, bcx, conv_w)


def _attn_bwd(q_bwd, do_aug, q_bwd_t, do_aug_t, k_aug, v_aug_t, gblocks, *, tq=1024):
    t = q_bwd.shape[1]
    tq = min(tq, t)
    tk = tq // 2
    nq, nk = t // tq, t // tk
    n_pairs = ATTN_W // PAIR
    n_g = len(gblocks)

    def body(q_ref, do_ref, qt_ref, dot_ref, k_ref, vt_ref, *rest):
        dq_ref, dqx_ref, dk_ref, dkx_ref, dv_ref = rest[n_g:n_g + 5]
        dq_scr = rest[2 * n_g + 5]
        scatter = _Exchange(rest[:n_g], rest[n_g + 5:2 * n_g + 5], *rest[2 * n_g + 6:], gather=False)
        j = pl.program_id(1)

        @pl.when((pl.program_id(0) == 0) & (j == 0))
        def _():
            scatter.start()

        @pl.when(j == 0)
        def _():
            dq_scr[...] = jnp.zeros_like(dq_scr)
        k = [k_ref[0], k_ref[1]]
        vt = [vt_ref[0], vt_ref[1]]

        def step(i, carry, diag, rows=tq, row0=0):
            qs = pl.multiple_of(i * tq + row0, tk)
            if diag:
                row = lax.broadcasted_iota(jnp.int32, (rows, tk), 0)
                col = lax.broadcasted_iota(jnp.int32, (rows, tk), 1)
            out = []
            for hh in range(2):
                dk_a, dv_a = carry[hh]
                q = q_ref[hh, pl.ds(qs, rows), :]
                dov = do_ref[hh, pl.ds(qs, rows), :]
                p = jnp.exp2(_dot_nt(q, k[hh]))
                if diag:
                    p = jnp.where(col + (j * tk - i * tq - row0) <= row, p, 0.0)
                ds = (p * _dot(dov, vt[hh])).astype(BF16)
                dv_a = dv_a + _dot(dot_ref[hh, :, pl.ds(qs, rows)], p.astype(BF16))
                dk_a = dk_a + _dot(qt_ref[hh, :, pl.ds(qs, rows)], ds)
                dq_scr[hh, pl.ds(qs, rows), :] += _dot(ds, k[hh])
                out.append((dk_a, dv_a))
            return tuple(out)

        zero = (jnp.zeros((PAIR, tk), F32), jnp.zeros((PAIR, tk), F32))
        carry = lax.cond(j % 2 == 0, lambda c: step(j // 2, c, True),
                         lambda c: step(j // 2, c, True, rows=tk, row0=tk), (zero, zero))
        full0 = j // 2 + 1
        odd = (nq - full0) % 2
        carry = lax.cond(odd == 1, lambda c: step(full0, c, False), lambda c: c, carry)
        (dk0, dv0), (dk1, dv1) = lax.fori_loop(
            0, (nq - full0) // 2, lambda ii, c: step(full0 + odd + 2 * ii, c, False, rows=2 * tq), carry)
        first_t = lax.broadcasted_iota(jnp.int32, (PAIR, 1), 0) < HEAD_DIM
        first = lax.broadcasted_iota(jnp.int32, (1, PAIR), 1) < HEAD_DIM
        dk_ref[...] = (jnp.where(first_t, dk0, dk1) * (1.0 / LOG2E)).astype(BF16).T
        dkx_ref[...] = jnp.where(first_t, dk1, dk0).T
        dv_ref[...] = jnp.where(first_t, dv0, dv1).astype(BF16).T

        @pl.when(j == nk - 1)
        def _():
            dq_ref[...] = (jnp.where(first, dq_scr[0], dq_scr[1]) * Q_SCALE).astype(BF16)
            dqx_ref[...] = jnp.where(first, dq_scr[1], dq_scr[0])

        @pl.when((pl.program_id(0) == n_pairs - 1) & (j == nk - 1))
        def _():
            scatter.wait()

    resident = lambda: pl.BlockSpec((2, t, PAIR), lambda p, j: (p, 0, 0), pipeline_mode=pl.Buffered(1))
    resident_t = lambda: pl.BlockSpec((2, PAIR, t), lambda p, j: (p, 0, 0), pipeline_mode=pl.Buffered(1))
    kv_in = lambda: pl.BlockSpec((2, tk, PAIR), lambda p, j: (p, j, 0))
    whole = lambda: pl.BlockSpec((t, PAIR), lambda p, j: (0, p))
    tile = lambda: pl.BlockSpec((tk, PAIR), lambda p, j: (j, p))
    b16 = jax.ShapeDtypeStruct((t, ATTN_W), BF16)
    f32 = jax.ShapeDtypeStruct((t, ATTN_W), F32)
    res = pl.pallas_call(
        body, name="attn_bwd", grid=(n_pairs, nk),
        in_specs=[resident(), resident(), resident_t(), resident_t(), kv_in(),
                  pl.BlockSpec((2, PAIR, tk), lambda p, j: (p, 0, j))] + [HBM_SPEC] * n_g,
        out_specs=[whole(), whole(), tile(), tile(), tile()] + [HBM_SPEC] * n_g,
        out_shape=[b16, f32, b16, f32, b16] + [jax.ShapeDtypeStruct(g.shape, g.dtype) for g in gblocks],
        scratch_shapes=[pltpu.VMEM((2, t, PAIR), F32)] + _Exchange.scratch(n_g),
        compiler_params=_cp(("arbitrary", "arbitrary")),
    )(q_bwd, do_aug, q_bwd_t, do_aug_t, k_aug, v_aug_t, *gblocks)
    return res[:5], res[5:]


def _fgate_bwd(dqx, dkx, sgate, *, tm=256):
    t = sgate.shape[0]
    tm = min(tm, t)
    nsteps = t // tm

    def body(dq_ref, dk_ref, sg_ref, df_ref, dbf_ref, carry):
        @pl.when(pl.program_id(0) == 0)
        def _():
            carry[...] = jnp.zeros_like(carry)
            dbf_ref[...] = jnp.zeros_like(dbf_ref)
        lane = lax.broadcasted_iota(jnp.int32, (ATTN_W, F_PAD), 0)
        head = lax.broadcasted_iota(jnp.int32, (ATTN_W, F_PAD), 1)
        aux = (head // 2) * PAIR + HEAD_DIM * (1 - head % 2)
        valid = head < N_HEADS
        pick_r = (valid & (lane == aux + AUX_ROWSUM)).astype(F32)
        pick_c = (valid & (lane == aux + AUX_BIAS)).astype(F32)
        hp = lax.Precision.HIGHEST
        dcum = (jnp.dot(dq_ref[...], pick_r, preferred_element_type=F32, precision=lax.Precision.HIGH)
                + jnp.dot(dk_ref[...], pick_c, preferred_element_type=F32, precision=lax.Precision.HIGH))
        r = lax.broadcasted_iota(jnp.int32, (tm, tm), 0)
        c = lax.broadcasted_iota(jnp.int32, (tm, tm), 1)
        tri = (c >= r).astype(F32)
        rc = jnp.dot(tri, dcum, preferred_element_type=F32, precision=hp) + carry[...]
        carry[...] = rc[0:1, :]
        df = rc * sg_ref[...]
        df_ref[...] = df.astype(BF16)
        dbf_ref[...] += _rows8(df)

    rev = lambda i: nsteps - 1 - i
    return pl.pallas_call(
        body, name="fgate_bwd", grid=(nsteps,),
        in_specs=[pl.BlockSpec((tm, ATTN_W), lambda i: (rev(i), 0)), pl.BlockSpec((tm, ATTN_W), lambda i: (rev(i), 0)),
                  pl.BlockSpec((tm, F_PAD), lambda i: (rev(i), 0))],
        out_specs=[pl.BlockSpec((tm, F_PAD), lambda i: (rev(i), 0)), pl.BlockSpec((8, F_PAD), lambda i: (0, 0))],
        out_shape=[jax.ShapeDtypeStruct((t, F_PAD), BF16), jax.ShapeDtypeStruct((8, F_PAD), F32)],
        scratch_shapes=[pltpu.VMEM((1, F_PAD), F32)],
        compiler_params=_cp(("arbitrary",)),
    )(dqx, dkx, sgate)


def _inproj_bwd(dq, dk, dv, df, dbcx, dh, x, g, win_pt, gblock, *, tm=512):
    t, d = x.shape
    tm = min(tm, t)
    nsteps = t // tm
    n_qkv = 3 * ATTN_W

    def body(dq_ref, dk_ref, dv_ref, df_ref, db_ref, dh_ref, x_ref, g_ref, w_ref, gb_ref, gx_ref, dg_ref, land_ref,
             *sems):
        scatter = _Exchange([gb_ref], [land_ref], *sems, gather=False)

        @pl.when(pl.program_id(0) == 0)
        def _():
            scatter.start()
            dg_ref[...] = jnp.zeros_like(dg_ref)
        dn = _dot(df_ref[...], w_ref[n_qkv:n_qkv + F_PAD, :])
        for k, r in enumerate((dq_ref, dk_ref, dv_ref)):
            dn = dn + _dot(r[...], w_ref[k * ATTN_W:(k + 1) * ATTN_W, :])
        for k in range(3):
            c0 = n_qkv + F_PAD + k * CONV_CH
            dn = dn + _dot(db_ref[:, k * CONV_CH:(k + 1) * CONV_CH], w_ref[c0:c0 + CONV_CH, :])
        dx, dg = _norm_bwd(dn, x_ref[...], g_ref[...])
        gx_ref[...] = dh_ref[...] + dx
        dg_ref[...] += dg

        @pl.when(pl.program_id(0) == nsteps - 1)
        def _():
            scatter.wait()

    row = lambda n_: pl.BlockSpec((tm, n_), lambda i: (i, 0))
    return pl.pallas_call(
        body, name="inproj_bwd", grid=(nsteps,),
        in_specs=[row(ATTN_W), row(ATTN_W), row(ATTN_W), row(F_PAD), row(3 * CONV_CH), row(d), row(d),
                  _const_spec((1, d)), _const_spec(win_pt.shape), HBM_SPEC],
        out_specs=[row(d), pl.BlockSpec((8, d), lambda i: (0, 0)), HBM_SPEC],
        out_shape=[jax.ShapeDtypeStruct((t, d), F32), jax.ShapeDtypeStruct((8, d), F32),
                   jax.ShapeDtypeStruct(gblock.shape, gblock.dtype)],
        scratch_shapes=_Exchange.scratch(1),
        compiler_params=_cp(("arbitrary",)),
    )(dq, dk, dv, df, dbcx, dh, x, g, win_pt, gblock)


LATE = ("w_out_0", "w_up_0", "w_down_0", "pool_w_1", "w_up_1", "w_down_1")


def _local_step(x, target, gains, b_f, conv_w, pool_scale, win_pt, shards):
    d = x.shape[1]
    n0, qkv, flog, bcx, cv = _norm_inproj(x, gains["mix0"], win_pt, conv_w)
    q_aug_t, k_aug, v_aug_t, sgate = _fgate_prep(flog, b_f, qkv)
    att, q_bwd, q_bwd_t, gathered = _attn_fwd(q_aug_t, k_aug, v_aug_t, [shards[n] for n in LATE])
    g = dict(zip(LATE, gathered))
    wout = g["w_out_0"].reshape(d, d)
    wup0, wup1 = g["w_up_0"], g["w_up_1"]
    wdown0, wdown1 = g["w_down_0"].reshape(-1, d), g["w_down_1"].reshape(-1, d)
    n_grp = len(POOL_WINDOWS)
    cg = d // n_grp
    poolw = g["pool_w_1"].reshape(N_DEV, n_grp, cg // N_DEV, cg).transpose(1, 0, 2, 3).reshape(n_grp, cg, cg)
    h1 = _outproj(att, cv, x, wout)
    h2, n1, a0, z0 = _mlp_fwd(h1, gains["ffn0"], wup0, wdown0, name="mlp_fwd0")
    h3, pooled = _pool_fwd(h2, gains["mix1"], poolw, pool_scale)
    loss, dh4, dh4_b, dg_final, n3, a1, z1 = _mlp_fwd_loss(h3, gains["ffn1"], wup1, wdown1, gains["final"], target,
                                                           name="mlp_fwd1")
    f = a1.shape[1]
    fb = f // N_DEV
    dh3, _, da1, dg_ffn1 = _mlp_bwd(dh4, h3, a1, gains["ffn1"], wup1, wdown1, name="mlp_bwd1")
    dwdown1 = _mm_tn(z1, dh4_b, name="dwdown1", ta=1024, tb=1024, tt=4096, out_dtype=BF16)
    dwup1 = _mm_tn(n3, da1, name="dwup1", ta=d, tb=fb, tt=4096, blocked_out=True, out_dtype=BF16)
    dh2, dh2_b, dpoolw, dscale, dg_mix1 = _pool_bwd(dh3, h2, pooled, gains["mix1"], poolw, pool_scale)
    dh1, dh1_b, da0, dg_ffn0 = _mlp_bwd(dh2, h1, a0, gains["ffn0"], wup0, wdown0, name="mlp_bwd0")
    dwdown0 = _mm_tn(z0, dh2_b, name="dwdown0", ta=1024, tb=1024, tt=4096, out_dtype=BF16)
    dwup0 = _mm_tn(n1, da0, name="dwup0", ta=d, tb=fb, tt=4096, blocked_out=True, out_dtype=BF16)
    do_aug, do_aug_t, dbcx, dconvw = _outproj_conv_bwd(dh1_b, att, wout, bcx, conv_w)
    dwout = _mm_tn_cat([att, cv], [dh1_b], name="dwout", tt=4096)
    gblocks = {
        "w_out_0": dwout.reshape(N_DEV, d // N_DEV, d), "w_up_0": dwup0, "w_up_1": dwup1,
        "w_down_0": dwdown0.reshape(N_DEV, -1, d), "w_down_1": dwdown1.reshape(N_DEV, -1, d),
        "pool_w_1": dpoolw.astype(BF16).reshape(n_grp, N_DEV, cg // N_DEV, cg).transpose(1, 0, 2, 3).reshape(
            N_DEV, n_grp * (cg // N_DEV), cg),
    }
    (dq, dqx, dk, dkx, dv), landed = _attn_bwd(q_bwd, do_aug, q_bwd_t, do_aug_t, k_aug, v_aug_t,
                                               [gblocks[n] for n in LATE])
    df, dbf = _fgate_bwd(dqx, dkx, sgate)
    dwin_t = jnp.concatenate(
        [_mm_tn_cat([dq, dk, dv], [n0], name="dwin_qkv", tt=2048),
         _mm_tn(df, n0, name="dwin_f", ta=F_PAD, tb=d, tt=2048, out_dtype=BF16)[:N_HEADS],
         _mm_tn(dbcx, n0, name="dwin_bcx", ta=512, tb=d, tt=4096, out_dtype=BF16)], axis=0)
    dwin_blocks = dwin_t.reshape(N_DEV, dwin_t.shape[0] // N_DEV, d)
    grad_x, dg_mix0, landed_win = _inproj_bwd(dq, dk, dv, df, dbcx, dh1, x, gains["mix0"], win_pt, dwin_blocks)
    small = dict(mix0=dg_mix0, ffn0=dg_ffn0, mix1=dg_mix1, pool_scale=dscale, ffn1=dg_ffn1, final=dg_final,
                 b_f=dbf, conv_w=dconvw)
    return loss, grad_x, dict(zip(LATE + ("w_in_0",), tuple(landed) + (landed_win,))), small


def _all_gather(shards):
    n = len(shards)

    def body(*refs):
        gather = _TwoLevelGather(refs[:n], refs[n:2 * n], *refs[2 * n:])
        gather.start()
        gather.forward()
        gather.wait()

    return pl.pallas_call(
        body, name="all_gather",
        in_specs=[HBM_SPEC] * n, out_specs=[HBM_SPEC] * n,
        out_shape=[jax.ShapeDtypeStruct((N_DEV,) + s.shape, s.dtype) for s in shards],
        scratch_shapes=[pltpu.SemaphoreType.DMA((7 * n,)), pltpu.SemaphoreType.DMA((7 * n,)),
                        pltpu.SemaphoreType.DMA((n,))],
    )(*shards)


SMALL_ROWS = 16


def _small_allreduce(parts):
    n, _, w = parts.shape
    assert n <= SMALL_ROWS

    def body(p_ref, o_ref, gath, send_sems, recv_sems):
        x, y, c = lax.axis_index("x"), lax.axis_index("y"), lax.axis_index("c")
        my = _slot(x, y, c)
        rows = [jnp.sum(p_ref[i], axis=0, keepdims=True) for i in range(n)]
        rows.append(jnp.zeros((SMALL_ROWS - n, w), F32))
        gath[my] = jnp.concatenate(rows, axis=0)
        copies = []
        for k in range(1, N_DEV):
            px, py, pc = x ^ (k >> 2), y ^ ((k >> 1) & 1), c ^ (k & 1)
            cp = pltpu.make_async_remote_copy(
                src_ref=gath.at[my], dst_ref=gath.at[my], send_sem=send_sems.at[k - 1], recv_sem=recv_sems.at[k - 1],
                device_id=(px, py, pc), device_id_type=MESH)
            cp.start()
            copies.append(cp)
        for cp in copies:
            cp.wait()
        acc = gath[0]
        for d in range(1, N_DEV):
            acc = acc + gath[d]
        o_ref[...] = acc

    return pl.pallas_call(
        body, name="small_allreduce",
        in_specs=[VMEM_SPEC], out_specs=VMEM_SPEC,
        out_shape=jax.ShapeDtypeStruct((SMALL_ROWS, w), F32),
        scratch_shapes=[pltpu.VMEM((N_DEV, SMALL_ROWS, w), F32), pltpu.SemaphoreType.DMA((N_DEV - 1,)),
                        pltpu.SemaphoreType.DMA((N_DEV - 1,))],
    )(parts)


def _adamw(g, w, m, v, *, name, tm=256):
    r, c = g.shape
    tm = tm if r % tm == 0 else r
    bc1 = 1.0 - ADAM_B1 ** ADAM_STEP
    bc2 = 1.0 - ADAM_B2 ** ADAM_STEP

    def body(g_ref, w_ref, m_ref, v_ref, d_ref, nm_ref, nv_ref):
        gv = g_ref[...]
        nm = ADAM_B1 * m_ref[...] + (1.0 - ADAM_B1) * gv
        nv = ADAM_B2 * v_ref[...] + (1.0 - ADAM_B2) * jnp.square(gv)
        nm_ref[...] = nm
        nv_ref[...] = nv
        d_ref[...] = -ADAM_LR * ((nm / bc1) / (jnp.sqrt(nv / bc2) + ADAM_EPS) + ADAM_WD * w_ref[...])

    blk = pl.BlockSpec((tm, c), lambda i: (i, 0))
    shp = jax.ShapeDtypeStruct((r, c), F32)
    return pl.pallas_call(
        body, name=name, grid=(r // tm,), in_specs=[blk] * 4, out_specs=[blk] * 3, out_shape=[shp] * 3,
        compiler_params=_cp(("parallel",)),
    )(g, w, m, v)


def _transpose_cast(a, *, name):
    def body(a_ref, o_ref):
        o_ref[...] = a_ref[...].T.astype(BF16)

    return pl.pallas_call(body, name=name, out_shape=jax.ShapeDtypeStruct(a.shape[::-1], BF16),
                          compiler_params=_cp())(a)


def _adamw_sum_t(parts, w, m, v, *, name):
    bc1 = 1.0 - ADAM_B1 ** ADAM_STEP
    bc2 = 1.0 - ADAM_B2 ** ADAM_STEP

    def body(p_ref, w_ref, m_ref, v_ref, g_ref, d_ref, nm_ref, nv_ref):
        acc = p_ref[0].astype(F32)
        for k in range(1, N_DEV):
            acc = acc + p_ref[k].astype(F32)
        gv = acc.T
        g_ref[...] = gv
        nm = ADAM_B1 * m_ref[...] + (1.0 - ADAM_B1) * gv
        nv = ADAM_B2 * v_ref[...] + (1.0 - ADAM_B2) * jnp.square(gv)
        nm_ref[...] = nm
        nv_ref[...] = nv
        d_ref[...] = -ADAM_LR * ((nm / bc1) / (jnp.sqrt(nv / bc2) + ADAM_EPS) + ADAM_WD * w_ref[...])

    shp = jax.ShapeDtypeStruct(w.shape, F32)
    return pl.pallas_call(body, name=name, out_shape=[shp] * 4, compiler_params=_cp())(parts, w, m, v)


def _adamw_sum(parts, w, m, v, *, name, tm=256):
    _, r, c = parts.shape
    tm = tm if r % tm == 0 else r
    bc1 = 1.0 - ADAM_B1 ** ADAM_STEP
    bc2 = 1.0 - ADAM_B2 ** ADAM_STEP

    def body(p_ref, w_ref, m_ref, v_ref, g_ref, d_ref, nm_ref, nv_ref):
        gv = p_ref[0].astype(F32)
        for k in range(1, N_DEV):
            gv = gv + p_ref[k].astype(F32)
        g_ref[...] = gv
        nm = ADAM_B1 * m_ref[...] + (1.0 - ADAM_B1) * gv
        nv = ADAM_B2 * v_ref[...] + (1.0 - ADAM_B2) * jnp.square(gv)
        nm_ref[...] = nm
        nv_ref[...] = nv
        d_ref[...] = -ADAM_LR * ((nm / bc1) / (jnp.sqrt(nv / bc2) + ADAM_EPS) + ADAM_WD * w_ref[...])

    blk = pl.BlockSpec((tm, c), lambda i: (i, 0))
    shp = jax.ShapeDtypeStruct((r, c), F32)
    return pl.pallas_call(
        body, name=name, grid=(r // tm,), in_specs=[pl.BlockSpec((N_DEV, tm, c), lambda i: (0, i, 0))] + [blk] * 3,
        out_specs=[blk] * 4, out_shape=[shp] * 4, compiler_params=_cp(("parallel",)),
    )(parts, w, m, v)


BIG = ("w_in_0", "w_out_0", "w_up_0", "w_down_0", "pool_w_1", "w_up_1", "w_down_1")
SMALL = ("norm_mix_0", "norm_ffn_0", "norm_mix_1", "pool_scale_1", "norm_ffn_1", "final_norm", "b_f_0", "conv_w_0")
WEIGHTS = ("norm_mix_0", "w_in_0", "b_f_0", "conv_w_0", "w_out_0", "norm_ffn_0", "w_up_0", "w_down_0", "norm_mix_1",
           "pool_w_1", "pool_scale_1", "norm_ffn_1", "w_up_1", "w_down_1", "final_norm")


def _pad_to(a, rows, cols):
    return jnp.pad(a, ((0, rows - a.shape[0]), (0, cols - a.shape[1])))


def _pack_small(p, width):
    rows = [p[n].reshape(1, -1) for n in SMALL[:6]]
    rows.append(_pad_to(p["b_f_0"].reshape(1, -1), 1, width))
    rows.append(_pad_to(p["conv_w_0"], 3, width))
    return _pad_to(jnp.concatenate(rows, axis=0), SMALL_ROWS, width)


def _unpack_small(a, like):
    out = {n: a[i] for i, n in enumerate(SMALL[:6])}
    out["b_f_0"] = a[6, :like["b_f_0"].shape[0]]
    out["conv_w_0"] = a[7:10, :like["conv_w_0"].shape[1]]
    return out


def kernel(x, norm_mix_0, w_in_0, b_f_0, conv_w_0, w_out_0, norm_ffn_0, w_up_0, w_down_0, norm_mix_1, pool_w_1, pool_scale_1, norm_ffn_1, w_up_1, w_down_1, final_norm, loss_target, m_norm_mix_0, m_w_in_0, m_b_f_0, m_conv_w_0, m_w_out_0, m_norm_ffn_0, m_w_up_0, m_w_down_0, m_norm_mix_1, m_pool_w_1, m_pool_scale_1, m_norm_ffn_1, m_w_up_1, m_w_down_1, m_final_norm, v_norm_mix_0, v_w_in_0, v_b_f_0, v_conv_w_0, v_w_out_0, v_norm_ffn_0, v_w_up_0, v_w_down_0, v_norm_mix_1, v_pool_w_1, v_pool_scale_1, v_norm_ffn_1, v_w_up_1, v_w_down_1, v_final_norm):
    w = dict(norm_mix_0=norm_mix_0, w_in_0=w_in_0, b_f_0=b_f_0, conv_w_0=conv_w_0, w_out_0=w_out_0,
             norm_ffn_0=norm_ffn_0, w_up_0=w_up_0, w_down_0=w_down_0, norm_mix_1=norm_mix_1, pool_w_1=pool_w_1,
             pool_scale_1=pool_scale_1, norm_ffn_1=norm_ffn_1, w_up_1=w_up_1, w_down_1=w_down_1, final_norm=final_norm)
    m = dict(norm_mix_0=m_norm_mix_0, w_in_0=m_w_in_0, b_f_0=m_b_f_0, conv_w_0=m_conv_w_0, w_out_0=m_w_out_0,
             norm_ffn_0=m_norm_ffn_0, w_up_0=m_w_up_0, w_down_0=m_w_down_0, norm_mix_1=m_norm_mix_1,
             pool_w_1=m_pool_w_1, pool_scale_1=m_pool_scale_1, norm_ffn_1=m_norm_ffn_1, w_up_1=m_w_up_1,
             w_down_1=m_w_down_1, final_norm=m_final_norm)
    v = dict(norm_mix_0=v_norm_mix_0, w_in_0=v_w_in_0, b_f_0=v_b_f_0, conv_w_0=v_conv_w_0, w_out_0=v_w_out_0,
             norm_ffn_0=v_norm_ffn_0, w_up_0=v_w_up_0, w_down_0=v_w_down_0, norm_mix_1=v_norm_mix_1,
             pool_w_1=v_pool_w_1, pool_scale_1=v_pool_scale_1, norm_ffn_1=v_norm_ffn_1, w_up_1=v_w_up_1,
             w_down_1=v_w_down_1, final_norm=v_final_norm)
    d = x.shape[-1]
    n_in = w_in_0.shape[1] * N_DEV
    n_qkv = 3 * ATTN_W
    pool_g, pool_rows, pool_c = pool_w_1.shape

    def shard2d(p):
        return {n: (p[n].reshape(pool_g * pool_rows, pool_c) if n == "pool_w_1" else p[n]) for n in BIG}
    w2, m2, v2 = shard2d(w), shard2d(m), shard2d(v)

    conv_cols = conv_w_0.shape[1]
    win_g8, conv_g8 = _all_gather([_transpose_cast(w_in_0, name="w_in_t"), _pad_to(conv_w_0, 8, 128)])
    conv_full = conv_g8[:, :, :conv_cols].transpose(1, 0, 2).reshape(8, N_DEV * conv_cols)
    win_t = win_g8.reshape(n_in, d)
    win_pt = jnp.concatenate([win_t[:n_qkv], _pad_to(win_t[n_qkv:n_qkv + N_HEADS], F_PAD, d),
                              win_t[n_qkv + N_HEADS:]], axis=0)

    gains = dict(mix0=norm_mix_0.reshape(1, d), ffn0=norm_ffn_0.reshape(1, d), mix1=norm_mix_1.reshape(1, d),
                 ffn1=norm_ffn_1.reshape(1, d), final=final_norm.reshape(1, d))
    dev = _slot(lax.axis_index("x"), lax.axis_index("y"), lax.axis_index("c"))
    loss8, grad_x, landed, small = _local_step(
        x[0], loss_target[0], gains, _pad_to(b_f_0.reshape(1, -1), 1, F_PAD), conv_full, pool_scale_1.reshape(1, d),
        win_pt, {n: w2[n].astype(BF16) for n in LATE})
    parts = jnp.concatenate(
        [small[k][None] for k in ("mix0", "ffn0", "mix1", "pool_scale", "ffn1", "final")]
        + [_pad_to(small["b_f"], 8, d)[None], jnp.pad(small["conv_w"], ((0, 0), (0, 0), (0, d - CONV_CH))),
           _pad_to(loss8[0:1, 0:1], 8, d)[None]], axis=0)
    tot = _small_allreduce(parts)
    loss = tot[10, 0]
    conv_g = lax.dynamic_slice(tot, (7, dev * conv_cols), (3, conv_cols))
    gs = tot.at[7:10].set(_pad_to(conv_g, 3, d))

    grads, deltas, new_m, new_v = {}, {}, {}, {}
    for n in BIG:
        if n in LATE:
            gr, dl, nm, nv = _adamw_sum(landed[n], w2[n], m2[n], v2[n], name="adamw_" + n)
        else:
            gr, dl, nm, nv = _adamw_sum_t(landed[n], w2[n], m2[n], v2[n], name="adamw_" + n)
        for dst, val in ((grads, gr), (deltas, dl), (new_m, nm), (new_v, nv)):
            dst[n] = val.reshape(w[n].shape)
    dl, nm, nv = _adamw(gs, _pack_small(w, d), _pack_small(m, d), _pack_small(v, d), name="adamw_small")
    for dst, val in ((grads, gs), (deltas, dl), (new_m, nm), (new_v, nv)):
        dst.update(_unpack_small(val, w))
    return (loss, grad_x[None], *[grads[n] for n in WEIGHTS], *[deltas[n] for n in WEIGHTS],
            *[new_m[n] for n in WEIGHTS], *[new_v[n] for n in WEIGHTS])
```

```python
import functools

import jax
import jax.numpy as jnp
from jax import lax
from jax.experimental import pallas as pl
from jax.experimental.pallas import tpu as pltpu

F32 = jnp.float32
BF16 = jnp.bfloat16

N_DEV = 8
N_HEADS = 8
HEAD_DIM = 64
PAIR = 2 * HEAD_DIM
ATTN_W = N_HEADS * HEAD_DIM
CONV_CH = 512
F_PAD = 128
POOL_WINDOWS = (2, 4, 8, 16)
POOL_HALO = 16
CONV_HALO = 16
RMS_EPS = 1e-6
Q_SCALE = HEAD_DIM ** -0.5
LOG2E = 1.4426950408889634
NEG = -1e30
AUX_BIAS = 0
AUX_LSE = 3
AUX_ROWSUM = 6
ADAM_LR, ADAM_B1, ADAM_B2, ADAM_EPS, ADAM_WD, ADAM_STEP = 0.001, 0.9, 0.999, 1e-08, 0.01, 10
MESH = pl.DeviceIdType.MESH
VMEM_LIMIT = 56 * 2**20


def _cp(sem=None, vmem=VMEM_LIMIT, **kw):
    return pltpu.CompilerParams(dimension_semantics=sem, vmem_limit_bytes=vmem, **kw)


def _dot(a, b):
    return jnp.dot(a, b, preferred_element_type=F32)


def _dot_nt(a, b):
    return lax.dot_general(a, b, (((1,), (1,)), ((), ())), preferred_element_type=F32)


def _dot_tn(a, b):
    return lax.dot_general(a, b, (((0,), (0,)), ((), ())), preferred_element_type=F32)


def _rstd(h):
    return lax.rsqrt(jnp.mean(h * h, axis=-1, keepdims=True) + RMS_EPS)


def _rows8(x):
    r, n = x.shape
    return jnp.sum(x.reshape(r // 8, 8, n), axis=0)


def _norm_bwd(dn, h, g):
    r = _rstd(h)
    xhat = h * r
    dy = dn * g
    dh = r * (dy - xhat * jnp.mean(dy * xhat, axis=-1, keepdims=True))
    return dh, _rows8(dn * xhat)


def _const_spec(shape):
    nd = len(shape)
    return pl.BlockSpec(shape, lambda *_: (0,) * nd, pipeline_mode=pl.Buffered(1))


HBM_SPEC = pl.BlockSpec(memory_space=pltpu.HBM)
VMEM_SPEC = pl.BlockSpec(memory_space=pltpu.VMEM)


def _slot(px, py, pc):
    return 4 * px + 2 * py + pc


class _Exchange:
    def __init__(self, srcs, dsts, send_sems, recv_sems, local_sems, gather):
        x, y, c = lax.axis_index("x"), lax.axis_index("y"), lax.axis_index("c")
        me = _slot(x, y, c)
        self.copies = []
        for a, (src, dst) in enumerate(zip(srcs, dsts)):
            self.copies.append(pltpu.make_async_copy(src if gather else src.at[me], dst.at[me], local_sems.at[a]))
            for k in range(1, N_DEV):
                px, py, pc = x ^ (k >> 2), y ^ ((k >> 1) & 1), c ^ (k & 1)
                self.copies.append(pltpu.make_async_remote_copy(
                    src_ref=src if gather else src.at[_slot(px, py, pc)], dst_ref=dst.at[me],
                    send_sem=send_sems.at[(N_DEV - 1) * a + k - 1], recv_sem=recv_sems.at[(N_DEV - 1) * a + k - 1],
                    device_id=(px, py, pc), device_id_type=MESH))

    def start(self):
        for cp in self.copies:
            cp.start()

    def wait(self):
        for cp in self.copies:
            cp.wait()

    @staticmethod
    def scratch(n):
        return [pltpu.SemaphoreType.DMA(((N_DEV - 1) * n,)), pltpu.SemaphoreType.DMA(((N_DEV - 1) * n,)),
                pltpu.SemaphoreType.DMA((n,))]


def _mesh_places():
    x, y, c = lax.axis_index("x"), lax.axis_index("y"), lax.axis_index("c")
    chips = [(1 - x, y), (x, 1 - y), (1 - x, 1 - y)]
    return (x, y, c), (x, y, 1 - c), chips


class _TwoLevelGather:
    def __init__(self, srcs, dsts, send_sems, recv_sems, local_sems):
        me, sib, chips = _mesh_places()
        c = me[2]
        n = len(srcs)

        def copy(a, k, block, to, src=None):
            dst = dsts[a].at[_slot(*block)]
            return pltpu.make_async_remote_copy(
                src_ref=dst if src is None else src, dst_ref=dst, send_sem=send_sems.at[7 * a + k],
                recv_sem=recv_sems.at[7 * a + k], device_id=to, device_id_type=MESH)

        self.mine = [pltpu.make_async_copy(srcs[a], dsts[a].at[_slot(*me)], local_sems.at[a]) for a in range(n)]
        self.first, self.landed, self.passed, self.rest = [], [], [], []
        for a in range(n):
            self.first.append(copy(a, 0, me, sib, src=srcs[a]))
            self.first += [copy(a, 1 + j, me, (*chip, c), src=srcs[a]) for j, chip in enumerate(chips)]
            self.landed += [copy(a, 1 + j, (*chip, c), me) for j, chip in enumerate(chips)]
            self.passed += [copy(a, 4 + j, (*chip, c), sib) for j, chip in enumerate(chips)]
            self.rest.append(copy(a, 0, sib, me))
            self.rest += [copy(a, 4 + j, (*chip, 1 - c), me) for j, chip in enumerate(chips)]

    def start(self):
        for cp in self.mine + self.first:
            cp.start()

    def forward(self):
        for arrived, onward in zip(self.landed, self.passed):
            arrived.wait_recv()
            onward.start()

    def wait(self):
        for cp in self.rest:
            cp.wait_recv()
        for cp in self.first + self.passed:
            cp.wait_send()
        for cp in self.mine:
            cp.wait()


def _norm_inproj(x, g, win_pt, conv_w, *, tm=512):
    t, d = x.shape
    n_all = win_pt.shape[0]
    n_qkv = 3 * ATTN_W
    n_bcx = 3 * CONV_CH
    assert n_all == n_qkv + F_PAD + n_bcx
    tm = min(tm, t)
    ch = CONV_CH

    def body(x_ref, g_ref, w_ref, cw_ref, n_ref, qkv_ref, f_ref, bcx_ref, cv_ref, ext):
        h = x_ref[...]
        n = (h * _rstd(h) * g_ref[...]).astype(BF16)
        n_ref[...] = n
        for c0 in range(0, n_qkv, 512):
            acc = _dot_nt(n, w_ref[c0:c0 + 512, :])
            if c0 < ATTN_W:
                acc = acc * (Q_SCALE * LOG2E)
            qkv_ref[:, c0:c0 + 512] = acc.astype(BF16)
        f_ref[...] = _dot_nt(n, w_ref[n_qkv:n_qkv + F_PAD, :])
        bcx = []
        for k in range(3):
            c0 = n_qkv + F_PAD + k * ch
            v = _dot_nt(n, w_ref[c0:c0 + ch, :]).astype(BF16)
            bcx_ref[:, k * ch:(k + 1) * ch] = v
            bcx.append(v.astype(F32))
        @pl.when(pl.program_id(0) == 0)
        def _():
            ext[tm:tm + CONV_HALO, :] = jnp.zeros((CONV_HALO, ch), F32)
        ext[0:CONV_HALO, :] = ext[tm:tm + CONV_HALO, :]
        ext[CONV_HALO:CONV_HALO + tm, :] = bcx[1] * bcx[2]
        conv = (cw_ref[0:1, :] * ext[CONV_HALO - 2:CONV_HALO - 2 + tm, :]
                + cw_ref[1:2, :] * ext[CONV_HALO - 1:CONV_HALO - 1 + tm, :]
                + cw_ref[2:3, :] * ext[CONV_HALO:CONV_HALO + tm, :])
        cv_ref[...] = (bcx[0] * conv).astype(BF16)

    return pl.pallas_call(
        body, name="norm_inproj", grid=(t // tm,),
        in_specs=[pl.BlockSpec((tm, d), lambda i: (i, 0)), _const_spec((1, d)), _const_spec((n_all, d)),
                  _const_spec((8, ch))],
        out_specs=[pl.BlockSpec((tm, d), lambda i: (i, 0)), pl.BlockSpec((tm, n_qkv), lambda i: (i, 0)),
                   pl.BlockSpec((tm, F_PAD), lambda i: (i, 0)), pl.BlockSpec((tm, n_bcx), lambda i: (i, 0)),
                   pl.BlockSpec((tm, ch), lambda i: (i, 0))],
        out_shape=[jax.ShapeDtypeStruct((t, d), BF16), jax.ShapeDtypeStruct((t, n_qkv), BF16),
                   jax.ShapeDtypeStruct((t, F_PAD), F32), jax.ShapeDtypeStruct((t, n_bcx), BF16),
                   jax.ShapeDtypeStruct((t, ch), BF16)],
        scratch_shapes=[pltpu.VMEM((CONV_HALO + tm, ch), F32)],
        compiler_params=_cp(("arbitrary",)),
    )(x, g, win_pt, conv_w)


def _head_lanes(h):
    lane = lax.broadcasted_iota(jnp.int32, (1, PAIR), 1)
    hh = h % 2
    return lane, lane // HEAD_DIM == hh, HEAD_DIM * (1 - hh)


def _pieces(col):
    hi = col.astype(BF16).astype(F32)
    r1 = col - hi
    mid = r1.astype(BF16).astype(F32)
    lo = (r1 - mid).astype(BF16).astype(F32)
    return hi, mid, lo


def _put_pieces(lane, first, col, other):
    hi, mid, lo = _pieces(col)
    return jnp.where(lane == first, hi, jnp.where(lane == first + 1, mid, jnp.where(lane == first + 2, lo, other)))


def _fgate_prep(flog, b_f, qkv, *, tm=512):
    t = flog.shape[0]
    tm = min(tm, t)

    def body(f_ref, b_ref, qkv_ref, qat_ref, ka_ref, vat_ref, sg_ref, carry):
        @pl.when(pl.program_id(0) == 0)
        def _():
            carry[...] = jnp.zeros_like(carry)
        z = f_ref[...] + b_ref[...]
        e = jnp.exp(-jnp.abs(z))
        logf = jnp.minimum(z, 0.0) - jnp.log(1.0 + e)
        sg_ref[...] = jnp.where(z >= 0, e, 1.0) / (1.0 + e)
        r = lax.broadcasted_iota(jnp.int32, (tm, tm), 0)
        c = lax.broadcasted_iota(jnp.int32, (tm, tm), 1)
        tri = (c <= r).astype(F32)
        cs = jnp.dot(tri, logf, preferred_element_type=F32, precision=lax.Precision.HIGHEST) + carry[...]
        carry[...] = cs[tm - 1:tm, :]
        cs2 = cs * LOG2E
        for h in range(N_HEADS):
            lane, head, aux = _head_lanes(h)
            p0 = (h // 2) * PAIR
            ones = ((lane >= aux + AUX_LSE) & (lane <= aux + AUX_ROWSUM)).astype(F32)
            bias = (lane >= aux + AUX_BIAS) & (lane < aux + AUX_BIAS + 3)
            k_aux = _put_pieces(lane, aux + AUX_BIAS, cs2[:, h:h + 1], ones)
            q_aug = jnp.where(head, qkv_ref[:, p0:p0 + PAIR], jnp.where(bias, -1.0, 0.0).astype(BF16))
            v_aug = jnp.where(head, qkv_ref[:, 2 * ATTN_W + p0:2 * ATTN_W + p0 + PAIR],
                              jnp.where(bias, 1.0, 0.0).astype(BF16))
            qat_ref[h] = q_aug.T
            ka_ref[h] = jnp.where(head, qkv_ref[:, ATTN_W + p0:ATTN_W + p0 + PAIR], k_aux.astype(BF16))
            vat_ref[h] = v_aug.T

    aug = lambda: pl.BlockSpec((N_HEADS, tm, PAIR), lambda i: (0, i, 0))
    aug_t = lambda: pl.BlockSpec((N_HEADS, PAIR, tm), lambda i: (0, 0, i))
    aug_shape = jax.ShapeDtypeStruct((N_HEADS, t, PAIR), BF16)
    aug_t_shape = jax.ShapeDtypeStruct((N_HEADS, PAIR, t), BF16)
    return pl.pallas_call(
        body, name="fgate_prep", grid=(t // tm,),
        in_specs=[pl.BlockSpec((tm, F_PAD), lambda i: (i, 0)), _const_spec((1, F_PAD)),
                  pl.BlockSpec((tm, 3 * ATTN_W), lambda i: (i, 0))],
        out_specs=[aug_t(), aug(), aug_t(), pl.BlockSpec((tm, F_PAD), lambda i: (i, 0))],
        out_shape=[aug_t_shape, aug_shape, aug_t_shape, jax.ShapeDtypeStruct((t, F_PAD), F32)],
        scratch_shapes=[pltpu.VMEM((1, F_PAD), F32)],
        compiler_params=_cp(("arbitrary",)),
    )(flog, b_f, qkv)


def _put_pieces_t(row, first, vec, other):
    hi, mid, lo = _pieces(vec)
    return jnp.where(row == first, hi, jnp.where(row == first + 1, mid, jnp.where(row == first + 2, lo, other)))


def _attn_fwd(q_aug_t, k_aug, v_aug_t, shards, *, tq=1024):
    t = k_aug.shape[1]
    tq = min(tq, t)
    tk = tq // 2
    nq = t // tq
    n_pairs = ATTN_W // PAIR
    n_sh = len(shards)
    forward_step = (11 * n_pairs * nq) // 16

    def body(qt_ref, k_ref, vt_ref, *rest):
        o_ref, qb_ref, qbt_ref = rest[n_sh:n_sh + 3]
        s_scr = rest[2 * n_sh + 3]
        gather = _TwoLevelGather(rest[:n_sh], rest[n_sh + 3:2 * n_sh + 3], *rest[2 * n_sh + 4:])
        i = pl.program_id(1)
        step = pl.program_id(0) * nq + i

        @pl.when(step == 0)
        def _():
            gather.start()

        @pl.when(step == forward_step)
        def _():
            gather.forward()
        key = lax.broadcasted_iota(jnp.int32, (tk, tq), 0)
        qry = lax.broadcasted_iota(jnp.int32, (tk, tq), 1)
        qt = [qt_ref[0], qt_ref[1]]

        def logits(hh, tile, slot, diag):
            s = _dot(k_ref[hh, pl.ds(pl.multiple_of(tile * tk, tk), tk), :], qt[hh])
            if diag:
                s = jnp.where(key + (tile * tk - i * tq) <= qry, s, NEG)
            s_scr[hh, slot] = s
            return jnp.max(s, axis=0, keepdims=True)

        def probs(hh, tile, slot, m, acc, tmax):
            mn = jnp.maximum(m, tmax)
            p = jnp.exp2(s_scr[hh, slot] - mn).astype(BF16)
            acc = jnp.exp2(m - mn) * acc + _dot(vt_ref[hh, :, pl.ds(pl.multiple_of(tile * tk, tk), tk)], p)
            return mn, acc

        def advance(carry, prev, slot, nxt, diag=False):
            out = []
            for hh in range(2):
                m, acc, tmax = carry[hh]
                m, acc = probs(hh, prev, slot, m, acc, tmax)
                out.append((m, acc, logits(hh, nxt, 1 - slot, diag)))
            return tuple(out)

        def two_tiles(jj, carry):
            carry = advance(carry, jnp.where(jj == 0, 2 * i, 2 * jj - 1), 1, 2 * jj)
            return advance(carry, 2 * jj, 0, 2 * jj + 1)

        init = tuple((jnp.full((1, tq), NEG, F32), jnp.zeros((PAIR, tq), F32), logits(hh, 2 * i + 1, 0, True))
                     for hh in range(2))
        carry = advance(init, 2 * i + 1, 0, 2 * i, diag=True)
        carry = lax.fori_loop(0, i // 2, lambda jj, c: two_tiles(2 * jj + 1, two_tiles(2 * jj, c)), carry)
        carry = lax.cond(i % 2 == 1, lambda c: two_tiles(i - 1, c), lambda c: c, carry)
        last = jnp.where(i == 0, 2 * i, 2 * i - 1)
        row = lax.broadcasted_iota(jnp.int32, (PAIR, 1), 0)
        res = []
        for hh in range(2):
            aux = HEAD_DIM * (1 - hh)
            m, acc, tmax = carry[hh]
            m, acc = probs(hh, last, 1, m, acc, tmax)
            l = acc[aux + AUX_BIAS:aux + AUX_BIAS + 1, :]
            qbt = _put_pieces_t(row, aux + AUX_LSE, -(m + jnp.log2(l)), qt[hh].astype(F32))
            qbt_ref[hh] = qbt.astype(BF16)
            qb_ref[hh] = qbt.astype(BF16).T
            res.append(acc * (1.0 / l))
        o_ref[...] = jnp.where(row < HEAD_DIM, res[0], res[1]).astype(BF16).T

        @pl.when((pl.program_id(0) == n_pairs - 1) & (i == nq - 1))
        def _():
            gather.wait()

    res = pl.pallas_call(
        body, name="attn_fwd", grid=(n_pairs, nq),
        in_specs=[pl.BlockSpec((2, PAIR, tq), lambda p, i: (p, 0, i)),
                  pl.BlockSpec((2, t, PAIR), lambda p, i: (p, 0, 0), pipeline_mode=pl.Buffered(1)),
                  pl.BlockSpec((2, PAIR, t), lambda p, i: (p, 0, 0), pipeline_mode=pl.Buffered(1))] + [HBM_SPEC] * n_sh,
        out_specs=[pl.BlockSpec((tq, PAIR), lambda p, i: (i, p)),
                   pl.BlockSpec((2, tq, PAIR), lambda p, i: (p, i, 0)),
                   pl.BlockSpec((2, PAIR, tq), lambda p, i: (p, 0, i))] + [HBM_SPEC] * n_sh,
        out_shape=[jax.ShapeDtypeStruct((t, ATTN_W), BF16), jax.ShapeDtypeStruct((N_HEADS, t, PAIR), BF16),
                   jax.ShapeDtypeStruct((N_HEADS, PAIR, t), BF16)]
        + [jax.ShapeDtypeStruct((N_DEV,) + s.shape, s.dtype) for s in shards],
        scratch_shapes=[pltpu.VMEM((2, 2, tk, tq), F32)] + _Exchange.scratch(n_sh),
        compiler_params=_cp(("arbitrary", "arbitrary")),
    )(q_aug_t, k_aug, v_aug_t, *shards)
    return res[0], res[1], res[2], res[3:]


def _prev_halo(tm, halo):
    return lambda i: (jnp.maximum(i * (tm // halo) - 1, 0), 0)


def _next_halo(tm, halo, t):
    return lambda i: (jnp.minimum((i + 1) * (tm // halo), t // halo - 1), 0)


def _mlp_tile(hh, g_ref, wu_ref, wd_ref, n_ref, a_ref, z_ref):
    n_blk, _, fb = wu_ref.shape
    n = (hh * _rstd(hh) * g_ref[...]).astype(BF16)
    n_ref[...] = n
    acc = hh
    for k in range(n_blk):
        a = _dot(n, wu_ref[k])
        zz = jnp.square(jnp.maximum(a, 0.0)).astype(BF16)
        a_ref[:, k * fb:(k + 1) * fb] = a.astype(BF16)
        z_ref[:, k * fb:(k + 1) * fb] = zz
        acc = acc + _dot(zz, wd_ref[k * fb:(k + 1) * fb, :])
    return acc


def _outproj(att, cv, x, wout, *, tm=512):
    t, d = x.shape
    tm = min(tm, t)

    def body(a_ref, c_ref, x_ref, w_ref, h_ref):
        h_ref[...] = x_ref[...] + _dot(a_ref[...], w_ref[0:ATTN_W, :]) + _dot(c_ref[...], w_ref[ATTN_W:, :])

    return pl.pallas_call(
        body, name="outproj", grid=(t // tm,),
        in_specs=[pl.BlockSpec((tm, ATTN_W), lambda i: (i, 0)), pl.BlockSpec((tm, CONV_CH), lambda i: (i, 0)),
                  pl.BlockSpec((tm, d), lambda i: (i, 0)), _const_spec(wout.shape)],
        out_specs=pl.BlockSpec((tm, d), lambda i: (i, 0)),
        out_shape=jax.ShapeDtypeStruct((t, d), F32),
        compiler_params=_cp(("parallel",)),
    )(att, cv, x, wout)


def _mlp_fwd(h, g, wup, wdown, *, name, tm=512):
    t, d = h.shape
    n_blk, _, fb = wup.shape
    f = n_blk * fb
    tm = min(tm, t)

    def body(h_ref, g_ref, wu_ref, wd_ref, ho_ref, n_ref, a_ref, z_ref):
        ho_ref[...] = _mlp_tile(h_ref[...], g_ref, wu_ref, wd_ref, n_ref, a_ref, z_ref)

    row = lambda n_: pl.BlockSpec((tm, n_), lambda i: (i, 0))
    return pl.pallas_call(
        body, name=name, grid=(t // tm,),
        in_specs=[row(d), _const_spec((1, d)), _const_spec(wup.shape), _const_spec(wdown.shape)],
        out_specs=[row(d), row(d), row(f), row(f)],
        out_shape=[jax.ShapeDtypeStruct((t, d), F32), jax.ShapeDtypeStruct((t, d), BF16),
                   jax.ShapeDtypeStruct((t, f), BF16), jax.ShapeDtypeStruct((t, f), BF16)],
        compiler_params=_cp(("parallel",)),
    )(h, g, wup, wdown)


def _mlp_fwd_loss(h, g, wup, wdown, g_out, target, *, name, tm=512):
    t, d = h.shape
    n_blk, _, fb = wup.shape
    f = n_blk * fb
    tm = min(tm, t)
    nsteps = t // tm

    def body(h_ref, g_ref, wu_ref, wd_ref, go_ref, y_ref, loss_ref, dh_ref, dhb_ref, dg_ref, n_ref, a_ref, z_ref, lacc):
        i = pl.program_id(0)

        @pl.when(i == 0)
        def _():
            lacc[...] = jnp.zeros_like(lacc)
            dg_ref[...] = jnp.zeros_like(dg_ref)
        hv = _mlp_tile(h_ref[...], g_ref, wu_ref, wd_ref, n_ref, a_ref, z_ref)
        gv = go_ref[...]
        r = _rstd(hv)
        xhat = hv * r
        err = xhat * gv - y_ref[...]
        lacc[...] += _rows8(err * err)
        dout = err * (1.0 / d)
        dy = dout * gv
        dg_ref[...] += _rows8(dout * xhat)
        dh = r * (dy - xhat * jnp.mean(dy * xhat, axis=-1, keepdims=True))
        dh_ref[...] = dh
        dhb_ref[...] = dh.astype(BF16)

        @pl.when(i == nsteps - 1)
        def _():
            loss_ref[...] = jnp.full(loss_ref.shape, (0.5 / d) * jnp.sum(lacc[...]), F32)

    row = lambda n_: pl.BlockSpec((tm, n_), lambda i: (i, 0))
    return pl.pallas_call(
        body, name=name, grid=(nsteps,),
        in_specs=[row(d), _const_spec((1, d)), _const_spec(wup.shape), _const_spec(wdown.shape), _const_spec((1, d)),
                  row(d)],
        out_specs=[pl.BlockSpec((8, 128), lambda i: (0, 0)), row(d), row(d), pl.BlockSpec((8, d), lambda i: (0, 0)),
                   row(d), row(f), row(f)],
        out_shape=[jax.ShapeDtypeStruct((8, 128), F32), jax.ShapeDtypeStruct((t, d), F32),
                   jax.ShapeDtypeStruct((t, d), BF16),
                   jax.ShapeDtypeStruct((8, d), F32), jax.ShapeDtypeStruct((t, d), BF16),
                   jax.ShapeDtypeStruct((t, f), BF16), jax.ShapeDtypeStruct((t, f), BF16)],
        scratch_shapes=[pltpu.VMEM((8, d), F32)],
        compiler_params=_cp(("arbitrary",)),
    )(h, g, wup, wdown, g_out, target)


def _pool_inv_count(i, tm):
    tglob = (i * tm + lax.broadcasted_iota(jnp.int32, (tm, 1), 0) + 1).astype(F32)
    return [1.0 / jnp.minimum(tglob, float(w)) for w in POOL_WINDOWS]


def _pool_fwd(h, g, poolw, scale, *, tm=512):
    t, d = h.shape
    tm = min(tm, t)
    cg = d // len(POOL_WINDOWS)

    def body(h_ref, hh_ref, g_ref, w_ref, s_ref, ho_ref, p_ref, ext):
        i = pl.program_id(0)
        hv = h_ref[...]
        halo = hh_ref[...]
        n = hv * _rstd(hv) * g_ref[...]
        ext[0:POOL_HALO, :] = jnp.where(i == 0, 0.0, halo * _rstd(halo) * g_ref[...])
        ext[POOL_HALO:POOL_HALO + tm, :] = n
        inv = _pool_inv_count(i, tm)
        for gi, w in enumerate(POOL_WINDOWS):
            cs = slice(gi * cg, (gi + 1) * cg)
            s = ext[POOL_HALO:POOL_HALO + tm, cs]
            for j in range(1, w):
                s = s + ext[POOL_HALO - j:POOL_HALO - j + tm, cs]
            pooled = (s * inv[gi] - n[:, cs]).astype(BF16)
            p_ref[:, cs] = pooled
            ho_ref[:, cs] = hv[:, cs] + _dot(pooled, w_ref[gi]) * s_ref[:, cs]

    row = lambda: pl.BlockSpec((tm, d), lambda i: (i, 0))
    return pl.pallas_call(
        body, name="pool_fwd", grid=(t // tm,),
        in_specs=[row(), pl.BlockSpec((POOL_HALO, d), _prev_halo(tm, POOL_HALO)), _const_spec((1, d)),
                  _const_spec(poolw.shape), _const_spec((1, d))],
        out_specs=[row(), row()],
        out_shape=[jax.ShapeDtypeStruct((t, d), F32), jax.ShapeDtypeStruct((t, d), BF16)],
        scratch_shapes=[pltpu.VMEM((POOL_HALO + tm, d), F32)],
        compiler_params=_cp(("parallel",)),
    )(h, h, g, poolw, scale)


def _mm_tn(a, b, *, name, ta, tb, tt, blocked_out=False, out_dtype=F32):
    t, ka = a.shape
    n = b.shape[1]
    ta, tb, tt = min(ta, ka), min(tb, n), min(tt, t)
    nt = t // tt

    def body(a_ref, b_ref, o_ref, acc):
        @pl.when(pl.program_id(2) == 0)
        def _():
            acc[...] = jnp.zeros_like(acc)
        acc[...] += _dot_tn(a_ref[...].astype(BF16), b_ref[...].astype(BF16))

        @pl.when(pl.program_id(2) == nt - 1)
        def _():
            o_ref[...] = acc[...].astype(out_dtype)

    if blocked_out:
        assert ta == ka
        out_shape = jax.ShapeDtypeStruct((n // tb, ka, tb), out_dtype)
        out_spec = pl.BlockSpec((None, ta, tb), lambda i, j, k: (j, i, 0))
    else:
        out_shape = jax.ShapeDtypeStruct((ka, n), out_dtype)
        out_spec = pl.BlockSpec((ta, tb), lambda i, j, k: (i, j))
    return pl.pallas_call(
        body, name=name, grid=(ka // ta, n // tb, nt),
        in_specs=[pl.BlockSpec((tt, ta), lambda i, j, k: (k, i)), pl.BlockSpec((tt, tb), lambda i, j, k: (k, j))],
        out_specs=out_spec, out_shape=out_shape, scratch_shapes=[pltpu.VMEM((ta, tb), F32)],
        compiler_params=_cp(("parallel", "parallel", "arbitrary")),
    )(a, b)


def _mm_tn_cat(a_list, b_list, *, name, tt, out_dtype=BF16):
    t = a_list[0].shape[0]
    ta, tb = a_list[0].shape[1], b_list[0].shape[1]
    na, nb = len(a_list), len(b_list)
    tt = min(tt, t)
    nt = t // tt

    def body(*refs):
        a_refs, b_refs, o_ref, acc = refs[:na], refs[na:na + nb], refs[na + nb], refs[na + nb + 1]
        i, j, k = pl.program_id(0), pl.program_id(1), pl.program_id(2)

        @pl.when(k == 0)
        def _():
            acc[...] = jnp.zeros_like(acc)
        for ia in range(na):
            for ib in range(nb):
                @pl.when((i == ia) & (j == ib))
                def _(ia=ia, ib=ib):
                    acc[...] += _dot_tn(a_refs[ia][...].astype(BF16), b_refs[ib][...].astype(BF16))

        @pl.when(k == nt - 1)
        def _():
            o_ref[...] = acc[...].astype(out_dtype)

    def held(m, axis):
        def index(i, j, k):
            cur = (i, j)[axis]
            return (jnp.where(cur == m, k, jnp.where(cur < m, 0, nt - 1)), 0)
        return index

    return pl.pallas_call(
        body, name=name, grid=(na, nb, nt),
        in_specs=[pl.BlockSpec((tt, ta), held(m, 0)) for m in range(na)]
        + [pl.BlockSpec((tt, tb), held(m, 1)) for m in range(nb)],
        out_specs=pl.BlockSpec((ta, tb), lambda i, j, k: (i, j)),
        out_shape=jax.ShapeDtypeStruct((na * ta, nb * tb), out_dtype), scratch_shapes=[pltpu.VMEM((ta, tb), F32)],
        compiler_params=_cp(("arbitrary", "arbitrary", "arbitrary")),
    )(*a_list, *b_list)


def _mlp_bwd(dho, h, a, g, wup, wdown, *, name, tm=512):
    t, d = h.shape
    n_blk, _, fb = wup.shape
    f = n_blk * fb
    tm = min(tm, t)

    def body(do_ref, h_ref, a_ref, g_ref, wu_ref, wd_ref, dh_ref, da_ref, dg_ref):
        @pl.when(pl.program_id(0) == 0)
        def _():
            dg_ref[...] = jnp.zeros_like(dg_ref)
        dho_v = do_ref[...]
        dob = dho_v.astype(BF16)
        dn = jnp.zeros((tm, d), F32)
        for k in range(n_blk):
            dz = _dot_nt(dob, wd_ref[k * fb:(k + 1) * fb, :])
            da = (dz * (2.0 * jnp.maximum(a_ref[:, k * fb:(k + 1) * fb].astype(F32), 0.0))).astype(BF16)
            da_ref[:, k * fb:(k + 1) * fb] = da
            dn = dn + _dot_nt(da, wu_ref[k])
        dh, dg = _norm_bwd(dn, h_ref[...], g_ref[...])
        dh_ref[...] = dho_v + dh
        dg_ref[...] += dg

    row = lambda n_: pl.BlockSpec((tm, n_), lambda i: (i, 0))
    return pl.pallas_call(
        body, name=name, grid=(t // tm,),
        in_specs=[row(d), row(d), row(f), _const_spec((1, d)), _const_spec(wup.shape), _const_spec(wdown.shape)],
        out_specs=[row(d), row(f), pl.BlockSpec((8, d), lambda i: (0, 0))],
        out_shape=[jax.ShapeDtypeStruct((t, d), F32), jax.ShapeDtypeStruct((t, f), BF16),
                   jax.ShapeDtypeStruct((8, d), F32)],
        compiler_params=_cp(("arbitrary",)),
    )(dho, h, a, g, wup, wdown)


def _pool_bwd(dho, h, pooled, g, poolw, scale, *, tm=512):
    t, d = h.shape
    tm = min(tm, t)
    ng = len(POOL_WINDOWS)
    cg = d // ng
    nsteps = t // tm

    def body(do_ref, dn_ref, h_ref, p_ref, g_ref, w_ref, s_ref, dh_ref, dhb_ref, dw_ref, ds_ref, dg_ref, ext):
        i = pl.program_id(0)

        @pl.when(i == 0)
        def _():
            dw_ref[...] = jnp.zeros_like(dw_ref)
            ds_ref[...] = jnp.zeros_like(ds_ref)
            dg_ref[...] = jnp.zeros_like(dg_ref)
        dho_v = do_ref[...]
        sv = s_ref[...]
        dyp = (dho_v * sv).astype(BF16)
        dyp_halo = (dn_ref[...] * sv).astype(BF16)
        inv = _pool_inv_count(i, tm)
        tnext = ((i + 1) * tm + lax.broadcasted_iota(jnp.int32, (POOL_HALO, 1), 0) + 1).astype(F32)
        last = i == nsteps - 1
        ypre_parts, dpooled_parts = [], []
        for gi, w in enumerate(POOL_WINDOWS):
            cs = slice(gi * cg, (gi + 1) * cg)
            pg = p_ref[:, cs]
            ypre_parts.append(_dot(pg, w_ref[gi]))
            dw_ref[gi] += _dot_tn(pg, dyp[:, cs])
            dpool = _dot_nt(dyp[:, cs], w_ref[gi])
            dpooled_parts.append(dpool)
            ext[0:tm, cs] = dpool * inv[gi]
            dpool_halo = _dot_nt(dyp_halo[:, cs], w_ref[gi]) * (1.0 / jnp.minimum(tnext, float(w)))
            ext[tm:tm + POOL_HALO, cs] = jnp.where(last, 0.0, dpool_halo)
        ds_ref[...] += _rows8(dho_v * jnp.concatenate(ypre_parts, axis=1))
        dn_parts = []
        for gi, w in enumerate(POOL_WINDOWS):
            cs = slice(gi * cg, (gi + 1) * cg)
            s = ext[0:tm, cs]
            for j in range(1, w):
                s = s + ext[j:j + tm, cs]
            dn_parts.append(s - dpooled_parts[gi])
        dh, dg = _norm_bwd(jnp.concatenate(dn_parts, axis=1), h_ref[...], g_ref[...])
        dh = dho_v + dh
        dh_ref[...] = dh
        dhb_ref[...] = dh.astype(BF16)
        dg_ref[...] += dg

    row = lambda: pl.BlockSpec((tm, d), lambda i: (i, 0))
    acc8 = lambda: pl.BlockSpec((8, d), lambda i: (0, 0))
    return pl.pallas_call(
        body, name="pool_bwd", grid=(nsteps,),
        in_specs=[row(), pl.BlockSpec((POOL_HALO, d), _next_halo(tm, POOL_HALO, t)), row(), row(),
                  _const_spec((1, d)), _const_spec(poolw.shape), _const_spec((1, d))],
        out_specs=[row(), row(), pl.BlockSpec((ng, cg, cg), lambda i: (0, 0, 0)), acc8(), acc8()],
        out_shape=[jax.ShapeDtypeStruct((t, d), F32), jax.ShapeDtypeStruct((t, d), BF16),
                   jax.ShapeDtypeStruct((ng, cg, cg), F32),
                   jax.ShapeDtypeStruct((8, d), F32), jax.ShapeDtypeStruct((8, d), F32)],
        scratch_shapes=[pltpu.VMEM((tm + POOL_HALO, d), F32)],
        compiler_params=_cp(("arbitrary",)),
    )(dho, dho, h, pooled, g, poolw, scale)


def _outproj_conv_bwd(dh, o, wout, bcx, conv_w, *, tm=512):
    t, d = dh.shape
    tm = min(tm, t)
    ch = CONV_CH
    nsteps = t // tm

    def body(dh_ref, o_ref, w_ref, b_ref, c_ref, x_ref, hc_ref, hx_ref, cw_ref,
             da_ref, dat_ref, db_ref, dw_ref, ext_u, ext_d):
        s = pl.program_id(0)

        @pl.when(s == 0)
        def _():
            dw_ref[...] = jnp.zeros_like(dw_ref)
            ext_d[0:CONV_HALO, :] = jnp.zeros((CONV_HALO, ch), F32)
        dhb = dh_ref[...].astype(BF16)
        for p in range(ATTN_W // PAIR):
            datt = _dot_nt(dhb, w_ref[p * PAIR:(p + 1) * PAIR, :])
            prod = datt * o_ref[:, p * PAIR:(p + 1) * PAIR].astype(F32)
            for hh in range(2):
                lane, head, aux = _head_lanes(hh)
                delta = jnp.sum(jnp.where(head, prod, 0.0), axis=1, keepdims=True)
                aug = _put_pieces(lane, aux + AUX_BIAS, -delta, jnp.where(head, datt, 0.0))
                da_ref[2 * p + hh] = aug.astype(BF16)
                dat_ref[2 * p + hh] = aug.astype(BF16).T
        dcv = _dot_nt(dhb, w_ref[ATTN_W:, :])
        b, c, x = b_ref[...].astype(F32), c_ref[...].astype(F32), x_ref[...].astype(F32)
        ext_u[0:CONV_HALO, :] = jnp.where(s == nsteps - 1, 0.0, hc_ref[...].astype(F32) * hx_ref[...].astype(F32))
        ext_u[CONV_HALO:CONV_HALO + tm, :] = c * x
        dconv = dcv * b
        ext_d[tm:tm + CONV_HALO, :] = ext_d[0:CONV_HALO, :]
        ext_d[0:tm, :] = dconv
        u = [ext_u[CONV_HALO - 2 + k:CONV_HALO - 2 + k + tm, :] for k in range(3)]
        conv = cw_ref[0:1, :] * u[0] + cw_ref[1:2, :] * u[1] + cw_ref[2:3, :] * u[2]
        du = (cw_ref[2:3, :] * dconv + cw_ref[1:2, :] * ext_d[1:1 + tm, :] + cw_ref[0:1, :] * ext_d[2:2 + tm, :])
        db_ref[:, 0:ch] = (dcv * conv).astype(BF16)
        db_ref[:, ch:2 * ch] = (du * x).astype(BF16)
        db_ref[:, 2 * ch:3 * ch] = (du * c).astype(BF16)
        for k in range(3):
            dw_ref[k] += _rows8(dconv * u[k])

    rev = lambda s: nsteps - 1 - s
    row = lambda n_: pl.BlockSpec((tm, n_), lambda s: (rev(s), 0))
    col = lambda k: pl.BlockSpec((tm, ch), lambda s: (rev(s), k))
    prev = lambda k: pl.BlockSpec((CONV_HALO, ch), lambda s: (_prev_halo(tm, CONV_HALO)(rev(s))[0], k))
    return pl.pallas_call(
        body, name="outproj_conv_bwd", grid=(nsteps,),
        in_specs=[row(d), row(ATTN_W), _const_spec(wout.shape), col(0), col(1), col(2), prev(1), prev(2),
                  _const_spec((8, ch))],
        out_specs=[pl.BlockSpec((N_HEADS, tm, PAIR), lambda s: (0, rev(s), 0)),
                   pl.BlockSpec((N_HEADS, PAIR, tm), lambda s: (0, 0, rev(s))),
                   row(3 * ch), pl.BlockSpec((3, 8, ch), lambda s: (0, 0, 0))],
        out_shape=[jax.ShapeDtypeStruct((N_HEADS, t, PAIR), BF16), jax.ShapeDtypeStruct((N_HEADS, PAIR, t), BF16),
                   jax.ShapeDtypeStruct((t, 3 * ch), BF16), jax.ShapeDtypeStruct((3, 8, ch), F32)],
        scratch_shapes=[pltpu.VMEM((CONV_HALO + tm, ch), F32), pltpu.VMEM((tm + CONV_HALO, ch), F32)],
        compiler_params=_cp(("arbitrary",)),
    )(dh, o, wout, bcx, bcx, bcx, bcx, bcx, conv_w)


def _attn_bwd(q_bwd, do_aug, q_bwd_t, do_aug_t, k_aug, v_aug_t, gblocks, *, tq=1024):
    t = q_bwd.shape[1]
    tq = min(tq, t)
    tk = tq // 2
    nq, nk = t // tq, t // tk
    n_pairs = ATTN_W // PAIR
    n_g = len(gblocks)

    def body(q_ref, do_ref, qt_ref, dot_ref, k_ref, vt_ref, *rest):
        dq_ref, dqx_ref, dk_ref, dkx_ref, dv_ref = rest[n_g:n_g + 5]
        dq_scr = rest[2 * n_g + 5]
        scatter = _Exchange(rest[:n_g], rest[n_g + 5:2 * n_g + 5], *rest[2 * n_g + 6:], gather=False)
        j = pl.program_id(1)

        @pl.when((pl.program_id(0) == 0) & (j == 0))
        def _():
            scatter.start()

        @pl.when(j == 0)
        def _():
            dq_scr[...] = jnp.zeros_like(dq_scr)
        k = [k_ref[0], k_ref[1]]
        kt = [k[0].T, k[1].T]
        vt = [vt_ref[0], vt_ref[1]]

        def step(i, carry, diag, rows=tq, row0=0):
            qs = pl.multiple_of(i * tq + row0, tk)
            if diag:
                row = lax.broadcasted_iota(jnp.int32, (rows, tk), 0)
                col = lax.broadcasted_iota(jnp.int32, (rows, tk), 1)
            out = []
            for hh in range(2):
                dk_a, dv_a = carry[hh]
                q = q_ref[hh, pl.ds(qs, rows), :]
                dov = do_ref[hh, pl.ds(qs, rows), :]
                p = jnp.exp2(_dot(q, kt[hh]))
                if diag:
                    p = jnp.where(col + (j * tk - i * tq - row0) <= row, p, 0.0)
                ds = (p * _dot(dov, vt[hh])).astype(BF16)
                dv_a = dv_a + _dot(dot_ref[hh, :, pl.ds(qs, rows)], p.astype(BF16))
                dk_a = dk_a + _dot(qt_ref[hh, :, pl.ds(qs, rows)], ds)
                dq_scr[hh, pl.ds(qs, rows), :] += _dot(ds, k[hh])
                out.append((dk_a, dv_a))
            return tuple(out)

        zero = (jnp.zeros((PAIR, tk), F32), jnp.zeros((PAIR, tk), F32))
        carry = lax.cond(j % 2 == 0, lambda c: step(j // 2, c, True),
                         lambda c: step(j // 2, c, True, rows=tk, row0=tk), (zero, zero))
        full0 = j // 2 + 1
        odd = (nq - full0) % 2
        carry = lax.cond(odd == 1, lambda c: step(full0, c, False), lambda c: c, carry)
        (dk0, dv0), (dk1, dv1) = lax.fori_loop(
            0, (nq - full0) // 2, lambda ii, c: step(full0 + odd + 2 * ii, c, False, rows=2 * tq), carry)
        first_t = lax.broadcasted_iota(jnp.int32, (PAIR, 1), 0) < HEAD_DIM
        first = lax.broadcasted_iota(jnp.int32, (1, PAIR), 1) < HEAD_DIM
        dk_ref[...] = (jnp.where(first_t, dk0, dk1) * (1.0 / LOG2E)).astype(BF16).T
        dkx_ref[...] = jnp.where(first_t, dk1, dk0).T
        dv_ref[...] = jnp.where(first_t, dv0, dv1).astype(BF16).T

        @pl.when(j == nk - 1)
        def _():
            dq_ref[...] = (jnp.where(first, dq_scr[0], dq_scr[1]) * Q_SCALE).astype(BF16)
            dqx_ref[...] = jnp.where(first, dq_scr[1], dq_scr[0])

        @pl.when((pl.program_id(0) == n_pairs - 1) & (j == nk - 1))
        def _():
            scatter.wait()

    resident = lambda: pl.BlockSpec((2, t, PAIR), lambda p, j: (p, 0, 0), pipeline_mode=pl.Buffered(1))
    resident_t = lambda: pl.BlockSpec((2, PAIR, t), lambda p, j: (p, 0, 0), pipeline_mode=pl.Buffered(1))
    kv_in = lambda: pl.BlockSpec((2, tk, PAIR), lambda p, j: (p, j, 0))
    whole = lambda: pl.BlockSpec((t, PAIR), lambda p, j: (0, p))
    tile = lambda: pl.BlockSpec((tk, PAIR), lambda p, j: (j, p))
    b16 = jax.ShapeDtypeStruct((t, ATTN_W), BF16)
    f32 = jax.ShapeDtypeStruct((t, ATTN_W), F32)
    res = pl.pallas_call(
        body, name="attn_bwd", grid=(n_pairs, nk),
        in_specs=[resident(), resident(), resident_t(), resident_t(), kv_in(),
                  pl.BlockSpec((2, PAIR, tk), lambda p, j: (p, 0, j))] + [HBM_SPEC] * n_g,
        out_specs=[whole(), whole(), tile(), tile(), tile()] + [HBM_SPEC] * n_g,
        out_shape=[b16, f32, b16, f32, b16] + [jax.ShapeDtypeStruct(g.shape, g.dtype) for g in gblocks],
        scratch_shapes=[pltpu.VMEM((2, t, PAIR), F32)] + _Exchange.scratch(n_g),
        compiler_params=_cp(("arbitrary", "arbitrary")),
    )(q_bwd, do_aug, q_bwd_t, do_aug_t, k_aug, v_aug_t, *gblocks)
    return res[:5], res[5:]


def _fgate_bwd(dqx, dkx, sgate, *, tm=256):
    t = sgate.shape[0]
    tm = min(tm, t)
    nsteps = t // tm

    def body(dq_ref, dk_ref, sg_ref, df_ref, dbf_ref, carry):
        @pl.when(pl.program_id(0) == 0)
        def _():
            carry[...] = jnp.zeros_like(carry)
            dbf_ref[...] = jnp.zeros_like(dbf_ref)
        lane = lax.broadcasted_iota(jnp.int32, (ATTN_W, F_PAD), 0)
        head = lax.broadcasted_iota(jnp.int32, (ATTN_W, F_PAD), 1)
        aux = (head // 2) * PAIR + HEAD_DIM * (1 - head % 2)
        valid = head < N_HEADS
        pick_r = (valid & (lane == aux + AUX_ROWSUM)).astype(F32)
        pick_c = (valid & (lane == aux + AUX_BIAS)).astype(F32)
        hp = lax.Precision.HIGHEST
        dcum = (jnp.dot(dq_ref[...], pick_r, preferred_element_type=F32, precision=lax.Precision.HIGH)
                + jnp.dot(dk_ref[...], pick_c, preferred_element_type=F32, precision=lax.Precision.HIGH))
        r = lax.broadcasted_iota(jnp.int32, (tm, tm), 0)
        c = lax.broadcasted_iota(jnp.int32, (tm, tm), 1)
        tri = (c >= r).astype(F32)
        rc = jnp.dot(tri, dcum, preferred_element_type=F32, precision=hp) + carry[...]
        carry[...] = rc[0:1, :]
        df = rc * sg_ref[...]
        df_ref[...] = df.astype(BF16)
        dbf_ref[...] += _rows8(df)

    rev = lambda i: nsteps - 1 - i
    return pl.pallas_call(
        body, name="fgate_bwd", grid=(nsteps,),
        in_specs=[pl.BlockSpec((tm, ATTN_W), lambda i: (rev(i), 0)), pl.BlockSpec((tm, ATTN_W), lambda i: (rev(i), 0)),
                  pl.BlockSpec((tm, F_PAD), lambda i: (rev(i), 0))],
        out_specs=[pl.BlockSpec((tm, F_PAD), lambda i: (rev(i), 0)), pl.BlockSpec((8, F_PAD), lambda i: (0, 0))],
        out_shape=[jax.ShapeDtypeStruct((t, F_PAD), BF16), jax.ShapeDtypeStruct((8, F_PAD), F32)],
        scratch_shapes=[pltpu.VMEM((1, F_PAD), F32)],
        compiler_params=_cp(("arbitrary",)),
    )(dqx, dkx, sgate)


def _inproj_bwd(dq, dk, dv, df, dbcx, dh, x, g, win_pt, gblock, *, tm=512):
    t, d = x.shape
    tm = min(tm, t)
    nsteps = t // tm
    n_qkv = 3 * ATTN_W

    def body(dq_ref, dk_ref, dv_ref, df_ref, db_ref, dh_ref, x_ref, g_ref, w_ref, gb_ref, gx_ref, dg_ref, land_ref,
             *sems):
        scatter = _Exchange([gb_ref], [land_ref], *sems, gather=False)

        @pl.when(pl.program_id(0) == 0)
        def _():
            scatter.start()
            dg_ref[...] = jnp.zeros_like(dg_ref)
        dn = _dot(df_ref[...], w_ref[n_qkv:n_qkv + F_PAD, :])
        for k, r in enumerate((dq_ref, dk_ref, dv_ref)):
            dn = dn + _dot(r[...], w_ref[k * ATTN_W:(k + 1) * ATTN_W, :])
        for k in range(3):
            c0 = n_qkv + F_PAD + k * CONV_CH
            dn = dn + _dot(db_ref[:, k * CONV_CH:(k + 1) * CONV_CH], w_ref[c0:c0 + CONV_CH, :])
        dx, dg = _norm_bwd(dn, x_ref[...], g_ref[...])
        gx_ref[...] = dh_ref[...] + dx
        dg_ref[...] += dg

        @pl.when(pl.program_id(0) == nsteps - 1)
        def _():
            scatter.wait()

    row = lambda n_: pl.BlockSpec((tm, n_), lambda i: (i, 0))
    return pl.pallas_call(
        body, name="inproj_bwd", grid=(nsteps,),
        in_specs=[row(ATTN_W), row(ATTN_W), row(ATTN_W), row(F_PAD), row(3 * CONV_CH), row(d), row(d),
                  _const_spec((1, d)), _const_spec(win_pt.shape), HBM_SPEC],
        out_specs=[row(d), pl.BlockSpec((8, d), lambda i: (0, 0)), HBM_SPEC],
        out_shape=[jax.ShapeDtypeStruct((t, d), F32), jax.ShapeDtypeStruct((8, d), F32),
                   jax.ShapeDtypeStruct(gblock.shape, gblock.dtype)],
        scratch_shapes=_Exchange.scratch(1),
        compiler_params=_cp(("arbitrary",)),
    )(dq, dk, dv, df, dbcx, dh, x, g, win_pt, gblock)


LATE = ("w_out_0", "w_up_0", "w_down_0", "pool_w_1", "w_up_1", "w_down_1")


def _local_step(x, target, gains, b_f, conv_w, pool_scale, win_pt, shards):
    d = x.shape[1]
    n0, qkv, flog, bcx, cv = _norm_inproj(x, gains["mix0"], win_pt, conv_w)
    q_aug_t, k_aug, v_aug_t, sgate = _fgate_prep(flog, b_f, qkv)
    att, q_bwd, q_bwd_t, gathered = _attn_fwd(q_aug_t, k_aug, v_aug_t, [shards[n] for n in LATE])
    g = dict(zip(LATE, gathered))
    wout = g["w_out_0"].reshape(d, d)
    wup0, wup1 = g["w_up_0"], g["w_up_1"]
    wdown0, wdown1 = g["w_down_0"].reshape(-1, d), g["w_down_1"].reshape(-1, d)
    n_grp = len(POOL_WINDOWS)
    cg = d // n_grp
    poolw = g["pool_w_1"].reshape(N_DEV, n_grp, cg // N_DEV, cg).transpose(1, 0, 2, 3).reshape(n_grp, cg, cg)
    h1 = _outproj(att, cv, x, wout)
    h2, n1, a0, z0 = _mlp_fwd(h1, gains["ffn0"], wup0, wdown0, name="mlp_fwd0")
    h3, pooled = _pool_fwd(h2, gains["mix1"], poolw, pool_scale)
    loss, dh4, dh4_b, dg_final, n3, a1, z1 = _mlp_fwd_loss(h3, gains["ffn1"], wup1, wdown1, gains["final"], target,
                                                           name="mlp_fwd1")
    f = a1.shape[1]
    fb = f // N_DEV
    dh3, da1, dg_ffn1 = _mlp_bwd(dh4, h3, a1, gains["ffn1"], wup1, wdown1, name="mlp_bwd1")
    dwdown1 = _mm_tn(z1, dh4_b, name="dwdown1", ta=1024, tb=1024, tt=4096, out_dtype=BF16)
    dwup1 = _mm_tn(n3, da1, name="dwup1", ta=d, tb=fb, tt=4096, blocked_out=True, out_dtype=BF16)
    dh2, dh2_b, dpoolw, dscale, dg_mix1 = _pool_bwd(dh3, h2, pooled, gains["mix1"], poolw, pool_scale)
    dh1, da0, dg_ffn0 = _mlp_bwd(dh2, h1, a0, gains["ffn0"], wup0, wdown0, name="mlp_bwd0")
    dwdown0 = _mm_tn(z0, dh2_b, name="dwdown0", ta=1024, tb=1024, tt=4096, out_dtype=BF16)
    dwup0 = _mm_tn(n1, da0, name="dwup0", ta=d, tb=fb, tt=4096, blocked_out=True, out_dtype=BF16)
    do_aug, do_aug_t, dbcx, dconvw = _outproj_conv_bwd(dh1, att, wout, bcx, conv_w)
    dwout = _mm_tn_cat([att, cv], [dh1], name="dwout", tt=2048)
    gblocks = {
        "w_out_0": dwout.reshape(N_DEV, d // N_DEV, d), "w_up_0": dwup0, "w_up_1": dwup1,
        "w_down_0": dwdown0.reshape(N_DEV, -1, d), "w_down_1": dwdown1.reshape(N_DEV, -1, d),
        "pool_w_1": dpoolw.astype(BF16).reshape(n_grp, N_DEV, cg // N_DEV, cg).transpose(1, 0, 2, 3).reshape(
            N_DEV, n_grp * (cg // N_DEV), cg),
    }
    (dq, dqx, dk, dkx, dv), landed = _attn_bwd(q_bwd, do_aug, q_bwd_t, do_aug_t, k_aug, v_aug_t,
                                               [gblocks[n] for n in LATE])
    df, dbf = _fgate_bwd(dqx, dkx, sgate)
    dwin_t = jnp.concatenate(
        [_mm_tn_cat([dq, dk, dv], [n0], name="dwin_qkv", tt=2048),
         _mm_tn(df, n0, name="dwin_f", ta=F_PAD, tb=d, tt=2048, out_dtype=BF16)[:N_HEADS],
         _mm_tn(dbcx, n0, name="dwin_bcx", ta=512, tb=d, tt=4096, out_dtype=BF16)], axis=0)
    dwin_blocks = dwin_t.reshape(N_DEV, dwin_t.shape[0] // N_DEV, d)
    grad_x, dg_mix0, landed_win = _inproj_bwd(dq, dk, dv, df, dbcx, dh1, x, gains["mix0"], win_pt, dwin_blocks)
    small = dict(mix0=dg_mix0, ffn0=dg_ffn0, mix1=dg_mix1, pool_scale=dscale, ffn1=dg_ffn1, final=dg_final,
                 b_f=dbf, conv_w=dconvw)
    return loss, grad_x, dict(zip(LATE + ("w_in_0",), tuple(landed) + (landed_win,))), small


def _all_gather(shards):
    n = len(shards)

    def body(*refs):
        gather = _TwoLevelGather(refs[:n], refs[n:2 * n], *refs[2 * n:])
        gather.start()
        gather.forward()
        gather.wait()

    return pl.pallas_call(
        body, name="all_gather",
        in_specs=[HBM_SPEC] * n, out_specs=[HBM_SPEC] * n,
        out_shape=[jax.ShapeDtypeStruct((N_DEV,) + s.shape, s.dtype) for s in shards],
        scratch_shapes=[pltpu.SemaphoreType.DMA((7 * n,)), pltpu.SemaphoreType.DMA((7 * n,)),
                        pltpu.SemaphoreType.DMA((n,))],
    )(*shards)


SMALL_ROWS = 16


def _small_allreduce(parts):
    n, _, w = parts.shape
    assert n <= SMALL_ROWS

    def body(p_ref, o_ref, gath, send_sems, recv_sems):
        x, y, c = lax.axis_index("x"), lax.axis_index("y"), lax.axis_index("c")
        my = _slot(x, y, c)
        rows = [jnp.sum(p_ref[i], axis=0, keepdims=True) for i in range(n)]
        rows.append(jnp.zeros((SMALL_ROWS - n, w), F32))
        gath[my] = jnp.concatenate(rows, axis=0)
        copies = []
        for k in range(1, N_DEV):
            px, py, pc = x ^ (k >> 2), y ^ ((k >> 1) & 1), c ^ (k & 1)
            cp = pltpu.make_async_remote_copy(
                src_ref=gath.at[my], dst_ref=gath.at[my], send_sem=send_sems.at[k - 1], recv_sem=recv_sems.at[k - 1],
                device_id=(px, py, pc), device_id_type=MESH)
            cp.start()
            copies.append(cp)
        for cp in copies:
            cp.wait()
        acc = gath[0]
        for d in range(1, N_DEV):
            acc = acc + gath[d]
        o_ref[...] = acc

    return pl.pallas_call(
        body, name="small_allreduce",
        in_specs=[VMEM_SPEC], out_specs=VMEM_SPEC,
        out_shape=jax.ShapeDtypeStruct((SMALL_ROWS, w), F32),
        scratch_shapes=[pltpu.VMEM((N_DEV, SMALL_ROWS, w), F32), pltpu.SemaphoreType.DMA((N_DEV - 1,)),
                        pltpu.SemaphoreType.DMA((N_DEV - 1,))],
    )(parts)


def _adamw(g, w, m, v, *, name, tm=256):
    r, c = g.shape
    tm = tm if r % tm == 0 else r
    bc1 = 1.0 - ADAM_B1 ** ADAM_STEP
    bc2 = 1.0 - ADAM_B2 ** ADAM_STEP

    def body(g_ref, w_ref, m_ref, v_ref, d_ref, nm_ref, nv_ref):
        gv = g_ref[...]
        nm = ADAM_B1 * m_ref[...] + (1.0 - ADAM_B1) * gv
        nv = ADAM_B2 * v_ref[...] + (1.0 - ADAM_B2) * jnp.square(gv)
        nm_ref[...] = nm
        nv_ref[...] = nv
        d_ref[...] = -ADAM_LR * ((nm / bc1) / (jnp.sqrt(nv / bc2) + ADAM_EPS) + ADAM_WD * w_ref[...])

    blk = pl.BlockSpec((tm, c), lambda i: (i, 0))
    shp = jax.ShapeDtypeStruct((r, c), F32)
    return pl.pallas_call(
        body, name=name, grid=(r // tm,), in_specs=[blk] * 4, out_specs=[blk] * 3, out_shape=[shp] * 3,
        compiler_params=_cp(("parallel",)),
    )(g, w, m, v)


def _transpose_cast(a, *, name):
    def body(a_ref, o_ref):
        o_ref[...] = a_ref[...].T.astype(BF16)

    return pl.pallas_call(body, name=name, out_shape=jax.ShapeDtypeStruct(a.shape[::-1], BF16),
                          compiler_params=_cp())(a)


def _adamw_sum_t(parts, w, m, v, *, name):
    bc1 = 1.0 - ADAM_B1 ** ADAM_STEP
    bc2 = 1.0 - ADAM_B2 ** ADAM_STEP

    def body(p_ref, w_ref, m_ref, v_ref, g_ref, d_ref, nm_ref, nv_ref):
        acc = p_ref[0].astype(F32)
        for k in range(1, N_DEV):
            acc = acc + p_ref[k].astype(F32)
        gv = acc.T
        g_ref[...] = gv
        nm = ADAM_B1 * m_ref[...] + (1.0 - ADAM_B1) * gv
        nv = ADAM_B2 * v_ref[...] + (1.0 - ADAM_B2) * jnp.square(gv)
        nm_ref[...] = nm
        nv_ref[...] = nv
        d_ref[...] = -ADAM_LR * ((nm / bc1) / (jnp.sqrt(nv / bc2) + ADAM_EPS) + ADAM_WD * w_ref[...])

    shp = jax.ShapeDtypeStruct(w.shape, F32)
    return pl.pallas_call(body, name=name, out_shape=[shp] * 4, compiler_params=_cp())(parts, w, m, v)


def _adamw_sum(parts, w, m, v, *, name, tm=256):
    _, r, c = parts.shape
    tm = tm if r % tm == 0 else r
    bc1 = 1.0 - ADAM_B1 ** ADAM_STEP
    bc2 = 1.0 - ADAM_B2 ** ADAM_STEP

    def body(p_ref, w_ref, m_ref, v_ref, g_ref, d_ref, nm_ref, nv_ref):
        gv = p_ref[0].astype(F32)
        for k in range(1, N_DEV):
            gv = gv + p_ref[k].astype(F32)
        g_ref[...] = gv
        nm = ADAM_B1 * m_ref[...] + (1.0 - ADAM_B1) * gv
        nv = ADAM_B2 * v_ref[...] + (1.0 - ADAM_B2) * jnp.square(gv)
        nm_ref[...] = nm
        nv_ref[...] = nv
        d_ref[...] = -ADAM_LR * ((nm / bc1) / (jnp.sqrt(nv / bc2) + ADAM_EPS) + ADAM_WD * w_ref[...])

    blk = pl.BlockSpec((tm, c), lambda i: (i, 0))
    shp = jax.ShapeDtypeStruct((r, c), F32)
    return pl.pallas_call(
        body, name=name, grid=(r // tm,), in_specs=[pl.BlockSpec((N_DEV, tm, c), lambda i: (0, i, 0))] + [blk] * 3,
        out_specs=[blk] * 4, out_shape=[shp] * 4, compiler_params=_cp(("parallel",)),
    )(parts, w, m, v)


BIG = ("w_in_0", "w_out_0", "w_up_0", "w_down_0", "pool_w_1", "w_up_1", "w_down_1")
SMALL = ("norm_mix_0", "norm_ffn_0", "norm_mix_1", "pool_scale_1", "norm_ffn_1", "final_norm", "b_f_0", "conv_w_0")
WEIGHTS = ("norm_mix_0", "w_in_0", "b_f_0", "conv_w_0", "w_out_0", "norm_ffn_0", "w_up_0", "w_down_0", "norm_mix_1",
           "pool_w_1", "pool_scale_1", "norm_ffn_1", "w_up_1", "w_down_1", "final_norm")


def _pad_to(a, rows, cols):
    return jnp.pad(a, ((0, rows - a.shape[0]), (0, cols - a.shape[1])))


def _pack_small(p, width):
    rows = [p[n].reshape(1, -1) for n in SMALL[:6]]
    rows.append(_pad_to(p["b_f_0"].reshape(1, -1), 1, width))
    rows.append(_pad_to(p["conv_w_0"], 3, width))
    return _pad_to(jnp.concatenate(rows, axis=0), SMALL_ROWS, width)


def _unpack_small(a, like):
    out = {n: a[i] for i, n in enumerate(SMALL[:6])}
    out["b_f_0"] = a[6, :like["b_f_0"].shape[0]]
    out["conv_w_0"] = a[7:10, :like["conv_w_0"].shape[1]]
    return out


def kernel(x, norm_mix_0, w_in_0, b_f_0, conv_w_0, w_out_0, norm_ffn_0, w_up_0, w_down_0, norm_mix_1, pool_w_1, pool_scale_1, norm_ffn_1, w_up_1, w_down_1, final_norm, loss_target, m_norm_mix_0, m_w_in_0, m_b_f_0, m_conv_w_0, m_w_out_0, m_norm_ffn_0, m_w_up_0, m_w_down_0, m_norm_mix_1, m_pool_w_1, m_pool_scale_1, m_norm_ffn_1, m_w_up_1, m_w_down_1, m_final_norm, v_norm_mix_0, v_w_in_0, v_b_f_0, v_conv_w_0, v_w_out_0, v_norm_ffn_0, v_w_up_0, v_w_down_0, v_norm_mix_1, v_pool_w_1, v_pool_scale_1, v_norm_ffn_1, v_w_up_1, v_w_down_1, v_final_norm):
    w = dict(norm_mix_0=norm_mix_0, w_in_0=w_in_0, b_f_0=b_f_0, conv_w_0=conv_w_0, w_out_0=w_out_0,
             norm_ffn_0=norm_ffn_0, w_up_0=w_up_0, w_down_0=w_down_0, norm_mix_1=norm_mix_1, pool_w_1=pool_w_1,
             pool_scale_1=pool_scale_1, norm_ffn_1=norm_ffn_1, w_up_1=w_up_1, w_down_1=w_down_1, final_norm=final_norm)
    m = dict(norm_mix_0=m_norm_mix_0, w_in_0=m_w_in_0, b_f_0=m_b_f_0, conv_w_0=m_conv_w_0, w_out_0=m_w_out_0,
             norm_ffn_0=m_norm_ffn_0, w_up_0=m_w_up_0, w_down_0=m_w_down_0, norm_mix_1=m_norm_mix_1,
             pool_w_1=m_pool_w_1, pool_scale_1=m_pool_scale_1, norm_ffn_1=m_norm_ffn_1, w_up_1=m_w_up_1,
             w_down_1=m_w_down_1, final_norm=m_final_norm)
    v = dict(norm_mix_0=v_norm_mix_0, w_in_0=v_w_in_0, b_f_0=v_b_f_0, conv_w_0=v_conv_w_0, w_out_0=v_w_out_0,
             norm_ffn_0=v_norm_ffn_0, w_up_0=v_w_up_0, w_down_0=v_w_down_0, norm_mix_1=v_norm_mix_1,
             pool_w_1=v_pool_w_1, pool_scale_1=v_pool_scale_1, norm_ffn_1=v_norm_ffn_1, w_up_1=v_w_up_1,
             w_down_1=v_w_down_1, final_norm=v_final_norm)
    d = x.shape[-1]
    n_in = w_in_0.shape[1] * N_DEV
    n_qkv = 3 * ATTN_W
    pool_g, pool_rows, pool_c = pool_w_1.shape

    def shard2d(p):
        return {n: (p[n].reshape(pool_g * pool_rows, pool_c) if n == "pool_w_1" else p[n]) for n in BIG}
    w2, m2, v2 = shard2d(w), shard2d(m), shard2d(v)

    conv_cols = conv_w_0.shape[1]
    win_g8, conv_g8 = _all_gather([_transpose_cast(w_in_0, name="w_in_t"), _pad_to(conv_w_0, 8, 128)])
    conv_full = conv_g8[:, :, :conv_cols].transpose(1, 0, 2).reshape(8, N_DEV * conv_cols)
    win_t = win_g8.reshape(n_in, d)
    win_pt = jnp.concatenate([win_t[:n_qkv], _pad_to(win_t[n_qkv:n_qkv + N_HEADS], F_PAD, d),
                              win_t[n_qkv + N_HEADS:]], axis=0)

    gains = dict(mix0=norm_mix_0.reshape(1, d), ffn0=norm_ffn_0.reshape(1, d), mix1=norm_mix_1.reshape(1, d),
                 ffn1=norm_ffn_1.reshape(1, d), final=final_norm.reshape(1, d))
    dev = _slot(lax.axis_index("x"), lax.axis_index("y"), lax.axis_index("c"))
    loss8, grad_x, landed, small = _local_step(
        x[0], loss_target[0], gains, _pad_to(b_f_0.reshape(1, -1), 1, F_PAD), conv_full, pool_scale_1.reshape(1, d),
        win_pt, {n: w2[n].astype(BF16) for n in LATE})
    parts = jnp.concatenate(
        [small[k][None] for k in ("mix0", "ffn0", "mix1", "pool_scale", "ffn1", "final")]
        + [_pad_to(small["b_f"], 8, d)[None], jnp.pad(small["conv_w"], ((0, 0), (0, 0), (0, d - CONV_CH))),
           _pad_to(loss8[0:1, 0:1], 8, d)[None]], axis=0)
    tot = _small_allreduce(parts)
    loss = tot[10, 0]
    conv_g = lax.dynamic_slice(tot, (7, dev * conv_cols), (3, conv_cols))
    gs = tot.at[7:10].set(_pad_to(conv_g, 3, d))

    grads, deltas, new_m, new_v = {}, {}, {}, {}
    for n in BIG:
        if n in LATE:
            gr, dl, nm, nv = _adamw_sum(landed[n], w2[n], m2[n], v2[n], name="adamw_" + n)
        else:
            gr, dl, nm, nv = _adamw_sum_t(landed[n], w2[n], m2[n], v2[n], name="adamw_" + n)
        for dst, val in ((grads, gr), (deltas, dl), (new_m, nm), (new_v, nv)):
            dst[n] = val.reshape(w[n].shape)
    dl, nm, nv = _adamw(gs, _pack_small(w, d), _pack_small(m, d), _pack_small(v, d), name="adamw_small")
    for dst, val in ((grads, gs), (deltas, dl), (new_m, nm), (new_v, nv)):
        dst.update(_unpack_small(val, w))
    return (loss, grad_x[None], *[grads[n] for n in WEIGHTS], *[deltas[n] for n in WEIGHTS],
            *[new_m[n] for n in WEIGHTS], *[new_v[n] for n in WEIGHTS])
```

```python
import functools

import jax
import jax.numpy as jnp
from jax import lax
from jax.experimental import pallas as pl
from jax.experimental.pallas import tpu as pltpu

F32 = jnp.float32
BF16 = jnp.bfloat16

N_DEV = 8
N_HEADS = 8
HEAD_DIM = 64
PAIR = 2 * HEAD_DIM
ATTN_W = N_HEADS * HEAD_DIM
CONV_CH = 512
F_PAD = 128
POOL_WINDOWS = (2, 4, 8, 16)
POOL_HALO = 16
CONV_HALO = 16
RMS_EPS = 1e-6
Q_SCALE = HEAD_DIM ** -0.5
LOG2E = 1.4426950408889634
NEG = -1e30
AUX_BIAS = 0
AUX_LSE = 3
AUX_ROWSUM = 6
ADAM_LR, ADAM_B1, ADAM_B2, ADAM_EPS, ADAM_WD, ADAM_STEP = 0.001, 0.9, 0.999, 1e-08, 0.01, 10
MESH = pl.DeviceIdType.MESH
VMEM_LIMIT = 56 * 2**20


def _cp(sem=None, vmem=VMEM_LIMIT, **kw):
    return pltpu.CompilerParams(dimension_semantics=sem, vmem_limit_bytes=vmem, **kw)


def _dot(a, b):
    return jnp.dot(a, b, preferred_element_type=F32)


def _dot_nt(a, b):
    return lax.dot_general(a, b, (((1,), (1,)), ((), ())), preferred_element_type=F32)


def _dot_tn(a, b):
    return lax.dot_general(a, b, (((0,), (0,)), ((), ())), preferred_element_type=F32)


def _rstd(h):
    return lax.rsqrt(jnp.mean(h * h, axis=-1, keepdims=True) + RMS_EPS)


def _rows8(x):
    r, n = x.shape
    return jnp.sum(x.reshape(r // 8, 8, n), axis=0)


def _norm_bwd(dn, h, g):
    r = _rstd(h)
    xhat = h * r
    dy = dn * g
    dh = r * (dy - xhat * jnp.mean(dy * xhat, axis=-1, keepdims=True))
    return dh, _rows8(dn * xhat)


def _const_spec(shape):
    nd = len(shape)
    return pl.BlockSpec(shape, lambda *_: (0,) * nd, pipeline_mode=pl.Buffered(1))


HBM_SPEC = pl.BlockSpec(memory_space=pltpu.HBM)
VMEM_SPEC = pl.BlockSpec(memory_space=pltpu.VMEM)


def _slot(px, py, pc):
    return 4 * px + 2 * py + pc


class _Exchange:
    def __init__(self, srcs, dsts, send_sems, recv_sems, local_sems, gather):
        x, y, c = lax.axis_index("x"), lax.axis_index("y"), lax.axis_index("c")
        me = _slot(x, y, c)
        self.copies = []
        for a, (src, dst) in enumerate(zip(srcs, dsts)):
            self.copies.append(pltpu.make_async_copy(src if gather else src.at[me], dst.at[me], local_sems.at[a]))
            for k in range(1, N_DEV):
                px, py, pc = x ^ (k >> 2), y ^ ((k >> 1) & 1), c ^ (k & 1)
                self.copies.append(pltpu.make_async_remote_copy(
                    src_ref=src if gather else src.at[_slot(px, py, pc)], dst_ref=dst.at[me],
                    send_sem=send_sems.at[(N_DEV - 1) * a + k - 1], recv_sem=recv_sems.at[(N_DEV - 1) * a + k - 1],
                    device_id=(px, py, pc), device_id_type=MESH))

    def start(self):
        for cp in self.copies:
            cp.start()

    def wait(self):
        for cp in self.copies:
            cp.wait()

    @staticmethod
    def scratch(n):
        return [pltpu.SemaphoreType.DMA(((N_DEV - 1) * n,)), pltpu.SemaphoreType.DMA(((N_DEV - 1) * n,)),
                pltpu.SemaphoreType.DMA((n,))]


def _mesh_places():
    x, y, c = lax.axis_index("x"), lax.axis_index("y"), lax.axis_index("c")
    chips = [(1 - x, y), (x, 1 - y), (1 - x, 1 - y)]
    return (x, y, c), (x, y, 1 - c), chips


class _TwoLevelGather:
    def __init__(self, srcs, dsts, send_sems, recv_sems, local_sems):
        me, sib, chips = _mesh_places()
        c = me[2]
        n = len(srcs)

        def copy(a, k, block, to, src=None):
            dst = dsts[a].at[_slot(*block)]
            return pltpu.make_async_remote_copy(
                src_ref=dst if src is None else src, dst_ref=dst, send_sem=send_sems.at[7 * a + k],
                recv_sem=recv_sems.at[7 * a + k], device_id=to, device_id_type=MESH)

        self.mine = [pltpu.make_async_copy(srcs[a], dsts[a].at[_slot(*me)], local_sems.at[a]) for a in range(n)]
        self.first, self.landed, self.passed, self.rest = [], [], [], []
        for a in range(n):
            self.first.append(copy(a, 0, me, sib, src=srcs[a]))
            self.first += [copy(a, 1 + j, me, (*chip, c), src=srcs[a]) for j, chip in enumerate(chips)]
            self.landed += [copy(a, 1 + j, (*chip, c), me) for j, chip in enumerate(chips)]
            self.passed += [copy(a, 4 + j, (*chip, c), sib) for j, chip in enumerate(chips)]
            self.rest.append(copy(a, 0, sib, me))
            self.rest += [copy(a, 4 + j, (*chip, 1 - c), me) for j, chip in enumerate(chips)]

    def start(self):
        for cp in self.mine + self.first:
            cp.start()

    def forward(self):
        for arrived, onward in zip(self.landed, self.passed):
            arrived.wait_recv()
            onward.start()

    def wait(self):
        for cp in self.rest:
            cp.wait_recv()
        for cp in self.first + self.passed:
            cp.wait_send()
        for cp in self.mine:
            cp.wait()


def _norm_inproj(x, g, win_pt, conv_w, *, tm=512):
    t, d = x.shape
    n_all = win_pt.shape[0]
    n_qkv = 3 * ATTN_W
    n_bcx = 3 * CONV_CH
    assert n_all == n_qkv + F_PAD + n_bcx
    tm = min(tm, t)
    ch = CONV_CH

    def body(x_ref, g_ref, w_ref, cw_ref, n_ref, qkv_ref, f_ref, bcx_ref, cv_ref, ext):
        h = x_ref[...]
        n = (h * _rstd(h) * g_ref[...]).astype(BF16)
        n_ref[...] = n
        for c0 in range(0, n_qkv, 512):
            acc = _dot_nt(n, w_ref[c0:c0 + 512, :])
            if c0 < ATTN_W:
                acc = acc * (Q_SCALE * LOG2E)
            qkv_ref[:, c0:c0 + 512] = acc.astype(BF16)
        f_ref[...] = _dot_nt(n, w_ref[n_qkv:n_qkv + F_PAD, :])
        bcx = []
        for k in range(3):
            c0 = n_qkv + F_PAD + k * ch
            v = _dot_nt(n, w_ref[c0:c0 + ch, :]).astype(BF16)
            bcx_ref[:, k * ch:(k + 1) * ch] = v
            bcx.append(v.astype(F32))
        @pl.when(pl.program_id(0) == 0)
        def _():
            ext[tm:tm + CONV_HALO, :] = jnp.zeros((CONV_HALO, ch), F32)
        ext[0:CONV_HALO, :] = ext[tm:tm + CONV_HALO, :]
        ext[CONV_HALO:CONV_HALO + tm, :] = bcx[1] * bcx[2]
        conv = (cw_ref[0:1, :] * ext[CONV_HALO - 2:CONV_HALO - 2 + tm, :]
                + cw_ref[1:2, :] * ext[CONV_HALO - 1:CONV_HALO - 1 + tm, :]
                + cw_ref[2:3, :] * ext[CONV_HALO:CONV_HALO + tm, :])
        cv_ref[...] = (bcx[0] * conv).astype(BF16)

    return pl.pallas_call(
        body, name="norm_inproj", grid=(t // tm,),
        in_specs=[pl.BlockSpec((tm, d), lambda i: (i, 0)), _const_spec((1, d)), _const_spec((n_all, d)),
                  _const_spec((8, ch))],
        out_specs=[pl.BlockSpec((tm, d), lambda i: (i, 0)), pl.BlockSpec((tm, n_qkv), lambda i: (i, 0)),
                   pl.BlockSpec((tm, F_PAD), lambda i: (i, 0)), pl.BlockSpec((tm, n_bcx), lambda i: (i, 0)),
                   pl.BlockSpec((tm, ch), lambda i: (i, 0))],
        out_shape=[jax.ShapeDtypeStruct((t, d), BF16), jax.ShapeDtypeStruct((t, n_qkv), BF16),
                   jax.ShapeDtypeStruct((t, F_PAD), F32), jax.ShapeDtypeStruct((t, n_bcx), BF16),
                   jax.ShapeDtypeStruct((t, ch), BF16)],
        scratch_shapes=[pltpu.VMEM((CONV_HALO + tm, ch), F32)],
        compiler_params=_cp(("arbitrary",)),
    )(x, g, win_pt, conv_w)


def _head_lanes(h):
    lane = lax.broadcasted_iota(jnp.int32, (1, PAIR), 1)
    hh = h % 2
    return lane, lane // HEAD_DIM == hh, HEAD_DIM * (1 - hh)


def _pieces(col):
    hi = col.astype(BF16).astype(F32)
    r1 = col - hi
    mid = r1.astype(BF16).astype(F32)
    lo = (r1 - mid).astype(BF16).astype(F32)
    return hi, mid, lo


def _put_pieces(lane, first, col, other):
    hi, mid, lo = _pieces(col)
    return jnp.where(lane == first, hi, jnp.where(lane == first + 1, mid, jnp.where(lane == first + 2, lo, other)))


def _fgate_prep(flog, b_f, qkv, *, tm=512):
    t = flog.shape[0]
    tm = min(tm, t)

    def body(f_ref, b_ref, qkv_ref, qat_ref, ka_ref, vat_ref, sg_ref, carry):
        @pl.when(pl.program_id(0) == 0)
        def _():
            carry[...] = jnp.zeros_like(carry)
        z = f_ref[...] + b_ref[...]
        e = jnp.exp(-jnp.abs(z))
        logf = jnp.minimum(z, 0.0) - jnp.log(1.0 + e)
        sg_ref[...] = jnp.where(z >= 0, e, 1.0) / (1.0 + e)
        r = lax.broadcasted_iota(jnp.int32, (tm, tm), 0)
        c = lax.broadcasted_iota(jnp.int32, (tm, tm), 1)
        tri = (c <= r).astype(F32)
        cs = jnp.dot(tri, logf, preferred_element_type=F32, precision=lax.Precision.HIGHEST) + carry[...]
        carry[...] = cs[tm - 1:tm, :]
        cs2 = cs * LOG2E
        for h in range(N_HEADS):
            lane, head, aux = _head_lanes(h)
            p0 = (h // 2) * PAIR
            ones = ((lane >= aux + AUX_LSE) & (lane <= aux + AUX_ROWSUM)).astype(F32)
            bias = (lane >= aux + AUX_BIAS) & (lane < aux + AUX_BIAS + 3)
            k_aux = _put_pieces(lane, aux + AUX_BIAS, cs2[:, h:h + 1], ones)
            q_aug = jnp.where(head, qkv_ref[:, p0:p0 + PAIR], jnp.where(bias, -1.0, 0.0).astype(BF16))
            v_aug = jnp.where(head, qkv_ref[:, 2 * ATTN_W + p0:2 * ATTN_W + p0 + PAIR],
                              jnp.where(bias, 1.0, 0.0).astype(BF16))
            qat_ref[h] = q_aug.T
            ka_ref[h] = jnp.where(head, qkv_ref[:, ATTN_W + p0:ATTN_W + p0 + PAIR], k_aux.astype(BF16))
            vat_ref[h] = v_aug.T

    aug = lambda: pl.BlockSpec((N_HEADS, tm, PAIR), lambda i: (0, i, 0))
    aug_t = lambda: pl.BlockSpec((N_HEADS, PAIR, tm), lambda i: (0, 0, i))
    aug_shape = jax.ShapeDtypeStruct((N_HEADS, t, PAIR), BF16)
    aug_t_shape = jax.ShapeDtypeStruct((N_HEADS, PAIR, t), BF16)
    return pl.pallas_call(
        body, name="fgate_prep", grid=(t // tm,),
        in_specs=[pl.BlockSpec((tm, F_PAD), lambda i: (i, 0)), _const_spec((1, F_PAD)),
                  pl.BlockSpec((tm, 3 * ATTN_W), lambda i: (i, 0))],
        out_specs=[aug_t(), aug(), aug_t(), pl.BlockSpec((tm, F_PAD), lambda i: (i, 0))],
        out_shape=[aug_t_shape, aug_shape, aug_t_shape, jax.ShapeDtypeStruct((t, F_PAD), F32)],
        scratch_shapes=[pltpu.VMEM((1, F_PAD), F32)],
        compiler_params=_cp(("arbitrary",)),
    )(flog, b_f, qkv)


def _put_pieces_t(row, first, vec, other):
    hi, mid, lo = _pieces(vec)
    return jnp.where(row == first, hi, jnp.where(row == first + 1, mid, jnp.where(row == first + 2, lo, other)))


def _attn_fwd(q_aug_t, k_aug, v_aug_t, shards, *, tq=1024):
    t = k_aug.shape[1]
    tq = min(tq, t)
    tk = tq // 2
    nq = t // tq
    n_pairs = ATTN_W // PAIR
    n_sh = len(shards)
    forward_step = (3 * n_pairs * nq) // 4

    def body(qt_ref, k_ref, vt_ref, *rest):
        o_ref, qb_ref, qbt_ref = rest[n_sh:n_sh + 3]
        s_scr = rest[2 * n_sh + 3]
        gather = _TwoLevelGather(rest[:n_sh], rest[n_sh + 3:2 * n_sh + 3], *rest[2 * n_sh + 4:])
        i = pl.program_id(1)
        step = pl.program_id(0) * nq + i

        @pl.when(step == 0)
        def _():
            gather.start()

        @pl.when(step == forward_step)
        def _():
            gather.forward()
        key = lax.broadcasted_iota(jnp.int32, (tk, tq), 0)
        qry = lax.broadcasted_iota(jnp.int32, (tk, tq), 1)
        qt = [qt_ref[0], qt_ref[1]]

        def logits(hh, tile, slot, diag):
            s = _dot(k_ref[hh, pl.ds(pl.multiple_of(tile * tk, tk), tk), :], qt[hh])
            if diag:
                s = jnp.where(key + (tile * tk - i * tq) <= qry, s, NEG)
            s_scr[hh, slot] = s
            return jnp.max(s, axis=0, keepdims=True)

        def probs(hh, tile, slot, m, acc, tmax):
            mn = jnp.maximum(m, tmax)
            p = jnp.exp2(s_scr[hh, slot] - mn).astype(BF16)
            acc = jnp.exp2(m - mn) * acc + _dot(vt_ref[hh, :, pl.ds(pl.multiple_of(tile * tk, tk), tk)], p)
            return mn, acc

        def advance(carry, prev, slot, nxt, diag=False):
            out = []
            for hh in range(2):
                m, acc, tmax = carry[hh]
                m, acc = probs(hh, prev, slot, m, acc, tmax)
                out.append((m, acc, logits(hh, nxt, 1 - slot, diag)))
            return tuple(out)

        def two_tiles(jj, carry):
            carry = advance(carry, jnp.where(jj == 0, 2 * i, 2 * jj - 1), 1, 2 * jj)
            return advance(carry, 2 * jj, 0, 2 * jj + 1)

        init = tuple((jnp.full((1, tq), NEG, F32), jnp.zeros((PAIR, tq), F32), logits(hh, 2 * i + 1, 0, True))
                     for hh in range(2))
        carry = advance(init, 2 * i + 1, 0, 2 * i, diag=True)
        carry = lax.fori_loop(0, i // 2, lambda jj, c: two_tiles(2 * jj + 1, two_tiles(2 * jj, c)), carry)
        carry = lax.cond(i % 2 == 1, lambda c: two_tiles(i - 1, c), lambda c: c, carry)
        last = jnp.where(i == 0, 2 * i, 2 * i - 1)
        row = lax.broadcasted_iota(jnp.int32, (PAIR, 1), 0)
        res = []
        for hh in range(2):
            aux = HEAD_DIM * (1 - hh)
            m, acc, tmax = carry[hh]
            m, acc = probs(hh, last, 1, m, acc, tmax)
            l = acc[aux + AUX_BIAS:aux + AUX_BIAS + 1, :]
            qbt = _put_pieces_t(row, aux + AUX_LSE, -(m + jnp.log2(l)), qt[hh].astype(F32))
            qbt_ref[hh] = qbt.astype(BF16)
            qb_ref[hh] = qbt.astype(BF16).T
            res.append(acc * (1.0 / l))
        o_ref[...] = jnp.where(row < HEAD_DIM, res[0], res[1]).astype(BF16).T

        @pl.when((pl.program_id(0) == n_pairs - 1) & (i == nq - 1))
        def _():
            gather.wait()

    res = pl.pallas_call(
        body, name="attn_fwd", grid=(n_pairs, nq),
        in_specs=[pl.BlockSpec((2, PAIR, tq), lambda p, i: (p, 0, i)),
                  pl.BlockSpec((2, t, PAIR), lambda p, i: (p, 0, 0), pipeline_mode=pl.Buffered(1)),
                  pl.BlockSpec((2, PAIR, t), lambda p, i: (p, 0, 0), pipeline_mode=pl.Buffered(1))] + [HBM_SPEC] * n_sh,
        out_specs=[pl.BlockSpec((tq, PAIR), lambda p, i: (i, p)),
                   pl.BlockSpec((2, tq, PAIR), lambda p, i: (p, i, 0)),
                   pl.BlockSpec((2, PAIR, tq), lambda p, i: (p, 0, i))] + [HBM_SPEC] * n_sh,
        out_shape=[jax.ShapeDtypeStruct((t, ATTN_W), BF16), jax.ShapeDtypeStruct((N_HEADS, t, PAIR), BF16),
                   jax.ShapeDtypeStruct((N_HEADS, PAIR, t), BF16)]
        + [jax.ShapeDtypeStruct((N_DEV,) + s.shape, s.dtype) for s in shards],
        scratch_shapes=[pltpu.VMEM((2, 2, tk, tq), F32)] + _Exchange.scratch(n_sh),
        compiler_params=_cp(("arbitrary", "arbitrary")),
    )(q_aug_t, k_aug, v_aug_t, *shards)
    return res[0], res[1], res[2], res[3:]


def _prev_halo(tm, halo):
    return lambda i: (jnp.maximum(i * (tm // halo) - 1, 0), 0)


def _next_halo(tm, halo, t):
    return lambda i: (jnp.minimum((i + 1) * (tm // halo), t // halo - 1), 0)


def _mlp_tile(hh, g_ref, wu_ref, wd_ref, n_ref, a_ref, z_ref):
    n_blk, _, fb = wu_ref.shape
    n = (hh * _rstd(hh) * g_ref[...]).astype(BF16)
    n_ref[...] = n
    acc = hh
    for k in range(n_blk):
        a = _dot(n, wu_ref[k])
        zz = jnp.square(jnp.maximum(a, 0.0)).astype(BF16)
        a_ref[:, k * fb:(k + 1) * fb] = a.astype(BF16)
        z_ref[:, k * fb:(k + 1) * fb] = zz
        acc = acc + _dot(zz, wd_ref[k * fb:(k + 1) * fb, :])
    return acc


def _outproj(att, cv, x, wout, *, tm=512):
    t, d = x.shape
    tm = min(tm, t)

    def body(a_ref, c_ref, x_ref, w_ref, h_ref):
        h_ref[...] = x_ref[...] + _dot(a_ref[...], w_ref[0:ATTN_W, :]) + _dot(c_ref[...], w_ref[ATTN_W:, :])

    return pl.pallas_call(
        body, name="outproj", grid=(t // tm,),
        in_specs=[pl.BlockSpec((tm, ATTN_W), lambda i: (i, 0)), pl.BlockSpec((tm, CONV_CH), lambda i: (i, 0)),
                  pl.BlockSpec((tm, d), lambda i: (i, 0)), _const_spec(wout.shape)],
        out_specs=pl.BlockSpec((tm, d), lambda i: (i, 0)),
        out_shape=jax.ShapeDtypeStruct((t, d), F32),
        compiler_params=_cp(("parallel",)),
    )(att, cv, x, wout)


def _mlp_fwd(h, g, wup, wdown, *, name, tm=512):
    t, d = h.shape
    n_blk, _, fb = wup.shape
    f = n_blk * fb
    tm = min(tm, t)

    def body(h_ref, g_ref, wu_ref, wd_ref, ho_ref, n_ref, a_ref, z_ref):
        ho_ref[...] = _mlp_tile(h_ref[...], g_ref, wu_ref, wd_ref, n_ref, a_ref, z_ref)

    row = lambda n_: pl.BlockSpec((tm, n_), lambda i: (i, 0))
    return pl.pallas_call(
        body, name=name, grid=(t // tm,),
        in_specs=[row(d), _const_spec((1, d)), _const_spec(wup.shape), _const_spec(wdown.shape)],
        out_specs=[row(d), row(d), row(f), row(f)],
        out_shape=[jax.ShapeDtypeStruct((t, d), F32), jax.ShapeDtypeStruct((t, d), BF16),
                   jax.ShapeDtypeStruct((t, f), BF16), jax.ShapeDtypeStruct((t, f), BF16)],
        compiler_params=_cp(("parallel",)),
    )(h, g, wup, wdown)


def _mlp_fwd_loss(h, g, wup, wdown, g_out, target, *, name, tm=512):
    t, d = h.shape
    n_blk, _, fb = wup.shape
    f = n_blk * fb
    tm = min(tm, t)
    nsteps = t // tm

    def body(h_ref, g_ref, wu_ref, wd_ref, go_ref, y_ref, loss_ref, dh_ref, dhb_ref, dg_ref, n_ref, a_ref, z_ref, lacc):
        i = pl.program_id(0)

        @pl.when(i == 0)
        def _():
            lacc[...] = jnp.zeros_like(lacc)
            dg_ref[...] = jnp.zeros_like(dg_ref)
        hv = _mlp_tile(h_ref[...], g_ref, wu_ref, wd_ref, n_ref, a_ref, z_ref)
        gv = go_ref[...]
        r = _rstd(hv)
        xhat = hv * r
        err = xhat * gv - y_ref[...]
        lacc[...] += _rows8(err * err)
        dout = err * (1.0 / d)
        dy = dout * gv
        dg_ref[...] += _rows8(dout * xhat)
        dh = r * (dy - xhat * jnp.mean(dy * xhat, axis=-1, keepdims=True))
        dh_ref[...] = dh
        dhb_ref[...] = dh.astype(BF16)

        @pl.when(i == nsteps - 1)
        def _():
            loss_ref[...] = jnp.full(loss_ref.shape, (0.5 / d) * jnp.sum(lacc[...]), F32)

    row = lambda n_: pl.BlockSpec((tm, n_), lambda i: (i, 0))
    return pl.pallas_call(
        body, name=name, grid=(nsteps,),
        in_specs=[row(d), _const_spec((1, d)), _const_spec(wup.shape), _const_spec(wdown.shape), _const_spec((1, d)),
                  row(d)],
        out_specs=[pl.BlockSpec((8, 128), lambda i: (0, 0)), row(d), row(d), pl.BlockSpec((8, d), lambda i: (0, 0)),
                   row(d), row(f), row(f)],
        out_shape=[jax.ShapeDtypeStruct((8, 128), F32), jax.ShapeDtypeStruct((t, d), F32),
                   jax.ShapeDtypeStruct((t, d), BF16),
                   jax.ShapeDtypeStruct((8, d), F32), jax.ShapeDtypeStruct((t, d), BF16),
                   jax.ShapeDtypeStruct((t, f), BF16), jax.ShapeDtypeStruct((t, f), BF16)],
        scratch_shapes=[pltpu.VMEM((8, d), F32)],
        compiler_params=_cp(("arbitrary",)),
    )(h, g, wup, wdown, g_out, target)


def _pool_inv_count(i, tm):
    tglob = (i * tm + lax.broadcasted_iota(jnp.int32, (tm, 1), 0) + 1).astype(F32)
    return [1.0 / jnp.minimum(tglob, float(w)) for w in POOL_WINDOWS]


def _pool_fwd(h, g, poolw, scale, *, tm=512):
    t, d = h.shape
    tm = min(tm, t)
    cg = d // len(POOL_WINDOWS)

    def body(h_ref, hh_ref, g_ref, w_ref, s_ref, ho_ref, p_ref, ext):
        i = pl.program_id(0)
        hv = h_ref[...]
        halo = hh_ref[...]
        n = hv * _rstd(hv) * g_ref[...]
        ext[0:POOL_HALO, :] = jnp.where(i == 0, 0.0, halo * _rstd(halo) * g_ref[...])
        ext[POOL_HALO:POOL_HALO + tm, :] = n
        inv = _pool_inv_count(i, tm)
        for gi, w in enumerate(POOL_WINDOWS):
            cs = slice(gi * cg, (gi + 1) * cg)
            s = ext[POOL_HALO:POOL_HALO + tm, cs]
            for j in range(1, w):
                s = s + ext[POOL_HALO - j:POOL_HALO - j + tm, cs]
            pooled = (s * inv[gi] - n[:, cs]).astype(BF16)
            p_ref[:, cs] = pooled
            ho_ref[:, cs] = hv[:, cs] + _dot(pooled, w_ref[gi]) * s_ref[:, cs]

    row = lambda: pl.BlockSpec((tm, d), lambda i: (i, 0))
    return pl.pallas_call(
        body, name="pool_fwd", grid=(t // tm,),
        in_specs=[row(), pl.BlockSpec((POOL_HALO, d), _prev_halo(tm, POOL_HALO)), _const_spec((1, d)),
                  _const_spec(poolw.shape), _const_spec((1, d))],
        out_specs=[row(), row()],
        out_shape=[jax.ShapeDtypeStruct((t, d), F32), jax.ShapeDtypeStruct((t, d), BF16)],
        scratch_shapes=[pltpu.VMEM((POOL_HALO + tm, d), F32)],
        compiler_params=_cp(("parallel",)),
    )(h, h, g, poolw, scale)


def _mm_tn(a, b, *, name, ta, tb, tt, blocked_out=False, out_dtype=F32):
    t, ka = a.shape
    n = b.shape[1]
    ta, tb, tt = min(ta, ka), min(tb, n), min(tt, t)
    nt = t // tt

    def body(a_ref, b_ref, o_ref, acc):
        @pl.when(pl.program_id(2) == 0)
        def _():
            acc[...] = jnp.zeros_like(acc)
        acc[...] += _dot_tn(a_ref[...].astype(BF16), b_ref[...].astype(BF16))

        @pl.when(pl.program_id(2) == nt - 1)
        def _():
            o_ref[...] = acc[...].astype(out_dtype)

    if blocked_out:
        assert ta == ka
        out_shape = jax.ShapeDtypeStruct((n // tb, ka, tb), out_dtype)
        out_spec = pl.BlockSpec((None, ta, tb), lambda i, j, k: (j, i, 0))
    else:
        out_shape = jax.ShapeDtypeStruct((ka, n), out_dtype)
        out_spec = pl.BlockSpec((ta, tb), lambda i, j, k: (i, j))
    return pl.pallas_call(
        body, name=name, grid=(ka // ta, n // tb, nt),
        in_specs=[pl.BlockSpec((tt, ta), lambda i, j, k: (k, i)), pl.BlockSpec((tt, tb), lambda i, j, k: (k, j))],
        out_specs=out_spec, out_shape=out_shape, scratch_shapes=[pltpu.VMEM((ta, tb), F32)],
        compiler_params=_cp(("parallel", "parallel", "arbitrary")),
    )(a, b)


def _mm_tn_cat(a_list, b_list, *, name, tt, out_dtype=BF16):
    t = a_list[0].shape[0]
    ta, tb = a_list[0].shape[1], b_list[0].shape[1]
    na, nb = len(a_list), len(b_list)
    tt = min(tt, t)
    nt = t // tt

    def body(*refs):
        a_refs, b_refs, o_ref, acc = refs[:na], refs[na:na + nb], refs[na + nb], refs[na + nb + 1]
        i, j, k = pl.program_id(0), pl.program_id(1), pl.program_id(2)

        @pl.when(k == 0)
        def _():
            acc[...] = jnp.zeros_like(acc)
        for ia in range(na):
            for ib in range(nb):
                @pl.when((i == ia) & (j == ib))
                def _(ia=ia, ib=ib):
                    acc[...] += _dot_tn(a_refs[ia][...].astype(BF16), b_refs[ib][...].astype(BF16))

        @pl.when(k == nt - 1)
        def _():
            o_ref[...] = acc[...].astype(out_dtype)

    def held(m, axis):
        def index(i, j, k):
            cur = (i, j)[axis]
            return (jnp.where(cur == m, k, jnp.where(cur < m, 0, nt - 1)), 0)
        return index

    return pl.pallas_call(
        body, name=name, grid=(na, nb, nt),
        in_specs=[pl.BlockSpec((tt, ta), held(m, 0)) for m in range(na)]
        + [pl.BlockSpec((tt, tb), held(m, 1)) for m in range(nb)],
        out_specs=pl.BlockSpec((ta, tb), lambda i, j, k: (i, j)),
        out_shape=jax.ShapeDtypeStruct((na * ta, nb * tb), out_dtype), scratch_shapes=[pltpu.VMEM((ta, tb), F32)],
        compiler_params=_cp(("arbitrary", "arbitrary", "arbitrary")),
    )(*a_list, *b_list)


def _mlp_bwd(dho, h, a, g, wup, wdown, *, name, tm=512):
    t, d = h.shape
    n_blk, _, fb = wup.shape
    f = n_blk * fb
    tm = min(tm, t)

    def body(do_ref, h_ref, a_ref, g_ref, wu_ref, wd_ref, dh_ref, da_ref, dg_ref):
        @pl.when(pl.program_id(0) == 0)
        def _():
            dg_ref[...] = jnp.zeros_like(dg_ref)
        dho_v = do_ref[...]
        dob = dho_v.astype(BF16)
        dn = jnp.zeros((tm, d), F32)
        for k in range(n_blk):
            dz = _dot_nt(dob, wd_ref[k * fb:(k + 1) * fb, :])
            da = (dz * (2.0 * jnp.maximum(a_ref[:, k * fb:(k + 1) * fb].astype(F32), 0.0))).astype(BF16)
            da_ref[:, k * fb:(k + 1) * fb] = da
            dn = dn + _dot_nt(da, wu_ref[k])
        dh, dg = _norm_bwd(dn, h_ref[...], g_ref[...])
        dh_ref[...] = dho_v + dh
        dg_ref[...] += dg

    row = lambda n_: pl.BlockSpec((tm, n_), lambda i: (i, 0))
    return pl.pallas_call(
        body, name=name, grid=(t // tm,),
        in_specs=[row(d), row(d), row(f), _const_spec((1, d)), _const_spec(wup.shape), _const_spec(wdown.shape)],
        out_specs=[row(d), row(f), pl.BlockSpec((8, d), lambda i: (0, 0))],
        out_shape=[jax.ShapeDtypeStruct((t, d), F32), jax.ShapeDtypeStruct((t, f), BF16),
                   jax.ShapeDtypeStruct((8, d), F32)],
        compiler_params=_cp(("arbitrary",)),
    )(dho, h, a, g, wup, wdown)


def _pool_bwd(dho, h, pooled, g, poolw, scale, *, tm=512):
    t, d = h.shape
    tm = min(tm, t)
    ng = len(POOL_WINDOWS)
    cg = d // ng
    nsteps = t // tm

    def body(do_ref, dn_ref, h_ref, p_ref, g_ref, w_ref, s_ref, dh_ref, dhb_ref, dw_ref, ds_ref, dg_ref, ext):
        i = pl.program_id(0)

        @pl.when(i == 0)
        def _():
            dw_ref[...] = jnp.zeros_like(dw_ref)
            ds_ref[...] = jnp.zeros_like(ds_ref)
            dg_ref[...] = jnp.zeros_like(dg_ref)
        dho_v = do_ref[...]
        sv = s_ref[...]
        dyp = (dho_v * sv).astype(BF16)
        dyp_halo = (dn_ref[...] * sv).astype(BF16)
        inv = _pool_inv_count(i, tm)
        tnext = ((i + 1) * tm + lax.broadcasted_iota(jnp.int32, (POOL_HALO, 1), 0) + 1).astype(F32)
        last = i == nsteps - 1
        ypre_parts, dpooled_parts = [], []
        for gi, w in enumerate(POOL_WINDOWS):
            cs = slice(gi * cg, (gi + 1) * cg)
            pg = p_ref[:, cs]
            ypre_parts.append(_dot(pg, w_ref[gi]))
            dw_ref[gi] += _dot_tn(pg, dyp[:, cs])
            dpool = _dot_nt(dyp[:, cs], w_ref[gi])
            dpooled_parts.append(dpool)
            ext[0:tm, cs] = dpool * inv[gi]
            dpool_halo = _dot_nt(dyp_halo[:, cs], w_ref[gi]) * (1.0 / jnp.minimum(tnext, float(w)))
            ext[tm:tm + POOL_HALO, cs] = jnp.where(last, 0.0, dpool_halo)
        ds_ref[...] += _rows8(dho_v * jnp.concatenate(ypre_parts, axis=1))
        dn_parts = []
        for gi, w in enumerate(POOL_WINDOWS):
            cs = slice(gi * cg, (gi + 1) * cg)
            s = ext[0:tm, cs]
            for j in range(1, w):
                s = s + ext[j:j + tm, cs]
            dn_parts.append(s - dpooled_parts[gi])
        dh, dg = _norm_bwd(jnp.concatenate(dn_parts, axis=1), h_ref[...], g_ref[...])
        dh = dho_v + dh
        dh_ref[...] = dh
        dhb_ref[...] = dh.astype(BF16)
        dg_ref[...] += dg

    row = lambda: pl.BlockSpec((tm, d), lambda i: (i, 0))
    acc8 = lambda: pl.BlockSpec((8, d), lambda i: (0, 0))
    return pl.pallas_call(
        body, name="pool_bwd", grid=(nsteps,),
        in_specs=[row(), pl.BlockSpec((POOL_HALO, d), _next_halo(tm, POOL_HALO, t)), row(), row(),
                  _const_spec((1, d)), _const_spec(poolw.shape), _const_spec((1, d))],
        out_specs=[row(), row(), pl.BlockSpec((ng, cg, cg), lambda i: (0, 0, 0)), acc8(), acc8()],
        out_shape=[jax.ShapeDtypeStruct((t, d), F32), jax.ShapeDtypeStruct((t, d), BF16),
                   jax.ShapeDtypeStruct((ng, cg, cg), F32),
                   jax.ShapeDtypeStruct((8, d), F32), jax.ShapeDtypeStruct((8, d), F32)],
        scratch_shapes=[pltpu.VMEM((tm + POOL_HALO, d), F32)],
        compiler_params=_cp(("arbitrary",)),
    )(dho, dho, h, pooled, g, poolw, scale)


def _outproj_conv_bwd(dh, o, wout, bcx, conv_w, *, tm=512):
    t, d = dh.shape
    tm = min(tm, t)
    ch = CONV_CH
    nsteps = t // tm

    def body(dh_ref, o_ref, w_ref, b_ref, c_ref, x_ref, hc_ref, hx_ref, cw_ref,
             da_ref, dat_ref, db_ref, dw_ref, ext_u, ext_d):
        s = pl.program_id(0)

        @pl.when(s == 0)
        def _():
            dw_ref[...] = jnp.zeros_like(dw_ref)
            ext_d[0:CONV_HALO, :] = jnp.zeros((CONV_HALO, ch), F32)
        dhb = dh_ref[...].astype(BF16)
        for p in range(ATTN_W // PAIR):
            datt = _dot_nt(dhb, w_ref[p * PAIR:(p + 1) * PAIR, :])
            prod = datt * o_ref[:, p * PAIR:(p + 1) * PAIR].astype(F32)
            for hh in range(2):
                lane, head, aux = _head_lanes(hh)
                delta = jnp.sum(jnp.where(head, prod, 0.0), axis=1, keepdims=True)
                aug = _put_pieces(lane, aux + AUX_BIAS, -delta, jnp.where(head, datt, 0.0))
                da_ref[2 * p + hh] = aug.astype(BF16)
                dat_ref[2 * p + hh] = aug.astype(BF16).T
        dcv = _dot_nt(dhb, w_ref[ATTN_W:, :])
        b, c, x = b_ref[...].astype(F32), c_ref[...].astype(F32), x_ref[...].astype(F32)
        ext_u[0:CONV_HALO, :] = jnp.where(s == nsteps - 1, 0.0, hc_ref[...].astype(F32) * hx_ref[...].astype(F32))
        ext_u[CONV_HALO:CONV_HALO + tm, :] = c * x
        dconv = dcv * b
        ext_d[tm:tm + CONV_HALO, :] = ext_d[0:CONV_HALO, :]
        ext_d[0:tm, :] = dconv
        u = [ext_u[CONV_HALO - 2 + k:CONV_HALO - 2 + k + tm, :] for k in range(3)]
        conv = cw_ref[0:1, :] * u[0] + cw_ref[1:2, :] * u[1] + cw_ref[2:3, :] * u[2]
        du = (cw_ref[2:3, :] * dconv + cw_ref[1:2, :] * ext_d[1:1 + tm, :] + cw_ref[0:1, :] * ext_d[2:2 + tm, :])
        db_ref[:, 0:ch] = (dcv * conv).astype(BF16)
        db_ref[:, ch:2 * ch] = (du * x).astype(BF16)
        db_ref[:, 2 * ch:3 * ch] = (du * c).astype(BF16)
        for k in range(3):
            dw_ref[k] += _rows8(dconv * u[k])

    rev = lambda s: nsteps - 1 - s
    row = lambda n_: pl.BlockSpec((tm, n_), lambda s: (rev(s), 0))
    col = lambda k: pl.BlockSpec((tm, ch), lambda s: (rev(s), k))
    prev = lambda k: pl.BlockSpec((CONV_HALO, ch), lambda s: (_prev_halo(tm, CONV_HALO)(rev(s))[0], k))
    return pl.pallas_call(
        body, name="outproj_conv_bwd", grid=(nsteps,),
        in_specs=[row(d), row(ATTN_W), _const_spec(wout.shape), col(0), col(1), col(2), prev(1), prev(2),
                  _const_spec((8, ch))],
        out_specs=[pl.BlockSpec((N_HEADS, tm, PAIR), lambda s: (0, rev(s), 0)),
                   pl.BlockSpec((N_HEADS, PAIR, tm), lambda s: (0, 0, rev(s))),
                   row(3 * ch), pl.BlockSpec((3, 8, ch), lambda s: (0, 0, 0))],
        out_shape=[jax.ShapeDtypeStruct((N_HEADS, t, PAIR), BF16), jax.ShapeDtypeStruct((N_HEADS, PAIR, t), BF16),
                   jax.ShapeDtypeStruct((t, 3 * ch), BF16), jax.ShapeDtypeStruct((3, 8, ch), F32)],
        scratch_shapes=[pltpu.VMEM((CONV_HALO + tm, ch), F32), pltpu.VMEM((tm + CONV_HALO, ch), F32)],
        compiler_params=_cp(("arbitrary",)),
    )(dh, o, wout, bcx, bcx, bcx, bcx, bcx, conv_w)


def _attn_bwd(q_bwd, do_aug, q_bwd_t, do_aug_t, k_aug, v_aug_t, gblocks, *, tq=1024):
    t = q_bwd.shape[1]
    tq = min(tq, t)
    tk = tq // 2
    nq, nk = t // tq, t // tk
    n_pairs = ATTN_W // PAIR
    n_g = len(gblocks)

    def body(q_ref, do_ref, qt_ref, dot_ref, k_ref, vt_ref, *rest):
        dq_ref, dqx_ref, dk_ref, dkx_ref, dv_ref = rest[n_g:n_g + 5]
        dq_scr = rest[2 * n_g + 5]
        scatter = _Exchange(rest[:n_g], rest[n_g + 5:2 * n_g + 5], *rest[2 * n_g + 6:], gather=False)
        j = pl.program_id(1)

        @pl.when((pl.program_id(0) == 0) & (j == 0))
        def _():
            scatter.start()

        @pl.when(j == 0)
        def _():
            dq_scr[...] = jnp.zeros_like(dq_scr)
        k = [k_ref[0], k_ref[1]]
        vt = [vt_ref[0], vt_ref[1]]

        def step(i, carry, diag, rows=tq, row0=0):
            qs = pl.multiple_of(i * tq + row0, tk)
            if diag:
                row = lax.broadcasted_iota(jnp.int32, (rows, tk), 0)
                col = lax.broadcasted_iota(jnp.int32, (rows, tk), 1)
            out = []
            for hh in range(2):
                dk_a, dv_a = carry[hh]
                q = q_ref[hh, pl.ds(qs, rows), :]
                dov = do_ref[hh, pl.ds(qs, rows), :]
                p = jnp.exp2(_dot_nt(q, k[hh]))
                if diag:
                    p = jnp.where(col + (j * tk - i * tq - row0) <= row, p, 0.0)
                ds = (p * _dot(dov, vt[hh])).astype(BF16)
                dv_a = dv_a + _dot(dot_ref[hh, :, pl.ds(qs, rows)], p.astype(BF16))
                dk_a = dk_a + _dot(qt_ref[hh, :, pl.ds(qs, rows)], ds)
                dq_scr[hh, pl.ds(qs, rows), :] += _dot(ds, k[hh])
                out.append((dk_a, dv_a))
            return tuple(out)

        zero = (jnp.zeros((PAIR, tk), F32), jnp.zeros((PAIR, tk), F32))
        carry = lax.cond(j % 2 == 0, lambda c: step(j // 2, c, True),
                         lambda c: step(j // 2, c, True, rows=tk, row0=tk), (zero, zero))
        full0 = j // 2 + 1
        odd = (nq - full0) % 2
        carry = lax.cond(odd == 1, lambda c: step(full0, c, False), lambda c: c, carry)
        (dk0, dv0), (dk1, dv1) = lax.fori_loop(
            0, (nq - full0) // 2, lambda ii, c: step(full0 + odd + 2 * ii, c, False, rows=2 * tq), carry)
        first_t = lax.broadcasted_iota(jnp.int32, (PAIR, 1), 0) < HEAD_DIM
        first = lax.broadcasted_iota(jnp.int32, (1, PAIR), 1) < HEAD_DIM
        dk_ref[...] = (jnp.where(first_t, dk0, dk1) * (1.0 / LOG2E)).astype(BF16).T
        dkx_ref[...] = jnp.where(first_t, dk1, dk0).T
        dv_ref[...] = jnp.where(first_t, dv0, dv1).astype(BF16).T

        @pl.when(j == nk - 1)
        def _():
            dq_ref[...] = (jnp.where(first, dq_scr[0], dq_scr[1]) * Q_SCALE).astype(BF16)
            dqx_ref[...] = jnp.where(first, dq_scr[1], dq_scr[0])

        @pl.when((pl.program_id(0) == n_pairs - 1) & (j == nk - 1))
        def _():
            scatter.wait()

    resident = lambda: pl.BlockSpec((2, t, PAIR), lambda p, j: (p, 0, 0), pipeline_mode=pl.Buffered(1))
    resident_t = lambda: pl.BlockSpec((2, PAIR, t), lambda p, j: (p, 0, 0), pipeline_mode=pl.Buffered(1))
    kv_in = lambda: pl.BlockSpec((2, tk, PAIR), lambda p, j: (p, j, 0))
    whole = lambda: pl.BlockSpec((t, PAIR), lambda p, j: (0, p))
    tile = lambda: pl.BlockSpec((tk, PAIR), lambda p, j: (j, p))
    b16 = jax.ShapeDtypeStruct((t, ATTN_W), BF16)
    f32 = jax.ShapeDtypeStruct((t, ATTN_W), F32)
    res = pl.pallas_call(
        body, name="attn_bwd", grid=(n_pairs, nk),
        in_specs=[resident(), resident(), resident_t(), resident_t(), kv_in(),
                  pl.BlockSpec((2, PAIR, tk), lambda p, j: (p, 0, j))] + [HBM_SPEC] * n_g,
        out_specs=[whole(), whole(), tile(), tile(), tile()] + [HBM_SPEC] * n_g,
        out_shape=[b16, f32, b16, f32, b16] + [jax.ShapeDtypeStruct(g.shape, g.dtype) for g in gblocks],
        scratch_shapes=[pltpu.VMEM((2, t, PAIR), F32)] + _Exchange.scratch(n_g),
        compiler_params=_cp(("arbitrary", "arbitrary")),
    )(q_bwd, do_aug, q_bwd_t, do_aug_t, k_aug, v_aug_t, *gblocks)
    return res[:5], res[5:]


def _fgate_bwd(dqx, dkx, sgate, *, tm=256):
    t = sgate.shape[0]
    tm = min(tm, t)
    nsteps = t // tm

    def body(dq_ref, dk_ref, sg_ref, df_ref, dbf_ref, carry):
        @pl.when(pl.program_id(0) == 0)
        def _():
            carry[...] = jnp.zeros_like(carry)
            dbf_ref[...] = jnp.zeros_like(dbf_ref)
        lane = lax.broadcasted_iota(jnp.int32, (ATTN_W, F_PAD), 0)
        head = lax.broadcasted_iota(jnp.int32, (ATTN_W, F_PAD), 1)
        aux = (head // 2) * PAIR + HEAD_DIM * (1 - head % 2)
        valid = head < N_HEADS
        pick_r = (valid & (lane == aux + AUX_ROWSUM)).astype(F32)
        pick_c = (valid & (lane == aux + AUX_BIAS)).astype(F32)
        hp = lax.Precision.HIGHEST
        dcum = (jnp.dot(dq_ref[...], pick_r, preferred_element_type=F32, precision=lax.Precision.HIGH)
                + jnp.dot(dk_ref[...], pick_c, preferred_element_type=F32, precision=lax.Precision.HIGH))
        r = lax.broadcasted_iota(jnp.int32, (tm, tm), 0)
        c = lax.broadcasted_iota(jnp.int32, (tm, tm), 1)
        tri = (c >= r).astype(F32)
        rc = jnp.dot(tri, dcum, preferred_element_type=F32, precision=hp) + carry[...]
        carry[...] = rc[0:1, :]
        df = rc * sg_ref[...]
        df_ref[...] = df.astype(BF16)
        dbf_ref[...] += _rows8(df)

    rev = lambda i: nsteps - 1 - i
    return pl.pallas_call(
        body, name="fgate_bwd", grid=(nsteps,),
        in_specs=[pl.BlockSpec((tm, ATTN_W), lambda i: (rev(i), 0)), pl.BlockSpec((tm, ATTN_W), lambda i: (rev(i), 0)),
                  pl.BlockSpec((tm, F_PAD), lambda i: (rev(i), 0))],
        out_specs=[pl.BlockSpec((tm, F_PAD), lambda i: (rev(i), 0)), pl.BlockSpec((8, F_PAD), lambda i: (0, 0))],
        out_shape=[jax.ShapeDtypeStruct((t, F_PAD), BF16), jax.ShapeDtypeStruct((8, F_PAD), F32)],
        scratch_shapes=[pltpu.VMEM((1, F_PAD), F32)],
        compiler_params=_cp(("arbitrary",)),
    )(dqx, dkx, sgate)


def _inproj_bwd(dq, dk, dv, df, dbcx, dh, x, g, win_pt, gblock, *, tm=512):
    t, d = x.shape
    tm = min(tm, t)
    nsteps = t // tm
    n_qkv = 3 * ATTN_W

    def body(dq_ref, dk_ref, dv_ref, df_ref, db_ref, dh_ref, x_ref, g_ref, w_ref, gb_ref, gx_ref, dg_ref, land_ref,
             *sems):
        scatter = _Exchange([gb_ref], [land_ref], *sems, gather=False)

        @pl.when(pl.program_id(0) == 0)
        def _():
            scatter.start()
            dg_ref[...] = jnp.zeros_like(dg_ref)
        dn = _dot(df_ref[...], w_ref[n_qkv:n_qkv + F_PAD, :])
        for k, r in enumerate((dq_ref, dk_ref, dv_ref)):
            dn = dn + _dot(r[...], w_ref[k * ATTN_W:(k + 1) * ATTN_W, :])
        for k in range(3):
            c0 = n_qkv + F_PAD + k * CONV_CH
            dn = dn + _dot(db_ref[:, k * CONV_CH:(k + 1) * CONV_CH], w_ref[c0:c0 + CONV_CH, :])
        dx, dg = _norm_bwd(dn, x_ref[...], g_ref[...])
        gx_ref[...] = dh_ref[...] + dx
        dg_ref[...] += dg

        @pl.when(pl.program_id(0) == nsteps - 1)
        def _():
            scatter.wait()

    row = lambda n_: pl.BlockSpec((tm, n_), lambda i: (i, 0))
    return pl.pallas_call(
        body, name="inproj_bwd", grid=(nsteps,),
        in_specs=[row(ATTN_W), row(ATTN_W), row(ATTN_W), row(F_PAD), row(3 * CONV_CH), row(d), row(d),
                  _const_spec((1, d)), _const_spec(win_pt.shape), HBM_SPEC],
        out_specs=[row(d), pl.BlockSpec((8, d), lambda i: (0, 0)), HBM_SPEC],
        out_shape=[jax.ShapeDtypeStruct((t, d), F32), jax.ShapeDtypeStruct((8, d), F32),
                   jax.ShapeDtypeStruct(gblock.shape, gblock.dtype)],
        scratch_shapes=_Exchange.scratch(1),
        compiler_params=_cp(("arbitrary",)),
    )(dq, dk, dv, df, dbcx, dh, x, g, win_pt, gblock)


LATE = ("w_out_0", "w_up_0", "w_down_0", "pool_w_1", "w_up_1", "w_down_1")


def _local_step(x, target, gains, b_f, conv_w, pool_scale, win_pt, shards):
    d = x.shape[1]
    n0, qkv, flog, bcx, cv = _norm_inproj(x, gains["mix0"], win_pt, conv_w)
    q_aug_t, k_aug, v_aug_t, sgate = _fgate_prep(flog, b_f, qkv)
    att, q_bwd, q_bwd_t, gathered = _attn_fwd(q_aug_t, k_aug, v_aug_t, [shards[n] for n in LATE])
    g = dict(zip(LATE, gathered))
    wout = g["w_out_0"].reshape(d, d)
    wup0, wup1 = g["w_up_0"], g["w_up_1"]
    wdown0, wdown1 = g["w_down_0"].reshape(-1, d), g["w_down_1"].reshape(-1, d)
    n_grp = len(POOL_WINDOWS)
    cg = d // n_grp
    poolw = g["pool_w_1"].reshape(N_DEV, n_grp, cg // N_DEV, cg).transpose(1, 0, 2, 3).reshape(n_grp, cg, cg)
    h1 = _outproj(att, cv, x, wout)
    h2, n1, a0, z0 = _mlp_fwd(h1, gains["ffn0"], wup0, wdown0, name="mlp_fwd0")
    h3, pooled = _pool_fwd(h2, gains["mix1"], poolw, pool_scale)
    loss, dh4, dh4_b, dg_final, n3, a1, z1 = _mlp_fwd_loss(h3, gains["ffn1"], wup1, wdown1, gains["final"], target,
                                                           name="mlp_fwd1")
    f = a1.shape[1]
    fb = f // N_DEV
    dh3, da1, dg_ffn1 = _mlp_bwd(dh4, h3, a1, gains["ffn1"], wup1, wdown1, name="mlp_bwd1")
    dwdown1 = _mm_tn(z1, dh4_b, name="dwdown1", ta=1024, tb=1024, tt=4096, out_dtype=BF16)
    dwup1 = _mm_tn(n3, da1, name="dwup1", ta=d, tb=fb, tt=4096, blocked_out=True, out_dtype=BF16)
    dh2, dh2_b, dpoolw, dscale, dg_mix1 = _pool_bwd(dh3, h2, pooled, gains["mix1"], poolw, pool_scale)
    dh1, da0, dg_ffn0 = _mlp_bwd(dh2, h1, a0, gains["ffn0"], wup0, wdown0, name="mlp_bwd0")
    dwdown0 = _mm_tn(z0, dh2_b, name="dwdown0", ta=1024, tb=1024, tt=4096, out_dtype=BF16)
    dwup0 = _mm_tn(n1, da0, name="dwup0", ta=d, tb=fb, tt=4096, blocked_out=True, out_dtype=BF16)
    do_aug, do_aug_t, dbcx, dconvw = _outproj_conv_bwd(dh1, att, wout, bcx, conv_w)
    dwout = _mm_tn_cat([att, cv], [dh1], name="dwout", tt=2048)
    gblocks = {
        "w_out_0": dwout.reshape(N_DEV, d // N_DEV, d), "w_up_0": dwup0, "w_up_1": dwup1,
        "w_down_0": dwdown0.reshape(N_DEV, -1, d), "w_down_1": dwdown1.reshape(N_DEV, -1, d),
        "pool_w_1": dpoolw.astype(BF16).reshape(n_grp, N_DEV, cg // N_DEV, cg).transpose(1, 0, 2, 3).reshape(
            N_DEV, n_grp * (cg // N_DEV), cg),
    }
    (dq, dqx, dk, dkx, dv), landed = _attn_bwd(q_bwd, do_aug, q_bwd_t, do_aug_t, k_aug, v_aug_t,
                                               [gblocks[n] for n in LATE])
    df, dbf = _fgate_bwd(dqx, dkx, sgate)
    dwin_t = jnp.concatenate(
        [_mm_tn_cat([dq, dk, dv], [n0], name="dwin_qkv", tt=2048),
         _mm_tn(df, n0, name="dwin_f", ta=F_PAD, tb=d, tt=2048, out_dtype=BF16)[:N_HEADS],
         _mm_tn(dbcx, n0, name="dwin_bcx", ta=512, tb=d, tt=4096, out_dtype=BF16)], axis=0)
    dwin_blocks = dwin_t.reshape(N_DEV, dwin_t.shape[0] // N_DEV, d)
    grad_x, dg_mix0, landed_win = _inproj_bwd(dq, dk, dv, df, dbcx, dh1, x, gains["mix0"], win_pt, dwin_blocks)
    small = dict(mix0=dg_mix0, ffn0=dg_ffn0, mix1=dg_mix1, pool_scale=dscale, ffn1=dg_ffn1, final=dg_final,
                 b_f=dbf, conv_w=dconvw)
    return loss, grad_x, dict(zip(LATE + ("w_in_0",), tuple(landed) + (landed_win,))), small


def _all_gather(shards):
    n = len(shards)

    def body(*refs):
        gather = _TwoLevelGather(refs[:n], refs[n:2 * n], *refs[2 * n:])
        gather.start()
        gather.forward()
        gather.wait()

    return pl.pallas_call(
        body, name="all_gather",
        in_specs=[HBM_SPEC] * n, out_specs=[HBM_SPEC] * n,
        out_shape=[jax.ShapeDtypeStruct((N_DEV,) + s.shape, s.dtype) for s in shards],
        scratch_shapes=[pltpu.SemaphoreType.DMA((7 * n,)), pltpu.SemaphoreType.DMA((7 * n,)),
                        pltpu.SemaphoreType.DMA((n,))],
    )(*shards)


SMALL_ROWS = 16


def _small_allreduce(parts):
    n, _, w = parts.shape
    assert n <= SMALL_ROWS

    def body(p_ref, o_ref, gath, send_sems, recv_sems):
        x, y, c = lax.axis_index("x"), lax.axis_index("y"), lax.axis_index("c")
        my = _slot(x, y, c)
        rows = [jnp.sum(p_ref[i], axis=0, keepdims=True) for i in range(n)]
        rows.append(jnp.zeros((SMALL_ROWS - n, w), F32))
        gath[my] = jnp.concatenate(rows, axis=0)
        copies = []
        for k in range(1, N_DEV):
            px, py, pc = x ^ (k >> 2), y ^ ((k >> 1) & 1), c ^ (k & 1)
            cp = pltpu.make_async_remote_copy(
                src_ref=gath.at[my], dst_ref=gath.at[my], send_sem=send_sems.at[k - 1], recv_sem=recv_sems.at[k - 1],
                device_id=(px, py, pc), device_id_type=MESH)
            cp.start()
            copies.append(cp)
        for cp in copies:
            cp.wait()
        acc = gath[0]
        for d in range(1, N_DEV):
            acc = acc + gath[d]
        o_ref[...] = acc

    return pl.pallas_call(
        body, name="small_allreduce",
        in_specs=[VMEM_SPEC], out_specs=VMEM_SPEC,
        out_shape=jax.ShapeDtypeStruct((SMALL_ROWS, w), F32),
        scratch_shapes=[pltpu.VMEM((N_DEV, SMALL_ROWS, w), F32), pltpu.SemaphoreType.DMA((N_DEV - 1,)),
                        pltpu.SemaphoreType.DMA((N_DEV - 1,))],
    )(parts)


def _adamw(g, w, m, v, *, name, tm=256):
    r, c = g.shape
    tm = tm if r % tm == 0 else r
    bc1 = 1.0 - ADAM_B1 ** ADAM_STEP
    bc2 = 1.0 - ADAM_B2 ** ADAM_STEP

    def body(g_ref, w_ref, m_ref, v_ref, d_ref, nm_ref, nv_ref):
        gv = g_ref[...]
        nm = ADAM_B1 * m_ref[...] + (1.0 - ADAM_B1) * gv
        nv = ADAM_B2 * v_ref[...] + (1.0 - ADAM_B2) * jnp.square(gv)
        nm_ref[...] = nm
        nv_ref[...] = nv
        d_ref[...] = -ADAM_LR * ((nm / bc1) / (jnp.sqrt(nv / bc2) + ADAM_EPS) + ADAM_WD * w_ref[...])

    blk = pl.BlockSpec((tm, c), lambda i: (i, 0))
    shp = jax.ShapeDtypeStruct((r, c), F32)
    return pl.pallas_call(
        body, name=name, grid=(r // tm,), in_specs=[blk] * 4, out_specs=[blk] * 3, out_shape=[shp] * 3,
        compiler_params=_cp(("parallel",)),
    )(g, w, m, v)


def _transpose_cast(a, *, name):
    def body(a_ref, o_ref):
        o_ref[...] = a_ref[...].T.astype(BF16)

    return pl.pallas_call(body, name=name, out_shape=jax.ShapeDtypeStruct(a.shape[::-1], BF16),
                          compiler_params=_cp())(a)


def _adamw_sum_t(parts, w, m, v, *, name):
    bc1 = 1.0 - ADAM_B1 ** ADAM_STEP
    bc2 = 1.0 - ADAM_B2 ** ADAM_STEP

    def body(p_ref, w_ref, m_ref, v_ref, g_ref, d_ref, nm_ref, nv_ref):
        acc = p_ref[0].astype(F32)
        for k in range(1, N_DEV):
            acc = acc + p_ref[k].astype(F32)
        gv = acc.T
        g_ref[...] = gv
        nm = ADAM_B1 * m_ref[...] + (1.0 - ADAM_B1) * gv
        nv = ADAM_B2 * v_ref[...] + (1.0 - ADAM_B2) * jnp.square(gv)
        nm_ref[...] = nm
        nv_ref[...] = nv
        d_ref[...] = -ADAM_LR * ((nm / bc1) / (jnp.sqrt(nv / bc2) + ADAM_EPS) + ADAM_WD * w_ref[...])

    shp = jax.ShapeDtypeStruct(w.shape, F32)
    return pl.pallas_call(body, name=name, out_shape=[shp] * 4, compiler_params=_cp())(parts, w, m, v)


def _adamw_sum(parts, w, m, v, *, name, tm=256):
    _, r, c = parts.shape
    tm = tm if r % tm == 0 else r
    bc1 = 1.0 - ADAM_B1 ** ADAM_STEP
    bc2 = 1.0 - ADAM_B2 ** ADAM_STEP

    def body(p_ref, w_ref, m_ref, v_ref, g_ref, d_ref, nm_ref, nv_ref):
        gv = p_ref[0].astype(F32)
        for k in range(1, N_DEV):
            gv = gv + p_ref[k].astype(F32)
        g_ref[...] = gv
        nm = ADAM_B1 * m_ref[...] + (1.0 - ADAM_B1) * gv
        nv = ADAM_B2 * v_ref[...] + (1.0 - ADAM_B2) * jnp.square(gv)
        nm_ref[...] = nm
        nv_ref[...] = nv
        d_ref[...] = -ADAM_LR * ((nm / bc1) / (jnp.sqrt(nv / bc2) + ADAM_EPS) + ADAM_WD * w_ref[...])

    blk = pl.BlockSpec((tm, c), lambda i: (i, 0))
    shp = jax.ShapeDtypeStruct((r, c), F32)
    return pl.pallas_call(
        body, name=name, grid=(r // tm,), in_specs=[pl.BlockSpec((N_DEV, tm, c), lambda i: (0, i, 0))] + [blk] * 3,
        out_specs=[blk] * 4, out_shape=[shp] * 4, compiler_params=_cp(("parallel",)),
    )(parts, w, m, v)


BIG = ("w_in_0", "w_out_0", "w_up_0", "w_down_0", "pool_w_1", "w_up_1", "w_down_1")
SMALL = ("norm_mix_0", "norm_ffn_0", "norm_mix_1", "pool_scale_1", "norm_ffn_1", "final_norm", "b_f_0", "conv_w_0")
WEIGHTS = ("norm_mix_0", "w_in_0", "b_f_0", "conv_w_0", "w_out_0", "norm_ffn_0", "w_up_0", "w_down_0", "norm_mix_1",
           "pool_w_1", "pool_scale_1", "norm_ffn_1", "w_up_1", "w_down_1", "final_norm")


def _pad_to(a, rows, cols):
    return jnp.pad(a, ((0, rows - a.shape[0]), (0, cols - a.shape[1])))


def _pack_small(p, width):
    rows = [p[n].reshape(1, -1) for n in SMALL[:6]]
    rows.append(_pad_to(p["b_f_0"].reshape(1, -1), 1, width))
    rows.append(_pad_to(p["conv_w_0"], 3, width))
    return _pad_to(jnp.concatenate(rows, axis=0), SMALL_ROWS, width)


def _unpack_small(a, like):
    out = {n: a[i] for i, n in enumerate(SMALL[:6])}
    out["b_f_0"] = a[6, :like["b_f_0"].shape[0]]
    out["conv_w_0"] = a[7:10, :like["conv_w_0"].shape[1]]
    return out


def kernel(x, norm_mix_0, w_in_0, b_f_0, conv_w_0, w_out_0, norm_ffn_0, w_up_0, w_down_0, norm_mix_1, pool_w_1, pool_scale_1, norm_ffn_1, w_up_1, w_down_1, final_norm, loss_target, m_norm_mix_0, m_w_in_0, m_b_f_0, m_conv_w_0, m_w_out_0, m_norm_ffn_0, m_w_up_0, m_w_down_0, m_norm_mix_1, m_pool_w_1, m_pool_scale_1, m_norm_ffn_1, m_w_up_1, m_w_down_1, m_final_norm, v_norm_mix_0, v_w_in_0, v_b_f_0, v_conv_w_0, v_w_out_0, v_norm_ffn_0, v_w_up_0, v_w_down_0, v_norm_mix_1, v_pool_w_1, v_pool_scale_1, v_norm_ffn_1, v_w_up_1, v_w_down_1, v_final_norm):
    w = dict(norm_mix_0=norm_mix_0, w_in_0=w_in_0, b_f_0=b_f_0, conv_w_0=conv_w_0, w_out_0=w_out_0,
             norm_ffn_0=norm_ffn_0, w_up_0=w_up_0, w_down_0=w_down_0, norm_mix_1=norm_mix_1, pool_w_1=pool_w_1,
             pool_scale_1=pool_scale_1, norm_ffn_1=norm_ffn_1, w_up_1=w_up_1, w_down_1=w_down_1, final_norm=final_norm)
    m = dict(norm_mix_0=m_norm_mix_0, w_in_0=m_w_in_0, b_f_0=m_b_f_0, conv_w_0=m_conv_w_0, w_out_0=m_w_out_0,
             norm_ffn_0=m_norm_ffn_0, w_up_0=m_w_up_0, w_down_0=m_w_down_0, norm_mix_1=m_norm_mix_1,
             pool_w_1=m_pool_w_1, pool_scale_1=m_pool_scale_1, norm_ffn_1=m_norm_ffn_1, w_up_1=m_w_up_1,
             w_down_1=m_w_down_1, final_norm=m_final_norm)
    v = dict(norm_mix_0=v_norm_mix_0, w_in_0=v_w_in_0, b_f_0=v_b_f_0, conv_w_0=v_conv_w_0, w_out_0=v_w_out_0,
             norm_ffn_0=v_norm_ffn_0, w_up_0=v_w_up_0, w_down_0=v_w_down_0, norm_mix_1=v_norm_mix_1,
             pool_w_1=v_pool_w_1, pool_scale_1=v_pool_scale_1, norm_ffn_1=v_norm_ffn_1, w_up_1=v_w_up_1,
             w_down_1=v_w_down_1, final_norm=v_final_norm)
    d = x.shape[-1]
    n_in = w_in_0.shape[1] * N_DEV
    n_qkv = 3 * ATTN_W
    pool_g, pool_rows, pool_c = pool_w_1.shape

    def shard2d(p):
        return {n: (p[n].reshape(pool_g * pool_rows, pool_c) if n == "pool_w_1" else p[n]) for n in BIG}
    w2, m2, v2 = shard2d(w), shard2d(m), shard2d(v)

    conv_cols = conv_w_0.shape[1]
    win_g8, conv_g8 = _all_gather([_transpose_cast(w_in_0, name="w_in_t"), _pad_to(conv_w_0, 8, 128)])
    conv_full = conv_g8[:, :, :conv_cols].transpose(1, 0, 2).reshape(8, N_DEV * conv_cols)
    win_t = win_g8.reshape(n_in, d)
    win_pt = jnp.concatenate([win_t[:n_qkv], _pad_to(win_t[n_qkv:n_qkv + N_HEADS], F_PAD, d),
                              win_t[n_qkv + N_HEADS:]], axis=0)

    gains = dict(mix0=norm_mix_0.reshape(1, d), ffn0=norm_ffn_0.reshape(1, d), mix1=norm_mix_1.reshape(1, d),
                 ffn1=norm_ffn_1.reshape(1, d), final=final_norm.reshape(1, d))
    dev = _slot(lax.axis_index("x"), lax.axis_index("y"), lax.axis_index("c"))
    loss8, grad_x, landed, small = _local_step(
        x[0], loss_target[0], gains, _pad_to(b_f_0.reshape(1, -1), 1, F_PAD), conv_full, pool_scale_1.reshape(1, d),
        win_pt, {n: w2[n].astype(BF16) for n in LATE})
    parts = jnp.concatenate(
        [small[k][None] for k in ("mix0", "ffn0", "mix1", "pool_scale", "ffn1", "final")]
        + [_pad_to(small["b_f"], 8, d)[None], jnp.pad(small["conv_w"], ((0, 0), (0, 0), (0, d - CONV_CH))),
           _pad_to(loss8[0:1, 0:1], 8, d)[None]], axis=0)
    tot = _small_allreduce(parts)
    loss = tot[10, 0]
    conv_g = lax.dynamic_slice(tot, (7, dev * conv_cols), (3, conv_cols))
    gs = tot.at[7:10].set(_pad_to(conv_g, 3, d))

    grads, deltas, new_m, new_v = {}, {}, {}, {}
    for n in BIG:
        if n in LATE:
            gr, dl, nm, nv = _adamw_sum(landed[n], w2[n], m2[n], v2[n], name="adamw_" + n)
        else:
            gr, dl, nm, nv = _adamw_sum_t(landed[n], w2[n], m2[n], v2[n], name="adamw_" + n)
        for dst, val in ((grads, gr), (deltas, dl), (new_m, nm), (new_v, nv)):
            dst[n] = val.reshape(w[n].shape)
    dl, nm, nv = _adamw(gs, _pack_small(w, d), _pack_small(m, d), _pack_small(v, d), name="adamw_small")
    for dst, val in ((grads, gs), (deltas, dl), (new_m, nm), (new_v, nv)):
        dst.update(_unpack_small(val, w))
    return (loss, grad_x[None], *[grads[n] for n in WEIGHTS], *[deltas[n] for n in WEIGHTS],
            *[new_m[n] for n in WEIGHTS], *[new_v[n] for n in WEIGHTS])
```

```python
import functools

import jax
import jax.numpy as jnp
from jax import lax
from jax.experimental import pallas as pl
from jax.experimental.pallas import tpu as pltpu

F32 = jnp.float32
BF16 = jnp.bfloat16

N_DEV = 8
N_HEADS = 8
HEAD_DIM = 64
PAIR = 2 * HEAD_DIM
ATTN_W = N_HEADS * HEAD_DIM
CONV_CH = 512
F_PAD = 128
POOL_WINDOWS = (2, 4, 8, 16)
POOL_HALO = 16
CONV_HALO = 16
RMS_EPS = 1e-6
Q_SCALE = HEAD_DIM ** -0.5
LOG2E = 1.4426950408889634
NEG = -1e30
AUX_BIAS = 0
AUX_LSE = 3
AUX_ROWSUM = 6
ADAM_LR, ADAM_B1, ADAM_B2, ADAM_EPS, ADAM_WD, ADAM_STEP = 0.001, 0.9, 0.999, 1e-08, 0.01, 10
MESH = pl.DeviceIdType.MESH
VMEM_LIMIT = 56 * 2**20


def _cp(sem=None, vmem=VMEM_LIMIT, **kw):
    return pltpu.CompilerParams(dimension_semantics=sem, vmem_limit_bytes=vmem, **kw)


def _dot(a, b):
    return jnp.dot(a, b, preferred_element_type=F32)


def _dot_nt(a, b):
    return lax.dot_general(a, b, (((1,), (1,)), ((), ())), preferred_element_type=F32)


def _dot_tn(a, b):
    return lax.dot_general(a, b, (((0,), (0,)), ((), ())), preferred_element_type=F32)


def _rstd(h):
    return lax.rsqrt(jnp.mean(h * h, axis=-1, keepdims=True) + RMS_EPS)


def _rows8(x):
    r, n = x.shape
    return jnp.sum(x.reshape(r // 8, 8, n), axis=0)


def _norm_bwd(dn, h, g):
    r = _rstd(h)
    xhat = h * r
    dy = dn * g
    dh = r * (dy - xhat * jnp.mean(dy * xhat, axis=-1, keepdims=True))
    return dh, _rows8(dn * xhat)


def _const_spec(shape):
    nd = len(shape)
    return pl.BlockSpec(shape, lambda *_: (0,) * nd, pipeline_mode=pl.Buffered(1))


HBM_SPEC = pl.BlockSpec(memory_space=pltpu.HBM)
VMEM_SPEC = pl.BlockSpec(memory_space=pltpu.VMEM)


def _slot(px, py, pc):
    return 4 * px + 2 * py + pc


class _Exchange:
    def __init__(self, srcs, dsts, send_sems, recv_sems, local_sems, gather):
        x, y, c = lax.axis_index("x"), lax.axis_index("y"), lax.axis_index("c")
        me = _slot(x, y, c)
        self.copies = []
        for a, (src, dst) in enumerate(zip(srcs, dsts)):
            self.copies.append(pltpu.make_async_copy(src if gather else src.at[me], dst.at[me], local_sems.at[a]))
            for k in range(1, N_DEV):
                px, py, pc = x ^ (k >> 2), y ^ ((k >> 1) & 1), c ^ (k & 1)
                self.copies.append(pltpu.make_async_remote_copy(
                    src_ref=src if gather else src.at[_slot(px, py, pc)], dst_ref=dst.at[me],
                    send_sem=send_sems.at[(N_DEV - 1) * a + k - 1], recv_sem=recv_sems.at[(N_DEV - 1) * a + k - 1],
                    device_id=(px, py, pc), device_id_type=MESH))

    def start(self):
        for cp in self.copies:
            cp.start()

    def wait(self):
        for cp in self.copies:
            cp.wait()

    @staticmethod
    def scratch(n):
        return [pltpu.SemaphoreType.DMA(((N_DEV - 1) * n,)), pltpu.SemaphoreType.DMA(((N_DEV - 1) * n,)),
                pltpu.SemaphoreType.DMA((n,))]


def _mesh_places():
    x, y, c = lax.axis_index("x"), lax.axis_index("y"), lax.axis_index("c")
    chips = [(1 - x, y), (x, 1 - y), (1 - x, 1 - y)]
    return (x, y, c), (x, y, 1 - c), chips


class _TwoLevelGather:
    def __init__(self, srcs, dsts, send_sems, recv_sems, local_sems):
        me, sib, chips = _mesh_places()
        c = me[2]
        n = len(srcs)

        def copy(a, k, block, to, src=None):
            dst = dsts[a].at[_slot(*block)]
            return pltpu.make_async_remote_copy(
                src_ref=dst if src is None else src, dst_ref=dst, send_sem=send_sems.at[7 * a + k],
                recv_sem=recv_sems.at[7 * a + k], device_id=to, device_id_type=MESH)

        self.mine = [pltpu.make_async_copy(srcs[a], dsts[a].at[_slot(*me)], local_sems.at[a]) for a in range(n)]
        self.first, self.landed, self.passed, self.rest = [], [], [], []
        for a in range(n):
            self.first.append(copy(a, 0, me, sib, src=srcs[a]))
            self.first += [copy(a, 1 + j, me, (*chip, c), src=srcs[a]) for j, chip in enumerate(chips)]
            self.landed += [copy(a, 1 + j, (*chip, c), me) for j, chip in enumerate(chips)]
            self.passed += [copy(a, 4 + j, (*chip, c), sib) for j, chip in enumerate(chips)]
            self.rest.append(copy(a, 0, sib, me))
            self.rest += [copy(a, 4 + j, (*chip, 1 - c), me) for j, chip in enumerate(chips)]

    def start(self):
        for cp in self.mine + self.first:
            cp.start()

    def forward(self):
        for arrived, onward in zip(self.landed, self.passed):
            arrived.wait_recv()
            onward.start()

    def wait(self):
        for cp in self.rest:
            cp.wait_recv()
        for cp in self.first + self.passed:
            cp.wait_send()
        for cp in self.mine:
            cp.wait()


def _norm_inproj(x, g, win_pt, conv_w, *, tm=512):
    t, d = x.shape
    n_all = win_pt.shape[0]
    n_qkv = 3 * ATTN_W
    n_bcx = 3 * CONV_CH
    assert n_all == n_qkv + F_PAD + n_bcx
    tm = min(tm, t)
    ch = CONV_CH

    def body(x_ref, g_ref, w_ref, cw_ref, n_ref, qkv_ref, f_ref, bcx_ref, cv_ref, ext):
        h = x_ref[...]
        n = (h * _rstd(h) * g_ref[...]).astype(BF16)
        n_ref[...] = n
        for c0 in range(0, n_qkv, 512):
            acc = _dot_nt(n, w_ref[c0:c0 + 512, :])
            if c0 < ATTN_W:
                acc = acc * (Q_SCALE * LOG2E)
            qkv_ref[:, c0:c0 + 512] = acc.astype(BF16)
        f_ref[...] = _dot_nt(n, w_ref[n_qkv:n_qkv + F_PAD, :])
        bcx = []
        for k in range(3):
            c0 = n_qkv + F_PAD + k * ch
            v = _dot_nt(n, w_ref[c0:c0 + ch, :]).astype(BF16)
            bcx_ref[:, k * ch:(k + 1) * ch] = v
            bcx.append(v.astype(F32))
        @pl.when(pl.program_id(0) == 0)
        def _():
            ext[tm:tm + CONV_HALO, :] = jnp.zeros((CONV_HALO, ch), F32)
        ext[0:CONV_HALO, :] = ext[tm:tm + CONV_HALO, :]
        ext[CONV_HALO:CONV_HALO + tm, :] = bcx[1] * bcx[2]
        conv = (cw_ref[0:1, :] * ext[CONV_HALO - 2:CONV_HALO - 2 + tm, :]
                + cw_ref[1:2, :] * ext[CONV_HALO - 1:CONV_HALO - 1 + tm, :]
                + cw_ref[2:3, :] * ext[CONV_HALO:CONV_HALO + tm, :])
        cv_ref[...] = (bcx[0] * conv).astype(BF16)

    return pl.pallas_call(
        body, name="norm_inproj", grid=(t // tm,),
        in_specs=[pl.BlockSpec((tm, d), lambda i: (i, 0)), _const_spec((1, d)), _const_spec((n_all, d)),
                  _const_spec((8, ch))],
        out_specs=[pl.BlockSpec((tm, d), lambda i: (i, 0)), pl.BlockSpec((tm, n_qkv), lambda i: (i, 0)),
                   pl.BlockSpec((tm, F_PAD), lambda i: (i, 0)), pl.BlockSpec((tm, n_bcx), lambda i: (i, 0)),
                   pl.BlockSpec((tm, ch), lambda i: (i, 0))],
        out_shape=[jax.ShapeDtypeStruct((t, d), BF16), jax.ShapeDtypeStruct((t, n_qkv), BF16),
                   jax.ShapeDtypeStruct((t, F_PAD), F32), jax.ShapeDtypeStruct((t, n_bcx), BF16),
                   jax.ShapeDtypeStruct((t, ch), BF16)],
        scratch_shapes=[pltpu.VMEM((CONV_HALO + tm, ch), F32)],
        compiler_params=_cp(("arbitrary",)),
    )(x, g, win_pt, conv_w)


def _head_lanes(h):
    lane = lax.broadcasted_iota(jnp.int32, (1, PAIR), 1)
    hh = h % 2
    return lane, lane // HEAD_DIM == hh, HEAD_DIM * (1 - hh)


def _pieces(col):
    hi = col.astype(BF16).astype(F32)
    r1 = col - hi
    mid = r1.astype(BF16).astype(F32)
    lo = (r1 - mid).astype(BF16).astype(F32)
    return hi, mid, lo


def _put_pieces(lane, first, col, other):
    hi, mid, lo = _pieces(col)
    return jnp.where(lane == first, hi, jnp.where(lane == first + 1, mid, jnp.where(lane == first + 2, lo, other)))


def _fgate_prep(flog, b_f, qkv, *, tm=512):
    t = flog.shape[0]
    tm = min(tm, t)

    def body(f_ref, b_ref, qkv_ref, qat_ref, ka_ref, vat_ref, sg_ref, carry):
        @pl.when(pl.program_id(0) == 0)
        def _():
            carry[...] = jnp.zeros_like(carry)
        z = f_ref[...] + b_ref[...]
        e = jnp.exp(-jnp.abs(z))
        logf = jnp.minimum(z, 0.0) - jnp.log(1.0 + e)
        sg_ref[...] = jnp.where(z >= 0, e, 1.0) / (1.0 + e)
        r = lax.broadcasted_iota(jnp.int32, (tm, tm), 0)
        c = lax.broadcasted_iota(jnp.int32, (tm, tm), 1)
        tri = (c <= r).astype(F32)
        cs = jnp.dot(tri, logf, preferred_element_type=F32, precision=lax.Precision.HIGHEST) + carry[...]
        carry[...] = cs[tm - 1:tm, :]
        cs2 = cs * LOG2E
        for h in range(N_HEADS):
            lane, head, aux = _head_lanes(h)
            p0 = (h // 2) * PAIR
            ones = ((lane >= aux + AUX_LSE) & (lane <= aux + AUX_ROWSUM)).astype(F32)
            bias = (lane >= aux + AUX_BIAS) & (lane < aux + AUX_BIAS + 3)
            k_aux = _put_pieces(lane, aux + AUX_BIAS, cs2[:, h:h + 1], ones)
            q_aug = jnp.where(head, qkv_ref[:, p0:p0 + PAIR], jnp.where(bias, -1.0, 0.0).astype(BF16))
            v_aug = jnp.where(head, qkv_ref[:, 2 * ATTN_W + p0:2 * ATTN_W + p0 + PAIR],
                              jnp.where(bias, 1.0, 0.0).astype(BF16))
            qat_ref[h] = q_aug.T
            ka_ref[h] = jnp.where(head, qkv_ref[:, ATTN_W + p0:ATTN_W + p0 + PAIR], k_aux.astype(BF16))
            vat_ref[h] = v_aug.T

    aug = lambda: pl.BlockSpec((N_HEADS, tm, PAIR), lambda i: (0, i, 0))
    aug_t = lambda: pl.BlockSpec((N_HEADS, PAIR, tm), lambda i: (0, 0, i))
    aug_shape = jax.ShapeDtypeStruct((N_HEADS, t, PAIR), BF16)
    aug_t_shape = jax.ShapeDtypeStruct((N_HEADS, PAIR, t), BF16)
    return pl.pallas_call(
        body, name="fgate_prep", grid=(t // tm,),
        in_specs=[pl.BlockSpec((tm, F_PAD), lambda i: (i, 0)), _const_spec((1, F_PAD)),
                  pl.BlockSpec((tm, 3 * ATTN_W), lambda i: (i, 0))],
        out_specs=[aug_t(), aug(), aug_t(), pl.BlockSpec((tm, F_PAD), lambda i: (i, 0))],
        out_shape=[aug_t_shape, aug_shape, aug_t_shape, jax.ShapeDtypeStruct((t, F_PAD), F32)],
        scratch_shapes=[pltpu.VMEM((1, F_PAD), F32)],
        compiler_params=_cp(("arbitrary",)),
    )(flog, b_f, qkv)


def _put_pieces_t(row, first, vec, other):
    hi, mid, lo = _pieces(vec)
    return jnp.where(row == first, hi, jnp.where(row == first + 1, mid, jnp.where(row == first + 2, lo, other)))


def _attn_fwd(q_aug_t, k_aug, v_aug_t, shards, *, tq=1024):
    t = k_aug.shape[1]
    tq = min(tq, t)
    tk = tq // 2
    nq = t // tq
    n_pairs = ATTN_W // PAIR
    n_sh = len(shards)
    forward_step = (3 * n_pairs * nq) // 4

    def body(qt_ref, k_ref, vt_ref, *rest):
        o_ref, qb_ref, qbt_ref = rest[n_sh:n_sh + 3]
        s_scr = rest[2 * n_sh + 3]
        gather = _TwoLevelGather(rest[:n_sh], rest[n_sh + 3:2 * n_sh + 3], *rest[2 * n_sh + 4:])
        i = pl.program_id(1)
        step = pl.program_id(0) * nq + i

        @pl.when(step == 0)
        def _():
            gather.start()

        @pl.when(step == forward_step)
        def _():
            gather.forward()
        key = lax.broadcasted_iota(jnp.int32, (tk, tq), 0)
        qry = lax.broadcasted_iota(jnp.int32, (tk, tq), 1)
        qt = [qt_ref[0], qt_ref[1]]

        def logits(hh, tile, slot, diag):
            s = _dot(k_ref[hh, pl.ds(pl.multiple_of(tile * tk, tk), tk), :], qt[hh])
            if diag:
                s = jnp.where(key + (tile * tk - i * tq) <= qry, s, NEG)
            s_scr[hh, slot] = s
            return jnp.max(s, axis=0, keepdims=True)

        def probs(hh, tile, slot, m, acc, tmax):
            mn = jnp.maximum(m, tmax)
            p = jnp.exp2(s_scr[hh, slot] - mn).astype(BF16)
            acc = jnp.exp2(m - mn) * acc + _dot(vt_ref[hh, :, pl.ds(pl.multiple_of(tile * tk, tk), tk)], p)
            return mn, acc

        def advance(carry, prev, slot, nxt, diag=False):
            out = []
            for hh in range(2):
                m, acc, tmax = carry[hh]
                m, acc = probs(hh, prev, slot, m, acc, tmax)
                out.append((m, acc, logits(hh, nxt, 1 - slot, diag)))
            return tuple(out)

        def two_tiles(jj, carry):
            carry = advance(carry, jnp.where(jj == 0, 2 * i, 2 * jj - 1), 1, 2 * jj)
            return advance(carry, 2 * jj, 0, 2 * jj + 1)

        init = tuple((jnp.full((1, tq), NEG, F32), jnp.zeros((PAIR, tq), F32), logits(hh, 2 * i + 1, 0, True))
                     for hh in range(2))
        carry = advance(init, 2 * i + 1, 0, 2 * i, diag=True)
        carry = lax.fori_loop(0, i // 2, lambda jj, c: two_tiles(2 * jj + 1, two_tiles(2 * jj, c)), carry)
        carry = lax.cond(i % 2 == 1, lambda c: two_tiles(i - 1, c), lambda c: c, carry)
        last = jnp.where(i == 0, 2 * i, 2 * i - 1)
        row = lax.broadcasted_iota(jnp.int32, (PAIR, 1), 0)
        res = []
        for hh in range(2):
            aux = HEAD_DIM * (1 - hh)
            m, acc, tmax = carry[hh]
            m, acc = probs(hh, last, 1, m, acc, tmax)
            l = acc[aux + AUX_BIAS:aux + AUX_BIAS + 1, :]
            qbt = _put_pieces_t(row, aux + AUX_LSE, -(m + jnp.log2(l)), qt[hh].astype(F32))
            qbt_ref[hh] = qbt.astype(BF16)
            qb_ref[hh] = qbt.astype(BF16).T
            res.append(acc * (1.0 / l))
        o_ref[...] = jnp.where(row < HEAD_DIM, res[0], res[1]).astype(BF16).T

        @pl.when((pl.program_id(0) == n_pairs - 1) & (i == nq - 1))
        def _():
            gather.wait()

    res = pl.pallas_call(
        body, name="attn_fwd", grid=(n_pairs, nq),
        in_specs=[pl.BlockSpec((2, PAIR, tq), lambda p, i: (p, 0, i)),
                  pl.BlockSpec((2, t, PAIR), lambda p, i: (p, 0, 0), pipeline_mode=pl.Buffered(1)),
                  pl.BlockSpec((2, PAIR, t), lambda p, i: (p, 0, 0), pipeline_mode=pl.Buffered(1))] + [HBM_SPEC] * n_sh,
        out_specs=[pl.BlockSpec((tq, PAIR), lambda p, i: (i, p)),
                   pl.BlockSpec((2, tq, PAIR), lambda p, i: (p, i, 0)),
                   pl.BlockSpec((2, PAIR, tq), lambda p, i: (p, 0, i))] + [HBM_SPEC] * n_sh,
        out_shape=[jax.ShapeDtypeStruct((t, ATTN_W), BF16), jax.ShapeDtypeStruct((N_HEADS, t, PAIR), BF16),
                   jax.ShapeDtypeStruct((N_HEADS, PAIR, t), BF16)]
        + [jax.ShapeDtypeStruct((N_DEV,) + s.shape, s.dtype) for s in shards],
        scratch_shapes=[pltpu.VMEM((2, 2, tk, tq), F32)] + _Exchange.scratch(n_sh),
        compiler_params=_cp(("arbitrary", "arbitrary")),
    )(q_aug_t, k_aug, v_aug_t, *shards)
    return res[0], res[1], res[2], res[3:]


def _prev_halo(tm, halo):
    return lambda i: (jnp.maximum(i * (tm // halo) - 1, 0), 0)


def _next_halo(tm, halo, t):
    return lambda i: (jnp.minimum((i + 1) * (tm // halo), t // halo - 1), 0)


def _mlp_tile(hh, g_ref, wu_ref, wd_ref, n_ref, a_ref, z_ref):
    n_blk, _, fb = wu_ref.shape
    n = (hh * _rstd(hh) * g_ref[...]).astype(BF16)
    n_ref[...] = n
    acc = hh
    for k in range(n_blk):
        a = _dot(n, wu_ref[k])
        zz = jnp.square(jnp.maximum(a, 0.0)).astype(BF16)
        a_ref[:, k * fb:(k + 1) * fb] = a.astype(BF16)
        z_ref[:, k * fb:(k + 1) * fb] = zz
        acc = acc + _dot(zz, wd_ref[k * fb:(k + 1) * fb, :])
    return acc


def _outproj(att, cv, x, wout, *, tm=512):
    t, d = x.shape
    tm = min(tm, t)

    def body(a_ref, c_ref, x_ref, w_ref, h_ref):
        h_ref[...] = x_ref[...] + _dot(a_ref[...], w_ref[0:ATTN_W, :]) + _dot(c_ref[...], w_ref[ATTN_W:, :])

    return pl.pallas_call(
        body, name="outproj", grid=(t // tm,),
        in_specs=[pl.BlockSpec((tm, ATTN_W), lambda i: (i, 0)), pl.BlockSpec((tm, CONV_CH), lambda i: (i, 0)),
                  pl.BlockSpec((tm, d), lambda i: (i, 0)), _const_spec(wout.shape)],
        out_specs=pl.BlockSpec((tm, d), lambda i: (i, 0)),
        out_shape=jax.ShapeDtypeStruct((t, d), F32),
        compiler_params=_cp(("parallel",)),
    )(att, cv, x, wout)


def _mlp_fwd(h, g, wup, wdown, *, name, tm=512):
    t, d = h.shape
    n_blk, _, fb = wup.shape
    f = n_blk * fb
    tm = min(tm, t)

    def body(h_ref, g_ref, wu_ref, wd_ref, ho_ref, n_ref, a_ref, z_ref):
        ho_ref[...] = _mlp_tile(h_ref[...], g_ref, wu_ref, wd_ref, n_ref, a_ref, z_ref)

    row = lambda n_: pl.BlockSpec((tm, n_), lambda i: (i, 0))
    return pl.pallas_call(
        body, name=name, grid=(t // tm,),
        in_specs=[row(d), _const_spec((1, d)), _const_spec(wup.shape), _const_spec(wdown.shape)],
        out_specs=[row(d), row(d), row(f), row(f)],
        out_shape=[jax.ShapeDtypeStruct((t, d), F32), jax.ShapeDtypeStruct((t, d), BF16),
                   jax.ShapeDtypeStruct((t, f), BF16), jax.ShapeDtypeStruct((t, f), BF16)],
        compiler_params=_cp(("parallel",)),
    )(h, g, wup, wdown)


def _mlp_fwd_loss(h, g, wup, wdown, g_out, target, *, name, tm=512):
    t, d = h.shape
    n_blk, _, fb = wup.shape
    f = n_blk * fb
    tm = min(tm, t)
    nsteps = t // tm

    def body(h_ref, g_ref, wu_ref, wd_ref, go_ref, y_ref, loss_ref, dh_ref, dhb_ref, dg_ref, n_ref, a_ref, z_ref, lacc):
        i = pl.program_id(0)

        @pl.when(i == 0)
        def _():
            lacc[...] = jnp.zeros_like(lacc)
            dg_ref[...] = jnp.zeros_like(dg_ref)
        hv = _mlp_tile(h_ref[...], g_ref, wu_ref, wd_ref, n_ref, a_ref, z_ref)
        gv = go_ref[...]
        r = _rstd(hv)
        xhat = hv * r
        err = xhat * gv - y_ref[...]
        lacc[...] += _rows8(err * err)
        dout = err * (1.0 / d)
        dy = dout * gv
        dg_ref[...] += _rows8(dout * xhat)
        dh = r * (dy - xhat * jnp.mean(dy * xhat, axis=-1, keepdims=True))
        dh_ref[...] = dh
        dhb_ref[...] = dh.astype(BF16)

        @pl.when(i == nsteps - 1)
        def _():
            loss_ref[...] = jnp.full(loss_ref.shape, (0.5 / d) * jnp.sum(lacc[...]), F32)

    row = lambda n_: pl.BlockSpec((tm, n_), lambda i: (i, 0))
    return pl.pallas_call(
        body, name=name, grid=(nsteps,),
        in_specs=[row(d), _const_spec((1, d)), _const_spec(wup.shape), _const_spec(wdown.shape), _const_spec((1, d)),
                  row(d)],
        out_specs=[pl.BlockSpec((8, 128), lambda i: (0, 0)), row(d), row(d), pl.BlockSpec((8, d), lambda i: (0, 0)),
                   row(d), row(f), row(f)],
        out_shape=[jax.ShapeDtypeStruct((8, 128), F32), jax.ShapeDtypeStruct((t, d), F32),
                   jax.ShapeDtypeStruct((t, d), BF16),
                   jax.ShapeDtypeStruct((8, d), F32), jax.ShapeDtypeStruct((t, d), BF16),
                   jax.ShapeDtypeStruct((t, f), BF16), jax.ShapeDtypeStruct((t, f), BF16)],
        scratch_shapes=[pltpu.VMEM((8, d), F32)],
        compiler_params=_cp(("arbitrary",)),
    )(h, g, wup, wdown, g_out, target)


def _pool_inv_count(i, tm):
    tglob = (i * tm + lax.broadcasted_iota(jnp.int32, (tm, 1), 0) + 1).astype(F32)
    return [1.0 / jnp.minimum(tglob, float(w)) for w in POOL_WINDOWS]


def _pool_fwd(h, g, poolw, scale, *, tm=512):
    t, d = h.shape
    tm = min(tm, t)
    cg = d // len(POOL_WINDOWS)

    def body(h_ref, hh_ref, g_ref, w_ref, s_ref, ho_ref, p_ref, ext):
        i = pl.program_id(0)
        hv = h_ref[...]
        halo = hh_ref[...]
        n = hv * _rstd(hv) * g_ref[...]
        ext[0:POOL_HALO, :] = jnp.where(i == 0, 0.0, halo * _rstd(halo) * g_ref[...])
        ext[POOL_HALO:POOL_HALO + tm, :] = n
        inv = _pool_inv_count(i, tm)
        for gi, w in enumerate(POOL_WINDOWS):
            cs = slice(gi * cg, (gi + 1) * cg)
            s = ext[POOL_HALO:POOL_HALO + tm, cs]
            for j in range(1, w):
                s = s + ext[POOL_HALO - j:POOL_HALO - j + tm, cs]
            pooled = (s * inv[gi] - n[:, cs]).astype(BF16)
            p_ref[:, cs] = pooled
            ho_ref[:, cs] = hv[:, cs] + _dot(pooled, w_ref[gi]) * s_ref[:, cs]

    row = lambda: pl.BlockSpec((tm, d), lambda i: (i, 0))
    return pl.pallas_call(
        body, name="pool_fwd", grid=(t // tm,),
        in_specs=[row(), pl.BlockSpec((POOL_HALO, d), _prev_halo(tm, POOL_HALO)), _const_spec((1, d)),
                  _const_spec(poolw.shape), _const_spec((1, d))],
        out_specs=[row(), row()],
        out_shape=[jax.ShapeDtypeStruct((t, d), F32), jax.ShapeDtypeStruct((t, d), BF16)],
        scratch_shapes=[pltpu.VMEM((POOL_HALO + tm, d), F32)],
        compiler_params=_cp(("parallel",)),
    )(h, h, g, poolw, scale)


def _mm_tn(a, b, *, name, ta, tb, tt, blocked_out=False, out_dtype=F32):
    t, ka = a.shape
    n = b.shape[1]
    ta, tb, tt = min(ta, ka), min(tb, n), min(tt, t)
    nt = t // tt

    def body(a_ref, b_ref, o_ref, acc):
        @pl.when(pl.program_id(2) == 0)
        def _():
            acc[...] = jnp.zeros_like(acc)
        acc[...] += _dot_tn(a_ref[...].astype(BF16), b_ref[...].astype(BF16))

        @pl.when(pl.program_id(2) == nt - 1)
        def _():
            o_ref[...] = acc[...].astype(out_dtype)

    if blocked_out:
        assert ta == ka
        out_shape = jax.ShapeDtypeStruct((n // tb, ka, tb), out_dtype)
        out_spec = pl.BlockSpec((None, ta, tb), lambda i, j, k: (j, i, 0))
    else:
        out_shape = jax.ShapeDtypeStruct((ka, n), out_dtype)
        out_spec = pl.BlockSpec((ta, tb), lambda i, j, k: (i, j))
    return pl.pallas_call(
        body, name=name, grid=(ka // ta, n // tb, nt),
        in_specs=[pl.BlockSpec((tt, ta), lambda i, j, k: (k, i)), pl.BlockSpec((tt, tb), lambda i, j, k: (k, j))],
        out_specs=out_spec, out_shape=out_shape, scratch_shapes=[pltpu.VMEM((ta, tb), F32)],
        compiler_params=_cp(("parallel", "parallel", "arbitrary")),
    )(a, b)


def _mm_tn_cat(a_list, b_list, *, name, tt, out_dtype=BF16):
    t = a_list[0].shape[0]
    ta, tb = a_list[0].shape[1], b_list[0].shape[1]
    na, nb = len(a_list), len(b_list)
    tt = min(tt, t)
    nt = t // tt

    def body(*refs):
        a_refs, b_refs, o_ref, acc = refs[:na], refs[na:na + nb], refs[na + nb], refs[na + nb + 1]
        i, j, k = pl.program_id(0), pl.program_id(1), pl.program_id(2)

        @pl.when(k == 0)
        def _():
            acc[...] = jnp.zeros_like(acc)
        for ia in range(na):
            for ib in range(nb):
                @pl.when((i == ia) & (j == ib))
                def _(ia=ia, ib=ib):
                    acc[...] += _dot_tn(a_refs[ia][...].astype(BF16), b_refs[ib][...].astype(BF16))

        @pl.when(k == nt - 1)
        def _():
            o_ref[...] = acc[...].astype(out_dtype)

    def held(m, axis):
        def index(i, j, k):
            cur = (i, j)[axis]
            return (jnp.where(cur == m, k, jnp.where(cur < m, 0, nt - 1)), 0)
        return index

    return pl.pallas_call(
        body, name=name, grid=(na, nb, nt),
        in_specs=[pl.BlockSpec((tt, ta), held(m, 0)) for m in range(na)]
        + [pl.BlockSpec((tt, tb), held(m, 1)) for m in range(nb)],
        out_specs=pl.BlockSpec((ta, tb), lambda i, j, k: (i, j)),
        out_shape=jax.ShapeDtypeStruct((na * ta, nb * tb), out_dtype), scratch_shapes=[pltpu.VMEM((ta, tb), F32)],
        compiler_params=_cp(("arbitrary", "arbitrary", "arbitrary")),
    )(*a_list, *b_list)


def _mlp_bwd(dho, h, a, g, wup, wdown, *, name, tm=512):
    t, d = h.shape
    n_blk, _, fb = wup.shape
    f = n_blk * fb
    tm = min(tm, t)

    def body(do_ref, h_ref, a_ref, g_ref, wu_ref, wd_ref, dh_ref, da_ref, dg_ref):
        @pl.when(pl.program_id(0) == 0)
        def _():
            dg_ref[...] = jnp.zeros_like(dg_ref)
        dho_v = do_ref[...]
        dob = dho_v.astype(BF16)
        dn = jnp.zeros((tm, d), F32)
        for k in range(n_blk):
            dz = _dot_nt(dob, wd_ref[k * fb:(k + 1) * fb, :])
            da = (dz * (2.0 * jnp.maximum(a_ref[:, k * fb:(k + 1) * fb].astype(F32), 0.0))).astype(BF16)
            da_ref[:, k * fb:(k + 1) * fb] = da
            dn = dn + _dot_nt(da, wu_ref[k])
        dh, dg = _norm_bwd(dn, h_ref[...], g_ref[...])
        dh_ref[...] = dho_v + dh
        dg_ref[...] += dg

    row = lambda n_: pl.BlockSpec((tm, n_), lambda i: (i, 0))
    return pl.pallas_call(
        body, name=name, grid=(t // tm,),
        in_specs=[row(d), row(d), row(f), _const_spec((1, d)), _const_spec(wup.shape), _const_spec(wdown.shape)],
        out_specs=[row(d), row(f), pl.BlockSpec((8, d), lambda i: (0, 0))],
        out_shape=[jax.ShapeDtypeStruct((t, d), F32), jax.ShapeDtypeStruct((t, f), BF16),
                   jax.ShapeDtypeStruct((8, d), F32)],
        compiler_params=_cp(("arbitrary",)),
    )(dho, h, a, g, wup, wdown)


def _pool_bwd(dho, h, pooled, g, poolw, scale, *, tm=512):
    t, d = h.shape
    tm = min(tm, t)
    ng = len(POOL_WINDOWS)
    cg = d // ng
    nsteps = t // tm

    def body(do_ref, dn_ref, h_ref, p_ref, g_ref, w_ref, s_ref, dh_ref, dhb_ref, dw_ref, ds_ref, dg_ref, ext):
        i = pl.program_id(0)

        @pl.when(i == 0)
        def _():
            dw_ref[...] = jnp.zeros_like(dw_ref)
            ds_ref[...] = jnp.zeros_like(ds_ref)
            dg_ref[...] = jnp.zeros_like(dg_ref)
        dho_v = do_ref[...]
        sv = s_ref[...]
        dyp = (dho_v * sv).astype(BF16)
        dyp_halo = (dn_ref[...] * sv).astype(BF16)
        inv = _pool_inv_count(i, tm)
        tnext = ((i + 1) * tm + lax.broadcasted_iota(jnp.int32, (POOL_HALO, 1), 0) + 1).astype(F32)
        last = i == nsteps - 1
        ypre_parts, dpooled_parts = [], []
        for gi, w in enumerate(POOL_WINDOWS):
            cs = slice(gi * cg, (gi + 1) * cg)
            pg = p_ref[:, cs]
            ypre_parts.append(_dot(pg, w_ref[gi]))
            dw_ref[gi] += _dot_tn(pg, dyp[:, cs])
            dpool = _dot_nt(dyp[:, cs], w_ref[gi])
            dpooled_parts.append(dpool)
            ext[0:tm, cs] = dpool * inv[gi]
            dpool_halo = _dot_nt(dyp_halo[:, cs], w_ref[gi]) * (1.0 / jnp.minimum(tnext, float(w)))
            ext[tm:tm + POOL_HALO, cs] = jnp.where(last, 0.0, dpool_halo)
        ds_ref[...] += _rows8(dho_v * jnp.concatenate(ypre_parts, axis=1))
        dn_parts = []
        for gi, w in enumerate(POOL_WINDOWS):
            cs = slice(gi * cg, (gi + 1) * cg)
            s = ext[0:tm, cs]
            for j in range(1, w):
                s = s + ext[j:j + tm, cs]
            dn_parts.append(s - dpooled_parts[gi])
        dh, dg = _norm_bwd(jnp.concatenate(dn_parts, axis=1), h_ref[...], g_ref[...])
        dh = dho_v + dh
        dh_ref[...] = dh
        dhb_ref[...] = dh.astype(BF16)
        dg_ref[...] += dg

    row = lambda: pl.BlockSpec((tm, d), lambda i: (i, 0))
    acc8 = lambda: pl.BlockSpec((8, d), lambda i: (0, 0))
    return pl.pallas_call(
        body, name="pool_bwd", grid=(nsteps,),
        in_specs=[row(), pl.BlockSpec((POOL_HALO, d), _next_halo(tm, POOL_HALO, t)), row(), row(),
                  _const_spec((1, d)), _const_spec(poolw.shape), _const_spec((1, d))],
        out_specs=[row(), row(), pl.BlockSpec((ng, cg, cg), lambda i: (0, 0, 0)), acc8(), acc8()],
        out_shape=[jax.ShapeDtypeStruct((t, d), F32), jax.ShapeDtypeStruct((t, d), BF16),
                   jax.ShapeDtypeStruct((ng, cg, cg), F32),
                   jax.ShapeDtypeStruct((8, d), F32), jax.ShapeDtypeStruct((8, d), F32)],
        scratch_shapes=[pltpu.VMEM((tm + POOL_HALO, d), F32)],
        compiler_params=_cp(("arbitrary",)),
    )(dho, dho, h, pooled, g, poolw, scale)


def _outproj_conv_bwd(dh, o, wout, bcx, conv_w, *, tm=512):
    t, d = dh.shape
    tm = min(tm, t)
    ch = CONV_CH
    nsteps = t // tm

    def body(dh_ref, o_ref, w_ref, b_ref, c_ref, x_ref, hc_ref, hx_ref, cw_ref,
             da_ref, dat_ref, db_ref, dw_ref, ext_u, ext_d):
        s = pl.program_id(0)

        @pl.when(s == 0)
        def _():
            dw_ref[...] = jnp.zeros_like(dw_ref)
            ext_d[0:CONV_HALO, :] = jnp.zeros((CONV_HALO, ch), F32)
        dhb = dh_ref[...].astype(BF16)
        for p in range(ATTN_W // PAIR):
            datt = _dot_nt(dhb, w_ref[p * PAIR:(p + 1) * PAIR, :])
            prod = datt * o_ref[:, p * PAIR:(p + 1) * PAIR].astype(F32)
            for hh in range(2):
                lane, head, aux = _head_lanes(hh)
                delta = jnp.sum(jnp.where(head, prod, 0.0), axis=1, keepdims=True)
                aug = _put_pieces(lane, aux + AUX_BIAS, -delta, jnp.where(head, datt, 0.0))
                da_ref[2 * p + hh] = aug.astype(BF16)
                dat_ref[2 * p + hh] = aug.astype(BF16).T
        dcv = _dot_nt(dhb, w_ref[ATTN_W:, :])
        b, c, x = b_ref[...].astype(F32), c_ref[...].astype(F32), x_ref[...].astype(F32)
        ext_u[0:CONV_HALO, :] = jnp.where(s == nsteps - 1, 0.0, hc_ref[...].astype(F32) * hx_ref[...].astype(F32))
        ext_u[CONV_HALO:CONV_HALO + tm, :] = c * x
        dconv = dcv * b
        ext_d[tm:tm + CONV_HALO, :] = ext_d[0:CONV_HALO, :]
        ext_d[0:tm, :] = dconv
        u = [ext_u[CONV_HALO - 2 + k:CONV_HALO - 2 + k + tm, :] for k in range(3)]
        conv = cw_ref[0:1, :] * u[0] + cw_ref[1:2, :] * u[1] + cw_ref[2:3, :] * u[2]
        du = (cw_ref[2:3, :] * dconv + cw_ref[1:2, :] * ext_d[1:1 + tm, :] + cw_ref[0:1, :] * ext_d[2:2 + tm, :])
        db_ref[:, 0:ch] = (dcv * conv).astype(BF16)
        db_ref[:, ch:2 * ch] = (du * x).astype(BF16)
        db_ref[:, 2 * ch:3 * ch] = (du * c).astype(BF16)
        for k in range(3):
            dw_ref[k] += _rows8(dconv * u[k])

    rev = lambda s: nsteps - 1 - s
    row = lambda n_: pl.BlockSpec((tm, n_), lambda s: (rev(s), 0))
    col = lambda k: pl.BlockSpec((tm, ch), lambda s: (rev(s), k))
    prev = lambda k: pl.BlockSpec((CONV_HALO, ch), lambda s: (_prev_halo(tm, CONV_HALO)(rev(s))[0], k))
    return pl.pallas_call(
        body, name="outproj_conv_bwd", grid=(nsteps,),
        in_specs=[row(d), row(ATTN_W), _const_spec(wout.shape), col(0), col(1), col(2), prev(1), prev(2),
                  _const_spec((8, ch))],
        out_specs=[pl.BlockSpec((N_HEADS, tm, PAIR), lambda s: (0, rev(s), 0)),
                   pl.BlockSpec((N_HEADS, PAIR, tm), lambda s: (0, 0, rev(s))),
                   row(3 * ch), pl.BlockSpec((3, 8, ch), lambda s: (0, 0, 0))],
        out_shape=[jax.ShapeDtypeStruct((N_HEADS, t, PAIR), BF16), jax.ShapeDtypeStruct((N_HEADS, PAIR, t), BF16),
                   jax.ShapeDtypeStruct((t, 3 * ch), BF16), jax.ShapeDtypeStruct((3, 8, ch), F32)],
        scratch_shapes=[pltpu.VMEM((CONV_HALO + tm, ch), F32), pltpu.VMEM((tm + CONV_HALO, ch), F32)],
        compiler_params=_cp(("arbitrary",)),
    )(dh, o, wout, bcx, bcx, bcx, bcx, bcx, conv_w)


def _attn_bwd(q_bwd, do_aug, q_bwd_t, do_aug_t, k_aug, v_aug_t, gblocks, *, tq=1024):
    t = q_bwd.shape[1]
    tq = min(tq, t)
    tk = tq // 2
    nq, nk = t // tq, t // tk
    n_pairs = ATTN_W // PAIR
    n_g = len(gblocks)

    def body(q_ref, do_ref, qt_ref, dot_ref, k_ref, vt_ref, *rest):
        dq_ref, dqx_ref, dk_ref, dkx_ref, dv_ref = rest[n_g:n_g + 5]
        dq_scr = rest[2 * n_g + 5]
        scatter = _Exchange(rest[:n_g], rest[n_g + 5:2 * n_g + 5], *rest[2 * n_g + 6:], gather=False)
        j = pl.program_id(1)

        @pl.when((pl.program_id(0) == 0) & (j == 0))
        def _():
            scatter.start()

        @pl.when(j == 0)
        def _():
            dq_scr[...] = jnp.zeros_like(dq_scr)
        k = [k_ref[0], k_ref[1]]
        vt = [vt_ref[0], vt_ref[1]]

        def step(i, carry, diag, rows=tq, row0=0):
            qs = pl.multiple_of(i * tq + row0, tk)
            if diag:
                row = lax.broadcasted_iota(jnp.int32, (rows, tk), 0)
                col = lax.broadcasted_iota(jnp.int32, (rows, tk), 1)
            out = []
            for hh in range(2):
                dk_a, dv_a = carry[hh]
                q = q_ref[hh, pl.ds(qs, rows), :]
                dov = do_ref[hh, pl.ds(qs, rows), :]
                p = jnp.exp2(_dot_nt(q, k[hh]))
                if diag:
                    p = jnp.where(col + (j * tk - i * tq - row0) <= row, p, 0.0)
                ds = (p * _dot(dov, vt[hh])).astype(BF16)
                dv_a = dv_a + _dot(dot_ref[hh, :, pl.ds(qs, rows)], p.astype(BF16))
                dk_a = dk_a + _dot(qt_ref[hh, :, pl.ds(qs, rows)], ds)
                dq_scr[hh, pl.ds(qs, rows), :] += _dot(ds, k[hh])
                out.append((dk_a, dv_a))
            return tuple(out)

        zero = (jnp.zeros((PAIR, tk), F32), jnp.zeros((PAIR, tk), F32))
        carry = lax.cond(j % 2 == 0, lambda c: step(j // 2, c, True),
                         lambda c: step(j // 2, c, True, rows=tk, row0=tk), (zero, zero))
        full0 = j // 2 + 1
        odd = (nq - full0) % 2
        carry = lax.cond(odd == 1, lambda c: step(full0, c, False), lambda c: c, carry)
        (dk0, dv0), (dk1, dv1) = lax.fori_loop(
            0, (nq - full0) // 2, lambda ii, c: step(full0 + odd + 2 * ii, c, False, rows=2 * tq), carry)
        first_t = lax.broadcasted_iota(jnp.int32, (PAIR, 1), 0) < HEAD_DIM
        first = lax.broadcasted_iota(jnp.int32, (1, PAIR), 1) < HEAD_DIM
        dk_ref[...] = (jnp.where(first_t, dk0, dk1) * (1.0 / LOG2E)).astype(BF16).T
        dkx_ref[...] = jnp.where(first_t, dk1, dk0).T
        dv_ref[...] = jnp.where(first_t, dv0, dv1).astype(BF16).T

        @pl.when(j == nk - 1)
        def _():
            dq_ref[...] = (jnp.where(first, dq_scr[0], dq_scr[1]) * Q_SCALE).astype(BF16)
            dqx_ref[...] = jnp.where(first, dq_scr[1], dq_scr[0])

        @pl.when((pl.program_id(0) == n_pairs - 1) & (j == nk - 1))
        def _():
            scatter.wait()

    resident = lambda: pl.BlockSpec((2, t, PAIR), lambda p, j: (p, 0, 0), pipeline_mode=pl.Buffered(1))
    resident_t = lambda: pl.BlockSpec((2, PAIR, t), lambda p, j: (p, 0, 0), pipeline_mode=pl.Buffered(1))
    kv_in = lambda: pl.BlockSpec((2, tk, PAIR), lambda p, j: (p, j, 0))
    whole = lambda: pl.BlockSpec((t, PAIR), lambda p, j: (0, p))
    tile = lambda: pl.BlockSpec((tk, PAIR), lambda p, j: (j, p))
    b16 = jax.ShapeDtypeStruct((t, ATTN_W), BF16)
    f32 = jax.ShapeDtypeStruct((t, ATTN_W), F32)
    res = pl.pallas_call(
        body, name="attn_bwd", grid=(n_pairs, nk),
        in_specs=[resident(), resident(), resident_t(), resident_t(), kv_in(),
                  pl.BlockSpec((2, PAIR, tk), lambda p, j: (p, 0, j))] + [HBM_SPEC] * n_g,
        out_specs=[whole(), whole(), tile(), tile(), tile()] + [HBM_SPEC] * n_g,
        out_shape=[b16, f32, b16, f32, b16] + [jax.ShapeDtypeStruct(g.shape, g.dtype) for g in gblocks],
        scratch_shapes=[pltpu.VMEM((2, t, PAIR), F32)] + _Exchange.scratch(n_g),
        compiler_params=_cp(("arbitrary", "arbitrary")),
    )(q_bwd, do_aug, q_bwd_t, do_aug_t, k_aug, v_aug_t, *gblocks)
    return res[:5], res[5:]


def _fgate_bwd(dqx, dkx, sgate, *, tm=256):
    t = sgate.shape[0]
    tm = min(tm, t)
    nsteps = t // tm

    def body(dq_ref, dk_ref, sg_ref, df_ref, dbf_ref, carry):
        @pl.when(pl.program_id(0) == 0)
        def _():
            carry[...] = jnp.zeros_like(carry)
            dbf_ref[...] = jnp.zeros_like(dbf_ref)
        lane = lax.broadcasted_iota(jnp.int32, (ATTN_W, F_PAD), 0)
        head = lax.broadcasted_iota(jnp.int32, (ATTN_W, F_PAD), 1)
        aux = (head // 2) * PAIR + HEAD_DIM * (1 - head % 2)
        valid = head < N_HEADS
        pick_r = (valid & (lane == aux + AUX_ROWSUM)).astype(F32)
        pick_c = (valid & (lane == aux + AUX_BIAS)).astype(F32)
        hp = lax.Precision.HIGHEST
        dcum = (jnp.dot(dq_ref[...], pick_r, preferred_element_type=F32, precision=lax.Precision.HIGH)
                + jnp.dot(dk_ref[...], pick_c, preferred_element_type=F32, precision=lax.Precision.HIGH))
        r = lax.broadcasted_iota(jnp.int32, (tm, tm), 0)
        c = lax.broadcasted_iota(jnp.int32, (tm, tm), 1)
        tri = (c >= r).astype(F32)
        rc = jnp.dot(tri, dcum, preferred_element_type=F32, precision=hp) + carry[...]
        carry[...] = rc[0:1, :]
        df = rc * sg_ref[...]
        df_ref[...] = df.astype(BF16)
        dbf_ref[...] += _rows8(df)

    rev = lambda i: nsteps - 1 - i
    return pl.pallas_call(
        body, name="fgate_bwd", grid=(nsteps,),
        in_specs=[pl.BlockSpec((tm, ATTN_W), lambda i: (rev(i), 0)), pl.BlockSpec((tm, ATTN_W), lambda i: (rev(i), 0)),
                  pl.BlockSpec((tm, F_PAD), lambda i: (rev(i), 0))],
        out_specs=[pl.BlockSpec((tm, F_PAD), lambda i: (rev(i), 0)), pl.BlockSpec((8, F_PAD), lambda i: (0, 0))],
        out_shape=[jax.ShapeDtypeStruct((t, F_PAD), BF16), jax.ShapeDtypeStruct((8, F_PAD), F32)],
        scratch_shapes=[pltpu.VMEM((1, F_PAD), F32)],
        compiler_params=_cp(("arbitrary",)),
    )(dqx, dkx, sgate)


def _inproj_bwd(dq, dk, dv, df, dbcx, dh, x, g, win_pt, gblock, *, tm=512):
    t, d = x.shape
    tm = min(tm, t)
    nsteps = t // tm
    n_qkv = 3 * ATTN_W

    def body(dq_ref, dk_ref, dv_ref, df_ref, db_ref, dh_ref, x_ref, g_ref, w_ref, gb_ref, gx_ref, dg_ref, land_ref,
             *sems):
        scatter = _Exchange([gb_ref], [land_ref], *sems, gather=False)

        @pl.when(pl.program_id(0) == 0)
        def _():
            scatter.start()
            dg_ref[...] = jnp.zeros_like(dg_ref)
        dn = _dot(df_ref[...], w_ref[n_qkv:n_qkv + F_PAD, :])
        for k, r in enumerate((dq_ref, dk_ref, dv_ref)):
            dn = dn + _dot(r[...], w_ref[k * ATTN_W:(k + 1) * ATTN_W, :])
        for k in range(3):
            c0 = n_qkv + F_PAD + k * CONV_CH
            dn = dn + _dot(db_ref[:, k * CONV_CH:(k + 1) * CONV_CH], w_ref[c0:c0 + CONV_CH, :])
        dx, dg = _norm_bwd(dn, x_ref[...], g_ref[...])
        gx_ref[...] = dh_ref[...] + dx
        dg_ref[...] += dg

        @pl.when(pl.program_id(0) == nsteps - 1)
        def _():
            scatter.wait()

    row = lambda n_: pl.BlockSpec((tm, n_), lambda i: (i, 0))
    return pl.pallas_call(
        body, name="inproj_bwd", grid=(nsteps,),
        in_specs=[row(ATTN_W), row(ATTN_W), row(ATTN_W), row(F_PAD), row(3 * CONV_CH), row(d), row(d),
                  _const_spec((1, d)), _const_spec(win_pt.shape), HBM_SPEC],
        out_specs=[row(d), pl.BlockSpec((8, d), lambda i: (0, 0)), HBM_SPEC],
        out_shape=[jax.ShapeDtypeStruct((t, d), F32), jax.ShapeDtypeStruct((8, d), F32),
                   jax.ShapeDtypeStruct(gblock.shape, gblock.dtype)],
        scratch_shapes=_Exchange.scratch(1),
        compiler_params=_cp(("arbitrary",)),
    )(dq, dk, dv, df, dbcx, dh, x, g, win_pt, gblock)


LATE = ("w_out_0", "w_up_0", "w_down_0", "pool_w_1", "w_up_1", "w_down_1")


def _local_step(x, target, gains, b_f, conv_w, pool_scale, win_pt, shards):
    d = x.shape[1]
    n0, qkv, flog, bcx, cv = _norm_inproj(x, gains["mix0"], win_pt, conv_w)
    q_aug_t, k_aug, v_aug_t, sgate = _fgate_prep(flog, b_f, qkv)
    att, q_bwd, q_bwd_t, gathered = _attn_fwd(q_aug_t, k_aug, v_aug_t, [shards[n] for n in LATE])
    g = dict(zip(LATE, gathered))
    wout = g["w_out_0"].reshape(d, d)
    wup0, wup1 = g["w_up_0"], g["w_up_1"]
    wdown0, wdown1 = g["w_down_0"].reshape(-1, d), g["w_down_1"].reshape(-1, d)
    n_grp = len(POOL_WINDOWS)
    cg = d // n_grp
    poolw = g["pool_w_1"].reshape(N_DEV, n_grp, cg // N_DEV, cg).transpose(1, 0, 2, 3).reshape(n_grp, cg, cg)
    h1 = _outproj(att, cv, x, wout)
    h2, n1, a0, z0 = _mlp_fwd(h1, gains["ffn0"], wup0, wdown0, name="mlp_fwd0")
    h3, pooled = _pool_fwd(h2, gains["mix1"], poolw, pool_scale)
    loss, dh4, dh4_b, dg_final, n3, a1, z1 = _mlp_fwd_loss(h3, gains["ffn1"], wup1, wdown1, gains["final"], target,
                                                           name="mlp_fwd1")
    f = a1.shape[1]
    fb = f // N_DEV
    dh3, da1, dg_ffn1 = _mlp_bwd(dh4, h3, a1, gains["ffn1"], wup1, wdown1, name="mlp_bwd1")
    dwdown1 = _mm_tn(z1, dh4_b, name="dwdown1", ta=1024, tb=1024, tt=4096, out_dtype=BF16)
    dwup1 = _mm_tn(n3, da1, name="dwup1", ta=d, tb=fb, tt=4096, blocked_out=True, out_dtype=BF16)
    dh2, dh2_b, dpoolw, dscale, dg_mix1 = _pool_bwd(dh3, h2, pooled, gains["mix1"], poolw, pool_scale)
    dh1, da0, dg_ffn0 = _mlp_bwd(dh2, h1, a0, gains["ffn0"], wup0, wdown0, name="mlp_bwd0")
    dwdown0 = _mm_tn(z0, dh2_b, name="dwdown0", ta=1024, tb=1024, tt=4096, out_dtype=BF16)
    dwup0 = _mm_tn(n1, da0, name="dwup0", ta=d, tb=fb, tt=4096, blocked_out=True, out_dtype=BF16)
    do_aug, do_aug_t, dbcx, dconvw = _outproj_conv_bwd(dh1, att, wout, bcx, conv_w)
    dwout = _mm_tn_cat([att, cv], [dh1], name="dwout", tt=2048)
    gblocks = {
        "w_out_0": dwout.reshape(N_DEV, d // N_DEV, d), "w_up_0": dwup0, "w_up_1": dwup1,
        "w_down_0": dwdown0.reshape(N_DEV, -1, d), "w_down_1": dwdown1.reshape(N_DEV, -1, d),
        "pool_w_1": dpoolw.astype(BF16).reshape(n_grp, N_DEV, cg // N_DEV, cg).transpose(1, 0, 2, 3).reshape(
            N_DEV, n_grp * (cg // N_DEV), cg),
    }
    (dq, dqx, dk, dkx, dv), landed = _attn_bwd(q_bwd, do_aug, q_bwd_t, do_aug_t, k_aug, v_aug_t,
                                               [gblocks[n] for n in LATE])
    df, dbf = _fgate_bwd(dqx, dkx, sgate)
    dwin_t = jnp.concatenate(
        [_mm_tn_cat([dq, dk, dv], [n0], name="dwin_qkv", tt=2048),
         _mm_tn(df, n0, name="dwin_f", ta=F_PAD, tb=d, tt=2048, out_dtype=BF16)[:N_HEADS],
         _mm_tn(dbcx, n0, name="dwin_bcx", ta=512, tb=d, tt=4096, out_dtype=BF16)], axis=0)
    dwin_blocks = dwin_t.reshape(N_DEV, dwin_t.shape[0] // N_DEV, d)
    grad_x, dg_mix0, landed_win = _inproj_bwd(dq, dk, dv, df, dbcx, dh1, x, gains["mix0"], win_pt, dwin_blocks)
    small = dict(mix0=dg_mix0, ffn0=dg_ffn0, mix1=dg_mix1, pool_scale=dscale, ffn1=dg_ffn1, final=dg_final,
                 b_f=dbf, conv_w=dconvw)
    return loss, grad_x, dict(zip(LATE + ("w_in_0",), tuple(landed) + (landed_win,))), small


def _all_gather(shards):
    n = len(shards)

    def body(*refs):
        gather = _TwoLevelGather(refs[:n], refs[n:2 * n], *refs[2 * n:])
        gather.start()
        gather.forward()
        gather.wait()

    return pl.pallas_call(
        body, name="all_gather",
        in_specs=[HBM_SPEC] * n, out_specs=[HBM_SPEC] * n,
        out_shape=[jax.ShapeDtypeStruct((N_DEV,) + s.shape, s.dtype) for s in shards],
        scratch_shapes=[pltpu.SemaphoreType.DMA((7 * n,)), pltpu.SemaphoreType.DMA((7 * n,)),
                        pltpu.SemaphoreType.DMA((n,))],
    )(*shards)


SMALL_ROWS = 16


def _small_allreduce(parts):
    n, _, w = parts.shape
    assert n <= SMALL_ROWS

    def body(p_ref, o_ref, gath, send_sems, recv_sems):
        x, y, c = lax.axis_index("x"), lax.axis_index("y"), lax.axis_index("c")
        my = _slot(x, y, c)
        rows = [jnp.sum(p_ref[i], axis=0, keepdims=True) for i in range(n)]
        rows.append(jnp.zeros((SMALL_ROWS - n, w), F32))
        gath[my] = jnp.concatenate(rows, axis=0)
        copies = []
        for k in range(1, N_DEV):
            px, py, pc = x ^ (k >> 2), y ^ ((k >> 1) & 1), c ^ (k & 1)
            cp = pltpu.make_async_remote_copy(
                src_ref=gath.at[my], dst_ref=gath.at[my], send_sem=send_sems.at[k - 1], recv_sem=recv_sems.at[k - 1],
                device_id=(px, py, pc), device_id_type=MESH)
            cp.start()
            copies.append(cp)
        for cp in copies:
            cp.wait()
        acc = gath[0]
        for d in range(1, N_DEV):
            acc = acc + gath[d]
        o_ref[...] = acc

    return pl.pallas_call(
        body, name="small_allreduce",
        in_specs=[VMEM_SPEC], out_specs=VMEM_SPEC,
        out_shape=jax.ShapeDtypeStruct((SMALL_ROWS, w), F32),
        scratch_shapes=[pltpu.VMEM((N_DEV, SMALL_ROWS, w), F32), pltpu.SemaphoreType.DMA((N_DEV - 1,)),
                        pltpu.SemaphoreType.DMA((N_DEV - 1,))],
    )(parts)


def _adamw(g, w, m, v, *, name, tm=256):
    r, c = g.shape
    tm = tm if r % tm == 0 else r
    bc1 = 1.0 - ADAM_B1 ** ADAM_STEP
    bc2 = 1.0 - ADAM_B2 ** ADAM_STEP

    def body(g_ref, w_ref, m_ref, v_ref, d_ref, nm_ref, nv_ref):
        gv = g_ref[...]
        nm = ADAM_B1 * m_ref[...] + (1.0 - ADAM_B1) * gv
        nv = ADAM_B2 * v_ref[...] + (1.0 - ADAM_B2) * jnp.square(gv)
        nm_ref[...] = nm
        nv_ref[...] = nv
        d_ref[...] = -ADAM_LR * ((nm / bc1) / (jnp.sqrt(nv / bc2) + ADAM_EPS) + ADAM_WD * w_ref[...])

    blk = pl.BlockSpec((tm, c), lambda i: (i, 0))
    shp = jax.ShapeDtypeStruct((r, c), F32)
    return pl.pallas_call(
        body, name=name, grid=(r // tm,), in_specs=[blk] * 4, out_specs=[blk] * 3, out_shape=[shp] * 3,
        compiler_params=_cp(("parallel",)),
    )(g, w, m, v)


def _transpose_cast(a, *, name):
    def body(a_ref, o_ref):
        o_ref[...] = a_ref[...].T.astype(BF16)

    return pl.pallas_call(body, name=name, out_shape=jax.ShapeDtypeStruct(a.shape[::-1], BF16),
                          compiler_params=_cp())(a)


def _adamw_sum_t(parts, w, m, v, *, name):
    bc1 = 1.0 - ADAM_B1 ** ADAM_STEP
    bc2 = 1.0 - ADAM_B2 ** ADAM_STEP

    def body(p_ref, w_ref, m_ref, v_ref, g_ref, d_ref, nm_ref, nv_ref):
        acc = p_ref[0].astype(F32)
        for k in range(1, N_DEV):
            acc = acc + p_ref[k].astype(F32)
        gv = acc.T
        g_ref[...] = gv
        nm = ADAM_B1 * m_ref[...] + (1.0 - ADAM_B1) * gv
        nv = ADAM_B2 * v_ref[...] + (1.0 - ADAM_B2) * jnp.square(gv)
        nm_ref[...] = nm
        nv_ref[...] = nv
        d_ref[...] = -ADAM_LR * ((nm / bc1) / (jnp.sqrt(nv / bc2) + ADAM_EPS) + ADAM_WD * w_ref[...])

    shp = jax.ShapeDtypeStruct(w.shape, F32)
    return pl.pallas_call(body, name=name, out_shape=[shp] * 4, compiler_params=_cp())(parts, w, m, v)


def _adamw_sum(parts, w, m, v, *, name, tm=512):
    _, r, c = parts.shape
    tm = tm if r % tm == 0 else r
    bc1 = 1.0 - ADAM_B1 ** ADAM_STEP
    bc2 = 1.0 - ADAM_B2 ** ADAM_STEP

    def body(p_ref, w_ref, m_ref, v_ref, g_ref, d_ref, nm_ref, nv_ref):
        gv = p_ref[0].astype(F32)
        for k in range(1, N_DEV):
            gv = gv + p_ref[k].astype(F32)
        g_ref[...] = gv
        nm = ADAM_B1 * m_ref[...] + (1.0 - ADAM_B1) * gv
        nv = ADAM_B2 * v_ref[...] + (1.0 - ADAM_B2) * jnp.square(gv)
        nm_ref[...] = nm
        nv_ref[...] = nv
        d_ref[...] = -ADAM_LR * ((nm / bc1) / (jnp.sqrt(nv / bc2) + ADAM_EPS) + ADAM_WD * w_ref[...])

    blk = pl.BlockSpec((tm, c), lambda i: (i, 0))
    shp = jax.ShapeDtypeStruct((r, c), F32)
    return pl.pallas_call(
        body, name=name, grid=(r // tm,), in_specs=[pl.BlockSpec((N_DEV, tm, c), lambda i: (0, i, 0))] + [blk] * 3,
        out_specs=[blk] * 4, out_shape=[shp] * 4, compiler_params=_cp(("parallel",)),
    )(parts, w, m, v)


BIG = ("w_in_0", "w_out_0", "w_up_0", "w_down_0", "pool_w_1", "w_up_1", "w_down_1")
SMALL = ("norm_mix_0", "norm_ffn_0", "norm_mix_1", "pool_scale_1", "norm_ffn_1", "final_norm", "b_f_0", "conv_w_0")
WEIGHTS = ("norm_mix_0", "w_in_0", "b_f_0", "conv_w_0", "w_out_0", "norm_ffn_0", "w_up_0", "w_down_0", "norm_mix_1",
           "pool_w_1", "pool_scale_1", "norm_ffn_1", "w_up_1", "w_down_1", "final_norm")


def _pad_to(a, rows, cols):
    return jnp.pad(a, ((0, rows - a.shape[0]), (0, cols - a.shape[1])))


def _pack_small(p, width):
    rows = [p[n].reshape(1, -1) for n in SMALL[:6]]
    rows.append(_pad_to(p["b_f_0"].reshape(1, -1), 1, width))
    rows.append(_pad_to(p["conv_w_0"], 3, width))
    return _pad_to(jnp.concatenate(rows, axis=0), SMALL_ROWS, width)


def _unpack_small(a, like):
    out = {n: a[i] for i, n in enumerate(SMALL[:6])}
    out["b_f_0"] = a[6, :like["b_f_0"].shape[0]]
    out["conv_w_0"] = a[7:10, :like["conv_w_0"].shape[1]]
    return out


def kernel(x, norm_mix_0, w_in_0, b_f_0, conv_w_0, w_out_0, norm_ffn_0, w_up_0, w_down_0, norm_mix_1, pool_w_1, pool_scale_1, norm_ffn_1, w_up_1, w_down_1, final_norm, loss_target, m_norm_mix_0, m_w_in_0, m_b_f_0, m_conv_w_0, m_w_out_0, m_norm_ffn_0, m_w_up_0, m_w_down_0, m_norm_mix_1, m_pool_w_1, m_pool_scale_1, m_norm_ffn_1, m_w_up_1, m_w_down_1, m_final_norm, v_norm_mix_0, v_w_in_0, v_b_f_0, v_conv_w_0, v_w_out_0, v_norm_ffn_0, v_w_up_0, v_w_down_0, v_norm_mix_1, v_pool_w_1, v_pool_scale_1, v_norm_ffn_1, v_w_up_1, v_w_down_1, v_final_norm):
    w = dict(norm_mix_0=norm_mix_0, w_in_0=w_in_0, b_f_0=b_f_0, conv_w_0=conv_w_0, w_out_0=w_out_0,
             norm_ffn_0=norm_ffn_0, w_up_0=w_up_0, w_down_0=w_down_0, norm_mix_1=norm_mix_1, pool_w_1=pool_w_1,
             pool_scale_1=pool_scale_1, norm_ffn_1=norm_ffn_1, w_up_1=w_up_1, w_down_1=w_down_1, final_norm=final_norm)
    m = dict(norm_mix_0=m_norm_mix_0, w_in_0=m_w_in_0, b_f_0=m_b_f_0, conv_w_0=m_conv_w_0, w_out_0=m_w_out_0,
             norm_ffn_0=m_norm_ffn_0, w_up_0=m_w_up_0, w_down_0=m_w_down_0, norm_mix_1=m_norm_mix_1,
             pool_w_1=m_pool_w_1, pool_scale_1=m_pool_scale_1, norm_ffn_1=m_norm_ffn_1, w_up_1=m_w_up_1,
             w_down_1=m_w_down_1, final_norm=m_final_norm)
    v = dict(norm_mix_0=v_norm_mix_0, w_in_0=v_w_in_0, b_f_0=v_b_f_0, conv_w_0=v_conv_w_0, w_out_0=v_w_out_0,
             norm_ffn_0=v_norm_ffn_0, w_up_0=v_w_up_0, w_down_0=v_w_down_0, norm_mix_1=v_norm_mix_1,
             pool_w_1=v_pool_w_1, pool_scale_1=v_pool_scale_1, norm_ffn_1=v_norm_ffn_1, w_up_1=v_w_up_1,
             w_down_1=v_w_down_1, final_norm=v_final_norm)
    d = x.shape[-1]
    n_in = w_in_0.shape[1] * N_DEV
    n_qkv = 3 * ATTN_W
    pool_g, pool_rows, pool_c = pool_w_1.shape

    def shard2d(p):
        return {n: (p[n].reshape(pool_g * pool_rows, pool_c) if n == "pool_w_1" else p[n]) for n in BIG}
    w2, m2, v2 = shard2d(w), shard2d(m), shard2d(v)

    conv_cols = conv_w_0.shape[1]
    win_g8, conv_g8 = _all_gather([_transpose_cast(w_in_0, name="w_in_t"), _pad_to(conv_w_0, 8, 128)])
    conv_full = conv_g8[:, :, :conv_cols].transpose(1, 0, 2).reshape(8, N_DEV * conv_cols)
    win_t = win_g8.reshape(n_in, d)
    win_pt = jnp.concatenate([win_t[:n_qkv], _pad_to(win_t[n_qkv:n_qkv + N_HEADS], F_PAD, d),
                              win_t[n_qkv + N_HEADS:]], axis=0)

    gains = dict(mix0=norm_mix_0.reshape(1, d), ffn0=norm_ffn_0.reshape(1, d), mix1=norm_mix_1.reshape(1, d),
                 ffn1=norm_ffn_1.reshape(1, d), final=final_norm.reshape(1, d))
    dev = _slot(lax.axis_index("x"), lax.axis_index("y"), lax.axis_index("c"))
    loss8, grad_x, landed, small = _local_step(
        x[0], loss_target[0], gains, _pad_to(b_f_0.reshape(1, -1), 1, F_PAD), conv_full, pool_scale_1.reshape(1, d),
        win_pt, {n: w2[n].astype(BF16) for n in LATE})
    parts = jnp.concatenate(
        [small[k][None] for k in ("mix0", "ffn0", "mix1", "pool_scale", "ffn1", "final")]
        + [_pad_to(small["b_f"], 8, d)[None], jnp.pad(small["conv_w"], ((0, 0), (0, 0), (0, d - CONV_CH))),
           _pad_to(loss8[0:1, 0:1], 8, d)[None]], axis=0)
    tot = _small_allreduce(parts)
    loss = tot[10, 0]
    conv_g = lax.dynamic_slice(tot, (7, dev * conv_cols), (3, conv_cols))
    gs = tot.at[7:10].set(_pad_to(conv_g, 3, d))

    grads, deltas, new_m, new_v = {}, {}, {}, {}
    for n in BIG:
        if n in LATE:
            gr, dl, nm, nv = _adamw_sum(landed[n], w2[n], m2[n], v2[n], name="adamw_" + n)
        else:
            gr, dl, nm, nv = _adamw_sum_t(landed[n], w2[n], m2[n], v2[n], name="adamw_" + n)
        for dst, val in ((grads, gr), (deltas, dl), (new_m, nm), (new_v, nv)):
            dst[n] = val.reshape(w[n].shape)
    dl, nm, nv = _adamw(gs, _pack_small(w, d), _pack_small(m, d), _pack_small(v, d), name="adamw_small")
    for dst, val in ((grads, gs), (deltas, dl), (new_m, nm), (new_v, nv)):
        dst.update(_unpack_small(val, w))
    return (loss, grad_x[None], *[grads[n] for n in WEIGHTS], *[deltas[n] for n in WEIGHTS],
            *[new_m[n] for n in WEIGHTS], *[new_v[n] for n in WEIGHTS])
```

```python
import functools

import jax
import jax.numpy as jnp
from jax import lax
from jax.experimental import pallas as pl
from jax.experimental.pallas import tpu as pltpu

F32 = jnp.float32
BF16 = jnp.bfloat16

N_DEV = 8
N_HEADS = 8
HEAD_DIM = 64
PAIR = 2 * HEAD_DIM
ATTN_W = N_HEADS * HEAD_DIM
CONV_CH = 512
F_PAD = 128
POOL_WINDOWS = (2, 4, 8, 16)
POOL_HALO = 16
CONV_HALO = 16
RMS_EPS = 1e-6
Q_SCALE = HEAD_DIM ** -0.5
LOG2E = 1.4426950408889634
NEG = -1e30
AUX_BIAS = 0
AUX_LSE = 3
AUX_ROWSUM = 6
ADAM_LR, ADAM_B1, ADAM_B2, ADAM_EPS, ADAM_WD, ADAM_STEP = 0.001, 0.9, 0.999, 1e-08, 0.01, 10
MESH = pl.DeviceIdType.MESH
VMEM_LIMIT = 56 * 2**20


def _cp(sem=None, vmem=VMEM_LIMIT, **kw):
    return pltpu.CompilerParams(dimension_semantics=sem, vmem_limit_bytes=vmem, **kw)


def _dot(a, b):
    return jnp.dot(a, b, preferred_element_type=F32)


def _dot_nt(a, b):
    return lax.dot_general(a, b, (((1,), (1,)), ((), ())), preferred_element_type=F32)


def _dot_tn(a, b):
    return lax.dot_general(a, b, (((0,), (0,)), ((), ())), preferred_element_type=F32)


def _rstd(h):
    return lax.rsqrt(jnp.mean(h * h, axis=-1, keepdims=True) + RMS_EPS)


def _rows8(x):
    r, n = x.shape
    return jnp.sum(x.reshape(r // 8, 8, n), axis=0)


def _norm_bwd(dn, h, g):
    r = _rstd(h)
    xhat = h * r
    dy = dn * g
    dh = r * (dy - xhat * jnp.mean(dy * xhat, axis=-1, keepdims=True))
    return dh, _rows8(dn * xhat)


def _const_spec(shape):
    nd = len(shape)
    return pl.BlockSpec(shape, lambda *_: (0,) * nd, pipeline_mode=pl.Buffered(1))


HBM_SPEC = pl.BlockSpec(memory_space=pltpu.HBM)
VMEM_SPEC = pl.BlockSpec(memory_space=pltpu.VMEM)


def _slot(px, py, pc):
    return 4 * px + 2 * py + pc


class _Exchange:
    def __init__(self, srcs, dsts, send_sems, recv_sems, local_sems, gather):
        x, y, c = lax.axis_index("x"), lax.axis_index("y"), lax.axis_index("c")
        me = _slot(x, y, c)
        self.copies = []
        for a, (src, dst) in enumerate(zip(srcs, dsts)):
            self.copies.append(pltpu.make_async_copy(src if gather else src.at[me], dst.at[me], local_sems.at[a]))
            for k in range(1, N_DEV):
                px, py, pc = x ^ (k >> 2), y ^ ((k >> 1) & 1), c ^ (k & 1)
                self.copies.append(pltpu.make_async_remote_copy(
                    src_ref=src if gather else src.at[_slot(px, py, pc)], dst_ref=dst.at[me],
                    send_sem=send_sems.at[(N_DEV - 1) * a + k - 1], recv_sem=recv_sems.at[(N_DEV - 1) * a + k - 1],
                    device_id=(px, py, pc), device_id_type=MESH))

    def start(self):
        for cp in self.copies:
            cp.start()

    def wait(self):
        for cp in self.copies:
            cp.wait()

    @staticmethod
    def scratch(n):
        return [pltpu.SemaphoreType.DMA(((N_DEV - 1) * n,)), pltpu.SemaphoreType.DMA(((N_DEV - 1) * n,)),
                pltpu.SemaphoreType.DMA((n,))]


def _mesh_places():
    x, y, c = lax.axis_index("x"), lax.axis_index("y"), lax.axis_index("c")
    chips = [(1 - x, y), (x, 1 - y), (1 - x, 1 - y)]
    return (x, y, c), (x, y, 1 - c), chips


class _TwoLevelGather:
    def __init__(self, srcs, dsts, send_sems, recv_sems, local_sems):
        me, sib, chips = _mesh_places()
        c = me[2]
        n = len(srcs)

        def copy(a, k, block, to, src=None):
            dst = dsts[a].at[_slot(*block)]
            return pltpu.make_async_remote_copy(
                src_ref=dst if src is None else src, dst_ref=dst, send_sem=send_sems.at[7 * a + k],
                recv_sem=recv_sems.at[7 * a + k], device_id=to, device_id_type=MESH)

        self.mine = [pltpu.make_async_copy(srcs[a], dsts[a].at[_slot(*me)], local_sems.at[a]) for a in range(n)]
        self.first, self.landed, self.passed, self.rest = [], [], [], []
        for a in range(n):
            self.first.append(copy(a, 0, me, sib, src=srcs[a]))
            self.first += [copy(a, 1 + j, me, (*chip, c), src=srcs[a]) for j, chip in enumerate(chips)]
            self.landed += [copy(a, 1 + j, (*chip, c), me) for j, chip in enumerate(chips)]
            self.passed += [copy(a, 4 + j, (*chip, c), sib) for j, chip in enumerate(chips)]
            self.rest.append(copy(a, 0, sib, me))
            self.rest += [copy(a, 4 + j, (*chip, 1 - c), me) for j, chip in enumerate(chips)]

    def start(self):
        for cp in self.mine + self.first:
            cp.start()

    def forward(self):
        for arrived, onward in zip(self.landed, self.passed):
            arrived.wait_recv()
            onward.start()

    def wait(self):
        for cp in self.rest:
            cp.wait_recv()
        for cp in self.first + self.passed:
            cp.wait_send()
        for cp in self.mine:
            cp.wait()


def _norm_inproj(x, g, win_pt, conv_w, *, tm=512):
    t, d = x.shape
    n_all = win_pt.shape[0]
    n_qkv = 3 * ATTN_W
    n_bcx = 3 * CONV_CH
    assert n_all == n_qkv + F_PAD + n_bcx
    tm = min(tm, t)
    ch = CONV_CH

    def body(x_ref, g_ref, w_ref, cw_ref, n_ref, qkv_ref, f_ref, bcx_ref, cv_ref, ext):
        h = x_ref[...]
        n = (h * _rstd(h) * g_ref[...]).astype(BF16)
        n_ref[...] = n
        for c0 in range(0, n_qkv, 512):
            acc = _dot_nt(n, w_ref[c0:c0 + 512, :])
            if c0 < ATTN_W:
                acc = acc * (Q_SCALE * LOG2E)
            qkv_ref[:, c0:c0 + 512] = acc.astype(BF16)
        f_ref[...] = _dot_nt(n, w_ref[n_qkv:n_qkv + F_PAD, :])
        bcx = []
        for k in range(3):
            c0 = n_qkv + F_PAD + k * ch
            v = _dot_nt(n, w_ref[c0:c0 + ch, :]).astype(BF16)
            bcx_ref[:, k * ch:(k + 1) * ch] = v
            bcx.append(v.astype(F32))
        @pl.when(pl.program_id(0) == 0)
        def _():
            ext[tm:tm + CONV_HALO, :] = jnp.zeros((CONV_HALO, ch), F32)
        ext[0:CONV_HALO, :] = ext[tm:tm + CONV_HALO, :]
        ext[CONV_HALO:CONV_HALO + tm, :] = bcx[1] * bcx[2]
        conv = (cw_ref[0:1, :] * ext[CONV_HALO - 2:CONV_HALO - 2 + tm, :]
                + cw_ref[1:2, :] * ext[CONV_HALO - 1:CONV_HALO - 1 + tm, :]
                + cw_ref[2:3, :] * ext[CONV_HALO:CONV_HALO + tm, :])
        cv_ref[...] = (bcx[0] * conv).astype(BF16)

    return pl.pallas_call(
        body, name="norm_inproj", grid=(t // tm,),
        in_specs=[pl.BlockSpec((tm, d), lambda i: (i, 0)), _const_spec((1, d)), _const_spec((n_all, d)),
                  _const_spec((8, ch))],
        out_specs=[pl.BlockSpec((tm, d), lambda i: (i, 0)), pl.BlockSpec((tm, n_qkv), lambda i: (i, 0)),
                   pl.BlockSpec((tm, F_PAD), lambda i: (i, 0)), pl.BlockSpec((tm, n_bcx), lambda i: (i, 0)),
                   pl.BlockSpec((tm, ch), lambda i: (i, 0))],
        out_shape=[jax.ShapeDtypeStruct((t, d), BF16), jax.ShapeDtypeStruct((t, n_qkv), BF16),
                   jax.ShapeDtypeStruct((t, F_PAD), F32), jax.ShapeDtypeStruct((t, n_bcx), BF16),
                   jax.ShapeDtypeStruct((t, ch), BF16)],
        scratch_shapes=[pltpu.VMEM((CONV_HALO + tm, ch), F32)],
        compiler_params=_cp(("arbitrary",)),
    )(x, g, win_pt, conv_w)


def _head_lanes(h):
    lane = lax.broadcasted_iota(jnp.int32, (1, PAIR), 1)
    hh = h % 2
    return lane, lane // HEAD_DIM == hh, HEAD_DIM * (1 - hh)


def _pieces(col):
    hi = col.astype(BF16).astype(F32)
    r1 = col - hi
    mid = r1.astype(BF16).astype(F32)
    lo = (r1 - mid).astype(BF16).astype(F32)
    return hi, mid, lo


def _put_pieces(lane, first, col, other):
    hi, mid, lo = _pieces(col)
    return jnp.where(lane == first, hi, jnp.where(lane == first + 1, mid, jnp.where(lane == first + 2, lo, other)))


def _fgate_prep(flog, b_f, qkv, *, tm=512):
    t = flog.shape[0]
    tm = min(tm, t)

    def body(f_ref, b_ref, qkv_ref, qat_ref, ka_ref, vat_ref, sg_ref, carry):
        @pl.when(pl.program_id(0) == 0)
        def _():
            carry[...] = jnp.zeros_like(carry)
        z = f_ref[...] + b_ref[...]
        e = jnp.exp(-jnp.abs(z))
        logf = jnp.minimum(z, 0.0) - jnp.log(1.0 + e)
        sg_ref[...] = jnp.where(z >= 0, e, 1.0) / (1.0 + e)
        r = lax.broadcasted_iota(jnp.int32, (tm, tm), 0)
        c = lax.broadcasted_iota(jnp.int32, (tm, tm), 1)
        tri = (c <= r).astype(F32)
        cs = jnp.dot(tri, logf, preferred_element_type=F32, precision=lax.Precision.HIGHEST) + carry[...]
        carry[...] = cs[tm - 1:tm, :]
        cs2 = cs * LOG2E
        for h in range(N_HEADS):
            lane, head, aux = _head_lanes(h)
            p0 = (h // 2) * PAIR
            ones = ((lane >= aux + AUX_LSE) & (lane <= aux + AUX_ROWSUM)).astype(F32)
            bias = (lane >= aux + AUX_BIAS) & (lane < aux + AUX_BIAS + 3)
            k_aux = _put_pieces(lane, aux + AUX_BIAS, cs2[:, h:h + 1], ones)
            q_aug = jnp.where(head, qkv_ref[:, p0:p0 + PAIR], jnp.where(bias, -1.0, 0.0).astype(BF16))
            v_aug = jnp.where(head, qkv_ref[:, 2 * ATTN_W + p0:2 * ATTN_W + p0 + PAIR],
                              jnp.where(bias, 1.0, 0.0).astype(BF16))
            qat_ref[h] = q_aug.T
            ka_ref[h] = jnp.where(head, qkv_ref[:, ATTN_W + p0:ATTN_W + p0 + PAIR], k_aux.astype(BF16))
            vat_ref[h] = v_aug.T

    aug = lambda: pl.BlockSpec((N_HEADS, tm, PAIR), lambda i: (0, i, 0))
    aug_t = lambda: pl.BlockSpec((N_HEADS, PAIR, tm), lambda i: (0, 0, i))
    aug_shape = jax.ShapeDtypeStruct((N_HEADS, t, PAIR), BF16)
    aug_t_shape = jax.ShapeDtypeStruct((N_HEADS, PAIR, t), BF16)
    return pl.pallas_call(
        body, name="fgate_prep", grid=(t // tm,),
        in_specs=[pl.BlockSpec((tm, F_PAD), lambda i: (i, 0)), _const_spec((1, F_PAD)),
                  pl.BlockSpec((tm, 3 * ATTN_W), lambda i: (i, 0))],
        out_specs=[aug_t(), aug(), aug_t(), pl.BlockSpec((tm, F_PAD), lambda i: (i, 0))],
        out_shape=[aug_t_shape, aug_shape, aug_t_shape, jax.ShapeDtypeStruct((t, F_PAD), F32)],
        scratch_shapes=[pltpu.VMEM((1, F_PAD), F32)],
        compiler_params=_cp(("arbitrary",)),
    )(flog, b_f, qkv)


def _put_pieces_t(row, first, vec, other):
    hi, mid, lo = _pieces(vec)
    return jnp.where(row == first, hi, jnp.where(row == first + 1, mid, jnp.where(row == first + 2, lo, other)))


def _attn_fwd(q_aug_t, k_aug, v_aug_t, shards, *, tq=1024):
    t = k_aug.shape[1]
    tq = min(tq, t)
    tk = tq // 2
    nq = t // tq
    n_pairs = ATTN_W // PAIR
    n_sh = len(shards)
    forward_step = (3 * n_pairs * nq) // 4

    def body(qt_ref, k_ref, vt_ref, *rest):
        o_ref, qb_ref, qbt_ref = rest[n_sh:n_sh + 3]
        s_scr = rest[2 * n_sh + 3]
        gather = _TwoLevelGather(rest[:n_sh], rest[n_sh + 3:2 * n_sh + 3], *rest[2 * n_sh + 4:])
        i = pl.program_id(1)
        step = pl.program_id(0) * nq + i

        @pl.when(step == 0)
        def _():
            gather.start()

        @pl.when(step == forward_step)
        def _():
            gather.forward()
        key = lax.broadcasted_iota(jnp.int32, (tk, tq), 0)
        qry = lax.broadcasted_iota(jnp.int32, (tk, tq), 1)
        qt = [qt_ref[0], qt_ref[1]]

        def logits(hh, tile, slot, diag):
            s = _dot(k_ref[hh, pl.ds(pl.multiple_of(tile * tk, tk), tk), :], qt[hh])
            if diag:
                s = jnp.where(key + (tile * tk - i * tq) <= qry, s, NEG)
            s_scr[hh, slot] = s
            return jnp.max(s, axis=0, keepdims=True)

        def probs(hh, tile, slot, m, acc, tmax):
            mn = jnp.maximum(m, tmax)
            p = jnp.exp2(s_scr[hh, slot] - mn).astype(BF16)
            acc = jnp.exp2(m - mn) * acc + _dot(vt_ref[hh, :, pl.ds(pl.multiple_of(tile * tk, tk), tk)], p)
            return mn, acc

        def advance(carry, prev, slot, nxt, diag=False):
            out = []
            for hh in range(2):
                m, acc, tmax = carry[hh]
                m, acc = probs(hh, prev, slot, m, acc, tmax)
                out.append((m, acc, logits(hh, nxt, 1 - slot, diag)))
            return tuple(out)

        def two_tiles(jj, carry):
            carry = advance(carry, jnp.where(jj == 0, 2 * i, 2 * jj - 1), 1, 2 * jj)
            return advance(carry, 2 * jj, 0, 2 * jj + 1)

        init = tuple((jnp.full((1, tq), NEG, F32), jnp.zeros((PAIR, tq), F32), logits(hh, 2 * i + 1, 0, True))
                     for hh in range(2))
        carry = advance(init, 2 * i + 1, 0, 2 * i, diag=True)
        carry = lax.fori_loop(0, i // 2, lambda jj, c: two_tiles(2 * jj + 1, two_tiles(2 * jj, c)), carry)
        carry = lax.cond(i % 2 == 1, lambda c: two_tiles(i - 1, c), lambda c: c, carry)
        last = jnp.where(i == 0, 2 * i, 2 * i - 1)
        row = lax.broadcasted_iota(jnp.int32, (PAIR, 1), 0)
        res = []
        for hh in range(2):
            aux = HEAD_DIM * (1 - hh)
            m, acc, tmax = carry[hh]
            m, acc = probs(hh, last, 1, m, acc, tmax)
            l = acc[aux + AUX_BIAS:aux + AUX_BIAS + 1, :]
            qbt = _put_pieces_t(row, aux + AUX_LSE, -(m + jnp.log2(l)), qt[hh].astype(F32))
            qbt_ref[hh] = qbt.astype(BF16)
            qb_ref[hh] = qbt.astype(BF16).T
            res.append(acc * (1.0 / l))
        o_ref[...] = jnp.where(row < HEAD_DIM, res[0], res[1]).astype(BF16).T

        @pl.when((pl.program_id(0) == n_pairs - 1) & (i == nq - 1))
        def _():
            gather.wait()

    res = pl.pallas_call(
        body, name="attn_fwd", grid=(n_pairs, nq),
        in_specs=[pl.BlockSpec((2, PAIR, tq), lambda p, i: (p, 0, i)),
                  pl.BlockSpec((2, t, PAIR), lambda p, i: (p, 0, 0)),
                  pl.BlockSpec((2, PAIR, t), lambda p, i: (p, 0, 0))] + [HBM_SPEC] * n_sh,
        out_specs=[pl.BlockSpec((tq, PAIR), lambda p, i: (i, p)),
                   pl.BlockSpec((2, tq, PAIR), lambda p, i: (p, i, 0)),
                   pl.BlockSpec((2, PAIR, tq), lambda p, i: (p, 0, i))] + [HBM_SPEC] * n_sh,
        out_shape=[jax.ShapeDtypeStruct((t, ATTN_W), BF16), jax.ShapeDtypeStruct((N_HEADS, t, PAIR), BF16),
                   jax.ShapeDtypeStruct((N_HEADS, PAIR, t), BF16)]
        + [jax.ShapeDtypeStruct((N_DEV,) + s.shape, s.dtype) for s in shards],
        scratch_shapes=[pltpu.VMEM((2, 2, tk, tq), F32)] + _Exchange.scratch(n_sh),
        compiler_params=_cp(("arbitrary", "arbitrary")),
    )(q_aug_t, k_aug, v_aug_t, *shards)
    return res[0], res[1], res[2], res[3:]


def _prev_halo(tm, halo):
    return lambda i: (jnp.maximum(i * (tm // halo) - 1, 0), 0)


def _next_halo(tm, halo, t):
    return lambda i: (jnp.minimum((i + 1) * (tm // halo), t // halo - 1), 0)


def _mlp_tile(hh, g_ref, wu_ref, wd_ref, n_ref, a_ref, z_ref):
    n_blk, _, fb = wu_ref.shape
    n = (hh * _rstd(hh) * g_ref[...]).astype(BF16)
    n_ref[...] = n
    acc = hh
    for k in range(n_blk):
        a = _dot(n, wu_ref[k])
        zz = jnp.square(jnp.maximum(a, 0.0)).astype(BF16)
        a_ref[:, k * fb:(k + 1) * fb] = a.astype(BF16)
        z_ref[:, k * fb:(k + 1) * fb] = zz
        acc = acc + _dot(zz, wd_ref[k * fb:(k + 1) * fb, :])
    return acc


def _outproj(att, cv, x, wout, *, tm=512):
    t, d = x.shape
    tm = min(tm, t)

    def body(a_ref, c_ref, x_ref, w_ref, h_ref):
        h_ref[...] = x_ref[...] + _dot(a_ref[...], w_ref[0:ATTN_W, :]) + _dot(c_ref[...], w_ref[ATTN_W:, :])

    return pl.pallas_call(
        body, name="outproj", grid=(t // tm,),
        in_specs=[pl.BlockSpec((tm, ATTN_W), lambda i: (i, 0)), pl.BlockSpec((tm, CONV_CH), lambda i: (i, 0)),
                  pl.BlockSpec((tm, d), lambda i: (i, 0)), _const_spec(wout.shape)],
        out_specs=pl.BlockSpec((tm, d), lambda i: (i, 0)),
        out_shape=jax.ShapeDtypeStruct((t, d), F32),
        compiler_params=_cp(("parallel",)),
    )(att, cv, x, wout)


def _mlp_fwd(h, g, wup, wdown, *, name, tm=512):
    t, d = h.shape
    n_blk, _, fb = wup.shape
    f = n_blk * fb
    tm = min(tm, t)

    def body(h_ref, g_ref, wu_ref, wd_ref, ho_ref, n_ref, a_ref, z_ref):
        ho_ref[...] = _mlp_tile(h_ref[...], g_ref, wu_ref, wd_ref, n_ref, a_ref, z_ref)

    row = lambda n_: pl.BlockSpec((tm, n_), lambda i: (i, 0))
    return pl.pallas_call(
        body, name=name, grid=(t // tm,),
        in_specs=[row(d), _const_spec((1, d)), _const_spec(wup.shape), _const_spec(wdown.shape)],
        out_specs=[row(d), row(d), row(f), row(f)],
        out_shape=[jax.ShapeDtypeStruct((t, d), F32), jax.ShapeDtypeStruct((t, d), BF16),
                   jax.ShapeDtypeStruct((t, f), BF16), jax.ShapeDtypeStruct((t, f), BF16)],
        compiler_params=_cp(("parallel",)),
    )(h, g, wup, wdown)


def _mlp_fwd_loss(h, g, wup, wdown, g_out, target, *, name, tm=512):
    t, d = h.shape
    n_blk, _, fb = wup.shape
    f = n_blk * fb
    tm = min(tm, t)
    nsteps = t // tm

    def body(h_ref, g_ref, wu_ref, wd_ref, go_ref, y_ref, loss_ref, dh_ref, dhb_ref, dg_ref, n_ref, a_ref, z_ref, lacc):
        i = pl.program_id(0)

        @pl.when(i == 0)
        def _():
            lacc[...] = jnp.zeros_like(lacc)
            dg_ref[...] = jnp.zeros_like(dg_ref)
        hv = _mlp_tile(h_ref[...], g_ref, wu_ref, wd_ref, n_ref, a_ref, z_ref)
        gv = go_ref[...]
        r = _rstd(hv)
        xhat = hv * r
        err = xhat * gv - y_ref[...]
        lacc[...] += _rows8(err * err)
        dout = err * (1.0 / d)
        dy = dout * gv
        dg_ref[...] += _rows8(dout * xhat)
        dh = r * (dy - xhat * jnp.mean(dy * xhat, axis=-1, keepdims=True))
        dh_ref[...] = dh
        dhb_ref[...] = dh.astype(BF16)

        @pl.when(i == nsteps - 1)
        def _():
            loss_ref[...] = jnp.full(loss_ref.shape, (0.5 / d) * jnp.sum(lacc[...]), F32)

    row = lambda n_: pl.BlockSpec((tm, n_), lambda i: (i, 0))
    return pl.pallas_call(
        body, name=name, grid=(nsteps,),
        in_specs=[row(d), _const_spec((1, d)), _const_spec(wup.shape), _const_spec(wdown.shape), _const_spec((1, d)),
                  row(d)],
        out_specs=[pl.BlockSpec((8, 128), lambda i: (0, 0)), row(d), row(d), pl.BlockSpec((8, d), lambda i: (0, 0)),
                   row(d), row(f), row(f)],
        out_shape=[jax.ShapeDtypeStruct((8, 128), F32), jax.ShapeDtypeStruct((t, d), F32),
                   jax.ShapeDtypeStruct((t, d), BF16),
                   jax.ShapeDtypeStruct((8, d), F32), jax.ShapeDtypeStruct((t, d), BF16),
                   jax.ShapeDtypeStruct((t, f), BF16), jax.ShapeDtypeStruct((t, f), BF16)],
        scratch_shapes=[pltpu.VMEM((8, d), F32)],
        compiler_params=_cp(("arbitrary",)),
    )(h, g, wup, wdown, g_out, target)


def _pool_inv_count(i, tm):
    tglob = (i * tm + lax.broadcasted_iota(jnp.int32, (tm, 1), 0) + 1).astype(F32)
    return [1.0 / jnp.minimum(tglob, float(w)) for w in POOL_WINDOWS]


def _pool_fwd(h, g, poolw, scale, *, tm=512):
    t, d = h.shape
    tm = min(tm, t)
    cg = d // len(POOL_WINDOWS)

    def body(h_ref, hh_ref, g_ref, w_ref, s_ref, ho_ref, p_ref, ext):
        i = pl.program_id(0)
        hv = h_ref[...]
        halo = hh_ref[...]
        n = hv * _rstd(hv) * g_ref[...]
        ext[0:POOL_HALO, :] = jnp.where(i == 0, 0.0, halo * _rstd(halo) * g_ref[...])
        ext[POOL_HALO:POOL_HALO + tm, :] = n
        inv = _pool_inv_count(i, tm)
        for gi, w in enumerate(POOL_WINDOWS):
            cs = slice(gi * cg, (gi + 1) * cg)
            s = ext[POOL_HALO:POOL_HALO + tm, cs]
            for j in range(1, w):
                s = s + ext[POOL_HALO - j:POOL_HALO - j + tm, cs]
            pooled = (s * inv[gi] - n[:, cs]).astype(BF16)
            p_ref[:, cs] = pooled
            ho_ref[:, cs] = hv[:, cs] + _dot(pooled, w_ref[gi]) * s_ref[:, cs]

    row = lambda: pl.BlockSpec((tm, d), lambda i: (i, 0))
    return pl.pallas_call(
        body, name="pool_fwd", grid=(t // tm,),
        in_specs=[row(), pl.BlockSpec((POOL_HALO, d), _prev_halo(tm, POOL_HALO)), _const_spec((1, d)),
                  _const_spec(poolw.shape), _const_spec((1, d))],
        out_specs=[row(), row()],
        out_shape=[jax.ShapeDtypeStruct((t, d), F32), jax.ShapeDtypeStruct((t, d), BF16)],
        scratch_shapes=[pltpu.VMEM((POOL_HALO + tm, d), F32)],
        compiler_params=_cp(("parallel",)),
    )(h, h, g, poolw, scale)


def _mm_tn(a, b, *, name, ta, tb, tt, blocked_out=False, out_dtype=F32):
    t, ka = a.shape
    n = b.shape[1]
    ta, tb, tt = min(ta, ka), min(tb, n), min(tt, t)
    nt = t // tt

    def body(a_ref, b_ref, o_ref, acc):
        @pl.when(pl.program_id(2) == 0)
        def _():
            acc[...] = jnp.zeros_like(acc)
        acc[...] += _dot_tn(a_ref[...].astype(BF16), b_ref[...].astype(BF16))

        @pl.when(pl.program_id(2) == nt - 1)
        def _():
            o_ref[...] = acc[...].astype(out_dtype)

    if blocked_out:
        assert ta == ka
        out_shape = jax.ShapeDtypeStruct((n // tb, ka, tb), out_dtype)
        out_spec = pl.BlockSpec((None, ta, tb), lambda i, j, k: (j, i, 0))
    else:
        out_shape = jax.ShapeDtypeStruct((ka, n), out_dtype)
        out_spec = pl.BlockSpec((ta, tb), lambda i, j, k: (i, j))
    return pl.pallas_call(
        body, name=name, grid=(ka // ta, n // tb, nt),
        in_specs=[pl.BlockSpec((tt, ta), lambda i, j, k: (k, i)), pl.BlockSpec((tt, tb), lambda i, j, k: (k, j))],
        out_specs=out_spec, out_shape=out_shape, scratch_shapes=[pltpu.VMEM((ta, tb), F32)],
        compiler_params=_cp(("parallel", "parallel", "arbitrary")),
    )(a, b)


def _mm_tn_cat(a_list, b_list, *, name, tt, out_dtype=BF16):
    t = a_list[0].shape[0]
    ta, tb = a_list[0].shape[1], b_list[0].shape[1]
    na, nb = len(a_list), len(b_list)
    tt = min(tt, t)
    nt = t // tt

    def body(*refs):
        a_refs, b_refs, o_ref, acc = refs[:na], refs[na:na + nb], refs[na + nb], refs[na + nb + 1]
        i, j, k = pl.program_id(0), pl.program_id(1), pl.program_id(2)

        @pl.when(k == 0)
        def _():
            acc[...] = jnp.zeros_like(acc)
        for ia in range(na):
            for ib in range(nb):
                @pl.when((i == ia) & (j == ib))
                def _(ia=ia, ib=ib):
                    acc[...] += _dot_tn(a_refs[ia][...].astype(BF16), b_refs[ib][...].astype(BF16))

        @pl.when(k == nt - 1)
        def _():
            o_ref[...] = acc[...].astype(out_dtype)

    def held(m, axis):
        def index(i, j, k):
            cur = (i, j)[axis]
            return (jnp.where(cur == m, k, jnp.where(cur < m, 0, nt - 1)), 0)
        return index

    return pl.pallas_call(
        body, name=name, grid=(na, nb, nt),
        in_specs=[pl.BlockSpec((tt, ta), held(m, 0)) for m in range(na)]
        + [pl.BlockSpec((tt, tb), held(m, 1)) for m in range(nb)],
        out_specs=pl.BlockSpec((ta, tb), lambda i, j, k: (i, j)),
        out_shape=jax.ShapeDtypeStruct((na * ta, nb * tb), out_dtype), scratch_shapes=[pltpu.VMEM((ta, tb), F32)],
        compiler_params=_cp(("arbitrary", "arbitrary", "arbitrary")),
    )(*a_list, *b_list)


def _mlp_bwd(dho, h, a, g, wup, wdown, *, name, tm=512):
    t, d = h.shape
    n_blk, _, fb = wup.shape
    f = n_blk * fb
    tm = min(tm, t)

    def body(do_ref, h_ref, a_ref, g_ref, wu_ref, wd_ref, dh_ref, da_ref, dg_ref):
        @pl.when(pl.program_id(0) == 0)
        def _():
            dg_ref[...] = jnp.zeros_like(dg_ref)
        dho_v = do_ref[...]
        dob = dho_v.astype(BF16)
        dn = jnp.zeros((tm, d), F32)
        for k in range(n_blk):
            dz = _dot_nt(dob, wd_ref[k * fb:(k + 1) * fb, :])
            da = (dz * (2.0 * jnp.maximum(a_ref[:, k * fb:(k + 1) * fb].astype(F32), 0.0))).astype(BF16)
            da_ref[:, k * fb:(k + 1) * fb] = da
            dn = dn + _dot_nt(da, wu_ref[k])
        dh, dg = _norm_bwd(dn, h_ref[...], g_ref[...])
        dh_ref[...] = dho_v + dh
        dg_ref[...] += dg

    row = lambda n_: pl.BlockSpec((tm, n_), lambda i: (i, 0))
    return pl.pallas_call(
        body, name=name, grid=(t // tm,),
        in_specs=[row(d), row(d), row(f), _const_spec((1, d)), _const_spec(wup.shape), _const_spec(wdown.shape)],
        out_specs=[row(d), row(f), pl.BlockSpec((8, d), lambda i: (0, 0))],
        out_shape=[jax.ShapeDtypeStruct((t, d), F32), jax.ShapeDtypeStruct((t, f), BF16),
                   jax.ShapeDtypeStruct((8, d), F32)],
        compiler_params=_cp(("arbitrary",)),
    )(dho, h, a, g, wup, wdown)


def _pool_bwd(dho, h, pooled, g, poolw, scale, *, tm=512):
    t, d = h.shape
    tm = min(tm, t)
    ng = len(POOL_WINDOWS)
    cg = d // ng
    nsteps = t // tm

    def body(do_ref, dn_ref, h_ref, p_ref, g_ref, w_ref, s_ref, dh_ref, dhb_ref, dw_ref, ds_ref, dg_ref, ext):
        i = pl.program_id(0)

        @pl.when(i == 0)
        def _():
            dw_ref[...] = jnp.zeros_like(dw_ref)
            ds_ref[...] = jnp.zeros_like(ds_ref)
            dg_ref[...] = jnp.zeros_like(dg_ref)
        dho_v = do_ref[...]
        sv = s_ref[...]
        dyp = (dho_v * sv).astype(BF16)
        dyp_halo = (dn_ref[...] * sv).astype(BF16)
        inv = _pool_inv_count(i, tm)
        tnext = ((i + 1) * tm + lax.broadcasted_iota(jnp.int32, (POOL_HALO, 1), 0) + 1).astype(F32)
        last = i == nsteps - 1
        ypre_parts, dpooled_parts = [], []
        for gi, w in enumerate(POOL_WINDOWS):
            cs = slice(gi * cg, (gi + 1) * cg)
            pg = p_ref[:, cs]
            ypre_parts.append(_dot(pg, w_ref[gi]))
            dw_ref[gi] += _dot_tn(pg, dyp[:, cs])
            dpool = _dot_nt(dyp[:, cs], w_ref[gi])
            dpooled_parts.append(dpool)
            ext[0:tm, cs] = dpool * inv[gi]
            dpool_halo = _dot_nt(dyp_halo[:, cs], w_ref[gi]) * (1.0 / jnp.minimum(tnext, float(w)))
            ext[tm:tm + POOL_HALO, cs] = jnp.where(last, 0.0, dpool_halo)
        ds_ref[...] += _rows8(dho_v * jnp.concatenate(ypre_parts, axis=1))
        dn_parts = []
        for gi, w in enumerate(POOL_WINDOWS):
            cs = slice(gi * cg, (gi + 1) * cg)
            s = ext[0:tm, cs]
            for j in range(1, w):
                s = s + ext[j:j + tm, cs]
            dn_parts.append(s - dpooled_parts[gi])
        dh, dg = _norm_bwd(jnp.concatenate(dn_parts, axis=1), h_ref[...], g_ref[...])
        dh = dho_v + dh
        dh_ref[...] = dh
        dhb_ref[...] = dh.astype(BF16)
        dg_ref[...] += dg

    row = lambda: pl.BlockSpec((tm, d), lambda i: (i, 0))
    acc8 = lambda: pl.BlockSpec((8, d), lambda i: (0, 0))
    return pl.pallas_call(
        body, name="pool_bwd", grid=(nsteps,),
        in_specs=[row(), pl.BlockSpec((POOL_HALO, d), _next_halo(tm, POOL_HALO, t)), row(), row(),
                  _const_spec((1, d)), _const_spec(poolw.shape), _const_spec((1, d))],
        out_specs=[row(), row(), pl.BlockSpec((ng, cg, cg), lambda i: (0, 0, 0)), acc8(), acc8()],
        out_shape=[jax.ShapeDtypeStruct((t, d), F32), jax.ShapeDtypeStruct((t, d), BF16),
                   jax.ShapeDtypeStruct((ng, cg, cg), F32),
                   jax.ShapeDtypeStruct((8, d), F32), jax.ShapeDtypeStruct((8, d), F32)],
        scratch_shapes=[pltpu.VMEM((tm + POOL_HALO, d), F32)],
        compiler_params=_cp(("arbitrary",)),
    )(dho, dho, h, pooled, g, poolw, scale)


def _outproj_conv_bwd(dh, o, wout, bcx, conv_w, *, tm=512):
    t, d = dh.shape
    tm = min(tm, t)
    ch = CONV_CH
    nsteps = t // tm

    def body(dh_ref, o_ref, w_ref, b_ref, c_ref, x_ref, hc_ref, hx_ref, cw_ref,
             da_ref, dat_ref, db_ref, dw_ref, ext_u, ext_d):
        s = pl.program_id(0)

        @pl.when(s == 0)
        def _():
            dw_ref[...] = jnp.zeros_like(dw_ref)
            ext_d[0:CONV_HALO, :] = jnp.zeros((CONV_HALO, ch), F32)
        dhb = dh_ref[...].astype(BF16)
        for p in range(ATTN_W // PAIR):
            datt = _dot_nt(dhb, w_ref[p * PAIR:(p + 1) * PAIR, :])
            prod = datt * o_ref[:, p * PAIR:(p + 1) * PAIR].astype(F32)
            for hh in range(2):
                lane, head, aux = _head_lanes(hh)
                delta = jnp.sum(jnp.where(head, prod, 0.0), axis=1, keepdims=True)
                aug = _put_pieces(lane, aux + AUX_BIAS, -delta, jnp.where(head, datt, 0.0))
                da_ref[2 * p + hh] = aug.astype(BF16)
                dat_ref[2 * p + hh] = aug.astype(BF16).T
        dcv = _dot_nt(dhb, w_ref[ATTN_W:, :])
        b, c, x = b_ref[...].astype(F32), c_ref[...].astype(F32), x_ref[...].astype(F32)
        ext_u[0:CONV_HALO, :] = jnp.where(s == nsteps - 1, 0.0, hc_ref[...].astype(F32) * hx_ref[...].astype(F32))
        ext_u[CONV_HALO:CONV_HALO + tm, :] = c * x
        dconv = dcv * b
        ext_d[tm:tm + CONV_HALO, :] = ext_d[0:CONV_HALO, :]
        ext_d[0:tm, :] = dconv
        u = [ext_u[CONV_HALO - 2 + k:CONV_HALO - 2 + k + tm, :] for k in range(3)]
        conv = cw_ref[0:1, :] * u[0] + cw_ref[1:2, :] * u[1] + cw_ref[2:3, :] * u[2]
        du = (cw_ref[2:3, :] * dconv + cw_ref[1:2, :] * ext_d[1:1 + tm, :] + cw_ref[0:1, :] * ext_d[2:2 + tm, :])
        db_ref[:, 0:ch] = (dcv * conv).astype(BF16)
        db_ref[:, ch:2 * ch] = (du * x).astype(BF16)
        db_ref[:, 2 * ch:3 * ch] = (du * c).astype(BF16)
        for k in range(3):
            dw_ref[k] += _rows8(dconv * u[k])

    rev = lambda s: nsteps - 1 - s
    row = lambda n_: pl.BlockSpec((tm, n_), lambda s: (rev(s), 0))
    col = lambda k: pl.BlockSpec((tm, ch), lambda s: (rev(s), k))
    prev = lambda k: pl.BlockSpec((CONV_HALO, ch), lambda s: (_prev_halo(tm, CONV_HALO)(rev(s))[0], k))
    return pl.pallas_call(
        body, name="outproj_conv_bwd", grid=(nsteps,),
        in_specs=[row(d), row(ATTN_W), _const_spec(wout.shape), col(0), col(1), col(2), prev(1), prev(2),
                  _const_spec((8, ch))],
        out_specs=[pl.BlockSpec((N_HEADS, tm, PAIR), lambda s: (0, rev(s), 0)),
                   pl.BlockSpec((N_HEADS, PAIR, tm), lambda s: (0, 0, rev(s))),
                   row(3 * ch), pl.BlockSpec((3, 8, ch), lambda s: (0, 0, 0))],
        out_shape=[jax.ShapeDtypeStruct((N_HEADS, t, PAIR), BF16), jax.ShapeDtypeStruct((N_HEADS, PAIR, t), BF16),
                   jax.ShapeDtypeStruct((t, 3 * ch), BF16), jax.ShapeDtypeStruct((3, 8, ch), F32)],
        scratch_shapes=[pltpu.VMEM((CONV_HALO + tm, ch), F32), pltpu.VMEM((tm + CONV_HALO, ch), F32)],
        compiler_params=_cp(("arbitrary",)),
    )(dh, o, wout, bcx, bcx, bcx, bcx, bcx, conv_w)


def _attn_bwd(q_bwd, do_aug, q_bwd_t, do_aug_t, k_aug, v_aug_t, gblocks, *, tq=1024):
    t = q_bwd.shape[1]
    tq = min(tq, t)
    tk = tq // 2
    nq, nk = t // tq, t // tk
    n_pairs = ATTN_W // PAIR
    n_g = len(gblocks)

    def body(q_ref, do_ref, qt_ref, dot_ref, k_ref, vt_ref, *rest):
        dq_ref, dqx_ref, dk_ref, dkx_ref, dv_ref = rest[n_g:n_g + 5]
        dq_scr = rest[2 * n_g + 5]
        scatter = _Exchange(rest[:n_g], rest[n_g + 5:2 * n_g + 5], *rest[2 * n_g + 6:], gather=False)
        j = pl.program_id(1)

        @pl.when((pl.program_id(0) == 0) & (j == 0))
        def _():
            scatter.start()

        @pl.when(j == 0)
        def _():
            dq_scr[...] = jnp.zeros_like(dq_scr)
        k = [k_ref[0], k_ref[1]]
        vt = [vt_ref[0], vt_ref[1]]

        def step(i, carry, diag, rows=tq, row0=0):
            qs = pl.multiple_of(i * tq + row0, tk)
            if diag:
                row = lax.broadcasted_iota(jnp.int32, (rows, tk), 0)
                col = lax.broadcasted_iota(jnp.int32, (rows, tk), 1)
            out = []
            for hh in range(2):
                dk_a, dv_a = carry[hh]
                q = q_ref[hh, pl.ds(qs, rows), :]
                dov = do_ref[hh, pl.ds(qs, rows), :]
                p = jnp.exp2(_dot_nt(q, k[hh]))
                if diag:
                    p = jnp.where(col + (j * tk - i * tq - row0) <= row, p, 0.0)
                ds = (p * _dot(dov, vt[hh])).astype(BF16)
                dv_a = dv_a + _dot(dot_ref[hh, :, pl.ds(qs, rows)], p.astype(BF16))
                dk_a = dk_a + _dot(qt_ref[hh, :, pl.ds(qs, rows)], ds)
                dq_scr[hh, pl.ds(qs, rows), :] += _dot(ds, k[hh])
                out.append((dk_a, dv_a))
            return tuple(out)

        zero = (jnp.zeros((PAIR, tk), F32), jnp.zeros((PAIR, tk), F32))
        carry = lax.cond(j % 2 == 0, lambda c: step(j // 2, c, True),
                         lambda c: step(j // 2, c, True, rows=tk, row0=tk), (zero, zero))
        full0 = j // 2 + 1
        odd = (nq - full0) % 2
        carry = lax.cond(odd == 1, lambda c: step(full0, c, False), lambda c: c, carry)
        (dk0, dv0), (dk1, dv1) = lax.fori_loop(
            0, (nq - full0) // 2, lambda ii, c: step(full0 + odd + 2 * ii, c, False, rows=2 * tq), carry)
        first_t = lax.broadcasted_iota(jnp.int32, (PAIR, 1), 0) < HEAD_DIM
        first = lax.broadcasted_iota(jnp.int32, (1, PAIR), 1) < HEAD_DIM
        dk_ref[...] = (jnp.where(first_t, dk0, dk1) * (1.0 / LOG2E)).astype(BF16).T
        dkx_ref[...] = jnp.where(first_t, dk1, dk0).T
        dv_ref[...] = jnp.where(first_t, dv0, dv1).astype(BF16).T

        @pl.when(j == nk - 1)
        def _():
            dq_ref[...] = (jnp.where(first, dq_scr[0], dq_scr[1]) * Q_SCALE).astype(BF16)
            dqx_ref[...] = jnp.where(first, dq_scr[1], dq_scr[0])

        @pl.when((pl.program_id(0) == n_pairs - 1) & (j == nk - 1))
        def _():
            scatter.wait()

    resident = lambda: pl.BlockSpec((2, t, PAIR), lambda p, j: (p, 0, 0), pipeline_mode=pl.Buffered(1))
    resident_t = lambda: pl.BlockSpec((2, PAIR, t), lambda p, j: (p, 0, 0), pipeline_mode=pl.Buffered(1))
    kv_in = lambda: pl.BlockSpec((2, tk, PAIR), lambda p, j: (p, j, 0))
    whole = lambda: pl.BlockSpec((t, PAIR), lambda p, j: (0, p))
    tile = lambda: pl.BlockSpec((tk, PAIR), lambda p, j: (j, p))
    b16 = jax.ShapeDtypeStruct((t, ATTN_W), BF16)
    f32 = jax.ShapeDtypeStruct((t, ATTN_W), F32)
    res = pl.pallas_call(
        body, name="attn_bwd", grid=(n_pairs, nk),
        in_specs=[resident(), resident(), resident_t(), resident_t(), kv_in(),
                  pl.BlockSpec((2, PAIR, tk), lambda p, j: (p, 0, j))] + [HBM_SPEC] * n_g,
        out_specs=[whole(), whole(), tile(), tile(), tile()] + [HBM_SPEC] * n_g,
        out_shape=[b16, f32, b16, f32, b16] + [jax.ShapeDtypeStruct(g.shape, g.dtype) for g in gblocks],
        scratch_shapes=[pltpu.VMEM((2, t, PAIR), F32)] + _Exchange.scratch(n_g),
        compiler_params=_cp(("arbitrary", "arbitrary")),
    )(q_bwd, do_aug, q_bwd_t, do_aug_t, k_aug, v_aug_t, *gblocks)
    return res[:5], res[5:]


def _fgate_bwd(dqx, dkx, sgate, *, tm=256):
    t = sgate.shape[0]
    tm = min(tm, t)
    nsteps = t // tm

    def body(dq_ref, dk_ref, sg_ref, df_ref, dbf_ref, carry):
        @pl.when(pl.program_id(0) == 0)
        def _():
            carry[...] = jnp.zeros_like(carry)
            dbf_ref[...] = jnp.zeros_like(dbf_ref)
        lane = lax.broadcasted_iota(jnp.int32, (ATTN_W, F_PAD), 0)
        head = lax.broadcasted_iota(jnp.int32, (ATTN_W, F_PAD), 1)
        aux = (head // 2) * PAIR + HEAD_DIM * (1 - head % 2)
        valid = head < N_HEADS
        pick_r = (valid & (lane == aux + AUX_ROWSUM)).astype(F32)
        pick_c = (valid & (lane == aux + AUX_BIAS)).astype(F32)
        hp = lax.Precision.HIGHEST
        dcum = (jnp.dot(dq_ref[...], pick_r, preferred_element_type=F32, precision=lax.Precision.HIGH)
                + jnp.dot(dk_ref[...], pick_c, preferred_element_type=F32, precision=lax.Precision.HIGH))
        r = lax.broadcasted_iota(jnp.int32, (tm, tm), 0)
        c = lax.broadcasted_iota(jnp.int32, (tm, tm), 1)
        tri = (c >= r).astype(F32)
        rc = jnp.dot(tri, dcum, preferred_element_type=F32, precision=hp) + carry[...]
        carry[...] = rc[0:1, :]
        df = rc * sg_ref[...]
        df_ref[...] = df.astype(BF16)
        dbf_ref[...] += _rows8(df)

    rev = lambda i: nsteps - 1 - i
    return pl.pallas_call(
        body, name="fgate_bwd", grid=(nsteps,),
        in_specs=[pl.BlockSpec((tm, ATTN_W), lambda i: (rev(i), 0)), pl.BlockSpec((tm, ATTN_W), lambda i: (rev(i), 0)),
                  pl.BlockSpec((tm, F_PAD), lambda i: (rev(i), 0))],
        out_specs=[pl.BlockSpec((tm, F_PAD), lambda i: (rev(i), 0)), pl.BlockSpec((8, F_PAD), lambda i: (0, 0))],
        out_shape=[jax.ShapeDtypeStruct((t, F_PAD), BF16), jax.ShapeDtypeStruct((8, F_PAD), F32)],
        scratch_shapes=[pltpu.VMEM((1, F_PAD), F32)],
        compiler_params=_cp(("arbitrary",)),
    )(dqx, dkx, sgate)


def _inproj_bwd(dq, dk, dv, df, dbcx, dh, x, g, win_pt, gblock, *, tm=512):
    t, d = x.shape
    tm = min(tm, t)
    nsteps = t // tm
    n_qkv = 3 * ATTN_W

    def body(dq_ref, dk_ref, dv_ref, df_ref, db_ref, dh_ref, x_ref, g_ref, w_ref, gb_ref, gx_ref, dg_ref, land_ref,
             *sems):
        scatter = _Exchange([gb_ref], [land_ref], *sems, gather=False)

        @pl.when(pl.program_id(0) == 0)
        def _():
            scatter.start()
            dg_ref[...] = jnp.zeros_like(dg_ref)
        dn = _dot(df_ref[...], w_ref[n_qkv:n_qkv + F_PAD, :])
        for k, r in enumerate((dq_ref, dk_ref, dv_ref)):
            dn = dn + _dot(r[...], w_ref[k * ATTN_W:(k + 1) * ATTN_W, :])
        for k in range(3):
            c0 = n_qkv + F_PAD + k * CONV_CH
            dn = dn + _dot(db_ref[:, k * CONV_CH:(k + 1) * CONV_CH], w_ref[c0:c0 + CONV_CH, :])
        dx, dg = _norm_bwd(dn, x_ref[...], g_ref[...])
        gx_ref[...] = dh_ref[...] + dx
        dg_ref[...] += dg

        @pl.when(pl.program_id(0) == nsteps - 1)
        def _():
            scatter.wait()

    row = lambda n_: pl.BlockSpec((tm, n_), lambda i: (i, 0))
    return pl.pallas_call(
        body, name="inproj_bwd", grid=(nsteps,),
        in_specs=[row(ATTN_W), row(ATTN_W), row(ATTN_W), row(F_PAD), row(3 * CONV_CH), row(d), row(d),
                  _const_spec((1, d)), _const_spec(win_pt.shape), HBM_SPEC],
        out_specs=[row(d), pl.BlockSpec((8, d), lambda i: (0, 0)), HBM_SPEC],
        out_shape=[jax.ShapeDtypeStruct((t, d), F32), jax.ShapeDtypeStruct((8, d), F32),
                   jax.ShapeDtypeStruct(gblock.shape, gblock.dtype)],
        scratch_shapes=_Exchange.scratch(1),
        compiler_params=_cp(("arbitrary",)),
    )(dq, dk, dv, df, dbcx, dh, x, g, win_pt, gblock)


LATE = ("w_out_0", "w_up_0", "w_down_0", "pool_w_1", "w_up_1", "w_down_1")


def _local_step(x, target, gains, b_f, conv_w, pool_scale, win_pt, shards):
    d = x.shape[1]
    n0, qkv, flog, bcx, cv = _norm_inproj(x, gains["mix0"], win_pt, conv_w)
    q_aug_t, k_aug, v_aug_t, sgate = _fgate_prep(flog, b_f, qkv)
    att, q_bwd, q_bwd_t, gathered = _attn_fwd(q_aug_t, k_aug, v_aug_t, [shards[n] for n in LATE])
    g = dict(zip(LATE, gathered))
    wout = g["w_out_0"].reshape(d, d)
    wup0, wup1 = g["w_up_0"], g["w_up_1"]
    wdown0, wdown1 = g["w_down_0"].reshape(-1, d), g["w_down_1"].reshape(-1, d)
    n_grp = len(POOL_WINDOWS)
    cg = d // n_grp
    poolw = g["pool_w_1"].reshape(N_DEV, n_grp, cg // N_DEV, cg).transpose(1, 0, 2, 3).reshape(n_grp, cg, cg)
    h1 = _outproj(att, cv, x, wout)
    h2, n1, a0, z0 = _mlp_fwd(h1, gains["ffn0"], wup0, wdown0, name="mlp_fwd0")
    h3, pooled = _pool_fwd(h2, gains["mix1"], poolw, pool_scale)
    loss, dh4, dh4_b, dg_final, n3, a1, z1 = _mlp_fwd_loss(h3, gains["ffn1"], wup1, wdown1, gains["final"], target,
                                                           name="mlp_fwd1")
    f = a1.shape[1]
    fb = f // N_DEV
    dh3, da1, dg_ffn1 = _mlp_bwd(dh4, h3, a1, gains["ffn1"], wup1, wdown1, name="mlp_bwd1")
    dwdown1 = _mm_tn(z1, dh4_b, name="dwdown1", ta=1024, tb=1024, tt=4096, out_dtype=BF16)
    dwup1 = _mm_tn(n3, da1, name="dwup1", ta=d, tb=fb, tt=4096, blocked_out=True, out_dtype=BF16)
    dh2, dh2_b, dpoolw, dscale, dg_mix1 = _pool_bwd(dh3, h2, pooled, gains["mix1"], poolw, pool_scale)
    dh1, da0, dg_ffn0 = _mlp_bwd(dh2, h1, a0, gains["ffn0"], wup0, wdown0, name="mlp_bwd0")
    dwdown0 = _mm_tn(z0, dh2_b, name="dwdown0", ta=1024, tb=1024, tt=4096, out_dtype=BF16)
    dwup0 = _mm_tn(n1, da0, name="dwup0", ta=d, tb=fb, tt=4096, blocked_out=True, out_dtype=BF16)
    do_aug, do_aug_t, dbcx, dconvw = _outproj_conv_bwd(dh1, att, wout, bcx, conv_w)
    dwout = _mm_tn_cat([att, cv], [dh1], name="dwout", tt=2048)
    gblocks = {
        "w_out_0": dwout.reshape(N_DEV, d // N_DEV, d), "w_up_0": dwup0, "w_up_1": dwup1,
        "w_down_0": dwdown0.reshape(N_DEV, -1, d), "w_down_1": dwdown1.reshape(N_DEV, -1, d),
        "pool_w_1": dpoolw.astype(BF16).reshape(n_grp, N_DEV, cg // N_DEV, cg).transpose(1, 0, 2, 3).reshape(
            N_DEV, n_grp * (cg // N_DEV), cg),
    }
    (dq, dqx, dk, dkx, dv), landed = _attn_bwd(q_bwd, do_aug, q_bwd_t, do_aug_t, k_aug, v_aug_t,
                                               [gblocks[n] for n in LATE])
    df, dbf = _fgate_bwd(dqx, dkx, sgate)
    dwin_t = jnp.concatenate(
        [_mm_tn_cat([dq, dk, dv], [n0], name="dwin_qkv", tt=2048),
         _mm_tn(df, n0, name="dwin_f", ta=F_PAD, tb=d, tt=2048, out_dtype=BF16)[:N_HEADS],
         _mm_tn(dbcx, n0, name="dwin_bcx", ta=512, tb=d, tt=4096, out_dtype=BF16)], axis=0)
    dwin_blocks = dwin_t.reshape(N_DEV, dwin_t.shape[0] // N_DEV, d)
    grad_x, dg_mix0, landed_win = _inproj_bwd(dq, dk, dv, df, dbcx, dh1, x, gains["mix0"], win_pt, dwin_blocks)
    small = dict(mix0=dg_mix0, ffn0=dg_ffn0, mix1=dg_mix1, pool_scale=dscale, ffn1=dg_ffn1, final=dg_final,
                 b_f=dbf, conv_w=dconvw)
    return loss, grad_x, dict(zip(LATE + ("w_in_0",), tuple(landed) + (landed_win,))), small


def _all_gather(shards):
    n = len(shards)

    def body(*refs):
        gather = _TwoLevelGather(refs[:n], refs[n:2 * n], *refs[2 * n:])
        gather.start()
        gather.forward()
        gather.wait()

    return pl.pallas_call(
        body, name="all_gather",
        in_specs=[HBM_SPEC] * n, out_specs=[HBM_SPEC] * n,
        out_shape=[jax.ShapeDtypeStruct((N_DEV,) + s.shape, s.dtype) for s in shards],
        scratch_shapes=[pltpu.SemaphoreType.DMA((7 * n,)), pltpu.SemaphoreType.DMA((7 * n,)),
                        pltpu.SemaphoreType.DMA((n,))],
    )(*shards)


SMALL_ROWS = 16


def _small_allreduce(parts):
    n, _, w = parts.shape
    assert n <= SMALL_ROWS

    def body(p_ref, o_ref, gath, send_sems, recv_sems):
        x, y, c = lax.axis_index("x"), lax.axis_index("y"), lax.axis_index("c")
        my = _slot(x, y, c)
        rows = [jnp.sum(p_ref[i], axis=0, keepdims=True) for i in range(n)]
        rows.append(jnp.zeros((SMALL_ROWS - n, w), F32))
        gath[my] = jnp.concatenate(rows, axis=0)
        copies = []
        for k in range(1, N_DEV):
            px, py, pc = x ^ (k >> 2), y ^ ((k >> 1) & 1), c ^ (k & 1)
            cp = pltpu.make_async_remote_copy(
                src_ref=gath.at[my], dst_ref=gath.at[my], send_sem=send_sems.at[k - 1], recv_sem=recv_sems.at[k - 1],
                device_id=(px, py, pc), device_id_type=MESH)
            cp.start()
            copies.append(cp)
        for cp in copies:
            cp.wait()
        acc = gath[0]
        for d in range(1, N_DEV):
            acc = acc + gath[d]
        o_ref[...] = acc

    return pl.pallas_call(
        body, name="small_allreduce",
        in_specs=[VMEM_SPEC], out_specs=VMEM_SPEC,
        out_shape=jax.ShapeDtypeStruct((SMALL_ROWS, w), F32),
        scratch_shapes=[pltpu.VMEM((N_DEV, SMALL_ROWS, w), F32), pltpu.SemaphoreType.DMA((N_DEV - 1,)),
                        pltpu.SemaphoreType.DMA((N_DEV - 1,))],
    )(parts)


def _adamw(g, w, m, v, *, name, tm=256):
    r, c = g.shape
    tm = tm if r % tm == 0 else r
    bc1 = 1.0 - ADAM_B1 ** ADAM_STEP
    bc2 = 1.0 - ADAM_B2 ** ADAM_STEP

    def body(g_ref, w_ref, m_ref, v_ref, d_ref, nm_ref, nv_ref):
        gv = g_ref[...]
        nm = ADAM_B1 * m_ref[...] + (1.0 - ADAM_B1) * gv
        nv = ADAM_B2 * v_ref[...] + (1.0 - ADAM_B2) * jnp.square(gv)
        nm_ref[...] = nm
        nv_ref[...] = nv
        d_ref[...] = -ADAM_LR * ((nm / bc1) / (jnp.sqrt(nv / bc2) + ADAM_EPS) + ADAM_WD * w_ref[...])

    blk = pl.BlockSpec((tm, c), lambda i: (i, 0))
    shp = jax.ShapeDtypeStruct((r, c), F32)
    return pl.pallas_call(
        body, name=name, grid=(r // tm,), in_specs=[blk] * 4, out_specs=[blk] * 3, out_shape=[shp] * 3,
        compiler_params=_cp(("parallel",)),
    )(g, w, m, v)


def _transpose_cast(a, *, name):
    def body(a_ref, o_ref):
        o_ref[...] = a_ref[...].T.astype(BF16)

    return pl.pallas_call(body, name=name, out_shape=jax.ShapeDtypeStruct(a.shape[::-1], BF16),
                          compiler_params=_cp())(a)


def _adamw_sum_t(parts, w, m, v, *, name):
    bc1 = 1.0 - ADAM_B1 ** ADAM_STEP
    bc2 = 1.0 - ADAM_B2 ** ADAM_STEP

    def body(p_ref, w_ref, m_ref, v_ref, g_ref, d_ref, nm_ref, nv_ref):
        acc = p_ref[0].astype(F32)
        for k in range(1, N_DEV):
            acc = acc + p_ref[k].astype(F32)
        gv = acc.T
        g_ref[...] = gv
        nm = ADAM_B1 * m_ref[...] + (1.0 - ADAM_B1) * gv
        nv = ADAM_B2 * v_ref[...] + (1.0 - ADAM_B2) * jnp.square(gv)
        nm_ref[...] = nm
        nv_ref[...] = nv
        d_ref[...] = -ADAM_LR * ((nm / bc1) / (jnp.sqrt(nv / bc2) + ADAM_EPS) + ADAM_WD * w_ref[...])

    shp = jax.ShapeDtypeStruct(w.shape, F32)
    return pl.pallas_call(body, name=name, out_shape=[shp] * 4, compiler_params=_cp())(parts, w, m, v)


def _adamw_sum(parts, w, m, v, *, name, tm=256):
    _, r, c = parts.shape
    tm = tm if r % tm == 0 else r
    bc1 = 1.0 - ADAM_B1 ** ADAM_STEP
    bc2 = 1.0 - ADAM_B2 ** ADAM_STEP

    def body(p_ref, w_ref, m_ref, v_ref, g_ref, d_ref, nm_ref, nv_ref):
        gv = p_ref[0].astype(F32)
        for k in range(1, N_DEV):
            gv = gv + p_ref[k].astype(F32)
        g_ref[...] = gv
        nm = ADAM_B1 * m_ref[...] + (1.0 - ADAM_B1) * gv
        nv = ADAM_B2 * v_ref[...] + (1.0 - ADAM_B2) * jnp.square(gv)
        nm_ref[...] = nm
        nv_ref[...] = nv
        d_ref[...] = -ADAM_LR * ((nm / bc1) / (jnp.sqrt(nv / bc2) + ADAM_EPS) + ADAM_WD * w_ref[...])

    blk = pl.BlockSpec((tm, c), lambda i: (i, 0))
    shp = jax.ShapeDtypeStruct((r, c), F32)
    return pl.pallas_call(
        body, name=name, grid=(r // tm,), in_specs=[pl.BlockSpec((N_DEV, tm, c), lambda i: (0, i, 0))] + [blk] * 3,
        out_specs=[blk] * 4, out_shape=[shp] * 4, compiler_params=_cp(("parallel",)),
    )(parts, w, m, v)


BIG = ("w_in_0", "w_out_0", "w_up_0", "w_down_0", "pool_w_1", "w_up_1", "w_down_1")
SMALL = ("norm_mix_0", "norm_ffn_0", "norm_mix_1", "pool_scale_1", "norm_ffn_1", "final_norm", "b_f_0", "conv_w_0")
WEIGHTS = ("norm_mix_0", "w_in_0", "b_f_0", "conv_w_0", "w_out_0", "norm_ffn_0", "w_up_0", "w_down_0", "norm_mix_1",
           "pool_w_1", "pool_scale_1", "norm_ffn_1", "w_up_1", "w_down_1", "final_norm")


def _pad_to(a, rows, cols):
    return jnp.pad(a, ((0, rows - a.shape[0]), (0, cols - a.shape[1])))


def _pack_small(p, width):
    rows = [p[n].reshape(1, -1) for n in SMALL[:6]]
    rows.append(_pad_to(p["b_f_0"].reshape(1, -1), 1, width))
    rows.append(_pad_to(p["conv_w_0"], 3, width))
    return _pad_to(jnp.concatenate(rows, axis=0), SMALL_ROWS, width)


def _unpack_small(a, like):
    out = {n: a[i] for i, n in enumerate(SMALL[:6])}
    out["b_f_0"] = a[6, :like["b_f_0"].shape[0]]
    out["conv_w_0"] = a[7:10, :like["conv_w_0"].shape[1]]
    return out


def kernel(x, norm_mix_0, w_in_0, b_f_0, conv_w_0, w_out_0, norm_ffn_0, w_up_0, w_down_0, norm_mix_1, pool_w_1, pool_scale_1, norm_ffn_1, w_up_1, w_down_1, final_norm, loss_target, m_norm_mix_0, m_w_in_0, m_b_f_0, m_conv_w_0, m_w_out_0, m_norm_ffn_0, m_w_up_0, m_w_down_0, m_norm_mix_1, m_pool_w_1, m_pool_scale_1, m_norm_ffn_1, m_w_up_1, m_w_down_1, m_final_norm, v_norm_mix_0, v_w_in_0, v_b_f_0, v_conv_w_0, v_w_out_0, v_norm_ffn_0, v_w_up_0, v_w_down_0, v_norm_mix_1, v_pool_w_1, v_pool_scale_1, v_norm_ffn_1, v_w_up_1, v_w_down_1, v_final_norm):
    w = dict(norm_mix_0=norm_mix_0, w_in_0=w_in_0, b_f_0=b_f_0, conv_w_0=conv_w_0, w_out_0=w_out_0,
             norm_ffn_0=norm_ffn_0, w_up_0=w_up_0, w_down_0=w_down_0, norm_mix_1=norm_mix_1, pool_w_1=pool_w_1,
             pool_scale_1=pool_scale_1, norm_ffn_1=norm_ffn_1, w_up_1=w_up_1, w_down_1=w_down_1, final_norm=final_norm)
    m = dict(norm_mix_0=m_norm_mix_0, w_in_0=m_w_in_0, b_f_0=m_b_f_0, conv_w_0=m_conv_w_0, w_out_0=m_w_out_0,
             norm_ffn_0=m_norm_ffn_0, w_up_0=m_w_up_0, w_down_0=m_w_down_0, norm_mix_1=m_norm_mix_1,
             pool_w_1=m_pool_w_1, pool_scale_1=m_pool_scale_1, norm_ffn_1=m_norm_ffn_1, w_up_1=m_w_up_1,
             w_down_1=m_w_down_1, final_norm=m_final_norm)
    v = dict(norm_mix_0=v_norm_mix_0, w_in_0=v_w_in_0, b_f_0=v_b_f_0, conv_w_0=v_conv_w_0, w_out_0=v_w_out_0,
             norm_ffn_0=v_norm_ffn_0, w_up_0=v_w_up_0, w_down_0=v_w_down_0, norm_mix_1=v_norm_mix_1,
             pool_w_1=v_pool_w_1, pool_scale_1=v_pool_scale_1, norm_ffn_1=v_norm_ffn_1, w_up_1=v_w_up_1,
             w_down_1=v_w_down_1, final_norm=v_final_norm)
    d = x.shape[-1]
    n_in = w_in_0.shape[1] * N_DEV
    n_qkv = 3 * ATTN_W
    pool_g, pool_rows, pool_c = pool_w_1.shape

    def shard2d(p):
        return {n: (p[n].reshape(pool_g * pool_rows, pool_c) if n == "pool_w_1" else p[n]) for n in BIG}
    w2, m2, v2 = shard2d(w), shard2d(m), shard2d(v)

    conv_cols = conv_w_0.shape[1]
    win_g8, conv_g8 = _all_gather([_transpose_cast(w_in_0, name="w_in_t"), _pad_to(conv_w_0, 8, 128)])
    conv_full = conv_g8[:, :, :conv_cols].transpose(1, 0, 2).reshape(8, N_DEV * conv_cols)
    win_t = win_g8.reshape(n_in, d)
    win_pt = jnp.concatenate([win_t[:n_qkv], _pad_to(win_t[n_qkv:n_qkv + N_HEADS], F_PAD, d),
                              win_t[n_qkv + N_HEADS:]], axis=0)

    gains = dict(mix0=norm_mix_0.reshape(1, d), ffn0=norm_ffn_0.reshape(1, d), mix1=norm_mix_1.reshape(1, d),
                 ffn1=norm_ffn_1.reshape(1, d), final=final_norm.reshape(1, d))
    dev = _slot(lax.axis_index("x"), lax.axis_index("y"), lax.axis_index("c"))
    loss8, grad_x, landed, small = _local_step(
        x[0], loss_target[0], gains, _pad_to(b_f_0.reshape(1, -1), 1, F_PAD), conv_full, pool_scale_1.reshape(1, d),
        win_pt, {n: w2[n].astype(BF16) for n in LATE})
    parts = jnp.concatenate(
        [small[k][None] for k in ("mix0", "ffn0", "mix1", "pool_scale", "ffn1", "final")]
        + [_pad_to(small["b_f"], 8, d)[None], jnp.pad(small["conv_w"], ((0, 0), (0, 0), (0, d - CONV_CH))),
           _pad_to(loss8[0:1, 0:1], 8, d)[None]], axis=0)
    tot = _small_allreduce(parts)
    loss = tot[10, 0]
    conv_g = lax.dynamic_slice(tot, (7, dev * conv_cols), (3, conv_cols))
    gs = tot.at[7:10].set(_pad_to(conv_g, 3, d))

    grads, deltas, new_m, new_v = {}, {}, {}, {}
    for n in BIG:
        if n in LATE:
            gr, dl, nm, nv = _adamw_sum(landed[n], w2[n], m2[n], v2[n], name="adamw_" + n)
        else:
            gr, dl, nm, nv = _adamw_sum_t(landed[n], w2[n], m2[n], v2[n], name="adamw_" + n)
        for dst, val in ((grads, gr), (deltas, dl), (new_m, nm), (new_v, nv)):
            dst[n] = val.reshape(w[n].shape)
    dl, nm, nv = _adamw(gs, _pack_small(w, d), _pack_small(m, d), _pack_small(v, d), name="adamw_small")
    for dst, val in ((grads, gs), (deltas, dl), (new_m, nm), (new_v, nv)):
        dst.update(_unpack_small(val, w))
    return (loss, grad_x[None], *[grads[n] for n in WEIGHTS], *[deltas[n] for n in WEIGHTS],
            *[new_m[n] for n in WEIGHTS], *[new_v[n] for n in WEIGHTS])
```

```python
import functools

import jax
import jax.numpy as jnp
from jax import lax
from jax.experimental import pallas as pl
from jax.experimental.pallas import tpu as pltpu

F32 = jnp.float32
BF16 = jnp.bfloat16

N_DEV = 8
N_HEADS = 8
HEAD_DIM = 64
PAIR = 2 * HEAD_DIM
ATTN_W = N_HEADS * HEAD_DIM
CONV_CH = 512
F_PAD = 128
POOL_WINDOWS = (2, 4, 8, 16)
POOL_HALO = 16
CONV_HALO = 16
RMS_EPS = 1e-6
Q_SCALE = HEAD_DIM ** -0.5
LOG2E = 1.4426950408889634
NEG = -1e30
AUX_BIAS = 0
AUX_LSE = 3
AUX_ROWSUM = 6
ADAM_LR, ADAM_B1, ADAM_B2, ADAM_EPS, ADAM_WD, ADAM_STEP = 0.001, 0.9, 0.999, 1e-08, 0.01, 10
MESH = pl.DeviceIdType.MESH
VMEM_LIMIT = 56 * 2**20


def _cp(sem=None, vmem=VMEM_LIMIT, **kw):
    return pltpu.CompilerParams(dimension_semantics=sem, vmem_limit_bytes=vmem, **kw)


def _dot(a, b):
    return jnp.dot(a, b, preferred_element_type=F32)


def _dot_nt(a, b):
    return lax.dot_general(a, b, (((1,), (1,)), ((), ())), preferred_element_type=F32)


def _dot_tn(a, b):
    return lax.dot_general(a, b, (((0,), (0,)), ((), ())), preferred_element_type=F32)


def _rstd(h):
    return lax.rsqrt(jnp.mean(h * h, axis=-1, keepdims=True) + RMS_EPS)


def _rows8(x):
    r, n = x.shape
    return jnp.sum(x.reshape(r // 8, 8, n), axis=0)


def _norm_bwd(dn, h, g):
    r = _rstd(h)
    xhat = h * r
    dy = dn * g
    dh = r * (dy - xhat * jnp.mean(dy * xhat, axis=-1, keepdims=True))
    return dh, _rows8(dn * xhat)


def _const_spec(shape):
    nd = len(shape)
    return pl.BlockSpec(shape, lambda *_: (0,) * nd, pipeline_mode=pl.Buffered(1))


HBM_SPEC = pl.BlockSpec(memory_space=pltpu.HBM)
VMEM_SPEC = pl.BlockSpec(memory_space=pltpu.VMEM)


def _slot(px, py, pc):
    return 4 * px + 2 * py + pc


class _Exchange:
    def __init__(self, srcs, dsts, send_sems, recv_sems, local_sems, gather):
        x, y, c = lax.axis_index("x"), lax.axis_index("y"), lax.axis_index("c")
        me = _slot(x, y, c)
        self.copies = []
        for a, (src, dst) in enumerate(zip(srcs, dsts)):
            self.copies.append(pltpu.make_async_copy(src if gather else src.at[me], dst.at[me], local_sems.at[a]))
            for k in range(1, N_DEV):
                px, py, pc = x ^ (k >> 2), y ^ ((k >> 1) & 1), c ^ (k & 1)
                self.copies.append(pltpu.make_async_remote_copy(
                    src_ref=src if gather else src.at[_slot(px, py, pc)], dst_ref=dst.at[me],
                    send_sem=send_sems.at[(N_DEV - 1) * a + k - 1], recv_sem=recv_sems.at[(N_DEV - 1) * a + k - 1],
                    device_id=(px, py, pc), device_id_type=MESH))

    def start(self):
        for cp in self.copies:
            cp.start()

    def wait(self):
        for cp in self.copies:
            cp.wait()

    @staticmethod
    def scratch(n):
        return [pltpu.SemaphoreType.DMA(((N_DEV - 1) * n,)), pltpu.SemaphoreType.DMA(((N_DEV - 1) * n,)),
                pltpu.SemaphoreType.DMA((n,))]


def _mesh_places():
    x, y, c = lax.axis_index("x"), lax.axis_index("y"), lax.axis_index("c")
    chips = [(1 - x, y), (x, 1 - y), (1 - x, 1 - y)]
    return (x, y, c), (x, y, 1 - c), chips


class _TwoLevelGather:
    def __init__(self, srcs, dsts, send_sems, recv_sems, local_sems):
        me, sib, chips = _mesh_places()
        c = me[2]
        n = len(srcs)

        def copy(a, k, block, to, src=None):
            dst = dsts[a].at[_slot(*block)]
            return pltpu.make_async_remote_copy(
                src_ref=dst if src is None else src, dst_ref=dst, send_sem=send_sems.at[7 * a + k],
                recv_sem=recv_sems.at[7 * a + k], device_id=to, device_id_type=MESH)

        self.mine = [pltpu.make_async_copy(srcs[a], dsts[a].at[_slot(*me)], local_sems.at[a]) for a in range(n)]
        self.first, self.landed, self.passed, self.rest = [], [], [], []
        for a in range(n):
            self.first.append(copy(a, 0, me, sib, src=srcs[a]))
            self.first += [copy(a, 1 + j, me, (*chip, c), src=srcs[a]) for j, chip in enumerate(chips)]
            self.landed += [copy(a, 1 + j, (*chip, c), me) for j, chip in enumerate(chips)]
            self.passed += [copy(a, 4 + j, (*chip, c), sib) for j, chip in enumerate(chips)]
            self.rest.append(copy(a, 0, sib, me))
            self.rest += [copy(a, 4 + j, (*chip, 1 - c), me) for j, chip in enumerate(chips)]

    def start(self):
        for cp in self.mine + self.first:
            cp.start()

    def forward(self):
        for arrived, onward in zip(self.landed, self.passed):
            arrived.wait_recv()
            onward.start()

    def wait(self):
        for cp in self.rest:
            cp.wait_recv()
        for cp in self.first + self.passed:
            cp.wait_send()
        for cp in self.mine:
            cp.wait()


def _norm_inproj(x, g, win_pt, conv_w, *, tm=512):
    t, d = x.shape
    n_all = win_pt.shape[0]
    n_qkv = 3 * ATTN_W
    n_bcx = 3 * CONV_CH
    assert n_all == n_qkv + F_PAD + n_bcx
    tm = min(tm, t)
    ch = CONV_CH

    def body(x_ref, g_ref, w_ref, cw_ref, n_ref, qkv_ref, f_ref, bcx_ref, cv_ref, ext):
        h = x_ref[...]
        n = (h * _rstd(h) * g_ref[...]).astype(BF16)
        n_ref[...] = n
        for c0 in range(0, n_qkv, 512):
            acc = _dot_nt(n, w_ref[c0:c0 + 512, :])
            if c0 < ATTN_W:
                acc = acc * (Q_SCALE * LOG2E)
            qkv_ref[:, c0:c0 + 512] = acc.astype(BF16)
        f_ref[...] = _dot_nt(n, w_ref[n_qkv:n_qkv + F_PAD, :])
        bcx = []
        for k in range(3):
            c0 = n_qkv + F_PAD + k * ch
            v = _dot_nt(n, w_ref[c0:c0 + ch, :]).astype(BF16)
            bcx_ref[:, k * ch:(k + 1) * ch] = v
            bcx.append(v.astype(F32))
        @pl.when(pl.program_id(0) == 0)
        def _():
            ext[tm:tm + CONV_HALO, :] = jnp.zeros((CONV_HALO, ch), F32)
        ext[0:CONV_HALO, :] = ext[tm:tm + CONV_HALO, :]
        ext[CONV_HALO:CONV_HALO + tm, :] = bcx[1] * bcx[2]
        conv = (cw_ref[0:1, :] * ext[CONV_HALO - 2:CONV_HALO - 2 + tm, :]
                + cw_ref[1:2, :] * ext[CONV_HALO - 1:CONV_HALO - 1 + tm, :]
                + cw_ref[2:3, :] * ext[CONV_HALO:CONV_HALO + tm, :])
        cv_ref[...] = (bcx[0] * conv).astype(BF16)

    return pl.pallas_call(
        body, name="norm_inproj", grid=(t // tm,),
        in_specs=[pl.BlockSpec((tm, d), lambda i: (i, 0)), _const_spec((1, d)), _const_spec((n_all, d)),
                  _const_spec((8, ch))],
        out_specs=[pl.BlockSpec((tm, d), lambda i: (i, 0)), pl.BlockSpec((tm, n_qkv), lambda i: (i, 0)),
                   pl.BlockSpec((tm, F_PAD), lambda i: (i, 0)), pl.BlockSpec((tm, n_bcx), lambda i: (i, 0)),
                   pl.BlockSpec((tm, ch), lambda i: (i, 0))],
        out_shape=[jax.ShapeDtypeStruct((t, d), BF16), jax.ShapeDtypeStruct((t, n_qkv), BF16),
                   jax.ShapeDtypeStruct((t, F_PAD), F32), jax.ShapeDtypeStruct((t, n_bcx), BF16),
                   jax.ShapeDtypeStruct((t, ch), BF16)],
        scratch_shapes=[pltpu.VMEM((CONV_HALO + tm, ch), F32)],
        compiler_params=_cp(("arbitrary",)),
    )(x, g, win_pt, conv_w)


def _head_lanes(h):
    lane = lax.broadcasted_iota(jnp.int32, (1, PAIR), 1)
    hh = h % 2
    return lane, lane // HEAD_DIM == hh, HEAD_DIM * (1 - hh)


def _pieces(col):
    hi = col.astype(BF16).astype(F32)
    r1 = col - hi
    mid = r1.astype(BF16).astype(F32)
    lo = (r1 - mid).astype(BF16).astype(F32)
    return hi, mid, lo


def _put_pieces(lane, first, col, other):
    hi, mid, lo = _pieces(col)
    return jnp.where(lane == first, hi, jnp.where(lane == first + 1, mid, jnp.where(lane == first + 2, lo, other)))


def _fgate_prep(flog, b_f, qkv, *, tm=512):
    t = flog.shape[0]
    tm = min(tm, t)

    def body(f_ref, b_ref, qkv_ref, qat_ref, ka_ref, vat_ref, sg_ref, carry):
        @pl.when(pl.program_id(0) == 0)
        def _():
            carry[...] = jnp.zeros_like(carry)
        z = f_ref[...] + b_ref[...]
        e = jnp.exp(-jnp.abs(z))
        logf = jnp.minimum(z, 0.0) - jnp.log(1.0 + e)
        sg_ref[...] = jnp.where(z >= 0, e, 1.0) / (1.0 + e)
        r = lax.broadcasted_iota(jnp.int32, (tm, tm), 0)
        c = lax.broadcasted_iota(jnp.int32, (tm, tm), 1)
        tri = (c <= r).astype(F32)
        cs = jnp.dot(tri, logf, preferred_element_type=F32, precision=lax.Precision.HIGHEST) + carry[...]
        carry[...] = cs[tm - 1:tm, :]
        cs2 = cs * LOG2E
        for h in range(N_HEADS):
            lane, head, aux = _head_lanes(h)
            p0 = (h // 2) * PAIR
            ones = ((lane >= aux + AUX_LSE) & (lane <= aux + AUX_ROWSUM)).astype(F32)
            bias = (lane >= aux + AUX_BIAS) & (lane < aux + AUX_BIAS + 3)
            k_aux = _put_pieces(lane, aux + AUX_BIAS, cs2[:, h:h + 1], ones)
            q_aug = jnp.where(head, qkv_ref[:, p0:p0 + PAIR], jnp.where(bias, -1.0, 0.0).astype(BF16))
            v_aug = jnp.where(head, qkv_ref[:, 2 * ATTN_W + p0:2 * ATTN_W + p0 + PAIR],
                              jnp.where(bias, 1.0, 0.0).astype(BF16))
            qat_ref[h] = q_aug.T
            ka_ref[h] = jnp.where(head, qkv_ref[:, ATTN_W + p0:ATTN_W + p0 + PAIR], k_aux.astype(BF16))
            vat_ref[h] = v_aug.T

    aug = lambda: pl.BlockSpec((N_HEADS, tm, PAIR), lambda i: (0, i, 0))
    aug_t = lambda: pl.BlockSpec((N_HEADS, PAIR, tm), lambda i: (0, 0, i))
    aug_shape = jax.ShapeDtypeStruct((N_HEADS, t, PAIR), BF16)
    aug_t_shape = jax.ShapeDtypeStruct((N_HEADS, PAIR, t), BF16)
    return pl.pallas_call(
        body, name="fgate_prep", grid=(t // tm,),
        in_specs=[pl.BlockSpec((tm, F_PAD), lambda i: (i, 0)), _const_spec((1, F_PAD)),
                  pl.BlockSpec((tm, 3 * ATTN_W), lambda i: (i, 0))],
        out_specs=[aug_t(), aug(), aug_t(), pl.BlockSpec((tm, F_PAD), lambda i: (i, 0))],
        out_shape=[aug_t_shape, aug_shape, aug_t_shape, jax.ShapeDtypeStruct((t, F_PAD), F32)],
        scratch_shapes=[pltpu.VMEM((1, F_PAD), F32)],
        compiler_params=_cp(("arbitrary",)),
    )(flog, b_f, qkv)


def _put_pieces_t(row, first, vec, other):
    hi, mid, lo = _pieces(vec)
    return jnp.where(row == first, hi, jnp.where(row == first + 1, mid, jnp.where(row == first + 2, lo, other)))


def _attn_fwd(q_aug_t, k_aug, v_aug_t, shards, *, tq=1024):
    t = k_aug.shape[1]
    tq = min(tq, t)
    tk = tq // 2
    nq = t // tq
    n_pairs = ATTN_W // PAIR
    n_sh = len(shards)
    forward_step = (3 * n_pairs * nq) // 4

    def body(qt_ref, k_ref, vt_ref, *rest):
        o_ref, qb_ref, qbt_ref = rest[n_sh:n_sh + 3]
        s_scr = rest[2 * n_sh + 3]
        gather = _TwoLevelGather(rest[:n_sh], rest[n_sh + 3:2 * n_sh + 3], *rest[2 * n_sh + 4:])
        i = pl.program_id(1)
        step = pl.program_id(0) * nq + i

        @pl.when(step == 0)
        def _():
            gather.start()

        @pl.when(step == forward_step)
        def _():
            gather.forward()
        key = lax.broadcasted_iota(jnp.int32, (tk, tq), 0)
        qry = lax.broadcasted_iota(jnp.int32, (tk, tq), 1)
        qt = [qt_ref[0], qt_ref[1]]

        def logits(hh, tile, slot, diag):
            s = _dot(k_ref[hh, pl.ds(pl.multiple_of(tile * tk, tk), tk), :], qt[hh])
            if diag:
                s = jnp.where(key + (tile * tk - i * tq) <= qry, s, NEG)
            s_scr[hh, slot] = s
            return jnp.max(s, axis=0, keepdims=True)

        def probs(hh, tile, slot, m, acc, tmax):
            mn = jnp.maximum(m, tmax)
            p = jnp.exp2(s_scr[hh, slot] - mn).astype(BF16)
            acc = jnp.exp2(m - mn) * acc + _dot(vt_ref[hh, :, pl.ds(pl.multiple_of(tile * tk, tk), tk)], p)
            return mn, acc

        def advance(carry, prev, slot, nxt, diag=False):
            out = []
            for hh in range(2):
                m, acc, tmax = carry[hh]
                m, acc = probs(hh, prev, slot, m, acc, tmax)
                out.append((m, acc, logits(hh, nxt, 1 - slot, diag)))
            return tuple(out)

        def two_tiles(jj, carry):
            carry = advance(carry, jnp.where(jj == 0, 2 * i, 2 * jj - 1), 1, 2 * jj)
            return advance(carry, 2 * jj, 0, 2 * jj + 1)

        init = tuple((jnp.full((1, tq), NEG, F32), jnp.zeros((PAIR, tq), F32), logits(hh, 2 * i + 1, 0, True))
                     for hh in range(2))
        carry = advance(init, 2 * i + 1, 0, 2 * i, diag=True)
        carry = lax.fori_loop(0, i // 2, lambda jj, c: two_tiles(2 * jj + 1, two_tiles(2 * jj, c)), carry)
        carry = lax.cond(i % 2 == 1, lambda c: two_tiles(i - 1, c), lambda c: c, carry)
        last = jnp.where(i == 0, 2 * i, 2 * i - 1)
        row = lax.broadcasted_iota(jnp.int32, (PAIR, 1), 0)
        res = []
        for hh in range(2):
            aux = HEAD_DIM * (1 - hh)
            m, acc, tmax = carry[hh]
            m, acc = probs(hh, last, 1, m, acc, tmax)
            l = acc[aux + AUX_BIAS:aux + AUX_BIAS + 1, :]
            qbt = _put_pieces_t(row, aux + AUX_LSE, -(m + jnp.log2(l)), qt[hh].astype(F32))
            qbt_ref[hh] = qbt.astype(BF16)
            qb_ref[hh] = qbt.astype(BF16).T
            res.append(acc * (1.0 / l))
        o_ref[...] = jnp.where(row < HEAD_DIM, res[0], res[1]).astype(BF16).T

        @pl.when((pl.program_id(0) == n_pairs - 1) & (i == nq - 1))
        def _():
            gather.wait()

    res = pl.pallas_call(
        body, name="attn_fwd", grid=(n_pairs, nq),
        in_specs=[pl.BlockSpec((2, PAIR, tq), lambda p, i: (p, 0, i)),
                  pl.BlockSpec((2, t, PAIR), lambda p, i: (p, 0, 0)),
                  pl.BlockSpec((2, PAIR, t), lambda p, i: (p, 0, 0))] + [HBM_SPEC] * n_sh,
        out_specs=[pl.BlockSpec((tq, PAIR), lambda p, i: (i, p)),
                   pl.BlockSpec((2, tq, PAIR), lambda p, i: (p, i, 0)),
                   pl.BlockSpec((2, PAIR, tq), lambda p, i: (p, 0, i))] + [HBM_SPEC] * n_sh,
        out_shape=[jax.ShapeDtypeStruct((t, ATTN_W), BF16), jax.ShapeDtypeStruct((N_HEADS, t, PAIR), BF16),
                   jax.ShapeDtypeStruct((N_HEADS, PAIR, t), BF16)]
        + [jax.ShapeDtypeStruct((N_DEV,) + s.shape, s.dtype) for s in shards],
        scratch_shapes=[pltpu.VMEM((2, 2, tk, tq), F32)] + _Exchange.scratch(n_sh),
        compiler_params=_cp(("arbitrary", "arbitrary")),
    )(q_aug_t, k_aug, v_aug_t, *shards)
    return res[0], res[1], res[2], res[3:]


def _prev_halo(tm, halo):
    return lambda i: (jnp.maximum(i * (tm // halo) - 1, 0), 0)


def _next_halo(tm, halo, t):
    return lambda i: (jnp.minimum((i + 1) * (tm // halo), t // halo - 1), 0)


def _mlp_tile(hh, g_ref, wu_ref, wd_ref, n_ref, a_ref, z_ref):
    n_blk, _, fb = wu_ref.shape
    n = (hh * _rstd(hh) * g_ref[...]).astype(BF16)
    n_ref[...] = n
    acc = hh
    for k in range(n_blk):
        a = _dot(n, wu_ref[k])
        zz = jnp.square(jnp.maximum(a, 0.0)).astype(BF16)
        a_ref[:, k * fb:(k + 1) * fb] = a.astype(BF16)
        z_ref[:, k * fb:(k + 1) * fb] = zz
        acc = acc + _dot(zz, wd_ref[k * fb:(k + 1) * fb, :])
    return acc


def _outproj(att, cv, x, wout, *, tm=512):
    t, d = x.shape
    tm = min(tm, t)

    def body(a_ref, c_ref, x_ref, w_ref, h_ref):
        h_ref[...] = x_ref[...] + _dot(a_ref[...], w_ref[0:ATTN_W, :]) + _dot(c_ref[...], w_ref[ATTN_W:, :])

    return pl.pallas_call(
        body, name="outproj", grid=(t // tm,),
        in_specs=[pl.BlockSpec((tm, ATTN_W), lambda i: (i, 0)), pl.BlockSpec((tm, CONV_CH), lambda i: (i, 0)),
                  pl.BlockSpec((tm, d), lambda i: (i, 0)), _const_spec(wout.shape)],
        out_specs=pl.BlockSpec((tm, d), lambda i: (i, 0)),
        out_shape=jax.ShapeDtypeStruct((t, d), F32),
        compiler_params=_cp(("parallel",)),
    )(att, cv, x, wout)


def _mlp_fwd(h, g, wup, wdown, *, name, tm=512):
    t, d = h.shape
    n_blk, _, fb = wup.shape
    f = n_blk * fb
    tm = min(tm, t)

    def body(h_ref, g_ref, wu_ref, wd_ref, ho_ref, n_ref, a_ref, z_ref):
        ho_ref[...] = _mlp_tile(h_ref[...], g_ref, wu_ref, wd_ref, n_ref, a_ref, z_ref)

    row = lambda n_: pl.BlockSpec((tm, n_), lambda i: (i, 0))
    return pl.pallas_call(
        body, name=name, grid=(t // tm,),
        in_specs=[row(d), _const_spec((1, d)), _const_spec(wup.shape), _const_spec(wdown.shape)],
        out_specs=[row(d), row(d), row(f), row(f)],
        out_shape=[jax.ShapeDtypeStruct((t, d), F32), jax.ShapeDtypeStruct((t, d), BF16),
                   jax.ShapeDtypeStruct((t, f), BF16), jax.ShapeDtypeStruct((t, f), BF16)],
        compiler_params=_cp(("parallel",)),
    )(h, g, wup, wdown)


def _mlp_fwd_loss(h, g, wup, wdown, g_out, target, *, name, tm=512):
    t, d = h.shape
    n_blk, _, fb = wup.shape
    f = n_blk * fb
    tm = min(tm, t)
    nsteps = t // tm

    def body(h_ref, g_ref, wu_ref, wd_ref, go_ref, y_ref, loss_ref, dh_ref, dhb_ref, dg_ref, n_ref, a_ref, z_ref, lacc):
        i = pl.program_id(0)

        @pl.when(i == 0)
        def _():
            lacc[...] = jnp.zeros_like(lacc)
            dg_ref[...] = jnp.zeros_like(dg_ref)
        hv = _mlp_tile(h_ref[...], g_ref, wu_ref, wd_ref, n_ref, a_ref, z_ref)
        gv = go_ref[...]
        r = _rstd(hv)
        xhat = hv * r
        err = xhat * gv - y_ref[...]
        lacc[...] += _rows8(err * err)
        dout = err * (1.0 / d)
        dy = dout * gv
        dg_ref[...] += _rows8(dout * xhat)
        dh = r * (dy - xhat * jnp.mean(dy * xhat, axis=-1, keepdims=True))
        dh_ref[...] = dh
        dhb_ref[...] = dh.astype(BF16)

        @pl.when(i == nsteps - 1)
        def _():
            loss_ref[...] = jnp.full(loss_ref.shape, (0.5 / d) * jnp.sum(lacc[...]), F32)

    row = lambda n_: pl.BlockSpec((tm, n_), lambda i: (i, 0))
    return pl.pallas_call(
        body, name=name, grid=(nsteps,),
        in_specs=[row(d), _const_spec((1, d)), _const_spec(wup.shape), _const_spec(wdown.shape), _const_spec((1, d)),
                  row(d)],
        out_specs=[pl.BlockSpec((8, 128), lambda i: (0, 0)), row(d), row(d), pl.BlockSpec((8, d), lambda i: (0, 0)),
                   row(d), row(f), row(f)],
        out_shape=[jax.ShapeDtypeStruct((8, 128), F32), jax.ShapeDtypeStruct((t, d), F32),
                   jax.ShapeDtypeStruct((t, d), BF16),
                   jax.ShapeDtypeStruct((8, d), F32), jax.ShapeDtypeStruct((t, d), BF16),
                   jax.ShapeDtypeStruct((t, f), BF16), jax.ShapeDtypeStruct((t, f), BF16)],
        scratch_shapes=[pltpu.VMEM((8, d), F32)],
        compiler_params=_cp(("arbitrary",)),
    )(h, g, wup, wdown, g_out, target)


def _pool_inv_count(i, tm):
    tglob = (i * tm + lax.broadcasted_iota(jnp.int32, (tm, 1), 0) + 1).astype(F32)
    return [1.0 / jnp.minimum(tglob, float(w)) for w in POOL_WINDOWS]


def _pool_fwd(h, g, poolw, scale, *, tm=512):
    t, d = h.shape
    tm = min(tm, t)
    cg = d // len(POOL_WINDOWS)

    def body(h_ref, hh_ref, g_ref, w_ref, s_ref, ho_ref, p_ref, ext):
        i = pl.program_id(0)
        hv = h_ref[...]
        halo = hh_ref[...]
        n = hv * _rstd(hv) * g_ref[...]
        ext[0:POOL_HALO, :] = jnp.where(i == 0, 0.0, halo * _rstd(halo) * g_ref[...])
        ext[POOL_HALO:POOL_HALO + tm, :] = n
        inv = _pool_inv_count(i, tm)
        for gi, w in enumerate(POOL_WINDOWS):
            cs = slice(gi * cg, (gi + 1) * cg)
            s = ext[POOL_HALO:POOL_HALO + tm, cs]
            for j in range(1, w):
                s = s + ext[POOL_HALO - j:POOL_HALO - j + tm, cs]
            pooled = (s * inv[gi] - n[:, cs]).astype(BF16)
            p_ref[:, cs] = pooled
            ho_ref[:, cs] = hv[:, cs] + _dot(pooled, w_ref[gi]) * s_ref[:, cs]

    row = lambda: pl.BlockSpec((tm, d), lambda i: (i, 0))
    return pl.pallas_call(
        body, name="pool_fwd", grid=(t // tm,),
        in_specs=[row(), pl.BlockSpec((POOL_HALO, d), _prev_halo(tm, POOL_HALO)), _const_spec((1, d)),
                  _const_spec(poolw.shape), _const_spec((1, d))],
        out_specs=[row(), row()],
        out_shape=[jax.ShapeDtypeStruct((t, d), F32), jax.ShapeDtypeStruct((t, d), BF16)],
        scratch_shapes=[pltpu.VMEM((POOL_HALO + tm, d), F32)],
        compiler_params=_cp(("parallel",)),
    )(h, h, g, poolw, scale)


def _mm_tn(a, b, *, name, ta, tb, tt, blocked_out=False, out_dtype=F32):
    t, ka = a.shape
    n = b.shape[1]
    ta, tb, tt = min(ta, ka), min(tb, n), min(tt, t)
    nt = t // tt

    def body(a_ref, b_ref, o_ref, acc):
        @pl.when(pl.program_id(2) == 0)
        def _():
            acc[...] = jnp.zeros_like(acc)
        acc[...] += _dot_tn(a_ref[...].astype(BF16), b_ref[...].astype(BF16))

        @pl.when(pl.program_id(2) == nt - 1)
        def _():
            o_ref[...] = acc[...].astype(out_dtype)

    if blocked_out:
        assert ta == ka
        out_shape = jax.ShapeDtypeStruct((n // tb, ka, tb), out_dtype)
        out_spec = pl.BlockSpec((None, ta, tb), lambda i, j, k: (j, i, 0))
    else:
        out_shape = jax.ShapeDtypeStruct((ka, n), out_dtype)
        out_spec = pl.BlockSpec((ta, tb), lambda i, j, k: (i, j))
    return pl.pallas_call(
        body, name=name, grid=(ka // ta, n // tb, nt),
        in_specs=[pl.BlockSpec((tt, ta), lambda i, j, k: (k, i)), pl.BlockSpec((tt, tb), lambda i, j, k: (k, j))],
        out_specs=out_spec, out_shape=out_shape, scratch_shapes=[pltpu.VMEM((ta, tb), F32)],
        compiler_params=_cp(("parallel", "parallel", "arbitrary")),
    )(a, b)


def _mm_tn_cat(a_list, b_list, *, name, tt, out_dtype=BF16):
    t = a_list[0].shape[0]
    ta, tb = a_list[0].shape[1], b_list[0].shape[1]
    na, nb = len(a_list), len(b_list)
    tt = min(tt, t)
    nt = t // tt

    def body(*refs):
        a_refs, b_refs, o_ref, acc = refs[:na], refs[na:na + nb], refs[na + nb], refs[na + nb + 1]
        i, j, k = pl.program_id(0), pl.program_id(1), pl.program_id(2)

        @pl.when(k == 0)
        def _():
            acc[...] = jnp.zeros_like(acc)
        for ia in range(na):
            for ib in range(nb):
                @pl.when((i == ia) & (j == ib))
                def _(ia=ia, ib=ib):
                    acc[...] += _dot_tn(a_refs[ia][...].astype(BF16), b_refs[ib][...].astype(BF16))

        @pl.when(k == nt - 1)
        def _():
            o_ref[...] = acc[...].astype(out_dtype)

    def held(m, axis):
        def index(i, j, k):
            cur = (i, j)[axis]
            return (jnp.where(cur == m, k, jnp.where(cur < m, 0, nt - 1)), 0)
        return index

    return pl.pallas_call(
        body, name=name, grid=(na, nb, nt),
        in_specs=[pl.BlockSpec((tt, ta), held(m, 0)) for m in range(na)]
        + [pl.BlockSpec((tt, tb), held(m, 1)) for m in range(nb)],
        out_specs=pl.BlockSpec((ta, tb), lambda i, j, k: (i, j)),
        out_shape=jax.ShapeDtypeStruct((na * ta, nb * tb), out_dtype), scratch_shapes=[pltpu.VMEM((ta, tb), F32)],
        compiler_params=_cp(("arbitrary", "arbitrary", "arbitrary")),
    )(*a_list, *b_list)


def _mlp_bwd(dho, h, a, g, wup, wdown, *, name, tm=512):
    t, d = h.shape
    n_blk, _, fb = wup.shape
    f = n_blk * fb
    tm = min(tm, t)

    def body(do_ref, h_ref, a_ref, g_ref, wu_ref, wd_ref, dh_ref, da_ref, dg_ref):
        @pl.when(pl.program_id(0) == 0)
        def _():
            dg_ref[...] = jnp.zeros_like(dg_ref)
        dho_v = do_ref[...]
        dob = dho_v.astype(BF16)
        dn = jnp.zeros((tm, d), F32)
        for k in range(n_blk):
            dz = _dot_nt(dob, wd_ref[k * fb:(k + 1) * fb, :])
            da = (dz * (2.0 * jnp.maximum(a_ref[:, k * fb:(k + 1) * fb].astype(F32), 0.0))).astype(BF16)
            da_ref[:, k * fb:(k + 1) * fb] = da
            dn = dn + _dot_nt(da, wu_ref[k])
        dh, dg = _norm_bwd(dn, h_ref[...], g_ref[...])
        dh_ref[...] = dho_v + dh
        dg_ref[...] += dg

    row = lambda n_: pl.BlockSpec((tm, n_), lambda i: (i, 0))
    return pl.pallas_call(
        body, name=name, grid=(t // tm,),
        in_specs=[row(d), row(d), row(f), _const_spec((1, d)), _const_spec(wup.shape), _const_spec(wdown.shape)],
        out_specs=[row(d), row(f), pl.BlockSpec((8, d), lambda i: (0, 0))],
        out_shape=[jax.ShapeDtypeStruct((t, d), F32), jax.ShapeDtypeStruct((t, f), BF16),
                   jax.ShapeDtypeStruct((8, d), F32)],
        compiler_params=_cp(("arbitrary",)),
    )(dho, h, a, g, wup, wdown)


def _pool_bwd(dho, h, pooled, g, poolw, scale, *, tm=512):
    t, d = h.shape
    tm = min(tm, t)
    ng = len(POOL_WINDOWS)
    cg = d // ng
    nsteps = t // tm

    def body(do_ref, dn_ref, h_ref, p_ref, g_ref, w_ref, s_ref, dh_ref, dhb_ref, dw_ref, ds_ref, dg_ref, ext):
        i = pl.program_id(0)

        @pl.when(i == 0)
        def _():
            dw_ref[...] = jnp.zeros_like(dw_ref)
            ds_ref[...] = jnp.zeros_like(ds_ref)
            dg_ref[...] = jnp.zeros_like(dg_ref)
        dho_v = do_ref[...]
        sv = s_ref[...]
        dyp = (dho_v * sv).astype(BF16)
        dyp_halo = (dn_ref[...] * sv).astype(BF16)
        inv = _pool_inv_count(i, tm)
        tnext = ((i + 1) * tm + lax.broadcasted_iota(jnp.int32, (POOL_HALO, 1), 0) + 1).astype(F32)
        last = i == nsteps - 1
        ypre_parts, dpooled_parts = [], []
        for gi, w in enumerate(POOL_WINDOWS):
            cs = slice(gi * cg, (gi + 1) * cg)
            pg = p_ref[:, cs]
            ypre_parts.append(_dot(pg, w_ref[gi]))
            dw_ref[gi] += _dot_tn(pg, dyp[:, cs])
            dpool = _dot_nt(dyp[:, cs], w_ref[gi])
            dpooled_parts.append(dpool)
            ext[0:tm, cs] = dpool * inv[gi]
            dpool_halo = _dot_nt(dyp_halo[:, cs], w_ref[gi]) * (1.0 / jnp.minimum(tnext, float(w)))
            ext[tm:tm + POOL_HALO, cs] = jnp.where(last, 0.0, dpool_halo)
        ds_ref[...] += _rows8(dho_v * jnp.concatenate(ypre_parts, axis=1))
        dn_parts = []
        for gi, w in enumerate(POOL_WINDOWS):
            cs = slice(gi * cg, (gi + 1) * cg)
            s = ext[0:tm, cs]
            for j in range(1, w):
                s = s + ext[j:j + tm, cs]
            dn_parts.append(s - dpooled_parts[gi])
        dh, dg = _norm_bwd(jnp.concatenate(dn_parts, axis=1), h_ref[...], g_ref[...])
        dh = dho_v + dh
        dh_ref[...] = dh
        dhb_ref[...] = dh.astype(BF16)
        dg_ref[...] += dg

    row = lambda: pl.BlockSpec((tm, d), lambda i: (i, 0))
    acc8 = lambda: pl.BlockSpec((8, d), lambda i: (0, 0))
    return pl.pallas_call(
        body, name="pool_bwd", grid=(nsteps,),
        in_specs=[row(), pl.BlockSpec((POOL_HALO, d), _next_halo(tm, POOL_HALO, t)), row(), row(),
                  _const_spec((1, d)), _const_spec(poolw.shape), _const_spec((1, d))],
        out_specs=[row(), row(), pl.BlockSpec((ng, cg, cg), lambda i: (0, 0, 0)), acc8(), acc8()],
        out_shape=[jax.ShapeDtypeStruct((t, d), F32), jax.ShapeDtypeStruct((t, d), BF16),
                   jax.ShapeDtypeStruct((ng, cg, cg), F32),
                   jax.ShapeDtypeStruct((8, d), F32), jax.ShapeDtypeStruct((8, d), F32)],
        scratch_shapes=[pltpu.VMEM((tm + POOL_HALO, d), F32)],
        compiler_params=_cp(("arbitrary",)),
    )(dho, dho, h, pooled, g, poolw, scale)


def _outproj_conv_bwd(dh, o, wout, bcx, conv_w, *, tm=512):
    t, d = dh.shape
    tm = min(tm, t)
    ch = CONV_CH
    nsteps = t // tm

    def body(dh_ref, o_ref, w_ref, b_ref, c_ref, x_ref, hc_ref, hx_ref, cw_ref,
             da_ref, dat_ref, db_ref, dw_ref, ext_u, ext_d):
        s = pl.program_id(0)

        @pl.when(s == 0)
        def _():
            dw_ref[...] = jnp.zeros_like(dw_ref)
            ext_d[0:CONV_HALO, :] = jnp.zeros((CONV_HALO, ch), F32)
        dhb = dh_ref[...].astype(BF16)
        for p in range(ATTN_W // PAIR):
            datt = _dot_nt(dhb, w_ref[p * PAIR:(p + 1) * PAIR, :])
            prod = datt * o_ref[:, p * PAIR:(p + 1) * PAIR].astype(F32)
            for hh in range(2):
                lane, head, aux = _head_lanes(hh)
                delta = jnp.sum(jnp.where(head, prod, 0.0), axis=1, keepdims=True)
                aug = _put_pieces(lane, aux + AUX_BIAS, -delta, jnp.where(head, datt, 0.0))
                da_ref[2 * p + hh] = aug.astype(BF16)
                dat_ref[2 * p + hh] = aug.astype(BF16).T
        dcv = _dot_nt(dhb, w_ref[ATTN_W:, :])
        b, c, x = b_ref[...].astype(F32), c_ref[...].astype(F32), x_ref[...].astype(F32)
        ext_u[0:CONV_HALO, :] = jnp.where(s == nsteps - 1, 0.0, hc_ref[...].astype(F32) * hx_ref[...].astype(F32))
        ext_u[CONV_HALO:CONV_HALO + tm, :] = c * x
        dconv = dcv * b
        ext_d[tm:tm + CONV_HALO, :] = ext_d[0:CONV_HALO, :]
        ext_d[0:tm, :] = dconv
        u = [ext_u[CONV_HALO - 2 + k:CONV_HALO - 2 + k + tm, :] for k in range(3)]
        conv = cw_ref[0:1, :] * u[0] + cw_ref[1:2, :] * u[1] + cw_ref[2:3, :] * u[2]
        du = (cw_ref[2:3, :] * dconv + cw_ref[1:2, :] * ext_d[1:1 + tm, :] + cw_ref[0:1, :] * ext_d[2:2 + tm, :])
        db_ref[:, 0:ch] = (dcv * conv).astype(BF16)
        db_ref[:, ch:2 * ch] = (du * x).astype(BF16)
        db_ref[:, 2 * ch:3 * ch] = (du * c).astype(BF16)
        for k in range(3):
            dw_ref[k] += _rows8(dconv * u[k])

    rev = lambda s: nsteps - 1 - s
    row = lambda n_: pl.BlockSpec((tm, n_), lambda s: (rev(s), 0))
    col = lambda k: pl.BlockSpec((tm, ch), lambda s: (rev(s), k))
    prev = lambda k: pl.BlockSpec((CONV_HALO, ch), lambda s: (_prev_halo(tm, CONV_HALO)(rev(s))[0], k))
    return pl.pallas_call(
        body, name="outproj_conv_bwd", grid=(nsteps,),
        in_specs=[row(d), row(ATTN_W), _const_spec(wout.shape), col(0), col(1), col(2), prev(1), prev(2),
                  _const_spec((8, ch))],
        out_specs=[pl.BlockSpec((N_HEADS, tm, PAIR), lambda s: (0, rev(s), 0)),
                   pl.BlockSpec((N_HEADS, PAIR, tm), lambda s: (0, 0, rev(s))),
                   row(3 * ch), pl.BlockSpec((3, 8, ch), lambda s: (0, 0, 0))],
        out_shape=[jax.ShapeDtypeStruct((N_HEADS, t, PAIR), BF16), jax.ShapeDtypeStruct((N_HEADS, PAIR, t), BF16),
                   jax.ShapeDtypeStruct((t, 3 * ch), BF16), jax.ShapeDtypeStruct((3, 8, ch), F32)],
        scratch_shapes=[pltpu.VMEM((CONV_HALO + tm, ch), F32), pltpu.VMEM((tm + CONV_HALO, ch), F32)],
        compiler_params=_cp(("arbitrary",)),
    )(dh, o, wout, bcx, bcx, bcx, bcx, bcx, conv_w)


def _attn_bwd(q_bwd, do_aug, q_bwd_t, do_aug_t, k_aug, v_aug_t, gblocks, *, tq=1024):
    t = q_bwd.shape[1]
    tq = min(tq, t)
    tk = tq // 2
    nq, nk = t // tq, t // tk
    n_pairs = ATTN_W // PAIR
    n_g = len(gblocks)

    def body(q_ref, do_ref, qt_ref, dot_ref, k_ref, vt_ref, *rest):
        dq_ref, dqx_ref, dk_ref, dkx_ref, dv_ref = rest[n_g:n_g + 5]
        dq_scr = rest[2 * n_g + 5]
        scatter = _Exchange(rest[:n_g], rest[n_g + 5:2 * n_g + 5], *rest[2 * n_g + 6:], gather=False)
        j = pl.program_id(1)

        @pl.when((pl.program_id(0) == 0) & (j == 0))
        def _():
            scatter.start()

        @pl.when(j == 0)
        def _():
            dq_scr[...] = jnp.zeros_like(dq_scr)
        k = [k_ref[0], k_ref[1]]
        vt = [vt_ref[0], vt_ref[1]]

        def step(i, carry, diag, rows=tq, row0=0):
            qs = pl.multiple_of(i * tq + row0, tk)
            if diag:
                row = lax.broadcasted_iota(jnp.int32, (rows, tk), 0)
                col = lax.broadcasted_iota(jnp.int32, (rows, tk), 1)
            out = []
            for hh in range(2):
                dk_a, dv_a = carry[hh]
                q = q_ref[hh, pl.ds(qs, rows), :]
                dov = do_ref[hh, pl.ds(qs, rows), :]
                p = jnp.exp2(_dot_nt(q, k[hh]))
                if diag:
                    p = jnp.where(col + (j * tk - i * tq - row0) <= row, p, 0.0)
                ds = (p * _dot(dov, vt[hh])).astype(BF16)
                dv_a = dv_a + _dot(dot_ref[hh, :, pl.ds(qs, rows)], p.astype(BF16))
                dk_a = dk_a + _dot(qt_ref[hh, :, pl.ds(qs, rows)], ds)
                dq_scr[hh, pl.ds(qs, rows), :] += _dot(ds, k[hh])
                out.append((dk_a, dv_a))
            return tuple(out)

        zero = (jnp.zeros((PAIR, tk), F32), jnp.zeros((PAIR, tk), F32))
        carry = lax.cond(j % 2 == 0, lambda c: step(j // 2, c, True),
                         lambda c: step(j // 2, c, True, rows=tk, row0=tk), (zero, zero))
        full0 = j // 2 + 1
        odd = (nq - full0) % 2
        carry = lax.cond(odd == 1, lambda c: step(full0, c, False), lambda c: c, carry)
        (dk0, dv0), (dk1, dv1) = lax.fori_loop(
            0, (nq - full0) // 2, lambda ii, c: step(full0 + odd + 2 * ii, c, False, rows=2 * tq), carry)
        first_t = lax.broadcasted_iota(jnp.int32, (PAIR, 1), 0) < HEAD_DIM
        first = lax.broadcasted_iota(jnp.int32, (1, PAIR), 1) < HEAD_DIM
        dk_ref[...] = (jnp.where(first_t, dk0, dk1) * (1.0 / LOG2E)).astype(BF16).T
        dkx_ref[...] = jnp.where(first_t, dk1, dk0).T
        dv_ref[...] = jnp.where(first_t, dv0, dv1).astype(BF16).T

        @pl.when(j == nk - 1)
        def _():
            dq_ref[...] = (jnp.where(first, dq_scr[0], dq_scr[1]) * Q_SCALE).astype(BF16)
            dqx_ref[...] = jnp.where(first, dq_scr[1], dq_scr[0])

        @pl.when((pl.program_id(0) == n_pairs - 1) & (j == nk - 1))
        def _():
            scatter.wait()

    resident = lambda: pl.BlockSpec((2, t, PAIR), lambda p, j: (p, 0, 0))
    resident_t = lambda: pl.BlockSpec((2, PAIR, t), lambda p, j: (p, 0, 0), pipeline_mode=pl.Buffered(1))
    kv_in = lambda: pl.BlockSpec((2, tk, PAIR), lambda p, j: (p, j, 0))
    whole = lambda: pl.BlockSpec((t, PAIR), lambda p, j: (0, p))
    tile = lambda: pl.BlockSpec((tk, PAIR), lambda p, j: (j, p))
    b16 = jax.ShapeDtypeStruct((t, ATTN_W), BF16)
    f32 = jax.ShapeDtypeStruct((t, ATTN_W), F32)
    res = pl.pallas_call(
        body, name="attn_bwd", grid=(n_pairs, nk),
        in_specs=[resident(), resident(), resident_t(), resident_t(), kv_in(),
                  pl.BlockSpec((2, PAIR, tk), lambda p, j: (p, 0, j))] + [HBM_SPEC] * n_g,
        out_specs=[whole(), whole(), tile(), tile(), tile()] + [HBM_SPEC] * n_g,
        out_shape=[b16, f32, b16, f32, b16] + [jax.ShapeDtypeStruct(g.shape, g.dtype) for g in gblocks],
        scratch_shapes=[pltpu.VMEM((2, t, PAIR), F32)] + _Exchange.scratch(n_g),
        compiler_params=_cp(("arbitrary", "arbitrary")),
    )(q_bwd, do_aug, q_bwd_t, do_aug_t, k_aug, v_aug_t, *gblocks)
    return res[:5], res[5:]


def _fgate_bwd(dqx, dkx, sgate, *, tm=256):
    t = sgate.shape[0]
    tm = min(tm, t)
    nsteps = t // tm

    def body(dq_ref, dk_ref, sg_ref, df_ref, dbf_ref, carry):
        @pl.when(pl.program_id(0) == 0)
        def _():
            carry[...] = jnp.zeros_like(carry)
            dbf_ref[...] = jnp.zeros_like(dbf_ref)
        lane = lax.broadcasted_iota(jnp.int32, (ATTN_W, F_PAD), 0)
        head = lax.broadcasted_iota(jnp.int32, (ATTN_W, F_PAD), 1)
        aux = (head // 2) * PAIR + HEAD_DIM * (1 - head % 2)
        valid = head < N_HEADS
        pick_r = (valid & (lane == aux + AUX_ROWSUM)).astype(F32)
        pick_c = (valid & (lane == aux + AUX_BIAS)).astype(F32)
        hp = lax.Precision.HIGHEST
        dcum = (jnp.dot(dq_ref[...], pick_r, preferred_element_type=F32, precision=lax.Precision.HIGH)
                + jnp.dot(dk_ref[...], pick_c, preferred_element_type=F32, precision=lax.Precision.HIGH))
        r = lax.broadcasted_iota(jnp.int32, (tm, tm), 0)
        c = lax.broadcasted_iota(jnp.int32, (tm, tm), 1)
        tri = (c >= r).astype(F32)
        rc = jnp.dot(tri, dcum, preferred_element_type=F32, precision=hp) + carry[...]
        carry[...] = rc[0:1, :]
        df = rc * sg_ref[...]
        df_ref[...] = df.astype(BF16)
        dbf_ref[...] += _rows8(df)

    rev = lambda i: nsteps - 1 - i
    return pl.pallas_call(
        body, name="fgate_bwd", grid=(nsteps,),
        in_specs=[pl.BlockSpec((tm, ATTN_W), lambda i: (rev(i), 0)), pl.BlockSpec((tm, ATTN_W), lambda i: (rev(i), 0)),
                  pl.BlockSpec((tm, F_PAD), lambda i: (rev(i), 0))],
        out_specs=[pl.BlockSpec((tm, F_PAD), lambda i: (rev(i), 0)), pl.BlockSpec((8, F_PAD), lambda i: (0, 0))],
        out_shape=[jax.ShapeDtypeStruct((t, F_PAD), BF16), jax.ShapeDtypeStruct((8, F_PAD), F32)],
        scratch_shapes=[pltpu.VMEM((1, F_PAD), F32)],
        compiler_params=_cp(("arbitrary",)),
    )(dqx, dkx, sgate)


def _inproj_bwd(dq, dk, dv, df, dbcx, dh, x, g, win_pt, gblock, *, tm=512):
    t, d = x.shape
    tm = min(tm, t)
    nsteps = t // tm
    n_qkv = 3 * ATTN_W

    def body(dq_ref, dk_ref, dv_ref, df_ref, db_ref, dh_ref, x_ref, g_ref, w_ref, gb_ref, gx_ref, dg_ref, land_ref,
             *sems):
        scatter = _Exchange([gb_ref], [land_ref], *sems, gather=False)

        @pl.when(pl.program_id(0) == 0)
        def _():
            scatter.start()
            dg_ref[...] = jnp.zeros_like(dg_ref)
        dn = _dot(df_ref[...], w_ref[n_qkv:n_qkv + F_PAD, :])
        for k, r in enumerate((dq_ref, dk_ref, dv_ref)):
            dn = dn + _dot(r[...], w_ref[k * ATTN_W:(k + 1) * ATTN_W, :])
        for k in range(3):
            c0 = n_qkv + F_PAD + k * CONV_CH
            dn = dn + _dot(db_ref[:, k * CONV_CH:(k + 1) * CONV_CH], w_ref[c0:c0 + CONV_CH, :])
        dx, dg = _norm_bwd(dn, x_ref[...], g_ref[...])
        gx_ref[...] = dh_ref[...] + dx
        dg_ref[...] += dg

        @pl.when(pl.program_id(0) == nsteps - 1)
        def _():
            scatter.wait()

    row = lambda n_: pl.BlockSpec((tm, n_), lambda i: (i, 0))
    return pl.pallas_call(
        body, name="inproj_bwd", grid=(nsteps,),
        in_specs=[row(ATTN_W), row(ATTN_W), row(ATTN_W), row(F_PAD), row(3 * CONV_CH), row(d), row(d),
                  _const_spec((1, d)), _const_spec(win_pt.shape), HBM_SPEC],
        out_specs=[row(d), pl.BlockSpec((8, d), lambda i: (0, 0)), HBM_SPEC],
        out_shape=[jax.ShapeDtypeStruct((t, d), F32), jax.ShapeDtypeStruct((8, d), F32),
                   jax.ShapeDtypeStruct(gblock.shape, gblock.dtype)],
        scratch_shapes=_Exchange.scratch(1),
        compiler_params=_cp(("arbitrary",)),
    )(dq, dk, dv, df, dbcx, dh, x, g, win_pt, gblock)


LATE = ("w_out_0", "w_up_0", "w_down_0", "pool_w_1", "w_up_1", "w_down_1")


def _local_step(x, target, gains, b_f, conv_w, pool_scale, win_pt, shards):
    d = x.shape[1]
    n0, qkv, flog, bcx, cv = _norm_inproj(x, gains["mix0"], win_pt, conv_w)
    q_aug_t, k_aug, v_aug_t, sgate = _fgate_prep(flog, b_f, qkv)
    att, q_bwd, q_bwd_t, gathered = _attn_fwd(q_aug_t, k_aug, v_aug_t, [shards[n] for n in LATE])
    g = dict(zip(LATE, gathered))
    wout = g["w_out_0"].reshape(d, d)
    wup0, wup1 = g["w_up_0"], g["w_up_1"]
    wdown0, wdown1 = g["w_down_0"].reshape(-1, d), g["w_down_1"].reshape(-1, d)
    n_grp = len(POOL_WINDOWS)
    cg = d // n_grp
    poolw = g["pool_w_1"].reshape(N_DEV, n_grp, cg // N_DEV, cg).transpose(1, 0, 2, 3).reshape(n_grp, cg, cg)
    h1 = _outproj(att, cv, x, wout)
    h2, n1, a0, z0 = _mlp_fwd(h1, gains["ffn0"], wup0, wdown0, name="mlp_fwd0")
    h3, pooled = _pool_fwd(h2, gains["mix1"], poolw, pool_scale)
    loss, dh4, dh4_b, dg_final, n3, a1, z1 = _mlp_fwd_loss(h3, gains["ffn1"], wup1, wdown1, gains["final"], target,
                                                           name="mlp_fwd1")
    f = a1.shape[1]
    fb = f // N_DEV
    dh3, da1, dg_ffn1 = _mlp_bwd(dh4, h3, a1, gains["ffn1"], wup1, wdown1, name="mlp_bwd1")
    dwdown1 = _mm_tn(z1, dh4_b, name="dwdown1", ta=1024, tb=1024, tt=4096, out_dtype=BF16)
    dwup1 = _mm_tn(n3, da1, name="dwup1", ta=d, tb=fb, tt=4096, blocked_out=True, out_dtype=BF16)
    dh2, dh2_b, dpoolw, dscale, dg_mix1 = _pool_bwd(dh3, h2, pooled, gains["mix1"], poolw, pool_scale)
    dh1, da0, dg_ffn0 = _mlp_bwd(dh2, h1, a0, gains["ffn0"], wup0, wdown0, name="mlp_bwd0")
    dwdown0 = _mm_tn(z0, dh2_b, name="dwdown0", ta=1024, tb=1024, tt=4096, out_dtype=BF16)
    dwup0 = _mm_tn(n1, da0, name="dwup0", ta=d, tb=fb, tt=4096, blocked_out=True, out_dtype=BF16)
    do_aug, do_aug_t, dbcx, dconvw = _outproj_conv_bwd(dh1, att, wout, bcx, conv_w)
    dwout = _mm_tn_cat([att, cv], [dh1], name="dwout", tt=2048)
    gblocks = {
        "w_out_0": dwout.reshape(N_DEV, d // N_DEV, d), "w_up_0": dwup0, "w_up_1": dwup1,
        "w_down_0": dwdown0.reshape(N_DEV, -1, d), "w_down_1": dwdown1.reshape(N_DEV, -1, d),
        "pool_w_1": dpoolw.astype(BF16).reshape(n_grp, N_DEV, cg // N_DEV, cg).transpose(1, 0, 2, 3).reshape(
            N_DEV, n_grp * (cg // N_DEV), cg),
    }
    (dq, dqx, dk, dkx, dv), landed = _attn_bwd(q_bwd, do_aug, q_bwd_t, do_aug_t, k_aug, v_aug_t,
                                               [gblocks[n] for n in LATE])
    df, dbf = _fgate_bwd(dqx, dkx, sgate)
    dwin_t = jnp.concatenate(
        [_mm_tn_cat([dq, dk, dv], [n0], name="dwin_qkv", tt=2048),
         _mm_tn(df, n0, name="dwin_f", ta=F_PAD, tb=d, tt=2048, out_dtype=BF16)[:N_HEADS],
         _mm_tn(dbcx, n0, name="dwin_bcx", ta=512, tb=d, tt=4096, out_dtype=BF16)], axis=0)
    dwin_blocks = dwin_t.reshape(N_DEV, dwin_t.shape[0] // N_DEV, d)
    grad_x, dg_mix0, landed_win = _inproj_bwd(dq, dk, dv, df, dbcx, dh1, x, gains["mix0"], win_pt, dwin_blocks)
    small = dict(mix0=dg_mix0, ffn0=dg_ffn0, mix1=dg_mix1, pool_scale=dscale, ffn1=dg_ffn1, final=dg_final,
                 b_f=dbf, conv_w=dconvw)
    return loss, grad_x, dict(zip(LATE + ("w_in_0",), tuple(landed) + (landed_win,))), small


def _all_gather(shards):
    n = len(shards)

    def body(*refs):
        gather = _TwoLevelGather(refs[:n], refs[n:2 * n], *refs[2 * n:])
        gather.start()
        gather.forward()
        gather.wait()

    return pl.pallas_call(
        body, name="all_gather",
        in_specs=[HBM_SPEC] * n, out_specs=[HBM_SPEC] * n,
        out_shape=[jax.ShapeDtypeStruct((N_DEV,) + s.shape, s.dtype) for s in shards],
        scratch_shapes=[pltpu.SemaphoreType.DMA((7 * n,)), pltpu.SemaphoreType.DMA((7 * n,)),
                        pltpu.SemaphoreType.DMA((n,))],
    )(*shards)


SMALL_ROWS = 16


def _small_allreduce(parts):
    n, _, w = parts.shape
    assert n <= SMALL_ROWS

    def body(p_ref, o_ref, gath, send_sems, recv_sems):
        x, y, c = lax.axis_index("x"), lax.axis_index("y"), lax.axis_index("c")
        my = _slot(x, y, c)
        rows = [jnp.sum(p_ref[i], axis=0, keepdims=True) for i in range(n)]
        rows.append(jnp.zeros((SMALL_ROWS - n, w), F32))
        gath[my] = jnp.concatenate(rows, axis=0)
        copies = []
        for k in range(1, N_DEV):
            px, py, pc = x ^ (k >> 2), y ^ ((k >> 1) & 1), c ^ (k & 1)
            cp = pltpu.make_async_remote_copy(
                src_ref=gath.at[my], dst_ref=gath.at[my], send_sem=send_sems.at[k - 1], recv_sem=recv_sems.at[k - 1],
                device_id=(px, py, pc), device_id_type=MESH)
            cp.start()
            copies.append(cp)
        for cp in copies:
            cp.wait()
        acc = gath[0]
        for d in range(1, N_DEV):
            acc = acc + gath[d]
        o_ref[...] = acc

    return pl.pallas_call(
        body, name="small_allreduce",
        in_specs=[VMEM_SPEC], out_specs=VMEM_SPEC,
        out_shape=jax.ShapeDtypeStruct((SMALL_ROWS, w), F32),
        scratch_shapes=[pltpu.VMEM((N_DEV, SMALL_ROWS, w), F32), pltpu.SemaphoreType.DMA((N_DEV - 1,)),
                        pltpu.SemaphoreType.DMA((N_DEV - 1,))],
    )(parts)


def _adamw(g, w, m, v, *, name, tm=256):
    r, c = g.shape
    tm = tm if r % tm == 0 else r
    bc1 = 1.0 - ADAM_B1 ** ADAM_STEP
    bc2 = 1.0 - ADAM_B2 ** ADAM_STEP

    def body(g_ref, w_ref, m_ref, v_ref, d_ref, nm_ref, nv_ref):
        gv = g_ref[...]
        nm = ADAM_B1 * m_ref[...] + (1.0 - ADAM_B1) * gv
        nv = ADAM_B2 * v_ref[...] + (1.0 - ADAM_B2) * jnp.square(gv)
        nm_ref[...] = nm
        nv_ref[...] = nv
        d_ref[...] = -ADAM_LR * ((nm / bc1) / (jnp.sqrt(nv / bc2) + ADAM_EPS) + ADAM_WD * w_ref[...])

    blk = pl.BlockSpec((tm, c), lambda i: (i, 0))
    shp = jax.ShapeDtypeStruct((r, c), F32)
    return pl.pallas_call(
        body, name=name, grid=(r // tm,), in_specs=[blk] * 4, out_specs=[blk] * 3, out_shape=[shp] * 3,
        compiler_params=_cp(("parallel",)),
    )(g, w, m, v)


def _transpose_cast(a, *, name):
    def body(a_ref, o_ref):
        o_ref[...] = a_ref[...].T.astype(BF16)

    return pl.pallas_call(body, name=name, out_shape=jax.ShapeDtypeStruct(a.shape[::-1], BF16),
                          compiler_params=_cp())(a)


def _adamw_sum_t(parts, w, m, v, *, name):
    bc1 = 1.0 - ADAM_B1 ** ADAM_STEP
    bc2 = 1.0 - ADAM_B2 ** ADAM_STEP

    def body(p_ref, w_ref, m_ref, v_ref, g_ref, d_ref, nm_ref, nv_ref):
        acc = p_ref[0].astype(F32)
        for k in range(1, N_DEV):
            acc = acc + p_ref[k].astype(F32)
        gv = acc.T
        g_ref[...] = gv
        nm = ADAM_B1 * m_ref[...] + (1.0 - ADAM_B1) * gv
        nv = ADAM_B2 * v_ref[...] + (1.0 - ADAM_B2) * jnp.square(gv)
        nm_ref[...] = nm
        nv_ref[...] = nv
        d_ref[...] = -ADAM_LR * ((nm / bc1) / (jnp.sqrt(nv / bc2) + ADAM_EPS) + ADAM_WD * w_ref[...])

    shp = jax.ShapeDtypeStruct(w.shape, F32)
    return pl.pallas_call(body, name=name, out_shape=[shp] * 4, compiler_params=_cp())(parts, w, m, v)


def _adamw_sum(parts, w, m, v, *, name, tm=256):
    _, r, c = parts.shape
    tm = tm if r % tm == 0 else r
    bc1 = 1.0 - ADAM_B1 ** ADAM_STEP
    bc2 = 1.0 - ADAM_B2 ** ADAM_STEP

    def body(p_ref, w_ref, m_ref, v_ref, g_ref, d_ref, nm_ref, nv_ref):
        gv = p_ref[0].astype(F32)
        for k in range(1, N_DEV):
            gv = gv + p_ref[k].astype(F32)
        g_ref[...] = gv
        nm = ADAM_B1 * m_ref[...] + (1.0 - ADAM_B1) * gv
        nv = ADAM_B2 * v_ref[...] + (1.0 - ADAM_B2) * jnp.square(gv)
        nm_ref[...] = nm
        nv_ref[...] = nv
        d_ref[...] = -ADAM_LR * ((nm / bc1) / (jnp.sqrt(nv / bc2) + ADAM_EPS) + ADAM_WD * w_ref[...])

    blk = pl.BlockSpec((tm, c), lambda i: (i, 0))
    shp = jax.ShapeDtypeStruct((r, c), F32)
    return pl.pallas_call(
        body, name=name, grid=(r // tm,), in_specs=[pl.BlockSpec((N_DEV, tm, c), lambda i: (0, i, 0))] + [blk] * 3,
        out_specs=[blk] * 4, out_shape=[shp] * 4, compiler_params=_cp(("parallel",)),
    )(parts, w, m, v)


BIG = ("w_in_0", "w_out_0", "w_up_0", "w_down_0", "pool_w_1", "w_up_1", "w_down_1")
SMALL = ("norm_mix_0", "norm_ffn_0", "norm_mix_1", "pool_scale_1", "norm_ffn_1", "final_norm", "b_f_0", "conv_w_0")
WEIGHTS = ("norm_mix_0", "w_in_0", "b_f_0", "conv_w_0", "w_out_0", "norm_ffn_0", "w_up_0", "w_down_0", "norm_mix_1",
           "pool_w_1", "pool_scale_1", "norm_ffn_1", "w_up_1", "w_down_1", "final_norm")


def _pad_to(a, rows, cols):
    return jnp.pad(a, ((0, rows - a.shape[0]), (0, cols - a.shape[1])))


def _pack_small(p, width):
    rows = [p[n].reshape(1, -1) for n in SMALL[:6]]
    rows.append(_pad_to(p["b_f_0"].reshape(1, -1), 1, width))
    rows.append(_pad_to(p["conv_w_0"], 3, width))
    return _pad_to(jnp.concatenate(rows, axis=0), SMALL_ROWS, width)


def _unpack_small(a, like):
    out = {n: a[i] for i, n in enumerate(SMALL[:6])}
    out["b_f_0"] = a[6, :like["b_f_0"].shape[0]]
    out["conv_w_0"] = a[7:10, :like["conv_w_0"].shape[1]]
    return out


def kernel(x, norm_mix_0, w_in_0, b_f_0, conv_w_0, w_out_0, norm_ffn_0, w_up_0, w_down_0, norm_mix_1, pool_w_1, pool_scale_1, norm_ffn_1, w_up_1, w_down_1, final_norm, loss_target, m_norm_mix_0, m_w_in_0, m_b_f_0, m_conv_w_0, m_w_out_0, m_norm_ffn_0, m_w_up_0, m_w_down_0, m_norm_mix_1, m_pool_w_1, m_pool_scale_1, m_norm_ffn_1, m_w_up_1, m_w_down_1, m_final_norm, v_norm_mix_0, v_w_in_0, v_b_f_0, v_conv_w_0, v_w_out_0, v_norm_ffn_0, v_w_up_0, v_w_down_0, v_norm_mix_1, v_pool_w_1, v_pool_scale_1, v_norm_ffn_1, v_w_up_1, v_w_down_1, v_final_norm):
    w = dict(norm_mix_0=norm_mix_0, w_in_0=w_in_0, b_f_0=b_f_0, conv_w_0=conv_w_0, w_out_0=w_out_0,
             norm_ffn_0=norm_ffn_0, w_up_0=w_up_0, w_down_0=w_down_0, norm_mix_1=norm_mix_1, pool_w_1=pool_w_1,
             pool_scale_1=pool_scale_1, norm_ffn_1=norm_ffn_1, w_up_1=w_up_1, w_down_1=w_down_1, final_norm=final_norm)
    m = dict(norm_mix_0=m_norm_mix_0, w_in_0=m_w_in_0, b_f_0=m_b_f_0, conv_w_0=m_conv_w_0, w_out_0=m_w_out_0,
             norm_ffn_0=m_norm_ffn_0, w_up_0=m_w_up_0, w_down_0=m_w_down_0, norm_mix_1=m_norm_mix_1,
             pool_w_1=m_pool_w_1, pool_scale_1=m_pool_scale_1, norm_ffn_1=m_norm_ffn_1, w_up_1=m_w_up_1,
             w_down_1=m_w_down_1, final_norm=m_final_norm)
    v = dict(norm_mix_0=v_norm_mix_0, w_in_0=v_w_in_0, b_f_0=v_b_f_0, conv_w_0=v_conv_w_0, w_out_0=v_w_out_0,
             norm_ffn_0=v_norm_ffn_0, w_up_0=v_w_up_0, w_down_0=v_w_down_0, norm_mix_1=v_norm_mix_1,
             pool_w_1=v_pool_w_1, pool_scale_1=v_pool_scale_1, norm_ffn_1=v_norm_ffn_1, w_up_1=v_w_up_1,
             w_down_1=v_w_down_1, final_norm=v_final_norm)
    d = x.shape[-1]
    n_in = w_in_0.shape[1] * N_DEV
    n_qkv = 3 * ATTN_W
    pool_g, pool_rows, pool_c = pool_w_1.shape

    def shard2d(p):
        return {n: (p[n].reshape(pool_g * pool_rows, pool_c) if n == "pool_w_1" else p[n]) for n in BIG}
    w2, m2, v2 = shard2d(w), shard2d(m), shard2d(v)

    conv_cols = conv_w_0.shape[1]
    win_g8, conv_g8 = _all_gather([_transpose_cast(w_in_0, name="w_in_t"), _pad_to(conv_w_0, 8, 128)])
    conv_full = conv_g8[:, :, :conv_cols].transpose(1, 0, 2).reshape(8, N_DEV * conv_cols)
    win_t = win_g8.reshape(n_in, d)
    win_pt = jnp.concatenate([win_t[:n_qkv], _pad_to(win_t[n_qkv:n_qkv + N_HEADS], F_PAD, d),
                              win_t[n_qkv + N_HEADS:]], axis=0)

    gains = dict(mix0=norm_mix_0.reshape(1, d), ffn0=norm_ffn_0.reshape(1, d), mix1=norm_mix_1.reshape(1, d),
                 ffn1=norm_ffn_1.reshape(1, d), final=final_norm.reshape(1, d))
    dev = _slot(lax.axis_index("x"), lax.axis_index("y"), lax.axis_index("c"))
    loss8, grad_x, landed, small = _local_step(
        x[0], loss_target[0], gains, _pad_to(b_f_0.reshape(1, -1), 1, F_PAD), conv_full, pool_scale_1.reshape(1, d),
        win_pt, {n: w2[n].astype(BF16) for n in LATE})
    parts = jnp.concatenate(
        [small[k][None] for k in ("mix0", "ffn0", "mix1", "pool_scale", "ffn1", "final")]
        + [_pad_to(small["b_f"], 8, d)[None], jnp.pad(small["conv_w"], ((0, 0), (0, 0), (0, d - CONV_CH))),
           _pad_to(loss8[0:1, 0:1], 8, d)[None]], axis=0)
    tot = _small_allreduce(parts)
    loss = tot[10, 0]
    conv_g = lax.dynamic_slice(tot, (7, dev * conv_cols), (3, conv_cols))
    gs = tot.at[7:10].set(_pad_to(conv_g, 3, d))

    grads, deltas, new_m, new_v = {}, {}, {}, {}
    for n in BIG:
        if n in LATE:
            gr, dl, nm, nv = _adamw_sum(landed[n], w2[n], m2[n], v2[n], name="adamw_" + n)
        else:
            gr, dl, nm, nv = _adamw_sum_t(landed[n], w2[n], m2[n], v2[n], name="adamw_" + n)
        for dst, val in ((grads, gr), (deltas, dl), (new_m, nm), (new_v, nv)):
            dst[n] = val.reshape(w[n].shape)
    dl, nm, nv = _adamw(gs, _pack_small(w, d), _pack_small(m, d), _pack_small(v, d), name="adamw_small")
    for dst, val in ((grads, gs), (deltas, dl), (new_m, nm), (new_v, nv)):
        dst.update(_unpack_small(val, w))
    return (loss, grad_x[None], *[grads[n] for n in WEIGHTS], *[deltas[n] for n in WEIGHTS],
            *[new_m[n] for n in WEIGHTS], *[new_v[n] for n in WEIGHTS])
```
